```python
import jax
import jax.numpy as jnp
from jax import lax
import numpy as np

D_MODEL = 1024
BATCH = 8
SEQ = 2048
DEPTH = 2

CTX_LEN = 256
GRID_W = 64
N_BRANCH = 4
BR_W = D_MODEL // 2
HEAD_DIM = 128
A_HEADS = BR_W // HEAD_DIM
A_KV_HEADS = A_HEADS // 2
A_GROUP = A_HEADS // A_KV_HEADS
Q_BLOCK = 128
ROPE_THETA = 10000.0
AXIS_DIM = HEAD_DIM // 2
B_CONV = 3
C_HEADS = 4
C_HEAD_K = BR_W // (2 * C_HEADS)
C_HEAD_V = BR_W // C_HEADS
C_KEY_W = C_HEADS * C_HEAD_K
C_GATE_RANK = 16
C_GATE_TAU = 16.0
C_CHUNK = 64
D_CONV = 31
ALPHA = (2 * DEPTH) ** 0.25
BETA = (8 * DEPTH) ** -0.25
EPS = 1e-6
IN_WIDTHS = (
    A_HEADS * HEAD_DIM, A_KV_HEADS * HEAD_DIM, A_KV_HEADS * HEAD_DIM, BR_W,
    BR_W, BR_W, BR_W, BR_W,
    C_KEY_W, C_KEY_W, C_HEADS * C_HEAD_V, BR_W, 2 * C_GATE_RANK,
    2 * BR_W, BR_W,
    N_BRANCH * D_MODEL,
)

kernel_name = 'hybrid_parallel_branch_diffusion_block'


def layer_norm(x):
    xf = x.astype(jnp.float32)
    mu = jnp.mean(xf, -1, keepdims=True)
    var = jnp.mean(jnp.square(xf - mu), -1, keepdims=True)
    return ((xf - mu) * lax.rsqrt(var + EPS)).astype(x.dtype)


def rms_norm(x, g):
    xf = x.astype(jnp.float32)
    y = xf * lax.rsqrt(jnp.mean(jnp.square(xf), -1, keepdims=True) + EPS)
    return y.astype(x.dtype) * g


def split_cols(w):
    cuts = [int(i) for i in np.cumsum(IN_WIDTHS)[:-1]]
    return jnp.split(w, cuts, axis=-1)


def dwconv(x, w):
    k, ch = w.shape
    return lax.conv_general_dilated(x, w[:, None, :].astype(x.dtype), (1,), [(k // 2, k // 2)],
                                    dimension_numbers=('NWC', 'WIO', 'NWC'), feature_group_count=ch)


def apply_rope(x, cos, sin):
    xp = x.reshape(x.shape[:-1] + (HEAD_DIM // 2, 2))
    x0, x1 = xp[..., 0], xp[..., 1]
    c = cos[:, None, :].astype(x.dtype)
    s = sin[:, None, :].astype(x.dtype)
    return jnp.stack([x0 * c - x1 * s, x0 * s + x1 * c], -1).reshape(x.shape)


def attn_heads(q, k, v, q_g, k_g, cos, sin):
    b, t, _ = q.shape
    qh = rms_norm(q.reshape(b, t, A_HEADS, HEAD_DIM), q_g)
    kh = rms_norm(k.reshape(b, t, A_KV_HEADS, HEAD_DIM), k_g)
    if cos is not None:
        qh = apply_rope(qh, cos, sin)
        kh = apply_rope(kh, cos, sin)
    qh = qh.reshape(b, t, A_KV_HEADS, A_GROUP, HEAD_DIM).transpose(0, 2, 3, 1, 4)
    kh = kh.transpose(0, 2, 1, 3)
    vh = v.reshape(b, t, A_KV_HEADS, HEAD_DIM).transpose(0, 2, 1, 3)
    return qh, kh, vh


def softmax_attention(q, k, v):
    s = jnp.einsum('bkgqd,bksd->bkgqs', q, k).astype(jnp.float32) * HEAD_DIM ** -0.5
    p = jax.nn.softmax(s, axis=-1).astype(v.dtype)
    return jnp.einsum('bkgqs,bksd->bkgqd', p, v)


def block_attention(q, k, v):
    b, kv, g, t, hd = q.shape
    nb = t // Q_BLOCK
    qb = q.reshape(b, kv, g, nb, Q_BLOCK, hd).transpose(3, 0, 1, 2, 4, 5)
    o = lax.map(lambda qi: softmax_attention(qi, k, v), qb)
    return o.transpose(1, 0, 4, 2, 3, 5).reshape(b, t, kv * g * hd)


def merge_heads(o):
    b, kv, g, t, hd = o.shape
    return o.transpose(0, 3, 1, 2, 4).reshape(b, t, kv * g * hd)


def gla_chunked(q, k, v, g, s0, want_out):
    b, h, l, _ = q.shape
    dv = v.shape[-1]
    n = l // C_CHUNK

    def chunks(a):
        return a.astype(jnp.float32).reshape(b, h, n, C_CHUNK, a.shape[-1]).transpose(2, 0, 1, 3, 4)

    mask = jnp.tril(jnp.ones((C_CHUNK, C_CHUNK), dtype=bool))[:, :, None]

    def step(s, inp):
        qc, kc, vc, gc = inp
        cum = jnp.cumsum(gc, axis=2)
        last = cum[:, :, -1:, :]
        s_new = jnp.exp(last[:, :, 0, :])[..., None] * s + jnp.einsum('bhjd,bhjv->bhdv', kc * jnp.exp(last - cum), vc)
        if not want_out:
            return s_new, None
        diff = cum[:, :, :, None, :] - cum[:, :, None, :, :]
        decay = jnp.exp(jnp.where(mask, diff, -jnp.inf))
        scores = jnp.einsum('bhid,bhjd,bhijd->bhij', qc, kc, decay)
        o = jnp.einsum('bhij,bhjv->bhiv', scores, vc) + jnp.einsum('bhid,bhdv->bhiv', qc * jnp.exp(cum), s)
        return s_new, o

    s_fin, o = lax.scan(step, s0, (chunks(q), chunks(k), chunks(v), chunks(g)))
    if not want_out:
        return None, s_fin
    return o.transpose(1, 2, 0, 3, 4).reshape(b, h, l, dv).astype(v.dtype), s_fin


def gla_branch(side_c, side_l, w2, b2, norm_g, want_ctx):
    def heads(a, d):
        return a.reshape(a.shape[0], a.shape[1], C_HEADS, d).transpose(0, 2, 1, 3)

    def prep(q, k, v, r):
        gates = [heads(jax.nn.log_sigmoid((r[..., i * C_GATE_RANK:(i + 1) * C_GATE_RANK] @ w2[i] + b2[i])
                                          .astype(jnp.float32)) / C_GATE_TAU, C_HEAD_K) for i in range(2)]
        return heads(q, C_HEAD_K) * C_HEAD_K ** -0.5, heads(k, C_HEAD_K), heads(v, C_HEAD_V), gates[0], gates[1]

    def flip(a):
        return jnp.flip(a, axis=2)

    qc, kc, vc, gcf, gcb = prep(*side_c)
    ql, kl, vl, glf, glb = prep(*side_l)
    s0 = jnp.zeros((qc.shape[0], C_HEADS, C_HEAD_K, C_HEAD_V), jnp.float32)
    oc_f, sc_f = gla_chunked(qc, kc, vc, gcf, s0, want_ctx)
    oc_b, sc_b = gla_chunked(flip(qc), flip(kc), flip(vc), flip(gcb), s0, want_ctx)
    ol_f, _ = gla_chunked(ql, kl, vl, glf, sc_f, True)
    ol_b, _ = gla_chunked(flip(ql), flip(kl), flip(vl), flip(glb), sc_b, True)

    def finish(of, ob):
        o = rms_norm(of + flip(ob), norm_g.reshape(C_HEADS, 1, C_HEAD_V))
        return o.transpose(0, 2, 1, 3).reshape(o.shape[0], o.shape[2], C_HEADS * C_HEAD_V)

    y_ctx = finish(oc_f, oc_b) if want_ctx else None
    return y_ctx, finish(ol_f, ol_b)


def short_conv(bg, cg, xin, z, w):
    return bg * dwconv(cg * xin, w) * jax.nn.silu(z)


def conformer_conv(glu, z, w, bias, g, beta):
    a, gt = jnp.split(glu, 2, axis=-1)
    hh = dwconv(a * jax.nn.sigmoid(gt), w) + bias
    hh = layer_norm(hh) * g + beta
    return jax.nn.silu(hh) * jax.nn.silu(z)


def merge(branches, mg, w_br, w_out):
    b, t, _ = mg.shape
    gates = jax.nn.sigmoid(mg.reshape(b, t, N_BRANCH, D_MODEL))
    acc = gates[:, :, 0] * (branches[0] @ w_br[0])
    for i in range(1, N_BRANCH):
        acc = acc + gates[:, :, i] * (branches[i] @ w_br[i])
    return acc @ w_out


def mixer(u_ctx, u_lat, cos, sin, w_in, q_g, k_g, b_w, c_w2, c_b2, c_g, d_w, d_b, d_g, d_beta, w_br, w_out, want_ctx):
    w_parts = split_cols(w_in)
    pc = [u_ctx @ w for w in w_parts]
    pl = [u_lat @ w for w in w_parts]
    qa_l, ka_l, va_l = attn_heads(pl[0], pl[1], pl[2], q_g, k_g, cos, sin)
    qa_c, ka_c, va_c = attn_heads(pc[0], pc[1], pc[2], q_g, k_g, None, None)
    ya_l = block_attention(qa_l, jnp.concatenate([ka_l, ka_c], 2), jnp.concatenate([va_l, va_c], 2)) * jax.nn.silu(pl[3])
    yc_c, yc_l = gla_branch((pc[8], pc[9], pc[10], pc[12]), (pl[8], pl[9], pl[10], pl[12]), c_w2, c_b2, c_g, want_ctx)
    yc_l = yc_l * jax.nn.silu(pl[11])

    def local_branches(p):
        yb = short_conv(p[4], p[5], p[6], p[7], b_w)
        yd = conformer_conv(p[13], p[14], d_w, d_b, d_g, d_beta)
        return yb, yd

    yb_l, yd_l = local_branches(pl)
    y_lat = merge([ya_l, yb_l, yc_l, yd_l], pl[15], w_br, w_out)
    if not want_ctx:
        return None, y_lat
    ya_c = merge_heads(softmax_attention(qa_c, ka_c, va_c)) * jax.nn.silu(pc[3])
    yb_c, yd_c = local_branches(pc)
    y_ctx = merge([ya_c, yb_c, yc_c * jax.nn.silu(pc[11]), yd_c], pc[15], w_br, w_out)
    return y_ctx, y_lat


def _fwd_setup_inputs(seed: int = 0) -> dict:
    key = jax.random.key(seed)
    ks = jax.random.split(key, 24)

    def nrm(k, shape, s):
        return jax.random.normal(k, shape, jnp.float32) * s

    total_in = sum(IN_WIDTHS)
    return {
        'x': nrm(ks[0], (BATCH, SEQ, D_MODEL), 1.0),
        'c': nrm(ks[1], (BATCH, D_MODEL), 1.0),
        'ctx': nrm(ks[2], (BATCH, CTX_LEN, D_MODEL), 1.0),
        'c_ctx': nrm(ks[3], (D_MODEL,), 1.0),
        'w_mod': nrm(ks[4], (DEPTH, D_MODEL, 3 * D_MODEL), D_MODEL ** -0.5),
        'b_mod': nrm(ks[5], (DEPTH, 3 * D_MODEL), 0.02),
        'w_in': nrm(ks[6], (DEPTH, D_MODEL, total_in), D_MODEL ** -0.5),
        'q_norm': 1.0 + nrm(ks[7], (DEPTH, HEAD_DIM), 0.02),
        'k_norm': 1.0 + nrm(ks[8], (DEPTH, HEAD_DIM), 0.02),
        'b_conv': nrm(ks[9], (DEPTH, B_CONV, BR_W), B_CONV ** -0.5),
        'c_gate_w2': nrm(ks[10], (DEPTH, 2, C_GATE_RANK, C_KEY_W), C_GATE_RANK ** -0.5),
        'c_gate_b': nrm(ks[11], (DEPTH, 2, C_KEY_W), 0.1),
        'c_norm': 1.0 + nrm(ks[12], (DEPTH, BR_W), 0.02),
        'd_conv_w': nrm(ks[13], (DEPTH, D_CONV, BR_W), D_CONV ** -0.5),
        'd_conv_b': nrm(ks[14], (DEPTH, BR_W), 0.02),
        'd_norm_g': 1.0 + nrm(ks[15], (DEPTH, BR_W), 0.02),
        'd_norm_b': nrm(ks[16], (DEPTH, BR_W), 0.02),
        'w_br': nrm(ks[17], (DEPTH, N_BRANCH, BR_W, D_MODEL), BETA * BR_W ** -0.5),
        'w_out': nrm(ks[18], (DEPTH, D_MODEL, D_MODEL), BETA * D_MODEL ** -0.5),
        'ln_g': 1.0 + nrm(ks[19], (DEPTH, D_MODEL), 0.02),
        'ln_b': nrm(ks[20], (DEPTH, D_MODEL), 0.02),
    }


def _fwd_reference(x, c, ctx, c_ctx, w_mod, b_mod, w_in, q_norm, k_norm, b_conv, c_gate_w2, c_gate_b, c_norm,
              d_conv_w, d_conv_b, d_norm_g, d_norm_b, w_br, w_out, ln_g, ln_b):
    rows = x.shape[1] // GRID_W
    row = jnp.repeat(jnp.arange(rows), GRID_W).astype(jnp.float32)
    col = jnp.tile(jnp.arange(GRID_W), rows).astype(jnp.float32)
    inv = ROPE_THETA ** (-jnp.arange(0, AXIS_DIM, 2, dtype=jnp.float32) / AXIS_DIM)
    ang = jnp.concatenate([row[:, None] * inv, col[:, None] * inv], -1)
    cos, sin = jnp.cos(ang), jnp.sin(ang)
    sc = jax.nn.silu(c)
    scc = jax.nn.silu(c_ctx)
    h_lat, h_ctx = x, ctx
    for l in range(DEPTH):
        want_ctx = l < DEPTH - 1
        shift, scale, gate = jnp.split(sc @ w_mod[l] + b_mod[l], 3, axis=-1)
        shift_c, scale_c, gate_c = jnp.split(scc @ w_mod[l] + b_mod[l], 3, axis=-1)
        u_lat = layer_norm(h_lat) * (1.0 + scale[:, None]) + shift[:, None]
        u_ctx = layer_norm(h_ctx) * (1.0 + scale_c) + shift_c
        y_ctx, y_lat = mixer(u_ctx, u_lat, cos, sin, w_in[l], q_norm[l], k_norm[l], b_conv[l], c_gate_w2[l],
                             c_gate_b[l], c_norm[l], d_conv_w[l], d_conv_b[l], d_norm_g[l], d_norm_b[l],
                             w_br[l], w_out[l], want_ctx)
        h_lat = layer_norm(ALPHA * h_lat + gate[:, None] * y_lat) * ln_g[l] + ln_b[l]
        if want_ctx:
            h_ctx = layer_norm(ALPHA * h_ctx + gate_c * y_ctx) * ln_g[l] + ln_b[l]
    return h_lat


import jax as _jax
import jax.numpy as _jnp

TWIN_FORMAT = 'train_step'
FWD_PARAMS = ['x', 'c', 'ctx', 'c_ctx', 'w_mod', 'b_mod', 'w_in', 'q_norm', 'k_norm', 'b_conv', 'c_gate_w2', 'c_gate_b', 'c_norm', 'd_conv_w', 'd_conv_b', 'd_norm_g', 'd_norm_b', 'w_br', 'w_out', 'ln_g', 'ln_b']
TWIN_WEIGHTS = ['c_ctx', 'w_mod', 'b_mod', 'w_in', 'q_norm', 'k_norm', 'b_conv', 'c_gate_w2', 'c_gate_b', 'c_norm', 'd_conv_w', 'd_conv_b', 'd_norm_g', 'd_norm_b', 'w_br', 'w_out', 'ln_g', 'ln_b']
TWIN_DIFF_INPUT = 'x'
TWIN_INPUTS = ['x', 'c', 'ctx', 'c_ctx', 'w_mod', 'b_mod', 'w_in', 'q_norm', 'k_norm', 'b_conv', 'c_gate_w2', 'c_gate_b', 'c_norm', 'd_conv_w', 'd_conv_b', 'd_norm_g', 'd_norm_b', 'w_br', 'w_out', 'ln_g', 'ln_b', 'loss_target', 'm_c_ctx', 'm_w_mod', 'm_b_mod', 'm_w_in', 'm_q_norm', 'm_k_norm', 'm_b_conv', 'm_c_gate_w2', 'm_c_gate_b', 'm_c_norm', 'm_d_conv_w', 'm_d_conv_b', 'm_d_norm_g', 'm_d_norm_b', 'm_w_br', 'm_w_out', 'm_ln_g', 'm_ln_b', 'v_c_ctx', 'v_w_mod', 'v_b_mod', 'v_w_in', 'v_q_norm', 'v_k_norm', 'v_b_conv', 'v_c_gate_w2', 'v_c_gate_b', 'v_c_norm', 'v_d_conv_w', 'v_d_conv_b', 'v_d_norm_g', 'v_d_norm_b', 'v_w_br', 'v_w_out', 'v_ln_g', 'v_ln_b']
TWIN_OUTPUTS = ['loss', 'grad_x', 'grad_c_ctx', 'grad_w_mod', 'grad_b_mod', 'grad_w_in', 'grad_q_norm', 'grad_k_norm', 'grad_b_conv', 'grad_c_gate_w2', 'grad_c_gate_b', 'grad_c_norm', 'grad_d_conv_w', 'grad_d_conv_b', 'grad_d_norm_g', 'grad_d_norm_b', 'grad_w_br', 'grad_w_out', 'grad_ln_g', 'grad_ln_b', 'delta_c_ctx', 'delta_w_mod', 'delta_b_mod', 'delta_w_in', 'delta_q_norm', 'delta_k_norm', 'delta_b_conv', 'delta_c_gate_w2', 'delta_c_gate_b', 'delta_c_norm', 'delta_d_conv_w', 'delta_d_conv_b', 'delta_d_norm_g', 'delta_d_norm_b', 'delta_w_br', 'delta_w_out', 'delta_ln_g', 'delta_ln_b', 'new_m_c_ctx', 'new_m_w_mod', 'new_m_b_mod', 'new_m_w_in', 'new_m_q_norm', 'new_m_k_norm', 'new_m_b_conv', 'new_m_c_gate_w2', 'new_m_c_gate_b', 'new_m_c_norm', 'new_m_d_conv_w', 'new_m_d_conv_b', 'new_m_d_norm_g', 'new_m_d_norm_b', 'new_m_w_br', 'new_m_w_out', 'new_m_ln_g', 'new_m_ln_b', 'new_v_c_ctx', 'new_v_w_mod', 'new_v_b_mod', 'new_v_w_in', 'new_v_q_norm', 'new_v_k_norm', 'new_v_b_conv', 'new_v_c_gate_w2', 'new_v_c_gate_b', 'new_v_c_norm', 'new_v_d_conv_w', 'new_v_d_conv_b', 'new_v_d_norm_g', 'new_v_d_norm_b', 'new_v_w_br', 'new_v_w_out', 'new_v_ln_g', 'new_v_ln_b']
TWIN_LEAF_KINDS = {'loss': 'loss', 'grad_x': 'grad_x', 'grad_c_ctx': 'grad_w', 'grad_w_mod': 'grad_w', 'grad_b_mod': 'grad_w', 'grad_w_in': 'grad_w', 'grad_q_norm': 'grad_w', 'grad_k_norm': 'grad_w', 'grad_b_conv': 'grad_w', 'grad_c_gate_w2': 'grad_w', 'grad_c_gate_b': 'grad_w', 'grad_c_norm': 'grad_w', 'grad_d_conv_w': 'grad_w', 'grad_d_conv_b': 'grad_w', 'grad_d_norm_g': 'grad_w', 'grad_d_norm_b': 'grad_w', 'grad_w_br': 'grad_w', 'grad_w_out': 'grad_w', 'grad_ln_g': 'grad_w', 'grad_ln_b': 'grad_w', 'delta_c_ctx': 'delta_w', 'delta_w_mod': 'delta_w', 'delta_b_mod': 'delta_w', 'delta_w_in': 'delta_w', 'delta_q_norm': 'delta_w', 'delta_k_norm': 'delta_w', 'delta_b_conv': 'delta_w', 'delta_c_gate_w2': 'delta_w', 'delta_c_gate_b': 'delta_w', 'delta_c_norm': 'delta_w', 'delta_d_conv_w': 'delta_w', 'delta_d_conv_b': 'delta_w', 'delta_d_norm_g': 'delta_w', 'delta_d_norm_b': 'delta_w', 'delta_w_br': 'delta_w', 'delta_w_out': 'delta_w', 'delta_ln_g': 'delta_w', 'delta_ln_b': 'delta_w', 'new_m_c_ctx': 'new_m', 'new_m_w_mod': 'new_m', 'new_m_b_mod': 'new_m', 'new_m_w_in': 'new_m', 'new_m_q_norm': 'new_m', 'new_m_k_norm': 'new_m', 'new_m_b_conv': 'new_m', 'new_m_c_gate_w2': 'new_m', 'new_m_c_gate_b': 'new_m', 'new_m_c_norm': 'new_m', 'new_m_d_conv_w': 'new_m', 'new_m_d_conv_b': 'new_m', 'new_m_d_norm_g': 'new_m', 'new_m_d_norm_b': 'new_m', 'new_m_w_br': 'new_m', 'new_m_w_out': 'new_m', 'new_m_ln_g': 'new_m', 'new_m_ln_b': 'new_m', 'new_v_c_ctx': 'new_v', 'new_v_w_mod': 'new_v', 'new_v_b_mod': 'new_v', 'new_v_w_in': 'new_v', 'new_v_q_norm': 'new_v', 'new_v_k_norm': 'new_v', 'new_v_b_conv': 'new_v', 'new_v_c_gate_w2': 'new_v', 'new_v_c_gate_b': 'new_v', 'new_v_c_norm': 'new_v', 'new_v_d_conv_w': 'new_v', 'new_v_d_conv_b': 'new_v', 'new_v_d_norm_g': 'new_v', 'new_v_d_norm_b': 'new_v', 'new_v_w_br': 'new_v', 'new_v_w_out': 'new_v', 'new_v_ln_g': 'new_v', 'new_v_ln_b': 'new_v'}


def _forward(args):
    return _fwd_reference(*[args[k] for k in FWD_PARAMS])


def _output_shape():
    out = _jax.eval_shape(lambda: _forward(_fwd_setup_inputs(0)))
    return out.shape, out.dtype

N_MICROBATCH = 1
ADAM_LR = 0.001
ADAM_B1 = 0.9
ADAM_B2 = 0.999
ADAM_EPS = 1e-08
ADAM_WD = 0.01
ADAM_STEP = 10
PER_EXAMPLE_BATCH_AXIS = {'x': 0, 'c': 0, 'ctx': 0, 'loss_target': 0}
SHARED_INPUTS = []
_WEIGHT_DTYPES = {'c_ctx': _jnp.float32, 'w_mod': _jnp.float32, 'b_mod': _jnp.float32, 'w_in': _jnp.float32, 'q_norm': _jnp.float32, 'k_norm': _jnp.float32, 'b_conv': _jnp.float32, 'c_gate_w2': _jnp.float32, 'c_gate_b': _jnp.float32, 'c_norm': _jnp.float32, 'd_conv_w': _jnp.float32, 'd_conv_b': _jnp.float32, 'd_norm_g': _jnp.float32, 'd_norm_b': _jnp.float32, 'w_br': _jnp.float32, 'w_out': _jnp.float32, 'ln_g': _jnp.float32, 'ln_b': _jnp.float32}
MOMENT_SCALE = {'c_ctx': 3.786719e-03, 'w_mod': 1.746694e-02, 'b_mod': 2.844577e-02, 'w_in': 1.083835e-02, 'q_norm': 2.533381e-03, 'k_norm': 2.483457e-03, 'b_conv': 2.017815e-02, 'c_gate_w2': 4.067145e-03, 'c_gate_b': 6.336633e-03, 'c_norm': 8.578717e-03, 'd_conv_w': 5.762167e-03, 'd_conv_b': 8.509282e-03, 'd_norm_g': 7.144806e-03, 'd_norm_b': 6.289397e-03, 'w_br': 1.664759e-02, 'w_out': 3.325273e-02, 'ln_g': 1.133492e+01, 'ln_b': 3.761235e-01}


def _to_microbatches(a, axis):
    t = _jnp.moveaxis(a, axis, 0)
    t = t.reshape((N_MICROBATCH, t.shape[0] // N_MICROBATCH) + t.shape[1:])
    return _jnp.moveaxis(t, 1, axis + 1)


def setup_inputs(seed: int = 0) -> dict:
    inp = _fwd_setup_inputs(seed)
    key = _jax.random.fold_in(_jax.random.key(seed), 7919)
    shape, _ = _output_shape()
    out = dict(inp)
    out["loss_target"] = _jax.random.normal(_jax.random.fold_in(key, 0), shape, _jnp.float32)
    for i, name in enumerate(TWIN_WEIGHTS):
        w = inp[name].astype(_jnp.float32)
        if MOMENT_SCALE is None:
            s = _jnp.sqrt(_jnp.mean(_jnp.square(w)) + 1e-30)
        else:
            s = MOMENT_SCALE[name]
        km, kv = _jax.random.split(_jax.random.fold_in(key, i + 1))
        out[name] = w
        out["m_" + name] = s * _jax.random.normal(km, w.shape, _jnp.float32)
        out["v_" + name] = (s * s) * _jax.random.uniform(kv, w.shape, _jnp.float32, 0.5, 1.5)
    if N_MICROBATCH > 1:
        for name, axis in PER_EXAMPLE_BATCH_AXIS.items():
            out[name] = _to_microbatches(out[name], axis)
    return {'x': out['x'], 'c': out['c'], 'ctx': out['ctx'], 'c_ctx': out['c_ctx'], 'w_mod': out['w_mod'], 'b_mod': out['b_mod'], 'w_in': out['w_in'], 'q_norm': out['q_norm'], 'k_norm': out['k_norm'], 'b_conv': out['b_conv'], 'c_gate_w2': out['c_gate_w2'], 'c_gate_b': out['c_gate_b'], 'c_norm': out['c_norm'], 'd_conv_w': out['d_conv_w'], 'd_conv_b': out['d_conv_b'], 'd_norm_g': out['d_norm_g'], 'd_norm_b': out['d_norm_b'], 'w_br': out['w_br'], 'w_out': out['w_out'], 'ln_g': out['ln_g'], 'ln_b': out['ln_b'], 'loss_target': out['loss_target'], 'm_c_ctx': out['m_c_ctx'], 'm_w_mod': out['m_w_mod'], 'm_b_mod': out['m_b_mod'], 'm_w_in': out['m_w_in'], 'm_q_norm': out['m_q_norm'], 'm_k_norm': out['m_k_norm'], 'm_b_conv': out['m_b_conv'], 'm_c_gate_w2': out['m_c_gate_w2'], 'm_c_gate_b': out['m_c_gate_b'], 'm_c_norm': out['m_c_norm'], 'm_d_conv_w': out['m_d_conv_w'], 'm_d_conv_b': out['m_d_conv_b'], 'm_d_norm_g': out['m_d_norm_g'], 'm_d_norm_b': out['m_d_norm_b'], 'm_w_br': out['m_w_br'], 'm_w_out': out['m_w_out'], 'm_ln_g': out['m_ln_g'], 'm_ln_b': out['m_ln_b'], 'v_c_ctx': out['v_c_ctx'], 'v_w_mod': out['v_w_mod'], 'v_b_mod': out['v_b_mod'], 'v_w_in': out['v_w_in'], 'v_q_norm': out['v_q_norm'], 'v_k_norm': out['v_k_norm'], 'v_b_conv': out['v_b_conv'], 'v_c_gate_w2': out['v_c_gate_w2'], 'v_c_gate_b': out['v_c_gate_b'], 'v_c_norm': out['v_c_norm'], 'v_d_conv_w': out['v_d_conv_w'], 'v_d_conv_b': out['v_d_conv_b'], 'v_d_norm_g': out['v_d_norm_g'], 'v_d_norm_b': out['v_d_norm_b'], 'v_w_br': out['v_w_br'], 'v_w_out': out['v_w_out'], 'v_ln_g': out['v_ln_g'], 'v_ln_b': out['v_ln_b']}


def _loss(weights, diff, rest, loss_target):
    with _jax.named_scope("forward"):
        args = {**rest, TWIN_DIFF_INPUT: diff, **{k: w.astype(_WEIGHT_DTYPES[k]) for k, w in weights.items()}}
        y = _forward(args)
    with _jax.named_scope("loss_head"):
        err = _jnp.square(y.astype(_jnp.float32) - loss_target)
        return 0.5 * _jnp.sum(_jnp.mean(err, axis=-1)) if err.ndim else 0.5 * err


def _adamw(w, g, m, v):
    m = ADAM_B1 * m + (1.0 - ADAM_B1) * g
    v = ADAM_B2 * v + (1.0 - ADAM_B2) * _jnp.square(g)
    m_hat = m / (1.0 - ADAM_B1 ** ADAM_STEP)
    v_hat = v / (1.0 - ADAM_B2 ** ADAM_STEP)
    delta = -ADAM_LR * (m_hat / (_jnp.sqrt(v_hat) + ADAM_EPS) + ADAM_WD * w)
    return delta, m, v


def reference(x, c, ctx, c_ctx, w_mod, b_mod, w_in, q_norm, k_norm, b_conv, c_gate_w2, c_gate_b, c_norm, d_conv_w, d_conv_b, d_norm_g, d_norm_b, w_br, w_out, ln_g, ln_b, loss_target, m_c_ctx, m_w_mod, m_b_mod, m_w_in, m_q_norm, m_k_norm, m_b_conv, m_c_gate_w2, m_c_gate_b, m_c_norm, m_d_conv_w, m_d_conv_b, m_d_norm_g, m_d_norm_b, m_w_br, m_w_out, m_ln_g, m_ln_b, v_c_ctx, v_w_mod, v_b_mod, v_w_in, v_q_norm, v_k_norm, v_b_conv, v_c_gate_w2, v_c_gate_b, v_c_norm, v_d_conv_w, v_d_conv_b, v_d_norm_g, v_d_norm_b, v_w_br, v_w_out, v_ln_g, v_ln_b):
    given = dict(x=x, c=c, ctx=ctx, c_ctx=c_ctx, w_mod=w_mod, b_mod=b_mod, w_in=w_in, q_norm=q_norm, k_norm=k_norm, b_conv=b_conv, c_gate_w2=c_gate_w2, c_gate_b=c_gate_b, c_norm=c_norm, d_conv_w=d_conv_w, d_conv_b=d_conv_b, d_norm_g=d_norm_g, d_norm_b=d_norm_b, w_br=w_br, w_out=w_out, ln_g=ln_g, ln_b=ln_b, loss_target=loss_target, m_c_ctx=m_c_ctx, m_w_mod=m_w_mod, m_b_mod=m_b_mod, m_w_in=m_w_in, m_q_norm=m_q_norm, m_k_norm=m_k_norm, m_b_conv=m_b_conv, m_c_gate_w2=m_c_gate_w2, m_c_gate_b=m_c_gate_b, m_c_norm=m_c_norm, m_d_conv_w=m_d_conv_w, m_d_conv_b=m_d_conv_b, m_d_norm_g=m_d_norm_g, m_d_norm_b=m_d_norm_b, m_w_br=m_w_br, m_w_out=m_w_out, m_ln_g=m_ln_g, m_ln_b=m_ln_b, v_c_ctx=v_c_ctx, v_w_mod=v_w_mod, v_b_mod=v_b_mod, v_w_in=v_w_in, v_q_norm=v_q_norm, v_k_norm=v_k_norm, v_b_conv=v_b_conv, v_c_gate_w2=v_c_gate_w2, v_c_gate_b=v_c_gate_b, v_c_norm=v_c_norm, v_d_conv_w=v_d_conv_w, v_d_conv_b=v_d_conv_b, v_d_norm_g=v_d_norm_g, v_d_norm_b=v_d_norm_b, v_w_br=v_w_br, v_w_out=v_w_out, v_ln_g=v_ln_g, v_ln_b=v_ln_b)
    weights = {n: given[n] for n in TWIN_WEIGHTS}
    shared = {n: given[n] for n in SHARED_INPUTS}
    per_example = {n: given[n] for n in ['x', 'c', 'ctx']}
    grad_fn = _jax.value_and_grad(_loss, argnums=(0, 1))

    def one_microbatch(ex, loss_target):
        ex = dict(ex)
        diff = ex.pop(TWIN_DIFF_INPUT)
        return grad_fn(weights, diff, {**shared, **ex}, loss_target)

    if N_MICROBATCH == 1:
        loss, (grad_w, grad_x) = one_microbatch(per_example, given["loss_target"])
    else:
        def body(carry, xs):
            loss_sum, grad_sum = carry
            l_k, (gw_k, gx_k) = one_microbatch(xs[0], xs[1])
            with _jax.named_scope("update"):
                return (loss_sum + l_k, _jax.tree.map(_jnp.add, grad_sum, gw_k)), gx_k

        init = (_jnp.zeros((), _jnp.float32), _jax.tree.map(_jnp.zeros_like, weights))
        (loss, grad_w), grad_x = _jax.lax.scan(body, init, (per_example, given["loss_target"]))
    with _jax.named_scope("update"):
        delta_w, new_m, new_v = {}, {}, {}
        for n in TWIN_WEIGHTS:
            delta_w[n], new_m[n], new_v[n] = _adamw(weights[n], grad_w[n], given["m_" + n], given["v_" + n])
    return (loss, grad_x, *[grad_w[n] for n in TWIN_WEIGHTS], *[delta_w[n] for n in TWIN_WEIGHTS],
            *[new_m[n] for n in TWIN_WEIGHTS], *[new_v[n] for n in TWIN_WEIGHTS])
```

```python
import functools

import jax
import jax.numpy as jnp
import numpy as np
from jax import lax
from jax.experimental import pallas as pl
from jax.experimental.pallas import tpu as pltpu

f32 = jnp.float32
bf16 = jnp.bfloat16

D = 1024
DEPTH = 2
GRID_W = 64
BRW = 512
HD = 128
A_HEADS = 4
C_HEADS = 4
C_KW = 256
C_RANK = 16
C_TAU = 16.0
CH = 64
KB = 3
KD = 31
ALPHA = (2 * DEPTH) ** 0.25
EPS = 1e-6
ROPE_THETA = 10000.0
N_IN = 10784
LR, B1, B2, AEPS, WD, STEP = 0.001, 0.9, 0.999, 1e-08, 0.01, 10

O_MG = 0
O_Q, O_K, O_V, O_GA = 4096, 4608, 4864, 5120
O_B, O_C, O_X, O_GB = 5632, 6144, 6656, 7168
O_CQ, O_CK, O_CV, O_GC = 7680, 7936, 8192, 8704
O_DA, O_DG, O_GD = 9216, 9728, 10240
O_R = 10752
NP = 10880
R0, R1, G1 = 5120, 5152, 6688

LANE = 128
SUB = 8
VMEM_LIMIT = 56 * 1024 * 1024
CONV_PAD = 16
GLA_SUB = 16
GLA_CLAMP = 60.0


def _cparams(sem, vmem=VMEM_LIMIT):
    return pltpu.CompilerParams(dimension_semantics=sem, vmem_limit_bytes=vmem)


def _dg(a, b, ca, cb):
    return lax.dot_general(a.astype(bf16), b.astype(bf16), (((ca,), (cb,)), ((), ())),
                           preferred_element_type=f32)


@jax.custom_vjp
def mm(a, b):
    return _dg(a, b, 1, 0)


mm.defvjp(lambda a, b: (_dg(a, b, 1, 0), (a, b)),
          lambda r, ct: (_dg(ct, r[1], 1, 1).astype(r[0].dtype), _dg(r[0], ct, 0, 0).astype(r[1].dtype)))


@jax.custom_vjp
def mm_nt(a, b):
    return _dg(a, b, 1, 1)


mm_nt.defvjp(lambda a, b: (_dg(a, b, 1, 1), (a, b)),
             lambda r, ct: (_dg(ct, r[1], 1, 0).astype(r[0].dtype), _dg(ct, r[0], 0, 0).astype(r[1].dtype)))


@jax.custom_vjp
def mm_tn(a, b):
    return _dg(a, b, 0, 0)


mm_tn.defvjp(lambda a, b: (_dg(a, b, 0, 0), (a, b)),
             lambda r, ct: (_dg(r[1], ct, 1, 1).astype(r[0].dtype), _dg(r[0], ct, 1, 0).astype(r[1].dtype)))


def _silu(x):
    return x * jax.nn.sigmoid(x)


def _ln(x):
    mu = jnp.mean(x, -1, keepdims=True)
    xc = x - mu
    var = jnp.mean(xc * xc, -1, keepdims=True)
    return xc * lax.rsqrt(var + EPS)


def _rms(x, g):
    return x * lax.rsqrt(jnp.mean(x * x, -1, keepdims=True) + EPS) * g


@jax.custom_vjp
def _rope(x, cos_f, sin_a, sin_b):
    return x * cos_f + pltpu.roll(x, HD - 1, 1) * sin_a + pltpu.roll(x, 1, 1) * sin_b


def _rope_fwd(x, cos_f, sin_a, sin_b):
    return _rope(x, cos_f, sin_a, sin_b), (cos_f, sin_a, sin_b)


def _rope_bwd(r, ct):
    cos_f, sin_a, sin_b = r
    dx = ct * cos_f + pltpu.roll(ct * sin_a, 1, 1) + pltpu.roll(ct * sin_b, HD - 1, 1)
    return dx, jnp.zeros_like(cos_f), jnp.zeros_like(sin_a), jnp.zeros_like(sin_b)


_rope.defvjp(_rope_fwd, _rope_bwd)


def _row_ids(i, tm):
    return i * tm + lax.broadcasted_iota(jnp.int32, (tm, 1), 0)


def _partial_rows(ref, rows):
    n = len(rows)
    for k, r in enumerate(rows):
        ref[k:k + 1, :] = r
    ref[n:SUB, :] = jnp.zeros((SUB - n, ref.shape[-1]), f32)


def _matmul(a, b, mode, tm, tn, tk, name, out_dtype=f32):
    if mode == "nn":
        (M, K), N = a.shape, b.shape[1]
        a_spec = pl.BlockSpec((tm, tk), lambda j, i, k: (i, k))
        b_spec = pl.BlockSpec((tk, tn), lambda j, i, k: (k, j))
        ca, cb = 1, 0
    elif mode == "nt":
        (M, K), N = a.shape, b.shape[0]
        a_spec = pl.BlockSpec((tm, tk), lambda j, i, k: (i, k))
        b_spec = pl.BlockSpec((tn, tk), lambda j, i, k: (j, k))
        ca, cb = 1, 1
    else:
        (K, M), N = a.shape, b.shape[1]
        a_spec = pl.BlockSpec((tk, tm), lambda j, i, k: (k, i))
        b_spec = pl.BlockSpec((tk, tn), lambda j, i, k: (k, j))
        ca, cb = 0, 0
    assert M % tm == 0 and N % tn == 0 and K % tk == 0, (name, M, N, K, tm, tn, tk)
    nk = K // tk

    def body(a_ref, b_ref, o_ref, acc_ref):
        k = pl.program_id(2)
        part = _dg(a_ref[...], b_ref[...], ca, cb)

        @pl.when(k == 0)
        def _():
            acc_ref[...] = part

        @pl.when(k > 0)
        def _():
            acc_ref[...] += part

        @pl.when(k == nk - 1)
        def _():
            o_ref[...] = acc_ref[...].astype(o_ref.dtype)

    return pl.pallas_call(
        body, name=name, grid=(N // tn, M // tm, nk),
        in_specs=[a_spec, b_spec], out_specs=pl.BlockSpec((tm, tn), lambda j, i, k: (i, j)),
        out_shape=jax.ShapeDtypeStruct((M, N), out_dtype),
        scratch_shapes=[pltpu.VMEM((tm, tn), f32)],
        compiler_params=_cparams(("parallel", "parallel", "arbitrary")),
    )(a, b)


MOD_TN = 768


def _mod_fwd(cin, w_mod, b_mod):
    def body(c_ref, w_ref, b_ref, o_ref):
        o_ref[...] = mm(_silu(c_ref[...]), w_ref[...]) + b_ref[...]

    return pl.pallas_call(
        body, name="mod_fwd", grid=(DEPTH, 3 * D // MOD_TN),
        in_specs=[pl.BlockSpec((SUB, D), lambda l, j: (0, 0)),
                  pl.BlockSpec((None, D, MOD_TN), lambda l, j: (l, 0, j)),
                  pl.BlockSpec((None, 1, MOD_TN), lambda l, j: (l, 0, j))],
        out_specs=pl.BlockSpec((None, SUB, MOD_TN), lambda l, j: (l, 0, j)),
        out_shape=jax.ShapeDtypeStruct((DEPTH, SUB, 3 * D), f32),
        compiler_params=_cparams(("parallel", "parallel")),
    )(cin, w_mod, b_mod.reshape(DEPTH, 1, 3 * D))


def _mod_bwd(cin, w_mod, dmodv):
    nj = 3 * D // MOD_TN

    def body(c_ref, w_ref, g_ref, dw_ref, dc_ref):
        _, vjp = jax.vjp(lambda c, w: mm(_silu(c), w), c_ref[...], w_ref[...])
        dc, dw = vjp(g_ref[...])
        dw_ref[...] = dw
        dc_ref[...] = dc

    return pl.pallas_call(
        body, name="mod_bwd", grid=(DEPTH, nj),
        in_specs=[pl.BlockSpec((SUB, D), lambda l, j: (0, 0)),
                  pl.BlockSpec((None, D, MOD_TN), lambda l, j: (l, 0, j)),
                  pl.BlockSpec((None, SUB, MOD_TN), lambda l, j: (l, 0, j))],
        out_specs=[pl.BlockSpec((None, D, MOD_TN), lambda l, j: (l, 0, j)),
                   pl.BlockSpec((None, None, SUB, D), lambda l, j: (l, j, 0, 0))],
        out_shape=[jax.ShapeDtypeStruct((DEPTH, D, 3 * D), f32),
                   jax.ShapeDtypeStruct((DEPTH, nj, SUB, D), f32)],
        compiler_params=_cparams(("parallel", "parallel")),
    )(cin, w_mod, dmodv)


def _u_fn(h, m_l, m_c, isctx):
    n = _ln(h)
    shift = jnp.where(isctx, m_c[:, 0:D], m_l[:, 0:D])
    scale = jnp.where(isctx, m_c[:, D:2 * D], m_l[:, D:2 * D])
    return n * (1.0 + scale) + shift


def _ln_fwd(h, modv_l, tc, tm, name):
    T = h.shape[0]

    def body(h_ref, m_ref, u_ref):
        isctx = _row_ids(pl.program_id(0), tm) < tc
        u_ref[...] = _u_fn(h_ref[...], m_ref[0:1, :], m_ref[1:2, :], isctx).astype(bf16)

    return pl.pallas_call(
        body, name=name, grid=(T // tm,),
        in_specs=[pl.BlockSpec((tm, D), lambda i: (i, 0)), pl.BlockSpec((SUB, 3 * D), lambda i: (0, 0))],
        out_specs=pl.BlockSpec((tm, D), lambda i: (i, 0)),
        out_shape=jax.ShapeDtypeStruct((T, D), bf16),
        compiler_params=_cparams(("parallel",)),
    )(h, modv_l)


def _ln_bwd(du, h, dh_res, modv_l, tc, tm, name):
    T = h.shape[0]
    nt = T // tm

    def body(du_ref, h_ref, r_ref, m_ref, dh_ref, dm_ref):
        isctx = _row_ids(pl.program_id(0), tm) < tc
        _, vjp = jax.vjp(lambda h, ml, mc: _u_fn(h, ml, mc, isctx), h_ref[...], m_ref[0:1, :], m_ref[1:2, :])
        dh, dml, dmc = vjp(du_ref[...])
        dh_ref[...] = dh + r_ref[...]
        _partial_rows(dm_ref, [dml, dmc])

    return pl.pallas_call(
        body, name=name, grid=(nt,),
        in_specs=[pl.BlockSpec((tm, D), lambda i: (i, 0)), pl.BlockSpec((tm, D), lambda i: (i, 0)),
                  pl.BlockSpec((tm, D), lambda i: (i, 0)), pl.BlockSpec((SUB, 3 * D), lambda i: (0, 0))],
        out_specs=[pl.BlockSpec((tm, D), lambda i: (i, 0)), pl.BlockSpec((None, SUB, 3 * D), lambda i: (i, 0, 0))],
        out_shape=[jax.ShapeDtypeStruct((T, D), f32), jax.ShapeDtypeStruct((nt, SUB, 3 * D), f32)],
        compiler_params=_cparams(("parallel",)),
    )(du, h, dh_res, modv_l)


def _prep_fn(q, k, qg, kg, cos_f, sin_a, sin_b):
    qs = [_rope(_rms(q[:, HD * i:HD * (i + 1)], qg), cos_f, sin_a, sin_b) for i in range(A_HEADS)]
    ks = [_rope(_rms(k[:, HD * i:HD * (i + 1)], kg), cos_f, sin_a, sin_b) for i in range(A_HEADS // 2)]
    return jnp.concatenate(qs, 1), jnp.concatenate(ks, 1)


def _tok(tm, w, off):
    return pl.BlockSpec((tm, w), lambda i: (i, off // w))


def _vec(w):
    return pl.BlockSpec((1, w), lambda i: (0, 0))


def _prep_fwd(P, qg, kg, rope, tm, name):
    T = P.shape[0]

    def body(q_ref, k_ref, v_ref, qg_ref, kg_ref, c_ref, sa_ref, sb_ref, qn_ref, kn_ref, vb_ref):
        qn, kn = _prep_fn(q_ref[...], k_ref[...], qg_ref[...], kg_ref[...], c_ref[...], sa_ref[...], sb_ref[...])
        qn_ref[...] = qn.astype(bf16)
        kn_ref[...] = kn.astype(bf16)
        vb_ref[...] = v_ref[...].astype(bf16)

    return pl.pallas_call(
        body, name=name, grid=(T // tm,),
        in_specs=[_tok(tm, 512, O_Q), _tok(tm, 256, O_K), _tok(tm, 256, O_V), _vec(HD), _vec(HD),
                  _tok(tm, HD, 0), _tok(tm, HD, 0), _tok(tm, HD, 0)],
        out_specs=[_tok(tm, 512, 0), _tok(tm, 256, 0), _tok(tm, 256, 0)],
        out_shape=[jax.ShapeDtypeStruct((T, 512), bf16), jax.ShapeDtypeStruct((T, 256), bf16),
                   jax.ShapeDtypeStruct((T, 256), bf16)],
        compiler_params=_cparams(("parallel",)),
    )(P, P, P, qg, kg, *rope)


def _prep_bwd(P, dqn, dkn, qg, kg, rope, tm, name):
    T = P.shape[0]
    nt = T // tm

    def body(q_ref, k_ref, dq_ref, dk_ref, qg_ref, kg_ref, c_ref, sa_ref, sb_ref, oq_ref, ok_ref, og_ref):
        tabs = (c_ref[...], sa_ref[...], sb_ref[...])
        _, vjp = jax.vjp(lambda q, k, a, b: _prep_fn(q, k, a, b, *tabs), q_ref[...], k_ref[...], qg_ref[...], kg_ref[...])
        dq, dk, dqg, dkg = vjp((dq_ref[...], dk_ref[...]))
        oq_ref[...] = dq
        ok_ref[...] = dk
        _partial_rows(og_ref, [dqg, dkg])

    return pl.pallas_call(
        body, name=name, grid=(nt,),
        in_specs=[_tok(tm, 512, O_Q), _tok(tm, 256, O_K), _tok(tm, 512, 0), _tok(tm, 256, 0), _vec(HD), _vec(HD),
                  _tok(tm, HD, 0), _tok(tm, HD, 0), _tok(tm, HD, 0)],
        out_specs=[_tok(tm, 512, 0), _tok(tm, 256, 0), pl.BlockSpec((None, SUB, HD), lambda i: (i, 0, 0))],
        out_shape=[jax.ShapeDtypeStruct((T, 512), f32), jax.ShapeDtypeStruct((T, 256), f32),
                   jax.ShapeDtypeStruct((nt, SUB, HD), f32)],
        compiler_params=_cparams(("parallel",)),
    )(P, P, dqn, dkn, qg, kg, *rope)


def _attn_fn(q, k, v, lim):
    s = mm_nt(q, k) * (HD ** -0.5)
    col = lax.broadcasted_iota(jnp.int32, s.shape, 1)
    s = jnp.where(col < lim, s, -1e30)
    m = jnp.max(s, -1, keepdims=True)
    e = jnp.exp(s - m)
    p = e / jnp.sum(e, -1, keepdims=True)
    return mm(p, v)


def _attn_fwd(qn, kn, vb, tc, tq, name):
    T = qn.shape[0]

    def body(q_ref, k_ref, v_ref, o_ref):
        lim = jnp.where(pl.program_id(1) * tq < tc, tc, T)
        o_ref[...] = _attn_fn(q_ref[...], k_ref[...], v_ref[...], lim)

    return pl.pallas_call(
        body, name=name, grid=(A_HEADS, T // tq),
        in_specs=[pl.BlockSpec((tq, HD), lambda h, i: (i, h)), pl.BlockSpec((T, HD), lambda h, i: (0, h // 2)),
                  pl.BlockSpec((T, HD), lambda h, i: (0, h // 2))],
        out_specs=pl.BlockSpec((tq, HD), lambda h, i: (i, h)),
        out_shape=jax.ShapeDtypeStruct((T, 512), f32),
        compiler_params=_cparams(("parallel", "parallel")),
    )(qn, kn, vb)


def _attn_bwd(qn, kn, vb, dya, tc, tq, name):
    T = qn.shape[0]

    def body(q_ref, k_ref, v_ref, g_ref, dq_ref, dk_ref, dv_ref):
        first = (pl.program_id(1) == 0) & (pl.program_id(2) == 0)
        lim = jnp.where(pl.program_id(2) * tq < tc, tc, T)
        _, vjp = jax.vjp(lambda q, k, v: _attn_fn(q, k, v, lim), q_ref[...].astype(f32), k_ref[...].astype(f32),
                         v_ref[...].astype(f32))
        dq, dk, dv = vjp(g_ref[...])
        dq_ref[...] = dq

        @pl.when(first)
        def _():
            dk_ref[...] = dk
            dv_ref[...] = dv

        @pl.when(jnp.logical_not(first))
        def _():
            dk_ref[...] += dk
            dv_ref[...] += dv

    qspec = pl.BlockSpec((tq, HD), lambda kv, g, i: (i, 2 * kv + g))
    kspec = pl.BlockSpec((T, HD), lambda kv, g, i: (0, kv))
    return pl.pallas_call(
        body, name=name, grid=(A_HEADS // 2, 2, T // tq),
        in_specs=[qspec, kspec, kspec, qspec], out_specs=[qspec, kspec, kspec],
        out_shape=[jax.ShapeDtypeStruct((T, 512), f32), jax.ShapeDtypeStruct((T, 256), f32),
                   jax.ShapeDtypeStruct((T, 256), f32)],
        compiler_params=_cparams(("parallel", "arbitrary", "arbitrary")),
    )(qn, kn, vb, dya)


def _conv_rows(tc, tl):
    return CONV_PAD + tc + CONV_PAD + tl + CONV_PAD


def _fill_pad(pad_ref, val, tc, tl):
    z = jnp.zeros((CONV_PAD, LANE), f32)
    pad_ref[0:CONV_PAD, :] = z
    pad_ref[CONV_PAD:CONV_PAD + tc, :] = val[0:tc]
    pad_ref[CONV_PAD + tc:2 * CONV_PAD + tc, :] = z
    pad_ref[2 * CONV_PAD + tc:2 * CONV_PAD + tc + tl, :] = val[tc:tc + tl]
    pad_ref[2 * CONV_PAD + tc + tl:3 * CONV_PAD + tc + tl, :] = z


def _conv_apply(pad_ref, w_ref, K, tc, tl, rc, emit, flip=False):
    half = K // 2
    for seg0, off, n in ((0, CONV_PAD, tc), (tc, 2 * CONV_PAD + tc, tl)):
        for r0 in range(0, n, rc):
            acc = None
            for k in range(K):
                sh = (half - k) if flip else (k - half)
                term = pad_ref[pl.ds(off + r0 + sh, rc), :] * w_ref[k:k + 1, :]
                acc = term if acc is None else acc + term
            emit(seg0 + r0, acc)


def _conv_wgrad(pad_ref, dy_ref, K, tc, tl, rc, dw_ref):
    half = K // 2
    for k in range(K):
        acc = jnp.zeros((1, LANE), f32)
        for seg0, off, n in ((0, CONV_PAD, tc), (tc, 2 * CONV_PAD + tc, tl)):
            for r0 in range(0, n, rc):
                acc = acc + jnp.sum(pad_ref[pl.ds(off + r0 + k - half, rc), :] * dy_ref[pl.ds(seg0 + r0, rc), :],
                                    axis=0, keepdims=True)
        dw_ref[k:k + 1, :] = acc


def _col(T, off):
    return pl.BlockSpec((T, LANE), lambda j: (0, off // LANE + j))


def _conv_fwd(P, wb, wd, bd, tc, tl, rc, name):
    T = tc + tl

    def body(b_ref, c_ref, x_ref, a_ref, g_ref, wb_ref, wd_ref, bd_ref, yb_ref, hh_ref, pad_ref):
        _fill_pad(pad_ref, c_ref[...] * x_ref[...], tc, tl)

        def emit_b(r0, y):
            yb_ref[pl.ds(r0, rc), :] = y * b_ref[pl.ds(r0, rc), :]

        _conv_apply(pad_ref, wb_ref, KB, tc, tl, rc, emit_b)
        _fill_pad(pad_ref, a_ref[...] * jax.nn.sigmoid(g_ref[...]), tc, tl)

        def emit_d(r0, y):
            hh_ref[pl.ds(r0, rc), :] = y + bd_ref[...]

        _conv_apply(pad_ref, wd_ref, KD, tc, tl, rc, emit_d)

    return pl.pallas_call(
        body, name=name, grid=(BRW // LANE,),
        in_specs=[_col(T, O_B), _col(T, O_C), _col(T, O_X), _col(T, O_DA), _col(T, O_DG),
                  pl.BlockSpec((KB, LANE), lambda j: (0, j)), pl.BlockSpec((KD, LANE), lambda j: (0, j)),
                  pl.BlockSpec((1, LANE), lambda j: (0, j))],
        out_specs=[_col(T, 0), _col(T, 0)],
        out_shape=[jax.ShapeDtypeStruct((T, BRW), f32), jax.ShapeDtypeStruct((T, BRW), f32)],
        scratch_shapes=[pltpu.VMEM((_conv_rows(tc, tl), LANE), f32)],
        compiler_params=_cparams(("parallel",)),
    )(P, P, P, P, P, wb, wd, bd)


def _conv_bwd(P, dyb, dhh, wb, wd, tc, tl, rc, name):
    T = tc + tl

    def body(b_ref, c_ref, x_ref, a_ref, g_ref, dyb_ref, dhh_ref, wb_ref, wd_ref,
             db_ref, dc_ref, dx_ref, da_ref, dg_ref, dwb_ref, dwd_ref, dbd_ref, pad_ref, pad2_ref, tmp_ref):
        _fill_pad(pad_ref, c_ref[...] * x_ref[...], tc, tl)

        def emit_cv(r0, y):
            db_ref[pl.ds(r0, rc), :] = y * dyb_ref[pl.ds(r0, rc), :]

        _conv_apply(pad_ref, wb_ref, KB, tc, tl, rc, emit_cv)
        tmp_ref[...] = dyb_ref[...] * b_ref[...]
        _conv_wgrad(pad_ref, tmp_ref, KB, tc, tl, rc, dwb_ref)
        _fill_pad(pad2_ref, tmp_ref[...], tc, tl)

        def emit_ds(r0, y):
            dc_ref[pl.ds(r0, rc), :] = y * x_ref[pl.ds(r0, rc), :]
            dx_ref[pl.ds(r0, rc), :] = y * c_ref[pl.ds(r0, rc), :]

        _conv_apply(pad2_ref, wb_ref, KB, tc, tl, rc, emit_ds, flip=True)
        _fill_pad(pad_ref, a_ref[...] * jax.nn.sigmoid(g_ref[...]), tc, tl)
        _conv_wgrad(pad_ref, dhh_ref, KD, tc, tl, rc, dwd_ref)
        dbd_ref[...] = jnp.sum(dhh_ref[...], axis=0, keepdims=True)
        _fill_pad(pad2_ref, dhh_ref[...], tc, tl)

        def emit_d2(r0, y):
            sg = jax.nn.sigmoid(g_ref[pl.ds(r0, rc), :])
            a = a_ref[pl.ds(r0, rc), :]
            da_ref[pl.ds(r0, rc), :] = y * sg
            dg_ref[pl.ds(r0, rc), :] = y * a * sg * (1.0 - sg)

        _conv_apply(pad2_ref, wd_ref, KD, tc, tl, rc, emit_d2, flip=True)

    big = jax.ShapeDtypeStruct((T, BRW), f32)
    return pl.pallas_call(
        body, name=name, grid=(BRW // LANE,),
        in_specs=[_col(T, O_B), _col(T, O_C), _col(T, O_X), _col(T, O_DA), _col(T, O_DG), _col(T, 0), _col(T, 0),
                  pl.BlockSpec((KB, LANE), lambda j: (0, j)), pl.BlockSpec((KD, LANE), lambda j: (0, j))],
        out_specs=[_col(T, 0)] * 5 + [pl.BlockSpec((KB, LANE), lambda j: (0, j)), pl.BlockSpec((KD, LANE), lambda j: (0, j)),
                                      pl.BlockSpec((1, LANE), lambda j: (0, j))],
        out_shape=[big] * 5 + [jax.ShapeDtypeStruct((KB, BRW), f32), jax.ShapeDtypeStruct((KD, BRW), f32),
                               jax.ShapeDtypeStruct((1, BRW), f32)],
        scratch_shapes=[pltpu.VMEM((_conv_rows(tc, tl), LANE), f32), pltpu.VMEM((_conv_rows(tc, tl), LANE), f32),
                        pltpu.VMEM((T, LANE), f32)],
        compiler_params=_cparams(("parallel",)),
    )(P, P, P, P, P, dyb, dhh, wb, wd)


def _gla_chunk(q, k, v, r, w2, b2, st, isfwd):
    z = mm(r, w2) + b2
    g = jax.nn.log_sigmoid(jnp.where(isfwd, z[:, 0:C_KW], z[:, C_KW:2 * C_KW])) / C_TAU
    ri = lax.broadcasted_iota(jnp.int32, (CH, CH), 0)
    ci = lax.broadcasted_iota(jnp.int32, (CH, CH), 1)
    keep = jnp.where(isfwd, ri - ci, ci - ri) >= 0
    tri = keep.astype(f32)
    cum = jnp.dot(tri, g, preferred_element_type=f32, precision=lax.Precision.HIGHEST)
    last = jnp.sum(g, axis=0, keepdims=True)
    q = q * (C_KW // C_HEADS) ** -0.5
    hv = lax.broadcasted_iota(jnp.int32, (BRW, C_KW), 0) // (BRW // C_HEADS)
    hk = lax.broadcasted_iota(jnp.int32, (BRW, C_KW), 1) // (C_KW // C_HEADS)
    st_new = st * jnp.exp(last) + jnp.where(hv == hk, mm_tn(v, k * jnp.exp(last - cum)), 0.0)
    o = mm_nt(q * jnp.exp(cum), st)
    rowi = lax.broadcasted_iota(jnp.int32, (CH, C_KW), 0)
    lane_head = lax.broadcasted_iota(jnp.int32, (CH, C_KW), 1) // (C_KW // C_HEADS)
    scores = [jnp.zeros((CH, CH), f32) for _ in range(C_HEADS)]
    for a in range(CH // GLA_SUB):
        idx = jnp.where(isfwd, GLA_SUB * a - 1, GLA_SUB * (a + 1))
        ref = jnp.sum(jnp.where(rowi == idx, cum, 0.0), axis=0, keepdims=True)
        qa = q * jnp.exp(jnp.minimum(cum - ref, 0.0))
        ka = k * jnp.exp(jnp.minimum(ref - cum, GLA_CLAMP))
        in_block = (ri // GLA_SUB == a) & keep
        for hd in range(C_HEADS):
            s = mm_nt(jnp.where(lane_head == hd, qa, 0.0), ka)
            scores[hd] = scores[hd] + jnp.where(in_block, s, 0.0)
    vw = BRW // C_HEADS
    o = o + jnp.concatenate([mm(scores[hd], v[:, vw * hd:vw * (hd + 1)]) for hd in range(C_HEADS)], axis=1)
    return o, st_new


def _gla_chunk_of(d, n, nc, nch):
    back = jnp.where(n < nc, nc - 1 - n, nch - 1 - (n - nc))
    return jnp.where(d == 0, n, back)


def _gla_fwd(P, w2, b2, tc, name):
    T = P.shape[0]
    nch, nc = T // CH, tc // CH

    def cmap(w, off):
        return pl.BlockSpec((CH, w), lambda d, n: (_gla_chunk_of(d, n, nc, nch), off // w))

    def body(q_ref, k_ref, v_ref, r_ref, w_ref, b_ref, o_ref, ss_ref, st_ref):
        @pl.when(pl.program_id(1) == 0)
        def _():
            st_ref[...] = jnp.zeros_like(st_ref)

        st = st_ref[...]
        ss_ref[...] = st
        o, st_new = _gla_chunk(q_ref[...], k_ref[...], v_ref[...], r_ref[...], w_ref[...], b_ref[...], st,
                               pl.program_id(0) == 0)
        o_ref[...] = o
        st_ref[...] = st_new

    return pl.pallas_call(
        body, name=name, grid=(2, nch),
        in_specs=[cmap(256, O_CQ), cmap(256, O_CK), cmap(512, O_CV), cmap(LANE, O_R),
                  pl.BlockSpec((LANE, 512), lambda d, n: (0, 0)), pl.BlockSpec((1, 512), lambda d, n: (0, 0))],
        out_specs=[pl.BlockSpec((None, CH, BRW), lambda d, n: (d, _gla_chunk_of(d, n, nc, nch), 0)),
                   pl.BlockSpec((None, None, BRW, C_KW), lambda d, n: (d, n, 0, 0))],
        out_shape=[jax.ShapeDtypeStruct((2, T, BRW), f32), jax.ShapeDtypeStruct((2, nch, BRW, C_KW), f32)],
        scratch_shapes=[pltpu.VMEM((BRW, C_KW), f32)],
        compiler_params=_cparams(("parallel", "arbitrary")),
    )(P, P, P, P, w2, b2)


def _gla_bwd(P, w2, b2, ssave, doc, tc, name):
    T = P.shape[0]
    nch, nc = T // CH, tc // CH

    def chunk(d, m):
        return _gla_chunk_of(d, nch - 1 - m, nc, nch)

    def cmap(w, off):
        return pl.BlockSpec((CH, w), lambda d, m: (chunk(d, m), off // w))

    def omap(w):
        return pl.BlockSpec((None, CH, w), lambda d, m: (d, chunk(d, m), 0))

    def body(q_ref, k_ref, v_ref, r_ref, w_ref, b_ref, ss_ref, g_ref,
             dq_ref, dk_ref, dv_ref, dr_ref, dw_ref, db_ref, dst_ref):
        m = pl.program_id(1)
        isfwd = pl.program_id(0) == 0

        @pl.when(m == 0)
        def _():
            dst_ref[...] = jnp.zeros_like(dst_ref)

        _, vjp = jax.vjp(lambda q, k, v, r, w, b, st: _gla_chunk(q, k, v, r, w, b, st, isfwd),
                         q_ref[...], k_ref[...], v_ref[...], r_ref[...], w_ref[...], b_ref[...], ss_ref[...])
        dq, dk, dv, dr, dw, db, dst = vjp((g_ref[...], dst_ref[...]))
        dq_ref[...] = dq
        dk_ref[...] = dk
        dv_ref[...] = dv
        dr_ref[...] = dr
        dst_ref[...] = dst

        @pl.when(m == 0)
        def _():
            dw_ref[...] = dw
            _partial_rows(db_ref, [db])

        @pl.when(m > 0)
        def _():
            dw_ref[...] += dw
            db_ref[0:1, :] += db

    return pl.pallas_call(
        body, name=name, grid=(2, nch),
        in_specs=[cmap(256, O_CQ), cmap(256, O_CK), cmap(512, O_CV), cmap(LANE, O_R),
                  pl.BlockSpec((LANE, 512), lambda d, m: (0, 0)), pl.BlockSpec((1, 512), lambda d, m: (0, 0)),
                  pl.BlockSpec((None, None, BRW, C_KW), lambda d, m: (d, nch - 1 - m, 0, 0)),
                  pl.BlockSpec((CH, BRW), lambda d, m: (chunk(d, m), 0))],
        out_specs=[omap(256), omap(256), omap(512), omap(LANE),
                   pl.BlockSpec((None, LANE, 512), lambda d, m: (d, 0, 0)),
                   pl.BlockSpec((None, SUB, 512), lambda d, m: (d, 0, 0))],
        out_shape=[jax.ShapeDtypeStruct((2, T, 256), f32), jax.ShapeDtypeStruct((2, T, 256), f32),
                   jax.ShapeDtypeStruct((2, T, 512), f32), jax.ShapeDtypeStruct((2, T, LANE), f32),
                   jax.ShapeDtypeStruct((2, LANE, 512), f32), jax.ShapeDtypeStruct((2, SUB, 512), f32)],
        scratch_shapes=[pltpu.VMEM((BRW, C_KW), f32)],
        compiler_params=_cparams(("parallel", "arbitrary")),
    )(P, P, P, P, w2, b2, ssave, doc)


def _merge_fn(h, m_l, m_c, isctx, ya, ga, yb, gb, of, ob, gc, hh, gd, mg, es, ey, cn, dng, dnb, lg, lb, wbr, wout):
    oc = of + ob
    yc = jnp.concatenate([_rms(oc[:, HD * i:HD * (i + 1)], cn[:, HD * i:HD * (i + 1)]) for i in range(C_HEADS)], 1)
    brs = [ya * _silu(ga), yb * _silu(gb), yc * _silu(gc), _silu(_ln(hh) * dng + dnb) * _silu(gd)]
    acc = None
    for i in range(4):
        t = jax.nn.sigmoid(mg[:, D * i:D * (i + 1)]) * (mm(brs[i], wbr[i]) + es[i])
        acc = t if acc is None else acc + t
    y = mm(acc, wout) + ey
    gate = jnp.where(isctx, m_c[:, 2 * D:3 * D], m_l[:, 2 * D:3 * D])
    hn = _ln(ALPHA * h + gate * y) * lg + lb
    return hn, (brs, acc)


def _merge_specs(tm):
    t = lambda w, off=0: _tok(tm, w, off)
    return [t(D), pl.BlockSpec((SUB, 3 * D), lambda i: (0, 0)),
            t(BRW), t(BRW, O_GA), t(BRW), t(BRW, O_GB),
            pl.BlockSpec((None, tm, BRW), lambda i: (0, i, 0)), pl.BlockSpec((None, tm, BRW), lambda i: (1, i, 0)),
            t(BRW, O_GC), t(BRW), t(BRW, O_GD), t(4 * D, O_MG),
            _vec(BRW), _vec(BRW), _vec(BRW), _vec(D), _vec(D),
            pl.BlockSpec((4, BRW, D), lambda i: (0, 0, 0)), pl.BlockSpec((D, D), lambda i: (0, 0))]


def _merge_fwd(h, modv_l, ya, yb, o2, hh, P, cn, dng, dnb, lg, lb, wbr, wout, tc, tm, name):
    T = h.shape[0]

    def body(h_ref, m_ref, ya_ref, ga_ref, yb_ref, gb_ref, of_ref, ob_ref, gc_ref, hh_ref, gd_ref, mg_ref,
             cn_ref, dng_ref, dnb_ref, lg_ref, lb_ref, wbr_ref, wout_ref, o_ref):
        isctx = _row_ids(pl.program_id(0), tm) < tc
        zero = jnp.zeros((tm, D), f32)
        hn, _ = _merge_fn(h_ref[...], m_ref[0:1, :], m_ref[1:2, :], isctx, ya_ref[...], ga_ref[...], yb_ref[...],
                          gb_ref[...], of_ref[...], ob_ref[...], gc_ref[...], hh_ref[...], gd_ref[...], mg_ref[...],
                          [zero] * 4, zero, cn_ref[...], dng_ref[...], dnb_ref[...], lg_ref[...], lb_ref[...],
                          [wbr_ref[i] for i in range(4)], wout_ref[...])
        o_ref[...] = hn

    return pl.pallas_call(
        body, name=name, grid=(T // tm,),
        in_specs=_merge_specs(tm), out_specs=_tok(tm, D, 0),
        out_shape=jax.ShapeDtypeStruct((T, D), f32),
        compiler_params=_cparams(("parallel",)),
    )(h, modv_l, ya, P, yb, P, o2, o2, P, hh, P, P, cn, dng, dnb, lg, lb, wbr, wout)


def _merge_bwd(dhn, h, modv_l, ya, yb, o2, hh, P, cn, dng, dnb, lg, lb, wbr, wout, tc, tm, name):
    T = h.shape[0]
    nt = T // tm

    def body(g_ref, h_ref, m_ref, ya_ref, ga_ref, yb_ref, gb_ref, of_ref, ob_ref, gc_ref, hh_ref, gd_ref, mg_ref,
             cn_ref, dng_ref, dnb_ref, lg_ref, lb_ref, wbr_ref, wout_ref,
             dh_ref, dm_ref, dya_ref, dga_ref, dyb_ref, dgb_ref, doc_ref, dgc_ref, dhh_ref, dgd_ref, dmg_ref,
             br_ref, z_ref, acc_ref, dy_ref, dv5_ref, dvd_ref):
        isctx = _row_ids(pl.program_id(0), tm) < tc
        zero = jnp.zeros((tm, D), f32)
        wbr_v = [wbr_ref[i] for i in range(4)]
        wout_v = wout_ref[...]

        def fn(h, ml, mc, ya, ga, yb, gb, oc, gc, hh, gd, mg, e0, e1, e2, e3, ey, cn, dng, dnb, lg, lb):
            return _merge_fn(h, ml, mc, isctx, ya, ga, yb, gb, oc, jnp.zeros_like(oc), gc, hh, gd, mg,
                             [e0, e1, e2, e3], ey, cn, dng, dnb, lg, lb, wbr_v, wout_v)

        _, vjp, (brs, acc) = jax.vjp(
            fn, h_ref[...], m_ref[0:1, :], m_ref[1:2, :], ya_ref[...], ga_ref[...], yb_ref[...], gb_ref[...],
            of_ref[...] + ob_ref[...], gc_ref[...], hh_ref[...], gd_ref[...], mg_ref[...], zero, zero, zero, zero, zero,
            cn_ref[...], dng_ref[...], dnb_ref[...], lg_ref[...], lb_ref[...], has_aux=True)
        (dh, dml, dmc, dya, dga, dyb, dgb, doc, dgc, dhh, dgd, dmg, z0, z1, z2, z3, dy,
         dcn, ddng, ddnb, dlg, dlb) = vjp(g_ref[...])
        dh_ref[...] = dh
        _partial_rows(dm_ref, [dml, dmc])
        dya_ref[...] = dya
        dga_ref[...] = dga
        dyb_ref[...] = dyb
        dgb_ref[...] = dgb
        doc_ref[...] = doc
        dgc_ref[...] = dgc
        dhh_ref[...] = dhh
        dgd_ref[...] = dgd
        dmg_ref[...] = dmg
        for i, z in enumerate((z0, z1, z2, z3)):
            br_ref[i] = brs[i].astype(bf16)
            z_ref[i] = z.astype(bf16)
        acc_ref[...] = acc.astype(bf16)
        dy_ref[...] = dy.astype(bf16)
        _partial_rows(dv5_ref, [dcn, ddng, ddnb])
        _partial_rows(dvd_ref, [dlg, dlb])

    t = lambda w: _tok(tm, w, 0)
    part = lambda w: pl.BlockSpec((None, SUB, w), lambda i: (i, 0, 0))
    sd = jax.ShapeDtypeStruct
    return pl.pallas_call(
        body, name=name, grid=(nt,),
        in_specs=[t(D)] + _merge_specs(tm),
        out_specs=[t(D), part(3 * D)] + [t(BRW)] * 8 + [t(4 * D),
                   pl.BlockSpec((4, tm, BRW), lambda i: (0, i, 0)), pl.BlockSpec((4, tm, D), lambda i: (0, i, 0)),
                   t(D), t(D), part(BRW), part(D)],
        out_shape=[sd((T, D), f32), sd((nt, SUB, 3 * D), f32)] + [sd((T, BRW), f32)] * 8 + [sd((T, 4 * D), f32),
                   sd((4, T, BRW), bf16), sd((4, T, D), bf16), sd((T, D), bf16), sd((T, D), bf16),
                   sd((nt, SUB, BRW), f32), sd((nt, SUB, D), f32)],
        compiler_params=_cparams(("parallel",)),
    )(dhn, h, modv_l, ya, P, yb, P, o2, o2, P, hh, P, P, cn, dng, dnb, lg, lb, wbr, wout)


def _loss_kernel(h, tgt, tc, tm, name):
    T = h.shape[0]
    nt = T // tm
    nct = tc // tm

    def body(h_ref, t_ref, d_ref, l_ref):
        i = pl.program_id(0)
        err = h_ref[...] - t_ref[...]
        lat = (i >= nct).astype(f32)
        d_ref[...] = err * (lat / D)
        l_ref[...] = jnp.zeros((SUB, LANE), f32) + lat * 0.5 * jnp.sum(err * err) / D

    return pl.pallas_call(
        body, name=name, grid=(nt,),
        in_specs=[pl.BlockSpec((tm, D), lambda i: (i, 0)),
                  pl.BlockSpec((tm, D), lambda i: (jnp.maximum(i - nct, 0), 0))],
        out_specs=[pl.BlockSpec((tm, D), lambda i: (i, 0)), pl.BlockSpec((None, SUB, LANE), lambda i: (i, 0, 0))],
        out_shape=[jax.ShapeDtypeStruct((T, D), f32), jax.ShapeDtypeStruct((nt, SUB, LANE), f32)],
        compiler_params=_cparams(("parallel",)),
    )(h, tgt)


def _rope_tables(tc, tl):
    t = jnp.arange(tl)
    inv = ROPE_THETA ** (-jnp.arange(0, HD // 2, 2, dtype=f32) / (HD // 2))
    ang = jnp.concatenate([(t // GRID_W).astype(f32)[:, None] * inv, (t % GRID_W).astype(f32)[:, None] * inv], -1)
    cos, sin = jnp.repeat(jnp.cos(ang), 2, axis=1), jnp.repeat(jnp.sin(ang), 2, axis=1)
    even = (jnp.arange(HD) % 2 == 0)[None, :]
    cos_f = jnp.concatenate([jnp.ones((tc, HD), f32), cos], 0)
    sin_a = jnp.concatenate([jnp.zeros((tc, HD), f32), jnp.where(even, -sin, 0.0)], 0)
    sin_b = jnp.concatenate([jnp.zeros((tc, HD), f32), jnp.where(even, 0.0, sin)], 0)
    return cos_f, sin_a, sin_b


def _pack_cols(w):
    pad = jnp.zeros(w.shape[:-1] + (NP - N_IN,), w.dtype)
    return jnp.concatenate([w[..., G1:N_IN], w[..., 0:R0], w[..., R1:G1], w[..., R0:R1], pad], -1)


def _unpack_cols(g):
    return jnp.concatenate([g[..., O_Q:O_DA], g[..., O_R:O_R + (R1 - R0)], g[..., O_DA:O_R], g[..., O_MG:O_Q]], -1)


def _gate_weights(w2_l, gb_l):
    w = jnp.zeros((LANE, 2 * C_KW), f32)
    w = w.at[0:C_RANK, 0:C_KW].set(w2_l[0]).at[C_RANK:2 * C_RANK, C_KW:2 * C_KW].set(w2_l[1])
    return w, jnp.concatenate([gb_l[0], gb_l[1]])[None, :]


def _local_step(x1, c1, ctx1, tgt1, c_ctx, w_mod, b_mod, wp, q_norm, k_norm, b_conv, w2, gb, c_norm, d_conv_w,
                d_conv_b, d_norm_g, d_norm_b, w_br, w_out, ln_g, ln_b, tm):
    tc, tl = ctx1.shape[0], x1.shape[0]
    T = tc + tl
    rc = min(256, tc)
    tmb = tm // 2
    rope = _rope_tables(tc, tl)
    cin = jnp.concatenate([c1, c_ctx[None, :], jnp.zeros((SUB - 2, D), f32)], 0)
    modv = _mod_fwd(cin, w_mod, b_mod)
    row = lambda v: v[None, :]
    tn_p = NP // 5

    h = jnp.concatenate([ctx1, x1], 0)
    saved = []
    for l in range(DEPTH):
        u = _ln_fwd(h, modv[l], tc, tm, f"ln_fwd{l}")
        P = _matmul(u, wp[l], "nn", tm, tn_p, D, f"proj{l}")
        qn, kn, vb = _prep_fwd(P, row(q_norm[l]), row(k_norm[l]), rope, tm, f"prep_fwd{l}")
        ya = _attn_fwd(qn, kn, vb, tc, tm, f"attn_fwd{l}")
        yb, hh = _conv_fwd(P, b_conv[l], d_conv_w[l], row(d_conv_b[l]), tc, tl, rc, f"conv_fwd{l}")
        w2p, b2p = _gate_weights(w2[l], gb[l])
        o2, ssave = _gla_fwd(P, w2p, b2p, tc, f"gla_fwd{l}")
        hn = _merge_fwd(h, modv[l], ya, yb, o2, hh, P, row(c_norm[l]), row(d_norm_g[l]), row(d_norm_b[l]),
                        row(ln_g[l]), row(ln_b[l]), w_br[l], w_out[l], tc, tm, f"merge_fwd{l}")
        saved.append((h, u, P, qn, kn, vb, ya, yb, hh, o2, ssave, w2p, b2p))
        h = hn

    dh, lparts = _loss_kernel(h, tgt1, tc, tm, "loss")
    loss = jnp.sum(lparts[:, 0, 0])

    g = {k: [None] * DEPTH for k in ("wp", "q_norm", "k_norm", "b_conv", "w2", "gb", "c_norm", "d_conv_w", "d_conv_b",
                                     "d_norm_g", "d_norm_b", "w_br", "w_out", "ln_g", "ln_b", "modv")}
    for l in reversed(range(DEPTH)):
        h_in, u, P, qn, kn, vb, ya, yb, hh, o2, ssave, w2p, b2p = saved[l]
        (dh_res, dm_mg, dya, dga, dyb, dgb, doc, dgc, dhh, dgd, dmg, br, z, acc, dy, dv5, dvd) = _merge_bwd(
            dh, h_in, modv[l], ya, yb, o2, hh, P, row(c_norm[l]), row(d_norm_g[l]), row(d_norm_b[l]),
            row(ln_g[l]), row(ln_b[l]), w_br[l], w_out[l], tc, tmb, f"merge_bwd{l}")
        g["w_br"][l] = jnp.stack([_matmul(br[i], z[i], "tn", BRW, D, T, f"dwbr{l}_{i}") for i in range(4)])
        g["w_out"][l] = _matmul(acc, dy, "tn", D, D, T, f"dwout{l}")
        v5 = jnp.sum(dv5, 0)
        g["c_norm"][l], g["d_norm_g"][l], g["d_norm_b"][l] = v5[0], v5[1], v5[2]
        vd = jnp.sum(dvd, 0)
        g["ln_g"][l], g["ln_b"][l] = vd[0], vd[1]
        dqn, dkn, dv = _attn_bwd(qn, kn, vb, dya, tc, tm, f"attn_bwd{l}")
        dq, dk, dqk = _prep_bwd(P, dqn, dkn, row(q_norm[l]), row(k_norm[l]), rope, tm, f"prep_bwd{l}")
        dqk = jnp.sum(dqk, 0)
        g["q_norm"][l], g["k_norm"][l] = dqk[0], dqk[1]
        dB, dC, dX, dDA, dDG, dwb, dwd, dbd = _conv_bwd(P, dyb, dhh, b_conv[l], d_conv_w[l], tc, tl, rc, f"conv_bwd{l}")
        g["b_conv"][l], g["d_conv_w"][l], g["d_conv_b"][l] = dwb, dwd, dbd[0]
        dq2, dk2, dv2, dr2, dw2p, db2p = _gla_bwd(P, w2p, b2p, ssave, doc, tc, f"gla_bwd{l}")
        dw2p = dw2p[0] + dw2p[1]
        db2p = db2p[0, 0] + db2p[1, 0]
        g["w2"][l] = jnp.stack([dw2p[0:C_RANK, 0:C_KW], dw2p[C_RANK:2 * C_RANK, C_KW:2 * C_KW]])
        g["gb"][l] = jnp.stack([db2p[0:C_KW], db2p[C_KW:2 * C_KW]])
        dP = jnp.concatenate([dmg, dq, dk, dv, dga, dB, dC, dX, dgb, dq2[0] + dq2[1], dk2[0] + dk2[1],
                              dv2[0] + dv2[1], dgc, dDA, dDG, dgd, dr2[0] + dr2[1]], axis=1)
        du = _matmul(dP, wp[l], "nt", tm, D, tn_p, f"du{l}")
        g["wp"][l] = _matmul(u, dP, "tn", D, NP // 17, T, f"dwp{l}")
        dh, dm_ln = _ln_bwd(du, h_in, dh_res, modv[l], tc, tm, f"ln_bwd{l}")
        g["modv"][l] = jnp.sum(dm_mg, 0) + jnp.sum(dm_ln, 0)

    dmodv = jnp.stack(g.pop("modv"))
    dw_mod, dcin = _mod_bwd(cin, w_mod, dmodv)
    grads = {k: jnp.stack(v) for k, v in g.items()}
    grads["w_mod"] = dw_mod
    grads["b_mod"] = dmodv[:, 0, :] + dmodv[:, 1, :]
    grads["c_ctx"] = jnp.sum(dcin, (0, 1))[1]
    return loss, dh[tc:], grads


def _adamw(w, g, m, v, name, tr=128):
    R, C = w.shape
    if R % tr:
        tr = R

    def body(w_ref, g_ref, m_ref, v_ref, d_ref, nm_ref, nv_ref):
        gg = g_ref[...]
        nm = B1 * m_ref[...] + (1.0 - B1) * gg
        nv = B2 * v_ref[...] + (1.0 - B2) * (gg * gg)
        m_hat = nm / (1.0 - B1 ** STEP)
        v_hat = nv / (1.0 - B2 ** STEP)
        d_ref[...] = -LR * (m_hat / (jnp.sqrt(v_hat) + AEPS) + WD * w_ref[...])
        nm_ref[...] = nm
        nv_ref[...] = nv

    spec = pl.BlockSpec((tr, C), lambda i: (i, 0))
    return pl.pallas_call(
        body, name=name, grid=(R // tr,), in_specs=[spec] * 4, out_specs=[spec] * 3,
        out_shape=[jax.ShapeDtypeStruct((R, C), f32)] * 3,
        compiler_params=_cparams(("parallel",)),
    )(w, g, m, v)


MESH = pl.DeviceIdType.MESH
ANY = pl.BlockSpec(memory_space=pl.ANY)
N_CHIPS = 4


def _place():
    x, y, c = lax.axis_index("x"), lax.axis_index("y"), lax.axis_index("c")
    chips = [(1 - x, y), (x, 1 - y), (1 - x, 1 - y)]
    return x, y, c, chips


def _rows(c, hr):
    return pl.ds(pl.multiple_of(c * hr, SUB), hr)


def _all_gather(arrs, name):
    n = len(arrs)

    def body(*refs):
        ins, outs = refs[:n], refs[n:2 * n]
        send, recv, loc = refs[2 * n:]
        x, y, c, chips = _place()
        me, sib = 2 * x + y, (x, y, 1 - c)

        def copy(a, k, chip_idx, cc, to, src=None):
            hr = ins[a].shape[0] // 2
            blk = outs[a].at[chip_idx, _rows(cc, hr), :]
            return pltpu.make_async_remote_copy(src_ref=blk if src is None else src, dst_ref=blk,
                                                send_sem=send.at[6 * a + k], recv_sem=recv.at[6 * a + k],
                                                device_id=to, device_id_type=MESH)

        local = [pltpu.make_async_copy(ins[a], outs[a].at[me], loc.at[a]) for a in range(n)]
        for cp in local:
            cp.start()
        first = [copy(a, j, me, c, (*chip, c), src=ins[a].at[_rows(c, ins[a].shape[0] // 2), :])
                 for a in range(n) for j, chip in enumerate(chips)]
        for cp in first:
            cp.start()
        passed = []
        for a in range(n):
            for j, chip in enumerate(chips):
                k = 2 * chip[0] + chip[1]
                copy(a, j, k, c, sib).wait_recv()
                fwd = copy(a, 3 + j, k, c, sib)
                fwd.start()
                passed.append(fwd)
        for a in range(n):
            for j, chip in enumerate(chips):
                copy(a, 3 + j, 2 * chip[0] + chip[1], 1 - c, sib).wait_recv()
        for cp in first + passed:
            cp.wait_send()
        for cp in local:
            cp.wait()

    return pl.pallas_call(
        body, name=name, in_specs=[ANY] * n, out_specs=[ANY] * n,
        out_shape=[jax.ShapeDtypeStruct((N_CHIPS,) + a.shape, a.dtype) for a in arrs],
        scratch_shapes=[pltpu.SemaphoreType.DMA((6 * n,)), pltpu.SemaphoreType.DMA((6 * n,)), pltpu.SemaphoreType.DMA((n,))],
    )(*arrs)


def _sibling_halves(arrs, name):
    n = len(arrs)

    def body(*refs):
        ins, outs = refs[:n], refs[n:2 * n]
        send, recv = refs[2 * n:]
        x, y, c, _ = _place()
        cps = []
        for a in range(n):
            hr = ins[a].shape[1] // 2
            cps.append(pltpu.make_async_remote_copy(src_ref=ins[a].at[:, _rows(1 - c, hr), :], dst_ref=outs[a],
                                                    send_sem=send.at[a], recv_sem=recv.at[a],
                                                    device_id=(x, y, 1 - c), device_id_type=MESH))
        for cp in cps:
            cp.start()
        for cp in cps:
            cp.wait()

    return pl.pallas_call(
        body, name=name, in_specs=[ANY] * n, out_specs=[ANY] * n,
        out_shape=[jax.ShapeDtypeStruct((a.shape[0], a.shape[1] // 2, a.shape[2]), a.dtype) for a in arrs],
        scratch_shapes=[pltpu.SemaphoreType.DMA((n,)), pltpu.SemaphoreType.DMA((n,))],
    )(*arrs)


def _add_half(gfull, land, cidx, name, tr=128):
    _, R, C = gfull.shape
    hr = R // 2
    tr = min(tr, hr)
    nb = hr // tr

    def body(c_ref, g_ref, l_ref, o_ref):
        o_ref[...] = g_ref[...] + l_ref[...]

    return pl.pallas_call(
        body, name=name,
        grid_spec=pltpu.PrefetchScalarGridSpec(
            num_scalar_prefetch=1, grid=(N_CHIPS, nb),
            in_specs=[pl.BlockSpec((None, tr, C), lambda s, i, cr: (s, cr[0] * nb + i, 0)),
                      pl.BlockSpec((None, tr, C), lambda s, i, cr: (s, i, 0))],
            out_specs=pl.BlockSpec((None, tr, C), lambda s, i, cr: (s, i, 0))),
        out_shape=jax.ShapeDtypeStruct((N_CHIPS, hr, C), f32),
        compiler_params=_cparams(("parallel", "parallel")),
    )(cidx, gfull, land)


def _chip_exchange(arrs, name):
    n = len(arrs)

    def body(*refs):
        ins, outs = refs[:n], refs[n:2 * n]
        send, recv, loc = refs[2 * n:]
        x, y, c, chips = _place()
        me = 2 * x + y
        local = [pltpu.make_async_copy(ins[a].at[me], outs[a].at[me], loc.at[a]) for a in range(n)]
        for cp in local:
            cp.start()
        cps = []
        for a in range(n):
            for j, chip in enumerate(chips):
                k = 2 * chip[0] + chip[1]
                cps.append((pltpu.make_async_remote_copy(
                    src_ref=ins[a].at[k], dst_ref=outs[a].at[me], send_sem=send.at[3 * a + j], recv_sem=recv.at[3 * a + j],
                    device_id=(*chip, c), device_id_type=MESH), a, j, k))
        for cp, *_ in cps:
            cp.start()
        for cp, a, j, k in cps:
            pltpu.make_async_remote_copy(src_ref=ins[a].at[k], dst_ref=outs[a].at[k], send_sem=send.at[3 * a + j],
                                         recv_sem=recv.at[3 * a + j], device_id=(x, y, c), device_id_type=MESH).wait_recv()
        for cp, *_ in cps:
            cp.wait_send()
        for cp in local:
            cp.wait()

    return pl.pallas_call(
        body, name=name, in_specs=[ANY] * n, out_specs=[ANY] * n,
        out_shape=[jax.ShapeDtypeStruct(a.shape, a.dtype) for a in arrs],
        scratch_shapes=[pltpu.SemaphoreType.DMA((3 * n,)), pltpu.SemaphoreType.DMA((3 * n,)), pltpu.SemaphoreType.DMA((n,))],
    )(*arrs)


def _sum_chips(land, name, tr=128):
    _, R, C = land.shape
    tr = min(tr, R)

    def body(l_ref, o_ref):
        o_ref[...] = ((l_ref[0] + l_ref[1]) + l_ref[2]) + l_ref[3]

    return pl.pallas_call(
        body, name=name, grid=(R // tr,),
        in_specs=[pl.BlockSpec((N_CHIPS, tr, C), lambda i: (0, i, 0))], out_specs=pl.BlockSpec((tr, C), lambda i: (i, 0)),
        out_shape=jax.ShapeDtypeStruct((R, C), f32),
        compiler_params=_cparams(("parallel",)),
    )(land)


def _sibling_concat(arrs, name):
    n = len(arrs)

    def body(*refs):
        ins, outs = refs[:n], refs[n:2 * n]
        send, recv, loc = refs[2 * n:]
        x, y, c, _ = _place()
        local, cps = [], []
        for a in range(n):
            hr = ins[a].shape[0]
            local.append(pltpu.make_async_copy(ins[a], outs[a].at[_rows(c, hr), :], loc.at[a]))
            cps.append(pltpu.make_async_remote_copy(src_ref=ins[a], dst_ref=outs[a].at[_rows(c, hr), :],
                                                    send_sem=send.at[a], recv_sem=recv.at[a],
                                                    device_id=(x, y, 1 - c), device_id_type=MESH))
        for cp in local + cps:
            cp.start()
        for a in range(n):
            hr = ins[a].shape[0]
            pltpu.make_async_remote_copy(src_ref=ins[a], dst_ref=outs[a].at[_rows(1 - c, hr), :], send_sem=send.at[a],
                                         recv_sem=recv.at[a], device_id=(x, y, 1 - c), device_id_type=MESH).wait_recv()
        for cp in cps:
            cp.wait_send()
        for cp in local:
            cp.wait()

    return pl.pallas_call(
        body, name=name, in_specs=[ANY] * n, out_specs=[ANY] * n,
        out_shape=[jax.ShapeDtypeStruct((2 * a.shape[0], a.shape[1]), a.dtype) for a in arrs],
        scratch_shapes=[pltpu.SemaphoreType.DMA((n,)), pltpu.SemaphoreType.DMA((n,)), pltpu.SemaphoreType.DMA((n,))],
    )(*arrs)


N_DEV = 8


def _all_reduce_small(v, name):
    R = v.shape[0]

    def body(v_ref, o_ref, land_ref, send, recv):
        x, y, c, _ = _place()
        me = 4 * x + 2 * y + c
        land_ref[me] = v_ref[...]
        cps = []
        for m in range(1, N_DEV):
            px, py, pc = [(1 - q) if (m >> s) & 1 else q for q, s in ((x, 2), (y, 1), (c, 0))]
            cps.append((pltpu.make_async_remote_copy(src_ref=v_ref, dst_ref=land_ref.at[me], send_sem=send.at[m - 1],
                                                     recv_sem=recv.at[m - 1], device_id=(px, py, pc), device_id_type=MESH),
                        4 * px + 2 * py + pc, m))
        for cp, *_ in cps:
            cp.start()
        for cp, peer, m in cps:
            pltpu.make_async_remote_copy(src_ref=v_ref, dst_ref=land_ref.at[peer], send_sem=send.at[m - 1],
                                         recv_sem=recv.at[m - 1], device_id=(x, y, c), device_id_type=MESH).wait_recv()
        for cp, *_ in cps:
            cp.wait_send()
        acc = land_ref[0]
        for k in range(1, N_DEV):
            acc = acc + land_ref[k]
        o_ref[...] = acc

    vm = pl.BlockSpec(memory_space=pltpu.VMEM)
    return pl.pallas_call(
        body, name=name, in_specs=[vm], out_specs=vm, out_shape=jax.ShapeDtypeStruct(v.shape, f32),
        scratch_shapes=[pltpu.VMEM((N_DEV, R, LANE), f32), pltpu.SemaphoreType.DMA((N_DEV - 1,)),
                        pltpu.SemaphoreType.DMA((N_DEV - 1,))],
        compiler_params=pltpu.CompilerParams(vmem_limit_bytes=VMEM_LIMIT),
    )(v)


def _pack_small(arrs, mult=2 * SUB):
    flat = jnp.concatenate([a.reshape(-1) for a in arrs])
    rows = -(-flat.shape[0] // (LANE * mult)) * mult
    return jnp.pad(flat, (0, rows * LANE - flat.shape[0])).reshape(rows, LANE)


def _unpack_small(vec, shapes):
    flat, out, o = vec.reshape(-1), [], 0
    for s in shapes:
        n = int(np.prod(s))
        out.append(flat[o:o + n].reshape(s))
        o += n
    return out


REPL_SMALL = ("c_ctx", "b_mod", "q_norm", "k_norm", "c_norm", "d_conv_b", "d_norm_g", "d_norm_b", "ln_g", "ln_b")
SHARD_SMALL = ("b_conv", "c_gate_w2", "c_gate_b", "d_conv_w")
BIG = ("w_mod", "w_in", "w_br", "w_out")
ORDER = ("c_ctx", "w_mod", "b_mod", "w_in", "q_norm", "k_norm", "b_conv", "c_gate_w2", "c_gate_b", "c_norm", "d_conv_w",
         "d_conv_b", "d_norm_g", "d_norm_b", "w_br", "w_out", "ln_g", "ln_b")


def _unshard_last(g4, shard_shape):
    g = g4.reshape((N_CHIPS,) + tuple(shard_shape))
    g = jnp.moveaxis(g, 0, -2)
    return g.reshape(tuple(shard_shape[:-1]) + (N_CHIPS * shard_shape[-1],))


def _pieces_last(full):
    w = full.shape[-1] // N_CHIPS
    g = full.reshape(full.shape[:-1] + (N_CHIPS, w))
    return jnp.moveaxis(g, -2, 0).reshape(N_CHIPS, -1, w)


def kernel(x, c, ctx, c_ctx, w_mod, b_mod, w_in, q_norm, k_norm, b_conv, c_gate_w2, c_gate_b, c_norm, d_conv_w, d_conv_b, d_norm_g, d_norm_b, w_br, w_out, ln_g, ln_b, loss_target, m_c_ctx, m_w_mod, m_b_mod, m_w_in, m_q_norm, m_k_norm, m_b_conv, m_c_gate_w2, m_c_gate_b, m_c_norm, m_d_conv_w, m_d_conv_b, m_d_norm_g, m_d_norm_b, m_w_br, m_w_out, m_ln_g, m_ln_b, v_c_ctx, v_w_mod, v_b_mod, v_w_in, v_q_norm, v_k_norm, v_b_conv, v_c_gate_w2, v_c_gate_b, v_c_norm, v_d_conv_w, v_d_conv_b, v_d_norm_g, v_d_norm_b, v_w_br, v_w_out, v_ln_g, v_ln_b):
    W = dict(c_ctx=c_ctx, w_mod=w_mod, b_mod=b_mod, w_in=w_in, q_norm=q_norm, k_norm=k_norm, b_conv=b_conv,
             c_gate_w2=c_gate_w2, c_gate_b=c_gate_b, c_norm=c_norm, d_conv_w=d_conv_w, d_conv_b=d_conv_b,
             d_norm_g=d_norm_g, d_norm_b=d_norm_b, w_br=w_br, w_out=w_out, ln_g=ln_g, ln_b=ln_b)
    M = dict(c_ctx=m_c_ctx, w_mod=m_w_mod, b_mod=m_b_mod, w_in=m_w_in, q_norm=m_q_norm, k_norm=m_k_norm, b_conv=m_b_conv,
             c_gate_w2=m_c_gate_w2, c_gate_b=m_c_gate_b, c_norm=m_c_norm, d_conv_w=m_d_conv_w, d_conv_b=m_d_conv_b,
             d_norm_g=m_d_norm_g, d_norm_b=m_d_norm_b, w_br=m_w_br, w_out=m_w_out, ln_g=m_ln_g, ln_b=m_ln_b)
    V = dict(c_ctx=v_c_ctx, w_mod=v_w_mod, b_mod=v_b_mod, w_in=v_w_in, q_norm=v_q_norm, k_norm=v_k_norm, b_conv=v_b_conv,
             c_gate_w2=v_c_gate_w2, c_gate_b=v_c_gate_b, c_norm=v_c_norm, d_conv_w=v_d_conv_w, d_conv_b=v_d_conv_b,
             d_norm_g=v_d_norm_g, d_norm_b=v_d_norm_b, w_br=v_w_br, w_out=v_w_out, ln_g=v_ln_g, ln_b=v_ln_b)
    chip = 2 * lax.axis_index("x") + lax.axis_index("y")
    cidx = lax.axis_index("c").astype(jnp.int32).reshape(1)

    big2d = {k: W[k].reshape(-1, W[k].shape[-1]) for k in BIG}
    small_shard = _pack_small([W[k] for k in SHARD_SMALL])
    gathered = _all_gather([big2d[k] for k in BIG] + [small_shard], "all_gather")
    G = dict(zip(BIG, gathered[:4]))
    full = {k: _unshard_last(G[k], W[k].shape) for k in ("w_mod", "w_in", "w_br")}
    full["w_out"] = jnp.moveaxis(G["w_out"].reshape((N_CHIPS,) + w_out.shape), 0, 1).reshape(DEPTH, D, D)
    smalls = [_unpack_small(gathered[4][s], [W[k].shape for k in SHARD_SMALL]) for s in range(N_CHIPS)]
    for i, k in enumerate(SHARD_SMALL):
        full[k] = jnp.concatenate([smalls[s][i] for s in range(N_CHIPS)], axis=-1)

    wp = _pack_cols(full["w_in"]).astype(bf16)
    loss, gx, g = _local_step(
        x[0], c, ctx[0], loss_target[0], c_ctx, full["w_mod"], b_mod, wp, q_norm, k_norm, full["b_conv"],
        full["c_gate_w2"], full["c_gate_b"], c_norm, full["d_conv_w"], d_conv_b, d_norm_g, d_norm_b,
        full["w_br"].astype(bf16), full["w_out"].astype(bf16), ln_g, ln_b, tm=256)
    g["w_in"] = _unpack_cols(g.pop("wp"))
    g["c_gate_w2"], g["c_gate_b"] = g.pop("w2"), g.pop("gb")
    loss = lax.psum(loss, ("x", "y", "c"))

    pieces = [_pieces_last(g[k]) for k in ("w_mod", "w_in", "w_br")]
    pieces.append(jnp.moveaxis(g["w_out"].reshape(DEPTH, N_CHIPS, D // N_CHIPS, D), 1, 0).reshape(N_CHIPS, -1, D))
    land_a = _sibling_halves(pieces, "rs_sibling_halves")
    pair = [_add_half(p, la, cidx, f"rs_pair_sum{i}") for i, (p, la) in enumerate(zip(pieces, land_a))]
    land_b = _chip_exchange(pair, "rs_chip_exchange")
    half = [_sum_chips(lb, f"rs_chip_sum{i}") for i, lb in enumerate(land_b)]
    red = dict(zip(BIG, _sibling_concat(half, "rs_sibling_concat")))

    small_names = REPL_SMALL + SHARD_SMALL
    gs = _all_reduce_small(_pack_small([g[k] for k in small_names]), "all_reduce_small")
    gsm = dict(zip(small_names, _unpack_small(gs, [g[k].shape for k in small_names])))
    for k in SHARD_SMALL:
        wdt = W[k].shape[-1]
        gsm[k] = lax.dynamic_slice_in_dim(gsm[k], chip * wdt, wdt, axis=gsm[k].ndim - 1)

    grad, delta, new_m, new_v = {}, {}, {}, {}
    for k in BIG:
        grad[k] = red[k].reshape(W[k].shape)
        d_, m_, v_ = _adamw(big2d[k], red[k], M[k].reshape(red[k].shape), V[k].reshape(red[k].shape), f"adamw_{k}")
        delta[k], new_m[k], new_v[k] = d_.reshape(W[k].shape), m_.reshape(W[k].shape), v_.reshape(W[k].shape)
    shapes = [W[k].shape for k in small_names]
    d_, m_, v_ = _adamw(_pack_small([W[k] for k in small_names]), _pack_small([gsm[k] for k in small_names]),
                        _pack_small([M[k] for k in small_names]), _pack_small([V[k] for k in small_names]), "adamw_small")
    for k, dd, mm_, vv in zip(small_names, _unpack_small(d_, shapes), _unpack_small(m_, shapes), _unpack_small(v_, shapes)):
        grad[k], delta[k], new_m[k], new_v[k] = gsm[k], dd, mm_, vv

    return (loss, gx[None], *[grad[k] for k in ORDER], *[delta[k] for k in ORDER], *[new_m[k] for k in ORDER],
            *[new_v[k] for k in ORDER])
```

```python
import functools

import jax
import jax.numpy as jnp
import numpy as np
from jax import lax
from jax.experimental import pallas as pl
from jax.experimental.pallas import tpu as pltpu

f32 = jnp.float32
bf16 = jnp.bfloat16

D = 1024
DEPTH = 2
GRID_W = 64
BRW = 512
HD = 128
A_HEADS = 4
C_HEADS = 4
C_KW = 256
C_RANK = 16
C_TAU = 16.0
CH = 64
KB = 3
KD = 31
ALPHA = (2 * DEPTH) ** 0.25
EPS = 1e-6
ROPE_THETA = 10000.0
N_IN = 10784
LR, B1, B2, AEPS, WD, STEP = 0.001, 0.9, 0.999, 1e-08, 0.01, 10

W_M, W_A, W_C, W_G = 4 * D + 4 * BRW, 1024, 5 * BRW, 1152
GROUPS = ("M", "A", "C", "G")
GROUP_W = dict(M=W_M, A=W_A, C=W_C, G=W_G)
M_GA, M_GB, M_GC, M_GD = 4 * D, 4 * D + BRW, 4 * D + 2 * BRW, 4 * D + 3 * BRW
A_K, A_V = 512, 768
G_K, G_V, G_R = 256, 512, 1024
CT = 5 * 128
S_Q, S_GA, S_B, S_C, S_X, S_GB, S_CQ, S_CV, S_GC, S_R, S_DA, S_DG, S_GD, S_MG = (
    0, 1024, 1536, 2048, 2560, 3072, 3584, 4096, 4608, 5120, 5152, 5664, 6176, 6688)

LANE = 128
SUB = 8
VMEM_LIMIT = 56 * 1024 * 1024
CONV_PAD = 16
GLA_SUB = 16
GLA_CLAMP = 60.0


def _cparams(sem, vmem=VMEM_LIMIT):
    return pltpu.CompilerParams(dimension_semantics=sem, vmem_limit_bytes=vmem)


def _dg(a, b, ca, cb):
    return lax.dot_general(a.astype(bf16), b.astype(bf16), (((ca,), (cb,)), ((), ())),
                           preferred_element_type=f32)


@jax.custom_vjp
def mm(a, b):
    return _dg(a, b, 1, 0)


mm.defvjp(lambda a, b: (_dg(a, b, 1, 0), (a, b)),
          lambda r, ct: (_dg(ct, r[1], 1, 1).astype(r[0].dtype), _dg(r[0], ct, 0, 0).astype(r[1].dtype)))


@jax.custom_vjp
def mm_nt(a, b):
    return _dg(a, b, 1, 1)


mm_nt.defvjp(lambda a, b: (_dg(a, b, 1, 1), (a, b)),
             lambda r, ct: (_dg(ct, r[1], 1, 0).astype(r[0].dtype), _dg(ct, r[0], 0, 0).astype(r[1].dtype)))


@jax.custom_vjp
def mm_tn(a, b):
    return _dg(a, b, 0, 0)


mm_tn.defvjp(lambda a, b: (_dg(a, b, 0, 0), (a, b)),
             lambda r, ct: (_dg(r[1], ct, 1, 1).astype(r[0].dtype), _dg(r[0], ct, 1, 0).astype(r[1].dtype)))


def _silu(x):
    return x * jax.nn.sigmoid(x)


def _ln(x):
    mu = jnp.mean(x, -1, keepdims=True)
    xc = x - mu
    var = jnp.mean(xc * xc, -1, keepdims=True)
    return xc * lax.rsqrt(var + EPS)


def _rms(x, g):
    return x * lax.rsqrt(jnp.mean(x * x, -1, keepdims=True) + EPS) * g


@jax.custom_vjp
def _rope(x, cos_f, sin_a, sin_b):
    return x * cos_f + pltpu.roll(x, HD - 1, 1) * sin_a + pltpu.roll(x, 1, 1) * sin_b


def _rope_fwd(x, cos_f, sin_a, sin_b):
    return _rope(x, cos_f, sin_a, sin_b), (cos_f, sin_a, sin_b)


def _rope_bwd(r, ct):
    cos_f, sin_a, sin_b = r
    dx = ct * cos_f + pltpu.roll(ct * sin_a, 1, 1) + pltpu.roll(ct * sin_b, HD - 1, 1)
    return dx, jnp.zeros_like(cos_f), jnp.zeros_like(sin_a), jnp.zeros_like(sin_b)


_rope.defvjp(_rope_fwd, _rope_bwd)


def _row_ids(i, tm):
    return i * tm + lax.broadcasted_iota(jnp.int32, (tm, 1), 0)


def _partial_rows(ref, rows):
    n = len(rows)
    for k, r in enumerate(rows):
        ref[k:k + 1, :] = r
    ref[n:SUB, :] = jnp.zeros((SUB - n, ref.shape[-1]), f32)


def _matmul(a, b, mode, tm, tn, tk, name, out_dtype=f32, add=None):
    if mode == "nn":
        (M, K), N = a.shape, b.shape[1]
        a_spec = pl.BlockSpec((tm, tk), lambda j, i, k: (i, k))
        b_spec = pl.BlockSpec((tk, tn), lambda j, i, k: (k, j))
        ca, cb = 1, 0
    elif mode == "nt":
        (M, K), N = a.shape, b.shape[0]
        a_spec = pl.BlockSpec((tm, tk), lambda j, i, k: (i, k))
        b_spec = pl.BlockSpec((tn, tk), lambda j, i, k: (j, k))
        ca, cb = 1, 1
    else:
        (K, M), N = a.shape, b.shape[1]
        a_spec = pl.BlockSpec((tk, tm), lambda j, i, k: (k, i))
        b_spec = pl.BlockSpec((tk, tn), lambda j, i, k: (k, j))
        ca, cb = 0, 0
    assert M % tm == 0 and N % tn == 0 and K % tk == 0, (name, M, N, K, tm, tn, tk)
    nk = K // tk

    o_spec = pl.BlockSpec((tm, tn), lambda j, i, k: (i, j))

    def body(a_ref, b_ref, *rest):
        add_ref = rest[0] if add is not None else None
        o_ref, acc_ref = rest[-2:]
        k = pl.program_id(2)
        part = _dg(a_ref[...], b_ref[...], ca, cb)

        @pl.when(k == 0)
        def _():
            acc_ref[...] = part if add_ref is None else part + add_ref[...]

        @pl.when(k > 0)
        def _():
            acc_ref[...] += part

        @pl.when(k == nk - 1)
        def _():
            o_ref[...] = acc_ref[...].astype(o_ref.dtype)

    return pl.pallas_call(
        body, name=name, grid=(N // tn, M // tm, nk),
        in_specs=[a_spec, b_spec] + ([o_spec] if add is not None else []), out_specs=o_spec,
        out_shape=jax.ShapeDtypeStruct((M, N), out_dtype),
        scratch_shapes=[pltpu.VMEM((tm, tn), f32)],
        compiler_params=_cparams(("parallel", "parallel", "arbitrary")),
    )(a, b, *([add] if add is not None else []))


def _matmul_tn_batched(a, b, name):
    B, K, M = a.shape
    N = b.shape[2]

    def body(a_ref, b_ref, o_ref):
        o_ref[...] = _dg(a_ref[...], b_ref[...], 0, 0)

    return pl.pallas_call(
        body, name=name, grid=(B,),
        in_specs=[pl.BlockSpec((None, K, M), lambda i: (i, 0, 0)), pl.BlockSpec((None, K, N), lambda i: (i, 0, 0))],
        out_specs=pl.BlockSpec((None, M, N), lambda i: (i, 0, 0)),
        out_shape=jax.ShapeDtypeStruct((B, M, N), f32),
        compiler_params=_cparams(("parallel",)),
    )(a, b)


MOD_TN = 768


def _mod_fwd(cin, w_mod, b_mod):
    def body(c_ref, w_ref, b_ref, o_ref):
        o_ref[...] = mm(_silu(c_ref[...]), w_ref[...]) + b_ref[...]

    return pl.pallas_call(
        body, name="mod_fwd", grid=(DEPTH, 3 * D // MOD_TN),
        in_specs=[pl.BlockSpec((SUB, D), lambda l, j: (0, 0)),
                  pl.BlockSpec((None, D, MOD_TN), lambda l, j: (l, 0, j)),
                  pl.BlockSpec((None, 1, MOD_TN), lambda l, j: (l, 0, j))],
        out_specs=pl.BlockSpec((None, SUB, MOD_TN), lambda l, j: (l, 0, j)),
        out_shape=jax.ShapeDtypeStruct((DEPTH, SUB, 3 * D), f32),
        compiler_params=_cparams(("parallel", "parallel")),
    )(cin, w_mod, b_mod.reshape(DEPTH, 1, 3 * D))


def _mod_bwd(cin, w_mod, dmodv):
    nj = 3 * D // MOD_TN

    def body(c_ref, w_ref, g_ref, dw_ref, dc_ref):
        _, vjp = jax.vjp(lambda c, w: mm(_silu(c), w), c_ref[...], w_ref[...].astype(f32))
        dc, dw = vjp(g_ref[...])
        dw_ref[...] = dw
        dc_ref[...] = dc

    return pl.pallas_call(
        body, name="mod_bwd", grid=(DEPTH, nj),
        in_specs=[pl.BlockSpec((SUB, D), lambda l, j: (0, 0)),
                  pl.BlockSpec((None, D, MOD_TN), lambda l, j: (l, 0, j)),
                  pl.BlockSpec((None, SUB, MOD_TN), lambda l, j: (l, 0, j))],
        out_specs=[pl.BlockSpec((None, D, MOD_TN), lambda l, j: (l, 0, j)),
                   pl.BlockSpec((None, None, SUB, D), lambda l, j: (l, j, 0, 0))],
        out_shape=[jax.ShapeDtypeStruct((DEPTH, D, 3 * D), f32),
                   jax.ShapeDtypeStruct((DEPTH, nj, SUB, D), f32)],
        compiler_params=_cparams(("parallel", "parallel")),
    )(cin, w_mod, dmodv)


def _u_fn(h, m_l, m_c, isctx):
    n = _ln(h)
    shift = jnp.where(isctx, m_c[:, 0:D], m_l[:, 0:D])
    scale = jnp.where(isctx, m_c[:, D:2 * D], m_l[:, D:2 * D])
    return n * (1.0 + scale) + shift


def _ln_fwd(h, modv_l, tc, tm, name):
    T = h.shape[0]

    def body(h_ref, m_ref, u_ref):
        isctx = _row_ids(pl.program_id(0), tm) < tc
        u_ref[...] = _u_fn(h_ref[...], m_ref[0:1, :], m_ref[1:2, :], isctx).astype(bf16)

    return pl.pallas_call(
        body, name=name, grid=(T // tm,),
        in_specs=[pl.BlockSpec((tm, D), lambda i: (i, 0)), pl.BlockSpec((SUB, 3 * D), lambda i: (0, 0))],
        out_specs=pl.BlockSpec((tm, D), lambda i: (i, 0)),
        out_shape=jax.ShapeDtypeStruct((T, D), bf16),
        compiler_params=_cparams(("parallel",)),
    )(h, modv_l)


def _ln_bwd(du, h, dh_res, modv_l, tc, tm, name):
    T = h.shape[0]
    nt = T // tm

    def body(du_ref, h_ref, r_ref, m_ref, dh_ref, dm_ref):
        isctx = _row_ids(pl.program_id(0), tm) < tc
        _, vjp = jax.vjp(lambda h, ml, mc: _u_fn(h, ml, mc, isctx), h_ref[...], m_ref[0:1, :], m_ref[1:2, :])
        dh, dml, dmc = vjp(du_ref[...])
        dh_ref[...] = dh + r_ref[...]
        _partial_rows(dm_ref, [dml, dmc])

    return pl.pallas_call(
        body, name=name, grid=(nt,),
        in_specs=[pl.BlockSpec((tm, D), lambda i: (i, 0)), pl.BlockSpec((tm, D), lambda i: (i, 0)),
                  pl.BlockSpec((tm, D), lambda i: (i, 0)), pl.BlockSpec((SUB, 3 * D), lambda i: (0, 0))],
        out_specs=[pl.BlockSpec((tm, D), lambda i: (i, 0)), pl.BlockSpec((None, SUB, 3 * D), lambda i: (i, 0, 0))],
        out_shape=[jax.ShapeDtypeStruct((T, D), f32), jax.ShapeDtypeStruct((nt, SUB, 3 * D), f32)],
        compiler_params=_cparams(("parallel",)),
    )(du, h, dh_res, modv_l)


def _prep_fn(q, k, qg, kg, cos_f, sin_a, sin_b):
    qs = [_rope(_rms(q[:, HD * i:HD * (i + 1)], qg), cos_f, sin_a, sin_b) for i in range(A_HEADS)]
    ks = [_rope(_rms(k[:, HD * i:HD * (i + 1)], kg), cos_f, sin_a, sin_b) for i in range(A_HEADS // 2)]
    return jnp.concatenate(qs, 1), jnp.concatenate(ks, 1)


def _tok(tm, w, off):
    return pl.BlockSpec((tm, w), lambda i: (i, off // w))


def _vec(w):
    return pl.BlockSpec((1, w), lambda i: (0, 0))


def _prep_fwd(P, qg, kg, rope, tm, name):
    T = P.shape[0]

    def body(q_ref, k_ref, v_ref, qg_ref, kg_ref, c_ref, sa_ref, sb_ref, qn_ref, kn_ref, vb_ref):
        qn, kn = _prep_fn(q_ref[...], k_ref[...], qg_ref[...], kg_ref[...], c_ref[...], sa_ref[...], sb_ref[...])
        qn_ref[...] = qn.astype(bf16)
        kn_ref[...] = kn.astype(bf16)
        vb_ref[...] = v_ref[...].astype(bf16)

    return pl.pallas_call(
        body, name=name, grid=(T // tm,),
        in_specs=[_tok(tm, 512, 0), _tok(tm, 256, A_K), _tok(tm, 256, A_V), _vec(HD), _vec(HD),
                  _tok(tm, HD, 0), _tok(tm, HD, 0), _tok(tm, HD, 0)],
        out_specs=[_tok(tm, 512, 0), _tok(tm, 256, 0), _tok(tm, 256, 0)],
        out_shape=[jax.ShapeDtypeStruct((T, 512), bf16), jax.ShapeDtypeStruct((T, 256), bf16),
                   jax.ShapeDtypeStruct((T, 256), bf16)],
        compiler_params=_cparams(("parallel",)),
    )(P, P, P, qg, kg, *rope)


def _prep_bwd(P, dqn, dkn, dv, qg, kg, rope, tm, name):
    T = P.shape[0]
    nt = T // tm

    def body(q_ref, k_ref, dq_ref, dk_ref, dv_ref, qg_ref, kg_ref, c_ref, sa_ref, sb_ref, o_ref, og_ref):
        tabs = (c_ref[...], sa_ref[...], sb_ref[...])
        _, vjp = jax.vjp(lambda q, k, a, b: _prep_fn(q, k, a, b, *tabs), q_ref[...], k_ref[...], qg_ref[...], kg_ref[...])
        dq, dk, dqg, dkg = vjp((dq_ref[...], dk_ref[...]))
        o_ref[:, 0:A_K] = dq
        o_ref[:, A_K:A_V] = dk
        o_ref[:, A_V:W_A] = dv_ref[...]
        _partial_rows(og_ref, [dqg, dkg])

    return pl.pallas_call(
        body, name=name, grid=(nt,),
        in_specs=[_tok(tm, 512, 0), _tok(tm, 256, A_K), _tok(tm, 512, 0), _tok(tm, 256, 0), _tok(tm, 256, 0),
                  _vec(HD), _vec(HD), _tok(tm, HD, 0), _tok(tm, HD, 0), _tok(tm, HD, 0)],
        out_specs=[_tok(tm, W_A, 0), pl.BlockSpec((None, SUB, HD), lambda i: (i, 0, 0))],
        out_shape=[jax.ShapeDtypeStruct((T, W_A), f32), jax.ShapeDtypeStruct((nt, SUB, HD), f32)],
        compiler_params=_cparams(("parallel",)),
    )(P, P, dqn, dkn, dv, qg, kg, *rope)


def _attn_fn(q, k, v, lim):
    s = mm_nt(q, k) * (HD ** -0.5)
    col = lax.broadcasted_iota(jnp.int32, s.shape, 1)
    s = jnp.where(col < lim, s, -1e30)
    m = jnp.max(s, -1, keepdims=True)
    e = jnp.exp(s - m)
    p = e / jnp.sum(e, -1, keepdims=True)
    return mm(p, v)


def _attn_fwd(qn, kn, vb, tc, tq, name):
    T = qn.shape[0]

    def body(q_ref, k_ref, v_ref, o_ref):
        lim = jnp.where(pl.program_id(1) * tq < tc, tc, T)
        o_ref[...] = _attn_fn(q_ref[...], k_ref[...], v_ref[...], lim)

    return pl.pallas_call(
        body, name=name, grid=(A_HEADS, T // tq),
        in_specs=[pl.BlockSpec((tq, HD), lambda h, i: (i, h)), pl.BlockSpec((T, HD), lambda h, i: (0, h // 2)),
                  pl.BlockSpec((T, HD), lambda h, i: (0, h // 2))],
        out_specs=pl.BlockSpec((tq, HD), lambda h, i: (i, h)),
        out_shape=jax.ShapeDtypeStruct((T, 512), f32),
        compiler_params=_cparams(("parallel", "parallel")),
    )(qn, kn, vb)


def _attn_bwd(qn, kn, vb, dya, tc, tq, name):
    T = qn.shape[0]

    def body(q_ref, k_ref, v_ref, g_ref, dq_ref, dk_ref, dv_ref):
        first = (pl.program_id(1) == 0) & (pl.program_id(2) == 0)
        lim = jnp.where(pl.program_id(2) * tq < tc, tc, T)
        _, vjp = jax.vjp(lambda q, k, v: _attn_fn(q, k, v, lim), q_ref[...].astype(f32), k_ref[...].astype(f32),
                         v_ref[...].astype(f32))
        dq, dk, dv = vjp(g_ref[...])
        dq_ref[...] = dq

        @pl.when(first)
        def _():
            dk_ref[...] = dk
            dv_ref[...] = dv

        @pl.when(jnp.logical_not(first))
        def _():
            dk_ref[...] += dk
            dv_ref[...] += dv

    qspec = pl.BlockSpec((tq, HD), lambda kv, g, i: (i, 2 * kv + g))
    kspec = pl.BlockSpec((T, HD), lambda kv, g, i: (0, kv))
    return pl.pallas_call(
        body, name=name, grid=(A_HEADS // 2, 2, T // tq),
        in_specs=[qspec, kspec, kspec, qspec], out_specs=[qspec, kspec, kspec],
        out_shape=[jax.ShapeDtypeStruct((T, 512), f32), jax.ShapeDtypeStruct((T, 256), f32),
                   jax.ShapeDtypeStruct((T, 256), f32)],
        compiler_params=_cparams(("parallel", "arbitrary", "arbitrary")),
    )(qn, kn, vb, dya)


def _conv_rows(tc, tl):
    return CONV_PAD + tc + CONV_PAD + tl + CONV_PAD


def _fill_pad(pad_ref, val, tc, tl):
    z = jnp.zeros((CONV_PAD, LANE), f32)
    pad_ref[0:CONV_PAD, :] = z
    pad_ref[CONV_PAD:CONV_PAD + tc, :] = val[0:tc]
    pad_ref[CONV_PAD + tc:2 * CONV_PAD + tc, :] = z
    pad_ref[2 * CONV_PAD + tc:2 * CONV_PAD + tc + tl, :] = val[tc:tc + tl]
    pad_ref[2 * CONV_PAD + tc + tl:3 * CONV_PAD + tc + tl, :] = z


def _conv_apply(pad_ref, w_ref, K, tc, tl, rc, emit, flip=False):
    half = K // 2
    for seg0, off, n in ((0, CONV_PAD, tc), (tc, 2 * CONV_PAD + tc, tl)):
        for r0 in range(0, n, rc):
            acc = None
            for k in range(K):
                sh = (half - k) if flip else (k - half)
                term = pad_ref[pl.ds(off + r0 + sh, rc), :] * w_ref[k:k + 1, :]
                acc = term if acc is None else acc + term
            emit(seg0 + r0, acc)


def _conv_wgrad(pad_ref, dy_ref, K, tc, tl, rc, dw_ref):
    half = K // 2
    for k in range(K):
        acc = jnp.zeros((1, LANE), f32)
        for seg0, off, n in ((0, CONV_PAD, tc), (tc, 2 * CONV_PAD + tc, tl)):
            for r0 in range(0, n, rc):
                acc = acc + jnp.sum(pad_ref[pl.ds(off + r0 + k - half, rc), :] * dy_ref[pl.ds(seg0 + r0, rc), :],
                                    axis=0, keepdims=True)
        dw_ref[k:k + 1, :] = acc


def _col(T, off):
    return pl.BlockSpec((T, LANE), lambda j: (0, off // LANE + j))


def _ctile(T):
    return pl.BlockSpec((T, CT), lambda j: (0, j))


C_B, C_C, C_X, C_A, C_G = (slice(LANE * i, LANE * (i + 1)) for i in range(5))


def _conv_fwd(P, wb, wd, bd, tc, tl, rc, name):
    T = tc + tl

    def body(p_ref, wb_ref, wd_ref, bd_ref, yb_ref, hh_ref, pad_ref):
        _fill_pad(pad_ref, p_ref[:, C_C] * p_ref[:, C_X], tc, tl)

        def emit_b(r0, y):
            yb_ref[pl.ds(r0, rc), :] = y * p_ref[pl.ds(r0, rc), C_B]

        _conv_apply(pad_ref, wb_ref, KB, tc, tl, rc, emit_b)
        _fill_pad(pad_ref, p_ref[:, C_A] * jax.nn.sigmoid(p_ref[:, C_G]), tc, tl)

        def emit_d(r0, y):
            hh_ref[pl.ds(r0, rc), :] = y + bd_ref[...]

        _conv_apply(pad_ref, wd_ref, KD, tc, tl, rc, emit_d)

    return pl.pallas_call(
        body, name=name, grid=(BRW // LANE,),
        in_specs=[_ctile(T), pl.BlockSpec((KB, LANE), lambda j: (0, j)), pl.BlockSpec((KD, LANE), lambda j: (0, j)),
                  pl.BlockSpec((1, LANE), lambda j: (0, j))],
        out_specs=[_col(T, 0), _col(T, 0)],
        out_shape=[jax.ShapeDtypeStruct((T, BRW), f32), jax.ShapeDtypeStruct((T, BRW), f32)],
        scratch_shapes=[pltpu.VMEM((_conv_rows(tc, tl), LANE), f32)],
        compiler_params=_cparams(("parallel",)),
    )(P, wb, wd, bd)


def _conv_bwd(P, dyb, dhh, wb, wd, tc, tl, rc, name):
    T = tc + tl

    def body(p_ref, dyb_ref, dhh_ref, wb_ref, wd_ref, dp_ref, dwb_ref, dwd_ref, dbd_ref, pad_ref, pad2_ref, tmp_ref):
        _fill_pad(pad_ref, p_ref[:, C_C] * p_ref[:, C_X], tc, tl)

        def emit_cv(r0, y):
            dp_ref[pl.ds(r0, rc), C_B] = y * dyb_ref[pl.ds(r0, rc), :]

        _conv_apply(pad_ref, wb_ref, KB, tc, tl, rc, emit_cv)
        tmp_ref[...] = dyb_ref[...] * p_ref[:, C_B]
        _conv_wgrad(pad_ref, tmp_ref, KB, tc, tl, rc, dwb_ref)
        _fill_pad(pad2_ref, tmp_ref[...], tc, tl)

        def emit_ds(r0, y):
            dp_ref[pl.ds(r0, rc), C_C] = y * p_ref[pl.ds(r0, rc), C_X]
            dp_ref[pl.ds(r0, rc), C_X] = y * p_ref[pl.ds(r0, rc), C_C]

        _conv_apply(pad2_ref, wb_ref, KB, tc, tl, rc, emit_ds, flip=True)
        _fill_pad(pad_ref, p_ref[:, C_A] * jax.nn.sigmoid(p_ref[:, C_G]), tc, tl)
        _conv_wgrad(pad_ref, dhh_ref, KD, tc, tl, rc, dwd_ref)
        dbd_ref[...] = jnp.sum(dhh_ref[...], axis=0, keepdims=True)
        _fill_pad(pad2_ref, dhh_ref[...], tc, tl)

        def emit_d2(r0, y):
            sg = jax.nn.sigmoid(p_ref[pl.ds(r0, rc), C_G])
            a = p_ref[pl.ds(r0, rc), C_A]
            dp_ref[pl.ds(r0, rc), C_A] = y * sg
            dp_ref[pl.ds(r0, rc), C_G] = y * a * sg * (1.0 - sg)

        _conv_apply(pad2_ref, wd_ref, KD, tc, tl, rc, emit_d2, flip=True)

    return pl.pallas_call(
        body, name=name, grid=(BRW // LANE,),
        in_specs=[_ctile(T), _col(T, 0), _col(T, 0),
                  pl.BlockSpec((KB, LANE), lambda j: (0, j)), pl.BlockSpec((KD, LANE), lambda j: (0, j))],
        out_specs=[_ctile(T), pl.BlockSpec((KB, LANE), lambda j: (0, j)), pl.BlockSpec((KD, LANE), lambda j: (0, j)),
                   pl.BlockSpec((1, LANE), lambda j: (0, j))],
        out_shape=[jax.ShapeDtypeStruct((T, W_C), f32), jax.ShapeDtypeStruct((KB, BRW), f32),
                   jax.ShapeDtypeStruct((KD, BRW), f32), jax.ShapeDtypeStruct((1, BRW), f32)],
        scratch_shapes=[pltpu.VMEM((_conv_rows(tc, tl), LANE), f32), pltpu.VMEM((_conv_rows(tc, tl), LANE), f32),
                        pltpu.VMEM((T, LANE), f32)],
        compiler_params=_cparams(("parallel",)),
    )(P, dyb, dhh, wb, wd)


def _gla_chunk(q, k, v, r, w2, b2, st, isfwd):
    z = mm(r, w2) + b2
    g = jax.nn.log_sigmoid(jnp.where(isfwd, z[:, 0:C_KW], z[:, C_KW:2 * C_KW])) / C_TAU
    ri = lax.broadcasted_iota(jnp.int32, (CH, CH), 0)
    ci = lax.broadcasted_iota(jnp.int32, (CH, CH), 1)
    keep = jnp.where(isfwd, ri - ci, ci - ri) >= 0
    tri = keep.astype(f32)
    cum = jnp.dot(tri, g, preferred_element_type=f32, precision=lax.Precision.HIGHEST)
    last = jnp.sum(g, axis=0, keepdims=True)
    q = q * (C_KW // C_HEADS) ** -0.5
    hv = lax.broadcasted_iota(jnp.int32, (BRW, C_KW), 0) // (BRW // C_HEADS)
    hk = lax.broadcasted_iota(jnp.int32, (BRW, C_KW), 1) // (C_KW // C_HEADS)
    st_new = st * jnp.exp(last) + jnp.where(hv == hk, mm_tn(v, k * jnp.exp(last - cum)), 0.0)
    o = mm_nt(q * jnp.exp(cum), st)
    rowi = lax.broadcasted_iota(jnp.int32, (CH, C_KW), 0)
    lane_head = lax.broadcasted_iota(jnp.int32, (CH, C_KW), 1) // (C_KW // C_HEADS)
    scores = [jnp.zeros((CH, CH), f32) for _ in range(C_HEADS)]
    for a in range(CH // GLA_SUB):
        idx = jnp.where(isfwd, GLA_SUB * a - 1, GLA_SUB * (a + 1))
        ref = jnp.sum(jnp.where(rowi == idx, cum, 0.0), axis=0, keepdims=True)
        qa = q * jnp.exp(jnp.minimum(cum - ref, 0.0))
        ka = k * jnp.exp(jnp.minimum(ref - cum, GLA_CLAMP))
        in_block = (ri // GLA_SUB == a) & keep
        for hd in range(C_HEADS):
            s = mm_nt(jnp.where(lane_head == hd, qa, 0.0), ka)
            scores[hd] = scores[hd] + jnp.where(in_block, s, 0.0)
    vw = BRW // C_HEADS
    o = o + jnp.concatenate([mm(scores[hd], v[:, vw * hd:vw * (hd + 1)]) for hd in range(C_HEADS)], axis=1)
    return o, st_new


def _gla_chunk_of(d, n, nc, nch):
    back = jnp.where(n < nc, nc - 1 - n, nch - 1 - (n - nc))
    return jnp.where(d == 0, n, back)


def _gla_fwd(P, w2, b2, tc, name):
    T = P.shape[0]
    nch, nc = T // CH, tc // CH

    def body(p_ref, w_ref, b_ref, o_ref, ss_ref, st_ref):
        @pl.when(pl.program_id(1) == 0)
        def _():
            st_ref[...] = jnp.zeros_like(st_ref)

        st = st_ref[...]
        ss_ref[...] = st
        o, st_new = _gla_chunk(p_ref[:, 0:G_K], p_ref[:, G_K:G_V], p_ref[:, G_V:G_R], p_ref[:, G_R:W_G], w_ref[...],
                               b_ref[...], st, pl.program_id(0) == 0)
        o_ref[...] = o
        st_ref[...] = st_new

    return pl.pallas_call(
        body, name=name, grid=(2, nch),
        in_specs=[pl.BlockSpec((CH, W_G), lambda d, n: (_gla_chunk_of(d, n, nc, nch), 0)),
                  pl.BlockSpec((LANE, 512), lambda d, n: (0, 0)), pl.BlockSpec((1, 512), lambda d, n: (0, 0))],
        out_specs=[pl.BlockSpec((None, CH, BRW), lambda d, n: (d, _gla_chunk_of(d, n, nc, nch), 0)),
                   pl.BlockSpec((None, None, BRW, C_KW), lambda d, n: (d, n, 0, 0))],
        out_shape=[jax.ShapeDtypeStruct((2, T, BRW), f32), jax.ShapeDtypeStruct((2, nch, BRW, C_KW), f32)],
        scratch_shapes=[pltpu.VMEM((BRW, C_KW), f32)],
        compiler_params=_cparams(("parallel", "arbitrary")),
    )(P, w2, b2)


def _gla_bwd(P, w2, b2, ssave, doc, tc, name):
    T = P.shape[0]
    nch, nc = T // CH, tc // CH

    def chunk(d, m):
        return _gla_chunk_of(d, nch - 1 - m, nc, nch)

    def body(p_ref, w_ref, b_ref, ss_ref, g_ref, dp_ref, dw_ref, db_ref, dst_ref):
        m = pl.program_id(1)
        isfwd = pl.program_id(0) == 0

        @pl.when(m == 0)
        def _():
            dst_ref[...] = jnp.zeros_like(dst_ref)

        _, vjp = jax.vjp(lambda q, k, v, r, w, b, st: _gla_chunk(q, k, v, r, w, b, st, isfwd),
                         p_ref[:, 0:G_K], p_ref[:, G_K:G_V], p_ref[:, G_V:G_R], p_ref[:, G_R:W_G], w_ref[...], b_ref[...],
                         ss_ref[...])
        dq, dk, dv, dr, dw, db, dst = vjp((g_ref[...], dst_ref[...]))
        dp_ref[:, 0:G_K] = dq
        dp_ref[:, G_K:G_V] = dk
        dp_ref[:, G_V:G_R] = dv
        dp_ref[:, G_R:W_G] = dr
        dst_ref[...] = dst

        @pl.when(m == 0)
        def _():
            dw_ref[...] = dw
            _partial_rows(db_ref, [db])

        @pl.when(m > 0)
        def _():
            dw_ref[...] += dw
            db_ref[0:1, :] += db

    return pl.pallas_call(
        body, name=name, grid=(2, nch),
        in_specs=[pl.BlockSpec((CH, W_G), lambda d, m: (chunk(d, m), 0)),
                  pl.BlockSpec((LANE, 512), lambda d, m: (0, 0)), pl.BlockSpec((1, 512), lambda d, m: (0, 0)),
                  pl.BlockSpec((None, None, BRW, C_KW), lambda d, m: (d, nch - 1 - m, 0, 0)),
                  pl.BlockSpec((CH, BRW), lambda d, m: (chunk(d, m), 0))],
        out_specs=[pl.BlockSpec((None, CH, W_G), lambda d, m: (d, chunk(d, m), 0)),
                   pl.BlockSpec((None, LANE, 512), lambda d, m: (d, 0, 0)),
                   pl.BlockSpec((None, SUB, 512), lambda d, m: (d, 0, 0))],
        out_shape=[jax.ShapeDtypeStruct((2, T, W_G), f32),
                   jax.ShapeDtypeStruct((2, LANE, 512), f32), jax.ShapeDtypeStruct((2, SUB, 512), f32)],
        scratch_shapes=[pltpu.VMEM((BRW, C_KW), f32)],
        compiler_params=_cparams(("parallel", "arbitrary")),
    )(P, w2, b2, ssave, doc)


def _sum_dirs(a, tm, name):
    _, T, W = a.shape

    def body(a_ref, o_ref):
        o_ref[...] = a_ref[0] + a_ref[1]

    return pl.pallas_call(
        body, name=name, grid=(T // tm,),
        in_specs=[pl.BlockSpec((2, tm, W), lambda i: (0, i, 0))], out_specs=pl.BlockSpec((tm, W), lambda i: (i, 0)),
        out_shape=jax.ShapeDtypeStruct((T, W), f32),
        compiler_params=_cparams(("parallel",)),
    )(a)


def _merge_fn(h, m_l, m_c, isctx, ya, ga, yb, gb, of, ob, gc, hh, gd, mg, es, ey, cn, dng, dnb, lg, lb, wbr, wout):
    oc = of + ob
    yc = jnp.concatenate([_rms(oc[:, HD * i:HD * (i + 1)], cn[:, HD * i:HD * (i + 1)]) for i in range(C_HEADS)], 1)
    brs = [ya * _silu(ga), yb * _silu(gb), yc * _silu(gc), _silu(_ln(hh) * dng + dnb) * _silu(gd)]
    acc = None
    for i in range(4):
        t = jax.nn.sigmoid(mg[:, D * i:D * (i + 1)]) * (mm(brs[i], wbr[i]) + es[i])
        acc = t if acc is None else acc + t
    y = mm(acc, wout) + ey
    gate = jnp.where(isctx, m_c[:, 2 * D:3 * D], m_l[:, 2 * D:3 * D])
    hn = _ln(ALPHA * h + gate * y) * lg + lb
    return hn, (brs, acc)


def _merge_specs(tm):
    t = lambda w, off=0: _tok(tm, w, off)
    return [t(D), pl.BlockSpec((SUB, 3 * D), lambda i: (0, 0)),
            t(BRW), t(BRW, M_GA), t(BRW), t(BRW, M_GB),
            pl.BlockSpec((None, tm, BRW), lambda i: (0, i, 0)), pl.BlockSpec((None, tm, BRW), lambda i: (1, i, 0)),
            t(BRW, M_GC), t(BRW), t(BRW, M_GD), t(4 * D, 0),
            _vec(BRW), _vec(BRW), _vec(BRW), _vec(D), _vec(D),
            pl.BlockSpec((4, BRW, D), lambda i: (0, 0, 0)), pl.BlockSpec((D, D), lambda i: (0, 0))]


def _merge_fwd(h, modv_l, ya, yb, o2, hh, P, cn, dng, dnb, lg, lb, wbr, wout, tc, tm, name):
    T = h.shape[0]

    def body(h_ref, m_ref, ya_ref, ga_ref, yb_ref, gb_ref, of_ref, ob_ref, gc_ref, hh_ref, gd_ref, mg_ref,
             cn_ref, dng_ref, dnb_ref, lg_ref, lb_ref, wbr_ref, wout_ref, o_ref):
        isctx = _row_ids(pl.program_id(0), tm) < tc
        zero = jnp.zeros((tm, D), f32)
        hn, _ = _merge_fn(h_ref[...], m_ref[0:1, :], m_ref[1:2, :], isctx, ya_ref[...], ga_ref[...], yb_ref[...],
                          gb_ref[...], of_ref[...], ob_ref[...], gc_ref[...], hh_ref[...], gd_ref[...], mg_ref[...],
                          [zero] * 4, zero, cn_ref[...], dng_ref[...], dnb_ref[...], lg_ref[...], lb_ref[...],
                          [wbr_ref[i] for i in range(4)], wout_ref[...])
        o_ref[...] = hn

    return pl.pallas_call(
        body, name=name, grid=(T // tm,),
        in_specs=_merge_specs(tm), out_specs=_tok(tm, D, 0),
        out_shape=jax.ShapeDtypeStruct((T, D), f32),
        compiler_params=_cparams(("parallel",)),
    )(h, modv_l, ya, P, yb, P, o2, o2, P, hh, P, P, cn, dng, dnb, lg, lb, wbr, wout)


def _merge_bwd(dhn, h, modv_l, ya, yb, o2, hh, P, cn, dng, dnb, lg, lb, wbr, wout, tc, tm, name):
    T = h.shape[0]
    nt = T // tm

    def body(g_ref, h_ref, m_ref, ya_ref, ga_ref, yb_ref, gb_ref, of_ref, ob_ref, gc_ref, hh_ref, gd_ref, mg_ref,
             cn_ref, dng_ref, dnb_ref, lg_ref, lb_ref, wbr_ref, wout_ref,
             dh_ref, dm_ref, dya_ref, dyb_ref, doc_ref, dhh_ref, dp_ref,
             br_ref, z_ref, acc_ref, dy_ref, dv5_ref, dvd_ref):
        isctx = _row_ids(pl.program_id(0), tm) < tc
        zero = jnp.zeros((tm, D), f32)
        wbr_v = [wbr_ref[i] for i in range(4)]
        wout_v = wout_ref[...]

        def fn(h, ml, mc, ya, ga, yb, gb, oc, gc, hh, gd, mg, e0, e1, e2, e3, ey, cn, dng, dnb, lg, lb):
            return _merge_fn(h, ml, mc, isctx, ya, ga, yb, gb, oc, jnp.zeros_like(oc), gc, hh, gd, mg,
                             [e0, e1, e2, e3], ey, cn, dng, dnb, lg, lb, wbr_v, wout_v)

        _, vjp, (brs, acc) = jax.vjp(
            fn, h_ref[...], m_ref[0:1, :], m_ref[1:2, :], ya_ref[...], ga_ref[...], yb_ref[...], gb_ref[...],
            of_ref[...] + ob_ref[...], gc_ref[...], hh_ref[...], gd_ref[...], mg_ref[...], zero, zero, zero, zero, zero,
            cn_ref[...], dng_ref[...], dnb_ref[...], lg_ref[...], lb_ref[...], has_aux=True)
        (dh, dml, dmc, dya, dga, dyb, dgb, doc, dgc, dhh, dgd, dmg, z0, z1, z2, z3, dy,
         dcn, ddng, ddnb, dlg, dlb) = vjp(g_ref[...])
        dh_ref[...] = dh
        _partial_rows(dm_ref, [dml, dmc])
        dya_ref[...] = dya
        dyb_ref[...] = dyb
        doc_ref[...] = doc
        dhh_ref[...] = dhh
        dp_ref[:, 0:M_GA] = dmg
        dp_ref[:, M_GA:M_GB] = dga
        dp_ref[:, M_GB:M_GC] = dgb
        dp_ref[:, M_GC:M_GD] = dgc
        dp_ref[:, M_GD:W_M] = dgd
        for i, z in enumerate((z0, z1, z2, z3)):
            br_ref[i] = brs[i].astype(bf16)
            z_ref[i] = z.astype(bf16)
        acc_ref[...] = acc.astype(bf16)
        dy_ref[...] = dy.astype(bf16)
        _partial_rows(dv5_ref, [dcn, ddng, ddnb])
        _partial_rows(dvd_ref, [dlg, dlb])

    t = lambda w: _tok(tm, w, 0)
    part = lambda w: pl.BlockSpec((None, SUB, w), lambda i: (i, 0, 0))
    sd = jax.ShapeDtypeStruct
    return pl.pallas_call(
        body, name=name, grid=(nt,),
        in_specs=[t(D)] + _merge_specs(tm),
        out_specs=[t(D), part(3 * D)] + [t(BRW)] * 4 + [t(W_M),
                   pl.BlockSpec((4, tm, BRW), lambda i: (0, i, 0)), pl.BlockSpec((4, tm, D), lambda i: (0, i, 0)),
                   t(D), t(D), part(BRW), part(D)],
        out_shape=[sd((T, D), f32), sd((nt, SUB, 3 * D), f32)] + [sd((T, BRW), f32)] * 4 + [sd((T, W_M), f32),
                   sd((4, T, BRW), bf16), sd((4, T, D), bf16), sd((T, D), bf16), sd((T, D), bf16),
                   sd((nt, SUB, BRW), f32), sd((nt, SUB, D), f32)],
        compiler_params=_cparams(("parallel",)),
    )(dhn, h, modv_l, ya, P, yb, P, o2, o2, P, hh, P, P, cn, dng, dnb, lg, lb, wbr, wout)


def _loss_kernel(h, tgt, tc, tm, name):
    T = h.shape[0]
    nt = T // tm
    nct = tc // tm

    def body(h_ref, t_ref, d_ref, l_ref):
        i = pl.program_id(0)
        err = h_ref[...] - t_ref[...]
        lat = (i >= nct).astype(f32)
        d_ref[...] = err * (lat / D)
        l_ref[...] = jnp.zeros((SUB, LANE), f32) + lat * 0.5 * jnp.sum(err * err) / D

    return pl.pallas_call(
        body, name=name, grid=(nt,),
        in_specs=[pl.BlockSpec((tm, D), lambda i: (i, 0)),
                  pl.BlockSpec((tm, D), lambda i: (jnp.maximum(i - nct, 0), 0))],
        out_specs=[pl.BlockSpec((tm, D), lambda i: (i, 0)), pl.BlockSpec((None, SUB, LANE), lambda i: (i, 0, 0))],
        out_shape=[jax.ShapeDtypeStruct((T, D), f32), jax.ShapeDtypeStruct((nt, SUB, LANE), f32)],
        compiler_params=_cparams(("parallel",)),
    )(h, tgt)


def _rope_tables(tc, tl):
    t = jnp.arange(tl)
    inv = ROPE_THETA ** (-jnp.arange(0, HD // 2, 2, dtype=f32) / (HD // 2))
    ang = jnp.concatenate([(t // GRID_W).astype(f32)[:, None] * inv, (t % GRID_W).astype(f32)[:, None] * inv], -1)
    cos, sin = jnp.repeat(jnp.cos(ang), 2, axis=1), jnp.repeat(jnp.sin(ang), 2, axis=1)
    even = (jnp.arange(HD) % 2 == 0)[None, :]
    cos_f = jnp.concatenate([jnp.ones((tc, HD), f32), cos], 0)
    sin_a = jnp.concatenate([jnp.zeros((tc, HD), f32), jnp.where(even, -sin, 0.0)], 0)
    sin_b = jnp.concatenate([jnp.zeros((tc, HD), f32), jnp.where(even, 0.0, sin)], 0)
    return cos_f, sin_a, sin_b


def _pack_groups(w):
    s = lambda a, n: w[..., a:a + n]
    conv = jnp.stack([s(S_B, BRW), s(S_C, BRW), s(S_X, BRW), s(S_DA, BRW), s(S_DG, BRW)], -2)
    conv = jnp.swapaxes(conv.reshape(conv.shape[:-1] + (BRW // LANE, LANE)), -2, -3).reshape(w.shape[:-1] + (W_C,))
    pad = jnp.zeros(w.shape[:-1] + (LANE - 2 * C_RANK,), w.dtype)
    return dict(M=jnp.concatenate([s(S_MG, 4 * D), s(S_GA, BRW), s(S_GB, BRW), s(S_GC, BRW), s(S_GD, BRW)], -1),
                A=s(S_Q, W_A), C=conv, G=jnp.concatenate([s(S_CQ, 2 * C_KW), s(S_CV, BRW), s(S_R, 2 * C_RANK), pad], -1))


def _unpack_groups(g):
    M, A, C, G = g["M"], g["A"], g["C"], g["G"]
    conv = jnp.swapaxes(C.reshape(C.shape[:-1] + (BRW // LANE, 5, LANE)), -2, -3).reshape(C.shape[:-1] + (5, BRW))
    return jnp.concatenate([A, M[..., M_GA:M_GB], conv[..., 0, :], conv[..., 1, :], conv[..., 2, :], M[..., M_GB:M_GC],
                            G[..., 0:G_R], M[..., M_GC:M_GD], G[..., G_R:G_R + 2 * C_RANK], conv[..., 3, :],
                            conv[..., 4, :], M[..., M_GD:W_M], M[..., 0:M_GA]], -1)


PROJ_TN = dict(M=2048, A=1024, C=1280, G=1152)
DWP_TN = dict(M=768, A=1024, C=640, G=1152)


def _gate_weights(w2_l, gb_l):
    w = jnp.zeros((LANE, 2 * C_KW), f32)
    w = w.at[0:C_RANK, 0:C_KW].set(w2_l[0]).at[C_RANK:2 * C_RANK, C_KW:2 * C_KW].set(w2_l[1])
    return w, jnp.concatenate([gb_l[0], gb_l[1]])[None, :]


def _local_step(x1, c1, ctx1, tgt1, c_ctx, w_mod, b_mod, wp, q_norm, k_norm, b_conv, w2, gb, c_norm, d_conv_w,
                d_conv_b, d_norm_g, d_norm_b, w_br, w_out, ln_g, ln_b, tm):
    tc, tl = ctx1.shape[0], x1.shape[0]
    T = tc + tl
    rc = min(256, tc)
    tmb = tm // 2
    tmm = 768 if T % 768 == 0 else tm
    rope = _rope_tables(tc, tl)
    cin = jnp.concatenate([c1, c_ctx[None, :], jnp.zeros((SUB - 2, D), f32)], 0)
    modv = _mod_fwd(cin, w_mod, b_mod)
    row = lambda v: v[None, :]

    h = jnp.concatenate([ctx1, x1], 0)
    saved = []
    for l in range(DEPTH):
        u = _ln_fwd(h, modv[l], tc, tm, f"ln_fwd{l}")
        P = {k: _matmul(u, wp[l][k], "nn", tmm, PROJ_TN[k], D, f"proj{l}{k}") for k in GROUPS}
        qn, kn, vb = _prep_fwd(P["A"], row(q_norm[l]), row(k_norm[l]), rope, tm, f"prep_fwd{l}")
        ya = _attn_fwd(qn, kn, vb, tc, tm, f"attn_fwd{l}")
        yb, hh = _conv_fwd(P["C"], b_conv[l], d_conv_w[l], row(d_conv_b[l]), tc, tl, rc, f"conv_fwd{l}")
        w2p, b2p = _gate_weights(w2[l], gb[l])
        o2, ssave = _gla_fwd(P["G"], w2p, b2p, tc, f"gla_fwd{l}")
        hn = _merge_fwd(h, modv[l], ya, yb, o2, hh, P["M"], row(c_norm[l]), row(d_norm_g[l]), row(d_norm_b[l]),
                        row(ln_g[l]), row(ln_b[l]), w_br[l], w_out[l], tc, tm, f"merge_fwd{l}")
        saved.append((h, u, P, qn, kn, vb, ya, yb, hh, o2, ssave, w2p, b2p))
        h = hn

    dh, lparts = _loss_kernel(h, tgt1, tc, tm, "loss")
    loss = jnp.sum(lparts[:, 0, 0])

    g = {k: [None] * DEPTH for k in ("wp", "q_norm", "k_norm", "b_conv", "w2", "gb", "c_norm", "d_conv_w", "d_conv_b",
                                     "d_norm_g", "d_norm_b", "w_br", "w_out", "ln_g", "ln_b", "modv")}
    for l in reversed(range(DEPTH)):
        h_in, u, P, qn, kn, vb, ya, yb, hh, o2, ssave, w2p, b2p = saved[l]
        dP = {}
        (dh_res, dm_mg, dya, dyb, doc, dhh, dP["M"], br, z, acc, dy, dv5, dvd) = _merge_bwd(
            dh, h_in, modv[l], ya, yb, o2, hh, P["M"], row(c_norm[l]), row(d_norm_g[l]), row(d_norm_b[l]),
            row(ln_g[l]), row(ln_b[l]), w_br[l], w_out[l], tc, tmb, f"merge_bwd{l}")
        g["w_br"][l] = _matmul_tn_batched(br, z, f"dwbr{l}")
        g["w_out"][l] = _matmul(acc, dy, "tn", D, D, T, f"dwout{l}")
        v5 = jnp.sum(dv5, 0)
        g["c_norm"][l], g["d_norm_g"][l], g["d_norm_b"][l] = v5[0], v5[1], v5[2]
        vd = jnp.sum(dvd, 0)
        g["ln_g"][l], g["ln_b"][l] = vd[0], vd[1]
        dqn, dkn, dv = _attn_bwd(qn, kn, vb, dya, tc, tm, f"attn_bwd{l}")
        dP["A"], dqk = _prep_bwd(P["A"], dqn, dkn, dv, row(q_norm[l]), row(k_norm[l]), rope, tm, f"prep_bwd{l}")
        dqk = jnp.sum(dqk, 0)
        g["q_norm"][l], g["k_norm"][l] = dqk[0], dqk[1]
        dP["C"], dwb, dwd, dbd = _conv_bwd(P["C"], dyb, dhh, b_conv[l], d_conv_w[l], tc, tl, rc, f"conv_bwd{l}")
        g["b_conv"][l], g["d_conv_w"][l], g["d_conv_b"][l] = dwb, dwd, dbd[0]
        dpg2, dw2p, db2p = _gla_bwd(P["G"], w2p, b2p, ssave, doc, tc, f"gla_bwd{l}")
        dP["G"] = _sum_dirs(dpg2, tm, f"gla_sum{l}")
        dw2p = dw2p[0] + dw2p[1]
        db2p = db2p[0, 0] + db2p[1, 0]
        g["w2"][l] = jnp.stack([dw2p[0:C_RANK, 0:C_KW], dw2p[C_RANK:2 * C_RANK, C_KW:2 * C_KW]])
        g["gb"][l] = jnp.stack([db2p[0:C_KW], db2p[C_KW:2 * C_KW]])
        du = None
        for k in GROUPS:
            du = _matmul(dP[k], wp[l][k], "nt", tmm, D, PROJ_TN[k], f"du{l}{k}", add=du)
        g["wp"][l] = {k: _matmul(u, dP[k], "tn", D, DWP_TN[k], T, f"dwp{l}{k}") for k in GROUPS}
        dh, dm_ln = _ln_bwd(du, h_in, dh_res, modv[l], tc, tm, f"ln_bwd{l}")
        g["modv"][l] = jnp.sum(dm_mg, 0) + jnp.sum(dm_ln, 0)

    dmodv = jnp.stack(g.pop("modv"))
    g["w_mod"], dcin = _mod_bwd(cin, w_mod, dmodv)
    g["b_mod"] = dmodv[:, 0, :] + dmodv[:, 1, :]
    g["c_ctx"] = jnp.sum(dcin, (0, 1))[1]
    return loss, dh[tc:], g


def _adamw(w, g, m, v, name, tr=128):
    R, C = w.shape
    if R % tr:
        tr = R

    def body(w_ref, g_ref, m_ref, v_ref, d_ref, nm_ref, nv_ref):
        gg = g_ref[...]
        nm = B1 * m_ref[...] + (1.0 - B1) * gg
        nv = B2 * v_ref[...] + (1.0 - B2) * (gg * gg)
        m_hat = nm / (1.0 - B1 ** STEP)
        v_hat = nv / (1.0 - B2 ** STEP)
        d_ref[...] = -LR * (m_hat / (jnp.sqrt(v_hat) + AEPS) + WD * w_ref[...])
        nm_ref[...] = nm
        nv_ref[...] = nv

    spec = pl.BlockSpec((tr, C), lambda i: (i, 0))
    return pl.pallas_call(
        body, name=name, grid=(R // tr,), in_specs=[spec] * 4, out_specs=[spec] * 3,
        out_shape=[jax.ShapeDtypeStruct((R, C), f32)] * 3,
        compiler_params=_cparams(("parallel",)),
    )(w, g, m, v)


MESH = pl.DeviceIdType.MESH
ANY = pl.BlockSpec(memory_space=pl.ANY)
N_CHIPS = 4


def _place():
    x, y, c = lax.axis_index("x"), lax.axis_index("y"), lax.axis_index("c")
    chips = [(1 - x, y), (x, 1 - y), (1 - x, 1 - y)]
    return x, y, c, chips


def _rows(c, hr):
    return pl.ds(pl.multiple_of(c * hr, SUB), hr)


def _all_gather(arrs, name):
    n = len(arrs)

    def body(*refs):
        ins, outs = refs[:n], refs[n:2 * n]
        send, recv, loc = refs[2 * n:]
        x, y, c, chips = _place()
        me, sib = 2 * x + y, (x, y, 1 - c)

        def copy(a, k, chip_idx, cc, to, src=None):
            hr = ins[a].shape[0] // 2
            blk = outs[a].at[chip_idx, _rows(cc, hr), :]
            return pltpu.make_async_remote_copy(src_ref=blk if src is None else src, dst_ref=blk,
                                                send_sem=send.at[6 * a + k], recv_sem=recv.at[6 * a + k],
                                                device_id=to, device_id_type=MESH)

        local = [pltpu.make_async_copy(ins[a], outs[a].at[me], loc.at[a]) for a in range(n)]
        for cp in local:
            cp.start()
        first = [copy(a, j, me, c, (*chip, c), src=ins[a].at[_rows(c, ins[a].shape[0] // 2), :])
                 for a in range(n) for j, chip in enumerate(chips)]
        for cp in first:
            cp.start()
        passed = []
        for a in range(n):
            for j, chip in enumerate(chips):
                k = 2 * chip[0] + chip[1]
                copy(a, j, k, c, sib).wait_recv()
                fwd = copy(a, 3 + j, k, c, sib)
                fwd.start()
                passed.append(fwd)
        for a in range(n):
            for j, chip in enumerate(chips):
                copy(a, 3 + j, 2 * chip[0] + chip[1], 1 - c, sib).wait_recv()
        for cp in first + passed:
            cp.wait_send()
        for cp in local:
            cp.wait()

    return pl.pallas_call(
        body, name=name, in_specs=[ANY] * n, out_specs=[ANY] * n,
        out_shape=[jax.ShapeDtypeStruct((N_CHIPS,) + a.shape, a.dtype) for a in arrs],
        scratch_shapes=[pltpu.SemaphoreType.DMA((6 * n,)), pltpu.SemaphoreType.DMA((6 * n,)), pltpu.SemaphoreType.DMA((n,))],
    )(*arrs)


def _sibling_halves(arrs, name):
    n = len(arrs)

    def body(*refs):
        ins, outs = refs[:n], refs[n:2 * n]
        send, recv = refs[2 * n:]
        x, y, c, _ = _place()
        cps = []
        for a in range(n):
            hr = ins[a].shape[1] // 2
            cps.append(pltpu.make_async_remote_copy(src_ref=ins[a].at[:, _rows(1 - c, hr), :], dst_ref=outs[a],
                                                    send_sem=send.at[a], recv_sem=recv.at[a],
                                                    device_id=(x, y, 1 - c), device_id_type=MESH))
        for cp in cps:
            cp.start()
        for cp in cps:
            cp.wait()

    return pl.pallas_call(
        body, name=name, in_specs=[ANY] * n, out_specs=[ANY] * n,
        out_shape=[jax.ShapeDtypeStruct((a.shape[0], a.shape[1] // 2, a.shape[2]), a.dtype) for a in arrs],
        scratch_shapes=[pltpu.SemaphoreType.DMA((n,)), pltpu.SemaphoreType.DMA((n,))],
    )(*arrs)


def _add_half(gfull, land, cidx, name, tr=128, out_dtype=bf16):
    _, R, C = gfull.shape
    hr = R // 2
    tr = min(tr, hr)
    nb = hr // tr

    def body(c_ref, g_ref, l_ref, o_ref):
        o_ref[...] = (g_ref[...] + l_ref[...]).astype(o_ref.dtype)

    return pl.pallas_call(
        body, name=name,
        grid_spec=pltpu.PrefetchScalarGridSpec(
            num_scalar_prefetch=1, grid=(N_CHIPS, nb),
            in_specs=[pl.BlockSpec((None, tr, C), lambda s, i, cr: (s, cr[0] * nb + i, 0)),
                      pl.BlockSpec((None, tr, C), lambda s, i, cr: (s, i, 0))],
            out_specs=pl.BlockSpec((None, tr, C), lambda s, i, cr: (s, i, 0))),
        out_shape=jax.ShapeDtypeStruct((N_CHIPS, hr, C), out_dtype),
        compiler_params=_cparams(("parallel", "parallel")),
    )(cidx, gfull, land)


def _chip_exchange(arrs, name):
    n = len(arrs)

    def body(*refs):
        ins, outs = refs[:n], refs[n:2 * n]
        send, recv, loc = refs[2 * n:]
        x, y, c, chips = _place()
        me = 2 * x + y
        local = [pltpu.make_async_copy(ins[a].at[me], outs[a].at[me], loc.at[a]) for a in range(n)]
        for cp in local:
            cp.start()
        cps = []
        for a in range(n):
            for j, chip in enumerate(chips):
                k = 2 * chip[0] + chip[1]
                cps.append((pltpu.make_async_remote_copy(
                    src_ref=ins[a].at[k], dst_ref=outs[a].at[me], send_sem=send.at[3 * a + j], recv_sem=recv.at[3 * a + j],
                    device_id=(*chip, c), device_id_type=MESH), a, j, k))
        for cp, *_ in cps:
            cp.start()
        for cp, a, j, k in cps:
            pltpu.make_async_remote_copy(src_ref=ins[a].at[k], dst_ref=outs[a].at[k], send_sem=send.at[3 * a + j],
                                         recv_sem=recv.at[3 * a + j], device_id=(x, y, c), device_id_type=MESH).wait_recv()
        for cp, *_ in cps:
            cp.wait_send()
        for cp in local:
            cp.wait()

    return pl.pallas_call(
        body, name=name, in_specs=[ANY] * n, out_specs=[ANY] * n,
        out_shape=[jax.ShapeDtypeStruct(a.shape, a.dtype) for a in arrs],
        scratch_shapes=[pltpu.SemaphoreType.DMA((3 * n,)), pltpu.SemaphoreType.DMA((3 * n,)), pltpu.SemaphoreType.DMA((n,))],
    )(*arrs)


def _sum_chips(land, name, tr=128):
    _, R, C = land.shape
    tr = min(tr, R)

    def body(l_ref, o_ref):
        o_ref[...] = ((l_ref[0].astype(f32) + l_ref[1].astype(f32)) + l_ref[2].astype(f32)) + l_ref[3].astype(f32)

    return pl.pallas_call(
        body, name=name, grid=(R // tr,),
        in_specs=[pl.BlockSpec((N_CHIPS, tr, C), lambda i: (0, i, 0))], out_specs=pl.BlockSpec((tr, C), lambda i: (i, 0)),
        out_shape=jax.ShapeDtypeStruct((R, C), f32),
        compiler_params=_cparams(("parallel",)),
    )(land)


def _sibling_concat(arrs, name):
    n = len(arrs)

    def body(*refs):
        ins, outs = refs[:n], refs[n:2 * n]
        send, recv, loc = refs[2 * n:]
        x, y, c, _ = _place()
        local, cps = [], []
        for a in range(n):
            hr = ins[a].shape[0]
            local.append(pltpu.make_async_copy(ins[a], outs[a].at[_rows(c, hr), :], loc.at[a]))
            cps.append(pltpu.make_async_remote_copy(src_ref=ins[a], dst_ref=outs[a].at[_rows(c, hr), :],
                                                    send_sem=send.at[a], recv_sem=recv.at[a],
                                                    device_id=(x, y, 1 - c), device_id_type=MESH))
        for cp in local + cps:
            cp.start()
        for a in range(n):
            hr = ins[a].shape[0]
            pltpu.make_async_remote_copy(src_ref=ins[a], dst_ref=outs[a].at[_rows(1 - c, hr), :], send_sem=send.at[a],
                                         recv_sem=recv.at[a], device_id=(x, y, 1 - c), device_id_type=MESH).wait_recv()
        for cp in cps:
            cp.wait_send()
        for cp in local:
            cp.wait()

    return pl.pallas_call(
        body, name=name, in_specs=[ANY] * n, out_specs=[ANY] * n,
        out_shape=[jax.ShapeDtypeStruct((2 * a.shape[0], a.shape[1]), a.dtype) for a in arrs],
        scratch_shapes=[pltpu.SemaphoreType.DMA((n,)), pltpu.SemaphoreType.DMA((n,)), pltpu.SemaphoreType.DMA((n,))],
    )(*arrs)


N_DEV = 8


def _all_reduce_small(v, name):
    R = v.shape[0]

    def body(v_ref, o_ref, land_ref, send, recv):
        x, y, c, _ = _place()
        me = 4 * x + 2 * y + c
        land_ref[me] = v_ref[...]
        cps = []
        for m in range(1, N_DEV):
            px, py, pc = [(1 - q) if (m >> s) & 1 else q for q, s in ((x, 2), (y, 1), (c, 0))]
            cps.append((pltpu.make_async_remote_copy(src_ref=v_ref, dst_ref=land_ref.at[me], send_sem=send.at[m - 1],
                                                     recv_sem=recv.at[m - 1], device_id=(px, py, pc), device_id_type=MESH),
                        4 * px + 2 * py + pc, m))
        for cp, *_ in cps:
            cp.start()
        for cp, peer, m in cps:
            pltpu.make_async_remote_copy(src_ref=v_ref, dst_ref=land_ref.at[peer], send_sem=send.at[m - 1],
                                         recv_sem=recv.at[m - 1], device_id=(x, y, c), device_id_type=MESH).wait_recv()
        for cp, *_ in cps:
            cp.wait_send()
        acc = land_ref[0]
        for k in range(1, N_DEV):
            acc = acc + land_ref[k]
        o_ref[...] = acc

    vm = pl.BlockSpec(memory_space=pltpu.VMEM)
    return pl.pallas_call(
        body, name=name, in_specs=[vm], out_specs=vm, out_shape=jax.ShapeDtypeStruct(v.shape, f32),
        scratch_shapes=[pltpu.VMEM((N_DEV, R, LANE), f32), pltpu.SemaphoreType.DMA((N_DEV - 1,)),
                        pltpu.SemaphoreType.DMA((N_DEV - 1,))],
        compiler_params=pltpu.CompilerParams(vmem_limit_bytes=VMEM_LIMIT),
    )(v)


def _pack_small(arrs, mult=2 * SUB):
    flat = jnp.concatenate([a.reshape(-1) for a in arrs])
    rows = -(-flat.shape[0] // (LANE * mult)) * mult
    return jnp.pad(flat, (0, rows * LANE - flat.shape[0])).reshape(rows, LANE)


def _unpack_small(vec, shapes):
    flat, out, o = vec.reshape(-1), [], 0
    for s in shapes:
        n = int(np.prod(s))
        out.append(flat[o:o + n].reshape(s))
        o += n
    return out


REPL_SMALL = ("c_ctx", "b_mod", "q_norm", "k_norm", "c_norm", "d_conv_b", "d_norm_g", "d_norm_b", "ln_g", "ln_b")
SHARD_SMALL = ("b_conv", "c_gate_w2", "c_gate_b", "d_conv_w")
BIG = ("w_mod", "w_in", "w_br", "w_out")
ORDER = ("c_ctx", "w_mod", "b_mod", "w_in", "q_norm", "k_norm", "b_conv", "c_gate_w2", "c_gate_b", "c_norm", "d_conv_w",
         "d_conv_b", "d_norm_g", "d_norm_b", "w_br", "w_out", "ln_g", "ln_b")


def _unshard_last(g4, shard_shape):
    g = g4.reshape((N_CHIPS,) + tuple(shard_shape))
    g = jnp.moveaxis(g, 0, -2)
    return g.reshape(tuple(shard_shape[:-1]) + (N_CHIPS * shard_shape[-1],))


def _pieces_last(full):
    w = full.shape[-1] // N_CHIPS
    g = full.reshape(full.shape[:-1] + (N_CHIPS, w))
    return jnp.moveaxis(g, -2, 0).reshape(N_CHIPS, -1, w)


def kernel(x, c, ctx, c_ctx, w_mod, b_mod, w_in, q_norm, k_norm, b_conv, c_gate_w2, c_gate_b, c_norm, d_conv_w, d_conv_b, d_norm_g, d_norm_b, w_br, w_out, ln_g, ln_b, loss_target, m_c_ctx, m_w_mod, m_b_mod, m_w_in, m_q_norm, m_k_norm, m_b_conv, m_c_gate_w2, m_c_gate_b, m_c_norm, m_d_conv_w, m_d_conv_b, m_d_norm_g, m_d_norm_b, m_w_br, m_w_out, m_ln_g, m_ln_b, v_c_ctx, v_w_mod, v_b_mod, v_w_in, v_q_norm, v_k_norm, v_b_conv, v_c_gate_w2, v_c_gate_b, v_c_norm, v_d_conv_w, v_d_conv_b, v_d_norm_g, v_d_norm_b, v_w_br, v_w_out, v_ln_g, v_ln_b):
    W = dict(c_ctx=c_ctx, w_mod=w_mod, b_mod=b_mod, w_in=w_in, q_norm=q_norm, k_norm=k_norm, b_conv=b_conv,
             c_gate_w2=c_gate_w2, c_gate_b=c_gate_b, c_norm=c_norm, d_conv_w=d_conv_w, d_conv_b=d_conv_b,
             d_norm_g=d_norm_g, d_norm_b=d_norm_b, w_br=w_br, w_out=w_out, ln_g=ln_g, ln_b=ln_b)
    M = dict(c_ctx=m_c_ctx, w_mod=m_w_mod, b_mod=m_b_mod, w_in=m_w_in, q_norm=m_q_norm, k_norm=m_k_norm, b_conv=m_b_conv,
             c_gate_w2=m_c_gate_w2, c_gate_b=m_c_gate_b, c_norm=m_c_norm, d_conv_w=m_d_conv_w, d_conv_b=m_d_conv_b,
             d_norm_g=m_d_norm_g, d_norm_b=m_d_norm_b, w_br=m_w_br, w_out=m_w_out, ln_g=m_ln_g, ln_b=m_ln_b)
    V = dict(c_ctx=v_c_ctx, w_mod=v_w_mod, b_mod=v_b_mod, w_in=v_w_in, q_norm=v_q_norm, k_norm=v_k_norm, b_conv=v_b_conv,
             c_gate_w2=v_c_gate_w2, c_gate_b=v_c_gate_b, c_norm=v_c_norm, d_conv_w=v_d_conv_w, d_conv_b=v_d_conv_b,
             d_norm_g=v_d_norm_g, d_norm_b=v_d_norm_b, w_br=v_w_br, w_out=v_w_out, ln_g=v_ln_g, ln_b=v_ln_b)
    chip = 2 * lax.axis_index("x") + lax.axis_index("y")
    cidx = lax.axis_index("c").astype(jnp.int32).reshape(1)

    big2d = {k: W[k].reshape(-1, W[k].shape[-1]) for k in BIG}
    small_shard = _pack_small([W[k] for k in SHARD_SMALL])
    gathered = _all_gather([big2d[k].astype(bf16) for k in BIG] + [small_shard], "all_gather")
    G = dict(zip(BIG, gathered[:4]))
    full = {k: _unshard_last(G[k], W[k].shape) for k in ("w_mod", "w_in", "w_br")}
    full["w_out"] = jnp.moveaxis(G["w_out"].reshape((N_CHIPS,) + w_out.shape), 0, 1).reshape(DEPTH, D, D)
    smalls = [_unpack_small(gathered[4][s], [W[k].shape for k in SHARD_SMALL]) for s in range(N_CHIPS)]
    for i, k in enumerate(SHARD_SMALL):
        full[k] = jnp.concatenate([smalls[s][i] for s in range(N_CHIPS)], axis=-1)

    wp = [_pack_groups(full["w_in"][l]) for l in range(DEPTH)]
    loss, gx, g = _local_step(
        x[0], c, ctx[0], loss_target[0], c_ctx, full["w_mod"], b_mod, wp, q_norm, k_norm, full["b_conv"],
        full["c_gate_w2"], full["c_gate_b"], c_norm, full["d_conv_w"], d_conv_b, d_norm_g, d_norm_b,
        [full["w_br"][l] for l in range(DEPTH)], [full["w_out"][l] for l in range(DEPTH)], ln_g, ln_b, tm=256)
    g["w_in"] = jnp.stack([_unpack_groups(gl) for gl in g.pop("wp")])
    g["c_gate_w2"], g["c_gate_b"] = g.pop("w2"), g.pop("gb")
    g = {k: (jnp.stack(v) if isinstance(v, list) else v) for k, v in g.items()}
    loss = lax.psum(loss, ("x", "y", "c"))

    pieces = [_pieces_last(g[k]) for k in ("w_mod", "w_in", "w_br")]
    pieces.append(jnp.moveaxis(g["w_out"].reshape(DEPTH, N_CHIPS, D // N_CHIPS, D), 1, 0).reshape(N_CHIPS, -1, D))
    land_a = _sibling_halves(pieces, "rs_sibling_halves")
    pair = [_add_half(p, la, cidx, f"rs_pair_sum{i}") for i, (p, la) in enumerate(zip(pieces, land_a))]
    land_b = _chip_exchange(pair, "rs_chip_exchange")
    half = [_sum_chips(lb, f"rs_chip_sum{i}") for i, lb in enumerate(land_b)]
    red = dict(zip(BIG, _sibling_concat(half, "rs_sibling_concat")))

    small_names = REPL_SMALL + SHARD_SMALL
    gs = _all_reduce_small(_pack_small([g[k] for k in small_names]), "all_reduce_small")
    gsm = dict(zip(small_names, _unpack_small(gs, [g[k].shape for k in small_names])))
    for k in SHARD_SMALL:
        wdt = W[k].shape[-1]
        gsm[k] = lax.dynamic_slice_in_dim(gsm[k], chip * wdt, wdt, axis=gsm[k].ndim - 1)

    grad, delta, new_m, new_v = {}, {}, {}, {}
    for k in BIG:
        grad[k] = red[k].reshape(W[k].shape)
        d_, m_, v_ = _adamw(big2d[k], red[k], M[k].reshape(red[k].shape), V[k].reshape(red[k].shape), f"adamw_{k}")
        delta[k], new_m[k], new_v[k] = d_.reshape(W[k].shape), m_.reshape(W[k].shape), v_.reshape(W[k].shape)
    shapes = [W[k].shape for k in small_names]
    d_, m_, v_ = _adamw(_pack_small([W[k] for k in small_names]), _pack_small([gsm[k] for k in small_names]),
                        _pack_small([M[k] for k in small_names]), _pack_small([V[k] for k in small_names]), "adamw_small")
    for k, dd, mm_, vv in zip(small_names, _unpack_small(d_, shapes), _unpack_small(m_, shapes), _unpack_small(v_, shapes)):
        grad[k], delta[k], new_m[k], new_v[k] = gsm[k], dd, mm_, vv

    return (loss, gx[None], *[grad[k] for k in ORDER], *[delta[k] for k in ORDER], *[new_m[k] for k in ORDER],
            *[new_v[k] for k in ORDER])
```

```python
import functools

import jax
import jax.numpy as jnp
import numpy as np
from jax import lax
from jax.experimental import pallas as pl
from jax.experimental.pallas import tpu as pltpu

f32 = jnp.float32
bf16 = jnp.bfloat16

D = 1024
DEPTH = 2
GRID_W = 64
BRW = 512
HD = 128
A_HEADS = 4
C_HEADS = 4
C_KW = 256
C_RANK = 16
C_TAU = 16.0
CH = 64
KB = 3
KD = 31
ALPHA = (2 * DEPTH) ** 0.25
EPS = 1e-6
ROPE_THETA = 10000.0
N_IN = 10784
LR, B1, B2, AEPS, WD, STEP = 0.001, 0.9, 0.999, 1e-08, 0.01, 10

W_M, W_A, W_C, W_G = 4 * D + 4 * BRW, 1024, 5 * BRW, 1152
GROUPS = ("M", "A", "C", "G")
GROUP_W = dict(M=W_M, A=W_A, C=W_C, G=W_G)
M_GA, M_GB, M_GC, M_GD = 4 * D, 4 * D + BRW, 4 * D + 2 * BRW, 4 * D + 3 * BRW
A_K, A_V = 512, 768
G_K, G_V, G_R = 256, 512, 1024
CT = 5 * 128
S_Q, S_GA, S_B, S_C, S_X, S_GB, S_CQ, S_CV, S_GC, S_R, S_DA, S_DG, S_GD, S_MG = (
    0, 1024, 1536, 2048, 2560, 3072, 3584, 4096, 4608, 5120, 5152, 5664, 6176, 6688)

LANE = 128
SUB = 8
VMEM_LIMIT = 56 * 1024 * 1024
CONV_PAD = 16
GLA_SUB = 16
GLA_CLAMP = 60.0


def _cparams(sem, vmem=VMEM_LIMIT):
    return pltpu.CompilerParams(dimension_semantics=sem, vmem_limit_bytes=vmem)


def _dg(a, b, ca, cb):
    return lax.dot_general(a.astype(bf16), b.astype(bf16), (((ca,), (cb,)), ((), ())),
                           preferred_element_type=f32)


@jax.custom_vjp
def mm(a, b):
    return _dg(a, b, 1, 0)


mm.defvjp(lambda a, b: (_dg(a, b, 1, 0), (a, b)),
          lambda r, ct: (_dg(ct, r[1], 1, 1).astype(r[0].dtype), _dg(r[0], ct, 0, 0).astype(r[1].dtype)))


@jax.custom_vjp
def mm_nt(a, b):
    return _dg(a, b, 1, 1)


mm_nt.defvjp(lambda a, b: (_dg(a, b, 1, 1), (a, b)),
             lambda r, ct: (_dg(ct, r[1], 1, 0).astype(r[0].dtype), _dg(ct, r[0], 0, 0).astype(r[1].dtype)))


@jax.custom_vjp
def mm_tn(a, b):
    return _dg(a, b, 0, 0)


mm_tn.defvjp(lambda a, b: (_dg(a, b, 0, 0), (a, b)),
             lambda r, ct: (_dg(r[1], ct, 1, 1).astype(r[0].dtype), _dg(r[0], ct, 1, 0).astype(r[1].dtype)))


def _silu(x):
    return x * jax.nn.sigmoid(x)


def _ln(x):
    mu = jnp.mean(x, -1, keepdims=True)
    xc = x - mu
    var = jnp.mean(xc * xc, -1, keepdims=True)
    return xc * lax.rsqrt(var + EPS)


def _rms(x, g):
    return x * lax.rsqrt(jnp.mean(x * x, -1, keepdims=True) + EPS) * g


@jax.custom_vjp
def _rope(x, cos_f, sin_a, sin_b):
    return x * cos_f + pltpu.roll(x, HD - 1, 1) * sin_a + pltpu.roll(x, 1, 1) * sin_b


def _rope_fwd(x, cos_f, sin_a, sin_b):
    return _rope(x, cos_f, sin_a, sin_b), (cos_f, sin_a, sin_b)


def _rope_bwd(r, ct):
    cos_f, sin_a, sin_b = r
    dx = ct * cos_f + pltpu.roll(ct * sin_a, 1, 1) + pltpu.roll(ct * sin_b, HD - 1, 1)
    return dx, jnp.zeros_like(cos_f), jnp.zeros_like(sin_a), jnp.zeros_like(sin_b)


_rope.defvjp(_rope_fwd, _rope_bwd)


def _row_ids(i, tm):
    return i * tm + lax.broadcasted_iota(jnp.int32, (tm, 1), 0)


def _partial_rows(ref, rows):
    n = len(rows)
    for k, r in enumerate(rows):
        ref[k:k + 1, :] = r
    ref[n:SUB, :] = jnp.zeros((SUB - n, ref.shape[-1]), f32)


def _matmul(a, b, mode, tm, tn, tk, name, out_dtype=f32, add=None):
    if mode == "nn":
        (M, K), N = a.shape, b.shape[1]
        a_spec = pl.BlockSpec((tm, tk), lambda j, i, k: (i, k))
        b_spec = pl.BlockSpec((tk, tn), lambda j, i, k: (k, j))
        ca, cb = 1, 0
    elif mode == "nt":
        (M, K), N = a.shape, b.shape[0]
        a_spec = pl.BlockSpec((tm, tk), lambda j, i, k: (i, k))
        b_spec = pl.BlockSpec((tn, tk), lambda j, i, k: (j, k))
        ca, cb = 1, 1
    else:
        (K, M), N = a.shape, b.shape[1]
        a_spec = pl.BlockSpec((tk, tm), lambda j, i, k: (k, i))
        b_spec = pl.BlockSpec((tk, tn), lambda j, i, k: (k, j))
        ca, cb = 0, 0
    assert M % tm == 0 and N % tn == 0 and K % tk == 0, (name, M, N, K, tm, tn, tk)
    nk = K // tk

    o_spec = pl.BlockSpec((tm, tn), lambda j, i, k: (i, j))

    def body(a_ref, b_ref, *rest):
        add_ref = rest[0] if add is not None else None
        o_ref, acc_ref = rest[-2:]
        k = pl.program_id(2)
        part = _dg(a_ref[...], b_ref[...], ca, cb)

        @pl.when(k == 0)
        def _():
            acc_ref[...] = part if add_ref is None else part + add_ref[...]

        @pl.when(k > 0)
        def _():
            acc_ref[...] += part

        @pl.when(k == nk - 1)
        def _():
            o_ref[...] = acc_ref[...].astype(o_ref.dtype)

    return pl.pallas_call(
        body, name=name, grid=(N // tn, M // tm, nk),
        in_specs=[a_spec, b_spec] + ([o_spec] if add is not None else []), out_specs=o_spec,
        out_shape=jax.ShapeDtypeStruct((M, N), out_dtype),
        scratch_shapes=[pltpu.VMEM((tm, tn), f32)],
        compiler_params=_cparams(("parallel", "parallel", "arbitrary")),
    )(a, b, *([add] if add is not None else []))


def _matmul_tn_batched(a, b, name):
    B, K, M = a.shape
    N = b.shape[2]

    def body(a_ref, b_ref, o_ref):
        o_ref[...] = _dg(a_ref[...], b_ref[...], 0, 0)

    return pl.pallas_call(
        body, name=name, grid=(B,),
        in_specs=[pl.BlockSpec((None, K, M), lambda i: (i, 0, 0)), pl.BlockSpec((None, K, N), lambda i: (i, 0, 0))],
        out_specs=pl.BlockSpec((None, M, N), lambda i: (i, 0, 0)),
        out_shape=jax.ShapeDtypeStruct((B, M, N), f32),
        compiler_params=_cparams(("parallel",)),
    )(a, b)


MOD_TN = 768


def _mod_fwd(cin, w_mod, b_mod):
    def body(c_ref, w_ref, b_ref, o_ref):
        o_ref[...] = mm(_silu(c_ref[...]), w_ref[...]) + b_ref[...]

    return pl.pallas_call(
        body, name="mod_fwd", grid=(DEPTH, 3 * D // MOD_TN),
        in_specs=[pl.BlockSpec((SUB, D), lambda l, j: (0, 0)),
                  pl.BlockSpec((None, D, MOD_TN), lambda l, j: (l, 0, j)),
                  pl.BlockSpec((None, 1, MOD_TN), lambda l, j: (l, 0, j))],
        out_specs=pl.BlockSpec((None, SUB, MOD_TN), lambda l, j: (l, 0, j)),
        out_shape=jax.ShapeDtypeStruct((DEPTH, SUB, 3 * D), f32),
        compiler_params=_cparams(("parallel", "parallel")),
    )(cin, w_mod, b_mod.reshape(DEPTH, 1, 3 * D))


def _mod_bwd(cin, w_mod, dmodv):
    nj = 3 * D // MOD_TN

    def body(c_ref, w_ref, g_ref, dw_ref, dc_ref):
        _, vjp = jax.vjp(lambda c, w: mm(_silu(c), w), c_ref[...], w_ref[...].astype(f32))
        dc, dw = vjp(g_ref[...])
        dw_ref[...] = dw
        dc_ref[...] = dc

    return pl.pallas_call(
        body, name="mod_bwd", grid=(DEPTH, nj),
        in_specs=[pl.BlockSpec((SUB, D), lambda l, j: (0, 0)),
                  pl.BlockSpec((None, D, MOD_TN), lambda l, j: (l, 0, j)),
                  pl.BlockSpec((None, SUB, MOD_TN), lambda l, j: (l, 0, j))],
        out_specs=[pl.BlockSpec((None, D, MOD_TN), lambda l, j: (l, 0, j)),
                   pl.BlockSpec((None, None, SUB, D), lambda l, j: (l, j, 0, 0))],
        out_shape=[jax.ShapeDtypeStruct((DEPTH, D, 3 * D), f32),
                   jax.ShapeDtypeStruct((DEPTH, nj, SUB, D), f32)],
        compiler_params=_cparams(("parallel", "parallel")),
    )(cin, w_mod, dmodv)


def _u_fn(h, m_l, m_c, isctx):
    n = _ln(h)
    shift = jnp.where(isctx, m_c[:, 0:D], m_l[:, 0:D])
    scale = jnp.where(isctx, m_c[:, D:2 * D], m_l[:, D:2 * D])
    return n * (1.0 + scale) + shift


def _ln_fwd(h, modv_l, tc, tm, name):
    T = h.shape[0]

    def body(h_ref, m_ref, u_ref):
        isctx = _row_ids(pl.program_id(0), tm) < tc
        u_ref[...] = _u_fn(h_ref[...], m_ref[0:1, :], m_ref[1:2, :], isctx).astype(bf16)

    return pl.pallas_call(
        body, name=name, grid=(T // tm,),
        in_specs=[pl.BlockSpec((tm, D), lambda i: (i, 0)), pl.BlockSpec((SUB, 3 * D), lambda i: (0, 0))],
        out_specs=pl.BlockSpec((tm, D), lambda i: (i, 0)),
        out_shape=jax.ShapeDtypeStruct((T, D), bf16),
        compiler_params=_cparams(("parallel",)),
    )(h, modv_l)


def _ln_bwd(du, h, dh_res, modv_l, tc, tm, name):
    T = h.shape[0]
    nt = T // tm

    def body(du_ref, h_ref, r_ref, m_ref, dh_ref, dm_ref):
        isctx = _row_ids(pl.program_id(0), tm) < tc
        _, vjp = jax.vjp(lambda h, ml, mc: _u_fn(h, ml, mc, isctx), h_ref[...], m_ref[0:1, :], m_ref[1:2, :])
        dh, dml, dmc = vjp(du_ref[...])
        dh_ref[...] = dh + r_ref[...]
        _partial_rows(dm_ref, [dml, dmc])

    return pl.pallas_call(
        body, name=name, grid=(nt,),
        in_specs=[pl.BlockSpec((tm, D), lambda i: (i, 0)), pl.BlockSpec((tm, D), lambda i: (i, 0)),
                  pl.BlockSpec((tm, D), lambda i: (i, 0)), pl.BlockSpec((SUB, 3 * D), lambda i: (0, 0))],
        out_specs=[pl.BlockSpec((tm, D), lambda i: (i, 0)), pl.BlockSpec((None, SUB, 3 * D), lambda i: (i, 0, 0))],
        out_shape=[jax.ShapeDtypeStruct((T, D), f32), jax.ShapeDtypeStruct((nt, SUB, 3 * D), f32)],
        compiler_params=_cparams(("parallel",)),
    )(du, h, dh_res, modv_l)


def _prep_fn(q, k, qg, kg, cos_f, sin_a, sin_b):
    qs = [_rope(_rms(q[:, HD * i:HD * (i + 1)], qg), cos_f, sin_a, sin_b) for i in range(A_HEADS)]
    ks = [_rope(_rms(k[:, HD * i:HD * (i + 1)], kg), cos_f, sin_a, sin_b) for i in range(A_HEADS // 2)]
    return jnp.concatenate(qs, 1), jnp.concatenate(ks, 1)


def _tok(tm, w, off):
    return pl.BlockSpec((tm, w), lambda i: (i, off // w))


def _vec(w):
    return pl.BlockSpec((1, w), lambda i: (0, 0))


def _prep_fwd(P, qg, kg, rope, tm, name):
    T = P.shape[0]

    def body(q_ref, k_ref, v_ref, qg_ref, kg_ref, c_ref, sa_ref, sb_ref, qn_ref, kn_ref, vb_ref):
        qn, kn = _prep_fn(q_ref[...], k_ref[...], qg_ref[...], kg_ref[...], c_ref[...], sa_ref[...], sb_ref[...])
        qn_ref[...] = qn.astype(bf16)
        kn_ref[...] = kn.astype(bf16)
        vb_ref[...] = v_ref[...].astype(bf16)

    return pl.pallas_call(
        body, name=name, grid=(T // tm,),
        in_specs=[_tok(tm, 512, 0), _tok(tm, 256, A_K), _tok(tm, 256, A_V), _vec(HD), _vec(HD),
                  _tok(tm, HD, 0), _tok(tm, HD, 0), _tok(tm, HD, 0)],
        out_specs=[_tok(tm, 512, 0), _tok(tm, 256, 0), _tok(tm, 256, 0)],
        out_shape=[jax.ShapeDtypeStruct((T, 512), bf16), jax.ShapeDtypeStruct((T, 256), bf16),
                   jax.ShapeDtypeStruct((T, 256), bf16)],
        compiler_params=_cparams(("parallel",)),
    )(P, P, P, qg, kg, *rope)


def _prep_bwd(P, dqn, dkn, dv, qg, kg, rope, tm, name):
    T = P.shape[0]
    nt = T // tm

    def body(q_ref, k_ref, dq_ref, dk_ref, dv_ref, qg_ref, kg_ref, c_ref, sa_ref, sb_ref, o_ref, og_ref):
        tabs = (c_ref[...], sa_ref[...], sb_ref[...])
        _, vjp = jax.vjp(lambda q, k, a, b: _prep_fn(q, k, a, b, *tabs), q_ref[...], k_ref[...], qg_ref[...], kg_ref[...])
        dq, dk, dqg, dkg = vjp((dq_ref[...], dk_ref[...]))
        o_ref[:, 0:A_K] = dq
        o_ref[:, A_K:A_V] = dk
        o_ref[:, A_V:W_A] = dv_ref[...]
        _partial_rows(og_ref, [dqg, dkg])

    return pl.pallas_call(
        body, name=name, grid=(nt,),
        in_specs=[_tok(tm, 512, 0), _tok(tm, 256, A_K), _tok(tm, 512, 0), _tok(tm, 256, 0), _tok(tm, 256, 0),
                  _vec(HD), _vec(HD), _tok(tm, HD, 0), _tok(tm, HD, 0), _tok(tm, HD, 0)],
        out_specs=[_tok(tm, W_A, 0), pl.BlockSpec((None, SUB, HD), lambda i: (i, 0, 0))],
        out_shape=[jax.ShapeDtypeStruct((T, W_A), f32), jax.ShapeDtypeStruct((nt, SUB, HD), f32)],
        compiler_params=_cparams(("parallel",)),
    )(P, P, dqn, dkn, dv, qg, kg, *rope)


def _attn_fn(q, k, v, lim):
    s = mm_nt(q, k) * (HD ** -0.5)
    col = lax.broadcasted_iota(jnp.int32, s.shape, 1)
    s = jnp.where(col < lim, s, -1e30)
    m = jnp.max(s, -1, keepdims=True)
    e = jnp.exp(s - m)
    p = e / jnp.sum(e, -1, keepdims=True)
    return mm(p, v)


def _attn_fwd(qn, kn, vb, tc, tq, name):
    T = qn.shape[0]

    def body(q_ref, k_ref, v_ref, o_ref):
        lim = jnp.where(pl.program_id(1) * tq < tc, tc, T)
        o_ref[...] = _attn_fn(q_ref[...], k_ref[...], v_ref[...], lim)

    return pl.pallas_call(
        body, name=name, grid=(A_HEADS, T // tq),
        in_specs=[pl.BlockSpec((tq, HD), lambda h, i: (i, h)), pl.BlockSpec((T, HD), lambda h, i: (0, h // 2)),
                  pl.BlockSpec((T, HD), lambda h, i: (0, h // 2))],
        out_specs=pl.BlockSpec((tq, HD), lambda h, i: (i, h)),
        out_shape=jax.ShapeDtypeStruct((T, 512), f32),
        compiler_params=_cparams(("parallel", "parallel")),
    )(qn, kn, vb)


def _attn_bwd(qn, kn, vb, dya, tc, tq, name):
    T = qn.shape[0]

    def body(q_ref, k_ref, v_ref, g_ref, dq_ref, dk_ref, dv_ref):
        first = (pl.program_id(1) == 0) & (pl.program_id(2) == 0)
        lim = jnp.where(pl.program_id(2) * tq < tc, tc, T)
        _, vjp = jax.vjp(lambda q, k, v: _attn_fn(q, k, v, lim), q_ref[...].astype(f32), k_ref[...].astype(f32),
                         v_ref[...].astype(f32))
        dq, dk, dv = vjp(g_ref[...])
        dq_ref[...] = dq

        @pl.when(first)
        def _():
            dk_ref[...] = dk
            dv_ref[...] = dv

        @pl.when(jnp.logical_not(first))
        def _():
            dk_ref[...] += dk
            dv_ref[...] += dv

    qspec = pl.BlockSpec((tq, HD), lambda kv, g, i: (i, 2 * kv + g))
    kspec = pl.BlockSpec((T, HD), lambda kv, g, i: (0, kv))
    return pl.pallas_call(
        body, name=name, grid=(A_HEADS // 2, 2, T // tq),
        in_specs=[qspec, kspec, kspec, qspec], out_specs=[qspec, kspec, kspec],
        out_shape=[jax.ShapeDtypeStruct((T, 512), f32), jax.ShapeDtypeStruct((T, 256), f32),
                   jax.ShapeDtypeStruct((T, 256), f32)],
        compiler_params=_cparams(("parallel", "arbitrary", "arbitrary")),
    )(qn, kn, vb, dya)


def _conv_rows(tc, tl):
    return CONV_PAD + tc + CONV_PAD + tl + CONV_PAD


def _fill_pad(pad_ref, val, tc, tl):
    z = jnp.zeros((CONV_PAD, LANE), f32)
    pad_ref[0:CONV_PAD, :] = z
    pad_ref[CONV_PAD:CONV_PAD + tc, :] = val[0:tc]
    pad_ref[CONV_PAD + tc:2 * CONV_PAD + tc, :] = z
    pad_ref[2 * CONV_PAD + tc:2 * CONV_PAD + tc + tl, :] = val[tc:tc + tl]
    pad_ref[2 * CONV_PAD + tc + tl:3 * CONV_PAD + tc + tl, :] = z


def _conv_apply(pad_ref, w_ref, K, tc, tl, rc, emit, flip=False):
    half = K // 2
    for seg0, off, n in ((0, CONV_PAD, tc), (tc, 2 * CONV_PAD + tc, tl)):
        for r0 in range(0, n, rc):
            acc = None
            for k in range(K):
                sh = (half - k) if flip else (k - half)
                term = pad_ref[pl.ds(off + r0 + sh, rc), :] * w_ref[k:k + 1, :]
                acc = term if acc is None else acc + term
            emit(seg0 + r0, acc)


def _conv_wgrad(pad_ref, dy_ref, K, tc, tl, rc, dw_ref):
    half = K // 2
    for k in range(K):
        acc = jnp.zeros((1, LANE), f32)
        for seg0, off, n in ((0, CONV_PAD, tc), (tc, 2 * CONV_PAD + tc, tl)):
            for r0 in range(0, n, rc):
                acc = acc + jnp.sum(pad_ref[pl.ds(off + r0 + k - half, rc), :] * dy_ref[pl.ds(seg0 + r0, rc), :],
                                    axis=0, keepdims=True)
        dw_ref[k:k + 1, :] = acc


def _col(T, off):
    return pl.BlockSpec((T, LANE), lambda j: (0, off // LANE + j))


def _ctile(T):
    return pl.BlockSpec((T, CT), lambda j: (0, j))


C_B, C_C, C_X, C_A, C_G = (slice(LANE * i, LANE * (i + 1)) for i in range(5))


def _conv_fwd(P, wb, wd, bd, tc, tl, rc, name):
    T = tc + tl

    def body(p_ref, wb_ref, wd_ref, bd_ref, yb_ref, hh_ref, pad_ref):
        _fill_pad(pad_ref, p_ref[:, C_C] * p_ref[:, C_X], tc, tl)

        def emit_b(r0, y):
            yb_ref[pl.ds(r0, rc), :] = y * p_ref[pl.ds(r0, rc), C_B]

        _conv_apply(pad_ref, wb_ref, KB, tc, tl, rc, emit_b)
        _fill_pad(pad_ref, p_ref[:, C_A] * jax.nn.sigmoid(p_ref[:, C_G]), tc, tl)

        def emit_d(r0, y):
            hh_ref[pl.ds(r0, rc), :] = y + bd_ref[...]

        _conv_apply(pad_ref, wd_ref, KD, tc, tl, rc, emit_d)

    return pl.pallas_call(
        body, name=name, grid=(BRW // LANE,),
        in_specs=[_ctile(T), pl.BlockSpec((KB, LANE), lambda j: (0, j)), pl.BlockSpec((KD, LANE), lambda j: (0, j)),
                  pl.BlockSpec((1, LANE), lambda j: (0, j))],
        out_specs=[_col(T, 0), _col(T, 0)],
        out_shape=[jax.ShapeDtypeStruct((T, BRW), f32), jax.ShapeDtypeStruct((T, BRW), f32)],
        scratch_shapes=[pltpu.VMEM((_conv_rows(tc, tl), LANE), f32)],
        compiler_params=_cparams(("parallel",)),
    )(P, wb, wd, bd)


def _conv_bwd(P, dyb, dhh, wb, wd, tc, tl, rc, name):
    T = tc + tl

    def body(p_ref, dyb_ref, dhh_ref, wb_ref, wd_ref, dp_ref, dwb_ref, dwd_ref, dbd_ref, pad_ref, pad2_ref, tmp_ref):
        _fill_pad(pad_ref, p_ref[:, C_C] * p_ref[:, C_X], tc, tl)

        def emit_cv(r0, y):
            dp_ref[pl.ds(r0, rc), C_B] = y * dyb_ref[pl.ds(r0, rc), :]

        _conv_apply(pad_ref, wb_ref, KB, tc, tl, rc, emit_cv)
        tmp_ref[...] = dyb_ref[...] * p_ref[:, C_B]
        _conv_wgrad(pad_ref, tmp_ref, KB, tc, tl, rc, dwb_ref)
        _fill_pad(pad2_ref, tmp_ref[...], tc, tl)

        def emit_ds(r0, y):
            dp_ref[pl.ds(r0, rc), C_C] = y * p_ref[pl.ds(r0, rc), C_X]
            dp_ref[pl.ds(r0, rc), C_X] = y * p_ref[pl.ds(r0, rc), C_C]

        _conv_apply(pad2_ref, wb_ref, KB, tc, tl, rc, emit_ds, flip=True)
        _fill_pad(pad_ref, p_ref[:, C_A] * jax.nn.sigmoid(p_ref[:, C_G]), tc, tl)
        _conv_wgrad(pad_ref, dhh_ref, KD, tc, tl, rc, dwd_ref)
        dbd_ref[...] = jnp.sum(dhh_ref[...], axis=0, keepdims=True)
        _fill_pad(pad2_ref, dhh_ref[...], tc, tl)

        def emit_d2(r0, y):
            sg = jax.nn.sigmoid(p_ref[pl.ds(r0, rc), C_G])
            a = p_ref[pl.ds(r0, rc), C_A]
            dp_ref[pl.ds(r0, rc), C_A] = y * sg
            dp_ref[pl.ds(r0, rc), C_G] = y * a * sg * (1.0 - sg)

        _conv_apply(pad2_ref, wd_ref, KD, tc, tl, rc, emit_d2, flip=True)

    return pl.pallas_call(
        body, name=name, grid=(BRW // LANE,),
        in_specs=[_ctile(T), _col(T, 0), _col(T, 0),
                  pl.BlockSpec((KB, LANE), lambda j: (0, j)), pl.BlockSpec((KD, LANE), lambda j: (0, j))],
        out_specs=[_ctile(T), pl.BlockSpec((KB, LANE), lambda j: (0, j)), pl.BlockSpec((KD, LANE), lambda j: (0, j)),
                   pl.BlockSpec((1, LANE), lambda j: (0, j))],
        out_shape=[jax.ShapeDtypeStruct((T, W_C), f32), jax.ShapeDtypeStruct((KB, BRW), f32),
                   jax.ShapeDtypeStruct((KD, BRW), f32), jax.ShapeDtypeStruct((1, BRW), f32)],
        scratch_shapes=[pltpu.VMEM((_conv_rows(tc, tl), LANE), f32), pltpu.VMEM((_conv_rows(tc, tl), LANE), f32),
                        pltpu.VMEM((T, LANE), f32)],
        compiler_params=_cparams(("parallel",)),
    )(P, dyb, dhh, wb, wd)


def _gla_chunk(q, k, v, r, w2, b2, st, isfwd):
    z = mm(r, w2) + b2
    g = jax.nn.log_sigmoid(jnp.where(isfwd, z[:, 0:C_KW], z[:, C_KW:2 * C_KW])) / C_TAU
    ri = lax.broadcasted_iota(jnp.int32, (CH, CH), 0)
    ci = lax.broadcasted_iota(jnp.int32, (CH, CH), 1)
    keep = jnp.where(isfwd, ri - ci, ci - ri) >= 0
    tri = keep.astype(f32)
    cum = jnp.dot(tri, g, preferred_element_type=f32, precision=lax.Precision.HIGHEST)
    last = jnp.sum(g, axis=0, keepdims=True)
    q = q * (C_KW // C_HEADS) ** -0.5
    hv = lax.broadcasted_iota(jnp.int32, (BRW, C_KW), 0) // (BRW // C_HEADS)
    hk = lax.broadcasted_iota(jnp.int32, (BRW, C_KW), 1) // (C_KW // C_HEADS)
    st_new = st * jnp.exp(last) + jnp.where(hv == hk, mm_tn(v, k * jnp.exp(last - cum)), 0.0)
    o = mm_nt(q * jnp.exp(cum), st)
    rowi = lax.broadcasted_iota(jnp.int32, (CH, C_KW), 0)
    lane_head = lax.broadcasted_iota(jnp.int32, (CH, C_KW), 1) // (C_KW // C_HEADS)
    scores = [jnp.zeros((CH, CH), f32) for _ in range(C_HEADS)]
    for a in range(CH // GLA_SUB):
        idx = jnp.where(isfwd, GLA_SUB * a - 1, GLA_SUB * (a + 1))
        ref = jnp.sum(jnp.where(rowi == idx, cum, 0.0), axis=0, keepdims=True)
        qa = q * jnp.exp(jnp.minimum(cum - ref, 0.0))
        ka = k * jnp.exp(jnp.minimum(ref - cum, GLA_CLAMP))
        in_block = (ri // GLA_SUB == a) & keep
        for hd in range(C_HEADS):
            s = mm_nt(jnp.where(lane_head == hd, qa, 0.0), ka)
            scores[hd] = scores[hd] + jnp.where(in_block, s, 0.0)
    vw = BRW // C_HEADS
    o = o + jnp.concatenate([mm(scores[hd], v[:, vw * hd:vw * (hd + 1)]) for hd in range(C_HEADS)], axis=1)
    return o, st_new


def _gla_chunk_of(d, n, nc, nch):
    back = jnp.where(n < nc, nc - 1 - n, nch - 1 - (n - nc))
    return jnp.where(d == 0, n, back)


def _gla_fwd(P, w2, b2, tc, name):
    T = P.shape[0]
    nch, nc = T // CH, tc // CH

    def body(p_ref, w_ref, b_ref, o_ref, ss_ref, st_ref):
        @pl.when(pl.program_id(1) == 0)
        def _():
            st_ref[...] = jnp.zeros_like(st_ref)

        st = st_ref[...]
        ss_ref[...] = st
        o, st_new = _gla_chunk(p_ref[:, 0:G_K], p_ref[:, G_K:G_V], p_ref[:, G_V:G_R], p_ref[:, G_R:W_G], w_ref[...],
                               b_ref[...], st, pl.program_id(0) == 0)
        o_ref[...] = o
        st_ref[...] = st_new

    return pl.pallas_call(
        body, name=name, grid=(2, nch),
        in_specs=[pl.BlockSpec((CH, W_G), lambda d, n: (_gla_chunk_of(d, n, nc, nch), 0)),
                  pl.BlockSpec((LANE, 512), lambda d, n: (0, 0)), pl.BlockSpec((1, 512), lambda d, n: (0, 0))],
        out_specs=[pl.BlockSpec((None, CH, BRW), lambda d, n: (d, _gla_chunk_of(d, n, nc, nch), 0)),
                   pl.BlockSpec((None, None, BRW, C_KW), lambda d, n: (d, n, 0, 0))],
        out_shape=[jax.ShapeDtypeStruct((2, T, BRW), f32), jax.ShapeDtypeStruct((2, nch, BRW, C_KW), f32)],
        scratch_shapes=[pltpu.VMEM((BRW, C_KW), f32)],
        compiler_params=_cparams(("parallel", "arbitrary")),
    )(P, w2, b2)


def _gla_bwd(P, w2, b2, ssave, doc, tc, name):
    T = P.shape[0]
    nch, nc = T // CH, tc // CH

    def chunk(d, m):
        return _gla_chunk_of(d, nch - 1 - m, nc, nch)

    def body(p_ref, w_ref, b_ref, ss_ref, g_ref, dp_ref, dw_ref, db_ref, dst_ref):
        m = pl.program_id(1)
        isfwd = pl.program_id(0) == 0

        @pl.when(m == 0)
        def _():
            dst_ref[...] = jnp.zeros_like(dst_ref)

        _, vjp = jax.vjp(lambda q, k, v, r, w, b, st: _gla_chunk(q, k, v, r, w, b, st, isfwd),
                         p_ref[:, 0:G_K], p_ref[:, G_K:G_V], p_ref[:, G_V:G_R], p_ref[:, G_R:W_G], w_ref[...], b_ref[...],
                         ss_ref[...])
        dq, dk, dv, dr, dw, db, dst = vjp((g_ref[...], dst_ref[...]))
        dp_ref[:, 0:G_K] = dq
        dp_ref[:, G_K:G_V] = dk
        dp_ref[:, G_V:G_R] = dv
        dp_ref[:, G_R:W_G] = dr
        dst_ref[...] = dst

        @pl.when(m == 0)
        def _():
            dw_ref[...] = dw
            _partial_rows(db_ref, [db])

        @pl.when(m > 0)
        def _():
            dw_ref[...] += dw
            db_ref[0:1, :] += db

    return pl.pallas_call(
        body, name=name, grid=(2, nch),
        in_specs=[pl.BlockSpec((CH, W_G), lambda d, m: (chunk(d, m), 0)),
                  pl.BlockSpec((LANE, 512), lambda d, m: (0, 0)), pl.BlockSpec((1, 512), lambda d, m: (0, 0)),
                  pl.BlockSpec((None, None, BRW, C_KW), lambda d, m: (d, nch - 1 - m, 0, 0)),
                  pl.BlockSpec((CH, BRW), lambda d, m: (chunk(d, m), 0))],
        out_specs=[pl.BlockSpec((None, CH, W_G), lambda d, m: (d, chunk(d, m), 0)),
                   pl.BlockSpec((None, LANE, 512), lambda d, m: (d, 0, 0)),
                   pl.BlockSpec((None, SUB, 512), lambda d, m: (d, 0, 0))],
        out_shape=[jax.ShapeDtypeStruct((2, T, W_G), f32),
                   jax.ShapeDtypeStruct((2, LANE, 512), f32), jax.ShapeDtypeStruct((2, SUB, 512), f32)],
        scratch_shapes=[pltpu.VMEM((BRW, C_KW), f32)],
        compiler_params=_cparams(("parallel", "arbitrary")),
    )(P, w2, b2, ssave, doc)


def _sum_dirs(a, tm, name):
    _, T, W = a.shape

    def body(a_ref, o_ref):
        o_ref[...] = a_ref[0] + a_ref[1]

    return pl.pallas_call(
        body, name=name, grid=(T // tm,),
        in_specs=[pl.BlockSpec((2, tm, W), lambda i: (0, i, 0))], out_specs=pl.BlockSpec((tm, W), lambda i: (i, 0)),
        out_shape=jax.ShapeDtypeStruct((T, W), f32),
        compiler_params=_cparams(("parallel",)),
    )(a)


def _merge_fn(h, m_l, m_c, isctx, ya, ga, yb, gb, of, ob, gc, hh, gd, mg, es, ey, cn, dng, dnb, lg, lb, wbr, wout):
    oc = of + ob
    yc = jnp.concatenate([_rms(oc[:, HD * i:HD * (i + 1)], cn[:, HD * i:HD * (i + 1)]) for i in range(C_HEADS)], 1)
    brs = [ya * _silu(ga), yb * _silu(gb), yc * _silu(gc), _silu(_ln(hh) * dng + dnb) * _silu(gd)]
    acc = None
    for i in range(4):
        t = jax.nn.sigmoid(mg[:, D * i:D * (i + 1)]) * (mm(brs[i], wbr[i]) + es[i])
        acc = t if acc is None else acc + t
    y = mm(acc, wout) + ey
    gate = jnp.where(isctx, m_c[:, 2 * D:3 * D], m_l[:, 2 * D:3 * D])
    hn = _ln(ALPHA * h + gate * y) * lg + lb
    return hn, (brs, acc)


def _merge_specs(tm):
    t = lambda w, off=0: _tok(tm, w, off)
    return [t(D), pl.BlockSpec((SUB, 3 * D), lambda i: (0, 0)),
            t(BRW), t(BRW, M_GA), t(BRW), t(BRW, M_GB),
            pl.BlockSpec((None, tm, BRW), lambda i: (0, i, 0)), pl.BlockSpec((None, tm, BRW), lambda i: (1, i, 0)),
            t(BRW, M_GC), t(BRW), t(BRW, M_GD), t(4 * D, 0),
            _vec(BRW), _vec(BRW), _vec(BRW), _vec(D), _vec(D),
            pl.BlockSpec((4, BRW, D), lambda i: (0, 0, 0)), pl.BlockSpec((D, D), lambda i: (0, 0))]


def _merge_fwd(h, modv_l, ya, yb, o2, hh, P, cn, dng, dnb, lg, lb, wbr, wout, tc, tm, name):
    T = h.shape[0]

    def body(h_ref, m_ref, ya_ref, ga_ref, yb_ref, gb_ref, of_ref, ob_ref, gc_ref, hh_ref, gd_ref, mg_ref,
             cn_ref, dng_ref, dnb_ref, lg_ref, lb_ref, wbr_ref, wout_ref, o_ref):
        isctx = _row_ids(pl.program_id(0), tm) < tc
        zero = jnp.zeros((tm, D), f32)
        hn, _ = _merge_fn(h_ref[...], m_ref[0:1, :], m_ref[1:2, :], isctx, ya_ref[...], ga_ref[...], yb_ref[...],
                          gb_ref[...], of_ref[...], ob_ref[...], gc_ref[...], hh_ref[...], gd_ref[...], mg_ref[...],
                          [zero] * 4, zero, cn_ref[...], dng_ref[...], dnb_ref[...], lg_ref[...], lb_ref[...],
                          [wbr_ref[i] for i in range(4)], wout_ref[...])
        o_ref[...] = hn

    return pl.pallas_call(
        body, name=name, grid=(T // tm,),
        in_specs=_merge_specs(tm), out_specs=_tok(tm, D, 0),
        out_shape=jax.ShapeDtypeStruct((T, D), f32),
        compiler_params=_cparams(("parallel",)),
    )(h, modv_l, ya, P, yb, P, o2, o2, P, hh, P, P, cn, dng, dnb, lg, lb, wbr, wout)


def _merge_bwd(dhn, h, modv_l, ya, yb, o2, hh, P, cn, dng, dnb, lg, lb, wbr, wout, tc, tm, name):
    T = h.shape[0]
    nt = T // tm

    def body(g_ref, h_ref, m_ref, ya_ref, ga_ref, yb_ref, gb_ref, of_ref, ob_ref, gc_ref, hh_ref, gd_ref, mg_ref,
             cn_ref, dng_ref, dnb_ref, lg_ref, lb_ref, wbr_ref, wout_ref,
             dh_ref, dm_ref, dya_ref, dyb_ref, doc_ref, dhh_ref, dp_ref,
             br_ref, z_ref, acc_ref, dy_ref, dv5_ref, dvd_ref):
        isctx = _row_ids(pl.program_id(0), tm) < tc
        zero = jnp.zeros((tm, D), f32)
        wbr_v = [wbr_ref[i] for i in range(4)]
        wout_v = wout_ref[...]

        def fn(h, ml, mc, ya, ga, yb, gb, oc, gc, hh, gd, mg, e0, e1, e2, e3, ey, cn, dng, dnb, lg, lb):
            return _merge_fn(h, ml, mc, isctx, ya, ga, yb, gb, oc, jnp.zeros_like(oc), gc, hh, gd, mg,
                             [e0, e1, e2, e3], ey, cn, dng, dnb, lg, lb, wbr_v, wout_v)

        _, vjp, (brs, acc) = jax.vjp(
            fn, h_ref[...], m_ref[0:1, :], m_ref[1:2, :], ya_ref[...], ga_ref[...], yb_ref[...], gb_ref[...],
            of_ref[...] + ob_ref[...], gc_ref[...], hh_ref[...], gd_ref[...], mg_ref[...], zero, zero, zero, zero, zero,
            cn_ref[...], dng_ref[...], dnb_ref[...], lg_ref[...], lb_ref[...], has_aux=True)
        (dh, dml, dmc, dya, dga, dyb, dgb, doc, dgc, dhh, dgd, dmg, z0, z1, z2, z3, dy,
         dcn, ddng, ddnb, dlg, dlb) = vjp(g_ref[...])
        dh_ref[...] = dh
        _partial_rows(dm_ref, [dml, dmc])
        dya_ref[...] = dya
        dyb_ref[...] = dyb
        doc_ref[...] = doc
        dhh_ref[...] = dhh
        dp_ref[:, 0:M_GA] = dmg
        dp_ref[:, M_GA:M_GB] = dga
        dp_ref[:, M_GB:M_GC] = dgb
        dp_ref[:, M_GC:M_GD] = dgc
        dp_ref[:, M_GD:W_M] = dgd
        for i, z in enumerate((z0, z1, z2, z3)):
            br_ref[i] = brs[i].astype(bf16)
            z_ref[i] = z.astype(bf16)
        acc_ref[...] = acc.astype(bf16)
        dy_ref[...] = dy.astype(bf16)
        _partial_rows(dv5_ref, [dcn, ddng, ddnb])
        _partial_rows(dvd_ref, [dlg, dlb])

    t = lambda w: _tok(tm, w, 0)
    part = lambda w: pl.BlockSpec((None, SUB, w), lambda i: (i, 0, 0))
    sd = jax.ShapeDtypeStruct
    return pl.pallas_call(
        body, name=name, grid=(nt,),
        in_specs=[t(D)] + _merge_specs(tm),
        out_specs=[t(D), part(3 * D)] + [t(BRW)] * 4 + [t(W_M),
                   pl.BlockSpec((4, tm, BRW), lambda i: (0, i, 0)), pl.BlockSpec((4, tm, D), lambda i: (0, i, 0)),
                   t(D), t(D), part(BRW), part(D)],
        out_shape=[sd((T, D), f32), sd((nt, SUB, 3 * D), f32)] + [sd((T, BRW), f32)] * 4 + [sd((T, W_M), f32),
                   sd((4, T, BRW), bf16), sd((4, T, D), bf16), sd((T, D), bf16), sd((T, D), bf16),
                   sd((nt, SUB, BRW), f32), sd((nt, SUB, D), f32)],
        compiler_params=_cparams(("parallel",)),
    )(dhn, h, modv_l, ya, P, yb, P, o2, o2, P, hh, P, P, cn, dng, dnb, lg, lb, wbr, wout)


def _loss_kernel(h, tgt, tc, tm, name):
    T = h.shape[0]
    nt = T // tm
    nct = tc // tm

    def body(h_ref, t_ref, d_ref, l_ref):
        i = pl.program_id(0)
        err = h_ref[...] - t_ref[...]
        lat = (i >= nct).astype(f32)
        d_ref[...] = err * (lat / D)
        l_ref[...] = jnp.zeros((SUB, LANE), f32) + lat * 0.5 * jnp.sum(err * err) / D

    return pl.pallas_call(
        body, name=name, grid=(nt,),
        in_specs=[pl.BlockSpec((tm, D), lambda i: (i, 0)),
                  pl.BlockSpec((tm, D), lambda i: (jnp.maximum(i - nct, 0), 0))],
        out_specs=[pl.BlockSpec((tm, D), lambda i: (i, 0)), pl.BlockSpec((None, SUB, LANE), lambda i: (i, 0, 0))],
        out_shape=[jax.ShapeDtypeStruct((T, D), f32), jax.ShapeDtypeStruct((nt, SUB, LANE), f32)],
        compiler_params=_cparams(("parallel",)),
    )(h, tgt)


def _rope_tables(tc, tl):
    t = jnp.arange(tl)
    inv = ROPE_THETA ** (-jnp.arange(0, HD // 2, 2, dtype=f32) / (HD // 2))
    ang = jnp.concatenate([(t // GRID_W).astype(f32)[:, None] * inv, (t % GRID_W).astype(f32)[:, None] * inv], -1)
    cos, sin = jnp.repeat(jnp.cos(ang), 2, axis=1), jnp.repeat(jnp.sin(ang), 2, axis=1)
    even = (jnp.arange(HD) % 2 == 0)[None, :]
    cos_f = jnp.concatenate([jnp.ones((tc, HD), f32), cos], 0)
    sin_a = jnp.concatenate([jnp.zeros((tc, HD), f32), jnp.where(even, -sin, 0.0)], 0)
    sin_b = jnp.concatenate([jnp.zeros((tc, HD), f32), jnp.where(even, 0.0, sin)], 0)
    return cos_f, sin_a, sin_b


def _pack_groups(w):
    s = lambda a, n: w[..., a:a + n]
    conv = jnp.stack([s(S_B, BRW), s(S_C, BRW), s(S_X, BRW), s(S_DA, BRW), s(S_DG, BRW)], -2)
    conv = jnp.swapaxes(conv.reshape(conv.shape[:-1] + (BRW // LANE, LANE)), -2, -3).reshape(w.shape[:-1] + (W_C,))
    pad = jnp.zeros(w.shape[:-1] + (LANE - 2 * C_RANK,), w.dtype)
    return dict(M=jnp.concatenate([s(S_MG, 4 * D), s(S_GA, BRW), s(S_GB, BRW), s(S_GC, BRW), s(S_GD, BRW)], -1),
                A=s(S_Q, W_A), C=conv, G=jnp.concatenate([s(S_CQ, 2 * C_KW), s(S_CV, BRW), s(S_R, 2 * C_RANK), pad], -1))


def _unpack_groups(g):
    M, A, C, G = g["M"], g["A"], g["C"], g["G"]
    conv = jnp.swapaxes(C.reshape(C.shape[:-1] + (BRW // LANE, 5, LANE)), -2, -3).reshape(C.shape[:-1] + (5, BRW))
    return jnp.concatenate([A, M[..., M_GA:M_GB], conv[..., 0, :], conv[..., 1, :], conv[..., 2, :], M[..., M_GB:M_GC],
                            G[..., 0:G_R], M[..., M_GC:M_GD], G[..., G_R:G_R + 2 * C_RANK], conv[..., 3, :],
                            conv[..., 4, :], M[..., M_GD:W_M], M[..., 0:M_GA]], -1)


PROJ_TN = dict(M=2048, A=1024, C=1280, G=1152)
DWP_TN = dict(M=768, A=1024, C=640, G=1152)


def _gate_weights(w2_l, gb_l):
    w = jnp.zeros((LANE, 2 * C_KW), f32)
    w = w.at[0:C_RANK, 0:C_KW].set(w2_l[0]).at[C_RANK:2 * C_RANK, C_KW:2 * C_KW].set(w2_l[1])
    return w, jnp.concatenate([gb_l[0], gb_l[1]])[None, :]


def _local_step(x1, c1, ctx1, tgt1, c_ctx, w_mod, b_mod, wp, q_norm, k_norm, b_conv, w2, gb, c_norm, d_conv_w,
                d_conv_b, d_norm_g, d_norm_b, w_br, w_out, ln_g, ln_b, tm):
    tc, tl = ctx1.shape[0], x1.shape[0]
    T = tc + tl
    rc = min(256, tc)
    tmb = tm // 2
    tmm = 768 if T % 768 == 0 else tm
    rope = _rope_tables(tc, tl)
    cin = jnp.concatenate([c1, c_ctx[None, :], jnp.zeros((SUB - 2, D), f32)], 0)
    modv = _mod_fwd(cin, w_mod, b_mod)
    row = lambda v: v[None, :]

    h = jnp.concatenate([ctx1, x1], 0)
    saved = []
    for l in range(DEPTH):
        u = _ln_fwd(h, modv[l], tc, tm, f"ln_fwd{l}")
        P = {k: _matmul(u, wp[l][k], "nn", tmm, PROJ_TN[k], D, f"proj{l}{k}") for k in GROUPS}
        qn, kn, vb = _prep_fwd(P["A"], row(q_norm[l]), row(k_norm[l]), rope, tm, f"prep_fwd{l}")
        ya = _attn_fwd(qn, kn, vb, tc, tm, f"attn_fwd{l}")
        yb, hh = _conv_fwd(P["C"], b_conv[l], d_conv_w[l], row(d_conv_b[l]), tc, tl, rc, f"conv_fwd{l}")
        w2p, b2p = _gate_weights(w2[l], gb[l])
        o2, ssave = _gla_fwd(P["G"], w2p, b2p, tc, f"gla_fwd{l}")
        hn = _merge_fwd(h, modv[l], ya, yb, o2, hh, P["M"], row(c_norm[l]), row(d_norm_g[l]), row(d_norm_b[l]),
                        row(ln_g[l]), row(ln_b[l]), w_br[l], w_out[l], tc, tm, f"merge_fwd{l}")
        saved.append((h, u, P, qn, kn, vb, ya, yb, hh, o2, ssave, w2p, b2p))
        h = hn

    dh, lparts = _loss_kernel(h, tgt1, tc, tm, "loss")
    loss = jnp.sum(lparts[:, 0, 0])

    g = {k: [None] * DEPTH for k in ("wp", "q_norm", "k_norm", "b_conv", "w2", "gb", "c_norm", "d_conv_w", "d_conv_b",
                                     "d_norm_g", "d_norm_b", "w_br", "w_out", "ln_g", "ln_b", "modv")}
    for l in reversed(range(DEPTH)):
        h_in, u, P, qn, kn, vb, ya, yb, hh, o2, ssave, w2p, b2p = saved[l]
        dP = {}
        (dh_res, dm_mg, dya, dyb, doc, dhh, dP["M"], br, z, acc, dy, dv5, dvd) = _merge_bwd(
            dh, h_in, modv[l], ya, yb, o2, hh, P["M"], row(c_norm[l]), row(d_norm_g[l]), row(d_norm_b[l]),
            row(ln_g[l]), row(ln_b[l]), w_br[l], w_out[l], tc, tmb, f"merge_bwd{l}")
        g["w_br"][l] = _matmul_tn_batched(br, z, f"dwbr{l}")
        g["w_out"][l] = _matmul(acc, dy, "tn", D, D, T, f"dwout{l}")
        v5 = jnp.sum(dv5, 0)
        g["c_norm"][l], g["d_norm_g"][l], g["d_norm_b"][l] = v5[0], v5[1], v5[2]
        vd = jnp.sum(dvd, 0)
        g["ln_g"][l], g["ln_b"][l] = vd[0], vd[1]
        dqn, dkn, dv = _attn_bwd(qn, kn, vb, dya, tc, tm, f"attn_bwd{l}")
        dP["A"], dqk = _prep_bwd(P["A"], dqn, dkn, dv, row(q_norm[l]), row(k_norm[l]), rope, tm, f"prep_bwd{l}")
        dqk = jnp.sum(dqk, 0)
        g["q_norm"][l], g["k_norm"][l] = dqk[0], dqk[1]
        dP["C"], dwb, dwd, dbd = _conv_bwd(P["C"], dyb, dhh, b_conv[l], d_conv_w[l], tc, tl, rc, f"conv_bwd{l}")
        g["b_conv"][l], g["d_conv_w"][l], g["d_conv_b"][l] = dwb, dwd, dbd[0]
        dpg2, dw2p, db2p = _gla_bwd(P["G"], w2p, b2p, ssave, doc, tc, f"gla_bwd{l}")
        dP["G"] = _sum_dirs(dpg2, tm, f"gla_sum{l}")
        dw2p = dw2p[0] + dw2p[1]
        db2p = db2p[0, 0] + db2p[1, 0]
        g["w2"][l] = jnp.stack([dw2p[0:C_RANK, 0:C_KW], dw2p[C_RANK:2 * C_RANK, C_KW:2 * C_KW]])
        g["gb"][l] = jnp.stack([db2p[0:C_KW], db2p[C_KW:2 * C_KW]])
        du = None
        for k in GROUPS:
            du = _matmul(dP[k], wp[l][k], "nt", tmm, D, PROJ_TN[k], f"du{l}{k}", add=du)
        g["wp"][l] = {k: _matmul(u, dP[k], "tn", D, DWP_TN[k], T, f"dwp{l}{k}") for k in GROUPS}
        dh, dm_ln = _ln_bwd(du, h_in, dh_res, modv[l], tc, tm, f"ln_bwd{l}")
        g["modv"][l] = jnp.sum(dm_mg, 0) + jnp.sum(dm_ln, 0)

    dmodv = jnp.stack(g.pop("modv"))
    g["w_mod"], dcin = _mod_bwd(cin, w_mod, dmodv)
    g["b_mod"] = dmodv[:, 0, :] + dmodv[:, 1, :]
    g["c_ctx"] = jnp.sum(dcin, (0, 1))[1]
    return loss, dh[tc:], g


def _adamw(w, g, m, v, name, tr=128):
    R, C = w.shape
    if R % tr:
        tr = R

    def body(w_ref, g_ref, m_ref, v_ref, d_ref, nm_ref, nv_ref):
        gg = g_ref[...]
        nm = B1 * m_ref[...] + (1.0 - B1) * gg
        nv = B2 * v_ref[...] + (1.0 - B2) * (gg * gg)
        m_hat = nm / (1.0 - B1 ** STEP)
        v_hat = nv / (1.0 - B2 ** STEP)
        d_ref[...] = -LR * (m_hat / (jnp.sqrt(v_hat) + AEPS) + WD * w_ref[...])
        nm_ref[...] = nm
        nv_ref[...] = nv

    spec = pl.BlockSpec((tr, C), lambda i: (i, 0))
    return pl.pallas_call(
        body, name=name, grid=(R // tr,), in_specs=[spec] * 4, out_specs=[spec] * 3,
        out_shape=[jax.ShapeDtypeStruct((R, C), f32)] * 3,
        compiler_params=_cparams(("parallel",)),
    )(w, g, m, v)


MESH = pl.DeviceIdType.MESH
ANY = pl.BlockSpec(memory_space=pl.ANY)
N_CHIPS = 4


def _place():
    x, y, c = lax.axis_index("x"), lax.axis_index("y"), lax.axis_index("c")
    chips = [(1 - x, y), (x, 1 - y), (1 - x, 1 - y)]
    return x, y, c, chips


def _rows(c, hr):
    return pl.ds(pl.multiple_of(c * hr, SUB), hr)


def _all_gather(arrs, name):
    n = len(arrs)

    def body(*refs):
        ins, outs = refs[:n], refs[n:2 * n]
        send, recv = refs[2 * n:]
        x, y, c, chips = _place()
        me, sib = 2 * x + y, (x, y, 1 - c)

        def copy(a, k, chip_idx, cc, to, src=None):
            hr = ins[a].shape[0] // 2
            blk = outs[a].at[chip_idx, _rows(cc, hr), :]
            return pltpu.make_async_remote_copy(src_ref=blk if src is None else src, dst_ref=blk,
                                                send_sem=send.at[6 * a + k], recv_sem=recv.at[6 * a + k],
                                                device_id=to, device_id_type=MESH)

        first = [copy(a, j, me, c, (*chip, c), src=ins[a].at[_rows(c, ins[a].shape[0] // 2), :])
                 for a in range(n) for j, chip in enumerate(chips)]
        for cp in first:
            cp.start()
        passed = []
        for a in range(n):
            for j, chip in enumerate(chips):
                k = 2 * chip[0] + chip[1]
                copy(a, j, k, c, sib).wait_recv()
                fwd = copy(a, 3 + j, k, c, sib)
                fwd.start()
                passed.append(fwd)
        for a in range(n):
            for j, chip in enumerate(chips):
                copy(a, 3 + j, 2 * chip[0] + chip[1], 1 - c, sib).wait_recv()
        for cp in first + passed:
            cp.wait_send()

    return pl.pallas_call(
        body, name=name, in_specs=[ANY] * n, out_specs=[ANY] * n,
        out_shape=[jax.ShapeDtypeStruct((N_CHIPS,) + a.shape, a.dtype) for a in arrs],
        scratch_shapes=[pltpu.SemaphoreType.DMA((6 * n,)), pltpu.SemaphoreType.DMA((6 * n,))],
    )(*arrs)


def _with_own(gathered, own, chip):
    sel = (jnp.arange(N_CHIPS) == chip).reshape((N_CHIPS,) + (1,) * own.ndim)
    return jnp.where(sel, own[None], gathered)


def _sibling_halves(arrs, name):
    n = len(arrs)

    def body(*refs):
        ins, outs = refs[:n], refs[n:2 * n]
        send, recv = refs[2 * n:]
        x, y, c, _ = _place()
        cps = []
        for a in range(n):
            hr = ins[a].shape[1] // 2
            cps.append(pltpu.make_async_remote_copy(src_ref=ins[a].at[:, _rows(1 - c, hr), :], dst_ref=outs[a],
                                                    send_sem=send.at[a], recv_sem=recv.at[a],
                                                    device_id=(x, y, 1 - c), device_id_type=MESH))
        for cp in cps:
            cp.start()
        for cp in cps:
            cp.wait()

    return pl.pallas_call(
        body, name=name, in_specs=[ANY] * n, out_specs=[ANY] * n,
        out_shape=[jax.ShapeDtypeStruct((a.shape[0], a.shape[1] // 2, a.shape[2]), a.dtype) for a in arrs],
        scratch_shapes=[pltpu.SemaphoreType.DMA((n,)), pltpu.SemaphoreType.DMA((n,))],
    )(*arrs)


def _add_half(gfull, land, cidx, name, tr=128, out_dtype=bf16):
    _, R, C = gfull.shape
    hr = R // 2
    tr = min(tr, hr)
    nb = hr // tr

    def body(c_ref, g_ref, l_ref, o_ref):
        o_ref[...] = (g_ref[...] + l_ref[...]).astype(o_ref.dtype)

    return pl.pallas_call(
        body, name=name,
        grid_spec=pltpu.PrefetchScalarGridSpec(
            num_scalar_prefetch=1, grid=(N_CHIPS, nb),
            in_specs=[pl.BlockSpec((None, tr, C), lambda s, i, cr: (s, cr[0] * nb + i, 0)),
                      pl.BlockSpec((None, tr, C), lambda s, i, cr: (s, i, 0))],
            out_specs=pl.BlockSpec((None, tr, C), lambda s, i, cr: (s, i, 0))),
        out_shape=jax.ShapeDtypeStruct((N_CHIPS, hr, C), out_dtype),
        compiler_params=_cparams(("parallel", "parallel")),
    )(cidx, gfull, land)


def _chip_exchange(arrs, name):
    n = len(arrs)

    def body(*refs):
        ins, outs = refs[:n], refs[n:2 * n]
        send, recv = refs[2 * n:]
        x, y, c, chips = _place()
        me = 2 * x + y
        cps = []
        for a in range(n):
            for j, chip in enumerate(chips):
                k = 2 * chip[0] + chip[1]
                cps.append((pltpu.make_async_remote_copy(
                    src_ref=ins[a].at[k], dst_ref=outs[a].at[me], send_sem=send.at[3 * a + j], recv_sem=recv.at[3 * a + j],
                    device_id=(*chip, c), device_id_type=MESH), a, j, k))
        for cp, *_ in cps:
            cp.start()
        for cp, a, j, k in cps:
            pltpu.make_async_remote_copy(src_ref=ins[a].at[k], dst_ref=outs[a].at[k], send_sem=send.at[3 * a + j],
                                         recv_sem=recv.at[3 * a + j], device_id=(x, y, c), device_id_type=MESH).wait_recv()
        for cp, *_ in cps:
            cp.wait_send()

    return pl.pallas_call(
        body, name=name, in_specs=[ANY] * n, out_specs=[ANY] * n,
        out_shape=[jax.ShapeDtypeStruct(a.shape, a.dtype) for a in arrs],
        scratch_shapes=[pltpu.SemaphoreType.DMA((3 * n,)), pltpu.SemaphoreType.DMA((3 * n,))],
    )(*arrs)


def _sum_chips(land, own, place, name, tr=128):
    _, hr, C = land.shape
    tr = min(tr, hr)
    nb = hr // tr

    def body(p_ref, l_ref, o_ref, out_ref):
        me = p_ref[0]
        mine = o_ref[...].astype(f32)
        acc = None
        for k in range(N_CHIPS):
            t = jnp.where(me == k, mine, l_ref[k].astype(f32))
            acc = t if acc is None else acc + t
        out_ref[...] = acc

    return pl.pallas_call(
        body, name=name,
        grid_spec=pltpu.PrefetchScalarGridSpec(
            num_scalar_prefetch=1, grid=(nb,),
            in_specs=[pl.BlockSpec((N_CHIPS, tr, C), lambda i, p: (0, i, 0)),
                      pl.BlockSpec((None, tr, C), lambda i, p: (p[0], i, 0))],
            out_specs=pl.BlockSpec((tr, C), lambda i, p: (p[1] * nb + i, 0))),
        out_shape=jax.ShapeDtypeStruct((2 * hr, C), f32),
        compiler_params=_cparams(("parallel",)),
    )(place, land, own)


def _sibling_fill(arrs, name):
    n = len(arrs)

    def body(*refs):
        outs = refs[n:2 * n]
        send, recv = refs[2 * n:]
        x, y, c, _ = _place()
        cps = []
        for a in range(n):
            hr = outs[a].shape[0] // 2
            cps.append(pltpu.make_async_remote_copy(src_ref=outs[a].at[_rows(c, hr), :], dst_ref=outs[a].at[_rows(c, hr), :],
                                                    send_sem=send.at[a], recv_sem=recv.at[a],
                                                    device_id=(x, y, 1 - c), device_id_type=MESH))
        for cp in cps:
            cp.start()
        for a in range(n):
            hr = outs[a].shape[0] // 2
            blk = outs[a].at[_rows(1 - c, hr), :]
            pltpu.make_async_remote_copy(src_ref=blk, dst_ref=blk, send_sem=send.at[a], recv_sem=recv.at[a],
                                         device_id=(x, y, 1 - c), device_id_type=MESH).wait_recv()
        for cp in cps:
            cp.wait_send()

    return pl.pallas_call(
        body, name=name, in_specs=[ANY] * n, out_specs=[ANY] * n,
        out_shape=[jax.ShapeDtypeStruct(a.shape, a.dtype) for a in arrs],
        input_output_aliases={a: a for a in range(n)},
        scratch_shapes=[pltpu.SemaphoreType.DMA((n,)), pltpu.SemaphoreType.DMA((n,))],
    )(*arrs)


N_DEV = 8


def _all_reduce_small(v, name):
    R = v.shape[0]

    def body(v_ref, o_ref, land_ref, send, recv):
        x, y, c, _ = _place()
        me = 4 * x + 2 * y + c
        land_ref[me] = v_ref[...]
        cps = []
        for m in range(1, N_DEV):
            px, py, pc = [(1 - q) if (m >> s) & 1 else q for q, s in ((x, 2), (y, 1), (c, 0))]
            cps.append((pltpu.make_async_remote_copy(src_ref=v_ref, dst_ref=land_ref.at[me], send_sem=send.at[m - 1],
                                                     recv_sem=recv.at[m - 1], device_id=(px, py, pc), device_id_type=MESH),
                        4 * px + 2 * py + pc, m))
        for cp, *_ in cps:
            cp.start()
        for cp, peer, m in cps:
            pltpu.make_async_remote_copy(src_ref=v_ref, dst_ref=land_ref.at[peer], send_sem=send.at[m - 1],
                                         recv_sem=recv.at[m - 1], device_id=(x, y, c), device_id_type=MESH).wait_recv()
        for cp, *_ in cps:
            cp.wait_send()
        acc = land_ref[0]
        for k in range(1, N_DEV):
            acc = acc + land_ref[k]
        o_ref[...] = acc

    vm = pl.BlockSpec(memory_space=pltpu.VMEM)
    return pl.pallas_call(
        body, name=name, in_specs=[vm], out_specs=vm, out_shape=jax.ShapeDtypeStruct(v.shape, f32),
        scratch_shapes=[pltpu.VMEM((N_DEV, R, LANE), f32), pltpu.SemaphoreType.DMA((N_DEV - 1,)),
                        pltpu.SemaphoreType.DMA((N_DEV - 1,))],
        compiler_params=pltpu.CompilerParams(vmem_limit_bytes=VMEM_LIMIT),
    )(v)


def _pack_small(arrs, mult=2 * SUB):
    flat = jnp.concatenate([a.reshape(-1) for a in arrs])
    rows = -(-flat.shape[0] // (LANE * mult)) * mult
    return jnp.pad(flat, (0, rows * LANE - flat.shape[0])).reshape(rows, LANE)


def _unpack_small(vec, shapes):
    flat, out, o = vec.reshape(-1), [], 0
    for s in shapes:
        n = int(np.prod(s))
        out.append(flat[o:o + n].reshape(s))
        o += n
    return out


REPL_SMALL = ("c_ctx", "b_mod", "q_norm", "k_norm", "c_norm", "d_conv_b", "d_norm_g", "d_norm_b", "ln_g", "ln_b")
SHARD_SMALL = ("b_conv", "c_gate_w2", "c_gate_b", "d_conv_w")
BIG = ("w_mod", "w_in", "w_br", "w_out")
ORDER = ("c_ctx", "w_mod", "b_mod", "w_in", "q_norm", "k_norm", "b_conv", "c_gate_w2", "c_gate_b", "c_norm", "d_conv_w",
         "d_conv_b", "d_norm_g", "d_norm_b", "w_br", "w_out", "ln_g", "ln_b")


def _unshard_last(g4, shard_shape):
    g = g4.reshape((N_CHIPS,) + tuple(shard_shape))
    g = jnp.moveaxis(g, 0, -2)
    return g.reshape(tuple(shard_shape[:-1]) + (N_CHIPS * shard_shape[-1],))


def _pieces_last(full):
    w = full.shape[-1] // N_CHIPS
    g = full.reshape(full.shape[:-1] + (N_CHIPS, w))
    return jnp.moveaxis(g, -2, 0).reshape(N_CHIPS, -1, w)


def kernel(x, c, ctx, c_ctx, w_mod, b_mod, w_in, q_norm, k_norm, b_conv, c_gate_w2, c_gate_b, c_norm, d_conv_w, d_conv_b, d_norm_g, d_norm_b, w_br, w_out, ln_g, ln_b, loss_target, m_c_ctx, m_w_mod, m_b_mod, m_w_in, m_q_norm, m_k_norm, m_b_conv, m_c_gate_w2, m_c_gate_b, m_c_norm, m_d_conv_w, m_d_conv_b, m_d_norm_g, m_d_norm_b, m_w_br, m_w_out, m_ln_g, m_ln_b, v_c_ctx, v_w_mod, v_b_mod, v_w_in, v_q_norm, v_k_norm, v_b_conv, v_c_gate_w2, v_c_gate_b, v_c_norm, v_d_conv_w, v_d_conv_b, v_d_norm_g, v_d_norm_b, v_w_br, v_w_out, v_ln_g, v_ln_b):
    W = dict(c_ctx=c_ctx, w_mod=w_mod, b_mod=b_mod, w_in=w_in, q_norm=q_norm, k_norm=k_norm, b_conv=b_conv,
             c_gate_w2=c_gate_w2, c_gate_b=c_gate_b, c_norm=c_norm, d_conv_w=d_conv_w, d_conv_b=d_conv_b,
             d_norm_g=d_norm_g, d_norm_b=d_norm_b, w_br=w_br, w_out=w_out, ln_g=ln_g, ln_b=ln_b)
    M = dict(c_ctx=m_c_ctx, w_mod=m_w_mod, b_mod=m_b_mod, w_in=m_w_in, q_norm=m_q_norm, k_norm=m_k_norm, b_conv=m_b_conv,
             c_gate_w2=m_c_gate_w2, c_gate_b=m_c_gate_b, c_norm=m_c_norm, d_conv_w=m_d_conv_w, d_conv_b=m_d_conv_b,
             d_norm_g=m_d_norm_g, d_norm_b=m_d_norm_b, w_br=m_w_br, w_out=m_w_out, ln_g=m_ln_g, ln_b=m_ln_b)
    V = dict(c_ctx=v_c_ctx, w_mod=v_w_mod, b_mod=v_b_mod, w_in=v_w_in, q_norm=v_q_norm, k_norm=v_k_norm, b_conv=v_b_conv,
             c_gate_w2=v_c_gate_w2, c_gate_b=v_c_gate_b, c_norm=v_c_norm, d_conv_w=v_d_conv_w, d_conv_b=v_d_conv_b,
             d_norm_g=v_d_norm_g, d_norm_b=v_d_norm_b, w_br=v_w_br, w_out=v_w_out, ln_g=v_ln_g, ln_b=v_ln_b)
    chip = 2 * lax.axis_index("x") + lax.axis_index("y")
    cidx = lax.axis_index("c").astype(jnp.int32).reshape(1)

    big2d = {k: W[k].reshape(-1, W[k].shape[-1]) for k in BIG}
    small_shard = _pack_small([W[k] for k in SHARD_SMALL])
    sent = [big2d[k].astype(bf16) for k in BIG] + [small_shard]
    gathered = [_with_own(g_, s_, chip) for g_, s_ in zip(_all_gather(sent, "all_gather"), sent)]
    G = dict(zip(BIG, gathered[:4]))
    full = {k: _unshard_last(G[k], W[k].shape) for k in ("w_mod", "w_in", "w_br")}
    full["w_out"] = jnp.moveaxis(G["w_out"].reshape((N_CHIPS,) + w_out.shape), 0, 1).reshape(DEPTH, D, D)
    smalls = [_unpack_small(gathered[4][s], [W[k].shape for k in SHARD_SMALL]) for s in range(N_CHIPS)]
    for i, k in enumerate(SHARD_SMALL):
        full[k] = jnp.concatenate([smalls[s][i] for s in range(N_CHIPS)], axis=-1)

    wp = [_pack_groups(full["w_in"][l]) for l in range(DEPTH)]
    loss, gx, g = _local_step(
        x[0], c, ctx[0], loss_target[0], c_ctx, full["w_mod"], b_mod, wp, q_norm, k_norm, full["b_conv"],
        full["c_gate_w2"], full["c_gate_b"], c_norm, full["d_conv_w"], d_conv_b, d_norm_g, d_norm_b,
        [full["w_br"][l] for l in range(DEPTH)], [full["w_out"][l] for l in range(DEPTH)], ln_g, ln_b, tm=256)
    g["w_in"] = jnp.stack([_unpack_groups(gl) for gl in g.pop("wp")])
    g["c_gate_w2"], g["c_gate_b"] = g.pop("w2"), g.pop("gb")
    g = {k: (jnp.stack(v) if isinstance(v, list) else v) for k, v in g.items()}
    loss = lax.psum(loss, ("x", "y", "c"))

    pieces = [_pieces_last(g[k]) for k in ("w_mod", "w_in", "w_br")]
    pieces.append(jnp.moveaxis(g["w_out"].reshape(DEPTH, N_CHIPS, D // N_CHIPS, D), 1, 0).reshape(N_CHIPS, -1, D))
    land_a = _sibling_halves(pieces, "rs_sibling_halves")
    pair = [_add_half(p, la, cidx, f"rs_pair_sum{i}") for i, (p, la) in enumerate(zip(pieces, land_a))]
    land_b = _chip_exchange(pair, "rs_chip_exchange")
    place = jnp.stack([chip, lax.axis_index("c")]).astype(jnp.int32)
    half = [_sum_chips(lb, pr, place, f"rs_chip_sum{i}") for i, (lb, pr) in enumerate(zip(land_b, pair))]
    red = dict(zip(BIG, _sibling_fill(half, "rs_sibling_fill")))

    small_names = REPL_SMALL + SHARD_SMALL
    gs = _all_reduce_small(_pack_small([g[k] for k in small_names]), "all_reduce_small")
    gsm = dict(zip(small_names, _unpack_small(gs, [g[k].shape for k in small_names])))
    for k in SHARD_SMALL:
        wdt = W[k].shape[-1]
        gsm[k] = lax.dynamic_slice_in_dim(gsm[k], chip * wdt, wdt, axis=gsm[k].ndim - 1)

    grad, delta, new_m, new_v = {}, {}, {}, {}
    for k in BIG:
        grad[k] = red[k].reshape(W[k].shape)
        d_, m_, v_ = _adamw(big2d[k], red[k], M[k].reshape(red[k].shape), V[k].reshape(red[k].shape), f"adamw_{k}")
        delta[k], new_m[k], new_v[k] = d_.reshape(W[k].shape), m_.reshape(W[k].shape), v_.reshape(W[k].shape)
    shapes = [W[k].shape for k in small_names]
    d_, m_, v_ = _adamw(_pack_small([W[k] for k in small_names]), _pack_small([gsm[k] for k in small_names]),
                        _pack_small([M[k] for k in small_names]), _pack_small([V[k] for k in small_names]), "adamw_small")
    for k, dd, mm_, vv in zip(small_names, _unpack_small(d_, shapes), _unpack_small(m_, shapes), _unpack_small(v_, shapes)):
        grad[k], delta[k], new_m[k], new_v[k] = gsm[k], dd, mm_, vv

    return (loss, gx[None], *[grad[k] for k in ORDER], *[delta[k] for k in ORDER], *[new_m[k] for k in ORDER],
            *[new_v[k] for k in ORDER])
```

```python
import functools

import jax
import jax.numpy as jnp
import numpy as np
from jax import lax
from jax.experimental import pallas as pl
from jax.experimental.pallas import tpu as pltpu

f32 = jnp.float32
bf16 = jnp.bfloat16

D = 1024
DEPTH = 2
GRID_W = 64
BRW = 512
HD = 128
A_HEADS = 4
C_HEADS = 4
C_KW = 256
C_RANK = 16
C_TAU = 16.0
CH = 64
KB = 3
KD = 31
ALPHA = (2 * DEPTH) ** 0.25
EPS = 1e-6
ROPE_THETA = 10000.0
N_IN = 10784
LR, B1, B2, AEPS, WD, STEP = 0.001, 0.9, 0.999, 1e-08, 0.01, 10

W_M, W_A, W_C, W_G = 4 * D + 4 * BRW, 1024, 5 * BRW, 1152
GROUPS = ("M", "A", "C", "G")
GROUP_W = dict(M=W_M, A=W_A, C=W_C, G=W_G)
M_GA, M_GB, M_GC, M_GD = 4 * D, 4 * D + BRW, 4 * D + 2 * BRW, 4 * D + 3 * BRW
A_K, A_V = 512, 768
G_K, G_V, G_R = 256, 512, 1024
CT = 5 * 128
S_Q, S_GA, S_B, S_C, S_X, S_GB, S_CQ, S_CV, S_GC, S_R, S_DA, S_DG, S_GD, S_MG = (
    0, 1024, 1536, 2048, 2560, 3072, 3584, 4096, 4608, 5120, 5152, 5664, 6176, 6688)

LANE = 128
SUB = 8
VMEM_LIMIT = 56 * 1024 * 1024
CONV_PAD = 16
GLA_SUB = 16
GLA_CLAMP = 60.0


def _cparams(sem, vmem=VMEM_LIMIT):
    return pltpu.CompilerParams(dimension_semantics=sem, vmem_limit_bytes=vmem)


def _dg(a, b, ca, cb):
    return lax.dot_general(a.astype(bf16), b.astype(bf16), (((ca,), (cb,)), ((), ())),
                           preferred_element_type=f32)


@jax.custom_vjp
def mm(a, b):
    return _dg(a, b, 1, 0)


mm.defvjp(lambda a, b: (_dg(a, b, 1, 0), (a, b)),
          lambda r, ct: (_dg(ct, r[1], 1, 1).astype(r[0].dtype), _dg(r[0], ct, 0, 0).astype(r[1].dtype)))


@jax.custom_vjp
def mm_nt(a, b):
    return _dg(a, b, 1, 1)


mm_nt.defvjp(lambda a, b: (_dg(a, b, 1, 1), (a, b)),
             lambda r, ct: (_dg(ct, r[1], 1, 0).astype(r[0].dtype), _dg(ct, r[0], 0, 0).astype(r[1].dtype)))


@jax.custom_vjp
def mm_tn(a, b):
    return _dg(a, b, 0, 0)


mm_tn.defvjp(lambda a, b: (_dg(a, b, 0, 0), (a, b)),
             lambda r, ct: (_dg(r[1], ct, 1, 1).astype(r[0].dtype), _dg(r[0], ct, 1, 0).astype(r[1].dtype)))


def _silu(x):
    return x * jax.nn.sigmoid(x)


def _ln(x):
    mu = jnp.mean(x, -1, keepdims=True)
    xc = x - mu
    var = jnp.mean(xc * xc, -1, keepdims=True)
    return xc * lax.rsqrt(var + EPS)


def _rms(x, g):
    return x * lax.rsqrt(jnp.mean(x * x, -1, keepdims=True) + EPS) * g


@jax.custom_vjp
def _rope(x, cos_f, sin_a, sin_b):
    return x * cos_f + pltpu.roll(x, HD - 1, 1) * sin_a + pltpu.roll(x, 1, 1) * sin_b


def _rope_fwd(x, cos_f, sin_a, sin_b):
    return _rope(x, cos_f, sin_a, sin_b), (cos_f, sin_a, sin_b)


def _rope_bwd(r, ct):
    cos_f, sin_a, sin_b = r
    dx = ct * cos_f + pltpu.roll(ct * sin_a, 1, 1) + pltpu.roll(ct * sin_b, HD - 1, 1)
    return dx, jnp.zeros_like(cos_f), jnp.zeros_like(sin_a), jnp.zeros_like(sin_b)


_rope.defvjp(_rope_fwd, _rope_bwd)


def _row_ids(i, tm):
    return i * tm + lax.broadcasted_iota(jnp.int32, (tm, 1), 0)


def _partial_rows(ref, rows):
    n = len(rows)
    for k, r in enumerate(rows):
        ref[k:k + 1, :] = r
    ref[n:SUB, :] = jnp.zeros((SUB - n, ref.shape[-1]), f32)


def _matmul(a, b, mode, tm, tn, tk, name, out_dtype=f32, add=None):
    if mode == "nn":
        (M, K), N = a.shape, b.shape[1]
        a_spec = pl.BlockSpec((tm, tk), lambda j, i, k: (i, k))
        b_spec = pl.BlockSpec((tk, tn), lambda j, i, k: (k, j))
        ca, cb = 1, 0
    elif mode == "nt":
        (M, K), N = a.shape, b.shape[0]
        a_spec = pl.BlockSpec((tm, tk), lambda j, i, k: (i, k))
        b_spec = pl.BlockSpec((tn, tk), lambda j, i, k: (j, k))
        ca, cb = 1, 1
    else:
        (K, M), N = a.shape, b.shape[1]
        a_spec = pl.BlockSpec((tk, tm), lambda j, i, k: (k, i))
        b_spec = pl.BlockSpec((tk, tn), lambda j, i, k: (k, j))
        ca, cb = 0, 0
    assert M % tm == 0 and N % tn == 0 and K % tk == 0, (name, M, N, K, tm, tn, tk)
    nk = K // tk

    o_spec = pl.BlockSpec((tm, tn), lambda j, i, k: (i, j))

    def body(a_ref, b_ref, *rest):
        add_ref = rest[0] if add is not None else None
        o_ref, acc_ref = rest[-2:]
        k = pl.program_id(2)
        part = _dg(a_ref[...], b_ref[...], ca, cb)

        @pl.when(k == 0)
        def _():
            acc_ref[...] = part if add_ref is None else part + add_ref[...]

        @pl.when(k > 0)
        def _():
            acc_ref[...] += part

        @pl.when(k == nk - 1)
        def _():
            o_ref[...] = acc_ref[...].astype(o_ref.dtype)

    return pl.pallas_call(
        body, name=name, grid=(N // tn, M // tm, nk),
        in_specs=[a_spec, b_spec] + ([o_spec] if add is not None else []), out_specs=o_spec,
        out_shape=jax.ShapeDtypeStruct((M, N), out_dtype),
        scratch_shapes=[pltpu.VMEM((tm, tn), f32)],
        compiler_params=_cparams(("parallel", "parallel", "arbitrary")),
    )(a, b, *([add] if add is not None else []))


def _matmul_tn_batched(a, b, ns, name):
    B, K, M = a.shape
    N = b.shape[2] // ns

    def body(a_ref, b_ref, o_ref):
        o_ref[...] = _dg(a_ref[...], b_ref[...], 0, 0)

    return pl.pallas_call(
        body, name=name, grid=(B, ns),
        in_specs=[pl.BlockSpec((None, K, M), lambda i, s: (i, 0, 0)), pl.BlockSpec((None, K, N), lambda i, s: (i, 0, s))],
        out_specs=pl.BlockSpec((None, None, M, N), lambda i, s: (s, i, 0, 0)),
        out_shape=jax.ShapeDtypeStruct((ns, B, M, N), f32),
        compiler_params=_cparams(("parallel", "parallel")),
    )(a, b)


MOD_TN = 768


def _mod_fwd(cin, w_mod, b_mod):
    def body(c_ref, w_ref, b_ref, o_ref):
        o_ref[...] = mm(_silu(c_ref[...]), w_ref[...]) + b_ref[...]

    return pl.pallas_call(
        body, name="mod_fwd", grid=(DEPTH, 3 * D // MOD_TN),
        in_specs=[pl.BlockSpec((SUB, D), lambda l, j: (0, 0)),
                  pl.BlockSpec((None, None, D, MOD_TN), lambda l, j: (j, l, 0, 0)),
                  pl.BlockSpec((None, 1, MOD_TN), lambda l, j: (l, 0, j))],
        out_specs=pl.BlockSpec((None, SUB, MOD_TN), lambda l, j: (l, 0, j)),
        out_shape=jax.ShapeDtypeStruct((DEPTH, SUB, 3 * D), f32),
        compiler_params=_cparams(("parallel", "parallel")),
    )(cin, w_mod, b_mod.reshape(DEPTH, 1, 3 * D))


def _mod_bwd(cin, w_mod, dmodv):
    nj = 3 * D // MOD_TN

    def body(c_ref, w_ref, g_ref, dw_ref, dc_ref):
        _, vjp = jax.vjp(lambda c, w: mm(_silu(c), w), c_ref[...], w_ref[...].astype(f32))
        dc, dw = vjp(g_ref[...])
        dw_ref[...] = dw
        dc_ref[...] = dc

    return pl.pallas_call(
        body, name="mod_bwd", grid=(DEPTH, nj),
        in_specs=[pl.BlockSpec((SUB, D), lambda l, j: (0, 0)),
                  pl.BlockSpec((None, None, D, MOD_TN), lambda l, j: (j, l, 0, 0)),
                  pl.BlockSpec((None, SUB, MOD_TN), lambda l, j: (l, 0, j))],
        out_specs=[pl.BlockSpec((None, None, D, MOD_TN), lambda l, j: (j, l, 0, 0)),
                   pl.BlockSpec((None, None, SUB, D), lambda l, j: (l, j, 0, 0))],
        out_shape=[jax.ShapeDtypeStruct((nj, DEPTH, D, MOD_TN), f32),
                   jax.ShapeDtypeStruct((DEPTH, nj, SUB, D), f32)],
        compiler_params=_cparams(("parallel", "parallel")),
    )(cin, w_mod, dmodv)


def _u_fn(h, m_l, m_c, isctx):
    n = _ln(h)
    shift = jnp.where(isctx, m_c[:, 0:D], m_l[:, 0:D])
    scale = jnp.where(isctx, m_c[:, D:2 * D], m_l[:, D:2 * D])
    return n * (1.0 + scale) + shift


def _ln_fwd(h, modv_l, tc, tm, name):
    T = h.shape[0]

    def body(h_ref, m_ref, u_ref):
        isctx = _row_ids(pl.program_id(0), tm) < tc
        u_ref[...] = _u_fn(h_ref[...], m_ref[0:1, :], m_ref[1:2, :], isctx).astype(bf16)

    return pl.pallas_call(
        body, name=name, grid=(T // tm,),
        in_specs=[pl.BlockSpec((tm, D), lambda i: (i, 0)), pl.BlockSpec((SUB, 3 * D), lambda i: (0, 0))],
        out_specs=pl.BlockSpec((tm, D), lambda i: (i, 0)),
        out_shape=jax.ShapeDtypeStruct((T, D), bf16),
        compiler_params=_cparams(("parallel",)),
    )(h, modv_l)


def _ln_bwd(du, h, dh_res, modv_l, tc, tm, name):
    T = h.shape[0]
    nt = T // tm

    def body(du_ref, h_ref, r_ref, m_ref, dh_ref, dm_ref):
        isctx = _row_ids(pl.program_id(0), tm) < tc
        _, vjp = jax.vjp(lambda h, ml, mc: _u_fn(h, ml, mc, isctx), h_ref[...], m_ref[0:1, :], m_ref[1:2, :])
        dh, dml, dmc = vjp(du_ref[...])
        dh_ref[...] = dh + r_ref[...]
        _partial_rows(dm_ref, [dml, dmc])

    return pl.pallas_call(
        body, name=name, grid=(nt,),
        in_specs=[pl.BlockSpec((tm, D), lambda i: (i, 0)), pl.BlockSpec((tm, D), lambda i: (i, 0)),
                  pl.BlockSpec((tm, D), lambda i: (i, 0)), pl.BlockSpec((SUB, 3 * D), lambda i: (0, 0))],
        out_specs=[pl.BlockSpec((tm, D), lambda i: (i, 0)), pl.BlockSpec((None, SUB, 3 * D), lambda i: (i, 0, 0))],
        out_shape=[jax.ShapeDtypeStruct((T, D), f32), jax.ShapeDtypeStruct((nt, SUB, 3 * D), f32)],
        compiler_params=_cparams(("parallel",)),
    )(du, h, dh_res, modv_l)


def _prep_fn(q, k, qg, kg, cos_f, sin_a, sin_b):
    qs = [_rope(_rms(q[:, HD * i:HD * (i + 1)], qg), cos_f, sin_a, sin_b) for i in range(A_HEADS)]
    ks = [_rope(_rms(k[:, HD * i:HD * (i + 1)], kg), cos_f, sin_a, sin_b) for i in range(A_HEADS // 2)]
    return jnp.concatenate(qs, 1), jnp.concatenate(ks, 1)


def _tok(tm, w, off):
    return pl.BlockSpec((tm, w), lambda i: (i, off // w))


def _vec(w):
    return pl.BlockSpec((1, w), lambda i: (0, 0))


def _prep_fwd(P, qg, kg, rope, tm, name):
    T = P.shape[0]

    def body(q_ref, k_ref, v_ref, qg_ref, kg_ref, c_ref, sa_ref, sb_ref, qn_ref, kn_ref, vb_ref):
        qn, kn = _prep_fn(q_ref[...], k_ref[...], qg_ref[...], kg_ref[...], c_ref[...], sa_ref[...], sb_ref[...])
        qn_ref[...] = qn.astype(bf16)
        kn_ref[...] = kn.astype(bf16)
        vb_ref[...] = v_ref[...].astype(bf16)

    return pl.pallas_call(
        body, name=name, grid=(T // tm,),
        in_specs=[_tok(tm, 512, 0), _tok(tm, 256, A_K), _tok(tm, 256, A_V), _vec(HD), _vec(HD),
                  _tok(tm, HD, 0), _tok(tm, HD, 0), _tok(tm, HD, 0)],
        out_specs=[_tok(tm, 512, 0), _tok(tm, 256, 0), _tok(tm, 256, 0)],
        out_shape=[jax.ShapeDtypeStruct((T, 512), bf16), jax.ShapeDtypeStruct((T, 256), bf16),
                   jax.ShapeDtypeStruct((T, 256), bf16)],
        compiler_params=_cparams(("parallel",)),
    )(P, P, P, qg, kg, *rope)


def _prep_bwd(P, dqn, dkn, dv, qg, kg, rope, tm, name):
    T = P.shape[0]
    nt = T // tm

    def body(q_ref, k_ref, dq_ref, dk_ref, dv_ref, qg_ref, kg_ref, c_ref, sa_ref, sb_ref, o_ref, og_ref):
        tabs = (c_ref[...], sa_ref[...], sb_ref[...])
        _, vjp = jax.vjp(lambda q, k, a, b: _prep_fn(q, k, a, b, *tabs), q_ref[...], k_ref[...], qg_ref[...], kg_ref[...])
        dq, dk, dqg, dkg = vjp((dq_ref[...], dk_ref[...]))
        o_ref[:, 0:A_K] = dq
        o_ref[:, A_K:A_V] = dk
        o_ref[:, A_V:W_A] = dv_ref[...]
        _partial_rows(og_ref, [dqg, dkg])

    return pl.pallas_call(
        body, name=name, grid=(nt,),
        in_specs=[_tok(tm, 512, 0), _tok(tm, 256, A_K), _tok(tm, 512, 0), _tok(tm, 256, 0), _tok(tm, 256, 0),
                  _vec(HD), _vec(HD), _tok(tm, HD, 0), _tok(tm, HD, 0), _tok(tm, HD, 0)],
        out_specs=[_tok(tm, W_A, 0), pl.BlockSpec((None, SUB, HD), lambda i: (i, 0, 0))],
        out_shape=[jax.ShapeDtypeStruct((T, W_A), f32), jax.ShapeDtypeStruct((nt, SUB, HD), f32)],
        compiler_params=_cparams(("parallel",)),
    )(P, P, dqn, dkn, dv, qg, kg, *rope)


def _attn_fn(q, k, v, lim):
    s = mm_nt(q, k) * (HD ** -0.5)
    col = lax.broadcasted_iota(jnp.int32, s.shape, 1)
    s = jnp.where(col < lim, s, -1e30)
    m = jnp.max(s, -1, keepdims=True)
    e = jnp.exp(s - m)
    p = e / jnp.sum(e, -1, keepdims=True)
    return mm(p, v)


def _attn_fwd(qn, kn, vb, tc, tq, name):
    T = qn.shape[0]

    def body(q_ref, k_ref, v_ref, o_ref):
        lim = jnp.where(pl.program_id(1) * tq < tc, tc, T)
        o_ref[...] = _attn_fn(q_ref[...], k_ref[...], v_ref[...], lim)

    return pl.pallas_call(
        body, name=name, grid=(A_HEADS, T // tq),
        in_specs=[pl.BlockSpec((tq, HD), lambda h, i: (i, h)), pl.BlockSpec((T, HD), lambda h, i: (0, h // 2)),
                  pl.BlockSpec((T, HD), lambda h, i: (0, h // 2))],
        out_specs=pl.BlockSpec((tq, HD), lambda h, i: (i, h)),
        out_shape=jax.ShapeDtypeStruct((T, 512), f32),
        compiler_params=_cparams(("parallel", "parallel")),
    )(qn, kn, vb)


def _attn_bwd(qn, kn, vb, dya, tc, tq, name):
    T = qn.shape[0]

    def body(q_ref, k_ref, v_ref, g_ref, dq_ref, dk_ref, dv_ref):
        first = (pl.program_id(1) == 0) & (pl.program_id(2) == 0)
        lim = jnp.where(pl.program_id(2) * tq < tc, tc, T)
        _, vjp = jax.vjp(lambda q, k, v: _attn_fn(q, k, v, lim), q_ref[...].astype(f32), k_ref[...].astype(f32),
                         v_ref[...].astype(f32))
        dq, dk, dv = vjp(g_ref[...])
        dq_ref[...] = dq

        @pl.when(first)
        def _():
            dk_ref[...] = dk
            dv_ref[...] = dv

        @pl.when(jnp.logical_not(first))
        def _():
            dk_ref[...] += dk
            dv_ref[...] += dv

    qspec = pl.BlockSpec((tq, HD), lambda kv, g, i: (i, 2 * kv + g))
    kspec = pl.BlockSpec((T, HD), lambda kv, g, i: (0, kv))
    return pl.pallas_call(
        body, name=name, grid=(A_HEADS // 2, 2, T // tq),
        in_specs=[qspec, kspec, kspec, qspec], out_specs=[qspec, kspec, kspec],
        out_shape=[jax.ShapeDtypeStruct((T, 512), f32), jax.ShapeDtypeStruct((T, 256), f32),
                   jax.ShapeDtypeStruct((T, 256), f32)],
        compiler_params=_cparams(("parallel", "arbitrary", "arbitrary")),
    )(qn, kn, vb, dya)


def _conv_rows(tc, tl):
    return CONV_PAD + tc + CONV_PAD + tl + CONV_PAD


def _fill_pad(pad_ref, val, tc, tl):
    z = jnp.zeros((CONV_PAD, LANE), f32)
    pad_ref[0:CONV_PAD, :] = z
    pad_ref[CONV_PAD:CONV_PAD + tc, :] = val[0:tc]
    pad_ref[CONV_PAD + tc:2 * CONV_PAD + tc, :] = z
    pad_ref[2 * CONV_PAD + tc:2 * CONV_PAD + tc + tl, :] = val[tc:tc + tl]
    pad_ref[2 * CONV_PAD + tc + tl:3 * CONV_PAD + tc + tl, :] = z


def _conv_apply(pad_ref, w_ref, K, tc, tl, rc, emit, flip=False):
    half = K // 2
    for seg0, off, n in ((0, CONV_PAD, tc), (tc, 2 * CONV_PAD + tc, tl)):
        for r0 in range(0, n, rc):
            acc = None
            for k in range(K):
                sh = (half - k) if flip else (k - half)
                term = pad_ref[pl.ds(off + r0 + sh, rc), :] * w_ref[k:k + 1, :]
                acc = term if acc is None else acc + term
            emit(seg0 + r0, acc)


def _conv_wgrad(pad_ref, dy_ref, K, tc, tl, rc, dw_ref):
    half = K // 2
    for k in range(K):
        acc = jnp.zeros((1, LANE), f32)
        for seg0, off, n in ((0, CONV_PAD, tc), (tc, 2 * CONV_PAD + tc, tl)):
            for r0 in range(0, n, rc):
                acc = acc + jnp.sum(pad_ref[pl.ds(off + r0 + k - half, rc), :] * dy_ref[pl.ds(seg0 + r0, rc), :],
                                    axis=0, keepdims=True)
        dw_ref[k:k + 1, :] = acc


def _col(T, off):
    return pl.BlockSpec((T, LANE), lambda j: (0, off // LANE + j))


def _ctile(T):
    return pl.BlockSpec((T, CT), lambda j: (0, j))


C_B, C_C, C_X, C_A, C_G = (slice(LANE * i, LANE * (i + 1)) for i in range(5))


def _conv_fwd(P, wb, wd, bd, tc, tl, rc, name):
    T = tc + tl

    def body(p_ref, wb_ref, wd_ref, bd_ref, yb_ref, hh_ref, pad_ref):
        _fill_pad(pad_ref, p_ref[:, C_C] * p_ref[:, C_X], tc, tl)

        def emit_b(r0, y):
            yb_ref[pl.ds(r0, rc), :] = y * p_ref[pl.ds(r0, rc), C_B]

        _conv_apply(pad_ref, wb_ref, KB, tc, tl, rc, emit_b)
        _fill_pad(pad_ref, p_ref[:, C_A] * jax.nn.sigmoid(p_ref[:, C_G]), tc, tl)

        def emit_d(r0, y):
            hh_ref[pl.ds(r0, rc), :] = y + bd_ref[...]

        _conv_apply(pad_ref, wd_ref, KD, tc, tl, rc, emit_d)

    return pl.pallas_call(
        body, name=name, grid=(BRW // LANE,),
        in_specs=[_ctile(T), pl.BlockSpec((KB, LANE), lambda j: (0, j)), pl.BlockSpec((KD, LANE), lambda j: (0, j)),
                  pl.BlockSpec((1, LANE), lambda j: (0, j))],
        out_specs=[_col(T, 0), _col(T, 0)],
        out_shape=[jax.ShapeDtypeStruct((T, BRW), f32), jax.ShapeDtypeStruct((T, BRW), f32)],
        scratch_shapes=[pltpu.VMEM((_conv_rows(tc, tl), LANE), f32)],
        compiler_params=_cparams(("parallel",)),
    )(P, wb, wd, bd)


def _conv_bwd(P, dyb, dhh, wb, wd, tc, tl, rc, name):
    T = tc + tl

    def body(p_ref, dyb_ref, dhh_ref, wb_ref, wd_ref, dp_ref, dwb_ref, dwd_ref, dbd_ref, pad_ref, pad2_ref, tmp_ref):
        _fill_pad(pad_ref, p_ref[:, C_C] * p_ref[:, C_X], tc, tl)

        def emit_cv(r0, y):
            dp_ref[pl.ds(r0, rc), C_B] = y * dyb_ref[pl.ds(r0, rc), :]

        _conv_apply(pad_ref, wb_ref, KB, tc, tl, rc, emit_cv)
        tmp_ref[...] = dyb_ref[...] * p_ref[:, C_B]
        _conv_wgrad(pad_ref, tmp_ref, KB, tc, tl, rc, dwb_ref)
        _fill_pad(pad2_ref, tmp_ref[...], tc, tl)

        def emit_ds(r0, y):
            dp_ref[pl.ds(r0, rc), C_C] = y * p_ref[pl.ds(r0, rc), C_X]
            dp_ref[pl.ds(r0, rc), C_X] = y * p_ref[pl.ds(r0, rc), C_C]

        _conv_apply(pad2_ref, wb_ref, KB, tc, tl, rc, emit_ds, flip=True)
        _fill_pad(pad_ref, p_ref[:, C_A] * jax.nn.sigmoid(p_ref[:, C_G]), tc, tl)
        _conv_wgrad(pad_ref, dhh_ref, KD, tc, tl, rc, dwd_ref)
        dbd_ref[...] = jnp.sum(dhh_ref[...], axis=0, keepdims=True)
        _fill_pad(pad2_ref, dhh_ref[...], tc, tl)

        def emit_d2(r0, y):
            sg = jax.nn.sigmoid(p_ref[pl.ds(r0, rc), C_G])
            a = p_ref[pl.ds(r0, rc), C_A]
            dp_ref[pl.ds(r0, rc), C_A] = y * sg
            dp_ref[pl.ds(r0, rc), C_G] = y * a * sg * (1.0 - sg)

        _conv_apply(pad2_ref, wd_ref, KD, tc, tl, rc, emit_d2, flip=True)

    return pl.pallas_call(
        body, name=name, grid=(BRW // LANE,),
        in_specs=[_ctile(T), _col(T, 0), _col(T, 0),
                  pl.BlockSpec((KB, LANE), lambda j: (0, j)), pl.BlockSpec((KD, LANE), lambda j: (0, j))],
        out_specs=[_ctile(T), pl.BlockSpec((KB, LANE), lambda j: (0, j)), pl.BlockSpec((KD, LANE), lambda j: (0, j)),
                   pl.BlockSpec((1, LANE), lambda j: (0, j))],
        out_shape=[jax.ShapeDtypeStruct((T, W_C), f32), jax.ShapeDtypeStruct((KB, BRW), f32),
                   jax.ShapeDtypeStruct((KD, BRW), f32), jax.ShapeDtypeStruct((1, BRW), f32)],
        scratch_shapes=[pltpu.VMEM((_conv_rows(tc, tl), LANE), f32), pltpu.VMEM((_conv_rows(tc, tl), LANE), f32),
                        pltpu.VMEM((T, LANE), f32)],
        compiler_params=_cparams(("parallel",)),
    )(P, dyb, dhh, wb, wd)


def _gla_chunk(q, k, v, r, w2, b2, st, isfwd):
    z = mm(r, w2) + b2
    g = jax.nn.log_sigmoid(jnp.where(isfwd, z[:, 0:C_KW], z[:, C_KW:2 * C_KW])) / C_TAU
    ri = lax.broadcasted_iota(jnp.int32, (CH, CH), 0)
    ci = lax.broadcasted_iota(jnp.int32, (CH, CH), 1)
    keep = jnp.where(isfwd, ri - ci, ci - ri) >= 0
    tri = keep.astype(f32)
    cum = jnp.dot(tri, g, preferred_element_type=f32, precision=lax.Precision.HIGHEST)
    last = jnp.sum(g, axis=0, keepdims=True)
    q = q * (C_KW // C_HEADS) ** -0.5
    hv = lax.broadcasted_iota(jnp.int32, (BRW, C_KW), 0) // (BRW // C_HEADS)
    hk = lax.broadcasted_iota(jnp.int32, (BRW, C_KW), 1) // (C_KW // C_HEADS)
    st_new = st * jnp.exp(last) + jnp.where(hv == hk, mm_tn(v, k * jnp.exp(last - cum)), 0.0)
    o = mm_nt(q * jnp.exp(cum), st)
    rowi = lax.broadcasted_iota(jnp.int32, (CH, C_KW), 0)
    lane_head = lax.broadcasted_iota(jnp.int32, (CH, C_KW), 1) // (C_KW // C_HEADS)
    scores = [jnp.zeros((CH, CH), f32) for _ in range(C_HEADS)]
    for a in range(CH // GLA_SUB):
        idx = jnp.where(isfwd, GLA_SUB * a - 1, GLA_SUB * (a + 1))
        ref = jnp.sum(jnp.where(rowi == idx, cum, 0.0), axis=0, keepdims=True)
        qa = q * jnp.exp(jnp.minimum(cum - ref, 0.0))
        ka = k * jnp.exp(jnp.minimum(ref - cum, GLA_CLAMP))
        in_block = (ri // GLA_SUB == a) & keep
        for hd in range(C_HEADS):
            s = mm_nt(jnp.where(lane_head == hd, qa, 0.0), ka)
            scores[hd] = scores[hd] + jnp.where(in_block, s, 0.0)
    vw = BRW // C_HEADS
    o = o + jnp.concatenate([mm(scores[hd], v[:, vw * hd:vw * (hd + 1)]) for hd in range(C_HEADS)], axis=1)
    return o, st_new


def _gla_chunk_of(d, n, nc, nch):
    back = jnp.where(n < nc, nc - 1 - n, nch - 1 - (n - nc))
    return jnp.where(d == 0, n, back)


def _gla_fwd(P, w2, b2, tc, name):
    T = P.shape[0]
    nch, nc = T // CH, tc // CH

    def body(p_ref, w_ref, b_ref, o_ref, ss_ref, st_ref):
        @pl.when(pl.program_id(1) == 0)
        def _():
            st_ref[...] = jnp.zeros_like(st_ref)

        st = st_ref[...]
        ss_ref[...] = st
        o, st_new = _gla_chunk(p_ref[:, 0:G_K], p_ref[:, G_K:G_V], p_ref[:, G_V:G_R], p_ref[:, G_R:W_G], w_ref[...],
                               b_ref[...], st, pl.program_id(0) == 0)
        o_ref[...] = o
        st_ref[...] = st_new

    return pl.pallas_call(
        body, name=name, grid=(2, nch),
        in_specs=[pl.BlockSpec((CH, W_G), lambda d, n: (_gla_chunk_of(d, n, nc, nch), 0)),
                  pl.BlockSpec((LANE, 512), lambda d, n: (0, 0)), pl.BlockSpec((1, 512), lambda d, n: (0, 0))],
        out_specs=[pl.BlockSpec((None, CH, BRW), lambda d, n: (d, _gla_chunk_of(d, n, nc, nch), 0)),
                   pl.BlockSpec((None, None, BRW, C_KW), lambda d, n: (d, n, 0, 0))],
        out_shape=[jax.ShapeDtypeStruct((2, T, BRW), f32), jax.ShapeDtypeStruct((2, nch, BRW, C_KW), f32)],
        scratch_shapes=[pltpu.VMEM((BRW, C_KW), f32)],
        compiler_params=_cparams(("parallel", "arbitrary")),
    )(P, w2, b2)


def _gla_bwd(P, w2, b2, ssave, doc, tc, name):
    T = P.shape[0]
    nch, nc = T // CH, tc // CH

    def chunk(d, m):
        return _gla_chunk_of(d, nch - 1 - m, nc, nch)

    def body(p_ref, w_ref, b_ref, ss_ref, g_ref, dp_ref, dw_ref, db_ref, dst_ref):
        m = pl.program_id(1)
        isfwd = pl.program_id(0) == 0

        @pl.when(m == 0)
        def _():
            dst_ref[...] = jnp.zeros_like(dst_ref)

        _, vjp = jax.vjp(lambda q, k, v, r, w, b, st: _gla_chunk(q, k, v, r, w, b, st, isfwd),
                         p_ref[:, 0:G_K], p_ref[:, G_K:G_V], p_ref[:, G_V:G_R], p_ref[:, G_R:W_G], w_ref[...], b_ref[...],
                         ss_ref[...])
        dq, dk, dv, dr, dw, db, dst = vjp((g_ref[...], dst_ref[...]))
        dp_ref[:, 0:G_K] = dq
        dp_ref[:, G_K:G_V] = dk
        dp_ref[:, G_V:G_R] = dv
        dp_ref[:, G_R:W_G] = dr
        dst_ref[...] = dst

        @pl.when(m == 0)
        def _():
            dw_ref[...] = dw
            _partial_rows(db_ref, [db])

        @pl.when(m > 0)
        def _():
            dw_ref[...] += dw
            db_ref[0:1, :] += db

    return pl.pallas_call(
        body, name=name, grid=(2, nch),
        in_specs=[pl.BlockSpec((CH, W_G), lambda d, m: (chunk(d, m), 0)),
                  pl.BlockSpec((LANE, 512), lambda d, m: (0, 0)), pl.BlockSpec((1, 512), lambda d, m: (0, 0)),
                  pl.BlockSpec((None, None, BRW, C_KW), lambda d, m: (d, nch - 1 - m, 0, 0)),
                  pl.BlockSpec((CH, BRW), lambda d, m: (chunk(d, m), 0))],
        out_specs=[pl.BlockSpec((None, CH, W_G), lambda d, m: (d, chunk(d, m), 0)),
                   pl.BlockSpec((None, LANE, 512), lambda d, m: (d, 0, 0)),
                   pl.BlockSpec((None, SUB, 512), lambda d, m: (d, 0, 0))],
        out_shape=[jax.ShapeDtypeStruct((2, T, W_G), f32),
                   jax.ShapeDtypeStruct((2, LANE, 512), f32), jax.ShapeDtypeStruct((2, SUB, 512), f32)],
        scratch_shapes=[pltpu.VMEM((BRW, C_KW), f32)],
        compiler_params=_cparams(("parallel", "arbitrary")),
    )(P, w2, b2, ssave, doc)


def _sum_dirs(a, tm, name):
    _, T, W = a.shape

    def body(a_ref, o_ref):
        o_ref[...] = a_ref[0] + a_ref[1]

    return pl.pallas_call(
        body, name=name, grid=(T // tm,),
        in_specs=[pl.BlockSpec((2, tm, W), lambda i: (0, i, 0))], out_specs=pl.BlockSpec((tm, W), lambda i: (i, 0)),
        out_shape=jax.ShapeDtypeStruct((T, W), f32),
        compiler_params=_cparams(("parallel",)),
    )(a)


def _merge_fn(h, m_l, m_c, isctx, ya, ga, yb, gb, of, ob, gc, hh, gd, mg, es, ey, cn, dng, dnb, lg, lb, wbr, wout):
    oc = of + ob
    yc = jnp.concatenate([_rms(oc[:, HD * i:HD * (i + 1)], cn[:, HD * i:HD * (i + 1)]) for i in range(C_HEADS)], 1)
    brs = [ya * _silu(ga), yb * _silu(gb), yc * _silu(gc), _silu(_ln(hh) * dng + dnb) * _silu(gd)]
    acc = None
    for i in range(4):
        t = jax.nn.sigmoid(mg[:, D * i:D * (i + 1)]) * (mm(brs[i], wbr[i]) + es[i])
        acc = t if acc is None else acc + t
    y = mm(acc, wout) + ey
    gate = jnp.where(isctx, m_c[:, 2 * D:3 * D], m_l[:, 2 * D:3 * D])
    hn = _ln(ALPHA * h + gate * y) * lg + lb
    return hn, (brs, acc)


def _merge_specs(tm):
    t = lambda w, off=0: _tok(tm, w, off)
    return [t(D), pl.BlockSpec((SUB, 3 * D), lambda i: (0, 0)),
            t(BRW), t(BRW, M_GA), t(BRW), t(BRW, M_GB),
            pl.BlockSpec((None, tm, BRW), lambda i: (0, i, 0)), pl.BlockSpec((None, tm, BRW), lambda i: (1, i, 0)),
            t(BRW, M_GC), t(BRW), t(BRW, M_GD), t(4 * D, 0),
            _vec(BRW), _vec(BRW), _vec(BRW), _vec(D), _vec(D),
            pl.BlockSpec((4, BRW, D), lambda i: (0, 0, 0)), pl.BlockSpec((D, D), lambda i: (0, 0))]


def _merge_fwd(h, modv_l, ya, yb, o2, hh, P, cn, dng, dnb, lg, lb, wbr, wout, tc, tm, name):
    T = h.shape[0]

    def body(h_ref, m_ref, ya_ref, ga_ref, yb_ref, gb_ref, of_ref, ob_ref, gc_ref, hh_ref, gd_ref, mg_ref,
             cn_ref, dng_ref, dnb_ref, lg_ref, lb_ref, wbr_ref, wout_ref, o_ref):
        isctx = _row_ids(pl.program_id(0), tm) < tc
        zero = jnp.zeros((tm, D), f32)
        hn, _ = _merge_fn(h_ref[...], m_ref[0:1, :], m_ref[1:2, :], isctx, ya_ref[...], ga_ref[...], yb_ref[...],
                          gb_ref[...], of_ref[...], ob_ref[...], gc_ref[...], hh_ref[...], gd_ref[...], mg_ref[...],
                          [zero] * 4, zero, cn_ref[...], dng_ref[...], dnb_ref[...], lg_ref[...], lb_ref[...],
                          [wbr_ref[i] for i in range(4)], wout_ref[...])
        o_ref[...] = hn

    return pl.pallas_call(
        body, name=name, grid=(T // tm,),
        in_specs=_merge_specs(tm), out_specs=_tok(tm, D, 0),
        out_shape=jax.ShapeDtypeStruct((T, D), f32),
        compiler_params=_cparams(("parallel",)),
    )(h, modv_l, ya, P, yb, P, o2, o2, P, hh, P, P, cn, dng, dnb, lg, lb, wbr, wout)


def _merge_bwd(dhn, h, modv_l, ya, yb, o2, hh, P, cn, dng, dnb, lg, lb, wbr, wout, tc, tm, name):
    T = h.shape[0]
    nt = T // tm

    def body(g_ref, h_ref, m_ref, ya_ref, ga_ref, yb_ref, gb_ref, of_ref, ob_ref, gc_ref, hh_ref, gd_ref, mg_ref,
             cn_ref, dng_ref, dnb_ref, lg_ref, lb_ref, wbr_ref, wout_ref,
             dh_ref, dm_ref, dya_ref, dyb_ref, doc_ref, dhh_ref, dp_ref,
             br_ref, z_ref, acc_ref, dy_ref, dv5_ref, dvd_ref):
        isctx = _row_ids(pl.program_id(0), tm) < tc
        zero = jnp.zeros((tm, D), f32)
        wbr_v = [wbr_ref[i] for i in range(4)]
        wout_v = wout_ref[...]

        def fn(h, ml, mc, ya, ga, yb, gb, oc, gc, hh, gd, mg, e0, e1, e2, e3, ey, cn, dng, dnb, lg, lb):
            return _merge_fn(h, ml, mc, isctx, ya, ga, yb, gb, oc, jnp.zeros_like(oc), gc, hh, gd, mg,
                             [e0, e1, e2, e3], ey, cn, dng, dnb, lg, lb, wbr_v, wout_v)

        _, vjp, (brs, acc) = jax.vjp(
            fn, h_ref[...], m_ref[0:1, :], m_ref[1:2, :], ya_ref[...], ga_ref[...], yb_ref[...], gb_ref[...],
            of_ref[...] + ob_ref[...], gc_ref[...], hh_ref[...], gd_ref[...], mg_ref[...], zero, zero, zero, zero, zero,
            cn_ref[...], dng_ref[...], dnb_ref[...], lg_ref[...], lb_ref[...], has_aux=True)
        (dh, dml, dmc, dya, dga, dyb, dgb, doc, dgc, dhh, dgd, dmg, z0, z1, z2, z3, dy,
         dcn, ddng, ddnb, dlg, dlb) = vjp(g_ref[...])
        dh_ref[...] = dh
        _partial_rows(dm_ref, [dml, dmc])
        dya_ref[...] = dya
        dyb_ref[...] = dyb
        doc_ref[...] = doc
        dhh_ref[...] = dhh
        dp_ref[:, 0:M_GA] = dmg
        dp_ref[:, M_GA:M_GB] = dga
        dp_ref[:, M_GB:M_GC] = dgb
        dp_ref[:, M_GC:M_GD] = dgc
        dp_ref[:, M_GD:W_M] = dgd
        for i, z in enumerate((z0, z1, z2, z3)):
            br_ref[i] = brs[i].astype(bf16)
            z_ref[i] = z.astype(bf16)
        acc_ref[...] = acc.astype(bf16)
        dy_ref[...] = dy.astype(bf16)
        _partial_rows(dv5_ref, [dcn, ddng, ddnb])
        _partial_rows(dvd_ref, [dlg, dlb])

    t = lambda w: _tok(tm, w, 0)
    part = lambda w: pl.BlockSpec((None, SUB, w), lambda i: (i, 0, 0))
    sd = jax.ShapeDtypeStruct
    return pl.pallas_call(
        body, name=name, grid=(nt,),
        in_specs=[t(D)] + _merge_specs(tm),
        out_specs=[t(D), part(3 * D)] + [t(BRW)] * 4 + [t(W_M),
                   pl.BlockSpec((4, tm, BRW), lambda i: (0, i, 0)), pl.BlockSpec((4, tm, D), lambda i: (0, i, 0)),
                   t(D), t(D), part(BRW), part(D)],
        out_shape=[sd((T, D), f32), sd((nt, SUB, 3 * D), f32)] + [sd((T, BRW), f32)] * 4 + [sd((T, W_M), f32),
                   sd((4, T, BRW), bf16), sd((4, T, D), bf16), sd((T, D), bf16), sd((T, D), bf16),
                   sd((nt, SUB, BRW), f32), sd((nt, SUB, D), f32)],
        compiler_params=_cparams(("parallel",)),
    )(dhn, h, modv_l, ya, P, yb, P, o2, o2, P, hh, P, P, cn, dng, dnb, lg, lb, wbr, wout)


def _loss_kernel(h, tgt, tc, tm, name):
    T = h.shape[0]
    nt = T // tm
    nct = tc // tm

    def body(h_ref, t_ref, d_ref, l_ref):
        i = pl.program_id(0)
        err = h_ref[...] - t_ref[...]
        lat = (i >= nct).astype(f32)
        d_ref[...] = err * (lat / D)
        l_ref[...] = jnp.zeros((SUB, LANE), f32) + lat * 0.5 * jnp.sum(err * err) / D

    return pl.pallas_call(
        body, name=name, grid=(nt,),
        in_specs=[pl.BlockSpec((tm, D), lambda i: (i, 0)),
                  pl.BlockSpec((tm, D), lambda i: (jnp.maximum(i - nct, 0), 0))],
        out_specs=[pl.BlockSpec((tm, D), lambda i: (i, 0)), pl.BlockSpec((None, SUB, LANE), lambda i: (i, 0, 0))],
        out_shape=[jax.ShapeDtypeStruct((T, D), f32), jax.ShapeDtypeStruct((nt, SUB, LANE), f32)],
        compiler_params=_cparams(("parallel",)),
    )(h, tgt)


def _rope_tables(tc, tl):
    t = jnp.arange(tl)
    inv = ROPE_THETA ** (-jnp.arange(0, HD // 2, 2, dtype=f32) / (HD // 2))
    ang = jnp.concatenate([(t // GRID_W).astype(f32)[:, None] * inv, (t % GRID_W).astype(f32)[:, None] * inv], -1)
    cos, sin = jnp.repeat(jnp.cos(ang), 2, axis=1), jnp.repeat(jnp.sin(ang), 2, axis=1)
    even = (jnp.arange(HD) % 2 == 0)[None, :]
    cos_f = jnp.concatenate([jnp.ones((tc, HD), f32), cos], 0)
    sin_a = jnp.concatenate([jnp.zeros((tc, HD), f32), jnp.where(even, -sin, 0.0)], 0)
    sin_b = jnp.concatenate([jnp.zeros((tc, HD), f32), jnp.where(even, 0.0, sin)], 0)
    return cos_f, sin_a, sin_b


N_CHIPS = 4
SHARD = N_IN // N_CHIPS


def _group_ranges():
    conv = [(s0 + LANE * j, LANE) for j in range(BRW // LANE) for s0 in (S_B, S_C, S_X, S_DA, S_DG)]
    return dict(M=[(S_MG, 4 * D), (S_GA, BRW), (S_GB, BRW), (S_GC, BRW), (S_GD, BRW)], A=[(S_Q, W_A)], C=conv,
                G=[(S_CQ, 2 * C_KW), (S_CV, BRW), (S_R, 2 * C_RANK)])


def _group_weights(w4):
    out = {}
    for k, ranges in _group_ranges().items():
        parts = []
        for a, n in ranges:
            while n > 0:
                s, r = divmod(a, SHARD)
                m = min(n, SHARD - r)
                parts.append(w4[s, r:r + m])
                a, n = a + m, n - m
        if k == "G":
            parts.append(jnp.zeros((LANE - 2 * C_RANK, D), w4.dtype))
        out[k] = jnp.concatenate(parts, 0)
    return out


def _ungroup(g):
    secs = []
    for k, ranges in _group_ranges().items():
        off = 0
        for a, n in ranges:
            secs.append((a, g[k][off:off + n]))
            off += n
    return jnp.concatenate([v for _, v in sorted(secs, key=lambda t: t[0])], 0)


PROJ_TN = dict(M=2048, A=1024, C=1280, G=1152)
DWP_TN = dict(M=768, A=1024, C=640, G=1152)


def _gate_weights(w2_l, gb_l):
    w = jnp.zeros((LANE, 2 * C_KW), f32)
    w = w.at[0:C_RANK, 0:C_KW].set(w2_l[0]).at[C_RANK:2 * C_RANK, C_KW:2 * C_KW].set(w2_l[1])
    return w, jnp.concatenate([gb_l[0], gb_l[1]])[None, :]


def _local_step(x1, c1, ctx1, tgt1, c_ctx, w_mod, b_mod, wp, q_norm, k_norm, b_conv, w2, gb, c_norm, d_conv_w,
                d_conv_b, d_norm_g, d_norm_b, w_br, w_out, ln_g, ln_b, tm):
    tc, tl = ctx1.shape[0], x1.shape[0]
    T = tc + tl
    rc = min(256, tc)
    tmb = tm // 2
    tmm = 768 if T % 768 == 0 else tm
    rope = _rope_tables(tc, tl)
    cin = jnp.concatenate([c1, c_ctx[None, :], jnp.zeros((SUB - 2, D), f32)], 0)
    modv = _mod_fwd(cin, w_mod, b_mod)
    row = lambda v: v[None, :]

    h = jnp.concatenate([ctx1, x1], 0)
    saved = []
    for l in range(DEPTH):
        u = _ln_fwd(h, modv[l], tc, tm, f"ln_fwd{l}")
        P = {k: _matmul(u, wp[l][k], "nt", tmm, PROJ_TN[k], D, f"proj{l}{k}") for k in GROUPS}
        qn, kn, vb = _prep_fwd(P["A"], row(q_norm[l]), row(k_norm[l]), rope, tm, f"prep_fwd{l}")
        ya = _attn_fwd(qn, kn, vb, tc, tm, f"attn_fwd{l}")
        yb, hh = _conv_fwd(P["C"], b_conv[l], d_conv_w[l], row(d_conv_b[l]), tc, tl, rc, f"conv_fwd{l}")
        w2p, b2p = _gate_weights(w2[l], gb[l])
        o2, ssave = _gla_fwd(P["G"], w2p, b2p, tc, f"gla_fwd{l}")
        hn = _merge_fwd(h, modv[l], ya, yb, o2, hh, P["M"], row(c_norm[l]), row(d_norm_g[l]), row(d_norm_b[l]),
                        row(ln_g[l]), row(ln_b[l]), w_br[l], w_out[l], tc, tm, f"merge_fwd{l}")
        saved.append((h, u, P, qn, kn, vb, ya, yb, hh, o2, ssave, w2p, b2p))
        h = hn

    dh, lparts = _loss_kernel(h, tgt1, tc, tm, "loss")
    loss = jnp.sum(lparts[:, 0, 0])

    g = {k: [None] * DEPTH for k in ("wp", "q_norm", "k_norm", "b_conv", "w2", "gb", "c_norm", "d_conv_w", "d_conv_b",
                                     "d_norm_g", "d_norm_b", "w_br", "w_out", "ln_g", "ln_b", "modv")}
    for l in reversed(range(DEPTH)):
        h_in, u, P, qn, kn, vb, ya, yb, hh, o2, ssave, w2p, b2p = saved[l]
        dP = {}
        (dh_res, dm_mg, dya, dyb, doc, dhh, dP["M"], br, z, acc, dy, dv5, dvd) = _merge_bwd(
            dh, h_in, modv[l], ya, yb, o2, hh, P["M"], row(c_norm[l]), row(d_norm_g[l]), row(d_norm_b[l]),
            row(ln_g[l]), row(ln_b[l]), w_br[l], w_out[l], tc, tmb, f"merge_bwd{l}")
        g["w_br"][l] = _matmul_tn_batched(br, z, N_CHIPS, f"dwbr{l}")
        g["w_out"][l] = _matmul(acc, dy, "tn", D, D, T, f"dwout{l}")
        v5 = jnp.sum(dv5, 0)
        g["c_norm"][l], g["d_norm_g"][l], g["d_norm_b"][l] = v5[0], v5[1], v5[2]
        vd = jnp.sum(dvd, 0)
        g["ln_g"][l], g["ln_b"][l] = vd[0], vd[1]
        dqn, dkn, dv = _attn_bwd(qn, kn, vb, dya, tc, tm, f"attn_bwd{l}")
        dP["A"], dqk = _prep_bwd(P["A"], dqn, dkn, dv, row(q_norm[l]), row(k_norm[l]), rope, tm, f"prep_bwd{l}")
        dqk = jnp.sum(dqk, 0)
        g["q_norm"][l], g["k_norm"][l] = dqk[0], dqk[1]
        dP["C"], dwb, dwd, dbd = _conv_bwd(P["C"], dyb, dhh, b_conv[l], d_conv_w[l], tc, tl, rc, f"conv_bwd{l}")
        g["b_conv"][l], g["d_conv_w"][l], g["d_conv_b"][l] = dwb, dwd, dbd[0]
        dpg2, dw2p, db2p = _gla_bwd(P["G"], w2p, b2p, ssave, doc, tc, f"gla_bwd{l}")
        dP["G"] = _sum_dirs(dpg2, tm, f"gla_sum{l}")
        dw2p = dw2p[0] + dw2p[1]
        db2p = db2p[0, 0] + db2p[1, 0]
        g["w2"][l] = jnp.stack([dw2p[0:C_RANK, 0:C_KW], dw2p[C_RANK:2 * C_RANK, C_KW:2 * C_KW]])
        g["gb"][l] = jnp.stack([db2p[0:C_KW], db2p[C_KW:2 * C_KW]])
        du = None
        for k in GROUPS:
            du = _matmul(dP[k], wp[l][k], "nn", tmm, D, PROJ_TN[k], f"du{l}{k}", add=du)
        g["wp"][l] = {k: _matmul(dP[k], u, "tn", DWP_TN[k], D, T, f"dwp{l}{k}") for k in GROUPS}
        dh, dm_ln = _ln_bwd(du, h_in, dh_res, modv[l], tc, tm, f"ln_bwd{l}")
        g["modv"][l] = jnp.sum(dm_mg, 0) + jnp.sum(dm_ln, 0)

    dmodv = jnp.stack(g.pop("modv"))
    g["w_mod"], dcin = _mod_bwd(cin, w_mod, dmodv)
    g["b_mod"] = dmodv[:, 0, :] + dmodv[:, 1, :]
    g["c_ctx"] = jnp.sum(dcin, (0, 1))[1]
    return loss, dh[tc:], g


HALF_TL = 256


def _adamw(w, g, m, v, name, tr=128):
    L, R, C = w.shape
    if R % tr == 0:
        grid, spec = (L, R // tr), pl.BlockSpec((None, tr, C), lambda l, i: (l, i, 0))
    elif R * C * 4 <= (1 << 20):
        grid, spec = (L, 1), pl.BlockSpec((None, R, C), lambda l, i: (l, 0, 0))
    else:
        grid, spec = (L, C // HALF_TL), pl.BlockSpec((None, R, HALF_TL), lambda l, i: (l, 0, i))

    def body(w_ref, g_ref, m_ref, v_ref, d_ref, nm_ref, nv_ref):
        gg = g_ref[...]
        nm = B1 * m_ref[...] + (1.0 - B1) * gg
        nv = B2 * v_ref[...] + (1.0 - B2) * (gg * gg)
        m_hat = nm / (1.0 - B1 ** STEP)
        v_hat = nv / (1.0 - B2 ** STEP)
        d_ref[...] = -LR * (m_hat / (jnp.sqrt(v_hat) + AEPS) + WD * w_ref[...])
        nm_ref[...] = nm
        nv_ref[...] = nv

    return pl.pallas_call(
        body, name=name, grid=grid, in_specs=[spec] * 4, out_specs=[spec] * 3,
        out_shape=[jax.ShapeDtypeStruct((L, R, C), f32)] * 3,
        compiler_params=_cparams(("parallel", "parallel")),
    )(w, g, m, v)


MESH = pl.DeviceIdType.MESH
ANY = pl.BlockSpec(memory_space=pl.ANY)
N_CHIPS = 4


def _place():
    x, y, c = lax.axis_index("x"), lax.axis_index("y"), lax.axis_index("c")
    chips = [(1 - x, y), (x, 1 - y), (1 - x, 1 - y)]
    return x, y, c, chips


def _half(ref, c, axis):
    n = ref.shape[axis] // 2
    last = axis in (-1, ref.ndim - 1)
    idx = [slice(None)] * ref.ndim
    idx[axis] = pl.ds(pl.multiple_of(c * n, LANE if last else SUB), n)
    return ref.at[tuple(idx)]


def _half_shape(shape, axis):
    s = list(shape)
    s[axis] //= 2
    return tuple(s)


def _all_gather(arrs, axes, name):
    n = len(arrs)

    def body(*refs):
        ins, outs = refs[:n], refs[n:2 * n]
        send, recv = refs[2 * n:]
        x, y, c, chips = _place()
        me, sib = 2 * x + y, (x, y, 1 - c)

        def copy(a, k, chip_idx, cc, to, src=None):
            blk = _half(outs[a].at[chip_idx], cc, axes[a])
            return pltpu.make_async_remote_copy(src_ref=blk if src is None else src, dst_ref=blk,
                                                send_sem=send.at[7 * a + k], recv_sem=recv.at[7 * a + k],
                                                device_id=to, device_id_type=MESH)

        own = [pltpu.make_async_remote_copy(src_ref=ins[a], dst_ref=outs[a].at[me], send_sem=send.at[7 * a + 6],
                                            recv_sem=recv.at[7 * a + 6], device_id=sib, device_id_type=MESH)
               for a in range(n)]
        first = own + [copy(a, j, me, c, (*chip, c), src=_half(ins[a], c, axes[a]))
                       for a in range(n) for j, chip in enumerate(chips)]
        for cp in first:
            cp.start()
        passed = []
        for a in range(n):
            for j, chip in enumerate(chips):
                k = 2 * chip[0] + chip[1]
                copy(a, j, k, c, sib).wait_recv()
                fwd = copy(a, 3 + j, k, c, sib)
                fwd.start()
                passed.append(fwd)
        for a in range(n):
            own[a].wait_recv()
            for j, chip in enumerate(chips):
                copy(a, 3 + j, 2 * chip[0] + chip[1], 1 - c, sib).wait_recv()
        for cp in first + passed:
            cp.wait_send()

    return pl.pallas_call(
        body, name=name, in_specs=[ANY] * n, out_specs=[ANY] * n,
        out_shape=[jax.ShapeDtypeStruct((N_CHIPS,) + a.shape, a.dtype) for a in arrs],
        scratch_shapes=[pltpu.SemaphoreType.DMA((7 * n,)), pltpu.SemaphoreType.DMA((7 * n,))],
    )(*arrs)


def _sibling_halves(arrs, axes, name):
    n = len(arrs)

    def body(*refs):
        ins, outs = refs[:n], refs[n:2 * n]
        send, recv = refs[2 * n:]
        x, y, c, _ = _place()
        cps = [pltpu.make_async_remote_copy(src_ref=_half(ins[a], 1 - c, axes[a] + 1), dst_ref=outs[a], send_sem=send.at[a],
                                            recv_sem=recv.at[a], device_id=(x, y, 1 - c), device_id_type=MESH)
               for a in range(n)]
        for cp in cps:
            cp.start()
        for cp in cps:
            cp.wait()

    return pl.pallas_call(
        body, name=name, in_specs=[ANY] * n, out_specs=[ANY] * n,
        out_shape=[jax.ShapeDtypeStruct(_half_shape(a.shape, axes[i] + 1), a.dtype) for i, a in enumerate(arrs)],
        scratch_shapes=[pltpu.SemaphoreType.DMA((n,)), pltpu.SemaphoreType.DMA((n,))],
    )(*arrs)


def _add_half(gfull, land, cidx, axis, name, tr=128, out_dtype=bf16):
    _, hr, hc = land.shape
    if axis == 0:
        tr = min(tr, hr)
        nb, blk = hr // tr, (None, tr, hc)
        g_spec = pl.BlockSpec(blk, lambda s, i, cr: (s, cr[0] * nb + i, 0))
        l_spec = pl.BlockSpec(blk, lambda s, i, cr: (s, i, 0))
    else:
        nb, blk = hc // HALF_TL, (None, hr, HALF_TL)
        g_spec = pl.BlockSpec(blk, lambda s, i, cr: (s, 0, cr[0] * nb + i))
        l_spec = pl.BlockSpec(blk, lambda s, i, cr: (s, 0, i))

    def body(c_ref, g_ref, l_ref, o_ref):
        o_ref[...] = (g_ref[...] + l_ref[...]).astype(o_ref.dtype)

    return pl.pallas_call(
        body, name=name,
        grid_spec=pltpu.PrefetchScalarGridSpec(
            num_scalar_prefetch=1, grid=(N_CHIPS, nb), in_specs=[g_spec, l_spec], out_specs=l_spec),
        out_shape=jax.ShapeDtypeStruct((N_CHIPS, hr, hc), out_dtype),
        compiler_params=_cparams(("parallel", "parallel")),
    )(cidx, gfull, land)


def _chip_exchange(arrs, name):
    n = len(arrs)

    def body(*refs):
        ins, outs = refs[:n], refs[n:2 * n]
        send, recv = refs[2 * n:]
        x, y, c, chips = _place()
        me = 2 * x + y
        cps = []
        for a in range(n):
            for j, chip in enumerate(chips):
                k = 2 * chip[0] + chip[1]
                cps.append((pltpu.make_async_remote_copy(
                    src_ref=ins[a].at[k], dst_ref=outs[a].at[me], send_sem=send.at[3 * a + j], recv_sem=recv.at[3 * a + j],
                    device_id=(*chip, c), device_id_type=MESH), a, j, k))
        for cp, *_ in cps:
            cp.start()
        for cp, a, j, k in cps:
            pltpu.make_async_remote_copy(src_ref=ins[a].at[k], dst_ref=outs[a].at[k], send_sem=send.at[3 * a + j],
                                         recv_sem=recv.at[3 * a + j], device_id=(x, y, c), device_id_type=MESH).wait_recv()
        for cp, *_ in cps:
            cp.wait_send()

    return pl.pallas_call(
        body, name=name, in_specs=[ANY] * n, out_specs=[ANY] * n,
        out_shape=[jax.ShapeDtypeStruct(a.shape, a.dtype) for a in arrs],
        scratch_shapes=[pltpu.SemaphoreType.DMA((3 * n,)), pltpu.SemaphoreType.DMA((3 * n,))],
    )(*arrs)


def _sum_chips(land, own, place, axis, layer, into, name, tr=128):
    _, hr, hc = land.shape
    fresh = not hasattr(into, "dtype")
    shape = tuple(into) if fresh else into.shape
    if axis == 0:
        tr = min(tr, hr)
        nb, blk = hr // tr, (tr, hc)
        l_map, m_map = (lambda i, p: (0, i, 0)), (lambda i, p: (p[0], i, 0))
        o_map = lambda i, p: (layer, p[1] * nb + i, 0)
    else:
        nb, blk = hc // HALF_TL, (hr, HALF_TL)
        l_map, m_map = (lambda i, p: (0, 0, i)), (lambda i, p: (p[0], 0, i))
        o_map = lambda i, p: (layer, 0, p[1] * nb + i)

    def body(p_ref, l_ref, o_ref, *rest):
        me = p_ref[0]
        mine = o_ref[...].astype(f32)
        acc = None
        for k in range(N_CHIPS):
            t = jnp.where(me == k, mine, l_ref[k].astype(f32))
            acc = t if acc is None else acc + t
        rest[-1][...] = acc

    return pl.pallas_call(
        body, name=name,
        grid_spec=pltpu.PrefetchScalarGridSpec(
            num_scalar_prefetch=1, grid=(nb,),
            in_specs=[pl.BlockSpec((N_CHIPS,) + blk, l_map), pl.BlockSpec((None,) + blk, m_map)] + ([] if fresh else [ANY]),
            out_specs=pl.BlockSpec((None,) + blk, o_map)),
        out_shape=jax.ShapeDtypeStruct(shape, f32),
        input_output_aliases={} if fresh else {3: 0},
        compiler_params=_cparams(("parallel",)),
    )(place, land, own, *([] if fresh else [into]))


def _sibling_fill(arrs, axes, name):
    n = len(arrs)

    def body(*refs):
        outs = refs[n:2 * n]
        send, recv = refs[2 * n:]
        x, y, c, _ = _place()
        cps = [pltpu.make_async_remote_copy(src_ref=_half(outs[a], c, axes[a] + 1), dst_ref=_half(outs[a], c, axes[a] + 1),
                                            send_sem=send.at[a], recv_sem=recv.at[a], device_id=(x, y, 1 - c),
                                            device_id_type=MESH) for a in range(n)]
        for cp in cps:
            cp.start()
        for a in range(n):
            blk = _half(outs[a], 1 - c, axes[a] + 1)
            pltpu.make_async_remote_copy(src_ref=blk, dst_ref=blk, send_sem=send.at[a], recv_sem=recv.at[a],
                                         device_id=(x, y, 1 - c), device_id_type=MESH).wait_recv()
        for cp in cps:
            cp.wait_send()

    return pl.pallas_call(
        body, name=name, in_specs=[ANY] * n, out_specs=[ANY] * n,
        out_shape=[jax.ShapeDtypeStruct(a.shape, a.dtype) for a in arrs],
        input_output_aliases={a: a for a in range(n)},
        scratch_shapes=[pltpu.SemaphoreType.DMA((n,)), pltpu.SemaphoreType.DMA((n,))],
    )(*arrs)


N_DEV = 8


def _all_reduce_small(v, name):
    R = v.shape[0]

    def body(v_ref, o_ref, land_ref, send, recv):
        x, y, c, _ = _place()
        me = 4 * x + 2 * y + c
        land_ref[me] = v_ref[...]
        cps = []
        for m in range(1, N_DEV):
            px, py, pc = [(1 - q) if (m >> s) & 1 else q for q, s in ((x, 2), (y, 1), (c, 0))]
            cps.append((pltpu.make_async_remote_copy(src_ref=v_ref, dst_ref=land_ref.at[me], send_sem=send.at[m - 1],
                                                     recv_sem=recv.at[m - 1], device_id=(px, py, pc), device_id_type=MESH),
                        4 * px + 2 * py + pc, m))
        for cp, *_ in cps:
            cp.start()
        for cp, peer, m in cps:
            pltpu.make_async_remote_copy(src_ref=v_ref, dst_ref=land_ref.at[peer], send_sem=send.at[m - 1],
                                         recv_sem=recv.at[m - 1], device_id=(x, y, c), device_id_type=MESH).wait_recv()
        for cp, *_ in cps:
            cp.wait_send()
        acc = land_ref[0]
        for k in range(1, N_DEV):
            acc = acc + land_ref[k]
        o_ref[...] = acc

    vm = pl.BlockSpec(memory_space=pltpu.VMEM)
    return pl.pallas_call(
        body, name=name, in_specs=[vm], out_specs=vm, out_shape=jax.ShapeDtypeStruct(v.shape, f32),
        scratch_shapes=[pltpu.VMEM((N_DEV, R, LANE), f32), pltpu.SemaphoreType.DMA((N_DEV - 1,)),
                        pltpu.SemaphoreType.DMA((N_DEV - 1,))],
        compiler_params=pltpu.CompilerParams(vmem_limit_bytes=VMEM_LIMIT),
    )(v)


def _pack_small(arrs, mult=2 * SUB):
    flat = jnp.concatenate([a.reshape(-1) for a in arrs])
    rows = -(-flat.shape[0] // (LANE * mult)) * mult
    return jnp.pad(flat, (0, rows * LANE - flat.shape[0])).reshape(rows, LANE)


def _unpack_small(vec, shapes):
    flat, out, o = vec.reshape(-1), [], 0
    for s in shapes:
        n = int(np.prod(s))
        out.append(flat[o:o + n].reshape(s))
        o += n
    return out


REPL_SMALL = ("c_ctx", "b_mod", "q_norm", "k_norm", "c_norm", "d_conv_b", "d_norm_g", "d_norm_b", "ln_g", "ln_b")
SHARD_SMALL = ("b_conv", "c_gate_w2", "c_gate_b", "d_conv_w")
BIG = ("w_mod", "w_in", "w_br", "w_out")
ORDER = ("c_ctx", "w_mod", "b_mod", "w_in", "q_norm", "k_norm", "b_conv", "c_gate_w2", "c_gate_b", "c_norm", "d_conv_w",
         "d_conv_b", "d_norm_g", "d_norm_b", "w_br", "w_out", "ln_g", "ln_b")


def _unshard_last(g4, shard_shape):
    g = g4.reshape((N_CHIPS,) + tuple(shard_shape))
    g = jnp.moveaxis(g, 0, -2)
    return g.reshape(tuple(shard_shape[:-1]) + (N_CHIPS * shard_shape[-1],))


def _pieces_last(full):
    w = full.shape[-1] // N_CHIPS
    g = full.reshape(full.shape[:-1] + (N_CHIPS, w))
    return jnp.moveaxis(g, -2, 0).reshape(N_CHIPS, -1, w)


def kernel(x, c, ctx, c_ctx, w_mod, b_mod, w_in, q_norm, k_norm, b_conv, c_gate_w2, c_gate_b, c_norm, d_conv_w, d_conv_b, d_norm_g, d_norm_b, w_br, w_out, ln_g, ln_b, loss_target, m_c_ctx, m_w_mod, m_b_mod, m_w_in, m_q_norm, m_k_norm, m_b_conv, m_c_gate_w2, m_c_gate_b, m_c_norm, m_d_conv_w, m_d_conv_b, m_d_norm_g, m_d_norm_b, m_w_br, m_w_out, m_ln_g, m_ln_b, v_c_ctx, v_w_mod, v_b_mod, v_w_in, v_q_norm, v_k_norm, v_b_conv, v_c_gate_w2, v_c_gate_b, v_c_norm, v_d_conv_w, v_d_conv_b, v_d_norm_g, v_d_norm_b, v_w_br, v_w_out, v_ln_g, v_ln_b):
    W = dict(c_ctx=c_ctx, w_mod=w_mod, b_mod=b_mod, w_in=w_in, q_norm=q_norm, k_norm=k_norm, b_conv=b_conv,
             c_gate_w2=c_gate_w2, c_gate_b=c_gate_b, c_norm=c_norm, d_conv_w=d_conv_w, d_conv_b=d_conv_b,
             d_norm_g=d_norm_g, d_norm_b=d_norm_b, w_br=w_br, w_out=w_out, ln_g=ln_g, ln_b=ln_b)
    M = dict(c_ctx=m_c_ctx, w_mod=m_w_mod, b_mod=m_b_mod, w_in=m_w_in, q_norm=m_q_norm, k_norm=m_k_norm, b_conv=m_b_conv,
             c_gate_w2=m_c_gate_w2, c_gate_b=m_c_gate_b, c_norm=m_c_norm, d_conv_w=m_d_conv_w, d_conv_b=m_d_conv_b,
             d_norm_g=m_d_norm_g, d_norm_b=m_d_norm_b, w_br=m_w_br, w_out=m_w_out, ln_g=m_ln_g, ln_b=m_ln_b)
    V = dict(c_ctx=v_c_ctx, w_mod=v_w_mod, b_mod=v_b_mod, w_in=v_w_in, q_norm=v_q_norm, k_norm=v_k_norm, b_conv=v_b_conv,
             c_gate_w2=v_c_gate_w2, c_gate_b=v_c_gate_b, c_norm=v_c_norm, d_conv_w=v_d_conv_w, d_conv_b=v_d_conv_b,
             d_norm_g=v_d_norm_g, d_norm_b=v_d_norm_b, w_br=v_w_br, w_out=v_w_out, ln_g=v_ln_g, ln_b=v_ln_b)
    chip = 2 * lax.axis_index("x") + lax.axis_index("y")
    cidx = lax.axis_index("c").astype(jnp.int32).reshape(1)

    place = jnp.stack([chip, lax.axis_index("c")]).astype(jnp.int32)

    AXIS = dict(w_in=1, w_mod=0, w_br=0, w_out=0)
    ex = dict(w_in=lambda a: jnp.swapaxes(a, 1, 2), w_mod=lambda a: a.reshape(1, DEPTH * D, -1),
              w_br=lambda a: a.reshape(DEPTH, 4 * BRW, -1), w_out=lambda a: a)
    Wx, Mx, Vx = ({k: ex[k](P_[k]) for k in BIG} for P_ in (W, M, V))

    small_shard = _pack_small([W[k] for k in SHARD_SMALL])
    G = {}
    for l, keys in enumerate((("w_in", "w_br", "w_out", "w_mod"), ("w_in", "w_br", "w_out"))):
        sent = [Wx[k][l].astype(bf16) for k in keys] + ([small_shard] if l == 0 else [])
        got = _all_gather(sent, [AXIS[k] for k in keys] + [0], f"all_gather{l}")
        G.update({(k, l): a for k, a in zip(keys, got)})
        if l == 0:
            smalls = [_unpack_small(got[-1][s], [W[k].shape for k in SHARD_SMALL]) for s in range(N_CHIPS)]
    full = {k: jnp.concatenate([smalls[s][i] for s in range(N_CHIPS)], axis=-1) for i, k in enumerate(SHARD_SMALL)}
    wp = [_group_weights(G["w_in", l]) for l in range(DEPTH)]
    wbr = [jnp.moveaxis(G["w_br", l].reshape(N_CHIPS, 4, BRW, D // N_CHIPS), 0, 2).reshape(4, BRW, D) for l in range(DEPTH)]
    wout = [G["w_out", l].reshape(D, D) for l in range(DEPTH)]
    wmod = G["w_mod", 0].reshape(N_CHIPS, DEPTH, D, 3 * D // N_CHIPS)

    loss, gx, g = _local_step(
        x[0], c, ctx[0], loss_target[0], c_ctx, wmod, b_mod, wp, q_norm, k_norm, full["b_conv"],
        full["c_gate_w2"], full["c_gate_b"], c_norm, full["d_conv_w"], d_conv_b, d_norm_g, d_norm_b,
        wbr, wout, ln_g, ln_b, tm=256)
    g["c_gate_w2"], g["c_gate_b"] = g.pop("w2"), g.pop("gb")
    loss = lax.psum(loss, ("x", "y", "c"))

    pieces = {("w_in", l): _ungroup(g["wp"][l]).reshape(N_CHIPS, SHARD, D) for l in range(DEPTH)}
    pieces.update({("w_br", l): g["w_br"][l].reshape(N_CHIPS, 4 * BRW, D // N_CHIPS) for l in range(DEPTH)})
    pieces.update({("w_out", l): g["w_out"][l].reshape(N_CHIPS, D // N_CHIPS, D) for l in range(DEPTH)})
    pieces["w_mod", 0] = g["w_mod"].reshape(N_CHIPS, DEPTH * D, 3 * D // N_CHIPS)
    red = {k: Wx[k].shape for k in BIG}
    for l in reversed(range(DEPTH)):
        keys = [k for k in BIG if (k, l) in pieces]
        axes = [AXIS[k] for k in keys]
        land_a = _sibling_halves([pieces[k, l] for k in keys], axes, f"rs_sibling_halves{l}")
        pair = [_add_half(pieces[k, l], la, cidx, AXIS[k], f"rs_pair_sum{l}_{k}") for k, la in zip(keys, land_a)]
        land_b = _chip_exchange(pair, f"rs_chip_exchange{l}")
        for k, lb, pr in zip(keys, land_b, pair):
            red[k] = _sum_chips(lb, pr, place, AXIS[k], l, red[k], f"rs_chip_sum{l}_{k}")
    red = dict(zip(BIG, _sibling_fill([red[k] for k in BIG], [AXIS[k] for k in BIG], "rs_sibling_fill")))
    g = {k: (jnp.stack(v) if isinstance(v, list) else v) for k, v in g.items() if k not in ("wp", "w_br", "w_out", "w_mod")}

    small_names = REPL_SMALL + SHARD_SMALL
    gs = _all_reduce_small(_pack_small([g[k] for k in small_names]), "all_reduce_small")
    gsm = dict(zip(small_names, _unpack_small(gs, [g[k].shape for k in small_names])))
    for k in SHARD_SMALL:
        wdt = W[k].shape[-1]
        gsm[k] = lax.dynamic_slice_in_dim(gsm[k], chip * wdt, wdt, axis=gsm[k].ndim - 1)

    grad, delta, new_m, new_v = {}, {}, {}, {}
    for k in BIG:
        back = (lambda a: jnp.swapaxes(a, 1, 2)) if k == "w_in" else (lambda a: a.reshape(W[k].shape))
        d_, m_, v_ = _adamw(Wx[k], red[k], Mx[k], Vx[k], f"adamw_{k}")
        grad[k], delta[k], new_m[k], new_v[k] = back(red[k]), back(d_), back(m_), back(v_)
    shapes = [W[k].shape for k in small_names]
    d_, m_, v_ = _adamw(*[_pack_small([P_[k] for k in small_names])[None] for P_ in (W, gsm, M, V)], "adamw_small")
    for k, dd, mm_, vv in zip(small_names, _unpack_small(d_, shapes), _unpack_small(m_, shapes), _unpack_small(v_, shapes)):
        grad[k], delta[k], new_m[k], new_v[k] = gsm[k], dd, mm_, vv

    return (loss, gx[None], *[grad[k] for k in ORDER], *[delta[k] for k in ORDER], *[new_m[k] for k in ORDER],
            *[new_v[k] for k in ORDER])
```

```python
import functools

import jax
import jax.numpy as jnp
import numpy as np
from jax import lax
from jax.experimental import pallas as pl
from jax.experimental.pallas import tpu as pltpu

f32 = jnp.float32
bf16 = jnp.bfloat16

D = 1024
DEPTH = 2
GRID_W = 64
BRW = 512
HD = 128
A_HEADS = 4
C_HEADS = 4
C_KW = 256
C_RANK = 16
C_TAU = 16.0
CH = 64
KB = 3
KD = 31
ALPHA = (2 * DEPTH) ** 0.25
EPS = 1e-6
ROPE_THETA = 10000.0
N_IN = 10784
LR, B1, B2, AEPS, WD, STEP = 0.001, 0.9, 0.999, 1e-08, 0.01, 10

W_M, W_A, W_C, W_G = 4 * D + 4 * BRW, 1024, 5 * BRW, 1152
GROUPS = ("M", "A", "C", "G")
GROUP_W = dict(M=W_M, A=W_A, C=W_C, G=W_G)
M_GA, M_GB, M_GC, M_GD = 4 * D, 4 * D + BRW, 4 * D + 2 * BRW, 4 * D + 3 * BRW
A_K, A_V = 512, 768
G_K, G_V, G_R = 256, 512, 1024
CT = 5 * 128
S_Q, S_GA, S_B, S_C, S_X, S_GB, S_CQ, S_CV, S_GC, S_R, S_DA, S_DG, S_GD, S_MG = (
    0, 1024, 1536, 2048, 2560, 3072, 3584, 4096, 4608, 5120, 5152, 5664, 6176, 6688)

LANE = 128
SUB = 8
VMEM_LIMIT = 56 * 1024 * 1024
CONV_PAD = 16
GLA_SUB = 16
GLA_CLAMP = 60.0


def _cparams(sem, vmem=VMEM_LIMIT):
    return pltpu.CompilerParams(dimension_semantics=sem, vmem_limit_bytes=vmem)


def _dg(a, b, ca, cb):
    return lax.dot_general(a.astype(bf16), b.astype(bf16), (((ca,), (cb,)), ((), ())),
                           preferred_element_type=f32)


@jax.custom_vjp
def mm(a, b):
    return _dg(a, b, 1, 0)


mm.defvjp(lambda a, b: (_dg(a, b, 1, 0), (a, b)),
          lambda r, ct: (_dg(ct, r[1], 1, 1).astype(r[0].dtype), _dg(r[0], ct, 0, 0).astype(r[1].dtype)))


@jax.custom_vjp
def mm_nt(a, b):
    return _dg(a, b, 1, 1)


mm_nt.defvjp(lambda a, b: (_dg(a, b, 1, 1), (a, b)),
             lambda r, ct: (_dg(ct, r[1], 1, 0).astype(r[0].dtype), _dg(ct, r[0], 0, 0).astype(r[1].dtype)))


@jax.custom_vjp
def mm_tn(a, b):
    return _dg(a, b, 0, 0)


mm_tn.defvjp(lambda a, b: (_dg(a, b, 0, 0), (a, b)),
             lambda r, ct: (_dg(r[1], ct, 1, 1).astype(r[0].dtype), _dg(r[0], ct, 1, 0).astype(r[1].dtype)))


def _silu(x):
    return x * jax.nn.sigmoid(x)


def _ln(x):
    mu = jnp.mean(x, -1, keepdims=True)
    xc = x - mu
    var = jnp.mean(xc * xc, -1, keepdims=True)
    return xc * lax.rsqrt(var + EPS)


def _rms(x, g):
    return x * lax.rsqrt(jnp.mean(x * x, -1, keepdims=True) + EPS) * g


@jax.custom_vjp
def _rope(x, cos_f, sin_a, sin_b):
    return x * cos_f + pltpu.roll(x, HD - 1, 1) * sin_a + pltpu.roll(x, 1, 1) * sin_b


def _rope_fwd(x, cos_f, sin_a, sin_b):
    return _rope(x, cos_f, sin_a, sin_b), (cos_f, sin_a, sin_b)


def _rope_bwd(r, ct):
    cos_f, sin_a, sin_b = r
    dx = ct * cos_f + pltpu.roll(ct * sin_a, 1, 1) + pltpu.roll(ct * sin_b, HD - 1, 1)
    return dx, jnp.zeros_like(cos_f), jnp.zeros_like(sin_a), jnp.zeros_like(sin_b)


_rope.defvjp(_rope_fwd, _rope_bwd)


def _row_ids(i, tm):
    return i * tm + lax.broadcasted_iota(jnp.int32, (tm, 1), 0)


def _partial_rows(ref, rows):
    n = len(rows)
    for k, r in enumerate(rows):
        ref[k:k + 1, :] = r
    ref[n:SUB, :] = jnp.zeros((SUB - n, ref.shape[-1]), f32)


def _matmul(a, b, mode, tm, tn, tk, name, out_dtype=f32, add=None):
    if mode == "nn":
        (M, K), N = a.shape, b.shape[1]
        a_spec = pl.BlockSpec((tm, tk), lambda j, i, k: (i, k))
        b_spec = pl.BlockSpec((tk, tn), lambda j, i, k: (k, j))
        ca, cb = 1, 0
    elif mode == "nt":
        (M, K), N = a.shape, b.shape[0]
        a_spec = pl.BlockSpec((tm, tk), lambda j, i, k: (i, k))
        b_spec = pl.BlockSpec((tn, tk), lambda j, i, k: (j, k))
        ca, cb = 1, 1
    else:
        (K, M), N = a.shape, b.shape[1]
        a_spec = pl.BlockSpec((tk, tm), lambda j, i, k: (k, i))
        b_spec = pl.BlockSpec((tk, tn), lambda j, i, k: (k, j))
        ca, cb = 0, 0
    assert M % tm == 0 and N % tn == 0 and K % tk == 0, (name, M, N, K, tm, tn, tk)
    nk = K // tk

    o_spec = pl.BlockSpec((tm, tn), lambda j, i, k: (i, j))

    def body(a_ref, b_ref, *rest):
        add_ref = rest[0] if add is not None else None
        o_ref, acc_ref = rest[-2:]
        k = pl.program_id(2)
        part = _dg(a_ref[...], b_ref[...], ca, cb)

        @pl.when(k == 0)
        def _():
            acc_ref[...] = part if add_ref is None else part + add_ref[...]

        @pl.when(k > 0)
        def _():
            acc_ref[...] += part

        @pl.when(k == nk - 1)
        def _():
            o_ref[...] = acc_ref[...].astype(o_ref.dtype)

    return pl.pallas_call(
        body, name=name, grid=(N // tn, M // tm, nk),
        in_specs=[a_spec, b_spec] + ([o_spec] if add is not None else []), out_specs=o_spec,
        out_shape=jax.ShapeDtypeStruct((M, N), out_dtype),
        scratch_shapes=[pltpu.VMEM((tm, tn), f32)],
        compiler_params=_cparams(("parallel", "parallel", "arbitrary")),
    )(a, b, *([add] if add is not None else []))


def _matmul_tn_batched(a, b, ns, name):
    B, K, M = a.shape
    N = b.shape[2] // ns

    def body(a_ref, b_ref, o_ref):
        o_ref[...] = _dg(a_ref[...], b_ref[...], 0, 0)

    return pl.pallas_call(
        body, name=name, grid=(B, ns),
        in_specs=[pl.BlockSpec((None, K, M), lambda i, s: (i, 0, 0)), pl.BlockSpec((None, K, N), lambda i, s: (i, 0, s))],
        out_specs=pl.BlockSpec((None, None, M, N), lambda i, s: (s, i, 0, 0)),
        out_shape=jax.ShapeDtypeStruct((ns, B, M, N), f32),
        compiler_params=_cparams(("parallel", "parallel")),
    )(a, b)


MOD_TN = 768


def _mod_fwd(cin, w_mod, b_mod):
    def body(c_ref, w_ref, b_ref, o_ref):
        o_ref[...] = mm(_silu(c_ref[...]), w_ref[...]) + b_ref[...]

    return pl.pallas_call(
        body, name="mod_fwd", grid=(DEPTH, 3 * D // MOD_TN),
        in_specs=[pl.BlockSpec((SUB, D), lambda l, j: (0, 0)),
                  pl.BlockSpec((None, None, D, MOD_TN), lambda l, j: (j, l, 0, 0)),
                  pl.BlockSpec((None, 1, MOD_TN), lambda l, j: (l, 0, j))],
        out_specs=pl.BlockSpec((None, SUB, MOD_TN), lambda l, j: (l, 0, j)),
        out_shape=jax.ShapeDtypeStruct((DEPTH, SUB, 3 * D), f32),
        compiler_params=_cparams(("parallel", "parallel")),
    )(cin, w_mod, b_mod.reshape(DEPTH, 1, 3 * D))


def _mod_bwd(cin, w_mod, dmodv):
    nj = 3 * D // MOD_TN

    def body(c_ref, w_ref, g_ref, dw_ref, dc_ref):
        _, vjp = jax.vjp(lambda c, w: mm(_silu(c), w), c_ref[...], w_ref[...].astype(f32))
        dc, dw = vjp(g_ref[...])
        dw_ref[...] = dw
        dc_ref[...] = dc

    return pl.pallas_call(
        body, name="mod_bwd", grid=(DEPTH, nj),
        in_specs=[pl.BlockSpec((SUB, D), lambda l, j: (0, 0)),
                  pl.BlockSpec((None, None, D, MOD_TN), lambda l, j: (j, l, 0, 0)),
                  pl.BlockSpec((None, SUB, MOD_TN), lambda l, j: (l, 0, j))],
        out_specs=[pl.BlockSpec((None, None, D, MOD_TN), lambda l, j: (j, l, 0, 0)),
                   pl.BlockSpec((None, None, SUB, D), lambda l, j: (l, j, 0, 0))],
        out_shape=[jax.ShapeDtypeStruct((nj, DEPTH, D, MOD_TN), f32),
                   jax.ShapeDtypeStruct((DEPTH, nj, SUB, D), f32)],
        compiler_params=_cparams(("parallel", "parallel")),
    )(cin, w_mod, dmodv)


def _u_fn(h, m_l, m_c, isctx):
    n = _ln(h)
    shift = jnp.where(isctx, m_c[:, 0:D], m_l[:, 0:D])
    scale = jnp.where(isctx, m_c[:, D:2 * D], m_l[:, D:2 * D])
    return n * (1.0 + scale) + shift


def _ln_fwd(h, modv_l, tc, tm, name):
    T = h.shape[0]

    def body(h_ref, m_ref, u_ref):
        isctx = _row_ids(pl.program_id(0), tm) < tc
        u_ref[...] = _u_fn(h_ref[...], m_ref[0:1, :], m_ref[1:2, :], isctx).astype(bf16)

    return pl.pallas_call(
        body, name=name, grid=(T // tm,),
        in_specs=[pl.BlockSpec((tm, D), lambda i: (i, 0)), pl.BlockSpec((SUB, 3 * D), lambda i: (0, 0))],
        out_specs=pl.BlockSpec((tm, D), lambda i: (i, 0)),
        out_shape=jax.ShapeDtypeStruct((T, D), bf16),
        compiler_params=_cparams(("parallel",)),
    )(h, modv_l)


def _ln_bwd(du, h, dh_res, modv_l, tc, tm, name):
    T = h.shape[0]
    nt = T // tm

    def body(du_ref, h_ref, r_ref, m_ref, dh_ref, dm_ref):
        isctx = _row_ids(pl.program_id(0), tm) < tc
        _, vjp = jax.vjp(lambda h, ml, mc: _u_fn(h, ml, mc, isctx), h_ref[...], m_ref[0:1, :], m_ref[1:2, :])
        dh, dml, dmc = vjp(du_ref[...])
        dh_ref[...] = dh + r_ref[...]
        _partial_rows(dm_ref, [dml, dmc])

    return pl.pallas_call(
        body, name=name, grid=(nt,),
        in_specs=[pl.BlockSpec((tm, D), lambda i: (i, 0)), pl.BlockSpec((tm, D), lambda i: (i, 0)),
                  pl.BlockSpec((tm, D), lambda i: (i, 0)), pl.BlockSpec((SUB, 3 * D), lambda i: (0, 0))],
        out_specs=[pl.BlockSpec((tm, D), lambda i: (i, 0)), pl.BlockSpec((None, SUB, 3 * D), lambda i: (i, 0, 0))],
        out_shape=[jax.ShapeDtypeStruct((T, D), f32), jax.ShapeDtypeStruct((nt, SUB, 3 * D), f32)],
        compiler_params=_cparams(("parallel",)),
    )(du, h, dh_res, modv_l)


def _prep_fn(q, k, qg, kg, cos_f, sin_a, sin_b):
    qs = [_rope(_rms(q[:, HD * i:HD * (i + 1)], qg), cos_f, sin_a, sin_b) for i in range(A_HEADS)]
    ks = [_rope(_rms(k[:, HD * i:HD * (i + 1)], kg), cos_f, sin_a, sin_b) for i in range(A_HEADS // 2)]
    return jnp.concatenate(qs, 1), jnp.concatenate(ks, 1)


def _tok(tm, w, off):
    return pl.BlockSpec((tm, w), lambda i: (i, off // w))


def _vec(w):
    return pl.BlockSpec((1, w), lambda i: (0, 0))


def _prep_fwd(P, qg, kg, rope, tm, name):
    T = P.shape[0]

    def body(q_ref, k_ref, v_ref, qg_ref, kg_ref, c_ref, sa_ref, sb_ref, qn_ref, kn_ref, vb_ref):
        qn, kn = _prep_fn(q_ref[...], k_ref[...], qg_ref[...], kg_ref[...], c_ref[...], sa_ref[...], sb_ref[...])
        qn_ref[...] = qn.astype(bf16)
        kn_ref[...] = kn.astype(bf16)
        vb_ref[...] = v_ref[...].astype(bf16)

    return pl.pallas_call(
        body, name=name, grid=(T // tm,),
        in_specs=[_tok(tm, 512, 0), _tok(tm, 256, A_K), _tok(tm, 256, A_V), _vec(HD), _vec(HD),
                  _tok(tm, HD, 0), _tok(tm, HD, 0), _tok(tm, HD, 0)],
        out_specs=[_tok(tm, 512, 0), _tok(tm, 256, 0), _tok(tm, 256, 0)],
        out_shape=[jax.ShapeDtypeStruct((T, 512), bf16), jax.ShapeDtypeStruct((T, 256), bf16),
                   jax.ShapeDtypeStruct((T, 256), bf16)],
        compiler_params=_cparams(("parallel",)),
    )(P, P, P, qg, kg, *rope)


def _prep_bwd(P, dqn, dkn, dv, qg, kg, rope, tm, name):
    T = P.shape[0]
    nt = T // tm

    def body(q_ref, k_ref, dq_ref, dk_ref, dv_ref, qg_ref, kg_ref, c_ref, sa_ref, sb_ref, o_ref, og_ref):
        tabs = (c_ref[...], sa_ref[...], sb_ref[...])
        _, vjp = jax.vjp(lambda q, k, a, b: _prep_fn(q, k, a, b, *tabs), q_ref[...], k_ref[...], qg_ref[...], kg_ref[...])
        dq, dk, dqg, dkg = vjp((dq_ref[...], dk_ref[...]))
        o_ref[:, 0:A_K] = dq
        o_ref[:, A_K:A_V] = dk
        o_ref[:, A_V:W_A] = dv_ref[...]
        _partial_rows(og_ref, [dqg, dkg])

    return pl.pallas_call(
        body, name=name, grid=(nt,),
        in_specs=[_tok(tm, 512, 0), _tok(tm, 256, A_K), _tok(tm, 512, 0), _tok(tm, 256, 0), _tok(tm, 256, 0),
                  _vec(HD), _vec(HD), _tok(tm, HD, 0), _tok(tm, HD, 0), _tok(tm, HD, 0)],
        out_specs=[_tok(tm, W_A, 0), pl.BlockSpec((None, SUB, HD), lambda i: (i, 0, 0))],
        out_shape=[jax.ShapeDtypeStruct((T, W_A), f32), jax.ShapeDtypeStruct((nt, SUB, HD), f32)],
        compiler_params=_cparams(("parallel",)),
    )(P, P, dqn, dkn, dv, qg, kg, *rope)


def _attn_fn(q, k, v, lim):
    s = mm_nt(q, k) * (HD ** -0.5)
    col = lax.broadcasted_iota(jnp.int32, s.shape, 1)
    s = jnp.where(col < lim, s, -1e30)
    m = jnp.max(s, -1, keepdims=True)
    e = jnp.exp(s - m)
    p = e / jnp.sum(e, -1, keepdims=True)
    return mm(p, v)


def _attn_fwd(qn, kn, vb, tc, tq, name):
    T = qn.shape[0]

    def body(q_ref, k_ref, v_ref, o_ref):
        lim = jnp.where(pl.program_id(1) * tq < tc, tc, T)
        o_ref[...] = _attn_fn(q_ref[...], k_ref[...], v_ref[...], lim)

    return pl.pallas_call(
        body, name=name, grid=(A_HEADS, T // tq),
        in_specs=[pl.BlockSpec((tq, HD), lambda h, i: (i, h)), pl.BlockSpec((T, HD), lambda h, i: (0, h // 2)),
                  pl.BlockSpec((T, HD), lambda h, i: (0, h // 2))],
        out_specs=pl.BlockSpec((tq, HD), lambda h, i: (i, h)),
        out_shape=jax.ShapeDtypeStruct((T, 512), f32),
        compiler_params=_cparams(("parallel", "parallel")),
    )(qn, kn, vb)


def _attn_bwd(qn, kn, vb, dya, tc, tq, name):
    T = qn.shape[0]

    def body(q_ref, k_ref, v_ref, g_ref, dq_ref, dk_ref, dv_ref):
        first = (pl.program_id(1) == 0) & (pl.program_id(2) == 0)
        lim = jnp.where(pl.program_id(2) * tq < tc, tc, T)
        _, vjp = jax.vjp(lambda q, k, v: _attn_fn(q, k, v, lim), q_ref[...].astype(f32), k_ref[...].astype(f32),
                         v_ref[...].astype(f32))
        dq, dk, dv = vjp(g_ref[...])
        dq_ref[...] = dq

        @pl.when(first)
        def _():
            dk_ref[...] = dk
            dv_ref[...] = dv

        @pl.when(jnp.logical_not(first))
        def _():
            dk_ref[...] += dk
            dv_ref[...] += dv

    qspec = pl.BlockSpec((tq, HD), lambda kv, g, i: (i, 2 * kv + g))
    kspec = pl.BlockSpec((T, HD), lambda kv, g, i: (0, kv))
    return pl.pallas_call(
        body, name=name, grid=(A_HEADS // 2, 2, T // tq),
        in_specs=[qspec, kspec, kspec, qspec], out_specs=[qspec, kspec, kspec],
        out_shape=[jax.ShapeDtypeStruct((T, 512), f32), jax.ShapeDtypeStruct((T, 256), f32),
                   jax.ShapeDtypeStruct((T, 256), f32)],
        compiler_params=_cparams(("parallel", "arbitrary", "arbitrary")),
    )(qn, kn, vb, dya)


def _conv_rows(tc, tl):
    return CONV_PAD + tc + CONV_PAD + tl + CONV_PAD


def _fill_pad(pad_ref, val, tc, tl):
    z = jnp.zeros((CONV_PAD, LANE), f32)
    pad_ref[0:CONV_PAD, :] = z
    pad_ref[CONV_PAD:CONV_PAD + tc, :] = val[0:tc]
    pad_ref[CONV_PAD + tc:2 * CONV_PAD + tc, :] = z
    pad_ref[2 * CONV_PAD + tc:2 * CONV_PAD + tc + tl, :] = val[tc:tc + tl]
    pad_ref[2 * CONV_PAD + tc + tl:3 * CONV_PAD + tc + tl, :] = z


def _conv_apply(pad_ref, w_ref, K, tc, tl, rc, emit, flip=False):
    half = K // 2
    for seg0, off, n in ((0, CONV_PAD, tc), (tc, 2 * CONV_PAD + tc, tl)):
        for r0 in range(0, n, rc):
            acc = None
            for k in range(K):
                sh = (half - k) if flip else (k - half)
                term = pad_ref[pl.ds(off + r0 + sh, rc), :] * w_ref[k:k + 1, :]
                acc = term if acc is None else acc + term
            emit(seg0 + r0, acc)


def _conv_wgrad(pad_ref, dy_ref, K, tc, tl, rc, dw_ref):
    half = K // 2
    for k in range(K):
        acc = jnp.zeros((1, LANE), f32)
        for seg0, off, n in ((0, CONV_PAD, tc), (tc, 2 * CONV_PAD + tc, tl)):
            for r0 in range(0, n, rc):
                acc = acc + jnp.sum(pad_ref[pl.ds(off + r0 + k - half, rc), :] * dy_ref[pl.ds(seg0 + r0, rc), :],
                                    axis=0, keepdims=True)
        dw_ref[k:k + 1, :] = acc


def _col(T, off):
    return pl.BlockSpec((T, LANE), lambda j: (0, off // LANE + j))


def _ctile(T):
    return pl.BlockSpec((T, CT), lambda j: (0, j))


C_B, C_C, C_X, C_A, C_G = (slice(LANE * i, LANE * (i + 1)) for i in range(5))


def _conv_fwd(P, wb, wd, bd, tc, tl, rc, name):
    T = tc + tl

    def body(p_ref, wb_ref, wd_ref, bd_ref, yb_ref, hh_ref, pad_ref):
        _fill_pad(pad_ref, p_ref[:, C_C] * p_ref[:, C_X], tc, tl)

        def emit_b(r0, y):
            yb_ref[pl.ds(r0, rc), :] = y * p_ref[pl.ds(r0, rc), C_B]

        _conv_apply(pad_ref, wb_ref, KB, tc, tl, rc, emit_b)
        _fill_pad(pad_ref, p_ref[:, C_A] * jax.nn.sigmoid(p_ref[:, C_G]), tc, tl)

        def emit_d(r0, y):
            hh_ref[pl.ds(r0, rc), :] = y + bd_ref[...]

        _conv_apply(pad_ref, wd_ref, KD, tc, tl, rc, emit_d)

    return pl.pallas_call(
        body, name=name, grid=(BRW // LANE,),
        in_specs=[_ctile(T), pl.BlockSpec((KB, LANE), lambda j: (0, j)), pl.BlockSpec((KD, LANE), lambda j: (0, j)),
                  pl.BlockSpec((1, LANE), lambda j: (0, j))],
        out_specs=[_col(T, 0), _col(T, 0)],
        out_shape=[jax.ShapeDtypeStruct((T, BRW), f32), jax.ShapeDtypeStruct((T, BRW), f32)],
        scratch_shapes=[pltpu.VMEM((_conv_rows(tc, tl), LANE), f32)],
        compiler_params=_cparams(("parallel",)),
    )(P, wb, wd, bd)


def _conv_bwd(P, dyb, dhh, wb, wd, tc, tl, rc, name):
    T = tc + tl

    def body(p_ref, dyb_ref, dhh_ref, wb_ref, wd_ref, dp_ref, dwb_ref, dwd_ref, dbd_ref, pad_ref, pad2_ref, tmp_ref):
        _fill_pad(pad_ref, p_ref[:, C_C] * p_ref[:, C_X], tc, tl)

        def emit_cv(r0, y):
            dp_ref[pl.ds(r0, rc), C_B] = y * dyb_ref[pl.ds(r0, rc), :]

        _conv_apply(pad_ref, wb_ref, KB, tc, tl, rc, emit_cv)
        tmp_ref[...] = dyb_ref[...] * p_ref[:, C_B]
        _conv_wgrad(pad_ref, tmp_ref, KB, tc, tl, rc, dwb_ref)
        _fill_pad(pad2_ref, tmp_ref[...], tc, tl)

        def emit_ds(r0, y):
            dp_ref[pl.ds(r0, rc), C_C] = y * p_ref[pl.ds(r0, rc), C_X]
            dp_ref[pl.ds(r0, rc), C_X] = y * p_ref[pl.ds(r0, rc), C_C]

        _conv_apply(pad2_ref, wb_ref, KB, tc, tl, rc, emit_ds, flip=True)
        _fill_pad(pad_ref, p_ref[:, C_A] * jax.nn.sigmoid(p_ref[:, C_G]), tc, tl)
        _conv_wgrad(pad_ref, dhh_ref, KD, tc, tl, rc, dwd_ref)
        dbd_ref[...] = jnp.sum(dhh_ref[...], axis=0, keepdims=True)
        _fill_pad(pad2_ref, dhh_ref[...], tc, tl)

        def emit_d2(r0, y):
            sg = jax.nn.sigmoid(p_ref[pl.ds(r0, rc), C_G])
            a = p_ref[pl.ds(r0, rc), C_A]
            dp_ref[pl.ds(r0, rc), C_A] = y * sg
            dp_ref[pl.ds(r0, rc), C_G] = y * a * sg * (1.0 - sg)

        _conv_apply(pad2_ref, wd_ref, KD, tc, tl, rc, emit_d2, flip=True)

    return pl.pallas_call(
        body, name=name, grid=(BRW // LANE,),
        in_specs=[_ctile(T), _col(T, 0), _col(T, 0),
                  pl.BlockSpec((KB, LANE), lambda j: (0, j)), pl.BlockSpec((KD, LANE), lambda j: (0, j))],
        out_specs=[_ctile(T), pl.BlockSpec((KB, LANE), lambda j: (0, j)), pl.BlockSpec((KD, LANE), lambda j: (0, j)),
                   pl.BlockSpec((1, LANE), lambda j: (0, j))],
        out_shape=[jax.ShapeDtypeStruct((T, W_C), f32), jax.ShapeDtypeStruct((KB, BRW), f32),
                   jax.ShapeDtypeStruct((KD, BRW), f32), jax.ShapeDtypeStruct((1, BRW), f32)],
        scratch_shapes=[pltpu.VMEM((_conv_rows(tc, tl), LANE), f32), pltpu.VMEM((_conv_rows(tc, tl), LANE), f32),
                        pltpu.VMEM((T, LANE), f32)],
        compiler_params=_cparams(("parallel",)),
    )(P, dyb, dhh, wb, wd)


def _gla_chunk(q, k, v, r, w2, b2, st, isfwd):
    z = mm(r, w2) + b2
    g = jax.nn.log_sigmoid(jnp.where(isfwd, z[:, 0:C_KW], z[:, C_KW:2 * C_KW])) / C_TAU
    ri = lax.broadcasted_iota(jnp.int32, (CH, CH), 0)
    ci = lax.broadcasted_iota(jnp.int32, (CH, CH), 1)
    keep = jnp.where(isfwd, ri - ci, ci - ri) >= 0
    tri = keep.astype(f32)
    cum = jnp.dot(tri, g, preferred_element_type=f32, precision=lax.Precision.HIGHEST)
    last = jnp.sum(g, axis=0, keepdims=True)
    q = q * (C_KW // C_HEADS) ** -0.5
    hv = lax.broadcasted_iota(jnp.int32, (BRW, C_KW), 0) // (BRW // C_HEADS)
    hk = lax.broadcasted_iota(jnp.int32, (BRW, C_KW), 1) // (C_KW // C_HEADS)
    st_new = st * jnp.exp(last) + jnp.where(hv == hk, mm_tn(v, k * jnp.exp(last - cum)), 0.0)
    o = mm_nt(q * jnp.exp(cum), st)
    rowi = lax.broadcasted_iota(jnp.int32, (CH, C_KW), 0)
    lane_head = lax.broadcasted_iota(jnp.int32, (CH, C_KW), 1) // (C_KW // C_HEADS)
    scores = [jnp.zeros((CH, CH), f32) for _ in range(C_HEADS)]
    for a in range(CH // GLA_SUB):
        idx = jnp.where(isfwd, GLA_SUB * a - 1, GLA_SUB * (a + 1))
        ref = jnp.sum(jnp.where(rowi == idx, cum, 0.0), axis=0, keepdims=True)
        qa = q * jnp.exp(jnp.minimum(cum - ref, 0.0))
        ka = k * jnp.exp(jnp.minimum(ref - cum, GLA_CLAMP))
        in_block = (ri // GLA_SUB == a) & keep
        for hd in range(C_HEADS):
            s = mm_nt(jnp.where(lane_head == hd, qa, 0.0), ka)
            scores[hd] = scores[hd] + jnp.where(in_block, s, 0.0)
    vw = BRW // C_HEADS
    o = o + jnp.concatenate([mm(scores[hd], v[:, vw * hd:vw * (hd + 1)]) for hd in range(C_HEADS)], axis=1)
    return o, st_new


def _gla_chunk_of(d, n, nc, nch):
    back = jnp.where(n < nc, nc - 1 - n, nch - 1 - (n - nc))
    return jnp.where(d == 0, n, back)


def _gla_fwd(P, w2, b2, tc, name):
    T = P.shape[0]
    nch, nc = T // CH, tc // CH

    def body(p_ref, w_ref, b_ref, o_ref, ss_ref, st_ref):
        @pl.when(pl.program_id(1) == 0)
        def _():
            st_ref[...] = jnp.zeros_like(st_ref)

        st = st_ref[...]
        ss_ref[...] = st
        o, st_new = _gla_chunk(p_ref[:, 0:G_K], p_ref[:, G_K:G_V], p_ref[:, G_V:G_R], p_ref[:, G_R:W_G], w_ref[...],
                               b_ref[...], st, pl.program_id(0) == 0)
        o_ref[...] = o
        st_ref[...] = st_new

    return pl.pallas_call(
        body, name=name, grid=(2, nch),
        in_specs=[pl.BlockSpec((CH, W_G), lambda d, n: (_gla_chunk_of(d, n, nc, nch), 0)),
                  pl.BlockSpec((LANE, 512), lambda d, n: (0, 0)), pl.BlockSpec((1, 512), lambda d, n: (0, 0))],
        out_specs=[pl.BlockSpec((None, CH, BRW), lambda d, n: (d, _gla_chunk_of(d, n, nc, nch), 0)),
                   pl.BlockSpec((None, None, BRW, C_KW), lambda d, n: (d, n, 0, 0))],
        out_shape=[jax.ShapeDtypeStruct((2, T, BRW), f32), jax.ShapeDtypeStruct((2, nch, BRW, C_KW), f32)],
        scratch_shapes=[pltpu.VMEM((BRW, C_KW), f32)],
        compiler_params=_cparams(("parallel", "arbitrary")),
    )(P, w2, b2)


def _gla_bwd(P, w2, b2, ssave, doc, tc, name):
    T = P.shape[0]
    nch, nc = T // CH, tc // CH

    def chunk(d, m):
        return _gla_chunk_of(d, nch - 1 - m, nc, nch)

    def body(p_ref, w_ref, b_ref, ss_ref, g_ref, dp_ref, dw_ref, db_ref, dst_ref):
        m = pl.program_id(1)
        isfwd = pl.program_id(0) == 0

        @pl.when(m == 0)
        def _():
            dst_ref[...] = jnp.zeros_like(dst_ref)

        _, vjp = jax.vjp(lambda q, k, v, r, w, b, st: _gla_chunk(q, k, v, r, w, b, st, isfwd),
                         p_ref[:, 0:G_K], p_ref[:, G_K:G_V], p_ref[:, G_V:G_R], p_ref[:, G_R:W_G], w_ref[...], b_ref[...],
                         ss_ref[...])
        dq, dk, dv, dr, dw, db, dst = vjp((g_ref[...], dst_ref[...]))
        dp_ref[:, 0:G_K] = dq
        dp_ref[:, G_K:G_V] = dk
        dp_ref[:, G_V:G_R] = dv
        dp_ref[:, G_R:W_G] = dr
        dst_ref[...] = dst

        @pl.when(m == 0)
        def _():
            dw_ref[...] = dw
            _partial_rows(db_ref, [db])

        @pl.when(m > 0)
        def _():
            dw_ref[...] += dw
            db_ref[0:1, :] += db

    return pl.pallas_call(
        body, name=name, grid=(2, nch),
        in_specs=[pl.BlockSpec((CH, W_G), lambda d, m: (chunk(d, m), 0)),
                  pl.BlockSpec((LANE, 512), lambda d, m: (0, 0)), pl.BlockSpec((1, 512), lambda d, m: (0, 0)),
                  pl.BlockSpec((None, None, BRW, C_KW), lambda d, m: (d, nch - 1 - m, 0, 0)),
                  pl.BlockSpec((CH, BRW), lambda d, m: (chunk(d, m), 0))],
        out_specs=[pl.BlockSpec((None, CH, W_G), lambda d, m: (d, chunk(d, m), 0)),
                   pl.BlockSpec((None, LANE, 512), lambda d, m: (d, 0, 0)),
                   pl.BlockSpec((None, SUB, 512), lambda d, m: (d, 0, 0))],
        out_shape=[jax.ShapeDtypeStruct((2, T, W_G), f32),
                   jax.ShapeDtypeStruct((2, LANE, 512), f32), jax.ShapeDtypeStruct((2, SUB, 512), f32)],
        scratch_shapes=[pltpu.VMEM((BRW, C_KW), f32)],
        compiler_params=_cparams(("parallel", "arbitrary")),
    )(P, w2, b2, ssave, doc)


def _sum_dirs(a, tm, name):
    _, T, W = a.shape

    def body(a_ref, o_ref):
        o_ref[...] = a_ref[0] + a_ref[1]

    return pl.pallas_call(
        body, name=name, grid=(T // tm,),
        in_specs=[pl.BlockSpec((2, tm, W), lambda i: (0, i, 0))], out_specs=pl.BlockSpec((tm, W), lambda i: (i, 0)),
        out_shape=jax.ShapeDtypeStruct((T, W), f32),
        compiler_params=_cparams(("parallel",)),
    )(a)


def _merge_fn(h, m_l, m_c, isctx, ya, ga, yb, gb, of, ob, gc, hh, gd, mg, es, ey, cn, dng, dnb, lg, lb, wbr, wout):
    oc = of + ob
    yc = jnp.concatenate([_rms(oc[:, HD * i:HD * (i + 1)], cn[:, HD * i:HD * (i + 1)]) for i in range(C_HEADS)], 1)
    brs = [ya * _silu(ga), yb * _silu(gb), yc * _silu(gc), _silu(_ln(hh) * dng + dnb) * _silu(gd)]
    acc = None
    for i in range(4):
        t = jax.nn.sigmoid(mg[:, D * i:D * (i + 1)]) * (mm(brs[i], wbr[i]) + es[i])
        acc = t if acc is None else acc + t
    y = mm(acc, wout) + ey
    gate = jnp.where(isctx, m_c[:, 2 * D:3 * D], m_l[:, 2 * D:3 * D])
    hn = _ln(ALPHA * h + gate * y) * lg + lb
    return hn, (brs, acc)


def _merge_specs(tm):
    t = lambda w, off=0: _tok(tm, w, off)
    return [t(D), pl.BlockSpec((SUB, 3 * D), lambda i: (0, 0)),
            t(BRW), t(BRW, M_GA), t(BRW), t(BRW, M_GB),
            pl.BlockSpec((None, tm, BRW), lambda i: (0, i, 0)), pl.BlockSpec((None, tm, BRW), lambda i: (1, i, 0)),
            t(BRW, M_GC), t(BRW), t(BRW, M_GD), t(4 * D, 0),
            _vec(BRW), _vec(BRW), _vec(BRW), _vec(D), _vec(D),
            pl.BlockSpec((4, BRW, D), lambda i: (0, 0, 0)), pl.BlockSpec((D, D), lambda i: (0, 0))]


def _merge_fwd(h, modv_l, ya, yb, o2, hh, P, cn, dng, dnb, lg, lb, wbr, wout, tc, tm, name):
    T = h.shape[0]

    def body(h_ref, m_ref, ya_ref, ga_ref, yb_ref, gb_ref, of_ref, ob_ref, gc_ref, hh_ref, gd_ref, mg_ref,
             cn_ref, dng_ref, dnb_ref, lg_ref, lb_ref, wbr_ref, wout_ref, o_ref):
        isctx = _row_ids(pl.program_id(0), tm) < tc
        zero = jnp.zeros((tm, D), f32)
        hn, _ = _merge_fn(h_ref[...], m_ref[0:1, :], m_ref[1:2, :], isctx, ya_ref[...], ga_ref[...], yb_ref[...],
                          gb_ref[...], of_ref[...], ob_ref[...], gc_ref[...], hh_ref[...], gd_ref[...], mg_ref[...],
                          [zero] * 4, zero, cn_ref[...], dng_ref[...], dnb_ref[...], lg_ref[...], lb_ref[...],
                          [wbr_ref[i] for i in range(4)], wout_ref[...])
        o_ref[...] = hn

    return pl.pallas_call(
        body, name=name, grid=(T // tm,),
        in_specs=_merge_specs(tm), out_specs=_tok(tm, D, 0),
        out_shape=jax.ShapeDtypeStruct((T, D), f32),
        compiler_params=_cparams(("parallel",)),
    )(h, modv_l, ya, P, yb, P, o2, o2, P, hh, P, P, cn, dng, dnb, lg, lb, wbr, wout)


def _merge_bwd(dhn, h, modv_l, ya, yb, o2, hh, P, cn, dng, dnb, lg, lb, wbr, wout, tc, tm, name):
    T = h.shape[0]
    nt = T // tm

    def body(g_ref, h_ref, m_ref, ya_ref, ga_ref, yb_ref, gb_ref, of_ref, ob_ref, gc_ref, hh_ref, gd_ref, mg_ref,
             cn_ref, dng_ref, dnb_ref, lg_ref, lb_ref, wbr_ref, wout_ref,
             dh_ref, dm_ref, dya_ref, dyb_ref, doc_ref, dhh_ref, dp_ref,
             br_ref, z_ref, acc_ref, dy_ref, dv5_ref, dvd_ref):
        isctx = _row_ids(pl.program_id(0), tm) < tc
        zero = jnp.zeros((tm, D), f32)
        wbr_v = [wbr_ref[i] for i in range(4)]
        wout_v = wout_ref[...]

        def fn(h, ml, mc, ya, ga, yb, gb, oc, gc, hh, gd, mg, e0, e1, e2, e3, ey, cn, dng, dnb, lg, lb):
            return _merge_fn(h, ml, mc, isctx, ya, ga, yb, gb, oc, jnp.zeros_like(oc), gc, hh, gd, mg,
                             [e0, e1, e2, e3], ey, cn, dng, dnb, lg, lb, wbr_v, wout_v)

        _, vjp, (brs, acc) = jax.vjp(
            fn, h_ref[...], m_ref[0:1, :], m_ref[1:2, :], ya_ref[...], ga_ref[...], yb_ref[...], gb_ref[...],
            of_ref[...] + ob_ref[...], gc_ref[...], hh_ref[...], gd_ref[...], mg_ref[...], zero, zero, zero, zero, zero,
            cn_ref[...], dng_ref[...], dnb_ref[...], lg_ref[...], lb_ref[...], has_aux=True)
        (dh, dml, dmc, dya, dga, dyb, dgb, doc, dgc, dhh, dgd, dmg, z0, z1, z2, z3, dy,
         dcn, ddng, ddnb, dlg, dlb) = vjp(g_ref[...])
        dh_ref[...] = dh
        _partial_rows(dm_ref, [dml, dmc])
        dya_ref[...] = dya
        dyb_ref[...] = dyb
        doc_ref[...] = doc
        dhh_ref[...] = dhh
        dp_ref[:, 0:M_GA] = dmg
        dp_ref[:, M_GA:M_GB] = dga
        dp_ref[:, M_GB:M_GC] = dgb
        dp_ref[:, M_GC:M_GD] = dgc
        dp_ref[:, M_GD:W_M] = dgd
        for i, z in enumerate((z0, z1, z2, z3)):
            br_ref[i] = brs[i].astype(bf16)
            z_ref[i] = z.astype(bf16)
        acc_ref[...] = acc.astype(bf16)
        dy_ref[...] = dy.astype(bf16)
        _partial_rows(dv5_ref, [dcn, ddng, ddnb])
        _partial_rows(dvd_ref, [dlg, dlb])

    t = lambda w: _tok(tm, w, 0)
    part = lambda w: pl.BlockSpec((None, SUB, w), lambda i: (i, 0, 0))
    sd = jax.ShapeDtypeStruct
    return pl.pallas_call(
        body, name=name, grid=(nt,),
        in_specs=[t(D)] + _merge_specs(tm),
        out_specs=[t(D), part(3 * D)] + [t(BRW)] * 4 + [t(W_M),
                   pl.BlockSpec((4, tm, BRW), lambda i: (0, i, 0)), pl.BlockSpec((4, tm, D), lambda i: (0, i, 0)),
                   t(D), t(D), part(BRW), part(D)],
        out_shape=[sd((T, D), f32), sd((nt, SUB, 3 * D), f32)] + [sd((T, BRW), f32)] * 4 + [sd((T, W_M), f32),
                   sd((4, T, BRW), bf16), sd((4, T, D), bf16), sd((T, D), bf16), sd((T, D), bf16),
                   sd((nt, SUB, BRW), f32), sd((nt, SUB, D), f32)],
        compiler_params=_cparams(("parallel",)),
    )(dhn, h, modv_l, ya, P, yb, P, o2, o2, P, hh, P, P, cn, dng, dnb, lg, lb, wbr, wout)


def _loss_kernel(h, tgt, tc, tm, name):
    T = h.shape[0]
    nt = T // tm
    nct = tc // tm

    def body(h_ref, t_ref, d_ref, l_ref):
        i = pl.program_id(0)
        err = h_ref[...] - t_ref[...]
        lat = (i >= nct).astype(f32)
        d_ref[...] = err * (lat / D)
        l_ref[...] = jnp.zeros((SUB, LANE), f32) + lat * 0.5 * jnp.sum(err * err) / D

    return pl.pallas_call(
        body, name=name, grid=(nt,),
        in_specs=[pl.BlockSpec((tm, D), lambda i: (i, 0)),
                  pl.BlockSpec((tm, D), lambda i: (jnp.maximum(i - nct, 0), 0))],
        out_specs=[pl.BlockSpec((tm, D), lambda i: (i, 0)), pl.BlockSpec((None, SUB, LANE), lambda i: (i, 0, 0))],
        out_shape=[jax.ShapeDtypeStruct((T, D), f32), jax.ShapeDtypeStruct((nt, SUB, LANE), f32)],
        compiler_params=_cparams(("parallel",)),
    )(h, tgt)


def _rope_tables(tc, tl):
    t = jnp.arange(tl)
    inv = ROPE_THETA ** (-jnp.arange(0, HD // 2, 2, dtype=f32) / (HD // 2))
    ang = jnp.concatenate([(t // GRID_W).astype(f32)[:, None] * inv, (t % GRID_W).astype(f32)[:, None] * inv], -1)
    cos, sin = jnp.repeat(jnp.cos(ang), 2, axis=1), jnp.repeat(jnp.sin(ang), 2, axis=1)
    even = (jnp.arange(HD) % 2 == 0)[None, :]
    cos_f = jnp.concatenate([jnp.ones((tc, HD), f32), cos], 0)
    sin_a = jnp.concatenate([jnp.zeros((tc, HD), f32), jnp.where(even, -sin, 0.0)], 0)
    sin_b = jnp.concatenate([jnp.zeros((tc, HD), f32), jnp.where(even, 0.0, sin)], 0)
    return cos_f, sin_a, sin_b


N_CHIPS = 4
SHARD = N_IN // N_CHIPS


def _group_ranges():
    conv = [(s0 + LANE * j, LANE) for j in range(BRW // LANE) for s0 in (S_B, S_C, S_X, S_DA, S_DG)]
    return dict(M=[(S_MG, 4 * D), (S_GA, BRW), (S_GB, BRW), (S_GC, BRW), (S_GD, BRW)], A=[(S_Q, W_A)], C=conv,
                G=[(S_CQ, 2 * C_KW), (S_CV, BRW), (S_R, 2 * C_RANK)])


def _group_weights(w4):
    out = {}
    for k, ranges in _group_ranges().items():
        parts = []
        for a, n in ranges:
            while n > 0:
                s, r = divmod(a, SHARD)
                m = min(n, SHARD - r)
                parts.append(w4[s, r:r + m])
                a, n = a + m, n - m
        if k == "G":
            parts.append(jnp.zeros((LANE - 2 * C_RANK, D), w4.dtype))
        out[k] = jnp.concatenate(parts, 0)
    return out


def _ungroup(g):
    secs = []
    for k, ranges in _group_ranges().items():
        off = 0
        for a, n in ranges:
            secs.append((a, g[k][off:off + n]))
            off += n
    return jnp.concatenate([v for _, v in sorted(secs, key=lambda t: t[0])], 0)


PROJ_TN = dict(M=2048, A=1024, C=1280, G=1152)
DWP_TN = dict(M=768, A=1024, C=640, G=1152)


def _gate_weights(w2_l, gb_l):
    w = jnp.zeros((LANE, 2 * C_KW), f32)
    w = w.at[0:C_RANK, 0:C_KW].set(w2_l[0]).at[C_RANK:2 * C_RANK, C_KW:2 * C_KW].set(w2_l[1])
    return w, jnp.concatenate([gb_l[0], gb_l[1]])[None, :]


def _local_step(x1, c1, ctx1, tgt1, c_ctx, w_mod, b_mod, weights_of, q_norm, k_norm, b_conv, w2, gb, c_norm, d_conv_w,
                d_conv_b, d_norm_g, d_norm_b, grads_done, ln_g, ln_b, tm, token=None):
    tc, tl = ctx1.shape[0], x1.shape[0]
    T = tc + tl
    rc = min(256, tc)
    tmb = tm // 2
    tmm = 768 if T % 768 == 0 else tm
    rope = _rope_tables(tc, tl)
    cin = jnp.concatenate([c1, c_ctx[None, :], jnp.zeros((SUB - 2, D), f32)], 0)
    if token is not None:
        cin = cin + token[:, 0:1]
    modv = _mod_fwd(cin, w_mod, b_mod)
    modv = [modv[l] for l in range(DEPTH)]
    row = lambda v: v[None, :]

    h = jnp.concatenate([ctx1, x1], 0)
    saved, wp, w_br, w_out = [], [None] * DEPTH, [None] * DEPTH, [None] * DEPTH
    for l in range(DEPTH):
        wp[l], w_br[l], w_out[l] = weights_of(l, h)
        u = _ln_fwd(h, modv[l], tc, tm, f"ln_fwd{l}")
        P = {k: _matmul(u, wp[l][k], "nt", tmm, PROJ_TN[k], D, f"proj{l}{k}") for k in GROUPS}
        qn, kn, vb = _prep_fwd(P["A"], row(q_norm[l]), row(k_norm[l]), rope, tm, f"prep_fwd{l}")
        ya = _attn_fwd(qn, kn, vb, tc, tm, f"attn_fwd{l}")
        yb, hh = _conv_fwd(P["C"], b_conv[l], d_conv_w[l], row(d_conv_b[l]), tc, tl, rc, f"conv_fwd{l}")
        w2p, b2p = _gate_weights(w2[l], gb[l])
        o2, ssave = _gla_fwd(P["G"], w2p, b2p, tc, f"gla_fwd{l}")
        hn = _merge_fwd(h, modv[l], ya, yb, o2, hh, P["M"], row(c_norm[l]), row(d_norm_g[l]), row(d_norm_b[l]),
                        row(ln_g[l]), row(ln_b[l]), w_br[l], w_out[l], tc, tm, f"merge_fwd{l}")
        saved.append((h, u, P, qn, kn, vb, ya, yb, hh, o2, ssave, w2p, b2p))
        h = hn

    dh, lparts = _loss_kernel(h, tgt1, tc, tm, "loss")
    loss = jnp.sum(lparts[:, 0, 0])

    g = {k: [None] * DEPTH for k in ("wp", "q_norm", "k_norm", "b_conv", "w2", "gb", "c_norm", "d_conv_w", "d_conv_b",
                                     "d_norm_g", "d_norm_b", "w_br", "w_out", "ln_g", "ln_b", "modv")}
    for l in reversed(range(DEPTH)):
        h_in, u, P, qn, kn, vb, ya, yb, hh, o2, ssave, w2p, b2p = saved[l]
        dP = {}
        (dh_res, dm_mg, dya, dyb, doc, dhh, dP["M"], br, z, acc, dy, dv5, dvd) = _merge_bwd(
            dh, h_in, modv[l], ya, yb, o2, hh, P["M"], row(c_norm[l]), row(d_norm_g[l]), row(d_norm_b[l]),
            row(ln_g[l]), row(ln_b[l]), w_br[l], w_out[l], tc, tmb, f"merge_bwd{l}")
        g["w_br"][l] = _matmul_tn_batched(br, z, N_CHIPS, f"dwbr{l}")
        g["w_out"][l] = _matmul(acc, dy, "tn", D, D, T, f"dwout{l}")
        v5 = jnp.sum(dv5, 0)
        g["c_norm"][l], g["d_norm_g"][l], g["d_norm_b"][l] = v5[0], v5[1], v5[2]
        vd = jnp.sum(dvd, 0)
        g["ln_g"][l], g["ln_b"][l] = vd[0], vd[1]
        dqn, dkn, dv = _attn_bwd(qn, kn, vb, dya, tc, tm, f"attn_bwd{l}")
        dP["A"], dqk = _prep_bwd(P["A"], dqn, dkn, dv, row(q_norm[l]), row(k_norm[l]), rope, tm, f"prep_bwd{l}")
        dqk = jnp.sum(dqk, 0)
        g["q_norm"][l], g["k_norm"][l] = dqk[0], dqk[1]
        dP["C"], dwb, dwd, dbd = _conv_bwd(P["C"], dyb, dhh, b_conv[l], d_conv_w[l], tc, tl, rc, f"conv_bwd{l}")
        g["b_conv"][l], g["d_conv_w"][l], g["d_conv_b"][l] = dwb, dwd, dbd[0]
        dpg2, dw2p, db2p = _gla_bwd(P["G"], w2p, b2p, ssave, doc, tc, f"gla_bwd{l}")
        dP["G"] = _sum_dirs(dpg2, tm, f"gla_sum{l}")
        dw2p = dw2p[0] + dw2p[1]
        db2p = db2p[0, 0] + db2p[1, 0]
        g["w2"][l] = jnp.stack([dw2p[0:C_RANK, 0:C_KW], dw2p[C_RANK:2 * C_RANK, C_KW:2 * C_KW]])
        g["gb"][l] = jnp.stack([db2p[0:C_KW], db2p[C_KW:2 * C_KW]])
        du = None
        for k in GROUPS:
            du = _matmul(dP[k], wp[l][k], "nn", tmm, D, PROJ_TN[k], f"du{l}{k}", add=du)
        g["wp"][l] = {k: _matmul(dP[k], u, "tn", DWP_TN[k], D, T, f"dwp{l}{k}") for k in GROUPS}
        dh, dm_ln = _ln_bwd(du, h_in, dh_res, modv[l], tc, tm, f"ln_bwd{l}")
        g["modv"][l] = jnp.sum(dm_mg, 0) + jnp.sum(dm_ln, 0)
        tk = grads_done(l, {k: g[k][l] for k in ("wp", "w_br", "w_out")})
        if tk is not None and l > 0:
            modv[l - 1] = modv[l - 1] + tk[:, 0:1]

    dmodv = jnp.stack(g.pop("modv"))
    g["w_mod"], dcin = _mod_bwd(cin, w_mod, dmodv)
    g["b_mod"] = dmodv[:, 0, :] + dmodv[:, 1, :]
    g["c_ctx"] = jnp.sum(dcin, (0, 1))[1]
    return loss, dh[tc:], g


HALF_TL = 256


def _adamw(w, g, m, v, name, tr=128):
    L, R, C = w.shape
    if R % tr == 0:
        grid, spec = (L, R // tr), pl.BlockSpec((None, tr, C), lambda l, i: (l, i, 0))
    elif R * C * 4 <= (1 << 20):
        grid, spec = (L, 1), pl.BlockSpec((None, R, C), lambda l, i: (l, 0, 0))
    else:
        grid, spec = (L, C // HALF_TL), pl.BlockSpec((None, R, HALF_TL), lambda l, i: (l, 0, i))

    def body(w_ref, g_ref, m_ref, v_ref, d_ref, nm_ref, nv_ref):
        gg = g_ref[...]
        nm = B1 * m_ref[...] + (1.0 - B1) * gg
        nv = B2 * v_ref[...] + (1.0 - B2) * (gg * gg)
        m_hat = nm / (1.0 - B1 ** STEP)
        v_hat = nv / (1.0 - B2 ** STEP)
        d_ref[...] = -LR * (m_hat / (jnp.sqrt(v_hat) + AEPS) + WD * w_ref[...])
        nm_ref[...] = nm
        nv_ref[...] = nv

    return pl.pallas_call(
        body, name=name, grid=grid, in_specs=[spec] * 4, out_specs=[spec] * 3,
        out_shape=[jax.ShapeDtypeStruct((L, R, C), f32)] * 3,
        compiler_params=_cparams(("parallel", "parallel")),
    )(w, g, m, v)


MESH = pl.DeviceIdType.MESH
ANY = pl.BlockSpec(memory_space=pl.ANY)
N_CHIPS = 4


def _place():
    x, y, c = lax.axis_index("x"), lax.axis_index("y"), lax.axis_index("c")
    chips = [(1 - x, y), (x, 1 - y), (1 - x, 1 - y)]
    return x, y, c, chips


def _half(ref, c, axis):
    n = ref.shape[axis] // 2
    last = axis in (-1, ref.ndim - 1)
    idx = [slice(None)] * ref.ndim
    idx[axis] = pl.ds(pl.multiple_of(c * n, LANE if last else SUB), n)
    return ref.at[tuple(idx)]


def _half_shape(shape, axis):
    s = list(shape)
    s[axis] //= 2
    return tuple(s)


def _all_gather(arrs, axes, name):
    n = len(arrs)

    def body(*refs):
        ins, outs = refs[:n], refs[n:2 * n]
        send, recv = refs[2 * n:]
        x, y, c, chips = _place()
        me, sib = 2 * x + y, (x, y, 1 - c)

        def copy(a, k, chip_idx, cc, to, src=None):
            blk = _half(outs[a].at[chip_idx], cc, axes[a])
            return pltpu.make_async_remote_copy(src_ref=blk if src is None else src, dst_ref=blk,
                                                send_sem=send.at[7 * a + k], recv_sem=recv.at[7 * a + k],
                                                device_id=to, device_id_type=MESH)

        own = [pltpu.make_async_remote_copy(src_ref=ins[a], dst_ref=outs[a].at[me], send_sem=send.at[7 * a + 6],
                                            recv_sem=recv.at[7 * a + 6], device_id=sib, device_id_type=MESH)
               for a in range(n)]
        first = own + [copy(a, j, me, c, (*chip, c), src=_half(ins[a], c, axes[a]))
                       for a in range(n) for j, chip in enumerate(chips)]
        for cp in first:
            cp.start()
        passed = []
        for a in range(n):
            for j, chip in enumerate(chips):
                k = 2 * chip[0] + chip[1]
                copy(a, j, k, c, sib).wait_recv()
                fwd = copy(a, 3 + j, k, c, sib)
                fwd.start()
                passed.append(fwd)
        for a in range(n):
            own[a].wait_recv()
            for j, chip in enumerate(chips):
                copy(a, 3 + j, 2 * chip[0] + chip[1], 1 - c, sib).wait_recv()
        for cp in first + passed:
            cp.wait_send()

    return pl.pallas_call(
        body, name=name, in_specs=[ANY] * n, out_specs=[ANY] * n,
        out_shape=[jax.ShapeDtypeStruct((N_CHIPS,) + a.shape, a.dtype) for a in arrs],
        scratch_shapes=[pltpu.SemaphoreType.DMA((7 * n,)), pltpu.SemaphoreType.DMA((7 * n,))],
    )(*arrs)


def _sibling_halves(arrs, axes, name):
    n = len(arrs)

    def body(*refs):
        ins, outs = refs[:n], refs[n:2 * n]
        send, recv = refs[2 * n:]
        x, y, c, _ = _place()
        cps = [pltpu.make_async_remote_copy(src_ref=_half(ins[a], 1 - c, axes[a] + 1), dst_ref=outs[a], send_sem=send.at[a],
                                            recv_sem=recv.at[a], device_id=(x, y, 1 - c), device_id_type=MESH)
               for a in range(n)]
        for cp in cps:
            cp.start()
        for cp in cps:
            cp.wait()

    return pl.pallas_call(
        body, name=name, in_specs=[ANY] * n, out_specs=[ANY] * n,
        out_shape=[jax.ShapeDtypeStruct(_half_shape(a.shape, axes[i] + 1), a.dtype) for i, a in enumerate(arrs)],
        scratch_shapes=[pltpu.SemaphoreType.DMA((n,)), pltpu.SemaphoreType.DMA((n,))],
    )(*arrs)


def _add_half(gfull, land, cidx, axis, name, tr=128, out_dtype=bf16):
    _, hr, hc = land.shape
    if axis == 0:
        tr = min(tr, hr)
        nb, blk = hr // tr, (None, tr, hc)
        g_spec = pl.BlockSpec(blk, lambda s, i, cr: (s, cr[0] * nb + i, 0))
        l_spec = pl.BlockSpec(blk, lambda s, i, cr: (s, i, 0))
    else:
        nb, blk = hc // HALF_TL, (None, hr, HALF_TL)
        g_spec = pl.BlockSpec(blk, lambda s, i, cr: (s, 0, cr[0] * nb + i))
        l_spec = pl.BlockSpec(blk, lambda s, i, cr: (s, 0, i))

    def body(c_ref, g_ref, l_ref, o_ref):
        o_ref[...] = (g_ref[...] + l_ref[...]).astype(o_ref.dtype)

    return pl.pallas_call(
        body, name=name,
        grid_spec=pltpu.PrefetchScalarGridSpec(
            num_scalar_prefetch=1, grid=(N_CHIPS, nb), in_specs=[g_spec, l_spec], out_specs=l_spec),
        out_shape=jax.ShapeDtypeStruct((N_CHIPS, hr, hc), out_dtype),
        compiler_params=_cparams(("parallel", "parallel")),
    )(cidx, gfull, land)


def _chip_exchange(arrs, name):
    n = len(arrs)

    def body(*refs):
        ins, outs = refs[:n], refs[n:2 * n]
        send, recv = refs[2 * n:]
        x, y, c, chips = _place()
        me = 2 * x + y
        cps = []
        for a in range(n):
            for j, chip in enumerate(chips):
                k = 2 * chip[0] + chip[1]
                cps.append((pltpu.make_async_remote_copy(
                    src_ref=ins[a].at[k], dst_ref=outs[a].at[me], send_sem=send.at[3 * a + j], recv_sem=recv.at[3 * a + j],
                    device_id=(*chip, c), device_id_type=MESH), a, j, k))
        for cp, *_ in cps:
            cp.start()
        for cp, a, j, k in cps:
            pltpu.make_async_remote_copy(src_ref=ins[a].at[k], dst_ref=outs[a].at[k], send_sem=send.at[3 * a + j],
                                         recv_sem=recv.at[3 * a + j], device_id=(x, y, c), device_id_type=MESH).wait_recv()
        for cp, *_ in cps:
            cp.wait_send()

    return pl.pallas_call(
        body, name=name, in_specs=[ANY] * n, out_specs=[ANY] * n,
        out_shape=[jax.ShapeDtypeStruct(a.shape, a.dtype) for a in arrs],
        scratch_shapes=[pltpu.SemaphoreType.DMA((3 * n,)), pltpu.SemaphoreType.DMA((3 * n,))],
    )(*arrs)


def _sum_chips(land, own, place, axis, layer, into, name, tr=128):
    _, hr, hc = land.shape
    fresh = not hasattr(into, "dtype")
    shape = tuple(into) if fresh else into.shape
    if axis == 0:
        tr = min(tr, hr)
        nb, blk = hr // tr, (tr, hc)
        l_map, m_map = (lambda i, p: (0, i, 0)), (lambda i, p: (p[0], i, 0))
        o_map = lambda i, p: (layer, p[1] * nb + i, 0)
    else:
        nb, blk = hc // HALF_TL, (hr, HALF_TL)
        l_map, m_map = (lambda i, p: (0, 0, i)), (lambda i, p: (p[0], 0, i))
        o_map = lambda i, p: (layer, 0, p[1] * nb + i)

    def body(p_ref, l_ref, o_ref, *rest):
        me = p_ref[0]
        mine = o_ref[...].astype(f32)
        acc = None
        for k in range(N_CHIPS):
            t = jnp.where(me == k, mine, l_ref[k].astype(f32))
            acc = t if acc is None else acc + t
        rest[-1][...] = acc

    return pl.pallas_call(
        body, name=name,
        grid_spec=pltpu.PrefetchScalarGridSpec(
            num_scalar_prefetch=1, grid=(nb,),
            in_specs=[pl.BlockSpec((N_CHIPS,) + blk, l_map), pl.BlockSpec((None,) + blk, m_map)] + ([] if fresh else [ANY]),
            out_specs=pl.BlockSpec((None,) + blk, o_map)),
        out_shape=jax.ShapeDtypeStruct(shape, f32),
        input_output_aliases={} if fresh else {3: 0},
        compiler_params=_cparams(("parallel",)),
    )(place, land, own, *([] if fresh else [into]))


def _sibling_fill(arrs, axes, name):
    n = len(arrs)

    def body(*refs):
        outs = refs[n:2 * n]
        send, recv = refs[2 * n:]
        x, y, c, _ = _place()
        cps = [pltpu.make_async_remote_copy(src_ref=_half(outs[a], c, axes[a] + 1), dst_ref=_half(outs[a], c, axes[a] + 1),
                                            send_sem=send.at[a], recv_sem=recv.at[a], device_id=(x, y, 1 - c),
                                            device_id_type=MESH) for a in range(n)]
        for cp in cps:
            cp.start()
        for a in range(n):
            blk = _half(outs[a], 1 - c, axes[a] + 1)
            pltpu.make_async_remote_copy(src_ref=blk, dst_ref=blk, send_sem=send.at[a], recv_sem=recv.at[a],
                                         device_id=(x, y, 1 - c), device_id_type=MESH).wait_recv()
        for cp in cps:
            cp.wait_send()

    return pl.pallas_call(
        body, name=name, in_specs=[ANY] * n, out_specs=[ANY] * n,
        out_shape=[jax.ShapeDtypeStruct(a.shape, a.dtype) for a in arrs],
        input_output_aliases={a: a for a in range(n)},
        scratch_shapes=[pltpu.SemaphoreType.DMA((n,)), pltpu.SemaphoreType.DMA((n,))],
    )(*arrs)


HBM = pl.BlockSpec(memory_space=pltpu.HBM)
SEM = pl.BlockSpec(memory_space=pltpu.SEMAPHORE)
EFFECT = pltpu.SideEffectType.DATAFLOW_SIDE_EFFECTING
PEERS = 4


def _split_copies(srcs, lands, send, recv, gather):
    x, y, c, chips = _place()
    me = 2 * x + y
    peers = [((*chip, c), 2 * chip[0] + chip[1]) for chip in chips] + ([((x, y, 1 - c), me)] if gather else [])
    out = []
    for a in range(len(srcs)):
        for j, (dev, k) in enumerate(peers):
            src = srcs[a] if gather else srcs[a].at[k]
            sems = dict(send_sem=send.at[PEERS * a + j], recv_sem=recv.at[PEERS * a + j], device_id=dev, device_id_type=MESH)
            out.append((pltpu.make_async_remote_copy(src_ref=src, dst_ref=lands[a].at[me], **sems),
                        pltpu.make_async_remote_copy(src_ref=src, dst_ref=lands[a].at[k], **sems)))
    return out


def _split_start(srcs, gather, after, name):
    n = len(srcs)
    lands = [lax.empty(((N_CHIPS,) + s.shape) if gather else s.shape, s.dtype) for s in srcs]

    def body(*refs):
        send, recv = refs[2 * n + 1], refs[2 * n + 2]
        for start, _ in _split_copies(refs[:n], refs[n:2 * n], send, recv, gather):
            start.start()
        refs[-1][...] = jnp.zeros_like(refs[-1])

    sems = pltpu.SemaphoreType.DMA((PEERS * n,))
    hbm = lambda a: pltpu.with_memory_space_constraint(a, pltpu.HBM)
    out = pl.pallas_call(
        body, name=name,
        out_shape=(sems, sems, *[pltpu.HBM(a.shape, a.dtype) for a in srcs + lands], jax.ShapeDtypeStruct((SUB, LANE), f32)),
        in_specs=[HBM] * (2 * n) + [ANY], out_specs=(SEM, SEM, *[HBM] * (2 * n), pl.BlockSpec(memory_space=pltpu.VMEM)),
        input_output_aliases={i: 2 + i for i in range(2 * n)},
        compiler_params=pltpu.CompilerParams(has_side_effects=EFFECT),
    )(*[hbm(a) for a in srcs + lands], after)
    return out[0], out[1], list(out[2:2 + n]), list(out[2 + n:2 + 2 * n]), out[-1]


def _split_wait(send, recv, srcs, lands, gather, after, name):
    n = len(srcs)

    def body(*refs):
        for start, arrival in _split_copies(refs[:n], refs[n:2 * n], refs[2 * n], refs[2 * n + 1], gather):
            start.wait_send()
            arrival.wait_recv()

    out = pl.pallas_call(
        body, name=name, out_shape=[pltpu.HBM(a.shape, a.dtype) for a in srcs + lands],
        in_specs=[HBM] * (2 * n) + [SEM, SEM, ANY], out_specs=[HBM] * (2 * n),
        input_output_aliases={i: i for i in range(2 * n)},
        compiler_params=pltpu.CompilerParams(has_side_effects=EFFECT),
    )(*srcs, *lands, send, recv, after)
    return list(out[:n]), list(out[n:])


N_DEV = 8


def _all_reduce_small(v, name):
    R = v.shape[0]

    def body(v_ref, o_ref, land_ref, send, recv):
        x, y, c, _ = _place()
        me = 4 * x + 2 * y + c
        land_ref[me] = v_ref[...]
        cps = []
        for m in range(1, N_DEV):
            px, py, pc = [(1 - q) if (m >> s) & 1 else q for q, s in ((x, 2), (y, 1), (c, 0))]
            cps.append((pltpu.make_async_remote_copy(src_ref=v_ref, dst_ref=land_ref.at[me], send_sem=send.at[m - 1],
                                                     recv_sem=recv.at[m - 1], device_id=(px, py, pc), device_id_type=MESH),
                        4 * px + 2 * py + pc, m))
        for cp, *_ in cps:
            cp.start()
        for cp, peer, m in cps:
            pltpu.make_async_remote_copy(src_ref=v_ref, dst_ref=land_ref.at[peer], send_sem=send.at[m - 1],
                                         recv_sem=recv.at[m - 1], device_id=(x, y, c), device_id_type=MESH).wait_recv()
        for cp, *_ in cps:
            cp.wait_send()
        acc = land_ref[0]
        for k in range(1, N_DEV):
            acc = acc + land_ref[k]
        o_ref[...] = acc

    vm = pl.BlockSpec(memory_space=pltpu.VMEM)
    return pl.pallas_call(
        body, name=name, in_specs=[vm], out_specs=vm, out_shape=jax.ShapeDtypeStruct(v.shape, f32),
        scratch_shapes=[pltpu.VMEM((N_DEV, R, LANE), f32), pltpu.SemaphoreType.DMA((N_DEV - 1,)),
                        pltpu.SemaphoreType.DMA((N_DEV - 1,))],
        compiler_params=pltpu.CompilerParams(vmem_limit_bytes=VMEM_LIMIT),
    )(v)


def _pack_small(arrs, mult=2 * SUB):
    flat = jnp.concatenate([a.reshape(-1) for a in arrs])
    rows = -(-flat.shape[0] // (LANE * mult)) * mult
    return jnp.pad(flat, (0, rows * LANE - flat.shape[0])).reshape(rows, LANE)


def _unpack_small(vec, shapes):
    flat, out, o = vec.reshape(-1), [], 0
    for s in shapes:
        n = int(np.prod(s))
        out.append(flat[o:o + n].reshape(s))
        o += n
    return out


REPL_SMALL = ("c_ctx", "b_mod", "q_norm", "k_norm", "c_norm", "d_conv_b", "d_norm_g", "d_norm_b", "ln_g", "ln_b")
SHARD_SMALL = ("b_conv", "c_gate_w2", "c_gate_b", "d_conv_w")
BIG = ("w_mod", "w_in", "w_br", "w_out")
ORDER = ("c_ctx", "w_mod", "b_mod", "w_in", "q_norm", "k_norm", "b_conv", "c_gate_w2", "c_gate_b", "c_norm", "d_conv_w",
         "d_conv_b", "d_norm_g", "d_norm_b", "w_br", "w_out", "ln_g", "ln_b")


def _unshard_last(g4, shard_shape):
    g = g4.reshape((N_CHIPS,) + tuple(shard_shape))
    g = jnp.moveaxis(g, 0, -2)
    return g.reshape(tuple(shard_shape[:-1]) + (N_CHIPS * shard_shape[-1],))


def _pieces_last(full):
    w = full.shape[-1] // N_CHIPS
    g = full.reshape(full.shape[:-1] + (N_CHIPS, w))
    return jnp.moveaxis(g, -2, 0).reshape(N_CHIPS, -1, w)


def kernel(x, c, ctx, c_ctx, w_mod, b_mod, w_in, q_norm, k_norm, b_conv, c_gate_w2, c_gate_b, c_norm, d_conv_w, d_conv_b, d_norm_g, d_norm_b, w_br, w_out, ln_g, ln_b, loss_target, m_c_ctx, m_w_mod, m_b_mod, m_w_in, m_q_norm, m_k_norm, m_b_conv, m_c_gate_w2, m_c_gate_b, m_c_norm, m_d_conv_w, m_d_conv_b, m_d_norm_g, m_d_norm_b, m_w_br, m_w_out, m_ln_g, m_ln_b, v_c_ctx, v_w_mod, v_b_mod, v_w_in, v_q_norm, v_k_norm, v_b_conv, v_c_gate_w2, v_c_gate_b, v_c_norm, v_d_conv_w, v_d_conv_b, v_d_norm_g, v_d_norm_b, v_w_br, v_w_out, v_ln_g, v_ln_b):
    W = dict(c_ctx=c_ctx, w_mod=w_mod, b_mod=b_mod, w_in=w_in, q_norm=q_norm, k_norm=k_norm, b_conv=b_conv,
             c_gate_w2=c_gate_w2, c_gate_b=c_gate_b, c_norm=c_norm, d_conv_w=d_conv_w, d_conv_b=d_conv_b,
             d_norm_g=d_norm_g, d_norm_b=d_norm_b, w_br=w_br, w_out=w_out, ln_g=ln_g, ln_b=ln_b)
    M = dict(c_ctx=m_c_ctx, w_mod=m_w_mod, b_mod=m_b_mod, w_in=m_w_in, q_norm=m_q_norm, k_norm=m_k_norm, b_conv=m_b_conv,
             c_gate_w2=m_c_gate_w2, c_gate_b=m_c_gate_b, c_norm=m_c_norm, d_conv_w=m_d_conv_w, d_conv_b=m_d_conv_b,
             d_norm_g=m_d_norm_g, d_norm_b=m_d_norm_b, w_br=m_w_br, w_out=m_w_out, ln_g=m_ln_g, ln_b=m_ln_b)
    V = dict(c_ctx=v_c_ctx, w_mod=v_w_mod, b_mod=v_b_mod, w_in=v_w_in, q_norm=v_q_norm, k_norm=v_k_norm, b_conv=v_b_conv,
             c_gate_w2=v_c_gate_w2, c_gate_b=v_c_gate_b, c_norm=v_c_norm, d_conv_w=v_d_conv_w, d_conv_b=v_d_conv_b,
             d_norm_g=v_d_norm_g, d_norm_b=v_d_norm_b, w_br=v_w_br, w_out=v_w_out, ln_g=v_ln_g, ln_b=v_ln_b)
    chip = 2 * lax.axis_index("x") + lax.axis_index("y")
    cidx = lax.axis_index("c").astype(jnp.int32).reshape(1)

    place = jnp.stack([chip, lax.axis_index("c")]).astype(jnp.int32)

    AXIS = dict(w_in=1, w_mod=0, w_br=0, w_out=0)
    ex = dict(w_in=lambda a: jnp.swapaxes(a, 1, 2), w_mod=lambda a: a.reshape(1, DEPTH * D, -1),
              w_br=lambda a: a.reshape(DEPTH, 4 * BRW, -1), w_out=lambda a: a)
    Wx, Mx, Vx = ({k: ex[k](P_[k]) for k in BIG} for P_ in (W, M, V))

    LAYER = ("w_in", "w_br", "w_out")
    small_shard = _pack_small([W[k] for k in SHARD_SMALL])
    keys0 = LAYER + ("w_mod",)
    got = _all_gather([Wx[k][0].astype(bf16) for k in keys0] + [small_shard], [AXIS[k] for k in keys0] + [0], "all_gather0")
    smalls = [_unpack_small(got[-1][s], [W[k].shape for k in SHARD_SMALL]) for s in range(N_CHIPS)]
    full = {k: jnp.concatenate([smalls[s][i] for s in range(N_CHIPS)], axis=-1) for i, k in enumerate(SHARD_SMALL)}
    wmod = got[3].reshape(N_CHIPS, DEPTH, D, 3 * D // N_CHIPS)
    ag_send, ag_recv, ag_src, ag_land, ag_token = _split_start([Wx[k][1].astype(bf16) for k in LAYER], True, got[0],
                                                               "all_gather1_start")

    def weights_of(l, h):
        g3 = got[:3] if l == 0 else _split_wait(ag_send, ag_recv, ag_src, ag_land, True, h, "all_gather1_wait")[1]
        return (_group_weights(g3[0]),
                jnp.moveaxis(g3[1].reshape(N_CHIPS, 4, BRW, D // N_CHIPS), 0, 2).reshape(4, BRW, D), g3[2].reshape(D, D))

    red = {k: Wx[k].shape for k in BIG}
    flying = {}

    def pair_sums(l, pieces):
        keys = list(pieces)
        land_a = _sibling_halves([pieces[k] for k in keys], [AXIS[k] for k in keys], f"rs_sibling_halves{l}")
        return keys, [_add_half(pieces[k], la, cidx, AXIS[k], f"rs_pair_sum{l}_{k}") for k, la in zip(keys, land_a)]

    def chip_sums(l, keys, land_b, pair):
        for k, lb, pr in zip(keys, land_b, pair):
            red[k] = _sum_chips(lb, pr, place, AXIS[k], l, red[k], f"rs_chip_sum{l}_{k}")

    def grads_done(l, gl):
        pieces = dict(w_in=_ungroup(gl["wp"]).reshape(N_CHIPS, SHARD, D),
                      w_br=gl["w_br"].reshape(N_CHIPS, 4 * BRW, D // N_CHIPS), w_out=gl["w_out"].reshape(N_CHIPS, D // N_CHIPS, D))
        if l == 0:
            flying[0] = pieces
            return None
        keys, pair = pair_sums(l, pieces)
        send, recv, src, land, token = _split_start(pair, False, jnp.zeros((SUB, LANE), f32), "rs_chip_exchange1_start")
        flying[l] = (keys, send, recv, src, land)
        return token

    loss, gx, g = _local_step(
        x[0], c, ctx[0], loss_target[0], c_ctx, wmod, b_mod, weights_of, q_norm, k_norm, full["b_conv"],
        full["c_gate_w2"], full["c_gate_b"], c_norm, full["d_conv_w"], d_conv_b, d_norm_g, d_norm_b,
        grads_done, ln_g, ln_b, tm=256, token=ag_token)
    g["c_gate_w2"], g["c_gate_b"] = g.pop("w2"), g.pop("gb")
    loss = lax.psum(loss, ("x", "y", "c"))

    keys, send, recv, src, land = flying[1]
    pair, land_b = _split_wait(send, recv, src, land, False, gx, "rs_chip_exchange1_wait")
    chip_sums(1, keys, land_b, pair)
    flying[0]["w_mod"] = g["w_mod"].reshape(N_CHIPS, DEPTH * D, 3 * D // N_CHIPS)
    keys, pair = pair_sums(0, flying[0])
    chip_sums(0, keys, _chip_exchange(pair, "rs_chip_exchange0"), pair)
    red = dict(zip(BIG, _sibling_fill([red[k] for k in BIG], [AXIS[k] for k in BIG], "rs_sibling_fill")))
    g = {k: (jnp.stack(v) if isinstance(v, list) else v) for k, v in g.items() if k not in ("wp", "w_br", "w_out", "w_mod")}

    small_names = REPL_SMALL + SHARD_SMALL
    gs = _all_reduce_small(_pack_small([g[k] for k in small_names]), "all_reduce_small")
    gsm = dict(zip(small_names, _unpack_small(gs, [g[k].shape for k in small_names])))
    for k in SHARD_SMALL:
        wdt = W[k].shape[-1]
        gsm[k] = lax.dynamic_slice_in_dim(gsm[k], chip * wdt, wdt, axis=gsm[k].ndim - 1)

    grad, delta, new_m, new_v = {}, {}, {}, {}
    for k in BIG:
        back = (lambda a: jnp.swapaxes(a, 1, 2)) if k == "w_in" else (lambda a: a.reshape(W[k].shape))
        d_, m_, v_ = _adamw(Wx[k], red[k], Mx[k], Vx[k], f"adamw_{k}")
        grad[k], delta[k], new_m[k], new_v[k] = back(red[k]), back(d_), back(m_), back(v_)
    shapes = [W[k].shape for k in small_names]
    d_, m_, v_ = _adamw(*[_pack_small([P_[k] for k in small_names])[None] for P_ in (W, gsm, M, V)], "adamw_small")
    for k, dd, mm_, vv in zip(small_names, _unpack_small(d_, shapes), _unpack_small(m_, shapes), _unpack_small(v_, shapes)):
        grad[k], delta[k], new_m[k], new_v[k] = gsm[k], dd, mm_, vv

    return (loss, gx[None], *[grad[k] for k in ORDER], *[delta[k] for k in ORDER], *[new_m[k] for k in ORDER],
            *[new_v[k] for k in ORDER])
```

```python
import functools

import jax
import jax.numpy as jnp
import numpy as np
from jax import lax
from jax.experimental import pallas as pl
from jax.experimental.pallas import tpu as pltpu

f32 = jnp.float32
bf16 = jnp.bfloat16

D = 1024
DEPTH = 2
GRID_W = 64
BRW = 512
HD = 128
A_HEADS = 4
C_HEADS = 4
C_KW = 256
C_RANK = 16
C_TAU = 16.0
CH = 64
KB = 3
KD = 31
ALPHA = (2 * DEPTH) ** 0.25
EPS = 1e-6
ROPE_THETA = 10000.0
N_IN = 10784
LR, B1, B2, AEPS, WD, STEP = 0.001, 0.9, 0.999, 1e-08, 0.01, 10

W_M, W_A, W_C, W_G = 4 * D + 4 * BRW, 1024, 5 * BRW, 1152
GROUPS = ("M", "A", "C", "G")
GROUP_W = dict(M=W_M, A=W_A, C=W_C, G=W_G)
M_GA, M_GB, M_GC, M_GD = 4 * D, 4 * D + BRW, 4 * D + 2 * BRW, 4 * D + 3 * BRW
A_K, A_V = 512, 768
G_K, G_V, G_R = 256, 512, 1024
CT = 5 * 128
S_Q, S_GA, S_B, S_C, S_X, S_GB, S_CQ, S_CV, S_GC, S_R, S_DA, S_DG, S_GD, S_MG = (
    0, 1024, 1536, 2048, 2560, 3072, 3584, 4096, 4608, 5120, 5152, 5664, 6176, 6688)

LANE = 128
SUB = 8
VMEM_LIMIT = 56 * 1024 * 1024
CONV_PAD = 16
GLA_SUB = 16
GLA_CLAMP = 60.0


def _cparams(sem, vmem=VMEM_LIMIT):
    return pltpu.CompilerParams(dimension_semantics=sem, vmem_limit_bytes=vmem)


def _dg(a, b, ca, cb):
    return lax.dot_general(a.astype(bf16), b.astype(bf16), (((ca,), (cb,)), ((), ())),
                           preferred_element_type=f32)


@jax.custom_vjp
def mm(a, b):
    return _dg(a, b, 1, 0)


mm.defvjp(lambda a, b: (_dg(a, b, 1, 0), (a, b)),
          lambda r, ct: (_dg(ct, r[1], 1, 1).astype(r[0].dtype), _dg(r[0], ct, 0, 0).astype(r[1].dtype)))


@jax.custom_vjp
def mm_nt(a, b):
    return _dg(a, b, 1, 1)


mm_nt.defvjp(lambda a, b: (_dg(a, b, 1, 1), (a, b)),
             lambda r, ct: (_dg(ct, r[1], 1, 0).astype(r[0].dtype), _dg(ct, r[0], 0, 0).astype(r[1].dtype)))


@jax.custom_vjp
def mm_tn(a, b):
    return _dg(a, b, 0, 0)


mm_tn.defvjp(lambda a, b: (_dg(a, b, 0, 0), (a, b)),
             lambda r, ct: (_dg(r[1], ct, 1, 1).astype(r[0].dtype), _dg(r[0], ct, 1, 0).astype(r[1].dtype)))


def _sigmoid(x):
    return 0.5 * jnp.tanh(0.5 * x) + 0.5


def _silu(x):
    return x * _sigmoid(x)


def _ln(x):
    mu = jnp.mean(x, -1, keepdims=True)
    xc = x - mu
    var = jnp.mean(xc * xc, -1, keepdims=True)
    return xc * lax.rsqrt(var + EPS)


def _rms(x, g):
    return x * lax.rsqrt(jnp.mean(x * x, -1, keepdims=True) + EPS) * g


@jax.custom_vjp
def _rope(x, cos_f, sin_a, sin_b):
    return x * cos_f + pltpu.roll(x, HD - 1, 1) * sin_a + pltpu.roll(x, 1, 1) * sin_b


def _rope_fwd(x, cos_f, sin_a, sin_b):
    return _rope(x, cos_f, sin_a, sin_b), (cos_f, sin_a, sin_b)


def _rope_bwd(r, ct):
    cos_f, sin_a, sin_b = r
    dx = ct * cos_f + pltpu.roll(ct * sin_a, 1, 1) + pltpu.roll(ct * sin_b, HD - 1, 1)
    return dx, jnp.zeros_like(cos_f), jnp.zeros_like(sin_a), jnp.zeros_like(sin_b)


_rope.defvjp(_rope_fwd, _rope_bwd)


def _row_ids(i, tm):
    return i * tm + lax.broadcasted_iota(jnp.int32, (tm, 1), 0)


def _partial_rows(ref, rows):
    n = len(rows)
    for k, r in enumerate(rows):
        ref[k:k + 1, :] = r
    ref[n:SUB, :] = jnp.zeros((SUB - n, ref.shape[-1]), f32)


def _matmul(a, b, mode, tm, tn, tk, name, out_dtype=f32, add=None):
    if mode == "nn":
        (M, K), N = a.shape, b.shape[1]
        a_spec = pl.BlockSpec((tm, tk), lambda j, i, k: (i, k))
        b_spec = pl.BlockSpec((tk, tn), lambda j, i, k: (k, j))
        ca, cb = 1, 0
    elif mode == "nt":
        (M, K), N = a.shape, b.shape[0]
        a_spec = pl.BlockSpec((tm, tk), lambda j, i, k: (i, k))
        b_spec = pl.BlockSpec((tn, tk), lambda j, i, k: (j, k))
        ca, cb = 1, 1
    else:
        (K, M), N = a.shape, b.shape[1]
        a_spec = pl.BlockSpec((tk, tm), lambda j, i, k: (k, i))
        b_spec = pl.BlockSpec((tk, tn), lambda j, i, k: (k, j))
        ca, cb = 0, 0
    assert M % tm == 0 and N % tn == 0 and K % tk == 0, (name, M, N, K, tm, tn, tk)
    nk = K // tk

    o_spec = pl.BlockSpec((tm, tn), lambda j, i, k: (i, j))

    def body(a_ref, b_ref, *rest):
        add_ref = rest[0] if add is not None else None
        o_ref, acc_ref = rest[-2:]
        k = pl.program_id(2)
        part = _dg(a_ref[...], b_ref[...], ca, cb)

        @pl.when(k == 0)
        def _():
            acc_ref[...] = part if add_ref is None else part + add_ref[...]

        @pl.when(k > 0)
        def _():
            acc_ref[...] += part

        @pl.when(k == nk - 1)
        def _():
            o_ref[...] = acc_ref[...].astype(o_ref.dtype)

    return pl.pallas_call(
        body, name=name, grid=(N // tn, M // tm, nk),
        in_specs=[a_spec, b_spec] + ([o_spec] if add is not None else []), out_specs=o_spec,
        out_shape=jax.ShapeDtypeStruct((M, N), out_dtype),
        scratch_shapes=[pltpu.VMEM((tm, tn), f32)],
        compiler_params=_cparams(("parallel", "parallel", "arbitrary")),
    )(a, b, *([add] if add is not None else []))


def _matmul_tn_batched(a, b, ns, name):
    B, K, M = a.shape
    N = b.shape[2] // ns

    def body(a_ref, b_ref, o_ref):
        o_ref[...] = _dg(a_ref[...], b_ref[...], 0, 0)

    return pl.pallas_call(
        body, name=name, grid=(B, ns),
        in_specs=[pl.BlockSpec((None, K, M), lambda i, s: (i, 0, 0)), pl.BlockSpec((None, K, N), lambda i, s: (i, 0, s))],
        out_specs=pl.BlockSpec((None, None, M, N), lambda i, s: (s, i, 0, 0)),
        out_shape=jax.ShapeDtypeStruct((ns, B, M, N), f32),
        compiler_params=_cparams(("parallel", "parallel")),
    )(a, b)


MOD_TN = 768


def _mod_fwd(cin, w_mod, b_mod):
    def body(c_ref, w_ref, b_ref, o_ref):
        o_ref[...] = mm(_silu(c_ref[...]), w_ref[...]) + b_ref[...]

    return pl.pallas_call(
        body, name="mod_fwd", grid=(DEPTH, 3 * D // MOD_TN),
        in_specs=[pl.BlockSpec((SUB, D), lambda l, j: (0, 0)),
                  pl.BlockSpec((None, None, D, MOD_TN), lambda l, j: (j, l, 0, 0)),
                  pl.BlockSpec((None, 1, MOD_TN), lambda l, j: (l, 0, j))],
        out_specs=pl.BlockSpec((None, SUB, MOD_TN), lambda l, j: (l, 0, j)),
        out_shape=jax.ShapeDtypeStruct((DEPTH, SUB, 3 * D), f32),
        compiler_params=_cparams(("parallel", "parallel")),
    )(cin, w_mod, b_mod.reshape(DEPTH, 1, 3 * D))


def _mod_bwd(cin, w_mod, dmodv):
    nj = 3 * D // MOD_TN

    def body(c_ref, w_ref, g_ref, dw_ref, dc_ref):
        _, vjp = jax.vjp(lambda c, w: mm(_silu(c), w), c_ref[...], w_ref[...].astype(f32))
        dc, dw = vjp(g_ref[...])
        dw_ref[...] = dw
        dc_ref[...] = dc

    return pl.pallas_call(
        body, name="mod_bwd", grid=(DEPTH, nj),
        in_specs=[pl.BlockSpec((SUB, D), lambda l, j: (0, 0)),
                  pl.BlockSpec((None, None, D, MOD_TN), lambda l, j: (j, l, 0, 0)),
                  pl.BlockSpec((None, SUB, MOD_TN), lambda l, j: (l, 0, j))],
        out_specs=[pl.BlockSpec((None, None, D, MOD_TN), lambda l, j: (j, l, 0, 0)),
                   pl.BlockSpec((None, None, SUB, D), lambda l, j: (l, j, 0, 0))],
        out_shape=[jax.ShapeDtypeStruct((nj, DEPTH, D, MOD_TN), f32),
                   jax.ShapeDtypeStruct((DEPTH, nj, SUB, D), f32)],
        compiler_params=_cparams(("parallel", "parallel")),
    )(cin, w_mod, dmodv)


def _u_fn(h, m_l, m_c, isctx):
    n = _ln(h)
    shift = jnp.where(isctx, m_c[:, 0:D], m_l[:, 0:D])
    scale = jnp.where(isctx, m_c[:, D:2 * D], m_l[:, D:2 * D])
    return n * (1.0 + scale) + shift


def _ln_fwd(h, modv_l, tc, tm, name):
    T = h.shape[0]

    def body(h_ref, m_ref, u_ref):
        isctx = _row_ids(pl.program_id(0), tm) < tc
        u_ref[...] = _u_fn(h_ref[...], m_ref[0:1, :], m_ref[1:2, :], isctx).astype(bf16)

    return pl.pallas_call(
        body, name=name, grid=(T // tm,),
        in_specs=[pl.BlockSpec((tm, D), lambda i: (i, 0)), pl.BlockSpec((SUB, 3 * D), lambda i: (0, 0))],
        out_specs=pl.BlockSpec((tm, D), lambda i: (i, 0)),
        out_shape=jax.ShapeDtypeStruct((T, D), bf16),
        compiler_params=_cparams(("parallel",)),
    )(h, modv_l)


def _ln_bwd(du, h, dh_res, modv_l, tc, tm, name):
    T = h.shape[0]
    nt = T // tm

    def body(du_ref, h_ref, r_ref, m_ref, dh_ref, dm_ref):
        isctx = _row_ids(pl.program_id(0), tm) < tc
        _, vjp = jax.vjp(lambda h, ml, mc: _u_fn(h, ml, mc, isctx), h_ref[...], m_ref[0:1, :], m_ref[1:2, :])
        dh, dml, dmc = vjp(du_ref[...])
        dh_ref[...] = dh + r_ref[...]
        _partial_rows(dm_ref, [dml, dmc])

    return pl.pallas_call(
        body, name=name, grid=(nt,),
        in_specs=[pl.BlockSpec((tm, D), lambda i: (i, 0)), pl.BlockSpec((tm, D), lambda i: (i, 0)),
                  pl.BlockSpec((tm, D), lambda i: (i, 0)), pl.BlockSpec((SUB, 3 * D), lambda i: (0, 0))],
        out_specs=[pl.BlockSpec((tm, D), lambda i: (i, 0)), pl.BlockSpec((None, SUB, 3 * D), lambda i: (i, 0, 0))],
        out_shape=[jax.ShapeDtypeStruct((T, D), f32), jax.ShapeDtypeStruct((nt, SUB, 3 * D), f32)],
        compiler_params=_cparams(("parallel",)),
    )(du, h, dh_res, modv_l)


def _prep_fn(q, k, qg, kg, cos_f, sin_a, sin_b):
    qs = [_rope(_rms(q[:, HD * i:HD * (i + 1)], qg), cos_f, sin_a, sin_b) for i in range(A_HEADS)]
    ks = [_rope(_rms(k[:, HD * i:HD * (i + 1)], kg), cos_f, sin_a, sin_b) for i in range(A_HEADS // 2)]
    return jnp.concatenate(qs, 1), jnp.concatenate(ks, 1)


def _tok(tm, w, off):
    return pl.BlockSpec((tm, w), lambda i: (i, off // w))


def _vec(w):
    return pl.BlockSpec((1, w), lambda i: (0, 0))


def _prep_fwd(P, qg, kg, rope, tm, name):
    T = P.shape[0]

    def body(q_ref, k_ref, v_ref, qg_ref, kg_ref, c_ref, sa_ref, sb_ref, qn_ref, kn_ref, vb_ref):
        qn, kn = _prep_fn(q_ref[...], k_ref[...], qg_ref[...], kg_ref[...], c_ref[...], sa_ref[...], sb_ref[...])
        qn_ref[...] = qn.astype(bf16)
        kn_ref[...] = kn.astype(bf16)
        vb_ref[...] = v_ref[...].astype(bf16)

    return pl.pallas_call(
        body, name=name, grid=(T // tm,),
        in_specs=[_tok(tm, 512, 0), _tok(tm, 256, A_K), _tok(tm, 256, A_V), _vec(HD), _vec(HD),
                  _tok(tm, HD, 0), _tok(tm, HD, 0), _tok(tm, HD, 0)],
        out_specs=[_tok(tm, 512, 0), _tok(tm, 256, 0), _tok(tm, 256, 0)],
        out_shape=[jax.ShapeDtypeStruct((T, 512), bf16), jax.ShapeDtypeStruct((T, 256), bf16),
                   jax.ShapeDtypeStruct((T, 256), bf16)],
        compiler_params=_cparams(("parallel",)),
    )(P, P, P, qg, kg, *rope)


def _prep_bwd(P, dqn, dkn, dv, qg, kg, rope, tm, name):
    T = P.shape[0]
    nt = T // tm

    def body(q_ref, k_ref, dq_ref, dk_ref, dv_ref, qg_ref, kg_ref, c_ref, sa_ref, sb_ref, o_ref, og_ref):
        tabs = (c_ref[...], sa_ref[...], sb_ref[...])
        _, vjp = jax.vjp(lambda q, k, a, b: _prep_fn(q, k, a, b, *tabs), q_ref[...], k_ref[...], qg_ref[...], kg_ref[...])
        dq, dk, dqg, dkg = vjp((dq_ref[...], dk_ref[...]))
        o_ref[:, 0:A_K] = dq
        o_ref[:, A_K:A_V] = dk
        o_ref[:, A_V:W_A] = dv_ref[...]
        _partial_rows(og_ref, [dqg, dkg])

    return pl.pallas_call(
        body, name=name, grid=(nt,),
        in_specs=[_tok(tm, 512, 0), _tok(tm, 256, A_K), _tok(tm, 512, 0), _tok(tm, 256, 0), _tok(tm, 256, 0),
                  _vec(HD), _vec(HD), _tok(tm, HD, 0), _tok(tm, HD, 0), _tok(tm, HD, 0)],
        out_specs=[_tok(tm, W_A, 0), pl.BlockSpec((None, SUB, HD), lambda i: (i, 0, 0))],
        out_shape=[jax.ShapeDtypeStruct((T, W_A), f32), jax.ShapeDtypeStruct((nt, SUB, HD), f32)],
        compiler_params=_cparams(("parallel",)),
    )(P, P, dqn, dkn, dv, qg, kg, *rope)


def _attn_fn(q, k, v, lim):
    s = mm_nt(q, k) * (HD ** -0.5)
    col = lax.broadcasted_iota(jnp.int32, s.shape, 1)
    s = jnp.where(col < lim, s, -1e30)
    m = lax.stop_gradient(jnp.max(s, -1, keepdims=True))
    e = jnp.exp(s - m)
    p = e * (1.0 / jnp.sum(e, -1, keepdims=True))
    return mm(p, v)


def _attn_fwd(qn, kn, vb, tc, tq, name):
    T = qn.shape[0]

    def body(q_ref, k_ref, v_ref, o_ref):
        lim = jnp.where(pl.program_id(1) * tq < tc, tc, T)
        o_ref[...] = _attn_fn(q_ref[...], k_ref[...], v_ref[...], lim)

    return pl.pallas_call(
        body, name=name, grid=(A_HEADS, T // tq),
        in_specs=[pl.BlockSpec((tq, HD), lambda h, i: (i, h)), pl.BlockSpec((T, HD), lambda h, i: (0, h // 2)),
                  pl.BlockSpec((T, HD), lambda h, i: (0, h // 2))],
        out_specs=pl.BlockSpec((tq, HD), lambda h, i: (i, h)),
        out_shape=jax.ShapeDtypeStruct((T, 512), f32),
        compiler_params=_cparams(("parallel", "parallel")),
    )(qn, kn, vb)


def _attn_bwd(qn, kn, vb, dya, tc, tq, name):
    T = qn.shape[0]

    def body(q_ref, k_ref, v_ref, g_ref, dq_ref, dk_ref, dv_ref):
        first = (pl.program_id(1) == 0) & (pl.program_id(2) == 0)
        lim = jnp.where(pl.program_id(2) * tq < tc, tc, T)
        _, vjp = jax.vjp(lambda q, k, v: _attn_fn(q, k, v, lim), q_ref[...].astype(f32), k_ref[...].astype(f32),
                         v_ref[...].astype(f32))
        dq, dk, dv = vjp(g_ref[...])
        dq_ref[...] = dq

        @pl.when(first)
        def _():
            dk_ref[...] = dk
            dv_ref[...] = dv

        @pl.when(jnp.logical_not(first))
        def _():
            dk_ref[...] += dk
            dv_ref[...] += dv

    qspec = pl.BlockSpec((tq, HD), lambda kv, g, i: (i, 2 * kv + g))
    kspec = pl.BlockSpec((T, HD), lambda kv, g, i: (0, kv))
    return pl.pallas_call(
        body, name=name, grid=(A_HEADS // 2, 2, T // tq),
        in_specs=[qspec, kspec, kspec, qspec], out_specs=[qspec, kspec, kspec],
        out_shape=[jax.ShapeDtypeStruct((T, 512), f32), jax.ShapeDtypeStruct((T, 256), f32),
                   jax.ShapeDtypeStruct((T, 256), f32)],
        compiler_params=_cparams(("parallel", "arbitrary", "arbitrary")),
    )(qn, kn, vb, dya)


def _conv_rows(tc, tl):
    return CONV_PAD + tc + CONV_PAD + tl + CONV_PAD


def _fill_pad(pad_ref, val, tc, tl):
    z = jnp.zeros((CONV_PAD, LANE), f32)
    pad_ref[0:CONV_PAD, :] = z
    pad_ref[CONV_PAD:CONV_PAD + tc, :] = val[0:tc]
    pad_ref[CONV_PAD + tc:2 * CONV_PAD + tc, :] = z
    pad_ref[2 * CONV_PAD + tc:2 * CONV_PAD + tc + tl, :] = val[tc:tc + tl]
    pad_ref[2 * CONV_PAD + tc + tl:3 * CONV_PAD + tc + tl, :] = z


def _conv_apply(pad_ref, w_ref, K, tc, tl, rc, emit, flip=False):
    half = K // 2
    for seg0, off, n in ((0, CONV_PAD, tc), (tc, 2 * CONV_PAD + tc, tl)):
        for r0 in range(0, n, rc):
            acc = None
            for k in range(K):
                sh = (half - k) if flip else (k - half)
                term = pad_ref[pl.ds(off + r0 + sh, rc), :] * w_ref[k:k + 1, :]
                acc = term if acc is None else acc + term
            emit(seg0 + r0, acc)


def _conv_wgrad(pad_ref, dy_ref, K, tc, tl, rc, dw_ref):
    half = K // 2
    for k in range(K):
        acc = jnp.zeros((1, LANE), f32)
        for seg0, off, n in ((0, CONV_PAD, tc), (tc, 2 * CONV_PAD + tc, tl)):
            for r0 in range(0, n, rc):
                acc = acc + jnp.sum(pad_ref[pl.ds(off + r0 + k - half, rc), :] * dy_ref[pl.ds(seg0 + r0, rc), :],
                                    axis=0, keepdims=True)
        dw_ref[k:k + 1, :] = acc


def _col(T, off):
    return pl.BlockSpec((T, LANE), lambda j: (0, off // LANE + j))


def _ctile(T):
    return pl.BlockSpec((T, CT), lambda j: (0, j))


C_B, C_C, C_X, C_A, C_G = (slice(LANE * i, LANE * (i + 1)) for i in range(5))


def _conv_fwd(P, wb, wd, bd, tc, tl, rc, name):
    T = tc + tl

    def body(p_ref, wb_ref, wd_ref, bd_ref, yb_ref, hh_ref, pad_ref):
        _fill_pad(pad_ref, p_ref[:, C_C] * p_ref[:, C_X], tc, tl)

        def emit_b(r0, y):
            yb_ref[pl.ds(r0, rc), :] = y * p_ref[pl.ds(r0, rc), C_B]

        _conv_apply(pad_ref, wb_ref, KB, tc, tl, rc, emit_b)
        _fill_pad(pad_ref, p_ref[:, C_A] * _sigmoid(p_ref[:, C_G]), tc, tl)

        def emit_d(r0, y):
            hh_ref[pl.ds(r0, rc), :] = y + bd_ref[...]

        _conv_apply(pad_ref, wd_ref, KD, tc, tl, rc, emit_d)

    return pl.pallas_call(
        body, name=name, grid=(BRW // LANE,),
        in_specs=[_ctile(T), pl.BlockSpec((KB, LANE), lambda j: (0, j)), pl.BlockSpec((KD, LANE), lambda j: (0, j)),
                  pl.BlockSpec((1, LANE), lambda j: (0, j))],
        out_specs=[_col(T, 0), _col(T, 0)],
        out_shape=[jax.ShapeDtypeStruct((T, BRW), f32), jax.ShapeDtypeStruct((T, BRW), f32)],
        scratch_shapes=[pltpu.VMEM((_conv_rows(tc, tl), LANE), f32)],
        compiler_params=_cparams(("parallel",)),
    )(P, wb, wd, bd)


def _conv_bwd(P, dyb, dhh, wb, wd, tc, tl, rc, name):
    T = tc + tl

    def body(p_ref, dyb_ref, dhh_ref, wb_ref, wd_ref, dp_ref, dwb_ref, dwd_ref, dbd_ref, pad_ref, pad2_ref, tmp_ref):
        _fill_pad(pad_ref, p_ref[:, C_C] * p_ref[:, C_X], tc, tl)

        def emit_cv(r0, y):
            dp_ref[pl.ds(r0, rc), C_B] = y * dyb_ref[pl.ds(r0, rc), :]

        _conv_apply(pad_ref, wb_ref, KB, tc, tl, rc, emit_cv)
        tmp_ref[...] = dyb_ref[...] * p_ref[:, C_B]
        _conv_wgrad(pad_ref, tmp_ref, KB, tc, tl, rc, dwb_ref)
        _fill_pad(pad2_ref, tmp_ref[...], tc, tl)

        def emit_ds(r0, y):
            dp_ref[pl.ds(r0, rc), C_C] = y * p_ref[pl.ds(r0, rc), C_X]
            dp_ref[pl.ds(r0, rc), C_X] = y * p_ref[pl.ds(r0, rc), C_C]

        _conv_apply(pad2_ref, wb_ref, KB, tc, tl, rc, emit_ds, flip=True)
        _fill_pad(pad_ref, p_ref[:, C_A] * _sigmoid(p_ref[:, C_G]), tc, tl)
        _conv_wgrad(pad_ref, dhh_ref, KD, tc, tl, rc, dwd_ref)
        dbd_ref[...] = jnp.sum(dhh_ref[...], axis=0, keepdims=True)
        _fill_pad(pad2_ref, dhh_ref[...], tc, tl)

        def emit_d2(r0, y):
            sg = _sigmoid(p_ref[pl.ds(r0, rc), C_G])
            a = p_ref[pl.ds(r0, rc), C_A]
            dp_ref[pl.ds(r0, rc), C_A] = y * sg
            dp_ref[pl.ds(r0, rc), C_G] = y * a * sg * (1.0 - sg)

        _conv_apply(pad2_ref, wd_ref, KD, tc, tl, rc, emit_d2, flip=True)

    return pl.pallas_call(
        body, name=name, grid=(BRW // LANE,),
        in_specs=[_ctile(T), _col(T, 0), _col(T, 0),
                  pl.BlockSpec((KB, LANE), lambda j: (0, j)), pl.BlockSpec((KD, LANE), lambda j: (0, j))],
        out_specs=[_ctile(T), pl.BlockSpec((KB, LANE), lambda j: (0, j)), pl.BlockSpec((KD, LANE), lambda j: (0, j)),
                   pl.BlockSpec((1, LANE), lambda j: (0, j))],
        out_shape=[jax.ShapeDtypeStruct((T, W_C), f32), jax.ShapeDtypeStruct((KB, BRW), f32),
                   jax.ShapeDtypeStruct((KD, BRW), f32), jax.ShapeDtypeStruct((1, BRW), f32)],
        scratch_shapes=[pltpu.VMEM((_conv_rows(tc, tl), LANE), f32), pltpu.VMEM((_conv_rows(tc, tl), LANE), f32),
                        pltpu.VMEM((T, LANE), f32)],
        compiler_params=_cparams(("parallel",)),
    )(P, dyb, dhh, wb, wd)


def _gla_chunk(q, k, v, r, w2, b2, st, isfwd):
    z = mm(r, w2) + b2
    g = jax.nn.log_sigmoid(z[:, 0:C_KW] if isfwd else z[:, C_KW:2 * C_KW]) / C_TAU
    ri = lax.broadcasted_iota(jnp.int32, (CH, CH), 0)
    ci = lax.broadcasted_iota(jnp.int32, (CH, CH), 1)
    tri = ((ci <= ri) if isfwd else (ci >= ri)).astype(f32)
    cum = jnp.dot(tri, g, preferred_element_type=f32, precision=lax.Precision.HIGHEST)
    last = jnp.sum(g, axis=0, keepdims=True)
    q = q * (C_KW // C_HEADS) ** -0.5
    hv = lax.broadcasted_iota(jnp.int32, (BRW, C_KW), 0) // (BRW // C_HEADS)
    hk = lax.broadcasted_iota(jnp.int32, (BRW, C_KW), 1) // (C_KW // C_HEADS)
    st_new = st * jnp.exp(last) + jnp.where(hv == hk, mm_tn(v, k * jnp.exp(last - cum)), 0.0)
    o = mm_nt(q * jnp.exp(cum), st)
    rowi = lax.broadcasted_iota(jnp.int32, (CH, C_KW), 0)
    srow = lax.broadcasted_iota(jnp.int32, (C_HEADS * CH, C_KW), 0)
    slane = lax.broadcasted_iota(jnp.int32, (C_HEADS * CH, C_KW), 1)
    own_lanes = srow // CH == slane // (C_KW // C_HEADS)
    pos = lax.broadcasted_iota(jnp.int32, (C_HEADS * CH, CH), 0) % CH
    key = lax.broadcasted_iota(jnp.int32, (C_HEADS * CH, CH), 1)
    scores = jnp.zeros((C_HEADS * CH, CH), f32)
    for a in range(CH // GLA_SUB):
        idx = GLA_SUB * a - 1 if isfwd else GLA_SUB * (a + 1)
        ref = jnp.sum(jnp.where(rowi == idx, cum, 0.0), axis=0, keepdims=True)
        qa = q * jnp.exp(jnp.minimum(cum - ref, 0.0))
        ka = k * jnp.exp(jnp.minimum(ref - cum, GLA_CLAMP))
        s = mm_nt(jnp.where(own_lanes, jnp.concatenate([qa] * C_HEADS, axis=0), 0.0), ka)
        scores = scores + jnp.where(pos // GLA_SUB == a, s, 0.0)
    scores = jnp.where((key <= pos) if isfwd else (key >= pos), scores, 0.0)
    vw = BRW // C_HEADS
    o = o + jnp.concatenate([mm(scores[CH * hd:CH * (hd + 1)], v[:, vw * hd:vw * (hd + 1)]) for hd in range(C_HEADS)],
                            axis=1)
    return o, st_new


def _gla_chunk_of(d, n, nc, nch):
    back = jnp.where(n < nc, nc - 1 - n, nch - 1 - (n - nc))
    return jnp.where(d == 0, n, back)


def _gla_fwd(P, w2, b2, tc, name):
    T = P.shape[0]
    nch, nc = T // CH, tc // CH

    back = lambda n: _gla_chunk_of(1, n, nc, nch)

    def body(pf_ref, pb_ref, w_ref, b_ref, of_ref, ob_ref, ssf_ref, ssb_ref, stf_ref, stb_ref):
        @pl.when(pl.program_id(0) == 0)
        def _():
            stf_ref[...] = jnp.zeros_like(stf_ref)
            stb_ref[...] = jnp.zeros_like(stb_ref)

        for p_ref, o_ref, ss_ref, st_ref, isfwd in ((pf_ref, of_ref, ssf_ref, stf_ref, True),
                                                    (pb_ref, ob_ref, ssb_ref, stb_ref, False)):
            st = st_ref[...]
            ss_ref[...] = st
            o, st_new = _gla_chunk(p_ref[:, 0:G_K], p_ref[:, G_K:G_V], p_ref[:, G_V:G_R], p_ref[:, G_R:W_G], w_ref[...],
                                   b_ref[...], st, isfwd)
            o_ref[...] = o
            st_ref[...] = st_new

    sd = jax.ShapeDtypeStruct
    return pl.pallas_call(
        body, name=name, grid=(nch,),
        in_specs=[pl.BlockSpec((CH, W_G), lambda n: (n, 0)), pl.BlockSpec((CH, W_G), lambda n: (back(n), 0)),
                  pl.BlockSpec((LANE, 512), lambda n: (0, 0)), pl.BlockSpec((1, 512), lambda n: (0, 0))],
        out_specs=[pl.BlockSpec((CH, BRW), lambda n: (n, 0)), pl.BlockSpec((CH, BRW), lambda n: (back(n), 0)),
                   pl.BlockSpec((None, BRW, C_KW), lambda n: (n, 0, 0)), pl.BlockSpec((None, BRW, C_KW), lambda n: (n, 0, 0))],
        out_shape=[sd((T, BRW), f32), sd((T, BRW), f32), sd((nch, BRW, C_KW), f32), sd((nch, BRW, C_KW), f32)],
        scratch_shapes=[pltpu.VMEM((BRW, C_KW), f32), pltpu.VMEM((BRW, C_KW), f32)],
        compiler_params=_cparams(("arbitrary",)),
    )(P, P, w2, b2)


def _gla_bwd(P, w2, b2, ssave, doc, tc, name):
    T = P.shape[0]
    nch, nc = T // CH, tc // CH

    fwd_chunk = lambda m: nch - 1 - m
    back_chunk = lambda m: _gla_chunk_of(1, nch - 1 - m, nc, nch)

    def body(pf_ref, pb_ref, w_ref, b_ref, ssf_ref, ssb_ref, gf_ref, gb_ref, dpf_ref, dpb_ref, dw_ref, db_ref,
             dstf_ref, dstb_ref):
        m = pl.program_id(0)

        @pl.when(m == 0)
        def _():
            dstf_ref[...] = jnp.zeros_like(dstf_ref)
            dstb_ref[...] = jnp.zeros_like(dstb_ref)

        dw_sum, db_sum = None, None
        for p_ref, ss_ref, g_ref, dp_ref, dst_ref, isfwd in ((pf_ref, ssf_ref, gf_ref, dpf_ref, dstf_ref, True),
                                                             (pb_ref, ssb_ref, gb_ref, dpb_ref, dstb_ref, False)):
            _, vjp = jax.vjp(lambda q, k, v, r, w, b, st: _gla_chunk(q, k, v, r, w, b, st, isfwd),
                             p_ref[:, 0:G_K], p_ref[:, G_K:G_V], p_ref[:, G_V:G_R], p_ref[:, G_R:W_G], w_ref[...],
                             b_ref[...], ss_ref[...])
            dq, dk, dv, dr, dw, db, dst = vjp((g_ref[...], dst_ref[...]))
            dp_ref[:, 0:G_K] = dq
            dp_ref[:, G_K:G_V] = dk
            dp_ref[:, G_V:G_R] = dv
            dp_ref[:, G_R:W_G] = dr
            dst_ref[...] = dst
            dw_sum = dw if dw_sum is None else dw_sum + dw
            db_sum = db if db_sum is None else db_sum + db

        @pl.when(m == 0)
        def _():
            dw_ref[...] = dw_sum
            _partial_rows(db_ref, [db_sum])

        @pl.when(m > 0)
        def _():
            dw_ref[...] += dw_sum
            db_ref[0:1, :] += db_sum

    ssf, ssb = ssave
    chunk_f = lambda w: pl.BlockSpec((CH, w), lambda m: (fwd_chunk(m), 0))
    chunk_b = lambda w: pl.BlockSpec((CH, w), lambda m: (back_chunk(m), 0))
    state = pl.BlockSpec((None, BRW, C_KW), lambda m: (nch - 1 - m, 0, 0))
    sd = jax.ShapeDtypeStruct
    return pl.pallas_call(
        body, name=name, grid=(nch,),
        in_specs=[chunk_f(W_G), chunk_b(W_G), pl.BlockSpec((LANE, 512), lambda m: (0, 0)), pl.BlockSpec((1, 512), lambda m: (0, 0)),
                  state, state, chunk_f(BRW), chunk_b(BRW)],
        out_specs=[chunk_f(W_G), chunk_b(W_G), pl.BlockSpec((LANE, 512), lambda m: (0, 0)), pl.BlockSpec((SUB, 512), lambda m: (0, 0))],
        out_shape=[sd((T, W_G), f32), sd((T, W_G), f32), sd((LANE, 512), f32), sd((SUB, 512), f32)],
        scratch_shapes=[pltpu.VMEM((BRW, C_KW), f32), pltpu.VMEM((BRW, C_KW), f32)],
        compiler_params=_cparams(("arbitrary",)),
    )(P, P, w2, b2, ssf, ssb, doc, doc)


def _sum_dirs(a, b, tm, name):
    T, W = a.shape

    def body(a_ref, b_ref, o_ref):
        o_ref[...] = a_ref[...] + b_ref[...]

    spec = pl.BlockSpec((tm, W), lambda i: (i, 0))
    return pl.pallas_call(
        body, name=name, grid=(T // tm,), in_specs=[spec, spec], out_specs=spec,
        out_shape=jax.ShapeDtypeStruct((T, W), f32),
        compiler_params=_cparams(("parallel",)),
    )(a, b)


def _merge_fn(h, m_l, m_c, isctx, ya, ga, yb, gb, of, ob, gc, hh, gd, mg, es, ey, cn, dng, dnb, lg, lb, wbr, wout):
    oc = of + ob
    yc = jnp.concatenate([_rms(oc[:, HD * i:HD * (i + 1)], cn[:, HD * i:HD * (i + 1)]) for i in range(C_HEADS)], 1)
    brs = [ya * _silu(ga), yb * _silu(gb), yc * _silu(gc), _silu(_ln(hh) * dng + dnb) * _silu(gd)]
    acc = None
    for i in range(4):
        t = _sigmoid(mg[:, D * i:D * (i + 1)]) * (mm(brs[i], wbr[i]) + es[i])
        acc = t if acc is None else acc + t
    y = mm(acc, wout) + ey
    gate = jnp.where(isctx, m_c[:, 2 * D:3 * D], m_l[:, 2 * D:3 * D])
    hn = _ln(ALPHA * h + gate * y) * lg + lb
    return hn, (brs, acc)


def _merge_specs(tm):
    t = lambda w, off=0: _tok(tm, w, off)
    return [t(D), pl.BlockSpec((SUB, 3 * D), lambda i: (0, 0)),
            t(BRW), t(BRW, M_GA), t(BRW), t(BRW, M_GB),
            t(BRW), t(BRW),
            t(BRW, M_GC), t(BRW), t(BRW, M_GD), t(4 * D, 0),
            _vec(BRW), _vec(BRW), _vec(BRW), _vec(D), _vec(D),
            pl.BlockSpec((4, BRW, D), lambda i: (0, 0, 0)), pl.BlockSpec((D, D), lambda i: (0, 0))]


def _merge_fwd(h, modv_l, ya, yb, o2, hh, P, cn, dng, dnb, lg, lb, wbr, wout, tc, tm, name):
    T = h.shape[0]

    def body(h_ref, m_ref, ya_ref, ga_ref, yb_ref, gb_ref, of_ref, ob_ref, gc_ref, hh_ref, gd_ref, mg_ref,
             cn_ref, dng_ref, dnb_ref, lg_ref, lb_ref, wbr_ref, wout_ref, o_ref):
        isctx = _row_ids(pl.program_id(0), tm) < tc
        zero = jnp.zeros((tm, D), f32)
        hn, _ = _merge_fn(h_ref[...], m_ref[0:1, :], m_ref[1:2, :], isctx, ya_ref[...], ga_ref[...], yb_ref[...],
                          gb_ref[...], of_ref[...], ob_ref[...], gc_ref[...], hh_ref[...], gd_ref[...], mg_ref[...],
                          [zero] * 4, zero, cn_ref[...], dng_ref[...], dnb_ref[...], lg_ref[...], lb_ref[...],
                          [wbr_ref[i] for i in range(4)], wout_ref[...])
        o_ref[...] = hn

    return pl.pallas_call(
        body, name=name, grid=(T // tm,),
        in_specs=_merge_specs(tm), out_specs=_tok(tm, D, 0),
        out_shape=jax.ShapeDtypeStruct((T, D), f32),
        compiler_params=_cparams(("parallel",)),
    )(h, modv_l, ya, P, yb, P, o2[0], o2[1], P, hh, P, P, cn, dng, dnb, lg, lb, wbr, wout)


def _merge_bwd(dhn, h, modv_l, ya, yb, o2, hh, P, cn, dng, dnb, lg, lb, wbr, wout, tc, tm, name):
    T = h.shape[0]
    nt = T // tm

    def body(g_ref, h_ref, m_ref, ya_ref, ga_ref, yb_ref, gb_ref, of_ref, ob_ref, gc_ref, hh_ref, gd_ref, mg_ref,
             cn_ref, dng_ref, dnb_ref, lg_ref, lb_ref, wbr_ref, wout_ref,
             dh_ref, dm_ref, dya_ref, dyb_ref, doc_ref, dhh_ref, dp_ref,
             br_ref, z_ref, acc_ref, dy_ref, dv5_ref, dvd_ref):
        isctx = _row_ids(pl.program_id(0), tm) < tc
        zero = jnp.zeros((tm, D), f32)
        wbr_v = [wbr_ref[i] for i in range(4)]
        wout_v = wout_ref[...]

        def fn(h, ml, mc, ya, ga, yb, gb, oc, gc, hh, gd, mg, e0, e1, e2, e3, ey, cn, dng, dnb, lg, lb):
            return _merge_fn(h, ml, mc, isctx, ya, ga, yb, gb, oc, jnp.zeros_like(oc), gc, hh, gd, mg,
                             [e0, e1, e2, e3], ey, cn, dng, dnb, lg, lb, wbr_v, wout_v)

        _, vjp, (brs, acc) = jax.vjp(
            fn, h_ref[...], m_ref[0:1, :], m_ref[1:2, :], ya_ref[...], ga_ref[...], yb_ref[...], gb_ref[...],
            of_ref[...] + ob_ref[...], gc_ref[...], hh_ref[...], gd_ref[...], mg_ref[...], zero, zero, zero, zero, zero,
            cn_ref[...], dng_ref[...], dnb_ref[...], lg_ref[...], lb_ref[...], has_aux=True)
        (dh, dml, dmc, dya, dga, dyb, dgb, doc, dgc, dhh, dgd, dmg, z0, z1, z2, z3, dy,
         dcn, ddng, ddnb, dlg, dlb) = vjp(g_ref[...])
        dh_ref[...] = dh
        _partial_rows(dm_ref, [dml, dmc])
        dya_ref[...] = dya
        dyb_ref[...] = dyb
        doc_ref[...] = doc
        dhh_ref[...] = dhh
        dp_ref[:, 0:M_GA] = dmg
        dp_ref[:, M_GA:M_GB] = dga
        dp_ref[:, M_GB:M_GC] = dgb
        dp_ref[:, M_GC:M_GD] = dgc
        dp_ref[:, M_GD:W_M] = dgd
        for i, z in enumerate((z0, z1, z2, z3)):
            br_ref[i] = brs[i].astype(bf16)
            z_ref[i] = z.astype(bf16)
        acc_ref[...] = acc.astype(bf16)
        dy_ref[...] = dy.astype(bf16)
        _partial_rows(dv5_ref, [dcn, ddng, ddnb])
        _partial_rows(dvd_ref, [dlg, dlb])

    t = lambda w: _tok(tm, w, 0)
    part = lambda w: pl.BlockSpec((None, SUB, w), lambda i: (i, 0, 0))
    sd = jax.ShapeDtypeStruct
    return pl.pallas_call(
        body, name=name, grid=(nt,),
        in_specs=[t(D)] + _merge_specs(tm),
        out_specs=[t(D), part(3 * D)] + [t(BRW)] * 4 + [t(W_M),
                   pl.BlockSpec((4, tm, BRW), lambda i: (0, i, 0)), pl.BlockSpec((4, tm, D), lambda i: (0, i, 0)),
                   t(D), t(D), part(BRW), part(D)],
        out_shape=[sd((T, D), f32), sd((nt, SUB, 3 * D), f32)] + [sd((T, BRW), f32)] * 4 + [sd((T, W_M), f32),
                   sd((4, T, BRW), bf16), sd((4, T, D), bf16), sd((T, D), bf16), sd((T, D), bf16),
                   sd((nt, SUB, BRW), f32), sd((nt, SUB, D), f32)],
        compiler_params=_cparams(("parallel",)),
    )(dhn, h, modv_l, ya, P, yb, P, o2[0], o2[1], P, hh, P, P, cn, dng, dnb, lg, lb, wbr, wout)


def _loss_kernel(h, tgt, tc, tm, name):
    T = h.shape[0]
    nt = T // tm
    nct = tc // tm

    def body(h_ref, t_ref, d_ref, l_ref):
        i = pl.program_id(0)
        err = h_ref[...] - t_ref[...]
        lat = (i >= nct).astype(f32)
        d_ref[...] = err * (lat / D)
        l_ref[...] = jnp.zeros((SUB, LANE), f32) + lat * 0.5 * jnp.sum(err * err) / D

    return pl.pallas_call(
        body, name=name, grid=(nt,),
        in_specs=[pl.BlockSpec((tm, D), lambda i: (i, 0)),
                  pl.BlockSpec((tm, D), lambda i: (jnp.maximum(i - nct, 0), 0))],
        out_specs=[pl.BlockSpec((tm, D), lambda i: (i, 0)), pl.BlockSpec((None, SUB, LANE), lambda i: (i, 0, 0))],
        out_shape=[jax.ShapeDtypeStruct((T, D), f32), jax.ShapeDtypeStruct((nt, SUB, LANE), f32)],
        compiler_params=_cparams(("parallel",)),
    )(h, tgt)


def _rope_tables(tc, tl):
    t = jnp.arange(tl)
    inv = ROPE_THETA ** (-jnp.arange(0, HD // 2, 2, dtype=f32) / (HD // 2))
    ang = jnp.concatenate([(t // GRID_W).astype(f32)[:, None] * inv, (t % GRID_W).astype(f32)[:, None] * inv], -1)
    cos, sin = jnp.repeat(jnp.cos(ang), 2, axis=1), jnp.repeat(jnp.sin(ang), 2, axis=1)
    even = (jnp.arange(HD) % 2 == 0)[None, :]
    cos_f = jnp.concatenate([jnp.ones((tc, HD), f32), cos], 0)
    sin_a = jnp.concatenate([jnp.zeros((tc, HD), f32), jnp.where(even, -sin, 0.0)], 0)
    sin_b = jnp.concatenate([jnp.zeros((tc, HD), f32), jnp.where(even, 0.0, sin)], 0)
    return cos_f, sin_a, sin_b


N_CHIPS = 4
SHARD = N_IN // N_CHIPS


def _group_ranges():
    conv = [(s0 + LANE * j, LANE) for j in range(BRW // LANE) for s0 in (S_B, S_C, S_X, S_DA, S_DG)]
    return dict(M=[(S_MG, 4 * D), (S_GA, BRW), (S_GB, BRW), (S_GC, BRW), (S_GD, BRW)], A=[(S_Q, W_A)], C=conv,
                G=[(S_CQ, 2 * C_KW), (S_CV, BRW), (S_R, 2 * C_RANK)])


def _group_weights(w4):
    out = {}
    for k, ranges in _group_ranges().items():
        parts = []
        for a, n in ranges:
            while n > 0:
                s, r = divmod(a, SHARD)
                m = min(n, SHARD - r)
                parts.append(w4[s, r:r + m])
                a, n = a + m, n - m
        if k == "G":
            parts.append(jnp.zeros((LANE - 2 * C_RANK, D), w4.dtype))
        out[k] = jnp.concatenate(parts, 0)
    return out


def _ungroup(g):
    secs = []
    for k, ranges in _group_ranges().items():
        off = 0
        for a, n in ranges:
            secs.append((a, g[k][off:off + n]))
            off += n
    return jnp.concatenate([v for _, v in sorted(secs, key=lambda t: t[0])], 0)


PROJ_TN = dict(M=2048, A=1024, C=1280, G=1152)
DWP_TN = dict(M=768, A=1024, C=640, G=1152)


def _gate_weights(w2_l, gb_l):
    w = jnp.zeros((LANE, 2 * C_KW), f32)
    w = w.at[0:C_RANK, 0:C_KW].set(w2_l[0]).at[C_RANK:2 * C_RANK, C_KW:2 * C_KW].set(w2_l[1])
    return w, jnp.concatenate([gb_l[0], gb_l[1]])[None, :]


def _local_step(x1, c1, ctx1, tgt1, c_ctx, w_mod, b_mod, weights_of, q_norm, k_norm, b_conv, w2, gb, c_norm, d_conv_w,
                d_conv_b, d_norm_g, d_norm_b, grads_done, ln_g, ln_b, tm, token=None):
    tc, tl = ctx1.shape[0], x1.shape[0]
    T = tc + tl
    rc = min(256, tc)
    tmb = tm // 2
    tmm = 768 if T % 768 == 0 else tm
    rope = _rope_tables(tc, tl)
    cin = jnp.concatenate([c1, c_ctx[None, :], jnp.zeros((SUB - 2, D), f32)], 0)
    if token is not None:
        cin = cin + token[:, 0:1]
    modv = _mod_fwd(cin, w_mod, b_mod)
    modv = [modv[l] for l in range(DEPTH)]
    row = lambda v: v[None, :]

    h = jnp.concatenate([ctx1, x1], 0)
    saved, wp, w_br, w_out = [], [None] * DEPTH, [None] * DEPTH, [None] * DEPTH
    for l in range(DEPTH):
        wp[l], w_br[l], w_out[l] = weights_of(l, h)
        u = _ln_fwd(h, modv[l], tc, tm, f"ln_fwd{l}")
        P = {k: _matmul(u, wp[l][k], "nt", tmm, PROJ_TN[k], D, f"proj{l}{k}") for k in GROUPS}
        qn, kn, vb = _prep_fwd(P["A"], row(q_norm[l]), row(k_norm[l]), rope, tm, f"prep_fwd{l}")
        ya = _attn_fwd(qn, kn, vb, tc, tm, f"attn_fwd{l}")
        yb, hh = _conv_fwd(P["C"], b_conv[l], d_conv_w[l], row(d_conv_b[l]), tc, tl, rc, f"conv_fwd{l}")
        w2p, b2p = _gate_weights(w2[l], gb[l])
        gla = _gla_fwd(P["G"], w2p, b2p, tc, f"gla_fwd{l}")
        o2, ssave = gla[:2], gla[2:]
        hn = _merge_fwd(h, modv[l], ya, yb, o2, hh, P["M"], row(c_norm[l]), row(d_norm_g[l]), row(d_norm_b[l]),
                        row(ln_g[l]), row(ln_b[l]), w_br[l], w_out[l], tc, tm, f"merge_fwd{l}")
        saved.append((h, u, P, qn, kn, vb, ya, yb, hh, o2, ssave, w2p, b2p))
        h = hn

    dh, lparts = _loss_kernel(h, tgt1, tc, tm, "loss")
    loss = jnp.sum(lparts[:, 0, 0])

    g = {k: [None] * DEPTH for k in ("wp", "q_norm", "k_norm", "b_conv", "w2", "gb", "c_norm", "d_conv_w", "d_conv_b",
                                     "d_norm_g", "d_norm_b", "w_br", "w_out", "ln_g", "ln_b", "modv")}
    for l in reversed(range(DEPTH)):
        h_in, u, P, qn, kn, vb, ya, yb, hh, o2, ssave, w2p, b2p = saved[l]
        dP = {}
        (dh_res, dm_mg, dya, dyb, doc, dhh, dP["M"], br, z, acc, dy, dv5, dvd) = _merge_bwd(
            dh, h_in, modv[l], ya, yb, o2, hh, P["M"], row(c_norm[l]), row(d_norm_g[l]), row(d_norm_b[l]),
            row(ln_g[l]), row(ln_b[l]), w_br[l], w_out[l], tc, tmb, f"merge_bwd{l}")
        g["w_br"][l] = _matmul_tn_batched(br, z, N_CHIPS, f"dwbr{l}")
        g["w_out"][l] = _matmul(acc, dy, "tn", D, D, T, f"dwout{l}")
        v5 = jnp.sum(dv5, 0)
        g["c_norm"][l], g["d_norm_g"][l], g["d_norm_b"][l] = v5[0], v5[1], v5[2]
        vd = jnp.sum(dvd, 0)
        g["ln_g"][l], g["ln_b"][l] = vd[0], vd[1]
        dqn, dkn, dv = _attn_bwd(qn, kn, vb, dya, tc, tm, f"attn_bwd{l}")
        dP["A"], dqk = _prep_bwd(P["A"], dqn, dkn, dv, row(q_norm[l]), row(k_norm[l]), rope, tm, f"prep_bwd{l}")
        dqk = jnp.sum(dqk, 0)
        g["q_norm"][l], g["k_norm"][l] = dqk[0], dqk[1]
        dP["C"], dwb, dwd, dbd = _conv_bwd(P["C"], dyb, dhh, b_conv[l], d_conv_w[l], tc, tl, rc, f"conv_bwd{l}")
        g["b_conv"][l], g["d_conv_w"][l], g["d_conv_b"][l] = dwb, dwd, dbd[0]
        dpf, dpb, dw2p, db2p = _gla_bwd(P["G"], w2p, b2p, ssave, doc, tc, f"gla_bwd{l}")
        dP["G"] = _sum_dirs(dpf, dpb, tm, f"gla_sum{l}")
        db2p = db2p[0]
        g["w2"][l] = jnp.stack([dw2p[0:C_RANK, 0:C_KW], dw2p[C_RANK:2 * C_RANK, C_KW:2 * C_KW]])
        g["gb"][l] = jnp.stack([db2p[0:C_KW], db2p[C_KW:2 * C_KW]])
        du = None
        for k in GROUPS:
            du = _matmul(dP[k], wp[l][k], "nn", tmm, D, PROJ_TN[k], f"du{l}{k}", add=du)
        g["wp"][l] = {k: _matmul(dP[k], u, "tn", DWP_TN[k], D, T, f"dwp{l}{k}") for k in GROUPS}
        dh, dm_ln = _ln_bwd(du, h_in, dh_res, modv[l], tc, tm, f"ln_bwd{l}")
        g["modv"][l] = jnp.sum(dm_mg, 0) + jnp.sum(dm_ln, 0)
        tk = grads_done(l, {k: g[k][l] for k in ("wp", "w_br", "w_out")})
        if tk is not None and l > 0:
            modv[l - 1] = modv[l - 1] + tk[:, 0:1]

    dmodv = jnp.stack(g.pop("modv"))
    g["w_mod"], dcin = _mod_bwd(cin, w_mod, dmodv)
    g["b_mod"] = dmodv[:, 0, :] + dmodv[:, 1, :]
    g["c_ctx"] = jnp.sum(dcin, (0, 1))[1]
    return loss, dh[tc:], g


HALF_TL = 256


def _adamw(w, g, m, v, name, tr=128):
    L, R, C = w.shape
    if R % tr == 0:
        grid, spec = (L, R // tr), pl.BlockSpec((None, tr, C), lambda l, i: (l, i, 0))
    elif R * C * 4 <= (1 << 20):
        grid, spec = (L, 1), pl.BlockSpec((None, R, C), lambda l, i: (l, 0, 0))
    else:
        grid, spec = (L, C // HALF_TL), pl.BlockSpec((None, R, HALF_TL), lambda l, i: (l, 0, i))

    def body(w_ref, g_ref, m_ref, v_ref, d_ref, nm_ref, nv_ref):
        gg = g_ref[...]
        nm = B1 * m_ref[...] + (1.0 - B1) * gg
        nv = B2 * v_ref[...] + (1.0 - B2) * (gg * gg)
        m_hat = nm / (1.0 - B1 ** STEP)
        v_hat = nv / (1.0 - B2 ** STEP)
        d_ref[...] = -LR * (m_hat / (jnp.sqrt(v_hat) + AEPS) + WD * w_ref[...])
        nm_ref[...] = nm
        nv_ref[...] = nv

    return pl.pallas_call(
        body, name=name, grid=grid, in_specs=[spec] * 4, out_specs=[spec] * 3,
        out_shape=[jax.ShapeDtypeStruct((L, R, C), f32)] * 3,
        compiler_params=_cparams(("parallel", "parallel")),
    )(w, g, m, v)


MESH = pl.DeviceIdType.MESH
ANY = pl.BlockSpec(memory_space=pl.ANY)
N_CHIPS = 4


def _place():
    x, y, c = lax.axis_index("x"), lax.axis_index("y"), lax.axis_index("c")
    chips = [(1 - x, y), (x, 1 - y), (1 - x, 1 - y)]
    return x, y, c, chips


def _half(ref, c, axis):
    n = ref.shape[axis] // 2
    last = axis in (-1, ref.ndim - 1)
    idx = [slice(None)] * ref.ndim
    idx[axis] = pl.ds(pl.multiple_of(c * n, LANE if last else SUB), n)
    return ref.at[tuple(idx)]


def _half_shape(shape, axis):
    s = list(shape)
    s[axis] //= 2
    return tuple(s)


def _all_gather(arrs, axes, name):
    n = len(arrs)

    def body(*refs):
        ins, outs = refs[:n], refs[n:2 * n]
        send, recv = refs[2 * n:]
        x, y, c, chips = _place()
        me, sib = 2 * x + y, (x, y, 1 - c)

        def copy(a, k, chip_idx, cc, to, src=None):
            blk = _half(outs[a].at[chip_idx], cc, axes[a])
            return pltpu.make_async_remote_copy(src_ref=blk if src is None else src, dst_ref=blk,
                                                send_sem=send.at[7 * a + k], recv_sem=recv.at[7 * a + k],
                                                device_id=to, device_id_type=MESH)

        own = [pltpu.make_async_remote_copy(src_ref=ins[a], dst_ref=outs[a].at[me], send_sem=send.at[7 * a + 6],
                                            recv_sem=recv.at[7 * a + 6], device_id=sib, device_id_type=MESH)
               for a in range(n)]
        first = own + [copy(a, j, me, c, (*chip, c), src=_half(ins[a], c, axes[a]))
                       for a in range(n) for j, chip in enumerate(chips)]
        for cp in first:
            cp.start()
        passed = []
        for a in range(n):
            for j, chip in enumerate(chips):
                k = 2 * chip[0] + chip[1]
                copy(a, j, k, c, sib).wait_recv()
                fwd = copy(a, 3 + j, k, c, sib)
                fwd.start()
                passed.append(fwd)
        for a in range(n):
            own[a].wait_recv()
            for j, chip in enumerate(chips):
                copy(a, 3 + j, 2 * chip[0] + chip[1], 1 - c, sib).wait_recv()
        for cp in first + passed:
            cp.wait_send()

    return pl.pallas_call(
        body, name=name, in_specs=[ANY] * n, out_specs=[ANY] * n,
        out_shape=[jax.ShapeDtypeStruct((N_CHIPS,) + a.shape, a.dtype) for a in arrs],
        scratch_shapes=[pltpu.SemaphoreType.DMA((7 * n,)), pltpu.SemaphoreType.DMA((7 * n,))],
    )(*arrs)


def _sibling_halves(arrs, axes, name):
    n = len(arrs)

    def body(*refs):
        ins, outs = refs[:n], refs[n:2 * n]
        send, recv = refs[2 * n:]
        x, y, c, _ = _place()
        cps = [pltpu.make_async_remote_copy(src_ref=_half(ins[a], 1 - c, axes[a] + 1), dst_ref=outs[a], send_sem=send.at[a],
                                            recv_sem=recv.at[a], device_id=(x, y, 1 - c), device_id_type=MESH)
               for a in range(n)]
        for cp in cps:
            cp.start()
        for cp in cps:
            cp.wait()

    return pl.pallas_call(
        body, name=name, in_specs=[ANY] * n, out_specs=[ANY] * n,
        out_shape=[jax.ShapeDtypeStruct(_half_shape(a.shape, axes[i] + 1), a.dtype) for i, a in enumerate(arrs)],
        scratch_shapes=[pltpu.SemaphoreType.DMA((n,)), pltpu.SemaphoreType.DMA((n,))],
    )(*arrs)


def _add_half(gfull, land, cidx, axis, name, tr=128, out_dtype=bf16):
    _, hr, hc = land.shape
    if axis == 0:
        tr = min(tr, hr)
        nb, blk = hr // tr, (None, tr, hc)
        g_spec = pl.BlockSpec(blk, lambda s, i, cr: (s, cr[0] * nb + i, 0))
        l_spec = pl.BlockSpec(blk, lambda s, i, cr: (s, i, 0))
    else:
        nb, blk = hc // HALF_TL, (None, hr, HALF_TL)
        g_spec = pl.BlockSpec(blk, lambda s, i, cr: (s, 0, cr[0] * nb + i))
        l_spec = pl.BlockSpec(blk, lambda s, i, cr: (s, 0, i))

    def body(c_ref, g_ref, l_ref, o_ref):
        o_ref[...] = (g_ref[...] + l_ref[...]).astype(o_ref.dtype)

    return pl.pallas_call(
        body, name=name,
        grid_spec=pltpu.PrefetchScalarGridSpec(
            num_scalar_prefetch=1, grid=(N_CHIPS, nb), in_specs=[g_spec, l_spec], out_specs=l_spec),
        out_shape=jax.ShapeDtypeStruct((N_CHIPS, hr, hc), out_dtype),
        compiler_params=_cparams(("parallel", "parallel")),
    )(cidx, gfull, land)


def _chip_exchange(arrs, name):
    n = len(arrs)

    def body(*refs):
        ins, outs = refs[:n], refs[n:2 * n]
        send, recv = refs[2 * n:]
        x, y, c, chips = _place()
        me = 2 * x + y
        cps = []
        for a in range(n):
            for j, chip in enumerate(chips):
                k = 2 * chip[0] + chip[1]
                cps.append((pltpu.make_async_remote_copy(
                    src_ref=ins[a].at[k], dst_ref=outs[a].at[me], send_sem=send.at[3 * a + j], recv_sem=recv.at[3 * a + j],
                    device_id=(*chip, c), device_id_type=MESH), a, j, k))
        for cp, *_ in cps:
            cp.start()
        for cp, a, j, k in cps:
            pltpu.make_async_remote_copy(src_ref=ins[a].at[k], dst_ref=outs[a].at[k], send_sem=send.at[3 * a + j],
                                         recv_sem=recv.at[3 * a + j], device_id=(x, y, c), device_id_type=MESH).wait_recv()
        for cp, *_ in cps:
            cp.wait_send()

    return pl.pallas_call(
        body, name=name, in_specs=[ANY] * n, out_specs=[ANY] * n,
        out_shape=[jax.ShapeDtypeStruct(a.shape, a.dtype) for a in arrs],
        scratch_shapes=[pltpu.SemaphoreType.DMA((3 * n,)), pltpu.SemaphoreType.DMA((3 * n,))],
    )(*arrs)


def _sum_chips(land, own, place, axis, layer, into, name, tr=128):
    _, hr, hc = land.shape
    fresh = not hasattr(into, "dtype")
    shape = tuple(into) if fresh else into.shape
    if axis == 0:
        tr = min(tr, hr)
        nb, blk = hr // tr, (tr, hc)
        l_map, m_map = (lambda i, p: (0, i, 0)), (lambda i, p: (p[0], i, 0))
        o_map = lambda i, p: (layer, p[1] * nb + i, 0)
    else:
        nb, blk = hc // HALF_TL, (hr, HALF_TL)
        l_map, m_map = (lambda i, p: (0, 0, i)), (lambda i, p: (p[0], 0, i))
        o_map = lambda i, p: (layer, 0, p[1] * nb + i)

    def body(p_ref, l_ref, o_ref, *rest):
        me = p_ref[0]
        mine = o_ref[...].astype(f32)
        acc = None
        for k in range(N_CHIPS):
            t = jnp.where(me == k, mine, l_ref[k].astype(f32))
            acc = t if acc is None else acc + t
        rest[-1][...] = acc

    return pl.pallas_call(
        body, name=name,
        grid_spec=pltpu.PrefetchScalarGridSpec(
            num_scalar_prefetch=1, grid=(nb,),
            in_specs=[pl.BlockSpec((N_CHIPS,) + blk, l_map), pl.BlockSpec((None,) + blk, m_map)] + ([] if fresh else [ANY]),
            out_specs=pl.BlockSpec((None,) + blk, o_map)),
        out_shape=jax.ShapeDtypeStruct(shape, f32),
        input_output_aliases={} if fresh else {3: 0},
        compiler_params=_cparams(("parallel",)),
    )(place, land, own, *([] if fresh else [into]))


def _sibling_fill(arrs, axes, name):
    n = len(arrs)

    def body(*refs):
        outs = refs[n:2 * n]
        send, recv = refs[2 * n:]
        x, y, c, _ = _place()
        cps = [pltpu.make_async_remote_copy(src_ref=_half(outs[a], c, axes[a] + 1), dst_ref=_half(outs[a], c, axes[a] + 1),
                                            send_sem=send.at[a], recv_sem=recv.at[a], device_id=(x, y, 1 - c),
                                            device_id_type=MESH) for a in range(n)]
        for cp in cps:
            cp.start()
        for a in range(n):
            blk = _half(outs[a], 1 - c, axes[a] + 1)
            pltpu.make_async_remote_copy(src_ref=blk, dst_ref=blk, send_sem=send.at[a], recv_sem=recv.at[a],
                                         device_id=(x, y, 1 - c), device_id_type=MESH).wait_recv()
        for cp in cps:
            cp.wait_send()

    return pl.pallas_call(
        body, name=name, in_specs=[ANY] * n, out_specs=[ANY] * n,
        out_shape=[jax.ShapeDtypeStruct(a.shape, a.dtype) for a in arrs],
        input_output_aliases={a: a for a in range(n)},
        scratch_shapes=[pltpu.SemaphoreType.DMA((n,)), pltpu.SemaphoreType.DMA((n,))],
    )(*arrs)


HBM = pl.BlockSpec(memory_space=pltpu.HBM)
SEM = pl.BlockSpec(memory_space=pltpu.SEMAPHORE)
EFFECT = pltpu.SideEffectType.DATAFLOW_SIDE_EFFECTING
PEERS = 4


def _split_copies(srcs, lands, send, recv, gather):
    x, y, c, chips = _place()
    me = 2 * x + y
    peers = [((*chip, c), 2 * chip[0] + chip[1]) for chip in chips] + ([((x, y, 1 - c), me)] if gather else [])
    out = []
    for a in range(len(srcs)):
        for j, (dev, k) in enumerate(peers):
            src = srcs[a] if gather else srcs[a].at[k]
            sems = dict(send_sem=send.at[PEERS * a + j], recv_sem=recv.at[PEERS * a + j], device_id=dev, device_id_type=MESH)
            out.append((pltpu.make_async_remote_copy(src_ref=src, dst_ref=lands[a].at[me], **sems),
                        pltpu.make_async_remote_copy(src_ref=src, dst_ref=lands[a].at[k], **sems)))
    return out


def _split_start(srcs, gather, after, name):
    n = len(srcs)
    lands = [lax.empty(((N_CHIPS,) + s.shape) if gather else s.shape, s.dtype) for s in srcs]

    def body(*refs):
        send, recv = refs[2 * n + 1], refs[2 * n + 2]
        for start, _ in _split_copies(refs[:n], refs[n:2 * n], send, recv, gather):
            start.start()
        refs[-1][...] = jnp.zeros_like(refs[-1])

    sems = pltpu.SemaphoreType.DMA((PEERS * n,))
    hbm = lambda a: pltpu.with_memory_space_constraint(a, pltpu.HBM)
    out = pl.pallas_call(
        body, name=name,
        out_shape=(sems, sems, *[pltpu.HBM(a.shape, a.dtype) for a in srcs + lands], jax.ShapeDtypeStruct((SUB, LANE), f32)),
        in_specs=[HBM] * (2 * n) + [ANY], out_specs=(SEM, SEM, *[HBM] * (2 * n), pl.BlockSpec(memory_space=pltpu.VMEM)),
        input_output_aliases={i: 2 + i for i in range(2 * n)},
        compiler_params=pltpu.CompilerParams(has_side_effects=EFFECT),
    )(*[hbm(a) for a in srcs + lands], after)
    return out[0], out[1], list(out[2:2 + n]), list(out[2 + n:2 + 2 * n]), out[-1]


def _split_wait(send, recv, srcs, lands, gather, after, name):
    n = len(srcs)

    def body(*refs):
        for start, arrival in _split_copies(refs[:n], refs[n:2 * n], refs[2 * n], refs[2 * n + 1], gather):
            start.wait_send()
            arrival.wait_recv()

    out = pl.pallas_call(
        body, name=name, out_shape=[pltpu.HBM(a.shape, a.dtype) for a in srcs + lands],
        in_specs=[HBM] * (2 * n) + [SEM, SEM, ANY], out_specs=[HBM] * (2 * n),
        input_output_aliases={i: i for i in range(2 * n)},
        compiler_params=pltpu.CompilerParams(has_side_effects=EFFECT),
    )(*srcs, *lands, send, recv, after)
    return list(out[:n]), list(out[n:])


N_DEV = 8


def _all_reduce_small(v, name):
    R = v.shape[0]

    def body(v_ref, o_ref, land_ref, send, recv):
        x, y, c, _ = _place()
        me = 4 * x + 2 * y + c
        land_ref[me] = v_ref[...]
        cps = []
        for m in range(1, N_DEV):
            px, py, pc = [(1 - q) if (m >> s) & 1 else q for q, s in ((x, 2), (y, 1), (c, 0))]
            cps.append((pltpu.make_async_remote_copy(src_ref=v_ref, dst_ref=land_ref.at[me], send_sem=send.at[m - 1],
                                                     recv_sem=recv.at[m - 1], device_id=(px, py, pc), device_id_type=MESH),
                        4 * px + 2 * py + pc, m))
        for cp, *_ in cps:
            cp.start()
        for cp, peer, m in cps:
            pltpu.make_async_remote_copy(src_ref=v_ref, dst_ref=land_ref.at[peer], send_sem=send.at[m - 1],
                                         recv_sem=recv.at[m - 1], device_id=(x, y, c), device_id_type=MESH).wait_recv()
        for cp, *_ in cps:
            cp.wait_send()
        acc = land_ref[0]
        for k in range(1, N_DEV):
            acc = acc + land_ref[k]
        o_ref[...] = acc

    vm = pl.BlockSpec(memory_space=pltpu.VMEM)
    return pl.pallas_call(
        body, name=name, in_specs=[vm], out_specs=vm, out_shape=jax.ShapeDtypeStruct(v.shape, f32),
        scratch_shapes=[pltpu.VMEM((N_DEV, R, LANE), f32), pltpu.SemaphoreType.DMA((N_DEV - 1,)),
                        pltpu.SemaphoreType.DMA((N_DEV - 1,))],
        compiler_params=pltpu.CompilerParams(vmem_limit_bytes=VMEM_LIMIT),
    )(v)


def _pack_small(arrs, mult=2 * SUB):
    flat = jnp.concatenate([a.reshape(-1) for a in arrs])
    rows = -(-flat.shape[0] // (LANE * mult)) * mult
    return jnp.pad(flat, (0, rows * LANE - flat.shape[0])).reshape(rows, LANE)


def _unpack_small(vec, shapes):
    flat, out, o = vec.reshape(-1), [], 0
    for s in shapes:
        n = int(np.prod(s))
        out.append(flat[o:o + n].reshape(s))
        o += n
    return out


REPL_SMALL = ("c_ctx", "b_mod", "q_norm", "k_norm", "c_norm", "d_conv_b", "d_norm_g", "d_norm_b", "ln_g", "ln_b")
SHARD_SMALL = ("b_conv", "c_gate_w2", "c_gate_b", "d_conv_w")
BIG = ("w_mod", "w_in", "w_br", "w_out")
ORDER = ("c_ctx", "w_mod", "b_mod", "w_in", "q_norm", "k_norm", "b_conv", "c_gate_w2", "c_gate_b", "c_norm", "d_conv_w",
         "d_conv_b", "d_norm_g", "d_norm_b", "w_br", "w_out", "ln_g", "ln_b")


def _unshard_last(g4, shard_shape):
    g = g4.reshape((N_CHIPS,) + tuple(shard_shape))
    g = jnp.moveaxis(g, 0, -2)
    return g.reshape(tuple(shard_shape[:-1]) + (N_CHIPS * shard_shape[-1],))


def _pieces_last(full):
    w = full.shape[-1] // N_CHIPS
    g = full.reshape(full.shape[:-1] + (N_CHIPS, w))
    return jnp.moveaxis(g, -2, 0).reshape(N_CHIPS, -1, w)


def kernel(x, c, ctx, c_ctx, w_mod, b_mod, w_in, q_norm, k_norm, b_conv, c_gate_w2, c_gate_b, c_norm, d_conv_w, d_conv_b, d_norm_g, d_norm_b, w_br, w_out, ln_g, ln_b, loss_target, m_c_ctx, m_w_mod, m_b_mod, m_w_in, m_q_norm, m_k_norm, m_b_conv, m_c_gate_w2, m_c_gate_b, m_c_norm, m_d_conv_w, m_d_conv_b, m_d_norm_g, m_d_norm_b, m_w_br, m_w_out, m_ln_g, m_ln_b, v_c_ctx, v_w_mod, v_b_mod, v_w_in, v_q_norm, v_k_norm, v_b_conv, v_c_gate_w2, v_c_gate_b, v_c_norm, v_d_conv_w, v_d_conv_b, v_d_norm_g, v_d_norm_b, v_w_br, v_w_out, v_ln_g, v_ln_b):
    W = dict(c_ctx=c_ctx, w_mod=w_mod, b_mod=b_mod, w_in=w_in, q_norm=q_norm, k_norm=k_norm, b_conv=b_conv,
             c_gate_w2=c_gate_w2, c_gate_b=c_gate_b, c_norm=c_norm, d_conv_w=d_conv_w, d_conv_b=d_conv_b,
             d_norm_g=d_norm_g, d_norm_b=d_norm_b, w_br=w_br, w_out=w_out, ln_g=ln_g, ln_b=ln_b)
    M = dict(c_ctx=m_c_ctx, w_mod=m_w_mod, b_mod=m_b_mod, w_in=m_w_in, q_norm=m_q_norm, k_norm=m_k_norm, b_conv=m_b_conv,
             c_gate_w2=m_c_gate_w2, c_gate_b=m_c_gate_b, c_norm=m_c_norm, d_conv_w=m_d_conv_w, d_conv_b=m_d_conv_b,
             d_norm_g=m_d_norm_g, d_norm_b=m_d_norm_b, w_br=m_w_br, w_out=m_w_out, ln_g=m_ln_g, ln_b=m_ln_b)
    V = dict(c_ctx=v_c_ctx, w_mod=v_w_mod, b_mod=v_b_mod, w_in=v_w_in, q_norm=v_q_norm, k_norm=v_k_norm, b_conv=v_b_conv,
             c_gate_w2=v_c_gate_w2, c_gate_b=v_c_gate_b, c_norm=v_c_norm, d_conv_w=v_d_conv_w, d_conv_b=v_d_conv_b,
             d_norm_g=v_d_norm_g, d_norm_b=v_d_norm_b, w_br=v_w_br, w_out=v_w_out, ln_g=v_ln_g, ln_b=v_ln_b)
    chip = 2 * lax.axis_index("x") + lax.axis_index("y")
    cidx = lax.axis_index("c").astype(jnp.int32).reshape(1)

    place = jnp.stack([chip, lax.axis_index("c")]).astype(jnp.int32)

    AXIS = dict(w_in=1, w_mod=0, w_br=0, w_out=0)
    ex = dict(w_in=lambda a: jnp.swapaxes(a, 1, 2), w_mod=lambda a: a.reshape(1, DEPTH * D, -1),
              w_br=lambda a: a.reshape(DEPTH, 4 * BRW, -1), w_out=lambda a: a)
    Wx, Mx, Vx = ({k: ex[k](P_[k]) for k in BIG} for P_ in (W, M, V))

    LAYER = ("w_in", "w_br", "w_out")
    small_shard = _pack_small([W[k] for k in SHARD_SMALL])
    keys0 = LAYER + ("w_mod",)
    got = _all_gather([Wx[k][0].astype(bf16) for k in keys0] + [small_shard], [AXIS[k] for k in keys0] + [0], "all_gather0")
    smalls = [_unpack_small(got[-1][s], [W[k].shape for k in SHARD_SMALL]) for s in range(N_CHIPS)]
    full = {k: jnp.concatenate([smalls[s][i] for s in range(N_CHIPS)], axis=-1) for i, k in enumerate(SHARD_SMALL)}
    wmod = got[3].reshape(N_CHIPS, DEPTH, D, 3 * D // N_CHIPS)
    ag_send, ag_recv, ag_src, ag_land, ag_token = _split_start([Wx[k][1].astype(bf16) for k in LAYER], True, got[0],
                                                               "all_gather1_start")

    def weights_of(l, h):
        g3 = got[:3] if l == 0 else _split_wait(ag_send, ag_recv, ag_src, ag_land, True, h, "all_gather1_wait")[1]
        return (_group_weights(g3[0]),
                jnp.moveaxis(g3[1].reshape(N_CHIPS, 4, BRW, D // N_CHIPS), 0, 2).reshape(4, BRW, D), g3[2].reshape(D, D))

    red = {k: Wx[k].shape for k in BIG}
    flying = {}

    def pair_sums(l, pieces):
        keys = list(pieces)
        land_a = _sibling_halves([pieces[k] for k in keys], [AXIS[k] for k in keys], f"rs_sibling_halves{l}")
        return keys, [_add_half(pieces[k], la, cidx, AXIS[k], f"rs_pair_sum{l}_{k}") for k, la in zip(keys, land_a)]

    def chip_sums(l, keys, land_b, pair):
        for k, lb, pr in zip(keys, land_b, pair):
            red[k] = _sum_chips(lb, pr, place, AXIS[k], l, red[k], f"rs_chip_sum{l}_{k}")

    def grads_done(l, gl):
        pieces = dict(w_in=_ungroup(gl["wp"]).reshape(N_CHIPS, SHARD, D),
                      w_br=gl["w_br"].reshape(N_CHIPS, 4 * BRW, D // N_CHIPS), w_out=gl["w_out"].reshape(N_CHIPS, D // N_CHIPS, D))
        if l == 0:
            flying[0] = pieces
            return None
        keys, pair = pair_sums(l, pieces)
        send, recv, src, land, token = _split_start(pair, False, jnp.zeros((SUB, LANE), f32), "rs_chip_exchange1_start")
        flying[l] = (keys, send, recv, src, land)
        return token

    loss, gx, g = _local_step(
        x[0], c, ctx[0], loss_target[0], c_ctx, wmod, b_mod, weights_of, q_norm, k_norm, full["b_conv"],
        full["c_gate_w2"], full["c_gate_b"], c_norm, full["d_conv_w"], d_conv_b, d_norm_g, d_norm_b,
        grads_done, ln_g, ln_b, tm=256, token=ag_token)
    g["c_gate_w2"], g["c_gate_b"] = g.pop("w2"), g.pop("gb")
    loss = lax.psum(loss, ("x", "y", "c"))

    keys, send, recv, src, land = flying[1]
    pair, land_b = _split_wait(send, recv, src, land, False, gx, "rs_chip_exchange1_wait")
    chip_sums(1, keys, land_b, pair)
    flying[0]["w_mod"] = g["w_mod"].reshape(N_CHIPS, DEPTH * D, 3 * D // N_CHIPS)
    keys, pair = pair_sums(0, flying[0])
    chip_sums(0, keys, _chip_exchange(pair, "rs_chip_exchange0"), pair)
    red = dict(zip(BIG, _sibling_fill([red[k] for k in BIG], [AXIS[k] for k in BIG], "rs_sibling_fill")))
    g = {k: (jnp.stack(v) if isinstance(v, list) else v) for k, v in g.items() if k not in ("wp", "w_br", "w_out", "w_mod")}

    small_names = REPL_SMALL + SHARD_SMALL
    gs = _all_reduce_small(_pack_small([g[k] for k in small_names]), "all_reduce_small")
    gsm = dict(zip(small_names, _unpack_small(gs, [g[k].shape for k in small_names])))
    for k in SHARD_SMALL:
        wdt = W[k].shape[-1]
        gsm[k] = lax.dynamic_slice_in_dim(gsm[k], chip * wdt, wdt, axis=gsm[k].ndim - 1)

    grad, delta, new_m, new_v = {}, {}, {}, {}
    for k in BIG:
        back = (lambda a: jnp.swapaxes(a, 1, 2)) if k == "w_in" else (lambda a: a.reshape(W[k].shape))
        d_, m_, v_ = _adamw(Wx[k], red[k], Mx[k], Vx[k], f"adamw_{k}")
        grad[k], delta[k], new_m[k], new_v[k] = back(red[k]), back(d_), back(m_), back(v_)
    shapes = [W[k].shape for k in small_names]
    d_, m_, v_ = _adamw(*[_pack_small([P_[k] for k in small_names])[None] for P_ in (W, gsm, M, V)], "adamw_small")
    for k, dd, mm_, vv in zip(small_names, _unpack_small(d_, shapes), _unpack_small(m_, shapes), _unpack_small(v_, shapes)):
        grad[k], delta[k], new_m[k], new_v[k] = gsm[k], dd, mm_, vv

    return (loss, gx[None], *[grad[k] for k in ORDER], *[delta[k] for k in ORDER], *[new_m[k] for k in ORDER],
            *[new_v[k] for k in ORDER])
```

```python
import functools

import jax
import jax.numpy as jnp
import numpy as np
from jax import lax
from jax.experimental import pallas as pl
from jax.experimental.pallas import tpu as pltpu

f32 = jnp.float32
bf16 = jnp.bfloat16

D = 1024
DEPTH = 2
GRID_W = 64
BRW = 512
HD = 128
A_HEADS = 4
C_HEADS = 4
C_KW = 256
C_RANK = 16
C_TAU = 16.0
CH = 64
KB = 3
KD = 31
ALPHA = (2 * DEPTH) ** 0.25
EPS = 1e-6
ROPE_THETA = 10000.0
N_IN = 10784
LR, B1, B2, AEPS, WD, STEP = 0.001, 0.9, 0.999, 1e-08, 0.01, 10

W_M, W_A, W_C, W_G = 4 * D + 4 * BRW, 1024, 5 * BRW, 1152
GROUPS = ("M", "A", "C", "G")
GROUP_W = dict(M=W_M, A=W_A, C=W_C, G=W_G)
M_GA, M_GB, M_GC, M_GD = 4 * D, 4 * D + BRW, 4 * D + 2 * BRW, 4 * D + 3 * BRW
A_K, A_V = 512, 768
G_K, G_V, G_R = 256, 512, 1024
CT = 5 * 128
S_Q, S_GA, S_B, S_C, S_X, S_GB, S_CQ, S_CV, S_GC, S_R, S_DA, S_DG, S_GD, S_MG = (
    0, 1024, 1536, 2048, 2560, 3072, 3584, 4096, 4608, 5120, 5152, 5664, 6176, 6688)

LANE = 128
SUB = 8
VMEM_LIMIT = 56 * 1024 * 1024
CONV_PAD = 16
GLA_SUB = 16
GLA_CLAMP = 60.0


def _cparams(sem, vmem=VMEM_LIMIT):
    return pltpu.CompilerParams(dimension_semantics=sem, vmem_limit_bytes=vmem)


def _dg(a, b, ca, cb):
    return lax.dot_general(a.astype(bf16), b.astype(bf16), (((ca,), (cb,)), ((), ())),
                           preferred_element_type=f32)


@jax.custom_vjp
def mm(a, b):
    return _dg(a, b, 1, 0)


mm.defvjp(lambda a, b: (_dg(a, b, 1, 0), (a, b)),
          lambda r, ct: (_dg(ct, r[1], 1, 1).astype(r[0].dtype), _dg(r[0], ct, 0, 0).astype(r[1].dtype)))


@jax.custom_vjp
def mm_nt(a, b):
    return _dg(a, b, 1, 1)


mm_nt.defvjp(lambda a, b: (_dg(a, b, 1, 1), (a, b)),
             lambda r, ct: (_dg(ct, r[1], 1, 0).astype(r[0].dtype), _dg(ct, r[0], 0, 0).astype(r[1].dtype)))


@jax.custom_vjp
def mm_tn(a, b):
    return _dg(a, b, 0, 0)


mm_tn.defvjp(lambda a, b: (_dg(a, b, 0, 0), (a, b)),
             lambda r, ct: (_dg(r[1], ct, 1, 1).astype(r[0].dtype), _dg(r[0], ct, 1, 0).astype(r[1].dtype)))


def _sigmoid(x):
    return 0.5 * jnp.tanh(0.5 * x) + 0.5


def _silu(x):
    return x * _sigmoid(x)


def _ln(x):
    mu = jnp.mean(x, -1, keepdims=True)
    xc = x - mu
    var = jnp.mean(xc * xc, -1, keepdims=True)
    return xc * lax.rsqrt(var + EPS)


def _rms(x, g):
    return x * lax.rsqrt(jnp.mean(x * x, -1, keepdims=True) + EPS) * g


@jax.custom_vjp
def _rope(x, cos_f, sin_a, sin_b):
    return x * cos_f + pltpu.roll(x, HD - 1, 1) * sin_a + pltpu.roll(x, 1, 1) * sin_b


def _rope_fwd(x, cos_f, sin_a, sin_b):
    return _rope(x, cos_f, sin_a, sin_b), (cos_f, sin_a, sin_b)


def _rope_bwd(r, ct):
    cos_f, sin_a, sin_b = r
    dx = ct * cos_f + pltpu.roll(ct * sin_a, 1, 1) + pltpu.roll(ct * sin_b, HD - 1, 1)
    return dx, jnp.zeros_like(cos_f), jnp.zeros_like(sin_a), jnp.zeros_like(sin_b)


_rope.defvjp(_rope_fwd, _rope_bwd)


def _row_ids(i, tm):
    return i * tm + lax.broadcasted_iota(jnp.int32, (tm, 1), 0)


def _partial_rows(ref, rows):
    n = len(rows)
    for k, r in enumerate(rows):
        ref[k:k + 1, :] = r
    ref[n:SUB, :] = jnp.zeros((SUB - n, ref.shape[-1]), f32)


def _matmul(a, b, mode, tm, tn, tk, name, out_dtype=f32, add=None):
    sect = a.ndim == 3
    a2 = (a.shape[1], a.shape[0] * a.shape[2]) if sect else a.shape
    if mode == "nn":
        (M, K), N = a2, b.shape[1]
        a_spec = pl.BlockSpec((None, tm, tk), lambda j, i, k: (k, i, 0)) if sect else pl.BlockSpec((tm, tk), lambda j, i, k: (i, k))
        b_spec = pl.BlockSpec((tk, tn), lambda j, i, k: (k, j))
        ca, cb = 1, 0
        assert not sect or tk == a.shape[2]
    elif mode == "nt":
        (M, K), N = a2, b.shape[0]
        assert not sect
        a_spec = pl.BlockSpec((tm, tk), lambda j, i, k: (i, k))
        b_spec = pl.BlockSpec((tn, tk), lambda j, i, k: (j, k))
        ca, cb = 1, 1
    else:
        (K, M), N = a2, b.shape[1]
        a_spec = pl.BlockSpec((None, tk, tm), lambda j, i, k: (i, k, 0)) if sect else pl.BlockSpec((tk, tm), lambda j, i, k: (k, i))
        b_spec = pl.BlockSpec((tk, tn), lambda j, i, k: (k, j))
        ca, cb = 0, 0
        assert not sect or tm == a.shape[2]
    assert M % tm == 0 and N % tn == 0 and K % tk == 0, (name, M, N, K, tm, tn, tk)
    nk = K // tk

    o_spec = pl.BlockSpec((tm, tn), lambda j, i, k: (i, j))

    def body(a_ref, b_ref, *rest):
        add_ref = rest[0] if add is not None else None
        o_ref, acc_ref = rest[-2:]
        k = pl.program_id(2)
        part = _dg(a_ref[...], b_ref[...], ca, cb)

        @pl.when(k == 0)
        def _():
            acc_ref[...] = part if add_ref is None else part + add_ref[...]

        @pl.when(k > 0)
        def _():
            acc_ref[...] += part

        @pl.when(k == nk - 1)
        def _():
            o_ref[...] = acc_ref[...].astype(o_ref.dtype)

    return pl.pallas_call(
        body, name=name, grid=(N // tn, M // tm, nk),
        in_specs=[a_spec, b_spec] + ([o_spec] if add is not None else []), out_specs=o_spec,
        out_shape=jax.ShapeDtypeStruct((M, N), out_dtype),
        scratch_shapes=[pltpu.VMEM((tm, tn), f32)],
        compiler_params=_cparams(("parallel", "parallel", "arbitrary")),
    )(a, b, *([add] if add is not None else []))


def _matmul_tn_batched(a, b, ns, name):
    B, K, M = a.shape
    N = b.shape[2] // ns

    def body(a_ref, b_ref, o_ref):
        o_ref[...] = _dg(a_ref[...], b_ref[...], 0, 0)

    return pl.pallas_call(
        body, name=name, grid=(B, ns),
        in_specs=[pl.BlockSpec((None, K, M), lambda i, s: (i, 0, 0)), pl.BlockSpec((None, K, N), lambda i, s: (i, 0, s))],
        out_specs=pl.BlockSpec((None, None, M, N), lambda i, s: (s, i, 0, 0)),
        out_shape=jax.ShapeDtypeStruct((ns, B, M, N), f32),
        compiler_params=_cparams(("parallel", "parallel")),
    )(a, b)


MOD_TN = 768


def _mod_fwd(cin, w_mod, b_mod):
    def body(c_ref, w_ref, b_ref, o_ref):
        o_ref[...] = mm(_silu(c_ref[...]), w_ref[...]) + b_ref[...]

    return pl.pallas_call(
        body, name="mod_fwd", grid=(DEPTH, 3 * D // MOD_TN),
        in_specs=[pl.BlockSpec((SUB, D), lambda l, j: (0, 0)),
                  pl.BlockSpec((None, None, D, MOD_TN), lambda l, j: (j, l, 0, 0)),
                  pl.BlockSpec((None, 1, MOD_TN), lambda l, j: (l, 0, j))],
        out_specs=pl.BlockSpec((None, SUB, MOD_TN), lambda l, j: (l, 0, j)),
        out_shape=jax.ShapeDtypeStruct((DEPTH, SUB, 3 * D), f32),
        compiler_params=_cparams(("parallel", "parallel")),
    )(cin, w_mod, b_mod.reshape(DEPTH, 1, 3 * D))


def _mod_bwd(cin, w_mod, dmodv):
    nj = 3 * D // MOD_TN

    def body(c_ref, w_ref, g_ref, dw_ref, dc_ref):
        _, vjp = jax.vjp(lambda c, w: mm(_silu(c), w), c_ref[...], w_ref[...].astype(f32))
        dc, dw = vjp(g_ref[...])
        dw_ref[...] = dw
        dc_ref[...] = dc

    return pl.pallas_call(
        body, name="mod_bwd", grid=(DEPTH, nj),
        in_specs=[pl.BlockSpec((SUB, D), lambda l, j: (0, 0)),
                  pl.BlockSpec((None, None, D, MOD_TN), lambda l, j: (j, l, 0, 0)),
                  pl.BlockSpec((None, SUB, MOD_TN), lambda l, j: (l, 0, j))],
        out_specs=[pl.BlockSpec((None, None, D, MOD_TN), lambda l, j: (j, l, 0, 0)),
                   pl.BlockSpec((None, None, SUB, D), lambda l, j: (l, j, 0, 0))],
        out_shape=[jax.ShapeDtypeStruct((nj, DEPTH, D, MOD_TN), f32),
                   jax.ShapeDtypeStruct((DEPTH, nj, SUB, D), f32)],
        compiler_params=_cparams(("parallel", "parallel")),
    )(cin, w_mod, dmodv)


def _u_fn(h, m_l, m_c, isctx):
    n = _ln(h)
    shift = jnp.where(isctx, m_c[:, 0:D], m_l[:, 0:D])
    scale = jnp.where(isctx, m_c[:, D:2 * D], m_l[:, D:2 * D])
    return n * (1.0 + scale) + shift


def _ln_fwd(h, modv_l, tc, tm, name):
    T = h.shape[0]

    def body(h_ref, m_ref, u_ref):
        isctx = _row_ids(pl.program_id(0), tm) < tc
        u_ref[...] = _u_fn(h_ref[...], m_ref[0:1, :], m_ref[1:2, :], isctx).astype(bf16)

    return pl.pallas_call(
        body, name=name, grid=(T // tm,),
        in_specs=[pl.BlockSpec((tm, D), lambda i: (i, 0)), pl.BlockSpec((SUB, 3 * D), lambda i: (0, 0))],
        out_specs=pl.BlockSpec((tm, D), lambda i: (i, 0)),
        out_shape=jax.ShapeDtypeStruct((T, D), bf16),
        compiler_params=_cparams(("parallel",)),
    )(h, modv_l)


def _ln_bwd(du, h, dh_res, modv_l, tc, tm, name):
    T = h.shape[0]
    nt = T // tm

    def body(du_ref, h_ref, r_ref, m_ref, dh_ref, dm_ref):
        isctx = _row_ids(pl.program_id(0), tm) < tc
        _, vjp = jax.vjp(lambda h, ml, mc: _u_fn(h, ml, mc, isctx), h_ref[...], m_ref[0:1, :], m_ref[1:2, :])
        dh, dml, dmc = vjp(du_ref[...])
        dh_ref[...] = dh + r_ref[...]
        _partial_rows(dm_ref, [dml, dmc])

    return pl.pallas_call(
        body, name=name, grid=(nt,),
        in_specs=[pl.BlockSpec((tm, D), lambda i: (i, 0)), pl.BlockSpec((tm, D), lambda i: (i, 0)),
                  pl.BlockSpec((tm, D), lambda i: (i, 0)), pl.BlockSpec((SUB, 3 * D), lambda i: (0, 0))],
        out_specs=[pl.BlockSpec((tm, D), lambda i: (i, 0)), pl.BlockSpec((None, SUB, 3 * D), lambda i: (i, 0, 0))],
        out_shape=[jax.ShapeDtypeStruct((T, D), f32), jax.ShapeDtypeStruct((nt, SUB, 3 * D), f32)],
        compiler_params=_cparams(("parallel",)),
    )(du, h, dh_res, modv_l)


def _prep_fn(q, k, qg, kg, cos_f, sin_a, sin_b):
    qs = [_rope(_rms(q[:, HD * i:HD * (i + 1)], qg), cos_f, sin_a, sin_b) for i in range(A_HEADS)]
    ks = [_rope(_rms(k[:, HD * i:HD * (i + 1)], kg), cos_f, sin_a, sin_b) for i in range(A_HEADS // 2)]
    return jnp.concatenate(qs, 1), jnp.concatenate(ks, 1)


def _tok(tm, w, off):
    return pl.BlockSpec((tm, w), lambda i: (i, off // w))


def _vec(w):
    return pl.BlockSpec((1, w), lambda i: (0, 0))


def _prep_fwd(P, qg, kg, rope, tm, name):
    T = P.shape[0]

    def body(q_ref, k_ref, v_ref, qg_ref, kg_ref, c_ref, sa_ref, sb_ref, qn_ref, kn_ref, vb_ref):
        qn, kn = _prep_fn(q_ref[...], k_ref[...], qg_ref[...], kg_ref[...], c_ref[...], sa_ref[...], sb_ref[...])
        qn_ref[...] = qn.astype(bf16)
        kn_ref[...] = kn.astype(bf16)
        vb_ref[...] = v_ref[...].astype(bf16)

    return pl.pallas_call(
        body, name=name, grid=(T // tm,),
        in_specs=[_tok(tm, 512, 0), _tok(tm, 256, A_K), _tok(tm, 256, A_V), _vec(HD), _vec(HD),
                  _tok(tm, HD, 0), _tok(tm, HD, 0), _tok(tm, HD, 0)],
        out_specs=[_tok(tm, 512, 0), _tok(tm, 256, 0), _tok(tm, 256, 0)],
        out_shape=[jax.ShapeDtypeStruct((T, 512), bf16), jax.ShapeDtypeStruct((T, 256), bf16),
                   jax.ShapeDtypeStruct((T, 256), bf16)],
        compiler_params=_cparams(("parallel",)),
    )(P, P, P, qg, kg, *rope)


def _prep_bwd(P, dqn, dkn, dv, qg, kg, rope, tm, name):
    T = P.shape[0]
    nt = T // tm

    def body(q_ref, k_ref, dq_ref, dk_ref, dv_ref, qg_ref, kg_ref, c_ref, sa_ref, sb_ref, o_ref, og_ref):
        tabs = (c_ref[...], sa_ref[...], sb_ref[...])
        _, vjp = jax.vjp(lambda q, k, a, b: _prep_fn(q, k, a, b, *tabs), q_ref[...], k_ref[...], qg_ref[...], kg_ref[...])
        dq, dk, dqg, dkg = vjp((dq_ref[...], dk_ref[...]))
        o_ref[:, 0:A_K] = dq.astype(bf16)
        o_ref[:, A_K:A_V] = dk.astype(bf16)
        o_ref[:, A_V:W_A] = dv_ref[...].astype(bf16)
        _partial_rows(og_ref, [dqg, dkg])

    return pl.pallas_call(
        body, name=name, grid=(nt,),
        in_specs=[_tok(tm, 512, 0), _tok(tm, 256, A_K), _tok(tm, 512, 0), _tok(tm, 256, 0), _tok(tm, 256, 0),
                  _vec(HD), _vec(HD), _tok(tm, HD, 0), _tok(tm, HD, 0), _tok(tm, HD, 0)],
        out_specs=[_tok(tm, W_A, 0), pl.BlockSpec((None, SUB, HD), lambda i: (i, 0, 0))],
        out_shape=[jax.ShapeDtypeStruct((T, W_A), bf16), jax.ShapeDtypeStruct((nt, SUB, HD), f32)],
        compiler_params=_cparams(("parallel",)),
    )(P, P, dqn, dkn, dv, qg, kg, *rope)


def _attn_fn(q, k, v, lim):
    s = mm_nt(q, k) * (HD ** -0.5)
    col = lax.broadcasted_iota(jnp.int32, s.shape, 1)
    s = jnp.where(col < lim, s, -1e30)
    m = lax.stop_gradient(jnp.max(s, -1, keepdims=True))
    e = jnp.exp(s - m)
    p = e * (1.0 / jnp.sum(e, -1, keepdims=True))
    return mm(p, v)


def _attn_fwd(qn, kn, vb, tc, tq, name):
    T = qn.shape[0]

    def body(q_ref, k_ref, v_ref, o_ref):
        lim = jnp.where(pl.program_id(1) * tq < tc, tc, T)
        o_ref[...] = _attn_fn(q_ref[...], k_ref[...], v_ref[...], lim)

    return pl.pallas_call(
        body, name=name, grid=(A_HEADS, T // tq),
        in_specs=[pl.BlockSpec((tq, HD), lambda h, i: (i, h)), pl.BlockSpec((T, HD), lambda h, i: (0, h // 2)),
                  pl.BlockSpec((T, HD), lambda h, i: (0, h // 2))],
        out_specs=pl.BlockSpec((tq, HD), lambda h, i: (i, h)),
        out_shape=jax.ShapeDtypeStruct((T, 512), f32),
        compiler_params=_cparams(("parallel", "parallel")),
    )(qn, kn, vb)


def _attn_bwd(qn, kn, vb, dya, tc, tq, name):
    T = qn.shape[0]

    def body(q_ref, k_ref, v_ref, g_ref, dq_ref, dk_ref, dv_ref):
        first = (pl.program_id(1) == 0) & (pl.program_id(2) == 0)
        lim = jnp.where(pl.program_id(2) * tq < tc, tc, T)
        _, vjp = jax.vjp(lambda q, k, v: _attn_fn(q, k, v, lim), q_ref[...].astype(f32), k_ref[...].astype(f32),
                         v_ref[...].astype(f32))
        dq, dk, dv = vjp(g_ref[...])
        dq_ref[...] = dq

        @pl.when(first)
        def _():
            dk_ref[...] = dk
            dv_ref[...] = dv

        @pl.when(jnp.logical_not(first))
        def _():
            dk_ref[...] += dk
            dv_ref[...] += dv

    qspec = pl.BlockSpec((tq, HD), lambda kv, g, i: (i, 2 * kv + g))
    kspec = pl.BlockSpec((T, HD), lambda kv, g, i: (0, kv))
    return pl.pallas_call(
        body, name=name, grid=(A_HEADS // 2, 2, T // tq),
        in_specs=[qspec, kspec, kspec, qspec], out_specs=[qspec, kspec, kspec],
        out_shape=[jax.ShapeDtypeStruct((T, 512), f32), jax.ShapeDtypeStruct((T, 256), f32),
                   jax.ShapeDtypeStruct((T, 256), f32)],
        compiler_params=_cparams(("parallel", "arbitrary", "arbitrary")),
    )(qn, kn, vb, dya)


def _conv_rows(tc, tl):
    return CONV_PAD + tc + CONV_PAD + tl + CONV_PAD


def _fill_pad(pad_ref, val, tc, tl):
    z = jnp.zeros((CONV_PAD, LANE), f32)
    pad_ref[0:CONV_PAD, :] = z
    pad_ref[CONV_PAD:CONV_PAD + tc, :] = val[0:tc]
    pad_ref[CONV_PAD + tc:2 * CONV_PAD + tc, :] = z
    pad_ref[2 * CONV_PAD + tc:2 * CONV_PAD + tc + tl, :] = val[tc:tc + tl]
    pad_ref[2 * CONV_PAD + tc + tl:3 * CONV_PAD + tc + tl, :] = z


def _conv_apply(pad_ref, w_ref, K, tc, tl, rc, emit, flip=False):
    half = K // 2
    for seg0, off, n in ((0, CONV_PAD, tc), (tc, 2 * CONV_PAD + tc, tl)):
        for r0 in range(0, n, rc):
            acc = None
            for k in range(K):
                sh = (half - k) if flip else (k - half)
                term = pad_ref[pl.ds(off + r0 + sh, rc), :] * w_ref[k:k + 1, :]
                acc = term if acc is None else acc + term
            emit(seg0 + r0, acc)


def _conv_wgrad(pad_ref, dy_ref, K, tc, tl, rc, dw_ref):
    half = K // 2
    for k in range(K):
        acc = jnp.zeros((1, LANE), f32)
        for seg0, off, n in ((0, CONV_PAD, tc), (tc, 2 * CONV_PAD + tc, tl)):
            for r0 in range(0, n, rc):
                acc = acc + jnp.sum(pad_ref[pl.ds(off + r0 + k - half, rc), :] * dy_ref[pl.ds(seg0 + r0, rc), :],
                                    axis=0, keepdims=True)
        dw_ref[k:k + 1, :] = acc


def _col(T, off):
    return pl.BlockSpec((T, LANE), lambda j: (0, off // LANE + j))


C_B, C_C, C_X, C_A, C_G = range(5)
N_SEC = 5


class _Sections:
    def __init__(self, refs):
        self.refs = refs

    def __getitem__(self, idx):
        rows, sec = idx
        return self.refs[sec][rows, :]

    def __setitem__(self, idx, val):
        rows, sec = idx
        self.refs[sec, rows, :] = val


def _sec_specs(T):
    return [pl.BlockSpec((T, LANE), functools.partial(lambda j, s: (0, s * (BRW // LANE) + j), s=s)) for s in range(N_SEC)]


def _conv_fwd(P, wb, wd, bd, tc, tl, rc, name):
    T = tc + tl

    def body(*refs):
        p_ref = _Sections(refs[:N_SEC])
        wb_ref, wd_ref, bd_ref, yb_ref, hh_ref, pad_ref = refs[N_SEC:]
        _fill_pad(pad_ref, p_ref[:, C_C] * p_ref[:, C_X], tc, tl)

        def emit_b(r0, y):
            yb_ref[pl.ds(r0, rc), :] = y * p_ref[pl.ds(r0, rc), C_B]

        _conv_apply(pad_ref, wb_ref, KB, tc, tl, rc, emit_b)
        _fill_pad(pad_ref, p_ref[:, C_A] * _sigmoid(p_ref[:, C_G]), tc, tl)

        def emit_d(r0, y):
            hh_ref[pl.ds(r0, rc), :] = y + bd_ref[...]

        _conv_apply(pad_ref, wd_ref, KD, tc, tl, rc, emit_d)

    return pl.pallas_call(
        body, name=name, grid=(BRW // LANE,),
        in_specs=_sec_specs(T) + [pl.BlockSpec((KB, LANE), lambda j: (0, j)), pl.BlockSpec((KD, LANE), lambda j: (0, j)),
                                  pl.BlockSpec((1, LANE), lambda j: (0, j))],
        out_specs=[_col(T, 0), _col(T, 0)],
        out_shape=[jax.ShapeDtypeStruct((T, BRW), f32), jax.ShapeDtypeStruct((T, BRW), f32)],
        scratch_shapes=[pltpu.VMEM((_conv_rows(tc, tl), LANE), f32)],
        compiler_params=_cparams(("parallel",)),
    )(*[P] * N_SEC, wb, wd, bd)


def _conv_bwd(P, dyb, dhh, wb, wd, tc, tl, rc, name):
    T = tc + tl

    def body(*refs):
        p_ref = _Sections(refs[:N_SEC])
        dyb_ref, dhh_ref, wb_ref, wd_ref, dp3_ref, dwb_ref, dwd_ref, dbd_ref, pad_ref, pad2_ref, tmp_ref = refs[N_SEC:]
        dp_ref = _Sections(dp3_ref)
        _fill_pad(pad_ref, p_ref[:, C_C] * p_ref[:, C_X], tc, tl)

        def emit_cv(r0, y):
            dp_ref[pl.ds(r0, rc), C_B] = (y * dyb_ref[pl.ds(r0, rc), :]).astype(bf16)

        _conv_apply(pad_ref, wb_ref, KB, tc, tl, rc, emit_cv)
        tmp_ref[...] = dyb_ref[...] * p_ref[:, C_B]
        _conv_wgrad(pad_ref, tmp_ref, KB, tc, tl, rc, dwb_ref)
        _fill_pad(pad2_ref, tmp_ref[...], tc, tl)

        def emit_ds(r0, y):
            dp_ref[pl.ds(r0, rc), C_C] = (y * p_ref[pl.ds(r0, rc), C_X]).astype(bf16)
            dp_ref[pl.ds(r0, rc), C_X] = (y * p_ref[pl.ds(r0, rc), C_C]).astype(bf16)

        _conv_apply(pad2_ref, wb_ref, KB, tc, tl, rc, emit_ds, flip=True)
        _fill_pad(pad_ref, p_ref[:, C_A] * _sigmoid(p_ref[:, C_G]), tc, tl)
        _conv_wgrad(pad_ref, dhh_ref, KD, tc, tl, rc, dwd_ref)
        dbd_ref[...] = jnp.sum(dhh_ref[...], axis=0, keepdims=True)
        _fill_pad(pad2_ref, dhh_ref[...], tc, tl)

        def emit_d2(r0, y):
            sg = _sigmoid(p_ref[pl.ds(r0, rc), C_G])
            a = p_ref[pl.ds(r0, rc), C_A]
            dp_ref[pl.ds(r0, rc), C_A] = (y * sg).astype(bf16)
            dp_ref[pl.ds(r0, rc), C_G] = (y * a * sg * (1.0 - sg)).astype(bf16)

        _conv_apply(pad2_ref, wd_ref, KD, tc, tl, rc, emit_d2, flip=True)

    return pl.pallas_call(
        body, name=name, grid=(BRW // LANE,),
        in_specs=_sec_specs(T) + [_col(T, 0), _col(T, 0),
                                  pl.BlockSpec((KB, LANE), lambda j: (0, j)), pl.BlockSpec((KD, LANE), lambda j: (0, j))],
        out_specs=[pl.BlockSpec((N_SEC, T, LANE), lambda j: (0, 0, j)), pl.BlockSpec((KB, LANE), lambda j: (0, j)),
                   pl.BlockSpec((KD, LANE), lambda j: (0, j)), pl.BlockSpec((1, LANE), lambda j: (0, j))],
        out_shape=[jax.ShapeDtypeStruct((N_SEC, T, BRW), bf16), jax.ShapeDtypeStruct((KB, BRW), f32),
                   jax.ShapeDtypeStruct((KD, BRW), f32), jax.ShapeDtypeStruct((1, BRW), f32)],
        scratch_shapes=[pltpu.VMEM((_conv_rows(tc, tl), LANE), f32), pltpu.VMEM((_conv_rows(tc, tl), LANE), f32),
                        pltpu.VMEM((T, LANE), f32)],
        compiler_params=_cparams(("parallel",)),
    )(*[P] * N_SEC, dyb, dhh, wb, wd)


def _gla_chunk(q, k, v, r, w2, b2, st, isfwd):
    z = mm(r, w2) + b2
    g = jax.nn.log_sigmoid(z[:, 0:C_KW] if isfwd else z[:, C_KW:2 * C_KW]) / C_TAU
    ri = lax.broadcasted_iota(jnp.int32, (CH, CH), 0)
    ci = lax.broadcasted_iota(jnp.int32, (CH, CH), 1)
    tri = ((ci <= ri) if isfwd else (ci >= ri)).astype(f32)
    cum = jnp.dot(tri, g, preferred_element_type=f32, precision=lax.Precision.HIGHEST)
    last = jnp.sum(g, axis=0, keepdims=True)
    q = q * (C_KW // C_HEADS) ** -0.5
    hv = lax.broadcasted_iota(jnp.int32, (BRW, C_KW), 0) // (BRW // C_HEADS)
    hk = lax.broadcasted_iota(jnp.int32, (BRW, C_KW), 1) // (C_KW // C_HEADS)
    st_new = st * jnp.exp(last) + jnp.where(hv == hk, mm_tn(v, k * jnp.exp(last - cum)), 0.0)
    o = mm_nt(q * jnp.exp(cum), st)
    rowi = lax.broadcasted_iota(jnp.int32, (CH, C_KW), 0)
    srow = lax.broadcasted_iota(jnp.int32, (C_HEADS * CH, C_KW), 0)
    slane = lax.broadcasted_iota(jnp.int32, (C_HEADS * CH, C_KW), 1)
    own_lanes = srow // CH == slane // (C_KW // C_HEADS)
    pos = lax.broadcasted_iota(jnp.int32, (C_HEADS * CH, CH), 0) % CH
    key = lax.broadcasted_iota(jnp.int32, (C_HEADS * CH, CH), 1)
    scores = jnp.zeros((C_HEADS * CH, CH), f32)
    for a in range(CH // GLA_SUB):
        idx = GLA_SUB * a - 1 if isfwd else GLA_SUB * (a + 1)
        ref = jnp.sum(jnp.where(rowi == idx, cum, 0.0), axis=0, keepdims=True)
        qa = q * jnp.exp(jnp.minimum(cum - ref, 0.0))
        ka = k * jnp.exp(jnp.minimum(ref - cum, GLA_CLAMP))
        s = mm_nt(jnp.where(own_lanes, jnp.concatenate([qa] * C_HEADS, axis=0), 0.0), ka)
        scores = scores + jnp.where(pos // GLA_SUB == a, s, 0.0)
    scores = jnp.where((key <= pos) if isfwd else (key >= pos), scores, 0.0)
    vw = BRW // C_HEADS
    o = o + jnp.concatenate([mm(scores[CH * hd:CH * (hd + 1)], v[:, vw * hd:vw * (hd + 1)]) for hd in range(C_HEADS)],
                            axis=1)
    return o, st_new


def _gla_chunk_of(d, n, nc, nch):
    back = jnp.where(n < nc, nc - 1 - n, nch - 1 - (n - nc))
    return jnp.where(d == 0, n, back)


def _gla_fwd(P, w2, b2, tc, name):
    T = P.shape[0]
    nch, nc = T // CH, tc // CH

    back = lambda n: _gla_chunk_of(1, n, nc, nch)

    def body(pf_ref, pb_ref, w_ref, b_ref, of_ref, ob_ref, ssf_ref, ssb_ref, stf_ref, stb_ref):
        @pl.when(pl.program_id(0) == 0)
        def _():
            stf_ref[...] = jnp.zeros_like(stf_ref)
            stb_ref[...] = jnp.zeros_like(stb_ref)

        for p_ref, o_ref, ss_ref, st_ref, isfwd in ((pf_ref, of_ref, ssf_ref, stf_ref, True),
                                                    (pb_ref, ob_ref, ssb_ref, stb_ref, False)):
            st = st_ref[...]
            ss_ref[...] = st
            o, st_new = _gla_chunk(p_ref[:, 0:G_K], p_ref[:, G_K:G_V], p_ref[:, G_V:G_R], p_ref[:, G_R:W_G], w_ref[...],
                                   b_ref[...], st, isfwd)
            o_ref[...] = o
            st_ref[...] = st_new

    sd = jax.ShapeDtypeStruct
    return pl.pallas_call(
        body, name=name, grid=(nch,),
        in_specs=[pl.BlockSpec((CH, W_G), lambda n: (n, 0)), pl.BlockSpec((CH, W_G), lambda n: (back(n), 0)),
                  pl.BlockSpec((LANE, 512), lambda n: (0, 0)), pl.BlockSpec((1, 512), lambda n: (0, 0))],
        out_specs=[pl.BlockSpec((CH, BRW), lambda n: (n, 0)), pl.BlockSpec((CH, BRW), lambda n: (back(n), 0)),
                   pl.BlockSpec((None, BRW, C_KW), lambda n: (n, 0, 0)), pl.BlockSpec((None, BRW, C_KW), lambda n: (n, 0, 0))],
        out_shape=[sd((T, BRW), f32), sd((T, BRW), f32), sd((nch, BRW, C_KW), f32), sd((nch, BRW, C_KW), f32)],
        scratch_shapes=[pltpu.VMEM((BRW, C_KW), f32), pltpu.VMEM((BRW, C_KW), f32)],
        compiler_params=_cparams(("arbitrary",)),
    )(P, P, w2, b2)


def _gla_bwd(P, w2, b2, ssave, doc, tc, name):
    T = P.shape[0]
    nch, nc = T // CH, tc // CH

    fwd_chunk = lambda m: nch - 1 - m
    back_chunk = lambda m: _gla_chunk_of(1, nch - 1 - m, nc, nch)

    def body(pf_ref, pb_ref, w_ref, b_ref, ssf_ref, ssb_ref, gf_ref, gb_ref, dpf_ref, dpb_ref, dw_ref, db_ref,
             dstf_ref, dstb_ref):
        m = pl.program_id(0)

        @pl.when(m == 0)
        def _():
            dstf_ref[...] = jnp.zeros_like(dstf_ref)
            dstb_ref[...] = jnp.zeros_like(dstb_ref)

        dw_sum, db_sum = None, None
        for p_ref, ss_ref, g_ref, dp_ref, dst_ref, isfwd in ((pf_ref, ssf_ref, gf_ref, dpf_ref, dstf_ref, True),
                                                             (pb_ref, ssb_ref, gb_ref, dpb_ref, dstb_ref, False)):
            _, vjp = jax.vjp(lambda q, k, v, r, w, b, st: _gla_chunk(q, k, v, r, w, b, st, isfwd),
                             p_ref[:, 0:G_K], p_ref[:, G_K:G_V], p_ref[:, G_V:G_R], p_ref[:, G_R:W_G], w_ref[...],
                             b_ref[...], ss_ref[...])
            dq, dk, dv, dr, dw, db, dst = vjp((g_ref[...], dst_ref[...]))
            dp_ref[:, 0:G_K] = dq
            dp_ref[:, G_K:G_V] = dk
            dp_ref[:, G_V:G_R] = dv
            dp_ref[:, G_R:W_G] = dr
            dst_ref[...] = dst
            dw_sum = dw if dw_sum is None else dw_sum + dw
            db_sum = db if db_sum is None else db_sum + db

        @pl.when(m == 0)
        def _():
            dw_ref[...] = dw_sum
            _partial_rows(db_ref, [db_sum])

        @pl.when(m > 0)
        def _():
            dw_ref[...] += dw_sum
            db_ref[0:1, :] += db_sum

    ssf, ssb = ssave
    chunk_f = lambda w: pl.BlockSpec((CH, w), lambda m: (fwd_chunk(m), 0))
    chunk_b = lambda w: pl.BlockSpec((CH, w), lambda m: (back_chunk(m), 0))
    state = pl.BlockSpec((None, BRW, C_KW), lambda m: (nch - 1 - m, 0, 0))
    sd = jax.ShapeDtypeStruct
    return pl.pallas_call(
        body, name=name, grid=(nch,),
        in_specs=[chunk_f(W_G), chunk_b(W_G), pl.BlockSpec((LANE, 512), lambda m: (0, 0)), pl.BlockSpec((1, 512), lambda m: (0, 0)),
                  state, state, chunk_f(BRW), chunk_b(BRW)],
        out_specs=[chunk_f(W_G), chunk_b(W_G), pl.BlockSpec((LANE, 512), lambda m: (0, 0)), pl.BlockSpec((SUB, 512), lambda m: (0, 0))],
        out_shape=[sd((T, W_G), f32), sd((T, W_G), f32), sd((LANE, 512), f32), sd((SUB, 512), f32)],
        scratch_shapes=[pltpu.VMEM((BRW, C_KW), f32), pltpu.VMEM((BRW, C_KW), f32)],
        compiler_params=_cparams(("arbitrary",)),
    )(P, P, w2, b2, ssf, ssb, doc, doc)


def _sum_dirs(a, b, tm, name):
    T, W = a.shape

    def body(a_ref, b_ref, o_ref):
        o_ref[...] = (a_ref[...] + b_ref[...]).astype(bf16)

    spec = pl.BlockSpec((tm, W), lambda i: (i, 0))
    return pl.pallas_call(
        body, name=name, grid=(T // tm,), in_specs=[spec, spec], out_specs=spec,
        out_shape=jax.ShapeDtypeStruct((T, W), bf16),
        compiler_params=_cparams(("parallel",)),
    )(a, b)


def _merge_fn(h, m_l, m_c, isctx, ya, ga, yb, gb, of, ob, gc, hh, gd, mg, es, ey, cn, dng, dnb, lg, lb, wbr, wout):
    oc = of + ob
    yc = jnp.concatenate([_rms(oc[:, HD * i:HD * (i + 1)], cn[:, HD * i:HD * (i + 1)]) for i in range(C_HEADS)], 1)
    brs = [ya * _silu(ga), yb * _silu(gb), yc * _silu(gc), _silu(_ln(hh) * dng + dnb) * _silu(gd)]
    acc = None
    for i in range(4):
        t = _sigmoid(mg[:, D * i:D * (i + 1)]) * (mm(brs[i], wbr[i]) + es[i])
        acc = t if acc is None else acc + t
    y = mm(acc, wout) + ey
    gate = jnp.where(isctx, m_c[:, 2 * D:3 * D], m_l[:, 2 * D:3 * D])
    hn = _ln(ALPHA * h + gate * y) * lg + lb
    return hn, (brs, acc)


def _merge_specs(tm):
    t = lambda w, off=0: _tok(tm, w, off)
    return [t(D), pl.BlockSpec((SUB, 3 * D), lambda i: (0, 0)),
            t(BRW), t(BRW, M_GA), t(BRW), t(BRW, M_GB),
            t(BRW), t(BRW),
            t(BRW, M_GC), t(BRW), t(BRW, M_GD), t(4 * D, 0),
            _vec(BRW), _vec(BRW), _vec(BRW), _vec(D), _vec(D),
            pl.BlockSpec((4, BRW, D), lambda i: (0, 0, 0)), pl.BlockSpec((D, D), lambda i: (0, 0))]


def _merge_fwd(h, modv_l, ya, yb, o2, hh, P, cn, dng, dnb, lg, lb, wbr, wout, tc, tm, name):
    T = h.shape[0]

    def body(h_ref, m_ref, ya_ref, ga_ref, yb_ref, gb_ref, of_ref, ob_ref, gc_ref, hh_ref, gd_ref, mg_ref,
             cn_ref, dng_ref, dnb_ref, lg_ref, lb_ref, wbr_ref, wout_ref, o_ref):
        isctx = _row_ids(pl.program_id(0), tm) < tc
        zero = jnp.zeros((tm, D), f32)
        hn, _ = _merge_fn(h_ref[...], m_ref[0:1, :], m_ref[1:2, :], isctx, ya_ref[...], ga_ref[...], yb_ref[...],
                          gb_ref[...], of_ref[...], ob_ref[...], gc_ref[...], hh_ref[...], gd_ref[...], mg_ref[...],
                          [zero] * 4, zero, cn_ref[...], dng_ref[...], dnb_ref[...], lg_ref[...], lb_ref[...],
                          [wbr_ref[i] for i in range(4)], wout_ref[...])
        o_ref[...] = hn

    return pl.pallas_call(
        body, name=name, grid=(T // tm,),
        in_specs=_merge_specs(tm), out_specs=_tok(tm, D, 0),
        out_shape=jax.ShapeDtypeStruct((T, D), f32),
        compiler_params=_cparams(("parallel",)),
    )(h, modv_l, ya, P, yb, P, o2[0], o2[1], P, hh, P, P, cn, dng, dnb, lg, lb, wbr, wout)


def _merge_bwd(dhn, h, modv_l, ya, yb, o2, hh, P, cn, dng, dnb, lg, lb, wbr, wout, tc, tm, name):
    T = h.shape[0]
    nt = T // tm

    def body(g_ref, h_ref, m_ref, ya_ref, ga_ref, yb_ref, gb_ref, of_ref, ob_ref, gc_ref, hh_ref, gd_ref, mg_ref,
             cn_ref, dng_ref, dnb_ref, lg_ref, lb_ref, wbr_ref, wout_ref,
             dh_ref, dm_ref, dya_ref, dyb_ref, doc_ref, dhh_ref, dp_ref,
             br_ref, z_ref, acc_ref, dy_ref, dv5_ref, dvd_ref):
        isctx = _row_ids(pl.program_id(0), tm) < tc
        zero = jnp.zeros((tm, D), f32)
        wbr_v = [wbr_ref[i] for i in range(4)]
        wout_v = wout_ref[...]

        def fn(h, ml, mc, ya, ga, yb, gb, oc, gc, hh, gd, mg, e0, e1, e2, e3, ey, cn, dng, dnb, lg, lb):
            return _merge_fn(h, ml, mc, isctx, ya, ga, yb, gb, oc, jnp.zeros_like(oc), gc, hh, gd, mg,
                             [e0, e1, e2, e3], ey, cn, dng, dnb, lg, lb, wbr_v, wout_v)

        _, vjp, (brs, acc) = jax.vjp(
            fn, h_ref[...], m_ref[0:1, :], m_ref[1:2, :], ya_ref[...], ga_ref[...], yb_ref[...], gb_ref[...],
            of_ref[...] + ob_ref[...], gc_ref[...], hh_ref[...], gd_ref[...], mg_ref[...], zero, zero, zero, zero, zero,
            cn_ref[...], dng_ref[...], dnb_ref[...], lg_ref[...], lb_ref[...], has_aux=True)
        (dh, dml, dmc, dya, dga, dyb, dgb, doc, dgc, dhh, dgd, dmg, z0, z1, z2, z3, dy,
         dcn, ddng, ddnb, dlg, dlb) = vjp(g_ref[...])
        dh_ref[...] = dh
        _partial_rows(dm_ref, [dml, dmc])
        dya_ref[...] = dya
        dyb_ref[...] = dyb
        doc_ref[...] = doc
        dhh_ref[...] = dhh
        dp_ref[:, 0:M_GA] = dmg.astype(bf16)
        dp_ref[:, M_GA:M_GB] = dga.astype(bf16)
        dp_ref[:, M_GB:M_GC] = dgb.astype(bf16)
        dp_ref[:, M_GC:M_GD] = dgc.astype(bf16)
        dp_ref[:, M_GD:W_M] = dgd.astype(bf16)
        for i, z in enumerate((z0, z1, z2, z3)):
            br_ref[i] = brs[i].astype(bf16)
            z_ref[i] = z.astype(bf16)
        acc_ref[...] = acc.astype(bf16)
        dy_ref[...] = dy.astype(bf16)
        _partial_rows(dv5_ref, [dcn, ddng, ddnb])
        _partial_rows(dvd_ref, [dlg, dlb])

    t = lambda w: _tok(tm, w, 0)
    part = lambda w: pl.BlockSpec((None, SUB, w), lambda i: (i, 0, 0))
    sd = jax.ShapeDtypeStruct
    return pl.pallas_call(
        body, name=name, grid=(nt,),
        in_specs=[t(D)] + _merge_specs(tm),
        out_specs=[t(D), part(3 * D)] + [t(BRW)] * 4 + [t(W_M),
                   pl.BlockSpec((4, tm, BRW), lambda i: (0, i, 0)), pl.BlockSpec((4, tm, D), lambda i: (0, i, 0)),
                   t(D), t(D), part(BRW), part(D)],
        out_shape=[sd((T, D), f32), sd((nt, SUB, 3 * D), f32)] + [sd((T, BRW), f32)] * 4 + [sd((T, W_M), bf16),
                   sd((4, T, BRW), bf16), sd((4, T, D), bf16), sd((T, D), bf16), sd((T, D), bf16),
                   sd((nt, SUB, BRW), f32), sd((nt, SUB, D), f32)],
        compiler_params=_cparams(("parallel",)),
    )(dhn, h, modv_l, ya, P, yb, P, o2[0], o2[1], P, hh, P, P, cn, dng, dnb, lg, lb, wbr, wout)


def _loss_kernel(h, tgt, tc, tm, name):
    T = h.shape[0]
    nt = T // tm
    nct = tc // tm

    def body(h_ref, t_ref, d_ref, l_ref):
        i = pl.program_id(0)
        err = h_ref[...] - t_ref[...]
        lat = (i >= nct).astype(f32)
        d_ref[...] = err * (lat / D)
        l_ref[...] = jnp.zeros((SUB, LANE), f32) + lat * 0.5 * jnp.sum(err * err) / D

    return pl.pallas_call(
        body, name=name, grid=(nt,),
        in_specs=[pl.BlockSpec((tm, D), lambda i: (i, 0)),
                  pl.BlockSpec((tm, D), lambda i: (jnp.maximum(i - nct, 0), 0))],
        out_specs=[pl.BlockSpec((tm, D), lambda i: (i, 0)), pl.BlockSpec((None, SUB, LANE), lambda i: (i, 0, 0))],
        out_shape=[jax.ShapeDtypeStruct((T, D), f32), jax.ShapeDtypeStruct((nt, SUB, LANE), f32)],
        compiler_params=_cparams(("parallel",)),
    )(h, tgt)


def _rope_tables(tc, tl):
    t = jnp.arange(tl)
    inv = ROPE_THETA ** (-jnp.arange(0, HD // 2, 2, dtype=f32) / (HD // 2))
    ang = jnp.concatenate([(t // GRID_W).astype(f32)[:, None] * inv, (t % GRID_W).astype(f32)[:, None] * inv], -1)
    cos, sin = jnp.repeat(jnp.cos(ang), 2, axis=1), jnp.repeat(jnp.sin(ang), 2, axis=1)
    even = (jnp.arange(HD) % 2 == 0)[None, :]
    cos_f = jnp.concatenate([jnp.ones((tc, HD), f32), cos], 0)
    sin_a = jnp.concatenate([jnp.zeros((tc, HD), f32), jnp.where(even, -sin, 0.0)], 0)
    sin_b = jnp.concatenate([jnp.zeros((tc, HD), f32), jnp.where(even, 0.0, sin)], 0)
    return cos_f, sin_a, sin_b


N_CHIPS = 4
SHARD = N_IN // N_CHIPS


def _group_ranges():
    return dict(M=[(S_MG, 4 * D), (S_GA, BRW), (S_GB, BRW), (S_GC, BRW), (S_GD, BRW)], A=[(S_Q, W_A)],
                C=[(S_B, 3 * BRW), (S_DA, 2 * BRW)], G=[(S_CQ, 2 * C_KW + BRW), (S_R, 2 * C_RANK)])


def _group_weights(w4):
    out = {}
    for k, ranges in _group_ranges().items():
        parts = []
        for a, n in ranges:
            while n > 0:
                s, r = divmod(a, SHARD)
                m = min(n, SHARD - r)
                parts.append(w4[s, r:r + m])
                a, n = a + m, n - m
        if k == "G":
            parts.append(jnp.zeros((LANE - 2 * C_RANK, D), w4.dtype))
        out[k] = jnp.concatenate(parts, 0)
    return out


def _ungroup(g):
    secs = []
    for k, ranges in _group_ranges().items():
        off = 0
        for a, n in ranges:
            secs.append((a, g[k][off:off + n]))
            off += n
    return jnp.concatenate([v for _, v in sorted(secs, key=lambda t: t[0])], 0)


PROJ_TN = dict(M=2048, A=1024, C=1280, G=1152)
DU_TK = dict(M=2048, A=1024, C=BRW, G=1152)
DWP_TN = dict(M=768, A=1024, C=BRW, G=1152)


def _gate_weights(w2_l, gb_l):
    w = jnp.zeros((LANE, 2 * C_KW), f32)
    w = w.at[0:C_RANK, 0:C_KW].set(w2_l[0]).at[C_RANK:2 * C_RANK, C_KW:2 * C_KW].set(w2_l[1])
    return w, jnp.concatenate([gb_l[0], gb_l[1]])[None, :]


def _local_step(x1, c1, ctx1, tgt1, c_ctx, w_mod, b_mod, weights_of, q_norm, k_norm, b_conv, w2, gb, c_norm, d_conv_w,
                d_conv_b, d_norm_g, d_norm_b, grads_done, ln_g, ln_b, tm, token=None):
    tc, tl = ctx1.shape[0], x1.shape[0]
    T = tc + tl
    rc = min(256, tc)
    tmb = tm // 2
    tmm = 768 if T % 768 == 0 else tm
    rope = _rope_tables(tc, tl)
    cin = jnp.concatenate([c1, c_ctx[None, :], jnp.zeros((SUB - 2, D), f32)], 0)
    if token is not None:
        cin = cin + token[:, 0:1]
    modv = _mod_fwd(cin, w_mod, b_mod)
    modv = [modv[l] for l in range(DEPTH)]
    row = lambda v: v[None, :]

    h = jnp.concatenate([ctx1, x1], 0)
    saved, wp, w_br, w_out = [], [None] * DEPTH, [None] * DEPTH, [None] * DEPTH
    for l in range(DEPTH):
        wp[l], w_br[l], w_out[l] = weights_of(l, h)
        u = _ln_fwd(h, modv[l], tc, tm, f"ln_fwd{l}")
        P = {k: _matmul(u, wp[l][k], "nt", tmm, PROJ_TN[k], D, f"proj{l}{k}") for k in GROUPS}
        qn, kn, vb = _prep_fwd(P["A"], row(q_norm[l]), row(k_norm[l]), rope, tm, f"prep_fwd{l}")
        ya = _attn_fwd(qn, kn, vb, tc, tm, f"attn_fwd{l}")
        yb, hh = _conv_fwd(P["C"], b_conv[l], d_conv_w[l], row(d_conv_b[l]), tc, tl, rc, f"conv_fwd{l}")
        w2p, b2p = _gate_weights(w2[l], gb[l])
        gla = _gla_fwd(P["G"], w2p, b2p, tc, f"gla_fwd{l}")
        o2, ssave = gla[:2], gla[2:]
        hn = _merge_fwd(h, modv[l], ya, yb, o2, hh, P["M"], row(c_norm[l]), row(d_norm_g[l]), row(d_norm_b[l]),
                        row(ln_g[l]), row(ln_b[l]), w_br[l], w_out[l], tc, tm, f"merge_fwd{l}")
        saved.append((h, u, P, qn, kn, vb, ya, yb, hh, o2, ssave, w2p, b2p))
        h = hn

    dh, lparts = _loss_kernel(h, tgt1, tc, tm, "loss")
    loss = jnp.sum(lparts[:, 0, 0])

    g = {k: [None] * DEPTH for k in ("wp", "q_norm", "k_norm", "b_conv", "w2", "gb", "c_norm", "d_conv_w", "d_conv_b",
                                     "d_norm_g", "d_norm_b", "w_br", "w_out", "ln_g", "ln_b", "modv")}
    for l in reversed(range(DEPTH)):
        h_in, u, P, qn, kn, vb, ya, yb, hh, o2, ssave, w2p, b2p = saved[l]
        dP = {}
        (dh_res, dm_mg, dya, dyb, doc, dhh, dP["M"], br, z, acc, dy, dv5, dvd) = _merge_bwd(
            dh, h_in, modv[l], ya, yb, o2, hh, P["M"], row(c_norm[l]), row(d_norm_g[l]), row(d_norm_b[l]),
            row(ln_g[l]), row(ln_b[l]), w_br[l], w_out[l], tc, tmb, f"merge_bwd{l}")
        g["w_br"][l] = _matmul_tn_batched(br, z, N_CHIPS, f"dwbr{l}")
        g["w_out"][l] = _matmul(acc, dy, "tn", D, D, T, f"dwout{l}")
        v5 = jnp.sum(dv5, 0)
        g["c_norm"][l], g["d_norm_g"][l], g["d_norm_b"][l] = v5[0], v5[1], v5[2]
        vd = jnp.sum(dvd, 0)
        g["ln_g"][l], g["ln_b"][l] = vd[0], vd[1]
        dqn, dkn, dv = _attn_bwd(qn, kn, vb, dya, tc, tm, f"attn_bwd{l}")
        dP["A"], dqk = _prep_bwd(P["A"], dqn, dkn, dv, row(q_norm[l]), row(k_norm[l]), rope, tm, f"prep_bwd{l}")
        dqk = jnp.sum(dqk, 0)
        g["q_norm"][l], g["k_norm"][l] = dqk[0], dqk[1]
        dP["C"], dwb, dwd, dbd = _conv_bwd(P["C"], dyb, dhh, b_conv[l], d_conv_w[l], tc, tl, rc, f"conv_bwd{l}")
        g["b_conv"][l], g["d_conv_w"][l], g["d_conv_b"][l] = dwb, dwd, dbd[0]
        dpf, dpb, dw2p, db2p = _gla_bwd(P["G"], w2p, b2p, ssave, doc, tc, f"gla_bwd{l}")
        dP["G"] = _sum_dirs(dpf, dpb, tm, f"gla_sum{l}")
        db2p = db2p[0]
        g["w2"][l] = jnp.stack([dw2p[0:C_RANK, 0:C_KW], dw2p[C_RANK:2 * C_RANK, C_KW:2 * C_KW]])
        g["gb"][l] = jnp.stack([db2p[0:C_KW], db2p[C_KW:2 * C_KW]])
        du = None
        for k in GROUPS:
            du = _matmul(dP[k], wp[l][k], "nn", tmm, D, DU_TK[k], f"du{l}{k}", add=du)
        g["wp"][l] = {k: _matmul(dP[k], u, "tn", DWP_TN[k], D, T, f"dwp{l}{k}") for k in GROUPS}
        dh, dm_ln = _ln_bwd(du, h_in, dh_res, modv[l], tc, tm, f"ln_bwd{l}")
        g["modv"][l] = jnp.sum(dm_mg, 0) + jnp.sum(dm_ln, 0)
        tk = grads_done(l, {k: g[k][l] for k in ("wp", "w_br", "w_out")})
        if tk is not None and l > 0:
            modv[l - 1] = modv[l - 1] + tk[:, 0:1]

    dmodv = jnp.stack(g.pop("modv"))
    g["w_mod"], dcin = _mod_bwd(cin, w_mod, dmodv)
    g["b_mod"] = dmodv[:, 0, :] + dmodv[:, 1, :]
    g["c_ctx"] = jnp.sum(dcin, (0, 1))[1]
    return loss, dh[tc:], g


HALF_TL = 256


def _adamw(w, g, m, v, name, tr=128):
    L, R, C = w.shape
    if R % tr == 0:
        grid, spec = (L, R // tr), pl.BlockSpec((None, tr, C), lambda l, i: (l, i, 0))
    elif R * C * 4 <= (1 << 20):
        grid, spec = (L, 1), pl.BlockSpec((None, R, C), lambda l, i: (l, 0, 0))
    else:
        grid, spec = (L, C // HALF_TL), pl.BlockSpec((None, R, HALF_TL), lambda l, i: (l, 0, i))

    def body(w_ref, g_ref, m_ref, v_ref, d_ref, nm_ref, nv_ref):
        gg = g_ref[...]
        nm = B1 * m_ref[...] + (1.0 - B1) * gg
        nv = B2 * v_ref[...] + (1.0 - B2) * (gg * gg)
        m_hat = nm / (1.0 - B1 ** STEP)
        v_hat = nv / (1.0 - B2 ** STEP)
        d_ref[...] = -LR * (m_hat / (jnp.sqrt(v_hat) + AEPS) + WD * w_ref[...])
        nm_ref[...] = nm
        nv_ref[...] = nv

    return pl.pallas_call(
        body, name=name, grid=grid, in_specs=[spec] * 4, out_specs=[spec] * 3,
        out_shape=[jax.ShapeDtypeStruct((L, R, C), f32)] * 3,
        compiler_params=_cparams(("parallel", "parallel")),
    )(w, g, m, v)


MESH = pl.DeviceIdType.MESH
ANY = pl.BlockSpec(memory_space=pl.ANY)
N_CHIPS = 4


def _place():
    x, y, c = lax.axis_index("x"), lax.axis_index("y"), lax.axis_index("c")
    chips = [(1 - x, y), (x, 1 - y), (1 - x, 1 - y)]
    return x, y, c, chips


def _half(ref, c, axis):
    n = ref.shape[axis] // 2
    last = axis in (-1, ref.ndim - 1)
    idx = [slice(None)] * ref.ndim
    idx[axis] = pl.ds(pl.multiple_of(c * n, LANE if last else SUB), n)
    return ref.at[tuple(idx)]


def _half_shape(shape, axis):
    s = list(shape)
    s[axis] //= 2
    return tuple(s)


def _all_gather(arrs, axes, name):
    n = len(arrs)

    def body(*refs):
        ins, outs = refs[:n], refs[n:2 * n]
        send, recv = refs[2 * n:]
        x, y, c, chips = _place()
        me, sib = 2 * x + y, (x, y, 1 - c)

        def copy(a, k, chip_idx, cc, to, src=None):
            blk = _half(outs[a].at[chip_idx], cc, axes[a])
            return pltpu.make_async_remote_copy(src_ref=blk if src is None else src, dst_ref=blk,
                                                send_sem=send.at[7 * a + k], recv_sem=recv.at[7 * a + k],
                                                device_id=to, device_id_type=MESH)

        own = [pltpu.make_async_remote_copy(src_ref=ins[a], dst_ref=outs[a].at[me], send_sem=send.at[7 * a + 6],
                                            recv_sem=recv.at[7 * a + 6], device_id=sib, device_id_type=MESH)
               for a in range(n)]
        first = own + [copy(a, j, me, c, (*chip, c), src=_half(ins[a], c, axes[a]))
                       for a in range(n) for j, chip in enumerate(chips)]
        for cp in first:
            cp.start()
        passed = []
        for a in range(n):
            for j, chip in enumerate(chips):
                k = 2 * chip[0] + chip[1]
                copy(a, j, k, c, sib).wait_recv()
                fwd = copy(a, 3 + j, k, c, sib)
                fwd.start()
                passed.append(fwd)
        for a in range(n):
            own[a].wait_recv()
            for j, chip in enumerate(chips):
                copy(a, 3 + j, 2 * chip[0] + chip[1], 1 - c, sib).wait_recv()
        for cp in first + passed:
            cp.wait_send()

    return pl.pallas_call(
        body, name=name, in_specs=[ANY] * n, out_specs=[ANY] * n,
        out_shape=[jax.ShapeDtypeStruct((N_CHIPS,) + a.shape, a.dtype) for a in arrs],
        scratch_shapes=[pltpu.SemaphoreType.DMA((7 * n,)), pltpu.SemaphoreType.DMA((7 * n,))],
    )(*arrs)


def _sibling_halves(arrs, axes, name):
    n = len(arrs)

    def body(*refs):
        ins, outs = refs[:n], refs[n:2 * n]
        send, recv = refs[2 * n:]
        x, y, c, _ = _place()
        cps = [pltpu.make_async_remote_copy(src_ref=_half(ins[a], 1 - c, axes[a] + 1), dst_ref=outs[a], send_sem=send.at[a],
                                            recv_sem=recv.at[a], device_id=(x, y, 1 - c), device_id_type=MESH)
               for a in range(n)]
        for cp in cps:
            cp.start()
        for cp in cps:
            cp.wait()

    return pl.pallas_call(
        body, name=name, in_specs=[ANY] * n, out_specs=[ANY] * n,
        out_shape=[jax.ShapeDtypeStruct(_half_shape(a.shape, axes[i] + 1), a.dtype) for i, a in enumerate(arrs)],
        scratch_shapes=[pltpu.SemaphoreType.DMA((n,)), pltpu.SemaphoreType.DMA((n,))],
    )(*arrs)


def _add_half(gfull, land, cidx, axis, name, tr=128, out_dtype=bf16):
    _, hr, hc = land.shape
    if axis == 0:
        tr = min(tr, hr)
        nb, blk = hr // tr, (None, tr, hc)
        g_spec = pl.BlockSpec(blk, lambda s, i, cr: (s, cr[0] * nb + i, 0))
        l_spec = pl.BlockSpec(blk, lambda s, i, cr: (s, i, 0))
    else:
        nb, blk = hc // HALF_TL, (None, hr, HALF_TL)
        g_spec = pl.BlockSpec(blk, lambda s, i, cr: (s, 0, cr[0] * nb + i))
        l_spec = pl.BlockSpec(blk, lambda s, i, cr: (s, 0, i))

    def body(c_ref, g_ref, l_ref, o_ref):
        o_ref[...] = (g_ref[...].astype(f32) + l_ref[...].astype(f32)).astype(o_ref.dtype)

    return pl.pallas_call(
        body, name=name,
        grid_spec=pltpu.PrefetchScalarGridSpec(
            num_scalar_prefetch=1, grid=(N_CHIPS, nb), in_specs=[g_spec, l_spec], out_specs=l_spec),
        out_shape=jax.ShapeDtypeStruct((N_CHIPS, hr, hc), out_dtype),
        compiler_params=_cparams(("parallel", "parallel")),
    )(cidx, gfull, land)


def _chip_exchange(arrs, name):
    n = len(arrs)

    def body(*refs):
        ins, outs = refs[:n], refs[n:2 * n]
        send, recv = refs[2 * n:]
        x, y, c, chips = _place()
        me = 2 * x + y
        cps = []
        for a in range(n):
            for j, chip in enumerate(chips):
                k = 2 * chip[0] + chip[1]
                cps.append((pltpu.make_async_remote_copy(
                    src_ref=ins[a].at[k], dst_ref=outs[a].at[me], send_sem=send.at[3 * a + j], recv_sem=recv.at[3 * a + j],
                    device_id=(*chip, c), device_id_type=MESH), a, j, k))
        for cp, *_ in cps:
            cp.start()
        for cp, a, j, k in cps:
            pltpu.make_async_remote_copy(src_ref=ins[a].at[k], dst_ref=outs[a].at[k], send_sem=send.at[3 * a + j],
                                         recv_sem=recv.at[3 * a + j], device_id=(x, y, c), device_id_type=MESH).wait_recv()
        for cp, *_ in cps:
            cp.wait_send()

    return pl.pallas_call(
        body, name=name, in_specs=[ANY] * n, out_specs=[ANY] * n,
        out_shape=[jax.ShapeDtypeStruct(a.shape, a.dtype) for a in arrs],
        scratch_shapes=[pltpu.SemaphoreType.DMA((3 * n,)), pltpu.SemaphoreType.DMA((3 * n,))],
    )(*arrs)


def _sum_chips(land, own, place, axis, layer, into, name, tr=128):
    _, hr, hc = land.shape
    fresh = not hasattr(into, "dtype")
    shape = tuple(into) if fresh else into.shape
    if axis == 0:
        tr = min(tr, hr)
        nb, blk = hr // tr, (tr, hc)
        l_map, m_map = (lambda i, p: (0, i, 0)), (lambda i, p: (p[0], i, 0))
        o_map = lambda i, p: (layer, p[1] * nb + i, 0)
    else:
        nb, blk = hc // HALF_TL, (hr, HALF_TL)
        l_map, m_map = (lambda i, p: (0, 0, i)), (lambda i, p: (p[0], 0, i))
        o_map = lambda i, p: (layer, 0, p[1] * nb + i)

    def body(p_ref, l_ref, o_ref, *rest):
        me = p_ref[0]
        mine = o_ref[...].astype(f32)
        acc = None
        for k in range(N_CHIPS):
            t = jnp.where(me == k, mine, l_ref[k].astype(f32))
            acc = t if acc is None else acc + t
        rest[-1][...] = acc

    return pl.pallas_call(
        body, name=name,
        grid_spec=pltpu.PrefetchScalarGridSpec(
            num_scalar_prefetch=1, grid=(nb,),
            in_specs=[pl.BlockSpec((N_CHIPS,) + blk, l_map), pl.BlockSpec((None,) + blk, m_map)] + ([] if fresh else [ANY]),
            out_specs=pl.BlockSpec((None,) + blk, o_map)),
        out_shape=jax.ShapeDtypeStruct(shape, f32),
        input_output_aliases={} if fresh else {3: 0},
        compiler_params=_cparams(("parallel",)),
    )(place, land, own, *([] if fresh else [into]))


def _sibling_fill(arrs, axes, name):
    n = len(arrs)

    def body(*refs):
        outs = refs[n:2 * n]
        send, recv = refs[2 * n:]
        x, y, c, _ = _place()
        cps = [pltpu.make_async_remote_copy(src_ref=_half(outs[a], c, axes[a] + 1), dst_ref=_half(outs[a], c, axes[a] + 1),
                                            send_sem=send.at[a], recv_sem=recv.at[a], device_id=(x, y, 1 - c),
                                            device_id_type=MESH) for a in range(n)]
        for cp in cps:
            cp.start()
        for a in range(n):
            blk = _half(outs[a], 1 - c, axes[a] + 1)
            pltpu.make_async_remote_copy(src_ref=blk, dst_ref=blk, send_sem=send.at[a], recv_sem=recv.at[a],
                                         device_id=(x, y, 1 - c), device_id_type=MESH).wait_recv()
        for cp in cps:
            cp.wait_send()

    return pl.pallas_call(
        body, name=name, in_specs=[ANY] * n, out_specs=[ANY] * n,
        out_shape=[jax.ShapeDtypeStruct(a.shape, a.dtype) for a in arrs],
        input_output_aliases={a: a for a in range(n)},
        scratch_shapes=[pltpu.SemaphoreType.DMA((n,)), pltpu.SemaphoreType.DMA((n,))],
    )(*arrs)


HBM = pl.BlockSpec(memory_space=pltpu.HBM)
SEM = pl.BlockSpec(memory_space=pltpu.SEMAPHORE)
EFFECT = pltpu.SideEffectType.DATAFLOW_SIDE_EFFECTING
PEERS = 4


def _split_copies(srcs, lands, send, recv, gather):
    x, y, c, chips = _place()
    me = 2 * x + y
    peers = [((*chip, c), 2 * chip[0] + chip[1]) for chip in chips] + ([((x, y, 1 - c), me)] if gather else [])
    out = []
    for a in range(len(srcs)):
        for j, (dev, k) in enumerate(peers):
            src = srcs[a] if gather else srcs[a].at[k]
            sems = dict(send_sem=send.at[PEERS * a + j], recv_sem=recv.at[PEERS * a + j], device_id=dev, device_id_type=MESH)
            out.append((pltpu.make_async_remote_copy(src_ref=src, dst_ref=lands[a].at[me], **sems),
                        pltpu.make_async_remote_copy(src_ref=src, dst_ref=lands[a].at[k], **sems)))
    return out


def _split_start(srcs, gather, after, name):
    n = len(srcs)
    lands = [lax.empty(((N_CHIPS,) + s.shape) if gather else s.shape, s.dtype) for s in srcs]

    def body(*refs):
        send, recv = refs[2 * n + 1], refs[2 * n + 2]
        for start, _ in _split_copies(refs[:n], refs[n:2 * n], send, recv, gather):
            start.start()
        refs[-1][...] = jnp.zeros_like(refs[-1])

    sems = pltpu.SemaphoreType.DMA((PEERS * n,))
    hbm = lambda a: pltpu.with_memory_space_constraint(a, pltpu.HBM)
    out = pl.pallas_call(
        body, name=name,
        out_shape=(sems, sems, *[pltpu.HBM(a.shape, a.dtype) for a in srcs + lands], jax.ShapeDtypeStruct((SUB, LANE), f32)),
        in_specs=[HBM] * (2 * n) + [ANY], out_specs=(SEM, SEM, *[HBM] * (2 * n), pl.BlockSpec(memory_space=pltpu.VMEM)),
        input_output_aliases={i: 2 + i for i in range(2 * n)},
        compiler_params=pltpu.CompilerParams(has_side_effects=EFFECT),
    )(*[hbm(a) for a in srcs + lands], after)
    return out[0], out[1], list(out[2:2 + n]), list(out[2 + n:2 + 2 * n]), out[-1]


def _split_wait(send, recv, srcs, lands, gather, after, name):
    n = len(srcs)

    def body(*refs):
        for start, arrival in _split_copies(refs[:n], refs[n:2 * n], refs[2 * n], refs[2 * n + 1], gather):
            start.wait_send()
            arrival.wait_recv()

    out = pl.pallas_call(
        body, name=name, out_shape=[pltpu.HBM(a.shape, a.dtype) for a in srcs + lands],
        in_specs=[HBM] * (2 * n) + [SEM, SEM, ANY], out_specs=[HBM] * (2 * n),
        input_output_aliases={i: i for i in range(2 * n)},
        compiler_params=pltpu.CompilerParams(has_side_effects=EFFECT),
    )(*srcs, *lands, send, recv, after)
    return list(out[:n]), list(out[n:])


N_DEV = 8


def _all_reduce_small(v, name):
    R = v.shape[0]

    def body(v_ref, o_ref, land_ref, send, recv):
        x, y, c, _ = _place()
        me = 4 * x + 2 * y + c
        land_ref[me] = v_ref[...]
        cps = []
        for m in range(1, N_DEV):
            px, py, pc = [(1 - q) if (m >> s) & 1 else q for q, s in ((x, 2), (y, 1), (c, 0))]
            cps.append((pltpu.make_async_remote_copy(src_ref=v_ref, dst_ref=land_ref.at[me], send_sem=send.at[m - 1],
                                                     recv_sem=recv.at[m - 1], device_id=(px, py, pc), device_id_type=MESH),
                        4 * px + 2 * py + pc, m))
        for cp, *_ in cps:
            cp.start()
        for cp, peer, m in cps:
            pltpu.make_async_remote_copy(src_ref=v_ref, dst_ref=land_ref.at[peer], send_sem=send.at[m - 1],
                                         recv_sem=recv.at[m - 1], device_id=(x, y, c), device_id_type=MESH).wait_recv()
        for cp, *_ in cps:
            cp.wait_send()
        acc = land_ref[0]
        for k in range(1, N_DEV):
            acc = acc + land_ref[k]
        o_ref[...] = acc

    vm = pl.BlockSpec(memory_space=pltpu.VMEM)
    return pl.pallas_call(
        body, name=name, in_specs=[vm], out_specs=vm, out_shape=jax.ShapeDtypeStruct(v.shape, f32),
        scratch_shapes=[pltpu.VMEM((N_DEV, R, LANE), f32), pltpu.SemaphoreType.DMA((N_DEV - 1,)),
                        pltpu.SemaphoreType.DMA((N_DEV - 1,))],
        compiler_params=pltpu.CompilerParams(vmem_limit_bytes=VMEM_LIMIT),
    )(v)


def _pack_small(arrs, mult=2 * SUB):
    flat = jnp.concatenate([a.reshape(-1) for a in arrs])
    rows = -(-flat.shape[0] // (LANE * mult)) * mult
    return jnp.pad(flat, (0, rows * LANE - flat.shape[0])).reshape(rows, LANE)


def _unpack_small(vec, shapes):
    flat, out, o = vec.reshape(-1), [], 0
    for s in shapes:
        n = int(np.prod(s))
        out.append(flat[o:o + n].reshape(s))
        o += n
    return out


REPL_SMALL = ("c_ctx", "b_mod", "q_norm", "k_norm", "c_norm", "d_conv_b", "d_norm_g", "d_norm_b", "ln_g", "ln_b")
SHARD_SMALL = ("b_conv", "c_gate_w2", "c_gate_b", "d_conv_w")
BIG = ("w_mod", "w_in", "w_br", "w_out")
ORDER = ("c_ctx", "w_mod", "b_mod", "w_in", "q_norm", "k_norm", "b_conv", "c_gate_w2", "c_gate_b", "c_norm", "d_conv_w",
         "d_conv_b", "d_norm_g", "d_norm_b", "w_br", "w_out", "ln_g", "ln_b")


def _unshard_last(g4, shard_shape):
    g = g4.reshape((N_CHIPS,) + tuple(shard_shape))
    g = jnp.moveaxis(g, 0, -2)
    return g.reshape(tuple(shard_shape[:-1]) + (N_CHIPS * shard_shape[-1],))


def _pieces_last(full):
    w = full.shape[-1] // N_CHIPS
    g = full.reshape(full.shape[:-1] + (N_CHIPS, w))
    return jnp.moveaxis(g, -2, 0).reshape(N_CHIPS, -1, w)


def kernel(x, c, ctx, c_ctx, w_mod, b_mod, w_in, q_norm, k_norm, b_conv, c_gate_w2, c_gate_b, c_norm, d_conv_w, d_conv_b, d_norm_g, d_norm_b, w_br, w_out, ln_g, ln_b, loss_target, m_c_ctx, m_w_mod, m_b_mod, m_w_in, m_q_norm, m_k_norm, m_b_conv, m_c_gate_w2, m_c_gate_b, m_c_norm, m_d_conv_w, m_d_conv_b, m_d_norm_g, m_d_norm_b, m_w_br, m_w_out, m_ln_g, m_ln_b, v_c_ctx, v_w_mod, v_b_mod, v_w_in, v_q_norm, v_k_norm, v_b_conv, v_c_gate_w2, v_c_gate_b, v_c_norm, v_d_conv_w, v_d_conv_b, v_d_norm_g, v_d_norm_b, v_w_br, v_w_out, v_ln_g, v_ln_b):
    W = dict(c_ctx=c_ctx, w_mod=w_mod, b_mod=b_mod, w_in=w_in, q_norm=q_norm, k_norm=k_norm, b_conv=b_conv,
             c_gate_w2=c_gate_w2, c_gate_b=c_gate_b, c_norm=c_norm, d_conv_w=d_conv_w, d_conv_b=d_conv_b,
             d_norm_g=d_norm_g, d_norm_b=d_norm_b, w_br=w_br, w_out=w_out, ln_g=ln_g, ln_b=ln_b)
    M = dict(c_ctx=m_c_ctx, w_mod=m_w_mod, b_mod=m_b_mod, w_in=m_w_in, q_norm=m_q_norm, k_norm=m_k_norm, b_conv=m_b_conv,
             c_gate_w2=m_c_gate_w2, c_gate_b=m_c_gate_b, c_norm=m_c_norm, d_conv_w=m_d_conv_w, d_conv_b=m_d_conv_b,
             d_norm_g=m_d_norm_g, d_norm_b=m_d_norm_b, w_br=m_w_br, w_out=m_w_out, ln_g=m_ln_g, ln_b=m_ln_b)
    V = dict(c_ctx=v_c_ctx, w_mod=v_w_mod, b_mod=v_b_mod, w_in=v_w_in, q_norm=v_q_norm, k_norm=v_k_norm, b_conv=v_b_conv,
             c_gate_w2=v_c_gate_w2, c_gate_b=v_c_gate_b, c_norm=v_c_norm, d_conv_w=v_d_conv_w, d_conv_b=v_d_conv_b,
             d_norm_g=v_d_norm_g, d_norm_b=v_d_norm_b, w_br=v_w_br, w_out=v_w_out, ln_g=v_ln_g, ln_b=v_ln_b)
    chip = 2 * lax.axis_index("x") + lax.axis_index("y")
    cidx = lax.axis_index("c").astype(jnp.int32).reshape(1)

    place = jnp.stack([chip, lax.axis_index("c")]).astype(jnp.int32)

    AXIS = dict(w_in=1, w_mod=0, w_br=0, w_out=0)
    ex = dict(w_in=lambda a: jnp.swapaxes(a, 1, 2), w_mod=lambda a: a.reshape(1, DEPTH * D, -1),
              w_br=lambda a: a.reshape(DEPTH, 4 * BRW, -1), w_out=lambda a: a)
    Wx, Mx, Vx = ({k: ex[k](P_[k]) for k in BIG} for P_ in (W, M, V))

    LAYER = ("w_in", "w_br", "w_out")
    small_shard = _pack_small([W[k] for k in SHARD_SMALL])
    keys0 = LAYER + ("w_mod",)
    got = _all_gather([Wx[k][0].astype(bf16) for k in keys0] + [small_shard], [AXIS[k] for k in keys0] + [0], "all_gather0")
    smalls = [_unpack_small(got[-1][s], [W[k].shape for k in SHARD_SMALL]) for s in range(N_CHIPS)]
    full = {k: jnp.concatenate([smalls[s][i] for s in range(N_CHIPS)], axis=-1) for i, k in enumerate(SHARD_SMALL)}
    wmod = got[3].reshape(N_CHIPS, DEPTH, D, 3 * D // N_CHIPS)
    ag_send, ag_recv, ag_src, ag_land, ag_token = _split_start([Wx[k][1].astype(bf16) for k in LAYER], True, got[0],
                                                               "all_gather1_start")

    def weights_of(l, h):
        g3 = got[:3] if l == 0 else _split_wait(ag_send, ag_recv, ag_src, ag_land, True, h, "all_gather1_wait")[1]
        return (_group_weights(g3[0]),
                jnp.moveaxis(g3[1].reshape(N_CHIPS, 4, BRW, D // N_CHIPS), 0, 2).reshape(4, BRW, D), g3[2].reshape(D, D))

    red = {k: Wx[k].shape for k in BIG}
    flying = {}

    def pair_sums(l, pieces):
        keys = list(pieces)
        land_a = _sibling_halves([pieces[k] for k in keys], [AXIS[k] for k in keys], f"rs_sibling_halves{l}")
        return keys, [_add_half(pieces[k], la, cidx, AXIS[k], f"rs_pair_sum{l}_{k}") for k, la in zip(keys, land_a)]

    def chip_sums(l, keys, land_b, pair):
        for k, lb, pr in zip(keys, land_b, pair):
            red[k] = _sum_chips(lb, pr, place, AXIS[k], l, red[k], f"rs_chip_sum{l}_{k}")

    def grads_done(l, gl):
        pieces = dict(w_in=_ungroup(gl["wp"]).astype(bf16).reshape(N_CHIPS, SHARD, D),
                      w_br=gl["w_br"].reshape(N_CHIPS, 4 * BRW, D // N_CHIPS), w_out=gl["w_out"].reshape(N_CHIPS, D // N_CHIPS, D))
        if l == 0:
            flying[0] = pieces
            return None
        keys, pair = pair_sums(l, pieces)
        send, recv, src, land, token = _split_start(pair, False, jnp.zeros((SUB, LANE), f32), "rs_chip_exchange1_start")
        flying[l] = (keys, send, recv, src, land)
        return token

    loss, gx, g = _local_step(
        x[0], c, ctx[0], loss_target[0], c_ctx, wmod, b_mod, weights_of, q_norm, k_norm, full["b_conv"],
        full["c_gate_w2"], full["c_gate_b"], c_norm, full["d_conv_w"], d_conv_b, d_norm_g, d_norm_b,
        grads_done, ln_g, ln_b, tm=256, token=ag_token)
    g["c_gate_w2"], g["c_gate_b"] = g.pop("w2"), g.pop("gb")
    loss = lax.psum(loss, ("x", "y", "c"))

    keys, send, recv, src, land = flying[1]
    pair, land_b = _split_wait(send, recv, src, land, False, gx, "rs_chip_exchange1_wait")
    chip_sums(1, keys, land_b, pair)
    flying[0]["w_mod"] = g["w_mod"].reshape(N_CHIPS, DEPTH * D, 3 * D // N_CHIPS)
    keys, pair = pair_sums(0, flying[0])
    chip_sums(0, keys, _chip_exchange(pair, "rs_chip_exchange0"), pair)
    red = dict(zip(BIG, _sibling_fill([red[k] for k in BIG], [AXIS[k] for k in BIG], "rs_sibling_fill")))
    g = {k: (jnp.stack(v) if isinstance(v, list) else v) for k, v in g.items() if k not in ("wp", "w_br", "w_out", "w_mod")}

    small_names = REPL_SMALL + SHARD_SMALL
    gs = _all_reduce_small(_pack_small([g[k] for k in small_names]), "all_reduce_small")
    gsm = dict(zip(small_names, _unpack_small(gs, [g[k].shape for k in small_names])))
    for k in SHARD_SMALL:
        wdt = W[k].shape[-1]
        gsm[k] = lax.dynamic_slice_in_dim(gsm[k], chip * wdt, wdt, axis=gsm[k].ndim - 1)

    grad, delta, new_m, new_v = {}, {}, {}, {}
    for k in BIG:
        back = (lambda a: jnp.swapaxes(a, 1, 2)) if k == "w_in" else (lambda a: a.reshape(W[k].shape))
        d_, m_, v_ = _adamw(Wx[k], red[k], Mx[k], Vx[k], f"adamw_{k}")
        grad[k], delta[k], new_m[k], new_v[k] = back(red[k]), back(d_), back(m_), back(v_)
    shapes = [W[k].shape for k in small_names]
    d_, m_, v_ = _adamw(*[_pack_small([P_[k] for k in small_names])[None] for P_ in (W, gsm, M, V)], "adamw_small")
    for k, dd, mm_, vv in zip(small_names, _unpack_small(d_, shapes), _unpack_small(m_, shapes), _unpack_small(v_, shapes)):
        grad[k], delta[k], new_m[k], new_v[k] = gsm[k], dd, mm_, vv

    return (loss, gx[None], *[grad[k] for k in ORDER], *[delta[k] for k in ORDER], *[new_m[k] for k in ORDER],
            *[new_v[k] for k in ORDER])
```

```python
import functools

import jax
import jax.numpy as jnp
import numpy as np
from jax import lax
from jax.experimental import pallas as pl
from jax.experimental.pallas import tpu as pltpu

f32 = jnp.float32
bf16 = jnp.bfloat16

D = 1024
DEPTH = 2
GRID_W = 64
BRW = 512
HD = 128
A_HEADS = 4
C_HEADS = 4
C_KW = 256
C_RANK = 16
C_TAU = 16.0
CH = 64
KB = 3
KD = 31
ALPHA = (2 * DEPTH) ** 0.25
EPS = 1e-6
ROPE_THETA = 10000.0
N_IN = 10784
LR, B1, B2, AEPS, WD, STEP = 0.001, 0.9, 0.999, 1e-08, 0.01, 10

W_M, W_A, W_C, W_G = 4 * D + 4 * BRW, 1024, 5 * BRW, 1152
GROUPS = ("M", "A", "C", "G")
GROUP_W = dict(M=W_M, A=W_A, C=W_C, G=W_G)
M_GA, M_GB, M_GC, M_GD = 4 * D, 4 * D + BRW, 4 * D + 2 * BRW, 4 * D + 3 * BRW
A_K, A_V = 512, 768
G_K, G_V, G_R = 256, 512, 1024
CT = 5 * 128
S_Q, S_GA, S_B, S_C, S_X, S_GB, S_CQ, S_CV, S_GC, S_R, S_DA, S_DG, S_GD, S_MG = (
    0, 1024, 1536, 2048, 2560, 3072, 3584, 4096, 4608, 5120, 5152, 5664, 6176, 6688)

LANE = 128
SUB = 8
VMEM_LIMIT = 56 * 1024 * 1024
CONV_PAD = 16
GLA_SUB = 16
GLA_CLAMP = 60.0


def _cparams(sem, vmem=VMEM_LIMIT):
    return pltpu.CompilerParams(dimension_semantics=sem, vmem_limit_bytes=vmem)


def _dg(a, b, ca, cb):
    return lax.dot_general(a.astype(bf16), b.astype(bf16), (((ca,), (cb,)), ((), ())),
                           preferred_element_type=f32)


@jax.custom_vjp
def mm(a, b):
    return _dg(a, b, 1, 0)


mm.defvjp(lambda a, b: (_dg(a, b, 1, 0), (a, b)),
          lambda r, ct: (_dg(ct, r[1], 1, 1).astype(r[0].dtype), _dg(r[0], ct, 0, 0).astype(r[1].dtype)))


@jax.custom_vjp
def mm_nt(a, b):
    return _dg(a, b, 1, 1)


mm_nt.defvjp(lambda a, b: (_dg(a, b, 1, 1), (a, b)),
             lambda r, ct: (_dg(ct, r[1], 1, 0).astype(r[0].dtype), _dg(ct, r[0], 0, 0).astype(r[1].dtype)))


@jax.custom_vjp
def mm_tn(a, b):
    return _dg(a, b, 0, 0)


mm_tn.defvjp(lambda a, b: (_dg(a, b, 0, 0), (a, b)),
             lambda r, ct: (_dg(r[1], ct, 1, 1).astype(r[0].dtype), _dg(r[0], ct, 1, 0).astype(r[1].dtype)))


def _sigmoid(x):
    return 0.5 * jnp.tanh(0.5 * x) + 0.5


def _silu(x):
    return x * _sigmoid(x)


def _ln(x):
    mu = jnp.mean(x, -1, keepdims=True)
    xc = x - mu
    var = jnp.mean(xc * xc, -1, keepdims=True)
    return xc * lax.rsqrt(var + EPS)


def _rms(x, g):
    return x * lax.rsqrt(jnp.mean(x * x, -1, keepdims=True) + EPS) * g


@jax.custom_vjp
def _rope(x, cos_f, sin_a, sin_b):
    return x * cos_f + pltpu.roll(x, HD - 1, 1) * sin_a + pltpu.roll(x, 1, 1) * sin_b


def _rope_fwd(x, cos_f, sin_a, sin_b):
    return _rope(x, cos_f, sin_a, sin_b), (cos_f, sin_a, sin_b)


def _rope_bwd(r, ct):
    cos_f, sin_a, sin_b = r
    dx = ct * cos_f + pltpu.roll(ct * sin_a, 1, 1) + pltpu.roll(ct * sin_b, HD - 1, 1)
    return dx, jnp.zeros_like(cos_f), jnp.zeros_like(sin_a), jnp.zeros_like(sin_b)


_rope.defvjp(_rope_fwd, _rope_bwd)


def _row_ids(i, tm):
    return i * tm + lax.broadcasted_iota(jnp.int32, (tm, 1), 0)


def _partial_rows(ref, rows):
    n = len(rows)
    for k, r in enumerate(rows):
        ref[k:k + 1, :] = r
    ref[n:SUB, :] = jnp.zeros((SUB - n, ref.shape[-1]), f32)


def _matmul(a, b, mode, tm, tn, tk, name, out_dtype=f32, add=None):
    sect = a.ndim == 3
    a2 = (a.shape[1], a.shape[0] * a.shape[2]) if sect else a.shape
    if mode == "nn":
        (M, K), N = a2, b.shape[1]
        a_spec = pl.BlockSpec((None, tm, tk), lambda j, i, k: (k, i, 0)) if sect else pl.BlockSpec((tm, tk), lambda j, i, k: (i, k))
        b_spec = pl.BlockSpec((tk, tn), lambda j, i, k: (k, j))
        ca, cb = 1, 0
        assert not sect or tk == a.shape[2]
    elif mode == "nt":
        (M, K), N = a2, b.shape[0]
        assert not sect
        a_spec = pl.BlockSpec((tm, tk), lambda j, i, k: (i, k))
        b_spec = pl.BlockSpec((tn, tk), lambda j, i, k: (j, k))
        ca, cb = 1, 1
    else:
        (K, M), N = a2, b.shape[1]
        a_spec = pl.BlockSpec((None, tk, tm), lambda j, i, k: (i, k, 0)) if sect else pl.BlockSpec((tk, tm), lambda j, i, k: (k, i))
        b_spec = pl.BlockSpec((tk, tn), lambda j, i, k: (k, j))
        ca, cb = 0, 0
        assert not sect or tm == a.shape[2]
    assert M % tm == 0 and N % tn == 0 and K % tk == 0, (name, M, N, K, tm, tn, tk)
    nk = K // tk

    o_spec = pl.BlockSpec((tm, tn), lambda j, i, k: (i, j))

    def body(a_ref, b_ref, *rest):
        add_ref = rest[0] if add is not None else None
        o_ref, acc_ref = rest[-2:]
        k = pl.program_id(2)
        part = _dg(a_ref[...], b_ref[...], ca, cb)

        @pl.when(k == 0)
        def _():
            acc_ref[...] = part if add_ref is None else part + add_ref[...]

        @pl.when(k > 0)
        def _():
            acc_ref[...] += part

        @pl.when(k == nk - 1)
        def _():
            o_ref[...] = acc_ref[...].astype(o_ref.dtype)

    return pl.pallas_call(
        body, name=name, grid=(N // tn, M // tm, nk),
        in_specs=[a_spec, b_spec] + ([o_spec] if add is not None else []), out_specs=o_spec,
        out_shape=jax.ShapeDtypeStruct((M, N), out_dtype),
        scratch_shapes=[pltpu.VMEM((tm, tn), f32)],
        compiler_params=_cparams(("parallel", "parallel", "arbitrary")),
    )(a, b, *([add] if add is not None else []))


def _matmul_tn_batched(a, b, ns, name):
    B, K, M = a.shape
    N = b.shape[2] // ns

    def body(a_ref, b_ref, o_ref):
        o_ref[...] = _dg(a_ref[...], b_ref[...], 0, 0)

    return pl.pallas_call(
        body, name=name, grid=(B, ns),
        in_specs=[pl.BlockSpec((None, K, M), lambda i, s: (i, 0, 0)), pl.BlockSpec((None, K, N), lambda i, s: (i, 0, s))],
        out_specs=pl.BlockSpec((None, None, M, N), lambda i, s: (s, i, 0, 0)),
        out_shape=jax.ShapeDtypeStruct((ns, B, M, N), f32),
        compiler_params=_cparams(("parallel", "parallel")),
    )(a, b)


MOD_TN = 768


def _mod_fwd(cin, w_mod, b_mod):
    def body(c_ref, w_ref, b_ref, o_ref):
        o_ref[...] = mm(_silu(c_ref[...]), w_ref[...]) + b_ref[...]

    return pl.pallas_call(
        body, name="mod_fwd", grid=(DEPTH, 3 * D // MOD_TN),
        in_specs=[pl.BlockSpec((SUB, D), lambda l, j: (0, 0)),
                  pl.BlockSpec((None, None, D, MOD_TN), lambda l, j: (j, l, 0, 0)),
                  pl.BlockSpec((None, 1, MOD_TN), lambda l, j: (l, 0, j))],
        out_specs=pl.BlockSpec((None, SUB, MOD_TN), lambda l, j: (l, 0, j)),
        out_shape=jax.ShapeDtypeStruct((DEPTH, SUB, 3 * D), f32),
        compiler_params=_cparams(("parallel", "parallel")),
    )(cin, w_mod, b_mod.reshape(DEPTH, 1, 3 * D))


def _mod_bwd(cin, w_mod, dmodv):
    nj = 3 * D // MOD_TN

    def body(c_ref, w_ref, g_ref, dw_ref, dc_ref):
        _, vjp = jax.vjp(lambda c, w: mm(_silu(c), w), c_ref[...], w_ref[...].astype(f32))
        dc, dw = vjp(g_ref[...])
        dw_ref[...] = dw
        dc_ref[...] = dc

    return pl.pallas_call(
        body, name="mod_bwd", grid=(DEPTH, nj),
        in_specs=[pl.BlockSpec((SUB, D), lambda l, j: (0, 0)),
                  pl.BlockSpec((None, None, D, MOD_TN), lambda l, j: (j, l, 0, 0)),
                  pl.BlockSpec((None, SUB, MOD_TN), lambda l, j: (l, 0, j))],
        out_specs=[pl.BlockSpec((None, None, D, MOD_TN), lambda l, j: (j, l, 0, 0)),
                   pl.BlockSpec((None, None, SUB, D), lambda l, j: (l, j, 0, 0))],
        out_shape=[jax.ShapeDtypeStruct((nj, DEPTH, D, MOD_TN), f32),
                   jax.ShapeDtypeStruct((DEPTH, nj, SUB, D), f32)],
        compiler_params=_cparams(("parallel", "parallel")),
    )(cin, w_mod, dmodv)


def _u_fn(h, m_l, m_c, isctx):
    n = _ln(h)
    shift = jnp.where(isctx, m_c[:, 0:D], m_l[:, 0:D])
    scale = jnp.where(isctx, m_c[:, D:2 * D], m_l[:, D:2 * D])
    return n * (1.0 + scale) + shift


def _ln_fwd(h, modv_l, tc, tm, name):
    T = h.shape[0]

    def body(h_ref, m_ref, u_ref):
        isctx = _row_ids(pl.program_id(0), tm) < tc
        u_ref[...] = _u_fn(h_ref[...], m_ref[0:1, :], m_ref[1:2, :], isctx).astype(bf16)

    return pl.pallas_call(
        body, name=name, grid=(T // tm,),
        in_specs=[pl.BlockSpec((tm, D), lambda i: (i, 0)), pl.BlockSpec((SUB, 3 * D), lambda i: (0, 0))],
        out_specs=pl.BlockSpec((tm, D), lambda i: (i, 0)),
        out_shape=jax.ShapeDtypeStruct((T, D), bf16),
        compiler_params=_cparams(("parallel",)),
    )(h, modv_l)


def _ln_bwd(du, h, dh_res, modv_l, tc, tm, name):
    T = h.shape[0]
    nt = T // tm

    def body(du_ref, h_ref, r_ref, m_ref, dh_ref, dm_ref):
        isctx = _row_ids(pl.program_id(0), tm) < tc
        _, vjp = jax.vjp(lambda h, ml, mc: _u_fn(h, ml, mc, isctx), h_ref[...], m_ref[0:1, :], m_ref[1:2, :])
        dh, dml, dmc = vjp(du_ref[...])
        dh_ref[...] = dh + r_ref[...]
        _partial_rows(dm_ref, [dml, dmc])

    return pl.pallas_call(
        body, name=name, grid=(nt,),
        in_specs=[pl.BlockSpec((tm, D), lambda i: (i, 0)), pl.BlockSpec((tm, D), lambda i: (i, 0)),
                  pl.BlockSpec((tm, D), lambda i: (i, 0)), pl.BlockSpec((SUB, 3 * D), lambda i: (0, 0))],
        out_specs=[pl.BlockSpec((tm, D), lambda i: (i, 0)), pl.BlockSpec((None, SUB, 3 * D), lambda i: (i, 0, 0))],
        out_shape=[jax.ShapeDtypeStruct((T, D), f32), jax.ShapeDtypeStruct((nt, SUB, 3 * D), f32)],
        compiler_params=_cparams(("parallel",)),
    )(du, h, dh_res, modv_l)


def _prep_fn(q, k, qg, kg, cos_f, sin_a, sin_b):
    qs = [_rope(_rms(q[:, HD * i:HD * (i + 1)], qg), cos_f, sin_a, sin_b) for i in range(A_HEADS)]
    ks = [_rope(_rms(k[:, HD * i:HD * (i + 1)], kg), cos_f, sin_a, sin_b) for i in range(A_HEADS // 2)]
    return jnp.concatenate(qs, 1), jnp.concatenate(ks, 1)


def _tok(tm, w, off):
    return pl.BlockSpec((tm, w), lambda i: (i, off // w))


def _vec(w):
    return pl.BlockSpec((1, w), lambda i: (0, 0))


def _prep_fwd(P, qg, kg, rope, tm, name):
    T = P.shape[0]

    def body(q_ref, k_ref, v_ref, qg_ref, kg_ref, c_ref, sa_ref, sb_ref, qn_ref, kn_ref, vb_ref):
        qn, kn = _prep_fn(q_ref[...], k_ref[...], qg_ref[...], kg_ref[...], c_ref[...], sa_ref[...], sb_ref[...])
        qn_ref[...] = qn.astype(bf16)
        kn_ref[...] = kn.astype(bf16)
        vb_ref[...] = v_ref[...].astype(bf16)

    return pl.pallas_call(
        body, name=name, grid=(T // tm,),
        in_specs=[_tok(tm, 512, 0), _tok(tm, 256, A_K), _tok(tm, 256, A_V), _vec(HD), _vec(HD),
                  _tok(tm, HD, 0), _tok(tm, HD, 0), _tok(tm, HD, 0)],
        out_specs=[_tok(tm, 512, 0), _tok(tm, 256, 0), _tok(tm, 256, 0)],
        out_shape=[jax.ShapeDtypeStruct((T, 512), bf16), jax.ShapeDtypeStruct((T, 256), bf16),
                   jax.ShapeDtypeStruct((T, 256), bf16)],
        compiler_params=_cparams(("parallel",)),
    )(P, P, P, qg, kg, *rope)


def _prep_bwd(P, dqn, dkn, dv, qg, kg, rope, tm, name):
    T = P.shape[0]
    nt = T // tm

    def body(q_ref, k_ref, dq_ref, dk_ref, dv_ref, qg_ref, kg_ref, c_ref, sa_ref, sb_ref, o_ref, og_ref):
        tabs = (c_ref[...], sa_ref[...], sb_ref[...])
        _, vjp = jax.vjp(lambda q, k, a, b: _prep_fn(q, k, a, b, *tabs), q_ref[...], k_ref[...], qg_ref[...], kg_ref[...])
        dq, dk, dqg, dkg = vjp((dq_ref[...], dk_ref[...]))
        o_ref[:, 0:A_K] = dq.astype(bf16)
        o_ref[:, A_K:A_V] = dk.astype(bf16)
        o_ref[:, A_V:W_A] = dv_ref[...].astype(bf16)
        _partial_rows(og_ref, [dqg, dkg])

    return pl.pallas_call(
        body, name=name, grid=(nt,),
        in_specs=[_tok(tm, 512, 0), _tok(tm, 256, A_K), _tok(tm, 512, 0), _tok(tm, 256, 0), _tok(tm, 256, 0),
                  _vec(HD), _vec(HD), _tok(tm, HD, 0), _tok(tm, HD, 0), _tok(tm, HD, 0)],
        out_specs=[_tok(tm, W_A, 0), pl.BlockSpec((None, SUB, HD), lambda i: (i, 0, 0))],
        out_shape=[jax.ShapeDtypeStruct((T, W_A), bf16), jax.ShapeDtypeStruct((nt, SUB, HD), f32)],
        compiler_params=_cparams(("parallel",)),
    )(P, P, dqn, dkn, dv, qg, kg, *rope)


def _attn_fn(q, k, v, lim):
    s = mm_nt(q, k) * (HD ** -0.5)
    col = lax.broadcasted_iota(jnp.int32, s.shape, 1)
    s = jnp.where(col < lim, s, -1e30)
    m = lax.stop_gradient(jnp.max(s, -1, keepdims=True))
    e = jnp.exp(s - m)
    p = e * (1.0 / jnp.sum(e, -1, keepdims=True))
    return mm(p, v)


def _attn_fwd(qn, kn, vb, tc, tq, name):
    T = qn.shape[0]

    def body(q_ref, k_ref, v_ref, o_ref):
        lim = jnp.where(pl.program_id(1) * tq < tc, tc, T)
        o_ref[...] = _attn_fn(q_ref[...], k_ref[...], v_ref[...], lim)

    return pl.pallas_call(
        body, name=name, grid=(A_HEADS, T // tq),
        in_specs=[pl.BlockSpec((tq, HD), lambda h, i: (i, h)), pl.BlockSpec((T, HD), lambda h, i: (0, h // 2)),
                  pl.BlockSpec((T, HD), lambda h, i: (0, h // 2))],
        out_specs=pl.BlockSpec((tq, HD), lambda h, i: (i, h)),
        out_shape=jax.ShapeDtypeStruct((T, 512), f32),
        compiler_params=_cparams(("parallel", "parallel")),
    )(qn, kn, vb)


def _attn_bwd(qn, kn, vb, dya, tc, tq, name):
    T = qn.shape[0]

    def body(q_ref, k_ref, v_ref, g_ref, dq_ref, dk_ref, dv_ref):
        first = (pl.program_id(1) == 0) & (pl.program_id(2) == 0)
        lim = jnp.where(pl.program_id(2) * tq < tc, tc, T)
        _, vjp = jax.vjp(lambda q, k, v: _attn_fn(q, k, v, lim), q_ref[...].astype(f32), k_ref[...].astype(f32),
                         v_ref[...].astype(f32))
        dq, dk, dv = vjp(g_ref[...])
        dq_ref[...] = dq

        @pl.when(first)
        def _():
            dk_ref[...] = dk
            dv_ref[...] = dv

        @pl.when(jnp.logical_not(first))
        def _():
            dk_ref[...] += dk
            dv_ref[...] += dv

    qspec = pl.BlockSpec((tq, HD), lambda kv, g, i: (i, 2 * kv + g))
    kspec = pl.BlockSpec((T, HD), lambda kv, g, i: (0, kv))
    return pl.pallas_call(
        body, name=name, grid=(A_HEADS // 2, 2, T // tq),
        in_specs=[qspec, kspec, kspec, qspec], out_specs=[qspec, kspec, kspec],
        out_shape=[jax.ShapeDtypeStruct((T, 512), f32), jax.ShapeDtypeStruct((T, 256), f32),
                   jax.ShapeDtypeStruct((T, 256), f32)],
        compiler_params=_cparams(("parallel", "arbitrary", "arbitrary")),
    )(qn, kn, vb, dya)


def _conv_rows(tc, tl):
    return CONV_PAD + tc + CONV_PAD + tl + CONV_PAD


def _fill_pad(pad_ref, val, tc, tl):
    z = jnp.zeros((CONV_PAD, LANE), f32)
    pad_ref[0:CONV_PAD, :] = z
    pad_ref[CONV_PAD:CONV_PAD + tc, :] = val[0:tc]
    pad_ref[CONV_PAD + tc:2 * CONV_PAD + tc, :] = z
    pad_ref[2 * CONV_PAD + tc:2 * CONV_PAD + tc + tl, :] = val[tc:tc + tl]
    pad_ref[2 * CONV_PAD + tc + tl:3 * CONV_PAD + tc + tl, :] = z


def _conv_apply(pad_ref, w_ref, K, tc, tl, rc, emit, flip=False):
    half = K // 2
    for seg0, off, n in ((0, CONV_PAD, tc), (tc, 2 * CONV_PAD + tc, tl)):
        for r0 in range(0, n, rc):
            acc = None
            for k in range(K):
                sh = (half - k) if flip else (k - half)
                term = pad_ref[pl.ds(off + r0 + sh, rc), :] * w_ref[k:k + 1, :]
                acc = term if acc is None else acc + term
            emit(seg0 + r0, acc)


def _conv_wgrad(pad_ref, dy_ref, K, tc, tl, rc, dw_ref):
    half = K // 2
    for k in range(K):
        acc = jnp.zeros((1, LANE), f32)
        for seg0, off, n in ((0, CONV_PAD, tc), (tc, 2 * CONV_PAD + tc, tl)):
            for r0 in range(0, n, rc):
                acc = acc + jnp.sum(pad_ref[pl.ds(off + r0 + k - half, rc), :] * dy_ref[pl.ds(seg0 + r0, rc), :],
                                    axis=0, keepdims=True)
        dw_ref[k:k + 1, :] = acc


def _col(T, off):
    return pl.BlockSpec((T, LANE), lambda j: (0, off // LANE + j))


C_B, C_C, C_X, C_A, C_G = range(5)
N_SEC = 5


class _Sections:
    def __init__(self, refs):
        self.refs = refs

    def __getitem__(self, idx):
        rows, sec = idx
        return self.refs[sec][rows, :]

    def __setitem__(self, idx, val):
        rows, sec = idx
        self.refs[sec, rows, :] = val


def _sec_specs(T):
    return [pl.BlockSpec((T, LANE), functools.partial(lambda j, s: (0, s * (BRW // LANE) + j), s=s)) for s in range(N_SEC)]


def _conv_fwd(P, wb, wd, bd, tc, tl, rc, name):
    T = tc + tl

    def body(*refs):
        p_ref = _Sections(refs[:N_SEC])
        wb_ref, wd_ref, bd_ref, yb_ref, hh_ref, pad_ref = refs[N_SEC:]
        _fill_pad(pad_ref, p_ref[:, C_C] * p_ref[:, C_X], tc, tl)

        def emit_b(r0, y):
            yb_ref[pl.ds(r0, rc), :] = y * p_ref[pl.ds(r0, rc), C_B]

        _conv_apply(pad_ref, wb_ref, KB, tc, tl, rc, emit_b)
        _fill_pad(pad_ref, p_ref[:, C_A] * _sigmoid(p_ref[:, C_G]), tc, tl)

        def emit_d(r0, y):
            hh_ref[pl.ds(r0, rc), :] = y + bd_ref[...]

        _conv_apply(pad_ref, wd_ref, KD, tc, tl, rc, emit_d)

    return pl.pallas_call(
        body, name=name, grid=(BRW // LANE,),
        in_specs=_sec_specs(T) + [pl.BlockSpec((KB, LANE), lambda j: (0, j)), pl.BlockSpec((KD, LANE), lambda j: (0, j)),
                                  pl.BlockSpec((1, LANE), lambda j: (0, j))],
        out_specs=[_col(T, 0), _col(T, 0)],
        out_shape=[jax.ShapeDtypeStruct((T, BRW), f32), jax.ShapeDtypeStruct((T, BRW), f32)],
        scratch_shapes=[pltpu.VMEM((_conv_rows(tc, tl), LANE), f32)],
        compiler_params=_cparams(("parallel",)),
    )(*[P] * N_SEC, wb, wd, bd)


def _conv_bwd(P, dyb, dhh, wb, wd, tc, tl, rc, name):
    T = tc + tl

    def body(*refs):
        p_ref = _Sections(refs[:N_SEC])
        dyb_ref, dhh_ref, wb_ref, wd_ref, dp3_ref, dwb_ref, dwd_ref, dbd_ref, pad_ref, pad2_ref, tmp_ref = refs[N_SEC:]
        dp_ref = _Sections(dp3_ref)
        _fill_pad(pad_ref, p_ref[:, C_C] * p_ref[:, C_X], tc, tl)

        def emit_cv(r0, y):
            dp_ref[pl.ds(r0, rc), C_B] = (y * dyb_ref[pl.ds(r0, rc), :]).astype(bf16)

        _conv_apply(pad_ref, wb_ref, KB, tc, tl, rc, emit_cv)
        tmp_ref[...] = dyb_ref[...] * p_ref[:, C_B]
        _conv_wgrad(pad_ref, tmp_ref, KB, tc, tl, rc, dwb_ref)
        _fill_pad(pad2_ref, tmp_ref[...], tc, tl)

        def emit_ds(r0, y):
            dp_ref[pl.ds(r0, rc), C_C] = (y * p_ref[pl.ds(r0, rc), C_X]).astype(bf16)
            dp_ref[pl.ds(r0, rc), C_X] = (y * p_ref[pl.ds(r0, rc), C_C]).astype(bf16)

        _conv_apply(pad2_ref, wb_ref, KB, tc, tl, rc, emit_ds, flip=True)
        _fill_pad(pad_ref, p_ref[:, C_A] * _sigmoid(p_ref[:, C_G]), tc, tl)
        _conv_wgrad(pad_ref, dhh_ref, KD, tc, tl, rc, dwd_ref)
        dbd_ref[...] = jnp.sum(dhh_ref[...], axis=0, keepdims=True)
        _fill_pad(pad2_ref, dhh_ref[...], tc, tl)

        def emit_d2(r0, y):
            sg = _sigmoid(p_ref[pl.ds(r0, rc), C_G])
            a = p_ref[pl.ds(r0, rc), C_A]
            dp_ref[pl.ds(r0, rc), C_A] = (y * sg).astype(bf16)
            dp_ref[pl.ds(r0, rc), C_G] = (y * a * sg * (1.0 - sg)).astype(bf16)

        _conv_apply(pad2_ref, wd_ref, KD, tc, tl, rc, emit_d2, flip=True)

    return pl.pallas_call(
        body, name=name, grid=(BRW // LANE,),
        in_specs=_sec_specs(T) + [_col(T, 0), _col(T, 0),
                                  pl.BlockSpec((KB, LANE), lambda j: (0, j)), pl.BlockSpec((KD, LANE), lambda j: (0, j))],
        out_specs=[pl.BlockSpec((N_SEC, T, LANE), lambda j: (0, 0, j)), pl.BlockSpec((KB, LANE), lambda j: (0, j)),
                   pl.BlockSpec((KD, LANE), lambda j: (0, j)), pl.BlockSpec((1, LANE), lambda j: (0, j))],
        out_shape=[jax.ShapeDtypeStruct((N_SEC, T, BRW), bf16), jax.ShapeDtypeStruct((KB, BRW), f32),
                   jax.ShapeDtypeStruct((KD, BRW), f32), jax.ShapeDtypeStruct((1, BRW), f32)],
        scratch_shapes=[pltpu.VMEM((_conv_rows(tc, tl), LANE), f32), pltpu.VMEM((_conv_rows(tc, tl), LANE), f32),
                        pltpu.VMEM((T, LANE), f32)],
        compiler_params=_cparams(("parallel",)),
    )(*[P] * N_SEC, dyb, dhh, wb, wd)


def _gla_chunk(q, k, v, r, w2, b2, st, isfwd):
    z = mm(r, w2) + b2
    g = jax.nn.log_sigmoid(z[:, 0:C_KW] if isfwd else z[:, C_KW:2 * C_KW]) / C_TAU
    ri = lax.broadcasted_iota(jnp.int32, (CH, CH), 0)
    ci = lax.broadcasted_iota(jnp.int32, (CH, CH), 1)
    tri = ((ci <= ri) if isfwd else (ci >= ri)).astype(f32)
    cum = jnp.dot(tri, g, preferred_element_type=f32, precision=lax.Precision.HIGHEST)
    last = jnp.sum(g, axis=0, keepdims=True)
    q = q * (C_KW // C_HEADS) ** -0.5
    hv = lax.broadcasted_iota(jnp.int32, (BRW, C_KW), 0) // (BRW // C_HEADS)
    hk = lax.broadcasted_iota(jnp.int32, (BRW, C_KW), 1) // (C_KW // C_HEADS)
    st_new = st * jnp.exp(last) + jnp.where(hv == hk, mm_tn(v, k * jnp.exp(last - cum)), 0.0)
    o = mm_nt(q * jnp.exp(cum), st)
    rowi = lax.broadcasted_iota(jnp.int32, (CH, C_KW), 0)
    srow = lax.broadcasted_iota(jnp.int32, (C_HEADS * CH, C_KW), 0)
    slane = lax.broadcasted_iota(jnp.int32, (C_HEADS * CH, C_KW), 1)
    own_lanes = srow // CH == slane // (C_KW // C_HEADS)
    pos = lax.broadcasted_iota(jnp.int32, (C_HEADS * CH, CH), 0) % CH
    key = lax.broadcasted_iota(jnp.int32, (C_HEADS * CH, CH), 1)
    scores = jnp.zeros((C_HEADS * CH, CH), f32)
    for a in range(CH // GLA_SUB):
        idx = GLA_SUB * a - 1 if isfwd else GLA_SUB * (a + 1)
        ref = jnp.sum(jnp.where(rowi == idx, cum, 0.0), axis=0, keepdims=True)
        qa = q * jnp.exp(jnp.minimum(cum - ref, 0.0))
        ka = k * jnp.exp(jnp.minimum(ref - cum, GLA_CLAMP))
        s = mm_nt(jnp.where(own_lanes, jnp.concatenate([qa] * C_HEADS, axis=0), 0.0), ka)
        scores = scores + jnp.where(pos // GLA_SUB == a, s, 0.0)
    scores = jnp.where((key <= pos) if isfwd else (key >= pos), scores, 0.0)
    vw = BRW // C_HEADS
    o = o + jnp.concatenate([mm(scores[CH * hd:CH * (hd + 1)], v[:, vw * hd:vw * (hd + 1)]) for hd in range(C_HEADS)],
                            axis=1)
    return o, st_new


def _gla_chunk_of(d, n, nc, nch):
    back = jnp.where(n < nc, nc - 1 - n, nch - 1 - (n - nc))
    return jnp.where(d == 0, n, back)


def _gla_fwd(P, w2, b2, tc, name):
    T = P.shape[0]
    nch, nc = T // CH, tc // CH

    back = lambda n: _gla_chunk_of(1, n, nc, nch)

    def body(pf_ref, pb_ref, w_ref, b_ref, of_ref, ob_ref, ssf_ref, ssb_ref, stf_ref, stb_ref):
        @pl.when(pl.program_id(0) == 0)
        def _():
            stf_ref[...] = jnp.zeros_like(stf_ref)
            stb_ref[...] = jnp.zeros_like(stb_ref)

        for p_ref, o_ref, ss_ref, st_ref, isfwd in ((pf_ref, of_ref, ssf_ref, stf_ref, True),
                                                    (pb_ref, ob_ref, ssb_ref, stb_ref, False)):
            st = st_ref[...]
            ss_ref[...] = st
            o, st_new = _gla_chunk(p_ref[:, 0:G_K], p_ref[:, G_K:G_V], p_ref[:, G_V:G_R], p_ref[:, G_R:W_G], w_ref[...],
                                   b_ref[...], st, isfwd)
            o_ref[...] = o
            st_ref[...] = st_new

    sd = jax.ShapeDtypeStruct
    return pl.pallas_call(
        body, name=name, grid=(nch,),
        in_specs=[pl.BlockSpec((CH, W_G), lambda n: (n, 0)), pl.BlockSpec((CH, W_G), lambda n: (back(n), 0)),
                  pl.BlockSpec((LANE, 512), lambda n: (0, 0)), pl.BlockSpec((1, 512), lambda n: (0, 0))],
        out_specs=[pl.BlockSpec((CH, BRW), lambda n: (n, 0)), pl.BlockSpec((CH, BRW), lambda n: (back(n), 0)),
                   pl.BlockSpec((None, BRW, C_KW), lambda n: (n, 0, 0)), pl.BlockSpec((None, BRW, C_KW), lambda n: (n, 0, 0))],
        out_shape=[sd((T, BRW), f32), sd((T, BRW), f32), sd((nch, BRW, C_KW), f32), sd((nch, BRW, C_KW), f32)],
        scratch_shapes=[pltpu.VMEM((BRW, C_KW), f32), pltpu.VMEM((BRW, C_KW), f32)],
        compiler_params=_cparams(("arbitrary",)),
    )(P, P, w2, b2)


def _gla_bwd(P, w2, b2, ssave, doc, tc, name):
    T = P.shape[0]
    nch, nc = T // CH, tc // CH

    fwd_chunk = lambda m: nch - 1 - m
    back_chunk = lambda m: _gla_chunk_of(1, nch - 1 - m, nc, nch)

    def body(pf_ref, pb_ref, w_ref, b_ref, ssf_ref, ssb_ref, gf_ref, gb_ref, dpf_ref, dpb_ref, dw_ref, db_ref,
             dstf_ref, dstb_ref):
        m = pl.program_id(0)

        @pl.when(m == 0)
        def _():
            dstf_ref[...] = jnp.zeros_like(dstf_ref)
            dstb_ref[...] = jnp.zeros_like(dstb_ref)

        dw_sum, db_sum = None, None
        for p_ref, ss_ref, g_ref, dp_ref, dst_ref, isfwd in ((pf_ref, ssf_ref, gf_ref, dpf_ref, dstf_ref, True),
                                                             (pb_ref, ssb_ref, gb_ref, dpb_ref, dstb_ref, False)):
            _, vjp = jax.vjp(lambda q, k, v, r, w, b, st: _gla_chunk(q, k, v, r, w, b, st, isfwd),
                             p_ref[:, 0:G_K], p_ref[:, G_K:G_V], p_ref[:, G_V:G_R], p_ref[:, G_R:W_G], w_ref[...],
                             b_ref[...], ss_ref[...])
            dq, dk, dv, dr, dw, db, dst = vjp((g_ref[...], dst_ref[...]))
            dp_ref[:, 0:G_K] = dq
            dp_ref[:, G_K:G_V] = dk
            dp_ref[:, G_V:G_R] = dv
            dp_ref[:, G_R:W_G] = dr
            dst_ref[...] = dst
            dw_sum = dw if dw_sum is None else dw_sum + dw
            db_sum = db if db_sum is None else db_sum + db

        @pl.when(m == 0)
        def _():
            dw_ref[...] = dw_sum
            _partial_rows(db_ref, [db_sum])

        @pl.when(m > 0)
        def _():
            dw_ref[...] += dw_sum
            db_ref[0:1, :] += db_sum

    ssf, ssb = ssave
    chunk_f = lambda w: pl.BlockSpec((CH, w), lambda m: (fwd_chunk(m), 0))
    chunk_b = lambda w: pl.BlockSpec((CH, w), lambda m: (back_chunk(m), 0))
    state = pl.BlockSpec((None, BRW, C_KW), lambda m: (nch - 1 - m, 0, 0))
    sd = jax.ShapeDtypeStruct
    return pl.pallas_call(
        body, name=name, grid=(nch,),
        in_specs=[chunk_f(W_G), chunk_b(W_G), pl.BlockSpec((LANE, 512), lambda m: (0, 0)), pl.BlockSpec((1, 512), lambda m: (0, 0)),
                  state, state, chunk_f(BRW), chunk_b(BRW)],
        out_specs=[chunk_f(W_G), chunk_b(W_G), pl.BlockSpec((LANE, 512), lambda m: (0, 0)), pl.BlockSpec((SUB, 512), lambda m: (0, 0))],
        out_shape=[sd((T, W_G), f32), sd((T, W_G), f32), sd((LANE, 512), f32), sd((SUB, 512), f32)],
        scratch_shapes=[pltpu.VMEM((BRW, C_KW), f32), pltpu.VMEM((BRW, C_KW), f32)],
        compiler_params=_cparams(("arbitrary",)),
    )(P, P, w2, b2, ssf, ssb, doc, doc)


def _sum_dirs(a, b, tm, name):
    T, W = a.shape

    def body(a_ref, b_ref, o_ref):
        o_ref[...] = (a_ref[...] + b_ref[...]).astype(bf16)

    spec = pl.BlockSpec((tm, W), lambda i: (i, 0))
    return pl.pallas_call(
        body, name=name, grid=(T // tm,), in_specs=[spec, spec], out_specs=spec,
        out_shape=jax.ShapeDtypeStruct((T, W), bf16),
        compiler_params=_cparams(("parallel",)),
    )(a, b)


def _merge_fn(h, m_l, m_c, isctx, ya, ga, yb, gb, of, ob, gc, hh, gd, mg, es, ey, cn, dng, dnb, lg, lb, wbr, wout):
    oc = of + ob
    yc = jnp.concatenate([_rms(oc[:, HD * i:HD * (i + 1)], cn[:, HD * i:HD * (i + 1)]) for i in range(C_HEADS)], 1)
    brs = [ya * _silu(ga), yb * _silu(gb), yc * _silu(gc), _silu(_ln(hh) * dng + dnb) * _silu(gd)]
    acc = None
    for i in range(4):
        t = _sigmoid(mg[:, D * i:D * (i + 1)]) * (mm(brs[i], wbr[i]) + es[i])
        acc = t if acc is None else acc + t
    y = mm(acc, wout) + ey
    gate = jnp.where(isctx, m_c[:, 2 * D:3 * D], m_l[:, 2 * D:3 * D])
    hn = _ln(ALPHA * h + gate * y) * lg + lb
    return hn, (brs, acc)


def _merge_specs(tm):
    t = lambda w, off=0: _tok(tm, w, off)
    return [t(D), pl.BlockSpec((SUB, 3 * D), lambda i: (0, 0)),
            t(BRW), t(BRW, M_GA), t(BRW), t(BRW, M_GB),
            t(BRW), t(BRW),
            t(BRW, M_GC), t(BRW), t(BRW, M_GD), t(4 * D, 0),
            _vec(BRW), _vec(BRW), _vec(BRW), _vec(D), _vec(D),
            pl.BlockSpec((4, BRW, D), lambda i: (0, 0, 0)), pl.BlockSpec((D, D), lambda i: (0, 0))]


def _merge_fwd(h, modv_l, ya, yb, o2, hh, P, cn, dng, dnb, lg, lb, wbr, wout, tc, tm, name):
    T = h.shape[0]

    def body(h_ref, m_ref, ya_ref, ga_ref, yb_ref, gb_ref, of_ref, ob_ref, gc_ref, hh_ref, gd_ref, mg_ref,
             cn_ref, dng_ref, dnb_ref, lg_ref, lb_ref, wbr_ref, wout_ref, o_ref):
        isctx = _row_ids(pl.program_id(0), tm) < tc
        zero = jnp.zeros((tm, D), f32)
        hn, _ = _merge_fn(h_ref[...], m_ref[0:1, :], m_ref[1:2, :], isctx, ya_ref[...], ga_ref[...], yb_ref[...],
                          gb_ref[...], of_ref[...], ob_ref[...], gc_ref[...], hh_ref[...], gd_ref[...], mg_ref[...],
                          [zero] * 4, zero, cn_ref[...], dng_ref[...], dnb_ref[...], lg_ref[...], lb_ref[...],
                          [wbr_ref[i] for i in range(4)], wout_ref[...])
        o_ref[...] = hn

    return pl.pallas_call(
        body, name=name, grid=(T // tm,),
        in_specs=_merge_specs(tm), out_specs=_tok(tm, D, 0),
        out_shape=jax.ShapeDtypeStruct((T, D), f32),
        compiler_params=_cparams(("parallel",)),
    )(h, modv_l, ya, P, yb, P, o2[0], o2[1], P, hh, P, P, cn, dng, dnb, lg, lb, wbr, wout)


def _merge_bwd(dhn, h, modv_l, ya, yb, o2, hh, P, cn, dng, dnb, lg, lb, wbr, wout, tc, tm, name):
    T = h.shape[0]
    nt = T // tm

    def body(g_ref, h_ref, m_ref, ya_ref, ga_ref, yb_ref, gb_ref, of_ref, ob_ref, gc_ref, hh_ref, gd_ref, mg_ref,
             cn_ref, dng_ref, dnb_ref, lg_ref, lb_ref, wbr_ref, wout_ref,
             dh_ref, dm_ref, dya_ref, dyb_ref, doc_ref, dhh_ref, dp_ref,
             br_ref, z_ref, acc_ref, dy_ref, dv5_ref, dvd_ref):
        isctx = _row_ids(pl.program_id(0), tm) < tc
        zero = jnp.zeros((tm, D), f32)
        wbr_v = [wbr_ref[i] for i in range(4)]
        wout_v = wout_ref[...]

        def fn(h, ml, mc, ya, ga, yb, gb, oc, gc, hh, gd, mg, e0, e1, e2, e3, ey, cn, dng, dnb, lg, lb):
            return _merge_fn(h, ml, mc, isctx, ya, ga, yb, gb, oc, jnp.zeros_like(oc), gc, hh, gd, mg,
                             [e0, e1, e2, e3], ey, cn, dng, dnb, lg, lb, wbr_v, wout_v)

        _, vjp, (brs, acc) = jax.vjp(
            fn, h_ref[...], m_ref[0:1, :], m_ref[1:2, :], ya_ref[...], ga_ref[...], yb_ref[...], gb_ref[...],
            of_ref[...] + ob_ref[...], gc_ref[...], hh_ref[...], gd_ref[...], mg_ref[...], zero, zero, zero, zero, zero,
            cn_ref[...], dng_ref[...], dnb_ref[...], lg_ref[...], lb_ref[...], has_aux=True)
        (dh, dml, dmc, dya, dga, dyb, dgb, doc, dgc, dhh, dgd, dmg, z0, z1, z2, z3, dy,
         dcn, ddng, ddnb, dlg, dlb) = vjp(g_ref[...])
        dh_ref[...] = dh
        _partial_rows(dm_ref, [dml, dmc])
        dya_ref[...] = dya
        dyb_ref[...] = dyb
        doc_ref[...] = doc
        dhh_ref[...] = dhh
        dp_ref[:, 0:M_GA] = dmg.astype(bf16)
        dp_ref[:, M_GA:M_GB] = dga.astype(bf16)
        dp_ref[:, M_GB:M_GC] = dgb.astype(bf16)
        dp_ref[:, M_GC:M_GD] = dgc.astype(bf16)
        dp_ref[:, M_GD:W_M] = dgd.astype(bf16)
        for i, z in enumerate((z0, z1, z2, z3)):
            br_ref[i] = brs[i].astype(bf16)
            z_ref[i] = z.astype(bf16)
        acc_ref[...] = acc.astype(bf16)
        dy_ref[...] = dy.astype(bf16)
        _partial_rows(dv5_ref, [dcn, ddng, ddnb])
        _partial_rows(dvd_ref, [dlg, dlb])

    t = lambda w: _tok(tm, w, 0)
    part = lambda w: pl.BlockSpec((None, SUB, w), lambda i: (i, 0, 0))
    sd = jax.ShapeDtypeStruct
    return pl.pallas_call(
        body, name=name, grid=(nt,),
        in_specs=[t(D)] + _merge_specs(tm),
        out_specs=[t(D), part(3 * D)] + [t(BRW)] * 4 + [t(W_M),
                   pl.BlockSpec((4, tm, BRW), lambda i: (0, i, 0)), pl.BlockSpec((4, tm, D), lambda i: (0, i, 0)),
                   t(D), t(D), part(BRW), part(D)],
        out_shape=[sd((T, D), f32), sd((nt, SUB, 3 * D), f32)] + [sd((T, BRW), f32)] * 4 + [sd((T, W_M), bf16),
                   sd((4, T, BRW), bf16), sd((4, T, D), bf16), sd((T, D), bf16), sd((T, D), bf16),
                   sd((nt, SUB, BRW), f32), sd((nt, SUB, D), f32)],
        compiler_params=_cparams(("parallel",)),
    )(dhn, h, modv_l, ya, P, yb, P, o2[0], o2[1], P, hh, P, P, cn, dng, dnb, lg, lb, wbr, wout)


def _loss_kernel(h, tgt, tc, tm, name):
    T = h.shape[0]
    nt = T // tm
    nct = tc // tm

    def body(h_ref, t_ref, d_ref, l_ref):
        i = pl.program_id(0)
        err = h_ref[...] - t_ref[...]
        lat = (i >= nct).astype(f32)
        d_ref[...] = err * (lat / D)
        l_ref[...] = jnp.zeros((SUB, LANE), f32) + lat * 0.5 * jnp.sum(err * err) / D

    return pl.pallas_call(
        body, name=name, grid=(nt,),
        in_specs=[pl.BlockSpec((tm, D), lambda i: (i, 0)),
                  pl.BlockSpec((tm, D), lambda i: (jnp.maximum(i - nct, 0), 0))],
        out_specs=[pl.BlockSpec((tm, D), lambda i: (i, 0)), pl.BlockSpec((None, SUB, LANE), lambda i: (i, 0, 0))],
        out_shape=[jax.ShapeDtypeStruct((T, D), f32), jax.ShapeDtypeStruct((nt, SUB, LANE), f32)],
        compiler_params=_cparams(("parallel",)),
    )(h, tgt)


def _rope_tables(tc, tl):
    t = jnp.arange(tl)
    inv = ROPE_THETA ** (-jnp.arange(0, HD // 2, 2, dtype=f32) / (HD // 2))
    ang = jnp.concatenate([(t // GRID_W).astype(f32)[:, None] * inv, (t % GRID_W).astype(f32)[:, None] * inv], -1)
    cos, sin = jnp.repeat(jnp.cos(ang), 2, axis=1), jnp.repeat(jnp.sin(ang), 2, axis=1)
    even = (jnp.arange(HD) % 2 == 0)[None, :]
    cos_f = jnp.concatenate([jnp.ones((tc, HD), f32), cos], 0)
    sin_a = jnp.concatenate([jnp.zeros((tc, HD), f32), jnp.where(even, -sin, 0.0)], 0)
    sin_b = jnp.concatenate([jnp.zeros((tc, HD), f32), jnp.where(even, 0.0, sin)], 0)
    return cos_f, sin_a, sin_b


N_CHIPS = 4
SHARD = N_IN // N_CHIPS


def _group_ranges():
    return dict(M=[(S_MG, 4 * D), (S_GA, BRW), (S_GB, BRW), (S_GC, BRW), (S_GD, BRW)], A=[(S_Q, W_A)],
                C=[(S_B, 3 * BRW), (S_DA, 2 * BRW)], G=[(S_CQ, 2 * C_KW + BRW), (S_R, 2 * C_RANK)])


def _group_weights(w4):
    out = {}
    for k, ranges in _group_ranges().items():
        parts = []
        for a, n in ranges:
            while n > 0:
                s, r = divmod(a, SHARD)
                m = min(n, SHARD - r)
                parts.append(w4[s, r:r + m])
                a, n = a + m, n - m
        if k == "G":
            parts.append(jnp.zeros((LANE - 2 * C_RANK, D), w4.dtype))
        out[k] = jnp.concatenate(parts, 0)
    return out


def _ungroup(g):
    secs = []
    for k, ranges in _group_ranges().items():
        off = 0
        for a, n in ranges:
            secs.append((a, g[k][off:off + n]))
            off += n
    return jnp.concatenate([v for _, v in sorted(secs, key=lambda t: t[0])], 0)


PROJ_TN = dict(M=2048, A=1024, C=1280, G=1152)
DU_TK = dict(M=2048, A=1024, C=BRW, G=1152)
DWP_TN = dict(M=768, A=1024, C=BRW, G=1152)


def _gate_weights(w2_l, gb_l):
    w = jnp.zeros((LANE, 2 * C_KW), f32)
    w = w.at[0:C_RANK, 0:C_KW].set(w2_l[0]).at[C_RANK:2 * C_RANK, C_KW:2 * C_KW].set(w2_l[1])
    return w, jnp.concatenate([gb_l[0], gb_l[1]])[None, :]


def _local_step(x1, c1, ctx1, tgt1, c_ctx, w_mod, b_mod, weights_of, q_norm, k_norm, b_conv, w2, gb, c_norm, d_conv_w,
                d_conv_b, d_norm_g, d_norm_b, grads_done, ln_g, ln_b, tm, token=None):
    tc, tl = ctx1.shape[0], x1.shape[0]
    T = tc + tl
    rc = min(256, tc)
    tmb = tm // 2
    tmm = 768 if T % 768 == 0 else tm
    rope = _rope_tables(tc, tl)
    cin = jnp.concatenate([c1, c_ctx[None, :], jnp.zeros((SUB - 2, D), f32)], 0)
    if token is not None:
        cin = cin + token[:, 0:1]
    modv = _mod_fwd(cin, w_mod, b_mod)
    modv = [modv[l] for l in range(DEPTH)]
    row = lambda v: v[None, :]

    h = jnp.concatenate([ctx1, x1], 0)
    saved, wp, w_br, w_out = [], [None] * DEPTH, [None] * DEPTH, [None] * DEPTH
    for l in range(DEPTH):
        wp[l], merge_weights = weights_of(l, h)
        u = _ln_fwd(h, modv[l], tc, tm, f"ln_fwd{l}")
        P = {k: _matmul(u, wp[l][k], "nt", tmm, PROJ_TN[k], D, f"proj{l}{k}") for k in GROUPS}
        qn, kn, vb = _prep_fwd(P["A"], row(q_norm[l]), row(k_norm[l]), rope, tm, f"prep_fwd{l}")
        ya = _attn_fwd(qn, kn, vb, tc, tm, f"attn_fwd{l}")
        yb, hh = _conv_fwd(P["C"], b_conv[l], d_conv_w[l], row(d_conv_b[l]), tc, tl, rc, f"conv_fwd{l}")
        w2p, b2p = _gate_weights(w2[l], gb[l])
        gla = _gla_fwd(P["G"], w2p, b2p, tc, f"gla_fwd{l}")
        o2, ssave = gla[:2], gla[2:]
        w_br[l], w_out[l] = merge_weights(o2[0])
        hn = _merge_fwd(h, modv[l], ya, yb, o2, hh, P["M"], row(c_norm[l]), row(d_norm_g[l]), row(d_norm_b[l]),
                        row(ln_g[l]), row(ln_b[l]), w_br[l], w_out[l], tc, tm, f"merge_fwd{l}")
        saved.append((h, u, P, qn, kn, vb, ya, yb, hh, o2, ssave, w2p, b2p))
        h = hn

    dh, lparts = _loss_kernel(h, tgt1, tc, tm, "loss")
    loss = jnp.sum(lparts[:, 0, 0])

    g = {k: [None] * DEPTH for k in ("wp", "q_norm", "k_norm", "b_conv", "w2", "gb", "c_norm", "d_conv_w", "d_conv_b",
                                     "d_norm_g", "d_norm_b", "w_br", "w_out", "ln_g", "ln_b", "modv")}
    for l in reversed(range(DEPTH)):
        h_in, u, P, qn, kn, vb, ya, yb, hh, o2, ssave, w2p, b2p = saved[l]
        dP = {}
        (dh_res, dm_mg, dya, dyb, doc, dhh, dP["M"], br, z, acc, dy, dv5, dvd) = _merge_bwd(
            dh, h_in, modv[l], ya, yb, o2, hh, P["M"], row(c_norm[l]), row(d_norm_g[l]), row(d_norm_b[l]),
            row(ln_g[l]), row(ln_b[l]), w_br[l], w_out[l], tc, tmb, f"merge_bwd{l}")
        g["w_br"][l] = _matmul_tn_batched(br, z, N_CHIPS, f"dwbr{l}")
        g["w_out"][l] = _matmul(acc, dy, "tn", D, D, T, f"dwout{l}")
        tk = grads_done(l, {k: g[k][l] for k in ("w_br", "w_out")})
        qg_l = row(q_norm[l]) if tk is None else row(q_norm[l]) + tk[0:1, :]
        v5 = jnp.sum(dv5, 0)
        g["c_norm"][l], g["d_norm_g"][l], g["d_norm_b"][l] = v5[0], v5[1], v5[2]
        vd = jnp.sum(dvd, 0)
        g["ln_g"][l], g["ln_b"][l] = vd[0], vd[1]
        dqn, dkn, dv = _attn_bwd(qn, kn, vb, dya, tc, tm, f"attn_bwd{l}")
        dP["A"], dqk = _prep_bwd(P["A"], dqn, dkn, dv, qg_l, row(k_norm[l]), rope, tm, f"prep_bwd{l}")
        dqk = jnp.sum(dqk, 0)
        g["q_norm"][l], g["k_norm"][l] = dqk[0], dqk[1]
        dP["C"], dwb, dwd, dbd = _conv_bwd(P["C"], dyb, dhh, b_conv[l], d_conv_w[l], tc, tl, rc, f"conv_bwd{l}")
        g["b_conv"][l], g["d_conv_w"][l], g["d_conv_b"][l] = dwb, dwd, dbd[0]
        dpf, dpb, dw2p, db2p = _gla_bwd(P["G"], w2p, b2p, ssave, doc, tc, f"gla_bwd{l}")
        dP["G"] = _sum_dirs(dpf, dpb, tm, f"gla_sum{l}")
        db2p = db2p[0]
        g["w2"][l] = jnp.stack([dw2p[0:C_RANK, 0:C_KW], dw2p[C_RANK:2 * C_RANK, C_KW:2 * C_KW]])
        g["gb"][l] = jnp.stack([db2p[0:C_KW], db2p[C_KW:2 * C_KW]])
        du = None
        for k in GROUPS:
            du = _matmul(dP[k], wp[l][k], "nn", tmm, D, DU_TK[k], f"du{l}{k}", add=du)
        g["wp"][l] = {k: _matmul(dP[k], u, "tn", DWP_TN[k], D, T, f"dwp{l}{k}") for k in GROUPS}
        dh, dm_ln = _ln_bwd(du, h_in, dh_res, modv[l], tc, tm, f"ln_bwd{l}")
        g["modv"][l] = jnp.sum(dm_mg, 0) + jnp.sum(dm_ln, 0)
        tk = grads_done(l, {"wp": g["wp"][l]})
        if tk is not None and l > 0:
            modv[l - 1] = modv[l - 1] + tk[:, 0:1]

    dmodv = jnp.stack(g.pop("modv"))
    g["w_mod"], dcin = _mod_bwd(cin, w_mod, dmodv)
    g["b_mod"] = dmodv[:, 0, :] + dmodv[:, 1, :]
    g["c_ctx"] = jnp.sum(dcin, (0, 1))[1]
    return loss, dh[tc:], g


HALF_TL = 256


def _adamw(w, g, m, v, name, tr=128, after=None):
    L, R, C = w.shape
    if R % tr == 0:
        grid, spec = (L, R // tr), pl.BlockSpec((None, tr, C), lambda l, i: (l, i, 0))
    elif R * C * 4 <= (1 << 20):
        grid, spec = (L, 1), pl.BlockSpec((None, R, C), lambda l, i: (l, 0, 0))
    else:
        grid, spec = (L, C // HALF_TL), pl.BlockSpec((None, R, HALF_TL), lambda l, i: (l, 0, i))

    def body(w_ref, g_ref, m_ref, v_ref, *rest):
        d_ref, nm_ref, nv_ref = rest[-3:]
        gg = g_ref[...]
        nm = B1 * m_ref[...] + (1.0 - B1) * gg
        nv = B2 * v_ref[...] + (1.0 - B2) * (gg * gg)
        m_hat = nm / (1.0 - B1 ** STEP)
        v_hat = nv / (1.0 - B2 ** STEP)
        d_ref[...] = -LR * (m_hat / (jnp.sqrt(v_hat) + AEPS) + WD * w_ref[...])
        nm_ref[...] = nm
        nv_ref[...] = nv

    return pl.pallas_call(
        body, name=name, grid=grid, in_specs=[spec] * 4 + ([] if after is None else [pl.BlockSpec(memory_space=pl.ANY)]),
        out_specs=[spec] * 3, out_shape=[jax.ShapeDtypeStruct((L, R, C), f32)] * 3,
        compiler_params=_cparams(("parallel", "parallel")),
    )(w, g, m, v, *([] if after is None else [after]))


MESH = pl.DeviceIdType.MESH
ANY = pl.BlockSpec(memory_space=pl.ANY)
N_CHIPS = 4


def _place():
    x, y, c = lax.axis_index("x"), lax.axis_index("y"), lax.axis_index("c")
    chips = [(1 - x, y), (x, 1 - y), (1 - x, 1 - y)]
    return x, y, c, chips


def _half(ref, c, axis):
    n = ref.shape[axis] // 2
    last = axis in (-1, ref.ndim - 1)
    idx = [slice(None)] * ref.ndim
    idx[axis] = pl.ds(pl.multiple_of(c * n, LANE if last else SUB), n)
    return ref.at[tuple(idx)]


def _half_shape(shape, axis):
    s = list(shape)
    s[axis] //= 2
    return tuple(s)


def _all_gather(arrs, axes, name):
    n = len(arrs)

    def body(*refs):
        ins, outs = refs[:n], refs[n:2 * n]
        send, recv = refs[2 * n:]
        x, y, c, chips = _place()
        me, sib = 2 * x + y, (x, y, 1 - c)

        def copy(a, k, chip_idx, cc, to, src=None):
            blk = _half(outs[a].at[chip_idx], cc, axes[a])
            return pltpu.make_async_remote_copy(src_ref=blk if src is None else src, dst_ref=blk,
                                                send_sem=send.at[7 * a + k], recv_sem=recv.at[7 * a + k],
                                                device_id=to, device_id_type=MESH)

        own = [pltpu.make_async_remote_copy(src_ref=ins[a], dst_ref=outs[a].at[me], send_sem=send.at[7 * a + 6],
                                            recv_sem=recv.at[7 * a + 6], device_id=sib, device_id_type=MESH)
               for a in range(n)]
        first = own + [copy(a, j, me, c, (*chip, c), src=_half(ins[a], c, axes[a]))
                       for a in range(n) for j, chip in enumerate(chips)]
        for cp in first:
            cp.start()
        passed = []
        for a in range(n):
            for j, chip in enumerate(chips):
                k = 2 * chip[0] + chip[1]
                copy(a, j, k, c, sib).wait_recv()
                fwd = copy(a, 3 + j, k, c, sib)
                fwd.start()
                passed.append(fwd)
        for a in range(n):
            own[a].wait_recv()
            for j, chip in enumerate(chips):
                copy(a, 3 + j, 2 * chip[0] + chip[1], 1 - c, sib).wait_recv()
        for cp in first + passed:
            cp.wait_send()

    return pl.pallas_call(
        body, name=name, in_specs=[ANY] * n, out_specs=[ANY] * n,
        out_shape=[jax.ShapeDtypeStruct((N_CHIPS,) + a.shape, a.dtype) for a in arrs],
        scratch_shapes=[pltpu.SemaphoreType.DMA((7 * n,)), pltpu.SemaphoreType.DMA((7 * n,))],
    )(*arrs)


def _sibling_halves(arrs, axes, name):
    n = len(arrs)

    def body(*refs):
        ins, outs = refs[:n], refs[n:2 * n]
        send, recv = refs[2 * n:]
        x, y, c, _ = _place()
        cps = [pltpu.make_async_remote_copy(src_ref=_half(ins[a], 1 - c, axes[a] + 1), dst_ref=outs[a], send_sem=send.at[a],
                                            recv_sem=recv.at[a], device_id=(x, y, 1 - c), device_id_type=MESH)
               for a in range(n)]
        for cp in cps:
            cp.start()
        for cp in cps:
            cp.wait()

    return pl.pallas_call(
        body, name=name, in_specs=[ANY] * n, out_specs=[ANY] * n,
        out_shape=[jax.ShapeDtypeStruct(_half_shape(a.shape, axes[i] + 1), a.dtype) for i, a in enumerate(arrs)],
        scratch_shapes=[pltpu.SemaphoreType.DMA((n,)), pltpu.SemaphoreType.DMA((n,))],
    )(*arrs)


def _add_half(gfull, land, cidx, axis, name, tr=128, out_dtype=bf16):
    _, hr, hc = land.shape
    if axis == 0:
        tr = min(tr, hr)
        nb, blk = hr // tr, (None, tr, hc)
        g_spec = pl.BlockSpec(blk, lambda s, i, cr: (s, cr[0] * nb + i, 0))
        l_spec = pl.BlockSpec(blk, lambda s, i, cr: (s, i, 0))
    else:
        nb, blk = hc // HALF_TL, (None, hr, HALF_TL)
        g_spec = pl.BlockSpec(blk, lambda s, i, cr: (s, 0, cr[0] * nb + i))
        l_spec = pl.BlockSpec(blk, lambda s, i, cr: (s, 0, i))

    def body(c_ref, g_ref, l_ref, o_ref):
        o_ref[...] = (g_ref[...].astype(f32) + l_ref[...].astype(f32)).astype(o_ref.dtype)

    return pl.pallas_call(
        body, name=name,
        grid_spec=pltpu.PrefetchScalarGridSpec(
            num_scalar_prefetch=1, grid=(N_CHIPS, nb), in_specs=[g_spec, l_spec], out_specs=l_spec),
        out_shape=jax.ShapeDtypeStruct((N_CHIPS, hr, hc), out_dtype),
        compiler_params=_cparams(("parallel", "parallel")),
    )(cidx, gfull, land)


def _chip_exchange(arrs, name):
    n = len(arrs)

    def body(*refs):
        ins, outs = refs[:n], refs[n:2 * n]
        send, recv = refs[2 * n:]
        x, y, c, chips = _place()
        me = 2 * x + y
        cps = []
        for a in range(n):
            for j, chip in enumerate(chips):
                k = 2 * chip[0] + chip[1]
                cps.append((pltpu.make_async_remote_copy(
                    src_ref=ins[a].at[k], dst_ref=outs[a].at[me], send_sem=send.at[3 * a + j], recv_sem=recv.at[3 * a + j],
                    device_id=(*chip, c), device_id_type=MESH), a, j, k))
        for cp, *_ in cps:
            cp.start()
        for cp, a, j, k in cps:
            pltpu.make_async_remote_copy(src_ref=ins[a].at[k], dst_ref=outs[a].at[k], send_sem=send.at[3 * a + j],
                                         recv_sem=recv.at[3 * a + j], device_id=(x, y, c), device_id_type=MESH).wait_recv()
        for cp, *_ in cps:
            cp.wait_send()

    return pl.pallas_call(
        body, name=name, in_specs=[ANY] * n, out_specs=[ANY] * n,
        out_shape=[jax.ShapeDtypeStruct(a.shape, a.dtype) for a in arrs],
        scratch_shapes=[pltpu.SemaphoreType.DMA((3 * n,)), pltpu.SemaphoreType.DMA((3 * n,))],
    )(*arrs)


def _sum_chips(land, own, place, axis, layer, into, name, tr=128):
    _, hr, hc = land.shape
    fresh = not hasattr(into, "dtype")
    shape = tuple(into) if fresh else into.shape
    if axis == 0:
        tr = min(tr, hr)
        nb, blk = hr // tr, (tr, hc)
        l_map, m_map = (lambda i, p: (0, i, 0)), (lambda i, p: (p[0], i, 0))
        o_map = lambda i, p: (layer, p[1] * nb + i, 0)
    else:
        nb, blk = hc // HALF_TL, (hr, HALF_TL)
        l_map, m_map = (lambda i, p: (0, 0, i)), (lambda i, p: (p[0], 0, i))
        o_map = lambda i, p: (layer, 0, p[1] * nb + i)

    def body(p_ref, l_ref, o_ref, *rest):
        me = p_ref[0]
        mine = o_ref[...].astype(f32)
        acc = None
        for k in range(N_CHIPS):
            t = jnp.where(me == k, mine, l_ref[k].astype(f32))
            acc = t if acc is None else acc + t
        rest[-1][...] = acc

    return pl.pallas_call(
        body, name=name,
        grid_spec=pltpu.PrefetchScalarGridSpec(
            num_scalar_prefetch=1, grid=(nb,),
            in_specs=[pl.BlockSpec((N_CHIPS,) + blk, l_map), pl.BlockSpec((None,) + blk, m_map)] + ([] if fresh else [ANY]),
            out_specs=pl.BlockSpec((None,) + blk, o_map)),
        out_shape=jax.ShapeDtypeStruct(shape, f32),
        input_output_aliases={} if fresh else {3: 0},
        compiler_params=_cparams(("parallel",)),
    )(place, land, own, *([] if fresh else [into]))


def _sibling_fill(arrs, axes, name):
    n = len(arrs)

    def body(*refs):
        outs = refs[n:2 * n]
        send, recv = refs[2 * n:]
        x, y, c, _ = _place()
        cps = [pltpu.make_async_remote_copy(src_ref=_half(outs[a], c, axes[a] + 1), dst_ref=_half(outs[a], c, axes[a] + 1),
                                            send_sem=send.at[a], recv_sem=recv.at[a], device_id=(x, y, 1 - c),
                                            device_id_type=MESH) for a in range(n)]
        for cp in cps:
            cp.start()
        for a in range(n):
            blk = _half(outs[a], 1 - c, axes[a] + 1)
            pltpu.make_async_remote_copy(src_ref=blk, dst_ref=blk, send_sem=send.at[a], recv_sem=recv.at[a],
                                         device_id=(x, y, 1 - c), device_id_type=MESH).wait_recv()
        for cp in cps:
            cp.wait_send()

    return pl.pallas_call(
        body, name=name, in_specs=[ANY] * n, out_specs=[ANY] * n,
        out_shape=[jax.ShapeDtypeStruct(a.shape, a.dtype) for a in arrs],
        input_output_aliases={a: a for a in range(n)},
        scratch_shapes=[pltpu.SemaphoreType.DMA((n,)), pltpu.SemaphoreType.DMA((n,))],
    )(*arrs)


HBM = pl.BlockSpec(memory_space=pltpu.HBM)
SEM = pl.BlockSpec(memory_space=pltpu.SEMAPHORE)
EFFECT = pltpu.SideEffectType.DATAFLOW_SIDE_EFFECTING
PEERS = 4


def _split_copies(srcs, lands, send, recv, gather):
    x, y, c, chips = _place()
    me = 2 * x + y
    peers = [((*chip, c), 2 * chip[0] + chip[1]) for chip in chips] + ([((x, y, 1 - c), me)] if gather else [])
    out = []
    for a in range(len(srcs)):
        for j, (dev, k) in enumerate(peers):
            src = srcs[a] if gather else srcs[a].at[k]
            sems = dict(send_sem=send.at[PEERS * a + j], recv_sem=recv.at[PEERS * a + j], device_id=dev, device_id_type=MESH)
            out.append((pltpu.make_async_remote_copy(src_ref=src, dst_ref=lands[a].at[me], **sems),
                        pltpu.make_async_remote_copy(src_ref=src, dst_ref=lands[a].at[k], **sems)))
    return out


def _split_start(srcs, gather, after, name):
    n = len(srcs)
    lands = [lax.empty(((N_CHIPS,) + s.shape) if gather else s.shape, s.dtype) for s in srcs]

    def body(*refs):
        send, recv = refs[2 * n + 1], refs[2 * n + 2]
        for start, _ in _split_copies(refs[:n], refs[n:2 * n], send, recv, gather):
            start.start()
        refs[-1][...] = jnp.zeros_like(refs[-1])

    sems = pltpu.SemaphoreType.DMA((PEERS * n,))
    hbm = lambda a: pltpu.with_memory_space_constraint(a, pltpu.HBM)
    out = pl.pallas_call(
        body, name=name,
        out_shape=(sems, sems, *[pltpu.HBM(a.shape, a.dtype) for a in srcs + lands], jax.ShapeDtypeStruct((SUB, LANE), f32)),
        in_specs=[HBM] * (2 * n) + [ANY], out_specs=(SEM, SEM, *[HBM] * (2 * n), pl.BlockSpec(memory_space=pltpu.VMEM)),
        input_output_aliases={i: 2 + i for i in range(2 * n)},
        compiler_params=pltpu.CompilerParams(has_side_effects=EFFECT),
    )(*[hbm(a) for a in srcs + lands], after)
    return out[0], out[1], list(out[2:2 + n]), list(out[2 + n:2 + 2 * n]), out[-1]


def _split_wait(send, recv, srcs, lands, gather, after, name):
    n = len(srcs)

    def body(*refs):
        for start, arrival in _split_copies(refs[:n], refs[n:2 * n], refs[2 * n], refs[2 * n + 1], gather):
            start.wait_send()
            arrival.wait_recv()

    out = pl.pallas_call(
        body, name=name, out_shape=[pltpu.HBM(a.shape, a.dtype) for a in srcs + lands],
        in_specs=[HBM] * (2 * n) + [SEM, SEM, ANY], out_specs=[HBM] * (2 * n),
        input_output_aliases={i: i for i in range(2 * n)},
        compiler_params=pltpu.CompilerParams(has_side_effects=EFFECT),
    )(*srcs, *lands, send, recv, after)
    return list(out[:n]), list(out[n:])


N_DEV = 8


def _all_reduce_small(v, name):
    R = v.shape[0]

    def body(v_ref, o_ref, land_ref, send, recv):
        x, y, c, _ = _place()
        me = 4 * x + 2 * y + c
        land_ref[me] = v_ref[...]
        cps = []
        for m in range(1, N_DEV):
            px, py, pc = [(1 - q) if (m >> s) & 1 else q for q, s in ((x, 2), (y, 1), (c, 0))]
            cps.append((pltpu.make_async_remote_copy(src_ref=v_ref, dst_ref=land_ref.at[me], send_sem=send.at[m - 1],
                                                     recv_sem=recv.at[m - 1], device_id=(px, py, pc), device_id_type=MESH),
                        4 * px + 2 * py + pc, m))
        for cp, *_ in cps:
            cp.start()
        for cp, peer, m in cps:
            pltpu.make_async_remote_copy(src_ref=v_ref, dst_ref=land_ref.at[peer], send_sem=send.at[m - 1],
                                         recv_sem=recv.at[m - 1], device_id=(x, y, c), device_id_type=MESH).wait_recv()
        for cp, *_ in cps:
            cp.wait_send()
        acc = land_ref[0]
        for k in range(1, N_DEV):
            acc = acc + land_ref[k]
        o_ref[...] = acc

    vm = pl.BlockSpec(memory_space=pltpu.VMEM)
    return pl.pallas_call(
        body, name=name, in_specs=[vm], out_specs=vm, out_shape=jax.ShapeDtypeStruct(v.shape, f32),
        scratch_shapes=[pltpu.VMEM((N_DEV, R, LANE), f32), pltpu.SemaphoreType.DMA((N_DEV - 1,)),
                        pltpu.SemaphoreType.DMA((N_DEV - 1,))],
        compiler_params=pltpu.CompilerParams(vmem_limit_bytes=VMEM_LIMIT),
    )(v)


def _pack_small(arrs, mult=2 * SUB):
    flat = jnp.concatenate([a.reshape(-1) for a in arrs])
    rows = -(-flat.shape[0] // (LANE * mult)) * mult
    return jnp.pad(flat, (0, rows * LANE - flat.shape[0])).reshape(rows, LANE)


def _unpack_small(vec, shapes):
    flat, out, o = vec.reshape(-1), [], 0
    for s in shapes:
        n = int(np.prod(s))
        out.append(flat[o:o + n].reshape(s))
        o += n
    return out


REPL_SMALL = ("c_ctx", "b_mod", "q_norm", "k_norm", "c_norm", "d_conv_b", "d_norm_g", "d_norm_b", "ln_g", "ln_b")
SHARD_SMALL = ("b_conv", "c_gate_w2", "c_gate_b", "d_conv_w")
BIG = ("w_mod", "w_in", "w_br", "w_out")
ORDER = ("c_ctx", "w_mod", "b_mod", "w_in", "q_norm", "k_norm", "b_conv", "c_gate_w2", "c_gate_b", "c_norm", "d_conv_w",
         "d_conv_b", "d_norm_g", "d_norm_b", "w_br", "w_out", "ln_g", "ln_b")


def _unshard_last(g4, shard_shape):
    g = g4.reshape((N_CHIPS,) + tuple(shard_shape))
    g = jnp.moveaxis(g, 0, -2)
    return g.reshape(tuple(shard_shape[:-1]) + (N_CHIPS * shard_shape[-1],))


def _pieces_last(full):
    w = full.shape[-1] // N_CHIPS
    g = full.reshape(full.shape[:-1] + (N_CHIPS, w))
    return jnp.moveaxis(g, -2, 0).reshape(N_CHIPS, -1, w)


def kernel(x, c, ctx, c_ctx, w_mod, b_mod, w_in, q_norm, k_norm, b_conv, c_gate_w2, c_gate_b, c_norm, d_conv_w, d_conv_b, d_norm_g, d_norm_b, w_br, w_out, ln_g, ln_b, loss_target, m_c_ctx, m_w_mod, m_b_mod, m_w_in, m_q_norm, m_k_norm, m_b_conv, m_c_gate_w2, m_c_gate_b, m_c_norm, m_d_conv_w, m_d_conv_b, m_d_norm_g, m_d_norm_b, m_w_br, m_w_out, m_ln_g, m_ln_b, v_c_ctx, v_w_mod, v_b_mod, v_w_in, v_q_norm, v_k_norm, v_b_conv, v_c_gate_w2, v_c_gate_b, v_c_norm, v_d_conv_w, v_d_conv_b, v_d_norm_g, v_d_norm_b, v_w_br, v_w_out, v_ln_g, v_ln_b):
    W = dict(c_ctx=c_ctx, w_mod=w_mod, b_mod=b_mod, w_in=w_in, q_norm=q_norm, k_norm=k_norm, b_conv=b_conv,
             c_gate_w2=c_gate_w2, c_gate_b=c_gate_b, c_norm=c_norm, d_conv_w=d_conv_w, d_conv_b=d_conv_b,
             d_norm_g=d_norm_g, d_norm_b=d_norm_b, w_br=w_br, w_out=w_out, ln_g=ln_g, ln_b=ln_b)
    M = dict(c_ctx=m_c_ctx, w_mod=m_w_mod, b_mod=m_b_mod, w_in=m_w_in, q_norm=m_q_norm, k_norm=m_k_norm, b_conv=m_b_conv,
             c_gate_w2=m_c_gate_w2, c_gate_b=m_c_gate_b, c_norm=m_c_norm, d_conv_w=m_d_conv_w, d_conv_b=m_d_conv_b,
             d_norm_g=m_d_norm_g, d_norm_b=m_d_norm_b, w_br=m_w_br, w_out=m_w_out, ln_g=m_ln_g, ln_b=m_ln_b)
    V = dict(c_ctx=v_c_ctx, w_mod=v_w_mod, b_mod=v_b_mod, w_in=v_w_in, q_norm=v_q_norm, k_norm=v_k_norm, b_conv=v_b_conv,
             c_gate_w2=v_c_gate_w2, c_gate_b=v_c_gate_b, c_norm=v_c_norm, d_conv_w=v_d_conv_w, d_conv_b=v_d_conv_b,
             d_norm_g=v_d_norm_g, d_norm_b=v_d_norm_b, w_br=v_w_br, w_out=v_w_out, ln_g=v_ln_g, ln_b=v_ln_b)
    chip = 2 * lax.axis_index("x") + lax.axis_index("y")
    cidx = lax.axis_index("c").astype(jnp.int32).reshape(1)

    place = jnp.stack([chip, lax.axis_index("c")]).astype(jnp.int32)

    AXIS = dict(w_in=1, w_mod=0, w_br=0, w_out=0)
    ex = dict(w_in=lambda a: jnp.swapaxes(a, 1, 2), w_mod=lambda a: a.reshape(1, DEPTH * D, -1),
              w_br=lambda a: a.reshape(DEPTH, 4 * BRW, -1), w_out=lambda a: a)
    Wx, Mx, Vx = ({k: ex[k](P_[k]) for k in BIG} for P_ in (W, M, V))

    LAYER, MERGE = ("w_in", "w_br", "w_out"), ("w_br", "w_out")
    small_shard = _pack_small([W[k] for k in SHARD_SMALL])
    keys0 = ("w_in", "w_mod")
    got = _all_gather([Wx[k][0].astype(bf16) for k in keys0] + [small_shard], [AXIS[k] for k in keys0] + [0], "all_gather0")
    smalls = [_unpack_small(got[-1][s], [W[k].shape for k in SHARD_SMALL]) for s in range(N_CHIPS)]
    full = {k: jnp.concatenate([smalls[s][i] for s in range(N_CHIPS)], axis=-1) for i, k in enumerate(SHARD_SMALL)}
    wmod = got[1].reshape(N_CHIPS, DEPTH, D, 3 * D // N_CHIPS)
    ag0b = _split_start([Wx[k][0].astype(bf16) for k in MERGE], True, got[0], "all_gather0b_start")
    ag1 = _split_start([Wx[k][1].astype(bf16) for k in LAYER], True, ag0b[4], "all_gather1_start")

    def merge_form(w_br4, w_out4):
        return jnp.moveaxis(w_br4.reshape(N_CHIPS, 4, BRW, D // N_CHIPS), 0, 2).reshape(4, BRW, D), w_out4.reshape(D, D)

    def weights_of(l, h):
        if l == 0:
            return _group_weights(got[0]), lambda after: merge_form(*_split_wait(*ag0b[:4], True, after, "all_gather0b_wait")[1])
        g3 = _split_wait(*ag1[:4], True, h, "all_gather1_wait")[1]
        return _group_weights(g3[0]), lambda after: merge_form(g3[1], g3[2])

    red = {k: Wx[k].shape for k in BIG}
    flights, held = {}, {}

    def launch(tag, l, pieces):
        keys = list(pieces)
        land_a = _sibling_halves([pieces[k] for k in keys], [AXIS[k] for k in keys], f"rs_sibling_halves{tag}")
        pair = [_add_half(pieces[k], la, cidx, AXIS[k], f"rs_pair_sum{tag}_{k}") for k, la in zip(keys, land_a)]
        flights[tag] = (l, keys, _split_start(pair, False, jnp.zeros((SUB, LANE), f32), f"rs_chip_exchange{tag}_start"))
        return flights[tag][2][4]

    def land(tag, after):
        l, keys, flight = flights.pop(tag)
        pair, land_b = _split_wait(*flight[:4], False, after, f"rs_chip_exchange{tag}_wait")
        for k, lb, pr in zip(keys, land_b, pair):
            red[k] = _sum_chips(lb, pr, place, AXIS[k], l, red[k], f"rs_chip_sum{tag}_{k}")

    def grads_done(l, gl):
        if "wp" in gl:
            pieces = dict(w_in=_ungroup(gl["wp"]).astype(bf16).reshape(N_CHIPS, SHARD, D))
            if l == 0:
                held["0c"] = pieces
                return None
            return launch("1", 1, {**pieces, **held.pop(1)})
        pieces = dict(w_br=gl["w_br"].reshape(N_CHIPS, 4 * BRW, D // N_CHIPS), w_out=gl["w_out"].reshape(N_CHIPS, D // N_CHIPS, D))
        if l == 0:
            return launch("0b", 0, pieces)
        held[1] = pieces
        return None

    loss, gx, g = _local_step(
        x[0], c, ctx[0], loss_target[0], c_ctx, wmod, b_mod, weights_of, q_norm, k_norm, full["b_conv"],
        full["c_gate_w2"], full["c_gate_b"], c_norm, full["d_conv_w"], d_conv_b, d_norm_g, d_norm_b,
        grads_done, ln_g, ln_b, tm=256, token=ag1[4])
    g["c_gate_w2"], g["c_gate_b"] = g.pop("w2"), g.pop("gb")
    loss = lax.psum(loss, ("x", "y", "c"))

    w_mod_pieces = g["w_mod"].reshape(N_CHIPS, DEPTH * D, 3 * D // N_CHIPS)
    g = {k: (jnp.stack(v) if isinstance(v, list) else v) for k, v in g.items() if k not in ("wp", "w_br", "w_out", "w_mod")}

    small_names = REPL_SMALL + SHARD_SMALL
    gs = _all_reduce_small(_pack_small([g[k] for k in small_names]), "all_reduce_small")
    gsm = dict(zip(small_names, _unpack_small(gs, [g[k].shape for k in small_names])))
    for k in SHARD_SMALL:
        wdt = W[k].shape[-1]
        gsm[k] = lax.dynamic_slice_in_dim(gsm[k], chip * wdt, wdt, axis=gsm[k].ndim - 1)

    grad, delta, new_m, new_v = {}, {}, {}, {}

    def adamw_big(keys, after):
        filled = _sibling_fill([red[k] for k in keys], [AXIS[k] for k in keys], "rs_sibling_fill_" + keys[0])
        for k, r in zip(keys, filled):
            back = (lambda a: jnp.swapaxes(a, 1, 2)) if k == "w_in" else (lambda a: a.reshape(W[k].shape))
            d_, m_, v_ = _adamw(Wx[k], r, Mx[k], Vx[k], f"adamw_{k}", after=after)
            grad[k], delta[k], new_m[k], new_v[k] = back(r), back(d_), back(m_), back(v_)
        return d_

    land("1", gx)
    land("0b", gx)
    token = launch("0c", 0, {**held.pop("0c"), "w_mod": w_mod_pieces})
    last = adamw_big(MERGE, token)
    shapes = [W[k].shape for k in small_names]
    d_, m_, v_ = _adamw(*[_pack_small([P_[k] for k in small_names])[None] for P_ in (W, gsm, M, V)], "adamw_small", after=last)
    for k, dd, mm_, vv in zip(small_names, _unpack_small(d_, shapes), _unpack_small(m_, shapes), _unpack_small(v_, shapes)):
        grad[k], delta[k], new_m[k], new_v[k] = gsm[k], dd, mm_, vv
    land("0c", d_)
    adamw_big(("w_in", "w_mod"), None)

    return (loss, gx[None], *[grad[k] for k in ORDER], *[delta[k] for k in ORDER], *[new_m[k] for k in ORDER],
            *[new_v[k] for k in ORDER])
```

```python
import functools

import jax
import jax.numpy as jnp
import numpy as np
from jax import lax
from jax.experimental import pallas as pl
from jax.experimental.pallas import tpu as pltpu

f32 = jnp.float32
bf16 = jnp.bfloat16

D = 1024
DEPTH = 2
GRID_W = 64
BRW = 512
HD = 128
A_HEADS = 4
C_HEADS = 4
C_KW = 256
C_RANK = 16
C_TAU = 16.0
CH = 64
KB = 3
KD = 31
ALPHA = (2 * DEPTH) ** 0.25
EPS = 1e-6
ROPE_THETA = 10000.0
N_IN = 10784
LR, B1, B2, AEPS, WD, STEP = 0.001, 0.9, 0.999, 1e-08, 0.01, 10

W_M, W_A, W_C, W_G = 4 * D + 4 * BRW, 1024, 5 * BRW, 1152
GROUPS = ("M", "A", "C", "G")
GROUP_W = dict(M=W_M, A=W_A, C=W_C, G=W_G)
M_GA, M_GB, M_GC, M_GD = 4 * D, 4 * D + BRW, 4 * D + 2 * BRW, 4 * D + 3 * BRW
A_K, A_V = 512, 768
G_K, G_V, G_R = 256, 512, 1024
CT = 5 * 128
S_Q, S_GA, S_B, S_C, S_X, S_GB, S_CQ, S_CV, S_GC, S_R, S_DA, S_DG, S_GD, S_MG = (
    0, 1024, 1536, 2048, 2560, 3072, 3584, 4096, 4608, 5120, 5152, 5664, 6176, 6688)

LANE = 128
SUB = 8
VMEM_LIMIT = 56 * 1024 * 1024
CONV_PAD = 16
GLA_SUB = 16
GLA_CLAMP = 60.0


def _cparams(sem, vmem=VMEM_LIMIT):
    return pltpu.CompilerParams(dimension_semantics=sem, vmem_limit_bytes=vmem)


def _dg(a, b, ca, cb):
    return lax.dot_general(a.astype(bf16), b.astype(bf16), (((ca,), (cb,)), ((), ())),
                           preferred_element_type=f32)


@jax.custom_vjp
def mm(a, b):
    return _dg(a, b, 1, 0)


mm.defvjp(lambda a, b: (_dg(a, b, 1, 0), (a, b)),
          lambda r, ct: (_dg(ct, r[1], 1, 1).astype(r[0].dtype), _dg(r[0], ct, 0, 0).astype(r[1].dtype)))


@jax.custom_vjp
def mm_nt(a, b):
    return _dg(a, b, 1, 1)


mm_nt.defvjp(lambda a, b: (_dg(a, b, 1, 1), (a, b)),
             lambda r, ct: (_dg(ct, r[1], 1, 0).astype(r[0].dtype), _dg(ct, r[0], 0, 0).astype(r[1].dtype)))


@jax.custom_vjp
def mm_tn(a, b):
    return _dg(a, b, 0, 0)


mm_tn.defvjp(lambda a, b: (_dg(a, b, 0, 0), (a, b)),
             lambda r, ct: (_dg(r[1], ct, 1, 1).astype(r[0].dtype), _dg(r[0], ct, 1, 0).astype(r[1].dtype)))


def _sigmoid(x):
    return 0.5 * jnp.tanh(0.5 * x) + 0.5


def _silu(x):
    return x * _sigmoid(x)


def _ln(x):
    mu = jnp.mean(x, -1, keepdims=True)
    xc = x - mu
    var = jnp.mean(xc * xc, -1, keepdims=True)
    return xc * lax.rsqrt(var + EPS)


def _rms(x, g):
    return x * lax.rsqrt(jnp.mean(x * x, -1, keepdims=True) + EPS) * g


@jax.custom_vjp
def _rope(x, cos_f, sin_a, sin_b):
    return x * cos_f + pltpu.roll(x, HD - 1, 1) * sin_a + pltpu.roll(x, 1, 1) * sin_b


def _rope_fwd(x, cos_f, sin_a, sin_b):
    return _rope(x, cos_f, sin_a, sin_b), (cos_f, sin_a, sin_b)


def _rope_bwd(r, ct):
    cos_f, sin_a, sin_b = r
    dx = ct * cos_f + pltpu.roll(ct * sin_a, 1, 1) + pltpu.roll(ct * sin_b, HD - 1, 1)
    return dx, jnp.zeros_like(cos_f), jnp.zeros_like(sin_a), jnp.zeros_like(sin_b)


_rope.defvjp(_rope_fwd, _rope_bwd)


def _row_ids(i, tm):
    return i * tm + lax.broadcasted_iota(jnp.int32, (tm, 1), 0)


def _partial_rows(ref, rows):
    n = len(rows)
    for k, r in enumerate(rows):
        ref[k:k + 1, :] = r
    ref[n:SUB, :] = jnp.zeros((SUB - n, ref.shape[-1]), f32)


def _matmul(a, b, mode, tm, tn, tk, name, out_dtype=f32, add=None, after=None):
    sect = a.ndim == 3
    a2 = (a.shape[1], a.shape[0] * a.shape[2]) if sect else a.shape
    if mode == "nn":
        (M, K), N = a2, b.shape[1]
        a_spec = pl.BlockSpec((None, tm, tk), lambda j, i, k: (k, i, 0)) if sect else pl.BlockSpec((tm, tk), lambda j, i, k: (i, k))
        b_spec = pl.BlockSpec((tk, tn), lambda j, i, k: (k, j))
        ca, cb = 1, 0
        assert not sect or tk == a.shape[2]
    elif mode == "nt":
        (M, K), N = a2, b.shape[0]
        assert not sect
        a_spec = pl.BlockSpec((tm, tk), lambda j, i, k: (i, k))
        b_spec = pl.BlockSpec((tn, tk), lambda j, i, k: (j, k))
        ca, cb = 1, 1
    else:
        (K, M), N = a2, b.shape[1]
        a_spec = pl.BlockSpec((None, tk, tm), lambda j, i, k: (i, k, 0)) if sect else pl.BlockSpec((tk, tm), lambda j, i, k: (k, i))
        b_spec = pl.BlockSpec((tk, tn), lambda j, i, k: (k, j))
        ca, cb = 0, 0
        assert not sect or tm == a.shape[2]
    assert M % tm == 0 and N % tn == 0 and K % tk == 0, (name, M, N, K, tm, tn, tk)
    nk = K // tk

    o_spec = pl.BlockSpec((tm, tn), lambda j, i, k: (i, j))

    def body(a_ref, b_ref, *rest):
        add_ref = rest[0] if add is not None else None
        o_ref, acc_ref = rest[-2:]
        k = pl.program_id(2)
        part = _dg(a_ref[...], b_ref[...], ca, cb)

        @pl.when(k == 0)
        def _():
            acc_ref[...] = part if add_ref is None else part + add_ref[...]

        @pl.when(k > 0)
        def _():
            acc_ref[...] += part

        @pl.when(k == nk - 1)
        def _():
            o_ref[...] = acc_ref[...].astype(o_ref.dtype)

    extra = ([] if add is None else [(o_spec, add)]) + ([] if after is None else [(pl.BlockSpec(memory_space=pl.ANY), after)])
    return pl.pallas_call(
        body, name=name, grid=(N // tn, M // tm, nk),
        in_specs=[a_spec, b_spec] + [s_ for s_, _ in extra], out_specs=o_spec,
        out_shape=jax.ShapeDtypeStruct((M, N), out_dtype),
        scratch_shapes=[pltpu.VMEM((tm, tn), f32)],
        compiler_params=_cparams(("parallel", "parallel", "arbitrary")),
    )(a, b, *[v_ for _, v_ in extra])


def _matmul_tn_batched(a, b, ns, name):
    B, K, M = a.shape
    N = b.shape[2] // ns

    def body(a_ref, b_ref, o_ref):
        o_ref[...] = _dg(a_ref[...], b_ref[...], 0, 0)

    return pl.pallas_call(
        body, name=name, grid=(B, ns),
        in_specs=[pl.BlockSpec((None, K, M), lambda i, s: (i, 0, 0)), pl.BlockSpec((None, K, N), lambda i, s: (i, 0, s))],
        out_specs=pl.BlockSpec((None, None, M, N), lambda i, s: (s, i, 0, 0)),
        out_shape=jax.ShapeDtypeStruct((ns, B, M, N), f32),
        compiler_params=_cparams(("parallel", "parallel")),
    )(a, b)


MOD_TN = 768


def _mod_fwd(cin, w_mod, b_mod):
    def body(c_ref, w_ref, b_ref, o_ref):
        o_ref[...] = mm(_silu(c_ref[...]), w_ref[...]) + b_ref[...]

    return pl.pallas_call(
        body, name="mod_fwd", grid=(DEPTH, 3 * D // MOD_TN),
        in_specs=[pl.BlockSpec((SUB, D), lambda l, j: (0, 0)),
                  pl.BlockSpec((None, None, D, MOD_TN), lambda l, j: (j, l, 0, 0)),
                  pl.BlockSpec((None, 1, MOD_TN), lambda l, j: (l, 0, j))],
        out_specs=pl.BlockSpec((None, SUB, MOD_TN), lambda l, j: (l, 0, j)),
        out_shape=jax.ShapeDtypeStruct((DEPTH, SUB, 3 * D), f32),
        compiler_params=_cparams(("parallel", "parallel")),
    )(cin, w_mod, b_mod.reshape(DEPTH, 1, 3 * D))


def _mod_bwd(cin, w_mod, dmodv):
    nj = 3 * D // MOD_TN

    def body(c_ref, w_ref, g_ref, dw_ref, dc_ref):
        _, vjp = jax.vjp(lambda c, w: mm(_silu(c), w), c_ref[...], w_ref[...].astype(f32))
        dc, dw = vjp(g_ref[...])
        dw_ref[...] = dw
        dc_ref[...] = dc

    return pl.pallas_call(
        body, name="mod_bwd", grid=(DEPTH, nj),
        in_specs=[pl.BlockSpec((SUB, D), lambda l, j: (0, 0)),
                  pl.BlockSpec((None, None, D, MOD_TN), lambda l, j: (j, l, 0, 0)),
                  pl.BlockSpec((None, SUB, MOD_TN), lambda l, j: (l, 0, j))],
        out_specs=[pl.BlockSpec((None, None, D, MOD_TN), lambda l, j: (j, l, 0, 0)),
                   pl.BlockSpec((None, None, SUB, D), lambda l, j: (l, j, 0, 0))],
        out_shape=[jax.ShapeDtypeStruct((nj, DEPTH, D, MOD_TN), f32),
                   jax.ShapeDtypeStruct((DEPTH, nj, SUB, D), f32)],
        compiler_params=_cparams(("parallel", "parallel")),
    )(cin, w_mod, dmodv)


def _u_fn(h, m_l, m_c, isctx):
    n = _ln(h)
    shift = jnp.where(isctx, m_c[:, 0:D], m_l[:, 0:D])
    scale = jnp.where(isctx, m_c[:, D:2 * D], m_l[:, D:2 * D])
    return n * (1.0 + scale) + shift


def _ln_fwd(h, modv_l, tc, tm, name):
    T = h.shape[0]

    def body(h_ref, m_ref, u_ref):
        isctx = _row_ids(pl.program_id(0), tm) < tc
        u_ref[...] = _u_fn(h_ref[...], m_ref[0:1, :], m_ref[1:2, :], isctx).astype(bf16)

    return pl.pallas_call(
        body, name=name, grid=(T // tm,),
        in_specs=[pl.BlockSpec((tm, D), lambda i: (i, 0)), pl.BlockSpec((SUB, 3 * D), lambda i: (0, 0))],
        out_specs=pl.BlockSpec((tm, D), lambda i: (i, 0)),
        out_shape=jax.ShapeDtypeStruct((T, D), bf16),
        compiler_params=_cparams(("parallel",)),
    )(h, modv_l)


def _ln_bwd(du, h, dh_res, modv_l, tc, tm, name):
    T = h.shape[0]
    nt = T // tm

    def body(du_ref, h_ref, r_ref, m_ref, dh_ref, dm_ref):
        isctx = _row_ids(pl.program_id(0), tm) < tc
        _, vjp = jax.vjp(lambda h, ml, mc: _u_fn(h, ml, mc, isctx), h_ref[...], m_ref[0:1, :], m_ref[1:2, :])
        dh, dml, dmc = vjp(du_ref[...])
        dh_ref[...] = dh + r_ref[...]
        _partial_rows(dm_ref, [dml, dmc])

    return pl.pallas_call(
        body, name=name, grid=(nt,),
        in_specs=[pl.BlockSpec((tm, D), lambda i: (i, 0)), pl.BlockSpec((tm, D), lambda i: (i, 0)),
                  pl.BlockSpec((tm, D), lambda i: (i, 0)), pl.BlockSpec((SUB, 3 * D), lambda i: (0, 0))],
        out_specs=[pl.BlockSpec((tm, D), lambda i: (i, 0)), pl.BlockSpec((None, SUB, 3 * D), lambda i: (i, 0, 0))],
        out_shape=[jax.ShapeDtypeStruct((T, D), f32), jax.ShapeDtypeStruct((nt, SUB, 3 * D), f32)],
        compiler_params=_cparams(("parallel",)),
    )(du, h, dh_res, modv_l)


def _prep_fn(q, k, qg, kg, cos_f, sin_a, sin_b):
    qs = [_rope(_rms(q[:, HD * i:HD * (i + 1)], qg), cos_f, sin_a, sin_b) for i in range(A_HEADS)]
    ks = [_rope(_rms(k[:, HD * i:HD * (i + 1)], kg), cos_f, sin_a, sin_b) for i in range(A_HEADS // 2)]
    return jnp.concatenate(qs, 1), jnp.concatenate(ks, 1)


def _tok(tm, w, off):
    return pl.BlockSpec((tm, w), lambda i: (i, off // w))


def _vec(w):
    return pl.BlockSpec((1, w), lambda i: (0, 0))


def _prep_fwd(P, qg, kg, rope, tm, name):
    T = P.shape[0]

    def body(q_ref, k_ref, v_ref, qg_ref, kg_ref, c_ref, sa_ref, sb_ref, qn_ref, kn_ref, vb_ref):
        qn, kn = _prep_fn(q_ref[...], k_ref[...], qg_ref[...], kg_ref[...], c_ref[...], sa_ref[...], sb_ref[...])
        qn_ref[...] = qn.astype(bf16)
        kn_ref[...] = kn.astype(bf16)
        vb_ref[...] = v_ref[...].astype(bf16)

    return pl.pallas_call(
        body, name=name, grid=(T // tm,),
        in_specs=[_tok(tm, 512, 0), _tok(tm, 256, A_K), _tok(tm, 256, A_V), _vec(HD), _vec(HD),
                  _tok(tm, HD, 0), _tok(tm, HD, 0), _tok(tm, HD, 0)],
        out_specs=[_tok(tm, 512, 0), _tok(tm, 256, 0), _tok(tm, 256, 0)],
        out_shape=[jax.ShapeDtypeStruct((T, 512), bf16), jax.ShapeDtypeStruct((T, 256), bf16),
                   jax.ShapeDtypeStruct((T, 256), bf16)],
        compiler_params=_cparams(("parallel",)),
    )(P, P, P, qg, kg, *rope)


def _prep_bwd(P, dqn, dkn, dv, qg, kg, rope, tm, name):
    T = P.shape[0]
    nt = T // tm

    def body(q_ref, k_ref, dq_ref, dk_ref, dv_ref, qg_ref, kg_ref, c_ref, sa_ref, sb_ref, o_ref, og_ref):
        tabs = (c_ref[...], sa_ref[...], sb_ref[...])
        _, vjp = jax.vjp(lambda q, k, a, b: _prep_fn(q, k, a, b, *tabs), q_ref[...], k_ref[...], qg_ref[...], kg_ref[...])
        dq, dk, dqg, dkg = vjp((dq_ref[...], dk_ref[...]))
        o_ref[:, 0:A_K] = dq.astype(bf16)
        o_ref[:, A_K:A_V] = dk.astype(bf16)
        o_ref[:, A_V:W_A] = dv_ref[...].astype(bf16)
        _partial_rows(og_ref, [dqg, dkg])

    return pl.pallas_call(
        body, name=name, grid=(nt,),
        in_specs=[_tok(tm, 512, 0), _tok(tm, 256, A_K), _tok(tm, 512, 0), _tok(tm, 256, 0), _tok(tm, 256, 0),
                  _vec(HD), _vec(HD), _tok(tm, HD, 0), _tok(tm, HD, 0), _tok(tm, HD, 0)],
        out_specs=[_tok(tm, W_A, 0), pl.BlockSpec((None, SUB, HD), lambda i: (i, 0, 0))],
        out_shape=[jax.ShapeDtypeStruct((T, W_A), bf16), jax.ShapeDtypeStruct((nt, SUB, HD), f32)],
        compiler_params=_cparams(("parallel",)),
    )(P, P, dqn, dkn, dv, qg, kg, *rope)


def _attn_fn(q, k, v, lim):
    s = mm_nt(q, k) * (HD ** -0.5)
    col = lax.broadcasted_iota(jnp.int32, s.shape, 1)
    s = jnp.where(col < lim, s, -1e30)
    m = lax.stop_gradient(jnp.max(s, -1, keepdims=True))
    e = jnp.exp(s - m)
    p = e * (1.0 / jnp.sum(e, -1, keepdims=True))
    return mm(p, v)


def _attn_fwd(qn, kn, vb, tc, tq, name):
    T = qn.shape[0]

    def body(q_ref, k_ref, v_ref, o_ref):
        lim = jnp.where(pl.program_id(1) * tq < tc, tc, T)
        o_ref[...] = _attn_fn(q_ref[...], k_ref[...], v_ref[...], lim)

    return pl.pallas_call(
        body, name=name, grid=(A_HEADS, T // tq),
        in_specs=[pl.BlockSpec((tq, HD), lambda h, i: (i, h)), pl.BlockSpec((T, HD), lambda h, i: (0, h // 2)),
                  pl.BlockSpec((T, HD), lambda h, i: (0, h // 2))],
        out_specs=pl.BlockSpec((tq, HD), lambda h, i: (i, h)),
        out_shape=jax.ShapeDtypeStruct((T, 512), f32),
        compiler_params=_cparams(("parallel", "parallel")),
    )(qn, kn, vb)


def _attn_bwd(qn, kn, vb, dya, tc, tq, name):
    T = qn.shape[0]

    def body(q_ref, k_ref, v_ref, g_ref, dq_ref, dk_ref, dv_ref):
        first = (pl.program_id(1) == 0) & (pl.program_id(2) == 0)
        lim = jnp.where(pl.program_id(2) * tq < tc, tc, T)
        _, vjp = jax.vjp(lambda q, k, v: _attn_fn(q, k, v, lim), q_ref[...].astype(f32), k_ref[...].astype(f32),
                         v_ref[...].astype(f32))
        dq, dk, dv = vjp(g_ref[...])
        dq_ref[...] = dq

        @pl.when(first)
        def _():
            dk_ref[...] = dk
            dv_ref[...] = dv

        @pl.when(jnp.logical_not(first))
        def _():
            dk_ref[...] += dk
            dv_ref[...] += dv

    qspec = pl.BlockSpec((tq, HD), lambda kv, g, i: (i, 2 * kv + g))
    kspec = pl.BlockSpec((T, HD), lambda kv, g, i: (0, kv))
    return pl.pallas_call(
        body, name=name, grid=(A_HEADS // 2, 2, T // tq),
        in_specs=[qspec, kspec, kspec, qspec], out_specs=[qspec, kspec, kspec],
        out_shape=[jax.ShapeDtypeStruct((T, 512), f32), jax.ShapeDtypeStruct((T, 256), f32),
                   jax.ShapeDtypeStruct((T, 256), f32)],
        compiler_params=_cparams(("parallel", "arbitrary", "arbitrary")),
    )(qn, kn, vb, dya)


def _conv_rows(tc, tl):
    return CONV_PAD + tc + CONV_PAD + tl + CONV_PAD


def _fill_pad(pad_ref, val, tc, tl):
    z = jnp.zeros((CONV_PAD, LANE), f32)
    pad_ref[0:CONV_PAD, :] = z
    pad_ref[CONV_PAD:CONV_PAD + tc, :] = val[0:tc]
    pad_ref[CONV_PAD + tc:2 * CONV_PAD + tc, :] = z
    pad_ref[2 * CONV_PAD + tc:2 * CONV_PAD + tc + tl, :] = val[tc:tc + tl]
    pad_ref[2 * CONV_PAD + tc + tl:3 * CONV_PAD + tc + tl, :] = z


def _conv_apply(pad_ref, w_ref, K, tc, tl, rc, emit, flip=False):
    half = K // 2
    for seg0, off, n in ((0, CONV_PAD, tc), (tc, 2 * CONV_PAD + tc, tl)):
        for r0 in range(0, n, rc):
            acc = None
            for k in range(K):
                sh = (half - k) if flip else (k - half)
                term = pad_ref[pl.ds(off + r0 + sh, rc), :] * w_ref[k:k + 1, :]
                acc = term if acc is None else acc + term
            emit(seg0 + r0, acc)


def _conv_wgrad(pad_ref, dy_ref, K, tc, tl, rc, dw_ref):
    half = K // 2
    for k in range(K):
        acc = jnp.zeros((1, LANE), f32)
        for seg0, off, n in ((0, CONV_PAD, tc), (tc, 2 * CONV_PAD + tc, tl)):
            for r0 in range(0, n, rc):
                acc = acc + jnp.sum(pad_ref[pl.ds(off + r0 + k - half, rc), :] * dy_ref[pl.ds(seg0 + r0, rc), :],
                                    axis=0, keepdims=True)
        dw_ref[k:k + 1, :] = acc


def _col(T, off):
    return pl.BlockSpec((T, LANE), lambda j: (0, off // LANE + j))


C_B, C_C, C_X, C_A, C_G = range(5)
N_SEC = 5


class _Sections:
    def __init__(self, refs):
        self.refs = refs

    def __getitem__(self, idx):
        rows, sec = idx
        return self.refs[sec][rows, :]

    def __setitem__(self, idx, val):
        rows, sec = idx
        self.refs[sec, rows, :] = val


def _sec_specs(T):
    return [pl.BlockSpec((T, LANE), functools.partial(lambda j, s: (0, s * (BRW // LANE) + j), s=s)) for s in range(N_SEC)]


def _conv_fwd(P, wb, wd, bd, tc, tl, rc, name):
    T = tc + tl

    def body(*refs):
        p_ref = _Sections(refs[:N_SEC])
        wb_ref, wd_ref, bd_ref, yb_ref, hh_ref, pad_ref = refs[N_SEC:]
        _fill_pad(pad_ref, p_ref[:, C_C] * p_ref[:, C_X], tc, tl)

        def emit_b(r0, y):
            yb_ref[pl.ds(r0, rc), :] = y * p_ref[pl.ds(r0, rc), C_B]

        _conv_apply(pad_ref, wb_ref, KB, tc, tl, rc, emit_b)
        _fill_pad(pad_ref, p_ref[:, C_A] * _sigmoid(p_ref[:, C_G]), tc, tl)

        def emit_d(r0, y):
            hh_ref[pl.ds(r0, rc), :] = y + bd_ref[...]

        _conv_apply(pad_ref, wd_ref, KD, tc, tl, rc, emit_d)

    return pl.pallas_call(
        body, name=name, grid=(BRW // LANE,),
        in_specs=_sec_specs(T) + [pl.BlockSpec((KB, LANE), lambda j: (0, j)), pl.BlockSpec((KD, LANE), lambda j: (0, j)),
                                  pl.BlockSpec((1, LANE), lambda j: (0, j))],
        out_specs=[_col(T, 0), _col(T, 0)],
        out_shape=[jax.ShapeDtypeStruct((T, BRW), f32), jax.ShapeDtypeStruct((T, BRW), f32)],
        scratch_shapes=[pltpu.VMEM((_conv_rows(tc, tl), LANE), f32)],
        compiler_params=_cparams(("parallel",)),
    )(*[P] * N_SEC, wb, wd, bd)


def _conv_bwd(P, dyb, dhh, wb, wd, tc, tl, rc, name):
    T = tc + tl

    def body(*refs):
        p_ref = _Sections(refs[:N_SEC])
        dyb_ref, dhh_ref, wb_ref, wd_ref, dp3_ref, dwb_ref, dwd_ref, dbd_ref, pad_ref, pad2_ref, tmp_ref = refs[N_SEC:]
        dp_ref = _Sections(dp3_ref)
        _fill_pad(pad_ref, p_ref[:, C_C] * p_ref[:, C_X], tc, tl)

        def emit_cv(r0, y):
            dp_ref[pl.ds(r0, rc), C_B] = (y * dyb_ref[pl.ds(r0, rc), :]).astype(bf16)

        _conv_apply(pad_ref, wb_ref, KB, tc, tl, rc, emit_cv)
        tmp_ref[...] = dyb_ref[...] * p_ref[:, C_B]
        _conv_wgrad(pad_ref, tmp_ref, KB, tc, tl, rc, dwb_ref)
        _fill_pad(pad2_ref, tmp_ref[...], tc, tl)

        def emit_ds(r0, y):
            dp_ref[pl.ds(r0, rc), C_C] = (y * p_ref[pl.ds(r0, rc), C_X]).astype(bf16)
            dp_ref[pl.ds(r0, rc), C_X] = (y * p_ref[pl.ds(r0, rc), C_C]).astype(bf16)

        _conv_apply(pad2_ref, wb_ref, KB, tc, tl, rc, emit_ds, flip=True)
        _fill_pad(pad_ref, p_ref[:, C_A] * _sigmoid(p_ref[:, C_G]), tc, tl)
        _conv_wgrad(pad_ref, dhh_ref, KD, tc, tl, rc, dwd_ref)
        dbd_ref[...] = jnp.sum(dhh_ref[...], axis=0, keepdims=True)
        _fill_pad(pad2_ref, dhh_ref[...], tc, tl)

        def emit_d2(r0, y):
            sg = _sigmoid(p_ref[pl.ds(r0, rc), C_G])
            a = p_ref[pl.ds(r0, rc), C_A]
            dp_ref[pl.ds(r0, rc), C_A] = (y * sg).astype(bf16)
            dp_ref[pl.ds(r0, rc), C_G] = (y * a * sg * (1.0 - sg)).astype(bf16)

        _conv_apply(pad2_ref, wd_ref, KD, tc, tl, rc, emit_d2, flip=True)

    return pl.pallas_call(
        body, name=name, grid=(BRW // LANE,),
        in_specs=_sec_specs(T) + [_col(T, 0), _col(T, 0),
                                  pl.BlockSpec((KB, LANE), lambda j: (0, j)), pl.BlockSpec((KD, LANE), lambda j: (0, j))],
        out_specs=[pl.BlockSpec((N_SEC, T, LANE), lambda j: (0, 0, j)), pl.BlockSpec((KB, LANE), lambda j: (0, j)),
                   pl.BlockSpec((KD, LANE), lambda j: (0, j)), pl.BlockSpec((1, LANE), lambda j: (0, j))],
        out_shape=[jax.ShapeDtypeStruct((N_SEC, T, BRW), bf16), jax.ShapeDtypeStruct((KB, BRW), f32),
                   jax.ShapeDtypeStruct((KD, BRW), f32), jax.ShapeDtypeStruct((1, BRW), f32)],
        scratch_shapes=[pltpu.VMEM((_conv_rows(tc, tl), LANE), f32), pltpu.VMEM((_conv_rows(tc, tl), LANE), f32),
                        pltpu.VMEM((T, LANE), f32)],
        compiler_params=_cparams(("parallel",)),
    )(*[P] * N_SEC, dyb, dhh, wb, wd)


def _gla_chunk(q, k, v, r, w2, b2, st, isfwd):
    z = mm(r, w2) + b2
    g = jax.nn.log_sigmoid(z[:, 0:C_KW] if isfwd else z[:, C_KW:2 * C_KW]) / C_TAU
    ri = lax.broadcasted_iota(jnp.int32, (CH, CH), 0)
    ci = lax.broadcasted_iota(jnp.int32, (CH, CH), 1)
    tri = ((ci <= ri) if isfwd else (ci >= ri)).astype(f32)
    cum = jnp.dot(tri, g, preferred_element_type=f32, precision=lax.Precision.HIGHEST)
    last = jnp.sum(g, axis=0, keepdims=True)
    q = q * (C_KW // C_HEADS) ** -0.5
    hv = lax.broadcasted_iota(jnp.int32, (BRW, C_KW), 0) // (BRW // C_HEADS)
    hk = lax.broadcasted_iota(jnp.int32, (BRW, C_KW), 1) // (C_KW // C_HEADS)
    st_new = st * jnp.exp(last) + jnp.where(hv == hk, mm_tn(v, k * jnp.exp(last - cum)), 0.0)
    o = mm_nt(q * jnp.exp(cum), st)
    rowi = lax.broadcasted_iota(jnp.int32, (CH, C_KW), 0)
    srow = lax.broadcasted_iota(jnp.int32, (C_HEADS * CH, C_KW), 0)
    slane = lax.broadcasted_iota(jnp.int32, (C_HEADS * CH, C_KW), 1)
    own_lanes = srow // CH == slane // (C_KW // C_HEADS)
    pos = lax.broadcasted_iota(jnp.int32, (C_HEADS * CH, CH), 0) % CH
    key = lax.broadcasted_iota(jnp.int32, (C_HEADS * CH, CH), 1)
    scores = jnp.zeros((C_HEADS * CH, CH), f32)
    for a in range(CH // GLA_SUB):
        idx = GLA_SUB * a - 1 if isfwd else GLA_SUB * (a + 1)
        ref = jnp.sum(jnp.where(rowi == idx, cum, 0.0), axis=0, keepdims=True)
        qa = q * jnp.exp(jnp.minimum(cum - ref, 0.0))
        ka = k * jnp.exp(jnp.minimum(ref - cum, GLA_CLAMP))
        s = mm_nt(jnp.where(own_lanes, jnp.concatenate([qa] * C_HEADS, axis=0), 0.0), ka)
        scores = scores + jnp.where(pos // GLA_SUB == a, s, 0.0)
    scores = jnp.where((key <= pos) if isfwd else (key >= pos), scores, 0.0)
    vw = BRW // C_HEADS
    o = o + jnp.concatenate([mm(scores[CH * hd:CH * (hd + 1)], v[:, vw * hd:vw * (hd + 1)]) for hd in range(C_HEADS)],
                            axis=1)
    return o, st_new


def _gla_chunk_of(d, n, nc, nch):
    back = jnp.where(n < nc, nc - 1 - n, nch - 1 - (n - nc))
    return jnp.where(d == 0, n, back)


def _gla_fwd(P, w2, b2, tc, name):
    T = P.shape[0]
    nch, nc = T // CH, tc // CH

    back = lambda n: _gla_chunk_of(1, n, nc, nch)

    def body(pf_ref, pb_ref, w_ref, b_ref, of_ref, ob_ref, ssf_ref, ssb_ref, stf_ref, stb_ref):
        @pl.when(pl.program_id(0) == 0)
        def _():
            stf_ref[...] = jnp.zeros_like(stf_ref)
            stb_ref[...] = jnp.zeros_like(stb_ref)

        for p_ref, o_ref, ss_ref, st_ref, isfwd in ((pf_ref, of_ref, ssf_ref, stf_ref, True),
                                                    (pb_ref, ob_ref, ssb_ref, stb_ref, False)):
            st = st_ref[...]
            ss_ref[...] = st
            o, st_new = _gla_chunk(p_ref[:, 0:G_K], p_ref[:, G_K:G_V], p_ref[:, G_V:G_R], p_ref[:, G_R:W_G], w_ref[...],
                                   b_ref[...], st, isfwd)
            o_ref[...] = o
            st_ref[...] = st_new

    sd = jax.ShapeDtypeStruct
    return pl.pallas_call(
        body, name=name, grid=(nch,),
        in_specs=[pl.BlockSpec((CH, W_G), lambda n: (n, 0)), pl.BlockSpec((CH, W_G), lambda n: (back(n), 0)),
                  pl.BlockSpec((LANE, 512), lambda n: (0, 0)), pl.BlockSpec((1, 512), lambda n: (0, 0))],
        out_specs=[pl.BlockSpec((CH, BRW), lambda n: (n, 0)), pl.BlockSpec((CH, BRW), lambda n: (back(n), 0)),
                   pl.BlockSpec((None, BRW, C_KW), lambda n: (n, 0, 0)), pl.BlockSpec((None, BRW, C_KW), lambda n: (n, 0, 0))],
        out_shape=[sd((T, BRW), f32), sd((T, BRW), f32), sd((nch, BRW, C_KW), f32), sd((nch, BRW, C_KW), f32)],
        scratch_shapes=[pltpu.VMEM((BRW, C_KW), f32), pltpu.VMEM((BRW, C_KW), f32)],
        compiler_params=_cparams(("arbitrary",)),
    )(P, P, w2, b2)


def _gla_bwd(P, w2, b2, ssave, doc, tc, name):
    T = P.shape[0]
    nch, nc = T // CH, tc // CH

    fwd_chunk = lambda m: nch - 1 - m
    back_chunk = lambda m: _gla_chunk_of(1, nch - 1 - m, nc, nch)

    def body(pf_ref, pb_ref, w_ref, b_ref, ssf_ref, ssb_ref, gf_ref, gb_ref, dpf_ref, dpb_ref, dw_ref, db_ref,
             dstf_ref, dstb_ref):
        m = pl.program_id(0)

        @pl.when(m == 0)
        def _():
            dstf_ref[...] = jnp.zeros_like(dstf_ref)
            dstb_ref[...] = jnp.zeros_like(dstb_ref)

        dw_sum, db_sum = None, None
        for p_ref, ss_ref, g_ref, dp_ref, dst_ref, isfwd in ((pf_ref, ssf_ref, gf_ref, dpf_ref, dstf_ref, True),
                                                             (pb_ref, ssb_ref, gb_ref, dpb_ref, dstb_ref, False)):
            _, vjp = jax.vjp(lambda q, k, v, r, w, b, st: _gla_chunk(q, k, v, r, w, b, st, isfwd),
                             p_ref[:, 0:G_K], p_ref[:, G_K:G_V], p_ref[:, G_V:G_R], p_ref[:, G_R:W_G], w_ref[...],
                             b_ref[...], ss_ref[...])
            dq, dk, dv, dr, dw, db, dst = vjp((g_ref[...], dst_ref[...]))
            dp_ref[:, 0:G_K] = dq
            dp_ref[:, G_K:G_V] = dk
            dp_ref[:, G_V:G_R] = dv
            dp_ref[:, G_R:W_G] = dr
            dst_ref[...] = dst
            dw_sum = dw if dw_sum is None else dw_sum + dw
            db_sum = db if db_sum is None else db_sum + db

        @pl.when(m == 0)
        def _():
            dw_ref[...] = dw_sum
            _partial_rows(db_ref, [db_sum])

        @pl.when(m > 0)
        def _():
            dw_ref[...] += dw_sum
            db_ref[0:1, :] += db_sum

    ssf, ssb = ssave
    chunk_f = lambda w: pl.BlockSpec((CH, w), lambda m: (fwd_chunk(m), 0))
    chunk_b = lambda w: pl.BlockSpec((CH, w), lambda m: (back_chunk(m), 0))
    state = pl.BlockSpec((None, BRW, C_KW), lambda m: (nch - 1 - m, 0, 0))
    sd = jax.ShapeDtypeStruct
    return pl.pallas_call(
        body, name=name, grid=(nch,),
        in_specs=[chunk_f(W_G), chunk_b(W_G), pl.BlockSpec((LANE, 512), lambda m: (0, 0)), pl.BlockSpec((1, 512), lambda m: (0, 0)),
                  state, state, chunk_f(BRW), chunk_b(BRW)],
        out_specs=[chunk_f(W_G), chunk_b(W_G), pl.BlockSpec((LANE, 512), lambda m: (0, 0)), pl.BlockSpec((SUB, 512), lambda m: (0, 0))],
        out_shape=[sd((T, W_G), f32), sd((T, W_G), f32), sd((LANE, 512), f32), sd((SUB, 512), f32)],
        scratch_shapes=[pltpu.VMEM((BRW, C_KW), f32), pltpu.VMEM((BRW, C_KW), f32)],
        compiler_params=_cparams(("arbitrary",)),
    )(P, P, w2, b2, ssf, ssb, doc, doc)


def _sum_dirs(a, b, tm, name):
    T, W = a.shape

    def body(a_ref, b_ref, o_ref):
        o_ref[...] = (a_ref[...] + b_ref[...]).astype(bf16)

    spec = pl.BlockSpec((tm, W), lambda i: (i, 0))
    return pl.pallas_call(
        body, name=name, grid=(T // tm,), in_specs=[spec, spec], out_specs=spec,
        out_shape=jax.ShapeDtypeStruct((T, W), bf16),
        compiler_params=_cparams(("parallel",)),
    )(a, b)


def _merge_fn(h, m_l, m_c, isctx, ya, ga, yb, gb, of, ob, gc, hh, gd, mg, es, ey, cn, dng, dnb, lg, lb, wbr, wout):
    oc = of + ob
    yc = jnp.concatenate([_rms(oc[:, HD * i:HD * (i + 1)], cn[:, HD * i:HD * (i + 1)]) for i in range(C_HEADS)], 1)
    brs = [ya * _silu(ga), yb * _silu(gb), yc * _silu(gc), _silu(_ln(hh) * dng + dnb) * _silu(gd)]
    acc = None
    for i in range(4):
        t = _sigmoid(mg[:, D * i:D * (i + 1)]) * (mm(brs[i], wbr[i]) + es[i])
        acc = t if acc is None else acc + t
    y = mm(acc, wout) + ey
    gate = jnp.where(isctx, m_c[:, 2 * D:3 * D], m_l[:, 2 * D:3 * D])
    hn = _ln(ALPHA * h + gate * y) * lg + lb
    return hn, (brs, acc)


def _merge_specs(tm):
    t = lambda w, off=0: _tok(tm, w, off)
    return [t(D), pl.BlockSpec((SUB, 3 * D), lambda i: (0, 0)),
            t(BRW), t(BRW, M_GA), t(BRW), t(BRW, M_GB),
            t(BRW), t(BRW),
            t(BRW, M_GC), t(BRW), t(BRW, M_GD), t(4 * D, 0),
            _vec(BRW), _vec(BRW), _vec(BRW), _vec(D), _vec(D),
            pl.BlockSpec((4, BRW, D), lambda i: (0, 0, 0)), pl.BlockSpec((D, D), lambda i: (0, 0))]


def _merge_fwd(h, modv_l, ya, yb, o2, hh, P, cn, dng, dnb, lg, lb, wbr, wout, tc, tm, name):
    T = h.shape[0]

    def body(h_ref, m_ref, ya_ref, ga_ref, yb_ref, gb_ref, of_ref, ob_ref, gc_ref, hh_ref, gd_ref, mg_ref,
             cn_ref, dng_ref, dnb_ref, lg_ref, lb_ref, wbr_ref, wout_ref, o_ref):
        isctx = _row_ids(pl.program_id(0), tm) < tc
        zero = jnp.zeros((tm, D), f32)
        hn, _ = _merge_fn(h_ref[...], m_ref[0:1, :], m_ref[1:2, :], isctx, ya_ref[...], ga_ref[...], yb_ref[...],
                          gb_ref[...], of_ref[...], ob_ref[...], gc_ref[...], hh_ref[...], gd_ref[...], mg_ref[...],
                          [zero] * 4, zero, cn_ref[...], dng_ref[...], dnb_ref[...], lg_ref[...], lb_ref[...],
                          [wbr_ref[i] for i in range(4)], wout_ref[...])
        o_ref[...] = hn

    return pl.pallas_call(
        body, name=name, grid=(T // tm,),
        in_specs=_merge_specs(tm), out_specs=_tok(tm, D, 0),
        out_shape=jax.ShapeDtypeStruct((T, D), f32),
        compiler_params=_cparams(("parallel",)),
    )(h, modv_l, ya, P, yb, P, o2[0], o2[1], P, hh, P, P, cn, dng, dnb, lg, lb, wbr, wout)


def _merge_bwd(dhn, h, modv_l, ya, yb, o2, hh, P, cn, dng, dnb, lg, lb, wbr, wout, tc, tm, name):
    T = h.shape[0]
    nt = T // tm

    def body(g_ref, h_ref, m_ref, ya_ref, ga_ref, yb_ref, gb_ref, of_ref, ob_ref, gc_ref, hh_ref, gd_ref, mg_ref,
             cn_ref, dng_ref, dnb_ref, lg_ref, lb_ref, wbr_ref, wout_ref,
             dh_ref, dm_ref, dya_ref, dyb_ref, doc_ref, dhh_ref, dp_ref,
             br_ref, z_ref, acc_ref, dy_ref, dv5_ref, dvd_ref):
        isctx = _row_ids(pl.program_id(0), tm) < tc
        zero = jnp.zeros((tm, D), f32)
        wbr_v = [wbr_ref[i] for i in range(4)]
        wout_v = wout_ref[...]

        def fn(h, ml, mc, ya, ga, yb, gb, oc, gc, hh, gd, mg, e0, e1, e2, e3, ey, cn, dng, dnb, lg, lb):
            return _merge_fn(h, ml, mc, isctx, ya, ga, yb, gb, oc, jnp.zeros_like(oc), gc, hh, gd, mg,
                             [e0, e1, e2, e3], ey, cn, dng, dnb, lg, lb, wbr_v, wout_v)

        _, vjp, (brs, acc) = jax.vjp(
            fn, h_ref[...], m_ref[0:1, :], m_ref[1:2, :], ya_ref[...], ga_ref[...], yb_ref[...], gb_ref[...],
            of_ref[...] + ob_ref[...], gc_ref[...], hh_ref[...], gd_ref[...], mg_ref[...], zero, zero, zero, zero, zero,
            cn_ref[...], dng_ref[...], dnb_ref[...], lg_ref[...], lb_ref[...], has_aux=True)
        (dh, dml, dmc, dya, dga, dyb, dgb, doc, dgc, dhh, dgd, dmg, z0, z1, z2, z3, dy,
         dcn, ddng, ddnb, dlg, dlb) = vjp(g_ref[...])
        dh_ref[...] = dh
        _partial_rows(dm_ref, [dml, dmc])
        dya_ref[...] = dya
        dyb_ref[...] = dyb
        doc_ref[...] = doc
        dhh_ref[...] = dhh
        dp_ref[:, 0:M_GA] = dmg.astype(bf16)
        dp_ref[:, M_GA:M_GB] = dga.astype(bf16)
        dp_ref[:, M_GB:M_GC] = dgb.astype(bf16)
        dp_ref[:, M_GC:M_GD] = dgc.astype(bf16)
        dp_ref[:, M_GD:W_M] = dgd.astype(bf16)
        for i, z in enumerate((z0, z1, z2, z3)):
            br_ref[i] = brs[i].astype(bf16)
            z_ref[i] = z.astype(bf16)
        acc_ref[...] = acc.astype(bf16)
        dy_ref[...] = dy.astype(bf16)
        _partial_rows(dv5_ref, [dcn, ddng, ddnb])
        _partial_rows(dvd_ref, [dlg, dlb])

    t = lambda w: _tok(tm, w, 0)
    part = lambda w: pl.BlockSpec((None, SUB, w), lambda i: (i, 0, 0))
    sd = jax.ShapeDtypeStruct
    return pl.pallas_call(
        body, name=name, grid=(nt,),
        in_specs=[t(D)] + _merge_specs(tm),
        out_specs=[t(D), part(3 * D)] + [t(BRW)] * 4 + [t(W_M),
                   pl.BlockSpec((4, tm, BRW), lambda i: (0, i, 0)), pl.BlockSpec((4, tm, D), lambda i: (0, i, 0)),
                   t(D), t(D), part(BRW), part(D)],
        out_shape=[sd((T, D), f32), sd((nt, SUB, 3 * D), f32)] + [sd((T, BRW), f32)] * 4 + [sd((T, W_M), bf16),
                   sd((4, T, BRW), bf16), sd((4, T, D), bf16), sd((T, D), bf16), sd((T, D), bf16),
                   sd((nt, SUB, BRW), f32), sd((nt, SUB, D), f32)],
        compiler_params=_cparams(("parallel",)),
    )(dhn, h, modv_l, ya, P, yb, P, o2[0], o2[1], P, hh, P, P, cn, dng, dnb, lg, lb, wbr, wout)


def _loss_kernel(h, tgt, tc, tm, name):
    T = h.shape[0]
    nt = T // tm
    nct = tc // tm

    def body(h_ref, t_ref, d_ref, l_ref):
        i = pl.program_id(0)
        err = h_ref[...] - t_ref[...]
        lat = (i >= nct).astype(f32)
        d_ref[...] = err * (lat / D)
        l_ref[...] = jnp.zeros((SUB, LANE), f32) + lat * 0.5 * jnp.sum(err * err) / D

    return pl.pallas_call(
        body, name=name, grid=(nt,),
        in_specs=[pl.BlockSpec((tm, D), lambda i: (i, 0)),
                  pl.BlockSpec((tm, D), lambda i: (jnp.maximum(i - nct, 0), 0))],
        out_specs=[pl.BlockSpec((tm, D), lambda i: (i, 0)), pl.BlockSpec((None, SUB, LANE), lambda i: (i, 0, 0))],
        out_shape=[jax.ShapeDtypeStruct((T, D), f32), jax.ShapeDtypeStruct((nt, SUB, LANE), f32)],
        compiler_params=_cparams(("parallel",)),
    )(h, tgt)


def _rope_tables(tc, tl):
    t = jnp.arange(tl)
    inv = ROPE_THETA ** (-jnp.arange(0, HD // 2, 2, dtype=f32) / (HD // 2))
    ang = jnp.concatenate([(t // GRID_W).astype(f32)[:, None] * inv, (t % GRID_W).astype(f32)[:, None] * inv], -1)
    cos, sin = jnp.repeat(jnp.cos(ang), 2, axis=1), jnp.repeat(jnp.sin(ang), 2, axis=1)
    even = (jnp.arange(HD) % 2 == 0)[None, :]
    cos_f = jnp.concatenate([jnp.ones((tc, HD), f32), cos], 0)
    sin_a = jnp.concatenate([jnp.zeros((tc, HD), f32), jnp.where(even, -sin, 0.0)], 0)
    sin_b = jnp.concatenate([jnp.zeros((tc, HD), f32), jnp.where(even, 0.0, sin)], 0)
    return cos_f, sin_a, sin_b


N_CHIPS = 4
SHARD = N_IN // N_CHIPS


def _group_ranges():
    return dict(M=[(S_MG, 4 * D), (S_GA, BRW), (S_GB, BRW), (S_GC, BRW), (S_GD, BRW)], A=[(S_Q, W_A)],
                C=[(S_B, 3 * BRW), (S_DA, 2 * BRW)], G=[(S_CQ, 2 * C_KW + BRW), (S_R, 2 * C_RANK)])


def _group_weights(w4):
    out = {}
    for k, ranges in _group_ranges().items():
        parts = []
        for a, n in ranges:
            while n > 0:
                s, r = divmod(a, SHARD)
                m = min(n, SHARD - r)
                parts.append(w4[s, r:r + m])
                a, n = a + m, n - m
        if k == "G":
            parts.append(jnp.zeros((LANE - 2 * C_RANK, D), w4.dtype))
        out[k] = jnp.concatenate(parts, 0)
    return out


def _ungroup(g):
    secs = []
    for k, ranges in _group_ranges().items():
        off = 0
        for a, n in ranges:
            secs.append((a, g[k][off:off + n]))
            off += n
    return jnp.concatenate([v for _, v in sorted(secs, key=lambda t: t[0])], 0)


PROJ_TN = dict(M=2048, A=1024, C=1280, G=1152)
DU_TK = dict(M=2048, A=1024, C=BRW, G=1152)
DWP_TN = dict(M=768, A=1024, C=BRW, G=1152)


def _gate_weights(w2_l, gb_l):
    w = jnp.zeros((LANE, 2 * C_KW), f32)
    w = w.at[0:C_RANK, 0:C_KW].set(w2_l[0]).at[C_RANK:2 * C_RANK, C_KW:2 * C_KW].set(w2_l[1])
    return w, jnp.concatenate([gb_l[0], gb_l[1]])[None, :]


def _local_step(x1, c1, ctx1, tgt1, c_ctx, w_mod, b_mod, weights_of, q_norm, k_norm, b_conv, w2, gb, c_norm, d_conv_w,
                d_conv_b, d_norm_g, d_norm_b, grads_done, ln_g, ln_b, tm, token=None):
    tc, tl = ctx1.shape[0], x1.shape[0]
    T = tc + tl
    rc = min(256, tc)
    tmb = tm // 2
    tmm = 768 if T % 768 == 0 else tm
    rope = _rope_tables(tc, tl)
    cin = jnp.concatenate([c1, c_ctx[None, :], jnp.zeros((SUB - 2, D), f32)], 0)
    if token is not None:
        cin = cin + token[:, 0:1]
    modv = _mod_fwd(cin, w_mod, b_mod)
    modv = [modv[l] for l in range(DEPTH)]
    row = lambda v: v[None, :]

    h = jnp.concatenate([ctx1, x1], 0)
    saved, wp, w_br, w_out = [], [None] * DEPTH, [None] * DEPTH, [None] * DEPTH
    for l in range(DEPTH):
        wp[l], merge_weights = weights_of(l, h)
        u = _ln_fwd(h, modv[l], tc, tm, f"ln_fwd{l}")
        P = {k: _matmul(u, wp[l][k], "nt", tmm, PROJ_TN[k], D, f"proj{l}{k}") for k in GROUPS}
        qn, kn, vb = _prep_fwd(P["A"], row(q_norm[l]), row(k_norm[l]), rope, tm, f"prep_fwd{l}")
        ya = _attn_fwd(qn, kn, vb, tc, tm, f"attn_fwd{l}")
        yb, hh = _conv_fwd(P["C"], b_conv[l], d_conv_w[l], row(d_conv_b[l]), tc, tl, rc, f"conv_fwd{l}")
        w2p, b2p = _gate_weights(w2[l], gb[l])
        gla = _gla_fwd(P["G"], w2p, b2p, tc, f"gla_fwd{l}")
        o2, ssave = gla[:2], gla[2:]
        w_br[l], w_out[l] = merge_weights(o2[0])
        hn = _merge_fwd(h, modv[l], ya, yb, o2, hh, P["M"], row(c_norm[l]), row(d_norm_g[l]), row(d_norm_b[l]),
                        row(ln_g[l]), row(ln_b[l]), w_br[l], w_out[l], tc, tm, f"merge_fwd{l}")
        saved.append((h, u, P, qn, kn, vb, ya, yb, hh, o2, ssave, w2p, b2p))
        h = hn

    dh, lparts = _loss_kernel(h, tgt1, tc, tm, "loss")
    loss = jnp.sum(lparts[:, 0, 0])

    g = {k: [None] * DEPTH for k in ("wp", "q_norm", "k_norm", "b_conv", "w2", "gb", "c_norm", "d_conv_w", "d_conv_b",
                                     "d_norm_g", "d_norm_b", "w_br", "w_out", "ln_g", "ln_b", "modv")}
    for l in reversed(range(DEPTH)):
        h_in, u, P, qn, kn, vb, ya, yb, hh, o2, ssave, w2p, b2p = saved[l]
        dP = {}
        (dh_res, dm_mg, dya, dyb, doc, dhh, dP["M"], br, z, acc, dy, dv5, dvd) = _merge_bwd(
            dh, h_in, modv[l], ya, yb, o2, hh, P["M"], row(c_norm[l]), row(d_norm_g[l]), row(d_norm_b[l]),
            row(ln_g[l]), row(ln_b[l]), w_br[l], w_out[l], tc, tmb, f"merge_bwd{l}")
        g["w_br"][l] = _matmul_tn_batched(br, z, N_CHIPS, f"dwbr{l}")
        g["w_out"][l] = _matmul(acc, dy, "tn", D, D, T, f"dwout{l}")
        tk = grads_done(l, {k: g[k][l] for k in ("w_br", "w_out")})
        qg_l = row(q_norm[l]) if tk is None else row(q_norm[l]) + tk[0:1, :]
        v5 = jnp.sum(dv5, 0)
        g["c_norm"][l], g["d_norm_g"][l], g["d_norm_b"][l] = v5[0], v5[1], v5[2]
        vd = jnp.sum(dvd, 0)
        g["ln_g"][l], g["ln_b"][l] = vd[0], vd[1]
        dqn, dkn, dv = _attn_bwd(qn, kn, vb, dya, tc, tm, f"attn_bwd{l}")
        dP["A"], dqk = _prep_bwd(P["A"], dqn, dkn, dv, qg_l, row(k_norm[l]), rope, tm, f"prep_bwd{l}")
        dqk = jnp.sum(dqk, 0)
        g["q_norm"][l], g["k_norm"][l] = dqk[0], dqk[1]
        dP["C"], dwb, dwd, dbd = _conv_bwd(P["C"], dyb, dhh, b_conv[l], d_conv_w[l], tc, tl, rc, f"conv_bwd{l}")
        g["b_conv"][l], g["d_conv_w"][l], g["d_conv_b"][l] = dwb, dwd, dbd[0]
        dpf, dpb, dw2p, db2p = _gla_bwd(P["G"], w2p, b2p, ssave, doc, tc, f"gla_bwd{l}")
        dP["G"] = _sum_dirs(dpf, dpb, tm, f"gla_sum{l}")
        db2p = db2p[0]
        g["w2"][l] = jnp.stack([dw2p[0:C_RANK, 0:C_KW], dw2p[C_RANK:2 * C_RANK, C_KW:2 * C_KW]])
        g["gb"][l] = jnp.stack([db2p[0:C_KW], db2p[C_KW:2 * C_KW]])
        g["wp"][l] = {k: _matmul(dP[k], u, "tn", DWP_TN[k], D, T, f"dwp{l}{k}") for k in GROUPS}
        tk = grads_done(l, {"wp": g["wp"][l]})
        du = None
        for k in GROUPS:
            du = _matmul(dP[k], wp[l][k], "nn", tmm, D, DU_TK[k], f"du{l}{k}", add=du, after=tk if du is None else None)
        dh, dm_ln = _ln_bwd(du, h_in, dh_res, modv[l], tc, tm, f"ln_bwd{l}")
        g["modv"][l] = jnp.sum(dm_mg, 0) + jnp.sum(dm_ln, 0)

    dmodv = jnp.stack(g.pop("modv"))
    g["w_mod"], dcin = _mod_bwd(cin, w_mod, dmodv)
    g["b_mod"] = dmodv[:, 0, :] + dmodv[:, 1, :]
    g["c_ctx"] = jnp.sum(dcin, (0, 1))[1]
    return loss, dh[tc:], g


HALF_TL = 256


def _adamw(w, g, m, v, name, tr=128, after=None):
    L, R, C = w.shape
    if R % tr == 0:
        grid, spec = (L, R // tr), pl.BlockSpec((None, tr, C), lambda l, i: (l, i, 0))
    elif R * C * 4 <= (1 << 20):
        grid, spec = (L, 1), pl.BlockSpec((None, R, C), lambda l, i: (l, 0, 0))
    else:
        grid, spec = (L, C // HALF_TL), pl.BlockSpec((None, R, HALF_TL), lambda l, i: (l, 0, i))

    def body(w_ref, g_ref, m_ref, v_ref, *rest):
        d_ref, nm_ref, nv_ref = rest[-3:]
        gg = g_ref[...]
        nm = B1 * m_ref[...] + (1.0 - B1) * gg
        nv = B2 * v_ref[...] + (1.0 - B2) * (gg * gg)
        m_hat = nm / (1.0 - B1 ** STEP)
        v_hat = nv / (1.0 - B2 ** STEP)
        d_ref[...] = -LR * (m_hat / (jnp.sqrt(v_hat) + AEPS) + WD * w_ref[...])
        nm_ref[...] = nm
        nv_ref[...] = nv

    return pl.pallas_call(
        body, name=name, grid=grid, in_specs=[spec] * 4 + ([] if after is None else [pl.BlockSpec(memory_space=pl.ANY)]),
        out_specs=[spec] * 3, out_shape=[jax.ShapeDtypeStruct((L, R, C), f32)] * 3,
        compiler_params=_cparams(("parallel", "parallel")),
    )(w, g, m, v, *([] if after is None else [after]))


MESH = pl.DeviceIdType.MESH
ANY = pl.BlockSpec(memory_space=pl.ANY)
N_CHIPS = 4


def _place():
    x, y, c = lax.axis_index("x"), lax.axis_index("y"), lax.axis_index("c")
    chips = [(1 - x, y), (x, 1 - y), (1 - x, 1 - y)]
    return x, y, c, chips


def _half(ref, c, axis):
    n = ref.shape[axis] // 2
    last = axis in (-1, ref.ndim - 1)
    idx = [slice(None)] * ref.ndim
    idx[axis] = pl.ds(pl.multiple_of(c * n, LANE if last else SUB), n)
    return ref.at[tuple(idx)]


def _half_shape(shape, axis):
    s = list(shape)
    s[axis] //= 2
    return tuple(s)


def _all_gather(arrs, axes, name):
    n = len(arrs)

    def body(*refs):
        ins, outs = refs[:n], refs[n:2 * n]
        send, recv = refs[2 * n:]
        x, y, c, chips = _place()
        me, sib = 2 * x + y, (x, y, 1 - c)

        def copy(a, k, chip_idx, cc, to, src=None):
            blk = _half(outs[a].at[chip_idx], cc, axes[a])
            return pltpu.make_async_remote_copy(src_ref=blk if src is None else src, dst_ref=blk,
                                                send_sem=send.at[7 * a + k], recv_sem=recv.at[7 * a + k],
                                                device_id=to, device_id_type=MESH)

        own = [pltpu.make_async_remote_copy(src_ref=ins[a], dst_ref=outs[a].at[me], send_sem=send.at[7 * a + 6],
                                            recv_sem=recv.at[7 * a + 6], device_id=sib, device_id_type=MESH)
               for a in range(n)]
        first = own + [copy(a, j, me, c, (*chip, c), src=_half(ins[a], c, axes[a]))
                       for a in range(n) for j, chip in enumerate(chips)]
        for cp in first:
            cp.start()
        passed = []
        for a in range(n):
            for j, chip in enumerate(chips):
                k = 2 * chip[0] + chip[1]
                copy(a, j, k, c, sib).wait_recv()
                fwd = copy(a, 3 + j, k, c, sib)
                fwd.start()
                passed.append(fwd)
        for a in range(n):
            own[a].wait_recv()
            for j, chip in enumerate(chips):
                copy(a, 3 + j, 2 * chip[0] + chip[1], 1 - c, sib).wait_recv()
        for cp in first + passed:
            cp.wait_send()

    return pl.pallas_call(
        body, name=name, in_specs=[ANY] * n, out_specs=[ANY] * n,
        out_shape=[jax.ShapeDtypeStruct((N_CHIPS,) + a.shape, a.dtype) for a in arrs],
        scratch_shapes=[pltpu.SemaphoreType.DMA((7 * n,)), pltpu.SemaphoreType.DMA((7 * n,))],
    )(*arrs)


def _sibling_halves(arrs, axes, name):
    n = len(arrs)

    def body(*refs):
        ins, outs = refs[:n], refs[n:2 * n]
        send, recv = refs[2 * n:]
        x, y, c, _ = _place()
        cps = [pltpu.make_async_remote_copy(src_ref=_half(ins[a], 1 - c, axes[a] + 1), dst_ref=outs[a], send_sem=send.at[a],
                                            recv_sem=recv.at[a], device_id=(x, y, 1 - c), device_id_type=MESH)
               for a in range(n)]
        for cp in cps:
            cp.start()
        for cp in cps:
            cp.wait()

    return pl.pallas_call(
        body, name=name, in_specs=[ANY] * n, out_specs=[ANY] * n,
        out_shape=[jax.ShapeDtypeStruct(_half_shape(a.shape, axes[i] + 1), a.dtype) for i, a in enumerate(arrs)],
        scratch_shapes=[pltpu.SemaphoreType.DMA((n,)), pltpu.SemaphoreType.DMA((n,))],
    )(*arrs)


def _add_half(gfull, land, cidx, axis, name, tr=128, out_dtype=bf16):
    _, hr, hc = land.shape
    if axis == 0:
        tr = min(tr, hr)
        nb, blk = hr // tr, (None, tr, hc)
        g_spec = pl.BlockSpec(blk, lambda s, i, cr: (s, cr[0] * nb + i, 0))
        l_spec = pl.BlockSpec(blk, lambda s, i, cr: (s, i, 0))
    else:
        nb, blk = hc // HALF_TL, (None, hr, HALF_TL)
        g_spec = pl.BlockSpec(blk, lambda s, i, cr: (s, 0, cr[0] * nb + i))
        l_spec = pl.BlockSpec(blk, lambda s, i, cr: (s, 0, i))

    def body(c_ref, g_ref, l_ref, o_ref):
        o_ref[...] = (g_ref[...].astype(f32) + l_ref[...].astype(f32)).astype(o_ref.dtype)

    return pl.pallas_call(
        body, name=name,
        grid_spec=pltpu.PrefetchScalarGridSpec(
            num_scalar_prefetch=1, grid=(N_CHIPS, nb), in_specs=[g_spec, l_spec], out_specs=l_spec),
        out_shape=jax.ShapeDtypeStruct((N_CHIPS, hr, hc), out_dtype),
        compiler_params=_cparams(("parallel", "parallel")),
    )(cidx, gfull, land)


def _chip_exchange(arrs, name):
    n = len(arrs)

    def body(*refs):
        ins, outs = refs[:n], refs[n:2 * n]
        send, recv = refs[2 * n:]
        x, y, c, chips = _place()
        me = 2 * x + y
        cps = []
        for a in range(n):
            for j, chip in enumerate(chips):
                k = 2 * chip[0] + chip[1]
                cps.append((pltpu.make_async_remote_copy(
                    src_ref=ins[a].at[k], dst_ref=outs[a].at[me], send_sem=send.at[3 * a + j], recv_sem=recv.at[3 * a + j],
                    device_id=(*chip, c), device_id_type=MESH), a, j, k))
        for cp, *_ in cps:
            cp.start()
        for cp, a, j, k in cps:
            pltpu.make_async_remote_copy(src_ref=ins[a].at[k], dst_ref=outs[a].at[k], send_sem=send.at[3 * a + j],
                                         recv_sem=recv.at[3 * a + j], device_id=(x, y, c), device_id_type=MESH).wait_recv()
        for cp, *_ in cps:
            cp.wait_send()

    return pl.pallas_call(
        body, name=name, in_specs=[ANY] * n, out_specs=[ANY] * n,
        out_shape=[jax.ShapeDtypeStruct(a.shape, a.dtype) for a in arrs],
        scratch_shapes=[pltpu.SemaphoreType.DMA((3 * n,)), pltpu.SemaphoreType.DMA((3 * n,))],
    )(*arrs)


def _sum_chips(land, own, place, axis, layer, into, name, tr=128):
    _, hr, hc = land.shape
    fresh = not hasattr(into, "dtype")
    shape = tuple(into) if fresh else into.shape
    if axis == 0:
        tr = min(tr, hr)
        nb, blk = hr // tr, (tr, hc)
        l_map, m_map = (lambda i, p: (0, i, 0)), (lambda i, p: (p[0], i, 0))
        o_map = lambda i, p: (layer, p[1] * nb + i, 0)
    else:
        nb, blk = hc // HALF_TL, (hr, HALF_TL)
        l_map, m_map = (lambda i, p: (0, 0, i)), (lambda i, p: (p[0], 0, i))
        o_map = lambda i, p: (layer, 0, p[1] * nb + i)

    def body(p_ref, l_ref, o_ref, *rest):
        me = p_ref[0]
        mine = o_ref[...].astype(f32)
        acc = None
        for k in range(N_CHIPS):
            t = jnp.where(me == k, mine, l_ref[k].astype(f32))
            acc = t if acc is None else acc + t
        rest[-1][...] = acc

    return pl.pallas_call(
        body, name=name,
        grid_spec=pltpu.PrefetchScalarGridSpec(
            num_scalar_prefetch=1, grid=(nb,),
            in_specs=[pl.BlockSpec((N_CHIPS,) + blk, l_map), pl.BlockSpec((None,) + blk, m_map)] + ([] if fresh else [ANY]),
            out_specs=pl.BlockSpec((None,) + blk, o_map)),
        out_shape=jax.ShapeDtypeStruct(shape, f32),
        input_output_aliases={} if fresh else {3: 0},
        compiler_params=_cparams(("parallel",)),
    )(place, land, own, *([] if fresh else [into]))


def _sibling_fill(arrs, axes, name):
    n = len(arrs)

    def body(*refs):
        outs = refs[n:2 * n]
        send, recv = refs[2 * n:]
        x, y, c, _ = _place()
        cps = [pltpu.make_async_remote_copy(src_ref=_half(outs[a], c, axes[a] + 1), dst_ref=_half(outs[a], c, axes[a] + 1),
                                            send_sem=send.at[a], recv_sem=recv.at[a], device_id=(x, y, 1 - c),
                                            device_id_type=MESH) for a in range(n)]
        for cp in cps:
            cp.start()
        for a in range(n):
            blk = _half(outs[a], 1 - c, axes[a] + 1)
            pltpu.make_async_remote_copy(src_ref=blk, dst_ref=blk, send_sem=send.at[a], recv_sem=recv.at[a],
                                         device_id=(x, y, 1 - c), device_id_type=MESH).wait_recv()
        for cp in cps:
            cp.wait_send()

    return pl.pallas_call(
        body, name=name, in_specs=[ANY] * n, out_specs=[ANY] * n,
        out_shape=[jax.ShapeDtypeStruct(a.shape, a.dtype) for a in arrs],
        input_output_aliases={a: a for a in range(n)},
        scratch_shapes=[pltpu.SemaphoreType.DMA((n,)), pltpu.SemaphoreType.DMA((n,))],
    )(*arrs)


HBM = pl.BlockSpec(memory_space=pltpu.HBM)
SEM = pl.BlockSpec(memory_space=pltpu.SEMAPHORE)
EFFECT = pltpu.SideEffectType.DATAFLOW_SIDE_EFFECTING
PEERS = 4


def _split_copies(srcs, lands, send, recv, gather):
    x, y, c, chips = _place()
    me = 2 * x + y
    peers = [((*chip, c), 2 * chip[0] + chip[1]) for chip in chips] + ([((x, y, 1 - c), me)] if gather else [])
    out = []
    for a in range(len(srcs)):
        for j, (dev, k) in enumerate(peers):
            src = srcs[a] if gather else srcs[a].at[k]
            sems = dict(send_sem=send.at[PEERS * a + j], recv_sem=recv.at[PEERS * a + j], device_id=dev, device_id_type=MESH)
            out.append((pltpu.make_async_remote_copy(src_ref=src, dst_ref=lands[a].at[me], **sems),
                        pltpu.make_async_remote_copy(src_ref=src, dst_ref=lands[a].at[k], **sems)))
    return out


def _split_start(srcs, gather, after, name):
    n = len(srcs)
    lands = [lax.empty(((N_CHIPS,) + s.shape) if gather else s.shape, s.dtype) for s in srcs]

    def body(*refs):
        send, recv = refs[2 * n + 1], refs[2 * n + 2]
        for start, _ in _split_copies(refs[:n], refs[n:2 * n], send, recv, gather):
            start.start()
        refs[-1][...] = jnp.zeros_like(refs[-1])

    sems = pltpu.SemaphoreType.DMA((PEERS * n,))
    hbm = lambda a: pltpu.with_memory_space_constraint(a, pltpu.HBM)
    out = pl.pallas_call(
        body, name=name,
        out_shape=(sems, sems, *[pltpu.HBM(a.shape, a.dtype) for a in srcs + lands], jax.ShapeDtypeStruct((SUB, LANE), f32)),
        in_specs=[HBM] * (2 * n) + [ANY], out_specs=(SEM, SEM, *[HBM] * (2 * n), pl.BlockSpec(memory_space=pltpu.VMEM)),
        input_output_aliases={i: 2 + i for i in range(2 * n)},
        compiler_params=pltpu.CompilerParams(has_side_effects=EFFECT),
    )(*[hbm(a) for a in srcs + lands], after)
    return out[0], out[1], list(out[2:2 + n]), list(out[2 + n:2 + 2 * n]), out[-1]


def _split_wait(send, recv, srcs, lands, gather, after, name):
    n = len(srcs)

    def body(*refs):
        for start, arrival in _split_copies(refs[:n], refs[n:2 * n], refs[2 * n], refs[2 * n + 1], gather):
            start.wait_send()
            arrival.wait_recv()

    out = pl.pallas_call(
        body, name=name, out_shape=[pltpu.HBM(a.shape, a.dtype) for a in srcs + lands],
        in_specs=[HBM] * (2 * n) + [SEM, SEM, ANY], out_specs=[HBM] * (2 * n),
        input_output_aliases={i: i for i in range(2 * n)},
        compiler_params=pltpu.CompilerParams(has_side_effects=EFFECT),
    )(*srcs, *lands, send, recv, after)
    return list(out[:n]), list(out[n:])


N_DEV = 8


def _all_reduce_small(v, name):
    R = v.shape[0]

    def body(v_ref, o_ref, land_ref, send, recv):
        x, y, c, _ = _place()
        me = 4 * x + 2 * y + c
        land_ref[me] = v_ref[...]
        cps = []
        for m in range(1, N_DEV):
            px, py, pc = [(1 - q) if (m >> s) & 1 else q for q, s in ((x, 2), (y, 1), (c, 0))]
            cps.append((pltpu.make_async_remote_copy(src_ref=v_ref, dst_ref=land_ref.at[me], send_sem=send.at[m - 1],
                                                     recv_sem=recv.at[m - 1], device_id=(px, py, pc), device_id_type=MESH),
                        4 * px + 2 * py + pc, m))
        for cp, *_ in cps:
            cp.start()
        for cp, peer, m in cps:
            pltpu.make_async_remote_copy(src_ref=v_ref, dst_ref=land_ref.at[peer], send_sem=send.at[m - 1],
                                         recv_sem=recv.at[m - 1], device_id=(x, y, c), device_id_type=MESH).wait_recv()
        for cp, *_ in cps:
            cp.wait_send()
        acc = land_ref[0]
        for k in range(1, N_DEV):
            acc = acc + land_ref[k]
        o_ref[...] = acc

    vm = pl.BlockSpec(memory_space=pltpu.VMEM)
    return pl.pallas_call(
        body, name=name, in_specs=[vm], out_specs=vm, out_shape=jax.ShapeDtypeStruct(v.shape, f32),
        scratch_shapes=[pltpu.VMEM((N_DEV, R, LANE), f32), pltpu.SemaphoreType.DMA((N_DEV - 1,)),
                        pltpu.SemaphoreType.DMA((N_DEV - 1,))],
        compiler_params=pltpu.CompilerParams(vmem_limit_bytes=VMEM_LIMIT),
    )(v)


def _pack_small(arrs, mult=2 * SUB):
    flat = jnp.concatenate([a.reshape(-1) for a in arrs])
    rows = -(-flat.shape[0] // (LANE * mult)) * mult
    return jnp.pad(flat, (0, rows * LANE - flat.shape[0])).reshape(rows, LANE)


def _unpack_small(vec, shapes):
    flat, out, o = vec.reshape(-1), [], 0
    for s in shapes:
        n = int(np.prod(s))
        out.append(flat[o:o + n].reshape(s))
        o += n
    return out


REPL_SMALL = ("c_ctx", "b_mod", "q_norm", "k_norm", "c_norm", "d_conv_b", "d_norm_g", "d_norm_b", "ln_g", "ln_b")
SHARD_SMALL = ("b_conv", "c_gate_w2", "c_gate_b", "d_conv_w")
BIG = ("w_mod", "w_in", "w_br", "w_out")
ORDER = ("c_ctx", "w_mod", "b_mod", "w_in", "q_norm", "k_norm", "b_conv", "c_gate_w2", "c_gate_b", "c_norm", "d_conv_w",
         "d_conv_b", "d_norm_g", "d_norm_b", "w_br", "w_out", "ln_g", "ln_b")


def _unshard_last(g4, shard_shape):
    g = g4.reshape((N_CHIPS,) + tuple(shard_shape))
    g = jnp.moveaxis(g, 0, -2)
    return g.reshape(tuple(shard_shape[:-1]) + (N_CHIPS * shard_shape[-1],))


def _pieces_last(full):
    w = full.shape[-1] // N_CHIPS
    g = full.reshape(full.shape[:-1] + (N_CHIPS, w))
    return jnp.moveaxis(g, -2, 0).reshape(N_CHIPS, -1, w)


def kernel(x, c, ctx, c_ctx, w_mod, b_mod, w_in, q_norm, k_norm, b_conv, c_gate_w2, c_gate_b, c_norm, d_conv_w, d_conv_b, d_norm_g, d_norm_b, w_br, w_out, ln_g, ln_b, loss_target, m_c_ctx, m_w_mod, m_b_mod, m_w_in, m_q_norm, m_k_norm, m_b_conv, m_c_gate_w2, m_c_gate_b, m_c_norm, m_d_conv_w, m_d_conv_b, m_d_norm_g, m_d_norm_b, m_w_br, m_w_out, m_ln_g, m_ln_b, v_c_ctx, v_w_mod, v_b_mod, v_w_in, v_q_norm, v_k_norm, v_b_conv, v_c_gate_w2, v_c_gate_b, v_c_norm, v_d_conv_w, v_d_conv_b, v_d_norm_g, v_d_norm_b, v_w_br, v_w_out, v_ln_g, v_ln_b):
    W = dict(c_ctx=c_ctx, w_mod=w_mod, b_mod=b_mod, w_in=w_in, q_norm=q_norm, k_norm=k_norm, b_conv=b_conv,
             c_gate_w2=c_gate_w2, c_gate_b=c_gate_b, c_norm=c_norm, d_conv_w=d_conv_w, d_conv_b=d_conv_b,
             d_norm_g=d_norm_g, d_norm_b=d_norm_b, w_br=w_br, w_out=w_out, ln_g=ln_g, ln_b=ln_b)
    M = dict(c_ctx=m_c_ctx, w_mod=m_w_mod, b_mod=m_b_mod, w_in=m_w_in, q_norm=m_q_norm, k_norm=m_k_norm, b_conv=m_b_conv,
             c_gate_w2=m_c_gate_w2, c_gate_b=m_c_gate_b, c_norm=m_c_norm, d_conv_w=m_d_conv_w, d_conv_b=m_d_conv_b,
             d_norm_g=m_d_norm_g, d_norm_b=m_d_norm_b, w_br=m_w_br, w_out=m_w_out, ln_g=m_ln_g, ln_b=m_ln_b)
    V = dict(c_ctx=v_c_ctx, w_mod=v_w_mod, b_mod=v_b_mod, w_in=v_w_in, q_norm=v_q_norm, k_norm=v_k_norm, b_conv=v_b_conv,
             c_gate_w2=v_c_gate_w2, c_gate_b=v_c_gate_b, c_norm=v_c_norm, d_conv_w=v_d_conv_w, d_conv_b=v_d_conv_b,
             d_norm_g=v_d_norm_g, d_norm_b=v_d_norm_b, w_br=v_w_br, w_out=v_w_out, ln_g=v_ln_g, ln_b=v_ln_b)
    chip = 2 * lax.axis_index("x") + lax.axis_index("y")
    cidx = lax.axis_index("c").astype(jnp.int32).reshape(1)

    place = jnp.stack([chip, lax.axis_index("c")]).astype(jnp.int32)

    AXIS = dict(w_in=1, w_mod=0, w_br=0, w_out=0)
    ex = dict(w_in=lambda a: jnp.swapaxes(a, 1, 2), w_mod=lambda a: a.reshape(1, DEPTH * D, -1),
              w_br=lambda a: a.reshape(DEPTH, 4 * BRW, -1), w_out=lambda a: a)
    Wx, Mx, Vx = ({k: ex[k](P_[k]) for k in BIG} for P_ in (W, M, V))

    LAYER, MERGE = ("w_in", "w_br", "w_out"), ("w_br", "w_out")
    small_shard = _pack_small([W[k] for k in SHARD_SMALL])
    keys0 = ("w_in", "w_mod")
    got = _all_gather([Wx[k][0].astype(bf16) for k in keys0] + [small_shard], [AXIS[k] for k in keys0] + [0], "all_gather0")
    smalls = [_unpack_small(got[-1][s], [W[k].shape for k in SHARD_SMALL]) for s in range(N_CHIPS)]
    full = {k: jnp.concatenate([smalls[s][i] for s in range(N_CHIPS)], axis=-1) for i, k in enumerate(SHARD_SMALL)}
    wmod = got[1].reshape(N_CHIPS, DEPTH, D, 3 * D // N_CHIPS)
    ag0b = _split_start([Wx[k][0].astype(bf16) for k in MERGE], True, got[0], "all_gather0b_start")
    ag1 = _split_start([Wx[k][1].astype(bf16) for k in LAYER], True, ag0b[4], "all_gather1_start")

    def merge_form(w_br4, w_out4):
        return jnp.moveaxis(w_br4.reshape(N_CHIPS, 4, BRW, D // N_CHIPS), 0, 2).reshape(4, BRW, D), w_out4.reshape(D, D)

    def weights_of(l, h):
        if l == 0:
            return _group_weights(got[0]), lambda after: merge_form(*_split_wait(*ag0b[:4], True, after, "all_gather0b_wait")[1])
        g3 = _split_wait(*ag1[:4], True, h, "all_gather1_wait")[1]
        return _group_weights(g3[0]), lambda after: merge_form(g3[1], g3[2])

    red = {k: Wx[k].shape for k in BIG}
    flights, held = {}, {}

    def launch(tag, l, pieces, after=None):
        keys = list(pieces)
        land_a = _sibling_halves([pieces[k] for k in keys], [AXIS[k] for k in keys], f"rs_sibling_halves{tag}")
        pair = [_add_half(pieces[k], la, cidx, AXIS[k], f"rs_pair_sum{tag}_{k}") for k, la in zip(keys, land_a)]
        after = jnp.zeros((SUB, LANE), f32) if after is None else after
        flights[tag] = (l, keys, _split_start(pair, False, after, f"rs_chip_exchange{tag}_start"))
        return flights[tag][2][4]

    def land(tag, after):
        l, keys, flight = flights.pop(tag)
        pair, land_b = _split_wait(*flight[:4], False, after, f"rs_chip_exchange{tag}_wait")
        for k, lb, pr in zip(keys, land_b, pair):
            red[k] = _sum_chips(lb, pr, place, AXIS[k], l, red[k], f"rs_chip_sum{tag}_{k}")

    def grads_done(l, gl):
        if "wp" in gl:
            pieces = dict(w_in=_ungroup(gl["wp"]).astype(bf16).reshape(N_CHIPS, SHARD, D))
            return launch("0c", 0, pieces) if l == 0 else launch("1", 1, {**pieces, **held.pop(1)})
        pieces = dict(w_br=gl["w_br"].reshape(N_CHIPS, 4 * BRW, D // N_CHIPS), w_out=gl["w_out"].reshape(N_CHIPS, D // N_CHIPS, D))
        if l == 0:
            return launch("0b", 0, pieces)
        held[1] = pieces
        return None

    loss, gx, g = _local_step(
        x[0], c, ctx[0], loss_target[0], c_ctx, wmod, b_mod, weights_of, q_norm, k_norm, full["b_conv"],
        full["c_gate_w2"], full["c_gate_b"], c_norm, full["d_conv_w"], d_conv_b, d_norm_g, d_norm_b,
        grads_done, ln_g, ln_b, tm=256, token=ag1[4])
    g["c_gate_w2"], g["c_gate_b"] = g.pop("w2"), g.pop("gb")
    loss = lax.psum(loss, ("x", "y", "c"))

    w_mod_pieces = g["w_mod"].reshape(N_CHIPS, DEPTH * D, 3 * D // N_CHIPS)
    g = {k: (jnp.stack(v) if isinstance(v, list) else v) for k, v in g.items() if k not in ("wp", "w_br", "w_out", "w_mod")}

    small_names = REPL_SMALL + SHARD_SMALL
    gs = _all_reduce_small(_pack_small([g[k] for k in small_names]), "all_reduce_small")
    gsm = dict(zip(small_names, _unpack_small(gs, [g[k].shape for k in small_names])))
    for k in SHARD_SMALL:
        wdt = W[k].shape[-1]
        gsm[k] = lax.dynamic_slice_in_dim(gsm[k], chip * wdt, wdt, axis=gsm[k].ndim - 1)

    grad, delta, new_m, new_v = {}, {}, {}, {}

    def adamw_big(keys, after):
        filled = _sibling_fill([red[k] for k in keys], [AXIS[k] for k in keys], "rs_sibling_fill_" + keys[0])
        for k, r in zip(keys, filled):
            back = (lambda a: jnp.swapaxes(a, 1, 2)) if k == "w_in" else (lambda a: a.reshape(W[k].shape))
            d_, m_, v_ = _adamw(Wx[k], r, Mx[k], Vx[k], f"adamw_{k}", after=after)
            grad[k], delta[k], new_m[k], new_v[k] = back(r), back(d_), back(m_), back(v_)
        return d_

    token = launch("0d", 0, {"w_mod": w_mod_pieces}, after=gs)
    land("1", gx)
    land("0b", gx)
    last = adamw_big(MERGE, token)
    shapes = [W[k].shape for k in small_names]
    d_, m_, v_ = _adamw(*[_pack_small([P_[k] for k in small_names])[None] for P_ in (W, gsm, M, V)], "adamw_small", after=last)
    for k, dd, mm_, vv in zip(small_names, _unpack_small(d_, shapes), _unpack_small(m_, shapes), _unpack_small(v_, shapes)):
        grad[k], delta[k], new_m[k], new_v[k] = gsm[k], dd, mm_, vv
    land("0c", d_)
    land("0d", d_)
    adamw_big(("w_in", "w_mod"), None)

    return (loss, gx[None], *[grad[k] for k in ORDER], *[delta[k] for k in ORDER], *[new_m[k] for k in ORDER],
            *[new_v[k] for k in ORDER])
```

```python
import functools

import jax
import jax.numpy as jnp
import numpy as np
from jax import lax
from jax.experimental import pallas as pl
from jax.experimental.pallas import tpu as pltpu

f32 = jnp.float32
bf16 = jnp.bfloat16

D = 1024
DEPTH = 2
GRID_W = 64
BRW = 512
HD = 128
A_HEADS = 4
C_HEADS = 4
C_KW = 256
C_RANK = 16
C_TAU = 16.0
CH = 128
KB = 3
KD = 31
ALPHA = (2 * DEPTH) ** 0.25
EPS = 1e-6
ROPE_THETA = 10000.0
N_IN = 10784
LR, B1, B2, AEPS, WD, STEP = 0.001, 0.9, 0.999, 1e-08, 0.01, 10

W_M, W_A, W_C, W_G = 4 * D + 4 * BRW, 1024, 5 * BRW, 1152
GROUPS = ("M", "A", "C", "G")
GROUP_W = dict(M=W_M, A=W_A, C=W_C, G=W_G)
M_GA, M_GB, M_GC, M_GD = 4 * D, 4 * D + BRW, 4 * D + 2 * BRW, 4 * D + 3 * BRW
A_K, A_V = 512, 768
G_K, G_V, G_R = 256, 512, 1024
CT = 5 * 128
S_Q, S_GA, S_B, S_C, S_X, S_GB, S_CQ, S_CV, S_GC, S_R, S_DA, S_DG, S_GD, S_MG = (
    0, 1024, 1536, 2048, 2560, 3072, 3584, 4096, 4608, 5120, 5152, 5664, 6176, 6688)

LANE = 128
SUB = 8
VMEM_LIMIT = 56 * 1024 * 1024
CONV_PAD = 16
GLA_SUB = 16
GLA_CLAMP = 60.0


def _cparams(sem, vmem=VMEM_LIMIT):
    return pltpu.CompilerParams(dimension_semantics=sem, vmem_limit_bytes=vmem)


def _dg(a, b, ca, cb):
    return lax.dot_general(a.astype(bf16), b.astype(bf16), (((ca,), (cb,)), ((), ())),
                           preferred_element_type=f32)


@jax.custom_vjp
def mm(a, b):
    return _dg(a, b, 1, 0)


mm.defvjp(lambda a, b: (_dg(a, b, 1, 0), (a, b)),
          lambda r, ct: (_dg(ct, r[1], 1, 1).astype(r[0].dtype), _dg(r[0], ct, 0, 0).astype(r[1].dtype)))


@jax.custom_vjp
def mm_nt(a, b):
    return _dg(a, b, 1, 1)


mm_nt.defvjp(lambda a, b: (_dg(a, b, 1, 1), (a, b)),
             lambda r, ct: (_dg(ct, r[1], 1, 0).astype(r[0].dtype), _dg(ct, r[0], 0, 0).astype(r[1].dtype)))


@jax.custom_vjp
def mm_tn(a, b):
    return _dg(a, b, 0, 0)


mm_tn.defvjp(lambda a, b: (_dg(a, b, 0, 0), (a, b)),
             lambda r, ct: (_dg(r[1], ct, 1, 1).astype(r[0].dtype), _dg(r[0], ct, 1, 0).astype(r[1].dtype)))


def _sigmoid(x):
    return 0.5 * jnp.tanh(0.5 * x) + 0.5


def _silu(x):
    return x * _sigmoid(x)


def _ln(x):
    mu = jnp.mean(x, -1, keepdims=True)
    xc = x - mu
    var = jnp.mean(xc * xc, -1, keepdims=True)
    return xc * lax.rsqrt(var + EPS)


def _rms(x, g):
    return x * lax.rsqrt(jnp.mean(x * x, -1, keepdims=True) + EPS) * g


@jax.custom_vjp
def _rope(x, cos_f, sin_a, sin_b):
    return x * cos_f + pltpu.roll(x, HD - 1, 1) * sin_a + pltpu.roll(x, 1, 1) * sin_b


def _rope_fwd(x, cos_f, sin_a, sin_b):
    return _rope(x, cos_f, sin_a, sin_b), (cos_f, sin_a, sin_b)


def _rope_bwd(r, ct):
    cos_f, sin_a, sin_b = r
    dx = ct * cos_f + pltpu.roll(ct * sin_a, 1, 1) + pltpu.roll(ct * sin_b, HD - 1, 1)
    return dx, jnp.zeros_like(cos_f), jnp.zeros_like(sin_a), jnp.zeros_like(sin_b)


_rope.defvjp(_rope_fwd, _rope_bwd)


def _row_ids(i, tm):
    return i * tm + lax.broadcasted_iota(jnp.int32, (tm, 1), 0)


def _partial_rows(ref, rows):
    n = len(rows)
    for k, r in enumerate(rows):
        ref[k:k + 1, :] = r
    ref[n:SUB, :] = jnp.zeros((SUB - n, ref.shape[-1]), f32)


def _matmul(a, b, mode, tm, tn, tk, name, out_dtype=f32, add=None, after=None):
    sect = a.ndim == 3
    a2 = (a.shape[1], a.shape[0] * a.shape[2]) if sect else a.shape
    if mode == "nn":
        (M, K), N = a2, b.shape[1]
        a_spec = pl.BlockSpec((None, tm, tk), lambda j, i, k: (k, i, 0)) if sect else pl.BlockSpec((tm, tk), lambda j, i, k: (i, k))
        b_spec = pl.BlockSpec((tk, tn), lambda j, i, k: (k, j))
        ca, cb = 1, 0
        assert not sect or tk == a.shape[2]
    elif mode == "nt":
        (M, K), N = a2, b.shape[0]
        assert not sect
        a_spec = pl.BlockSpec((tm, tk), lambda j, i, k: (i, k))
        b_spec = pl.BlockSpec((tn, tk), lambda j, i, k: (j, k))
        ca, cb = 1, 1
    else:
        (K, M), N = a2, b.shape[1]
        a_spec = pl.BlockSpec((None, tk, tm), lambda j, i, k: (i, k, 0)) if sect else pl.BlockSpec((tk, tm), lambda j, i, k: (k, i))
        b_spec = pl.BlockSpec((tk, tn), lambda j, i, k: (k, j))
        ca, cb = 0, 0
        assert not sect or tm == a.shape[2]
    assert M % tm == 0 and N % tn == 0 and K % tk == 0, (name, M, N, K, tm, tn, tk)
    nk = K // tk

    o_spec = pl.BlockSpec((tm, tn), lambda j, i, k: (i, j))

    def body(a_ref, b_ref, *rest):
        add_ref = rest[0] if add is not None else None
        o_ref, acc_ref = rest[-2:]
        k = pl.program_id(2)
        part = _dg(a_ref[...], b_ref[...], ca, cb)

        @pl.when(k == 0)
        def _():
            acc_ref[...] = part if add_ref is None else part + add_ref[...]

        @pl.when(k > 0)
        def _():
            acc_ref[...] += part

        @pl.when(k == nk - 1)
        def _():
            o_ref[...] = acc_ref[...].astype(o_ref.dtype)

    extra = ([] if add is None else [(o_spec, add)]) + ([] if after is None else [(pl.BlockSpec(memory_space=pl.ANY), after)])
    return pl.pallas_call(
        body, name=name, grid=(N // tn, M // tm, nk),
        in_specs=[a_spec, b_spec] + [s_ for s_, _ in extra], out_specs=o_spec,
        out_shape=jax.ShapeDtypeStruct((M, N), out_dtype),
        scratch_shapes=[pltpu.VMEM((tm, tn), f32)],
        compiler_params=_cparams(("parallel", "parallel", "arbitrary")),
    )(a, b, *[v_ for _, v_ in extra])


def _matmul_tn_batched(a, b, ns, name):
    B, K, M = a.shape
    N = b.shape[2] // ns

    def body(a_ref, b_ref, o_ref):
        o_ref[...] = _dg(a_ref[...], b_ref[...], 0, 0)

    return pl.pallas_call(
        body, name=name, grid=(B, ns),
        in_specs=[pl.BlockSpec((None, K, M), lambda i, s: (i, 0, 0)), pl.BlockSpec((None, K, N), lambda i, s: (i, 0, s))],
        out_specs=pl.BlockSpec((None, None, M, N), lambda i, s: (s, i, 0, 0)),
        out_shape=jax.ShapeDtypeStruct((ns, B, M, N), f32),
        compiler_params=_cparams(("parallel", "parallel")),
    )(a, b)


MOD_TN = 768


def _mod_fwd(cin, w_mod, b_mod):
    def body(c_ref, w_ref, b_ref, o_ref):
        o_ref[...] = mm(_silu(c_ref[...]), w_ref[...]) + b_ref[...]

    return pl.pallas_call(
        body, name="mod_fwd", grid=(DEPTH, 3 * D // MOD_TN),
        in_specs=[pl.BlockSpec((SUB, D), lambda l, j: (0, 0)),
                  pl.BlockSpec((None, None, D, MOD_TN), lambda l, j: (j, l, 0, 0)),
                  pl.BlockSpec((None, 1, MOD_TN), lambda l, j: (l, 0, j))],
        out_specs=pl.BlockSpec((None, SUB, MOD_TN), lambda l, j: (l, 0, j)),
        out_shape=jax.ShapeDtypeStruct((DEPTH, SUB, 3 * D), f32),
        compiler_params=_cparams(("parallel", "parallel")),
    )(cin, w_mod, b_mod.reshape(DEPTH, 1, 3 * D))


def _mod_bwd(cin, w_mod, dmodv):
    nj = 3 * D // MOD_TN

    def body(c_ref, w_ref, g_ref, dw_ref, dc_ref):
        _, vjp = jax.vjp(lambda c, w: mm(_silu(c), w), c_ref[...], w_ref[...].astype(f32))
        dc, dw = vjp(g_ref[...])
        dw_ref[...] = dw
        dc_ref[...] = dc

    return pl.pallas_call(
        body, name="mod_bwd", grid=(DEPTH, nj),
        in_specs=[pl.BlockSpec((SUB, D), lambda l, j: (0, 0)),
                  pl.BlockSpec((None, None, D, MOD_TN), lambda l, j: (j, l, 0, 0)),
                  pl.BlockSpec((None, SUB, MOD_TN), lambda l, j: (l, 0, j))],
        out_specs=[pl.BlockSpec((None, None, D, MOD_TN), lambda l, j: (j, l, 0, 0)),
                   pl.BlockSpec((None, None, SUB, D), lambda l, j: (l, j, 0, 0))],
        out_shape=[jax.ShapeDtypeStruct((nj, DEPTH, D, MOD_TN), f32),
                   jax.ShapeDtypeStruct((DEPTH, nj, SUB, D), f32)],
        compiler_params=_cparams(("parallel", "parallel")),
    )(cin, w_mod, dmodv)


def _u_fn(h, m_l, m_c, isctx):
    n = _ln(h)
    shift = jnp.where(isctx, m_c[:, 0:D], m_l[:, 0:D])
    scale = jnp.where(isctx, m_c[:, D:2 * D], m_l[:, D:2 * D])
    return n * (1.0 + scale) + shift


def _ln_fwd(h, modv_l, tc, tm, name):
    T = h.shape[0]

    def body(h_ref, m_ref, u_ref):
        isctx = _row_ids(pl.program_id(0), tm) < tc
        u_ref[...] = _u_fn(h_ref[...], m_ref[0:1, :], m_ref[1:2, :], isctx).astype(bf16)

    return pl.pallas_call(
        body, name=name, grid=(T // tm,),
        in_specs=[pl.BlockSpec((tm, D), lambda i: (i, 0)), pl.BlockSpec((SUB, 3 * D), lambda i: (0, 0))],
        out_specs=pl.BlockSpec((tm, D), lambda i: (i, 0)),
        out_shape=jax.ShapeDtypeStruct((T, D), bf16),
        compiler_params=_cparams(("parallel",)),
    )(h, modv_l)


def _ln_bwd(du, h, dh_res, modv_l, tc, tm, name):
    T = h.shape[0]
    nt = T // tm

    def body(du_ref, h_ref, r_ref, m_ref, dh_ref, dm_ref):
        isctx = _row_ids(pl.program_id(0), tm) < tc
        _, vjp = jax.vjp(lambda h, ml, mc: _u_fn(h, ml, mc, isctx), h_ref[...], m_ref[0:1, :], m_ref[1:2, :])
        dh, dml, dmc = vjp(du_ref[...])
        dh_ref[...] = dh + r_ref[...]
        _partial_rows(dm_ref, [dml, dmc])

    return pl.pallas_call(
        body, name=name, grid=(nt,),
        in_specs=[pl.BlockSpec((tm, D), lambda i: (i, 0)), pl.BlockSpec((tm, D), lambda i: (i, 0)),
                  pl.BlockSpec((tm, D), lambda i: (i, 0)), pl.BlockSpec((SUB, 3 * D), lambda i: (0, 0))],
        out_specs=[pl.BlockSpec((tm, D), lambda i: (i, 0)), pl.BlockSpec((None, SUB, 3 * D), lambda i: (i, 0, 0))],
        out_shape=[jax.ShapeDtypeStruct((T, D), f32), jax.ShapeDtypeStruct((nt, SUB, 3 * D), f32)],
        compiler_params=_cparams(("parallel",)),
    )(du, h, dh_res, modv_l)


def _prep_fn(q, k, qg, kg, cos_f, sin_a, sin_b):
    qs = [_rope(_rms(q[:, HD * i:HD * (i + 1)], qg), cos_f, sin_a, sin_b) for i in range(A_HEADS)]
    ks = [_rope(_rms(k[:, HD * i:HD * (i + 1)], kg), cos_f, sin_a, sin_b) for i in range(A_HEADS // 2)]
    return jnp.concatenate(qs, 1), jnp.concatenate(ks, 1)


def _tok(tm, w, off):
    return pl.BlockSpec((tm, w), lambda i: (i, off // w))


def _vec(w):
    return pl.BlockSpec((1, w), lambda i: (0, 0))


def _prep_fwd(P, qg, kg, rope, tm, name):
    T = P.shape[0]

    def body(q_ref, k_ref, v_ref, qg_ref, kg_ref, c_ref, sa_ref, sb_ref, qn_ref, kn_ref, vb_ref):
        qn, kn = _prep_fn(q_ref[...].astype(f32), k_ref[...].astype(f32), qg_ref[...], kg_ref[...], c_ref[...], sa_ref[...],
                          sb_ref[...])
        qn_ref[...] = qn.astype(bf16)
        kn_ref[...] = kn.astype(bf16)
        vb_ref[...] = v_ref[...].astype(bf16)

    return pl.pallas_call(
        body, name=name, grid=(T // tm,),
        in_specs=[_tok(tm, 512, 0), _tok(tm, 256, A_K), _tok(tm, 256, A_V), _vec(HD), _vec(HD),
                  _tok(tm, HD, 0), _tok(tm, HD, 0), _tok(tm, HD, 0)],
        out_specs=[_tok(tm, 512, 0), _tok(tm, 256, 0), _tok(tm, 256, 0)],
        out_shape=[jax.ShapeDtypeStruct((T, 512), bf16), jax.ShapeDtypeStruct((T, 256), bf16),
                   jax.ShapeDtypeStruct((T, 256), bf16)],
        compiler_params=_cparams(("parallel",)),
    )(P, P, P, qg, kg, *rope)


def _prep_bwd(P, dqn, dkn, dv, qg, kg, rope, tm, name):
    T = P.shape[0]
    nt = T // tm

    def body(q_ref, k_ref, dq_ref, dk_ref, dv_ref, qg_ref, kg_ref, c_ref, sa_ref, sb_ref, o_ref, og_ref):
        tabs = (c_ref[...], sa_ref[...], sb_ref[...])
        _, vjp = jax.vjp(lambda q, k, a, b: _prep_fn(q, k, a, b, *tabs), q_ref[...].astype(f32), k_ref[...].astype(f32),
                         qg_ref[...], kg_ref[...])
        dq, dk, dqg, dkg = vjp((dq_ref[...], dk_ref[...]))
        o_ref[:, 0:A_K] = dq.astype(bf16)
        o_ref[:, A_K:A_V] = dk.astype(bf16)
        o_ref[:, A_V:W_A] = dv_ref[...].astype(bf16)
        _partial_rows(og_ref, [dqg, dkg])

    return pl.pallas_call(
        body, name=name, grid=(nt,),
        in_specs=[_tok(tm, 512, 0), _tok(tm, 256, A_K), _tok(tm, 512, 0), _tok(tm, 256, 0), _tok(tm, 256, 0),
                  _vec(HD), _vec(HD), _tok(tm, HD, 0), _tok(tm, HD, 0), _tok(tm, HD, 0)],
        out_specs=[_tok(tm, W_A, 0), pl.BlockSpec((None, SUB, HD), lambda i: (i, 0, 0))],
        out_shape=[jax.ShapeDtypeStruct((T, W_A), bf16), jax.ShapeDtypeStruct((nt, SUB, HD), f32)],
        compiler_params=_cparams(("parallel",)),
    )(P, P, dqn, dkn, dv, qg, kg, *rope)


def _attn_fn(q, k, v, lim):
    s = mm_nt(q, k) * (HD ** -0.5)
    col = lax.broadcasted_iota(jnp.int32, s.shape, 1)
    s = jnp.where(col < lim, s, -1e30)
    m = lax.stop_gradient(jnp.max(s, -1, keepdims=True))
    e = jnp.exp(s - m)
    p = e * (1.0 / jnp.sum(e, -1, keepdims=True))
    return mm(p, v)


def _attn_fwd(qn, kn, vb, tc, tq, name):
    T = qn.shape[0]

    def body(q_ref, k_ref, v_ref, o_ref):
        lim = jnp.where(pl.program_id(1) * tq < tc, tc, T)
        o_ref[...] = _attn_fn(q_ref[...], k_ref[...], v_ref[...], lim)

    return pl.pallas_call(
        body, name=name, grid=(A_HEADS, T // tq),
        in_specs=[pl.BlockSpec((tq, HD), lambda h, i: (i, h)), pl.BlockSpec((T, HD), lambda h, i: (0, h // 2)),
                  pl.BlockSpec((T, HD), lambda h, i: (0, h // 2))],
        out_specs=pl.BlockSpec((tq, HD), lambda h, i: (i, h)),
        out_shape=jax.ShapeDtypeStruct((T, 512), f32),
        compiler_params=_cparams(("parallel", "parallel")),
    )(qn, kn, vb)


def _attn_bwd(qn, kn, vb, dya, tc, tq, name):
    T = qn.shape[0]

    def body(q_ref, k_ref, v_ref, g_ref, dq_ref, dk_ref, dv_ref):
        first = (pl.program_id(1) == 0) & (pl.program_id(2) == 0)
        lim = jnp.where(pl.program_id(2) * tq < tc, tc, T)
        _, vjp = jax.vjp(lambda q, k, v: _attn_fn(q, k, v, lim), q_ref[...].astype(f32), k_ref[...].astype(f32),
                         v_ref[...].astype(f32))
        dq, dk, dv = vjp(g_ref[...])
        dq_ref[...] = dq

        @pl.when(first)
        def _():
            dk_ref[...] = dk
            dv_ref[...] = dv

        @pl.when(jnp.logical_not(first))
        def _():
            dk_ref[...] += dk
            dv_ref[...] += dv

    qspec = pl.BlockSpec((tq, HD), lambda kv, g, i: (i, 2 * kv + g))
    kspec = pl.BlockSpec((T, HD), lambda kv, g, i: (0, kv))
    return pl.pallas_call(
        body, name=name, grid=(A_HEADS // 2, 2, T // tq),
        in_specs=[qspec, kspec, kspec, qspec], out_specs=[qspec, kspec, kspec],
        out_shape=[jax.ShapeDtypeStruct((T, 512), f32), jax.ShapeDtypeStruct((T, 256), f32),
                   jax.ShapeDtypeStruct((T, 256), f32)],
        compiler_params=_cparams(("parallel", "arbitrary", "arbitrary")),
    )(qn, kn, vb, dya)


def _conv_rows(tc, tl):
    return CONV_PAD + tc + CONV_PAD + tl + CONV_PAD


def _fill_pad(pad_ref, val, tc, tl):
    z = jnp.zeros((CONV_PAD, LANE), f32)
    pad_ref[0:CONV_PAD, :] = z
    pad_ref[CONV_PAD:CONV_PAD + tc, :] = val[0:tc]
    pad_ref[CONV_PAD + tc:2 * CONV_PAD + tc, :] = z
    pad_ref[2 * CONV_PAD + tc:2 * CONV_PAD + tc + tl, :] = val[tc:tc + tl]
    pad_ref[2 * CONV_PAD + tc + tl:3 * CONV_PAD + tc + tl, :] = z


def _conv_apply(pad_ref, w_ref, K, tc, tl, rc, emit, flip=False):
    half = K // 2
    for seg0, off, n in ((0, CONV_PAD, tc), (tc, 2 * CONV_PAD + tc, tl)):
        for r0 in range(0, n, rc):
            acc = None
            for k in range(K):
                sh = (half - k) if flip else (k - half)
                term = pad_ref[pl.ds(off + r0 + sh, rc), :] * w_ref[k:k + 1, :]
                acc = term if acc is None else acc + term
            emit(seg0 + r0, acc)


def _conv_wgrad(pad_ref, dy_ref, K, tc, tl, rc, dw_ref):
    half = K // 2
    for k in range(K):
        acc = jnp.zeros((1, LANE), f32)
        for seg0, off, n in ((0, CONV_PAD, tc), (tc, 2 * CONV_PAD + tc, tl)):
            for r0 in range(0, n, rc):
                acc = acc + jnp.sum(pad_ref[pl.ds(off + r0 + k - half, rc), :] * dy_ref[pl.ds(seg0 + r0, rc), :],
                                    axis=0, keepdims=True)
        dw_ref[k:k + 1, :] = acc


def _col(T, off):
    return pl.BlockSpec((T, LANE), lambda j: (0, off // LANE + j))


C_B, C_C, C_X, C_A, C_G = range(5)
N_SEC = 5


class _Sections:
    def __init__(self, refs):
        self.refs = refs

    def __getitem__(self, idx):
        rows, sec = idx
        return self.refs[sec][rows, :].astype(f32)

    def __setitem__(self, idx, val):
        rows, sec = idx
        self.refs[sec, rows, :] = val


def _sec_specs(T):
    return [pl.BlockSpec((T, LANE), functools.partial(lambda j, s: (0, s * (BRW // LANE) + j), s=s)) for s in range(N_SEC)]


def _conv_fwd(P, wb, wd, bd, tc, tl, rc, name):
    T = tc + tl

    def body(*refs):
        p_ref = _Sections(refs[:N_SEC])
        wb_ref, wd_ref, bd_ref, yb_ref, hh_ref, pad_ref = refs[N_SEC:]
        _fill_pad(pad_ref, p_ref[:, C_C] * p_ref[:, C_X], tc, tl)

        def emit_b(r0, y):
            yb_ref[pl.ds(r0, rc), :] = y * p_ref[pl.ds(r0, rc), C_B]

        _conv_apply(pad_ref, wb_ref, KB, tc, tl, rc, emit_b)
        _fill_pad(pad_ref, p_ref[:, C_A] * _sigmoid(p_ref[:, C_G]), tc, tl)

        def emit_d(r0, y):
            hh_ref[pl.ds(r0, rc), :] = y + bd_ref[...]

        _conv_apply(pad_ref, wd_ref, KD, tc, tl, rc, emit_d)

    return pl.pallas_call(
        body, name=name, grid=(BRW // LANE,),
        in_specs=_sec_specs(T) + [pl.BlockSpec((KB, LANE), lambda j: (0, j)), pl.BlockSpec((KD, LANE), lambda j: (0, j)),
                                  pl.BlockSpec((1, LANE), lambda j: (0, j))],
        out_specs=[_col(T, 0), _col(T, 0)],
        out_shape=[jax.ShapeDtypeStruct((T, BRW), f32), jax.ShapeDtypeStruct((T, BRW), f32)],
        scratch_shapes=[pltpu.VMEM((_conv_rows(tc, tl), LANE), f32)],
        compiler_params=_cparams(("parallel",)),
    )(*[P] * N_SEC, wb, wd, bd)


def _conv_bwd(P, dyb, dhh, wb, wd, tc, tl, rc, name):
    T = tc + tl

    def body(*refs):
        p_ref = _Sections(refs[:N_SEC])
        dyb_ref, dhh_ref, wb_ref, wd_ref, dp3_ref, dwb_ref, dwd_ref, dbd_ref, pad_ref, pad2_ref, tmp_ref = refs[N_SEC:]
        dp_ref = _Sections(dp3_ref)
        _fill_pad(pad_ref, p_ref[:, C_C] * p_ref[:, C_X], tc, tl)

        def emit_cv(r0, y):
            dp_ref[pl.ds(r0, rc), C_B] = (y * dyb_ref[pl.ds(r0, rc), :]).astype(bf16)

        _conv_apply(pad_ref, wb_ref, KB, tc, tl, rc, emit_cv)
        tmp_ref[...] = dyb_ref[...] * p_ref[:, C_B]
        _conv_wgrad(pad_ref, tmp_ref, KB, tc, tl, rc, dwb_ref)
        _fill_pad(pad2_ref, tmp_ref[...], tc, tl)

        def emit_ds(r0, y):
            dp_ref[pl.ds(r0, rc), C_C] = (y * p_ref[pl.ds(r0, rc), C_X]).astype(bf16)
            dp_ref[pl.ds(r0, rc), C_X] = (y * p_ref[pl.ds(r0, rc), C_C]).astype(bf16)

        _conv_apply(pad2_ref, wb_ref, KB, tc, tl, rc, emit_ds, flip=True)
        _fill_pad(pad_ref, p_ref[:, C_A] * _sigmoid(p_ref[:, C_G]), tc, tl)
        _conv_wgrad(pad_ref, dhh_ref, KD, tc, tl, rc, dwd_ref)
        dbd_ref[...] = jnp.sum(dhh_ref[...], axis=0, keepdims=True)
        _fill_pad(pad2_ref, dhh_ref[...], tc, tl)

        def emit_d2(r0, y):
            sg = _sigmoid(p_ref[pl.ds(r0, rc), C_G])
            a = p_ref[pl.ds(r0, rc), C_A]
            dp_ref[pl.ds(r0, rc), C_A] = (y * sg).astype(bf16)
            dp_ref[pl.ds(r0, rc), C_G] = (y * a * sg * (1.0 - sg)).astype(bf16)

        _conv_apply(pad2_ref, wd_ref, KD, tc, tl, rc, emit_d2, flip=True)

    return pl.pallas_call(
        body, name=name, grid=(BRW // LANE,),
        in_specs=_sec_specs(T) + [_col(T, 0), _col(T, 0),
                                  pl.BlockSpec((KB, LANE), lambda j: (0, j)), pl.BlockSpec((KD, LANE), lambda j: (0, j))],
        out_specs=[pl.BlockSpec((N_SEC, T, LANE), lambda j: (0, 0, j)), pl.BlockSpec((KB, LANE), lambda j: (0, j)),
                   pl.BlockSpec((KD, LANE), lambda j: (0, j)), pl.BlockSpec((1, LANE), lambda j: (0, j))],
        out_shape=[jax.ShapeDtypeStruct((N_SEC, T, BRW), bf16), jax.ShapeDtypeStruct((KB, BRW), f32),
                   jax.ShapeDtypeStruct((KD, BRW), f32), jax.ShapeDtypeStruct((1, BRW), f32)],
        scratch_shapes=[pltpu.VMEM((_conv_rows(tc, tl), LANE), f32), pltpu.VMEM((_conv_rows(tc, tl), LANE), f32),
                        pltpu.VMEM((T, LANE), f32)],
        compiler_params=_cparams(("parallel",)),
    )(*[P] * N_SEC, dyb, dhh, wb, wd)


def _gla_chunk(q, k, v, r, w2, b2, st, isfwd):
    z = mm(r, w2) + b2
    g = jax.nn.log_sigmoid(z[:, 0:C_KW] if isfwd else z[:, C_KW:2 * C_KW]) / C_TAU
    ri = lax.broadcasted_iota(jnp.int32, (CH, CH), 0)
    ci = lax.broadcasted_iota(jnp.int32, (CH, CH), 1)
    tri = ((ci <= ri) if isfwd else (ci >= ri)).astype(f32)
    cum = jnp.dot(tri, g, preferred_element_type=f32, precision=lax.Precision.HIGHEST)
    last = jnp.sum(g, axis=0, keepdims=True)
    q = q * (C_KW // C_HEADS) ** -0.5
    hv = lax.broadcasted_iota(jnp.int32, (BRW, C_KW), 0) // (BRW // C_HEADS)
    hk = lax.broadcasted_iota(jnp.int32, (BRW, C_KW), 1) // (C_KW // C_HEADS)
    st_new = st * jnp.exp(last) + jnp.where(hv == hk, mm_tn(v, k * jnp.exp(last - cum)), 0.0)
    o = mm_nt(q * jnp.exp(cum), st)
    rowi = lax.broadcasted_iota(jnp.int32, (CH, C_KW), 0)
    srow = lax.broadcasted_iota(jnp.int32, (C_HEADS * CH, C_KW), 0)
    slane = lax.broadcasted_iota(jnp.int32, (C_HEADS * CH, C_KW), 1)
    own_lanes = srow // CH == slane // (C_KW // C_HEADS)
    pos = lax.broadcasted_iota(jnp.int32, (C_HEADS * CH, CH), 0) % CH
    key = lax.broadcasted_iota(jnp.int32, (C_HEADS * CH, CH), 1)
    scores = jnp.zeros((C_HEADS * CH, CH), f32)
    for a in range(CH // GLA_SUB):
        idx = GLA_SUB * a - 1 if isfwd else GLA_SUB * (a + 1)
        ref = jnp.sum(jnp.where(rowi == idx, cum, 0.0), axis=0, keepdims=True)
        qa = q * jnp.exp(jnp.minimum(cum - ref, 0.0))
        ka = k * jnp.exp(jnp.minimum(ref - cum, GLA_CLAMP))
        s = mm_nt(jnp.where(own_lanes, jnp.concatenate([qa] * C_HEADS, axis=0), 0.0), ka)
        scores = scores + jnp.where(pos // GLA_SUB == a, s, 0.0)
    scores = jnp.where((key <= pos) if isfwd else (key >= pos), scores, 0.0)
    vw = BRW // C_HEADS
    o = o + jnp.concatenate([mm(scores[CH * hd:CH * (hd + 1)], v[:, vw * hd:vw * (hd + 1)]) for hd in range(C_HEADS)],
                            axis=1)
    return o, st_new


def _gla_chunk_of(d, n, nc, nch):
    back = jnp.where(n < nc, nc - 1 - n, nch - 1 - (n - nc))
    return jnp.where(d == 0, n, back)


def _gla_fwd(P, w2, b2, tc, name):
    T = P.shape[0]
    nch, nc = T // CH, tc // CH

    back = lambda n: _gla_chunk_of(1, n, nc, nch)

    def body(pf_ref, pb_ref, w_ref, b_ref, of_ref, ob_ref, ssf_ref, ssb_ref, stf_ref, stb_ref):
        @pl.when(pl.program_id(0) == 0)
        def _():
            stf_ref[...] = jnp.zeros_like(stf_ref)
            stb_ref[...] = jnp.zeros_like(stb_ref)

        for p_ref, o_ref, ss_ref, st_ref, isfwd in ((pf_ref, of_ref, ssf_ref, stf_ref, True),
                                                    (pb_ref, ob_ref, ssb_ref, stb_ref, False)):
            st = st_ref[...]
            ss_ref[...] = st
            p = p_ref[...].astype(f32)
            o, st_new = _gla_chunk(p[:, 0:G_K], p[:, G_K:G_V], p[:, G_V:G_R], p[:, G_R:W_G], w_ref[...], b_ref[...], st, isfwd)
            o_ref[...] = o
            st_ref[...] = st_new

    sd = jax.ShapeDtypeStruct
    return pl.pallas_call(
        body, name=name, grid=(nch,),
        in_specs=[pl.BlockSpec((CH, W_G), lambda n: (n, 0)), pl.BlockSpec((CH, W_G), lambda n: (back(n), 0)),
                  pl.BlockSpec((LANE, 512), lambda n: (0, 0)), pl.BlockSpec((1, 512), lambda n: (0, 0))],
        out_specs=[pl.BlockSpec((CH, BRW), lambda n: (n, 0)), pl.BlockSpec((CH, BRW), lambda n: (back(n), 0)),
                   pl.BlockSpec((None, BRW, C_KW), lambda n: (n, 0, 0)), pl.BlockSpec((None, BRW, C_KW), lambda n: (n, 0, 0))],
        out_shape=[sd((T, BRW), f32), sd((T, BRW), f32), sd((nch, BRW, C_KW), f32), sd((nch, BRW, C_KW), f32)],
        scratch_shapes=[pltpu.VMEM((BRW, C_KW), f32), pltpu.VMEM((BRW, C_KW), f32)],
        compiler_params=_cparams(("arbitrary",)),
    )(P, P, w2, b2)


def _gla_bwd(P, w2, b2, ssave, doc, tc, name):
    T = P.shape[0]
    nch, nc = T // CH, tc // CH

    fwd_chunk = lambda m: nch - 1 - m
    back_chunk = lambda m: _gla_chunk_of(1, nch - 1 - m, nc, nch)

    def body(pf_ref, pb_ref, w_ref, b_ref, ssf_ref, ssb_ref, gf_ref, gb_ref, dpf_ref, dpb_ref, dw_ref, db_ref,
             dstf_ref, dstb_ref):
        m = pl.program_id(0)

        @pl.when(m == 0)
        def _():
            dstf_ref[...] = jnp.zeros_like(dstf_ref)
            dstb_ref[...] = jnp.zeros_like(dstb_ref)

        dw_sum, db_sum = None, None
        for p_ref, ss_ref, g_ref, dp_ref, dst_ref, isfwd in ((pf_ref, ssf_ref, gf_ref, dpf_ref, dstf_ref, True),
                                                             (pb_ref, ssb_ref, gb_ref, dpb_ref, dstb_ref, False)):
            p = p_ref[...].astype(f32)
            _, vjp = jax.vjp(lambda q, k, v, r, w, b, st: _gla_chunk(q, k, v, r, w, b, st, isfwd),
                             p[:, 0:G_K], p[:, G_K:G_V], p[:, G_V:G_R], p[:, G_R:W_G], w_ref[...], b_ref[...], ss_ref[...])
            dq, dk, dv, dr, dw, db, dst = vjp((g_ref[...], dst_ref[...]))
            dp_ref[:, 0:G_K] = dq
            dp_ref[:, G_K:G_V] = dk
            dp_ref[:, G_V:G_R] = dv
            dp_ref[:, G_R:W_G] = dr
            dst_ref[...] = dst
            dw_sum = dw if dw_sum is None else dw_sum + dw
            db_sum = db if db_sum is None else db_sum + db

        @pl.when(m == 0)
        def _():
            dw_ref[...] = dw_sum
            _partial_rows(db_ref, [db_sum])

        @pl.when(m > 0)
        def _():
            dw_ref[...] += dw_sum
            db_ref[0:1, :] += db_sum

    ssf, ssb = ssave
    chunk_f = lambda w: pl.BlockSpec((CH, w), lambda m: (fwd_chunk(m), 0))
    chunk_b = lambda w: pl.BlockSpec((CH, w), lambda m: (back_chunk(m), 0))
    state = pl.BlockSpec((None, BRW, C_KW), lambda m: (nch - 1 - m, 0, 0))
    sd = jax.ShapeDtypeStruct
    return pl.pallas_call(
        body, name=name, grid=(nch,),
        in_specs=[chunk_f(W_G), chunk_b(W_G), pl.BlockSpec((LANE, 512), lambda m: (0, 0)), pl.BlockSpec((1, 512), lambda m: (0, 0)),
                  state, state, chunk_f(BRW), chunk_b(BRW)],
        out_specs=[chunk_f(W_G), chunk_b(W_G), pl.BlockSpec((LANE, 512), lambda m: (0, 0)), pl.BlockSpec((SUB, 512), lambda m: (0, 0))],
        out_shape=[sd((T, W_G), f32), sd((T, W_G), f32), sd((LANE, 512), f32), sd((SUB, 512), f32)],
        scratch_shapes=[pltpu.VMEM((BRW, C_KW), f32), pltpu.VMEM((BRW, C_KW), f32)],
        compiler_params=_cparams(("arbitrary",)),
    )(P, P, w2, b2, ssf, ssb, doc, doc)


def _sum_dirs(a, b, tm, name):
    T, W = a.shape

    def body(a_ref, b_ref, o_ref):
        o_ref[...] = (a_ref[...] + b_ref[...]).astype(bf16)

    spec = pl.BlockSpec((tm, W), lambda i: (i, 0))
    return pl.pallas_call(
        body, name=name, grid=(T // tm,), in_specs=[spec, spec], out_specs=spec,
        out_shape=jax.ShapeDtypeStruct((T, W), bf16),
        compiler_params=_cparams(("parallel",)),
    )(a, b)


def _merge_fn(h, m_l, m_c, isctx, ya, ga, yb, gb, of, ob, gc, hh, gd, mg, es, ey, cn, dng, dnb, lg, lb, wbr, wout):
    oc = of + ob
    yc = jnp.concatenate([_rms(oc[:, HD * i:HD * (i + 1)], cn[:, HD * i:HD * (i + 1)]) for i in range(C_HEADS)], 1)
    brs = [ya * _silu(ga), yb * _silu(gb), yc * _silu(gc), _silu(_ln(hh) * dng + dnb) * _silu(gd)]
    acc = None
    for i in range(4):
        t = _sigmoid(mg[:, D * i:D * (i + 1)]) * (mm(brs[i], wbr[i]) + es[i])
        acc = t if acc is None else acc + t
    y = mm(acc, wout) + ey
    gate = jnp.where(isctx, m_c[:, 2 * D:3 * D], m_l[:, 2 * D:3 * D])
    hn = _ln(ALPHA * h + gate * y) * lg + lb
    return hn, (brs, acc)


def _merge_specs(tm):
    t = lambda w, off=0: _tok(tm, w, off)
    return [t(D), pl.BlockSpec((SUB, 3 * D), lambda i: (0, 0)),
            t(BRW), t(BRW, M_GA), t(BRW), t(BRW, M_GB),
            t(BRW), t(BRW),
            t(BRW, M_GC), t(BRW), t(BRW, M_GD), t(4 * D, 0),
            _vec(BRW), _vec(BRW), _vec(BRW), _vec(D), _vec(D),
            pl.BlockSpec((4, BRW, D), lambda i: (0, 0, 0)), pl.BlockSpec((D, D), lambda i: (0, 0))]


def _merge_fwd(h, modv_l, ya, yb, o2, hh, P, cn, dng, dnb, lg, lb, wbr, wout, tc, tm, name):
    T = h.shape[0]

    def body(h_ref, m_ref, ya_ref, ga_ref, yb_ref, gb_ref, of_ref, ob_ref, gc_ref, hh_ref, gd_ref, mg_ref,
             cn_ref, dng_ref, dnb_ref, lg_ref, lb_ref, wbr_ref, wout_ref, o_ref):
        isctx = _row_ids(pl.program_id(0), tm) < tc
        zero = jnp.zeros((tm, D), f32)
        up = lambda r: r[...].astype(f32)
        hn, _ = _merge_fn(h_ref[...], m_ref[0:1, :], m_ref[1:2, :], isctx, ya_ref[...], up(ga_ref), yb_ref[...],
                          up(gb_ref), of_ref[...], ob_ref[...], up(gc_ref), hh_ref[...], up(gd_ref), up(mg_ref),
                          [zero] * 4, zero, cn_ref[...], dng_ref[...], dnb_ref[...], lg_ref[...], lb_ref[...],
                          [wbr_ref[i] for i in range(4)], wout_ref[...])
        o_ref[...] = hn

    return pl.pallas_call(
        body, name=name, grid=(T // tm,),
        in_specs=_merge_specs(tm), out_specs=_tok(tm, D, 0),
        out_shape=jax.ShapeDtypeStruct((T, D), f32),
        compiler_params=_cparams(("parallel",)),
    )(h, modv_l, ya, P, yb, P, o2[0], o2[1], P, hh, P, P, cn, dng, dnb, lg, lb, wbr, wout)


def _merge_bwd(dhn, h, modv_l, ya, yb, o2, hh, P, cn, dng, dnb, lg, lb, wbr, wout, tc, tm, name):
    T = h.shape[0]
    nt = T // tm

    def body(g_ref, h_ref, m_ref, ya_ref, ga_ref, yb_ref, gb_ref, of_ref, ob_ref, gc_ref, hh_ref, gd_ref, mg_ref,
             cn_ref, dng_ref, dnb_ref, lg_ref, lb_ref, wbr_ref, wout_ref,
             dh_ref, dm_ref, dya_ref, dyb_ref, doc_ref, dhh_ref, dp_ref,
             br_ref, z_ref, acc_ref, dy_ref, dv5_ref, dvd_ref):
        isctx = _row_ids(pl.program_id(0), tm) < tc
        zero = jnp.zeros((tm, D), f32)
        wbr_v = [wbr_ref[i] for i in range(4)]
        wout_v = wout_ref[...]
        up = lambda r: r[...].astype(f32)

        def fn(h, ml, mc, ya, ga, yb, gb, oc, gc, hh, gd, mg, e0, e1, e2, e3, ey, cn, dng, dnb, lg, lb):
            return _merge_fn(h, ml, mc, isctx, ya, ga, yb, gb, oc, jnp.zeros_like(oc), gc, hh, gd, mg,
                             [e0, e1, e2, e3], ey, cn, dng, dnb, lg, lb, wbr_v, wout_v)

        _, vjp, (brs, acc) = jax.vjp(
            fn, h_ref[...], m_ref[0:1, :], m_ref[1:2, :], ya_ref[...], up(ga_ref), yb_ref[...], up(gb_ref),
            of_ref[...] + ob_ref[...], up(gc_ref), hh_ref[...], up(gd_ref), up(mg_ref), zero, zero, zero, zero, zero,
            cn_ref[...], dng_ref[...], dnb_ref[...], lg_ref[...], lb_ref[...], has_aux=True)
        (dh, dml, dmc, dya, dga, dyb, dgb, doc, dgc, dhh, dgd, dmg, z0, z1, z2, z3, dy,
         dcn, ddng, ddnb, dlg, dlb) = vjp(g_ref[...])
        dh_ref[...] = dh
        _partial_rows(dm_ref, [dml, dmc])
        dya_ref[...] = dya
        dyb_ref[...] = dyb
        doc_ref[...] = doc
        dhh_ref[...] = dhh
        dp_ref[:, 0:M_GA] = dmg.astype(bf16)
        dp_ref[:, M_GA:M_GB] = dga.astype(bf16)
        dp_ref[:, M_GB:M_GC] = dgb.astype(bf16)
        dp_ref[:, M_GC:M_GD] = dgc.astype(bf16)
        dp_ref[:, M_GD:W_M] = dgd.astype(bf16)
        for i, z in enumerate((z0, z1, z2, z3)):
            br_ref[i] = brs[i].astype(bf16)
            z_ref[i] = z.astype(bf16)
        acc_ref[...] = acc.astype(bf16)
        dy_ref[...] = dy.astype(bf16)
        _partial_rows(dv5_ref, [dcn, ddng, ddnb])
        _partial_rows(dvd_ref, [dlg, dlb])

    t = lambda w: _tok(tm, w, 0)
    part = lambda w: pl.BlockSpec((None, SUB, w), lambda i: (i, 0, 0))
    sd = jax.ShapeDtypeStruct
    return pl.pallas_call(
        body, name=name, grid=(nt,),
        in_specs=[t(D)] + _merge_specs(tm),
        out_specs=[t(D), part(3 * D)] + [t(BRW)] * 4 + [t(W_M),
                   pl.BlockSpec((4, tm, BRW), lambda i: (0, i, 0)), pl.BlockSpec((4, tm, D), lambda i: (0, i, 0)),
                   t(D), t(D), part(BRW), part(D)],
        out_shape=[sd((T, D), f32), sd((nt, SUB, 3 * D), f32)] + [sd((T, BRW), f32)] * 4 + [sd((T, W_M), bf16),
                   sd((4, T, BRW), bf16), sd((4, T, D), bf16), sd((T, D), bf16), sd((T, D), bf16),
                   sd((nt, SUB, BRW), f32), sd((nt, SUB, D), f32)],
        compiler_params=_cparams(("parallel",)),
    )(dhn, h, modv_l, ya, P, yb, P, o2[0], o2[1], P, hh, P, P, cn, dng, dnb, lg, lb, wbr, wout)


def _loss_kernel(h, tgt, tc, tm, name):
    T = h.shape[0]
    nt = T // tm
    nct = tc // tm

    def body(h_ref, t_ref, d_ref, l_ref):
        i = pl.program_id(0)
        err = h_ref[...] - t_ref[...]
        lat = (i >= nct).astype(f32)
        d_ref[...] = err * (lat / D)
        l_ref[...] = jnp.zeros((SUB, LANE), f32) + lat * 0.5 * jnp.sum(err * err) / D

    return pl.pallas_call(
        body, name=name, grid=(nt,),
        in_specs=[pl.BlockSpec((tm, D), lambda i: (i, 0)),
                  pl.BlockSpec((tm, D), lambda i: (jnp.maximum(i - nct, 0), 0))],
        out_specs=[pl.BlockSpec((tm, D), lambda i: (i, 0)), pl.BlockSpec((None, SUB, LANE), lambda i: (i, 0, 0))],
        out_shape=[jax.ShapeDtypeStruct((T, D), f32), jax.ShapeDtypeStruct((nt, SUB, LANE), f32)],
        compiler_params=_cparams(("parallel",)),
    )(h, tgt)


def _rope_tables(tc, tl):
    t = jnp.arange(tl)
    inv = ROPE_THETA ** (-jnp.arange(0, HD // 2, 2, dtype=f32) / (HD // 2))
    ang = jnp.concatenate([(t // GRID_W).astype(f32)[:, None] * inv, (t % GRID_W).astype(f32)[:, None] * inv], -1)
    cos, sin = jnp.repeat(jnp.cos(ang), 2, axis=1), jnp.repeat(jnp.sin(ang), 2, axis=1)
    even = (jnp.arange(HD) % 2 == 0)[None, :]
    cos_f = jnp.concatenate([jnp.ones((tc, HD), f32), cos], 0)
    sin_a = jnp.concatenate([jnp.zeros((tc, HD), f32), jnp.where(even, -sin, 0.0)], 0)
    sin_b = jnp.concatenate([jnp.zeros((tc, HD), f32), jnp.where(even, 0.0, sin)], 0)
    return cos_f, sin_a, sin_b


N_CHIPS = 4
SHARD = N_IN // N_CHIPS


def _group_ranges():
    return dict(M=[(S_MG, 4 * D), (S_GA, BRW), (S_GB, BRW), (S_GC, BRW), (S_GD, BRW)], A=[(S_Q, W_A)],
                C=[(S_B, 3 * BRW), (S_DA, 2 * BRW)], G=[(S_CQ, 2 * C_KW + BRW), (S_R, 2 * C_RANK)])


def _group_weights(w4):
    out = {}
    for k, ranges in _group_ranges().items():
        parts = []
        for a, n in ranges:
            while n > 0:
                s, r = divmod(a, SHARD)
                m = min(n, SHARD - r)
                parts.append(w4[s, r:r + m])
                a, n = a + m, n - m
        if k == "G":
            parts.append(jnp.zeros((LANE - 2 * C_RANK, D), w4.dtype))
        out[k] = jnp.concatenate(parts, 0)
    return out


def _ungroup(g):
    secs = []
    for k, ranges in _group_ranges().items():
        off = 0
        for a, n in ranges:
            secs.append((a, g[k][off:off + n]))
            off += n
    return jnp.concatenate([v for _, v in sorted(secs, key=lambda t: t[0])], 0)


PROJ_TN = dict(M=2048, A=1024, C=1280, G=1152)
DU_TK = dict(M=2048, A=1024, C=BRW, G=1152)
DWP_TN = dict(M=768, A=1024, C=BRW, G=1152)


def _gate_weights(w2_l, gb_l):
    w = jnp.zeros((LANE, 2 * C_KW), f32)
    w = w.at[0:C_RANK, 0:C_KW].set(w2_l[0]).at[C_RANK:2 * C_RANK, C_KW:2 * C_KW].set(w2_l[1])
    return w, jnp.concatenate([gb_l[0], gb_l[1]])[None, :]


def _local_step(x1, c1, ctx1, tgt1, c_ctx, w_mod, b_mod, weights_of, q_norm, k_norm, b_conv, w2, gb, c_norm, d_conv_w,
                d_conv_b, d_norm_g, d_norm_b, grads_done, ln_g, ln_b, tm, token=None):
    tc, tl = ctx1.shape[0], x1.shape[0]
    T = tc + tl
    rc = min(256, tc)
    tmb = tm // 2
    tmm = 768 if T % 768 == 0 else tm
    rope = _rope_tables(tc, tl)
    cin = jnp.concatenate([c1, c_ctx[None, :], jnp.zeros((SUB - 2, D), f32)], 0)
    if token is not None:
        cin = cin + token[:, 0:1]
    modv = _mod_fwd(cin, w_mod, b_mod)
    modv = [modv[l] for l in range(DEPTH)]
    row = lambda v: v[None, :]

    h = jnp.concatenate([ctx1, x1], 0)
    saved, wp, w_br, w_out = [], [None] * DEPTH, [None] * DEPTH, [None] * DEPTH
    for l in range(DEPTH):
        wp[l], merge_weights = weights_of(l, h)
        u = _ln_fwd(h, modv[l], tc, tm, f"ln_fwd{l}")
        P = {k: _matmul(u, wp[l][k], "nt", tmm, PROJ_TN[k], D, f"proj{l}{k}", out_dtype=bf16) for k in GROUPS}
        qn, kn, vb = _prep_fwd(P["A"], row(q_norm[l]), row(k_norm[l]), rope, tm, f"prep_fwd{l}")
        ya = _attn_fwd(qn, kn, vb, tc, tm, f"attn_fwd{l}")
        yb, hh = _conv_fwd(P["C"], b_conv[l], d_conv_w[l], row(d_conv_b[l]), tc, tl, rc, f"conv_fwd{l}")
        w2p, b2p = _gate_weights(w2[l], gb[l])
        gla = _gla_fwd(P["G"], w2p, b2p, tc, f"gla_fwd{l}")
        o2, ssave = gla[:2], gla[2:]
        w_br[l], w_out[l] = merge_weights(o2[0])
        hn = _merge_fwd(h, modv[l], ya, yb, o2, hh, P["M"], row(c_norm[l]), row(d_norm_g[l]), row(d_norm_b[l]),
                        row(ln_g[l]), row(ln_b[l]), w_br[l], w_out[l], tc, tm, f"merge_fwd{l}")
        saved.append((h, u, P, qn, kn, vb, ya, yb, hh, o2, ssave, w2p, b2p))
        h = hn

    dh, lparts = _loss_kernel(h, tgt1, tc, tm, "loss")
    loss = jnp.sum(lparts[:, 0, 0])

    g = {k: [None] * DEPTH for k in ("wp", "q_norm", "k_norm", "b_conv", "w2", "gb", "c_norm", "d_conv_w", "d_conv_b",
                                     "d_norm_g", "d_norm_b", "w_br", "w_out", "ln_g", "ln_b", "modv")}
    for l in reversed(range(DEPTH)):
        h_in, u, P, qn, kn, vb, ya, yb, hh, o2, ssave, w2p, b2p = saved[l]
        dP = {}
        (dh_res, dm_mg, dya, dyb, doc, dhh, dP["M"], br, z, acc, dy, dv5, dvd) = _merge_bwd(
            dh, h_in, modv[l], ya, yb, o2, hh, P["M"], row(c_norm[l]), row(d_norm_g[l]), row(d_norm_b[l]),
            row(ln_g[l]), row(ln_b[l]), w_br[l], w_out[l], tc, tmb, f"merge_bwd{l}")
        g["w_br"][l] = _matmul_tn_batched(br, z, N_CHIPS, f"dwbr{l}")
        g["w_out"][l] = _matmul(acc, dy, "tn", D, D, T, f"dwout{l}")
        tk = grads_done(l, {k: g[k][l] for k in ("w_br", "w_out")})
        qg_l = row(q_norm[l]) if tk is None else row(q_norm[l]) + tk[0:1, :]
        v5 = jnp.sum(dv5, 0)
        g["c_norm"][l], g["d_norm_g"][l], g["d_norm_b"][l] = v5[0], v5[1], v5[2]
        vd = jnp.sum(dvd, 0)
        g["ln_g"][l], g["ln_b"][l] = vd[0], vd[1]
        dqn, dkn, dv = _attn_bwd(qn, kn, vb, dya, tc, tm, f"attn_bwd{l}")
        dP["A"], dqk = _prep_bwd(P["A"], dqn, dkn, dv, qg_l, row(k_norm[l]), rope, tm, f"prep_bwd{l}")
        dqk = jnp.sum(dqk, 0)
        g["q_norm"][l], g["k_norm"][l] = dqk[0], dqk[1]
        dP["C"], dwb, dwd, dbd = _conv_bwd(P["C"], dyb, dhh, b_conv[l], d_conv_w[l], tc, tl, rc, f"conv_bwd{l}")
        g["b_conv"][l], g["d_conv_w"][l], g["d_conv_b"][l] = dwb, dwd, dbd[0]
        dpf, dpb, dw2p, db2p = _gla_bwd(P["G"], w2p, b2p, ssave, doc, tc, f"gla_bwd{l}")
        dP["G"] = _sum_dirs(dpf, dpb, tm, f"gla_sum{l}")
        db2p = db2p[0]
        g["w2"][l] = jnp.stack([dw2p[0:C_RANK, 0:C_KW], dw2p[C_RANK:2 * C_RANK, C_KW:2 * C_KW]])
        g["gb"][l] = jnp.stack([db2p[0:C_KW], db2p[C_KW:2 * C_KW]])
        g["wp"][l] = {k: _matmul(dP[k], u, "tn", DWP_TN[k], D, T, f"dwp{l}{k}") for k in GROUPS}
        tk = grads_done(l, {"wp": g["wp"][l]})
        du = None
        for k in GROUPS:
            du = _matmul(dP[k], wp[l][k], "nn", tmm, D, DU_TK[k], f"du{l}{k}", add=du, after=tk if du is None else None)
        dh, dm_ln = _ln_bwd(du, h_in, dh_res, modv[l], tc, tm, f"ln_bwd{l}")
        g["modv"][l] = jnp.sum(dm_mg, 0) + jnp.sum(dm_ln, 0)

    dmodv = jnp.stack(g.pop("modv"))
    g["w_mod"], dcin = _mod_bwd(cin, w_mod, dmodv)
    g["b_mod"] = dmodv[:, 0, :] + dmodv[:, 1, :]
    g["c_ctx"] = jnp.sum(dcin, (0, 1))[1]
    return loss, dh[tc:], g


HALF_TL = 256


def _adamw(w, g, m, v, name, tr=128, after=None):
    L, R, C = w.shape
    if R % tr == 0:
        grid, spec = (L, R // tr), pl.BlockSpec((None, tr, C), lambda l, i: (l, i, 0))
    elif R * C * 4 <= (1 << 20):
        grid, spec = (L, 1), pl.BlockSpec((None, R, C), lambda l, i: (l, 0, 0))
    else:
        grid, spec = (L, C // HALF_TL), pl.BlockSpec((None, R, HALF_TL), lambda l, i: (l, 0, i))

    def body(w_ref, g_ref, m_ref, v_ref, *rest):
        d_ref, nm_ref, nv_ref = rest[-3:]
        gg = g_ref[...]
        nm = B1 * m_ref[...] + (1.0 - B1) * gg
        nv = B2 * v_ref[...] + (1.0 - B2) * (gg * gg)
        m_hat = nm / (1.0 - B1 ** STEP)
        v_hat = nv / (1.0 - B2 ** STEP)
        d_ref[...] = -LR * (m_hat / (jnp.sqrt(v_hat) + AEPS) + WD * w_ref[...])
        nm_ref[...] = nm
        nv_ref[...] = nv

    return pl.pallas_call(
        body, name=name, grid=grid, in_specs=[spec] * 4 + ([] if after is None else [pl.BlockSpec(memory_space=pl.ANY)]),
        out_specs=[spec] * 3, out_shape=[jax.ShapeDtypeStruct((L, R, C), f32)] * 3,
        compiler_params=_cparams(("parallel", "parallel")),
    )(w, g, m, v, *([] if after is None else [after]))


MESH = pl.DeviceIdType.MESH
ANY = pl.BlockSpec(memory_space=pl.ANY)
N_CHIPS = 4


def _place():
    x, y, c = lax.axis_index("x"), lax.axis_index("y"), lax.axis_index("c")
    chips = [(1 - x, y), (x, 1 - y), (1 - x, 1 - y)]
    return x, y, c, chips


def _half(ref, c, axis):
    n = ref.shape[axis] // 2
    last = axis in (-1, ref.ndim - 1)
    idx = [slice(None)] * ref.ndim
    idx[axis] = pl.ds(pl.multiple_of(c * n, LANE if last else SUB), n)
    return ref.at[tuple(idx)]


def _half_shape(shape, axis):
    s = list(shape)
    s[axis] //= 2
    return tuple(s)


def _all_gather(arrs, axes, name):
    n = len(arrs)

    def body(*refs):
        ins, outs = refs[:n], refs[n:2 * n]
        send, recv = refs[2 * n:]
        x, y, c, chips = _place()
        me, sib = 2 * x + y, (x, y, 1 - c)

        def copy(a, k, chip_idx, cc, to, src=None):
            blk = _half(outs[a].at[chip_idx], cc, axes[a])
            return pltpu.make_async_remote_copy(src_ref=blk if src is None else src, dst_ref=blk,
                                                send_sem=send.at[7 * a + k], recv_sem=recv.at[7 * a + k],
                                                device_id=to, device_id_type=MESH)

        own = [pltpu.make_async_remote_copy(src_ref=ins[a], dst_ref=outs[a].at[me], send_sem=send.at[7 * a + 6],
                                            recv_sem=recv.at[7 * a + 6], device_id=sib, device_id_type=MESH)
               for a in range(n)]
        first = own + [copy(a, j, me, c, (*chip, c), src=_half(ins[a], c, axes[a]))
                       for a in range(n) for j, chip in enumerate(chips)]
        for cp in first:
            cp.start()
        passed = []
        for a in range(n):
            for j, chip in enumerate(chips):
                k = 2 * chip[0] + chip[1]
                copy(a, j, k, c, sib).wait_recv()
                fwd = copy(a, 3 + j, k, c, sib)
                fwd.start()
                passed.append(fwd)
        for a in range(n):
            own[a].wait_recv()
            for j, chip in enumerate(chips):
                copy(a, 3 + j, 2 * chip[0] + chip[1], 1 - c, sib).wait_recv()
        for cp in first + passed:
            cp.wait_send()

    return pl.pallas_call(
        body, name=name, in_specs=[ANY] * n, out_specs=[ANY] * n,
        out_shape=[jax.ShapeDtypeStruct((N_CHIPS,) + a.shape, a.dtype) for a in arrs],
        scratch_shapes=[pltpu.SemaphoreType.DMA((7 * n,)), pltpu.SemaphoreType.DMA((7 * n,))],
    )(*arrs)


def _sibling_halves(arrs, axes, name):
    n = len(arrs)

    def body(*refs):
        ins, outs = refs[:n], refs[n:2 * n]
        send, recv = refs[2 * n:]
        x, y, c, _ = _place()
        cps = [pltpu.make_async_remote_copy(src_ref=_half(ins[a], 1 - c, axes[a] + 1), dst_ref=outs[a], send_sem=send.at[a],
                                            recv_sem=recv.at[a], device_id=(x, y, 1 - c), device_id_type=MESH)
               for a in range(n)]
        for cp in cps:
            cp.start()
        for cp in cps:
            cp.wait()

    return pl.pallas_call(
        body, name=name, in_specs=[ANY] * n, out_specs=[ANY] * n,
        out_shape=[jax.ShapeDtypeStruct(_half_shape(a.shape, axes[i] + 1), a.dtype) for i, a in enumerate(arrs)],
        scratch_shapes=[pltpu.SemaphoreType.DMA((n,)), pltpu.SemaphoreType.DMA((n,))],
    )(*arrs)


def _add_half(gfull, land, cidx, axis, name, tr=128, out_dtype=bf16):
    _, hr, hc = land.shape
    if axis == 0:
        tr = min(tr, hr)
        nb, blk = hr // tr, (None, tr, hc)
        g_spec = pl.BlockSpec(blk, lambda s, i, cr: (s, cr[0] * nb + i, 0))
        l_spec = pl.BlockSpec(blk, lambda s, i, cr: (s, i, 0))
    else:
        nb, blk = hc // HALF_TL, (None, hr, HALF_TL)
        g_spec = pl.BlockSpec(blk, lambda s, i, cr: (s, 0, cr[0] * nb + i))
        l_spec = pl.BlockSpec(blk, lambda s, i, cr: (s, 0, i))

    def body(c_ref, g_ref, l_ref, o_ref):
        o_ref[...] = (g_ref[...].astype(f32) + l_ref[...].astype(f32)).astype(o_ref.dtype)

    return pl.pallas_call(
        body, name=name,
        grid_spec=pltpu.PrefetchScalarGridSpec(
            num_scalar_prefetch=1, grid=(N_CHIPS, nb), in_specs=[g_spec, l_spec], out_specs=l_spec),
        out_shape=jax.ShapeDtypeStruct((N_CHIPS, hr, hc), out_dtype),
        compiler_params=_cparams(("parallel", "parallel")),
    )(cidx, gfull, land)


def _chip_exchange(arrs, name):
    n = len(arrs)

    def body(*refs):
        ins, outs = refs[:n], refs[n:2 * n]
        send, recv = refs[2 * n:]
        x, y, c, chips = _place()
        me = 2 * x + y
        cps = []
        for a in range(n):
            for j, chip in enumerate(chips):
                k = 2 * chip[0] + chip[1]
                cps.append((pltpu.make_async_remote_copy(
                    src_ref=ins[a].at[k], dst_ref=outs[a].at[me], send_sem=send.at[3 * a + j], recv_sem=recv.at[3 * a + j],
                    device_id=(*chip, c), device_id_type=MESH), a, j, k))
        for cp, *_ in cps:
            cp.start()
        for cp, a, j, k in cps:
            pltpu.make_async_remote_copy(src_ref=ins[a].at[k], dst_ref=outs[a].at[k], send_sem=send.at[3 * a + j],
                                         recv_sem=recv.at[3 * a + j], device_id=(x, y, c), device_id_type=MESH).wait_recv()
        for cp, *_ in cps:
            cp.wait_send()

    return pl.pallas_call(
        body, name=name, in_specs=[ANY] * n, out_specs=[ANY] * n,
        out_shape=[jax.ShapeDtypeStruct(a.shape, a.dtype) for a in arrs],
        scratch_shapes=[pltpu.SemaphoreType.DMA((3 * n,)), pltpu.SemaphoreType.DMA((3 * n,))],
    )(*arrs)


def _sum_chips(land, own, place, axis, layer, into, name, tr=128):
    _, hr, hc = land.shape
    fresh = not hasattr(into, "dtype")
    shape = tuple(into) if fresh else into.shape
    if axis == 0:
        tr = min(tr, hr)
        nb, blk = hr // tr, (tr, hc)
        l_map, m_map = (lambda i, p: (0, i, 0)), (lambda i, p: (p[0], i, 0))
        o_map = lambda i, p: (layer, p[1] * nb + i, 0)
    else:
        nb, blk = hc // HALF_TL, (hr, HALF_TL)
        l_map, m_map = (lambda i, p: (0, 0, i)), (lambda i, p: (p[0], 0, i))
        o_map = lambda i, p: (layer, 0, p[1] * nb + i)

    def body(p_ref, l_ref, o_ref, *rest):
        me = p_ref[0]
        mine = o_ref[...].astype(f32)
        acc = None
        for k in range(N_CHIPS):
            t = jnp.where(me == k, mine, l_ref[k].astype(f32))
            acc = t if acc is None else acc + t
        rest[-1][...] = acc

    return pl.pallas_call(
        body, name=name,
        grid_spec=pltpu.PrefetchScalarGridSpec(
            num_scalar_prefetch=1, grid=(nb,),
            in_specs=[pl.BlockSpec((N_CHIPS,) + blk, l_map), pl.BlockSpec((None,) + blk, m_map)] + ([] if fresh else [ANY]),
            out_specs=pl.BlockSpec((None,) + blk, o_map)),
        out_shape=jax.ShapeDtypeStruct(shape, f32),
        input_output_aliases={} if fresh else {3: 0},
        compiler_params=_cparams(("parallel",)),
    )(place, land, own, *([] if fresh else [into]))


def _sibling_fill(arrs, axes, name):
    n = len(arrs)

    def body(*refs):
        outs = refs[n:2 * n]
        send, recv = refs[2 * n:]
        x, y, c, _ = _place()
        cps = [pltpu.make_async_remote_copy(src_ref=_half(outs[a], c, axes[a] + 1), dst_ref=_half(outs[a], c, axes[a] + 1),
                                            send_sem=send.at[a], recv_sem=recv.at[a], device_id=(x, y, 1 - c),
                                            device_id_type=MESH) for a in range(n)]
        for cp in cps:
            cp.start()
        for a in range(n):
            blk = _half(outs[a], 1 - c, axes[a] + 1)
            pltpu.make_async_remote_copy(src_ref=blk, dst_ref=blk, send_sem=send.at[a], recv_sem=recv.at[a],
                                         device_id=(x, y, 1 - c), device_id_type=MESH).wait_recv()
        for cp in cps:
            cp.wait_send()

    return pl.pallas_call(
        body, name=name, in_specs=[ANY] * n, out_specs=[ANY] * n,
        out_shape=[jax.ShapeDtypeStruct(a.shape, a.dtype) for a in arrs],
        input_output_aliases={a: a for a in range(n)},
        scratch_shapes=[pltpu.SemaphoreType.DMA((n,)), pltpu.SemaphoreType.DMA((n,))],
    )(*arrs)


HBM = pl.BlockSpec(memory_space=pltpu.HBM)
SEM = pl.BlockSpec(memory_space=pltpu.SEMAPHORE)
EFFECT = pltpu.SideEffectType.DATAFLOW_SIDE_EFFECTING
PEERS = 4


def _split_copies(srcs, lands, send, recv, gather):
    x, y, c, chips = _place()
    me = 2 * x + y
    peers = [((*chip, c), 2 * chip[0] + chip[1]) for chip in chips] + ([((x, y, 1 - c), me)] if gather else [])
    out = []
    for a in range(len(srcs)):
        for j, (dev, k) in enumerate(peers):
            src = srcs[a] if gather else srcs[a].at[k]
            sems = dict(send_sem=send.at[PEERS * a + j], recv_sem=recv.at[PEERS * a + j], device_id=dev, device_id_type=MESH)
            out.append((pltpu.make_async_remote_copy(src_ref=src, dst_ref=lands[a].at[me], **sems),
                        pltpu.make_async_remote_copy(src_ref=src, dst_ref=lands[a].at[k], **sems)))
    return out


def _split_start(srcs, gather, after, name):
    n = len(srcs)
    lands = [lax.empty(((N_CHIPS,) + s.shape) if gather else s.shape, s.dtype) for s in srcs]

    def body(*refs):
        send, recv = refs[2 * n + 1], refs[2 * n + 2]
        for start, _ in _split_copies(refs[:n], refs[n:2 * n], send, recv, gather):
            start.start()
        refs[-1][...] = jnp.zeros_like(refs[-1])

    sems = pltpu.SemaphoreType.DMA((PEERS * n,))
    hbm = lambda a: pltpu.with_memory_space_constraint(a, pltpu.HBM)
    out = pl.pallas_call(
        body, name=name,
        out_shape=(sems, sems, *[pltpu.HBM(a.shape, a.dtype) for a in srcs + lands], jax.ShapeDtypeStruct((SUB, LANE), f32)),
        in_specs=[HBM] * (2 * n) + [ANY], out_specs=(SEM, SEM, *[HBM] * (2 * n), pl.BlockSpec(memory_space=pltpu.VMEM)),
        input_output_aliases={i: 2 + i for i in range(2 * n)},
        compiler_params=pltpu.CompilerParams(has_side_effects=EFFECT),
    )(*[hbm(a) for a in srcs + lands], after)
    return out[0], out[1], list(out[2:2 + n]), list(out[2 + n:2 + 2 * n]), out[-1]


def _split_wait(send, recv, srcs, lands, gather, after, name):
    n = len(srcs)

    def body(*refs):
        for start, arrival in _split_copies(refs[:n], refs[n:2 * n], refs[2 * n], refs[2 * n + 1], gather):
            start.wait_send()
            arrival.wait_recv()

    out = pl.pallas_call(
        body, name=name, out_shape=[pltpu.HBM(a.shape, a.dtype) for a in srcs + lands],
        in_specs=[HBM] * (2 * n) + [SEM, SEM, ANY], out_specs=[HBM] * (2 * n),
        input_output_aliases={i: i for i in range(2 * n)},
        compiler_params=pltpu.CompilerParams(has_side_effects=EFFECT),
    )(*srcs, *lands, send, recv, after)
    return list(out[:n]), list(out[n:])


N_DEV = 8


def _all_reduce_small(v, name):
    R = v.shape[0]

    def body(v_ref, o_ref, land_ref, send, recv):
        x, y, c, _ = _place()
        me = 4 * x + 2 * y + c
        land_ref[me] = v_ref[...]
        cps = []
        for m in range(1, N_DEV):
            px, py, pc = [(1 - q) if (m >> s) & 1 else q for q, s in ((x, 2), (y, 1), (c, 0))]
            cps.append((pltpu.make_async_remote_copy(src_ref=v_ref, dst_ref=land_ref.at[me], send_sem=send.at[m - 1],
                                                     recv_sem=recv.at[m - 1], device_id=(px, py, pc), device_id_type=MESH),
                        4 * px + 2 * py + pc, m))
        for cp, *_ in cps:
            cp.start()
        for cp, peer, m in cps:
            pltpu.make_async_remote_copy(src_ref=v_ref, dst_ref=land_ref.at[peer], send_sem=send.at[m - 1],
                                         recv_sem=recv.at[m - 1], device_id=(x, y, c), device_id_type=MESH).wait_recv()
        for cp, *_ in cps:
            cp.wait_send()
        acc = land_ref[0]
        for k in range(1, N_DEV):
            acc = acc + land_ref[k]
        o_ref[...] = acc

    vm = pl.BlockSpec(memory_space=pltpu.VMEM)
    return pl.pallas_call(
        body, name=name, in_specs=[vm], out_specs=vm, out_shape=jax.ShapeDtypeStruct(v.shape, f32),
        scratch_shapes=[pltpu.VMEM((N_DEV, R, LANE), f32), pltpu.SemaphoreType.DMA((N_DEV - 1,)),
                        pltpu.SemaphoreType.DMA((N_DEV - 1,))],
        compiler_params=pltpu.CompilerParams(vmem_limit_bytes=VMEM_LIMIT),
    )(v)


def _pack_small(arrs, mult=2 * SUB):
    flat = jnp.concatenate([a.reshape(-1) for a in arrs])
    rows = -(-flat.shape[0] // (LANE * mult)) * mult
    return jnp.pad(flat, (0, rows * LANE - flat.shape[0])).reshape(rows, LANE)


def _unpack_small(vec, shapes):
    flat, out, o = vec.reshape(-1), [], 0
    for s in shapes:
        n = int(np.prod(s))
        out.append(flat[o:o + n].reshape(s))
        o += n
    return out


REPL_SMALL = ("c_ctx", "b_mod", "q_norm", "k_norm", "c_norm", "d_conv_b", "d_norm_g", "d_norm_b", "ln_g", "ln_b")
SHARD_SMALL = ("b_conv", "c_gate_w2", "c_gate_b", "d_conv_w")
BIG = ("w_mod", "w_in", "w_br", "w_out")
ORDER = ("c_ctx", "w_mod", "b_mod", "w_in", "q_norm", "k_norm", "b_conv", "c_gate_w2", "c_gate_b", "c_norm", "d_conv_w",
         "d_conv_b", "d_norm_g", "d_norm_b", "w_br", "w_out", "ln_g", "ln_b")


def _unshard_last(g4, shard_shape):
    g = g4.reshape((N_CHIPS,) + tuple(shard_shape))
    g = jnp.moveaxis(g, 0, -2)
    return g.reshape(tuple(shard_shape[:-1]) + (N_CHIPS * shard_shape[-1],))


def _pieces_last(full):
    w = full.shape[-1] // N_CHIPS
    g = full.reshape(full.shape[:-1] + (N_CHIPS, w))
    return jnp.moveaxis(g, -2, 0).reshape(N_CHIPS, -1, w)


def kernel(x, c, ctx, c_ctx, w_mod, b_mod, w_in, q_norm, k_norm, b_conv, c_gate_w2, c_gate_b, c_norm, d_conv_w, d_conv_b, d_norm_g, d_norm_b, w_br, w_out, ln_g, ln_b, loss_target, m_c_ctx, m_w_mod, m_b_mod, m_w_in, m_q_norm, m_k_norm, m_b_conv, m_c_gate_w2, m_c_gate_b, m_c_norm, m_d_conv_w, m_d_conv_b, m_d_norm_g, m_d_norm_b, m_w_br, m_w_out, m_ln_g, m_ln_b, v_c_ctx, v_w_mod, v_b_mod, v_w_in, v_q_norm, v_k_norm, v_b_conv, v_c_gate_w2, v_c_gate_b, v_c_norm, v_d_conv_w, v_d_conv_b, v_d_norm_g, v_d_norm_b, v_w_br, v_w_out, v_ln_g, v_ln_b):
    W = dict(c_ctx=c_ctx, w_mod=w_mod, b_mod=b_mod, w_in=w_in, q_norm=q_norm, k_norm=k_norm, b_conv=b_conv,
             c_gate_w2=c_gate_w2, c_gate_b=c_gate_b, c_norm=c_norm, d_conv_w=d_conv_w, d_conv_b=d_conv_b,
             d_norm_g=d_norm_g, d_norm_b=d_norm_b, w_br=w_br, w_out=w_out, ln_g=ln_g, ln_b=ln_b)
    M = dict(c_ctx=m_c_ctx, w_mod=m_w_mod, b_mod=m_b_mod, w_in=m_w_in, q_norm=m_q_norm, k_norm=m_k_norm, b_conv=m_b_conv,
             c_gate_w2=m_c_gate_w2, c_gate_b=m_c_gate_b, c_norm=m_c_norm, d_conv_w=m_d_conv_w, d_conv_b=m_d_conv_b,
             d_norm_g=m_d_norm_g, d_norm_b=m_d_norm_b, w_br=m_w_br, w_out=m_w_out, ln_g=m_ln_g, ln_b=m_ln_b)
    V = dict(c_ctx=v_c_ctx, w_mod=v_w_mod, b_mod=v_b_mod, w_in=v_w_in, q_norm=v_q_norm, k_norm=v_k_norm, b_conv=v_b_conv,
             c_gate_w2=v_c_gate_w2, c_gate_b=v_c_gate_b, c_norm=v_c_norm, d_conv_w=v_d_conv_w, d_conv_b=v_d_conv_b,
             d_norm_g=v_d_norm_g, d_norm_b=v_d_norm_b, w_br=v_w_br, w_out=v_w_out, ln_g=v_ln_g, ln_b=v_ln_b)
    chip = 2 * lax.axis_index("x") + lax.axis_index("y")
    cidx = lax.axis_index("c").astype(jnp.int32).reshape(1)

    place = jnp.stack([chip, lax.axis_index("c")]).astype(jnp.int32)

    AXIS = dict(w_in=1, w_mod=0, w_br=0, w_out=0)
    ex = dict(w_in=lambda a: jnp.swapaxes(a, 1, 2), w_mod=lambda a: a.reshape(1, DEPTH * D, -1),
              w_br=lambda a: a.reshape(DEPTH, 4 * BRW, -1), w_out=lambda a: a)
    Wx, Mx, Vx = ({k: ex[k](P_[k]) for k in BIG} for P_ in (W, M, V))

    LAYER, MERGE = ("w_in", "w_br", "w_out"), ("w_br", "w_out")
    small_shard = _pack_small([W[k] for k in SHARD_SMALL])
    keys0 = ("w_in", "w_mod")
    got = _all_gather([Wx[k][0].astype(bf16) for k in keys0] + [small_shard], [AXIS[k] for k in keys0] + [0], "all_gather0")
    smalls = [_unpack_small(got[-1][s], [W[k].shape for k in SHARD_SMALL]) for s in range(N_CHIPS)]
    full = {k: jnp.concatenate([smalls[s][i] for s in range(N_CHIPS)], axis=-1) for i, k in enumerate(SHARD_SMALL)}
    wmod = got[1].reshape(N_CHIPS, DEPTH, D, 3 * D // N_CHIPS)
    ag0b = _split_start([Wx[k][0].astype(bf16) for k in MERGE], True, got[0], "all_gather0b_start")
    ag1 = _split_start([Wx[k][1].astype(bf16) for k in LAYER], True, ag0b[4], "all_gather1_start")

    def merge_form(w_br4, w_out4):
        return jnp.moveaxis(w_br4.reshape(N_CHIPS, 4, BRW, D // N_CHIPS), 0, 2).reshape(4, BRW, D), w_out4.reshape(D, D)

    def weights_of(l, h):
        if l == 0:
            return _group_weights(got[0]), lambda after: merge_form(*_split_wait(*ag0b[:4], True, after, "all_gather0b_wait")[1])
        g3 = _split_wait(*ag1[:4], True, h, "all_gather1_wait")[1]
        return _group_weights(g3[0]), lambda after: merge_form(g3[1], g3[2])

    red = {k: Wx[k].shape for k in BIG}
    flights, held = {}, {}

    def launch(tag, l, pieces, after=None):
        keys = list(pieces)
        land_a = _sibling_halves([pieces[k] for k in keys], [AXIS[k] for k in keys], f"rs_sibling_halves{tag}")
        pair = [_add_half(pieces[k], la, cidx, AXIS[k], f"rs_pair_sum{tag}_{k}") for k, la in zip(keys, land_a)]
        after = jnp.zeros((SUB, LANE), f32) if after is None else after
        flights[tag] = (l, keys, _split_start(pair, False, after, f"rs_chip_exchange{tag}_start"))
        return flights[tag][2][4]

    def land(tag, after):
        l, keys, flight = flights.pop(tag)
        pair, land_b = _split_wait(*flight[:4], False, after, f"rs_chip_exchange{tag}_wait")
        for k, lb, pr in zip(keys, land_b, pair):
            red[k] = _sum_chips(lb, pr, place, AXIS[k], l, red[k], f"rs_chip_sum{tag}_{k}")

    def grads_done(l, gl):
        if "wp" in gl:
            pieces = dict(w_in=_ungroup(gl["wp"]).astype(bf16).reshape(N_CHIPS, SHARD, D))
            return launch("0c", 0, pieces) if l == 0 else launch("1", 1, {**pieces, **held.pop(1)})
        pieces = dict(w_br=gl["w_br"].reshape(N_CHIPS, 4 * BRW, D // N_CHIPS), w_out=gl["w_out"].reshape(N_CHIPS, D // N_CHIPS, D))
        if l == 0:
            return launch("0b", 0, pieces)
        held[1] = pieces
        return None

    loss, gx, g = _local_step(
        x[0], c, ctx[0], loss_target[0], c_ctx, wmod, b_mod, weights_of, q_norm, k_norm, full["b_conv"],
        full["c_gate_w2"], full["c_gate_b"], c_norm, full["d_conv_w"], d_conv_b, d_norm_g, d_norm_b,
        grads_done, ln_g, ln_b, tm=256, token=ag1[4])
    g["c_gate_w2"], g["c_gate_b"] = g.pop("w2"), g.pop("gb")
    loss = lax.psum(loss, ("x", "y", "c"))

    w_mod_pieces = g["w_mod"].reshape(N_CHIPS, DEPTH * D, 3 * D // N_CHIPS)
    g = {k: (jnp.stack(v) if isinstance(v, list) else v) for k, v in g.items() if k not in ("wp", "w_br", "w_out", "w_mod")}

    small_names = REPL_SMALL + SHARD_SMALL
    gs = _all_reduce_small(_pack_small([g[k] for k in small_names]), "all_reduce_small")
    gsm = dict(zip(small_names, _unpack_small(gs, [g[k].shape for k in small_names])))
    for k in SHARD_SMALL:
        wdt = W[k].shape[-1]
        gsm[k] = lax.dynamic_slice_in_dim(gsm[k], chip * wdt, wdt, axis=gsm[k].ndim - 1)

    grad, delta, new_m, new_v = {}, {}, {}, {}

    def adamw_big(keys, after):
        filled = _sibling_fill([red[k] for k in keys], [AXIS[k] for k in keys], "rs_sibling_fill_" + keys[0])
        for k, r in zip(keys, filled):
            back = (lambda a: jnp.swapaxes(a, 1, 2)) if k == "w_in" else (lambda a: a.reshape(W[k].shape))
            d_, m_, v_ = _adamw(Wx[k], r, Mx[k], Vx[k], f"adamw_{k}", after=after)
            grad[k], delta[k], new_m[k], new_v[k] = back(r), back(d_), back(m_), back(v_)
        return d_

    token = launch("0d", 0, {"w_mod": w_mod_pieces}, after=gs)
    land("1", gx)
    land("0b", gx)
    last = adamw_big(MERGE, token)
    shapes = [W[k].shape for k in small_names]
    d_, m_, v_ = _adamw(*[_pack_small([P_[k] for k in small_names])[None] for P_ in (W, gsm, M, V)], "adamw_small", after=last)
    for k, dd, mm_, vv in zip(small_names, _unpack_small(d_, shapes), _unpack_small(m_, shapes), _unpack_small(v_, shapes)):
        grad[k], delta[k], new_m[k], new_v[k] = gsm[k], dd, mm_, vv
    land("0c", d_)
    land("0d", d_)
    adamw_big(("w_in", "w_mod"), None)

    return (loss, gx[None], *[grad[k] for k in ORDER], *[delta[k] for k in ORDER], *[new_m[k] for k in ORDER],
            *[new_v[k] for k in ORDER])
```

```python
import functools

import jax
import jax.numpy as jnp
import numpy as np
from jax import lax
from jax.experimental import pallas as pl
from jax.experimental.pallas import tpu as pltpu

f32 = jnp.float32
bf16 = jnp.bfloat16

D = 1024
DEPTH = 2
GRID_W = 64
BRW = 512
HD = 128
A_HEADS = 4
C_HEADS = 4
C_KW = 256
C_RANK = 16
C_TAU = 16.0
CH = 128
KB = 3
KD = 31
ALPHA = (2 * DEPTH) ** 0.25
EPS = 1e-6
ROPE_THETA = 10000.0
N_IN = 10784
LR, B1, B2, AEPS, WD, STEP = 0.001, 0.9, 0.999, 1e-08, 0.01, 10

W_M, W_A, W_C, W_G = 4 * D + 4 * BRW, 1024, 5 * BRW, 1152
GROUPS = ("M", "A", "C", "G")
GROUP_W = dict(M=W_M, A=W_A, C=W_C, G=W_G)
M_GA, M_GB, M_GC, M_GD = 4 * D, 4 * D + BRW, 4 * D + 2 * BRW, 4 * D + 3 * BRW
A_K, A_V = 512, 768
G_K, G_V, G_R = 256, 512, 1024
CT = 5 * 128
S_Q, S_GA, S_B, S_C, S_X, S_GB, S_CQ, S_CV, S_GC, S_R, S_DA, S_DG, S_GD, S_MG = (
    0, 1024, 1536, 2048, 2560, 3072, 3584, 4096, 4608, 5120, 5152, 5664, 6176, 6688)

LANE = 128
SUB = 8
VMEM_LIMIT = 56 * 1024 * 1024
CONV_PAD = 16
GLA_SUB = 16
GLA_CLAMP = 60.0


def _cparams(sem, vmem=VMEM_LIMIT):
    return pltpu.CompilerParams(dimension_semantics=sem, vmem_limit_bytes=vmem)


def _dg(a, b, ca, cb):
    return lax.dot_general(a.astype(bf16), b.astype(bf16), (((ca,), (cb,)), ((), ())),
                           preferred_element_type=f32)


@jax.custom_vjp
def mm(a, b):
    return _dg(a, b, 1, 0)


mm.defvjp(lambda a, b: (_dg(a, b, 1, 0), (a, b)),
          lambda r, ct: (_dg(ct, r[1], 1, 1).astype(r[0].dtype), _dg(r[0], ct, 0, 0).astype(r[1].dtype)))


@jax.custom_vjp
def mm_nt(a, b):
    return _dg(a, b, 1, 1)


mm_nt.defvjp(lambda a, b: (_dg(a, b, 1, 1), (a, b)),
             lambda r, ct: (_dg(ct, r[1], 1, 0).astype(r[0].dtype), _dg(ct, r[0], 0, 0).astype(r[1].dtype)))


@jax.custom_vjp
def mm_tn(a, b):
    return _dg(a, b, 0, 0)


mm_tn.defvjp(lambda a, b: (_dg(a, b, 0, 0), (a, b)),
             lambda r, ct: (_dg(r[1], ct, 1, 1).astype(r[0].dtype), _dg(r[0], ct, 1, 0).astype(r[1].dtype)))


def _sigmoid(x):
    return 0.5 * jnp.tanh(0.5 * x) + 0.5


def _silu(x):
    return x * _sigmoid(x)


def _ln(x):
    mu = jnp.mean(x, -1, keepdims=True)
    xc = x - mu
    var = jnp.mean(xc * xc, -1, keepdims=True)
    return xc * lax.rsqrt(var + EPS)


def _rms(x, g):
    return x * lax.rsqrt(jnp.mean(x * x, -1, keepdims=True) + EPS) * g


@jax.custom_vjp
def _rope(x, cos_f, sin_a, sin_b):
    return x * cos_f + pltpu.roll(x, HD - 1, 1) * sin_a + pltpu.roll(x, 1, 1) * sin_b


def _rope_fwd(x, cos_f, sin_a, sin_b):
    return _rope(x, cos_f, sin_a, sin_b), (cos_f, sin_a, sin_b)


def _rope_bwd(r, ct):
    cos_f, sin_a, sin_b = r
    dx = ct * cos_f + pltpu.roll(ct * sin_a, 1, 1) + pltpu.roll(ct * sin_b, HD - 1, 1)
    return dx, jnp.zeros_like(cos_f), jnp.zeros_like(sin_a), jnp.zeros_like(sin_b)


_rope.defvjp(_rope_fwd, _rope_bwd)


def _row_ids(i, tm):
    return i * tm + lax.broadcasted_iota(jnp.int32, (tm, 1), 0)


def _partial_rows(ref, rows):
    n = len(rows)
    for k, r in enumerate(rows):
        ref[k:k + 1, :] = r
    ref[n:SUB, :] = jnp.zeros((SUB - n, ref.shape[-1]), f32)


def _matmul(a, b, mode, tm, tn, tk, name, out_dtype=f32, add=None, after=None):
    sect = a.ndim == 3
    a2 = (a.shape[1], a.shape[0] * a.shape[2]) if sect else a.shape
    if mode == "nn":
        (M, K), N = a2, b.shape[1]
        a_spec = pl.BlockSpec((None, tm, tk), lambda j, i, k: (k, i, 0)) if sect else pl.BlockSpec((tm, tk), lambda j, i, k: (i, k))
        b_spec = pl.BlockSpec((tk, tn), lambda j, i, k: (k, j))
        ca, cb = 1, 0
        assert not sect or tk == a.shape[2]
    elif mode == "nt":
        (M, K), N = a2, b.shape[0]
        assert not sect
        a_spec = pl.BlockSpec((tm, tk), lambda j, i, k: (i, k))
        b_spec = pl.BlockSpec((tn, tk), lambda j, i, k: (j, k))
        ca, cb = 1, 1
    else:
        (K, M), N = a2, b.shape[1]
        a_spec = pl.BlockSpec((None, tk, tm), lambda j, i, k: (i, k, 0)) if sect else pl.BlockSpec((tk, tm), lambda j, i, k: (k, i))
        b_spec = pl.BlockSpec((tk, tn), lambda j, i, k: (k, j))
        ca, cb = 0, 0
        assert not sect or tm == a.shape[2]
    assert M % tm == 0 and N % tn == 0 and K % tk == 0, (name, M, N, K, tm, tn, tk)
    nk = K // tk

    o_spec = pl.BlockSpec((tm, tn), lambda j, i, k: (i, j))

    def body(a_ref, b_ref, *rest):
        add_ref = rest[0] if add is not None else None
        o_ref, acc_ref = rest[-2:]
        k = pl.program_id(2)
        part = _dg(a_ref[...], b_ref[...], ca, cb)

        @pl.when(k == 0)
        def _():
            acc_ref[...] = part if add_ref is None else part + add_ref[...]

        @pl.when(k > 0)
        def _():
            acc_ref[...] += part

        @pl.when(k == nk - 1)
        def _():
            o_ref[...] = acc_ref[...].astype(o_ref.dtype)

    extra = ([] if add is None else [(o_spec, add)]) + ([] if after is None else [(pl.BlockSpec(memory_space=pl.ANY), after)])
    return pl.pallas_call(
        body, name=name, grid=(N // tn, M // tm, nk),
        in_specs=[a_spec, b_spec] + [s_ for s_, _ in extra], out_specs=o_spec,
        out_shape=jax.ShapeDtypeStruct((M, N), out_dtype),
        scratch_shapes=[pltpu.VMEM((tm, tn), f32)],
        compiler_params=_cparams(("parallel", "parallel", "arbitrary")),
    )(a, b, *[v_ for _, v_ in extra])


def _matmul_groups(a, b, tks, tm, name, after=None):
    keys = list(a)
    M = a[keys[0]].shape[-2]
    N = b[keys[0]].shape[1]
    count = {g: b[g].shape[0] // tks[g] for g in keys}
    first, total = {}, 0
    for g in keys:
        first[g], total = total, total + count[g]

    def k_of(g):
        return lambda s: jnp.clip(s - first[g], 0, count[g] - 1)

    a_specs = [pl.BlockSpec((None, tm, tks[g]), functools.partial(lambda i, s, kk: (kk(s), i, 0), kk=k_of(g)))
               if a[g].ndim == 3 else pl.BlockSpec((tm, tks[g]), functools.partial(lambda i, s, kk: (i, kk(s)), kk=k_of(g)))
               for g in keys]
    b_specs = [pl.BlockSpec((tks[g], N), functools.partial(lambda i, s, kk: (kk(s), 0), kk=k_of(g))) for g in keys]
    n = len(keys)

    def body(*refs):
        o_ref, acc_ref = refs[-2:]
        s = pl.program_id(1)

        @pl.when(s == 0)
        def _():
            acc_ref[...] = jnp.zeros_like(acc_ref)

        for j, g in enumerate(keys):
            @pl.when((s >= first[g]) & (s < first[g] + count[g]))
            def _(j=j):
                acc_ref[...] += _dg(refs[j][...], refs[n + j][...], 1, 0)

        @pl.when(s == total - 1)
        def _():
            o_ref[...] = acc_ref[...]

    extra = [] if after is None else [after]
    return pl.pallas_call(
        body, name=name, grid=(M // tm, total),
        in_specs=a_specs + b_specs + [pl.BlockSpec(memory_space=pl.ANY)] * len(extra),
        out_specs=pl.BlockSpec((tm, N), lambda i, s: (i, 0)),
        out_shape=jax.ShapeDtypeStruct((M, N), f32),
        scratch_shapes=[pltpu.VMEM((tm, N), f32)],
        compiler_params=_cparams(("parallel", "arbitrary")),
    )(*[a[g] for g in keys], *[b[g] for g in keys], *extra)


def _matmul_tn_batched(a, b, ns, name):
    B, K, M = a.shape
    N = b.shape[2] // ns

    def body(a_ref, b_ref, o_ref):
        o_ref[...] = _dg(a_ref[...], b_ref[...], 0, 0)

    return pl.pallas_call(
        body, name=name, grid=(B, ns),
        in_specs=[pl.BlockSpec((None, K, M), lambda i, s: (i, 0, 0)), pl.BlockSpec((None, K, N), lambda i, s: (i, 0, s))],
        out_specs=pl.BlockSpec((None, None, M, N), lambda i, s: (s, i, 0, 0)),
        out_shape=jax.ShapeDtypeStruct((ns, B, M, N), f32),
        compiler_params=_cparams(("parallel", "parallel")),
    )(a, b)


MOD_TN = 768


def _mod_fwd(cin, w_mod, b_mod):
    def body(c_ref, w_ref, b_ref, o_ref):
        o_ref[...] = mm(_silu(c_ref[...]), w_ref[...]) + b_ref[...]

    return pl.pallas_call(
        body, name="mod_fwd", grid=(DEPTH, 3 * D // MOD_TN),
        in_specs=[pl.BlockSpec((SUB, D), lambda l, j: (0, 0)),
                  pl.BlockSpec((None, None, D, MOD_TN), lambda l, j: (j, l, 0, 0)),
                  pl.BlockSpec((None, 1, MOD_TN), lambda l, j: (l, 0, j))],
        out_specs=pl.BlockSpec((None, SUB, MOD_TN), lambda l, j: (l, 0, j)),
        out_shape=jax.ShapeDtypeStruct((DEPTH, SUB, 3 * D), f32),
        compiler_params=_cparams(("parallel", "parallel")),
    )(cin, w_mod, b_mod.reshape(DEPTH, 1, 3 * D))


def _mod_bwd(cin, w_mod, dmodv):
    nj = 3 * D // MOD_TN

    def body(c_ref, w_ref, g_ref, dw_ref, dc_ref):
        _, vjp = jax.vjp(lambda c, w: mm(_silu(c), w), c_ref[...], w_ref[...].astype(f32))
        dc, dw = vjp(g_ref[...])
        dw_ref[...] = dw
        dc_ref[...] = dc

    return pl.pallas_call(
        body, name="mod_bwd", grid=(DEPTH, nj),
        in_specs=[pl.BlockSpec((SUB, D), lambda l, j: (0, 0)),
                  pl.BlockSpec((None, None, D, MOD_TN), lambda l, j: (j, l, 0, 0)),
                  pl.BlockSpec((None, SUB, MOD_TN), lambda l, j: (l, 0, j))],
        out_specs=[pl.BlockSpec((None, None, D, MOD_TN), lambda l, j: (j, l, 0, 0)),
                   pl.BlockSpec((None, None, SUB, D), lambda l, j: (l, j, 0, 0))],
        out_shape=[jax.ShapeDtypeStruct((nj, DEPTH, D, MOD_TN), f32),
                   jax.ShapeDtypeStruct((DEPTH, nj, SUB, D), f32)],
        compiler_params=_cparams(("parallel", "parallel")),
    )(cin, w_mod, dmodv)


def _u_fn(h, m_l, m_c, isctx):
    n = _ln(h)
    shift = jnp.where(isctx, m_c[:, 0:D], m_l[:, 0:D])
    scale = jnp.where(isctx, m_c[:, D:2 * D], m_l[:, D:2 * D])
    return n * (1.0 + scale) + shift


def _ln_fwd(h, modv_l, tc, tm, name):
    T = h.shape[0]

    def body(h_ref, m_ref, u_ref):
        isctx = _row_ids(pl.program_id(0), tm) < tc
        u_ref[...] = _u_fn(h_ref[...], m_ref[0:1, :], m_ref[1:2, :], isctx).astype(bf16)

    return pl.pallas_call(
        body, name=name, grid=(T // tm,),
        in_specs=[pl.BlockSpec((tm, D), lambda i: (i, 0)), pl.BlockSpec((SUB, 3 * D), lambda i: (0, 0))],
        out_specs=pl.BlockSpec((tm, D), lambda i: (i, 0)),
        out_shape=jax.ShapeDtypeStruct((T, D), bf16),
        compiler_params=_cparams(("parallel",)),
    )(h, modv_l)


def _ln_bwd(du, h, dh_res, modv_l, tc, tm, name):
    T = h.shape[0]
    nt = T // tm

    def body(du_ref, h_ref, r_ref, m_ref, dh_ref, dm_ref):
        isctx = _row_ids(pl.program_id(0), tm) < tc
        _, vjp = jax.vjp(lambda h, ml, mc: _u_fn(h, ml, mc, isctx), h_ref[...], m_ref[0:1, :], m_ref[1:2, :])
        dh, dml, dmc = vjp(du_ref[...])
        dh_ref[...] = dh + r_ref[...]
        _partial_rows(dm_ref, [dml, dmc])

    return pl.pallas_call(
        body, name=name, grid=(nt,),
        in_specs=[pl.BlockSpec((tm, D), lambda i: (i, 0)), pl.BlockSpec((tm, D), lambda i: (i, 0)),
                  pl.BlockSpec((tm, D), lambda i: (i, 0)), pl.BlockSpec((SUB, 3 * D), lambda i: (0, 0))],
        out_specs=[pl.BlockSpec((tm, D), lambda i: (i, 0)), pl.BlockSpec((None, SUB, 3 * D), lambda i: (i, 0, 0))],
        out_shape=[jax.ShapeDtypeStruct((T, D), f32), jax.ShapeDtypeStruct((nt, SUB, 3 * D), f32)],
        compiler_params=_cparams(("parallel",)),
    )(du, h, dh_res, modv_l)


def _prep_fn(q, k, qg, kg, cos_f, sin_a, sin_b):
    qs = [_rope(_rms(q[:, HD * i:HD * (i + 1)], qg), cos_f, sin_a, sin_b) * (HD ** -0.5) for i in range(A_HEADS)]
    ks = [_rope(_rms(k[:, HD * i:HD * (i + 1)], kg), cos_f, sin_a, sin_b) for i in range(A_HEADS // 2)]
    return jnp.concatenate(qs, 1), jnp.concatenate(ks, 1)


def _tok(tm, w, off):
    return pl.BlockSpec((tm, w), lambda i: (i, off // w))


def _vec(w):
    return pl.BlockSpec((1, w), lambda i: (0, 0))


def _prep_fwd(P, qg, kg, rope, tm, name):
    T = P.shape[0]

    def body(q_ref, k_ref, v_ref, qg_ref, kg_ref, c_ref, sa_ref, sb_ref, qn_ref, kn_ref, vb_ref):
        qn, kn = _prep_fn(q_ref[...].astype(f32), k_ref[...].astype(f32), qg_ref[...], kg_ref[...], c_ref[...], sa_ref[...],
                          sb_ref[...])
        qn_ref[...] = qn.astype(bf16)
        kn_ref[...] = kn.astype(bf16)
        vb_ref[...] = v_ref[...].astype(bf16)

    return pl.pallas_call(
        body, name=name, grid=(T // tm,),
        in_specs=[_tok(tm, 512, 0), _tok(tm, 256, A_K), _tok(tm, 256, A_V), _vec(HD), _vec(HD),
                  _tok(tm, HD, 0), _tok(tm, HD, 0), _tok(tm, HD, 0)],
        out_specs=[_tok(tm, 512, 0), _tok(tm, 256, 0), _tok(tm, 256, 0)],
        out_shape=[jax.ShapeDtypeStruct((T, 512), bf16), jax.ShapeDtypeStruct((T, 256), bf16),
                   jax.ShapeDtypeStruct((T, 256), bf16)],
        compiler_params=_cparams(("parallel",)),
    )(P, P, P, qg, kg, *rope)


def _prep_bwd(P, dqn, dkn, dv, qg, kg, rope, tm, name):
    T = P.shape[0]
    nt = T // tm

    def body(q_ref, k_ref, dq_ref, dk_ref, dv_ref, qg_ref, kg_ref, c_ref, sa_ref, sb_ref, o_ref, og_ref):
        tabs = (c_ref[...], sa_ref[...], sb_ref[...])
        _, vjp = jax.vjp(lambda q, k, a, b: _prep_fn(q, k, a, b, *tabs), q_ref[...].astype(f32), k_ref[...].astype(f32),
                         qg_ref[...], kg_ref[...])
        dq, dk, dqg, dkg = vjp((dq_ref[...], dk_ref[...]))
        o_ref[:, 0:A_K] = dq.astype(bf16)
        o_ref[:, A_K:A_V] = dk.astype(bf16)
        o_ref[:, A_V:W_A] = dv_ref[...].astype(bf16)
        _partial_rows(og_ref, [dqg, dkg])

    return pl.pallas_call(
        body, name=name, grid=(nt,),
        in_specs=[_tok(tm, 512, 0), _tok(tm, 256, A_K), _tok(tm, 512, 0), _tok(tm, 256, 0), _tok(tm, 256, 0),
                  _vec(HD), _vec(HD), _tok(tm, HD, 0), _tok(tm, HD, 0), _tok(tm, HD, 0)],
        out_specs=[_tok(tm, W_A, 0), pl.BlockSpec((None, SUB, HD), lambda i: (i, 0, 0))],
        out_shape=[jax.ShapeDtypeStruct((T, W_A), bf16), jax.ShapeDtypeStruct((nt, SUB, HD), f32)],
        compiler_params=_cparams(("parallel",)),
    )(P, P, dqn, dkn, dv, qg, kg, *rope)


def _attn_fn(q, k, v, lim):
    col = lax.broadcasted_iota(jnp.int32, (1, k.shape[0]), 1)
    s = mm_nt(q, k) + jnp.where(col < lim, 0.0, -1e30)
    m = lax.stop_gradient(jnp.max(s, -1, keepdims=True))
    e = jnp.exp(s - m)
    p = e * (1.0 / jnp.sum(e, -1, keepdims=True))
    return mm(p, v)


def _attn_fwd(qn, kn, vb, tc, tq, name):
    T = qn.shape[0]

    def body(q_ref, k_ref, v_ref, o_ref):
        lim = jnp.where(pl.program_id(1) * tq < tc, tc, T)
        o_ref[...] = _attn_fn(q_ref[...], k_ref[...], v_ref[...], lim)

    return pl.pallas_call(
        body, name=name, grid=(A_HEADS, T // tq),
        in_specs=[pl.BlockSpec((tq, HD), lambda h, i: (i, h)), pl.BlockSpec((T, HD), lambda h, i: (0, h // 2)),
                  pl.BlockSpec((T, HD), lambda h, i: (0, h // 2))],
        out_specs=pl.BlockSpec((tq, HD), lambda h, i: (i, h)),
        out_shape=jax.ShapeDtypeStruct((T, 512), f32),
        compiler_params=_cparams(("parallel", "parallel")),
    )(qn, kn, vb)


def _attn_bwd(qn, kn, vb, dya, tc, tq, name):
    T = qn.shape[0]

    def body(q_ref, k_ref, v_ref, g_ref, dq_ref, dk_ref, dv_ref):
        first = (pl.program_id(1) == 0) & (pl.program_id(2) == 0)
        lim = jnp.where(pl.program_id(2) * tq < tc, tc, T)
        _, vjp = jax.vjp(lambda q, k, v: _attn_fn(q, k, v, lim), q_ref[...].astype(f32), k_ref[...].astype(f32),
                         v_ref[...].astype(f32))
        dq, dk, dv = vjp(g_ref[...])
        dq_ref[...] = dq

        @pl.when(first)
        def _():
            dk_ref[...] = dk
            dv_ref[...] = dv

        @pl.when(jnp.logical_not(first))
        def _():
            dk_ref[...] += dk
            dv_ref[...] += dv

    qspec = pl.BlockSpec((tq, HD), lambda kv, g, i: (i, 2 * kv + g))
    kspec = pl.BlockSpec((T, HD), lambda kv, g, i: (0, kv))
    return pl.pallas_call(
        body, name=name, grid=(A_HEADS // 2, 2, T // tq),
        in_specs=[qspec, kspec, kspec, qspec], out_specs=[qspec, kspec, kspec],
        out_shape=[jax.ShapeDtypeStruct((T, 512), f32), jax.ShapeDtypeStruct((T, 256), f32),
                   jax.ShapeDtypeStruct((T, 256), f32)],
        compiler_params=_cparams(("parallel", "arbitrary", "arbitrary")),
    )(qn, kn, vb, dya)


def _conv_rows(tc, tl):
    return CONV_PAD + tc + CONV_PAD + tl + CONV_PAD


def _fill_pad(pad_ref, val, tc, tl):
    z = jnp.zeros((CONV_PAD, LANE), f32)
    pad_ref[0:CONV_PAD, :] = z
    pad_ref[CONV_PAD:CONV_PAD + tc, :] = val[0:tc]
    pad_ref[CONV_PAD + tc:2 * CONV_PAD + tc, :] = z
    pad_ref[2 * CONV_PAD + tc:2 * CONV_PAD + tc + tl, :] = val[tc:tc + tl]
    pad_ref[2 * CONV_PAD + tc + tl:3 * CONV_PAD + tc + tl, :] = z


def _conv_apply(pad_ref, w_ref, K, tc, tl, rc, emit, flip=False):
    half = K // 2
    for seg0, off, n in ((0, CONV_PAD, tc), (tc, 2 * CONV_PAD + tc, tl)):
        for r0 in range(0, n, rc):
            acc = None
            for k in range(K):
                sh = (half - k) if flip else (k - half)
                term = pad_ref[pl.ds(off + r0 + sh, rc), :] * w_ref[k:k + 1, :]
                acc = term if acc is None else acc + term
            emit(seg0 + r0, acc)


def _conv_wgrad(pad_ref, dy_ref, K, tc, tl, rc, dw_ref):
    half = K // 2
    for k in range(K):
        acc = jnp.zeros((1, LANE), f32)
        for seg0, off, n in ((0, CONV_PAD, tc), (tc, 2 * CONV_PAD + tc, tl)):
            for r0 in range(0, n, rc):
                acc = acc + jnp.sum(pad_ref[pl.ds(off + r0 + k - half, rc), :] * dy_ref[pl.ds(seg0 + r0, rc), :],
                                    axis=0, keepdims=True)
        dw_ref[k:k + 1, :] = acc


def _col(T, off):
    return pl.BlockSpec((T, LANE), lambda j: (0, off // LANE + j))


C_B, C_C, C_X, C_A, C_G = range(5)
N_SEC = 5


class _Sections:
    def __init__(self, refs):
        self.refs = refs

    def __getitem__(self, idx):
        rows, sec = idx
        return self.refs[sec][rows, :].astype(f32)

    def __setitem__(self, idx, val):
        rows, sec = idx
        self.refs[sec, rows, :] = val


def _sec_specs(T):
    return [pl.BlockSpec((T, LANE), functools.partial(lambda j, s: (0, s * (BRW // LANE) + j), s=s)) for s in range(N_SEC)]


def _conv_fwd(P, wb, wd, bd, tc, tl, rc, name):
    T = tc + tl

    def body(*refs):
        p_ref = _Sections(refs[:N_SEC])
        wb_ref, wd_ref, bd_ref, yb_ref, hh_ref, pad_ref = refs[N_SEC:]
        _fill_pad(pad_ref, p_ref[:, C_C] * p_ref[:, C_X], tc, tl)

        def emit_b(r0, y):
            yb_ref[pl.ds(r0, rc), :] = y * p_ref[pl.ds(r0, rc), C_B]

        _conv_apply(pad_ref, wb_ref, KB, tc, tl, rc, emit_b)
        _fill_pad(pad_ref, p_ref[:, C_A] * _sigmoid(p_ref[:, C_G]), tc, tl)

        def emit_d(r0, y):
            hh_ref[pl.ds(r0, rc), :] = y + bd_ref[...]

        _conv_apply(pad_ref, wd_ref, KD, tc, tl, rc, emit_d)

    return pl.pallas_call(
        body, name=name, grid=(BRW // LANE,),
        in_specs=_sec_specs(T) + [pl.BlockSpec((KB, LANE), lambda j: (0, j)), pl.BlockSpec((KD, LANE), lambda j: (0, j)),
                                  pl.BlockSpec((1, LANE), lambda j: (0, j))],
        out_specs=[_col(T, 0), _col(T, 0)],
        out_shape=[jax.ShapeDtypeStruct((T, BRW), f32), jax.ShapeDtypeStruct((T, BRW), f32)],
        scratch_shapes=[pltpu.VMEM((_conv_rows(tc, tl), LANE), f32)],
        compiler_params=_cparams(("parallel",)),
    )(*[P] * N_SEC, wb, wd, bd)


def _conv_bwd(P, dyb, dhh, wb, wd, tc, tl, rc, name):
    T = tc + tl

    def body(*refs):
        p_ref = _Sections(refs[:N_SEC])
        dyb_ref, dhh_ref, wb_ref, wd_ref, dp3_ref, dwb_ref, dwd_ref, dbd_ref, pad_ref, pad2_ref, tmp_ref = refs[N_SEC:]
        dp_ref = _Sections(dp3_ref)
        _fill_pad(pad_ref, p_ref[:, C_C] * p_ref[:, C_X], tc, tl)

        def emit_cv(r0, y):
            dp_ref[pl.ds(r0, rc), C_B] = (y * dyb_ref[pl.ds(r0, rc), :]).astype(bf16)

        _conv_apply(pad_ref, wb_ref, KB, tc, tl, rc, emit_cv)
        tmp_ref[...] = dyb_ref[...] * p_ref[:, C_B]
        _conv_wgrad(pad_ref, tmp_ref, KB, tc, tl, rc, dwb_ref)
        _fill_pad(pad2_ref, tmp_ref[...], tc, tl)

        def emit_ds(r0, y):
            dp_ref[pl.ds(r0, rc), C_C] = (y * p_ref[pl.ds(r0, rc), C_X]).astype(bf16)
            dp_ref[pl.ds(r0, rc), C_X] = (y * p_ref[pl.ds(r0, rc), C_C]).astype(bf16)

        _conv_apply(pad2_ref, wb_ref, KB, tc, tl, rc, emit_ds, flip=True)
        _fill_pad(pad_ref, p_ref[:, C_A] * _sigmoid(p_ref[:, C_G]), tc, tl)
        _conv_wgrad(pad_ref, dhh_ref, KD, tc, tl, rc, dwd_ref)
        dbd_ref[...] = jnp.sum(dhh_ref[...], axis=0, keepdims=True)
        _fill_pad(pad2_ref, dhh_ref[...], tc, tl)

        def emit_d2(r0, y):
            sg = _sigmoid(p_ref[pl.ds(r0, rc), C_G])
            a = p_ref[pl.ds(r0, rc), C_A]
            dp_ref[pl.ds(r0, rc), C_A] = (y * sg).astype(bf16)
            dp_ref[pl.ds(r0, rc), C_G] = (y * a * sg * (1.0 - sg)).astype(bf16)

        _conv_apply(pad2_ref, wd_ref, KD, tc, tl, rc, emit_d2, flip=True)

    return pl.pallas_call(
        body, name=name, grid=(BRW // LANE,),
        in_specs=_sec_specs(T) + [_col(T, 0), _col(T, 0),
                                  pl.BlockSpec((KB, LANE), lambda j: (0, j)), pl.BlockSpec((KD, LANE), lambda j: (0, j))],
        out_specs=[pl.BlockSpec((N_SEC, T, LANE), lambda j: (0, 0, j)), pl.BlockSpec((KB, LANE), lambda j: (0, j)),
                   pl.BlockSpec((KD, LANE), lambda j: (0, j)), pl.BlockSpec((1, LANE), lambda j: (0, j))],
        out_shape=[jax.ShapeDtypeStruct((N_SEC, T, BRW), bf16), jax.ShapeDtypeStruct((KB, BRW), f32),
                   jax.ShapeDtypeStruct((KD, BRW), f32), jax.ShapeDtypeStruct((1, BRW), f32)],
        scratch_shapes=[pltpu.VMEM((_conv_rows(tc, tl), LANE), f32), pltpu.VMEM((_conv_rows(tc, tl), LANE), f32),
                        pltpu.VMEM((T, LANE), f32)],
        compiler_params=_cparams(("parallel",)),
    )(*[P] * N_SEC, dyb, dhh, wb, wd)


def _gla_chunk(q, k, v, r, w2, b2, st, isfwd):
    z = mm(r, w2) + b2
    g = jax.nn.log_sigmoid(z[:, 0:C_KW] if isfwd else z[:, C_KW:2 * C_KW]) / C_TAU
    ri = lax.broadcasted_iota(jnp.int32, (CH, CH), 0)
    ci = lax.broadcasted_iota(jnp.int32, (CH, CH), 1)
    tri = ((ci <= ri) if isfwd else (ci >= ri)).astype(f32)
    cum = jnp.dot(tri, g, preferred_element_type=f32, precision=lax.Precision.HIGHEST)
    last = jnp.sum(g, axis=0, keepdims=True)
    q = q * (C_KW // C_HEADS) ** -0.5
    hv = lax.broadcasted_iota(jnp.int32, (BRW, C_KW), 0) // (BRW // C_HEADS)
    hk = lax.broadcasted_iota(jnp.int32, (BRW, C_KW), 1) // (C_KW // C_HEADS)
    st_new = st * jnp.exp(last) + jnp.where(hv == hk, mm_tn(v, k * jnp.exp(last - cum)), 0.0)
    o = mm_nt(q * jnp.exp(cum), st)
    rowi = lax.broadcasted_iota(jnp.int32, (CH, C_KW), 0)
    srow = lax.broadcasted_iota(jnp.int32, (C_HEADS * CH, C_KW), 0)
    slane = lax.broadcasted_iota(jnp.int32, (C_HEADS * CH, C_KW), 1)
    own_lanes = srow // CH == slane // (C_KW // C_HEADS)
    pos = lax.broadcasted_iota(jnp.int32, (C_HEADS * CH, CH), 0) % CH
    key = lax.broadcasted_iota(jnp.int32, (C_HEADS * CH, CH), 1)
    scores = jnp.zeros((C_HEADS * CH, CH), f32)
    for a in range(CH // GLA_SUB):
        idx = GLA_SUB * a - 1 if isfwd else GLA_SUB * (a + 1)
        ref = jnp.sum(jnp.where(rowi == idx, cum, 0.0), axis=0, keepdims=True)
        qa = q * jnp.exp(jnp.minimum(cum - ref, 0.0))
        ka = k * jnp.exp(jnp.minimum(ref - cum, GLA_CLAMP))
        s = mm_nt(jnp.where(own_lanes, jnp.concatenate([qa] * C_HEADS, axis=0), 0.0), ka)
        scores = scores + jnp.where(pos // GLA_SUB == a, s, 0.0)
    scores = jnp.where((key <= pos) if isfwd else (key >= pos), scores, 0.0)
    vw = BRW // C_HEADS
    o = o + jnp.concatenate([mm(scores[CH * hd:CH * (hd + 1)], v[:, vw * hd:vw * (hd + 1)]) for hd in range(C_HEADS)],
                            axis=1)
    return o, st_new


def _gla_chunk_of(d, n, nc, nch):
    back = jnp.where(n < nc, nc - 1 - n, nch - 1 - (n - nc))
    return jnp.where(d == 0, n, back)


def _gla_fwd(P, w2, b2, tc, name):
    T = P.shape[0]
    nch, nc = T // CH, tc // CH

    back = lambda n: _gla_chunk_of(1, n, nc, nch)

    def body(pf_ref, pb_ref, w_ref, b_ref, of_ref, ob_ref, ssf_ref, ssb_ref, stf_ref, stb_ref):
        @pl.when(pl.program_id(0) == 0)
        def _():
            stf_ref[...] = jnp.zeros_like(stf_ref)
            stb_ref[...] = jnp.zeros_like(stb_ref)

        for p_ref, o_ref, ss_ref, st_ref, isfwd in ((pf_ref, of_ref, ssf_ref, stf_ref, True),
                                                    (pb_ref, ob_ref, ssb_ref, stb_ref, False)):
            st = st_ref[...]
            ss_ref[...] = st
            p = p_ref[...].astype(f32)
            o, st_new = _gla_chunk(p[:, 0:G_K], p[:, G_K:G_V], p[:, G_V:G_R], p[:, G_R:W_G], w_ref[...], b_ref[...], st, isfwd)
            o_ref[...] = o
            st_ref[...] = st_new

    sd = jax.ShapeDtypeStruct
    return pl.pallas_call(
        body, name=name, grid=(nch,),
        in_specs=[pl.BlockSpec((CH, W_G), lambda n: (n, 0)), pl.BlockSpec((CH, W_G), lambda n: (back(n), 0)),
                  pl.BlockSpec((LANE, 512), lambda n: (0, 0)), pl.BlockSpec((1, 512), lambda n: (0, 0))],
        out_specs=[pl.BlockSpec((CH, BRW), lambda n: (n, 0)), pl.BlockSpec((CH, BRW), lambda n: (back(n), 0)),
                   pl.BlockSpec((None, BRW, C_KW), lambda n: (n, 0, 0)), pl.BlockSpec((None, BRW, C_KW), lambda n: (n, 0, 0))],
        out_shape=[sd((T, BRW), f32), sd((T, BRW), f32), sd((nch, BRW, C_KW), f32), sd((nch, BRW, C_KW), f32)],
        scratch_shapes=[pltpu.VMEM((BRW, C_KW), f32), pltpu.VMEM((BRW, C_KW), f32)],
        compiler_params=_cparams(("arbitrary",)),
    )(P, P, w2, b2)


def _gla_bwd(P, w2, b2, ssave, doc, tc, name):
    T = P.shape[0]
    nch, nc = T // CH, tc // CH

    fwd_chunk = lambda m: nch - 1 - m
    back_chunk = lambda m: _gla_chunk_of(1, nch - 1 - m, nc, nch)

    def body(pf_ref, pb_ref, w_ref, b_ref, ssf_ref, ssb_ref, gf_ref, gb_ref, dpf_ref, dpb_ref, dw_ref, db_ref,
             dstf_ref, dstb_ref):
        m = pl.program_id(0)

        @pl.when(m == 0)
        def _():
            dstf_ref[...] = jnp.zeros_like(dstf_ref)
            dstb_ref[...] = jnp.zeros_like(dstb_ref)

        dw_sum, db_sum = None, None
        for p_ref, ss_ref, g_ref, dp_ref, dst_ref, isfwd in ((pf_ref, ssf_ref, gf_ref, dpf_ref, dstf_ref, True),
                                                             (pb_ref, ssb_ref, gb_ref, dpb_ref, dstb_ref, False)):
            p = p_ref[...].astype(f32)
            _, vjp = jax.vjp(lambda q, k, v, r, w, b, st: _gla_chunk(q, k, v, r, w, b, st, isfwd),
                             p[:, 0:G_K], p[:, G_K:G_V], p[:, G_V:G_R], p[:, G_R:W_G], w_ref[...], b_ref[...], ss_ref[...])
            dq, dk, dv, dr, dw, db, dst = vjp((g_ref[...], dst_ref[...]))
            dp_ref[:, 0:G_K] = dq
            dp_ref[:, G_K:G_V] = dk
            dp_ref[:, G_V:G_R] = dv
            dp_ref[:, G_R:W_G] = dr
            dst_ref[...] = dst
            dw_sum = dw if dw_sum is None else dw_sum + dw
            db_sum = db if db_sum is None else db_sum + db

        @pl.when(m == 0)
        def _():
            dw_ref[...] = dw_sum
            _partial_rows(db_ref, [db_sum])

        @pl.when(m > 0)
        def _():
            dw_ref[...] += dw_sum
            db_ref[0:1, :] += db_sum

    ssf, ssb = ssave
    chunk_f = lambda w: pl.BlockSpec((CH, w), lambda m: (fwd_chunk(m), 0))
    chunk_b = lambda w: pl.BlockSpec((CH, w), lambda m: (back_chunk(m), 0))
    state = pl.BlockSpec((None, BRW, C_KW), lambda m: (nch - 1 - m, 0, 0))
    sd = jax.ShapeDtypeStruct
    return pl.pallas_call(
        body, name=name, grid=(nch,),
        in_specs=[chunk_f(W_G), chunk_b(W_G), pl.BlockSpec((LANE, 512), lambda m: (0, 0)), pl.BlockSpec((1, 512), lambda m: (0, 0)),
                  state, state, chunk_f(BRW), chunk_b(BRW)],
        out_specs=[chunk_f(W_G), chunk_b(W_G), pl.BlockSpec((LANE, 512), lambda m: (0, 0)), pl.BlockSpec((SUB, 512), lambda m: (0, 0))],
        out_shape=[sd((T, W_G), f32), sd((T, W_G), f32), sd((LANE, 512), f32), sd((SUB, 512), f32)],
        scratch_shapes=[pltpu.VMEM((BRW, C_KW), f32), pltpu.VMEM((BRW, C_KW), f32)],
        compiler_params=_cparams(("arbitrary",)),
    )(P, P, w2, b2, ssf, ssb, doc, doc)


def _sum_dirs(a, b, tm, name):
    T, W = a.shape

    def body(a_ref, b_ref, o_ref):
        o_ref[...] = (a_ref[...] + b_ref[...]).astype(bf16)

    spec = pl.BlockSpec((tm, W), lambda i: (i, 0))
    return pl.pallas_call(
        body, name=name, grid=(T // tm,), in_specs=[spec, spec], out_specs=spec,
        out_shape=jax.ShapeDtypeStruct((T, W), bf16),
        compiler_params=_cparams(("parallel",)),
    )(a, b)


def _merge_fn(h, m_l, m_c, isctx, ya, ga, yb, gb, of, ob, gc, hh, gd, mg, es, ey, cn, dng, dnb, lg, lb, wbr, wout):
    oc = of + ob
    yc = jnp.concatenate([_rms(oc[:, HD * i:HD * (i + 1)], cn[:, HD * i:HD * (i + 1)]) for i in range(C_HEADS)], 1)
    brs = [ya * _silu(ga), yb * _silu(gb), yc * _silu(gc), _silu(_ln(hh) * dng + dnb) * _silu(gd)]
    acc = None
    for i in range(4):
        t = _sigmoid(mg[:, D * i:D * (i + 1)]) * (mm(brs[i], wbr[i]) + es[i])
        acc = t if acc is None else acc + t
    y = mm(acc, wout) + ey
    gate = jnp.where(isctx, m_c[:, 2 * D:3 * D], m_l[:, 2 * D:3 * D])
    hn = _ln(ALPHA * h + gate * y) * lg + lb
    return hn, (brs, acc)


def _merge_specs(tm):
    t = lambda w, off=0: _tok(tm, w, off)
    return [t(D), pl.BlockSpec((SUB, 3 * D), lambda i: (0, 0)),
            t(BRW), t(BRW, M_GA), t(BRW), t(BRW, M_GB),
            t(BRW), t(BRW),
            t(BRW, M_GC), t(BRW), t(BRW, M_GD), t(4 * D, 0),
            _vec(BRW), _vec(BRW), _vec(BRW), _vec(D), _vec(D),
            pl.BlockSpec((4, BRW, D), lambda i: (0, 0, 0)), pl.BlockSpec((D, D), lambda i: (0, 0))]


def _merge_fwd(h, modv_l, ya, yb, o2, hh, P, cn, dng, dnb, lg, lb, wbr, wout, tc, tm, name):
    T = h.shape[0]

    def body(h_ref, m_ref, ya_ref, ga_ref, yb_ref, gb_ref, of_ref, ob_ref, gc_ref, hh_ref, gd_ref, mg_ref,
             cn_ref, dng_ref, dnb_ref, lg_ref, lb_ref, wbr_ref, wout_ref, o_ref):
        isctx = _row_ids(pl.program_id(0), tm) < tc
        zero = jnp.zeros((tm, D), f32)
        up = lambda r: r[...].astype(f32)
        hn, _ = _merge_fn(h_ref[...], m_ref[0:1, :], m_ref[1:2, :], isctx, ya_ref[...], up(ga_ref), yb_ref[...],
                          up(gb_ref), of_ref[...], ob_ref[...], up(gc_ref), hh_ref[...], up(gd_ref), up(mg_ref),
                          [zero] * 4, zero, cn_ref[...], dng_ref[...], dnb_ref[...], lg_ref[...], lb_ref[...],
                          [wbr_ref[i] for i in range(4)], wout_ref[...])
        o_ref[...] = hn

    return pl.pallas_call(
        body, name=name, grid=(T // tm,),
        in_specs=_merge_specs(tm), out_specs=_tok(tm, D, 0),
        out_shape=jax.ShapeDtypeStruct((T, D), f32),
        compiler_params=_cparams(("parallel",)),
    )(h, modv_l, ya, P, yb, P, o2[0], o2[1], P, hh, P, P, cn, dng, dnb, lg, lb, wbr, wout)


def _merge_bwd(dhn, h, modv_l, ya, yb, o2, hh, P, cn, dng, dnb, lg, lb, wbr, wout, tc, tm, name):
    T = h.shape[0]
    nt = T // tm

    def body(g_ref, h_ref, m_ref, ya_ref, ga_ref, yb_ref, gb_ref, of_ref, ob_ref, gc_ref, hh_ref, gd_ref, mg_ref,
             cn_ref, dng_ref, dnb_ref, lg_ref, lb_ref, wbr_ref, wout_ref,
             dh_ref, dm_ref, dya_ref, dyb_ref, doc_ref, dhh_ref, dp_ref,
             br_ref, z_ref, acc_ref, dy_ref, dv5_ref, dvd_ref):
        isctx = _row_ids(pl.program_id(0), tm) < tc
        zero = jnp.zeros((tm, D), f32)
        wbr_v = [wbr_ref[i] for i in range(4)]
        wout_v = wout_ref[...]
        up = lambda r: r[...].astype(f32)

        def fn(h, ml, mc, ya, ga, yb, gb, oc, gc, hh, gd, mg, e0, e1, e2, e3, ey, cn, dng, dnb, lg, lb):
            return _merge_fn(h, ml, mc, isctx, ya, ga, yb, gb, oc, jnp.zeros_like(oc), gc, hh, gd, mg,
                             [e0, e1, e2, e3], ey, cn, dng, dnb, lg, lb, wbr_v, wout_v)

        _, vjp, (brs, acc) = jax.vjp(
            fn, h_ref[...], m_ref[0:1, :], m_ref[1:2, :], ya_ref[...], up(ga_ref), yb_ref[...], up(gb_ref),
            of_ref[...] + ob_ref[...], up(gc_ref), hh_ref[...], up(gd_ref), up(mg_ref), zero, zero, zero, zero, zero,
            cn_ref[...], dng_ref[...], dnb_ref[...], lg_ref[...], lb_ref[...], has_aux=True)
        (dh, dml, dmc, dya, dga, dyb, dgb, doc, dgc, dhh, dgd, dmg, z0, z1, z2, z3, dy,
         dcn, ddng, ddnb, dlg, dlb) = vjp(g_ref[...])
        dh_ref[...] = dh
        _partial_rows(dm_ref, [dml, dmc])
        dya_ref[...] = dya
        dyb_ref[...] = dyb
        doc_ref[...] = doc
        dhh_ref[...] = dhh
        dp_ref[:, 0:M_GA] = dmg.astype(bf16)
        dp_ref[:, M_GA:M_GB] = dga.astype(bf16)
        dp_ref[:, M_GB:M_GC] = dgb.astype(bf16)
        dp_ref[:, M_GC:M_GD] = dgc.astype(bf16)
        dp_ref[:, M_GD:W_M] = dgd.astype(bf16)
        for i, z in enumerate((z0, z1, z2, z3)):
            br_ref[i] = brs[i].astype(bf16)
            z_ref[i] = z.astype(bf16)
        acc_ref[...] = acc.astype(bf16)
        dy_ref[...] = dy.astype(bf16)
        _partial_rows(dv5_ref, [dcn, ddng, ddnb])
        _partial_rows(dvd_ref, [dlg, dlb])

    t = lambda w: _tok(tm, w, 0)
    part = lambda w: pl.BlockSpec((None, SUB, w), lambda i: (i, 0, 0))
    sd = jax.ShapeDtypeStruct
    return pl.pallas_call(
        body, name=name, grid=(nt,),
        in_specs=[t(D)] + _merge_specs(tm),
        out_specs=[t(D), part(3 * D)] + [t(BRW)] * 4 + [t(W_M),
                   pl.BlockSpec((4, tm, BRW), lambda i: (0, i, 0)), pl.BlockSpec((4, tm, D), lambda i: (0, i, 0)),
                   t(D), t(D), part(BRW), part(D)],
        out_shape=[sd((T, D), f32), sd((nt, SUB, 3 * D), f32)] + [sd((T, BRW), f32)] * 4 + [sd((T, W_M), bf16),
                   sd((4, T, BRW), bf16), sd((4, T, D), bf16), sd((T, D), bf16), sd((T, D), bf16),
                   sd((nt, SUB, BRW), f32), sd((nt, SUB, D), f32)],
        compiler_params=_cparams(("parallel",)),
    )(dhn, h, modv_l, ya, P, yb, P, o2[0], o2[1], P, hh, P, P, cn, dng, dnb, lg, lb, wbr, wout)


def _loss_kernel(h, tgt, tc, tm, name):
    T = h.shape[0]
    nt = T // tm
    nct = tc // tm

    def body(h_ref, t_ref, d_ref, l_ref):
        i = pl.program_id(0)
        err = h_ref[...] - t_ref[...]
        lat = (i >= nct).astype(f32)
        d_ref[...] = err * (lat / D)
        l_ref[...] = jnp.zeros((SUB, LANE), f32) + lat * 0.5 * jnp.sum(err * err) / D

    return pl.pallas_call(
        body, name=name, grid=(nt,),
        in_specs=[pl.BlockSpec((tm, D), lambda i: (i, 0)),
                  pl.BlockSpec((tm, D), lambda i: (jnp.maximum(i - nct, 0), 0))],
        out_specs=[pl.BlockSpec((tm, D), lambda i: (i, 0)), pl.BlockSpec((None, SUB, LANE), lambda i: (i, 0, 0))],
        out_shape=[jax.ShapeDtypeStruct((T, D), f32), jax.ShapeDtypeStruct((nt, SUB, LANE), f32)],
        compiler_params=_cparams(("parallel",)),
    )(h, tgt)


def _rope_tables(tc, tl):
    t = jnp.arange(tl)
    inv = ROPE_THETA ** (-jnp.arange(0, HD // 2, 2, dtype=f32) / (HD // 2))
    ang = jnp.concatenate([(t // GRID_W).astype(f32)[:, None] * inv, (t % GRID_W).astype(f32)[:, None] * inv], -1)
    cos, sin = jnp.repeat(jnp.cos(ang), 2, axis=1), jnp.repeat(jnp.sin(ang), 2, axis=1)
    even = (jnp.arange(HD) % 2 == 0)[None, :]
    cos_f = jnp.concatenate([jnp.ones((tc, HD), f32), cos], 0)
    sin_a = jnp.concatenate([jnp.zeros((tc, HD), f32), jnp.where(even, -sin, 0.0)], 0)
    sin_b = jnp.concatenate([jnp.zeros((tc, HD), f32), jnp.where(even, 0.0, sin)], 0)
    return cos_f, sin_a, sin_b


N_CHIPS = 4
SHARD = N_IN // N_CHIPS


def _group_ranges():
    return dict(M=[(S_MG, 4 * D), (S_GA, BRW), (S_GB, BRW), (S_GC, BRW), (S_GD, BRW)], A=[(S_Q, W_A)],
                C=[(S_B, 3 * BRW), (S_DA, 2 * BRW)], G=[(S_CQ, 2 * C_KW + BRW), (S_R, 2 * C_RANK)])


def _group_weights(w4):
    out = {}
    for k, ranges in _group_ranges().items():
        parts = []
        for a, n in ranges:
            while n > 0:
                s, r = divmod(a, SHARD)
                m = min(n, SHARD - r)
                parts.append(w4[s, r:r + m])
                a, n = a + m, n - m
        if k == "G":
            parts.append(jnp.zeros((LANE - 2 * C_RANK, D), w4.dtype))
        out[k] = jnp.concatenate(parts, 0)
    return out


def _ungroup(g):
    secs = []
    for k, ranges in _group_ranges().items():
        off = 0
        for a, n in ranges:
            secs.append((a, g[k][off:off + n]))
            off += n
    return jnp.concatenate([v for _, v in sorted(secs, key=lambda t: t[0])], 0)


PROJ_TN = dict(M=2048, A=1024, C=1280, G=1152)
DU_TK = dict(M=2048, A=1024, C=BRW, G=1152)
DWP_TN = dict(M=768, A=1024, C=BRW, G=1152)


def _gate_weights(w2_l, gb_l):
    w = jnp.zeros((LANE, 2 * C_KW), f32)
    w = w.at[0:C_RANK, 0:C_KW].set(w2_l[0]).at[C_RANK:2 * C_RANK, C_KW:2 * C_KW].set(w2_l[1])
    return w, jnp.concatenate([gb_l[0], gb_l[1]])[None, :]


def _local_step(x1, c1, ctx1, tgt1, c_ctx, w_mod, b_mod, weights_of, q_norm, k_norm, b_conv, w2, gb, c_norm, d_conv_w,
                d_conv_b, d_norm_g, d_norm_b, grads_done, ln_g, ln_b, tm, token=None):
    tc, tl = ctx1.shape[0], x1.shape[0]
    T = tc + tl
    rc = min(256, tc)
    tmb = tm // 2
    tmm = 768 if T % 768 == 0 else tm
    rope = _rope_tables(tc, tl)
    cin = jnp.concatenate([c1, c_ctx[None, :], jnp.zeros((SUB - 2, D), f32)], 0)
    if token is not None:
        cin = cin + token[:, 0:1]
    modv = _mod_fwd(cin, w_mod, b_mod)
    modv = [modv[l] for l in range(DEPTH)]
    row = lambda v: v[None, :]

    h = jnp.concatenate([ctx1, x1], 0)
    saved, wp, w_br, w_out = [], [None] * DEPTH, [None] * DEPTH, [None] * DEPTH
    for l in range(DEPTH):
        wp[l], merge_weights = weights_of(l, h)
        u = _ln_fwd(h, modv[l], tc, tm, f"ln_fwd{l}")
        P = {k: _matmul(u, wp[l][k], "nt", tmm, PROJ_TN[k], D, f"proj{l}{k}", out_dtype=bf16) for k in GROUPS}
        qn, kn, vb = _prep_fwd(P["A"], row(q_norm[l]), row(k_norm[l]), rope, tm, f"prep_fwd{l}")
        ya = _attn_fwd(qn, kn, vb, tc, tm, f"attn_fwd{l}")
        yb, hh = _conv_fwd(P["C"], b_conv[l], d_conv_w[l], row(d_conv_b[l]), tc, tl, rc, f"conv_fwd{l}")
        w2p, b2p = _gate_weights(w2[l], gb[l])
        gla = _gla_fwd(P["G"], w2p, b2p, tc, f"gla_fwd{l}")
        o2, ssave = gla[:2], gla[2:]
        w_br[l], w_out[l] = merge_weights(o2[0])
        hn = _merge_fwd(h, modv[l], ya, yb, o2, hh, P["M"], row(c_norm[l]), row(d_norm_g[l]), row(d_norm_b[l]),
                        row(ln_g[l]), row(ln_b[l]), w_br[l], w_out[l], tc, tm, f"merge_fwd{l}")
        saved.append((h, u, P, qn, kn, vb, ya, yb, hh, o2, ssave, w2p, b2p))
        h = hn

    dh, lparts = _loss_kernel(h, tgt1, tc, tm, "loss")
    loss = jnp.sum(lparts[:, 0, 0])

    g = {k: [None] * DEPTH for k in ("wp", "q_norm", "k_norm", "b_conv", "w2", "gb", "c_norm", "d_conv_w", "d_conv_b",
                                     "d_norm_g", "d_norm_b", "w_br", "w_out", "ln_g", "ln_b", "modv")}
    for l in reversed(range(DEPTH)):
        h_in, u, P, qn, kn, vb, ya, yb, hh, o2, ssave, w2p, b2p = saved[l]
        dP = {}
        (dh_res, dm_mg, dya, dyb, doc, dhh, dP["M"], br, z, acc, dy, dv5, dvd) = _merge_bwd(
            dh, h_in, modv[l], ya, yb, o2, hh, P["M"], row(c_norm[l]), row(d_norm_g[l]), row(d_norm_b[l]),
            row(ln_g[l]), row(ln_b[l]), w_br[l], w_out[l], tc, tmb, f"merge_bwd{l}")
        g["w_br"][l] = _matmul_tn_batched(br, z, N_CHIPS, f"dwbr{l}")
        g["w_out"][l] = _matmul(acc, dy, "tn", D, D, T, f"dwout{l}")
        tk = grads_done(l, {k: g[k][l] for k in ("w_br", "w_out")})
        qg_l = row(q_norm[l]) if tk is None else row(q_norm[l]) + tk[0:1, :]
        v5 = jnp.sum(dv5, 0)
        g["c_norm"][l], g["d_norm_g"][l], g["d_norm_b"][l] = v5[0], v5[1], v5[2]
        vd = jnp.sum(dvd, 0)
        g["ln_g"][l], g["ln_b"][l] = vd[0], vd[1]
        dqn, dkn, dv = _attn_bwd(qn, kn, vb, dya, tc, tm, f"attn_bwd{l}")
        dP["A"], dqk = _prep_bwd(P["A"], dqn, dkn, dv, qg_l, row(k_norm[l]), rope, tm, f"prep_bwd{l}")
        dqk = jnp.sum(dqk, 0)
        g["q_norm"][l], g["k_norm"][l] = dqk[0], dqk[1]
        dP["C"], dwb, dwd, dbd = _conv_bwd(P["C"], dyb, dhh, b_conv[l], d_conv_w[l], tc, tl, rc, f"conv_bwd{l}")
        g["b_conv"][l], g["d_conv_w"][l], g["d_conv_b"][l] = dwb, dwd, dbd[0]
        dpf, dpb, dw2p, db2p = _gla_bwd(P["G"], w2p, b2p, ssave, doc, tc, f"gla_bwd{l}")
        dP["G"] = _sum_dirs(dpf, dpb, tm, f"gla_sum{l}")
        db2p = db2p[0]
        g["w2"][l] = jnp.stack([dw2p[0:C_RANK, 0:C_KW], dw2p[C_RANK:2 * C_RANK, C_KW:2 * C_KW]])
        g["gb"][l] = jnp.stack([db2p[0:C_KW], db2p[C_KW:2 * C_KW]])
        g["wp"][l] = {k: _matmul(dP[k], u, "tn", DWP_TN[k], D, T, f"dwp{l}{k}") for k in GROUPS}
        tk = grads_done(l, {"wp": g["wp"][l]})
        du = _matmul_groups(dP, wp[l], DU_TK, tmm, f"du{l}", after=tk)
        dh, dm_ln = _ln_bwd(du, h_in, dh_res, modv[l], tc, tm, f"ln_bwd{l}")
        g["modv"][l] = jnp.sum(dm_mg, 0) + jnp.sum(dm_ln, 0)

    dmodv = jnp.stack(g.pop("modv"))
    g["w_mod"], dcin = _mod_bwd(cin, w_mod, dmodv)
    g["b_mod"] = dmodv[:, 0, :] + dmodv[:, 1, :]
    g["c_ctx"] = jnp.sum(dcin, (0, 1))[1]
    return loss, dh[tc:], g


HALF_TL = 256


def _adamw(w, g, m, v, name, tr=128, after=None):
    L, R, C = w.shape
    if R % tr == 0:
        grid, spec = (L, R // tr), pl.BlockSpec((None, tr, C), lambda l, i: (l, i, 0))
    elif R * C * 4 <= (1 << 20):
        grid, spec = (L, 1), pl.BlockSpec((None, R, C), lambda l, i: (l, 0, 0))
    else:
        grid, spec = (L, C // HALF_TL), pl.BlockSpec((None, R, HALF_TL), lambda l, i: (l, 0, i))

    def body(w_ref, g_ref, m_ref, v_ref, *rest):
        d_ref, nm_ref, nv_ref = rest[-3:]
        gg = g_ref[...]
        nm = B1 * m_ref[...] + (1.0 - B1) * gg
        nv = B2 * v_ref[...] + (1.0 - B2) * (gg * gg)
        m_hat = nm / (1.0 - B1 ** STEP)
        v_hat = nv / (1.0 - B2 ** STEP)
        d_ref[...] = -LR * (m_hat / (jnp.sqrt(v_hat) + AEPS) + WD * w_ref[...])
        nm_ref[...] = nm
        nv_ref[...] = nv

    return pl.pallas_call(
        body, name=name, grid=grid, in_specs=[spec] * 4 + ([] if after is None else [pl.BlockSpec(memory_space=pl.ANY)]),
        out_specs=[spec] * 3, out_shape=[jax.ShapeDtypeStruct((L, R, C), f32)] * 3,
        compiler_params=_cparams(("parallel", "parallel")),
    )(w, g, m, v, *([] if after is None else [after]))


MESH = pl.DeviceIdType.MESH
ANY = pl.BlockSpec(memory_space=pl.ANY)
N_CHIPS = 4


def _place():
    x, y, c = lax.axis_index("x"), lax.axis_index("y"), lax.axis_index("c")
    chips = [(1 - x, y), (x, 1 - y), (1 - x, 1 - y)]
    return x, y, c, chips


def _half(ref, c, axis):
    n = ref.shape[axis] // 2
    last = axis in (-1, ref.ndim - 1)
    idx = [slice(None)] * ref.ndim
    idx[axis] = pl.ds(pl.multiple_of(c * n, LANE if last else SUB), n)
    return ref.at[tuple(idx)]


def _half_shape(shape, axis):
    s = list(shape)
    s[axis] //= 2
    return tuple(s)


def _all_gather(arrs, axes, name):
    n = len(arrs)

    def body(*refs):
        ins, outs = refs[:n], refs[n:2 * n]
        send, recv = refs[2 * n:]
        x, y, c, chips = _place()
        me, sib = 2 * x + y, (x, y, 1 - c)

        def copy(a, k, chip_idx, cc, to, src=None):
            blk = _half(outs[a].at[chip_idx], cc, axes[a])
            return pltpu.make_async_remote_copy(src_ref=blk if src is None else src, dst_ref=blk,
                                                send_sem=send.at[7 * a + k], recv_sem=recv.at[7 * a + k],
                                                device_id=to, device_id_type=MESH)

        own = [pltpu.make_async_remote_copy(src_ref=ins[a], dst_ref=outs[a].at[me], send_sem=send.at[7 * a + 6],
                                            recv_sem=recv.at[7 * a + 6], device_id=sib, device_id_type=MESH)
               for a in range(n)]
        first = own + [copy(a, j, me, c, (*chip, c), src=_half(ins[a], c, axes[a]))
                       for a in range(n) for j, chip in enumerate(chips)]
        for cp in first:
            cp.start()
        passed = []
        for a in range(n):
            for j, chip in enumerate(chips):
                k = 2 * chip[0] + chip[1]
                copy(a, j, k, c, sib).wait_recv()
                fwd = copy(a, 3 + j, k, c, sib)
                fwd.start()
                passed.append(fwd)
        for a in range(n):
            own[a].wait_recv()
            for j, chip in enumerate(chips):
                copy(a, 3 + j, 2 * chip[0] + chip[1], 1 - c, sib).wait_recv()
        for cp in first + passed:
            cp.wait_send()

    return pl.pallas_call(
        body, name=name, in_specs=[ANY] * n, out_specs=[ANY] * n,
        out_shape=[jax.ShapeDtypeStruct((N_CHIPS,) + a.shape, a.dtype) for a in arrs],
        scratch_shapes=[pltpu.SemaphoreType.DMA((7 * n,)), pltpu.SemaphoreType.DMA((7 * n,))],
    )(*arrs)


def _sibling_halves(arrs, axes, name):
    n = len(arrs)

    def body(*refs):
        ins, outs = refs[:n], refs[n:2 * n]
        send, recv = refs[2 * n:]
        x, y, c, _ = _place()
        cps = [pltpu.make_async_remote_copy(src_ref=_half(ins[a], 1 - c, axes[a] + 1), dst_ref=outs[a], send_sem=send.at[a],
                                            recv_sem=recv.at[a], device_id=(x, y, 1 - c), device_id_type=MESH)
               for a in range(n)]
        for cp in cps:
            cp.start()
        for cp in cps:
            cp.wait()

    return pl.pallas_call(
        body, name=name, in_specs=[ANY] * n, out_specs=[ANY] * n,
        out_shape=[jax.ShapeDtypeStruct(_half_shape(a.shape, axes[i] + 1), a.dtype) for i, a in enumerate(arrs)],
        scratch_shapes=[pltpu.SemaphoreType.DMA((n,)), pltpu.SemaphoreType.DMA((n,))],
    )(*arrs)


def _add_half(gfull, land, cidx, axis, name, tr=128, out_dtype=bf16):
    _, hr, hc = land.shape
    if axis == 0:
        tr = min(tr, hr)
        nb, blk = hr // tr, (None, tr, hc)
        g_spec = pl.BlockSpec(blk, lambda s, i, cr: (s, cr[0] * nb + i, 0))
        l_spec = pl.BlockSpec(blk, lambda s, i, cr: (s, i, 0))
    else:
        nb, blk = hc // HALF_TL, (None, hr, HALF_TL)
        g_spec = pl.BlockSpec(blk, lambda s, i, cr: (s, 0, cr[0] * nb + i))
        l_spec = pl.BlockSpec(blk, lambda s, i, cr: (s, 0, i))

    def body(c_ref, g_ref, l_ref, o_ref):
        o_ref[...] = (g_ref[...].astype(f32) + l_ref[...].astype(f32)).astype(o_ref.dtype)

    return pl.pallas_call(
        body, name=name,
        grid_spec=pltpu.PrefetchScalarGridSpec(
            num_scalar_prefetch=1, grid=(N_CHIPS, nb), in_specs=[g_spec, l_spec], out_specs=l_spec),
        out_shape=jax.ShapeDtypeStruct((N_CHIPS, hr, hc), out_dtype),
        compiler_params=_cparams(("parallel", "parallel")),
    )(cidx, gfull, land)


def _chip_exchange(arrs, name):
    n = len(arrs)

    def body(*refs):
        ins, outs = refs[:n], refs[n:2 * n]
        send, recv = refs[2 * n:]
        x, y, c, chips = _place()
        me = 2 * x + y
        cps = []
        for a in range(n):
            for j, chip in enumerate(chips):
                k = 2 * chip[0] + chip[1]
                cps.append((pltpu.make_async_remote_copy(
                    src_ref=ins[a].at[k], dst_ref=outs[a].at[me], send_sem=send.at[3 * a + j], recv_sem=recv.at[3 * a + j],
                    device_id=(*chip, c), device_id_type=MESH), a, j, k))
        for cp, *_ in cps:
            cp.start()
        for cp, a, j, k in cps:
            pltpu.make_async_remote_copy(src_ref=ins[a].at[k], dst_ref=outs[a].at[k], send_sem=send.at[3 * a + j],
                                         recv_sem=recv.at[3 * a + j], device_id=(x, y, c), device_id_type=MESH).wait_recv()
        for cp, *_ in cps:
            cp.wait_send()

    return pl.pallas_call(
        body, name=name, in_specs=[ANY] * n, out_specs=[ANY] * n,
        out_shape=[jax.ShapeDtypeStruct(a.shape, a.dtype) for a in arrs],
        scratch_shapes=[pltpu.SemaphoreType.DMA((3 * n,)), pltpu.SemaphoreType.DMA((3 * n,))],
    )(*arrs)


def _sum_chips(land, own, place, axis, layer, into, name, tr=128):
    _, hr, hc = land.shape
    fresh = not hasattr(into, "dtype")
    shape = tuple(into) if fresh else into.shape
    if axis == 0:
        tr = min(tr, hr)
        nb, blk = hr // tr, (tr, hc)
        l_map, m_map = (lambda i, p: (0, i, 0)), (lambda i, p: (p[0], i, 0))
        o_map = lambda i, p: (layer, p[1] * nb + i, 0)
    else:
        nb, blk = hc // HALF_TL, (hr, HALF_TL)
        l_map, m_map = (lambda i, p: (0, 0, i)), (lambda i, p: (p[0], 0, i))
        o_map = lambda i, p: (layer, 0, p[1] * nb + i)

    def body(p_ref, l_ref, o_ref, *rest):
        me = p_ref[0]
        mine = o_ref[...].astype(f32)
        acc = None
        for k in range(N_CHIPS):
            t = jnp.where(me == k, mine, l_ref[k].astype(f32))
            acc = t if acc is None else acc + t
        rest[-1][...] = acc

    return pl.pallas_call(
        body, name=name,
        grid_spec=pltpu.PrefetchScalarGridSpec(
            num_scalar_prefetch=1, grid=(nb,),
            in_specs=[pl.BlockSpec((N_CHIPS,) + blk, l_map), pl.BlockSpec((None,) + blk, m_map)] + ([] if fresh else [ANY]),
            out_specs=pl.BlockSpec((None,) + blk, o_map)),
        out_shape=jax.ShapeDtypeStruct(shape, f32),
        input_output_aliases={} if fresh else {3: 0},
        compiler_params=_cparams(("parallel",)),
    )(place, land, own, *([] if fresh else [into]))


def _sibling_fill(arrs, axes, name):
    n = len(arrs)

    def body(*refs):
        outs = refs[n:2 * n]
        send, recv = refs[2 * n:]
        x, y, c, _ = _place()
        cps = [pltpu.make_async_remote_copy(src_ref=_half(outs[a], c, axes[a] + 1), dst_ref=_half(outs[a], c, axes[a] + 1),
                                            send_sem=send.at[a], recv_sem=recv.at[a], device_id=(x, y, 1 - c),
                                            device_id_type=MESH) for a in range(n)]
        for cp in cps:
            cp.start()
        for a in range(n):
            blk = _half(outs[a], 1 - c, axes[a] + 1)
            pltpu.make_async_remote_copy(src_ref=blk, dst_ref=blk, send_sem=send.at[a], recv_sem=recv.at[a],
                                         device_id=(x, y, 1 - c), device_id_type=MESH).wait_recv()
        for cp in cps:
            cp.wait_send()

    return pl.pallas_call(
        body, name=name, in_specs=[ANY] * n, out_specs=[ANY] * n,
        out_shape=[jax.ShapeDtypeStruct(a.shape, a.dtype) for a in arrs],
        input_output_aliases={a: a for a in range(n)},
        scratch_shapes=[pltpu.SemaphoreType.DMA((n,)), pltpu.SemaphoreType.DMA((n,))],
    )(*arrs)


HBM = pl.BlockSpec(memory_space=pltpu.HBM)
SEM = pl.BlockSpec(memory_space=pltpu.SEMAPHORE)
EFFECT = pltpu.SideEffectType.DATAFLOW_SIDE_EFFECTING
PEERS = 4


def _split_copies(srcs, lands, send, recv, gather):
    x, y, c, chips = _place()
    me = 2 * x + y
    peers = [((*chip, c), 2 * chip[0] + chip[1]) for chip in chips] + ([((x, y, 1 - c), me)] if gather else [])
    out = []
    for a in range(len(srcs)):
        for j, (dev, k) in enumerate(peers):
            src = srcs[a] if gather else srcs[a].at[k]
            sems = dict(send_sem=send.at[PEERS * a + j], recv_sem=recv.at[PEERS * a + j], device_id=dev, device_id_type=MESH)
            out.append((pltpu.make_async_remote_copy(src_ref=src, dst_ref=lands[a].at[me], **sems),
                        pltpu.make_async_remote_copy(src_ref=src, dst_ref=lands[a].at[k], **sems)))
    return out


def _split_start(srcs, gather, after, name):
    n = len(srcs)
    lands = [lax.empty(((N_CHIPS,) + s.shape) if gather else s.shape, s.dtype) for s in srcs]

    def body(*refs):
        send, recv = refs[2 * n + 1], refs[2 * n + 2]
        for start, _ in _split_copies(refs[:n], refs[n:2 * n], send, recv, gather):
            start.start()
        refs[-1][...] = jnp.zeros_like(refs[-1])

    sems = pltpu.SemaphoreType.DMA((PEERS * n,))
    hbm = lambda a: pltpu.with_memory_space_constraint(a, pltpu.HBM)
    out = pl.pallas_call(
        body, name=name,
        out_shape=(sems, sems, *[pltpu.HBM(a.shape, a.dtype) for a in srcs + lands], jax.ShapeDtypeStruct((SUB, LANE), f32)),
        in_specs=[HBM] * (2 * n) + [ANY], out_specs=(SEM, SEM, *[HBM] * (2 * n), pl.BlockSpec(memory_space=pltpu.VMEM)),
        input_output_aliases={i: 2 + i for i in range(2 * n)},
        compiler_params=pltpu.CompilerParams(has_side_effects=EFFECT),
    )(*[hbm(a) for a in srcs + lands], after)
    return out[0], out[1], list(out[2:2 + n]), list(out[2 + n:2 + 2 * n]), out[-1]


def _split_wait(send, recv, srcs, lands, gather, after, name):
    n = len(srcs)

    def body(*refs):
        for start, arrival in _split_copies(refs[:n], refs[n:2 * n], refs[2 * n], refs[2 * n + 1], gather):
            start.wait_send()
            arrival.wait_recv()

    out = pl.pallas_call(
        body, name=name, out_shape=[pltpu.HBM(a.shape, a.dtype) for a in srcs + lands],
        in_specs=[HBM] * (2 * n) + [SEM, SEM, ANY], out_specs=[HBM] * (2 * n),
        input_output_aliases={i: i for i in range(2 * n)},
        compiler_params=pltpu.CompilerParams(has_side_effects=EFFECT),
    )(*srcs, *lands, send, recv, after)
    return list(out[:n]), list(out[n:])


N_DEV = 8


def _all_reduce_small(v, name):
    R = v.shape[0]

    def body(v_ref, o_ref, land_ref, send, recv):
        x, y, c, _ = _place()
        me = 4 * x + 2 * y + c
        land_ref[me] = v_ref[...]
        cps = []
        for m in range(1, N_DEV):
            px, py, pc = [(1 - q) if (m >> s) & 1 else q for q, s in ((x, 2), (y, 1), (c, 0))]
            cps.append((pltpu.make_async_remote_copy(src_ref=v_ref, dst_ref=land_ref.at[me], send_sem=send.at[m - 1],
                                                     recv_sem=recv.at[m - 1], device_id=(px, py, pc), device_id_type=MESH),
                        4 * px + 2 * py + pc, m))
        for cp, *_ in cps:
            cp.start()
        for cp, peer, m in cps:
            pltpu.make_async_remote_copy(src_ref=v_ref, dst_ref=land_ref.at[peer], send_sem=send.at[m - 1],
                                         recv_sem=recv.at[m - 1], device_id=(x, y, c), device_id_type=MESH).wait_recv()
        for cp, *_ in cps:
            cp.wait_send()
        acc = land_ref[0]
        for k in range(1, N_DEV):
            acc = acc + land_ref[k]
        o_ref[...] = acc

    vm = pl.BlockSpec(memory_space=pltpu.VMEM)
    return pl.pallas_call(
        body, name=name, in_specs=[vm], out_specs=vm, out_shape=jax.ShapeDtypeStruct(v.shape, f32),
        scratch_shapes=[pltpu.VMEM((N_DEV, R, LANE), f32), pltpu.SemaphoreType.DMA((N_DEV - 1,)),
                        pltpu.SemaphoreType.DMA((N_DEV - 1,))],
        compiler_params=pltpu.CompilerParams(vmem_limit_bytes=VMEM_LIMIT),
    )(v)


def _pack_small(arrs, mult=2 * SUB):
    flat = jnp.concatenate([a.reshape(-1) for a in arrs])
    rows = -(-flat.shape[0] // (LANE * mult)) * mult
    return jnp.pad(flat, (0, rows * LANE - flat.shape[0])).reshape(rows, LANE)


def _unpack_small(vec, shapes):
    flat, out, o = vec.reshape(-1), [], 0
    for s in shapes:
        n = int(np.prod(s))
        out.append(flat[o:o + n].reshape(s))
        o += n
    return out


REPL_SMALL = ("c_ctx", "b_mod", "q_norm", "k_norm", "c_norm", "d_conv_b", "d_norm_g", "d_norm_b", "ln_g", "ln_b")
SHARD_SMALL = ("b_conv", "c_gate_w2", "c_gate_b", "d_conv_w")
BIG = ("w_mod", "w_in", "w_br", "w_out")
ORDER = ("c_ctx", "w_mod", "b_mod", "w_in", "q_norm", "k_norm", "b_conv", "c_gate_w2", "c_gate_b", "c_norm", "d_conv_w",
         "d_conv_b", "d_norm_g", "d_norm_b", "w_br", "w_out", "ln_g", "ln_b")


def _unshard_last(g4, shard_shape):
    g = g4.reshape((N_CHIPS,) + tuple(shard_shape))
    g = jnp.moveaxis(g, 0, -2)
    return g.reshape(tuple(shard_shape[:-1]) + (N_CHIPS * shard_shape[-1],))


def _pieces_last(full):
    w = full.shape[-1] // N_CHIPS
    g = full.reshape(full.shape[:-1] + (N_CHIPS, w))
    return jnp.moveaxis(g, -2, 0).reshape(N_CHIPS, -1, w)


def kernel(x, c, ctx, c_ctx, w_mod, b_mod, w_in, q_norm, k_norm, b_conv, c_gate_w2, c_gate_b, c_norm, d_conv_w, d_conv_b, d_norm_g, d_norm_b, w_br, w_out, ln_g, ln_b, loss_target, m_c_ctx, m_w_mod, m_b_mod, m_w_in, m_q_norm, m_k_norm, m_b_conv, m_c_gate_w2, m_c_gate_b, m_c_norm, m_d_conv_w, m_d_conv_b, m_d_norm_g, m_d_norm_b, m_w_br, m_w_out, m_ln_g, m_ln_b, v_c_ctx, v_w_mod, v_b_mod, v_w_in, v_q_norm, v_k_norm, v_b_conv, v_c_gate_w2, v_c_gate_b, v_c_norm, v_d_conv_w, v_d_conv_b, v_d_norm_g, v_d_norm_b, v_w_br, v_w_out, v_ln_g, v_ln_b):
    W = dict(c_ctx=c_ctx, w_mod=w_mod, b_mod=b_mod, w_in=w_in, q_norm=q_norm, k_norm=k_norm, b_conv=b_conv,
             c_gate_w2=c_gate_w2, c_gate_b=c_gate_b, c_norm=c_norm, d_conv_w=d_conv_w, d_conv_b=d_conv_b,
             d_norm_g=d_norm_g, d_norm_b=d_norm_b, w_br=w_br, w_out=w_out, ln_g=ln_g, ln_b=ln_b)
    M = dict(c_ctx=m_c_ctx, w_mod=m_w_mod, b_mod=m_b_mod, w_in=m_w_in, q_norm=m_q_norm, k_norm=m_k_norm, b_conv=m_b_conv,
             c_gate_w2=m_c_gate_w2, c_gate_b=m_c_gate_b, c_norm=m_c_norm, d_conv_w=m_d_conv_w, d_conv_b=m_d_conv_b,
             d_norm_g=m_d_norm_g, d_norm_b=m_d_norm_b, w_br=m_w_br, w_out=m_w_out, ln_g=m_ln_g, ln_b=m_ln_b)
    V = dict(c_ctx=v_c_ctx, w_mod=v_w_mod, b_mod=v_b_mod, w_in=v_w_in, q_norm=v_q_norm, k_norm=v_k_norm, b_conv=v_b_conv,
             c_gate_w2=v_c_gate_w2, c_gate_b=v_c_gate_b, c_norm=v_c_norm, d_conv_w=v_d_conv_w, d_conv_b=v_d_conv_b,
             d_norm_g=v_d_norm_g, d_norm_b=v_d_norm_b, w_br=v_w_br, w_out=v_w_out, ln_g=v_ln_g, ln_b=v_ln_b)
    chip = 2 * lax.axis_index("x") + lax.axis_index("y")
    cidx = lax.axis_index("c").astype(jnp.int32).reshape(1)

    place = jnp.stack([chip, lax.axis_index("c")]).astype(jnp.int32)

    AXIS = dict(w_in=1, w_mod=0, w_br=0, w_out=0)
    ex = dict(w_in=lambda a: jnp.swapaxes(a, 1, 2), w_mod=lambda a: a.reshape(1, DEPTH * D, -1),
              w_br=lambda a: a.reshape(DEPTH, 4 * BRW, -1), w_out=lambda a: a)
    Wx, Mx, Vx = ({k: ex[k](P_[k]) for k in BIG} for P_ in (W, M, V))

    LAYER, MERGE = ("w_in", "w_br", "w_out"), ("w_br", "w_out")
    small_shard = _pack_small([W[k] for k in SHARD_SMALL])
    keys0 = ("w_in", "w_mod")
    got = _all_gather([Wx[k][0].astype(bf16) for k in keys0] + [small_shard], [AXIS[k] for k in keys0] + [0], "all_gather0")
    smalls = [_unpack_small(got[-1][s], [W[k].shape for k in SHARD_SMALL]) for s in range(N_CHIPS)]
    full = {k: jnp.concatenate([smalls[s][i] for s in range(N_CHIPS)], axis=-1) for i, k in enumerate(SHARD_SMALL)}
    wmod = got[1].reshape(N_CHIPS, DEPTH, D, 3 * D // N_CHIPS)
    ag0b = _split_start([Wx[k][0].astype(bf16) for k in MERGE], True, got[0], "all_gather0b_start")
    ag1 = _split_start([Wx[k][1].astype(bf16) for k in LAYER], True, ag0b[4], "all_gather1_start")

    def merge_form(w_br4, w_out4):
        return jnp.moveaxis(w_br4.reshape(N_CHIPS, 4, BRW, D // N_CHIPS), 0, 2).reshape(4, BRW, D), w_out4.reshape(D, D)

    def weights_of(l, h):
        if l == 0:
            return _group_weights(got[0]), lambda after: merge_form(*_split_wait(*ag0b[:4], True, after, "all_gather0b_wait")[1])
        g3 = _split_wait(*ag1[:4], True, h, "all_gather1_wait")[1]
        return _group_weights(g3[0]), lambda after: merge_form(g3[1], g3[2])

    red = {k: Wx[k].shape for k in BIG}
    flights, held = {}, {}

    def launch(tag, l, pieces, after=None):
        keys = list(pieces)
        land_a = _sibling_halves([pieces[k] for k in keys], [AXIS[k] for k in keys], f"rs_sibling_halves{tag}")
        pair = [_add_half(pieces[k], la, cidx, AXIS[k], f"rs_pair_sum{tag}_{k}") for k, la in zip(keys, land_a)]
        after = jnp.zeros((SUB, LANE), f32) if after is None else after
        flights[tag] = (l, keys, _split_start(pair, False, after, f"rs_chip_exchange{tag}_start"))
        return flights[tag][2][4]

    def land(tag, after):
        l, keys, flight = flights.pop(tag)
        pair, land_b = _split_wait(*flight[:4], False, after, f"rs_chip_exchange{tag}_wait")
        for k, lb, pr in zip(keys, land_b, pair):
            red[k] = _sum_chips(lb, pr, place, AXIS[k], l, red[k], f"rs_chip_sum{tag}_{k}")

    def grads_done(l, gl):
        if "wp" in gl:
            pieces = dict(w_in=_ungroup(gl["wp"]).astype(bf16).reshape(N_CHIPS, SHARD, D))
            return launch("0c", 0, pieces) if l == 0 else launch("1", 1, {**pieces, **held.pop(1)})
        pieces = dict(w_br=gl["w_br"].reshape(N_CHIPS, 4 * BRW, D // N_CHIPS), w_out=gl["w_out"].reshape(N_CHIPS, D // N_CHIPS, D))
        if l == 0:
            return launch("0b", 0, pieces)
        held[1] = pieces
        return None

    loss, gx, g = _local_step(
        x[0], c, ctx[0], loss_target[0], c_ctx, wmod, b_mod, weights_of, q_norm, k_norm, full["b_conv"],
        full["c_gate_w2"], full["c_gate_b"], c_norm, full["d_conv_w"], d_conv_b, d_norm_g, d_norm_b,
        grads_done, ln_g, ln_b, tm=256, token=ag1[4])
    g["c_gate_w2"], g["c_gate_b"] = g.pop("w2"), g.pop("gb")
    loss = lax.psum(loss, ("x", "y", "c"))

    w_mod_pieces = g["w_mod"].reshape(N_CHIPS, DEPTH * D, 3 * D // N_CHIPS)
    g = {k: (jnp.stack(v) if isinstance(v, list) else v) for k, v in g.items() if k not in ("wp", "w_br", "w_out", "w_mod")}

    small_names = REPL_SMALL + SHARD_SMALL
    gs = _all_reduce_small(_pack_small([g[k] for k in small_names]), "all_reduce_small")
    gsm = dict(zip(small_names, _unpack_small(gs, [g[k].shape for k in small_names])))
    for k in SHARD_SMALL:
        wdt = W[k].shape[-1]
        gsm[k] = lax.dynamic_slice_in_dim(gsm[k], chip * wdt, wdt, axis=gsm[k].ndim - 1)

    grad, delta, new_m, new_v = {}, {}, {}, {}

    def adamw_big(keys, after):
        filled = _sibling_fill([red[k] for k in keys], [AXIS[k] for k in keys], "rs_sibling_fill_" + keys[0])
        for k, r in zip(keys, filled):
            back = (lambda a: jnp.swapaxes(a, 1, 2)) if k == "w_in" else (lambda a: a.reshape(W[k].shape))
            d_, m_, v_ = _adamw(Wx[k], r, Mx[k], Vx[k], f"adamw_{k}", after=after)
            grad[k], delta[k], new_m[k], new_v[k] = back(r), back(d_), back(m_), back(v_)
        return d_

    token = launch("0d", 0, {"w_mod": w_mod_pieces}, after=gs)
    land("1", gx)
    land("0b", gx)
    last = adamw_big(MERGE, token)
    shapes = [W[k].shape for k in small_names]
    d_, m_, v_ = _adamw(*[_pack_small([P_[k] for k in small_names])[None] for P_ in (W, gsm, M, V)], "adamw_small", after=last)
    for k, dd, mm_, vv in zip(small_names, _unpack_small(d_, shapes), _unpack_small(m_, shapes), _unpack_small(v_, shapes)):
        grad[k], delta[k], new_m[k], new_v[k] = gsm[k], dd, mm_, vv
    land("0c", d_)
    land("0d", d_)
    adamw_big(("w_in", "w_mod"), None)

    return (loss, gx[None], *[grad[k] for k in ORDER], *[delta[k] for k in ORDER], *[new_m[k] for k in ORDER],
            *[new_v[k] for k in ORDER])
```

```python
import functools

import jax
import jax.numpy as jnp
import numpy as np
from jax import lax
from jax.experimental import pallas as pl
from jax.experimental.pallas import tpu as pltpu

f32 = jnp.float32
bf16 = jnp.bfloat16

D = 1024
DEPTH = 2
GRID_W = 64
BRW = 512
HD = 128
A_HEADS = 4
C_HEADS = 4
C_KW = 256
C_RANK = 16
C_TAU = 16.0
CH = 128
KB = 3
KD = 31
ALPHA = (2 * DEPTH) ** 0.25
EPS = 1e-6
ROPE_THETA = 10000.0
N_IN = 10784
LR, B1, B2, AEPS, WD, STEP = 0.001, 0.9, 0.999, 1e-08, 0.01, 10

W_M, W_A, W_C, W_G = 4 * D + 4 * BRW, 1024, 5 * BRW, 1152
GROUPS = ("M", "A", "C", "G")
GROUP_W = dict(M=W_M, A=W_A, C=W_C, G=W_G)
M_GA, M_GB, M_GC, M_GD = 4 * D, 4 * D + BRW, 4 * D + 2 * BRW, 4 * D + 3 * BRW
A_K, A_V = 512, 768
G_K, G_V, G_R = 256, 512, 1024
CT = 5 * 128
S_Q, S_GA, S_B, S_C, S_X, S_GB, S_CQ, S_CV, S_GC, S_R, S_DA, S_DG, S_GD, S_MG = (
    0, 1024, 1536, 2048, 2560, 3072, 3584, 4096, 4608, 5120, 5152, 5664, 6176, 6688)

LANE = 128
SUB = 8
VMEM_LIMIT = 56 * 1024 * 1024
CONV_PAD = 16
GLA_SUB = 16
GLA_CLAMP = 60.0


def _cparams(sem, vmem=VMEM_LIMIT):
    return pltpu.CompilerParams(dimension_semantics=sem, vmem_limit_bytes=vmem)


def _dg(a, b, ca, cb):
    return lax.dot_general(a.astype(bf16), b.astype(bf16), (((ca,), (cb,)), ((), ())),
                           preferred_element_type=f32)


@jax.custom_vjp
def mm(a, b):
    return _dg(a, b, 1, 0)


mm.defvjp(lambda a, b: (_dg(a, b, 1, 0), (a, b)),
          lambda r, ct: (_dg(ct, r[1], 1, 1).astype(r[0].dtype), _dg(r[0], ct, 0, 0).astype(r[1].dtype)))


@jax.custom_vjp
def mm_nt(a, b):
    return _dg(a, b, 1, 1)


mm_nt.defvjp(lambda a, b: (_dg(a, b, 1, 1), (a, b)),
             lambda r, ct: (_dg(ct, r[1], 1, 0).astype(r[0].dtype), _dg(ct, r[0], 0, 0).astype(r[1].dtype)))


@jax.custom_vjp
def mm_tn(a, b):
    return _dg(a, b, 0, 0)


mm_tn.defvjp(lambda a, b: (_dg(a, b, 0, 0), (a, b)),
             lambda r, ct: (_dg(r[1], ct, 1, 1).astype(r[0].dtype), _dg(r[0], ct, 1, 0).astype(r[1].dtype)))


def _sigmoid(x):
    return 0.5 * jnp.tanh(0.5 * x) + 0.5


def _silu(x):
    return x * _sigmoid(x)


def _ln(x):
    mu = jnp.mean(x, -1, keepdims=True)
    xc = x - mu
    var = jnp.mean(xc * xc, -1, keepdims=True)
    return xc * lax.rsqrt(var + EPS)


def _rms(x, g):
    return x * lax.rsqrt(jnp.mean(x * x, -1, keepdims=True) + EPS) * g


@jax.custom_vjp
def _rope(x, cos_f, sin_a, sin_b):
    return x * cos_f + pltpu.roll(x, HD - 1, 1) * sin_a + pltpu.roll(x, 1, 1) * sin_b


def _rope_fwd(x, cos_f, sin_a, sin_b):
    return _rope(x, cos_f, sin_a, sin_b), (cos_f, sin_a, sin_b)


def _rope_bwd(r, ct):
    cos_f, sin_a, sin_b = r
    dx = ct * cos_f + pltpu.roll(ct * sin_a, 1, 1) + pltpu.roll(ct * sin_b, HD - 1, 1)
    return dx, jnp.zeros_like(cos_f), jnp.zeros_like(sin_a), jnp.zeros_like(sin_b)


_rope.defvjp(_rope_fwd, _rope_bwd)


def _row_ids(i, tm):
    return i * tm + lax.broadcasted_iota(jnp.int32, (tm, 1), 0)


def _partial_rows(ref, rows):
    n = len(rows)
    for k, r in enumerate(rows):
        ref[k:k + 1, :] = r
    ref[n:SUB, :] = jnp.zeros((SUB - n, ref.shape[-1]), f32)


def _matmul(a, b, mode, tm, tn, tk, name, out_dtype=f32, add=None, after=None):
    sect = a.ndim == 3
    a2 = (a.shape[1], a.shape[0] * a.shape[2]) if sect else a.shape
    if mode == "nn":
        (M, K), N = a2, b.shape[1]
        a_spec = pl.BlockSpec((None, tm, tk), lambda j, i, k: (k, i, 0)) if sect else pl.BlockSpec((tm, tk), lambda j, i, k: (i, k))
        b_spec = pl.BlockSpec((tk, tn), lambda j, i, k: (k, j))
        ca, cb = 1, 0
        assert not sect or tk == a.shape[2]
    elif mode == "nt":
        (M, K), N = a2, b.shape[0]
        assert not sect
        a_spec = pl.BlockSpec((tm, tk), lambda j, i, k: (i, k))
        b_spec = pl.BlockSpec((tn, tk), lambda j, i, k: (j, k))
        ca, cb = 1, 1
    else:
        (K, M), N = a2, b.shape[1]
        a_spec = pl.BlockSpec((None, tk, tm), lambda j, i, k: (i, k, 0)) if sect else pl.BlockSpec((tk, tm), lambda j, i, k: (k, i))
        b_spec = pl.BlockSpec((tk, tn), lambda j, i, k: (k, j))
        ca, cb = 0, 0
        assert not sect or tm == a.shape[2]
    assert M % tm == 0 and N % tn == 0 and K % tk == 0, (name, M, N, K, tm, tn, tk)
    nk = K // tk

    o_spec = pl.BlockSpec((tm, tn), lambda j, i, k: (i, j))

    def body(a_ref, b_ref, *rest):
        add_ref = rest[0] if add is not None else None
        o_ref, acc_ref = rest[-2:]
        k = pl.program_id(2)
        part = _dg(a_ref[...], b_ref[...], ca, cb)

        @pl.when(k == 0)
        def _():
            acc_ref[...] = part if add_ref is None else part + add_ref[...]

        @pl.when(k > 0)
        def _():
            acc_ref[...] += part

        @pl.when(k == nk - 1)
        def _():
            o_ref[...] = acc_ref[...].astype(o_ref.dtype)

    extra = ([] if add is None else [(o_spec, add)]) + ([] if after is None else [(pl.BlockSpec(memory_space=pl.ANY), after)])
    return pl.pallas_call(
        body, name=name, grid=(N // tn, M // tm, nk),
        in_specs=[a_spec, b_spec] + [s_ for s_, _ in extra], out_specs=o_spec,
        out_shape=jax.ShapeDtypeStruct((M, N), out_dtype),
        scratch_shapes=[pltpu.VMEM((tm, tn), f32)],
        compiler_params=_cparams(("parallel", "parallel", "arbitrary")),
    )(a, b, *[v_ for _, v_ in extra])


def _matmul_groups(a, b, tks, tm, name, after=None):
    keys = list(a)
    M = a[keys[0]].shape[-2]
    N = b[keys[0]].shape[1]
    count = {g: b[g].shape[0] // tks[g] for g in keys}
    first, total = {}, 0
    for g in keys:
        first[g], total = total, total + count[g]

    def k_of(g):
        return lambda s: jnp.clip(s - first[g], 0, count[g] - 1)

    a_specs = [pl.BlockSpec((None, tm, tks[g]), functools.partial(lambda i, s, kk: (kk(s), i, 0), kk=k_of(g)))
               if a[g].ndim == 3 else pl.BlockSpec((tm, tks[g]), functools.partial(lambda i, s, kk: (i, kk(s)), kk=k_of(g)))
               for g in keys]
    b_specs = [pl.BlockSpec((tks[g], N), functools.partial(lambda i, s, kk: (kk(s), 0), kk=k_of(g))) for g in keys]
    n = len(keys)

    def body(*refs):
        o_ref, acc_ref = refs[-2:]
        s = pl.program_id(1)

        @pl.when(s == 0)
        def _():
            acc_ref[...] = jnp.zeros_like(acc_ref)

        for j, g in enumerate(keys):
            @pl.when((s >= first[g]) & (s < first[g] + count[g]))
            def _(j=j):
                acc_ref[...] += _dg(refs[j][...], refs[n + j][...], 1, 0)

        @pl.when(s == total - 1)
        def _():
            o_ref[...] = acc_ref[...]

    extra = [] if after is None else [after]
    return pl.pallas_call(
        body, name=name, grid=(M // tm, total),
        in_specs=a_specs + b_specs + [pl.BlockSpec(memory_space=pl.ANY)] * len(extra),
        out_specs=pl.BlockSpec((tm, N), lambda i, s: (i, 0)),
        out_shape=jax.ShapeDtypeStruct((M, N), f32),
        scratch_shapes=[pltpu.VMEM((tm, N), f32)],
        compiler_params=_cparams(("parallel", "arbitrary")),
    )(*[a[g] for g in keys], *[b[g] for g in keys], *extra)


def _matmul_tn_batched(a, b, ns, name):
    B, K, M = a.shape
    N = b.shape[2] // ns

    def body(a_ref, b_ref, o_ref):
        o_ref[...] = _dg(a_ref[...], b_ref[...], 0, 0).astype(bf16)

    return pl.pallas_call(
        body, name=name, grid=(B, ns),
        in_specs=[pl.BlockSpec((None, K, M), lambda i, s: (i, 0, 0)), pl.BlockSpec((None, K, N), lambda i, s: (i, 0, s))],
        out_specs=pl.BlockSpec((None, None, M, N), lambda i, s: (s, i, 0, 0)),
        out_shape=jax.ShapeDtypeStruct((ns, B, M, N), bf16),
        compiler_params=_cparams(("parallel", "parallel")),
    )(a, b)


MOD_TN = 768


def _mod_fwd(cin, w_mod, b_mod):
    def body(c_ref, w_ref, b_ref, o_ref):
        o_ref[...] = mm(_silu(c_ref[...]), w_ref[...]) + b_ref[...]

    return pl.pallas_call(
        body, name="mod_fwd", grid=(DEPTH, 3 * D // MOD_TN),
        in_specs=[pl.BlockSpec((SUB, D), lambda l, j: (0, 0)),
                  pl.BlockSpec((None, None, D, MOD_TN), lambda l, j: (j, l, 0, 0)),
                  pl.BlockSpec((None, 1, MOD_TN), lambda l, j: (l, 0, j))],
        out_specs=pl.BlockSpec((None, SUB, MOD_TN), lambda l, j: (l, 0, j)),
        out_shape=jax.ShapeDtypeStruct((DEPTH, SUB, 3 * D), f32),
        compiler_params=_cparams(("parallel", "parallel")),
    )(cin, w_mod, b_mod.reshape(DEPTH, 1, 3 * D))


def _mod_bwd(cin, w_mod, dmodv):
    nj = 3 * D // MOD_TN

    def body(c_ref, w_ref, g_ref, dw_ref, dc_ref):
        _, vjp = jax.vjp(lambda c, w: mm(_silu(c), w), c_ref[...], w_ref[...].astype(f32))
        dc, dw = vjp(g_ref[...])
        dw_ref[...] = dw.astype(bf16)
        dc_ref[...] = dc

    return pl.pallas_call(
        body, name="mod_bwd", grid=(DEPTH, nj),
        in_specs=[pl.BlockSpec((SUB, D), lambda l, j: (0, 0)),
                  pl.BlockSpec((None, None, D, MOD_TN), lambda l, j: (j, l, 0, 0)),
                  pl.BlockSpec((None, SUB, MOD_TN), lambda l, j: (l, 0, j))],
        out_specs=[pl.BlockSpec((None, None, D, MOD_TN), lambda l, j: (j, l, 0, 0)),
                   pl.BlockSpec((None, None, SUB, D), lambda l, j: (l, j, 0, 0))],
        out_shape=[jax.ShapeDtypeStruct((nj, DEPTH, D, MOD_TN), bf16),
                   jax.ShapeDtypeStruct((DEPTH, nj, SUB, D), f32)],
        compiler_params=_cparams(("parallel", "parallel")),
    )(cin, w_mod, dmodv)


def _u_fn(h, m_l, m_c, isctx):
    n = _ln(h)
    shift = jnp.where(isctx, m_c[:, 0:D], m_l[:, 0:D])
    scale = jnp.where(isctx, m_c[:, D:2 * D], m_l[:, D:2 * D])
    return n * (1.0 + scale) + shift


def _ln_fwd(h, modv_l, tc, tm, name):
    T = h.shape[0]

    def body(h_ref, m_ref, u_ref):
        isctx = _row_ids(pl.program_id(0), tm) < tc
        u_ref[...] = _u_fn(h_ref[...], m_ref[0:1, :], m_ref[1:2, :], isctx).astype(bf16)

    return pl.pallas_call(
        body, name=name, grid=(T // tm,),
        in_specs=[pl.BlockSpec((tm, D), lambda i: (i, 0)), pl.BlockSpec((SUB, 3 * D), lambda i: (0, 0))],
        out_specs=pl.BlockSpec((tm, D), lambda i: (i, 0)),
        out_shape=jax.ShapeDtypeStruct((T, D), bf16),
        compiler_params=_cparams(("parallel",)),
    )(h, modv_l)


def _ln_bwd(du, h, dh_res, modv_l, tc, tm, name, latent_only=False):
    T = h.shape[0]
    nt, nct = T // tm, tc // tm

    def body(du_ref, h_ref, r_ref, m_ref, dh_ref, dm_ref):
        isctx = _row_ids(pl.program_id(0), tm) < tc
        _, vjp = jax.vjp(lambda h, ml, mc: _u_fn(h, ml, mc, isctx), h_ref[...], m_ref[0:1, :], m_ref[1:2, :])
        dh, dml, dmc = vjp(du_ref[...])
        dh_ref[...] = dh + r_ref[...]
        _partial_rows(dm_ref, [dml, dmc])

    dh_map = (lambda i: (jnp.maximum(i - nct, 0), 0)) if latent_only else (lambda i: (i, 0))
    return pl.pallas_call(
        body, name=name, grid=(nt,),
        in_specs=[pl.BlockSpec((tm, D), lambda i: (i, 0)), pl.BlockSpec((tm, D), lambda i: (i, 0)),
                  pl.BlockSpec((tm, D), lambda i: (i, 0)), pl.BlockSpec((SUB, 3 * D), lambda i: (0, 0))],
        out_specs=[pl.BlockSpec((tm, D), dh_map), pl.BlockSpec((None, SUB, 3 * D), lambda i: (i, 0, 0))],
        out_shape=[jax.ShapeDtypeStruct((T - tc if latent_only else T, D), f32), jax.ShapeDtypeStruct((nt, SUB, 3 * D), f32)],
        compiler_params=_cparams(("arbitrary",)),
    )(du, h, dh_res, modv_l)


def _prep_fn(q, k, qg, kg, cos_f, sin_a, sin_b):
    qs = [_rope(_rms(q[:, HD * i:HD * (i + 1)], qg), cos_f, sin_a, sin_b) * (HD ** -0.5) for i in range(A_HEADS)]
    ks = [_rope(_rms(k[:, HD * i:HD * (i + 1)], kg), cos_f, sin_a, sin_b) for i in range(A_HEADS // 2)]
    return jnp.concatenate(qs, 1), jnp.concatenate(ks, 1)


def _tok(tm, w, off):
    return pl.BlockSpec((tm, w), lambda i: (i, off // w))


def _vec(w):
    return pl.BlockSpec((1, w), lambda i: (0, 0))


def _prep_fwd(P, qg, kg, rope, tm, name):
    T = P.shape[0]

    def body(q_ref, k_ref, v_ref, qg_ref, kg_ref, c_ref, sa_ref, sb_ref, qn_ref, kn_ref, vb_ref):
        qn, kn = _prep_fn(q_ref[...].astype(f32), k_ref[...].astype(f32), qg_ref[...], kg_ref[...], c_ref[...], sa_ref[...],
                          sb_ref[...])
        qn_ref[...] = qn.astype(bf16)
        kn_ref[...] = kn.astype(bf16)
        vb_ref[...] = v_ref[...].astype(bf16)

    return pl.pallas_call(
        body, name=name, grid=(T // tm,),
        in_specs=[_tok(tm, 512, 0), _tok(tm, 256, A_K), _tok(tm, 256, A_V), _vec(HD), _vec(HD),
                  _tok(tm, HD, 0), _tok(tm, HD, 0), _tok(tm, HD, 0)],
        out_specs=[_tok(tm, 512, 0), _tok(tm, 256, 0), _tok(tm, 256, 0)],
        out_shape=[jax.ShapeDtypeStruct((T, 512), bf16), jax.ShapeDtypeStruct((T, 256), bf16),
                   jax.ShapeDtypeStruct((T, 256), bf16)],
        compiler_params=_cparams(("parallel",)),
    )(P, P, P, qg, kg, *rope)


def _prep_bwd(P, dqn, dkn, dv, qg, kg, rope, tm, name):
    T = P.shape[0]
    nt = T // tm

    def body(q_ref, k_ref, dq_ref, dk_ref, dv_ref, qg_ref, kg_ref, c_ref, sa_ref, sb_ref, o_ref, og_ref):
        tabs = (c_ref[...], sa_ref[...], sb_ref[...])
        _, vjp = jax.vjp(lambda q, k, a, b: _prep_fn(q, k, a, b, *tabs), q_ref[...].astype(f32), k_ref[...].astype(f32),
                         qg_ref[...], kg_ref[...])
        dq, dk, dqg, dkg = vjp((dq_ref[...], dk_ref[...]))
        o_ref[:, 0:A_K] = dq.astype(bf16)
        o_ref[:, A_K:A_V] = dk.astype(bf16)
        o_ref[:, A_V:W_A] = dv_ref[...].astype(bf16)
        _partial_rows(og_ref, [dqg, dkg])

    return pl.pallas_call(
        body, name=name, grid=(nt,),
        in_specs=[_tok(tm, 512, 0), _tok(tm, 256, A_K), _tok(tm, 512, 0), _tok(tm, 256, 0), _tok(tm, 256, 0),
                  _vec(HD), _vec(HD), _tok(tm, HD, 0), _tok(tm, HD, 0), _tok(tm, HD, 0)],
        out_specs=[_tok(tm, W_A, 0), pl.BlockSpec((None, SUB, HD), lambda i: (i, 0, 0))],
        out_shape=[jax.ShapeDtypeStruct((T, W_A), bf16), jax.ShapeDtypeStruct((nt, SUB, HD), f32)],
        compiler_params=_cparams(("parallel",)),
    )(P, P, dqn, dkn, dv, qg, kg, *rope)


def _attn_fn(q, k, v, lim):
    col = lax.broadcasted_iota(jnp.int32, (1, k.shape[0]), 1)
    s = mm_nt(q, k) + jnp.where(col < lim, 0.0, -1e30)
    m = lax.stop_gradient(jnp.max(s, -1, keepdims=True))
    e = jnp.exp(s - m)
    p = e * (1.0 / jnp.sum(e, -1, keepdims=True))
    return mm(p, v)


def _attn_fwd(qn, kn, vb, tc, tq, name):
    T = qn.shape[0]

    def body(q_ref, k_ref, v_ref, o_ref):
        lim = jnp.where(pl.program_id(1) * tq < tc, tc, T)
        o_ref[...] = _attn_fn(q_ref[...], k_ref[...], v_ref[...], lim)

    return pl.pallas_call(
        body, name=name, grid=(A_HEADS, T // tq),
        in_specs=[pl.BlockSpec((tq, HD), lambda h, i: (i, h)), pl.BlockSpec((T, HD), lambda h, i: (0, h // 2)),
                  pl.BlockSpec((T, HD), lambda h, i: (0, h // 2))],
        out_specs=pl.BlockSpec((tq, HD), lambda h, i: (i, h)),
        out_shape=jax.ShapeDtypeStruct((T, 512), f32),
        compiler_params=_cparams(("parallel", "parallel")),
    )(qn, kn, vb)


def _attn_bwd(qn, kn, vb, dya, tc, tq, name):
    T = qn.shape[0]

    def body(q_ref, k_ref, v_ref, g_ref, dq_ref, dk_ref, dv_ref):
        first = (pl.program_id(1) == 0) & (pl.program_id(2) == 0)
        lim = jnp.where(pl.program_id(2) * tq < tc, tc, T)
        _, vjp = jax.vjp(lambda q, k, v: _attn_fn(q, k, v, lim), q_ref[...].astype(f32), k_ref[...].astype(f32),
                         v_ref[...].astype(f32))
        dq, dk, dv = vjp(g_ref[...])
        dq_ref[...] = dq

        @pl.when(first)
        def _():
            dk_ref[...] = dk
            dv_ref[...] = dv

        @pl.when(jnp.logical_not(first))
        def _():
            dk_ref[...] += dk
            dv_ref[...] += dv

    qspec = pl.BlockSpec((tq, HD), lambda kv, g, i: (i, 2 * kv + g))
    kspec = pl.BlockSpec((T, HD), lambda kv, g, i: (0, kv))
    return pl.pallas_call(
        body, name=name, grid=(A_HEADS // 2, 2, T // tq),
        in_specs=[qspec, kspec, kspec, qspec], out_specs=[qspec, kspec, kspec],
        out_shape=[jax.ShapeDtypeStruct((T, 512), f32), jax.ShapeDtypeStruct((T, 256), f32),
                   jax.ShapeDtypeStruct((T, 256), f32)],
        compiler_params=_cparams(("parallel", "arbitrary", "arbitrary")),
    )(qn, kn, vb, dya)


def _conv_rows(tc, tl):
    return CONV_PAD + tc + CONV_PAD + tl + CONV_PAD


def _fill_pad(pad_ref, val, tc, tl):
    z = jnp.zeros((CONV_PAD, LANE), f32)
    pad_ref[0:CONV_PAD, :] = z
    pad_ref[CONV_PAD:CONV_PAD + tc, :] = val[0:tc]
    pad_ref[CONV_PAD + tc:2 * CONV_PAD + tc, :] = z
    pad_ref[2 * CONV_PAD + tc:2 * CONV_PAD + tc + tl, :] = val[tc:tc + tl]
    pad_ref[2 * CONV_PAD + tc + tl:3 * CONV_PAD + tc + tl, :] = z


def _conv_apply(pad_ref, w_ref, K, tc, tl, rc, emit, flip=False):
    half = K // 2
    for seg0, off, n in ((0, CONV_PAD, tc), (tc, 2 * CONV_PAD + tc, tl)):
        for r0 in range(0, n, rc):
            acc = None
            for k in range(K):
                sh = (half - k) if flip else (k - half)
                term = pad_ref[pl.ds(off + r0 + sh, rc), :] * w_ref[k:k + 1, :]
                acc = term if acc is None else acc + term
            emit(seg0 + r0, acc)


def _conv_wgrad(pad_ref, dy_ref, K, tc, tl, rc, dw_ref):
    half = K // 2
    for k in range(K):
        acc = jnp.zeros((1, LANE), f32)
        for seg0, off, n in ((0, CONV_PAD, tc), (tc, 2 * CONV_PAD + tc, tl)):
            for r0 in range(0, n, rc):
                acc = acc + jnp.sum(pad_ref[pl.ds(off + r0 + k - half, rc), :] * dy_ref[pl.ds(seg0 + r0, rc), :],
                                    axis=0, keepdims=True)
        dw_ref[k:k + 1, :] = acc


def _col(T, off):
    return pl.BlockSpec((T, LANE), lambda j: (0, off // LANE + j))


C_B, C_C, C_X, C_A, C_G = range(5)
N_SEC = 5


class _Sections:
    def __init__(self, refs):
        self.refs = refs

    def __getitem__(self, idx):
        rows, sec = idx
        return self.refs[sec][rows, :].astype(f32)

    def __setitem__(self, idx, val):
        rows, sec = idx
        self.refs[sec, rows, :] = val


def _sec_specs(T):
    return [pl.BlockSpec((T, LANE), functools.partial(lambda j, s: (0, s * (BRW // LANE) + j), s=s)) for s in range(N_SEC)]


def _conv_fwd(P, wb, wd, bd, tc, tl, rc, name):
    T = tc + tl

    def body(*refs):
        p_ref = _Sections(refs[:N_SEC])
        wb_ref, wd_ref, bd_ref, yb_ref, hh_ref, pad_ref = refs[N_SEC:]
        _fill_pad(pad_ref, p_ref[:, C_C] * p_ref[:, C_X], tc, tl)

        def emit_b(r0, y):
            yb_ref[pl.ds(r0, rc), :] = y * p_ref[pl.ds(r0, rc), C_B]

        _conv_apply(pad_ref, wb_ref, KB, tc, tl, rc, emit_b)
        _fill_pad(pad_ref, p_ref[:, C_A] * _sigmoid(p_ref[:, C_G]), tc, tl)

        def emit_d(r0, y):
            hh_ref[pl.ds(r0, rc), :] = y + bd_ref[...]

        _conv_apply(pad_ref, wd_ref, KD, tc, tl, rc, emit_d)

    return pl.pallas_call(
        body, name=name, grid=(BRW // LANE,),
        in_specs=_sec_specs(T) + [pl.BlockSpec((KB, LANE), lambda j: (0, j)), pl.BlockSpec((KD, LANE), lambda j: (0, j)),
                                  pl.BlockSpec((1, LANE), lambda j: (0, j))],
        out_specs=[_col(T, 0), _col(T, 0)],
        out_shape=[jax.ShapeDtypeStruct((T, BRW), f32), jax.ShapeDtypeStruct((T, BRW), f32)],
        scratch_shapes=[pltpu.VMEM((_conv_rows(tc, tl), LANE), f32)],
        compiler_params=_cparams(("parallel",)),
    )(*[P] * N_SEC, wb, wd, bd)


def _conv_bwd(P, dyb, dhh, wb, wd, tc, tl, rc, name):
    T = tc + tl

    def body(*refs):
        p_ref = _Sections(refs[:N_SEC])
        dyb_ref, dhh_ref, wb_ref, wd_ref, dp3_ref, dwb_ref, dwd_ref, dbd_ref, pad_ref, pad2_ref, tmp_ref = refs[N_SEC:]
        dp_ref = _Sections(dp3_ref)
        _fill_pad(pad_ref, p_ref[:, C_C] * p_ref[:, C_X], tc, tl)

        def emit_cv(r0, y):
            dp_ref[pl.ds(r0, rc), C_B] = (y * dyb_ref[pl.ds(r0, rc), :]).astype(bf16)

        _conv_apply(pad_ref, wb_ref, KB, tc, tl, rc, emit_cv)
        tmp_ref[...] = dyb_ref[...] * p_ref[:, C_B]
        _conv_wgrad(pad_ref, tmp_ref, KB, tc, tl, rc, dwb_ref)
        _fill_pad(pad2_ref, tmp_ref[...], tc, tl)

        def emit_ds(r0, y):
            dp_ref[pl.ds(r0, rc), C_C] = (y * p_ref[pl.ds(r0, rc), C_X]).astype(bf16)
            dp_ref[pl.ds(r0, rc), C_X] = (y * p_ref[pl.ds(r0, rc), C_C]).astype(bf16)

        _conv_apply(pad2_ref, wb_ref, KB, tc, tl, rc, emit_ds, flip=True)
        _fill_pad(pad_ref, p_ref[:, C_A] * _sigmoid(p_ref[:, C_G]), tc, tl)
        _conv_wgrad(pad_ref, dhh_ref, KD, tc, tl, rc, dwd_ref)
        dbd_ref[...] = jnp.sum(dhh_ref[...], axis=0, keepdims=True)
        _fill_pad(pad2_ref, dhh_ref[...], tc, tl)

        def emit_d2(r0, y):
            sg = _sigmoid(p_ref[pl.ds(r0, rc), C_G])
            a = p_ref[pl.ds(r0, rc), C_A]
            dp_ref[pl.ds(r0, rc), C_A] = (y * sg).astype(bf16)
            dp_ref[pl.ds(r0, rc), C_G] = (y * a * sg * (1.0 - sg)).astype(bf16)

        _conv_apply(pad2_ref, wd_ref, KD, tc, tl, rc, emit_d2, flip=True)

    return pl.pallas_call(
        body, name=name, grid=(BRW // LANE,),
        in_specs=_sec_specs(T) + [_col(T, 0), _col(T, 0),
                                  pl.BlockSpec((KB, LANE), lambda j: (0, j)), pl.BlockSpec((KD, LANE), lambda j: (0, j))],
        out_specs=[pl.BlockSpec((N_SEC, T, LANE), lambda j: (0, 0, j)), pl.BlockSpec((KB, LANE), lambda j: (0, j)),
                   pl.BlockSpec((KD, LANE), lambda j: (0, j)), pl.BlockSpec((1, LANE), lambda j: (0, j))],
        out_shape=[jax.ShapeDtypeStruct((N_SEC, T, BRW), bf16), jax.ShapeDtypeStruct((KB, BRW), f32),
                   jax.ShapeDtypeStruct((KD, BRW), f32), jax.ShapeDtypeStruct((1, BRW), f32)],
        scratch_shapes=[pltpu.VMEM((_conv_rows(tc, tl), LANE), f32), pltpu.VMEM((_conv_rows(tc, tl), LANE), f32),
                        pltpu.VMEM((T, LANE), f32)],
        compiler_params=_cparams(("parallel",)),
    )(*[P] * N_SEC, dyb, dhh, wb, wd)


def _gla_chunk(q, k, v, r, w2, b2, st, isfwd):
    z = mm(r, w2) + b2
    g = jax.nn.log_sigmoid(z[:, 0:C_KW] if isfwd else z[:, C_KW:2 * C_KW]) / C_TAU
    ri = lax.broadcasted_iota(jnp.int32, (CH, CH), 0)
    ci = lax.broadcasted_iota(jnp.int32, (CH, CH), 1)
    tri = ((ci <= ri) if isfwd else (ci >= ri)).astype(f32)
    cum = jnp.dot(tri, g, preferred_element_type=f32, precision=lax.Precision.HIGHEST)
    last = jnp.sum(g, axis=0, keepdims=True)
    q = q * (C_KW // C_HEADS) ** -0.5
    hv = lax.broadcasted_iota(jnp.int32, (BRW, C_KW), 0) // (BRW // C_HEADS)
    hk = lax.broadcasted_iota(jnp.int32, (BRW, C_KW), 1) // (C_KW // C_HEADS)
    st_new = st * jnp.exp(last) + jnp.where(hv == hk, mm_tn(v, k * jnp.exp(last - cum)), 0.0)
    o = mm_nt(q * jnp.exp(cum), st)
    rowi = lax.broadcasted_iota(jnp.int32, (CH, C_KW), 0)
    srow = lax.broadcasted_iota(jnp.int32, (C_HEADS * CH, C_KW), 0)
    slane = lax.broadcasted_iota(jnp.int32, (C_HEADS * CH, C_KW), 1)
    own_lanes = srow // CH == slane // (C_KW // C_HEADS)
    pos = lax.broadcasted_iota(jnp.int32, (C_HEADS * CH, CH), 0) % CH
    key = lax.broadcasted_iota(jnp.int32, (C_HEADS * CH, CH), 1)
    scores = jnp.zeros((C_HEADS * CH, CH), f32)
    for a in range(CH // GLA_SUB):
        idx = GLA_SUB * a - 1 if isfwd else GLA_SUB * (a + 1)
        ref = jnp.sum(jnp.where(rowi == idx, cum, 0.0), axis=0, keepdims=True)
        qa = q * jnp.exp(jnp.minimum(cum - ref, 0.0))
        ka = k * jnp.exp(jnp.minimum(ref - cum, GLA_CLAMP))
        s = mm_nt(jnp.where(own_lanes, jnp.concatenate([qa] * C_HEADS, axis=0), 0.0), ka)
        scores = scores + jnp.where(pos // GLA_SUB == a, s, 0.0)
    scores = jnp.where((key <= pos) if isfwd else (key >= pos), scores, 0.0)
    vw = BRW // C_HEADS
    o = o + jnp.concatenate([mm(scores[CH * hd:CH * (hd + 1)], v[:, vw * hd:vw * (hd + 1)]) for hd in range(C_HEADS)],
                            axis=1)
    return o, st_new


def _gla_chunk_of(d, n, nc, nch):
    back = jnp.where(n < nc, nc - 1 - n, nch - 1 - (n - nc))
    return jnp.where(d == 0, n, back)


def _gla_fwd(P, w2, b2, tc, name):
    T = P.shape[0]
    nch, nc = T // CH, tc // CH

    back = lambda n: _gla_chunk_of(1, n, nc, nch)

    def body(pf_ref, pb_ref, w_ref, b_ref, of_ref, ob_ref, ssf_ref, ssb_ref, stf_ref, stb_ref):
        @pl.when(pl.program_id(0) == 0)
        def _():
            stf_ref[...] = jnp.zeros_like(stf_ref)
            stb_ref[...] = jnp.zeros_like(stb_ref)

        for p_ref, o_ref, ss_ref, st_ref, isfwd in ((pf_ref, of_ref, ssf_ref, stf_ref, True),
                                                    (pb_ref, ob_ref, ssb_ref, stb_ref, False)):
            st = st_ref[...]
            ss_ref[...] = st
            p = p_ref[...].astype(f32)
            o, st_new = _gla_chunk(p[:, 0:G_K], p[:, G_K:G_V], p[:, G_V:G_R], p[:, G_R:W_G], w_ref[...], b_ref[...], st, isfwd)
            o_ref[...] = o
            st_ref[...] = st_new

    sd = jax.ShapeDtypeStruct
    return pl.pallas_call(
        body, name=name, grid=(nch,),
        in_specs=[pl.BlockSpec((CH, W_G), lambda n: (n, 0)), pl.BlockSpec((CH, W_G), lambda n: (back(n), 0)),
                  pl.BlockSpec((LANE, 512), lambda n: (0, 0)), pl.BlockSpec((1, 512), lambda n: (0, 0))],
        out_specs=[pl.BlockSpec((CH, BRW), lambda n: (n, 0)), pl.BlockSpec((CH, BRW), lambda n: (back(n), 0)),
                   pl.BlockSpec((None, BRW, C_KW), lambda n: (n, 0, 0)), pl.BlockSpec((None, BRW, C_KW), lambda n: (n, 0, 0))],
        out_shape=[sd((T, BRW), f32), sd((T, BRW), f32), sd((nch, BRW, C_KW), f32), sd((nch, BRW, C_KW), f32)],
        scratch_shapes=[pltpu.VMEM((BRW, C_KW), f32), pltpu.VMEM((BRW, C_KW), f32)],
        compiler_params=_cparams(("arbitrary",)),
    )(P, P, w2, b2)


def _gla_bwd(P, w2, b2, ssave, doc, tc, name):
    T = P.shape[0]
    nch, nc = T // CH, tc // CH

    fwd_chunk = lambda m: nch - 1 - m
    back_chunk = lambda m: _gla_chunk_of(1, nch - 1 - m, nc, nch)

    def body(pf_ref, pb_ref, w_ref, b_ref, ssf_ref, ssb_ref, gf_ref, gb_ref, dpf_ref, dpb_ref, dw_ref, db_ref,
             dstf_ref, dstb_ref):
        m = pl.program_id(0)

        @pl.when(m == 0)
        def _():
            dstf_ref[...] = jnp.zeros_like(dstf_ref)
            dstb_ref[...] = jnp.zeros_like(dstb_ref)

        dw_sum, db_sum = None, None
        for p_ref, ss_ref, g_ref, dp_ref, dst_ref, isfwd in ((pf_ref, ssf_ref, gf_ref, dpf_ref, dstf_ref, True),
                                                             (pb_ref, ssb_ref, gb_ref, dpb_ref, dstb_ref, False)):
            p = p_ref[...].astype(f32)
            _, vjp = jax.vjp(lambda q, k, v, r, w, b, st: _gla_chunk(q, k, v, r, w, b, st, isfwd),
                             p[:, 0:G_K], p[:, G_K:G_V], p[:, G_V:G_R], p[:, G_R:W_G], w_ref[...], b_ref[...], ss_ref[...])
            dq, dk, dv, dr, dw, db, dst = vjp((g_ref[...], dst_ref[...]))
            dp_ref[:, 0:G_K] = dq
            dp_ref[:, G_K:G_V] = dk
            dp_ref[:, G_V:G_R] = dv
            dp_ref[:, G_R:W_G] = dr
            dst_ref[...] = dst
            dw_sum = dw if dw_sum is None else dw_sum + dw
            db_sum = db if db_sum is None else db_sum + db

        @pl.when(m == 0)
        def _():
            dw_ref[...] = dw_sum
            _partial_rows(db_ref, [db_sum])

        @pl.when(m > 0)
        def _():
            dw_ref[...] += dw_sum
            db_ref[0:1, :] += db_sum

    ssf, ssb = ssave
    chunk_f = lambda w: pl.BlockSpec((CH, w), lambda m: (fwd_chunk(m), 0))
    chunk_b = lambda w: pl.BlockSpec((CH, w), lambda m: (back_chunk(m), 0))
    state = pl.BlockSpec((None, BRW, C_KW), lambda m: (nch - 1 - m, 0, 0))
    sd = jax.ShapeDtypeStruct
    return pl.pallas_call(
        body, name=name, grid=(nch,),
        in_specs=[chunk_f(W_G), chunk_b(W_G), pl.BlockSpec((LANE, 512), lambda m: (0, 0)), pl.BlockSpec((1, 512), lambda m: (0, 0)),
                  state, state, chunk_f(BRW), chunk_b(BRW)],
        out_specs=[chunk_f(W_G), chunk_b(W_G), pl.BlockSpec((LANE, 512), lambda m: (0, 0)), pl.BlockSpec((SUB, 512), lambda m: (0, 0))],
        out_shape=[sd((T, W_G), f32), sd((T, W_G), f32), sd((LANE, 512), f32), sd((SUB, 512), f32)],
        scratch_shapes=[pltpu.VMEM((BRW, C_KW), f32), pltpu.VMEM((BRW, C_KW), f32)],
        compiler_params=_cparams(("arbitrary",)),
    )(P, P, w2, b2, ssf, ssb, doc, doc)


def _sum_dirs(a, b, tm, name):
    T, W = a.shape

    def body(a_ref, b_ref, o_ref):
        o_ref[...] = (a_ref[...] + b_ref[...]).astype(bf16)

    spec = pl.BlockSpec((tm, W), lambda i: (i, 0))
    return pl.pallas_call(
        body, name=name, grid=(T // tm,), in_specs=[spec, spec], out_specs=spec,
        out_shape=jax.ShapeDtypeStruct((T, W), bf16),
        compiler_params=_cparams(("parallel",)),
    )(a, b)


def _merge_fn(h, m_l, m_c, isctx, ya, ga, yb, gb, of, ob, gc, hh, gd, mg, es, ey, cn, dng, dnb, lg, lb, wbr, wout):
    oc = of + ob
    yc = jnp.concatenate([_rms(oc[:, HD * i:HD * (i + 1)], cn[:, HD * i:HD * (i + 1)]) for i in range(C_HEADS)], 1)
    brs = [ya * _silu(ga), yb * _silu(gb), yc * _silu(gc), _silu(_ln(hh) * dng + dnb) * _silu(gd)]
    acc = None
    for i in range(4):
        t = _sigmoid(mg[:, D * i:D * (i + 1)]) * (mm(brs[i], wbr[i]) + es[i])
        acc = t if acc is None else acc + t
    y = mm(acc, wout) + ey
    gate = jnp.where(isctx, m_c[:, 2 * D:3 * D], m_l[:, 2 * D:3 * D])
    hn = _ln(ALPHA * h + gate * y) * lg + lb
    return hn, (brs, acc)


def _merge_specs(tm):
    t = lambda w, off=0: _tok(tm, w, off)
    return [t(D), pl.BlockSpec((SUB, 3 * D), lambda i: (0, 0)),
            t(BRW), t(BRW, M_GA), t(BRW), t(BRW, M_GB),
            t(BRW), t(BRW),
            t(BRW, M_GC), t(BRW), t(BRW, M_GD), t(4 * D, 0),
            _vec(BRW), _vec(BRW), _vec(BRW), _vec(D), _vec(D),
            pl.BlockSpec((4, BRW, D), lambda i: (0, 0, 0)), pl.BlockSpec((D, D), lambda i: (0, 0))]


def _merge_fwd(h, modv_l, ya, yb, o2, hh, P, cn, dng, dnb, lg, lb, wbr, wout, tc, tm, name):
    T = h.shape[0]

    def body(h_ref, m_ref, ya_ref, ga_ref, yb_ref, gb_ref, of_ref, ob_ref, gc_ref, hh_ref, gd_ref, mg_ref,
             cn_ref, dng_ref, dnb_ref, lg_ref, lb_ref, wbr_ref, wout_ref, o_ref):
        isctx = _row_ids(pl.program_id(0), tm) < tc
        zero = jnp.zeros((tm, D), f32)
        up = lambda r: r[...].astype(f32)
        hn, _ = _merge_fn(h_ref[...], m_ref[0:1, :], m_ref[1:2, :], isctx, ya_ref[...], up(ga_ref), yb_ref[...],
                          up(gb_ref), of_ref[...], ob_ref[...], up(gc_ref), hh_ref[...], up(gd_ref), up(mg_ref),
                          [zero] * 4, zero, cn_ref[...], dng_ref[...], dnb_ref[...], lg_ref[...], lb_ref[...],
                          [wbr_ref[i] for i in range(4)], wout_ref[...])
        o_ref[...] = hn

    return pl.pallas_call(
        body, name=name, grid=(T // tm,),
        in_specs=_merge_specs(tm), out_specs=_tok(tm, D, 0),
        out_shape=jax.ShapeDtypeStruct((T, D), f32),
        compiler_params=_cparams(("parallel",)),
    )(h, modv_l, ya, P, yb, P, o2[0], o2[1], P, hh, P, P, cn, dng, dnb, lg, lb, wbr, wout)


def _merge_bwd(dhn, h, modv_l, ya, yb, o2, hh, P, cn, dng, dnb, lg, lb, wbr, wout, tc, tm, name):
    T = h.shape[0]
    nt = T // tm

    def body(g_ref, h_ref, m_ref, ya_ref, ga_ref, yb_ref, gb_ref, of_ref, ob_ref, gc_ref, hh_ref, gd_ref, mg_ref,
             cn_ref, dng_ref, dnb_ref, lg_ref, lb_ref, wbr_ref, wout_ref,
             dh_ref, dm_ref, dya_ref, dyb_ref, doc_ref, dhh_ref, dp_ref,
             br_ref, z_ref, acc_ref, dy_ref, dv5_ref, dvd_ref):
        isctx = _row_ids(pl.program_id(0), tm) < tc
        zero = jnp.zeros((tm, D), f32)
        wbr_v = [wbr_ref[i] for i in range(4)]
        wout_v = wout_ref[...]
        up = lambda r: r[...].astype(f32)

        def fn(h, ml, mc, ya, ga, yb, gb, oc, gc, hh, gd, mg, e0, e1, e2, e3, ey, cn, dng, dnb, lg, lb):
            return _merge_fn(h, ml, mc, isctx, ya, ga, yb, gb, oc, jnp.zeros_like(oc), gc, hh, gd, mg,
                             [e0, e1, e2, e3], ey, cn, dng, dnb, lg, lb, wbr_v, wout_v)

        _, vjp, (brs, acc) = jax.vjp(
            fn, h_ref[...], m_ref[0:1, :], m_ref[1:2, :], ya_ref[...], up(ga_ref), yb_ref[...], up(gb_ref),
            of_ref[...] + ob_ref[...], up(gc_ref), hh_ref[...], up(gd_ref), up(mg_ref), zero, zero, zero, zero, zero,
            cn_ref[...], dng_ref[...], dnb_ref[...], lg_ref[...], lb_ref[...], has_aux=True)
        (dh, dml, dmc, dya, dga, dyb, dgb, doc, dgc, dhh, dgd, dmg, z0, z1, z2, z3, dy,
         dcn, ddng, ddnb, dlg, dlb) = vjp(g_ref[...])
        dh_ref[...] = dh
        _partial_rows(dm_ref, [dml, dmc])
        dya_ref[...] = dya
        dyb_ref[...] = dyb
        doc_ref[...] = doc
        dhh_ref[...] = dhh
        dp_ref[:, 0:M_GA] = dmg.astype(bf16)
        dp_ref[:, M_GA:M_GB] = dga.astype(bf16)
        dp_ref[:, M_GB:M_GC] = dgb.astype(bf16)
        dp_ref[:, M_GC:M_GD] = dgc.astype(bf16)
        dp_ref[:, M_GD:W_M] = dgd.astype(bf16)
        for i, z in enumerate((z0, z1, z2, z3)):
            br_ref[i] = brs[i].astype(bf16)
            z_ref[i] = z.astype(bf16)
        acc_ref[...] = acc.astype(bf16)
        dy_ref[...] = dy.astype(bf16)
        _partial_rows(dv5_ref, [dcn, ddng, ddnb])
        _partial_rows(dvd_ref, [dlg, dlb])

    t = lambda w: _tok(tm, w, 0)
    part = lambda w: pl.BlockSpec((None, SUB, w), lambda i: (i, 0, 0))
    sd = jax.ShapeDtypeStruct
    return pl.pallas_call(
        body, name=name, grid=(nt,),
        in_specs=[t(D)] + _merge_specs(tm),
        out_specs=[t(D), part(3 * D)] + [t(BRW)] * 4 + [t(W_M),
                   pl.BlockSpec((4, tm, BRW), lambda i: (0, i, 0)), pl.BlockSpec((4, tm, D), lambda i: (0, i, 0)),
                   t(D), t(D), part(BRW), part(D)],
        out_shape=[sd((T, D), f32), sd((nt, SUB, 3 * D), f32)] + [sd((T, BRW), f32)] * 4 + [sd((T, W_M), bf16),
                   sd((4, T, BRW), bf16), sd((4, T, D), bf16), sd((T, D), bf16), sd((T, D), bf16),
                   sd((nt, SUB, BRW), f32), sd((nt, SUB, D), f32)],
        compiler_params=_cparams(("parallel",)),
    )(dhn, h, modv_l, ya, P, yb, P, o2[0], o2[1], P, hh, P, P, cn, dng, dnb, lg, lb, wbr, wout)


def _loss_kernel(h, tgt, tc, tm, name):
    T = h.shape[0]
    nt = T // tm
    nct = tc // tm

    def body(h_ref, t_ref, d_ref, l_ref):
        i = pl.program_id(0)
        err = h_ref[...] - t_ref[...]
        lat = (i >= nct).astype(f32)
        d_ref[...] = err * (lat / D)
        l_ref[...] = jnp.zeros((SUB, LANE), f32) + lat * 0.5 * jnp.sum(err * err) / D

    return pl.pallas_call(
        body, name=name, grid=(nt,),
        in_specs=[pl.BlockSpec((tm, D), lambda i: (i, 0)),
                  pl.BlockSpec((tm, D), lambda i: (jnp.maximum(i - nct, 0), 0))],
        out_specs=[pl.BlockSpec((tm, D), lambda i: (i, 0)), pl.BlockSpec((None, SUB, LANE), lambda i: (i, 0, 0))],
        out_shape=[jax.ShapeDtypeStruct((T, D), f32), jax.ShapeDtypeStruct((nt, SUB, LANE), f32)],
        compiler_params=_cparams(("parallel",)),
    )(h, tgt)


def _rope_tables(tc, tl):
    t = jnp.arange(tl)
    inv = ROPE_THETA ** (-jnp.arange(0, HD // 2, 2, dtype=f32) / (HD // 2))
    ang = jnp.concatenate([(t // GRID_W).astype(f32)[:, None] * inv, (t % GRID_W).astype(f32)[:, None] * inv], -1)
    cos, sin = jnp.repeat(jnp.cos(ang), 2, axis=1), jnp.repeat(jnp.sin(ang), 2, axis=1)
    even = (jnp.arange(HD) % 2 == 0)[None, :]
    cos_f = jnp.concatenate([jnp.ones((tc, HD), f32), cos], 0)
    sin_a = jnp.concatenate([jnp.zeros((tc, HD), f32), jnp.where(even, -sin, 0.0)], 0)
    sin_b = jnp.concatenate([jnp.zeros((tc, HD), f32), jnp.where(even, 0.0, sin)], 0)
    return cos_f, sin_a, sin_b


N_CHIPS = 4
SHARD = N_IN // N_CHIPS


def _group_ranges():
    return dict(M=[(S_MG, 4 * D), (S_GA, BRW), (S_GB, BRW), (S_GC, BRW), (S_GD, BRW)], A=[(S_Q, W_A)],
                C=[(S_B, 3 * BRW), (S_DA, 2 * BRW)], G=[(S_CQ, 2 * C_KW + BRW), (S_R, 2 * C_RANK)])


def _group_weights(w4):
    out = {}
    for k, ranges in _group_ranges().items():
        parts = []
        for a, n in ranges:
            while n > 0:
                s, r = divmod(a, SHARD)
                m = min(n, SHARD - r)
                parts.append(w4[s, r:r + m])
                a, n = a + m, n - m
        if k == "G":
            parts.append(jnp.zeros((LANE - 2 * C_RANK, D), w4.dtype))
        out[k] = jnp.concatenate(parts, 0)
    return out


def _ungroup(g):
    secs = []
    for k, ranges in _group_ranges().items():
        off = 0
        for a, n in ranges:
            secs.append((a, g[k][off:off + n]))
            off += n
    return jnp.concatenate([v for _, v in sorted(secs, key=lambda t: t[0])], 0)


PROJ_TN = dict(M=2048, A=1024, C=1280, G=1152)
DU_TK = dict(M=2048, A=1024, C=BRW, G=1152)
DWP_TN = dict(M=768, A=1024, C=BRW, G=1152)


def _gate_weights(w2_l, gb_l):
    w = jnp.zeros((LANE, 2 * C_KW), f32)
    w = w.at[0:C_RANK, 0:C_KW].set(w2_l[0]).at[C_RANK:2 * C_RANK, C_KW:2 * C_KW].set(w2_l[1])
    return w, jnp.concatenate([gb_l[0], gb_l[1]])[None, :]


def _local_step(x1, c1, ctx1, tgt1, c_ctx, w_mod, b_mod, weights_of, q_norm, k_norm, b_conv, w2, gb, c_norm, d_conv_w,
                d_conv_b, d_norm_g, d_norm_b, grads_done, ln_g, ln_b, tm, token=None):
    tc, tl = ctx1.shape[0], x1.shape[0]
    T = tc + tl
    rc = min(256, tc)
    tmb = tm // 2
    tmm = 768 if T % 768 == 0 else tm
    rope = _rope_tables(tc, tl)
    cin = jnp.concatenate([c1, c_ctx[None, :], jnp.zeros((SUB - 2, D), f32)], 0)
    if token is not None:
        cin = cin + token[:, 0:1]
    modv = _mod_fwd(cin, w_mod, b_mod)
    modv = [modv[l] for l in range(DEPTH)]
    row = lambda v: v[None, :]

    h = jnp.concatenate([ctx1, x1], 0)
    saved, wp, w_br, w_out = [], [None] * DEPTH, [None] * DEPTH, [None] * DEPTH
    for l in range(DEPTH):
        wp[l], merge_weights = weights_of(l, h)
        u = _ln_fwd(h, modv[l], tc, tm, f"ln_fwd{l}")
        P = {k: _matmul(u, wp[l][k], "nt", tmm, PROJ_TN[k], D, f"proj{l}{k}", out_dtype=bf16) for k in GROUPS}
        qn, kn, vb = _prep_fwd(P["A"], row(q_norm[l]), row(k_norm[l]), rope, tm, f"prep_fwd{l}")
        ya = _attn_fwd(qn, kn, vb, tc, tm, f"attn_fwd{l}")
        yb, hh = _conv_fwd(P["C"], b_conv[l], d_conv_w[l], row(d_conv_b[l]), tc, tl, rc, f"conv_fwd{l}")
        w2p, b2p = _gate_weights(w2[l], gb[l])
        gla = _gla_fwd(P["G"], w2p, b2p, tc, f"gla_fwd{l}")
        o2, ssave = gla[:2], gla[2:]
        w_br[l], w_out[l] = merge_weights(o2[0])
        hn = _merge_fwd(h, modv[l], ya, yb, o2, hh, P["M"], row(c_norm[l]), row(d_norm_g[l]), row(d_norm_b[l]),
                        row(ln_g[l]), row(ln_b[l]), w_br[l], w_out[l], tc, tm, f"merge_fwd{l}")
        saved.append((h, u, P, qn, kn, vb, ya, yb, hh, o2, ssave, w2p, b2p))
        h = hn

    dh, lparts = _loss_kernel(h, tgt1, tc, tm, "loss")
    loss = jnp.sum(lparts[:, 0, 0])

    g = {k: [None] * DEPTH for k in ("wp", "q_norm", "k_norm", "b_conv", "w2", "gb", "c_norm", "d_conv_w", "d_conv_b",
                                     "d_norm_g", "d_norm_b", "w_br", "w_out", "ln_g", "ln_b", "modv")}
    for l in reversed(range(DEPTH)):
        h_in, u, P, qn, kn, vb, ya, yb, hh, o2, ssave, w2p, b2p = saved[l]
        dP = {}
        (dh_res, dm_mg, dya, dyb, doc, dhh, dP["M"], br, z, acc, dy, dv5, dvd) = _merge_bwd(
            dh, h_in, modv[l], ya, yb, o2, hh, P["M"], row(c_norm[l]), row(d_norm_g[l]), row(d_norm_b[l]),
            row(ln_g[l]), row(ln_b[l]), w_br[l], w_out[l], tc, tmb, f"merge_bwd{l}")
        g["w_br"][l] = _matmul_tn_batched(br, z, N_CHIPS, f"dwbr{l}")
        g["w_out"][l] = _matmul(acc, dy, "tn", D, D, T, f"dwout{l}", out_dtype=bf16)
        tk = grads_done(l, {k: g[k][l] for k in ("w_br", "w_out")})
        qg_l = row(q_norm[l]) if tk is None else row(q_norm[l]) + tk[0:1, :]
        v5 = jnp.sum(dv5, 0)
        g["c_norm"][l], g["d_norm_g"][l], g["d_norm_b"][l] = v5[0], v5[1], v5[2]
        vd = jnp.sum(dvd, 0)
        g["ln_g"][l], g["ln_b"][l] = vd[0], vd[1]
        dqn, dkn, dv = _attn_bwd(qn, kn, vb, dya, tc, tm, f"attn_bwd{l}")
        dP["A"], dqk = _prep_bwd(P["A"], dqn, dkn, dv, qg_l, row(k_norm[l]), rope, tm, f"prep_bwd{l}")
        dqk = jnp.sum(dqk, 0)
        g["q_norm"][l], g["k_norm"][l] = dqk[0], dqk[1]
        dP["C"], dwb, dwd, dbd = _conv_bwd(P["C"], dyb, dhh, b_conv[l], d_conv_w[l], tc, tl, rc, f"conv_bwd{l}")
        g["b_conv"][l], g["d_conv_w"][l], g["d_conv_b"][l] = dwb, dwd, dbd[0]
        dpf, dpb, dw2p, db2p = _gla_bwd(P["G"], w2p, b2p, ssave, doc, tc, f"gla_bwd{l}")
        dP["G"] = _sum_dirs(dpf, dpb, tm, f"gla_sum{l}")
        db2p = db2p[0]
        g["w2"][l] = jnp.stack([dw2p[0:C_RANK, 0:C_KW], dw2p[C_RANK:2 * C_RANK, C_KW:2 * C_KW]])
        g["gb"][l] = jnp.stack([db2p[0:C_KW], db2p[C_KW:2 * C_KW]])
        g["wp"][l] = {k: _matmul(dP[k], u, "tn", DWP_TN[k], D, T, f"dwp{l}{k}", out_dtype=bf16) for k in GROUPS}
        tk = grads_done(l, {"wp": g["wp"][l]})
        du = _matmul_groups(dP, wp[l], DU_TK, tmm, f"du{l}", after=tk)
        dh, dm_ln = _ln_bwd(du, h_in, dh_res, modv[l], tc, tm, f"ln_bwd{l}", latent_only=(l == 0))
        g["modv"][l] = jnp.sum(dm_mg, 0) + jnp.sum(dm_ln, 0)

    dmodv = jnp.stack(g.pop("modv"))
    g["w_mod"], dcin = _mod_bwd(cin, w_mod, dmodv)
    g["b_mod"] = dmodv[:, 0, :] + dmodv[:, 1, :]
    g["c_ctx"] = jnp.sum(dcin, (0, 1))[1]
    return loss, dh, g


HALF_TL = 256


def _adamw(w, g, m, v, name, tr=128, after=None):
    L, R, C = w.shape
    if R % tr == 0:
        grid, spec = (L, R // tr), pl.BlockSpec((None, tr, C), lambda l, i: (l, i, 0))
    elif R * C * 4 <= (1 << 20):
        grid, spec = (L, 1), pl.BlockSpec((None, R, C), lambda l, i: (l, 0, 0))
    else:
        grid, spec = (L, C // HALF_TL), pl.BlockSpec((None, R, HALF_TL), lambda l, i: (l, 0, i))

    def body(w_ref, g_ref, m_ref, v_ref, *rest):
        d_ref, nm_ref, nv_ref = rest[-3:]
        gg = g_ref[...]
        nm = B1 * m_ref[...] + (1.0 - B1) * gg
        nv = B2 * v_ref[...] + (1.0 - B2) * (gg * gg)
        m_hat = nm / (1.0 - B1 ** STEP)
        v_hat = nv / (1.0 - B2 ** STEP)
        d_ref[...] = -LR * (m_hat / (jnp.sqrt(v_hat) + AEPS) + WD * w_ref[...])
        nm_ref[...] = nm
        nv_ref[...] = nv

    return pl.pallas_call(
        body, name=name, grid=grid, in_specs=[spec] * 4 + ([] if after is None else [pl.BlockSpec(memory_space=pl.ANY)]),
        out_specs=[spec] * 3, out_shape=[jax.ShapeDtypeStruct((L, R, C), f32)] * 3,
        compiler_params=_cparams(("parallel", "parallel")),
    )(w, g, m, v, *([] if after is None else [after]))


MESH = pl.DeviceIdType.MESH
ANY = pl.BlockSpec(memory_space=pl.ANY)
N_CHIPS = 4


def _place():
    x, y, c = lax.axis_index("x"), lax.axis_index("y"), lax.axis_index("c")
    chips = [(1 - x, y), (x, 1 - y), (1 - x, 1 - y)]
    return x, y, c, chips


def _half(ref, c, axis):
    n = ref.shape[axis] // 2
    last = axis in (-1, ref.ndim - 1)
    idx = [slice(None)] * ref.ndim
    idx[axis] = pl.ds(pl.multiple_of(c * n, LANE if last else SUB), n)
    return ref.at[tuple(idx)]


def _half_shape(shape, axis):
    s = list(shape)
    s[axis] //= 2
    return tuple(s)


def _all_gather(arrs, axes, name):
    n = len(arrs)

    def body(*refs):
        ins, outs = refs[:n], refs[n:2 * n]
        send, recv = refs[2 * n:]
        x, y, c, chips = _place()
        me, sib = 2 * x + y, (x, y, 1 - c)

        def copy(a, k, chip_idx, cc, to, src=None):
            blk = _half(outs[a].at[chip_idx], cc, axes[a])
            return pltpu.make_async_remote_copy(src_ref=blk if src is None else src, dst_ref=blk,
                                                send_sem=send.at[7 * a + k], recv_sem=recv.at[7 * a + k],
                                                device_id=to, device_id_type=MESH)

        own = [pltpu.make_async_remote_copy(src_ref=ins[a], dst_ref=outs[a].at[me], send_sem=send.at[7 * a + 6],
                                            recv_sem=recv.at[7 * a + 6], device_id=sib, device_id_type=MESH)
               for a in range(n)]
        first = own + [copy(a, j, me, c, (*chip, c), src=_half(ins[a], c, axes[a]))
                       for a in range(n) for j, chip in enumerate(chips)]
        for cp in first:
            cp.start()
        passed = []
        for a in range(n):
            for j, chip in enumerate(chips):
                k = 2 * chip[0] + chip[1]
                copy(a, j, k, c, sib).wait_recv()
                fwd = copy(a, 3 + j, k, c, sib)
                fwd.start()
                passed.append(fwd)
        for a in range(n):
            own[a].wait_recv()
            for j, chip in enumerate(chips):
                copy(a, 3 + j, 2 * chip[0] + chip[1], 1 - c, sib).wait_recv()
        for cp in first + passed:
            cp.wait_send()

    return pl.pallas_call(
        body, name=name, in_specs=[ANY] * n, out_specs=[ANY] * n,
        out_shape=[jax.ShapeDtypeStruct((N_CHIPS,) + a.shape, a.dtype) for a in arrs],
        scratch_shapes=[pltpu.SemaphoreType.DMA((7 * n,)), pltpu.SemaphoreType.DMA((7 * n,))],
    )(*arrs)


def _sibling_halves(arrs, axes, name):
    n = len(arrs)

    def body(*refs):
        ins, outs = refs[:n], refs[n:2 * n]
        send, recv = refs[2 * n:]
        x, y, c, _ = _place()
        cps = [pltpu.make_async_remote_copy(src_ref=_half(ins[a], 1 - c, axes[a] + 1), dst_ref=outs[a], send_sem=send.at[a],
                                            recv_sem=recv.at[a], device_id=(x, y, 1 - c), device_id_type=MESH)
               for a in range(n)]
        for cp in cps:
            cp.start()
        for cp in cps:
            cp.wait()

    return pl.pallas_call(
        body, name=name, in_specs=[ANY] * n, out_specs=[ANY] * n,
        out_shape=[jax.ShapeDtypeStruct(_half_shape(a.shape, axes[i] + 1), a.dtype) for i, a in enumerate(arrs)],
        scratch_shapes=[pltpu.SemaphoreType.DMA((n,)), pltpu.SemaphoreType.DMA((n,))],
    )(*arrs)


def _add_half(gfull, land, cidx, axis, name, tr=128, out_dtype=bf16):
    _, hr, hc = land.shape
    if axis == 0:
        tr = min(tr, hr)
        nb, blk = hr // tr, (None, tr, hc)
        g_spec = pl.BlockSpec(blk, lambda s, i, cr: (s, cr[0] * nb + i, 0))
        l_spec = pl.BlockSpec(blk, lambda s, i, cr: (s, i, 0))
    else:
        nb, blk = hc // HALF_TL, (None, hr, HALF_TL)
        g_spec = pl.BlockSpec(blk, lambda s, i, cr: (s, 0, cr[0] * nb + i))
        l_spec = pl.BlockSpec(blk, lambda s, i, cr: (s, 0, i))

    def body(c_ref, g_ref, l_ref, o_ref):
        o_ref[...] = (g_ref[...].astype(f32) + l_ref[...].astype(f32)).astype(o_ref.dtype)

    return pl.pallas_call(
        body, name=name,
        grid_spec=pltpu.PrefetchScalarGridSpec(
            num_scalar_prefetch=1, grid=(N_CHIPS, nb), in_specs=[g_spec, l_spec], out_specs=l_spec),
        out_shape=jax.ShapeDtypeStruct((N_CHIPS, hr, hc), out_dtype),
        compiler_params=_cparams(("parallel", "parallel")),
    )(cidx, gfull, land)


def _chip_exchange(arrs, name):
    n = len(arrs)

    def body(*refs):
        ins, outs = refs[:n], refs[n:2 * n]
        send, recv = refs[2 * n:]
        x, y, c, chips = _place()
        me = 2 * x + y
        cps = []
        for a in range(n):
            for j, chip in enumerate(chips):
                k = 2 * chip[0] + chip[1]
                cps.append((pltpu.make_async_remote_copy(
                    src_ref=ins[a].at[k], dst_ref=outs[a].at[me], send_sem=send.at[3 * a + j], recv_sem=recv.at[3 * a + j],
                    device_id=(*chip, c), device_id_type=MESH), a, j, k))
        for cp, *_ in cps:
            cp.start()
        for cp, a, j, k in cps:
            pltpu.make_async_remote_copy(src_ref=ins[a].at[k], dst_ref=outs[a].at[k], send_sem=send.at[3 * a + j],
                                         recv_sem=recv.at[3 * a + j], device_id=(x, y, c), device_id_type=MESH).wait_recv()
        for cp, *_ in cps:
            cp.wait_send()

    return pl.pallas_call(
        body, name=name, in_specs=[ANY] * n, out_specs=[ANY] * n,
        out_shape=[jax.ShapeDtypeStruct(a.shape, a.dtype) for a in arrs],
        scratch_shapes=[pltpu.SemaphoreType.DMA((3 * n,)), pltpu.SemaphoreType.DMA((3 * n,))],
    )(*arrs)


def _sum_chips(land, own, place, axis, layer, into, name, tr=128):
    _, hr, hc = land.shape
    fresh = not hasattr(into, "dtype")
    shape = tuple(into) if fresh else into.shape
    if axis == 0:
        tr = min(tr, hr)
        nb, blk = hr // tr, (tr, hc)
        l_map, m_map = (lambda i, p: (0, i, 0)), (lambda i, p: (p[0], i, 0))
        o_map = lambda i, p: (layer, p[1] * nb + i, 0)
    else:
        nb, blk = hc // HALF_TL, (hr, HALF_TL)
        l_map, m_map = (lambda i, p: (0, 0, i)), (lambda i, p: (p[0], 0, i))
        o_map = lambda i, p: (layer, 0, p[1] * nb + i)

    def body(p_ref, l_ref, o_ref, *rest):
        me = p_ref[0]
        mine = o_ref[...].astype(f32)
        acc = None
        for k in range(N_CHIPS):
            t = jnp.where(me == k, mine, l_ref[k].astype(f32))
            acc = t if acc is None else acc + t
        rest[-1][...] = acc

    return pl.pallas_call(
        body, name=name,
        grid_spec=pltpu.PrefetchScalarGridSpec(
            num_scalar_prefetch=1, grid=(nb,),
            in_specs=[pl.BlockSpec((N_CHIPS,) + blk, l_map), pl.BlockSpec((None,) + blk, m_map)] + ([] if fresh else [ANY]),
            out_specs=pl.BlockSpec((None,) + blk, o_map)),
        out_shape=jax.ShapeDtypeStruct(shape, f32),
        input_output_aliases={} if fresh else {3: 0},
        compiler_params=_cparams(("parallel",)),
    )(place, land, own, *([] if fresh else [into]))


def _sibling_fill(arrs, axes, name):
    n = len(arrs)

    def body(*refs):
        outs = refs[n:2 * n]
        send, recv = refs[2 * n:]
        x, y, c, _ = _place()
        cps = [pltpu.make_async_remote_copy(src_ref=_half(outs[a], c, axes[a] + 1), dst_ref=_half(outs[a], c, axes[a] + 1),
                                            send_sem=send.at[a], recv_sem=recv.at[a], device_id=(x, y, 1 - c),
                                            device_id_type=MESH) for a in range(n)]
        for cp in cps:
            cp.start()
        for a in range(n):
            blk = _half(outs[a], 1 - c, axes[a] + 1)
            pltpu.make_async_remote_copy(src_ref=blk, dst_ref=blk, send_sem=send.at[a], recv_sem=recv.at[a],
                                         device_id=(x, y, 1 - c), device_id_type=MESH).wait_recv()
        for cp in cps:
            cp.wait_send()

    return pl.pallas_call(
        body, name=name, in_specs=[ANY] * n, out_specs=[ANY] * n,
        out_shape=[jax.ShapeDtypeStruct(a.shape, a.dtype) for a in arrs],
        input_output_aliases={a: a for a in range(n)},
        scratch_shapes=[pltpu.SemaphoreType.DMA((n,)), pltpu.SemaphoreType.DMA((n,))],
    )(*arrs)


HBM = pl.BlockSpec(memory_space=pltpu.HBM)
SEM = pl.BlockSpec(memory_space=pltpu.SEMAPHORE)
EFFECT = pltpu.SideEffectType.DATAFLOW_SIDE_EFFECTING
PEERS = 4


def _split_copies(srcs, lands, send, recv, gather, axes=None):
    x, y, c, chips = _place()
    me = 2 * x + y
    if axes is not None:
        out = []
        for a in range(len(srcs)):
            sems = dict(send_sem=send.at[PEERS * a], recv_sem=recv.at[PEERS * a], device_id=(x, y, 1 - c), device_id_type=MESH)
            copy = pltpu.make_async_remote_copy(src_ref=_half(srcs[a], 1 - c, axes[a] + 1), dst_ref=lands[a], **sems)
            out.append((copy, copy))
        return out
    peers = [((*chip, c), 2 * chip[0] + chip[1]) for chip in chips] + ([((x, y, 1 - c), me)] if gather else [])
    out = []
    for a in range(len(srcs)):
        for j, (dev, k) in enumerate(peers):
            src = srcs[a] if gather else srcs[a].at[k]
            sems = dict(send_sem=send.at[PEERS * a + j], recv_sem=recv.at[PEERS * a + j], device_id=dev, device_id_type=MESH)
            out.append((pltpu.make_async_remote_copy(src_ref=src, dst_ref=lands[a].at[me], **sems),
                        pltpu.make_async_remote_copy(src_ref=src, dst_ref=lands[a].at[k], **sems)))
    return out


def _split_start(srcs, gather, after, name, axes=None):
    n = len(srcs)
    if axes is not None:
        lands = [lax.empty(_half_shape(s.shape, axes[a] + 1), s.dtype) for a, s in enumerate(srcs)]
    else:
        lands = [lax.empty(((N_CHIPS,) + s.shape) if gather else s.shape, s.dtype) for s in srcs]

    def body(*refs):
        send, recv = refs[2 * n + 1], refs[2 * n + 2]
        for start, _ in _split_copies(refs[:n], refs[n:2 * n], send, recv, gather, axes):
            start.start()
        refs[-1][...] = jnp.zeros_like(refs[-1])

    sems = pltpu.SemaphoreType.DMA((PEERS * n,))
    hbm = lambda a: pltpu.with_memory_space_constraint(a, pltpu.HBM)
    out = pl.pallas_call(
        body, name=name,
        out_shape=(sems, sems, *[pltpu.HBM(a.shape, a.dtype) for a in srcs + lands], jax.ShapeDtypeStruct((SUB, LANE), f32)),
        in_specs=[HBM] * (2 * n) + [ANY], out_specs=(SEM, SEM, *[HBM] * (2 * n), pl.BlockSpec(memory_space=pltpu.VMEM)),
        input_output_aliases={i: 2 + i for i in range(2 * n)},
        compiler_params=pltpu.CompilerParams(has_side_effects=EFFECT),
    )(*[hbm(a) for a in srcs + lands], after)
    return out[0], out[1], list(out[2:2 + n]), list(out[2 + n:2 + 2 * n]), out[-1]


def _split_wait(send, recv, srcs, lands, gather, after, name, axes=None):
    n = len(srcs)

    def body(*refs):
        for start, arrival in _split_copies(refs[:n], refs[n:2 * n], refs[2 * n], refs[2 * n + 1], gather, axes):
            start.wait_send()
            arrival.wait_recv()

    out = pl.pallas_call(
        body, name=name, out_shape=[pltpu.HBM(a.shape, a.dtype) for a in srcs + lands],
        in_specs=[HBM] * (2 * n) + [SEM, SEM, ANY], out_specs=[HBM] * (2 * n),
        input_output_aliases={i: i for i in range(2 * n)},
        compiler_params=pltpu.CompilerParams(has_side_effects=EFFECT),
    )(*srcs, *lands, send, recv, after)
    return list(out[:n]), list(out[n:])


N_DEV = 8


def _all_reduce_small(v, name):
    R = v.shape[0]

    def body(v_ref, o_ref, land_ref, send, recv):
        x, y, c, _ = _place()
        me = 4 * x + 2 * y + c
        land_ref[me] = v_ref[...]
        cps = []
        for m in range(1, N_DEV):
            px, py, pc = [(1 - q) if (m >> s) & 1 else q for q, s in ((x, 2), (y, 1), (c, 0))]
            cps.append((pltpu.make_async_remote_copy(src_ref=v_ref, dst_ref=land_ref.at[me], send_sem=send.at[m - 1],
                                                     recv_sem=recv.at[m - 1], device_id=(px, py, pc), device_id_type=MESH),
                        4 * px + 2 * py + pc, m))
        for cp, *_ in cps:
            cp.start()
        for cp, peer, m in cps:
            pltpu.make_async_remote_copy(src_ref=v_ref, dst_ref=land_ref.at[peer], send_sem=send.at[m - 1],
                                         recv_sem=recv.at[m - 1], device_id=(x, y, c), device_id_type=MESH).wait_recv()
        for cp, *_ in cps:
            cp.wait_send()
        acc = land_ref[0]
        for k in range(1, N_DEV):
            acc = acc + land_ref[k]
        o_ref[...] = acc

    vm = pl.BlockSpec(memory_space=pltpu.VMEM)
    return pl.pallas_call(
        body, name=name, in_specs=[vm], out_specs=vm, out_shape=jax.ShapeDtypeStruct(v.shape, f32),
        scratch_shapes=[pltpu.VMEM((N_DEV, R, LANE), f32), pltpu.SemaphoreType.DMA((N_DEV - 1,)),
                        pltpu.SemaphoreType.DMA((N_DEV - 1,))],
        compiler_params=pltpu.CompilerParams(vmem_limit_bytes=VMEM_LIMIT),
    )(v)


def _pack_small(arrs, mult=2 * SUB):
    flat = jnp.concatenate([a.reshape(-1) for a in arrs])
    rows = -(-flat.shape[0] // (LANE * mult)) * mult
    return jnp.pad(flat, (0, rows * LANE - flat.shape[0])).reshape(rows, LANE)


def _unpack_small(vec, shapes):
    flat, out, o = vec.reshape(-1), [], 0
    for s in shapes:
        n = int(np.prod(s))
        out.append(flat[o:o + n].reshape(s))
        o += n
    return out


REPL_SMALL = ("c_ctx", "b_mod", "q_norm", "k_norm", "c_norm", "d_conv_b", "d_norm_g", "d_norm_b", "ln_g", "ln_b")
SHARD_SMALL = ("b_conv", "c_gate_w2", "c_gate_b", "d_conv_w")
BIG = ("w_mod", "w_in", "w_br", "w_out")
ORDER = ("c_ctx", "w_mod", "b_mod", "w_in", "q_norm", "k_norm", "b_conv", "c_gate_w2", "c_gate_b", "c_norm", "d_conv_w",
         "d_conv_b", "d_norm_g", "d_norm_b", "w_br", "w_out", "ln_g", "ln_b")


def _unshard_last(g4, shard_shape):
    g = g4.reshape((N_CHIPS,) + tuple(shard_shape))
    g = jnp.moveaxis(g, 0, -2)
    return g.reshape(tuple(shard_shape[:-1]) + (N_CHIPS * shard_shape[-1],))


def _pieces_last(full):
    w = full.shape[-1] // N_CHIPS
    g = full.reshape(full.shape[:-1] + (N_CHIPS, w))
    return jnp.moveaxis(g, -2, 0).reshape(N_CHIPS, -1, w)


def kernel(x, c, ctx, c_ctx, w_mod, b_mod, w_in, q_norm, k_norm, b_conv, c_gate_w2, c_gate_b, c_norm, d_conv_w, d_conv_b, d_norm_g, d_norm_b, w_br, w_out, ln_g, ln_b, loss_target, m_c_ctx, m_w_mod, m_b_mod, m_w_in, m_q_norm, m_k_norm, m_b_conv, m_c_gate_w2, m_c_gate_b, m_c_norm, m_d_conv_w, m_d_conv_b, m_d_norm_g, m_d_norm_b, m_w_br, m_w_out, m_ln_g, m_ln_b, v_c_ctx, v_w_mod, v_b_mod, v_w_in, v_q_norm, v_k_norm, v_b_conv, v_c_gate_w2, v_c_gate_b, v_c_norm, v_d_conv_w, v_d_conv_b, v_d_norm_g, v_d_norm_b, v_w_br, v_w_out, v_ln_g, v_ln_b):
    W = dict(c_ctx=c_ctx, w_mod=w_mod, b_mod=b_mod, w_in=w_in, q_norm=q_norm, k_norm=k_norm, b_conv=b_conv,
             c_gate_w2=c_gate_w2, c_gate_b=c_gate_b, c_norm=c_norm, d_conv_w=d_conv_w, d_conv_b=d_conv_b,
             d_norm_g=d_norm_g, d_norm_b=d_norm_b, w_br=w_br, w_out=w_out, ln_g=ln_g, ln_b=ln_b)
    M = dict(c_ctx=m_c_ctx, w_mod=m_w_mod, b_mod=m_b_mod, w_in=m_w_in, q_norm=m_q_norm, k_norm=m_k_norm, b_conv=m_b_conv,
             c_gate_w2=m_c_gate_w2, c_gate_b=m_c_gate_b, c_norm=m_c_norm, d_conv_w=m_d_conv_w, d_conv_b=m_d_conv_b,
             d_norm_g=m_d_norm_g, d_norm_b=m_d_norm_b, w_br=m_w_br, w_out=m_w_out, ln_g=m_ln_g, ln_b=m_ln_b)
    V = dict(c_ctx=v_c_ctx, w_mod=v_w_mod, b_mod=v_b_mod, w_in=v_w_in, q_norm=v_q_norm, k_norm=v_k_norm, b_conv=v_b_conv,
             c_gate_w2=v_c_gate_w2, c_gate_b=v_c_gate_b, c_norm=v_c_norm, d_conv_w=v_d_conv_w, d_conv_b=v_d_conv_b,
             d_norm_g=v_d_norm_g, d_norm_b=v_d_norm_b, w_br=v_w_br, w_out=v_w_out, ln_g=v_ln_g, ln_b=v_ln_b)
    chip = 2 * lax.axis_index("x") + lax.axis_index("y")
    cidx = lax.axis_index("c").astype(jnp.int32).reshape(1)

    place = jnp.stack([chip, lax.axis_index("c")]).astype(jnp.int32)

    AXIS = dict(w_in=1, w_mod=0, w_br=0, w_out=0)
    ex = dict(w_in=lambda a: jnp.swapaxes(a, 1, 2), w_mod=lambda a: a.reshape(1, DEPTH * D, -1),
              w_br=lambda a: a.reshape(DEPTH, 4 * BRW, -1), w_out=lambda a: a)
    Wx, Mx, Vx = ({k: ex[k](P_[k]) for k in BIG} for P_ in (W, M, V))

    LAYER, MERGE = ("w_in", "w_br", "w_out"), ("w_br", "w_out")
    small_shard = _pack_small([W[k] for k in SHARD_SMALL])
    keys0 = ("w_in", "w_mod")
    got = _all_gather([Wx[k][0].astype(bf16) for k in keys0] + [small_shard], [AXIS[k] for k in keys0] + [0], "all_gather0")
    smalls = [_unpack_small(got[-1][s], [W[k].shape for k in SHARD_SMALL]) for s in range(N_CHIPS)]
    full = {k: jnp.concatenate([smalls[s][i] for s in range(N_CHIPS)], axis=-1) for i, k in enumerate(SHARD_SMALL)}
    wmod = got[1].reshape(N_CHIPS, DEPTH, D, 3 * D // N_CHIPS)
    ag0b = _split_start([Wx[k][0].astype(bf16) for k in MERGE], True, got[0], "all_gather0b_start")
    ag1 = _split_start([Wx[k][1].astype(bf16) for k in LAYER], True, ag0b[4], "all_gather1_start")

    def merge_form(w_br4, w_out4):
        return jnp.moveaxis(w_br4.reshape(N_CHIPS, 4, BRW, D // N_CHIPS), 0, 2).reshape(4, BRW, D), w_out4.reshape(D, D)

    def weights_of(l, h):
        if l == 0:
            return _group_weights(got[0]), lambda after: merge_form(*_split_wait(*ag0b[:4], True, after, "all_gather0b_wait")[1])
        g3 = _split_wait(*ag1[:4], True, h, "all_gather1_wait")[1]
        return _group_weights(g3[0]), lambda after: merge_form(g3[1], g3[2])

    red = {k: Wx[k].shape for k in BIG}
    flights, held = {}, {}

    def launch(tag, l, pieces, after=None):
        keys = list(pieces)
        land_a = _sibling_halves([pieces[k] for k in keys], [AXIS[k] for k in keys], f"rs_sibling_halves{tag}")
        pair = [_add_half(pieces[k], la, cidx, AXIS[k], f"rs_pair_sum{tag}_{k}") for k, la in zip(keys, land_a)]
        after = jnp.zeros((SUB, LANE), f32) if after is None else after
        flights[tag] = (l, keys, _split_start(pair, False, after, f"rs_chip_exchange{tag}_start"))
        return flights[tag][2][4]

    def land(tag, after):
        l, keys, flight = flights.pop(tag)
        pair, land_b = _split_wait(*flight[:4], False, after, f"rs_chip_exchange{tag}_wait")
        for k, lb, pr in zip(keys, land_b, pair):
            red[k] = _sum_chips(lb, pr, place, AXIS[k], l, red[k], f"rs_chip_sum{tag}_{k}")

    def grads_done(l, gl):
        if "wp" in gl:
            pieces = dict(w_in=_ungroup(gl["wp"]).reshape(N_CHIPS, SHARD, D))
            return launch("0c", 0, pieces) if l == 0 else launch("1", 1, {**pieces, **held.pop(1)})
        pieces = dict(w_br=gl["w_br"].reshape(N_CHIPS, 4 * BRW, D // N_CHIPS), w_out=gl["w_out"].reshape(N_CHIPS, D // N_CHIPS, D))
        if l == 0:
            return launch("0b", 0, pieces)
        held[1] = pieces
        return None

    loss, gx, g = _local_step(
        x[0], c, ctx[0], loss_target[0], c_ctx, wmod, b_mod, weights_of, q_norm, k_norm, full["b_conv"],
        full["c_gate_w2"], full["c_gate_b"], c_norm, full["d_conv_w"], d_conv_b, d_norm_g, d_norm_b,
        grads_done, ln_g, ln_b, tm=256, token=ag1[4])
    g["c_gate_w2"], g["c_gate_b"] = g.pop("w2"), g.pop("gb")
    loss = lax.psum(loss, ("x", "y", "c"))

    w_mod_pieces = g["w_mod"].reshape(N_CHIPS, DEPTH * D, 3 * D // N_CHIPS)
    g = {k: (jnp.stack(v) if isinstance(v, list) else v) for k, v in g.items() if k not in ("wp", "w_br", "w_out", "w_mod")}

    small_names = REPL_SMALL + SHARD_SMALL
    gs = _all_reduce_small(_pack_small([g[k] for k in small_names]), "all_reduce_small")
    gsm = dict(zip(small_names, _unpack_small(gs, [g[k].shape for k in small_names])))
    for k in SHARD_SMALL:
        wdt = W[k].shape[-1]
        gsm[k] = lax.dynamic_slice_in_dim(gsm[k], chip * wdt, wdt, axis=gsm[k].ndim - 1)

    grad, delta, new_m, new_v = {}, {}, {}, {}

    def adamw_big(keys, after):
        filled = _sibling_fill([red[k] for k in keys], [AXIS[k] for k in keys], "rs_sibling_fill_" + keys[0])
        for k, r in zip(keys, filled):
            back = (lambda a: jnp.swapaxes(a, 1, 2)) if k == "w_in" else (lambda a: a.reshape(W[k].shape))
            d_, m_, v_ = _adamw(Wx[k], r, Mx[k], Vx[k], f"adamw_{k}", after=after)
            grad[k], delta[k], new_m[k], new_v[k] = back(r), back(d_), back(m_), back(v_)
        return d_

    token = launch("0d", 0, {"w_mod": w_mod_pieces}, after=gs)
    land("1", gx)
    land("0b", gx)
    last = adamw_big(MERGE, token)
    shapes = [W[k].shape for k in small_names]
    d_, m_, v_ = _adamw(*[_pack_small([P_[k] for k in small_names])[None] for P_ in (W, gsm, M, V)], "adamw_small", after=last)
    for k, dd, mm_, vv in zip(small_names, _unpack_small(d_, shapes), _unpack_small(m_, shapes), _unpack_small(v_, shapes)):
        grad[k], delta[k], new_m[k], new_v[k] = gsm[k], dd, mm_, vv
    land("0c", d_)
    land("0d", d_)
    adamw_big(("w_in", "w_mod"), None)

    return (loss, gx[None], *[grad[k] for k in ORDER], *[delta[k] for k in ORDER], *[new_m[k] for k in ORDER],
            *[new_v[k] for k in ORDER])
```

```python
import functools

import jax
import jax.numpy as jnp
import numpy as np
from jax import lax
from jax.experimental import pallas as pl
from jax.experimental.pallas import tpu as pltpu

f32 = jnp.float32
bf16 = jnp.bfloat16

D = 1024
DEPTH = 2
GRID_W = 64
BRW = 512
HD = 128
A_HEADS = 4
C_HEADS = 4
C_KW = 256
C_RANK = 16
C_TAU = 16.0
CH = 128
KB = 3
KD = 31
ALPHA = (2 * DEPTH) ** 0.25
EPS = 1e-6
ROPE_THETA = 10000.0
N_IN = 10784
LR, B1, B2, AEPS, WD, STEP = 0.001, 0.9, 0.999, 1e-08, 0.01, 10

W_M, W_A, W_C, W_G = 4 * D + 4 * BRW, 1024, 5 * BRW, 1152
GROUPS = ("M", "A", "C", "G")
GROUP_W = dict(M=W_M, A=W_A, C=W_C, G=W_G)
M_GA, M_GB, M_GC, M_GD = 4 * D, 4 * D + BRW, 4 * D + 2 * BRW, 4 * D + 3 * BRW
A_K, A_V = 512, 768
G_K, G_V, G_R = 256, 512, 1024
CT = 5 * 128
S_Q, S_GA, S_B, S_C, S_X, S_GB, S_CQ, S_CV, S_GC, S_R, S_DA, S_DG, S_GD, S_MG = (
    0, 1024, 1536, 2048, 2560, 3072, 3584, 4096, 4608, 5120, 5152, 5664, 6176, 6688)

LANE = 128
SUB = 8
VMEM_LIMIT = 56 * 1024 * 1024
CONV_PAD = 16
GLA_SUB = 16
GLA_CLAMP = 60.0


def _cparams(sem, vmem=VMEM_LIMIT):
    return pltpu.CompilerParams(dimension_semantics=sem, vmem_limit_bytes=vmem)


def _dg(a, b, ca, cb):
    return lax.dot_general(a.astype(bf16), b.astype(bf16), (((ca,), (cb,)), ((), ())),
                           preferred_element_type=f32)


@jax.custom_vjp
def mm(a, b):
    return _dg(a, b, 1, 0)


mm.defvjp(lambda a, b: (_dg(a, b, 1, 0), (a, b)),
          lambda r, ct: (_dg(ct, r[1], 1, 1).astype(r[0].dtype), _dg(r[0], ct, 0, 0).astype(r[1].dtype)))


@jax.custom_vjp
def mm_nt(a, b):
    return _dg(a, b, 1, 1)


mm_nt.defvjp(lambda a, b: (_dg(a, b, 1, 1), (a, b)),
             lambda r, ct: (_dg(ct, r[1], 1, 0).astype(r[0].dtype), _dg(ct, r[0], 0, 0).astype(r[1].dtype)))


@jax.custom_vjp
def mm_tn(a, b):
    return _dg(a, b, 0, 0)


mm_tn.defvjp(lambda a, b: (_dg(a, b, 0, 0), (a, b)),
             lambda r, ct: (_dg(r[1], ct, 1, 1).astype(r[0].dtype), _dg(r[0], ct, 1, 0).astype(r[1].dtype)))


def _sigmoid(x):
    return 0.5 * jnp.tanh(0.5 * x) + 0.5


def _silu(x):
    return x * _sigmoid(x)


def _ln(x):
    mu = jnp.mean(x, -1, keepdims=True)
    xc = x - mu
    var = jnp.mean(xc * xc, -1, keepdims=True)
    return xc * lax.rsqrt(var + EPS)


def _rms(x, g):
    return x * lax.rsqrt(jnp.mean(x * x, -1, keepdims=True) + EPS) * g


@jax.custom_vjp
def _rope(x, cos_f, sin_a, sin_b):
    return x * cos_f + pltpu.roll(x, HD - 1, 1) * sin_a + pltpu.roll(x, 1, 1) * sin_b


def _rope_fwd(x, cos_f, sin_a, sin_b):
    return _rope(x, cos_f, sin_a, sin_b), (cos_f, sin_a, sin_b)


def _rope_bwd(r, ct):
    cos_f, sin_a, sin_b = r
    dx = ct * cos_f + pltpu.roll(ct * sin_a, 1, 1) + pltpu.roll(ct * sin_b, HD - 1, 1)
    return dx, jnp.zeros_like(cos_f), jnp.zeros_like(sin_a), jnp.zeros_like(sin_b)


_rope.defvjp(_rope_fwd, _rope_bwd)


def _row_ids(i, tm):
    return i * tm + lax.broadcasted_iota(jnp.int32, (tm, 1), 0)


def _partial_rows(ref, rows):
    n = len(rows)
    for k, r in enumerate(rows):
        ref[k:k + 1, :] = r
    ref[n:SUB, :] = jnp.zeros((SUB - n, ref.shape[-1]), f32)


def _matmul(a, b, mode, tm, tn, tk, name, out_dtype=f32, add=None, after=None):
    sect = a.ndim == 3
    a2 = (a.shape[1], a.shape[0] * a.shape[2]) if sect else a.shape
    if mode == "nn":
        (M, K), N = a2, b.shape[1]
        a_spec = pl.BlockSpec((None, tm, tk), lambda j, i, k: (k, i, 0)) if sect else pl.BlockSpec((tm, tk), lambda j, i, k: (i, k))
        b_spec = pl.BlockSpec((tk, tn), lambda j, i, k: (k, j))
        ca, cb = 1, 0
        assert not sect or tk == a.shape[2]
    elif mode == "nt":
        (M, K), N = a2, b.shape[0]
        assert not sect
        a_spec = pl.BlockSpec((tm, tk), lambda j, i, k: (i, k))
        b_spec = pl.BlockSpec((tn, tk), lambda j, i, k: (j, k))
        ca, cb = 1, 1
    else:
        (K, M), N = a2, b.shape[1]
        a_spec = pl.BlockSpec((None, tk, tm), lambda j, i, k: (i, k, 0)) if sect else pl.BlockSpec((tk, tm), lambda j, i, k: (k, i))
        b_spec = pl.BlockSpec((tk, tn), lambda j, i, k: (k, j))
        ca, cb = 0, 0
        assert not sect or tm == a.shape[2]
    assert M % tm == 0 and N % tn == 0 and K % tk == 0, (name, M, N, K, tm, tn, tk)
    nk = K // tk

    o_spec = pl.BlockSpec((tm, tn), lambda j, i, k: (i, j))

    def body(a_ref, b_ref, *rest):
        add_ref = rest[0] if add is not None else None
        o_ref, acc_ref = rest[-2:]
        k = pl.program_id(2)
        part = _dg(a_ref[...], b_ref[...], ca, cb)

        @pl.when(k == 0)
        def _():
            acc_ref[...] = part if add_ref is None else part + add_ref[...]

        @pl.when(k > 0)
        def _():
            acc_ref[...] += part

        @pl.when(k == nk - 1)
        def _():
            o_ref[...] = acc_ref[...].astype(o_ref.dtype)

    extra = ([] if add is None else [(o_spec, add)]) + ([] if after is None else [(pl.BlockSpec(memory_space=pl.ANY), after)])
    return pl.pallas_call(
        body, name=name, grid=(N // tn, M // tm, nk),
        in_specs=[a_spec, b_spec] + [s_ for s_, _ in extra], out_specs=o_spec,
        out_shape=jax.ShapeDtypeStruct((M, N), out_dtype),
        scratch_shapes=[pltpu.VMEM((tm, tn), f32)],
        compiler_params=_cparams(("parallel", "parallel", "arbitrary")),
    )(a, b, *[v_ for _, v_ in extra])


def _matmul_groups(a, b, tks, tm, name, after=None):
    keys = list(a)
    M = a[keys[0]].shape[-2]
    N = b[keys[0]].shape[1]
    count = {g: b[g].shape[0] // tks[g] for g in keys}
    first, total = {}, 0
    for g in keys:
        first[g], total = total, total + count[g]

    def k_of(g):
        return lambda s: jnp.clip(s - first[g], 0, count[g] - 1)

    a_specs = [pl.BlockSpec((None, tm, tks[g]), functools.partial(lambda i, s, kk: (kk(s), i, 0), kk=k_of(g)))
               if a[g].ndim == 3 else pl.BlockSpec((tm, tks[g]), functools.partial(lambda i, s, kk: (i, kk(s)), kk=k_of(g)))
               for g in keys]
    b_specs = [pl.BlockSpec((tks[g], N), functools.partial(lambda i, s, kk: (kk(s), 0), kk=k_of(g))) for g in keys]
    n = len(keys)

    def body(*refs):
        o_ref, acc_ref = refs[-2:]
        s = pl.program_id(1)

        @pl.when(s == 0)
        def _():
            acc_ref[...] = jnp.zeros_like(acc_ref)

        for j, g in enumerate(keys):
            @pl.when((s >= first[g]) & (s < first[g] + count[g]))
            def _(j=j):
                acc_ref[...] += _dg(refs[j][...], refs[n + j][...], 1, 0)

        @pl.when(s == total - 1)
        def _():
            o_ref[...] = acc_ref[...]

    extra = [] if after is None else [after]
    return pl.pallas_call(
        body, name=name, grid=(M // tm, total),
        in_specs=a_specs + b_specs + [pl.BlockSpec(memory_space=pl.ANY)] * len(extra),
        out_specs=pl.BlockSpec((tm, N), lambda i, s: (i, 0)),
        out_shape=jax.ShapeDtypeStruct((M, N), f32),
        scratch_shapes=[pltpu.VMEM((tm, N), f32)],
        compiler_params=_cparams(("parallel", "arbitrary")),
    )(*[a[g] for g in keys], *[b[g] for g in keys], *extra)


def _matmul_tn_batched(a, b, ns, name):
    B, K, M = a.shape
    N = b.shape[2] // ns

    def body(a_ref, b_ref, o_ref):
        o_ref[...] = _dg(a_ref[...], b_ref[...], 0, 0).astype(bf16)

    return pl.pallas_call(
        body, name=name, grid=(B, ns),
        in_specs=[pl.BlockSpec((None, K, M), lambda i, s: (i, 0, 0)), pl.BlockSpec((None, K, N), lambda i, s: (i, 0, s))],
        out_specs=pl.BlockSpec((None, None, M, N), lambda i, s: (s, i, 0, 0)),
        out_shape=jax.ShapeDtypeStruct((ns, B, M, N), bf16),
        compiler_params=_cparams(("parallel", "parallel")),
    )(a, b)


MOD_TN = 768


def _mod_fwd(cin, w_mod, b_mod):
    def body(c_ref, w_ref, b_ref, o_ref):
        o_ref[...] = mm(_silu(c_ref[...]), w_ref[...]) + b_ref[...]

    return pl.pallas_call(
        body, name="mod_fwd", grid=(DEPTH, 3 * D // MOD_TN),
        in_specs=[pl.BlockSpec((SUB, D), lambda l, j: (0, 0)),
                  pl.BlockSpec((None, None, D, MOD_TN), lambda l, j: (j, l, 0, 0)),
                  pl.BlockSpec((None, 1, MOD_TN), lambda l, j: (l, 0, j))],
        out_specs=pl.BlockSpec((None, SUB, MOD_TN), lambda l, j: (l, 0, j)),
        out_shape=jax.ShapeDtypeStruct((DEPTH, SUB, 3 * D), f32),
        compiler_params=_cparams(("parallel", "parallel")),
    )(cin, w_mod, b_mod.reshape(DEPTH, 1, 3 * D))


def _mod_bwd(cin, w_mod, dmodv):
    nj = 3 * D // MOD_TN

    def body(c_ref, w_ref, g_ref, dw_ref, dc_ref):
        _, vjp = jax.vjp(lambda c, w: mm(_silu(c), w), c_ref[...], w_ref[...].astype(f32))
        dc, dw = vjp(g_ref[...])
        dw_ref[...] = dw.astype(bf16)
        dc_ref[...] = dc

    return pl.pallas_call(
        body, name="mod_bwd", grid=(DEPTH, nj),
        in_specs=[pl.BlockSpec((SUB, D), lambda l, j: (0, 0)),
                  pl.BlockSpec((None, None, D, MOD_TN), lambda l, j: (j, l, 0, 0)),
                  pl.BlockSpec((None, SUB, MOD_TN), lambda l, j: (l, 0, j))],
        out_specs=[pl.BlockSpec((None, None, D, MOD_TN), lambda l, j: (j, l, 0, 0)),
                   pl.BlockSpec((None, None, SUB, D), lambda l, j: (l, j, 0, 0))],
        out_shape=[jax.ShapeDtypeStruct((nj, DEPTH, D, MOD_TN), bf16),
                   jax.ShapeDtypeStruct((DEPTH, nj, SUB, D), f32)],
        compiler_params=_cparams(("parallel", "parallel")),
    )(cin, w_mod, dmodv)


def _u_fn(h, m_l, m_c, isctx):
    n = _ln(h)
    shift = jnp.where(isctx, m_c[:, 0:D], m_l[:, 0:D])
    scale = jnp.where(isctx, m_c[:, D:2 * D], m_l[:, D:2 * D])
    return n * (1.0 + scale) + shift


def _ln_fwd(h, modv_l, tc, tm, name):
    T = h.shape[0]

    def body(h_ref, m_ref, u_ref):
        isctx = _row_ids(pl.program_id(0), tm) < tc
        u_ref[...] = _u_fn(h_ref[...], m_ref[0:1, :], m_ref[1:2, :], isctx).astype(bf16)

    return pl.pallas_call(
        body, name=name, grid=(T // tm,),
        in_specs=[pl.BlockSpec((tm, D), lambda i: (i, 0)), pl.BlockSpec((SUB, 3 * D), lambda i: (0, 0))],
        out_specs=pl.BlockSpec((tm, D), lambda i: (i, 0)),
        out_shape=jax.ShapeDtypeStruct((T, D), bf16),
        compiler_params=_cparams(("parallel",)),
    )(h, modv_l)


def _ln_bwd(du, h, dh_res, modv_l, tc, tm, name, latent_only=False):
    T = h.shape[0]
    nt, nct = T // tm, tc // tm

    def body(du_ref, h_ref, r_ref, m_ref, dh_ref, dm_ref):
        isctx = _row_ids(pl.program_id(0), tm) < tc
        _, vjp = jax.vjp(lambda h, ml, mc: _u_fn(h, ml, mc, isctx), h_ref[...], m_ref[0:1, :], m_ref[1:2, :])
        dh, dml, dmc = vjp(du_ref[...])
        dh_ref[...] = dh + r_ref[...]
        _partial_rows(dm_ref, [dml, dmc])

    dh_map = (lambda i: (jnp.maximum(i - nct, 0), 0)) if latent_only else (lambda i: (i, 0))
    return pl.pallas_call(
        body, name=name, grid=(nt,),
        in_specs=[pl.BlockSpec((tm, D), lambda i: (i, 0)), pl.BlockSpec((tm, D), lambda i: (i, 0)),
                  pl.BlockSpec((tm, D), lambda i: (i, 0)), pl.BlockSpec((SUB, 3 * D), lambda i: (0, 0))],
        out_specs=[pl.BlockSpec((tm, D), dh_map), pl.BlockSpec((None, SUB, 3 * D), lambda i: (i, 0, 0))],
        out_shape=[jax.ShapeDtypeStruct((T - tc if latent_only else T, D), f32), jax.ShapeDtypeStruct((nt, SUB, 3 * D), f32)],
        compiler_params=_cparams(("arbitrary",)),
    )(du, h, dh_res, modv_l)


def _prep_fn(q, k, qg, kg, cos_f, sin_a, sin_b):
    qs = [_rope(_rms(q[:, HD * i:HD * (i + 1)], qg), cos_f, sin_a, sin_b) * (HD ** -0.5) for i in range(A_HEADS)]
    ks = [_rope(_rms(k[:, HD * i:HD * (i + 1)], kg), cos_f, sin_a, sin_b) for i in range(A_HEADS // 2)]
    return jnp.concatenate(qs, 1), jnp.concatenate(ks, 1)


def _tok(tm, w, off):
    return pl.BlockSpec((tm, w), lambda i: (i, off // w))


def _vec(w):
    return pl.BlockSpec((1, w), lambda i: (0, 0))


def _prep_fwd(P, qg, kg, rope, tm, name):
    T = P.shape[0]

    def body(q_ref, k_ref, v_ref, qg_ref, kg_ref, c_ref, sa_ref, sb_ref, qn_ref, kn_ref, vb_ref):
        qn, kn = _prep_fn(q_ref[...].astype(f32), k_ref[...].astype(f32), qg_ref[...], kg_ref[...], c_ref[...], sa_ref[...],
                          sb_ref[...])
        qn_ref[...] = qn.astype(bf16)
        kn_ref[...] = kn.astype(bf16)
        vb_ref[...] = v_ref[...].astype(bf16)

    return pl.pallas_call(
        body, name=name, grid=(T // tm,),
        in_specs=[_tok(tm, 512, 0), _tok(tm, 256, A_K), _tok(tm, 256, A_V), _vec(HD), _vec(HD),
                  _tok(tm, HD, 0), _tok(tm, HD, 0), _tok(tm, HD, 0)],
        out_specs=[_tok(tm, 512, 0), _tok(tm, 256, 0), _tok(tm, 256, 0)],
        out_shape=[jax.ShapeDtypeStruct((T, 512), bf16), jax.ShapeDtypeStruct((T, 256), bf16),
                   jax.ShapeDtypeStruct((T, 256), bf16)],
        compiler_params=_cparams(("parallel",)),
    )(P, P, P, qg, kg, *rope)


def _prep_bwd(P, dqn, dkn, dv, qg, kg, rope, tm, name):
    T = P.shape[0]
    nt = T // tm

    def body(q_ref, k_ref, dq_ref, dk_ref, dv_ref, qg_ref, kg_ref, c_ref, sa_ref, sb_ref, o_ref, og_ref):
        tabs = (c_ref[...], sa_ref[...], sb_ref[...])
        _, vjp = jax.vjp(lambda q, k, a, b: _prep_fn(q, k, a, b, *tabs), q_ref[...].astype(f32), k_ref[...].astype(f32),
                         qg_ref[...], kg_ref[...])
        dq, dk, dqg, dkg = vjp((dq_ref[...], dk_ref[...]))
        o_ref[:, 0:A_K] = dq.astype(bf16)
        o_ref[:, A_K:A_V] = dk.astype(bf16)
        o_ref[:, A_V:W_A] = dv_ref[...].astype(bf16)
        _partial_rows(og_ref, [dqg, dkg])

    return pl.pallas_call(
        body, name=name, grid=(nt,),
        in_specs=[_tok(tm, 512, 0), _tok(tm, 256, A_K), _tok(tm, 512, 0), _tok(tm, 256, 0), _tok(tm, 256, 0),
                  _vec(HD), _vec(HD), _tok(tm, HD, 0), _tok(tm, HD, 0), _tok(tm, HD, 0)],
        out_specs=[_tok(tm, W_A, 0), pl.BlockSpec((None, SUB, HD), lambda i: (i, 0, 0))],
        out_shape=[jax.ShapeDtypeStruct((T, W_A), bf16), jax.ShapeDtypeStruct((nt, SUB, HD), f32)],
        compiler_params=_cparams(("parallel",)),
    )(P, P, dqn, dkn, dv, qg, kg, *rope)


def _attn_fn(q, k, v, lim):
    col = lax.broadcasted_iota(jnp.int32, (1, k.shape[0]), 1)
    s = mm_nt(q, k) + jnp.where(col < lim, 0.0, -1e30)
    m = lax.stop_gradient(jnp.max(s, -1, keepdims=True))
    e = jnp.exp(s - m)
    p = e * (1.0 / jnp.sum(e, -1, keepdims=True))
    return mm(p, v)


def _attn_fwd(qn, kn, vb, tc, tq, name):
    T = qn.shape[0]

    def body(q_ref, k_ref, v_ref, o_ref):
        lim = jnp.where(pl.program_id(1) * tq < tc, tc, T)
        o_ref[...] = _attn_fn(q_ref[...], k_ref[...], v_ref[...], lim)

    return pl.pallas_call(
        body, name=name, grid=(A_HEADS, T // tq),
        in_specs=[pl.BlockSpec((tq, HD), lambda h, i: (i, h)), pl.BlockSpec((T, HD), lambda h, i: (0, h // 2)),
                  pl.BlockSpec((T, HD), lambda h, i: (0, h // 2))],
        out_specs=pl.BlockSpec((tq, HD), lambda h, i: (i, h)),
        out_shape=jax.ShapeDtypeStruct((T, 512), f32),
        compiler_params=_cparams(("parallel", "parallel")),
    )(qn, kn, vb)


def _attn_bwd(qn, kn, vb, dya, tc, tq, name):
    T = qn.shape[0]

    def body(q_ref, k_ref, v_ref, g_ref, dq_ref, dk_ref, dv_ref):
        first = (pl.program_id(1) == 0) & (pl.program_id(2) == 0)
        lim = jnp.where(pl.program_id(2) * tq < tc, tc, T)
        _, vjp = jax.vjp(lambda q, k, v: _attn_fn(q, k, v, lim), q_ref[...].astype(f32), k_ref[...].astype(f32),
                         v_ref[...].astype(f32))
        dq, dk, dv = vjp(g_ref[...])
        dq_ref[...] = dq

        @pl.when(first)
        def _():
            dk_ref[...] = dk
            dv_ref[...] = dv

        @pl.when(jnp.logical_not(first))
        def _():
            dk_ref[...] += dk
            dv_ref[...] += dv

    qspec = pl.BlockSpec((tq, HD), lambda kv, g, i: (i, 2 * kv + g))
    kspec = pl.BlockSpec((T, HD), lambda kv, g, i: (0, kv))
    return pl.pallas_call(
        body, name=name, grid=(A_HEADS // 2, 2, T // tq),
        in_specs=[qspec, kspec, kspec, qspec], out_specs=[qspec, kspec, kspec],
        out_shape=[jax.ShapeDtypeStruct((T, 512), f32), jax.ShapeDtypeStruct((T, 256), f32),
                   jax.ShapeDtypeStruct((T, 256), f32)],
        compiler_params=_cparams(("parallel", "arbitrary", "arbitrary")),
    )(qn, kn, vb, dya)


def _conv_rows(tc, tl):
    return CONV_PAD + tc + CONV_PAD + tl + CONV_PAD


def _fill_pad(pad_ref, val, tc, tl):
    z = jnp.zeros((CONV_PAD, LANE), f32)
    pad_ref[0:CONV_PAD, :] = z
    pad_ref[CONV_PAD:CONV_PAD + tc, :] = val[0:tc]
    pad_ref[CONV_PAD + tc:2 * CONV_PAD + tc, :] = z
    pad_ref[2 * CONV_PAD + tc:2 * CONV_PAD + tc + tl, :] = val[tc:tc + tl]
    pad_ref[2 * CONV_PAD + tc + tl:3 * CONV_PAD + tc + tl, :] = z


def _conv_apply(pad_ref, w_ref, K, tc, tl, rc, emit, flip=False):
    half = K // 2
    for seg0, off, n in ((0, CONV_PAD, tc), (tc, 2 * CONV_PAD + tc, tl)):
        for r0 in range(0, n, rc):
            acc = None
            for k in range(K):
                sh = (half - k) if flip else (k - half)
                term = pad_ref[pl.ds(off + r0 + sh, rc), :] * w_ref[k:k + 1, :]
                acc = term if acc is None else acc + term
            emit(seg0 + r0, acc)


def _conv_wgrad(pad_ref, dy_ref, K, tc, tl, rc, dw_ref):
    half = K // 2
    for k in range(K):
        acc = jnp.zeros((1, LANE), f32)
        for seg0, off, n in ((0, CONV_PAD, tc), (tc, 2 * CONV_PAD + tc, tl)):
            for r0 in range(0, n, rc):
                acc = acc + jnp.sum(pad_ref[pl.ds(off + r0 + k - half, rc), :] * dy_ref[pl.ds(seg0 + r0, rc), :],
                                    axis=0, keepdims=True)
        dw_ref[k:k + 1, :] = acc


def _col(T, off):
    return pl.BlockSpec((T, LANE), lambda j: (0, off // LANE + j))


C_B, C_C, C_X, C_A, C_G = range(5)
N_SEC = 5


class _Sections:
    def __init__(self, refs):
        self.refs = refs

    def __getitem__(self, idx):
        rows, sec = idx
        return self.refs[sec][rows, :].astype(f32)

    def __setitem__(self, idx, val):
        rows, sec = idx
        self.refs[sec, rows, :] = val


def _sec_specs(T):
    return [pl.BlockSpec((T, LANE), functools.partial(lambda j, s: (0, s * (BRW // LANE) + j), s=s)) for s in range(N_SEC)]


def _conv_fwd(P, wb, wd, bd, tc, tl, rc, name):
    T = tc + tl

    def body(*refs):
        p_ref = _Sections(refs[:N_SEC])
        wb_ref, wd_ref, bd_ref, yb_ref, hh_ref, pad_ref = refs[N_SEC:]
        _fill_pad(pad_ref, p_ref[:, C_C] * p_ref[:, C_X], tc, tl)

        def emit_b(r0, y):
            yb_ref[pl.ds(r0, rc), :] = y * p_ref[pl.ds(r0, rc), C_B]

        _conv_apply(pad_ref, wb_ref, KB, tc, tl, rc, emit_b)
        _fill_pad(pad_ref, p_ref[:, C_A] * _sigmoid(p_ref[:, C_G]), tc, tl)

        def emit_d(r0, y):
            hh_ref[pl.ds(r0, rc), :] = y + bd_ref[...]

        _conv_apply(pad_ref, wd_ref, KD, tc, tl, rc, emit_d)

    return pl.pallas_call(
        body, name=name, grid=(BRW // LANE,),
        in_specs=_sec_specs(T) + [pl.BlockSpec((KB, LANE), lambda j: (0, j)), pl.BlockSpec((KD, LANE), lambda j: (0, j)),
                                  pl.BlockSpec((1, LANE), lambda j: (0, j))],
        out_specs=[_col(T, 0), _col(T, 0)],
        out_shape=[jax.ShapeDtypeStruct((T, BRW), f32), jax.ShapeDtypeStruct((T, BRW), f32)],
        scratch_shapes=[pltpu.VMEM((_conv_rows(tc, tl), LANE), f32)],
        compiler_params=_cparams(("parallel",)),
    )(*[P] * N_SEC, wb, wd, bd)


def _conv_bwd(P, dyb, dhh, wb, wd, tc, tl, rc, name):
    T = tc + tl

    def body(*refs):
        p_ref = _Sections(refs[:N_SEC])
        dyb_ref, dhh_ref, wb_ref, wd_ref, dp3_ref, dwb_ref, dwd_ref, dbd_ref, pad_ref, pad2_ref, tmp_ref = refs[N_SEC:]
        dp_ref = _Sections(dp3_ref)
        _fill_pad(pad_ref, p_ref[:, C_C] * p_ref[:, C_X], tc, tl)

        def emit_cv(r0, y):
            dp_ref[pl.ds(r0, rc), C_B] = (y * dyb_ref[pl.ds(r0, rc), :]).astype(bf16)

        _conv_apply(pad_ref, wb_ref, KB, tc, tl, rc, emit_cv)
        tmp_ref[...] = dyb_ref[...] * p_ref[:, C_B]
        _conv_wgrad(pad_ref, tmp_ref, KB, tc, tl, rc, dwb_ref)
        _fill_pad(pad2_ref, tmp_ref[...], tc, tl)

        def emit_ds(r0, y):
            dp_ref[pl.ds(r0, rc), C_C] = (y * p_ref[pl.ds(r0, rc), C_X]).astype(bf16)
            dp_ref[pl.ds(r0, rc), C_X] = (y * p_ref[pl.ds(r0, rc), C_C]).astype(bf16)

        _conv_apply(pad2_ref, wb_ref, KB, tc, tl, rc, emit_ds, flip=True)
        _fill_pad(pad_ref, p_ref[:, C_A] * _sigmoid(p_ref[:, C_G]), tc, tl)
        _conv_wgrad(pad_ref, dhh_ref, KD, tc, tl, rc, dwd_ref)
        dbd_ref[...] = jnp.sum(dhh_ref[...], axis=0, keepdims=True)
        _fill_pad(pad2_ref, dhh_ref[...], tc, tl)

        def emit_d2(r0, y):
            sg = _sigmoid(p_ref[pl.ds(r0, rc), C_G])
            a = p_ref[pl.ds(r0, rc), C_A]
            dp_ref[pl.ds(r0, rc), C_A] = (y * sg).astype(bf16)
            dp_ref[pl.ds(r0, rc), C_G] = (y * a * sg * (1.0 - sg)).astype(bf16)

        _conv_apply(pad2_ref, wd_ref, KD, tc, tl, rc, emit_d2, flip=True)

    return pl.pallas_call(
        body, name=name, grid=(BRW // LANE,),
        in_specs=_sec_specs(T) + [_col(T, 0), _col(T, 0),
                                  pl.BlockSpec((KB, LANE), lambda j: (0, j)), pl.BlockSpec((KD, LANE), lambda j: (0, j))],
        out_specs=[pl.BlockSpec((N_SEC, T, LANE), lambda j: (0, 0, j)), pl.BlockSpec((KB, LANE), lambda j: (0, j)),
                   pl.BlockSpec((KD, LANE), lambda j: (0, j)), pl.BlockSpec((1, LANE), lambda j: (0, j))],
        out_shape=[jax.ShapeDtypeStruct((N_SEC, T, BRW), bf16), jax.ShapeDtypeStruct((KB, BRW), f32),
                   jax.ShapeDtypeStruct((KD, BRW), f32), jax.ShapeDtypeStruct((1, BRW), f32)],
        scratch_shapes=[pltpu.VMEM((_conv_rows(tc, tl), LANE), f32), pltpu.VMEM((_conv_rows(tc, tl), LANE), f32),
                        pltpu.VMEM((T, LANE), f32)],
        compiler_params=_cparams(("parallel",)),
    )(*[P] * N_SEC, dyb, dhh, wb, wd)


def _gla_chunk(q, k, v, r, w2, b2, st, isfwd):
    z = mm(r, w2) + b2
    g = jax.nn.log_sigmoid(z[:, 0:C_KW] if isfwd else z[:, C_KW:2 * C_KW]) / C_TAU
    ri = lax.broadcasted_iota(jnp.int32, (CH, CH), 0)
    ci = lax.broadcasted_iota(jnp.int32, (CH, CH), 1)
    tri = ((ci <= ri) if isfwd else (ci >= ri)).astype(f32)
    cum = jnp.dot(tri, g, preferred_element_type=f32, precision=lax.Precision.HIGHEST)
    last = jnp.sum(g, axis=0, keepdims=True)
    q = q * (C_KW // C_HEADS) ** -0.5
    hv = lax.broadcasted_iota(jnp.int32, (BRW, C_KW), 0) // (BRW // C_HEADS)
    hk = lax.broadcasted_iota(jnp.int32, (BRW, C_KW), 1) // (C_KW // C_HEADS)
    st_new = st * jnp.exp(last) + jnp.where(hv == hk, mm_tn(v, k * jnp.exp(last - cum)), 0.0)
    o = mm_nt(q * jnp.exp(cum), st)
    rowi = lax.broadcasted_iota(jnp.int32, (CH, C_KW), 0)
    srow = lax.broadcasted_iota(jnp.int32, (C_HEADS * CH, C_KW), 0)
    slane = lax.broadcasted_iota(jnp.int32, (C_HEADS * CH, C_KW), 1)
    own_lanes = srow // CH == slane // (C_KW // C_HEADS)
    pos = lax.broadcasted_iota(jnp.int32, (C_HEADS * CH, CH), 0) % CH
    key = lax.broadcasted_iota(jnp.int32, (C_HEADS * CH, CH), 1)
    scores = jnp.zeros((C_HEADS * CH, CH), f32)
    for a in range(CH // GLA_SUB):
        idx = GLA_SUB * a - 1 if isfwd else GLA_SUB * (a + 1)
        ref = jnp.sum(jnp.where(rowi == idx, cum, 0.0), axis=0, keepdims=True)
        qa = q * jnp.exp(jnp.minimum(cum - ref, 0.0))
        ka = k * jnp.exp(jnp.minimum(ref - cum, GLA_CLAMP))
        s = mm_nt(jnp.where(own_lanes, jnp.concatenate([qa] * C_HEADS, axis=0), 0.0), ka)
        scores = scores + jnp.where(pos // GLA_SUB == a, s, 0.0)
    scores = jnp.where((key <= pos) if isfwd else (key >= pos), scores, 0.0)
    vw = BRW // C_HEADS
    o = o + jnp.concatenate([mm(scores[CH * hd:CH * (hd + 1)], v[:, vw * hd:vw * (hd + 1)]) for hd in range(C_HEADS)],
                            axis=1)
    return o, st_new


def _gla_chunk_of(d, n, nc, nch):
    back = jnp.where(n < nc, nc - 1 - n, nch - 1 - (n - nc))
    return jnp.where(d == 0, n, back)


def _gla_fwd(P, w2, b2, tc, name):
    T = P.shape[0]
    nch, nc = T // CH, tc // CH

    back = lambda n: _gla_chunk_of(1, n, nc, nch)

    def body(pf_ref, pb_ref, w_ref, b_ref, of_ref, ob_ref, ssf_ref, ssb_ref, stf_ref, stb_ref):
        @pl.when(pl.program_id(0) == 0)
        def _():
            stf_ref[...] = jnp.zeros_like(stf_ref)
            stb_ref[...] = jnp.zeros_like(stb_ref)

        for p_ref, o_ref, ss_ref, st_ref, isfwd in ((pf_ref, of_ref, ssf_ref, stf_ref, True),
                                                    (pb_ref, ob_ref, ssb_ref, stb_ref, False)):
            st = st_ref[...]
            ss_ref[...] = st
            p = p_ref[...].astype(f32)
            o, st_new = _gla_chunk(p[:, 0:G_K], p[:, G_K:G_V], p[:, G_V:G_R], p[:, G_R:W_G], w_ref[...], b_ref[...], st, isfwd)
            o_ref[...] = o
            st_ref[...] = st_new

    sd = jax.ShapeDtypeStruct
    return pl.pallas_call(
        body, name=name, grid=(nch,),
        in_specs=[pl.BlockSpec((CH, W_G), lambda n: (n, 0)), pl.BlockSpec((CH, W_G), lambda n: (back(n), 0)),
                  pl.BlockSpec((LANE, 512), lambda n: (0, 0)), pl.BlockSpec((1, 512), lambda n: (0, 0))],
        out_specs=[pl.BlockSpec((CH, BRW), lambda n: (n, 0)), pl.BlockSpec((CH, BRW), lambda n: (back(n), 0)),
                   pl.BlockSpec((None, BRW, C_KW), lambda n: (n, 0, 0)), pl.BlockSpec((None, BRW, C_KW), lambda n: (n, 0, 0))],
        out_shape=[sd((T, BRW), f32), sd((T, BRW), f32), sd((nch, BRW, C_KW), f32), sd((nch, BRW, C_KW), f32)],
        scratch_shapes=[pltpu.VMEM((BRW, C_KW), f32), pltpu.VMEM((BRW, C_KW), f32)],
        compiler_params=_cparams(("arbitrary",)),
    )(P, P, w2, b2)


def _gla_bwd(P, w2, b2, ssave, doc, tc, name):
    T = P.shape[0]
    nch, nc = T // CH, tc // CH

    fwd_chunk = lambda m: nch - 1 - m
    back_chunk = lambda m: _gla_chunk_of(1, nch - 1 - m, nc, nch)

    def body(pf_ref, pb_ref, w_ref, b_ref, ssf_ref, ssb_ref, gf_ref, gb_ref, dpf_ref, dpb_ref, dw_ref, db_ref,
             dstf_ref, dstb_ref):
        m = pl.program_id(0)

        @pl.when(m == 0)
        def _():
            dstf_ref[...] = jnp.zeros_like(dstf_ref)
            dstb_ref[...] = jnp.zeros_like(dstb_ref)

        dw_sum, db_sum = None, None
        for p_ref, ss_ref, g_ref, dp_ref, dst_ref, isfwd in ((pf_ref, ssf_ref, gf_ref, dpf_ref, dstf_ref, True),
                                                             (pb_ref, ssb_ref, gb_ref, dpb_ref, dstb_ref, False)):
            p = p_ref[...].astype(f32)
            _, vjp = jax.vjp(lambda q, k, v, r, w, b, st: _gla_chunk(q, k, v, r, w, b, st, isfwd),
                             p[:, 0:G_K], p[:, G_K:G_V], p[:, G_V:G_R], p[:, G_R:W_G], w_ref[...], b_ref[...], ss_ref[...])
            dq, dk, dv, dr, dw, db, dst = vjp((g_ref[...], dst_ref[...]))
            dp_ref[:, 0:G_K] = dq
            dp_ref[:, G_K:G_V] = dk
            dp_ref[:, G_V:G_R] = dv
            dp_ref[:, G_R:W_G] = dr
            dst_ref[...] = dst
            dw_sum = dw if dw_sum is None else dw_sum + dw
            db_sum = db if db_sum is None else db_sum + db

        @pl.when(m == 0)
        def _():
            dw_ref[...] = dw_sum
            _partial_rows(db_ref, [db_sum])

        @pl.when(m > 0)
        def _():
            dw_ref[...] += dw_sum
            db_ref[0:1, :] += db_sum

    ssf, ssb = ssave
    chunk_f = lambda w: pl.BlockSpec((CH, w), lambda m: (fwd_chunk(m), 0))
    chunk_b = lambda w: pl.BlockSpec((CH, w), lambda m: (back_chunk(m), 0))
    state = pl.BlockSpec((None, BRW, C_KW), lambda m: (nch - 1 - m, 0, 0))
    sd = jax.ShapeDtypeStruct
    return pl.pallas_call(
        body, name=name, grid=(nch,),
        in_specs=[chunk_f(W_G), chunk_b(W_G), pl.BlockSpec((LANE, 512), lambda m: (0, 0)), pl.BlockSpec((1, 512), lambda m: (0, 0)),
                  state, state, chunk_f(BRW), chunk_b(BRW)],
        out_specs=[chunk_f(W_G), chunk_b(W_G), pl.BlockSpec((LANE, 512), lambda m: (0, 0)), pl.BlockSpec((SUB, 512), lambda m: (0, 0))],
        out_shape=[sd((T, W_G), f32), sd((T, W_G), f32), sd((LANE, 512), f32), sd((SUB, 512), f32)],
        scratch_shapes=[pltpu.VMEM((BRW, C_KW), f32), pltpu.VMEM((BRW, C_KW), f32)],
        compiler_params=_cparams(("arbitrary",)),
    )(P, P, w2, b2, ssf, ssb, doc, doc)


def _sum_dirs(a, b, tm, name):
    T, W = a.shape

    def body(a_ref, b_ref, o_ref):
        o_ref[...] = (a_ref[...] + b_ref[...]).astype(bf16)

    spec = pl.BlockSpec((tm, W), lambda i: (i, 0))
    return pl.pallas_call(
        body, name=name, grid=(T // tm,), in_specs=[spec, spec], out_specs=spec,
        out_shape=jax.ShapeDtypeStruct((T, W), bf16),
        compiler_params=_cparams(("parallel",)),
    )(a, b)


def _merge_fn(h, m_l, m_c, isctx, ya, ga, yb, gb, of, ob, gc, hh, gd, mg, es, ey, cn, dng, dnb, lg, lb, wbr, wout):
    oc = of + ob
    yc = jnp.concatenate([_rms(oc[:, HD * i:HD * (i + 1)], cn[:, HD * i:HD * (i + 1)]) for i in range(C_HEADS)], 1)
    brs = [ya * _silu(ga), yb * _silu(gb), yc * _silu(gc), _silu(_ln(hh) * dng + dnb) * _silu(gd)]
    acc = None
    for i in range(4):
        t = _sigmoid(mg[:, D * i:D * (i + 1)]) * (mm(brs[i], wbr[i]) + es[i])
        acc = t if acc is None else acc + t
    y = mm(acc, wout) + ey
    gate = jnp.where(isctx, m_c[:, 2 * D:3 * D], m_l[:, 2 * D:3 * D])
    hn = _ln(ALPHA * h + gate * y) * lg + lb
    return hn, (brs, acc)


def _merge_specs(tm):
    t = lambda w, off=0: _tok(tm, w, off)
    return [t(D), pl.BlockSpec((SUB, 3 * D), lambda i: (0, 0)),
            t(BRW), t(BRW, M_GA), t(BRW), t(BRW, M_GB),
            t(BRW), t(BRW),
            t(BRW, M_GC), t(BRW), t(BRW, M_GD), t(4 * D, 0),
            _vec(BRW), _vec(BRW), _vec(BRW), _vec(D), _vec(D),
            pl.BlockSpec((4, BRW, D), lambda i: (0, 0, 0)), pl.BlockSpec((D, D), lambda i: (0, 0))]


def _merge_fwd(h, modv_l, ya, yb, o2, hh, P, cn, dng, dnb, lg, lb, wbr, wout, tc, tm, name):
    T = h.shape[0]

    def body(h_ref, m_ref, ya_ref, ga_ref, yb_ref, gb_ref, of_ref, ob_ref, gc_ref, hh_ref, gd_ref, mg_ref,
             cn_ref, dng_ref, dnb_ref, lg_ref, lb_ref, wbr_ref, wout_ref, o_ref):
        isctx = _row_ids(pl.program_id(0), tm) < tc
        zero = jnp.zeros((tm, D), f32)
        up = lambda r: r[...].astype(f32)
        hn, _ = _merge_fn(h_ref[...], m_ref[0:1, :], m_ref[1:2, :], isctx, ya_ref[...], up(ga_ref), yb_ref[...],
                          up(gb_ref), of_ref[...], ob_ref[...], up(gc_ref), hh_ref[...], up(gd_ref), up(mg_ref),
                          [zero] * 4, zero, cn_ref[...], dng_ref[...], dnb_ref[...], lg_ref[...], lb_ref[...],
                          [wbr_ref[i] for i in range(4)], wout_ref[...])
        o_ref[...] = hn

    return pl.pallas_call(
        body, name=name, grid=(T // tm,),
        in_specs=_merge_specs(tm), out_specs=_tok(tm, D, 0),
        out_shape=jax.ShapeDtypeStruct((T, D), f32),
        compiler_params=_cparams(("parallel",)),
    )(h, modv_l, ya, P, yb, P, o2[0], o2[1], P, hh, P, P, cn, dng, dnb, lg, lb, wbr, wout)


def _merge_bwd(dhn, h, modv_l, ya, yb, o2, hh, P, cn, dng, dnb, lg, lb, wbr, wout, tc, tm, name):
    T = h.shape[0]
    nt = T // tm

    def body(g_ref, h_ref, m_ref, ya_ref, ga_ref, yb_ref, gb_ref, of_ref, ob_ref, gc_ref, hh_ref, gd_ref, mg_ref,
             cn_ref, dng_ref, dnb_ref, lg_ref, lb_ref, wbr_ref, wout_ref,
             dh_ref, dm_ref, dya_ref, dyb_ref, doc_ref, dhh_ref, dp_ref,
             br_ref, z_ref, acc_ref, dy_ref, dv5_ref, dvd_ref):
        isctx = _row_ids(pl.program_id(0), tm) < tc
        zero = jnp.zeros((tm, D), f32)
        wbr_v = [wbr_ref[i] for i in range(4)]
        wout_v = wout_ref[...]
        up = lambda r: r[...].astype(f32)

        def fn(h, ml, mc, ya, ga, yb, gb, oc, gc, hh, gd, mg, e0, e1, e2, e3, ey, cn, dng, dnb, lg, lb):
            return _merge_fn(h, ml, mc, isctx, ya, ga, yb, gb, oc, jnp.zeros_like(oc), gc, hh, gd, mg,
                             [e0, e1, e2, e3], ey, cn, dng, dnb, lg, lb, wbr_v, wout_v)

        _, vjp, (brs, acc) = jax.vjp(
            fn, h_ref[...], m_ref[0:1, :], m_ref[1:2, :], ya_ref[...], up(ga_ref), yb_ref[...], up(gb_ref),
            of_ref[...] + ob_ref[...], up(gc_ref), hh_ref[...], up(gd_ref), up(mg_ref), zero, zero, zero, zero, zero,
            cn_ref[...], dng_ref[...], dnb_ref[...], lg_ref[...], lb_ref[...], has_aux=True)
        (dh, dml, dmc, dya, dga, dyb, dgb, doc, dgc, dhh, dgd, dmg, z0, z1, z2, z3, dy,
         dcn, ddng, ddnb, dlg, dlb) = vjp(g_ref[...])
        dh_ref[...] = dh
        _partial_rows(dm_ref, [dml, dmc])
        dya_ref[...] = dya
        dyb_ref[...] = dyb
        doc_ref[...] = doc
        dhh_ref[...] = dhh
        dp_ref[:, 0:M_GA] = dmg.astype(bf16)
        dp_ref[:, M_GA:M_GB] = dga.astype(bf16)
        dp_ref[:, M_GB:M_GC] = dgb.astype(bf16)
        dp_ref[:, M_GC:M_GD] = dgc.astype(bf16)
        dp_ref[:, M_GD:W_M] = dgd.astype(bf16)
        for i, z in enumerate((z0, z1, z2, z3)):
            br_ref[i] = brs[i].astype(bf16)
            z_ref[i] = z.astype(bf16)
        acc_ref[...] = acc.astype(bf16)
        dy_ref[...] = dy.astype(bf16)
        _partial_rows(dv5_ref, [dcn, ddng, ddnb])
        _partial_rows(dvd_ref, [dlg, dlb])

    t = lambda w: _tok(tm, w, 0)
    part = lambda w: pl.BlockSpec((None, SUB, w), lambda i: (i, 0, 0))
    sd = jax.ShapeDtypeStruct
    return pl.pallas_call(
        body, name=name, grid=(nt,),
        in_specs=[t(D)] + _merge_specs(tm),
        out_specs=[t(D), part(3 * D)] + [t(BRW)] * 4 + [t(W_M),
                   pl.BlockSpec((4, tm, BRW), lambda i: (0, i, 0)), pl.BlockSpec((4, tm, D), lambda i: (0, i, 0)),
                   t(D), t(D), part(BRW), part(D)],
        out_shape=[sd((T, D), f32), sd((nt, SUB, 3 * D), f32)] + [sd((T, BRW), f32)] * 4 + [sd((T, W_M), bf16),
                   sd((4, T, BRW), bf16), sd((4, T, D), bf16), sd((T, D), bf16), sd((T, D), bf16),
                   sd((nt, SUB, BRW), f32), sd((nt, SUB, D), f32)],
        compiler_params=_cparams(("parallel",)),
    )(dhn, h, modv_l, ya, P, yb, P, o2[0], o2[1], P, hh, P, P, cn, dng, dnb, lg, lb, wbr, wout)


def _loss_kernel(h, tgt, tc, tm, name):
    T = h.shape[0]
    nt = T // tm
    nct = tc // tm

    def body(h_ref, t_ref, d_ref, l_ref):
        i = pl.program_id(0)
        err = h_ref[...] - t_ref[...]
        lat = (i >= nct).astype(f32)
        d_ref[...] = err * (lat / D)
        l_ref[...] = jnp.zeros((SUB, LANE), f32) + lat * 0.5 * jnp.sum(err * err) / D

    return pl.pallas_call(
        body, name=name, grid=(nt,),
        in_specs=[pl.BlockSpec((tm, D), lambda i: (i, 0)),
                  pl.BlockSpec((tm, D), lambda i: (jnp.maximum(i - nct, 0), 0))],
        out_specs=[pl.BlockSpec((tm, D), lambda i: (i, 0)), pl.BlockSpec((None, SUB, LANE), lambda i: (i, 0, 0))],
        out_shape=[jax.ShapeDtypeStruct((T, D), f32), jax.ShapeDtypeStruct((nt, SUB, LANE), f32)],
        compiler_params=_cparams(("parallel",)),
    )(h, tgt)


def _rope_tables(tc, tl):
    t = jnp.arange(tl)
    inv = ROPE_THETA ** (-jnp.arange(0, HD // 2, 2, dtype=f32) / (HD // 2))
    ang = jnp.concatenate([(t // GRID_W).astype(f32)[:, None] * inv, (t % GRID_W).astype(f32)[:, None] * inv], -1)
    cos, sin = jnp.repeat(jnp.cos(ang), 2, axis=1), jnp.repeat(jnp.sin(ang), 2, axis=1)
    even = (jnp.arange(HD) % 2 == 0)[None, :]
    cos_f = jnp.concatenate([jnp.ones((tc, HD), f32), cos], 0)
    sin_a = jnp.concatenate([jnp.zeros((tc, HD), f32), jnp.where(even, -sin, 0.0)], 0)
    sin_b = jnp.concatenate([jnp.zeros((tc, HD), f32), jnp.where(even, 0.0, sin)], 0)
    return cos_f, sin_a, sin_b


N_CHIPS = 4
SHARD = N_IN // N_CHIPS


def _group_ranges():
    return dict(M=[(S_MG, 4 * D), (S_GA, BRW), (S_GB, BRW), (S_GC, BRW), (S_GD, BRW)], A=[(S_Q, W_A)],
                C=[(S_B, 3 * BRW), (S_DA, 2 * BRW)], G=[(S_CQ, 2 * C_KW + BRW), (S_R, 2 * C_RANK)])


ROW_PHASE = SHARD % 16


def _group_weights(w4, phased=False):
    out = {}
    for k, ranges in _group_ranges().items():
        parts = []
        for a, n in ranges:
            while n > 0:
                s, r = divmod(a, SHARD)
                m = min(n, SHARD - r)
                r += ROW_PHASE * (s % 2) if phased else 0
                parts.append(w4[s, r:r + m])
                a, n = a + m, n - m
        if k == "G":
            parts.append(jnp.zeros((LANE - 2 * C_RANK, D), w4.dtype))
        out[k] = jnp.concatenate(parts, 0)
    return out


def _ungroup(g):
    secs = []
    for k, ranges in _group_ranges().items():
        off = 0
        for a, n in ranges:
            secs.append((a, g[k][off:off + n]))
            off += n
    return jnp.concatenate([v for _, v in sorted(secs, key=lambda t: t[0])], 0)


PROJ_TN = dict(M=2048, A=1024, C=1280, G=1152)
DU_TK = dict(M=2048, A=1024, C=BRW, G=1152)
DWP_TN = dict(M=768, A=1024, C=BRW, G=1152)


def _gate_weights(w2_l, gb_l):
    w = jnp.zeros((LANE, 2 * C_KW), f32)
    w = w.at[0:C_RANK, 0:C_KW].set(w2_l[0]).at[C_RANK:2 * C_RANK, C_KW:2 * C_KW].set(w2_l[1])
    return w, jnp.concatenate([gb_l[0], gb_l[1]])[None, :]


def _local_step(x1, c1, ctx1, tgt1, c_ctx, w_mod, b_mod, weights_of, q_norm, k_norm, b_conv, w2, gb, c_norm, d_conv_w,
                d_conv_b, d_norm_g, d_norm_b, grads_done, ln_g, ln_b, tm, token=None):
    tc, tl = ctx1.shape[0], x1.shape[0]
    T = tc + tl
    rc = min(256, tc)
    tmb = tm // 2
    tmm = 768 if T % 768 == 0 else tm
    rope = _rope_tables(tc, tl)
    cin = jnp.concatenate([c1, c_ctx[None, :], jnp.zeros((SUB - 2, D), f32)], 0)
    if token is not None:
        cin = cin + token[:, 0:1]
    modv = _mod_fwd(cin, w_mod, b_mod)
    modv = [modv[l] for l in range(DEPTH)]
    row = lambda v: v[None, :]

    h = jnp.concatenate([ctx1, x1], 0)
    saved, wp, w_br, w_out = [], [None] * DEPTH, [None] * DEPTH, [None] * DEPTH
    for l in range(DEPTH):
        wp[l], merge_weights = weights_of(l, h)
        u = _ln_fwd(h, modv[l], tc, tm, f"ln_fwd{l}")
        P = {k: _matmul(u, wp[l][k], "nt", tmm, PROJ_TN[k], D, f"proj{l}{k}", out_dtype=bf16) for k in GROUPS}
        qn, kn, vb = _prep_fwd(P["A"], row(q_norm[l]), row(k_norm[l]), rope, tm, f"prep_fwd{l}")
        ya = _attn_fwd(qn, kn, vb, tc, tm, f"attn_fwd{l}")
        yb, hh = _conv_fwd(P["C"], b_conv[l], d_conv_w[l], row(d_conv_b[l]), tc, tl, rc, f"conv_fwd{l}")
        w2p, b2p = _gate_weights(w2[l], gb[l])
        gla = _gla_fwd(P["G"], w2p, b2p, tc, f"gla_fwd{l}")
        o2, ssave = gla[:2], gla[2:]
        w_br[l], w_out[l] = merge_weights(o2[0])
        hn = _merge_fwd(h, modv[l], ya, yb, o2, hh, P["M"], row(c_norm[l]), row(d_norm_g[l]), row(d_norm_b[l]),
                        row(ln_g[l]), row(ln_b[l]), w_br[l], w_out[l], tc, tm, f"merge_fwd{l}")
        saved.append((h, u, P, qn, kn, vb, ya, yb, hh, o2, ssave, w2p, b2p))
        h = hn

    dh, lparts = _loss_kernel(h, tgt1, tc, tm, "loss")
    loss = jnp.sum(lparts[:, 0, 0])

    g = {k: [None] * DEPTH for k in ("wp", "q_norm", "k_norm", "b_conv", "w2", "gb", "c_norm", "d_conv_w", "d_conv_b",
                                     "d_norm_g", "d_norm_b", "w_br", "w_out", "ln_g", "ln_b", "modv")}
    for l in reversed(range(DEPTH)):
        h_in, u, P, qn, kn, vb, ya, yb, hh, o2, ssave, w2p, b2p = saved[l]
        dP = {}
        (dh_res, dm_mg, dya, dyb, doc, dhh, dP["M"], br, z, acc, dy, dv5, dvd) = _merge_bwd(
            dh, h_in, modv[l], ya, yb, o2, hh, P["M"], row(c_norm[l]), row(d_norm_g[l]), row(d_norm_b[l]),
            row(ln_g[l]), row(ln_b[l]), w_br[l], w_out[l], tc, tmb, f"merge_bwd{l}")
        g["w_br"][l] = _matmul_tn_batched(br, z, N_CHIPS, f"dwbr{l}")
        g["w_out"][l] = _matmul(acc, dy, "tn", D, D, T, f"dwout{l}", out_dtype=bf16)
        tk = grads_done(l, {k: g[k][l] for k in ("w_br", "w_out")})
        qg_l = row(q_norm[l]) if tk is None else row(q_norm[l]) + tk[0:1, :]
        v5 = jnp.sum(dv5, 0)
        g["c_norm"][l], g["d_norm_g"][l], g["d_norm_b"][l] = v5[0], v5[1], v5[2]
        vd = jnp.sum(dvd, 0)
        g["ln_g"][l], g["ln_b"][l] = vd[0], vd[1]
        dqn, dkn, dv = _attn_bwd(qn, kn, vb, dya, tc, tm, f"attn_bwd{l}")
        dP["A"], dqk = _prep_bwd(P["A"], dqn, dkn, dv, qg_l, row(k_norm[l]), rope, tm, f"prep_bwd{l}")
        dqk = jnp.sum(dqk, 0)
        g["q_norm"][l], g["k_norm"][l] = dqk[0], dqk[1]
        dP["C"], dwb, dwd, dbd = _conv_bwd(P["C"], dyb, dhh, b_conv[l], d_conv_w[l], tc, tl, rc, f"conv_bwd{l}")
        g["b_conv"][l], g["d_conv_w"][l], g["d_conv_b"][l] = dwb, dwd, dbd[0]
        dpf, dpb, dw2p, db2p = _gla_bwd(P["G"], w2p, b2p, ssave, doc, tc, f"gla_bwd{l}")
        dP["G"] = _sum_dirs(dpf, dpb, tm, f"gla_sum{l}")
        db2p = db2p[0]
        g["w2"][l] = jnp.stack([dw2p[0:C_RANK, 0:C_KW], dw2p[C_RANK:2 * C_RANK, C_KW:2 * C_KW]])
        g["gb"][l] = jnp.stack([db2p[0:C_KW], db2p[C_KW:2 * C_KW]])
        g["wp"][l] = {k: _matmul(dP[k], u, "tn", DWP_TN[k], D, T, f"dwp{l}{k}", out_dtype=bf16) for k in GROUPS}
        tk = grads_done(l, {"wp": g["wp"][l]})
        du = _matmul_groups(dP, wp[l], DU_TK, tmm, f"du{l}", after=tk)
        dh, dm_ln = _ln_bwd(du, h_in, dh_res, modv[l], tc, tm, f"ln_bwd{l}", latent_only=(l == 0))
        g["modv"][l] = jnp.sum(dm_mg, 0) + jnp.sum(dm_ln, 0)

    dmodv = jnp.stack(g.pop("modv"))
    g["w_mod"], dcin = _mod_bwd(cin, w_mod, dmodv)
    g["b_mod"] = dmodv[:, 0, :] + dmodv[:, 1, :]
    g["c_ctx"] = jnp.sum(dcin, (0, 1))[1]
    return loss, dh, g


HALF_TL = 256


TILE_BYTES = 1 << 20


def _row_tile(rows, cols, itemsize=4):
    tr = min(rows, 128)
    while rows % (2 * tr) == 0 and 2 * tr * cols * itemsize <= TILE_BYTES:
        tr *= 2
    return tr


def _adamw(w, g, m, v, name, tr=None, after=None):
    L, R, C = w.shape
    tr = _row_tile(R, C) if tr is None else tr
    if R % tr == 0:
        grid, spec = (L, R // tr), pl.BlockSpec((None, tr, C), lambda l, i: (l, i, 0))
    elif R * C * 4 <= (1 << 20):
        grid, spec = (L, 1), pl.BlockSpec((None, R, C), lambda l, i: (l, 0, 0))
    else:
        grid, spec = (L, C // HALF_TL), pl.BlockSpec((None, R, HALF_TL), lambda l, i: (l, 0, i))

    def body(w_ref, g_ref, m_ref, v_ref, *rest):
        go_ref, d_ref, nm_ref, nv_ref = rest[-4:]
        gg = g_ref[...]
        go_ref[...] = gg
        nm = B1 * m_ref[...] + (1.0 - B1) * gg
        nv = B2 * v_ref[...] + (1.0 - B2) * (gg * gg)
        m_hat = nm / (1.0 - B1 ** STEP)
        v_hat = nv / (1.0 - B2 ** STEP)
        d_ref[...] = -LR * (m_hat / (jnp.sqrt(v_hat) + AEPS) + WD * w_ref[...])
        nm_ref[...] = nm
        nv_ref[...] = nv

    return pl.pallas_call(
        body, name=name, grid=grid, in_specs=[spec] * 4 + ([] if after is None else [pl.BlockSpec(memory_space=pl.ANY)]),
        out_specs=[spec] * 4, out_shape=[jax.ShapeDtypeStruct((L, R, C), f32)] * 4,
        compiler_params=_cparams(("parallel", "parallel")),
    )(w, g, m, v, *([] if after is None else [after]))


MESH = pl.DeviceIdType.MESH
ANY = pl.BlockSpec(memory_space=pl.ANY)
N_CHIPS = 4


def _place():
    x, y, c = lax.axis_index("x"), lax.axis_index("y"), lax.axis_index("c")
    chips = [(1 - x, y), (x, 1 - y), (1 - x, 1 - y)]
    return x, y, c, chips


def _half(ref, c, axis):
    n = ref.shape[axis] // 2
    last = axis in (-1, ref.ndim - 1)
    idx = [slice(None)] * ref.ndim
    idx[axis] = pl.ds(pl.multiple_of(c * n, LANE if last else SUB), n)
    return ref.at[tuple(idx)]


def _half_shape(shape, axis):
    s = list(shape)
    s[axis] //= 2
    return tuple(s)


def _all_gather(arrs, axes, name):
    n = len(arrs)

    def body(*refs):
        ins, outs = refs[:n], refs[n:2 * n]
        send, recv = refs[2 * n:]
        x, y, c, chips = _place()
        me, sib = 2 * x + y, (x, y, 1 - c)

        def copy(a, k, chip_idx, cc, to, src=None):
            blk = _half(outs[a].at[chip_idx], cc, axes[a])
            return pltpu.make_async_remote_copy(src_ref=blk if src is None else src, dst_ref=blk,
                                                send_sem=send.at[7 * a + k], recv_sem=recv.at[7 * a + k],
                                                device_id=to, device_id_type=MESH)

        own = [pltpu.make_async_remote_copy(src_ref=ins[a], dst_ref=outs[a].at[me], send_sem=send.at[7 * a + 6],
                                            recv_sem=recv.at[7 * a + 6], device_id=sib, device_id_type=MESH)
               for a in range(n)]
        first = own + [copy(a, j, me, c, (*chip, c), src=_half(ins[a], c, axes[a]))
                       for a in range(n) for j, chip in enumerate(chips)]
        for cp in first:
            cp.start()
        passed = []
        for a in range(n):
            for j, chip in enumerate(chips):
                k = 2 * chip[0] + chip[1]
                copy(a, j, k, c, sib).wait_recv()
                fwd = copy(a, 3 + j, k, c, sib)
                fwd.start()
                passed.append(fwd)
        for a in range(n):
            own[a].wait_recv()
            for j, chip in enumerate(chips):
                copy(a, 3 + j, 2 * chip[0] + chip[1], 1 - c, sib).wait_recv()
        for cp in first + passed:
            cp.wait_send()

    return pl.pallas_call(
        body, name=name, in_specs=[ANY] * n, out_specs=[ANY] * n,
        out_shape=[jax.ShapeDtypeStruct((N_CHIPS,) + a.shape, a.dtype) for a in arrs],
        scratch_shapes=[pltpu.SemaphoreType.DMA((7 * n,)), pltpu.SemaphoreType.DMA((7 * n,))],
    )(*arrs)


def _sibling_halves(arrs, axes, name):
    n = len(arrs)

    def body(*refs):
        ins, outs = refs[:n], refs[n:2 * n]
        send, recv = refs[2 * n:]
        x, y, c, _ = _place()
        cps = [pltpu.make_async_remote_copy(src_ref=_half(ins[a], 1 - c, axes[a] + 1), dst_ref=outs[a], send_sem=send.at[a],
                                            recv_sem=recv.at[a], device_id=(x, y, 1 - c), device_id_type=MESH)
               for a in range(n)]
        for cp in cps:
            cp.start()
        for cp in cps:
            cp.wait()

    return pl.pallas_call(
        body, name=name, in_specs=[ANY] * n, out_specs=[ANY] * n,
        out_shape=[jax.ShapeDtypeStruct(_half_shape(a.shape, axes[i] + 1), a.dtype) for i, a in enumerate(arrs)],
        scratch_shapes=[pltpu.SemaphoreType.DMA((n,)), pltpu.SemaphoreType.DMA((n,))],
    )(*arrs)


def _add_half(gfull, land, cidx, axis, name, tr=None, out_dtype=bf16):
    _, hr, hc = land.shape
    if axis == 0:
        tr = min(tr, hr) if tr else _row_tile(hr, hc)
        nb, blk = hr // tr, (None, tr, hc)
        g_spec = pl.BlockSpec(blk, lambda s, i, cr: (s, cr[0] * nb + i, 0))
        l_spec = pl.BlockSpec(blk, lambda s, i, cr: (s, i, 0))
    else:
        nb, blk = hc // HALF_TL, (None, hr, HALF_TL)
        g_spec = pl.BlockSpec(blk, lambda s, i, cr: (s, 0, cr[0] * nb + i))
        l_spec = pl.BlockSpec(blk, lambda s, i, cr: (s, 0, i))

    def body(c_ref, g_ref, l_ref, o_ref):
        o_ref[...] = (g_ref[...].astype(f32) + l_ref[...].astype(f32)).astype(o_ref.dtype)

    return pl.pallas_call(
        body, name=name,
        grid_spec=pltpu.PrefetchScalarGridSpec(
            num_scalar_prefetch=1, grid=(N_CHIPS, nb), in_specs=[g_spec, l_spec], out_specs=l_spec),
        out_shape=jax.ShapeDtypeStruct((N_CHIPS, hr, hc), out_dtype),
        compiler_params=_cparams(("parallel", "parallel")),
    )(cidx, gfull, land)


def _chip_exchange(arrs, name):
    n = len(arrs)

    def body(*refs):
        ins, outs = refs[:n], refs[n:2 * n]
        send, recv = refs[2 * n:]
        x, y, c, chips = _place()
        me = 2 * x + y
        cps = []
        for a in range(n):
            for j, chip in enumerate(chips):
                k = 2 * chip[0] + chip[1]
                cps.append((pltpu.make_async_remote_copy(
                    src_ref=ins[a].at[k], dst_ref=outs[a].at[me], send_sem=send.at[3 * a + j], recv_sem=recv.at[3 * a + j],
                    device_id=(*chip, c), device_id_type=MESH), a, j, k))
        for cp, *_ in cps:
            cp.start()
        for cp, a, j, k in cps:
            pltpu.make_async_remote_copy(src_ref=ins[a].at[k], dst_ref=outs[a].at[k], send_sem=send.at[3 * a + j],
                                         recv_sem=recv.at[3 * a + j], device_id=(x, y, c), device_id_type=MESH).wait_recv()
        for cp, *_ in cps:
            cp.wait_send()

    return pl.pallas_call(
        body, name=name, in_specs=[ANY] * n, out_specs=[ANY] * n,
        out_shape=[jax.ShapeDtypeStruct(a.shape, a.dtype) for a in arrs],
        scratch_shapes=[pltpu.SemaphoreType.DMA((3 * n,)), pltpu.SemaphoreType.DMA((3 * n,))],
    )(*arrs)


def _sum_chips(land, own, place, axis, layer, into, name, tr=None):
    _, hr, hc = land.shape
    fresh = not hasattr(into, "dtype")
    shape = tuple(into) if fresh else into.shape
    if axis == 0:
        tr = min(tr, hr) if tr else _row_tile(hr, 4 * hc, 2)
        nb, blk = hr // tr, (tr, hc)
        l_map, m_map = (lambda i, p: (0, i, 0)), (lambda i, p: (p[0], i, 0))
        o_map = lambda i, p: (layer, p[1] * nb + i, 0)
    else:
        nb, blk = hc // HALF_TL, (hr, HALF_TL)
        l_map, m_map = (lambda i, p: (0, 0, i)), (lambda i, p: (p[0], 0, i))
        o_map = lambda i, p: (layer, 0, p[1] * nb + i)

    def body(p_ref, l_ref, o_ref, *rest):
        me = p_ref[0]
        mine = o_ref[...].astype(f32)
        acc = None
        for k in range(N_CHIPS):
            t = jnp.where(me == k, mine, l_ref[k].astype(f32))
            acc = t if acc is None else acc + t
        rest[-1][...] = acc

    return pl.pallas_call(
        body, name=name,
        grid_spec=pltpu.PrefetchScalarGridSpec(
            num_scalar_prefetch=1, grid=(nb,),
            in_specs=[pl.BlockSpec((N_CHIPS,) + blk, l_map), pl.BlockSpec((None,) + blk, m_map)] + ([] if fresh else [ANY]),
            out_specs=pl.BlockSpec((None,) + blk, o_map)),
        out_shape=jax.ShapeDtypeStruct(shape, f32),
        input_output_aliases={} if fresh else {3: 0},
        compiler_params=_cparams(("parallel",)),
    )(place, land, own, *([] if fresh else [into]))


def _sibling_fill(arrs, axes, name):
    n = len(arrs)

    def body(*refs):
        outs = refs[n:2 * n]
        send, recv = refs[2 * n:]
        x, y, c, _ = _place()
        cps = [pltpu.make_async_remote_copy(src_ref=_half(outs[a], c, axes[a] + 1), dst_ref=_half(outs[a], c, axes[a] + 1),
                                            send_sem=send.at[a], recv_sem=recv.at[a], device_id=(x, y, 1 - c),
                                            device_id_type=MESH) for a in range(n)]
        for cp in cps:
            cp.start()
        for a in range(n):
            blk = _half(outs[a], 1 - c, axes[a] + 1)
            pltpu.make_async_remote_copy(src_ref=blk, dst_ref=blk, send_sem=send.at[a], recv_sem=recv.at[a],
                                         device_id=(x, y, 1 - c), device_id_type=MESH).wait_recv()
        for cp in cps:
            cp.wait_send()

    return pl.pallas_call(
        body, name=name, in_specs=[ANY] * n, out_specs=[ANY] * n,
        out_shape=[jax.ShapeDtypeStruct(a.shape, a.dtype) for a in arrs],
        input_output_aliases={a: a for a in range(n)},
        scratch_shapes=[pltpu.SemaphoreType.DMA((n,)), pltpu.SemaphoreType.DMA((n,))],
    )(*arrs)


HBM = pl.BlockSpec(memory_space=pltpu.HBM)
SEM = pl.BlockSpec(memory_space=pltpu.SEMAPHORE)
EFFECT = pltpu.SideEffectType.DATAFLOW_SIDE_EFFECTING
PEERS = 4


def _split_copies(srcs, lands, send, recv, gather, axes=None):
    x, y, c, chips = _place()
    me = 2 * x + y
    if axes is not None:
        out = []
        for a in range(len(srcs)):
            sems = dict(send_sem=send.at[PEERS * a], recv_sem=recv.at[PEERS * a], device_id=(x, y, 1 - c), device_id_type=MESH)
            copy = pltpu.make_async_remote_copy(src_ref=_half(srcs[a], 1 - c, axes[a] + 1), dst_ref=lands[a], **sems)
            out.append((copy, copy))
        return out
    peers = [((*chip, c), 2 * chip[0] + chip[1]) for chip in chips] + ([((x, y, 1 - c), me)] if gather else [])
    out = []
    for a in range(len(srcs)):
        for j, (dev, k) in enumerate(peers):
            src = srcs[a] if gather else srcs[a].at[k]
            sems = dict(send_sem=send.at[PEERS * a + j], recv_sem=recv.at[PEERS * a + j], device_id=dev, device_id_type=MESH)
            out.append((pltpu.make_async_remote_copy(src_ref=src, dst_ref=lands[a].at[me], **sems),
                        pltpu.make_async_remote_copy(src_ref=src, dst_ref=lands[a].at[k], **sems)))
    return out


def _split_start(srcs, gather, after, name, axes=None):
    n = len(srcs)
    if axes is not None:
        lands = [lax.empty(_half_shape(s.shape, axes[a] + 1), s.dtype) for a, s in enumerate(srcs)]
    else:
        lands = [lax.empty(((N_CHIPS,) + s.shape) if gather else s.shape, s.dtype) for s in srcs]

    def body(*refs):
        send, recv = refs[2 * n + 1], refs[2 * n + 2]
        for start, _ in _split_copies(refs[:n], refs[n:2 * n], send, recv, gather, axes):
            start.start()
        refs[-1][...] = jnp.zeros_like(refs[-1])

    sems = pltpu.SemaphoreType.DMA((PEERS * n,))
    hbm = lambda a: pltpu.with_memory_space_constraint(a, pltpu.HBM)
    out = pl.pallas_call(
        body, name=name,
        out_shape=(sems, sems, *[pltpu.HBM(a.shape, a.dtype) for a in srcs + lands], jax.ShapeDtypeStruct((SUB, LANE), f32)),
        in_specs=[HBM] * (2 * n) + [ANY], out_specs=(SEM, SEM, *[HBM] * (2 * n), pl.BlockSpec(memory_space=pltpu.VMEM)),
        input_output_aliases={i: 2 + i for i in range(2 * n)},
        compiler_params=pltpu.CompilerParams(has_side_effects=EFFECT),
    )(*[hbm(a) for a in srcs + lands], after)
    return out[0], out[1], list(out[2:2 + n]), list(out[2 + n:2 + 2 * n]), out[-1]


def _split_wait(send, recv, srcs, lands, gather, after, name, axes=None):
    n = len(srcs)

    def body(*refs):
        for start, arrival in _split_copies(refs[:n], refs[n:2 * n], refs[2 * n], refs[2 * n + 1], gather, axes):
            start.wait_send()
            arrival.wait_recv()

    out = pl.pallas_call(
        body, name=name, out_shape=[pltpu.HBM(a.shape, a.dtype) for a in srcs + lands],
        in_specs=[HBM] * (2 * n) + [SEM, SEM, ANY], out_specs=[HBM] * (2 * n),
        input_output_aliases={i: i for i in range(2 * n)},
        compiler_params=pltpu.CompilerParams(has_side_effects=EFFECT),
    )(*srcs, *lands, send, recv, after)
    return list(out[:n]), list(out[n:])


N_DEV = 8


def _all_reduce_small(v, name):
    R = v.shape[0]

    def body(v_ref, o_ref, land_ref, send, recv):
        x, y, c, _ = _place()
        me = 4 * x + 2 * y + c
        land_ref[me] = v_ref[...]
        cps = []
        for m in range(1, N_DEV):
            px, py, pc = [(1 - q) if (m >> s) & 1 else q for q, s in ((x, 2), (y, 1), (c, 0))]
            cps.append((pltpu.make_async_remote_copy(src_ref=v_ref, dst_ref=land_ref.at[me], send_sem=send.at[m - 1],
                                                     recv_sem=recv.at[m - 1], device_id=(px, py, pc), device_id_type=MESH),
                        4 * px + 2 * py + pc, m))
        for cp, *_ in cps:
            cp.start()
        for cp, peer, m in cps:
            pltpu.make_async_remote_copy(src_ref=v_ref, dst_ref=land_ref.at[peer], send_sem=send.at[m - 1],
                                         recv_sem=recv.at[m - 1], device_id=(x, y, c), device_id_type=MESH).wait_recv()
        for cp, *_ in cps:
            cp.wait_send()
        acc = land_ref[0]
        for k in range(1, N_DEV):
            acc = acc + land_ref[k]
        o_ref[...] = acc

    vm = pl.BlockSpec(memory_space=pltpu.VMEM)
    return pl.pallas_call(
        body, name=name, in_specs=[vm], out_specs=vm, out_shape=jax.ShapeDtypeStruct(v.shape, f32),
        scratch_shapes=[pltpu.VMEM((N_DEV, R, LANE), f32), pltpu.SemaphoreType.DMA((N_DEV - 1,)),
                        pltpu.SemaphoreType.DMA((N_DEV - 1,))],
        compiler_params=pltpu.CompilerParams(vmem_limit_bytes=VMEM_LIMIT),
    )(v)


def _pack_small(arrs, mult=2 * SUB):
    flat = jnp.concatenate([a.reshape(-1) for a in arrs])
    rows = -(-flat.shape[0] // (LANE * mult)) * mult
    return jnp.pad(flat, (0, rows * LANE - flat.shape[0])).reshape(rows, LANE)


def _unpack_small(vec, shapes):
    flat, out, o = vec.reshape(-1), [], 0
    for s in shapes:
        n = int(np.prod(s))
        out.append(flat[o:o + n].reshape(s))
        o += n
    return out


REPL_SMALL = ("c_ctx", "b_mod", "q_norm", "k_norm", "c_norm", "d_conv_b", "d_norm_g", "d_norm_b", "ln_g", "ln_b")
SHARD_SMALL = ("b_conv", "c_gate_w2", "c_gate_b", "d_conv_w")
BIG = ("w_mod", "w_in", "w_br", "w_out")
ORDER = ("c_ctx", "w_mod", "b_mod", "w_in", "q_norm", "k_norm", "b_conv", "c_gate_w2", "c_gate_b", "c_norm", "d_conv_w",
         "d_conv_b", "d_norm_g", "d_norm_b", "w_br", "w_out", "ln_g", "ln_b")


def _unshard_last(g4, shard_shape):
    g = g4.reshape((N_CHIPS,) + tuple(shard_shape))
    g = jnp.moveaxis(g, 0, -2)
    return g.reshape(tuple(shard_shape[:-1]) + (N_CHIPS * shard_shape[-1],))


def _pieces_last(full):
    w = full.shape[-1] // N_CHIPS
    g = full.reshape(full.shape[:-1] + (N_CHIPS, w))
    return jnp.moveaxis(g, -2, 0).reshape(N_CHIPS, -1, w)


def kernel(x, c, ctx, c_ctx, w_mod, b_mod, w_in, q_norm, k_norm, b_conv, c_gate_w2, c_gate_b, c_norm, d_conv_w, d_conv_b, d_norm_g, d_norm_b, w_br, w_out, ln_g, ln_b, loss_target, m_c_ctx, m_w_mod, m_b_mod, m_w_in, m_q_norm, m_k_norm, m_b_conv, m_c_gate_w2, m_c_gate_b, m_c_norm, m_d_conv_w, m_d_conv_b, m_d_norm_g, m_d_norm_b, m_w_br, m_w_out, m_ln_g, m_ln_b, v_c_ctx, v_w_mod, v_b_mod, v_w_in, v_q_norm, v_k_norm, v_b_conv, v_c_gate_w2, v_c_gate_b, v_c_norm, v_d_conv_w, v_d_conv_b, v_d_norm_g, v_d_norm_b, v_w_br, v_w_out, v_ln_g, v_ln_b):
    W = dict(c_ctx=c_ctx, w_mod=w_mod, b_mod=b_mod, w_in=w_in, q_norm=q_norm, k_norm=k_norm, b_conv=b_conv,
             c_gate_w2=c_gate_w2, c_gate_b=c_gate_b, c_norm=c_norm, d_conv_w=d_conv_w, d_conv_b=d_conv_b,
             d_norm_g=d_norm_g, d_norm_b=d_norm_b, w_br=w_br, w_out=w_out, ln_g=ln_g, ln_b=ln_b)
    M = dict(c_ctx=m_c_ctx, w_mod=m_w_mod, b_mod=m_b_mod, w_in=m_w_in, q_norm=m_q_norm, k_norm=m_k_norm, b_conv=m_b_conv,
             c_gate_w2=m_c_gate_w2, c_gate_b=m_c_gate_b, c_norm=m_c_norm, d_conv_w=m_d_conv_w, d_conv_b=m_d_conv_b,
             d_norm_g=m_d_norm_g, d_norm_b=m_d_norm_b, w_br=m_w_br, w_out=m_w_out, ln_g=m_ln_g, ln_b=m_ln_b)
    V = dict(c_ctx=v_c_ctx, w_mod=v_w_mod, b_mod=v_b_mod, w_in=v_w_in, q_norm=v_q_norm, k_norm=v_k_norm, b_conv=v_b_conv,
             c_gate_w2=v_c_gate_w2, c_gate_b=v_c_gate_b, c_norm=v_c_norm, d_conv_w=v_d_conv_w, d_conv_b=v_d_conv_b,
             d_norm_g=v_d_norm_g, d_norm_b=v_d_norm_b, w_br=v_w_br, w_out=v_w_out, ln_g=v_ln_g, ln_b=v_ln_b)
    chip = 2 * lax.axis_index("x") + lax.axis_index("y")
    cidx = lax.axis_index("c").astype(jnp.int32).reshape(1)

    place = jnp.stack([chip, lax.axis_index("c")]).astype(jnp.int32)

    AXIS = dict(w_in=1, w_mod=0, w_br=0, w_out=0)
    ex = dict(w_in=lambda a: jnp.swapaxes(a, 1, 2), w_mod=lambda a: a.reshape(1, DEPTH * D, -1),
              w_br=lambda a: a.reshape(DEPTH, 4 * BRW, -1), w_out=lambda a: a)
    Wx, Mx, Vx = ({k: ex[k](P_[k]) for k in BIG} for P_ in (W, M, V))

    LAYER, MERGE = ("w_in", "w_br", "w_out"), ("w_br", "w_out")
    small_shard = _pack_small([W[k] for k in SHARD_SMALL])
    keys0 = ("w_in", "w_mod")

    def sent(k, l):
        a = Wx[k][l].astype(bf16)
        if k != "w_in":
            return a
        return lax.dynamic_update_slice(jnp.zeros((SHARD + ROW_PHASE, D), bf16), a, (ROW_PHASE * (chip % 2), 0))

    got = _all_gather([sent(k, 0) for k in keys0] + [small_shard], [AXIS[k] for k in keys0] + [0], "all_gather0")
    smalls = [_unpack_small(got[-1][s], [W[k].shape for k in SHARD_SMALL]) for s in range(N_CHIPS)]
    full = {k: jnp.concatenate([smalls[s][i] for s in range(N_CHIPS)], axis=-1) for i, k in enumerate(SHARD_SMALL)}
    wmod = got[1].reshape(N_CHIPS, DEPTH, D, 3 * D // N_CHIPS)
    ag0b = _split_start([sent(k, 0) for k in MERGE], True, got[0], "all_gather0b_start")
    ag1 = _split_start([sent(k, 1) for k in LAYER], True, ag0b[4], "all_gather1_start")

    def merge_form(w_br4, w_out4):
        return jnp.moveaxis(w_br4.reshape(N_CHIPS, 4, BRW, D // N_CHIPS), 0, 2).reshape(4, BRW, D), w_out4.reshape(D, D)

    def weights_of(l, h):
        if l == 0:
            return (_group_weights(got[0], True),
                    lambda after: merge_form(*_split_wait(*ag0b[:4], True, after, "all_gather0b_wait")[1]))
        g3 = _split_wait(*ag1[:4], True, h, "all_gather1_wait")[1]
        return _group_weights(g3[0], True), lambda after: merge_form(g3[1], g3[2])

    red = {k: Wx[k].shape for k in BIG}
    flights, held = {}, {}

    def launch(tag, l, pieces, after=None):
        keys = list(pieces)
        land_a = _sibling_halves([pieces[k] for k in keys], [AXIS[k] for k in keys], f"rs_sibling_halves{tag}")
        pair = [_add_half(pieces[k], la, cidx, AXIS[k], f"rs_pair_sum{tag}_{k}") for k, la in zip(keys, land_a)]
        after = jnp.zeros((SUB, LANE), f32) if after is None else after
        flights[tag] = (l, keys, _split_start(pair, False, after, f"rs_chip_exchange{tag}_start"))
        return flights[tag][2][4]

    def land(tag, after):
        l, keys, flight = flights.pop(tag)
        pair, land_b = _split_wait(*flight[:4], False, after, f"rs_chip_exchange{tag}_wait")
        for k, lb, pr in zip(keys, land_b, pair):
            red[k] = _sum_chips(lb, pr, place, AXIS[k], l, red[k], f"rs_chip_sum{tag}_{k}")

    def grads_done(l, gl):
        if "wp" in gl:
            pieces = dict(w_in=_ungroup(gl["wp"]).reshape(N_CHIPS, SHARD, D))
            return launch("0c", 0, pieces) if l == 0 else launch("1", 1, {**pieces, **held.pop(1)})
        pieces = dict(w_br=gl["w_br"].reshape(N_CHIPS, 4 * BRW, D // N_CHIPS), w_out=gl["w_out"].reshape(N_CHIPS, D // N_CHIPS, D))
        if l == 0:
            return launch("0b", 0, pieces)
        held[1] = pieces
        return None

    loss, gx, g = _local_step(
        x[0], c, ctx[0], loss_target[0], c_ctx, wmod, b_mod, weights_of, q_norm, k_norm, full["b_conv"],
        full["c_gate_w2"], full["c_gate_b"], c_norm, full["d_conv_w"], d_conv_b, d_norm_g, d_norm_b,
        grads_done, ln_g, ln_b, tm=256, token=ag1[4])
    g["c_gate_w2"], g["c_gate_b"] = g.pop("w2"), g.pop("gb")
    loss = lax.psum(loss, ("x", "y", "c"))

    w_mod_pieces = g["w_mod"].reshape(N_CHIPS, DEPTH * D, 3 * D // N_CHIPS)
    g = {k: (jnp.stack(v) if isinstance(v, list) else v) for k, v in g.items() if k not in ("wp", "w_br", "w_out", "w_mod")}

    small_names = REPL_SMALL + SHARD_SMALL
    gs = _all_reduce_small(_pack_small([g[k] for k in small_names]), "all_reduce_small")
    gsm = dict(zip(small_names, _unpack_small(gs, [g[k].shape for k in small_names])))
    for k in SHARD_SMALL:
        wdt = W[k].shape[-1]
        gsm[k] = lax.dynamic_slice_in_dim(gsm[k], chip * wdt, wdt, axis=gsm[k].ndim - 1)

    grad, delta, new_m, new_v = {}, {}, {}, {}

    def adamw_big(keys, after):
        filled = _sibling_fill([red[k] for k in keys], [AXIS[k] for k in keys], "rs_sibling_fill_" + keys[0])
        for k, r in zip(keys, filled):
            back = (lambda a: jnp.swapaxes(a, 1, 2)) if k == "w_in" else (lambda a: a.reshape(W[k].shape))
            g_, d_, m_, v_ = _adamw(Wx[k], r, Mx[k], Vx[k], f"adamw_{k}", after=after)
            grad[k], delta[k], new_m[k], new_v[k] = back(g_), back(d_), back(m_), back(v_)
        return d_

    token = launch("0d", 0, {"w_mod": w_mod_pieces}, after=gs)
    land("1", gx)
    land("0b", gx)
    last = adamw_big(MERGE, token)
    shapes = [W[k].shape for k in small_names]
    _, d_, m_, v_ = _adamw(*[_pack_small([P_[k] for k in small_names])[None] for P_ in (W, gsm, M, V)], "adamw_small", after=last)
    for k, dd, mm_, vv in zip(small_names, _unpack_small(d_, shapes), _unpack_small(m_, shapes), _unpack_small(v_, shapes)):
        grad[k], delta[k], new_m[k], new_v[k] = gsm[k], dd, mm_, vv
    land("0c", d_)
    land("0d", d_)
    adamw_big(("w_in", "w_mod"), None)

    return (loss, gx[None], *[grad[k] for k in ORDER], *[delta[k] for k in ORDER], *[new_m[k] for k in ORDER],
            *[new_v[k] for k in ORDER])
```

```python
import functools

import jax
import jax.numpy as jnp
import numpy as np
from jax import lax
from jax.experimental import pallas as pl
from jax.experimental.pallas import tpu as pltpu

f32 = jnp.float32
bf16 = jnp.bfloat16

D = 1024
DEPTH = 2
GRID_W = 64
BRW = 512
HD = 128
A_HEADS = 4
C_HEADS = 4
C_KW = 256
C_RANK = 16
C_TAU = 16.0
CH = 128
KB = 3
KD = 31
ALPHA = (2 * DEPTH) ** 0.25
EPS = 1e-6
ROPE_THETA = 10000.0
N_IN = 10784
LR, B1, B2, AEPS, WD, STEP = 0.001, 0.9, 0.999, 1e-08, 0.01, 10

W_M, W_A, W_C, W_G = 4 * D + 4 * BRW, 1024, 5 * BRW, 1152
GROUPS = ("M", "A", "C", "G")
GROUP_W = dict(M=W_M, A=W_A, C=W_C, G=W_G)
M_GA, M_GB, M_GC, M_GD = 4 * D, 4 * D + BRW, 4 * D + 2 * BRW, 4 * D + 3 * BRW
A_K, A_V = 512, 768
G_K, G_V, G_R = 256, 512, 1024
CT = 5 * 128
S_Q, S_GA, S_B, S_C, S_X, S_GB, S_CQ, S_CV, S_GC, S_R, S_DA, S_DG, S_GD, S_MG = (
    0, 1024, 1536, 2048, 2560, 3072, 3584, 4096, 4608, 5120, 5152, 5664, 6176, 6688)

LANE = 128
SUB = 8
VMEM_LIMIT = 56 * 1024 * 1024
CONV_PAD = 16
GLA_SUB = 16
GLA_CLAMP = 60.0


def _cparams(sem, vmem=VMEM_LIMIT):
    return pltpu.CompilerParams(dimension_semantics=sem, vmem_limit_bytes=vmem)


def _dg(a, b, ca, cb):
    return lax.dot_general(a.astype(bf16), b.astype(bf16), (((ca,), (cb,)), ((), ())),
                           preferred_element_type=f32)


@jax.custom_vjp
def mm(a, b):
    return _dg(a, b, 1, 0)


mm.defvjp(lambda a, b: (_dg(a, b, 1, 0), (a, b)),
          lambda r, ct: (_dg(ct, r[1], 1, 1).astype(r[0].dtype), _dg(r[0], ct, 0, 0).astype(r[1].dtype)))


@jax.custom_vjp
def mm_nt(a, b):
    return _dg(a, b, 1, 1)


mm_nt.defvjp(lambda a, b: (_dg(a, b, 1, 1), (a, b)),
             lambda r, ct: (_dg(ct, r[1], 1, 0).astype(r[0].dtype), _dg(ct, r[0], 0, 0).astype(r[1].dtype)))


@jax.custom_vjp
def mm_tn(a, b):
    return _dg(a, b, 0, 0)


mm_tn.defvjp(lambda a, b: (_dg(a, b, 0, 0), (a, b)),
             lambda r, ct: (_dg(r[1], ct, 1, 1).astype(r[0].dtype), _dg(r[0], ct, 1, 0).astype(r[1].dtype)))


def _sigmoid(x):
    return 0.5 * jnp.tanh(0.5 * x) + 0.5


def _silu(x):
    return x * _sigmoid(x)


def _ln(x):
    mu = jnp.mean(x, -1, keepdims=True)
    xc = x - mu
    var = jnp.mean(xc * xc, -1, keepdims=True)
    return xc * lax.rsqrt(var + EPS)


def _rms(x, g):
    return x * lax.rsqrt(jnp.mean(x * x, -1, keepdims=True) + EPS) * g


@jax.custom_vjp
def _rope(x, cos_f, sin_a, sin_b):
    return x * cos_f + pltpu.roll(x, HD - 1, 1) * sin_a + pltpu.roll(x, 1, 1) * sin_b


def _rope_fwd(x, cos_f, sin_a, sin_b):
    return _rope(x, cos_f, sin_a, sin_b), (cos_f, sin_a, sin_b)


def _rope_bwd(r, ct):
    cos_f, sin_a, sin_b = r
    dx = ct * cos_f + pltpu.roll(ct * sin_a, 1, 1) + pltpu.roll(ct * sin_b, HD - 1, 1)
    return dx, jnp.zeros_like(cos_f), jnp.zeros_like(sin_a), jnp.zeros_like(sin_b)


_rope.defvjp(_rope_fwd, _rope_bwd)


def _row_ids(i, tm):
    return i * tm + lax.broadcasted_iota(jnp.int32, (tm, 1), 0)


def _partial_rows(ref, rows):
    n = len(rows)
    for k, r in enumerate(rows):
        ref[k:k + 1, :] = r
    ref[n:SUB, :] = jnp.zeros((SUB - n, ref.shape[-1]), f32)


def _matmul(a, b, mode, tm, tn, tk, name, out_dtype=f32, add=None, after=None):
    sect = a.ndim == 3
    a2 = (a.shape[1], a.shape[0] * a.shape[2]) if sect else a.shape
    if mode == "nn":
        (M, K), N = a2, b.shape[1]
        a_spec = pl.BlockSpec((None, tm, tk), lambda j, i, k: (k, i, 0)) if sect else pl.BlockSpec((tm, tk), lambda j, i, k: (i, k))
        b_spec = pl.BlockSpec((tk, tn), lambda j, i, k: (k, j))
        ca, cb = 1, 0
        assert not sect or tk == a.shape[2]
    elif mode == "nt":
        (M, K), N = a2, b.shape[0]
        assert not sect
        a_spec = pl.BlockSpec((tm, tk), lambda j, i, k: (i, k))
        b_spec = pl.BlockSpec((tn, tk), lambda j, i, k: (j, k))
        ca, cb = 1, 1
    else:
        (K, M), N = a2, b.shape[1]
        a_spec = pl.BlockSpec((None, tk, tm), lambda j, i, k: (i, k, 0)) if sect else pl.BlockSpec((tk, tm), lambda j, i, k: (k, i))
        b_spec = pl.BlockSpec((tk, tn), lambda j, i, k: (k, j))
        ca, cb = 0, 0
        assert not sect or tm == a.shape[2]
    assert M % tm == 0 and N % tn == 0 and K % tk == 0, (name, M, N, K, tm, tn, tk)
    nk = K // tk

    o_spec = pl.BlockSpec((tm, tn), lambda j, i, k: (i, j))

    def body(a_ref, b_ref, *rest):
        add_ref = rest[0] if add is not None else None
        o_ref, acc_ref = rest[-2:]
        k = pl.program_id(2)
        part = _dg(a_ref[...], b_ref[...], ca, cb)

        @pl.when(k == 0)
        def _():
            acc_ref[...] = part if add_ref is None else part + add_ref[...]

        @pl.when(k > 0)
        def _():
            acc_ref[...] += part

        @pl.when(k == nk - 1)
        def _():
            o_ref[...] = acc_ref[...].astype(o_ref.dtype)

    extra = ([] if add is None else [(o_spec, add)]) + ([] if after is None else [(pl.BlockSpec(memory_space=pl.ANY), after)])
    return pl.pallas_call(
        body, name=name, grid=(N // tn, M // tm, nk),
        in_specs=[a_spec, b_spec] + [s_ for s_, _ in extra], out_specs=o_spec,
        out_shape=jax.ShapeDtypeStruct((M, N), out_dtype),
        scratch_shapes=[pltpu.VMEM((tm, tn), f32)],
        compiler_params=_cparams(("parallel", "parallel", "arbitrary")),
    )(a, b, *[v_ for _, v_ in extra])


def _matmul_groups(a, b, tks, tm, name, after=None):
    keys = list(a)
    M = a[keys[0]].shape[-2]
    N = b[keys[0]].shape[1]
    count = {g: b[g].shape[0] // tks[g] for g in keys}
    first, total = {}, 0
    for g in keys:
        first[g], total = total, total + count[g]

    def k_of(g):
        return lambda s: jnp.clip(s - first[g], 0, count[g] - 1)

    a_specs = [pl.BlockSpec((None, tm, tks[g]), functools.partial(lambda i, s, kk: (kk(s), i, 0), kk=k_of(g)))
               if a[g].ndim == 3 else pl.BlockSpec((tm, tks[g]), functools.partial(lambda i, s, kk: (i, kk(s)), kk=k_of(g)))
               for g in keys]
    b_specs = [pl.BlockSpec((tks[g], N), functools.partial(lambda i, s, kk: (kk(s), 0), kk=k_of(g))) for g in keys]
    n = len(keys)

    def body(*refs):
        o_ref, acc_ref = refs[-2:]
        s = pl.program_id(1)

        @pl.when(s == 0)
        def _():
            acc_ref[...] = jnp.zeros_like(acc_ref)

        for j, g in enumerate(keys):
            @pl.when((s >= first[g]) & (s < first[g] + count[g]))
            def _(j=j):
                acc_ref[...] += _dg(refs[j][...], refs[n + j][...], 1, 0)

        @pl.when(s == total - 1)
        def _():
            o_ref[...] = acc_ref[...]

    extra = [] if after is None else [after]
    return pl.pallas_call(
        body, name=name, grid=(M // tm, total),
        in_specs=a_specs + b_specs + [pl.BlockSpec(memory_space=pl.ANY)] * len(extra),
        out_specs=pl.BlockSpec((tm, N), lambda i, s: (i, 0)),
        out_shape=jax.ShapeDtypeStruct((M, N), f32),
        scratch_shapes=[pltpu.VMEM((tm, N), f32)],
        compiler_params=_cparams(("parallel", "arbitrary")),
    )(*[a[g] for g in keys], *[b[g] for g in keys], *extra)


def _matmul_tn_batched(a, b, ns, name):
    B, K, M = a.shape
    N = b.shape[2] // ns

    def body(a_ref, b_ref, o_ref):
        o_ref[...] = _dg(a_ref[...], b_ref[...], 0, 0).astype(bf16)

    return pl.pallas_call(
        body, name=name, grid=(B, ns),
        in_specs=[pl.BlockSpec((None, K, M), lambda i, s: (i, 0, 0)), pl.BlockSpec((None, K, N), lambda i, s: (i, 0, s))],
        out_specs=pl.BlockSpec((None, None, M, N), lambda i, s: (s, i, 0, 0)),
        out_shape=jax.ShapeDtypeStruct((ns, B, M, N), bf16),
        compiler_params=_cparams(("parallel", "parallel")),
    )(a, b)


MOD_TN = 768


def _mod_fwd(cin, w_mod_l, b_mod_l, name):
    def body(c_ref, w_ref, b_ref, o_ref):
        o_ref[...] = mm(_silu(c_ref[...]), w_ref[...]) + b_ref[...]

    return pl.pallas_call(
        body, name=name, grid=(3 * D // MOD_TN,),
        in_specs=[pl.BlockSpec((SUB, D), lambda j: (0, 0)), pl.BlockSpec((None, D, MOD_TN), lambda j: (j, 0, 0)),
                  pl.BlockSpec((1, MOD_TN), lambda j: (0, j))],
        out_specs=pl.BlockSpec((SUB, MOD_TN), lambda j: (0, j)),
        out_shape=jax.ShapeDtypeStruct((SUB, 3 * D), f32),
        compiler_params=_cparams(("parallel",)),
    )(cin, w_mod_l, b_mod_l[None, :])


def _mod_bwd(cin, w_mod, dmodv):
    nj = 3 * D // MOD_TN

    def body(c_ref, w_ref, g_ref, dw_ref, dc_ref):
        _, vjp = jax.vjp(lambda c, w: mm(_silu(c), w), c_ref[...], w_ref[...].astype(f32))
        dc, dw = vjp(g_ref[...])
        dw_ref[...] = dw.astype(bf16)
        dc_ref[...] = dc

    return pl.pallas_call(
        body, name="mod_bwd", grid=(DEPTH, nj),
        in_specs=[pl.BlockSpec((SUB, D), lambda l, j: (0, 0)),
                  pl.BlockSpec((None, None, D, MOD_TN), lambda l, j: (j, l, 0, 0)),
                  pl.BlockSpec((None, SUB, MOD_TN), lambda l, j: (l, 0, j))],
        out_specs=[pl.BlockSpec((None, None, D, MOD_TN), lambda l, j: (j, l, 0, 0)),
                   pl.BlockSpec((None, None, SUB, D), lambda l, j: (l, j, 0, 0))],
        out_shape=[jax.ShapeDtypeStruct((nj, DEPTH, D, MOD_TN), bf16),
                   jax.ShapeDtypeStruct((DEPTH, nj, SUB, D), f32)],
        compiler_params=_cparams(("parallel", "parallel")),
    )(cin, w_mod, dmodv)


def _u_fn(h, m_l, m_c, isctx):
    n = _ln(h)
    shift = jnp.where(isctx, m_c[:, 0:D], m_l[:, 0:D])
    scale = jnp.where(isctx, m_c[:, D:2 * D], m_l[:, D:2 * D])
    return n * (1.0 + scale) + shift


def _ln_fwd(h, modv_l, tc, tm, name):
    T = h.shape[0]

    def body(h_ref, m_ref, u_ref):
        isctx = _row_ids(pl.program_id(0), tm) < tc
        u_ref[...] = _u_fn(h_ref[...], m_ref[0:1, :], m_ref[1:2, :], isctx).astype(bf16)

    return pl.pallas_call(
        body, name=name, grid=(T // tm,),
        in_specs=[pl.BlockSpec((tm, D), lambda i: (i, 0)), pl.BlockSpec((SUB, 3 * D), lambda i: (0, 0))],
        out_specs=pl.BlockSpec((tm, D), lambda i: (i, 0)),
        out_shape=jax.ShapeDtypeStruct((T, D), bf16),
        compiler_params=_cparams(("parallel",)),
    )(h, modv_l)


def _ln_bwd(du, h, dh_res, modv_l, tc, tm, name, latent_only=False):
    T = h.shape[0]
    nt, nct = T // tm, tc // tm

    def body(du_ref, h_ref, r_ref, m_ref, dh_ref, dm_ref):
        isctx = _row_ids(pl.program_id(0), tm) < tc
        _, vjp = jax.vjp(lambda h, ml, mc: _u_fn(h, ml, mc, isctx), h_ref[...], m_ref[0:1, :], m_ref[1:2, :])
        dh, dml, dmc = vjp(du_ref[...])
        dh_ref[...] = dh + r_ref[...]
        _partial_rows(dm_ref, [dml, dmc])

    dh_map = (lambda i: (jnp.maximum(i - nct, 0), 0)) if latent_only else (lambda i: (i, 0))
    return pl.pallas_call(
        body, name=name, grid=(nt,),
        in_specs=[pl.BlockSpec((tm, D), lambda i: (i, 0)), pl.BlockSpec((tm, D), lambda i: (i, 0)),
                  pl.BlockSpec((tm, D), lambda i: (i, 0)), pl.BlockSpec((SUB, 3 * D), lambda i: (0, 0))],
        out_specs=[pl.BlockSpec((tm, D), dh_map), pl.BlockSpec((None, SUB, 3 * D), lambda i: (i, 0, 0))],
        out_shape=[jax.ShapeDtypeStruct((T - tc if latent_only else T, D), f32), jax.ShapeDtypeStruct((nt, SUB, 3 * D), f32)],
        compiler_params=_cparams(("arbitrary",)),
    )(du, h, dh_res, modv_l)


def _prep_fn(q, k, qg, kg, cos_f, sin_a, sin_b):
    qs = [_rope(_rms(q[:, HD * i:HD * (i + 1)], qg), cos_f, sin_a, sin_b) * (HD ** -0.5) for i in range(A_HEADS)]
    ks = [_rope(_rms(k[:, HD * i:HD * (i + 1)], kg), cos_f, sin_a, sin_b) for i in range(A_HEADS // 2)]
    return jnp.concatenate(qs, 1), jnp.concatenate(ks, 1)


def _tok(tm, w, off):
    return pl.BlockSpec((tm, w), lambda i: (i, off // w))


def _vec(w):
    return pl.BlockSpec((1, w), lambda i: (0, 0))


def _prep_fwd(P, qg, kg, rope, tm, name):
    T = P.shape[0]

    def body(q_ref, k_ref, v_ref, qg_ref, kg_ref, c_ref, sa_ref, sb_ref, qn_ref, kn_ref, vb_ref):
        qn, kn = _prep_fn(q_ref[...].astype(f32), k_ref[...].astype(f32), qg_ref[...], kg_ref[...], c_ref[...], sa_ref[...],
                          sb_ref[...])
        qn_ref[...] = qn.astype(bf16)
        kn_ref[...] = kn.astype(bf16)
        vb_ref[...] = v_ref[...].astype(bf16)

    return pl.pallas_call(
        body, name=name, grid=(T // tm,),
        in_specs=[_tok(tm, 512, 0), _tok(tm, 256, A_K), _tok(tm, 256, A_V), _vec(HD), _vec(HD),
                  _tok(tm, HD, 0), _tok(tm, HD, 0), _tok(tm, HD, 0)],
        out_specs=[_tok(tm, 512, 0), _tok(tm, 256, 0), _tok(tm, 256, 0)],
        out_shape=[jax.ShapeDtypeStruct((T, 512), bf16), jax.ShapeDtypeStruct((T, 256), bf16),
                   jax.ShapeDtypeStruct((T, 256), bf16)],
        compiler_params=_cparams(("parallel",)),
    )(P, P, P, qg, kg, *rope)


def _prep_bwd(P, dqn, dkn, dv, qg, kg, rope, tm, name):
    T = P.shape[0]
    nt = T // tm

    def body(q_ref, k_ref, dq_ref, dk_ref, dv_ref, qg_ref, kg_ref, c_ref, sa_ref, sb_ref, o_ref, og_ref):
        tabs = (c_ref[...], sa_ref[...], sb_ref[...])
        _, vjp = jax.vjp(lambda q, k, a, b: _prep_fn(q, k, a, b, *tabs), q_ref[...].astype(f32), k_ref[...].astype(f32),
                         qg_ref[...], kg_ref[...])
        dq, dk, dqg, dkg = vjp((dq_ref[...], dk_ref[...]))
        o_ref[:, 0:A_K] = dq.astype(bf16)
        o_ref[:, A_K:A_V] = dk.astype(bf16)
        o_ref[:, A_V:W_A] = dv_ref[...].astype(bf16)
        _partial_rows(og_ref, [dqg, dkg])

    return pl.pallas_call(
        body, name=name, grid=(nt,),
        in_specs=[_tok(tm, 512, 0), _tok(tm, 256, A_K), _tok(tm, 512, 0), _tok(tm, 256, 0), _tok(tm, 256, 0),
                  _vec(HD), _vec(HD), _tok(tm, HD, 0), _tok(tm, HD, 0), _tok(tm, HD, 0)],
        out_specs=[_tok(tm, W_A, 0), pl.BlockSpec((None, SUB, HD), lambda i: (i, 0, 0))],
        out_shape=[jax.ShapeDtypeStruct((T, W_A), bf16), jax.ShapeDtypeStruct((nt, SUB, HD), f32)],
        compiler_params=_cparams(("parallel",)),
    )(P, P, dqn, dkn, dv, qg, kg, *rope)


def _attn_fn(q, k, v, lim):
    col = lax.broadcasted_iota(jnp.int32, (1, k.shape[0]), 1)
    s = mm_nt(q, k) + jnp.where(col < lim, 0.0, -1e30)
    m = lax.stop_gradient(jnp.max(s, -1, keepdims=True))
    e = jnp.exp(s - m)
    p = e * (1.0 / jnp.sum(e, -1, keepdims=True))
    return mm(p, v)


def _attn_fwd(qn, kn, vb, tc, tq, name):
    T = qn.shape[0]

    def body(q_ref, k_ref, v_ref, o_ref):
        lim = jnp.where(pl.program_id(1) * tq < tc, tc, T)
        o_ref[...] = _attn_fn(q_ref[...], k_ref[...], v_ref[...], lim)

    return pl.pallas_call(
        body, name=name, grid=(A_HEADS, T // tq),
        in_specs=[pl.BlockSpec((tq, HD), lambda h, i: (i, h)), pl.BlockSpec((T, HD), lambda h, i: (0, h // 2)),
                  pl.BlockSpec((T, HD), lambda h, i: (0, h // 2))],
        out_specs=pl.BlockSpec((tq, HD), lambda h, i: (i, h)),
        out_shape=jax.ShapeDtypeStruct((T, 512), f32),
        compiler_params=_cparams(("parallel", "parallel")),
    )(qn, kn, vb)


def _attn_bwd(qn, kn, vb, dya, tc, tq, name):
    T = qn.shape[0]

    def body(q_ref, k_ref, v_ref, g_ref, dq_ref, dk_ref, dv_ref):
        first = (pl.program_id(1) == 0) & (pl.program_id(2) == 0)
        lim = jnp.where(pl.program_id(2) * tq < tc, tc, T)
        _, vjp = jax.vjp(lambda q, k, v: _attn_fn(q, k, v, lim), q_ref[...].astype(f32), k_ref[...].astype(f32),
                         v_ref[...].astype(f32))
        dq, dk, dv = vjp(g_ref[...])
        dq_ref[...] = dq

        @pl.when(first)
        def _():
            dk_ref[...] = dk
            dv_ref[...] = dv

        @pl.when(jnp.logical_not(first))
        def _():
            dk_ref[...] += dk
            dv_ref[...] += dv

    qspec = pl.BlockSpec((tq, HD), lambda kv, g, i: (i, 2 * kv + g))
    kspec = pl.BlockSpec((T, HD), lambda kv, g, i: (0, kv))
    return pl.pallas_call(
        body, name=name, grid=(A_HEADS // 2, 2, T // tq),
        in_specs=[qspec, kspec, kspec, qspec], out_specs=[qspec, kspec, kspec],
        out_shape=[jax.ShapeDtypeStruct((T, 512), f32), jax.ShapeDtypeStruct((T, 256), f32),
                   jax.ShapeDtypeStruct((T, 256), f32)],
        compiler_params=_cparams(("parallel", "arbitrary", "arbitrary")),
    )(qn, kn, vb, dya)


def _conv_rows(tc, tl):
    return CONV_PAD + tc + CONV_PAD + tl + CONV_PAD


def _fill_pad(pad_ref, val, tc, tl):
    z = jnp.zeros((CONV_PAD, LANE), f32)
    pad_ref[0:CONV_PAD, :] = z
    pad_ref[CONV_PAD:CONV_PAD + tc, :] = val[0:tc]
    pad_ref[CONV_PAD + tc:2 * CONV_PAD + tc, :] = z
    pad_ref[2 * CONV_PAD + tc:2 * CONV_PAD + tc + tl, :] = val[tc:tc + tl]
    pad_ref[2 * CONV_PAD + tc + tl:3 * CONV_PAD + tc + tl, :] = z


def _conv_apply(pad_ref, w_ref, K, tc, tl, rc, emit, flip=False):
    half = K // 2
    for seg0, off, n in ((0, CONV_PAD, tc), (tc, 2 * CONV_PAD + tc, tl)):
        for r0 in range(0, n, rc):
            acc = None
            for k in range(K):
                sh = (half - k) if flip else (k - half)
                term = pad_ref[pl.ds(off + r0 + sh, rc), :] * w_ref[k:k + 1, :]
                acc = term if acc is None else acc + term
            emit(seg0 + r0, acc)


def _conv_wgrad(pad_ref, dy_ref, K, tc, tl, rc, dw_ref):
    half = K // 2
    for k in range(K):
        acc = jnp.zeros((1, LANE), f32)
        for seg0, off, n in ((0, CONV_PAD, tc), (tc, 2 * CONV_PAD + tc, tl)):
            for r0 in range(0, n, rc):
                acc = acc + jnp.sum(pad_ref[pl.ds(off + r0 + k - half, rc), :] * dy_ref[pl.ds(seg0 + r0, rc), :],
                                    axis=0, keepdims=True)
        dw_ref[k:k + 1, :] = acc


def _col(T, off):
    return pl.BlockSpec((T, LANE), lambda j: (0, off // LANE + j))


C_B, C_C, C_X, C_A, C_G = range(5)
N_SEC = 5


class _Sections:
    def __init__(self, refs):
        self.refs = refs

    def __getitem__(self, idx):
        rows, sec = idx
        return self.refs[sec][rows, :].astype(f32)

    def __setitem__(self, idx, val):
        rows, sec = idx
        self.refs[sec, rows, :] = val


def _sec_specs(T):
    return [pl.BlockSpec((T, LANE), functools.partial(lambda j, s: (0, s * (BRW // LANE) + j), s=s)) for s in range(N_SEC)]


def _conv_fwd(P, wb, wd, bd, tc, tl, rc, name):
    T = tc + tl

    def body(*refs):
        p_ref = _Sections(refs[:N_SEC])
        wb_ref, wd_ref, bd_ref, yb_ref, hh_ref, pad_ref = refs[N_SEC:]
        _fill_pad(pad_ref, p_ref[:, C_C] * p_ref[:, C_X], tc, tl)

        def emit_b(r0, y):
            yb_ref[pl.ds(r0, rc), :] = y * p_ref[pl.ds(r0, rc), C_B]

        _conv_apply(pad_ref, wb_ref, KB, tc, tl, rc, emit_b)
        _fill_pad(pad_ref, p_ref[:, C_A] * _sigmoid(p_ref[:, C_G]), tc, tl)

        def emit_d(r0, y):
            hh_ref[pl.ds(r0, rc), :] = y + bd_ref[...]

        _conv_apply(pad_ref, wd_ref, KD, tc, tl, rc, emit_d)

    return pl.pallas_call(
        body, name=name, grid=(BRW // LANE,),
        in_specs=_sec_specs(T) + [pl.BlockSpec((KB, LANE), lambda j: (0, j)), pl.BlockSpec((KD, LANE), lambda j: (0, j)),
                                  pl.BlockSpec((1, LANE), lambda j: (0, j))],
        out_specs=[_col(T, 0), _col(T, 0)],
        out_shape=[jax.ShapeDtypeStruct((T, BRW), f32), jax.ShapeDtypeStruct((T, BRW), f32)],
        scratch_shapes=[pltpu.VMEM((_conv_rows(tc, tl), LANE), f32)],
        compiler_params=_cparams(("parallel",)),
    )(*[P] * N_SEC, wb, wd, bd)


def _conv_bwd(P, dyb, dhh, wb, wd, tc, tl, rc, name):
    T = tc + tl

    def body(*refs):
        p_ref = _Sections(refs[:N_SEC])
        dyb_ref, dhh_ref, wb_ref, wd_ref, dp3_ref, dwb_ref, dwd_ref, dbd_ref, pad_ref, pad2_ref, tmp_ref = refs[N_SEC:]
        dp_ref = _Sections(dp3_ref)
        _fill_pad(pad_ref, p_ref[:, C_C] * p_ref[:, C_X], tc, tl)

        def emit_cv(r0, y):
            dp_ref[pl.ds(r0, rc), C_B] = (y * dyb_ref[pl.ds(r0, rc), :]).astype(bf16)

        _conv_apply(pad_ref, wb_ref, KB, tc, tl, rc, emit_cv)
        tmp_ref[...] = dyb_ref[...] * p_ref[:, C_B]
        _conv_wgrad(pad_ref, tmp_ref, KB, tc, tl, rc, dwb_ref)
        _fill_pad(pad2_ref, tmp_ref[...], tc, tl)

        def emit_ds(r0, y):
            dp_ref[pl.ds(r0, rc), C_C] = (y * p_ref[pl.ds(r0, rc), C_X]).astype(bf16)
            dp_ref[pl.ds(r0, rc), C_X] = (y * p_ref[pl.ds(r0, rc), C_C]).astype(bf16)

        _conv_apply(pad2_ref, wb_ref, KB, tc, tl, rc, emit_ds, flip=True)
        _fill_pad(pad_ref, p_ref[:, C_A] * _sigmoid(p_ref[:, C_G]), tc, tl)
        _conv_wgrad(pad_ref, dhh_ref, KD, tc, tl, rc, dwd_ref)
        dbd_ref[...] = jnp.sum(dhh_ref[...], axis=0, keepdims=True)
        _fill_pad(pad2_ref, dhh_ref[...], tc, tl)

        def emit_d2(r0, y):
            sg = _sigmoid(p_ref[pl.ds(r0, rc), C_G])
            a = p_ref[pl.ds(r0, rc), C_A]
            dp_ref[pl.ds(r0, rc), C_A] = (y * sg).astype(bf16)
            dp_ref[pl.ds(r0, rc), C_G] = (y * a * sg * (1.0 - sg)).astype(bf16)

        _conv_apply(pad2_ref, wd_ref, KD, tc, tl, rc, emit_d2, flip=True)

    return pl.pallas_call(
        body, name=name, grid=(BRW // LANE,),
        in_specs=_sec_specs(T) + [_col(T, 0), _col(T, 0),
                                  pl.BlockSpec((KB, LANE), lambda j: (0, j)), pl.BlockSpec((KD, LANE), lambda j: (0, j))],
        out_specs=[pl.BlockSpec((N_SEC, T, LANE), lambda j: (0, 0, j)), pl.BlockSpec((KB, LANE), lambda j: (0, j)),
                   pl.BlockSpec((KD, LANE), lambda j: (0, j)), pl.BlockSpec((1, LANE), lambda j: (0, j))],
        out_shape=[jax.ShapeDtypeStruct((N_SEC, T, BRW), bf16), jax.ShapeDtypeStruct((KB, BRW), f32),
                   jax.ShapeDtypeStruct((KD, BRW), f32), jax.ShapeDtypeStruct((1, BRW), f32)],
        scratch_shapes=[pltpu.VMEM((_conv_rows(tc, tl), LANE), f32), pltpu.VMEM((_conv_rows(tc, tl), LANE), f32),
                        pltpu.VMEM((T, LANE), f32)],
        compiler_params=_cparams(("parallel",)),
    )(*[P] * N_SEC, dyb, dhh, wb, wd)


def _gla_chunk(q, k, v, r, w2, b2, st, isfwd):
    z = mm(r, w2) + b2
    g = jax.nn.log_sigmoid(z[:, 0:C_KW] if isfwd else z[:, C_KW:2 * C_KW]) / C_TAU
    ri = lax.broadcasted_iota(jnp.int32, (CH, CH), 0)
    ci = lax.broadcasted_iota(jnp.int32, (CH, CH), 1)
    tri = ((ci <= ri) if isfwd else (ci >= ri)).astype(f32)
    cum = jnp.dot(tri, g, preferred_element_type=f32, precision=lax.Precision.HIGHEST)
    last = jnp.sum(g, axis=0, keepdims=True)
    q = q * (C_KW // C_HEADS) ** -0.5
    hv = lax.broadcasted_iota(jnp.int32, (BRW, C_KW), 0) // (BRW // C_HEADS)
    hk = lax.broadcasted_iota(jnp.int32, (BRW, C_KW), 1) // (C_KW // C_HEADS)
    st_new = st * jnp.exp(last) + jnp.where(hv == hk, mm_tn(v, k * jnp.exp(last - cum)), 0.0)
    o = mm_nt(q * jnp.exp(cum), st)
    rowi = lax.broadcasted_iota(jnp.int32, (CH, C_KW), 0)
    srow = lax.broadcasted_iota(jnp.int32, (C_HEADS * CH, C_KW), 0)
    slane = lax.broadcasted_iota(jnp.int32, (C_HEADS * CH, C_KW), 1)
    own_lanes = srow // CH == slane // (C_KW // C_HEADS)
    pos = lax.broadcasted_iota(jnp.int32, (C_HEADS * CH, CH), 0) % CH
    key = lax.broadcasted_iota(jnp.int32, (C_HEADS * CH, CH), 1)
    scores = jnp.zeros((C_HEADS * CH, CH), f32)
    for a in range(CH // GLA_SUB):
        idx = GLA_SUB * a - 1 if isfwd else GLA_SUB * (a + 1)
        ref = jnp.sum(jnp.where(rowi == idx, cum, 0.0), axis=0, keepdims=True)
        qa = q * jnp.exp(jnp.minimum(cum - ref, 0.0))
        ka = k * jnp.exp(jnp.minimum(ref - cum, GLA_CLAMP))
        s = mm_nt(jnp.where(own_lanes, jnp.concatenate([qa] * C_HEADS, axis=0), 0.0), ka)
        scores = scores + jnp.where(pos // GLA_SUB == a, s, 0.0)
    scores = jnp.where((key <= pos) if isfwd else (key >= pos), scores, 0.0)
    vw = BRW // C_HEADS
    o = o + jnp.concatenate([mm(scores[CH * hd:CH * (hd + 1)], v[:, vw * hd:vw * (hd + 1)]) for hd in range(C_HEADS)],
                            axis=1)
    return o, st_new


def _gla_chunk_of(d, n, nc, nch):
    back = jnp.where(n < nc, nc - 1 - n, nch - 1 - (n - nc))
    return jnp.where(d == 0, n, back)


def _gla_fwd(P, w2, b2, tc, name):
    T = P.shape[0]
    nch, nc = T // CH, tc // CH

    back = lambda n: _gla_chunk_of(1, n, nc, nch)

    def body(pf_ref, pb_ref, w_ref, b_ref, of_ref, ob_ref, ssf_ref, ssb_ref, stf_ref, stb_ref):
        @pl.when(pl.program_id(0) == 0)
        def _():
            stf_ref[...] = jnp.zeros_like(stf_ref)
            stb_ref[...] = jnp.zeros_like(stb_ref)

        for p_ref, o_ref, ss_ref, st_ref, isfwd in ((pf_ref, of_ref, ssf_ref, stf_ref, True),
                                                    (pb_ref, ob_ref, ssb_ref, stb_ref, False)):
            st = st_ref[...]
            ss_ref[...] = st
            p = p_ref[...].astype(f32)
            o, st_new = _gla_chunk(p[:, 0:G_K], p[:, G_K:G_V], p[:, G_V:G_R], p[:, G_R:W_G], w_ref[...], b_ref[...], st, isfwd)
            o_ref[...] = o
            st_ref[...] = st_new

    sd = jax.ShapeDtypeStruct
    return pl.pallas_call(
        body, name=name, grid=(nch,),
        in_specs=[pl.BlockSpec((CH, W_G), lambda n: (n, 0)), pl.BlockSpec((CH, W_G), lambda n: (back(n), 0)),
                  pl.BlockSpec((LANE, 512), lambda n: (0, 0)), pl.BlockSpec((1, 512), lambda n: (0, 0))],
        out_specs=[pl.BlockSpec((CH, BRW), lambda n: (n, 0)), pl.BlockSpec((CH, BRW), lambda n: (back(n), 0)),
                   pl.BlockSpec((None, BRW, C_KW), lambda n: (n, 0, 0)), pl.BlockSpec((None, BRW, C_KW), lambda n: (n, 0, 0))],
        out_shape=[sd((T, BRW), f32), sd((T, BRW), f32), sd((nch, BRW, C_KW), f32), sd((nch, BRW, C_KW), f32)],
        scratch_shapes=[pltpu.VMEM((BRW, C_KW), f32), pltpu.VMEM((BRW, C_KW), f32)],
        compiler_params=_cparams(("arbitrary",)),
    )(P, P, w2, b2)


def _gla_bwd(P, w2, b2, ssave, doc, tc, name):
    T = P.shape[0]
    nch, nc = T // CH, tc // CH

    fwd_chunk = lambda m: nch - 1 - m
    back_chunk = lambda m: _gla_chunk_of(1, nch - 1 - m, nc, nch)

    def body(pf_ref, pb_ref, w_ref, b_ref, ssf_ref, ssb_ref, gf_ref, gb_ref, dpf_ref, dpb_ref, dw_ref, db_ref,
             dstf_ref, dstb_ref):
        m = pl.program_id(0)

        @pl.when(m == 0)
        def _():
            dstf_ref[...] = jnp.zeros_like(dstf_ref)
            dstb_ref[...] = jnp.zeros_like(dstb_ref)

        dw_sum, db_sum = None, None
        for p_ref, ss_ref, g_ref, dp_ref, dst_ref, isfwd in ((pf_ref, ssf_ref, gf_ref, dpf_ref, dstf_ref, True),
                                                             (pb_ref, ssb_ref, gb_ref, dpb_ref, dstb_ref, False)):
            p = p_ref[...].astype(f32)
            _, vjp = jax.vjp(lambda q, k, v, r, w, b, st: _gla_chunk(q, k, v, r, w, b, st, isfwd),
                             p[:, 0:G_K], p[:, G_K:G_V], p[:, G_V:G_R], p[:, G_R:W_G], w_ref[...], b_ref[...], ss_ref[...])
            dq, dk, dv, dr, dw, db, dst = vjp((g_ref[...], dst_ref[...]))
            dp_ref[:, 0:G_K] = dq
            dp_ref[:, G_K:G_V] = dk
            dp_ref[:, G_V:G_R] = dv
            dp_ref[:, G_R:W_G] = dr
            dst_ref[...] = dst
            dw_sum = dw if dw_sum is None else dw_sum + dw
            db_sum = db if db_sum is None else db_sum + db

        @pl.when(m == 0)
        def _():
            dw_ref[...] = dw_sum
            _partial_rows(db_ref, [db_sum])

        @pl.when(m > 0)
        def _():
            dw_ref[...] += dw_sum
            db_ref[0:1, :] += db_sum

    ssf, ssb = ssave
    chunk_f = lambda w: pl.BlockSpec((CH, w), lambda m: (fwd_chunk(m), 0))
    chunk_b = lambda w: pl.BlockSpec((CH, w), lambda m: (back_chunk(m), 0))
    state = pl.BlockSpec((None, BRW, C_KW), lambda m: (nch - 1 - m, 0, 0))
    sd = jax.ShapeDtypeStruct
    return pl.pallas_call(
        body, name=name, grid=(nch,),
        in_specs=[chunk_f(W_G), chunk_b(W_G), pl.BlockSpec((LANE, 512), lambda m: (0, 0)), pl.BlockSpec((1, 512), lambda m: (0, 0)),
                  state, state, chunk_f(BRW), chunk_b(BRW)],
        out_specs=[chunk_f(W_G), chunk_b(W_G), pl.BlockSpec((LANE, 512), lambda m: (0, 0)), pl.BlockSpec((SUB, 512), lambda m: (0, 0))],
        out_shape=[sd((T, W_G), f32), sd((T, W_G), f32), sd((LANE, 512), f32), sd((SUB, 512), f32)],
        scratch_shapes=[pltpu.VMEM((BRW, C_KW), f32), pltpu.VMEM((BRW, C_KW), f32)],
        compiler_params=_cparams(("arbitrary",)),
    )(P, P, w2, b2, ssf, ssb, doc, doc)


def _sum_dirs(a, b, tm, name):
    T, W = a.shape

    def body(a_ref, b_ref, o_ref):
        o_ref[...] = (a_ref[...] + b_ref[...]).astype(bf16)

    spec = pl.BlockSpec((tm, W), lambda i: (i, 0))
    return pl.pallas_call(
        body, name=name, grid=(T // tm,), in_specs=[spec, spec], out_specs=spec,
        out_shape=jax.ShapeDtypeStruct((T, W), bf16),
        compiler_params=_cparams(("parallel",)),
    )(a, b)


def _merge_fn(h, m_l, m_c, isctx, ya, ga, yb, gb, of, ob, gc, hh, gd, mg, es, ey, cn, dng, dnb, lg, lb, wbr, wout):
    oc = of + ob
    yc = jnp.concatenate([_rms(oc[:, HD * i:HD * (i + 1)], cn[:, HD * i:HD * (i + 1)]) for i in range(C_HEADS)], 1)
    brs = [ya * _silu(ga), yb * _silu(gb), yc * _silu(gc), _silu(_ln(hh) * dng + dnb) * _silu(gd)]
    acc = None
    for i in range(4):
        t = _sigmoid(mg[:, D * i:D * (i + 1)]) * (mm(brs[i], wbr[i]) + es[i])
        acc = t if acc is None else acc + t
    y = mm(acc, wout) + ey
    gate = jnp.where(isctx, m_c[:, 2 * D:3 * D], m_l[:, 2 * D:3 * D])
    hn = _ln(ALPHA * h + gate * y) * lg + lb
    return hn, (brs, acc)


def _merge_specs(tm):
    t = lambda w, off=0: _tok(tm, w, off)
    return [t(D), pl.BlockSpec((SUB, 3 * D), lambda i: (0, 0)),
            t(BRW), t(BRW, M_GA), t(BRW), t(BRW, M_GB),
            t(BRW), t(BRW),
            t(BRW, M_GC), t(BRW), t(BRW, M_GD), t(4 * D, 0),
            _vec(BRW), _vec(BRW), _vec(BRW), _vec(D), _vec(D),
            pl.BlockSpec((4, BRW, D), lambda i: (0, 0, 0)), pl.BlockSpec((D, D), lambda i: (0, 0))]


def _merge_fwd(h, modv_l, ya, yb, o2, hh, P, cn, dng, dnb, lg, lb, wbr, wout, tc, tm, name):
    T = h.shape[0]

    def body(h_ref, m_ref, ya_ref, ga_ref, yb_ref, gb_ref, of_ref, ob_ref, gc_ref, hh_ref, gd_ref, mg_ref,
             cn_ref, dng_ref, dnb_ref, lg_ref, lb_ref, wbr_ref, wout_ref, o_ref):
        isctx = _row_ids(pl.program_id(0), tm) < tc
        zero = jnp.zeros((tm, D), f32)
        up = lambda r: r[...].astype(f32)
        hn, _ = _merge_fn(h_ref[...], m_ref[0:1, :], m_ref[1:2, :], isctx, ya_ref[...], up(ga_ref), yb_ref[...],
                          up(gb_ref), of_ref[...], ob_ref[...], up(gc_ref), hh_ref[...], up(gd_ref), up(mg_ref),
                          [zero] * 4, zero, cn_ref[...], dng_ref[...], dnb_ref[...], lg_ref[...], lb_ref[...],
                          [wbr_ref[i] for i in range(4)], wout_ref[...])
        o_ref[...] = hn

    return pl.pallas_call(
        body, name=name, grid=(T // tm,),
        in_specs=_merge_specs(tm), out_specs=_tok(tm, D, 0),
        out_shape=jax.ShapeDtypeStruct((T, D), f32),
        compiler_params=_cparams(("parallel",)),
    )(h, modv_l, ya, P, yb, P, o2[0], o2[1], P, hh, P, P, cn, dng, dnb, lg, lb, wbr, wout)


def _merge_bwd(dhn, h, modv_l, ya, yb, o2, hh, P, cn, dng, dnb, lg, lb, wbr, wout, tc, tm, name):
    T = h.shape[0]
    nt = T // tm

    def body(g_ref, h_ref, m_ref, ya_ref, ga_ref, yb_ref, gb_ref, of_ref, ob_ref, gc_ref, hh_ref, gd_ref, mg_ref,
             cn_ref, dng_ref, dnb_ref, lg_ref, lb_ref, wbr_ref, wout_ref,
             dh_ref, dm_ref, dya_ref, dyb_ref, doc_ref, dhh_ref, dp_ref,
             br_ref, z_ref, acc_ref, dy_ref, dv5_ref, dvd_ref):
        isctx = _row_ids(pl.program_id(0), tm) < tc
        zero = jnp.zeros((tm, D), f32)
        wbr_v = [wbr_ref[i] for i in range(4)]
        wout_v = wout_ref[...]
        up = lambda r: r[...].astype(f32)

        def fn(h, ml, mc, ya, ga, yb, gb, oc, gc, hh, gd, mg, e0, e1, e2, e3, ey, cn, dng, dnb, lg, lb):
            return _merge_fn(h, ml, mc, isctx, ya, ga, yb, gb, oc, jnp.zeros_like(oc), gc, hh, gd, mg,
                             [e0, e1, e2, e3], ey, cn, dng, dnb, lg, lb, wbr_v, wout_v)

        _, vjp, (brs, acc) = jax.vjp(
            fn, h_ref[...], m_ref[0:1, :], m_ref[1:2, :], ya_ref[...], up(ga_ref), yb_ref[...], up(gb_ref),
            of_ref[...] + ob_ref[...], up(gc_ref), hh_ref[...], up(gd_ref), up(mg_ref), zero, zero, zero, zero, zero,
            cn_ref[...], dng_ref[...], dnb_ref[...], lg_ref[...], lb_ref[...], has_aux=True)
        (dh, dml, dmc, dya, dga, dyb, dgb, doc, dgc, dhh, dgd, dmg, z0, z1, z2, z3, dy,
         dcn, ddng, ddnb, dlg, dlb) = vjp(g_ref[...])
        dh_ref[...] = dh
        _partial_rows(dm_ref, [dml, dmc])
        dya_ref[...] = dya
        dyb_ref[...] = dyb
        doc_ref[...] = doc
        dhh_ref[...] = dhh
        dp_ref[:, 0:M_GA] = dmg.astype(bf16)
        dp_ref[:, M_GA:M_GB] = dga.astype(bf16)
        dp_ref[:, M_GB:M_GC] = dgb.astype(bf16)
        dp_ref[:, M_GC:M_GD] = dgc.astype(bf16)
        dp_ref[:, M_GD:W_M] = dgd.astype(bf16)
        for i, z in enumerate((z0, z1, z2, z3)):
            br_ref[i] = brs[i].astype(bf16)
            z_ref[i] = z.astype(bf16)
        acc_ref[...] = acc.astype(bf16)
        dy_ref[...] = dy.astype(bf16)
        _partial_rows(dv5_ref, [dcn, ddng, ddnb])
        _partial_rows(dvd_ref, [dlg, dlb])

    t = lambda w: _tok(tm, w, 0)
    part = lambda w: pl.BlockSpec((None, SUB, w), lambda i: (i, 0, 0))
    sd = jax.ShapeDtypeStruct
    return pl.pallas_call(
        body, name=name, grid=(nt,),
        in_specs=[t(D)] + _merge_specs(tm),
        out_specs=[t(D), part(3 * D)] + [t(BRW)] * 4 + [t(W_M),
                   pl.BlockSpec((4, tm, BRW), lambda i: (0, i, 0)), pl.BlockSpec((4, tm, D), lambda i: (0, i, 0)),
                   t(D), t(D), part(BRW), part(D)],
        out_shape=[sd((T, D), f32), sd((nt, SUB, 3 * D), f32)] + [sd((T, BRW), f32)] * 4 + [sd((T, W_M), bf16),
                   sd((4, T, BRW), bf16), sd((4, T, D), bf16), sd((T, D), bf16), sd((T, D), bf16),
                   sd((nt, SUB, BRW), f32), sd((nt, SUB, D), f32)],
        compiler_params=_cparams(("parallel",)),
    )(dhn, h, modv_l, ya, P, yb, P, o2[0], o2[1], P, hh, P, P, cn, dng, dnb, lg, lb, wbr, wout)


def _loss_kernel(h, tgt, tc, tm, name):
    T = h.shape[0]
    nt = T // tm
    nct = tc // tm

    def body(h_ref, t_ref, d_ref, l_ref):
        i = pl.program_id(0)
        err = h_ref[...] - t_ref[...]
        lat = (i >= nct).astype(f32)
        d_ref[...] = err * (lat / D)
        l_ref[...] = jnp.zeros((SUB, LANE), f32) + lat * 0.5 * jnp.sum(err * err) / D

    return pl.pallas_call(
        body, name=name, grid=(nt,),
        in_specs=[pl.BlockSpec((tm, D), lambda i: (i, 0)),
                  pl.BlockSpec((tm, D), lambda i: (jnp.maximum(i - nct, 0), 0))],
        out_specs=[pl.BlockSpec((tm, D), lambda i: (i, 0)), pl.BlockSpec((None, SUB, LANE), lambda i: (i, 0, 0))],
        out_shape=[jax.ShapeDtypeStruct((T, D), f32), jax.ShapeDtypeStruct((nt, SUB, LANE), f32)],
        compiler_params=_cparams(("parallel",)),
    )(h, tgt)


def _rope_tables(tc, tl):
    t = jnp.arange(tl)
    inv = ROPE_THETA ** (-jnp.arange(0, HD // 2, 2, dtype=f32) / (HD // 2))
    ang = jnp.concatenate([(t // GRID_W).astype(f32)[:, None] * inv, (t % GRID_W).astype(f32)[:, None] * inv], -1)
    cos, sin = jnp.repeat(jnp.cos(ang), 2, axis=1), jnp.repeat(jnp.sin(ang), 2, axis=1)
    even = (jnp.arange(HD) % 2 == 0)[None, :]
    cos_f = jnp.concatenate([jnp.ones((tc, HD), f32), cos], 0)
    sin_a = jnp.concatenate([jnp.zeros((tc, HD), f32), jnp.where(even, -sin, 0.0)], 0)
    sin_b = jnp.concatenate([jnp.zeros((tc, HD), f32), jnp.where(even, 0.0, sin)], 0)
    return cos_f, sin_a, sin_b


N_CHIPS = 4
SHARD = N_IN // N_CHIPS


def _group_ranges():
    return dict(M=[(S_MG, 4 * D), (S_GA, BRW), (S_GB, BRW), (S_GC, BRW), (S_GD, BRW)], A=[(S_Q, W_A)],
                C=[(S_B, 3 * BRW), (S_DA, 2 * BRW)], G=[(S_CQ, 2 * C_KW + BRW), (S_R, 2 * C_RANK)])


def _group_weights(w4):
    out = {}
    for k, ranges in _group_ranges().items():
        parts = []
        for a, n in ranges:
            while n > 0:
                s, r = divmod(a, SHARD)
                m = min(n, SHARD - r)
                parts.append(w4[s, r:r + m])
                a, n = a + m, n - m
        if k == "G":
            parts.append(jnp.zeros((LANE - 2 * C_RANK, D), w4.dtype))
        out[k] = jnp.concatenate(parts, 0)
    return out


def _ungroup(g):
    secs = []
    for k, ranges in _group_ranges().items():
        off = 0
        for a, n in ranges:
            secs.append((a, g[k][off:off + n]))
            off += n
    return jnp.concatenate([v for _, v in sorted(secs, key=lambda t: t[0])], 0)


PROJ_TN = dict(M=2048, A=1024, C=1280, G=1152)
DU_TK = dict(M=2048, A=1024, C=BRW, G=1152)
DWP_TN = dict(M=768, A=1024, C=BRW, G=1152)


def _gate_weights(w2_l, gb_l):
    w = jnp.zeros((LANE, 2 * C_KW), f32)
    w = w.at[0:C_RANK, 0:C_KW].set(w2_l[0]).at[C_RANK:2 * C_RANK, C_KW:2 * C_KW].set(w2_l[1])
    return w, jnp.concatenate([gb_l[0], gb_l[1]])[None, :]


def _local_step(x1, c1, ctx1, tgt1, c_ctx, b_mod, weights_of, q_norm, k_norm, b_conv, w2, gb, c_norm, d_conv_w,
                d_conv_b, d_norm_g, d_norm_b, grads_done, ln_g, ln_b, tm, token=None):
    tc, tl = ctx1.shape[0], x1.shape[0]
    T = tc + tl
    rc = min(256, tc)
    tmb = tm // 2
    tmm = 768 if T % 768 == 0 else tm
    rope = _rope_tables(tc, tl)
    cin = jnp.concatenate([c1, c_ctx[None, :], jnp.zeros((SUB - 2, D), f32)], 0)
    if token is not None:
        cin = cin + token[:, 0:1]
    row = lambda v: v[None, :]

    h = jnp.concatenate([ctx1, x1], 0)
    saved, wp, w_br, w_out, w_mod, modv = [], *([None] * DEPTH for _ in range(5))
    for l in range(DEPTH):
        wp[l], merge_weights, w_mod[l] = weights_of(l, h)
        modv[l] = _mod_fwd(cin, w_mod[l], b_mod[l], f"mod_fwd{l}")
        u = _ln_fwd(h, modv[l], tc, tm, f"ln_fwd{l}")
        P = {k: _matmul(u, wp[l][k], "nt", tmm, PROJ_TN[k], D, f"proj{l}{k}", out_dtype=bf16) for k in GROUPS}
        qn, kn, vb = _prep_fwd(P["A"], row(q_norm[l]), row(k_norm[l]), rope, tm, f"prep_fwd{l}")
        ya = _attn_fwd(qn, kn, vb, tc, tm, f"attn_fwd{l}")
        yb, hh = _conv_fwd(P["C"], b_conv[l], d_conv_w[l], row(d_conv_b[l]), tc, tl, rc, f"conv_fwd{l}")
        w2p, b2p = _gate_weights(w2[l], gb[l])
        gla = _gla_fwd(P["G"], w2p, b2p, tc, f"gla_fwd{l}")
        o2, ssave = gla[:2], gla[2:]
        w_br[l], w_out[l] = merge_weights(o2[0])
        hn = _merge_fwd(h, modv[l], ya, yb, o2, hh, P["M"], row(c_norm[l]), row(d_norm_g[l]), row(d_norm_b[l]),
                        row(ln_g[l]), row(ln_b[l]), w_br[l], w_out[l], tc, tm, f"merge_fwd{l}")
        saved.append((h, u, P, qn, kn, vb, ya, yb, hh, o2, ssave, w2p, b2p))
        h = hn

    dh, lparts = _loss_kernel(h, tgt1, tc, tm, "loss")
    loss = jnp.sum(lparts[:, 0, 0])

    g = {k: [None] * DEPTH for k in ("wp", "q_norm", "k_norm", "b_conv", "w2", "gb", "c_norm", "d_conv_w", "d_conv_b",
                                     "d_norm_g", "d_norm_b", "w_br", "w_out", "ln_g", "ln_b", "modv")}
    for l in reversed(range(DEPTH)):
        h_in, u, P, qn, kn, vb, ya, yb, hh, o2, ssave, w2p, b2p = saved[l]
        dP = {}
        (dh_res, dm_mg, dya, dyb, doc, dhh, dP["M"], br, z, acc, dy, dv5, dvd) = _merge_bwd(
            dh, h_in, modv[l], ya, yb, o2, hh, P["M"], row(c_norm[l]), row(d_norm_g[l]), row(d_norm_b[l]),
            row(ln_g[l]), row(ln_b[l]), w_br[l], w_out[l], tc, tmb, f"merge_bwd{l}")
        g["w_br"][l] = _matmul_tn_batched(br, z, N_CHIPS, f"dwbr{l}")
        g["w_out"][l] = _matmul(acc, dy, "tn", D, D, T, f"dwout{l}", out_dtype=bf16)
        tk = grads_done(l, {k: g[k][l] for k in ("w_br", "w_out")})
        qg_l = row(q_norm[l]) if tk is None else row(q_norm[l]) + tk[0:1, :]
        v5 = jnp.sum(dv5, 0)
        g["c_norm"][l], g["d_norm_g"][l], g["d_norm_b"][l] = v5[0], v5[1], v5[2]
        vd = jnp.sum(dvd, 0)
        g["ln_g"][l], g["ln_b"][l] = vd[0], vd[1]
        dqn, dkn, dv = _attn_bwd(qn, kn, vb, dya, tc, tm, f"attn_bwd{l}")
        dP["A"], dqk = _prep_bwd(P["A"], dqn, dkn, dv, qg_l, row(k_norm[l]), rope, tm, f"prep_bwd{l}")
        dqk = jnp.sum(dqk, 0)
        g["q_norm"][l], g["k_norm"][l] = dqk[0], dqk[1]
        dP["C"], dwb, dwd, dbd = _conv_bwd(P["C"], dyb, dhh, b_conv[l], d_conv_w[l], tc, tl, rc, f"conv_bwd{l}")
        g["b_conv"][l], g["d_conv_w"][l], g["d_conv_b"][l] = dwb, dwd, dbd[0]
        dpf, dpb, dw2p, db2p = _gla_bwd(P["G"], w2p, b2p, ssave, doc, tc, f"gla_bwd{l}")
        dP["G"] = _sum_dirs(dpf, dpb, tm, f"gla_sum{l}")
        db2p = db2p[0]
        g["w2"][l] = jnp.stack([dw2p[0:C_RANK, 0:C_KW], dw2p[C_RANK:2 * C_RANK, C_KW:2 * C_KW]])
        g["gb"][l] = jnp.stack([db2p[0:C_KW], db2p[C_KW:2 * C_KW]])
        g["wp"][l] = {k: _matmul(dP[k], u, "tn", DWP_TN[k], D, T, f"dwp{l}{k}", out_dtype=bf16) for k in GROUPS}
        tk = grads_done(l, {"wp": g["wp"][l]})
        du = _matmul_groups(dP, wp[l], DU_TK, tmm, f"du{l}", after=tk)
        dh, dm_ln = _ln_bwd(du, h_in, dh_res, modv[l], tc, tm, f"ln_bwd{l}", latent_only=(l == 0))
        g["modv"][l] = jnp.sum(dm_mg, 0) + jnp.sum(dm_ln, 0)

    dmodv = jnp.stack(g.pop("modv"))
    g["w_mod"], dcin = _mod_bwd(cin, jnp.stack(w_mod, axis=1), dmodv)
    g["b_mod"] = dmodv[:, 0, :] + dmodv[:, 1, :]
    g["c_ctx"] = jnp.sum(dcin, (0, 1))[1]
    return loss, dh, g


HALF_TL = 256


TILE_BYTES = 1 << 20


def _row_tile(rows, cols, itemsize=4):
    tr = min(rows, 128)
    while rows % (2 * tr) == 0 and 2 * tr * cols * itemsize <= TILE_BYTES:
        tr *= 2
    return tr


def _adamw(w, g, m, v, name, tr=None, after=None):
    L, R, C = w.shape
    tr = _row_tile(R, C) if tr is None else tr
    if R % tr == 0:
        grid, spec = (L, R // tr), pl.BlockSpec((None, tr, C), lambda l, i: (l, i, 0))
    elif R * C * 4 <= (1 << 20):
        grid, spec = (L, 1), pl.BlockSpec((None, R, C), lambda l, i: (l, 0, 0))
    else:
        grid, spec = (L, C // HALF_TL), pl.BlockSpec((None, R, HALF_TL), lambda l, i: (l, 0, i))

    def body(w_ref, g_ref, m_ref, v_ref, *rest):
        go_ref, d_ref, nm_ref, nv_ref = rest[-4:]
        gg = g_ref[...]
        go_ref[...] = gg
        nm = B1 * m_ref[...] + (1.0 - B1) * gg
        nv = B2 * v_ref[...] + (1.0 - B2) * (gg * gg)
        m_hat = nm / (1.0 - B1 ** STEP)
        v_hat = nv / (1.0 - B2 ** STEP)
        d_ref[...] = -LR * (m_hat / (jnp.sqrt(v_hat) + AEPS) + WD * w_ref[...])
        nm_ref[...] = nm
        nv_ref[...] = nv

    return pl.pallas_call(
        body, name=name, grid=grid, in_specs=[spec] * 4 + ([] if after is None else [pl.BlockSpec(memory_space=pl.ANY)]),
        out_specs=[spec] * 4, out_shape=[jax.ShapeDtypeStruct((L, R, C), f32)] * 4,
        compiler_params=_cparams(("parallel", "parallel")),
    )(w, g, m, v, *([] if after is None else [after]))


MESH = pl.DeviceIdType.MESH
ANY = pl.BlockSpec(memory_space=pl.ANY)
N_CHIPS = 4


def _place():
    x, y, c = lax.axis_index("x"), lax.axis_index("y"), lax.axis_index("c")
    chips = [(1 - x, y), (x, 1 - y), (1 - x, 1 - y)]
    return x, y, c, chips


def _half(ref, c, axis):
    n = ref.shape[axis] // 2
    last = axis in (-1, ref.ndim - 1)
    idx = [slice(None)] * ref.ndim
    idx[axis] = pl.ds(pl.multiple_of(c * n, LANE if last else SUB), n)
    return ref.at[tuple(idx)]


def _half_shape(shape, axis):
    s = list(shape)
    s[axis] //= 2
    return tuple(s)


def _all_gather(arrs, axes, name):
    n = len(arrs)

    def body(*refs):
        ins, outs = refs[:n], refs[n:2 * n]
        send, recv = refs[2 * n:]
        x, y, c, chips = _place()
        me, sib = 2 * x + y, (x, y, 1 - c)

        def copy(a, k, chip_idx, cc, to, src=None):
            blk = _half(outs[a].at[chip_idx], cc, axes[a])
            return pltpu.make_async_remote_copy(src_ref=blk if src is None else src, dst_ref=blk,
                                                send_sem=send.at[7 * a + k], recv_sem=recv.at[7 * a + k],
                                                device_id=to, device_id_type=MESH)

        own = [pltpu.make_async_remote_copy(src_ref=ins[a], dst_ref=outs[a].at[me], send_sem=send.at[7 * a + 6],
                                            recv_sem=recv.at[7 * a + 6], device_id=sib, device_id_type=MESH)
               for a in range(n)]
        first = own + [copy(a, j, me, c, (*chip, c), src=_half(ins[a], c, axes[a]))
                       for a in range(n) for j, chip in enumerate(chips)]
        for cp in first:
            cp.start()
        passed = []
        for a in range(n):
            for j, chip in enumerate(chips):
                k = 2 * chip[0] + chip[1]
                copy(a, j, k, c, sib).wait_recv()
                fwd = copy(a, 3 + j, k, c, sib)
                fwd.start()
                passed.append(fwd)
        for a in range(n):
            own[a].wait_recv()
            for j, chip in enumerate(chips):
                copy(a, 3 + j, 2 * chip[0] + chip[1], 1 - c, sib).wait_recv()
        for cp in first + passed:
            cp.wait_send()

    return pl.pallas_call(
        body, name=name, in_specs=[ANY] * n, out_specs=[ANY] * n,
        out_shape=[jax.ShapeDtypeStruct((N_CHIPS,) + a.shape, a.dtype) for a in arrs],
        scratch_shapes=[pltpu.SemaphoreType.DMA((7 * n,)), pltpu.SemaphoreType.DMA((7 * n,))],
    )(*arrs)


def _sibling_halves(arrs, axes, name):
    n = len(arrs)

    def body(*refs):
        ins, outs = refs[:n], refs[n:2 * n]
        send, recv = refs[2 * n:]
        x, y, c, _ = _place()
        cps = [pltpu.make_async_remote_copy(src_ref=_half(ins[a], 1 - c, axes[a] + 1), dst_ref=outs[a], send_sem=send.at[a],
                                            recv_sem=recv.at[a], device_id=(x, y, 1 - c), device_id_type=MESH)
               for a in range(n)]
        for cp in cps:
            cp.start()
        for cp in cps:
            cp.wait()

    return pl.pallas_call(
        body, name=name, in_specs=[ANY] * n, out_specs=[ANY] * n,
        out_shape=[jax.ShapeDtypeStruct(_half_shape(a.shape, axes[i] + 1), a.dtype) for i, a in enumerate(arrs)],
        scratch_shapes=[pltpu.SemaphoreType.DMA((n,)), pltpu.SemaphoreType.DMA((n,))],
    )(*arrs)


def _add_half(gfull, land, cidx, axis, name, tr=None, out_dtype=bf16):
    _, hr, hc = land.shape
    if axis == 0:
        tr = min(tr, hr) if tr else _row_tile(hr, hc)
        nb, blk = hr // tr, (None, tr, hc)
        g_spec = pl.BlockSpec(blk, lambda s, i, cr: (s, cr[0] * nb + i, 0))
        l_spec = pl.BlockSpec(blk, lambda s, i, cr: (s, i, 0))
    else:
        nb, blk = hc // HALF_TL, (None, hr, HALF_TL)
        g_spec = pl.BlockSpec(blk, lambda s, i, cr: (s, 0, cr[0] * nb + i))
        l_spec = pl.BlockSpec(blk, lambda s, i, cr: (s, 0, i))

    def body(c_ref, g_ref, l_ref, o_ref):
        o_ref[...] = (g_ref[...].astype(f32) + l_ref[...].astype(f32)).astype(o_ref.dtype)

    return pl.pallas_call(
        body, name=name,
        grid_spec=pltpu.PrefetchScalarGridSpec(
            num_scalar_prefetch=1, grid=(N_CHIPS, nb), in_specs=[g_spec, l_spec], out_specs=l_spec),
        out_shape=jax.ShapeDtypeStruct((N_CHIPS, hr, hc), out_dtype),
        compiler_params=_cparams(("parallel", "parallel")),
    )(cidx, gfull, land)


def _chip_exchange(arrs, name):
    n = len(arrs)

    def body(*refs):
        ins, outs = refs[:n], refs[n:2 * n]
        send, recv = refs[2 * n:]
        x, y, c, chips = _place()
        me = 2 * x + y
        cps = []
        for a in range(n):
            for j, chip in enumerate(chips):
                k = 2 * chip[0] + chip[1]
                cps.append((pltpu.make_async_remote_copy(
                    src_ref=ins[a].at[k], dst_ref=outs[a].at[me], send_sem=send.at[3 * a + j], recv_sem=recv.at[3 * a + j],
                    device_id=(*chip, c), device_id_type=MESH), a, j, k))
        for cp, *_ in cps:
            cp.start()
        for cp, a, j, k in cps:
            pltpu.make_async_remote_copy(src_ref=ins[a].at[k], dst_ref=outs[a].at[k], send_sem=send.at[3 * a + j],
                                         recv_sem=recv.at[3 * a + j], device_id=(x, y, c), device_id_type=MESH).wait_recv()
        for cp, *_ in cps:
            cp.wait_send()

    return pl.pallas_call(
        body, name=name, in_specs=[ANY] * n, out_specs=[ANY] * n,
        out_shape=[jax.ShapeDtypeStruct(a.shape, a.dtype) for a in arrs],
        scratch_shapes=[pltpu.SemaphoreType.DMA((3 * n,)), pltpu.SemaphoreType.DMA((3 * n,))],
    )(*arrs)


def _sum_chips(land, own, place, axis, layer, into, name, tr=None):
    _, hr, hc = land.shape
    fresh = not hasattr(into, "dtype")
    shape = tuple(into) if fresh else into.shape
    if axis == 0:
        tr = min(tr, hr) if tr else _row_tile(hr, 4 * hc, 2)
        nb, blk = hr // tr, (tr, hc)
        l_map, m_map = (lambda i, p: (0, i, 0)), (lambda i, p: (p[0], i, 0))
        o_map = lambda i, p: (layer, p[1] * nb + i, 0)
    else:
        nb, blk = hc // HALF_TL, (hr, HALF_TL)
        l_map, m_map = (lambda i, p: (0, 0, i)), (lambda i, p: (p[0], 0, i))
        o_map = lambda i, p: (layer, 0, p[1] * nb + i)

    def body(p_ref, l_ref, o_ref, *rest):
        me = p_ref[0]
        mine = o_ref[...].astype(f32)
        acc = None
        for k in range(N_CHIPS):
            t = jnp.where(me == k, mine, l_ref[k].astype(f32))
            acc = t if acc is None else acc + t
        rest[-1][...] = acc

    return pl.pallas_call(
        body, name=name,
        grid_spec=pltpu.PrefetchScalarGridSpec(
            num_scalar_prefetch=1, grid=(nb,),
            in_specs=[pl.BlockSpec((N_CHIPS,) + blk, l_map), pl.BlockSpec((None,) + blk, m_map)] + ([] if fresh else [ANY]),
            out_specs=pl.BlockSpec((None,) + blk, o_map)),
        out_shape=jax.ShapeDtypeStruct(shape, f32),
        input_output_aliases={} if fresh else {3: 0},
        compiler_params=_cparams(("parallel",)),
    )(place, land, own, *([] if fresh else [into]))


def _sibling_fill(arrs, axes, name):
    n = len(arrs)

    def body(*refs):
        outs = refs[n:2 * n]
        send, recv = refs[2 * n:]
        x, y, c, _ = _place()
        cps = [pltpu.make_async_remote_copy(src_ref=_half(outs[a], c, axes[a] + 1), dst_ref=_half(outs[a], c, axes[a] + 1),
                                            send_sem=send.at[a], recv_sem=recv.at[a], device_id=(x, y, 1 - c),
                                            device_id_type=MESH) for a in range(n)]
        for cp in cps:
            cp.start()
        for a in range(n):
            blk = _half(outs[a], 1 - c, axes[a] + 1)
            pltpu.make_async_remote_copy(src_ref=blk, dst_ref=blk, send_sem=send.at[a], recv_sem=recv.at[a],
                                         device_id=(x, y, 1 - c), device_id_type=MESH).wait_recv()
        for cp in cps:
            cp.wait_send()

    return pl.pallas_call(
        body, name=name, in_specs=[ANY] * n, out_specs=[ANY] * n,
        out_shape=[jax.ShapeDtypeStruct(a.shape, a.dtype) for a in arrs],
        input_output_aliases={a: a for a in range(n)},
        scratch_shapes=[pltpu.SemaphoreType.DMA((n,)), pltpu.SemaphoreType.DMA((n,))],
    )(*arrs)


HBM = pl.BlockSpec(memory_space=pltpu.HBM)
SEM = pl.BlockSpec(memory_space=pltpu.SEMAPHORE)
EFFECT = pltpu.SideEffectType.DATAFLOW_SIDE_EFFECTING
PEERS = 4


def _split_copies(srcs, lands, send, recv, gather, axes=None):
    x, y, c, chips = _place()
    me = 2 * x + y
    if axes is not None:
        out = []
        for a in range(len(srcs)):
            sems = dict(send_sem=send.at[PEERS * a], recv_sem=recv.at[PEERS * a], device_id=(x, y, 1 - c), device_id_type=MESH)
            copy = pltpu.make_async_remote_copy(src_ref=_half(srcs[a], 1 - c, axes[a] + 1), dst_ref=lands[a], **sems)
            out.append((copy, copy))
        return out
    peers = [((*chip, c), 2 * chip[0] + chip[1]) for chip in chips] + ([((x, y, 1 - c), me)] if gather else [])
    out = []
    for a in range(len(srcs)):
        for j, (dev, k) in enumerate(peers):
            src = srcs[a] if gather else srcs[a].at[k]
            sems = dict(send_sem=send.at[PEERS * a + j], recv_sem=recv.at[PEERS * a + j], device_id=dev, device_id_type=MESH)
            out.append((pltpu.make_async_remote_copy(src_ref=src, dst_ref=lands[a].at[me], **sems),
                        pltpu.make_async_remote_copy(src_ref=src, dst_ref=lands[a].at[k], **sems)))
    return out


def _split_start(srcs, gather, after, name, axes=None):
    n = len(srcs)
    if axes is not None:
        lands = [lax.empty(_half_shape(s.shape, axes[a] + 1), s.dtype) for a, s in enumerate(srcs)]
    else:
        lands = [lax.empty(((N_CHIPS,) + s.shape) if gather else s.shape, s.dtype) for s in srcs]

    def body(*refs):
        send, recv = refs[2 * n + 1], refs[2 * n + 2]
        for start, _ in _split_copies(refs[:n], refs[n:2 * n], send, recv, gather, axes):
            start.start()
        refs[-1][...] = jnp.zeros_like(refs[-1])

    sems = pltpu.SemaphoreType.DMA((PEERS * n,))
    hbm = lambda a: pltpu.with_memory_space_constraint(a, pltpu.HBM)
    out = pl.pallas_call(
        body, name=name,
        out_shape=(sems, sems, *[pltpu.HBM(a.shape, a.dtype) for a in srcs + lands], jax.ShapeDtypeStruct((SUB, LANE), f32)),
        in_specs=[HBM] * (2 * n) + [ANY], out_specs=(SEM, SEM, *[HBM] * (2 * n), pl.BlockSpec(memory_space=pltpu.VMEM)),
        input_output_aliases={i: 2 + i for i in range(2 * n)},
        compiler_params=pltpu.CompilerParams(has_side_effects=EFFECT),
    )(*[hbm(a) for a in srcs + lands], after)
    return out[0], out[1], list(out[2:2 + n]), list(out[2 + n:2 + 2 * n]), out[-1]


def _split_wait(send, recv, srcs, lands, gather, after, name, axes=None):
    n = len(srcs)

    def body(*refs):
        for start, arrival in _split_copies(refs[:n], refs[n:2 * n], refs[2 * n], refs[2 * n + 1], gather, axes):
            start.wait_send()
            arrival.wait_recv()

    out = pl.pallas_call(
        body, name=name, out_shape=[pltpu.HBM(a.shape, a.dtype) for a in srcs + lands],
        in_specs=[HBM] * (2 * n) + [SEM, SEM, ANY], out_specs=[HBM] * (2 * n),
        input_output_aliases={i: i for i in range(2 * n)},
        compiler_params=pltpu.CompilerParams(has_side_effects=EFFECT),
    )(*srcs, *lands, send, recv, after)
    return list(out[:n]), list(out[n:])


N_DEV = 8


def _all_reduce_small(v, name):
    R = v.shape[0]

    def body(v_ref, o_ref, land_ref, send, recv):
        x, y, c, _ = _place()
        me = 4 * x + 2 * y + c
        land_ref[me] = v_ref[...]
        cps = []
        for m in range(1, N_DEV):
            px, py, pc = [(1 - q) if (m >> s) & 1 else q for q, s in ((x, 2), (y, 1), (c, 0))]
            cps.append((pltpu.make_async_remote_copy(src_ref=v_ref, dst_ref=land_ref.at[me], send_sem=send.at[m - 1],
                                                     recv_sem=recv.at[m - 1], device_id=(px, py, pc), device_id_type=MESH),
                        4 * px + 2 * py + pc, m))
        for cp, *_ in cps:
            cp.start()
        for cp, peer, m in cps:
            pltpu.make_async_remote_copy(src_ref=v_ref, dst_ref=land_ref.at[peer], send_sem=send.at[m - 1],
                                         recv_sem=recv.at[m - 1], device_id=(x, y, c), device_id_type=MESH).wait_recv()
        for cp, *_ in cps:
            cp.wait_send()
        acc = land_ref[0]
        for k in range(1, N_DEV):
            acc = acc + land_ref[k]
        o_ref[...] = acc

    vm = pl.BlockSpec(memory_space=pltpu.VMEM)
    return pl.pallas_call(
        body, name=name, in_specs=[vm], out_specs=vm, out_shape=jax.ShapeDtypeStruct(v.shape, f32),
        scratch_shapes=[pltpu.VMEM((N_DEV, R, LANE), f32), pltpu.SemaphoreType.DMA((N_DEV - 1,)),
                        pltpu.SemaphoreType.DMA((N_DEV - 1,))],
        compiler_params=pltpu.CompilerParams(vmem_limit_bytes=VMEM_LIMIT),
    )(v)


def _pack_small(arrs, mult=2 * SUB):
    flat = jnp.concatenate([a.reshape(-1) for a in arrs])
    rows = -(-flat.shape[0] // (LANE * mult)) * mult
    return jnp.pad(flat, (0, rows * LANE - flat.shape[0])).reshape(rows, LANE)


def _unpack_small(vec, shapes):
    flat, out, o = vec.reshape(-1), [], 0
    for s in shapes:
        n = int(np.prod(s))
        out.append(flat[o:o + n].reshape(s))
        o += n
    return out


REPL_SMALL = ("c_ctx", "b_mod", "q_norm", "k_norm", "c_norm", "d_conv_b", "d_norm_g", "d_norm_b", "ln_g", "ln_b")
SHARD_SMALL = ("b_conv", "c_gate_w2", "c_gate_b", "d_conv_w")
BIG = ("w_mod", "w_in", "w_br", "w_out")
ORDER = ("c_ctx", "w_mod", "b_mod", "w_in", "q_norm", "k_norm", "b_conv", "c_gate_w2", "c_gate_b", "c_norm", "d_conv_w",
         "d_conv_b", "d_norm_g", "d_norm_b", "w_br", "w_out", "ln_g", "ln_b")


def _unshard_last(g4, shard_shape):
    g = g4.reshape((N_CHIPS,) + tuple(shard_shape))
    g = jnp.moveaxis(g, 0, -2)
    return g.reshape(tuple(shard_shape[:-1]) + (N_CHIPS * shard_shape[-1],))


def _pieces_last(full):
    w = full.shape[-1] // N_CHIPS
    g = full.reshape(full.shape[:-1] + (N_CHIPS, w))
    return jnp.moveaxis(g, -2, 0).reshape(N_CHIPS, -1, w)


def kernel(x, c, ctx, c_ctx, w_mod, b_mod, w_in, q_norm, k_norm, b_conv, c_gate_w2, c_gate_b, c_norm, d_conv_w, d_conv_b, d_norm_g, d_norm_b, w_br, w_out, ln_g, ln_b, loss_target, m_c_ctx, m_w_mod, m_b_mod, m_w_in, m_q_norm, m_k_norm, m_b_conv, m_c_gate_w2, m_c_gate_b, m_c_norm, m_d_conv_w, m_d_conv_b, m_d_norm_g, m_d_norm_b, m_w_br, m_w_out, m_ln_g, m_ln_b, v_c_ctx, v_w_mod, v_b_mod, v_w_in, v_q_norm, v_k_norm, v_b_conv, v_c_gate_w2, v_c_gate_b, v_c_norm, v_d_conv_w, v_d_conv_b, v_d_norm_g, v_d_norm_b, v_w_br, v_w_out, v_ln_g, v_ln_b):
    W = dict(c_ctx=c_ctx, w_mod=w_mod, b_mod=b_mod, w_in=w_in, q_norm=q_norm, k_norm=k_norm, b_conv=b_conv,
             c_gate_w2=c_gate_w2, c_gate_b=c_gate_b, c_norm=c_norm, d_conv_w=d_conv_w, d_conv_b=d_conv_b,
             d_norm_g=d_norm_g, d_norm_b=d_norm_b, w_br=w_br, w_out=w_out, ln_g=ln_g, ln_b=ln_b)
    M = dict(c_ctx=m_c_ctx, w_mod=m_w_mod, b_mod=m_b_mod, w_in=m_w_in, q_norm=m_q_norm, k_norm=m_k_norm, b_conv=m_b_conv,
             c_gate_w2=m_c_gate_w2, c_gate_b=m_c_gate_b, c_norm=m_c_norm, d_conv_w=m_d_conv_w, d_conv_b=m_d_conv_b,
             d_norm_g=m_d_norm_g, d_norm_b=m_d_norm_b, w_br=m_w_br, w_out=m_w_out, ln_g=m_ln_g, ln_b=m_ln_b)
    V = dict(c_ctx=v_c_ctx, w_mod=v_w_mod, b_mod=v_b_mod, w_in=v_w_in, q_norm=v_q_norm, k_norm=v_k_norm, b_conv=v_b_conv,
             c_gate_w2=v_c_gate_w2, c_gate_b=v_c_gate_b, c_norm=v_c_norm, d_conv_w=v_d_conv_w, d_conv_b=v_d_conv_b,
             d_norm_g=v_d_norm_g, d_norm_b=v_d_norm_b, w_br=v_w_br, w_out=v_w_out, ln_g=v_ln_g, ln_b=v_ln_b)
    chip = 2 * lax.axis_index("x") + lax.axis_index("y")
    cidx = lax.axis_index("c").astype(jnp.int32).reshape(1)

    place = jnp.stack([chip, lax.axis_index("c")]).astype(jnp.int32)

    AXIS = dict(w_in=1, w_mod=0, w_br=0, w_out=0)
    ex = dict(w_in=lambda a: jnp.swapaxes(a, 1, 2), w_mod=lambda a: a.reshape(1, DEPTH * D, -1),
              w_br=lambda a: a.reshape(DEPTH, 4 * BRW, -1), w_out=lambda a: a)
    Wx, Mx, Vx = ({k: ex[k](P_[k]) for k in BIG} for P_ in (W, M, V))

    LAYER, MERGE = ("w_in", "w_br", "w_out"), ("w_br", "w_out")
    small_shard = _pack_small([W[k] for k in SHARD_SMALL])
    keys0 = ("w_in", "w_mod")
    sent = lambda k, l: (w_mod[l] if k == "w_mod" else Wx[k][l]).astype(bf16)
    got = _all_gather([sent(k, 0) for k in keys0] + [small_shard], [AXIS[k] for k in keys0] + [0], "all_gather0")
    smalls = [_unpack_small(got[-1][s], [W[k].shape for k in SHARD_SMALL]) for s in range(N_CHIPS)]
    full = {k: jnp.concatenate([smalls[s][i] for s in range(N_CHIPS)], axis=-1) for i, k in enumerate(SHARD_SMALL)}
    ag0b = _split_start([sent(k, 0) for k in MERGE], True, got[0], "all_gather0b_start")
    ag1 = _split_start([sent(k, 1) for k in LAYER + ("w_mod",)], True, ag0b[4], "all_gather1_start")

    def merge_form(w_br4, w_out4):
        return jnp.moveaxis(w_br4.reshape(N_CHIPS, 4, BRW, D // N_CHIPS), 0, 2).reshape(4, BRW, D), w_out4.reshape(D, D)

    def weights_of(l, h):
        if l == 0:
            return (_group_weights(got[0]),
                    lambda after: merge_form(*_split_wait(*ag0b[:4], True, after, "all_gather0b_wait")[1]), got[1])
        g4 = _split_wait(*ag1[:4], True, h, "all_gather1_wait")[1]
        return _group_weights(g4[0]), lambda after: merge_form(g4[1], g4[2]), g4[3]

    red = {k: Wx[k].shape for k in BIG}
    flights, held = {}, {}

    def launch(tag, l, pieces, after=None):
        keys = list(pieces)
        land_a = _sibling_halves([pieces[k] for k in keys], [AXIS[k] for k in keys], f"rs_sibling_halves{tag}")
        pair = [_add_half(pieces[k], la, cidx, AXIS[k], f"rs_pair_sum{tag}_{k}") for k, la in zip(keys, land_a)]
        after = jnp.zeros((SUB, LANE), f32) if after is None else after
        flights[tag] = (l, keys, _split_start(pair, False, after, f"rs_chip_exchange{tag}_start"))
        return flights[tag][2][4]

    def land(tag, after):
        l, keys, flight = flights.pop(tag)
        pair, land_b = _split_wait(*flight[:4], False, after, f"rs_chip_exchange{tag}_wait")
        for k, lb, pr in zip(keys, land_b, pair):
            red[k] = _sum_chips(lb, pr, place, AXIS[k], l, red[k], f"rs_chip_sum{tag}_{k}")

    def grads_done(l, gl):
        if "wp" in gl:
            pieces = dict(w_in=_ungroup(gl["wp"]).reshape(N_CHIPS, SHARD, D))
            return launch("0c", 0, pieces) if l == 0 else launch("1", 1, {**pieces, **held.pop(1)})
        pieces = dict(w_br=gl["w_br"].reshape(N_CHIPS, 4 * BRW, D // N_CHIPS), w_out=gl["w_out"].reshape(N_CHIPS, D // N_CHIPS, D))
        if l == 0:
            return launch("0b", 0, pieces)
        held[1] = pieces
        return None

    loss, gx, g = _local_step(
        x[0], c, ctx[0], loss_target[0], c_ctx, b_mod, weights_of, q_norm, k_norm, full["b_conv"],
        full["c_gate_w2"], full["c_gate_b"], c_norm, full["d_conv_w"], d_conv_b, d_norm_g, d_norm_b,
        grads_done, ln_g, ln_b, tm=256, token=ag1[4])
    g["c_gate_w2"], g["c_gate_b"] = g.pop("w2"), g.pop("gb")
    loss = lax.psum(loss, ("x", "y", "c"))

    w_mod_pieces = g["w_mod"].reshape(N_CHIPS, DEPTH * D, 3 * D // N_CHIPS)
    g = {k: (jnp.stack(v) if isinstance(v, list) else v) for k, v in g.items() if k not in ("wp", "w_br", "w_out", "w_mod")}

    small_names = REPL_SMALL + SHARD_SMALL
    gs = _all_reduce_small(_pack_small([g[k] for k in small_names]), "all_reduce_small")
    gsm = dict(zip(small_names, _unpack_small(gs, [g[k].shape for k in small_names])))
    for k in SHARD_SMALL:
        wdt = W[k].shape[-1]
        gsm[k] = lax.dynamic_slice_in_dim(gsm[k], chip * wdt, wdt, axis=gsm[k].ndim - 1)

    grad, delta, new_m, new_v = {}, {}, {}, {}

    def adamw_big(keys, after):
        filled = _sibling_fill([red[k] for k in keys], [AXIS[k] for k in keys], "rs_sibling_fill_" + keys[0])
        for k, r in zip(keys, filled):
            back = (lambda a: jnp.swapaxes(a, 1, 2)) if k == "w_in" else (lambda a: a.reshape(W[k].shape))
            g_, d_, m_, v_ = _adamw(Wx[k], r, Mx[k], Vx[k], f"adamw_{k}", after=after)
            grad[k], delta[k], new_m[k], new_v[k] = back(g_), back(d_), back(m_), back(v_)
        return d_

    token = launch("0d", 0, {"w_mod": w_mod_pieces}, after=gs)
    land("1", gx)
    land("0b", gx)
    last = adamw_big(MERGE, token)
    shapes = [W[k].shape for k in small_names]
    _, d_, m_, v_ = _adamw(*[_pack_small([P_[k] for k in small_names])[None] for P_ in (W, gsm, M, V)], "adamw_small", after=last)
    for k, dd, mm_, vv in zip(small_names, _unpack_small(d_, shapes), _unpack_small(m_, shapes), _unpack_small(v_, shapes)):
        grad[k], delta[k], new_m[k], new_v[k] = gsm[k], dd, mm_, vv
    land("0c", d_)
    land("0d", d_)
    adamw_big(("w_in", "w_mod"), None)

    return (loss, gx[None], *[grad[k] for k in ORDER], *[delta[k] for k in ORDER], *[new_m[k] for k in ORDER],
            *[new_v[k] for k in ORDER])
```

```python
import functools

import jax
import jax.numpy as jnp
import numpy as np
from jax import lax
from jax.experimental import pallas as pl
from jax.experimental.pallas import tpu as pltpu

f32 = jnp.float32
bf16 = jnp.bfloat16

D = 1024
DEPTH = 2
GRID_W = 64
BRW = 512
HD = 128
A_HEADS = 4
C_HEADS = 4
C_KW = 256
C_RANK = 16
C_TAU = 16.0
CH = 128
KB = 3
KD = 31
ALPHA = (2 * DEPTH) ** 0.25
EPS = 1e-6
ROPE_THETA = 10000.0
N_IN = 10784
LR, B1, B2, AEPS, WD, STEP = 0.001, 0.9, 0.999, 1e-08, 0.01, 10

W_M, W_A, W_C, W_G = 4 * D + 4 * BRW, 1024, 5 * BRW, 1152
GROUPS = ("M", "A", "C", "G")
GROUP_W = dict(M=W_M, A=W_A, C=W_C, G=W_G)
M_GA, M_GB, M_GC, M_GD = 4 * D, 4 * D + BRW, 4 * D + 2 * BRW, 4 * D + 3 * BRW
A_K, A_V = 512, 768
G_K, G_V, G_R = 256, 512, 1024
CT = 5 * 128
S_Q, S_GA, S_B, S_C, S_X, S_GB, S_CQ, S_CV, S_GC, S_R, S_DA, S_DG, S_GD, S_MG = (
    0, 1024, 1536, 2048, 2560, 3072, 3584, 4096, 4608, 5120, 5152, 5664, 6176, 6688)

LANE = 128
SUB = 8
VMEM_LIMIT = 56 * 1024 * 1024
CONV_PAD = 16
GLA_SUB = 16
GLA_CLAMP = 60.0


def _cparams(sem, vmem=VMEM_LIMIT):
    return pltpu.CompilerParams(dimension_semantics=sem, vmem_limit_bytes=vmem)


def _dg(a, b, ca, cb):
    return lax.dot_general(a.astype(bf16), b.astype(bf16), (((ca,), (cb,)), ((), ())),
                           preferred_element_type=f32)


@jax.custom_vjp
def mm(a, b):
    return _dg(a, b, 1, 0)


mm.defvjp(lambda a, b: (_dg(a, b, 1, 0), (a, b)),
          lambda r, ct: (_dg(ct, r[1], 1, 1).astype(r[0].dtype), _dg(r[0], ct, 0, 0).astype(r[1].dtype)))


@jax.custom_vjp
def mm_nt(a, b):
    return _dg(a, b, 1, 1)


mm_nt.defvjp(lambda a, b: (_dg(a, b, 1, 1), (a, b)),
             lambda r, ct: (_dg(ct, r[1], 1, 0).astype(r[0].dtype), _dg(ct, r[0], 0, 0).astype(r[1].dtype)))


@jax.custom_vjp
def mm_tn(a, b):
    return _dg(a, b, 0, 0)


mm_tn.defvjp(lambda a, b: (_dg(a, b, 0, 0), (a, b)),
             lambda r, ct: (_dg(r[1], ct, 1, 1).astype(r[0].dtype), _dg(r[0], ct, 1, 0).astype(r[1].dtype)))


def _sigmoid(x):
    return 0.5 * jnp.tanh(0.5 * x) + 0.5


def _silu(x):
    return x * _sigmoid(x)


def _ln(x):
    mu = jnp.mean(x, -1, keepdims=True)
    xc = x - mu
    var = jnp.mean(xc * xc, -1, keepdims=True)
    return xc * lax.rsqrt(var + EPS)


def _rms(x, g):
    return x * lax.rsqrt(jnp.mean(x * x, -1, keepdims=True) + EPS) * g


@jax.custom_vjp
def _rope(x, cos_f, sin_a, sin_b):
    return x * cos_f + pltpu.roll(x, HD - 1, 1) * sin_a + pltpu.roll(x, 1, 1) * sin_b


def _rope_fwd(x, cos_f, sin_a, sin_b):
    return _rope(x, cos_f, sin_a, sin_b), (cos_f, sin_a, sin_b)


def _rope_bwd(r, ct):
    cos_f, sin_a, sin_b = r
    dx = ct * cos_f + pltpu.roll(ct * sin_a, 1, 1) + pltpu.roll(ct * sin_b, HD - 1, 1)
    return dx, jnp.zeros_like(cos_f), jnp.zeros_like(sin_a), jnp.zeros_like(sin_b)


_rope.defvjp(_rope_fwd, _rope_bwd)


def _row_ids(i, tm):
    return i * tm + lax.broadcasted_iota(jnp.int32, (tm, 1), 0)


def _partial_rows(ref, rows):
    n = len(rows)
    for k, r in enumerate(rows):
        ref[k:k + 1, :] = r
    ref[n:SUB, :] = jnp.zeros((SUB - n, ref.shape[-1]), f32)


def _matmul(a, b, mode, tm, tn, tk, name, out_dtype=f32, add=None, after=None):
    sect = a.ndim == 3
    a2 = (a.shape[1], a.shape[0] * a.shape[2]) if sect else a.shape
    if mode == "nn":
        (M, K), N = a2, b.shape[1]
        a_spec = pl.BlockSpec((None, tm, tk), lambda j, i, k: (k, i, 0)) if sect else pl.BlockSpec((tm, tk), lambda j, i, k: (i, k))
        b_spec = pl.BlockSpec((tk, tn), lambda j, i, k: (k, j))
        ca, cb = 1, 0
        assert not sect or tk == a.shape[2]
    elif mode == "nt":
        (M, K), N = a2, b.shape[0]
        assert not sect
        a_spec = pl.BlockSpec((tm, tk), lambda j, i, k: (i, k))
        b_spec = pl.BlockSpec((tn, tk), lambda j, i, k: (j, k))
        ca, cb = 1, 1
    else:
        (K, M), N = a2, b.shape[1]
        a_spec = pl.BlockSpec((None, tk, tm), lambda j, i, k: (i, k, 0)) if sect else pl.BlockSpec((tk, tm), lambda j, i, k: (k, i))
        b_spec = pl.BlockSpec((tk, tn), lambda j, i, k: (k, j))
        ca, cb = 0, 0
        assert not sect or tm == a.shape[2]
    assert M % tm == 0 and N % tn == 0 and K % tk == 0, (name, M, N, K, tm, tn, tk)
    nk = K // tk

    o_spec = pl.BlockSpec((tm, tn), lambda j, i, k: (i, j))

    def body(a_ref, b_ref, *rest):
        add_ref = rest[0] if add is not None else None
        o_ref, acc_ref = rest[-2:]
        k = pl.program_id(2)
        part = _dg(a_ref[...], b_ref[...], ca, cb)

        @pl.when(k == 0)
        def _():
            acc_ref[...] = part if add_ref is None else part + add_ref[...]

        @pl.when(k > 0)
        def _():
            acc_ref[...] += part

        @pl.when(k == nk - 1)
        def _():
            o_ref[...] = acc_ref[...].astype(o_ref.dtype)

    extra = ([] if add is None else [(o_spec, add)]) + ([] if after is None else [(pl.BlockSpec(memory_space=pl.ANY), after)])
    return pl.pallas_call(
        body, name=name, grid=(N // tn, M // tm, nk),
        in_specs=[a_spec, b_spec] + [s_ for s_, _ in extra], out_specs=o_spec,
        out_shape=jax.ShapeDtypeStruct((M, N), out_dtype),
        scratch_shapes=[pltpu.VMEM((tm, tn), f32)],
        compiler_params=_cparams(("parallel", "parallel", "arbitrary")),
    )(a, b, *[v_ for _, v_ in extra])


def _matmul_groups(a, b, tks, tm, name, after=None):
    keys = list(a)
    M = a[keys[0]].shape[-2]
    N = b[keys[0]].shape[1]
    count = {g: b[g].shape[0] // tks[g] for g in keys}
    first, total = {}, 0
    for g in keys:
        first[g], total = total, total + count[g]

    def k_of(g):
        return lambda s: jnp.clip(s - first[g], 0, count[g] - 1)

    a_specs = [pl.BlockSpec((None, tm, tks[g]), functools.partial(lambda i, s, kk: (kk(s), i, 0), kk=k_of(g)))
               if a[g].ndim == 3 else pl.BlockSpec((tm, tks[g]), functools.partial(lambda i, s, kk: (i, kk(s)), kk=k_of(g)))
               for g in keys]
    b_specs = [pl.BlockSpec((tks[g], N), functools.partial(lambda i, s, kk: (kk(s), 0), kk=k_of(g))) for g in keys]
    n = len(keys)

    def body(*refs):
        o_ref, acc_ref = refs[-2:]
        s = pl.program_id(1)

        @pl.when(s == 0)
        def _():
            acc_ref[...] = jnp.zeros_like(acc_ref)

        for j, g in enumerate(keys):
            @pl.when((s >= first[g]) & (s < first[g] + count[g]))
            def _(j=j):
                acc_ref[...] += _dg(refs[j][...], refs[n + j][...], 1, 0)

        @pl.when(s == total - 1)
        def _():
            o_ref[...] = acc_ref[...]

    extra = [] if after is None else [after]
    return pl.pallas_call(
        body, name=name, grid=(M // tm, total),
        in_specs=a_specs + b_specs + [pl.BlockSpec(memory_space=pl.ANY)] * len(extra),
        out_specs=pl.BlockSpec((tm, N), lambda i, s: (i, 0)),
        out_shape=jax.ShapeDtypeStruct((M, N), f32),
        scratch_shapes=[pltpu.VMEM((tm, N), f32)],
        compiler_params=_cparams(("parallel", "arbitrary")),
    )(*[a[g] for g in keys], *[b[g] for g in keys], *extra)


def _matmul_tn_batched(a, b, ns, name):
    B, K, M = a.shape
    N = b.shape[2] // ns

    def body(a_ref, b_ref, o_ref):
        o_ref[...] = _dg(a_ref[...], b_ref[...], 0, 0).astype(bf16)

    return pl.pallas_call(
        body, name=name, grid=(B, ns),
        in_specs=[pl.BlockSpec((None, K, M), lambda i, s: (i, 0, 0)), pl.BlockSpec((None, K, N), lambda i, s: (i, 0, s))],
        out_specs=pl.BlockSpec((None, None, M, N), lambda i, s: (s, i, 0, 0)),
        out_shape=jax.ShapeDtypeStruct((ns, B, M, N), bf16),
        compiler_params=_cparams(("parallel", "parallel")),
    )(a, b)


MOD_TN = 768


def _mod_fwd(cin, w_mod_l, b_mod_l, name):
    def body(c_ref, w_ref, b_ref, o_ref):
        o_ref[...] = mm(_silu(c_ref[...]), w_ref[...]) + b_ref[...]

    return pl.pallas_call(
        body, name=name, grid=(3 * D // MOD_TN,),
        in_specs=[pl.BlockSpec((SUB, D), lambda j: (0, 0)), pl.BlockSpec((None, D, MOD_TN), lambda j: (j, 0, 0)),
                  pl.BlockSpec((1, MOD_TN), lambda j: (0, j))],
        out_specs=pl.BlockSpec((SUB, MOD_TN), lambda j: (0, j)),
        out_shape=jax.ShapeDtypeStruct((SUB, 3 * D), f32),
        compiler_params=_cparams(("parallel",)),
    )(cin, w_mod_l, b_mod_l[None, :])


def _mod_bwd(cin, w_mod, dmodv):
    nj = 3 * D // MOD_TN

    def body(c_ref, w_ref, g_ref, dw_ref, dc_ref):
        _, vjp = jax.vjp(lambda c, w: mm(_silu(c), w), c_ref[...], w_ref[...].astype(f32))
        dc, dw = vjp(g_ref[...])
        dw_ref[...] = dw.astype(bf16)
        dc_ref[...] = dc

    return pl.pallas_call(
        body, name="mod_bwd", grid=(DEPTH, nj),
        in_specs=[pl.BlockSpec((SUB, D), lambda l, j: (0, 0)),
                  pl.BlockSpec((None, None, D, MOD_TN), lambda l, j: (j, l, 0, 0)),
                  pl.BlockSpec((None, SUB, MOD_TN), lambda l, j: (l, 0, j))],
        out_specs=[pl.BlockSpec((None, None, D, MOD_TN), lambda l, j: (j, l, 0, 0)),
                   pl.BlockSpec((None, None, SUB, D), lambda l, j: (l, j, 0, 0))],
        out_shape=[jax.ShapeDtypeStruct((nj, DEPTH, D, MOD_TN), bf16),
                   jax.ShapeDtypeStruct((DEPTH, nj, SUB, D), f32)],
        compiler_params=_cparams(("parallel", "parallel")),
    )(cin, w_mod, dmodv)


def _u_fn(h, m_l, m_c, isctx):
    n = _ln(h)
    shift = jnp.where(isctx, m_c[:, 0:D], m_l[:, 0:D])
    scale = jnp.where(isctx, m_c[:, D:2 * D], m_l[:, D:2 * D])
    return n * (1.0 + scale) + shift


def _ln_fwd(h, modv_l, tc, tm, name):
    T = h.shape[0]

    def body(h_ref, m_ref, u_ref):
        isctx = _row_ids(pl.program_id(0), tm) < tc
        u_ref[...] = _u_fn(h_ref[...], m_ref[0:1, :], m_ref[1:2, :], isctx).astype(bf16)

    return pl.pallas_call(
        body, name=name, grid=(T // tm,),
        in_specs=[pl.BlockSpec((tm, D), lambda i: (i, 0)), pl.BlockSpec((SUB, 3 * D), lambda i: (0, 0))],
        out_specs=pl.BlockSpec((tm, D), lambda i: (i, 0)),
        out_shape=jax.ShapeDtypeStruct((T, D), bf16),
        compiler_params=_cparams(("parallel",)),
    )(h, modv_l)


def _ln_bwd(du, h, dh_res, modv_l, tc, tm, name, latent_only=False):
    T = h.shape[0]
    nt, nct = T // tm, tc // tm

    def body(du_ref, h_ref, r_ref, m_ref, dh_ref, dm_ref):
        isctx = _row_ids(pl.program_id(0), tm) < tc
        _, vjp = jax.vjp(lambda h, ml, mc: _u_fn(h, ml, mc, isctx), h_ref[...], m_ref[0:1, :], m_ref[1:2, :])
        dh, dml, dmc = vjp(du_ref[...])
        dh_ref[...] = dh + r_ref[...]
        _partial_rows(dm_ref, [dml, dmc])

    dh_map = (lambda i: (jnp.maximum(i - nct, 0), 0)) if latent_only else (lambda i: (i, 0))
    return pl.pallas_call(
        body, name=name, grid=(nt,),
        in_specs=[pl.BlockSpec((tm, D), lambda i: (i, 0)), pl.BlockSpec((tm, D), lambda i: (i, 0)),
                  pl.BlockSpec((tm, D), lambda i: (i, 0)), pl.BlockSpec((SUB, 3 * D), lambda i: (0, 0))],
        out_specs=[pl.BlockSpec((tm, D), dh_map), pl.BlockSpec((None, SUB, 3 * D), lambda i: (i, 0, 0))],
        out_shape=[jax.ShapeDtypeStruct((T - tc if latent_only else T, D), f32), jax.ShapeDtypeStruct((nt, SUB, 3 * D), f32)],
        compiler_params=_cparams(("arbitrary",)),
    )(du, h, dh_res, modv_l)


def _prep_fn(q, k, qg, kg, cos_f, sin_a, sin_b):
    qs = [_rope(_rms(q[:, HD * i:HD * (i + 1)], qg), cos_f, sin_a, sin_b) * (HD ** -0.5) for i in range(A_HEADS)]
    ks = [_rope(_rms(k[:, HD * i:HD * (i + 1)], kg), cos_f, sin_a, sin_b) for i in range(A_HEADS // 2)]
    return jnp.concatenate(qs, 1), jnp.concatenate(ks, 1)


def _tok(tm, w, off):
    return pl.BlockSpec((tm, w), lambda i: (i, off // w))


def _vec(w):
    return pl.BlockSpec((1, w), lambda i: (0, 0))


def _prep_fwd(P, qg, kg, rope, tm, name):
    T = P.shape[0]

    def body(q_ref, k_ref, v_ref, qg_ref, kg_ref, c_ref, sa_ref, sb_ref, qn_ref, kn_ref, vb_ref):
        qn, kn = _prep_fn(q_ref[...].astype(f32), k_ref[...].astype(f32), qg_ref[...], kg_ref[...], c_ref[...], sa_ref[...],
                          sb_ref[...])
        qn_ref[...] = qn.astype(bf16)
        kn_ref[...] = kn.astype(bf16)
        vb_ref[...] = v_ref[...].astype(bf16)

    return pl.pallas_call(
        body, name=name, grid=(T // tm,),
        in_specs=[_tok(tm, 512, 0), _tok(tm, 256, A_K), _tok(tm, 256, A_V), _vec(HD), _vec(HD),
                  _tok(tm, HD, 0), _tok(tm, HD, 0), _tok(tm, HD, 0)],
        out_specs=[_tok(tm, 512, 0), _tok(tm, 256, 0), _tok(tm, 256, 0)],
        out_shape=[jax.ShapeDtypeStruct((T, 512), bf16), jax.ShapeDtypeStruct((T, 256), bf16),
                   jax.ShapeDtypeStruct((T, 256), bf16)],
        compiler_params=_cparams(("parallel",)),
    )(P, P, P, qg, kg, *rope)


def _prep_bwd(P, dqn, dkn, dv, qg, kg, rope, tm, name):
    T = P.shape[0]
    nt = T // tm

    def body(q_ref, k_ref, dq_ref, dk_ref, dv_ref, qg_ref, kg_ref, c_ref, sa_ref, sb_ref, o_ref, og_ref):
        tabs = (c_ref[...], sa_ref[...], sb_ref[...])
        _, vjp = jax.vjp(lambda q, k, a, b: _prep_fn(q, k, a, b, *tabs), q_ref[...].astype(f32), k_ref[...].astype(f32),
                         qg_ref[...], kg_ref[...])
        dq, dk, dqg, dkg = vjp((dq_ref[...], dk_ref[...]))
        o_ref[:, 0:A_K] = dq.astype(bf16)
        o_ref[:, A_K:A_V] = dk.astype(bf16)
        o_ref[:, A_V:W_A] = dv_ref[...].astype(bf16)
        _partial_rows(og_ref, [dqg, dkg])

    return pl.pallas_call(
        body, name=name, grid=(nt,),
        in_specs=[_tok(tm, 512, 0), _tok(tm, 256, A_K), _tok(tm, 512, 0), _tok(tm, 256, 0), _tok(tm, 256, 0),
                  _vec(HD), _vec(HD), _tok(tm, HD, 0), _tok(tm, HD, 0), _tok(tm, HD, 0)],
        out_specs=[_tok(tm, W_A, 0), pl.BlockSpec((None, SUB, HD), lambda i: (i, 0, 0))],
        out_shape=[jax.ShapeDtypeStruct((T, W_A), bf16), jax.ShapeDtypeStruct((nt, SUB, HD), f32)],
        compiler_params=_cparams(("parallel",)),
    )(P, P, dqn, dkn, dv, qg, kg, *rope)


def _attn_fn(q, k, v, lim):
    col = lax.broadcasted_iota(jnp.int32, (1, k.shape[0]), 1)
    s = mm_nt(q, k) + jnp.where(col < lim, 0.0, -1e30)
    m = lax.stop_gradient(jnp.max(s, -1, keepdims=True))
    e = jnp.exp(s - m)
    p = e * (1.0 / jnp.sum(e, -1, keepdims=True))
    return mm(p, v)


def _attn_fwd(qn, kn, vb, tc, tq, name):
    T = qn.shape[0]

    def body(q_ref, k_ref, v_ref, o_ref):
        lim = jnp.where(pl.program_id(1) * tq < tc, tc, T)
        o_ref[...] = _attn_fn(q_ref[...], k_ref[...], v_ref[...], lim)

    return pl.pallas_call(
        body, name=name, grid=(A_HEADS, T // tq),
        in_specs=[pl.BlockSpec((tq, HD), lambda h, i: (i, h)), pl.BlockSpec((T, HD), lambda h, i: (0, h // 2)),
                  pl.BlockSpec((T, HD), lambda h, i: (0, h // 2))],
        out_specs=pl.BlockSpec((tq, HD), lambda h, i: (i, h)),
        out_shape=jax.ShapeDtypeStruct((T, 512), f32),
        compiler_params=_cparams(("parallel", "parallel")),
    )(qn, kn, vb)


def _attn_bwd(qn, kn, vb, dya, tc, tq, name):
    T = qn.shape[0]

    def body(q_ref, k_ref, v_ref, g_ref, dq_ref, dk_ref, dv_ref):
        first = (pl.program_id(1) == 0) & (pl.program_id(2) == 0)
        lim = jnp.where(pl.program_id(2) * tq < tc, tc, T)
        _, vjp = jax.vjp(lambda q, k, v: _attn_fn(q, k, v, lim), q_ref[...].astype(f32), k_ref[...].astype(f32),
                         v_ref[...].astype(f32))
        dq, dk, dv = vjp(g_ref[...])
        dq_ref[...] = dq

        @pl.when(first)
        def _():
            dk_ref[...] = dk
            dv_ref[...] = dv

        @pl.when(jnp.logical_not(first))
        def _():
            dk_ref[...] += dk
            dv_ref[...] += dv

    qspec = pl.BlockSpec((tq, HD), lambda kv, g, i: (i, 2 * kv + g))
    kspec = pl.BlockSpec((T, HD), lambda kv, g, i: (0, kv))
    return pl.pallas_call(
        body, name=name, grid=(A_HEADS // 2, 2, T // tq),
        in_specs=[qspec, kspec, kspec, qspec], out_specs=[qspec, kspec, kspec],
        out_shape=[jax.ShapeDtypeStruct((T, 512), f32), jax.ShapeDtypeStruct((T, 256), f32),
                   jax.ShapeDtypeStruct((T, 256), f32)],
        compiler_params=_cparams(("parallel", "arbitrary", "arbitrary")),
    )(qn, kn, vb, dya)


def _conv_rows(tc, tl):
    return CONV_PAD + tc + CONV_PAD + tl + CONV_PAD


def _fill_pad(pad_ref, val, tc, tl):
    z = jnp.zeros((CONV_PAD, LANE), f32)
    pad_ref[0:CONV_PAD, :] = z
    pad_ref[CONV_PAD:CONV_PAD + tc, :] = val[0:tc]
    pad_ref[CONV_PAD + tc:2 * CONV_PAD + tc, :] = z
    pad_ref[2 * CONV_PAD + tc:2 * CONV_PAD + tc + tl, :] = val[tc:tc + tl]
    pad_ref[2 * CONV_PAD + tc + tl:3 * CONV_PAD + tc + tl, :] = z


def _conv_apply(pad_ref, w_ref, K, tc, tl, rc, emit, flip=False):
    half = K // 2
    for seg0, off, n in ((0, CONV_PAD, tc), (tc, 2 * CONV_PAD + tc, tl)):
        for r0 in range(0, n, rc):
            acc = None
            for k in range(K):
                sh = (half - k) if flip else (k - half)
                term = pad_ref[pl.ds(off + r0 + sh, rc), :] * w_ref[k:k + 1, :]
                acc = term if acc is None else acc + term
            emit(seg0 + r0, acc)


def _conv_wgrad(pad_ref, dy_ref, K, tc, tl, rc, dw_ref):
    half = K // 2
    for k in range(K):
        acc = jnp.zeros((1, LANE), f32)
        for seg0, off, n in ((0, CONV_PAD, tc), (tc, 2 * CONV_PAD + tc, tl)):
            for r0 in range(0, n, rc):
                acc = acc + jnp.sum(pad_ref[pl.ds(off + r0 + k - half, rc), :] * dy_ref[pl.ds(seg0 + r0, rc), :],
                                    axis=0, keepdims=True)
        dw_ref[k:k + 1, :] = acc


def _col(T, off):
    return pl.BlockSpec((T, LANE), lambda j: (0, off // LANE + j))


C_B, C_C, C_X, C_A, C_G = range(5)
N_SEC = 5


class _Sections:
    def __init__(self, refs):
        self.refs = refs

    def __getitem__(self, idx):
        rows, sec = idx
        return self.refs[sec][rows, :].astype(f32)

    def __setitem__(self, idx, val):
        rows, sec = idx
        self.refs[sec, rows, :] = val


def _sec_specs(T):
    return [pl.BlockSpec((T, LANE), functools.partial(lambda j, s: (0, s * (BRW // LANE) + j), s=s)) for s in range(N_SEC)]


def _conv_fwd(P, wb, wd, bd, tc, tl, rc, name):
    T = tc + tl

    def body(*refs):
        p_ref = _Sections(refs[:N_SEC])
        wb_ref, wd_ref, bd_ref, yb_ref, hh_ref, pad_ref = refs[N_SEC:]
        _fill_pad(pad_ref, p_ref[:, C_C] * p_ref[:, C_X], tc, tl)

        def emit_b(r0, y):
            yb_ref[pl.ds(r0, rc), :] = y * p_ref[pl.ds(r0, rc), C_B]

        _conv_apply(pad_ref, wb_ref, KB, tc, tl, rc, emit_b)
        _fill_pad(pad_ref, p_ref[:, C_A] * _sigmoid(p_ref[:, C_G]), tc, tl)

        def emit_d(r0, y):
            hh_ref[pl.ds(r0, rc), :] = y + bd_ref[...]

        _conv_apply(pad_ref, wd_ref, KD, tc, tl, rc, emit_d)

    return pl.pallas_call(
        body, name=name, grid=(BRW // LANE,),
        in_specs=_sec_specs(T) + [pl.BlockSpec((KB, LANE), lambda j: (0, j)), pl.BlockSpec((KD, LANE), lambda j: (0, j)),
                                  pl.BlockSpec((1, LANE), lambda j: (0, j))],
        out_specs=[_col(T, 0), _col(T, 0)],
        out_shape=[jax.ShapeDtypeStruct((T, BRW), f32), jax.ShapeDtypeStruct((T, BRW), f32)],
        scratch_shapes=[pltpu.VMEM((_conv_rows(tc, tl), LANE), f32)],
        compiler_params=_cparams(("parallel",)),
    )(*[P] * N_SEC, wb, wd, bd)


def _conv_bwd(P, dyb, dhh, wb, wd, tc, tl, rc, name):
    T = tc + tl

    def body(*refs):
        p_ref = _Sections(refs[:N_SEC])
        dyb_ref, dhh_ref, wb_ref, wd_ref, dp3_ref, dwb_ref, dwd_ref, dbd_ref, pad_ref, pad2_ref, tmp_ref = refs[N_SEC:]
        dp_ref = _Sections(dp3_ref)
        _fill_pad(pad_ref, p_ref[:, C_C] * p_ref[:, C_X], tc, tl)

        def emit_cv(r0, y):
            dp_ref[pl.ds(r0, rc), C_B] = (y * dyb_ref[pl.ds(r0, rc), :]).astype(bf16)

        _conv_apply(pad_ref, wb_ref, KB, tc, tl, rc, emit_cv)
        tmp_ref[...] = dyb_ref[...] * p_ref[:, C_B]
        _conv_wgrad(pad_ref, tmp_ref, KB, tc, tl, rc, dwb_ref)
        _fill_pad(pad2_ref, tmp_ref[...], tc, tl)

        def emit_ds(r0, y):
            dp_ref[pl.ds(r0, rc), C_C] = (y * p_ref[pl.ds(r0, rc), C_X]).astype(bf16)
            dp_ref[pl.ds(r0, rc), C_X] = (y * p_ref[pl.ds(r0, rc), C_C]).astype(bf16)

        _conv_apply(pad2_ref, wb_ref, KB, tc, tl, rc, emit_ds, flip=True)
        _fill_pad(pad_ref, p_ref[:, C_A] * _sigmoid(p_ref[:, C_G]), tc, tl)
        _conv_wgrad(pad_ref, dhh_ref, KD, tc, tl, rc, dwd_ref)
        dbd_ref[...] = jnp.sum(dhh_ref[...], axis=0, keepdims=True)
        _fill_pad(pad2_ref, dhh_ref[...], tc, tl)

        def emit_d2(r0, y):
            sg = _sigmoid(p_ref[pl.ds(r0, rc), C_G])
            a = p_ref[pl.ds(r0, rc), C_A]
            dp_ref[pl.ds(r0, rc), C_A] = (y * sg).astype(bf16)
            dp_ref[pl.ds(r0, rc), C_G] = (y * a * sg * (1.0 - sg)).astype(bf16)

        _conv_apply(pad2_ref, wd_ref, KD, tc, tl, rc, emit_d2, flip=True)

    return pl.pallas_call(
        body, name=name, grid=(BRW // LANE,),
        in_specs=_sec_specs(T) + [_col(T, 0), _col(T, 0),
                                  pl.BlockSpec((KB, LANE), lambda j: (0, j)), pl.BlockSpec((KD, LANE), lambda j: (0, j))],
        out_specs=[pl.BlockSpec((N_SEC, T, LANE), lambda j: (0, 0, j)), pl.BlockSpec((KB, LANE), lambda j: (0, j)),
                   pl.BlockSpec((KD, LANE), lambda j: (0, j)), pl.BlockSpec((1, LANE), lambda j: (0, j))],
        out_shape=[jax.ShapeDtypeStruct((N_SEC, T, BRW), bf16), jax.ShapeDtypeStruct((KB, BRW), f32),
                   jax.ShapeDtypeStruct((KD, BRW), f32), jax.ShapeDtypeStruct((1, BRW), f32)],
        scratch_shapes=[pltpu.VMEM((_conv_rows(tc, tl), LANE), f32), pltpu.VMEM((_conv_rows(tc, tl), LANE), f32),
                        pltpu.VMEM((T, LANE), f32)],
        compiler_params=_cparams(("parallel",)),
    )(*[P] * N_SEC, dyb, dhh, wb, wd)


def _gla_chunk(q, k, v, r, w2, b2, st, isfwd):
    z = mm(r, w2) + b2
    g = jax.nn.log_sigmoid(z[:, 0:C_KW] if isfwd else z[:, C_KW:2 * C_KW]) / C_TAU
    ri = lax.broadcasted_iota(jnp.int32, (CH, CH), 0)
    ci = lax.broadcasted_iota(jnp.int32, (CH, CH), 1)
    tri = ((ci <= ri) if isfwd else (ci >= ri)).astype(f32)
    cum = jnp.dot(tri, g, preferred_element_type=f32, precision=lax.Precision.HIGHEST)
    last = jnp.sum(g, axis=0, keepdims=True)
    q = q * (C_KW // C_HEADS) ** -0.5
    hv = lax.broadcasted_iota(jnp.int32, (BRW, C_KW), 0) // (BRW // C_HEADS)
    hk = lax.broadcasted_iota(jnp.int32, (BRW, C_KW), 1) // (C_KW // C_HEADS)
    st_new = st * jnp.exp(last) + jnp.where(hv == hk, mm_tn(v, k * jnp.exp(last - cum)), 0.0)
    o = mm_nt(q * jnp.exp(cum), st)
    rowi = lax.broadcasted_iota(jnp.int32, (CH, C_KW), 0)
    srow = lax.broadcasted_iota(jnp.int32, (C_HEADS * CH, C_KW), 0)
    slane = lax.broadcasted_iota(jnp.int32, (C_HEADS * CH, C_KW), 1)
    own_lanes = srow // CH == slane // (C_KW // C_HEADS)
    pos = lax.broadcasted_iota(jnp.int32, (C_HEADS * CH, CH), 0) % CH
    key = lax.broadcasted_iota(jnp.int32, (C_HEADS * CH, CH), 1)
    scores = jnp.zeros((C_HEADS * CH, CH), f32)
    for a in range(CH // GLA_SUB):
        idx = GLA_SUB * a - 1 if isfwd else GLA_SUB * (a + 1)
        ref = jnp.sum(jnp.where(rowi == idx, cum, 0.0), axis=0, keepdims=True)
        qa = q * jnp.exp(jnp.minimum(cum - ref, 0.0))
        ka = k * jnp.exp(jnp.minimum(ref - cum, GLA_CLAMP))
        s = mm_nt(jnp.where(own_lanes, jnp.concatenate([qa] * C_HEADS, axis=0), 0.0), ka)
        scores = scores + jnp.where(pos // GLA_SUB == a, s, 0.0)
    scores = jnp.where((key <= pos) if isfwd else (key >= pos), scores, 0.0)
    vw = BRW // C_HEADS
    o = o + jnp.concatenate([mm(scores[CH * hd:CH * (hd + 1)], v[:, vw * hd:vw * (hd + 1)]) for hd in range(C_HEADS)],
                            axis=1)
    return o, st_new


def _gla_chunk_of(d, n, nc, nch):
    back = jnp.where(n < nc, nc - 1 - n, nch - 1 - (n - nc))
    return jnp.where(d == 0, n, back)


def _gla_fwd(P, w2, b2, tc, name):
    T = P.shape[0]
    nch, nc = T // CH, tc // CH

    back = lambda n: _gla_chunk_of(1, n, nc, nch)

    def body(pf_ref, pb_ref, w_ref, b_ref, of_ref, ob_ref, ssf_ref, ssb_ref, stf_ref, stb_ref):
        @pl.when(pl.program_id(0) == 0)
        def _():
            stf_ref[...] = jnp.zeros_like(stf_ref)
            stb_ref[...] = jnp.zeros_like(stb_ref)

        for p_ref, o_ref, ss_ref, st_ref, isfwd in ((pf_ref, of_ref, ssf_ref, stf_ref, True),
                                                    (pb_ref, ob_ref, ssb_ref, stb_ref, False)):
            st = st_ref[...]
            ss_ref[...] = st
            p = p_ref[...].astype(f32)
            o, st_new = _gla_chunk(p[:, 0:G_K], p[:, G_K:G_V], p[:, G_V:G_R], p[:, G_R:W_G], w_ref[...], b_ref[...], st, isfwd)
            o_ref[...] = o
            st_ref[...] = st_new

    sd = jax.ShapeDtypeStruct
    return pl.pallas_call(
        body, name=name, grid=(nch,),
        in_specs=[pl.BlockSpec((CH, W_G), lambda n: (n, 0)), pl.BlockSpec((CH, W_G), lambda n: (back(n), 0)),
                  pl.BlockSpec((LANE, 512), lambda n: (0, 0)), pl.BlockSpec((1, 512), lambda n: (0, 0))],
        out_specs=[pl.BlockSpec((CH, BRW), lambda n: (n, 0)), pl.BlockSpec((CH, BRW), lambda n: (back(n), 0)),
                   pl.BlockSpec((None, BRW, C_KW), lambda n: (n, 0, 0)), pl.BlockSpec((None, BRW, C_KW), lambda n: (n, 0, 0))],
        out_shape=[sd((T, BRW), f32), sd((T, BRW), f32), sd((nch, BRW, C_KW), f32), sd((nch, BRW, C_KW), f32)],
        scratch_shapes=[pltpu.VMEM((BRW, C_KW), f32), pltpu.VMEM((BRW, C_KW), f32)],
        compiler_params=_cparams(("arbitrary",)),
    )(P, P, w2, b2)


def _gla_bwd(P, w2, b2, ssave, doc, tc, name):
    T = P.shape[0]
    nch, nc = T // CH, tc // CH

    fwd_chunk = lambda m: nch - 1 - m
    back_chunk = lambda m: _gla_chunk_of(1, nch - 1 - m, nc, nch)

    def body(pf_ref, pb_ref, w_ref, b_ref, ssf_ref, ssb_ref, gf_ref, gb_ref, dpf_ref, dpb_ref, dw_ref, db_ref,
             dstf_ref, dstb_ref):
        m = pl.program_id(0)

        @pl.when(m == 0)
        def _():
            dstf_ref[...] = jnp.zeros_like(dstf_ref)
            dstb_ref[...] = jnp.zeros_like(dstb_ref)

        dw_sum, db_sum = None, None
        for p_ref, ss_ref, g_ref, dp_ref, dst_ref, isfwd in ((pf_ref, ssf_ref, gf_ref, dpf_ref, dstf_ref, True),
                                                             (pb_ref, ssb_ref, gb_ref, dpb_ref, dstb_ref, False)):
            p = p_ref[...].astype(f32)
            _, vjp = jax.vjp(lambda q, k, v, r, w, b, st: _gla_chunk(q, k, v, r, w, b, st, isfwd),
                             p[:, 0:G_K], p[:, G_K:G_V], p[:, G_V:G_R], p[:, G_R:W_G], w_ref[...], b_ref[...], ss_ref[...])
            dq, dk, dv, dr, dw, db, dst = vjp((g_ref[...], dst_ref[...]))
            dp_ref[:, 0:G_K] = dq
            dp_ref[:, G_K:G_V] = dk
            dp_ref[:, G_V:G_R] = dv
            dp_ref[:, G_R:W_G] = dr
            dst_ref[...] = dst
            dw_sum = dw if dw_sum is None else dw_sum + dw
            db_sum = db if db_sum is None else db_sum + db

        @pl.when(m == 0)
        def _():
            dw_ref[...] = dw_sum
            _partial_rows(db_ref, [db_sum])

        @pl.when(m > 0)
        def _():
            dw_ref[...] += dw_sum
            db_ref[0:1, :] += db_sum

    ssf, ssb = ssave
    chunk_f = lambda w: pl.BlockSpec((CH, w), lambda m: (fwd_chunk(m), 0))
    chunk_b = lambda w: pl.BlockSpec((CH, w), lambda m: (back_chunk(m), 0))
    state = pl.BlockSpec((None, BRW, C_KW), lambda m: (nch - 1 - m, 0, 0))
    sd = jax.ShapeDtypeStruct
    return pl.pallas_call(
        body, name=name, grid=(nch,),
        in_specs=[chunk_f(W_G), chunk_b(W_G), pl.BlockSpec((LANE, 512), lambda m: (0, 0)), pl.BlockSpec((1, 512), lambda m: (0, 0)),
                  state, state, chunk_f(BRW), chunk_b(BRW)],
        out_specs=[chunk_f(W_G), chunk_b(W_G), pl.BlockSpec((LANE, 512), lambda m: (0, 0)), pl.BlockSpec((SUB, 512), lambda m: (0, 0))],
        out_shape=[sd((T, W_G), f32), sd((T, W_G), f32), sd((LANE, 512), f32), sd((SUB, 512), f32)],
        scratch_shapes=[pltpu.VMEM((BRW, C_KW), f32), pltpu.VMEM((BRW, C_KW), f32)],
        compiler_params=_cparams(("arbitrary",)),
    )(P, P, w2, b2, ssf, ssb, doc, doc)


def _sum_dirs(a, b, tm, name):
    T, W = a.shape

    def body(a_ref, b_ref, o_ref):
        o_ref[...] = (a_ref[...] + b_ref[...]).astype(bf16)

    spec = pl.BlockSpec((tm, W), lambda i: (i, 0))
    return pl.pallas_call(
        body, name=name, grid=(T // tm,), in_specs=[spec, spec], out_specs=spec,
        out_shape=jax.ShapeDtypeStruct((T, W), bf16),
        compiler_params=_cparams(("parallel",)),
    )(a, b)


def _merge_fn(h, m_l, m_c, isctx, ya, ga, yb, gb, of, ob, gc, hh, gd, mg, es, ey, cn, dng, dnb, lg, lb, wbr, wout):
    oc = of + ob
    yc = jnp.concatenate([_rms(oc[:, HD * i:HD * (i + 1)], cn[:, HD * i:HD * (i + 1)]) for i in range(C_HEADS)], 1)
    brs = [ya * _silu(ga), yb * _silu(gb), yc * _silu(gc), _silu(_ln(hh) * dng + dnb) * _silu(gd)]
    acc = None
    for i in range(4):
        t = _sigmoid(mg[:, D * i:D * (i + 1)]) * (mm(brs[i], wbr[i]) + es[i])
        acc = t if acc is None else acc + t
    y = mm(acc, wout) + ey
    gate = jnp.where(isctx, m_c[:, 2 * D:3 * D], m_l[:, 2 * D:3 * D])
    hn = _ln(ALPHA * h + gate * y) * lg + lb
    return hn, (brs, acc)


def _merge_specs(tm):
    t = lambda w, off=0: _tok(tm, w, off)
    return [t(D), pl.BlockSpec((SUB, 3 * D), lambda i: (0, 0)),
            t(BRW), t(BRW, M_GA), t(BRW), t(BRW, M_GB),
            t(BRW), t(BRW),
            t(BRW, M_GC), t(BRW), t(BRW, M_GD), t(4 * D, 0),
            _vec(BRW), _vec(BRW), _vec(BRW), _vec(D), _vec(D),
            pl.BlockSpec((4, BRW, D), lambda i: (0, 0, 0)), pl.BlockSpec((D, D), lambda i: (0, 0))]


def _merge_fwd(h, modv_l, ya, yb, o2, hh, P, cn, dng, dnb, lg, lb, wbr, wout, tc, tm, name):
    T = h.shape[0]

    def body(h_ref, m_ref, ya_ref, ga_ref, yb_ref, gb_ref, of_ref, ob_ref, gc_ref, hh_ref, gd_ref, mg_ref,
             cn_ref, dng_ref, dnb_ref, lg_ref, lb_ref, wbr_ref, wout_ref, o_ref):
        isctx = _row_ids(pl.program_id(0), tm) < tc
        zero = jnp.zeros((tm, D), f32)
        up = lambda r: r[...].astype(f32)
        hn, _ = _merge_fn(h_ref[...], m_ref[0:1, :], m_ref[1:2, :], isctx, ya_ref[...], up(ga_ref), yb_ref[...],
                          up(gb_ref), of_ref[...], ob_ref[...], up(gc_ref), hh_ref[...], up(gd_ref), up(mg_ref),
                          [zero] * 4, zero, cn_ref[...], dng_ref[...], dnb_ref[...], lg_ref[...], lb_ref[...],
                          [wbr_ref[i] for i in range(4)], wout_ref[...])
        o_ref[...] = hn

    return pl.pallas_call(
        body, name=name, grid=(T // tm,),
        in_specs=_merge_specs(tm), out_specs=_tok(tm, D, 0),
        out_shape=jax.ShapeDtypeStruct((T, D), f32),
        compiler_params=_cparams(("parallel",)),
    )(h, modv_l, ya, P, yb, P, o2[0], o2[1], P, hh, P, P, cn, dng, dnb, lg, lb, wbr, wout)


def _merge_bwd(dhn, h, modv_l, ya, yb, o2, hh, P, cn, dng, dnb, lg, lb, wbr, wout, tc, tm, name):
    T = h.shape[0]
    nt = T // tm

    def body(g_ref, h_ref, m_ref, ya_ref, ga_ref, yb_ref, gb_ref, of_ref, ob_ref, gc_ref, hh_ref, gd_ref, mg_ref,
             cn_ref, dng_ref, dnb_ref, lg_ref, lb_ref, wbr_ref, wout_ref,
             dh_ref, dm_ref, dya_ref, dyb_ref, doc_ref, dhh_ref, dp_ref,
             br_ref, z_ref, acc_ref, dy_ref, dv5_ref, dvd_ref):
        isctx = _row_ids(pl.program_id(0), tm) < tc
        zero = jnp.zeros((tm, D), f32)
        wbr_v = [wbr_ref[i] for i in range(4)]
        wout_v = wout_ref[...]
        up = lambda r: r[...].astype(f32)

        def fn(h, ml, mc, ya, ga, yb, gb, oc, gc, hh, gd, mg, e0, e1, e2, e3, ey, cn, dng, dnb, lg, lb):
            return _merge_fn(h, ml, mc, isctx, ya, ga, yb, gb, oc, jnp.zeros_like(oc), gc, hh, gd, mg,
                             [e0, e1, e2, e3], ey, cn, dng, dnb, lg, lb, wbr_v, wout_v)

        _, vjp, (brs, acc) = jax.vjp(
            fn, h_ref[...], m_ref[0:1, :], m_ref[1:2, :], ya_ref[...], up(ga_ref), yb_ref[...], up(gb_ref),
            of_ref[...] + ob_ref[...], up(gc_ref), hh_ref[...], up(gd_ref), up(mg_ref), zero, zero, zero, zero, zero,
            cn_ref[...], dng_ref[...], dnb_ref[...], lg_ref[...], lb_ref[...], has_aux=True)
        (dh, dml, dmc, dya, dga, dyb, dgb, doc, dgc, dhh, dgd, dmg, z0, z1, z2, z3, dy,
         dcn, ddng, ddnb, dlg, dlb) = vjp(g_ref[...])
        dh_ref[...] = dh
        _partial_rows(dm_ref, [dml, dmc])
        dya_ref[...] = dya
        dyb_ref[...] = dyb
        doc_ref[...] = doc
        dhh_ref[...] = dhh
        dp_ref[:, 0:M_GA] = dmg.astype(bf16)
        dp_ref[:, M_GA:M_GB] = dga.astype(bf16)
        dp_ref[:, M_GB:M_GC] = dgb.astype(bf16)
        dp_ref[:, M_GC:M_GD] = dgc.astype(bf16)
        dp_ref[:, M_GD:W_M] = dgd.astype(bf16)
        for i, z in enumerate((z0, z1, z2, z3)):
            br_ref[i] = brs[i].astype(bf16)
            z_ref[i] = z.astype(bf16)
        acc_ref[...] = acc.astype(bf16)
        dy_ref[...] = dy.astype(bf16)
        _partial_rows(dv5_ref, [dcn, ddng, ddnb])
        _partial_rows(dvd_ref, [dlg, dlb])

    t = lambda w: _tok(tm, w, 0)
    part = lambda w: pl.BlockSpec((None, SUB, w), lambda i: (i, 0, 0))
    sd = jax.ShapeDtypeStruct
    return pl.pallas_call(
        body, name=name, grid=(nt,),
        in_specs=[t(D)] + _merge_specs(tm),
        out_specs=[t(D), part(3 * D)] + [t(BRW)] * 4 + [t(W_M),
                   pl.BlockSpec((4, tm, BRW), lambda i: (0, i, 0)), pl.BlockSpec((4, tm, D), lambda i: (0, i, 0)),
                   t(D), t(D), part(BRW), part(D)],
        out_shape=[sd((T, D), f32), sd((nt, SUB, 3 * D), f32)] + [sd((T, BRW), f32)] * 4 + [sd((T, W_M), bf16),
                   sd((4, T, BRW), bf16), sd((4, T, D), bf16), sd((T, D), bf16), sd((T, D), bf16),
                   sd((nt, SUB, BRW), f32), sd((nt, SUB, D), f32)],
        compiler_params=_cparams(("parallel",)),
    )(dhn, h, modv_l, ya, P, yb, P, o2[0], o2[1], P, hh, P, P, cn, dng, dnb, lg, lb, wbr, wout)


def _loss_kernel(h, tgt, tc, tm, name):
    T = h.shape[0]
    nt = T // tm
    nct = tc // tm

    def body(h_ref, t_ref, d_ref, l_ref):
        i = pl.program_id(0)
        err = h_ref[...] - t_ref[...]
        lat = (i >= nct).astype(f32)
        d_ref[...] = err * (lat / D)
        l_ref[...] = jnp.zeros((SUB, LANE), f32) + lat * 0.5 * jnp.sum(err * err) / D

    return pl.pallas_call(
        body, name=name, grid=(nt,),
        in_specs=[pl.BlockSpec((tm, D), lambda i: (i, 0)),
                  pl.BlockSpec((tm, D), lambda i: (jnp.maximum(i - nct, 0), 0))],
        out_specs=[pl.BlockSpec((tm, D), lambda i: (i, 0)), pl.BlockSpec((None, SUB, LANE), lambda i: (i, 0, 0))],
        out_shape=[jax.ShapeDtypeStruct((T, D), f32), jax.ShapeDtypeStruct((nt, SUB, LANE), f32)],
        compiler_params=_cparams(("parallel",)),
    )(h, tgt)


def _rope_tables(tc, tl):
    t = jnp.arange(tl)
    inv = ROPE_THETA ** (-jnp.arange(0, HD // 2, 2, dtype=f32) / (HD // 2))
    ang = jnp.concatenate([(t // GRID_W).astype(f32)[:, None] * inv, (t % GRID_W).astype(f32)[:, None] * inv], -1)
    cos, sin = jnp.repeat(jnp.cos(ang), 2, axis=1), jnp.repeat(jnp.sin(ang), 2, axis=1)
    even = (jnp.arange(HD) % 2 == 0)[None, :]
    cos_f = jnp.concatenate([jnp.ones((tc, HD), f32), cos], 0)
    sin_a = jnp.concatenate([jnp.zeros((tc, HD), f32), jnp.where(even, -sin, 0.0)], 0)
    sin_b = jnp.concatenate([jnp.zeros((tc, HD), f32), jnp.where(even, 0.0, sin)], 0)
    return cos_f, sin_a, sin_b


N_CHIPS = 4
SHARD = N_IN // N_CHIPS


def _group_ranges():
    return dict(M=[(S_MG, 4 * D), (S_GA, BRW), (S_GB, BRW), (S_GC, BRW), (S_GD, BRW)], A=[(S_Q, W_A)],
                C=[(S_B, 3 * BRW), (S_DA, 2 * BRW)], G=[(S_CQ, 2 * C_KW + BRW), (S_R, 2 * C_RANK)])


def _group_weights(w4):
    out = {}
    for k, ranges in _group_ranges().items():
        parts = []
        for a, n in ranges:
            while n > 0:
                s, r = divmod(a, SHARD)
                m = min(n, SHARD - r)
                parts.append(w4[s, r:r + m])
                a, n = a + m, n - m
        if k == "G":
            parts.append(jnp.zeros((LANE - 2 * C_RANK, D), w4.dtype))
        out[k] = jnp.concatenate(parts, 0)
    return out


def _ungroup(g):
    secs = []
    for k, ranges in _group_ranges().items():
        off = 0
        for a, n in ranges:
            secs.append((a, g[k][off:off + n]))
            off += n
    return jnp.concatenate([v for _, v in sorted(secs, key=lambda t: t[0])], 0)


PROJ_TN = dict(M=2048, A=1024, C=1280, G=1152)
DU_TK = dict(M=2048, A=1024, C=BRW, G=1152)
DWP_TN = dict(M=768, A=1024, C=BRW, G=1152)


def _gate_weights(w2_l, gb_l):
    w = jnp.zeros((LANE, 2 * C_KW), f32)
    w = w.at[0:C_RANK, 0:C_KW].set(w2_l[0]).at[C_RANK:2 * C_RANK, C_KW:2 * C_KW].set(w2_l[1])
    return w, jnp.concatenate([gb_l[0], gb_l[1]])[None, :]


def _local_step(x1, c1, ctx1, tgt1, c_ctx, b_mod, weights_of, q_norm, k_norm, b_conv, w2, gb, c_norm, d_conv_w,
                d_conv_b, d_norm_g, d_norm_b, grads_done, ln_g, ln_b, tm, token=None):
    tc, tl = ctx1.shape[0], x1.shape[0]
    T = tc + tl
    rc = min(256, tc)
    tmb = tm // 2
    tmm = 768 if T % 768 == 0 else tm
    rope = _rope_tables(tc, tl)
    cin = jnp.concatenate([c1, c_ctx[None, :], jnp.zeros((SUB - 2, D), f32)], 0)
    if token is not None:
        cin = cin + token[:, 0:1]
    row = lambda v: v[None, :]

    h = jnp.concatenate([ctx1, x1], 0)
    saved, wp, w_br, w_out, w_mod, modv = [], *([None] * DEPTH for _ in range(5))
    for l in range(DEPTH):
        wp[l], merge_weights, w_mod[l] = weights_of(l, h)
        modv[l] = _mod_fwd(cin, w_mod[l], b_mod[l], f"mod_fwd{l}")
        u = _ln_fwd(h, modv[l], tc, tm, f"ln_fwd{l}")
        P = {k: _matmul(u, wp[l][k], "nt", tmm, PROJ_TN[k], D, f"proj{l}{k}", out_dtype=bf16) for k in GROUPS}
        qn, kn, vb = _prep_fwd(P["A"], row(q_norm[l]), row(k_norm[l]), rope, tm, f"prep_fwd{l}")
        ya = _attn_fwd(qn, kn, vb, tc, tm, f"attn_fwd{l}")
        yb, hh = _conv_fwd(P["C"], b_conv[l], d_conv_w[l], row(d_conv_b[l]), tc, tl, rc, f"conv_fwd{l}")
        w2p, b2p = _gate_weights(w2[l], gb[l])
        gla = _gla_fwd(P["G"], w2p, b2p, tc, f"gla_fwd{l}")
        o2, ssave = gla[:2], gla[2:]
        w_br[l], w_out[l] = merge_weights(o2[0])
        hn = _merge_fwd(h, modv[l], ya, yb, o2, hh, P["M"], row(c_norm[l]), row(d_norm_g[l]), row(d_norm_b[l]),
                        row(ln_g[l]), row(ln_b[l]), w_br[l], w_out[l], tc, tm, f"merge_fwd{l}")
        saved.append((h, u, P, qn, kn, vb, ya, yb, hh, o2, ssave, w2p, b2p))
        h = hn

    dh, lparts = _loss_kernel(h, tgt1, tc, tm, "loss")
    loss = jnp.sum(lparts[:, 0, 0])

    g = {k: [None] * DEPTH for k in ("wp", "q_norm", "k_norm", "b_conv", "w2", "gb", "c_norm", "d_conv_w", "d_conv_b",
                                     "d_norm_g", "d_norm_b", "w_br", "w_out", "ln_g", "ln_b", "modv")}
    for l in reversed(range(DEPTH)):
        h_in, u, P, qn, kn, vb, ya, yb, hh, o2, ssave, w2p, b2p = saved[l]
        dP = {}
        (dh_res, dm_mg, dya, dyb, doc, dhh, dP["M"], br, z, acc, dy, dv5, dvd) = _merge_bwd(
            dh, h_in, modv[l], ya, yb, o2, hh, P["M"], row(c_norm[l]), row(d_norm_g[l]), row(d_norm_b[l]),
            row(ln_g[l]), row(ln_b[l]), w_br[l], w_out[l], tc, tmb, f"merge_bwd{l}")
        g["w_br"][l] = _matmul_tn_batched(br, z, N_CHIPS, f"dwbr{l}")
        g["w_out"][l] = _matmul(acc, dy, "tn", D, D, T, f"dwout{l}", out_dtype=bf16)
        tk = grads_done(l, {k: g[k][l] for k in ("w_br", "w_out")})
        qg_l = row(q_norm[l]) if tk is None else row(q_norm[l]) + tk[0:1, :]
        v5 = jnp.sum(dv5, 0)
        g["c_norm"][l], g["d_norm_g"][l], g["d_norm_b"][l] = v5[0], v5[1], v5[2]
        vd = jnp.sum(dvd, 0)
        g["ln_g"][l], g["ln_b"][l] = vd[0], vd[1]
        dqn, dkn, dv = _attn_bwd(qn, kn, vb, dya, tc, tm, f"attn_bwd{l}")
        dP["A"], dqk = _prep_bwd(P["A"], dqn, dkn, dv, qg_l, row(k_norm[l]), rope, tm, f"prep_bwd{l}")
        dqk = jnp.sum(dqk, 0)
        g["q_norm"][l], g["k_norm"][l] = dqk[0], dqk[1]
        dP["C"], dwb, dwd, dbd = _conv_bwd(P["C"], dyb, dhh, b_conv[l], d_conv_w[l], tc, tl, rc, f"conv_bwd{l}")
        g["b_conv"][l], g["d_conv_w"][l], g["d_conv_b"][l] = dwb, dwd, dbd[0]
        dpf, dpb, dw2p, db2p = _gla_bwd(P["G"], w2p, b2p, ssave, doc, tc, f"gla_bwd{l}")
        dP["G"] = _sum_dirs(dpf, dpb, tm, f"gla_sum{l}")
        db2p = db2p[0]
        g["w2"][l] = jnp.stack([dw2p[0:C_RANK, 0:C_KW], dw2p[C_RANK:2 * C_RANK, C_KW:2 * C_KW]])
        g["gb"][l] = jnp.stack([db2p[0:C_KW], db2p[C_KW:2 * C_KW]])
        g["wp"][l] = {k: _matmul(dP[k], u, "tn", DWP_TN[k], D, T, f"dwp{l}{k}", out_dtype=bf16) for k in GROUPS}
        tk = grads_done(l, {"wp": g["wp"][l]})
        du = _matmul_groups(dP, wp[l], DU_TK, tmm, f"du{l}", after=tk)
        dh, dm_ln = _ln_bwd(du, h_in, dh_res, modv[l], tc, tm, f"ln_bwd{l}", latent_only=(l == 0))
        g["modv"][l] = jnp.sum(dm_mg, 0) + jnp.sum(dm_ln, 0)

    dmodv = jnp.stack(g.pop("modv"))
    g["w_mod"], dcin = _mod_bwd(cin, jnp.stack(w_mod, axis=1), dmodv)
    g["b_mod"] = dmodv[:, 0, :] + dmodv[:, 1, :]
    g["c_ctx"] = jnp.sum(dcin, (0, 1))[1]
    return loss, dh, g


HALF_TL = 256


TILE_BYTES = 1 << 20


def _row_tile(rows, cols, itemsize=4):
    tr = min(rows, 128)
    while rows % (2 * tr) == 0 and 2 * tr * cols * itemsize <= TILE_BYTES:
        tr *= 2
    return tr


def _adamw(w, g, m, v, name, tr=None, after=None):
    L, R, C = w.shape
    tr = _row_tile(R, C) if tr is None else tr
    if R % tr == 0:
        grid, spec = (L, R // tr), pl.BlockSpec((None, tr, C), lambda l, i: (l, i, 0))
    elif R * C * 4 <= (1 << 20):
        grid, spec = (L, 1), pl.BlockSpec((None, R, C), lambda l, i: (l, 0, 0))
    else:
        grid, spec = (L, C // HALF_TL), pl.BlockSpec((None, R, HALF_TL), lambda l, i: (l, 0, i))

    def body(w_ref, g_ref, m_ref, v_ref, *rest):
        go_ref, d_ref, nm_ref, nv_ref = rest[-4:]
        gg = g_ref[...]
        go_ref[...] = gg
        nm = B1 * m_ref[...] + (1.0 - B1) * gg
        nv = B2 * v_ref[...] + (1.0 - B2) * (gg * gg)
        m_hat = nm / (1.0 - B1 ** STEP)
        v_hat = nv / (1.0 - B2 ** STEP)
        d_ref[...] = -LR * (m_hat / (jnp.sqrt(v_hat) + AEPS) + WD * w_ref[...])
        nm_ref[...] = nm
        nv_ref[...] = nv

    return pl.pallas_call(
        body, name=name, grid=grid, in_specs=[spec] * 4 + ([] if after is None else [pl.BlockSpec(memory_space=pl.ANY)]),
        out_specs=[spec] * 4, out_shape=[jax.ShapeDtypeStruct((L, R, C), f32)] * 4,
        compiler_params=_cparams(("parallel", "parallel")),
    )(w, g, m, v, *([] if after is None else [after]))


MESH = pl.DeviceIdType.MESH
ANY = pl.BlockSpec(memory_space=pl.ANY)
N_CHIPS = 4


def _place():
    x, y, c = lax.axis_index("x"), lax.axis_index("y"), lax.axis_index("c")
    chips = [(1 - x, y), (x, 1 - y), (1 - x, 1 - y)]
    return x, y, c, chips


def _half(ref, c, axis):
    n = ref.shape[axis] // 2
    last = axis in (-1, ref.ndim - 1)
    idx = [slice(None)] * ref.ndim
    idx[axis] = pl.ds(pl.multiple_of(c * n, LANE if last else SUB), n)
    return ref.at[tuple(idx)]


def _half_shape(shape, axis):
    s = list(shape)
    s[axis] //= 2
    return tuple(s)


def _all_gather(arrs, axes, name):
    n = len(arrs)

    def body(*refs):
        ins, outs = refs[:n], refs[n:2 * n]
        send, recv = refs[2 * n:]
        x, y, c, chips = _place()
        me, sib = 2 * x + y, (x, y, 1 - c)

        def copy(a, k, chip_idx, cc, to, src=None):
            blk = _half(outs[a].at[chip_idx], cc, axes[a])
            return pltpu.make_async_remote_copy(src_ref=blk if src is None else src, dst_ref=blk,
                                                send_sem=send.at[7 * a + k], recv_sem=recv.at[7 * a + k],
                                                device_id=to, device_id_type=MESH)

        own = [pltpu.make_async_remote_copy(src_ref=ins[a], dst_ref=outs[a].at[me], send_sem=send.at[7 * a + 6],
                                            recv_sem=recv.at[7 * a + 6], device_id=sib, device_id_type=MESH)
               for a in range(n)]
        first = own + [copy(a, j, me, c, (*chip, c), src=_half(ins[a], c, axes[a]))
                       for a in range(n) for j, chip in enumerate(chips)]
        for cp in first:
            cp.start()
        passed = []
        for a in range(n):
            for j, chip in enumerate(chips):
                k = 2 * chip[0] + chip[1]
                copy(a, j, k, c, sib).wait_recv()
                fwd = copy(a, 3 + j, k, c, sib)
                fwd.start()
                passed.append(fwd)
        for a in range(n):
            own[a].wait_recv()
            for j, chip in enumerate(chips):
                copy(a, 3 + j, 2 * chip[0] + chip[1], 1 - c, sib).wait_recv()
        for cp in first + passed:
            cp.wait_send()

    return pl.pallas_call(
        body, name=name, in_specs=[ANY] * n, out_specs=[ANY] * n,
        out_shape=[jax.ShapeDtypeStruct((N_CHIPS,) + a.shape, a.dtype) for a in arrs],
        scratch_shapes=[pltpu.SemaphoreType.DMA((7 * n,)), pltpu.SemaphoreType.DMA((7 * n,))],
    )(*arrs)


def _sibling_halves(arrs, axes, name):
    n = len(arrs)

    def body(*refs):
        ins, outs = refs[:n], refs[n:2 * n]
        send, recv = refs[2 * n:]
        x, y, c, _ = _place()
        cps = [pltpu.make_async_remote_copy(src_ref=_half(ins[a], 1 - c, axes[a] + 1), dst_ref=outs[a], send_sem=send.at[a],
                                            recv_sem=recv.at[a], device_id=(x, y, 1 - c), device_id_type=MESH)
               for a in range(n)]
        for cp in cps:
            cp.start()
        for cp in cps:
            cp.wait()

    return pl.pallas_call(
        body, name=name, in_specs=[ANY] * n, out_specs=[ANY] * n,
        out_shape=[jax.ShapeDtypeStruct(_half_shape(a.shape, axes[i] + 1), a.dtype) for i, a in enumerate(arrs)],
        scratch_shapes=[pltpu.SemaphoreType.DMA((n,)), pltpu.SemaphoreType.DMA((n,))],
    )(*arrs)


def _add_half(gfull, land, cidx, axis, name, tr=None, out_dtype=bf16):
    _, hr, hc = land.shape
    if axis == 0:
        tr = min(tr, hr) if tr else _row_tile(hr, hc)
        nb, blk = hr // tr, (None, tr, hc)
        g_spec = pl.BlockSpec(blk, lambda s, i, cr: (s, cr[0] * nb + i, 0))
        l_spec = pl.BlockSpec(blk, lambda s, i, cr: (s, i, 0))
    else:
        nb, blk = hc // HALF_TL, (None, hr, HALF_TL)
        g_spec = pl.BlockSpec(blk, lambda s, i, cr: (s, 0, cr[0] * nb + i))
        l_spec = pl.BlockSpec(blk, lambda s, i, cr: (s, 0, i))

    def body(c_ref, g_ref, l_ref, o_ref):
        o_ref[...] = (g_ref[...].astype(f32) + l_ref[...].astype(f32)).astype(o_ref.dtype)

    return pl.pallas_call(
        body, name=name,
        grid_spec=pltpu.PrefetchScalarGridSpec(
            num_scalar_prefetch=1, grid=(N_CHIPS, nb), in_specs=[g_spec, l_spec], out_specs=l_spec),
        out_shape=jax.ShapeDtypeStruct((N_CHIPS, hr, hc), out_dtype),
        compiler_params=_cparams(("parallel", "parallel")),
    )(cidx, gfull, land)


def _chip_exchange(arrs, name):
    n = len(arrs)

    def body(*refs):
        ins, outs = refs[:n], refs[n:2 * n]
        send, recv = refs[2 * n:]
        x, y, c, chips = _place()
        me = 2 * x + y
        cps = []
        for a in range(n):
            for j, chip in enumerate(chips):
                k = 2 * chip[0] + chip[1]
                cps.append((pltpu.make_async_remote_copy(
                    src_ref=ins[a].at[k], dst_ref=outs[a].at[me], send_sem=send.at[3 * a + j], recv_sem=recv.at[3 * a + j],
                    device_id=(*chip, c), device_id_type=MESH), a, j, k))
        for cp, *_ in cps:
            cp.start()
        for cp, a, j, k in cps:
            pltpu.make_async_remote_copy(src_ref=ins[a].at[k], dst_ref=outs[a].at[k], send_sem=send.at[3 * a + j],
                                         recv_sem=recv.at[3 * a + j], device_id=(x, y, c), device_id_type=MESH).wait_recv()
        for cp, *_ in cps:
            cp.wait_send()

    return pl.pallas_call(
        body, name=name, in_specs=[ANY] * n, out_specs=[ANY] * n,
        out_shape=[jax.ShapeDtypeStruct(a.shape, a.dtype) for a in arrs],
        scratch_shapes=[pltpu.SemaphoreType.DMA((3 * n,)), pltpu.SemaphoreType.DMA((3 * n,))],
    )(*arrs)


def _sum_chips(land, own, place, axis, layer, into, name, tr=None):
    _, hr, hc = land.shape
    fresh = not hasattr(into, "dtype")
    shape = tuple(into) if fresh else into.shape
    if axis == 0:
        tr = min(tr, hr) if tr else _row_tile(hr, 4 * hc, 2)
        nb, blk = hr // tr, (tr, hc)
        l_map, m_map = (lambda i, p: (0, i, 0)), (lambda i, p: (p[0], i, 0))
        o_map = lambda i, p: (layer, p[1] * nb + i, 0)
    else:
        nb, blk = hc // HALF_TL, (hr, HALF_TL)
        l_map, m_map = (lambda i, p: (0, 0, i)), (lambda i, p: (p[0], 0, i))
        o_map = lambda i, p: (layer, 0, p[1] * nb + i)

    def body(p_ref, l_ref, o_ref, *rest):
        me = p_ref[0]
        mine = o_ref[...].astype(f32)
        acc = None
        for k in range(N_CHIPS):
            t = jnp.where(me == k, mine, l_ref[k].astype(f32))
            acc = t if acc is None else acc + t
        rest[-1][...] = acc

    return pl.pallas_call(
        body, name=name,
        grid_spec=pltpu.PrefetchScalarGridSpec(
            num_scalar_prefetch=1, grid=(nb,),
            in_specs=[pl.BlockSpec((N_CHIPS,) + blk, l_map), pl.BlockSpec((None,) + blk, m_map)] + ([] if fresh else [ANY]),
            out_specs=pl.BlockSpec((None,) + blk, o_map)),
        out_shape=jax.ShapeDtypeStruct(shape, f32),
        input_output_aliases={} if fresh else {3: 0},
        compiler_params=_cparams(("parallel",)),
    )(place, land, own, *([] if fresh else [into]))


def _sibling_fill(arrs, axes, name):
    n = len(arrs)

    def body(*refs):
        outs = refs[n:2 * n]
        send, recv = refs[2 * n:]
        x, y, c, _ = _place()
        cps = [pltpu.make_async_remote_copy(src_ref=_half(outs[a], c, axes[a] + 1), dst_ref=_half(outs[a], c, axes[a] + 1),
                                            send_sem=send.at[a], recv_sem=recv.at[a], device_id=(x, y, 1 - c),
                                            device_id_type=MESH) for a in range(n)]
        for cp in cps:
            cp.start()
        for a in range(n):
            blk = _half(outs[a], 1 - c, axes[a] + 1)
            pltpu.make_async_remote_copy(src_ref=blk, dst_ref=blk, send_sem=send.at[a], recv_sem=recv.at[a],
                                         device_id=(x, y, 1 - c), device_id_type=MESH).wait_recv()
        for cp in cps:
            cp.wait_send()

    return pl.pallas_call(
        body, name=name, in_specs=[ANY] * n, out_specs=[ANY] * n,
        out_shape=[jax.ShapeDtypeStruct(a.shape, a.dtype) for a in arrs],
        input_output_aliases={a: a for a in range(n)},
        scratch_shapes=[pltpu.SemaphoreType.DMA((n,)), pltpu.SemaphoreType.DMA((n,))],
    )(*arrs)


HBM = pl.BlockSpec(memory_space=pltpu.HBM)
SEM = pl.BlockSpec(memory_space=pltpu.SEMAPHORE)
EFFECT = pltpu.SideEffectType.DATAFLOW_SIDE_EFFECTING
PEERS = 4


def _split_copies(srcs, lands, send, recv, gather, axes=None):
    x, y, c, chips = _place()
    me = 2 * x + y
    if axes is not None:
        out = []
        for a in range(len(srcs)):
            sems = dict(send_sem=send.at[PEERS * a], recv_sem=recv.at[PEERS * a], device_id=(x, y, 1 - c), device_id_type=MESH)
            copy = pltpu.make_async_remote_copy(src_ref=_half(srcs[a], 1 - c, axes[a] + 1), dst_ref=lands[a], **sems)
            out.append((copy, copy))
        return out
    peers = [((*chip, c), 2 * chip[0] + chip[1]) for chip in chips] + ([((x, y, 1 - c), me)] if gather else [])
    out = []
    for a in range(len(srcs)):
        for j, (dev, k) in enumerate(peers):
            src = srcs[a] if gather else srcs[a].at[k]
            sems = dict(send_sem=send.at[PEERS * a + j], recv_sem=recv.at[PEERS * a + j], device_id=dev, device_id_type=MESH)
            out.append((pltpu.make_async_remote_copy(src_ref=src, dst_ref=lands[a].at[me], **sems),
                        pltpu.make_async_remote_copy(src_ref=src, dst_ref=lands[a].at[k], **sems)))
    return out


def _split_start(srcs, gather, after, name, axes=None):
    n = len(srcs)
    if axes is not None:
        lands = [lax.empty(_half_shape(s.shape, axes[a] + 1), s.dtype) for a, s in enumerate(srcs)]
    else:
        lands = [lax.empty(((N_CHIPS,) + s.shape) if gather else s.shape, s.dtype) for s in srcs]

    def body(*refs):
        send, recv = refs[2 * n + 1], refs[2 * n + 2]
        for start, _ in _split_copies(refs[:n], refs[n:2 * n], send, recv, gather, axes):
            start.start()
        refs[-1][...] = jnp.zeros_like(refs[-1])

    sems = pltpu.SemaphoreType.DMA((PEERS * n,))
    hbm = lambda a: pltpu.with_memory_space_constraint(a, pltpu.HBM)
    out = pl.pallas_call(
        body, name=name,
        out_shape=(sems, sems, *[pltpu.HBM(a.shape, a.dtype) for a in srcs + lands], jax.ShapeDtypeStruct((SUB, LANE), f32)),
        in_specs=[HBM] * (2 * n) + [ANY], out_specs=(SEM, SEM, *[HBM] * (2 * n), pl.BlockSpec(memory_space=pltpu.VMEM)),
        input_output_aliases={i: 2 + i for i in range(2 * n)},
        compiler_params=pltpu.CompilerParams(has_side_effects=EFFECT),
    )(*[hbm(a) for a in srcs + lands], after)
    return out[0], out[1], list(out[2:2 + n]), list(out[2 + n:2 + 2 * n]), out[-1]


def _split_wait(send, recv, srcs, lands, gather, after, name, axes=None):
    n = len(srcs)

    def body(*refs):
        for start, arrival in _split_copies(refs[:n], refs[n:2 * n], refs[2 * n], refs[2 * n + 1], gather, axes):
            start.wait_send()
            arrival.wait_recv()

    out = pl.pallas_call(
        body, name=name, out_shape=[pltpu.HBM(a.shape, a.dtype) for a in srcs + lands],
        in_specs=[HBM] * (2 * n) + [SEM, SEM, ANY], out_specs=[HBM] * (2 * n),
        input_output_aliases={i: i for i in range(2 * n)},
        compiler_params=pltpu.CompilerParams(has_side_effects=EFFECT),
    )(*srcs, *lands, send, recv, after)
    return list(out[:n]), list(out[n:])


N_DEV = 8


def _all_reduce_small(v, name):
    R = v.shape[0]

    def body(v_ref, o_ref, land_ref, send, recv):
        x, y, c, _ = _place()
        me = 4 * x + 2 * y + c
        land_ref[me] = v_ref[...]
        cps = []
        for m in range(1, N_DEV):
            px, py, pc = [(1 - q) if (m >> s) & 1 else q for q, s in ((x, 2), (y, 1), (c, 0))]
            cps.append((pltpu.make_async_remote_copy(src_ref=v_ref, dst_ref=land_ref.at[me], send_sem=send.at[m - 1],
                                                     recv_sem=recv.at[m - 1], device_id=(px, py, pc), device_id_type=MESH),
                        4 * px + 2 * py + pc, m))
        for cp, *_ in cps:
            cp.start()
        for cp, peer, m in cps:
            pltpu.make_async_remote_copy(src_ref=v_ref, dst_ref=land_ref.at[peer], send_sem=send.at[m - 1],
                                         recv_sem=recv.at[m - 1], device_id=(x, y, c), device_id_type=MESH).wait_recv()
        for cp, *_ in cps:
            cp.wait_send()
        acc = land_ref[0]
        for k in range(1, N_DEV):
            acc = acc + land_ref[k]
        o_ref[...] = acc

    vm = pl.BlockSpec(memory_space=pltpu.VMEM)
    return pl.pallas_call(
        body, name=name, in_specs=[vm], out_specs=vm, out_shape=jax.ShapeDtypeStruct(v.shape, f32),
        scratch_shapes=[pltpu.VMEM((N_DEV, R, LANE), f32), pltpu.SemaphoreType.DMA((N_DEV - 1,)),
                        pltpu.SemaphoreType.DMA((N_DEV - 1,))],
        compiler_params=pltpu.CompilerParams(vmem_limit_bytes=VMEM_LIMIT),
    )(v)


def _pack_small(arrs, mult=2 * SUB):
    flat = jnp.concatenate([a.reshape(-1) for a in arrs])
    rows = -(-flat.shape[0] // (LANE * mult)) * mult
    return jnp.pad(flat, (0, rows * LANE - flat.shape[0])).reshape(rows, LANE)


def _unpack_small(vec, shapes):
    flat, out, o = vec.reshape(-1), [], 0
    for s in shapes:
        n = int(np.prod(s))
        out.append(flat[o:o + n].reshape(s))
        o += n
    return out


REPL_SMALL = ("c_ctx", "b_mod", "q_norm", "k_norm", "c_norm", "d_conv_b", "d_norm_g", "d_norm_b", "ln_g", "ln_b")
SHARD_SMALL = ("b_conv", "c_gate_w2", "c_gate_b", "d_conv_w")
BIG = ("w_mod", "w_in", "w_br", "w_out")
ORDER = ("c_ctx", "w_mod", "b_mod", "w_in", "q_norm", "k_norm", "b_conv", "c_gate_w2", "c_gate_b", "c_norm", "d_conv_w",
         "d_conv_b", "d_norm_g", "d_norm_b", "w_br", "w_out", "ln_g", "ln_b")


def _unshard_last(g4, shard_shape):
    g = g4.reshape((N_CHIPS,) + tuple(shard_shape))
    g = jnp.moveaxis(g, 0, -2)
    return g.reshape(tuple(shard_shape[:-1]) + (N_CHIPS * shard_shape[-1],))


def _pieces_last(full):
    w = full.shape[-1] // N_CHIPS
    g = full.reshape(full.shape[:-1] + (N_CHIPS, w))
    return jnp.moveaxis(g, -2, 0).reshape(N_CHIPS, -1, w)


def kernel(x, c, ctx, c_ctx, w_mod, b_mod, w_in, q_norm, k_norm, b_conv, c_gate_w2, c_gate_b, c_norm, d_conv_w, d_conv_b, d_norm_g, d_norm_b, w_br, w_out, ln_g, ln_b, loss_target, m_c_ctx, m_w_mod, m_b_mod, m_w_in, m_q_norm, m_k_norm, m_b_conv, m_c_gate_w2, m_c_gate_b, m_c_norm, m_d_conv_w, m_d_conv_b, m_d_norm_g, m_d_norm_b, m_w_br, m_w_out, m_ln_g, m_ln_b, v_c_ctx, v_w_mod, v_b_mod, v_w_in, v_q_norm, v_k_norm, v_b_conv, v_c_gate_w2, v_c_gate_b, v_c_norm, v_d_conv_w, v_d_conv_b, v_d_norm_g, v_d_norm_b, v_w_br, v_w_out, v_ln_g, v_ln_b):
    W = dict(c_ctx=c_ctx, w_mod=w_mod, b_mod=b_mod, w_in=w_in, q_norm=q_norm, k_norm=k_norm, b_conv=b_conv,
             c_gate_w2=c_gate_w2, c_gate_b=c_gate_b, c_norm=c_norm, d_conv_w=d_conv_w, d_conv_b=d_conv_b,
             d_norm_g=d_norm_g, d_norm_b=d_norm_b, w_br=w_br, w_out=w_out, ln_g=ln_g, ln_b=ln_b)
    M = dict(c_ctx=m_c_ctx, w_mod=m_w_mod, b_mod=m_b_mod, w_in=m_w_in, q_norm=m_q_norm, k_norm=m_k_norm, b_conv=m_b_conv,
             c_gate_w2=m_c_gate_w2, c_gate_b=m_c_gate_b, c_norm=m_c_norm, d_conv_w=m_d_conv_w, d_conv_b=m_d_conv_b,
             d_norm_g=m_d_norm_g, d_norm_b=m_d_norm_b, w_br=m_w_br, w_out=m_w_out, ln_g=m_ln_g, ln_b=m_ln_b)
    V = dict(c_ctx=v_c_ctx, w_mod=v_w_mod, b_mod=v_b_mod, w_in=v_w_in, q_norm=v_q_norm, k_norm=v_k_norm, b_conv=v_b_conv,
             c_gate_w2=v_c_gate_w2, c_gate_b=v_c_gate_b, c_norm=v_c_norm, d_conv_w=v_d_conv_w, d_conv_b=v_d_conv_b,
             d_norm_g=v_d_norm_g, d_norm_b=v_d_norm_b, w_br=v_w_br, w_out=v_w_out, ln_g=v_ln_g, ln_b=v_ln_b)
    chip = 2 * lax.axis_index("x") + lax.axis_index("y")
    cidx = lax.axis_index("c").astype(jnp.int32).reshape(1)

    place = jnp.stack([chip, lax.axis_index("c")]).astype(jnp.int32)

    AXIS = dict(w_in=1, w_mod=0, w_br=0, w_out=0)
    ex = dict(w_in=lambda a: jnp.swapaxes(a, 1, 2), w_mod=lambda a: a.reshape(1, DEPTH * D, -1),
              w_br=lambda a: a.reshape(DEPTH, 4 * BRW, -1), w_out=lambda a: a)
    Wx, Mx, Vx = ({k: ex[k](P_[k]) for k in BIG} for P_ in (W, M, V))

    LAYER, MERGE = ("w_in", "w_br", "w_out"), ("w_br", "w_out")
    small_shard = _pack_small([W[k] for k in SHARD_SMALL])
    keys0 = ("w_in", "w_mod")
    sent = lambda k, l: (w_mod[l] if k == "w_mod" else Wx[k][l]).astype(bf16)
    got = _all_gather([sent(k, 0) for k in keys0] + [small_shard], [AXIS[k] for k in keys0] + [0], "all_gather0")
    smalls = [_unpack_small(got[-1][s], [W[k].shape for k in SHARD_SMALL]) for s in range(N_CHIPS)]
    full = {k: jnp.concatenate([smalls[s][i] for s in range(N_CHIPS)], axis=-1) for i, k in enumerate(SHARD_SMALL)}
    ag0b = _split_start([sent(k, 0) for k in MERGE], True, got[0], "all_gather0b_start")
    ag1 = _split_start([sent(k, 1) for k in keys0], True, ag0b[4], "all_gather1_start")
    ag1b = _split_start([sent(k, 1) for k in MERGE], True, ag1[4], "all_gather1b_start")

    def merge_form(w_br4, w_out4):
        return jnp.moveaxis(w_br4.reshape(N_CHIPS, 4, BRW, D // N_CHIPS), 0, 2).reshape(4, BRW, D), w_out4.reshape(D, D)

    def weights_of(l, h):
        first = got if l == 0 else _split_wait(*ag1[:4], True, h, "all_gather1_wait")[1]
        flight = (ag0b, ag1b)[l]
        return (_group_weights(first[0]),
                lambda after: merge_form(*_split_wait(*flight[:4], True, after, f"all_gather{l}b_wait")[1]), first[1])

    red = {k: Wx[k].shape for k in BIG}
    flights, held = {}, {}

    def launch(tag, l, pieces, after=None):
        keys = list(pieces)
        land_a = _sibling_halves([pieces[k] for k in keys], [AXIS[k] for k in keys], f"rs_sibling_halves{tag}")
        pair = [_add_half(pieces[k], la, cidx, AXIS[k], f"rs_pair_sum{tag}_{k}") for k, la in zip(keys, land_a)]
        after = jnp.zeros((SUB, LANE), f32) if after is None else after
        flights[tag] = (l, keys, _split_start(pair, False, after, f"rs_chip_exchange{tag}_start"))
        return flights[tag][2][4]

    def land(tag, after):
        l, keys, flight = flights.pop(tag)
        pair, land_b = _split_wait(*flight[:4], False, after, f"rs_chip_exchange{tag}_wait")
        for k, lb, pr in zip(keys, land_b, pair):
            red[k] = _sum_chips(lb, pr, place, AXIS[k], l, red[k], f"rs_chip_sum{tag}_{k}")

    def grads_done(l, gl):
        if "wp" in gl:
            pieces = dict(w_in=_ungroup(gl["wp"]).reshape(N_CHIPS, SHARD, D))
            return launch("0c", 0, pieces) if l == 0 else launch("1", 1, {**pieces, **held.pop(1)})
        pieces = dict(w_br=gl["w_br"].reshape(N_CHIPS, 4 * BRW, D // N_CHIPS), w_out=gl["w_out"].reshape(N_CHIPS, D // N_CHIPS, D))
        if l == 0:
            return launch("0b", 0, pieces)
        held[1] = pieces
        return None

    loss, gx, g = _local_step(
        x[0], c, ctx[0], loss_target[0], c_ctx, b_mod, weights_of, q_norm, k_norm, full["b_conv"],
        full["c_gate_w2"], full["c_gate_b"], c_norm, full["d_conv_w"], d_conv_b, d_norm_g, d_norm_b,
        grads_done, ln_g, ln_b, tm=256, token=ag1b[4])
    g["c_gate_w2"], g["c_gate_b"] = g.pop("w2"), g.pop("gb")
    loss = lax.psum(loss, ("x", "y", "c"))

    w_mod_pieces = g["w_mod"].reshape(N_CHIPS, DEPTH * D, 3 * D // N_CHIPS)
    g = {k: (jnp.stack(v) if isinstance(v, list) else v) for k, v in g.items() if k not in ("wp", "w_br", "w_out", "w_mod")}

    small_names = REPL_SMALL + SHARD_SMALL
    gs = _all_reduce_small(_pack_small([g[k] for k in small_names]), "all_reduce_small")
    gsm = dict(zip(small_names, _unpack_small(gs, [g[k].shape for k in small_names])))
    for k in SHARD_SMALL:
        wdt = W[k].shape[-1]
        gsm[k] = lax.dynamic_slice_in_dim(gsm[k], chip * wdt, wdt, axis=gsm[k].ndim - 1)

    grad, delta, new_m, new_v = {}, {}, {}, {}

    def adamw_big(keys, after):
        filled = _sibling_fill([red[k] for k in keys], [AXIS[k] for k in keys], "rs_sibling_fill_" + keys[0])
        for k, r in zip(keys, filled):
            back = (lambda a: jnp.swapaxes(a, 1, 2)) if k == "w_in" else (lambda a: a.reshape(W[k].shape))
            g_, d_, m_, v_ = _adamw(Wx[k], r, Mx[k], Vx[k], f"adamw_{k}", after=after)
            grad[k], delta[k], new_m[k], new_v[k] = back(g_), back(d_), back(m_), back(v_)
        return d_

    token = launch("0d", 0, {"w_mod": w_mod_pieces}, after=gs)
    land("1", gx)
    land("0b", gx)
    last = adamw_big(MERGE, token)
    shapes = [W[k].shape for k in small_names]
    _, d_, m_, v_ = _adamw(*[_pack_small([P_[k] for k in small_names])[None] for P_ in (W, gsm, M, V)], "adamw_small", after=last)
    for k, dd, mm_, vv in zip(small_names, _unpack_small(d_, shapes), _unpack_small(m_, shapes), _unpack_small(v_, shapes)):
        grad[k], delta[k], new_m[k], new_v[k] = gsm[k], dd, mm_, vv
    land("0c", d_)
    land("0d", d_)
    adamw_big(("w_in", "w_mod"), None)

    return (loss, gx[None], *[grad[k] for k in ORDER], *[delta[k] for k in ORDER], *[new_m[k] for k in ORDER],
            *[new_v[k] for k in ORDER])
```

```python
import functools

import jax
import jax.numpy as jnp
import numpy as np
from jax import lax
from jax.experimental import pallas as pl
from jax.experimental.pallas import tpu as pltpu

f32 = jnp.float32
bf16 = jnp.bfloat16

D = 1024
DEPTH = 2
GRID_W = 64
BRW = 512
HD = 128
A_HEADS = 4
C_HEADS = 4
C_KW = 256
C_RANK = 16
C_TAU = 16.0
CH = 128
KB = 3
KD = 31
ALPHA = (2 * DEPTH) ** 0.25
EPS = 1e-6
ROPE_THETA = 10000.0
N_IN = 10784
LR, B1, B2, AEPS, WD, STEP = 0.001, 0.9, 0.999, 1e-08, 0.01, 10

W_M, W_A, W_C, W_G = 4 * D + 4 * BRW, 1024, 5 * BRW, 1152
GROUPS = ("M", "A", "C", "G")
M_GA, M_GB, M_GC, M_GD = 4 * D, 4 * D + BRW, 4 * D + 2 * BRW, 4 * D + 3 * BRW
A_K, A_V = 512, 768
G_K, G_V, G_R = 256, 512, 1024
S_Q, S_GA, S_B, S_C, S_X, S_GB, S_CQ, S_CV, S_GC, S_R, S_DA, S_DG, S_GD, S_MG = (
    0, 1024, 1536, 2048, 2560, 3072, 3584, 4096, 4608, 5120, 5152, 5664, 6176, 6688)

LANE = 128
SUB = 8
VMEM_LIMIT = 56 * 1024 * 1024
CONV_PAD = 16
GLA_SUB = 16
GLA_CLAMP = 60.0


def _cparams(sem, vmem=VMEM_LIMIT):
    return pltpu.CompilerParams(dimension_semantics=sem, vmem_limit_bytes=vmem)


def _dg(a, b, ca, cb):
    return lax.dot_general(a.astype(bf16), b.astype(bf16), (((ca,), (cb,)), ((), ())),
                           preferred_element_type=f32)


@jax.custom_vjp
def mm(a, b):
    return _dg(a, b, 1, 0)


mm.defvjp(lambda a, b: (_dg(a, b, 1, 0), (a, b)),
          lambda r, ct: (_dg(ct, r[1], 1, 1).astype(r[0].dtype), _dg(r[0], ct, 0, 0).astype(r[1].dtype)))


@jax.custom_vjp
def mm_nt(a, b):
    return _dg(a, b, 1, 1)


mm_nt.defvjp(lambda a, b: (_dg(a, b, 1, 1), (a, b)),
             lambda r, ct: (_dg(ct, r[1], 1, 0).astype(r[0].dtype), _dg(ct, r[0], 0, 0).astype(r[1].dtype)))


@jax.custom_vjp
def mm_tn(a, b):
    return _dg(a, b, 0, 0)


mm_tn.defvjp(lambda a, b: (_dg(a, b, 0, 0), (a, b)),
             lambda r, ct: (_dg(r[1], ct, 1, 1).astype(r[0].dtype), _dg(r[0], ct, 1, 0).astype(r[1].dtype)))


@jax.custom_vjp
def _sigmoid(x):
    return 0.5 * jnp.tanh(0.5 * x) + 0.5


def _sigmoid_fwd(x):
    s = _sigmoid(x)
    return s, s


_sigmoid.defvjp(_sigmoid_fwd, lambda s, ct: (ct * (s - s * s),))


@jax.custom_vjp
def _silu(x):
    return x * _sigmoid(x)


def _silu_fwd(x):
    s = _sigmoid(x)
    return x * s, (x, s)


_silu.defvjp(_silu_fwd, lambda r, ct: (ct * (r[1] + r[0] * (r[1] - r[1] * r[1])),))


def _ln(x):
    mu = jnp.mean(x, -1, keepdims=True)
    xc = x - mu
    var = jnp.mean(xc * xc, -1, keepdims=True)
    return xc * lax.rsqrt(var + EPS)


def _rms(x, g):
    return x * lax.rsqrt(jnp.mean(x * x, -1, keepdims=True) + EPS) * g


@jax.custom_vjp
def _rope(x, cos_f, sin_a, sin_b):
    return x * cos_f + pltpu.roll(x, HD - 1, 1) * sin_a + pltpu.roll(x, 1, 1) * sin_b


def _rope_fwd(x, cos_f, sin_a, sin_b):
    return _rope(x, cos_f, sin_a, sin_b), (cos_f, sin_a, sin_b)


def _rope_bwd(r, ct):
    cos_f, sin_a, sin_b = r
    dx = ct * cos_f + pltpu.roll(ct * sin_a, 1, 1) + pltpu.roll(ct * sin_b, HD - 1, 1)
    return dx, jnp.zeros_like(cos_f), jnp.zeros_like(sin_a), jnp.zeros_like(sin_b)


_rope.defvjp(_rope_fwd, _rope_bwd)


def _row_ids(i, tm):
    return i * tm + lax.broadcasted_iota(jnp.int32, (tm, 1), 0)


def _partial_rows(ref, rows):
    n = len(rows)
    for k, r in enumerate(rows):
        ref[k:k + 1, :] = r
    ref[n:SUB, :] = jnp.zeros((SUB - n, ref.shape[-1]), f32)


def _matmul(a, b, mode, tm, tn, tk, name, out_dtype=f32, add=None, after=None):
    sect = a.ndim == 3
    a2 = (a.shape[1], a.shape[0] * a.shape[2]) if sect else a.shape
    if mode == "nn":
        (M, K), N = a2, b.shape[1]
        a_spec = pl.BlockSpec((None, tm, tk), lambda j, i, k: (k, i, 0)) if sect else pl.BlockSpec((tm, tk), lambda j, i, k: (i, k))
        b_spec = pl.BlockSpec((tk, tn), lambda j, i, k: (k, j))
        ca, cb = 1, 0
        assert not sect or tk == a.shape[2]
    elif mode == "nt":
        (M, K), N = a2, b.shape[0]
        assert not sect
        a_spec = pl.BlockSpec((tm, tk), lambda j, i, k: (i, k))
        b_spec = pl.BlockSpec((tn, tk), lambda j, i, k: (j, k))
        ca, cb = 1, 1
    else:
        (K, M), N = a2, b.shape[1]
        a_spec = pl.BlockSpec((None, tk, tm), lambda j, i, k: (i, k, 0)) if sect else pl.BlockSpec((tk, tm), lambda j, i, k: (k, i))
        b_spec = pl.BlockSpec((tk, tn), lambda j, i, k: (k, j))
        ca, cb = 0, 0
        assert not sect or tm == a.shape[2]
    assert M % tm == 0 and N % tn == 0 and K % tk == 0, (name, M, N, K, tm, tn, tk)
    nk = K // tk

    o_spec = pl.BlockSpec((tm, tn), lambda j, i, k: (i, j))

    def body(a_ref, b_ref, *rest):
        add_ref = rest[0] if add is not None else None
        o_ref, acc_ref = rest[-2:]
        k = pl.program_id(2)
        part = _dg(a_ref[...], b_ref[...], ca, cb)

        @pl.when(k == 0)
        def _():
            acc_ref[...] = part if add_ref is None else part + add_ref[...]

        @pl.when(k > 0)
        def _():
            acc_ref[...] += part

        @pl.when(k == nk - 1)
        def _():
            o_ref[...] = acc_ref[...].astype(o_ref.dtype)

    extra = ([] if add is None else [(o_spec, add)]) + ([] if after is None else [(pl.BlockSpec(memory_space=pl.ANY), after)])
    return pl.pallas_call(
        body, name=name, grid=(N // tn, M // tm, nk),
        in_specs=[a_spec, b_spec] + [s_ for s_, _ in extra], out_specs=o_spec,
        out_shape=jax.ShapeDtypeStruct((M, N), out_dtype),
        scratch_shapes=[pltpu.VMEM((tm, tn), f32)],
        compiler_params=_cparams(("parallel", "parallel", "arbitrary")),
    )(a, b, *[v_ for _, v_ in extra])


def _matmul_groups(a, b, tks, tm, name, after=None):
    keys = list(a)
    M = a[keys[0]].shape[-2]
    N = b[keys[0]].shape[1]
    count = {g: b[g].shape[0] // tks[g] for g in keys}
    first, total = {}, 0
    for g in keys:
        first[g], total = total, total + count[g]

    def k_of(g):
        return lambda s: jnp.clip(s - first[g], 0, count[g] - 1)

    a_specs = [pl.BlockSpec((None, tm, tks[g]), functools.partial(lambda i, s, kk: (kk(s), i, 0), kk=k_of(g)))
               if a[g].ndim == 3 else pl.BlockSpec((tm, tks[g]), functools.partial(lambda i, s, kk: (i, kk(s)), kk=k_of(g)))
               for g in keys]
    b_specs = [pl.BlockSpec((tks[g], N), functools.partial(lambda i, s, kk: (kk(s), 0), kk=k_of(g))) for g in keys]
    n = len(keys)

    def body(*refs):
        o_ref, acc_ref = refs[-2:]
        s = pl.program_id(1)

        @pl.when(s == 0)
        def _():
            acc_ref[...] = jnp.zeros_like(acc_ref)

        for j, g in enumerate(keys):
            @pl.when((s >= first[g]) & (s < first[g] + count[g]))
            def _(j=j):
                acc_ref[...] += _dg(refs[j][...], refs[n + j][...], 1, 0)

        @pl.when(s == total - 1)
        def _():
            o_ref[...] = acc_ref[...]

    extra = [] if after is None else [after]
    return pl.pallas_call(
        body, name=name, grid=(M // tm, total),
        in_specs=a_specs + b_specs + [pl.BlockSpec(memory_space=pl.ANY)] * len(extra),
        out_specs=pl.BlockSpec((tm, N), lambda i, s: (i, 0)),
        out_shape=jax.ShapeDtypeStruct((M, N), f32),
        scratch_shapes=[pltpu.VMEM((tm, N), f32)],
        compiler_params=_cparams(("parallel", "arbitrary")),
    )(*[a[g] for g in keys], *[b[g] for g in keys], *extra)


def _matmul_tn_batched(a, b, ns, name):
    B, K, M = a.shape
    N = b.shape[2] // ns

    def body(a_ref, b_ref, o_ref):
        o_ref[...] = _dg(a_ref[...], b_ref[...], 0, 0).astype(bf16)

    return pl.pallas_call(
        body, name=name, grid=(B, ns),
        in_specs=[pl.BlockSpec((None, K, M), lambda i, s: (i, 0, 0)), pl.BlockSpec((None, K, N), lambda i, s: (i, 0, s))],
        out_specs=pl.BlockSpec((None, None, M, N), lambda i, s: (s, i, 0, 0)),
        out_shape=jax.ShapeDtypeStruct((ns, B, M, N), bf16),
        compiler_params=_cparams(("parallel", "parallel")),
    )(a, b)


MOD_TN = 768


def _mod_fwd(cin, w_mod_l, b_mod_l, name):
    def body(c_ref, w_ref, b_ref, o_ref):
        o_ref[...] = mm(_silu(c_ref[...]), w_ref[...]) + b_ref[...]

    return pl.pallas_call(
        body, name=name, grid=(3 * D // MOD_TN,),
        in_specs=[pl.BlockSpec((SUB, D), lambda j: (0, 0)), pl.BlockSpec((None, D, MOD_TN), lambda j: (j, 0, 0)),
                  pl.BlockSpec((1, MOD_TN), lambda j: (0, j))],
        out_specs=pl.BlockSpec((SUB, MOD_TN), lambda j: (0, j)),
        out_shape=jax.ShapeDtypeStruct((SUB, 3 * D), f32),
        compiler_params=_cparams(("parallel",)),
    )(cin, w_mod_l, b_mod_l[None, :])


def _mod_bwd(cin, w_mod, dmodv):
    nj = 3 * D // MOD_TN

    def body(c_ref, w_ref, g_ref, dw_ref, dc_ref):
        _, vjp = jax.vjp(lambda c, w: mm(_silu(c), w), c_ref[...], w_ref[...].astype(f32))
        dc, dw = vjp(g_ref[...])
        dw_ref[...] = dw.astype(bf16)
        dc_ref[...] = dc

    return pl.pallas_call(
        body, name="mod_bwd", grid=(DEPTH, nj),
        in_specs=[pl.BlockSpec((SUB, D), lambda l, j: (0, 0)),
                  pl.BlockSpec((None, None, D, MOD_TN), lambda l, j: (j, l, 0, 0)),
                  pl.BlockSpec((None, SUB, MOD_TN), lambda l, j: (l, 0, j))],
        out_specs=[pl.BlockSpec((None, None, D, MOD_TN), lambda l, j: (j, l, 0, 0)),
                   pl.BlockSpec((None, None, SUB, D), lambda l, j: (l, j, 0, 0))],
        out_shape=[jax.ShapeDtypeStruct((nj, DEPTH, D, MOD_TN), bf16),
                   jax.ShapeDtypeStruct((DEPTH, nj, SUB, D), f32)],
        compiler_params=_cparams(("parallel", "parallel")),
    )(cin, w_mod, dmodv)


def _u_fn(h, m_l, m_c, isctx):
    n = _ln(h)
    shift = jnp.where(isctx, m_c[:, 0:D], m_l[:, 0:D])
    scale = jnp.where(isctx, m_c[:, D:2 * D], m_l[:, D:2 * D])
    return n * (1.0 + scale) + shift


def _ln_fwd(h, modv_l, tc, tm, name):
    T = h.shape[0]

    def body(h_ref, m_ref, u_ref):
        isctx = _row_ids(pl.program_id(0), tm) < tc
        u_ref[...] = _u_fn(h_ref[...], m_ref[0:1, :], m_ref[1:2, :], isctx).astype(bf16)

    return pl.pallas_call(
        body, name=name, grid=(T // tm,),
        in_specs=[pl.BlockSpec((tm, D), lambda i: (i, 0)), pl.BlockSpec((SUB, 3 * D), lambda i: (0, 0))],
        out_specs=pl.BlockSpec((tm, D), lambda i: (i, 0)),
        out_shape=jax.ShapeDtypeStruct((T, D), bf16),
        compiler_params=_cparams(("parallel",)),
    )(h, modv_l)


def _ln_bwd(du, h, dh_res, modv_l, tc, tm, name, latent_only=False):
    T = h.shape[0]
    nt, nct = T // tm, tc // tm

    def body(du_ref, h_ref, r_ref, m_ref, dh_ref, dm_ref):
        isctx = _row_ids(pl.program_id(0), tm) < tc
        _, vjp = jax.vjp(lambda h, ml, mc: _u_fn(h, ml, mc, isctx), h_ref[...], m_ref[0:1, :], m_ref[1:2, :])
        dh, dml, dmc = vjp(du_ref[...])
        dh_ref[...] = dh + r_ref[...]
        _partial_rows(dm_ref, [dml, dmc])

    dh_map = (lambda i: (jnp.maximum(i - nct, 0), 0)) if latent_only else (lambda i: (i, 0))
    return pl.pallas_call(
        body, name=name, grid=(nt,),
        in_specs=[pl.BlockSpec((tm, D), lambda i: (i, 0)), pl.BlockSpec((tm, D), lambda i: (i, 0)),
                  pl.BlockSpec((tm, D), lambda i: (i, 0)), pl.BlockSpec((SUB, 3 * D), lambda i: (0, 0))],
        out_specs=[pl.BlockSpec((tm, D), dh_map), pl.BlockSpec((None, SUB, 3 * D), lambda i: (i, 0, 0))],
        out_shape=[jax.ShapeDtypeStruct((T - tc if latent_only else T, D), f32), jax.ShapeDtypeStruct((nt, SUB, 3 * D), f32)],
        compiler_params=_cparams(("arbitrary",)),
    )(du, h, dh_res, modv_l)


def _prep_fn(q, k, qg, kg, cos_f, sin_a, sin_b):
    qs = [_rope(_rms(q[:, HD * i:HD * (i + 1)], qg), cos_f, sin_a, sin_b) * (HD ** -0.5) for i in range(A_HEADS)]
    ks = [_rope(_rms(k[:, HD * i:HD * (i + 1)], kg), cos_f, sin_a, sin_b) for i in range(A_HEADS // 2)]
    return jnp.concatenate(qs, 1), jnp.concatenate(ks, 1)


def _tok(tm, w, off):
    return pl.BlockSpec((tm, w), lambda i: (i, off // w))


def _vec(w):
    return pl.BlockSpec((1, w), lambda i: (0, 0))


def _prep_fwd(P, qg, kg, rope, tm, name):
    T = P.shape[0]

    def body(q_ref, k_ref, v_ref, qg_ref, kg_ref, c_ref, sa_ref, sb_ref, qn_ref, kn_ref, vb_ref):
        qn, kn = _prep_fn(q_ref[...].astype(f32), k_ref[...].astype(f32), qg_ref[...], kg_ref[...], c_ref[...], sa_ref[...],
                          sb_ref[...])
        qn_ref[...] = qn.astype(bf16)
        kn_ref[...] = kn.astype(bf16)
        vb_ref[...] = v_ref[...].astype(bf16)

    return pl.pallas_call(
        body, name=name, grid=(T // tm,),
        in_specs=[_tok(tm, 512, 0), _tok(tm, 256, A_K), _tok(tm, 256, A_V), _vec(HD), _vec(HD),
                  _tok(tm, HD, 0), _tok(tm, HD, 0), _tok(tm, HD, 0)],
        out_specs=[_tok(tm, 512, 0), _tok(tm, 256, 0), _tok(tm, 256, 0)],
        out_shape=[jax.ShapeDtypeStruct((T, 512), bf16), jax.ShapeDtypeStruct((T, 256), bf16),
                   jax.ShapeDtypeStruct((T, 256), bf16)],
        compiler_params=_cparams(("parallel",)),
    )(P, P, P, qg, kg, *rope)


def _prep_bwd(P, dqn, dkn, dv, qg, kg, rope, tm, name):
    T = P.shape[0]
    nt = T // tm

    def body(q_ref, k_ref, dq_ref, dk_ref, dv_ref, qg_ref, kg_ref, c_ref, sa_ref, sb_ref, o_ref, og_ref):
        tabs = (c_ref[...], sa_ref[...], sb_ref[...])
        _, vjp = jax.vjp(lambda q, k, a, b: _prep_fn(q, k, a, b, *tabs), q_ref[...].astype(f32), k_ref[...].astype(f32),
                         qg_ref[...], kg_ref[...])
        dq, dk, dqg, dkg = vjp((dq_ref[...], dk_ref[...]))
        o_ref[:, 0:A_K] = dq.astype(bf16)
        o_ref[:, A_K:A_V] = dk.astype(bf16)
        o_ref[:, A_V:W_A] = dv_ref[...].astype(bf16)
        _partial_rows(og_ref, [dqg, dkg])

    return pl.pallas_call(
        body, name=name, grid=(nt,),
        in_specs=[_tok(tm, 512, 0), _tok(tm, 256, A_K), _tok(tm, 512, 0), _tok(tm, 256, 0), _tok(tm, 256, 0),
                  _vec(HD), _vec(HD), _tok(tm, HD, 0), _tok(tm, HD, 0), _tok(tm, HD, 0)],
        out_specs=[_tok(tm, W_A, 0), pl.BlockSpec((None, SUB, HD), lambda i: (i, 0, 0))],
        out_shape=[jax.ShapeDtypeStruct((T, W_A), bf16), jax.ShapeDtypeStruct((nt, SUB, HD), f32)],
        compiler_params=_cparams(("parallel",)),
    )(P, P, dqn, dkn, dv, qg, kg, *rope)


def _attn_fn(q, k, v, lim):
    col = lax.broadcasted_iota(jnp.int32, (1, k.shape[0]), 1)
    s = mm_nt(q, k) + jnp.where(col < lim, 0.0, -1e30)
    m = lax.stop_gradient(jnp.max(s, -1, keepdims=True))
    e = jnp.exp(s - m)
    p = e * (1.0 / jnp.sum(e, -1, keepdims=True))
    return mm(p, v)


def _attn_fwd(qn, kn, vb, tc, tq, name):
    T = qn.shape[0]

    def body(q_ref, k_ref, v_ref, o_ref):
        lim = jnp.where(pl.program_id(1) * tq < tc, tc, T)
        o_ref[...] = _attn_fn(q_ref[...], k_ref[...], v_ref[...], lim)

    return pl.pallas_call(
        body, name=name, grid=(A_HEADS, T // tq),
        in_specs=[pl.BlockSpec((tq, HD), lambda h, i: (i, h)), pl.BlockSpec((T, HD), lambda h, i: (0, h // 2)),
                  pl.BlockSpec((T, HD), lambda h, i: (0, h // 2))],
        out_specs=pl.BlockSpec((tq, HD), lambda h, i: (i, h)),
        out_shape=jax.ShapeDtypeStruct((T, 512), f32),
        compiler_params=_cparams(("parallel", "parallel")),
    )(qn, kn, vb)


def _attn_bwd(qn, kn, vb, dya, tc, tq, name):
    T = qn.shape[0]

    def body(q_ref, k_ref, v_ref, g_ref, dq_ref, dk_ref, dv_ref):
        first = (pl.program_id(1) == 0) & (pl.program_id(2) == 0)
        lim = jnp.where(pl.program_id(2) * tq < tc, tc, T)
        _, vjp = jax.vjp(lambda q, k, v: _attn_fn(q, k, v, lim), q_ref[...].astype(f32), k_ref[...].astype(f32),
                         v_ref[...].astype(f32))
        dq, dk, dv = vjp(g_ref[...])
        dq_ref[...] = dq

        @pl.when(first)
        def _():
            dk_ref[...] = dk
            dv_ref[...] = dv

        @pl.when(jnp.logical_not(first))
        def _():
            dk_ref[...] += dk
            dv_ref[...] += dv

    qspec = pl.BlockSpec((tq, HD), lambda kv, g, i: (i, 2 * kv + g))
    kspec = pl.BlockSpec((T, HD), lambda kv, g, i: (0, kv))
    return pl.pallas_call(
        body, name=name, grid=(A_HEADS // 2, 2, T // tq),
        in_specs=[qspec, kspec, kspec, qspec], out_specs=[qspec, kspec, kspec],
        out_shape=[jax.ShapeDtypeStruct((T, 512), f32), jax.ShapeDtypeStruct((T, 256), f32),
                   jax.ShapeDtypeStruct((T, 256), f32)],
        compiler_params=_cparams(("parallel", "arbitrary", "arbitrary")),
    )(qn, kn, vb, dya)


def _conv_rows(tc, tl):
    return CONV_PAD + tc + CONV_PAD + tl + CONV_PAD


def _fill_pad(pad_ref, val, tc, tl):
    z = jnp.zeros((CONV_PAD, LANE), f32)
    pad_ref[0:CONV_PAD, :] = z
    pad_ref[CONV_PAD:CONV_PAD + tc, :] = val[0:tc]
    pad_ref[CONV_PAD + tc:2 * CONV_PAD + tc, :] = z
    pad_ref[2 * CONV_PAD + tc:2 * CONV_PAD + tc + tl, :] = val[tc:tc + tl]
    pad_ref[2 * CONV_PAD + tc + tl:3 * CONV_PAD + tc + tl, :] = z


def _conv_apply(pad_ref, w_ref, K, tc, tl, rc, emit, flip=False):
    half = K // 2
    for seg0, off, n in ((0, CONV_PAD, tc), (tc, 2 * CONV_PAD + tc, tl)):
        for r0 in range(0, n, rc):
            acc = None
            for k in range(K):
                sh = (half - k) if flip else (k - half)
                term = pad_ref[pl.ds(off + r0 + sh, rc), :] * w_ref[k:k + 1, :]
                acc = term if acc is None else acc + term
            emit(seg0 + r0, acc)


def _conv_wgrad(pad_ref, dy_ref, K, tc, tl, rc, dw_ref):
    half = K // 2
    for k in range(K):
        acc = jnp.zeros((1, LANE), f32)
        for seg0, off, n in ((0, CONV_PAD, tc), (tc, 2 * CONV_PAD + tc, tl)):
            for r0 in range(0, n, rc):
                acc = acc + jnp.sum(pad_ref[pl.ds(off + r0 + k - half, rc), :] * dy_ref[pl.ds(seg0 + r0, rc), :],
                                    axis=0, keepdims=True)
        dw_ref[k:k + 1, :] = acc


def _col(T, off):
    return pl.BlockSpec((T, LANE), lambda j: (0, off // LANE + j))


C_B, C_C, C_X, C_A, C_G = range(5)
N_SEC = 5


class _Sections:
    def __init__(self, refs):
        self.refs = refs

    def __getitem__(self, idx):
        rows, sec = idx
        return self.refs[sec][rows, :].astype(f32)

    def __setitem__(self, idx, val):
        rows, sec = idx
        self.refs[sec, rows, :] = val


def _sec_specs(T):
    return [pl.BlockSpec((T, LANE), functools.partial(lambda j, s: (0, s * (BRW // LANE) + j), s=s)) for s in range(N_SEC)]


def _conv_fwd(P, wb, wd, bd, tc, tl, rc, name):
    T = tc + tl

    def body(*refs):
        p_ref = _Sections(refs[:N_SEC])
        wb_ref, wd_ref, bd_ref, yb_ref, hh_ref, pad_ref = refs[N_SEC:]
        _fill_pad(pad_ref, p_ref[:, C_C] * p_ref[:, C_X], tc, tl)

        def emit_b(r0, y):
            yb_ref[pl.ds(r0, rc), :] = y * p_ref[pl.ds(r0, rc), C_B]

        _conv_apply(pad_ref, wb_ref, KB, tc, tl, rc, emit_b)
        _fill_pad(pad_ref, p_ref[:, C_A] * _sigmoid(p_ref[:, C_G]), tc, tl)

        def emit_d(r0, y):
            hh_ref[pl.ds(r0, rc), :] = y + bd_ref[...]

        _conv_apply(pad_ref, wd_ref, KD, tc, tl, rc, emit_d)

    return pl.pallas_call(
        body, name=name, grid=(BRW // LANE,),
        in_specs=_sec_specs(T) + [pl.BlockSpec((KB, LANE), lambda j: (0, j)), pl.BlockSpec((KD, LANE), lambda j: (0, j)),
                                  pl.BlockSpec((1, LANE), lambda j: (0, j))],
        out_specs=[_col(T, 0), _col(T, 0)],
        out_shape=[jax.ShapeDtypeStruct((T, BRW), f32), jax.ShapeDtypeStruct((T, BRW), f32)],
        scratch_shapes=[pltpu.VMEM((_conv_rows(tc, tl), LANE), f32)],
        compiler_params=_cparams(("parallel",)),
    )(*[P] * N_SEC, wb, wd, bd)


def _conv_bwd(P, dyb, dhh, wb, wd, tc, tl, rc, name):
    T = tc + tl

    def body(*refs):
        p_ref = _Sections(refs[:N_SEC])
        dyb_ref, dhh_ref, wb_ref, wd_ref, dp3_ref, dwb_ref, dwd_ref, dbd_ref, pad_ref, pad2_ref, tmp_ref = refs[N_SEC:]
        dp_ref = _Sections(dp3_ref)
        _fill_pad(pad_ref, p_ref[:, C_C] * p_ref[:, C_X], tc, tl)

        def emit_cv(r0, y):
            dp_ref[pl.ds(r0, rc), C_B] = (y * dyb_ref[pl.ds(r0, rc), :]).astype(bf16)

        _conv_apply(pad_ref, wb_ref, KB, tc, tl, rc, emit_cv)
        tmp_ref[...] = dyb_ref[...] * p_ref[:, C_B]
        _conv_wgrad(pad_ref, tmp_ref, KB, tc, tl, rc, dwb_ref)
        _fill_pad(pad2_ref, tmp_ref[...], tc, tl)

        def emit_ds(r0, y):
            dp_ref[pl.ds(r0, rc), C_C] = (y * p_ref[pl.ds(r0, rc), C_X]).astype(bf16)
            dp_ref[pl.ds(r0, rc), C_X] = (y * p_ref[pl.ds(r0, rc), C_C]).astype(bf16)

        _conv_apply(pad2_ref, wb_ref, KB, tc, tl, rc, emit_ds, flip=True)
        _fill_pad(pad_ref, p_ref[:, C_A] * _sigmoid(p_ref[:, C_G]), tc, tl)
        _conv_wgrad(pad_ref, dhh_ref, KD, tc, tl, rc, dwd_ref)
        dbd_ref[...] = jnp.sum(dhh_ref[...], axis=0, keepdims=True)
        _fill_pad(pad2_ref, dhh_ref[...], tc, tl)

        def emit_d2(r0, y):
            sg = _sigmoid(p_ref[pl.ds(r0, rc), C_G])
            a = p_ref[pl.ds(r0, rc), C_A]
            dp_ref[pl.ds(r0, rc), C_A] = (y * sg).astype(bf16)
            dp_ref[pl.ds(r0, rc), C_G] = (y * a * sg * (1.0 - sg)).astype(bf16)

        _conv_apply(pad2_ref, wd_ref, KD, tc, tl, rc, emit_d2, flip=True)

    return pl.pallas_call(
        body, name=name, grid=(BRW // LANE,),
        in_specs=_sec_specs(T) + [_col(T, 0), _col(T, 0),
                                  pl.BlockSpec((KB, LANE), lambda j: (0, j)), pl.BlockSpec((KD, LANE), lambda j: (0, j))],
        out_specs=[pl.BlockSpec((N_SEC, T, LANE), lambda j: (0, 0, j)), pl.BlockSpec((KB, LANE), lambda j: (0, j)),
                   pl.BlockSpec((KD, LANE), lambda j: (0, j)), pl.BlockSpec((1, LANE), lambda j: (0, j))],
        out_shape=[jax.ShapeDtypeStruct((N_SEC, T, BRW), bf16), jax.ShapeDtypeStruct((KB, BRW), f32),
                   jax.ShapeDtypeStruct((KD, BRW), f32), jax.ShapeDtypeStruct((1, BRW), f32)],
        scratch_shapes=[pltpu.VMEM((_conv_rows(tc, tl), LANE), f32), pltpu.VMEM((_conv_rows(tc, tl), LANE), f32),
                        pltpu.VMEM((T, LANE), f32)],
        compiler_params=_cparams(("parallel",)),
    )(*[P] * N_SEC, dyb, dhh, wb, wd)


def _gla_chunk(q, k, v, r, w2, b2, st, isfwd):
    z = mm(r, w2) + b2
    g = jax.nn.log_sigmoid(z[:, 0:C_KW] if isfwd else z[:, C_KW:2 * C_KW]) / C_TAU
    ri = lax.broadcasted_iota(jnp.int32, (CH, CH), 0)
    ci = lax.broadcasted_iota(jnp.int32, (CH, CH), 1)
    tri = ((ci <= ri) if isfwd else (ci >= ri)).astype(f32)
    cum = jnp.dot(tri, g, preferred_element_type=f32, precision=lax.Precision.HIGHEST)
    last = jnp.sum(g, axis=0, keepdims=True)
    q = q * (C_KW // C_HEADS) ** -0.5
    hv = lax.broadcasted_iota(jnp.int32, (BRW, C_KW), 0) // (BRW // C_HEADS)
    hk = lax.broadcasted_iota(jnp.int32, (BRW, C_KW), 1) // (C_KW // C_HEADS)
    st_new = st * jnp.exp(last) + jnp.where(hv == hk, mm_tn(v, k * jnp.exp(last - cum)), 0.0)
    o = mm_nt(q * jnp.exp(cum), st)
    rowi = lax.broadcasted_iota(jnp.int32, (CH, C_KW), 0)
    srow = lax.broadcasted_iota(jnp.int32, (C_HEADS * CH, C_KW), 0)
    slane = lax.broadcasted_iota(jnp.int32, (C_HEADS * CH, C_KW), 1)
    own_lanes = srow // CH == slane // (C_KW // C_HEADS)
    pos = lax.broadcasted_iota(jnp.int32, (C_HEADS * CH, CH), 0) % CH
    key = lax.broadcasted_iota(jnp.int32, (C_HEADS * CH, CH), 1)
    scores = jnp.zeros((C_HEADS * CH, CH), f32)
    for a in range(CH // GLA_SUB):
        idx = GLA_SUB * a - 1 if isfwd else GLA_SUB * (a + 1)
        ref = jnp.sum(jnp.where(rowi == idx, cum, 0.0), axis=0, keepdims=True)
        qa = q * jnp.exp(jnp.minimum(cum - ref, 0.0))
        ka = k * jnp.exp(jnp.minimum(ref - cum, GLA_CLAMP))
        s = mm_nt(jnp.where(own_lanes, jnp.concatenate([qa] * C_HEADS, axis=0), 0.0), ka)
        scores = scores + jnp.where(pos // GLA_SUB == a, s, 0.0)
    scores = jnp.where((key <= pos) if isfwd else (key >= pos), scores, 0.0)
    vw = BRW // C_HEADS
    o = o + jnp.concatenate([mm(scores[CH * hd:CH * (hd + 1)], v[:, vw * hd:vw * (hd + 1)]) for hd in range(C_HEADS)],
                            axis=1)
    return o, st_new


def _gla_chunk_of(d, n, nc, nch):
    back = jnp.where(n < nc, nc - 1 - n, nch - 1 - (n - nc))
    return jnp.where(d == 0, n, back)


def _gla_fwd(P, w2, b2, tc, name):
    T = P.shape[0]
    nch, nc = T // CH, tc // CH

    back = lambda n: _gla_chunk_of(1, n, nc, nch)

    def body(pf_ref, pb_ref, w_ref, b_ref, of_ref, ob_ref, ssf_ref, ssb_ref, stf_ref, stb_ref):
        @pl.when(pl.program_id(0) == 0)
        def _():
            stf_ref[...] = jnp.zeros_like(stf_ref)
            stb_ref[...] = jnp.zeros_like(stb_ref)

        for p_ref, o_ref, ss_ref, st_ref, isfwd in ((pf_ref, of_ref, ssf_ref, stf_ref, True),
                                                    (pb_ref, ob_ref, ssb_ref, stb_ref, False)):
            st = st_ref[...]
            ss_ref[...] = st
            p = p_ref[...].astype(f32)
            o, st_new = _gla_chunk(p[:, 0:G_K], p[:, G_K:G_V], p[:, G_V:G_R], p[:, G_R:W_G], w_ref[...], b_ref[...], st, isfwd)
            o_ref[...] = o
            st_ref[...] = st_new

    sd = jax.ShapeDtypeStruct
    return pl.pallas_call(
        body, name=name, grid=(nch,),
        in_specs=[pl.BlockSpec((CH, W_G), lambda n: (n, 0)), pl.BlockSpec((CH, W_G), lambda n: (back(n), 0)),
                  pl.BlockSpec((LANE, 512), lambda n: (0, 0)), pl.BlockSpec((1, 512), lambda n: (0, 0))],
        out_specs=[pl.BlockSpec((CH, BRW), lambda n: (n, 0)), pl.BlockSpec((CH, BRW), lambda n: (back(n), 0)),
                   pl.BlockSpec((None, BRW, C_KW), lambda n: (n, 0, 0)), pl.BlockSpec((None, BRW, C_KW), lambda n: (n, 0, 0))],
        out_shape=[sd((T, BRW), f32), sd((T, BRW), f32), sd((nch, BRW, C_KW), f32), sd((nch, BRW, C_KW), f32)],
        scratch_shapes=[pltpu.VMEM((BRW, C_KW), f32), pltpu.VMEM((BRW, C_KW), f32)],
        compiler_params=_cparams(("arbitrary",)),
    )(P, P, w2, b2)


def _gla_bwd(P, w2, b2, ssave, doc, tc, name):
    T = P.shape[0]
    nch, nc = T // CH, tc // CH

    fwd_chunk = lambda m: nch - 1 - m
    back_chunk = lambda m: _gla_chunk_of(1, nch - 1 - m, nc, nch)

    def body(pf_ref, pb_ref, w_ref, b_ref, ssf_ref, ssb_ref, gf_ref, gb_ref, dpf_ref, dpb_ref, dw_ref, db_ref,
             dstf_ref, dstb_ref):
        m = pl.program_id(0)

        @pl.when(m == 0)
        def _():
            dstf_ref[...] = jnp.zeros_like(dstf_ref)
            dstb_ref[...] = jnp.zeros_like(dstb_ref)

        dw_sum, db_sum = None, None
        for p_ref, ss_ref, g_ref, dp_ref, dst_ref, isfwd in ((pf_ref, ssf_ref, gf_ref, dpf_ref, dstf_ref, True),
                                                             (pb_ref, ssb_ref, gb_ref, dpb_ref, dstb_ref, False)):
            p = p_ref[...].astype(f32)
            _, vjp = jax.vjp(lambda q, k, v, r, w, b, st: _gla_chunk(q, k, v, r, w, b, st, isfwd),
                             p[:, 0:G_K], p[:, G_K:G_V], p[:, G_V:G_R], p[:, G_R:W_G], w_ref[...], b_ref[...], ss_ref[...])
            dq, dk, dv, dr, dw, db, dst = vjp((g_ref[...], dst_ref[...]))
            dp_ref[:, 0:G_K] = dq
            dp_ref[:, G_K:G_V] = dk
            dp_ref[:, G_V:G_R] = dv
            dp_ref[:, G_R:W_G] = dr
            dst_ref[...] = dst
            dw_sum = dw if dw_sum is None else dw_sum + dw
            db_sum = db if db_sum is None else db_sum + db

        @pl.when(m == 0)
        def _():
            dw_ref[...] = dw_sum
            _partial_rows(db_ref, [db_sum])

        @pl.when(m > 0)
        def _():
            dw_ref[...] += dw_sum
            db_ref[0:1, :] += db_sum

    ssf, ssb = ssave
    chunk_f = lambda w: pl.BlockSpec((CH, w), lambda m: (fwd_chunk(m), 0))
    chunk_b = lambda w: pl.BlockSpec((CH, w), lambda m: (back_chunk(m), 0))
    state = pl.BlockSpec((None, BRW, C_KW), lambda m: (nch - 1 - m, 0, 0))
    sd = jax.ShapeDtypeStruct
    return pl.pallas_call(
        body, name=name, grid=(nch,),
        in_specs=[chunk_f(W_G), chunk_b(W_G), pl.BlockSpec((LANE, 512), lambda m: (0, 0)), pl.BlockSpec((1, 512), lambda m: (0, 0)),
                  state, state, chunk_f(BRW), chunk_b(BRW)],
        out_specs=[chunk_f(W_G), chunk_b(W_G), pl.BlockSpec((LANE, 512), lambda m: (0, 0)), pl.BlockSpec((SUB, 512), lambda m: (0, 0))],
        out_shape=[sd((T, W_G), f32), sd((T, W_G), f32), sd((LANE, 512), f32), sd((SUB, 512), f32)],
        scratch_shapes=[pltpu.VMEM((BRW, C_KW), f32), pltpu.VMEM((BRW, C_KW), f32)],
        compiler_params=_cparams(("arbitrary",)),
    )(P, P, w2, b2, ssf, ssb, doc, doc)


def _sum_dirs(a, b, tm, name):
    T, W = a.shape

    def body(a_ref, b_ref, o_ref):
        o_ref[...] = (a_ref[...] + b_ref[...]).astype(bf16)

    spec = pl.BlockSpec((tm, W), lambda i: (i, 0))
    return pl.pallas_call(
        body, name=name, grid=(T // tm,), in_specs=[spec, spec], out_specs=spec,
        out_shape=jax.ShapeDtypeStruct((T, W), bf16),
        compiler_params=_cparams(("parallel",)),
    )(a, b)


def _merge_fn(h, m_l, m_c, isctx, ya, ga, yb, gb, of, ob, gc, hh, gd, mg, es, ey, cn, dng, dnb, lg, lb, wbr, wout):
    oc = of + ob
    yc = jnp.concatenate([_rms(oc[:, HD * i:HD * (i + 1)], cn[:, HD * i:HD * (i + 1)]) for i in range(C_HEADS)], 1)
    brs = [ya * _silu(ga), yb * _silu(gb), yc * _silu(gc), _silu(_ln(hh) * dng + dnb) * _silu(gd)]
    acc = None
    for i in range(4):
        t = _sigmoid(mg[:, D * i:D * (i + 1)]) * (mm(brs[i], wbr[i]) + es[i])
        acc = t if acc is None else acc + t
    y = mm(acc, wout) + ey
    gate = jnp.where(isctx, m_c[:, 2 * D:3 * D], m_l[:, 2 * D:3 * D])
    hn = _ln(ALPHA * h + gate * y) * lg + lb
    return hn, (brs, acc)


def _merge_specs(tm):
    t = lambda w, off=0: _tok(tm, w, off)
    return [t(D), pl.BlockSpec((SUB, 3 * D), lambda i: (0, 0)),
            t(BRW), t(BRW, M_GA), t(BRW), t(BRW, M_GB),
            t(BRW), t(BRW),
            t(BRW, M_GC), t(BRW), t(BRW, M_GD), t(4 * D, 0),
            _vec(BRW), _vec(BRW), _vec(BRW), _vec(D), _vec(D),
            pl.BlockSpec((4, BRW, D), lambda i: (0, 0, 0)), pl.BlockSpec((D, D), lambda i: (0, 0))]


def _merge_fwd(h, modv_l, ya, yb, o2, hh, P, cn, dng, dnb, lg, lb, wbr, wout, tc, tm, name):
    T = h.shape[0]

    def body(h_ref, m_ref, ya_ref, ga_ref, yb_ref, gb_ref, of_ref, ob_ref, gc_ref, hh_ref, gd_ref, mg_ref,
             cn_ref, dng_ref, dnb_ref, lg_ref, lb_ref, wbr_ref, wout_ref, o_ref):
        isctx = _row_ids(pl.program_id(0), tm) < tc
        zero = jnp.zeros((tm, D), f32)
        up = lambda r: r[...].astype(f32)
        hn, _ = _merge_fn(h_ref[...], m_ref[0:1, :], m_ref[1:2, :], isctx, ya_ref[...], up(ga_ref), yb_ref[...],
                          up(gb_ref), of_ref[...], ob_ref[...], up(gc_ref), hh_ref[...], up(gd_ref), up(mg_ref),
                          [zero] * 4, zero, cn_ref[...], dng_ref[...], dnb_ref[...], lg_ref[...], lb_ref[...],
                          [wbr_ref[i] for i in range(4)], wout_ref[...])
        o_ref[...] = hn

    return pl.pallas_call(
        body, name=name, grid=(T // tm,),
        in_specs=_merge_specs(tm), out_specs=_tok(tm, D, 0),
        out_shape=jax.ShapeDtypeStruct((T, D), f32),
        compiler_params=_cparams(("parallel",)),
    )(h, modv_l, ya, P, yb, P, o2[0], o2[1], P, hh, P, P, cn, dng, dnb, lg, lb, wbr, wout)


def _merge_bwd(dhn, h, modv_l, ya, yb, o2, hh, P, cn, dng, dnb, lg, lb, wbr, wout, tc, tm, name):
    T = h.shape[0]
    nt = T // tm

    def body(g_ref, h_ref, m_ref, ya_ref, ga_ref, yb_ref, gb_ref, of_ref, ob_ref, gc_ref, hh_ref, gd_ref, mg_ref,
             cn_ref, dng_ref, dnb_ref, lg_ref, lb_ref, wbr_ref, wout_ref,
             dh_ref, dm_ref, dya_ref, dyb_ref, doc_ref, dhh_ref, dp_ref,
             br_ref, z_ref, acc_ref, dy_ref, dv5_ref, dvd_ref):
        isctx = _row_ids(pl.program_id(0), tm) < tc
        zero = jnp.zeros((tm, D), f32)
        wbr_v = [wbr_ref[i] for i in range(4)]
        wout_v = wout_ref[...]
        up = lambda r: r[...].astype(f32)

        def fn(h, ml, mc, ya, ga, yb, gb, oc, gc, hh, gd, mg, e0, e1, e2, e3, ey, cn, dng, dnb, lg, lb):
            return _merge_fn(h, ml, mc, isctx, ya, ga, yb, gb, oc, jnp.zeros_like(oc), gc, hh, gd, mg,
                             [e0, e1, e2, e3], ey, cn, dng, dnb, lg, lb, wbr_v, wout_v)

        _, vjp, (brs, acc) = jax.vjp(
            fn, h_ref[...], m_ref[0:1, :], m_ref[1:2, :], ya_ref[...], up(ga_ref), yb_ref[...], up(gb_ref),
            of_ref[...] + ob_ref[...], up(gc_ref), hh_ref[...], up(gd_ref), up(mg_ref), zero, zero, zero, zero, zero,
            cn_ref[...], dng_ref[...], dnb_ref[...], lg_ref[...], lb_ref[...], has_aux=True)
        (dh, dml, dmc, dya, dga, dyb, dgb, doc, dgc, dhh, dgd, dmg, z0, z1, z2, z3, dy,
         dcn, ddng, ddnb, dlg, dlb) = vjp(g_ref[...])
        dh_ref[...] = dh
        _partial_rows(dm_ref, [dml, dmc])
        dya_ref[...] = dya
        dyb_ref[...] = dyb
        doc_ref[...] = doc
        dhh_ref[...] = dhh
        dp_ref[:, 0:M_GA] = dmg.astype(bf16)
        dp_ref[:, M_GA:M_GB] = dga.astype(bf16)
        dp_ref[:, M_GB:M_GC] = dgb.astype(bf16)
        dp_ref[:, M_GC:M_GD] = dgc.astype(bf16)
        dp_ref[:, M_GD:W_M] = dgd.astype(bf16)
        for i, z in enumerate((z0, z1, z2, z3)):
            br_ref[i] = brs[i].astype(bf16)
            z_ref[i] = z.astype(bf16)
        acc_ref[...] = acc.astype(bf16)
        dy_ref[...] = dy.astype(bf16)
        _partial_rows(dv5_ref, [dcn, ddng, ddnb])
        _partial_rows(dvd_ref, [dlg, dlb])

    t = lambda w: _tok(tm, w, 0)
    part = lambda w: pl.BlockSpec((None, SUB, w), lambda i: (i, 0, 0))
    sd = jax.ShapeDtypeStruct
    return pl.pallas_call(
        body, name=name, grid=(nt,),
        in_specs=[t(D)] + _merge_specs(tm),
        out_specs=[t(D), part(3 * D)] + [t(BRW)] * 4 + [t(W_M),
                   pl.BlockSpec((4, tm, BRW), lambda i: (0, i, 0)), pl.BlockSpec((4, tm, D), lambda i: (0, i, 0)),
                   t(D), t(D), part(BRW), part(D)],
        out_shape=[sd((T, D), f32), sd((nt, SUB, 3 * D), f32)] + [sd((T, BRW), f32)] * 4 + [sd((T, W_M), bf16),
                   sd((4, T, BRW), bf16), sd((4, T, D), bf16), sd((T, D), bf16), sd((T, D), bf16),
                   sd((nt, SUB, BRW), f32), sd((nt, SUB, D), f32)],
        compiler_params=_cparams(("parallel",)),
    )(dhn, h, modv_l, ya, P, yb, P, o2[0], o2[1], P, hh, P, P, cn, dng, dnb, lg, lb, wbr, wout)


def _loss_kernel(h, tgt, tc, tm, name):
    T = h.shape[0]
    nt = T // tm
    nct = tc // tm

    def body(h_ref, t_ref, d_ref, l_ref):
        i = pl.program_id(0)
        err = h_ref[...] - t_ref[...]
        lat = (i >= nct).astype(f32)
        d_ref[...] = err * (lat / D)
        l_ref[...] = jnp.zeros((SUB, LANE), f32) + lat * 0.5 * jnp.sum(err * err) / D

    return pl.pallas_call(
        body, name=name, grid=(nt,),
        in_specs=[pl.BlockSpec((tm, D), lambda i: (i, 0)),
                  pl.BlockSpec((tm, D), lambda i: (jnp.maximum(i - nct, 0), 0))],
        out_specs=[pl.BlockSpec((tm, D), lambda i: (i, 0)), pl.BlockSpec((None, SUB, LANE), lambda i: (i, 0, 0))],
        out_shape=[jax.ShapeDtypeStruct((T, D), f32), jax.ShapeDtypeStruct((nt, SUB, LANE), f32)],
        compiler_params=_cparams(("parallel",)),
    )(h, tgt)


def _rope_tables(tc, tl):
    t = jnp.arange(tl)
    inv = ROPE_THETA ** (-jnp.arange(0, HD // 2, 2, dtype=f32) / (HD // 2))
    ang = jnp.concatenate([(t // GRID_W).astype(f32)[:, None] * inv, (t % GRID_W).astype(f32)[:, None] * inv], -1)
    cos, sin = jnp.repeat(jnp.cos(ang), 2, axis=1), jnp.repeat(jnp.sin(ang), 2, axis=1)
    even = (jnp.arange(HD) % 2 == 0)[None, :]
    cos_f = jnp.concatenate([jnp.ones((tc, HD), f32), cos], 0)
    sin_a = jnp.concatenate([jnp.zeros((tc, HD), f32), jnp.where(even, -sin, 0.0)], 0)
    sin_b = jnp.concatenate([jnp.zeros((tc, HD), f32), jnp.where(even, 0.0, sin)], 0)
    return cos_f, sin_a, sin_b


N_CHIPS = 4
SHARD = N_IN // N_CHIPS


def _group_ranges():
    return dict(M=[(S_MG, 4 * D), (S_GA, BRW), (S_GB, BRW), (S_GC, BRW), (S_GD, BRW)], A=[(S_Q, W_A)],
                C=[(S_B, 3 * BRW), (S_DA, 2 * BRW)], G=[(S_CQ, 2 * C_KW + BRW), (S_R, 2 * C_RANK)])


def _group_weights(w4):
    out = {}
    for k, ranges in _group_ranges().items():
        parts = []
        for a, n in ranges:
            while n > 0:
                s, r = divmod(a, SHARD)
                m = min(n, SHARD - r)
                parts.append(w4[s, r:r + m])
                a, n = a + m, n - m
        if k == "G":
            parts.append(jnp.zeros((LANE - 2 * C_RANK, D), w4.dtype))
        out[k] = jnp.concatenate(parts, 0)
    return out


def _ungroup(g):
    secs = []
    for k, ranges in _group_ranges().items():
        off = 0
        for a, n in ranges:
            secs.append((a, g[k][off:off + n]))
            off += n
    return jnp.concatenate([v for _, v in sorted(secs, key=lambda t: t[0])], 0)


PROJ_TN = dict(M=2048, A=1024, C=1280, G=1152)
DU_TK = dict(M=2048, A=1024, C=BRW, G=1152)
DWP_TN = dict(M=768, A=1024, C=BRW, G=1152)


def _gate_weights(w2_l, gb_l):
    w = jnp.zeros((LANE, 2 * C_KW), f32)
    w = w.at[0:C_RANK, 0:C_KW].set(w2_l[0]).at[C_RANK:2 * C_RANK, C_KW:2 * C_KW].set(w2_l[1])
    return w, jnp.concatenate([gb_l[0], gb_l[1]])[None, :]


def _local_step(x1, c1, ctx1, tgt1, c_ctx, b_mod, weights_of, q_norm, k_norm, b_conv, w2, gb, c_norm, d_conv_w,
                d_conv_b, d_norm_g, d_norm_b, grads_done, ln_g, ln_b, tm, token=None):
    tc, tl = ctx1.shape[0], x1.shape[0]
    T = tc + tl
    rc = min(256, tc)
    tmb = tm // 2
    tmm = 768 if T % 768 == 0 else tm
    rope = _rope_tables(tc, tl)
    cin = jnp.concatenate([c1, c_ctx[None, :], jnp.zeros((SUB - 2, D), f32)], 0)
    if token is not None:
        cin = cin + token[:, 0:1]
    row = lambda v: v[None, :]

    h = jnp.concatenate([ctx1, x1], 0)
    saved, wp, w_br, w_out, w_mod, modv = [], *([None] * DEPTH for _ in range(5))
    for l in range(DEPTH):
        wp[l], merge_weights, w_mod[l] = weights_of(l, h)
        modv[l] = _mod_fwd(cin, w_mod[l], b_mod[l], f"mod_fwd{l}")
        u = _ln_fwd(h, modv[l], tc, tm, f"ln_fwd{l}")
        P = {k: _matmul(u, wp[l][k], "nt", tmm, PROJ_TN[k], D, f"proj{l}{k}", out_dtype=bf16) for k in GROUPS}
        qn, kn, vb = _prep_fwd(P["A"], row(q_norm[l]), row(k_norm[l]), rope, tm, f"prep_fwd{l}")
        ya = _attn_fwd(qn, kn, vb, tc, tm, f"attn_fwd{l}")
        yb, hh = _conv_fwd(P["C"], b_conv[l], d_conv_w[l], row(d_conv_b[l]), tc, tl, rc, f"conv_fwd{l}")
        w2p, b2p = _gate_weights(w2[l], gb[l])
        gla = _gla_fwd(P["G"], w2p, b2p, tc, f"gla_fwd{l}")
        o2, ssave = gla[:2], gla[2:]
        w_br[l], w_out[l] = merge_weights(o2[0])
        hn = _merge_fwd(h, modv[l], ya, yb, o2, hh, P["M"], row(c_norm[l]), row(d_norm_g[l]), row(d_norm_b[l]),
                        row(ln_g[l]), row(ln_b[l]), w_br[l], w_out[l], tc, tm, f"merge_fwd{l}")
        saved.append((h, u, P, qn, kn, vb, ya, yb, hh, o2, ssave, w2p, b2p))
        h = hn

    dh, lparts = _loss_kernel(h, tgt1, tc, tm, "loss")
    loss = jnp.sum(lparts[:, 0, 0])

    g = {k: [None] * DEPTH for k in ("wp", "q_norm", "k_norm", "b_conv", "w2", "gb", "c_norm", "d_conv_w", "d_conv_b",
                                     "d_norm_g", "d_norm_b", "w_br", "w_out", "ln_g", "ln_b", "modv")}
    for l in reversed(range(DEPTH)):
        h_in, u, P, qn, kn, vb, ya, yb, hh, o2, ssave, w2p, b2p = saved[l]
        dP = {}
        (dh_res, dm_mg, dya, dyb, doc, dhh, dP["M"], br, z, acc, dy, dv5, dvd) = _merge_bwd(
            dh, h_in, modv[l], ya, yb, o2, hh, P["M"], row(c_norm[l]), row(d_norm_g[l]), row(d_norm_b[l]),
            row(ln_g[l]), row(ln_b[l]), w_br[l], w_out[l], tc, tmb, f"merge_bwd{l}")
        g["w_br"][l] = _matmul_tn_batched(br, z, N_CHIPS, f"dwbr{l}")
        g["w_out"][l] = _matmul(acc, dy, "tn", D, D, T, f"dwout{l}", out_dtype=bf16)
        tk = grads_done(l, {k: g[k][l] for k in ("w_br", "w_out")})
        qg_l = row(q_norm[l]) if tk is None else row(q_norm[l]) + tk[0:1, :]
        v5 = jnp.sum(dv5, 0)
        g["c_norm"][l], g["d_norm_g"][l], g["d_norm_b"][l] = v5[0], v5[1], v5[2]
        vd = jnp.sum(dvd, 0)
        g["ln_g"][l], g["ln_b"][l] = vd[0], vd[1]
        dqn, dkn, dv = _attn_bwd(qn, kn, vb, dya, tc, tm, f"attn_bwd{l}")
        dP["A"], dqk = _prep_bwd(P["A"], dqn, dkn, dv, qg_l, row(k_norm[l]), rope, tm, f"prep_bwd{l}")
        dqk = jnp.sum(dqk, 0)
        g["q_norm"][l], g["k_norm"][l] = dqk[0], dqk[1]
        dP["C"], dwb, dwd, dbd = _conv_bwd(P["C"], dyb, dhh, b_conv[l], d_conv_w[l], tc, tl, rc, f"conv_bwd{l}")
        g["b_conv"][l], g["d_conv_w"][l], g["d_conv_b"][l] = dwb, dwd, dbd[0]
        dpf, dpb, dw2p, db2p = _gla_bwd(P["G"], w2p, b2p, ssave, doc, tc, f"gla_bwd{l}")
        dP["G"] = _sum_dirs(dpf, dpb, tm, f"gla_sum{l}")
        db2p = db2p[0]
        g["w2"][l] = jnp.stack([dw2p[0:C_RANK, 0:C_KW], dw2p[C_RANK:2 * C_RANK, C_KW:2 * C_KW]])
        g["gb"][l] = jnp.stack([db2p[0:C_KW], db2p[C_KW:2 * C_KW]])
        g["wp"][l] = {k: _matmul(dP[k], u, "tn", DWP_TN[k], D, T, f"dwp{l}{k}", out_dtype=bf16) for k in GROUPS}
        tk = grads_done(l, {"wp": g["wp"][l]})
        du = _matmul_groups(dP, wp[l], DU_TK, tmm, f"du{l}", after=tk)
        dh, dm_ln = _ln_bwd(du, h_in, dh_res, modv[l], tc, tm, f"ln_bwd{l}", latent_only=(l == 0))
        g["modv"][l] = jnp.sum(dm_mg, 0) + jnp.sum(dm_ln, 0)

    dmodv = jnp.stack(g.pop("modv"))
    g["w_mod"], dcin = _mod_bwd(cin, jnp.stack(w_mod, axis=1), dmodv)
    g["b_mod"] = dmodv[:, 0, :] + dmodv[:, 1, :]
    g["c_ctx"] = jnp.sum(dcin, (0, 1))[1]
    return loss, dh, g


HALF_TL = 256


TILE_BYTES = 1 << 20


def _row_tile(rows, cols, itemsize=4):
    tr = min(rows, 128)
    while rows % (2 * tr) == 0 and 2 * tr * cols * itemsize <= TILE_BYTES:
        tr *= 2
    return tr


def _adamw(w, g, m, v, name, tr=None, after=None):
    L, R, C = w.shape
    tr = _row_tile(R, C) if tr is None else tr
    if R % tr == 0:
        grid, spec = (L, R // tr), pl.BlockSpec((None, tr, C), lambda l, i: (l, i, 0))
    elif R * C * 4 <= (1 << 20):
        grid, spec = (L, 1), pl.BlockSpec((None, R, C), lambda l, i: (l, 0, 0))
    else:
        grid, spec = (L, C // HALF_TL), pl.BlockSpec((None, R, HALF_TL), lambda l, i: (l, 0, i))

    def body(w_ref, g_ref, m_ref, v_ref, *rest):
        go_ref, d_ref, nm_ref, nv_ref = rest[-4:]
        gg = g_ref[...]
        go_ref[...] = gg
        nm = B1 * m_ref[...] + (1.0 - B1) * gg
        nv = B2 * v_ref[...] + (1.0 - B2) * (gg * gg)
        m_hat = nm / (1.0 - B1 ** STEP)
        v_hat = nv / (1.0 - B2 ** STEP)
        d_ref[...] = -LR * (m_hat / (jnp.sqrt(v_hat) + AEPS) + WD * w_ref[...])
        nm_ref[...] = nm
        nv_ref[...] = nv

    return pl.pallas_call(
        body, name=name, grid=grid, in_specs=[spec] * 4 + ([] if after is None else [pl.BlockSpec(memory_space=pl.ANY)]),
        out_specs=[spec] * 4, out_shape=[jax.ShapeDtypeStruct((L, R, C), f32)] * 4,
        compiler_params=_cparams(("parallel", "parallel")),
    )(w, g, m, v, *([] if after is None else [after]))


MESH = pl.DeviceIdType.MESH
ANY = pl.BlockSpec(memory_space=pl.ANY)


def _place():
    x, y, c = lax.axis_index("x"), lax.axis_index("y"), lax.axis_index("c")
    chips = [(1 - x, y), (x, 1 - y), (1 - x, 1 - y)]
    return x, y, c, chips


def _half(ref, c, axis):
    n = ref.shape[axis] // 2
    last = axis in (-1, ref.ndim - 1)
    idx = [slice(None)] * ref.ndim
    idx[axis] = pl.ds(pl.multiple_of(c * n, LANE if last else SUB), n)
    return ref.at[tuple(idx)]


def _half_shape(shape, axis):
    s = list(shape)
    s[axis] //= 2
    return tuple(s)


def _all_gather(arrs, axes, name):
    n = len(arrs)

    def body(*refs):
        ins, outs = refs[:n], refs[n:2 * n]
        send, recv = refs[2 * n:]
        x, y, c, chips = _place()
        me, sib = 2 * x + y, (x, y, 1 - c)

        def copy(a, k, chip_idx, cc, to, src=None):
            blk = _half(outs[a].at[chip_idx], cc, axes[a])
            return pltpu.make_async_remote_copy(src_ref=blk if src is None else src, dst_ref=blk,
                                                send_sem=send.at[7 * a + k], recv_sem=recv.at[7 * a + k],
                                                device_id=to, device_id_type=MESH)

        own = [pltpu.make_async_remote_copy(src_ref=ins[a], dst_ref=outs[a].at[me], send_sem=send.at[7 * a + 6],
                                            recv_sem=recv.at[7 * a + 6], device_id=sib, device_id_type=MESH)
               for a in range(n)]
        first = own + [copy(a, j, me, c, (*chip, c), src=_half(ins[a], c, axes[a]))
                       for a in range(n) for j, chip in enumerate(chips)]
        for cp in first:
            cp.start()
        passed = []
        for a in range(n):
            for j, chip in enumerate(chips):
                k = 2 * chip[0] + chip[1]
                copy(a, j, k, c, sib).wait_recv()
                fwd = copy(a, 3 + j, k, c, sib)
                fwd.start()
                passed.append(fwd)
        for a in range(n):
            own[a].wait_recv()
            for j, chip in enumerate(chips):
                copy(a, 3 + j, 2 * chip[0] + chip[1], 1 - c, sib).wait_recv()
        for cp in first + passed:
            cp.wait_send()

    return pl.pallas_call(
        body, name=name, in_specs=[ANY] * n, out_specs=[ANY] * n,
        out_shape=[jax.ShapeDtypeStruct((N_CHIPS,) + a.shape, a.dtype) for a in arrs],
        scratch_shapes=[pltpu.SemaphoreType.DMA((7 * n,)), pltpu.SemaphoreType.DMA((7 * n,))],
    )(*arrs)


def _sibling_halves(arrs, axes, name):
    n = len(arrs)

    def body(*refs):
        ins, outs = refs[:n], refs[n:2 * n]
        send, recv = refs[2 * n:]
        x, y, c, _ = _place()
        cps = [pltpu.make_async_remote_copy(src_ref=_half(ins[a], 1 - c, axes[a] + 1), dst_ref=outs[a], send_sem=send.at[a],
                                            recv_sem=recv.at[a], device_id=(x, y, 1 - c), device_id_type=MESH)
               for a in range(n)]
        for cp in cps:
            cp.start()
        for cp in cps:
            cp.wait()

    return pl.pallas_call(
        body, name=name, in_specs=[ANY] * n, out_specs=[ANY] * n,
        out_shape=[jax.ShapeDtypeStruct(_half_shape(a.shape, axes[i] + 1), a.dtype) for i, a in enumerate(arrs)],
        scratch_shapes=[pltpu.SemaphoreType.DMA((n,)), pltpu.SemaphoreType.DMA((n,))],
    )(*arrs)


def _add_half(gfull, land, cidx, axis, name, tr=None, out_dtype=bf16):
    _, hr, hc = land.shape
    if axis == 0:
        tr = min(tr, hr) if tr else _row_tile(hr, hc)
        nb, blk = hr // tr, (None, tr, hc)
        g_spec = pl.BlockSpec(blk, lambda s, i, cr: (s, cr[0] * nb + i, 0))
        l_spec = pl.BlockSpec(blk, lambda s, i, cr: (s, i, 0))
    else:
        nb, blk = hc // HALF_TL, (None, hr, HALF_TL)
        g_spec = pl.BlockSpec(blk, lambda s, i, cr: (s, 0, cr[0] * nb + i))
        l_spec = pl.BlockSpec(blk, lambda s, i, cr: (s, 0, i))

    def body(c_ref, g_ref, l_ref, o_ref):
        o_ref[...] = (g_ref[...].astype(f32) + l_ref[...].astype(f32)).astype(o_ref.dtype)

    return pl.pallas_call(
        body, name=name,
        grid_spec=pltpu.PrefetchScalarGridSpec(
            num_scalar_prefetch=1, grid=(N_CHIPS, nb), in_specs=[g_spec, l_spec], out_specs=l_spec),
        out_shape=jax.ShapeDtypeStruct((N_CHIPS, hr, hc), out_dtype),
        compiler_params=_cparams(("parallel", "parallel")),
    )(cidx, gfull, land)


def _sum_chips(land, own, place, axis, layer, into, name, tr=None):
    _, hr, hc = land.shape
    fresh = not hasattr(into, "dtype")
    shape = tuple(into) if fresh else into.shape
    if axis == 0:
        tr = min(tr, hr) if tr else _row_tile(hr, 4 * hc, 2)
        nb, blk = hr // tr, (tr, hc)
        l_map, m_map = (lambda i, p: (0, i, 0)), (lambda i, p: (p[0], i, 0))
        o_map = lambda i, p: (layer, p[1] * nb + i, 0)
    else:
        nb, blk = hc // HALF_TL, (hr, HALF_TL)
        l_map, m_map = (lambda i, p: (0, 0, i)), (lambda i, p: (p[0], 0, i))
        o_map = lambda i, p: (layer, 0, p[1] * nb + i)

    def body(p_ref, l_ref, o_ref, *rest):
        me = p_ref[0]
        mine = o_ref[...].astype(f32)
        acc = None
        for k in range(N_CHIPS):
            t = jnp.where(me == k, mine, l_ref[k].astype(f32))
            acc = t if acc is None else acc + t
        rest[-1][...] = acc

    return pl.pallas_call(
        body, name=name,
        grid_spec=pltpu.PrefetchScalarGridSpec(
            num_scalar_prefetch=1, grid=(nb,),
            in_specs=[pl.BlockSpec((N_CHIPS,) + blk, l_map), pl.BlockSpec((None,) + blk, m_map)] + ([] if fresh else [ANY]),
            out_specs=pl.BlockSpec((None,) + blk, o_map)),
        out_shape=jax.ShapeDtypeStruct(shape, f32),
        input_output_aliases={} if fresh else {3: 0},
        compiler_params=_cparams(("parallel",)),
    )(place, land, own, *([] if fresh else [into]))


def _sibling_fill(arrs, axes, name):
    n = len(arrs)

    def body(*refs):
        outs = refs[n:2 * n]
        send, recv = refs[2 * n:]
        x, y, c, _ = _place()
        cps = [pltpu.make_async_remote_copy(src_ref=_half(outs[a], c, axes[a] + 1), dst_ref=_half(outs[a], c, axes[a] + 1),
                                            send_sem=send.at[a], recv_sem=recv.at[a], device_id=(x, y, 1 - c),
                                            device_id_type=MESH) for a in range(n)]
        for cp in cps:
            cp.start()
        for a in range(n):
            blk = _half(outs[a], 1 - c, axes[a] + 1)
            pltpu.make_async_remote_copy(src_ref=blk, dst_ref=blk, send_sem=send.at[a], recv_sem=recv.at[a],
                                         device_id=(x, y, 1 - c), device_id_type=MESH).wait_recv()
        for cp in cps:
            cp.wait_send()

    return pl.pallas_call(
        body, name=name, in_specs=[ANY] * n, out_specs=[ANY] * n,
        out_shape=[jax.ShapeDtypeStruct(a.shape, a.dtype) for a in arrs],
        input_output_aliases={a: a for a in range(n)},
        scratch_shapes=[pltpu.SemaphoreType.DMA((n,)), pltpu.SemaphoreType.DMA((n,))],
    )(*arrs)


HBM = pl.BlockSpec(memory_space=pltpu.HBM)
SEM = pl.BlockSpec(memory_space=pltpu.SEMAPHORE)
EFFECT = pltpu.SideEffectType.DATAFLOW_SIDE_EFFECTING
PEERS = 4


def _split_copies(srcs, lands, send, recv, gather, axes=None):
    x, y, c, chips = _place()
    me = 2 * x + y
    if axes is not None:
        out = []
        for a in range(len(srcs)):
            sems = dict(send_sem=send.at[PEERS * a], recv_sem=recv.at[PEERS * a], device_id=(x, y, 1 - c), device_id_type=MESH)
            copy = pltpu.make_async_remote_copy(src_ref=_half(srcs[a], 1 - c, axes[a] + 1), dst_ref=lands[a], **sems)
            out.append((copy, copy))
        return out
    peers = [((*chip, c), 2 * chip[0] + chip[1]) for chip in chips] + ([((x, y, 1 - c), me)] if gather else [])
    out = []
    for a in range(len(srcs)):
        for j, (dev, k) in enumerate(peers):
            src = srcs[a] if gather else srcs[a].at[k]
            sems = dict(send_sem=send.at[PEERS * a + j], recv_sem=recv.at[PEERS * a + j], device_id=dev, device_id_type=MESH)
            out.append((pltpu.make_async_remote_copy(src_ref=src, dst_ref=lands[a].at[me], **sems),
                        pltpu.make_async_remote_copy(src_ref=src, dst_ref=lands[a].at[k], **sems)))
    return out


def _split_start(srcs, gather, after, name, axes=None):
    n = len(srcs)
    if axes is not None:
        lands = [lax.empty(_half_shape(s.shape, axes[a] + 1), s.dtype) for a, s in enumerate(srcs)]
    else:
        lands = [lax.empty(((N_CHIPS,) + s.shape) if gather else s.shape, s.dtype) for s in srcs]

    def body(*refs):
        send, recv = refs[2 * n + 1], refs[2 * n + 2]
        for start, _ in _split_copies(refs[:n], refs[n:2 * n], send, recv, gather, axes):
            start.start()
        refs[-1][...] = jnp.zeros_like(refs[-1])

    sems = pltpu.SemaphoreType.DMA((PEERS * n,))
    hbm = lambda a: pltpu.with_memory_space_constraint(a, pltpu.HBM)
    out = pl.pallas_call(
        body, name=name,
        out_shape=(sems, sems, *[pltpu.HBM(a.shape, a.dtype) for a in srcs + lands], jax.ShapeDtypeStruct((SUB, LANE), f32)),
        in_specs=[HBM] * (2 * n) + [ANY], out_specs=(SEM, SEM, *[HBM] * (2 * n), pl.BlockSpec(memory_space=pltpu.VMEM)),
        input_output_aliases={i: 2 + i for i in range(2 * n)},
        compiler_params=pltpu.CompilerParams(has_side_effects=EFFECT),
    )(*[hbm(a) for a in srcs + lands], after)
    return out[0], out[1], list(out[2:2 + n]), list(out[2 + n:2 + 2 * n]), out[-1]


def _split_wait(send, recv, srcs, lands, gather, after, name, axes=None):
    n = len(srcs)

    def body(*refs):
        for start, arrival in _split_copies(refs[:n], refs[n:2 * n], refs[2 * n], refs[2 * n + 1], gather, axes):
            start.wait_send()
            arrival.wait_recv()

    out = pl.pallas_call(
        body, name=name, out_shape=[pltpu.HBM(a.shape, a.dtype) for a in srcs + lands],
        in_specs=[HBM] * (2 * n) + [SEM, SEM, ANY], out_specs=[HBM] * (2 * n),
        input_output_aliases={i: i for i in range(2 * n)},
        compiler_params=pltpu.CompilerParams(has_side_effects=EFFECT),
    )(*srcs, *lands, send, recv, after)
    return list(out[:n]), list(out[n:])


N_DEV = 8


def _all_reduce_small(v, name):
    R = v.shape[0]

    def body(v_ref, o_ref, land_ref, send, recv):
        x, y, c, _ = _place()
        me = 4 * x + 2 * y + c
        land_ref[me] = v_ref[...]
        cps = []
        for m in range(1, N_DEV):
            px, py, pc = [(1 - q) if (m >> s) & 1 else q for q, s in ((x, 2), (y, 1), (c, 0))]
            cps.append((pltpu.make_async_remote_copy(src_ref=v_ref, dst_ref=land_ref.at[me], send_sem=send.at[m - 1],
                                                     recv_sem=recv.at[m - 1], device_id=(px, py, pc), device_id_type=MESH),
                        4 * px + 2 * py + pc, m))
        for cp, *_ in cps:
            cp.start()
        for cp, peer, m in cps:
            pltpu.make_async_remote_copy(src_ref=v_ref, dst_ref=land_ref.at[peer], send_sem=send.at[m - 1],
                                         recv_sem=recv.at[m - 1], device_id=(x, y, c), device_id_type=MESH).wait_recv()
        for cp, *_ in cps:
            cp.wait_send()
        acc = land_ref[0]
        for k in range(1, N_DEV):
            acc = acc + land_ref[k]
        o_ref[...] = acc

    vm = pl.BlockSpec(memory_space=pltpu.VMEM)
    return pl.pallas_call(
        body, name=name, in_specs=[vm], out_specs=vm, out_shape=jax.ShapeDtypeStruct(v.shape, f32),
        scratch_shapes=[pltpu.VMEM((N_DEV, R, LANE), f32), pltpu.SemaphoreType.DMA((N_DEV - 1,)),
                        pltpu.SemaphoreType.DMA((N_DEV - 1,))],
        compiler_params=pltpu.CompilerParams(vmem_limit_bytes=VMEM_LIMIT),
    )(v)


def _pack_small(arrs, mult=2 * SUB):
    flat = jnp.concatenate([a.reshape(-1) for a in arrs])
    rows = -(-flat.shape[0] // (LANE * mult)) * mult
    return jnp.pad(flat, (0, rows * LANE - flat.shape[0])).reshape(rows, LANE)


def _unpack_small(vec, shapes):
    flat, out, o = vec.reshape(-1), [], 0
    for s in shapes:
        n = int(np.prod(s))
        out.append(flat[o:o + n].reshape(s))
        o += n
    return out


REPL_SMALL = ("c_ctx", "b_mod", "q_norm", "k_norm", "c_norm", "d_conv_b", "d_norm_g", "d_norm_b", "ln_g", "ln_b")
SHARD_SMALL = ("b_conv", "c_gate_w2", "c_gate_b", "d_conv_w")
BIG = ("w_mod", "w_in", "w_br", "w_out")
ORDER = ("c_ctx", "w_mod", "b_mod", "w_in", "q_norm", "k_norm", "b_conv", "c_gate_w2", "c_gate_b", "c_norm", "d_conv_w",
         "d_conv_b", "d_norm_g", "d_norm_b", "w_br", "w_out", "ln_g", "ln_b")


def kernel(x, c, ctx, c_ctx, w_mod, b_mod, w_in, q_norm, k_norm, b_conv, c_gate_w2, c_gate_b, c_norm, d_conv_w, d_conv_b, d_norm_g, d_norm_b, w_br, w_out, ln_g, ln_b, loss_target, m_c_ctx, m_w_mod, m_b_mod, m_w_in, m_q_norm, m_k_norm, m_b_conv, m_c_gate_w2, m_c_gate_b, m_c_norm, m_d_conv_w, m_d_conv_b, m_d_norm_g, m_d_norm_b, m_w_br, m_w_out, m_ln_g, m_ln_b, v_c_ctx, v_w_mod, v_b_mod, v_w_in, v_q_norm, v_k_norm, v_b_conv, v_c_gate_w2, v_c_gate_b, v_c_norm, v_d_conv_w, v_d_conv_b, v_d_norm_g, v_d_norm_b, v_w_br, v_w_out, v_ln_g, v_ln_b):
    W = dict(c_ctx=c_ctx, w_mod=w_mod, b_mod=b_mod, w_in=w_in, q_norm=q_norm, k_norm=k_norm, b_conv=b_conv,
             c_gate_w2=c_gate_w2, c_gate_b=c_gate_b, c_norm=c_norm, d_conv_w=d_conv_w, d_conv_b=d_conv_b,
             d_norm_g=d_norm_g, d_norm_b=d_norm_b, w_br=w_br, w_out=w_out, ln_g=ln_g, ln_b=ln_b)
    M = dict(c_ctx=m_c_ctx, w_mod=m_w_mod, b_mod=m_b_mod, w_in=m_w_in, q_norm=m_q_norm, k_norm=m_k_norm, b_conv=m_b_conv,
             c_gate_w2=m_c_gate_w2, c_gate_b=m_c_gate_b, c_norm=m_c_norm, d_conv_w=m_d_conv_w, d_conv_b=m_d_conv_b,
             d_norm_g=m_d_norm_g, d_norm_b=m_d_norm_b, w_br=m_w_br, w_out=m_w_out, ln_g=m_ln_g, ln_b=m_ln_b)
    V = dict(c_ctx=v_c_ctx, w_mod=v_w_mod, b_mod=v_b_mod, w_in=v_w_in, q_norm=v_q_norm, k_norm=v_k_norm, b_conv=v_b_conv,
             c_gate_w2=v_c_gate_w2, c_gate_b=v_c_gate_b, c_norm=v_c_norm, d_conv_w=v_d_conv_w, d_conv_b=v_d_conv_b,
             d_norm_g=v_d_norm_g, d_norm_b=v_d_norm_b, w_br=v_w_br, w_out=v_w_out, ln_g=v_ln_g, ln_b=v_ln_b)
    chip = 2 * lax.axis_index("x") + lax.axis_index("y")
    cidx = lax.axis_index("c").astype(jnp.int32).reshape(1)

    place = jnp.stack([chip, lax.axis_index("c")]).astype(jnp.int32)

    AXIS = dict(w_in=1, w_mod=0, w_br=0, w_out=0)
    ex = dict(w_in=lambda a: jnp.swapaxes(a, 1, 2), w_mod=lambda a: a.reshape(1, DEPTH * D, -1),
              w_br=lambda a: a.reshape(DEPTH, 4 * BRW, -1), w_out=lambda a: a)
    Wx, Mx, Vx = ({k: ex[k](P_[k]) for k in BIG} for P_ in (W, M, V))

    LAYER, MERGE = ("w_in", "w_br", "w_out"), ("w_br", "w_out")
    small_shard = _pack_small([W[k] for k in SHARD_SMALL])
    keys0 = ("w_in", "w_mod")
    sent = lambda k, l: (w_mod[l] if k == "w_mod" else Wx[k][l]).astype(bf16)
    got = _all_gather([sent(k, 0) for k in keys0] + [small_shard], [AXIS[k] for k in keys0] + [0], "all_gather0")
    smalls = [_unpack_small(got[-1][s], [W[k].shape for k in SHARD_SMALL]) for s in range(N_CHIPS)]
    full = {k: jnp.concatenate([smalls[s][i] for s in range(N_CHIPS)], axis=-1) for i, k in enumerate(SHARD_SMALL)}
    ag0b = _split_start([sent(k, 0) for k in MERGE], True, got[0], "all_gather0b_start")
    ag1 = _split_start([sent(k, 1) for k in keys0], True, ag0b[4], "all_gather1_start")
    ag1b = _split_start([sent(k, 1) for k in MERGE], True, ag1[4], "all_gather1b_start")

    def merge_form(w_br4, w_out4):
        return jnp.moveaxis(w_br4.reshape(N_CHIPS, 4, BRW, D // N_CHIPS), 0, 2).reshape(4, BRW, D), w_out4.reshape(D, D)

    def weights_of(l, h):
        first = got if l == 0 else _split_wait(*ag1[:4], True, h, "all_gather1_wait")[1]
        flight = (ag0b, ag1b)[l]
        return (_group_weights(first[0]),
                lambda after: merge_form(*_split_wait(*flight[:4], True, after, f"all_gather{l}b_wait")[1]), first[1])

    red = {k: Wx[k].shape for k in BIG}
    flights, held = {}, {}

    def launch(tag, l, pieces, after=None):
        keys = list(pieces)
        land_a = _sibling_halves([pieces[k] for k in keys], [AXIS[k] for k in keys], f"rs_sibling_halves{tag}")
        pair = [_add_half(pieces[k], la, cidx, AXIS[k], f"rs_pair_sum{tag}_{k}") for k, la in zip(keys, land_a)]
        after = jnp.zeros((SUB, LANE), f32) if after is None else after
        flights[tag] = (l, keys, _split_start(pair, False, after, f"rs_chip_exchange{tag}_start"))
        return flights[tag][2][4]

    def land(tag, after):
        l, keys, flight = flights.pop(tag)
        pair, land_b = _split_wait(*flight[:4], False, after, f"rs_chip_exchange{tag}_wait")
        for k, lb, pr in zip(keys, land_b, pair):
            red[k] = _sum_chips(lb, pr, place, AXIS[k], l, red[k], f"rs_chip_sum{tag}_{k}")

    def grads_done(l, gl):
        if "wp" in gl:
            pieces = dict(w_in=_ungroup(gl["wp"]).reshape(N_CHIPS, SHARD, D))
            return launch("0c", 0, pieces) if l == 0 else launch("1", 1, {**pieces, **held.pop(1)})
        pieces = dict(w_br=gl["w_br"].reshape(N_CHIPS, 4 * BRW, D // N_CHIPS), w_out=gl["w_out"].reshape(N_CHIPS, D // N_CHIPS, D))
        if l == 0:
            return launch("0b", 0, pieces)
        held[1] = pieces
        return None

    loss, gx, g = _local_step(
        x[0], c, ctx[0], loss_target[0], c_ctx, b_mod, weights_of, q_norm, k_norm, full["b_conv"],
        full["c_gate_w2"], full["c_gate_b"], c_norm, full["d_conv_w"], d_conv_b, d_norm_g, d_norm_b,
        grads_done, ln_g, ln_b, tm=256, token=ag1b[4])
    g["c_gate_w2"], g["c_gate_b"] = g.pop("w2"), g.pop("gb")
    loss = lax.psum(loss, ("x", "y", "c"))

    w_mod_pieces = g["w_mod"].reshape(N_CHIPS, DEPTH * D, 3 * D // N_CHIPS)
    g = {k: (jnp.stack(v) if isinstance(v, list) else v) for k, v in g.items() if k not in ("wp", "w_br", "w_out", "w_mod")}

    small_names = REPL_SMALL + SHARD_SMALL
    gs = _all_reduce_small(_pack_small([g[k] for k in small_names]), "all_reduce_small")
    gsm = dict(zip(small_names, _unpack_small(gs, [g[k].shape for k in small_names])))
    for k in SHARD_SMALL:
        wdt = W[k].shape[-1]
        gsm[k] = lax.dynamic_slice_in_dim(gsm[k], chip * wdt, wdt, axis=gsm[k].ndim - 1)

    grad, delta, new_m, new_v = {}, {}, {}, {}

    def adamw_big(keys, after):
        filled = _sibling_fill([red[k] for k in keys], [AXIS[k] for k in keys], "rs_sibling_fill_" + keys[0])
        for k, r in zip(keys, filled):
            back = (lambda a: jnp.swapaxes(a, 1, 2)) if k == "w_in" else (lambda a: a.reshape(W[k].shape))
            g_, d_, m_, v_ = _adamw(Wx[k], r, Mx[k], Vx[k], f"adamw_{k}", after=after)
            grad[k], delta[k], new_m[k], new_v[k] = back(g_), back(d_), back(m_), back(v_)
        return d_

    token = launch("0d", 0, {"w_mod": w_mod_pieces}, after=gs)
    land("1", gx)
    land("0b", gx)
    last = adamw_big(MERGE, token)
    shapes = [W[k].shape for k in small_names]
    _, d_, m_, v_ = _adamw(*[_pack_small([P_[k] for k in small_names])[None] for P_ in (W, gsm, M, V)], "adamw_small", after=last)
    for k, dd, mm_, vv in zip(small_names, _unpack_small(d_, shapes), _unpack_small(m_, shapes), _unpack_small(v_, shapes)):
        grad[k], delta[k], new_m[k], new_v[k] = gsm[k], dd, mm_, vv
    land("0c", d_)
    land("0d", d_)
    adamw_big(("w_in", "w_mod"), None)

    return (loss, gx[None], *[grad[k] for k in ORDER], *[delta[k] for k in ORDER], *[new_m[k] for k in ORDER],
            *[new_v[k] for k in ORDER])
```

```python
import functools

import jax
import jax.numpy as jnp
import numpy as np
from jax import lax
from jax.experimental import pallas as pl
from jax.experimental.pallas import tpu as pltpu

f32 = jnp.float32
bf16 = jnp.bfloat16

D = 1024
DEPTH = 2
GRID_W = 64
BRW = 512
HD = 128
A_HEADS = 4
C_HEADS = 4
C_KW = 256
C_RANK = 16
C_TAU = 16.0
CH = 128
KB = 3
KD = 31
ALPHA = (2 * DEPTH) ** 0.25
EPS = 1e-6
ROPE_THETA = 10000.0
N_IN = 10784
LR, B1, B2, AEPS, WD, STEP = 0.001, 0.9, 0.999, 1e-08, 0.01, 10

W_M, W_A, W_C, W_G = 4 * D + 4 * BRW, 1024, 5 * BRW, 1152
GROUPS = ("M", "A", "C", "G")
M_GA, M_GB, M_GC, M_GD = 4 * D, 4 * D + BRW, 4 * D + 2 * BRW, 4 * D + 3 * BRW
A_K, A_V = 512, 768
G_K, G_V, G_R = 256, 512, 1024
S_Q, S_GA, S_B, S_C, S_X, S_GB, S_CQ, S_CV, S_GC, S_R, S_DA, S_DG, S_GD, S_MG = (
    0, 1024, 1536, 2048, 2560, 3072, 3584, 4096, 4608, 5120, 5152, 5664, 6176, 6688)

LANE = 128
SUB = 8
VMEM_LIMIT = 56 * 1024 * 1024
CONV_PAD = 16
GLA_SUB = 16
GLA_CLAMP = 60.0


def _cparams(sem, vmem=VMEM_LIMIT):
    return pltpu.CompilerParams(dimension_semantics=sem, vmem_limit_bytes=vmem)


def _dg(a, b, ca, cb):
    return lax.dot_general(a.astype(bf16), b.astype(bf16), (((ca,), (cb,)), ((), ())),
                           preferred_element_type=f32)


@jax.custom_vjp
def mm(a, b):
    return _dg(a, b, 1, 0)


mm.defvjp(lambda a, b: (_dg(a, b, 1, 0), (a, b)),
          lambda r, ct: (_dg(ct, r[1], 1, 1).astype(r[0].dtype), _dg(r[0], ct, 0, 0).astype(r[1].dtype)))


@jax.custom_vjp
def mm_nt(a, b):
    return _dg(a, b, 1, 1)


mm_nt.defvjp(lambda a, b: (_dg(a, b, 1, 1), (a, b)),
             lambda r, ct: (_dg(ct, r[1], 1, 0).astype(r[0].dtype), _dg(ct, r[0], 0, 0).astype(r[1].dtype)))


@jax.custom_vjp
def mm_tn(a, b):
    return _dg(a, b, 0, 0)


mm_tn.defvjp(lambda a, b: (_dg(a, b, 0, 0), (a, b)),
             lambda r, ct: (_dg(r[1], ct, 1, 1).astype(r[0].dtype), _dg(r[0], ct, 1, 0).astype(r[1].dtype)))


@jax.custom_vjp
def _sigmoid(x):
    return 0.5 * jnp.tanh(0.5 * x) + 0.5


def _sigmoid_fwd(x):
    s = _sigmoid(x)
    return s, s


_sigmoid.defvjp(_sigmoid_fwd, lambda s, ct: (ct * (s - s * s),))


@jax.custom_vjp
def _silu(x):
    return x * _sigmoid(x)


def _silu_fwd(x):
    s = _sigmoid(x)
    return x * s, (x, s)


_silu.defvjp(_silu_fwd, lambda r, ct: (ct * (r[1] + r[0] * (r[1] - r[1] * r[1])),))


def _ln(x):
    mu = jnp.mean(x, -1, keepdims=True)
    xc = x - mu
    var = jnp.mean(xc * xc, -1, keepdims=True)
    return xc * lax.rsqrt(var + EPS)


def _rms(x, g):
    return x * lax.rsqrt(jnp.mean(x * x, -1, keepdims=True) + EPS) * g


@jax.custom_vjp
def _rope(x, cos_f, sin_a, sin_b):
    return x * cos_f + pltpu.roll(x, HD - 1, 1) * sin_a + pltpu.roll(x, 1, 1) * sin_b


def _rope_fwd(x, cos_f, sin_a, sin_b):
    return _rope(x, cos_f, sin_a, sin_b), (cos_f, sin_a, sin_b)


def _rope_bwd(r, ct):
    cos_f, sin_a, sin_b = r
    dx = ct * cos_f + pltpu.roll(ct * sin_a, 1, 1) + pltpu.roll(ct * sin_b, HD - 1, 1)
    return dx, jnp.zeros_like(cos_f), jnp.zeros_like(sin_a), jnp.zeros_like(sin_b)


_rope.defvjp(_rope_fwd, _rope_bwd)


def _row_ids(i, tm):
    return i * tm + lax.broadcasted_iota(jnp.int32, (tm, 1), 0)


def _partial_rows(ref, rows):
    n = len(rows)
    for k, r in enumerate(rows):
        ref[k:k + 1, :] = r
    ref[n:SUB, :] = jnp.zeros((SUB - n, ref.shape[-1]), f32)


def _matmul(a, b, mode, tm, tn, tk, name, out_dtype=f32, add=None, after=None):
    sect = a.ndim == 3
    a2 = (a.shape[1], a.shape[0] * a.shape[2]) if sect else a.shape
    if mode == "nn":
        (M, K), N = a2, b.shape[1]
        a_spec = pl.BlockSpec((None, tm, tk), lambda j, i, k: (k, i, 0)) if sect else pl.BlockSpec((tm, tk), lambda j, i, k: (i, k))
        b_spec = pl.BlockSpec((tk, tn), lambda j, i, k: (k, j))
        ca, cb = 1, 0
        assert not sect or tk == a.shape[2]
    elif mode == "nt":
        (M, K), N = a2, b.shape[0]
        assert not sect
        a_spec = pl.BlockSpec((tm, tk), lambda j, i, k: (i, k))
        b_spec = pl.BlockSpec((tn, tk), lambda j, i, k: (j, k))
        ca, cb = 1, 1
    else:
        (K, M), N = a2, b.shape[1]
        a_spec = pl.BlockSpec((None, tk, tm), lambda j, i, k: (i, k, 0)) if sect else pl.BlockSpec((tk, tm), lambda j, i, k: (k, i))
        b_spec = pl.BlockSpec((tk, tn), lambda j, i, k: (k, j))
        ca, cb = 0, 0
        assert not sect or tm == a.shape[2]
    assert M % tm == 0 and N % tn == 0 and K % tk == 0, (name, M, N, K, tm, tn, tk)
    nk = K // tk

    o_spec = pl.BlockSpec((tm, tn), lambda j, i, k: (i, j))

    def body(a_ref, b_ref, *rest):
        add_ref = rest[0] if add is not None else None
        o_ref, acc_ref = rest[-2:]
        k = pl.program_id(2)
        part = _dg(a_ref[...], b_ref[...], ca, cb)

        @pl.when(k == 0)
        def _():
            acc_ref[...] = part if add_ref is None else part + add_ref[...]

        @pl.when(k > 0)
        def _():
            acc_ref[...] += part

        @pl.when(k == nk - 1)
        def _():
            o_ref[...] = acc_ref[...].astype(o_ref.dtype)

    extra = ([] if add is None else [(o_spec, add)]) + ([] if after is None else [(pl.BlockSpec(memory_space=pl.ANY), after)])
    return pl.pallas_call(
        body, name=name, grid=(N // tn, M // tm, nk),
        in_specs=[a_spec, b_spec] + [s_ for s_, _ in extra], out_specs=o_spec,
        out_shape=jax.ShapeDtypeStruct((M, N), out_dtype),
        scratch_shapes=[pltpu.VMEM((tm, tn), f32)],
        compiler_params=_cparams(("parallel", "parallel", "arbitrary")),
    )(a, b, *[v_ for _, v_ in extra])


def _matmul_groups(a, b, tks, tm, name, after=None):
    keys = list(a)
    M = a[keys[0]].shape[-2]
    N = b[keys[0]].shape[1]
    count = {g: b[g].shape[0] // tks[g] for g in keys}
    first, total = {}, 0
    for g in keys:
        first[g], total = total, total + count[g]

    def k_of(g):
        return lambda s: jnp.clip(s - first[g], 0, count[g] - 1)

    a_specs = [pl.BlockSpec((None, tm, tks[g]), functools.partial(lambda i, s, kk: (kk(s), i, 0), kk=k_of(g)))
               if a[g].ndim == 3 else pl.BlockSpec((tm, tks[g]), functools.partial(lambda i, s, kk: (i, kk(s)), kk=k_of(g)))
               for g in keys]
    b_specs = [pl.BlockSpec((tks[g], N), functools.partial(lambda i, s, kk: (kk(s), 0), kk=k_of(g))) for g in keys]
    n = len(keys)

    def body(*refs):
        o_ref, acc_ref = refs[-2:]
        s = pl.program_id(1)

        @pl.when(s == 0)
        def _():
            acc_ref[...] = jnp.zeros_like(acc_ref)

        for j, g in enumerate(keys):
            @pl.when((s >= first[g]) & (s < first[g] + count[g]))
            def _(j=j):
                acc_ref[...] += _dg(refs[j][...], refs[n + j][...], 1, 0)

        @pl.when(s == total - 1)
        def _():
            o_ref[...] = acc_ref[...]

    extra = [] if after is None else [after]
    return pl.pallas_call(
        body, name=name, grid=(M // tm, total),
        in_specs=a_specs + b_specs + [pl.BlockSpec(memory_space=pl.ANY)] * len(extra),
        out_specs=pl.BlockSpec((tm, N), lambda i, s: (i, 0)),
        out_shape=jax.ShapeDtypeStruct((M, N), f32),
        scratch_shapes=[pltpu.VMEM((tm, N), f32)],
        compiler_params=_cparams(("parallel", "arbitrary")),
    )(*[a[g] for g in keys], *[b[g] for g in keys], *extra)


def _matmul_tn_batched(a, b, ns, name):
    B, K, M = a.shape
    N = b.shape[2] // ns

    def body(a_ref, b_ref, o_ref):
        o_ref[...] = _dg(a_ref[...], b_ref[...], 0, 0).astype(bf16)

    return pl.pallas_call(
        body, name=name, grid=(B, ns),
        in_specs=[pl.BlockSpec((None, K, M), lambda i, s: (i, 0, 0)), pl.BlockSpec((None, K, N), lambda i, s: (i, 0, s))],
        out_specs=pl.BlockSpec((None, None, M, N), lambda i, s: (s, i, 0, 0)),
        out_shape=jax.ShapeDtypeStruct((ns, B, M, N), bf16),
        compiler_params=_cparams(("parallel", "parallel")),
    )(a, b)


MOD_TN = 768


def _mod_fwd(cin, w_mod_l, b_mod_l, name):
    def body(c_ref, w_ref, b_ref, o_ref):
        o_ref[...] = mm(_silu(c_ref[...]), w_ref[...]) + b_ref[...]

    return pl.pallas_call(
        body, name=name, grid=(3 * D // MOD_TN,),
        in_specs=[pl.BlockSpec((SUB, D), lambda j: (0, 0)), pl.BlockSpec((None, D, MOD_TN), lambda j: (j, 0, 0)),
                  pl.BlockSpec((1, MOD_TN), lambda j: (0, j))],
        out_specs=pl.BlockSpec((SUB, MOD_TN), lambda j: (0, j)),
        out_shape=jax.ShapeDtypeStruct((SUB, 3 * D), f32),
        compiler_params=_cparams(("parallel",)),
    )(cin, w_mod_l, b_mod_l[None, :])


def _mod_bwd(cin, w_mods, dmodv):
    nj = 3 * D // MOD_TN

    def body(c_ref, *refs):
        g_ref, dw_ref, dc_ref = refs[DEPTH:]
        w = refs[0][...]
        for l in range(1, DEPTH):
            w = jnp.where(pl.program_id(0) == l, refs[l][...], w)
        _, vjp = jax.vjp(lambda c, w: mm(_silu(c), w), c_ref[...], w.astype(f32))
        dc, dw = vjp(g_ref[...])
        dw_ref[...] = dw.astype(bf16)
        dc_ref[...] = dc

    return pl.pallas_call(
        body, name="mod_bwd", grid=(DEPTH, nj),
        in_specs=[pl.BlockSpec((SUB, D), lambda l, j: (0, 0))]
        + [pl.BlockSpec((None, D, MOD_TN), lambda l, j: (j, 0, 0))] * DEPTH
        + [pl.BlockSpec((None, SUB, MOD_TN), lambda l, j: (l, 0, j))],
        out_specs=[pl.BlockSpec((None, None, D, MOD_TN), lambda l, j: (j, l, 0, 0)),
                   pl.BlockSpec((None, None, SUB, D), lambda l, j: (l, j, 0, 0))],
        out_shape=[jax.ShapeDtypeStruct((nj, DEPTH, D, MOD_TN), bf16),
                   jax.ShapeDtypeStruct((DEPTH, nj, SUB, D), f32)],
        compiler_params=_cparams(("parallel", "parallel")),
    )(cin, *w_mods, dmodv)


def _u_fn(h, m_l, m_c, isctx):
    n = _ln(h)
    shift = jnp.where(isctx, m_c[:, 0:D], m_l[:, 0:D])
    scale = jnp.where(isctx, m_c[:, D:2 * D], m_l[:, D:2 * D])
    return n * (1.0 + scale) + shift


def _ln_fwd(h, modv_l, tc, tm, name):
    T = h.shape[0]

    def body(h_ref, m_ref, u_ref):
        isctx = _row_ids(pl.program_id(0), tm) < tc
        u_ref[...] = _u_fn(h_ref[...], m_ref[0:1, :], m_ref[1:2, :], isctx).astype(bf16)

    return pl.pallas_call(
        body, name=name, grid=(T // tm,),
        in_specs=[pl.BlockSpec((tm, D), lambda i: (i, 0)), pl.BlockSpec((SUB, 3 * D), lambda i: (0, 0))],
        out_specs=pl.BlockSpec((tm, D), lambda i: (i, 0)),
        out_shape=jax.ShapeDtypeStruct((T, D), bf16),
        compiler_params=_cparams(("parallel",)),
    )(h, modv_l)


def _ln_bwd(du, h, dh_res, modv_l, tc, tm, name, latent_only=False):
    T = h.shape[0]
    nt, nct = T // tm, tc // tm

    def body(du_ref, h_ref, r_ref, m_ref, dh_ref, dm_ref):
        isctx = _row_ids(pl.program_id(0), tm) < tc
        _, vjp = jax.vjp(lambda h, ml, mc: _u_fn(h, ml, mc, isctx), h_ref[...], m_ref[0:1, :], m_ref[1:2, :])
        dh, dml, dmc = vjp(du_ref[...])
        dh_ref[...] = dh + r_ref[...]
        _partial_rows(dm_ref, [dml, dmc])

    dh_map = (lambda i: (jnp.maximum(i - nct, 0), 0)) if latent_only else (lambda i: (i, 0))
    return pl.pallas_call(
        body, name=name, grid=(nt,),
        in_specs=[pl.BlockSpec((tm, D), lambda i: (i, 0)), pl.BlockSpec((tm, D), lambda i: (i, 0)),
                  pl.BlockSpec((tm, D), lambda i: (i, 0)), pl.BlockSpec((SUB, 3 * D), lambda i: (0, 0))],
        out_specs=[pl.BlockSpec((tm, D), dh_map), pl.BlockSpec((None, SUB, 3 * D), lambda i: (i, 0, 0))],
        out_shape=[jax.ShapeDtypeStruct((T - tc if latent_only else T, D), f32), jax.ShapeDtypeStruct((nt, SUB, 3 * D), f32)],
        compiler_params=_cparams(("arbitrary",)),
    )(du, h, dh_res, modv_l)


def _prep_fn(q, k, qg, kg, cos_f, sin_a, sin_b):
    qs = [_rope(_rms(q[:, HD * i:HD * (i + 1)], qg), cos_f, sin_a, sin_b) * (HD ** -0.5) for i in range(A_HEADS)]
    ks = [_rope(_rms(k[:, HD * i:HD * (i + 1)], kg), cos_f, sin_a, sin_b) for i in range(A_HEADS // 2)]
    return jnp.concatenate(qs, 1), jnp.concatenate(ks, 1)


def _tok(tm, w, off):
    return pl.BlockSpec((tm, w), lambda i: (i, off // w))


def _vec(w):
    return pl.BlockSpec((1, w), lambda i: (0, 0))


def _prep_fwd(P, qg, kg, rope, tm, name):
    T = P.shape[0]

    def body(q_ref, k_ref, v_ref, qg_ref, kg_ref, c_ref, sa_ref, sb_ref, qn_ref, kn_ref, vb_ref):
        qn, kn = _prep_fn(q_ref[...].astype(f32), k_ref[...].astype(f32), qg_ref[...], kg_ref[...], c_ref[...], sa_ref[...],
                          sb_ref[...])
        qn_ref[...] = qn.astype(bf16)
        kn_ref[...] = kn.astype(bf16)
        vb_ref[...] = v_ref[...].astype(bf16)

    return pl.pallas_call(
        body, name=name, grid=(T // tm,),
        in_specs=[_tok(tm, 512, 0), _tok(tm, 256, A_K), _tok(tm, 256, A_V), _vec(HD), _vec(HD),
                  _tok(tm, HD, 0), _tok(tm, HD, 0), _tok(tm, HD, 0)],
        out_specs=[_tok(tm, 512, 0), _tok(tm, 256, 0), _tok(tm, 256, 0)],
        out_shape=[jax.ShapeDtypeStruct((T, 512), bf16), jax.ShapeDtypeStruct((T, 256), bf16),
                   jax.ShapeDtypeStruct((T, 256), bf16)],
        compiler_params=_cparams(("parallel",)),
    )(P, P, P, qg, kg, *rope)


def _prep_bwd(P, dqn, dkn, dv, qg, kg, rope, tm, name):
    T = P.shape[0]
    nt = T // tm

    def body(q_ref, k_ref, dq_ref, dk_ref, dv_ref, qg_ref, kg_ref, c_ref, sa_ref, sb_ref, o_ref, og_ref):
        tabs = (c_ref[...], sa_ref[...], sb_ref[...])
        _, vjp = jax.vjp(lambda q, k, a, b: _prep_fn(q, k, a, b, *tabs), q_ref[...].astype(f32), k_ref[...].astype(f32),
                         qg_ref[...], kg_ref[...])
        dq, dk, dqg, dkg = vjp((dq_ref[...], dk_ref[...]))
        o_ref[:, 0:A_K] = dq.astype(bf16)
        o_ref[:, A_K:A_V] = dk.astype(bf16)
        o_ref[:, A_V:W_A] = dv_ref[...].astype(bf16)
        _partial_rows(og_ref, [dqg, dkg])

    return pl.pallas_call(
        body, name=name, grid=(nt,),
        in_specs=[_tok(tm, 512, 0), _tok(tm, 256, A_K), _tok(tm, 512, 0), _tok(tm, 256, 0), _tok(tm, 256, 0),
                  _vec(HD), _vec(HD), _tok(tm, HD, 0), _tok(tm, HD, 0), _tok(tm, HD, 0)],
        out_specs=[_tok(tm, W_A, 0), pl.BlockSpec((None, SUB, HD), lambda i: (i, 0, 0))],
        out_shape=[jax.ShapeDtypeStruct((T, W_A), bf16), jax.ShapeDtypeStruct((nt, SUB, HD), f32)],
        compiler_params=_cparams(("parallel",)),
    )(P, P, dqn, dkn, dv, qg, kg, *rope)


def _attn_fn(q, k, v, lim):
    col = lax.broadcasted_iota(jnp.int32, (1, k.shape[0]), 1)
    s = mm_nt(q, k) + jnp.where(col < lim, 0.0, -1e30)
    m = lax.stop_gradient(jnp.max(s, -1, keepdims=True))
    e = jnp.exp(s - m)
    p = e * (1.0 / jnp.sum(e, -1, keepdims=True))
    return mm(p, v)


def _attn_fwd(qn, kn, vb, tc, tq, name):
    T = qn.shape[0]

    def body(q_ref, k_ref, v_ref, o_ref):
        lim = jnp.where(pl.program_id(1) * tq < tc, tc, T)
        o_ref[...] = _attn_fn(q_ref[...], k_ref[...], v_ref[...], lim)

    return pl.pallas_call(
        body, name=name, grid=(A_HEADS, T // tq),
        in_specs=[pl.BlockSpec((tq, HD), lambda h, i: (i, h)), pl.BlockSpec((T, HD), lambda h, i: (0, h // 2)),
                  pl.BlockSpec((T, HD), lambda h, i: (0, h // 2))],
        out_specs=pl.BlockSpec((tq, HD), lambda h, i: (i, h)),
        out_shape=jax.ShapeDtypeStruct((T, 512), f32),
        compiler_params=_cparams(("parallel", "parallel")),
    )(qn, kn, vb)


def _attn_bwd(qn, kn, vb, dya, tc, tq, name):
    T = qn.shape[0]

    def body(q_ref, k_ref, v_ref, g_ref, dq_ref, dk_ref, dv_ref):
        first = (pl.program_id(1) == 0) & (pl.program_id(2) == 0)
        lim = jnp.where(pl.program_id(2) * tq < tc, tc, T)
        _, vjp = jax.vjp(lambda q, k, v: _attn_fn(q, k, v, lim), q_ref[...].astype(f32), k_ref[...].astype(f32),
                         v_ref[...].astype(f32))
        dq, dk, dv = vjp(g_ref[...])
        dq_ref[...] = dq

        @pl.when(first)
        def _():
            dk_ref[...] = dk
            dv_ref[...] = dv

        @pl.when(jnp.logical_not(first))
        def _():
            dk_ref[...] += dk
            dv_ref[...] += dv

    qspec = pl.BlockSpec((tq, HD), lambda kv, g, i: (i, 2 * kv + g))
    kspec = pl.BlockSpec((T, HD), lambda kv, g, i: (0, kv))
    return pl.pallas_call(
        body, name=name, grid=(A_HEADS // 2, 2, T // tq),
        in_specs=[qspec, kspec, kspec, qspec], out_specs=[qspec, kspec, kspec],
        out_shape=[jax.ShapeDtypeStruct((T, 512), f32), jax.ShapeDtypeStruct((T, 256), f32),
                   jax.ShapeDtypeStruct((T, 256), f32)],
        compiler_params=_cparams(("parallel", "arbitrary", "arbitrary")),
    )(qn, kn, vb, dya)


def _conv_rows(tc, tl):
    return CONV_PAD + tc + CONV_PAD + tl + CONV_PAD


def _fill_pad(pad_ref, val, tc, tl):
    z = jnp.zeros((CONV_PAD, LANE), f32)
    pad_ref[0:CONV_PAD, :] = z
    pad_ref[CONV_PAD:CONV_PAD + tc, :] = val[0:tc]
    pad_ref[CONV_PAD + tc:2 * CONV_PAD + tc, :] = z
    pad_ref[2 * CONV_PAD + tc:2 * CONV_PAD + tc + tl, :] = val[tc:tc + tl]
    pad_ref[2 * CONV_PAD + tc + tl:3 * CONV_PAD + tc + tl, :] = z


def _conv_apply(pad_ref, w_ref, K, tc, tl, rc, emit, flip=False):
    half = K // 2
    for seg0, off, n in ((0, CONV_PAD, tc), (tc, 2 * CONV_PAD + tc, tl)):
        for r0 in range(0, n, rc):
            acc = None
            for k in range(K):
                sh = (half - k) if flip else (k - half)
                term = pad_ref[pl.ds(off + r0 + sh, rc), :] * w_ref[k:k + 1, :]
                acc = term if acc is None else acc + term
            emit(seg0 + r0, acc)


def _conv_wgrad(pad_ref, dy_ref, K, tc, tl, rc, dw_ref):
    half = K // 2
    for k in range(K):
        acc = jnp.zeros((1, LANE), f32)
        for seg0, off, n in ((0, CONV_PAD, tc), (tc, 2 * CONV_PAD + tc, tl)):
            for r0 in range(0, n, rc):
                acc = acc + jnp.sum(pad_ref[pl.ds(off + r0 + k - half, rc), :] * dy_ref[pl.ds(seg0 + r0, rc), :],
                                    axis=0, keepdims=True)
        dw_ref[k:k + 1, :] = acc


def _col(T, off):
    return pl.BlockSpec((T, LANE), lambda j: (0, off // LANE + j))


C_B, C_C, C_X, C_A, C_G = range(5)
N_SEC = 5


class _Sections:
    def __init__(self, refs):
        self.refs = refs

    def __getitem__(self, idx):
        rows, sec = idx
        return self.refs[sec][rows, :].astype(f32)

    def __setitem__(self, idx, val):
        rows, sec = idx
        self.refs[sec, rows, :] = val


def _sec_specs(T):
    return [pl.BlockSpec((T, LANE), functools.partial(lambda j, s: (0, s * (BRW // LANE) + j), s=s)) for s in range(N_SEC)]


def _conv_fwd(P, wb, wd, bd, tc, tl, rc, name):
    T = tc + tl

    def body(*refs):
        p_ref = _Sections(refs[:N_SEC])
        wb_ref, wd_ref, bd_ref, yb_ref, hh_ref, pad_ref = refs[N_SEC:]
        _fill_pad(pad_ref, p_ref[:, C_C] * p_ref[:, C_X], tc, tl)

        def emit_b(r0, y):
            yb_ref[pl.ds(r0, rc), :] = y * p_ref[pl.ds(r0, rc), C_B]

        _conv_apply(pad_ref, wb_ref, KB, tc, tl, rc, emit_b)
        _fill_pad(pad_ref, p_ref[:, C_A] * _sigmoid(p_ref[:, C_G]), tc, tl)

        def emit_d(r0, y):
            hh_ref[pl.ds(r0, rc), :] = y + bd_ref[...]

        _conv_apply(pad_ref, wd_ref, KD, tc, tl, rc, emit_d)

    return pl.pallas_call(
        body, name=name, grid=(BRW // LANE,),
        in_specs=_sec_specs(T) + [pl.BlockSpec((KB, LANE), lambda j: (0, j)), pl.BlockSpec((KD, LANE), lambda j: (0, j)),
                                  pl.BlockSpec((1, LANE), lambda j: (0, j))],
        out_specs=[_col(T, 0), _col(T, 0)],
        out_shape=[jax.ShapeDtypeStruct((T, BRW), f32), jax.ShapeDtypeStruct((T, BRW), f32)],
        scratch_shapes=[pltpu.VMEM((_conv_rows(tc, tl), LANE), f32)],
        compiler_params=_cparams(("parallel",)),
    )(*[P] * N_SEC, wb, wd, bd)


def _conv_bwd(P, dyb, dhh, wb, wd, tc, tl, rc, name):
    T = tc + tl

    def body(*refs):
        p_ref = _Sections(refs[:N_SEC])
        dyb_ref, dhh_ref, wb_ref, wd_ref, dp3_ref, dwb_ref, dwd_ref, dbd_ref, pad_ref, pad2_ref, tmp_ref = refs[N_SEC:]
        dp_ref = _Sections(dp3_ref)
        _fill_pad(pad_ref, p_ref[:, C_C] * p_ref[:, C_X], tc, tl)

        def emit_cv(r0, y):
            dp_ref[pl.ds(r0, rc), C_B] = (y * dyb_ref[pl.ds(r0, rc), :]).astype(bf16)

        _conv_apply(pad_ref, wb_ref, KB, tc, tl, rc, emit_cv)
        tmp_ref[...] = dyb_ref[...] * p_ref[:, C_B]
        _conv_wgrad(pad_ref, tmp_ref, KB, tc, tl, rc, dwb_ref)
        _fill_pad(pad2_ref, tmp_ref[...], tc, tl)

        def emit_ds(r0, y):
            dp_ref[pl.ds(r0, rc), C_C] = (y * p_ref[pl.ds(r0, rc), C_X]).astype(bf16)
            dp_ref[pl.ds(r0, rc), C_X] = (y * p_ref[pl.ds(r0, rc), C_C]).astype(bf16)

        _conv_apply(pad2_ref, wb_ref, KB, tc, tl, rc, emit_ds, flip=True)
        _fill_pad(pad_ref, p_ref[:, C_A] * _sigmoid(p_ref[:, C_G]), tc, tl)
        _conv_wgrad(pad_ref, dhh_ref, KD, tc, tl, rc, dwd_ref)
        dbd_ref[...] = jnp.sum(dhh_ref[...], axis=0, keepdims=True)
        _fill_pad(pad2_ref, dhh_ref[...], tc, tl)

        def emit_d2(r0, y):
            sg = _sigmoid(p_ref[pl.ds(r0, rc), C_G])
            a = p_ref[pl.ds(r0, rc), C_A]
            dp_ref[pl.ds(r0, rc), C_A] = (y * sg).astype(bf16)
            dp_ref[pl.ds(r0, rc), C_G] = (y * a * sg * (1.0 - sg)).astype(bf16)

        _conv_apply(pad2_ref, wd_ref, KD, tc, tl, rc, emit_d2, flip=True)

    return pl.pallas_call(
        body, name=name, grid=(BRW // LANE,),
        in_specs=_sec_specs(T) + [_col(T, 0), _col(T, 0),
                                  pl.BlockSpec((KB, LANE), lambda j: (0, j)), pl.BlockSpec((KD, LANE), lambda j: (0, j))],
        out_specs=[pl.BlockSpec((N_SEC, T, LANE), lambda j: (0, 0, j)), pl.BlockSpec((KB, LANE), lambda j: (0, j)),
                   pl.BlockSpec((KD, LANE), lambda j: (0, j)), pl.BlockSpec((1, LANE), lambda j: (0, j))],
        out_shape=[jax.ShapeDtypeStruct((N_SEC, T, BRW), bf16), jax.ShapeDtypeStruct((KB, BRW), f32),
                   jax.ShapeDtypeStruct((KD, BRW), f32), jax.ShapeDtypeStruct((1, BRW), f32)],
        scratch_shapes=[pltpu.VMEM((_conv_rows(tc, tl), LANE), f32), pltpu.VMEM((_conv_rows(tc, tl), LANE), f32),
                        pltpu.VMEM((T, LANE), f32)],
        compiler_params=_cparams(("parallel",)),
    )(*[P] * N_SEC, dyb, dhh, wb, wd)


def _gla_chunk(q, k, v, r, w2, b2, st, isfwd):
    z = mm(r, w2) + b2
    g = jax.nn.log_sigmoid(z[:, 0:C_KW] if isfwd else z[:, C_KW:2 * C_KW]) / C_TAU
    ri = lax.broadcasted_iota(jnp.int32, (CH, CH), 0)
    ci = lax.broadcasted_iota(jnp.int32, (CH, CH), 1)
    tri = ((ci <= ri) if isfwd else (ci >= ri)).astype(f32)
    cum = jnp.dot(tri, g, preferred_element_type=f32, precision=lax.Precision.HIGHEST)
    last = jnp.sum(g, axis=0, keepdims=True)
    q = q * (C_KW // C_HEADS) ** -0.5
    hv = lax.broadcasted_iota(jnp.int32, (BRW, C_KW), 0) // (BRW // C_HEADS)
    hk = lax.broadcasted_iota(jnp.int32, (BRW, C_KW), 1) // (C_KW // C_HEADS)
    st_new = st * jnp.exp(last) + jnp.where(hv == hk, mm_tn(v, k * jnp.exp(last - cum)), 0.0)
    o = mm_nt(q * jnp.exp(cum), st)
    rowi = lax.broadcasted_iota(jnp.int32, (CH, C_KW), 0)
    srow = lax.broadcasted_iota(jnp.int32, (C_HEADS * CH, C_KW), 0)
    slane = lax.broadcasted_iota(jnp.int32, (C_HEADS * CH, C_KW), 1)
    own_lanes = srow // CH == slane // (C_KW // C_HEADS)
    pos = lax.broadcasted_iota(jnp.int32, (C_HEADS * CH, CH), 0) % CH
    key = lax.broadcasted_iota(jnp.int32, (C_HEADS * CH, CH), 1)
    scores = jnp.zeros((C_HEADS * CH, CH), f32)
    for a in range(CH // GLA_SUB):
        idx = GLA_SUB * a - 1 if isfwd else GLA_SUB * (a + 1)
        ref = jnp.sum(jnp.where(rowi == idx, cum, 0.0), axis=0, keepdims=True)
        qa = q * jnp.exp(jnp.minimum(cum - ref, 0.0))
        ka = k * jnp.exp(jnp.minimum(ref - cum, GLA_CLAMP))
        s = mm_nt(jnp.where(own_lanes, jnp.concatenate([qa] * C_HEADS, axis=0), 0.0), ka)
        scores = scores + jnp.where(pos // GLA_SUB == a, s, 0.0)
    scores = jnp.where((key <= pos) if isfwd else (key >= pos), scores, 0.0)
    vw = BRW // C_HEADS
    o = o + jnp.concatenate([mm(scores[CH * hd:CH * (hd + 1)], v[:, vw * hd:vw * (hd + 1)]) for hd in range(C_HEADS)],
                            axis=1)
    return o, st_new


def _gla_chunk_of(d, n, nc, nch):
    back = jnp.where(n < nc, nc - 1 - n, nch - 1 - (n - nc))
    return jnp.where(d == 0, n, back)


def _gla_fwd(P, w2, b2, tc, name):
    T = P.shape[0]
    nch, nc = T // CH, tc // CH

    back = lambda n: _gla_chunk_of(1, n, nc, nch)

    def body(pf_ref, pb_ref, w_ref, b_ref, of_ref, ob_ref, ssf_ref, ssb_ref, stf_ref, stb_ref):
        @pl.when(pl.program_id(0) == 0)
        def _():
            stf_ref[...] = jnp.zeros_like(stf_ref)
            stb_ref[...] = jnp.zeros_like(stb_ref)

        for p_ref, o_ref, ss_ref, st_ref, isfwd in ((pf_ref, of_ref, ssf_ref, stf_ref, True),
                                                    (pb_ref, ob_ref, ssb_ref, stb_ref, False)):
            st = st_ref[...]
            ss_ref[...] = st
            p = p_ref[...].astype(f32)
            o, st_new = _gla_chunk(p[:, 0:G_K], p[:, G_K:G_V], p[:, G_V:G_R], p[:, G_R:W_G], w_ref[...], b_ref[...], st, isfwd)
            o_ref[...] = o
            st_ref[...] = st_new

    sd = jax.ShapeDtypeStruct
    return pl.pallas_call(
        body, name=name, grid=(nch,),
        in_specs=[pl.BlockSpec((CH, W_G), lambda n: (n, 0)), pl.BlockSpec((CH, W_G), lambda n: (back(n), 0)),
                  pl.BlockSpec((LANE, 512), lambda n: (0, 0)), pl.BlockSpec((1, 512), lambda n: (0, 0))],
        out_specs=[pl.BlockSpec((CH, BRW), lambda n: (n, 0)), pl.BlockSpec((CH, BRW), lambda n: (back(n), 0)),
                   pl.BlockSpec((None, BRW, C_KW), lambda n: (n, 0, 0)), pl.BlockSpec((None, BRW, C_KW), lambda n: (n, 0, 0))],
        out_shape=[sd((T, BRW), f32), sd((T, BRW), f32), sd((nch, BRW, C_KW), f32), sd((nch, BRW, C_KW), f32)],
        scratch_shapes=[pltpu.VMEM((BRW, C_KW), f32), pltpu.VMEM((BRW, C_KW), f32)],
        compiler_params=_cparams(("arbitrary",)),
    )(P, P, w2, b2)


def _gla_bwd(P, w2, b2, ssave, doc, tc, name):
    T = P.shape[0]
    nch, nc = T // CH, tc // CH

    fwd_chunk = lambda m: nch - 1 - m
    back_chunk = lambda m: _gla_chunk_of(1, nch - 1 - m, nc, nch)

    def body(pf_ref, pb_ref, w_ref, b_ref, ssf_ref, ssb_ref, gf_ref, gb_ref, dpf_ref, dpb_ref, dw_ref, db_ref,
             dstf_ref, dstb_ref):
        m = pl.program_id(0)

        @pl.when(m == 0)
        def _():
            dstf_ref[...] = jnp.zeros_like(dstf_ref)
            dstb_ref[...] = jnp.zeros_like(dstb_ref)

        dw_sum, db_sum = None, None
        for p_ref, ss_ref, g_ref, dp_ref, dst_ref, isfwd in ((pf_ref, ssf_ref, gf_ref, dpf_ref, dstf_ref, True),
                                                             (pb_ref, ssb_ref, gb_ref, dpb_ref, dstb_ref, False)):
            p = p_ref[...].astype(f32)
            _, vjp = jax.vjp(lambda q, k, v, r, w, b, st: _gla_chunk(q, k, v, r, w, b, st, isfwd),
                             p[:, 0:G_K], p[:, G_K:G_V], p[:, G_V:G_R], p[:, G_R:W_G], w_ref[...], b_ref[...], ss_ref[...])
            dq, dk, dv, dr, dw, db, dst = vjp((g_ref[...], dst_ref[...]))
            dp_ref[:, 0:G_K] = dq
            dp_ref[:, G_K:G_V] = dk
            dp_ref[:, G_V:G_R] = dv
            dp_ref[:, G_R:W_G] = dr
            dst_ref[...] = dst
            dw_sum = dw if dw_sum is None else dw_sum + dw
            db_sum = db if db_sum is None else db_sum + db

        @pl.when(m == 0)
        def _():
            dw_ref[...] = dw_sum
            _partial_rows(db_ref, [db_sum])

        @pl.when(m > 0)
        def _():
            dw_ref[...] += dw_sum
            db_ref[0:1, :] += db_sum

    ssf, ssb = ssave
    chunk_f = lambda w: pl.BlockSpec((CH, w), lambda m: (fwd_chunk(m), 0))
    chunk_b = lambda w: pl.BlockSpec((CH, w), lambda m: (back_chunk(m), 0))
    state = pl.BlockSpec((None, BRW, C_KW), lambda m: (nch - 1 - m, 0, 0))
    sd = jax.ShapeDtypeStruct
    return pl.pallas_call(
        body, name=name, grid=(nch,),
        in_specs=[chunk_f(W_G), chunk_b(W_G), pl.BlockSpec((LANE, 512), lambda m: (0, 0)), pl.BlockSpec((1, 512), lambda m: (0, 0)),
                  state, state, chunk_f(BRW), chunk_b(BRW)],
        out_specs=[chunk_f(W_G), chunk_b(W_G), pl.BlockSpec((LANE, 512), lambda m: (0, 0)), pl.BlockSpec((SUB, 512), lambda m: (0, 0))],
        out_shape=[sd((T, W_G), f32), sd((T, W_G), f32), sd((LANE, 512), f32), sd((SUB, 512), f32)],
        scratch_shapes=[pltpu.VMEM((BRW, C_KW), f32), pltpu.VMEM((BRW, C_KW), f32)],
        compiler_params=_cparams(("arbitrary",)),
    )(P, P, w2, b2, ssf, ssb, doc, doc)


def _sum_dirs(a, b, tm, name):
    T, W = a.shape

    def body(a_ref, b_ref, o_ref):
        o_ref[...] = (a_ref[...] + b_ref[...]).astype(bf16)

    spec = pl.BlockSpec((tm, W), lambda i: (i, 0))
    return pl.pallas_call(
        body, name=name, grid=(T // tm,), in_specs=[spec, spec], out_specs=spec,
        out_shape=jax.ShapeDtypeStruct((T, W), bf16),
        compiler_params=_cparams(("parallel",)),
    )(a, b)


def _merge_fn(h, m_l, m_c, isctx, ya, ga, yb, gb, of, ob, gc, hh, gd, mg, es, ey, cn, dng, dnb, lg, lb, wbr, wout):
    oc = of + ob
    yc = jnp.concatenate([_rms(oc[:, HD * i:HD * (i + 1)], cn[:, HD * i:HD * (i + 1)]) for i in range(C_HEADS)], 1)
    brs = [ya * _silu(ga), yb * _silu(gb), yc * _silu(gc), _silu(_ln(hh) * dng + dnb) * _silu(gd)]
    acc = None
    for i in range(4):
        t = _sigmoid(mg[:, D * i:D * (i + 1)]) * (mm(brs[i], wbr[i]) + es[i])
        acc = t if acc is None else acc + t
    y = mm(acc, wout) + ey
    gate = jnp.where(isctx, m_c[:, 2 * D:3 * D], m_l[:, 2 * D:3 * D])
    hn = _ln(ALPHA * h + gate * y) * lg + lb
    return hn, (brs, acc)


def _merge_specs(tm):
    t = lambda w, off=0: _tok(tm, w, off)
    return [t(D), pl.BlockSpec((SUB, 3 * D), lambda i: (0, 0)),
            t(BRW), t(BRW, M_GA), t(BRW), t(BRW, M_GB),
            t(BRW), t(BRW),
            t(BRW, M_GC), t(BRW), t(BRW, M_GD), t(4 * D, 0),
            _vec(BRW), _vec(BRW), _vec(BRW), _vec(D), _vec(D),
            pl.BlockSpec((4, BRW, D), lambda i: (0, 0, 0)), pl.BlockSpec((D, D), lambda i: (0, 0))]


def _merge_fwd(h, modv_l, ya, yb, o2, hh, P, cn, dng, dnb, lg, lb, wbr, wout, tc, tm, name):
    T = h.shape[0]

    def body(h_ref, m_ref, ya_ref, ga_ref, yb_ref, gb_ref, of_ref, ob_ref, gc_ref, hh_ref, gd_ref, mg_ref,
             cn_ref, dng_ref, dnb_ref, lg_ref, lb_ref, wbr_ref, wout_ref, o_ref):
        isctx = _row_ids(pl.program_id(0), tm) < tc
        zero = jnp.zeros((tm, D), f32)
        up = lambda r: r[...].astype(f32)
        hn, _ = _merge_fn(h_ref[...], m_ref[0:1, :], m_ref[1:2, :], isctx, ya_ref[...], up(ga_ref), yb_ref[...],
                          up(gb_ref), of_ref[...], ob_ref[...], up(gc_ref), hh_ref[...], up(gd_ref), up(mg_ref),
                          [zero] * 4, zero, cn_ref[...], dng_ref[...], dnb_ref[...], lg_ref[...], lb_ref[...],
                          [wbr_ref[i] for i in range(4)], wout_ref[...])
        o_ref[...] = hn

    return pl.pallas_call(
        body, name=name, grid=(T // tm,),
        in_specs=_merge_specs(tm), out_specs=_tok(tm, D, 0),
        out_shape=jax.ShapeDtypeStruct((T, D), f32),
        compiler_params=_cparams(("parallel",)),
    )(h, modv_l, ya, P, yb, P, o2[0], o2[1], P, hh, P, P, cn, dng, dnb, lg, lb, wbr, wout)


def _merge_bwd(dhn, h, modv_l, ya, yb, o2, hh, P, cn, dng, dnb, lg, lb, wbr, wout, tc, tm, name):
    T = h.shape[0]
    nt = T // tm

    def body(g_ref, h_ref, m_ref, ya_ref, ga_ref, yb_ref, gb_ref, of_ref, ob_ref, gc_ref, hh_ref, gd_ref, mg_ref,
             cn_ref, dng_ref, dnb_ref, lg_ref, lb_ref, wbr_ref, wout_ref,
             dh_ref, dm_ref, dya_ref, dyb_ref, doc_ref, dhh_ref, dp_ref,
             br_ref, z_ref, acc_ref, dy_ref, dv5_ref, dvd_ref):
        isctx = _row_ids(pl.program_id(0), tm) < tc
        zero = jnp.zeros((tm, D), f32)
        wbr_v = [wbr_ref[i] for i in range(4)]
        wout_v = wout_ref[...]
        up = lambda r: r[...].astype(f32)

        def fn(h, ml, mc, ya, ga, yb, gb, oc, gc, hh, gd, mg, e0, e1, e2, e3, ey, cn, dng, dnb, lg, lb):
            return _merge_fn(h, ml, mc, isctx, ya, ga, yb, gb, oc, jnp.zeros_like(oc), gc, hh, gd, mg,
                             [e0, e1, e2, e3], ey, cn, dng, dnb, lg, lb, wbr_v, wout_v)

        _, vjp, (brs, acc) = jax.vjp(
            fn, h_ref[...], m_ref[0:1, :], m_ref[1:2, :], ya_ref[...], up(ga_ref), yb_ref[...], up(gb_ref),
            of_ref[...] + ob_ref[...], up(gc_ref), hh_ref[...], up(gd_ref), up(mg_ref), zero, zero, zero, zero, zero,
            cn_ref[...], dng_ref[...], dnb_ref[...], lg_ref[...], lb_ref[...], has_aux=True)
        (dh, dml, dmc, dya, dga, dyb, dgb, doc, dgc, dhh, dgd, dmg, z0, z1, z2, z3, dy,
         dcn, ddng, ddnb, dlg, dlb) = vjp(g_ref[...])
        dh_ref[...] = dh
        _partial_rows(dm_ref, [dml, dmc])
        dya_ref[...] = dya
        dyb_ref[...] = dyb
        doc_ref[...] = doc
        dhh_ref[...] = dhh
        dp_ref[:, 0:M_GA] = dmg.astype(bf16)
        dp_ref[:, M_GA:M_GB] = dga.astype(bf16)
        dp_ref[:, M_GB:M_GC] = dgb.astype(bf16)
        dp_ref[:, M_GC:M_GD] = dgc.astype(bf16)
        dp_ref[:, M_GD:W_M] = dgd.astype(bf16)
        for i, z in enumerate((z0, z1, z2, z3)):
            br_ref[i] = brs[i].astype(bf16)
            z_ref[i] = z.astype(bf16)
        acc_ref[...] = acc.astype(bf16)
        dy_ref[...] = dy.astype(bf16)
        _partial_rows(dv5_ref, [dcn, ddng, ddnb])
        _partial_rows(dvd_ref, [dlg, dlb])

    t = lambda w: _tok(tm, w, 0)
    part = lambda w: pl.BlockSpec((None, SUB, w), lambda i: (i, 0, 0))
    sd = jax.ShapeDtypeStruct
    return pl.pallas_call(
        body, name=name, grid=(nt,),
        in_specs=[t(D)] + _merge_specs(tm),
        out_specs=[t(D), part(3 * D)] + [t(BRW)] * 4 + [t(W_M),
                   pl.BlockSpec((4, tm, BRW), lambda i: (0, i, 0)), pl.BlockSpec((4, tm, D), lambda i: (0, i, 0)),
                   t(D), t(D), part(BRW), part(D)],
        out_shape=[sd((T, D), f32), sd((nt, SUB, 3 * D), f32)] + [sd((T, BRW), f32)] * 4 + [sd((T, W_M), bf16),
                   sd((4, T, BRW), bf16), sd((4, T, D), bf16), sd((T, D), bf16), sd((T, D), bf16),
                   sd((nt, SUB, BRW), f32), sd((nt, SUB, D), f32)],
        compiler_params=_cparams(("parallel",)),
    )(dhn, h, modv_l, ya, P, yb, P, o2[0], o2[1], P, hh, P, P, cn, dng, dnb, lg, lb, wbr, wout)


def _loss_kernel(h, tgt, tc, tm, name):
    T = h.shape[0]
    nt = T // tm
    nct = tc // tm

    def body(h_ref, t_ref, d_ref, l_ref):
        i = pl.program_id(0)
        err = h_ref[...] - t_ref[...]
        lat = (i >= nct).astype(f32)
        d_ref[...] = err * (lat / D)
        l_ref[...] = jnp.zeros((SUB, LANE), f32) + lat * 0.5 * jnp.sum(err * err) / D

    return pl.pallas_call(
        body, name=name, grid=(nt,),
        in_specs=[pl.BlockSpec((tm, D), lambda i: (i, 0)),
                  pl.BlockSpec((tm, D), lambda i: (jnp.maximum(i - nct, 0), 0))],
        out_specs=[pl.BlockSpec((tm, D), lambda i: (i, 0)), pl.BlockSpec((None, SUB, LANE), lambda i: (i, 0, 0))],
        out_shape=[jax.ShapeDtypeStruct((T, D), f32), jax.ShapeDtypeStruct((nt, SUB, LANE), f32)],
        compiler_params=_cparams(("parallel",)),
    )(h, tgt)


def _rope_tables(tc, tl):
    t = jnp.arange(tl)
    inv = ROPE_THETA ** (-jnp.arange(0, HD // 2, 2, dtype=f32) / (HD // 2))
    ang = jnp.concatenate([(t // GRID_W).astype(f32)[:, None] * inv, (t % GRID_W).astype(f32)[:, None] * inv], -1)
    cos, sin = jnp.repeat(jnp.cos(ang), 2, axis=1), jnp.repeat(jnp.sin(ang), 2, axis=1)
    even = (jnp.arange(HD) % 2 == 0)[None, :]
    cos_f = jnp.concatenate([jnp.ones((tc, HD), f32), cos], 0)
    sin_a = jnp.concatenate([jnp.zeros((tc, HD), f32), jnp.where(even, -sin, 0.0)], 0)
    sin_b = jnp.concatenate([jnp.zeros((tc, HD), f32), jnp.where(even, 0.0, sin)], 0)
    return cos_f, sin_a, sin_b


N_CHIPS = 4
SHARD = N_IN // N_CHIPS


def _group_ranges():
    return dict(M=[(S_MG, 4 * D), (S_GA, BRW), (S_GB, BRW), (S_GC, BRW), (S_GD, BRW)], A=[(S_Q, W_A)],
                C=[(S_B, 3 * BRW), (S_DA, 2 * BRW)], G=[(S_CQ, 2 * C_KW + BRW), (S_R, 2 * C_RANK)])


def _group_weights(w4):
    out = {}
    for k, ranges in _group_ranges().items():
        parts = []
        for a, n in ranges:
            n = LANE if (k, a) == ("G", S_R) else n
            while n > 0:
                s, r = divmod(a, SHARD)
                m = min(n, SHARD - r)
                parts.append(w4[s, r:r + m])
                a, n = a + m, n - m
        out[k] = jnp.concatenate(parts, 0)
    return out


def _ungroup(g):
    secs = []
    for k, ranges in _group_ranges().items():
        off = 0
        for a, n in ranges:
            secs.append((a, g[k][off:off + n]))
            off += n
    return jnp.concatenate([v for _, v in sorted(secs, key=lambda t: t[0])], 0)


PROJ_TN = dict(M=2048, A=1024, C=1280, G=1152)
DU_TK = dict(M=2048, A=1024, C=BRW, G=1152)
DWP_TN = dict(M=768, A=1024, C=BRW, G=1152)


def _gate_weights(w2_l, gb_l):
    w = jnp.zeros((LANE, 2 * C_KW), f32)
    w = w.at[0:C_RANK, 0:C_KW].set(w2_l[0]).at[C_RANK:2 * C_RANK, C_KW:2 * C_KW].set(w2_l[1])
    return w, jnp.concatenate([gb_l[0], gb_l[1]])[None, :]


def _local_step(x1, c1, ctx1, tgt1, c_ctx, b_mod, weights_of, q_norm, k_norm, b_conv, w2, gb, c_norm, d_conv_w,
                d_conv_b, d_norm_g, d_norm_b, grads_done, ln_g, ln_b, tm, token=None):
    tc, tl = ctx1.shape[0], x1.shape[0]
    T = tc + tl
    rc = min(256, tc)
    tmb = tm // 2
    tmm = 768 if T % 768 == 0 else tm
    rope = _rope_tables(tc, tl)
    cin = jnp.concatenate([c1, c_ctx[None, :], jnp.zeros((SUB - 2, D), f32)], 0)
    if token is not None:
        cin = cin + token[:, 0:1]
    row = lambda v: v[None, :]

    h = jnp.concatenate([ctx1, x1], 0)
    saved, wp, w_br, w_out, w_mod, modv = [], *([None] * DEPTH for _ in range(5))
    for l in range(DEPTH):
        wp[l], merge_weights, w_mod[l] = weights_of(l, h)
        modv[l] = _mod_fwd(cin, w_mod[l], b_mod[l], f"mod_fwd{l}")
        u = _ln_fwd(h, modv[l], tc, tm, f"ln_fwd{l}")
        P = {k: _matmul(u, wp[l][k], "nt", tmm, PROJ_TN[k], D, f"proj{l}{k}", out_dtype=bf16) for k in GROUPS}
        qn, kn, vb = _prep_fwd(P["A"], row(q_norm[l]), row(k_norm[l]), rope, tm, f"prep_fwd{l}")
        ya = _attn_fwd(qn, kn, vb, tc, tm, f"attn_fwd{l}")
        yb, hh = _conv_fwd(P["C"], b_conv[l], d_conv_w[l], row(d_conv_b[l]), tc, tl, rc, f"conv_fwd{l}")
        w2p, b2p = _gate_weights(w2[l], gb[l])
        gla = _gla_fwd(P["G"], w2p, b2p, tc, f"gla_fwd{l}")
        o2, ssave = gla[:2], gla[2:]
        w_br[l], w_out[l] = merge_weights(o2[0])
        hn = _merge_fwd(h, modv[l], ya, yb, o2, hh, P["M"], row(c_norm[l]), row(d_norm_g[l]), row(d_norm_b[l]),
                        row(ln_g[l]), row(ln_b[l]), w_br[l], w_out[l], tc, tm, f"merge_fwd{l}")
        saved.append((h, u, P, qn, kn, vb, ya, yb, hh, o2, ssave, w2p, b2p))
        h = hn

    dh, lparts = _loss_kernel(h, tgt1, tc, tm, "loss")
    loss = jnp.sum(lparts[:, 0, 0])

    g = {k: [None] * DEPTH for k in ("wp", "q_norm", "k_norm", "b_conv", "w2", "gb", "c_norm", "d_conv_w", "d_conv_b",
                                     "d_norm_g", "d_norm_b", "w_br", "w_out", "ln_g", "ln_b", "modv")}
    for l in reversed(range(DEPTH)):
        h_in, u, P, qn, kn, vb, ya, yb, hh, o2, ssave, w2p, b2p = saved[l]
        dP = {}
        (dh_res, dm_mg, dya, dyb, doc, dhh, dP["M"], br, z, acc, dy, dv5, dvd) = _merge_bwd(
            dh, h_in, modv[l], ya, yb, o2, hh, P["M"], row(c_norm[l]), row(d_norm_g[l]), row(d_norm_b[l]),
            row(ln_g[l]), row(ln_b[l]), w_br[l], w_out[l], tc, tmb, f"merge_bwd{l}")
        g["w_br"][l] = _matmul_tn_batched(br, z, N_CHIPS, f"dwbr{l}")
        g["w_out"][l] = _matmul(acc, dy, "tn", D, D, T, f"dwout{l}", out_dtype=bf16)
        tk = grads_done(l, {k: g[k][l] for k in ("w_br", "w_out")})
        qg_l = row(q_norm[l]) if tk is None else row(q_norm[l]) + tk[0:1, :]
        v5 = jnp.sum(dv5, 0)
        g["c_norm"][l], g["d_norm_g"][l], g["d_norm_b"][l] = v5[0], v5[1], v5[2]
        vd = jnp.sum(dvd, 0)
        g["ln_g"][l], g["ln_b"][l] = vd[0], vd[1]
        dqn, dkn, dv = _attn_bwd(qn, kn, vb, dya, tc, tm, f"attn_bwd{l}")
        dP["A"], dqk = _prep_bwd(P["A"], dqn, dkn, dv, qg_l, row(k_norm[l]), rope, tm, f"prep_bwd{l}")
        dqk = jnp.sum(dqk, 0)
        g["q_norm"][l], g["k_norm"][l] = dqk[0], dqk[1]
        dP["C"], dwb, dwd, dbd = _conv_bwd(P["C"], dyb, dhh, b_conv[l], d_conv_w[l], tc, tl, rc, f"conv_bwd{l}")
        g["b_conv"][l], g["d_conv_w"][l], g["d_conv_b"][l] = dwb, dwd, dbd[0]
        dpf, dpb, dw2p, db2p = _gla_bwd(P["G"], w2p, b2p, ssave, doc, tc, f"gla_bwd{l}")
        dP["G"] = _sum_dirs(dpf, dpb, tm, f"gla_sum{l}")
        db2p = db2p[0]
        g["w2"][l] = jnp.stack([dw2p[0:C_RANK, 0:C_KW], dw2p[C_RANK:2 * C_RANK, C_KW:2 * C_KW]])
        g["gb"][l] = jnp.stack([db2p[0:C_KW], db2p[C_KW:2 * C_KW]])
        g["wp"][l] = {k: _matmul(dP[k], u, "tn", DWP_TN[k], D, T, f"dwp{l}{k}", out_dtype=bf16) for k in GROUPS}
        tk = grads_done(l, {"wp": g["wp"][l]})
        du = _matmul_groups(dP, wp[l], DU_TK, tmm, f"du{l}", after=tk)
        dh, dm_ln = _ln_bwd(du, h_in, dh_res, modv[l], tc, tm, f"ln_bwd{l}", latent_only=(l == 0))
        g["modv"][l] = jnp.sum(dm_mg, 0) + jnp.sum(dm_ln, 0)

    dmodv = jnp.stack(g.pop("modv"))
    g["w_mod"], dcin = _mod_bwd(cin, w_mod, dmodv)
    g["b_mod"] = dmodv[:, 0, :] + dmodv[:, 1, :]
    g["c_ctx"] = jnp.sum(dcin, (0, 1))[1]
    return loss, dh, g


HALF_TL = 256


TILE_BYTES = 1 << 20


def _row_tile(rows, cols, itemsize=4):
    tr = min(rows, 128)
    while rows % (2 * tr) == 0 and 2 * tr * cols * itemsize <= TILE_BYTES:
        tr *= 2
    return tr


def _adamw(w, g, m, v, name, tr=None, after=None):
    L, R, C = w.shape
    tr = _row_tile(R, C) if tr is None else tr
    if R % tr == 0:
        grid, spec = (L, R // tr), pl.BlockSpec((None, tr, C), lambda l, i: (l, i, 0))
    elif R * C * 4 <= (1 << 20):
        grid, spec = (L, 1), pl.BlockSpec((None, R, C), lambda l, i: (l, 0, 0))
    else:
        grid, spec = (L, C // HALF_TL), pl.BlockSpec((None, R, HALF_TL), lambda l, i: (l, 0, i))

    def body(w_ref, g_ref, m_ref, v_ref, *rest):
        go_ref, d_ref, nm_ref, nv_ref = rest[-4:]
        gg = g_ref[...]
        go_ref[...] = gg
        nm = B1 * m_ref[...] + (1.0 - B1) * gg
        nv = B2 * v_ref[...] + (1.0 - B2) * (gg * gg)
        m_hat = nm / (1.0 - B1 ** STEP)
        v_hat = nv / (1.0 - B2 ** STEP)
        d_ref[...] = -LR * (m_hat / (jnp.sqrt(v_hat) + AEPS) + WD * w_ref[...])
        nm_ref[...] = nm
        nv_ref[...] = nv

    return pl.pallas_call(
        body, name=name, grid=grid, in_specs=[spec] * 4 + ([] if after is None else [pl.BlockSpec(memory_space=pl.ANY)]),
        out_specs=[spec] * 4, out_shape=[jax.ShapeDtypeStruct((L, R, C), f32)] * 4,
        compiler_params=_cparams(("parallel", "parallel")),
    )(w, g, m, v, *([] if after is None else [after]))


MESH = pl.DeviceIdType.MESH
ANY = pl.BlockSpec(memory_space=pl.ANY)


def _place():
    x, y, c = lax.axis_index("x"), lax.axis_index("y"), lax.axis_index("c")
    chips = [(1 - x, y), (x, 1 - y), (1 - x, 1 - y)]
    return x, y, c, chips


def _half(ref, c, axis):
    n = ref.shape[axis] // 2
    last = axis in (-1, ref.ndim - 1)
    idx = [slice(None)] * ref.ndim
    idx[axis] = pl.ds(pl.multiple_of(c * n, LANE if last else SUB), n)
    return ref.at[tuple(idx)]


def _half_shape(shape, axis):
    s = list(shape)
    s[axis] //= 2
    return tuple(s)


def _all_gather(arrs, axes, name):
    n = len(arrs)

    def body(*refs):
        ins, outs = refs[:n], refs[n:2 * n]
        send, recv = refs[2 * n:]
        x, y, c, chips = _place()
        me, sib = 2 * x + y, (x, y, 1 - c)

        def copy(a, k, chip_idx, cc, to, src=None):
            blk = _half(outs[a].at[chip_idx], cc, axes[a])
            return pltpu.make_async_remote_copy(src_ref=blk if src is None else src, dst_ref=blk,
                                                send_sem=send.at[7 * a + k], recv_sem=recv.at[7 * a + k],
                                                device_id=to, device_id_type=MESH)

        own = [pltpu.make_async_remote_copy(src_ref=ins[a], dst_ref=outs[a].at[me], send_sem=send.at[7 * a + 6],
                                            recv_sem=recv.at[7 * a + 6], device_id=sib, device_id_type=MESH)
               for a in range(n)]
        first = own + [copy(a, j, me, c, (*chip, c), src=_half(ins[a], c, axes[a]))
                       for a in range(n) for j, chip in enumerate(chips)]
        for cp in first:
            cp.start()
        passed = []
        for a in range(n):
            for j, chip in enumerate(chips):
                k = 2 * chip[0] + chip[1]
                copy(a, j, k, c, sib).wait_recv()
                fwd = copy(a, 3 + j, k, c, sib)
                fwd.start()
                passed.append(fwd)
        for a in range(n):
            own[a].wait_recv()
            for j, chip in enumerate(chips):
                copy(a, 3 + j, 2 * chip[0] + chip[1], 1 - c, sib).wait_recv()
        for cp in first + passed:
            cp.wait_send()

    return pl.pallas_call(
        body, name=name, in_specs=[ANY] * n, out_specs=[ANY] * n,
        out_shape=[jax.ShapeDtypeStruct((N_CHIPS,) + a.shape, a.dtype) for a in arrs],
        scratch_shapes=[pltpu.SemaphoreType.DMA((7 * n,)), pltpu.SemaphoreType.DMA((7 * n,))],
    )(*arrs)


def _sibling_halves(arrs, axes, name):
    n = len(arrs)

    def body(*refs):
        ins, outs = refs[:n], refs[n:2 * n]
        send, recv = refs[2 * n:]
        x, y, c, _ = _place()
        cps = [pltpu.make_async_remote_copy(src_ref=_half(ins[a], 1 - c, axes[a] + 1), dst_ref=outs[a], send_sem=send.at[a],
                                            recv_sem=recv.at[a], device_id=(x, y, 1 - c), device_id_type=MESH)
               for a in range(n)]
        for cp in cps:
            cp.start()
        for cp in cps:
            cp.wait()

    return pl.pallas_call(
        body, name=name, in_specs=[ANY] * n, out_specs=[ANY] * n,
        out_shape=[jax.ShapeDtypeStruct(_half_shape(a.shape, axes[i] + 1), a.dtype) for i, a in enumerate(arrs)],
        scratch_shapes=[pltpu.SemaphoreType.DMA((n,)), pltpu.SemaphoreType.DMA((n,))],
    )(*arrs)


def _add_half(gfull, land, cidx, axis, name, tr=None, out_dtype=bf16):
    _, hr, hc = land.shape
    if axis == 0:
        tr = min(tr, hr) if tr else _row_tile(hr, hc)
        nb, blk = hr // tr, (None, tr, hc)
        g_spec = pl.BlockSpec(blk, lambda s, i, cr: (s, cr[0] * nb + i, 0))
        l_spec = pl.BlockSpec(blk, lambda s, i, cr: (s, i, 0))
    else:
        nb, blk = hc // HALF_TL, (None, hr, HALF_TL)
        g_spec = pl.BlockSpec(blk, lambda s, i, cr: (s, 0, cr[0] * nb + i))
        l_spec = pl.BlockSpec(blk, lambda s, i, cr: (s, 0, i))

    def body(c_ref, g_ref, l_ref, o_ref):
        o_ref[...] = (g_ref[...].astype(f32) + l_ref[...].astype(f32)).astype(o_ref.dtype)

    return pl.pallas_call(
        body, name=name,
        grid_spec=pltpu.PrefetchScalarGridSpec(
            num_scalar_prefetch=1, grid=(N_CHIPS, nb), in_specs=[g_spec, l_spec], out_specs=l_spec),
        out_shape=jax.ShapeDtypeStruct((N_CHIPS, hr, hc), out_dtype),
        compiler_params=_cparams(("parallel", "parallel")),
    )(cidx, gfull, land)


def _sum_chips(land, own, place, axis, layer, into, name, tr=None):
    _, hr, hc = land.shape
    fresh = not hasattr(into, "dtype")
    shape = tuple(into) if fresh else into.shape
    if axis == 0:
        tr = min(tr, hr) if tr else _row_tile(hr, 4 * hc, 2)
        nb, blk = hr // tr, (tr, hc)
        l_map, m_map = (lambda i, p: (0, i, 0)), (lambda i, p: (p[0], i, 0))
        o_map = lambda i, p: (layer, p[1] * nb + i, 0)
    else:
        nb, blk = hc // HALF_TL, (hr, HALF_TL)
        l_map, m_map = (lambda i, p: (0, 0, i)), (lambda i, p: (p[0], 0, i))
        o_map = lambda i, p: (layer, 0, p[1] * nb + i)

    def body(p_ref, l_ref, o_ref, *rest):
        me = p_ref[0]
        mine = o_ref[...].astype(f32)
        acc = None
        for k in range(N_CHIPS):
            t = jnp.where(me == k, mine, l_ref[k].astype(f32))
            acc = t if acc is None else acc + t
        rest[-1][...] = acc

    return pl.pallas_call(
        body, name=name,
        grid_spec=pltpu.PrefetchScalarGridSpec(
            num_scalar_prefetch=1, grid=(nb,),
            in_specs=[pl.BlockSpec((N_CHIPS,) + blk, l_map), pl.BlockSpec((None,) + blk, m_map)] + ([] if fresh else [ANY]),
            out_specs=pl.BlockSpec((None,) + blk, o_map)),
        out_shape=jax.ShapeDtypeStruct(shape, f32),
        input_output_aliases={} if fresh else {3: 0},
        compiler_params=_cparams(("parallel",)),
    )(place, land, own, *([] if fresh else [into]))


def _sibling_fill(arrs, axes, name):
    n = len(arrs)

    def body(*refs):
        outs = refs[n:2 * n]
        send, recv = refs[2 * n:]
        x, y, c, _ = _place()
        cps = [pltpu.make_async_remote_copy(src_ref=_half(outs[a], c, axes[a] + 1), dst_ref=_half(outs[a], c, axes[a] + 1),
                                            send_sem=send.at[a], recv_sem=recv.at[a], device_id=(x, y, 1 - c),
                                            device_id_type=MESH) for a in range(n)]
        for cp in cps:
            cp.start()
        for a in range(n):
            blk = _half(outs[a], 1 - c, axes[a] + 1)
            pltpu.make_async_remote_copy(src_ref=blk, dst_ref=blk, send_sem=send.at[a], recv_sem=recv.at[a],
                                         device_id=(x, y, 1 - c), device_id_type=MESH).wait_recv()
        for cp in cps:
            cp.wait_send()

    return pl.pallas_call(
        body, name=name, in_specs=[ANY] * n, out_specs=[ANY] * n,
        out_shape=[jax.ShapeDtypeStruct(a.shape, a.dtype) for a in arrs],
        input_output_aliases={a: a for a in range(n)},
        scratch_shapes=[pltpu.SemaphoreType.DMA((n,)), pltpu.SemaphoreType.DMA((n,))],
    )(*arrs)


HBM = pl.BlockSpec(memory_space=pltpu.HBM)
SEM = pl.BlockSpec(memory_space=pltpu.SEMAPHORE)
EFFECT = pltpu.SideEffectType.DATAFLOW_SIDE_EFFECTING
PEERS = 4


def _split_copies(srcs, lands, send, recv, gather, axes=None):
    x, y, c, chips = _place()
    me = 2 * x + y
    if axes is not None:
        out = []
        for a in range(len(srcs)):
            sems = dict(send_sem=send.at[PEERS * a], recv_sem=recv.at[PEERS * a], device_id=(x, y, 1 - c), device_id_type=MESH)
            copy = pltpu.make_async_remote_copy(src_ref=_half(srcs[a], 1 - c, axes[a] + 1), dst_ref=lands[a], **sems)
            out.append((copy, copy))
        return out
    peers = [((*chip, c), 2 * chip[0] + chip[1]) for chip in chips] + ([((x, y, 1 - c), me)] if gather else [])
    out = []
    for a in range(len(srcs)):
        for j, (dev, k) in enumerate(peers):
            src = srcs[a] if gather else srcs[a].at[k]
            sems = dict(send_sem=send.at[PEERS * a + j], recv_sem=recv.at[PEERS * a + j], device_id=dev, device_id_type=MESH)
            out.append((pltpu.make_async_remote_copy(src_ref=src, dst_ref=lands[a].at[me], **sems),
                        pltpu.make_async_remote_copy(src_ref=src, dst_ref=lands[a].at[k], **sems)))
    return out


def _split_start(srcs, gather, after, name, axes=None):
    n = len(srcs)
    if axes is not None:
        lands = [lax.empty(_half_shape(s.shape, axes[a] + 1), s.dtype) for a, s in enumerate(srcs)]
    else:
        lands = [lax.empty(((N_CHIPS,) + s.shape) if gather else s.shape, s.dtype) for s in srcs]

    def body(*refs):
        send, recv = refs[2 * n + 1], refs[2 * n + 2]
        for start, _ in _split_copies(refs[:n], refs[n:2 * n], send, recv, gather, axes):
            start.start()
        refs[-1][...] = jnp.zeros_like(refs[-1])

    sems = pltpu.SemaphoreType.DMA((PEERS * n,))
    hbm = lambda a: pltpu.with_memory_space_constraint(a, pltpu.HBM)
    out = pl.pallas_call(
        body, name=name,
        out_shape=(sems, sems, *[pltpu.HBM(a.shape, a.dtype) for a in srcs + lands], jax.ShapeDtypeStruct((SUB, LANE), f32)),
        in_specs=[HBM] * (2 * n) + [ANY], out_specs=(SEM, SEM, *[HBM] * (2 * n), pl.BlockSpec(memory_space=pltpu.VMEM)),
        input_output_aliases={i: 2 + i for i in range(2 * n)},
        compiler_params=pltpu.CompilerParams(has_side_effects=EFFECT),
    )(*[hbm(a) for a in srcs + lands], after)
    return out[0], out[1], list(out[2:2 + n]), list(out[2 + n:2 + 2 * n]), out[-1]


def _split_wait(send, recv, srcs, lands, gather, after, name, axes=None):
    n = len(srcs)

    def body(*refs):
        for start, arrival in _split_copies(refs[:n], refs[n:2 * n], refs[2 * n], refs[2 * n + 1], gather, axes):
            start.wait_send()
            arrival.wait_recv()

    out = pl.pallas_call(
        body, name=name, out_shape=[pltpu.HBM(a.shape, a.dtype) for a in srcs + lands],
        in_specs=[HBM] * (2 * n) + [SEM, SEM, ANY], out_specs=[HBM] * (2 * n),
        input_output_aliases={i: i for i in range(2 * n)},
        compiler_params=pltpu.CompilerParams(has_side_effects=EFFECT),
    )(*srcs, *lands, send, recv, after)
    return list(out[:n]), list(out[n:])


N_DEV = 8


def _all_reduce_small(v, name):
    R = v.shape[0]

    def body(v_ref, o_ref, land_ref, send, recv):
        x, y, c, _ = _place()
        me = 4 * x + 2 * y + c
        land_ref[me] = v_ref[...]
        cps = []
        for m in range(1, N_DEV):
            px, py, pc = [(1 - q) if (m >> s) & 1 else q for q, s in ((x, 2), (y, 1), (c, 0))]
            cps.append((pltpu.make_async_remote_copy(src_ref=v_ref, dst_ref=land_ref.at[me], send_sem=send.at[m - 1],
                                                     recv_sem=recv.at[m - 1], device_id=(px, py, pc), device_id_type=MESH),
                        4 * px + 2 * py + pc, m))
        for cp, *_ in cps:
            cp.start()
        for cp, peer, m in cps:
            pltpu.make_async_remote_copy(src_ref=v_ref, dst_ref=land_ref.at[peer], send_sem=send.at[m - 1],
                                         recv_sem=recv.at[m - 1], device_id=(x, y, c), device_id_type=MESH).wait_recv()
        for cp, *_ in cps:
            cp.wait_send()
        acc = land_ref[0]
        for k in range(1, N_DEV):
            acc = acc + land_ref[k]
        o_ref[...] = acc

    vm = pl.BlockSpec(memory_space=pltpu.VMEM)
    return pl.pallas_call(
        body, name=name, in_specs=[vm], out_specs=vm, out_shape=jax.ShapeDtypeStruct(v.shape, f32),
        scratch_shapes=[pltpu.VMEM((N_DEV, R, LANE), f32), pltpu.SemaphoreType.DMA((N_DEV - 1,)),
                        pltpu.SemaphoreType.DMA((N_DEV - 1,))],
        compiler_params=pltpu.CompilerParams(vmem_limit_bytes=VMEM_LIMIT),
    )(v)


def _pack_small(arrs, mult=2 * SUB):
    flat = jnp.concatenate([a.reshape(-1) for a in arrs])
    rows = -(-flat.shape[0] // (LANE * mult)) * mult
    return jnp.pad(flat, (0, rows * LANE - flat.shape[0])).reshape(rows, LANE)


def _unpack_small(vec, shapes):
    flat, out, o = vec.reshape(-1), [], 0
    for s in shapes:
        n = int(np.prod(s))
        out.append(flat[o:o + n].reshape(s))
        o += n
    return out


REPL_SMALL = ("c_ctx", "b_mod", "q_norm", "k_norm", "c_norm", "d_conv_b", "d_norm_g", "d_norm_b", "ln_g", "ln_b")
SHARD_SMALL = ("b_conv", "c_gate_w2", "c_gate_b", "d_conv_w")
BIG = ("w_mod", "w_in", "w_br", "w_out")
ORDER = ("c_ctx", "w_mod", "b_mod", "w_in", "q_norm", "k_norm", "b_conv", "c_gate_w2", "c_gate_b", "c_norm", "d_conv_w",
         "d_conv_b", "d_norm_g", "d_norm_b", "w_br", "w_out", "ln_g", "ln_b")


def kernel(x, c, ctx, c_ctx, w_mod, b_mod, w_in, q_norm, k_norm, b_conv, c_gate_w2, c_gate_b, c_norm, d_conv_w, d_conv_b, d_norm_g, d_norm_b, w_br, w_out, ln_g, ln_b, loss_target, m_c_ctx, m_w_mod, m_b_mod, m_w_in, m_q_norm, m_k_norm, m_b_conv, m_c_gate_w2, m_c_gate_b, m_c_norm, m_d_conv_w, m_d_conv_b, m_d_norm_g, m_d_norm_b, m_w_br, m_w_out, m_ln_g, m_ln_b, v_c_ctx, v_w_mod, v_b_mod, v_w_in, v_q_norm, v_k_norm, v_b_conv, v_c_gate_w2, v_c_gate_b, v_c_norm, v_d_conv_w, v_d_conv_b, v_d_norm_g, v_d_norm_b, v_w_br, v_w_out, v_ln_g, v_ln_b):
    W = dict(c_ctx=c_ctx, w_mod=w_mod, b_mod=b_mod, w_in=w_in, q_norm=q_norm, k_norm=k_norm, b_conv=b_conv,
             c_gate_w2=c_gate_w2, c_gate_b=c_gate_b, c_norm=c_norm, d_conv_w=d_conv_w, d_conv_b=d_conv_b,
             d_norm_g=d_norm_g, d_norm_b=d_norm_b, w_br=w_br, w_out=w_out, ln_g=ln_g, ln_b=ln_b)
    M = dict(c_ctx=m_c_ctx, w_mod=m_w_mod, b_mod=m_b_mod, w_in=m_w_in, q_norm=m_q_norm, k_norm=m_k_norm, b_conv=m_b_conv,
             c_gate_w2=m_c_gate_w2, c_gate_b=m_c_gate_b, c_norm=m_c_norm, d_conv_w=m_d_conv_w, d_conv_b=m_d_conv_b,
             d_norm_g=m_d_norm_g, d_norm_b=m_d_norm_b, w_br=m_w_br, w_out=m_w_out, ln_g=m_ln_g, ln_b=m_ln_b)
    V = dict(c_ctx=v_c_ctx, w_mod=v_w_mod, b_mod=v_b_mod, w_in=v_w_in, q_norm=v_q_norm, k_norm=v_k_norm, b_conv=v_b_conv,
             c_gate_w2=v_c_gate_w2, c_gate_b=v_c_gate_b, c_norm=v_c_norm, d_conv_w=v_d_conv_w, d_conv_b=v_d_conv_b,
             d_norm_g=v_d_norm_g, d_norm_b=v_d_norm_b, w_br=v_w_br, w_out=v_w_out, ln_g=v_ln_g, ln_b=v_ln_b)
    chip = 2 * lax.axis_index("x") + lax.axis_index("y")
    cidx = lax.axis_index("c").astype(jnp.int32).reshape(1)

    place = jnp.stack([chip, lax.axis_index("c")]).astype(jnp.int32)

    AXIS = dict(w_in=1, w_mod=0, w_br=0, w_out=0)
    ex = dict(w_in=lambda a: jnp.swapaxes(a, 1, 2), w_mod=lambda a: a.reshape(1, DEPTH * D, -1),
              w_br=lambda a: a.reshape(DEPTH, 4 * BRW, -1), w_out=lambda a: a)
    Wx, Mx, Vx = ({k: ex[k](P_[k]) for k in BIG} for P_ in (W, M, V))

    LAYER, MERGE = ("w_in", "w_br", "w_out"), ("w_br", "w_out")
    small_shard = _pack_small([W[k] for k in SHARD_SMALL])
    keys0 = ("w_in", "w_mod")
    sent = lambda k, l: (w_mod[l] if k == "w_mod" else Wx[k][l]).astype(bf16)
    got = _all_gather([sent(k, 0) for k in keys0] + [small_shard], [AXIS[k] for k in keys0] + [0], "all_gather0")
    smalls = [_unpack_small(got[-1][s], [W[k].shape for k in SHARD_SMALL]) for s in range(N_CHIPS)]
    full = {k: jnp.concatenate([smalls[s][i] for s in range(N_CHIPS)], axis=-1) for i, k in enumerate(SHARD_SMALL)}
    ag0b = _split_start([sent(k, 0) for k in MERGE], True, got[0], "all_gather0b_start")
    ag1 = _split_start([sent(k, 1) for k in keys0], True, ag0b[4], "all_gather1_start")
    ag1b = _split_start([sent(k, 1) for k in MERGE], True, ag1[4], "all_gather1b_start")

    def merge_form(w_br4, w_out4):
        return jnp.moveaxis(w_br4.reshape(N_CHIPS, 4, BRW, D // N_CHIPS), 0, 2).reshape(4, BRW, D), w_out4.reshape(D, D)

    def weights_of(l, h):
        first = got if l == 0 else _split_wait(*ag1[:4], True, h, "all_gather1_wait")[1]
        flight = (ag0b, ag1b)[l]
        return (_group_weights(first[0]),
                lambda after: merge_form(*_split_wait(*flight[:4], True, after, f"all_gather{l}b_wait")[1]), first[1])

    red = {k: Wx[k].shape for k in BIG}
    flights, held = {}, {}

    def launch(tag, l, pieces, after=None):
        keys = list(pieces)
        land_a = _sibling_halves([pieces[k] for k in keys], [AXIS[k] for k in keys], f"rs_sibling_halves{tag}")
        pair = [_add_half(pieces[k], la, cidx, AXIS[k], f"rs_pair_sum{tag}_{k}") for k, la in zip(keys, land_a)]
        after = jnp.zeros((SUB, LANE), f32) if after is None else after
        flights[tag] = (l, keys, _split_start(pair, False, after, f"rs_chip_exchange{tag}_start"))
        return flights[tag][2][4]

    def land(tag, after):
        l, keys, flight = flights.pop(tag)
        pair, land_b = _split_wait(*flight[:4], False, after, f"rs_chip_exchange{tag}_wait")
        for k, lb, pr in zip(keys, land_b, pair):
            red[k] = _sum_chips(lb, pr, place, AXIS[k], l, red[k], f"rs_chip_sum{tag}_{k}")

    def grads_done(l, gl):
        if "wp" in gl:
            pieces = dict(w_in=_ungroup(gl["wp"]).reshape(N_CHIPS, SHARD, D))
            return launch("0c", 0, pieces) if l == 0 else launch("1", 1, {**pieces, **held.pop(1)})
        pieces = dict(w_br=gl["w_br"].reshape(N_CHIPS, 4 * BRW, D // N_CHIPS), w_out=gl["w_out"].reshape(N_CHIPS, D // N_CHIPS, D))
        if l == 0:
            return launch("0b", 0, pieces)
        held[1] = pieces
        return None

    loss, gx, g = _local_step(
        x[0], c, ctx[0], loss_target[0], c_ctx, b_mod, weights_of, q_norm, k_norm, full["b_conv"],
        full["c_gate_w2"], full["c_gate_b"], c_norm, full["d_conv_w"], d_conv_b, d_norm_g, d_norm_b,
        grads_done, ln_g, ln_b, tm=256, token=ag1b[4])
    g["c_gate_w2"], g["c_gate_b"] = g.pop("w2"), g.pop("gb")
    loss = lax.psum(loss, ("x", "y", "c"))

    w_mod_pieces = g["w_mod"].reshape(N_CHIPS, DEPTH * D, 3 * D // N_CHIPS)
    g = {k: (jnp.stack(v) if isinstance(v, list) else v) for k, v in g.items() if k not in ("wp", "w_br", "w_out", "w_mod")}

    small_names = REPL_SMALL + SHARD_SMALL
    gs = _all_reduce_small(_pack_small([g[k] for k in small_names]), "all_reduce_small")
    gsm = dict(zip(small_names, _unpack_small(gs, [g[k].shape for k in small_names])))
    for k in SHARD_SMALL:
        wdt = W[k].shape[-1]
        gsm[k] = lax.dynamic_slice_in_dim(gsm[k], chip * wdt, wdt, axis=gsm[k].ndim - 1)

    grad, delta, new_m, new_v = {}, {}, {}, {}

    def adamw_big(keys, after):
        filled = _sibling_fill([red[k] for k in keys], [AXIS[k] for k in keys], "rs_sibling_fill_" + keys[0])
        for k, r in zip(keys, filled):
            back = (lambda a: jnp.swapaxes(a, 1, 2)) if k == "w_in" else (lambda a: a.reshape(W[k].shape))
            g_, d_, m_, v_ = _adamw(Wx[k], r, Mx[k], Vx[k], f"adamw_{k}", after=after)
            grad[k], delta[k], new_m[k], new_v[k] = back(g_), back(d_), back(m_), back(v_)
        return d_

    token = launch("0d", 0, {"w_mod": w_mod_pieces}, after=gs)
    land("1", gx)
    land("0b", gx)
    last = adamw_big(MERGE, token)
    shapes = [W[k].shape for k in small_names]
    _, d_, m_, v_ = _adamw(*[_pack_small([P_[k] for k in small_names])[None] for P_ in (W, gsm, M, V)], "adamw_small", after=last)
    for k, dd, mm_, vv in zip(small_names, _unpack_small(d_, shapes), _unpack_small(m_, shapes), _unpack_small(v_, shapes)):
        grad[k], delta[k], new_m[k], new_v[k] = gsm[k], dd, mm_, vv
    land("0c", d_)
    land("0d", d_)
    adamw_big(("w_in", "w_mod"), None)

    return (loss, gx[None], *[grad[k] for k in ORDER], *[delta[k] for k in ORDER], *[new_m[k] for k in ORDER],
            *[new_v[k] for k in ORDER])
```

```python
import functools

import jax
import jax.numpy as jnp
import numpy as np
from jax import lax
from jax.experimental import pallas as pl
from jax.experimental.pallas import tpu as pltpu

f32 = jnp.float32
bf16 = jnp.bfloat16

D = 1024
DEPTH = 2
GRID_W = 64
BRW = 512
HD = 128
A_HEADS = 4
C_HEADS = 4
C_KW = 256
C_RANK = 16
C_TAU = 16.0
CH = 128
KB = 3
KD = 31
ALPHA = (2 * DEPTH) ** 0.25
EPS = 1e-6
ROPE_THETA = 10000.0
N_IN = 10784
LR, B1, B2, AEPS, WD, STEP = 0.001, 0.9, 0.999, 1e-08, 0.01, 10

W_M, W_A, W_C, W_G = 4 * D + 4 * BRW, 1024, 5 * BRW, 1152
GROUPS = ("M", "A", "C", "G")
M_GA, M_GB, M_GC, M_GD = 4 * D, 4 * D + BRW, 4 * D + 2 * BRW, 4 * D + 3 * BRW
A_K, A_V = 512, 768
G_K, G_V, G_R = 256, 512, 1024
S_Q, S_GA, S_B, S_C, S_X, S_GB, S_CQ, S_CV, S_GC, S_R, S_DA, S_DG, S_GD, S_MG = (
    0, 1024, 1536, 2048, 2560, 3072, 3584, 4096, 4608, 5120, 5152, 5664, 6176, 6688)

LANE = 128
SUB = 8
VMEM_LIMIT = 56 * 1024 * 1024
CONV_PAD = 16
GLA_SUB = 16
GLA_CLAMP = 60.0


def _cparams(sem, vmem=VMEM_LIMIT):
    return pltpu.CompilerParams(dimension_semantics=sem, vmem_limit_bytes=vmem)


def _dg(a, b, ca, cb):
    return lax.dot_general(a.astype(bf16), b.astype(bf16), (((ca,), (cb,)), ((), ())),
                           preferred_element_type=f32)


@jax.custom_vjp
def mm(a, b):
    return _dg(a, b, 1, 0)


mm.defvjp(lambda a, b: (_dg(a, b, 1, 0), (a, b)),
          lambda r, ct: (_dg(ct, r[1], 1, 1).astype(r[0].dtype), _dg(r[0], ct, 0, 0).astype(r[1].dtype)))


@jax.custom_vjp
def mm_nt(a, b):
    return _dg(a, b, 1, 1)


mm_nt.defvjp(lambda a, b: (_dg(a, b, 1, 1), (a, b)),
             lambda r, ct: (_dg(ct, r[1], 1, 0).astype(r[0].dtype), _dg(ct, r[0], 0, 0).astype(r[1].dtype)))


@jax.custom_vjp
def mm_tn(a, b):
    return _dg(a, b, 0, 0)


mm_tn.defvjp(lambda a, b: (_dg(a, b, 0, 0), (a, b)),
             lambda r, ct: (_dg(r[1], ct, 1, 1).astype(r[0].dtype), _dg(r[0], ct, 1, 0).astype(r[1].dtype)))


@jax.custom_vjp
def _sigmoid(x):
    return 0.5 * jnp.tanh(0.5 * x) + 0.5


def _sigmoid_fwd(x):
    s = _sigmoid(x)
    return s, s


_sigmoid.defvjp(_sigmoid_fwd, lambda s, ct: (ct * (s - s * s),))


@jax.custom_vjp
def _silu(x):
    return x * _sigmoid(x)


def _silu_fwd(x):
    s = _sigmoid(x)
    return x * s, (x, s)


_silu.defvjp(_silu_fwd, lambda r, ct: (ct * (r[1] + r[0] * (r[1] - r[1] * r[1])),))


def _ln(x):
    mu = jnp.mean(x, -1, keepdims=True)
    xc = x - mu
    var = jnp.mean(xc * xc, -1, keepdims=True)
    return xc * lax.rsqrt(var + EPS)


def _rms(x, g):
    return x * lax.rsqrt(jnp.mean(x * x, -1, keepdims=True) + EPS) * g


@jax.custom_vjp
def _rope(x, cos_f, sin_a, sin_b):
    return x * cos_f + pltpu.roll(x, HD - 1, 1) * sin_a + pltpu.roll(x, 1, 1) * sin_b


def _rope_fwd(x, cos_f, sin_a, sin_b):
    return _rope(x, cos_f, sin_a, sin_b), (cos_f, sin_a, sin_b)


def _rope_bwd(r, ct):
    cos_f, sin_a, sin_b = r
    dx = ct * cos_f + pltpu.roll(ct * sin_a, 1, 1) + pltpu.roll(ct * sin_b, HD - 1, 1)
    return dx, jnp.zeros_like(cos_f), jnp.zeros_like(sin_a), jnp.zeros_like(sin_b)


_rope.defvjp(_rope_fwd, _rope_bwd)


def _row_ids(i, tm):
    return i * tm + lax.broadcasted_iota(jnp.int32, (tm, 1), 0)


def _partial_rows(ref, rows):
    n = len(rows)
    for k, r in enumerate(rows):
        ref[k:k + 1, :] = r
    ref[n:SUB, :] = jnp.zeros((SUB - n, ref.shape[-1]), f32)


def _matmul(a, b, mode, tm, tn, tk, name, out_dtype=f32, add=None, after=None):
    sect = a.ndim == 3
    a2 = (a.shape[1], a.shape[0] * a.shape[2]) if sect else a.shape
    if mode == "nn":
        (M, K), N = a2, b.shape[1]
        a_spec = pl.BlockSpec((None, tm, tk), lambda j, i, k: (k, i, 0)) if sect else pl.BlockSpec((tm, tk), lambda j, i, k: (i, k))
        b_spec = pl.BlockSpec((tk, tn), lambda j, i, k: (k, j))
        ca, cb = 1, 0
        assert not sect or tk == a.shape[2]
    elif mode == "nt":
        (M, K), N = a2, b.shape[0]
        assert not sect
        a_spec = pl.BlockSpec((tm, tk), lambda j, i, k: (i, k))
        b_spec = pl.BlockSpec((tn, tk), lambda j, i, k: (j, k))
        ca, cb = 1, 1
    else:
        (K, M), N = a2, b.shape[1]
        a_spec = pl.BlockSpec((None, tk, tm), lambda j, i, k: (i, k, 0)) if sect else pl.BlockSpec((tk, tm), lambda j, i, k: (k, i))
        b_spec = pl.BlockSpec((tk, tn), lambda j, i, k: (k, j))
        ca, cb = 0, 0
        assert not sect or tm == a.shape[2]
    assert M % tm == 0 and N % tn == 0 and K % tk == 0, (name, M, N, K, tm, tn, tk)
    nk = K // tk

    o_spec = pl.BlockSpec((tm, tn), lambda j, i, k: (i, j))

    def body(a_ref, b_ref, *rest):
        add_ref = rest[0] if add is not None else None
        o_ref, acc_ref = rest[-2:]
        k = pl.program_id(2)
        part = _dg(a_ref[...], b_ref[...], ca, cb)

        @pl.when(k == 0)
        def _():
            acc_ref[...] = part if add_ref is None else part + add_ref[...]

        @pl.when(k > 0)
        def _():
            acc_ref[...] += part

        @pl.when(k == nk - 1)
        def _():
            o_ref[...] = acc_ref[...].astype(o_ref.dtype)

    extra = ([] if add is None else [(o_spec, add)]) + ([] if after is None else [(pl.BlockSpec(memory_space=pl.ANY), after)])
    return pl.pallas_call(
        body, name=name, grid=(N // tn, M // tm, nk),
        in_specs=[a_spec, b_spec] + [s_ for s_, _ in extra], out_specs=o_spec,
        out_shape=jax.ShapeDtypeStruct((M, N), out_dtype),
        scratch_shapes=[pltpu.VMEM((tm, tn), f32)],
        compiler_params=_cparams(("parallel", "parallel", "arbitrary")),
    )(a, b, *[v_ for _, v_ in extra])


def _matmul_groups(a, b, tks, tm, name, after=None):
    keys = list(a)
    M = a[keys[0]].shape[-2]
    N = b[keys[0]].shape[1]
    count = {g: b[g].shape[0] // tks[g] for g in keys}
    first, total = {}, 0
    for g in keys:
        first[g], total = total, total + count[g]

    def k_of(g):
        return lambda s: jnp.clip(s - first[g], 0, count[g] - 1)

    a_specs = [pl.BlockSpec((None, tm, tks[g]), functools.partial(lambda i, s, kk: (kk(s), i, 0), kk=k_of(g)))
               if a[g].ndim == 3 else pl.BlockSpec((tm, tks[g]), functools.partial(lambda i, s, kk: (i, kk(s)), kk=k_of(g)))
               for g in keys]
    b_specs = [pl.BlockSpec((tks[g], N), functools.partial(lambda i, s, kk: (kk(s), 0), kk=k_of(g))) for g in keys]
    n = len(keys)

    def body(*refs):
        o_ref, acc_ref = refs[-2:]
        s = pl.program_id(1)

        @pl.when(s == 0)
        def _():
            acc_ref[...] = jnp.zeros_like(acc_ref)

        for j, g in enumerate(keys):
            @pl.when((s >= first[g]) & (s < first[g] + count[g]))
            def _(j=j):
                acc_ref[...] += _dg(refs[j][...], refs[n + j][...], 1, 0)

        @pl.when(s == total - 1)
        def _():
            o_ref[...] = acc_ref[...]

    extra = [] if after is None else [after]
    return pl.pallas_call(
        body, name=name, grid=(M // tm, total),
        in_specs=a_specs + b_specs + [pl.BlockSpec(memory_space=pl.ANY)] * len(extra),
        out_specs=pl.BlockSpec((tm, N), lambda i, s: (i, 0)),
        out_shape=jax.ShapeDtypeStruct((M, N), f32),
        scratch_shapes=[pltpu.VMEM((tm, N), f32)],
        compiler_params=_cparams(("parallel", "arbitrary")),
    )(*[a[g] for g in keys], *[b[g] for g in keys], *extra)


def _matmul_tn_batched(a, b, ns, name):
    B, K, M = a.shape
    N = b.shape[2] // ns

    def body(a_ref, b_ref, o_ref):
        o_ref[...] = _dg(a_ref[...], b_ref[...], 0, 0).astype(bf16)

    return pl.pallas_call(
        body, name=name, grid=(B, ns),
        in_specs=[pl.BlockSpec((None, K, M), lambda i, s: (i, 0, 0)), pl.BlockSpec((None, K, N), lambda i, s: (i, 0, s))],
        out_specs=pl.BlockSpec((None, None, M, N), lambda i, s: (s, i, 0, 0)),
        out_shape=jax.ShapeDtypeStruct((ns, B, M, N), bf16),
        compiler_params=_cparams(("parallel", "parallel")),
    )(a, b)


MOD_TN = 768


def _mod_fwd(cin, w_mod_l, b_mod_l, name):
    def body(c_ref, w_ref, b_ref, o_ref):
        o_ref[...] = mm(_silu(c_ref[...]), w_ref[...]) + b_ref[...]

    return pl.pallas_call(
        body, name=name, grid=(3 * D // MOD_TN,),
        in_specs=[pl.BlockSpec((SUB, D), lambda j: (0, 0)), pl.BlockSpec((None, D, MOD_TN), lambda j: (j, 0, 0)),
                  pl.BlockSpec((1, MOD_TN), lambda j: (0, j))],
        out_specs=pl.BlockSpec((SUB, MOD_TN), lambda j: (0, j)),
        out_shape=jax.ShapeDtypeStruct((SUB, 3 * D), f32),
        compiler_params=_cparams(("parallel",)),
    )(cin, w_mod_l, b_mod_l[None, :])


def _mod_bwd(cin, w_mods, dmodv):
    nj = 3 * D // MOD_TN

    def body(c_ref, *refs):
        g_ref, dw_ref, dc_ref = refs[DEPTH:]
        w = refs[0][...]
        for l in range(1, DEPTH):
            w = jnp.where(pl.program_id(0) == l, refs[l][...], w)
        _, vjp = jax.vjp(lambda c, w: mm(_silu(c), w), c_ref[...], w.astype(f32))
        dc, dw = vjp(g_ref[...])
        dw_ref[...] = dw.astype(bf16)
        dc_ref[...] = dc

    return pl.pallas_call(
        body, name="mod_bwd", grid=(DEPTH, nj),
        in_specs=[pl.BlockSpec((SUB, D), lambda l, j: (0, 0))]
        + [pl.BlockSpec((None, D, MOD_TN), lambda l, j: (j, 0, 0))] * DEPTH
        + [pl.BlockSpec((None, SUB, MOD_TN), lambda l, j: (l, 0, j))],
        out_specs=[pl.BlockSpec((None, None, D, MOD_TN), lambda l, j: (j, l, 0, 0)),
                   pl.BlockSpec((None, None, SUB, D), lambda l, j: (l, j, 0, 0))],
        out_shape=[jax.ShapeDtypeStruct((nj, DEPTH, D, MOD_TN), bf16),
                   jax.ShapeDtypeStruct((DEPTH, nj, SUB, D), f32)],
        compiler_params=_cparams(("parallel", "parallel")),
    )(cin, *w_mods, dmodv)


def _u_fn(h, m_l, m_c, isctx):
    n = _ln(h)
    shift = jnp.where(isctx, m_c[:, 0:D], m_l[:, 0:D])
    scale = jnp.where(isctx, m_c[:, D:2 * D], m_l[:, D:2 * D])
    return n * (1.0 + scale) + shift


def _ln_fwd(h, modv_l, tc, tm, name):
    T = h.shape[0]

    def body(h_ref, m_ref, u_ref):
        isctx = _row_ids(pl.program_id(0), tm) < tc
        u_ref[...] = _u_fn(h_ref[...], m_ref[0:1, :], m_ref[1:2, :], isctx).astype(bf16)

    return pl.pallas_call(
        body, name=name, grid=(T // tm,),
        in_specs=[pl.BlockSpec((tm, D), lambda i: (i, 0)), pl.BlockSpec((SUB, 3 * D), lambda i: (0, 0))],
        out_specs=pl.BlockSpec((tm, D), lambda i: (i, 0)),
        out_shape=jax.ShapeDtypeStruct((T, D), bf16),
        compiler_params=_cparams(("parallel",)),
    )(h, modv_l)


def _ln_bwd(du, h, dh_res, modv_l, tc, tm, name, latent_only=False):
    T = h.shape[0]
    nt, nct = T // tm, tc // tm

    def body(du_ref, h_ref, r_ref, m_ref, dh_ref, dm_ref):
        isctx = _row_ids(pl.program_id(0), tm) < tc
        _, vjp = jax.vjp(lambda h, ml, mc: _u_fn(h, ml, mc, isctx), h_ref[...], m_ref[0:1, :], m_ref[1:2, :])
        dh, dml, dmc = vjp(du_ref[...])
        dh_ref[...] = dh + r_ref[...]
        _partial_rows(dm_ref, [dml, dmc])

    dh_map = (lambda i: (jnp.maximum(i - nct, 0), 0)) if latent_only else (lambda i: (i, 0))
    return pl.pallas_call(
        body, name=name, grid=(nt,),
        in_specs=[pl.BlockSpec((tm, D), lambda i: (i, 0)), pl.BlockSpec((tm, D), lambda i: (i, 0)),
                  pl.BlockSpec((tm, D), lambda i: (i, 0)), pl.BlockSpec((SUB, 3 * D), lambda i: (0, 0))],
        out_specs=[pl.BlockSpec((tm, D), dh_map), pl.BlockSpec((None, SUB, 3 * D), lambda i: (i, 0, 0))],
        out_shape=[jax.ShapeDtypeStruct((T - tc if latent_only else T, D), f32), jax.ShapeDtypeStruct((nt, SUB, 3 * D), f32)],
        compiler_params=_cparams(("arbitrary",)),
    )(du, h, dh_res, modv_l)


def _prep_fn(q, k, qg, kg, cos_f, sin_a, sin_b):
    qs = [_rope(_rms(q[:, HD * i:HD * (i + 1)], qg), cos_f, sin_a, sin_b) * (HD ** -0.5) for i in range(A_HEADS)]
    ks = [_rope(_rms(k[:, HD * i:HD * (i + 1)], kg), cos_f, sin_a, sin_b) for i in range(A_HEADS // 2)]
    return jnp.concatenate(qs, 1), jnp.concatenate(ks, 1)


def _tok(tm, w, off):
    return pl.BlockSpec((tm, w), lambda i: (i, off // w))


def _vec(w):
    return pl.BlockSpec((1, w), lambda i: (0, 0))


def _prep_fwd(P, qg, kg, rope, tm, name):
    T = P.shape[0]

    def body(q_ref, k_ref, v_ref, qg_ref, kg_ref, c_ref, sa_ref, sb_ref, qn_ref, kn_ref, vb_ref):
        qn, kn = _prep_fn(q_ref[...].astype(f32), k_ref[...].astype(f32), qg_ref[...], kg_ref[...], c_ref[...], sa_ref[...],
                          sb_ref[...])
        qn_ref[...] = qn.astype(bf16)
        kn_ref[...] = kn.astype(bf16)
        vb_ref[...] = v_ref[...].astype(bf16)

    return pl.pallas_call(
        body, name=name, grid=(T // tm,),
        in_specs=[_tok(tm, 512, 0), _tok(tm, 256, A_K), _tok(tm, 256, A_V), _vec(HD), _vec(HD),
                  _tok(tm, HD, 0), _tok(tm, HD, 0), _tok(tm, HD, 0)],
        out_specs=[_tok(tm, 512, 0), _tok(tm, 256, 0), _tok(tm, 256, 0)],
        out_shape=[jax.ShapeDtypeStruct((T, 512), bf16), jax.ShapeDtypeStruct((T, 256), bf16),
                   jax.ShapeDtypeStruct((T, 256), bf16)],
        compiler_params=_cparams(("parallel",)),
    )(P, P, P, qg, kg, *rope)


def _prep_bwd(P, dqn, dkn, dv, qg, kg, rope, tm, name):
    T = P.shape[0]
    nt = T // tm

    def body(q_ref, k_ref, dq_ref, dk_ref, dv_ref, qg_ref, kg_ref, c_ref, sa_ref, sb_ref, o_ref, og_ref):
        tabs = (c_ref[...], sa_ref[...], sb_ref[...])
        _, vjp = jax.vjp(lambda q, k, a, b: _prep_fn(q, k, a, b, *tabs), q_ref[...].astype(f32), k_ref[...].astype(f32),
                         qg_ref[...], kg_ref[...])
        dq, dk, dqg, dkg = vjp((dq_ref[...], dk_ref[...]))
        o_ref[:, 0:A_K] = dq.astype(bf16)
        o_ref[:, A_K:A_V] = dk.astype(bf16)
        o_ref[:, A_V:W_A] = dv_ref[...].astype(bf16)
        _partial_rows(og_ref, [dqg, dkg])

    return pl.pallas_call(
        body, name=name, grid=(nt,),
        in_specs=[_tok(tm, 512, 0), _tok(tm, 256, A_K), _tok(tm, 512, 0), _tok(tm, 256, 0), _tok(tm, 256, 0),
                  _vec(HD), _vec(HD), _tok(tm, HD, 0), _tok(tm, HD, 0), _tok(tm, HD, 0)],
        out_specs=[_tok(tm, W_A, 0), pl.BlockSpec((None, SUB, HD), lambda i: (i, 0, 0))],
        out_shape=[jax.ShapeDtypeStruct((T, W_A), bf16), jax.ShapeDtypeStruct((nt, SUB, HD), f32)],
        compiler_params=_cparams(("parallel",)),
    )(P, P, dqn, dkn, dv, qg, kg, *rope)


def _attn_fn(q, k, v, lim):
    col = lax.broadcasted_iota(jnp.int32, (1, k.shape[0]), 1)
    s = mm_nt(q, k) + jnp.where(col < lim, 0.0, -1e30)
    m = lax.stop_gradient(jnp.max(s, -1, keepdims=True))
    e = jnp.exp(s - m)
    p = e * (1.0 / jnp.sum(e, -1, keepdims=True))
    return mm(p, v)


def _attn_fwd(qn, kn, vb, tc, tq, name):
    T = qn.shape[0]

    def body(q_ref, k_ref, v_ref, o_ref):
        lim = jnp.where(pl.program_id(1) * tq < tc, tc, T)
        o_ref[...] = _attn_fn(q_ref[...], k_ref[...], v_ref[...], lim)

    return pl.pallas_call(
        body, name=name, grid=(A_HEADS, T // tq),
        in_specs=[pl.BlockSpec((tq, HD), lambda h, i: (i, h)), pl.BlockSpec((T, HD), lambda h, i: (0, h // 2)),
                  pl.BlockSpec((T, HD), lambda h, i: (0, h // 2))],
        out_specs=pl.BlockSpec((tq, HD), lambda h, i: (i, h)),
        out_shape=jax.ShapeDtypeStruct((T, 512), f32),
        compiler_params=_cparams(("parallel", "parallel")),
    )(qn, kn, vb)


def _attn_bwd(qn, kn, vb, dya, tc, tq, name):
    T = qn.shape[0]

    def body(q_ref, k_ref, v_ref, g_ref, dq_ref, dk_ref, dv_ref):
        first = (pl.program_id(1) == 0) & (pl.program_id(2) == 0)
        lim = jnp.where(pl.program_id(2) * tq < tc, tc, T)
        _, vjp = jax.vjp(lambda q, k, v: _attn_fn(q, k, v, lim), q_ref[...].astype(f32), k_ref[...].astype(f32),
                         v_ref[...].astype(f32))
        dq, dk, dv = vjp(g_ref[...])
        dq_ref[...] = dq

        @pl.when(first)
        def _():
            dk_ref[...] = dk
            dv_ref[...] = dv

        @pl.when(jnp.logical_not(first))
        def _():
            dk_ref[...] += dk
            dv_ref[...] += dv

    qspec = pl.BlockSpec((tq, HD), lambda kv, g, i: (i, 2 * kv + g))
    kspec = pl.BlockSpec((T, HD), lambda kv, g, i: (0, kv))
    return pl.pallas_call(
        body, name=name, grid=(A_HEADS // 2, 2, T // tq),
        in_specs=[qspec, kspec, kspec, qspec], out_specs=[qspec, kspec, kspec],
        out_shape=[jax.ShapeDtypeStruct((T, 512), f32), jax.ShapeDtypeStruct((T, 256), f32),
                   jax.ShapeDtypeStruct((T, 256), f32)],
        compiler_params=_cparams(("parallel", "arbitrary", "arbitrary")),
    )(qn, kn, vb, dya)


def _conv_rows(tc, tl):
    return CONV_PAD + tc + CONV_PAD + tl + CONV_PAD


def _fill_pad(pad_ref, val, tc, tl):
    z = jnp.zeros((CONV_PAD, LANE), f32)
    pad_ref[0:CONV_PAD, :] = z
    pad_ref[CONV_PAD:CONV_PAD + tc, :] = val[0:tc]
    pad_ref[CONV_PAD + tc:2 * CONV_PAD + tc, :] = z
    pad_ref[2 * CONV_PAD + tc:2 * CONV_PAD + tc + tl, :] = val[tc:tc + tl]
    pad_ref[2 * CONV_PAD + tc + tl:3 * CONV_PAD + tc + tl, :] = z


def _conv_apply(pad_ref, w_ref, K, tc, tl, rc, emit, flip=False):
    half = K // 2
    for seg0, off, n in ((0, CONV_PAD, tc), (tc, 2 * CONV_PAD + tc, tl)):
        for r0 in range(0, n, rc):
            acc = None
            for k in range(K):
                sh = (half - k) if flip else (k - half)
                term = pad_ref[pl.ds(off + r0 + sh, rc), :] * w_ref[k:k + 1, :]
                acc = term if acc is None else acc + term
            emit(seg0 + r0, acc)


def _conv_wgrad(pad_ref, dy_ref, K, tc, tl, rc, dw_ref):
    half = K // 2
    for k in range(K):
        acc = jnp.zeros((1, LANE), f32)
        for seg0, off, n in ((0, CONV_PAD, tc), (tc, 2 * CONV_PAD + tc, tl)):
            for r0 in range(0, n, rc):
                acc = acc + jnp.sum(pad_ref[pl.ds(off + r0 + k - half, rc), :] * dy_ref[pl.ds(seg0 + r0, rc), :],
                                    axis=0, keepdims=True)
        dw_ref[k:k + 1, :] = acc


def _col(T, off):
    return pl.BlockSpec((T, LANE), lambda j: (0, off // LANE + j))


C_B, C_C, C_X, C_A, C_G = range(5)
N_SEC = 5


class _Sections:
    def __init__(self, refs):
        self.refs = refs

    def __getitem__(self, idx):
        rows, sec = idx
        return self.refs[sec][rows, :].astype(f32)

    def __setitem__(self, idx, val):
        rows, sec = idx
        self.refs[sec, rows, :] = val


def _sec_specs(T):
    return [pl.BlockSpec((T, LANE), functools.partial(lambda j, s: (0, s * (BRW // LANE) + j), s=s)) for s in range(N_SEC)]


def _conv_fwd(P, wb, wd, bd, tc, tl, rc, name):
    T = tc + tl

    def body(*refs):
        p_ref = _Sections(refs[:N_SEC])
        wb_ref, wd_ref, bd_ref, yb_ref, hh_ref, pad_ref = refs[N_SEC:]
        _fill_pad(pad_ref, p_ref[:, C_C] * p_ref[:, C_X], tc, tl)

        def emit_b(r0, y):
            yb_ref[pl.ds(r0, rc), :] = y * p_ref[pl.ds(r0, rc), C_B]

        _conv_apply(pad_ref, wb_ref, KB, tc, tl, rc, emit_b)
        _fill_pad(pad_ref, p_ref[:, C_A] * _sigmoid(p_ref[:, C_G]), tc, tl)

        def emit_d(r0, y):
            hh_ref[pl.ds(r0, rc), :] = y + bd_ref[...]

        _conv_apply(pad_ref, wd_ref, KD, tc, tl, rc, emit_d)

    return pl.pallas_call(
        body, name=name, grid=(BRW // LANE,),
        in_specs=_sec_specs(T) + [pl.BlockSpec((KB, LANE), lambda j: (0, j)), pl.BlockSpec((KD, LANE), lambda j: (0, j)),
                                  pl.BlockSpec((1, LANE), lambda j: (0, j))],
        out_specs=[_col(T, 0), _col(T, 0)],
        out_shape=[jax.ShapeDtypeStruct((T, BRW), f32), jax.ShapeDtypeStruct((T, BRW), f32)],
        scratch_shapes=[pltpu.VMEM((_conv_rows(tc, tl), LANE), f32)],
        compiler_params=_cparams(("parallel",)),
    )(*[P] * N_SEC, wb, wd, bd)


def _conv_bwd(P, dyb, dhh, wb, wd, tc, tl, rc, name):
    T = tc + tl

    def body(*refs):
        p_ref = _Sections(refs[:N_SEC])
        dyb_ref, dhh_ref, wb_ref, wd_ref, dp3_ref, dwb_ref, dwd_ref, dbd_ref, pad_ref, pad2_ref, tmp_ref = refs[N_SEC:]
        dp_ref = _Sections(dp3_ref)
        _fill_pad(pad_ref, p_ref[:, C_C] * p_ref[:, C_X], tc, tl)

        def emit_cv(r0, y):
            dp_ref[pl.ds(r0, rc), C_B] = (y * dyb_ref[pl.ds(r0, rc), :]).astype(bf16)

        _conv_apply(pad_ref, wb_ref, KB, tc, tl, rc, emit_cv)
        tmp_ref[...] = dyb_ref[...] * p_ref[:, C_B]
        _conv_wgrad(pad_ref, tmp_ref, KB, tc, tl, rc, dwb_ref)
        _fill_pad(pad2_ref, tmp_ref[...], tc, tl)

        def emit_ds(r0, y):
            dp_ref[pl.ds(r0, rc), C_C] = (y * p_ref[pl.ds(r0, rc), C_X]).astype(bf16)
            dp_ref[pl.ds(r0, rc), C_X] = (y * p_ref[pl.ds(r0, rc), C_C]).astype(bf16)

        _conv_apply(pad2_ref, wb_ref, KB, tc, tl, rc, emit_ds, flip=True)
        _fill_pad(pad_ref, p_ref[:, C_A] * _sigmoid(p_ref[:, C_G]), tc, tl)
        _conv_wgrad(pad_ref, dhh_ref, KD, tc, tl, rc, dwd_ref)
        dbd_ref[...] = jnp.sum(dhh_ref[...], axis=0, keepdims=True)
        _fill_pad(pad2_ref, dhh_ref[...], tc, tl)

        def emit_d2(r0, y):
            sg = _sigmoid(p_ref[pl.ds(r0, rc), C_G])
            a = p_ref[pl.ds(r0, rc), C_A]
            dp_ref[pl.ds(r0, rc), C_A] = (y * sg).astype(bf16)
            dp_ref[pl.ds(r0, rc), C_G] = (y * a * sg * (1.0 - sg)).astype(bf16)

        _conv_apply(pad2_ref, wd_ref, KD, tc, tl, rc, emit_d2, flip=True)

    return pl.pallas_call(
        body, name=name, grid=(BRW // LANE,),
        in_specs=_sec_specs(T) + [_col(T, 0), _col(T, 0),
                                  pl.BlockSpec((KB, LANE), lambda j: (0, j)), pl.BlockSpec((KD, LANE), lambda j: (0, j))],
        out_specs=[pl.BlockSpec((N_SEC, T, LANE), lambda j: (0, 0, j)), pl.BlockSpec((KB, LANE), lambda j: (0, j)),
                   pl.BlockSpec((KD, LANE), lambda j: (0, j)), pl.BlockSpec((1, LANE), lambda j: (0, j))],
        out_shape=[jax.ShapeDtypeStruct((N_SEC, T, BRW), bf16), jax.ShapeDtypeStruct((KB, BRW), f32),
                   jax.ShapeDtypeStruct((KD, BRW), f32), jax.ShapeDtypeStruct((1, BRW), f32)],
        scratch_shapes=[pltpu.VMEM((_conv_rows(tc, tl), LANE), f32), pltpu.VMEM((_conv_rows(tc, tl), LANE), f32),
                        pltpu.VMEM((T, LANE), f32)],
        compiler_params=_cparams(("parallel",)),
    )(*[P] * N_SEC, dyb, dhh, wb, wd)


def _gla_chunk(q, k, v, r, w2, b2, st, isfwd):
    z = mm(r, w2) + b2
    g = jax.nn.log_sigmoid(z[:, 0:C_KW] if isfwd else z[:, C_KW:2 * C_KW]) / C_TAU
    ri = lax.broadcasted_iota(jnp.int32, (CH, CH), 0)
    ci = lax.broadcasted_iota(jnp.int32, (CH, CH), 1)
    tri = ((ci <= ri) if isfwd else (ci >= ri)).astype(f32)
    cum = jnp.dot(tri, g, preferred_element_type=f32, precision=lax.Precision.HIGHEST)
    last = jnp.sum(g, axis=0, keepdims=True)
    q = q * (C_KW // C_HEADS) ** -0.5
    hv = lax.broadcasted_iota(jnp.int32, (BRW, C_KW), 0) // (BRW // C_HEADS)
    hk = lax.broadcasted_iota(jnp.int32, (BRW, C_KW), 1) // (C_KW // C_HEADS)
    st_new = st * jnp.exp(last) + jnp.where(hv == hk, mm_tn(v, k * jnp.exp(last - cum)), 0.0)
    o = mm_nt(q * jnp.exp(cum), st)
    rowi = lax.broadcasted_iota(jnp.int32, (CH, C_KW), 0)
    srow = lax.broadcasted_iota(jnp.int32, (C_HEADS * CH, C_KW), 0)
    slane = lax.broadcasted_iota(jnp.int32, (C_HEADS * CH, C_KW), 1)
    own_lanes = srow // CH == slane // (C_KW // C_HEADS)
    pos = lax.broadcasted_iota(jnp.int32, (C_HEADS * CH, CH), 0) % CH
    key = lax.broadcasted_iota(jnp.int32, (C_HEADS * CH, CH), 1)
    scores = jnp.zeros((C_HEADS * CH, CH), f32)
    for a in range(CH // GLA_SUB):
        idx = GLA_SUB * a - 1 if isfwd else GLA_SUB * (a + 1)
        ref = jnp.sum(jnp.where(rowi == idx, cum, 0.0), axis=0, keepdims=True)
        qa = q * jnp.exp(jnp.minimum(cum - ref, 0.0))
        ka = k * jnp.exp(jnp.minimum(ref - cum, GLA_CLAMP))
        s = mm_nt(jnp.where(own_lanes, jnp.concatenate([qa] * C_HEADS, axis=0), 0.0), ka)
        scores = scores + jnp.where(pos // GLA_SUB == a, s, 0.0)
    scores = jnp.where((key <= pos) if isfwd else (key >= pos), scores, 0.0)
    vw = BRW // C_HEADS
    o = o + jnp.concatenate([mm(scores[CH * hd:CH * (hd + 1)], v[:, vw * hd:vw * (hd + 1)]) for hd in range(C_HEADS)],
                            axis=1)
    return o, st_new


def _gla_chunk_of(d, n, nc, nch):
    back = jnp.where(n < nc, nc - 1 - n, nch - 1 - (n - nc))
    return jnp.where(d == 0, n, back)


def _gla_fwd(P, w2, b2, tc, name):
    T = P.shape[0]
    nch, nc = T // CH, tc // CH

    back = lambda n: _gla_chunk_of(1, n, nc, nch)

    def body(pf_ref, pb_ref, w_ref, b_ref, of_ref, ob_ref, ssf_ref, ssb_ref, stf_ref, stb_ref):
        @pl.when(pl.program_id(0) == 0)
        def _():
            stf_ref[...] = jnp.zeros_like(stf_ref)
            stb_ref[...] = jnp.zeros_like(stb_ref)

        for p_ref, o_ref, ss_ref, st_ref, isfwd in ((pf_ref, of_ref, ssf_ref, stf_ref, True),
                                                    (pb_ref, ob_ref, ssb_ref, stb_ref, False)):
            st = st_ref[...]
            ss_ref[...] = st
            p = p_ref[...].astype(f32)
            o, st_new = _gla_chunk(p[:, 0:G_K], p[:, G_K:G_V], p[:, G_V:G_R], p[:, G_R:W_G], w_ref[...], b_ref[...], st, isfwd)
            o_ref[...] = o
            st_ref[...] = st_new

    sd = jax.ShapeDtypeStruct
    return pl.pallas_call(
        body, name=name, grid=(nch,),
        in_specs=[pl.BlockSpec((CH, W_G), lambda n: (n, 0)), pl.BlockSpec((CH, W_G), lambda n: (back(n), 0)),
                  pl.BlockSpec((LANE, 512), lambda n: (0, 0)), pl.BlockSpec((1, 512), lambda n: (0, 0))],
        out_specs=[pl.BlockSpec((CH, BRW), lambda n: (n, 0)), pl.BlockSpec((CH, BRW), lambda n: (back(n), 0)),
                   pl.BlockSpec((None, BRW, C_KW), lambda n: (n, 0, 0)), pl.BlockSpec((None, BRW, C_KW), lambda n: (n, 0, 0))],
        out_shape=[sd((T, BRW), f32), sd((T, BRW), f32), sd((nch, BRW, C_KW), f32), sd((nch, BRW, C_KW), f32)],
        scratch_shapes=[pltpu.VMEM((BRW, C_KW), f32), pltpu.VMEM((BRW, C_KW), f32)],
        compiler_params=_cparams(("arbitrary",)),
    )(P, P, w2, b2)


def _gla_bwd(P, w2, b2, ssave, doc, tc, name):
    T = P.shape[0]
    nch, nc = T // CH, tc // CH

    fwd_chunk = lambda m: nch - 1 - m
    back_chunk = lambda m: _gla_chunk_of(1, nch - 1 - m, nc, nch)

    def body(pf_ref, pb_ref, w_ref, b_ref, ssf_ref, ssb_ref, gf_ref, gb_ref, dpf_ref, dpb_ref, dw_ref, db_ref,
             dstf_ref, dstb_ref):
        m = pl.program_id(0)

        @pl.when(m == 0)
        def _():
            dstf_ref[...] = jnp.zeros_like(dstf_ref)
            dstb_ref[...] = jnp.zeros_like(dstb_ref)

        dw_sum, db_sum = None, None
        for p_ref, ss_ref, g_ref, dp_ref, dst_ref, isfwd in ((pf_ref, ssf_ref, gf_ref, dpf_ref, dstf_ref, True),
                                                             (pb_ref, ssb_ref, gb_ref, dpb_ref, dstb_ref, False)):
            p = p_ref[...].astype(f32)
            _, vjp = jax.vjp(lambda q, k, v, r, w, b, st: _gla_chunk(q, k, v, r, w, b, st, isfwd),
                             p[:, 0:G_K], p[:, G_K:G_V], p[:, G_V:G_R], p[:, G_R:W_G], w_ref[...], b_ref[...], ss_ref[...])
            dq, dk, dv, dr, dw, db, dst = vjp((g_ref[...], dst_ref[...]))
            dp_ref[:, 0:G_K] = dq
            dp_ref[:, G_K:G_V] = dk
            dp_ref[:, G_V:G_R] = dv
            dp_ref[:, G_R:W_G] = dr
            dst_ref[...] = dst
            dw_sum = dw if dw_sum is None else dw_sum + dw
            db_sum = db if db_sum is None else db_sum + db

        @pl.when(m == 0)
        def _():
            dw_ref[...] = dw_sum
            _partial_rows(db_ref, [db_sum])

        @pl.when(m > 0)
        def _():
            dw_ref[...] += dw_sum
            db_ref[0:1, :] += db_sum

    ssf, ssb = ssave
    chunk_f = lambda w: pl.BlockSpec((CH, w), lambda m: (fwd_chunk(m), 0))
    chunk_b = lambda w: pl.BlockSpec((CH, w), lambda m: (back_chunk(m), 0))
    state = pl.BlockSpec((None, BRW, C_KW), lambda m: (nch - 1 - m, 0, 0))
    sd = jax.ShapeDtypeStruct
    return pl.pallas_call(
        body, name=name, grid=(nch,),
        in_specs=[chunk_f(W_G), chunk_b(W_G), pl.BlockSpec((LANE, 512), lambda m: (0, 0)), pl.BlockSpec((1, 512), lambda m: (0, 0)),
                  state, state, chunk_f(BRW), chunk_b(BRW)],
        out_specs=[chunk_f(W_G), chunk_b(W_G), pl.BlockSpec((LANE, 512), lambda m: (0, 0)), pl.BlockSpec((SUB, 512), lambda m: (0, 0))],
        out_shape=[sd((T, W_G), f32), sd((T, W_G), f32), sd((LANE, 512), f32), sd((SUB, 512), f32)],
        scratch_shapes=[pltpu.VMEM((BRW, C_KW), f32), pltpu.VMEM((BRW, C_KW), f32)],
        compiler_params=_cparams(("arbitrary",)),
    )(P, P, w2, b2, ssf, ssb, doc, doc)


def _sum_dirs(a, b, tm, name):
    T, W = a.shape

    def body(a_ref, b_ref, o_ref):
        o_ref[...] = (a_ref[...] + b_ref[...]).astype(bf16)

    spec = pl.BlockSpec((tm, W), lambda i: (i, 0))
    return pl.pallas_call(
        body, name=name, grid=(T // tm,), in_specs=[spec, spec], out_specs=spec,
        out_shape=jax.ShapeDtypeStruct((T, W), bf16),
        compiler_params=_cparams(("parallel",)),
    )(a, b)


def _merge_fn(h, m_l, m_c, isctx, ya, ga, yb, gb, of, ob, gc, hh, gd, mg, es, ey, cn, dng, dnb, lg, lb, wbr, wout):
    oc = of + ob
    yc = jnp.concatenate([_rms(oc[:, HD * i:HD * (i + 1)], cn[:, HD * i:HD * (i + 1)]) for i in range(C_HEADS)], 1)
    brs = [ya * _silu(ga), yb * _silu(gb), yc * _silu(gc), _silu(_ln(hh) * dng + dnb) * _silu(gd)]
    acc = None
    for i in range(4):
        t = _sigmoid(mg[:, D * i:D * (i + 1)]) * (mm(brs[i], wbr[i]) + es[i])
        acc = t if acc is None else acc + t
    y = mm(acc, wout) + ey
    gate = jnp.where(isctx, m_c[:, 2 * D:3 * D], m_l[:, 2 * D:3 * D])
    hn = _ln(ALPHA * h + gate * y) * lg + lb
    return hn, (brs, acc)


def _merge_specs(tm):
    t = lambda w, off=0: _tok(tm, w, off)
    return [t(D), pl.BlockSpec((SUB, 3 * D), lambda i: (0, 0)),
            t(BRW), t(BRW, M_GA), t(BRW), t(BRW, M_GB),
            t(BRW), t(BRW),
            t(BRW, M_GC), t(BRW), t(BRW, M_GD), t(4 * D, 0),
            _vec(BRW), _vec(BRW), _vec(BRW), _vec(D), _vec(D),
            pl.BlockSpec((4, BRW, D), lambda i: (0, 0, 0)), pl.BlockSpec((D, D), lambda i: (0, 0))]


def _merge_fwd(h, modv_l, ya, yb, o2, hh, P, cn, dng, dnb, lg, lb, wbr, wout, tc, tm, name):
    T = h.shape[0]

    def body(h_ref, m_ref, ya_ref, ga_ref, yb_ref, gb_ref, of_ref, ob_ref, gc_ref, hh_ref, gd_ref, mg_ref,
             cn_ref, dng_ref, dnb_ref, lg_ref, lb_ref, wbr_ref, wout_ref, o_ref):
        isctx = _row_ids(pl.program_id(0), tm) < tc
        zero = jnp.zeros((tm, D), f32)
        up = lambda r: r[...].astype(f32)
        hn, _ = _merge_fn(h_ref[...], m_ref[0:1, :], m_ref[1:2, :], isctx, ya_ref[...], up(ga_ref), yb_ref[...],
                          up(gb_ref), of_ref[...], ob_ref[...], up(gc_ref), hh_ref[...], up(gd_ref), up(mg_ref),
                          [zero] * 4, zero, cn_ref[...], dng_ref[...], dnb_ref[...], lg_ref[...], lb_ref[...],
                          [wbr_ref[i] for i in range(4)], wout_ref[...])
        o_ref[...] = hn

    return pl.pallas_call(
        body, name=name, grid=(T // tm,),
        in_specs=_merge_specs(tm), out_specs=_tok(tm, D, 0),
        out_shape=jax.ShapeDtypeStruct((T, D), f32),
        compiler_params=_cparams(("parallel",)),
    )(h, modv_l, ya, P, yb, P, o2[0], o2[1], P, hh, P, P, cn, dng, dnb, lg, lb, wbr, wout)


def _merge_bwd(dhn, h, modv_l, ya, yb, o2, hh, P, cn, dng, dnb, lg, lb, wbr, wout, tc, tm, name):
    T = h.shape[0]
    nt = T // tm

    def body(g_ref, h_ref, m_ref, ya_ref, ga_ref, yb_ref, gb_ref, of_ref, ob_ref, gc_ref, hh_ref, gd_ref, mg_ref,
             cn_ref, dng_ref, dnb_ref, lg_ref, lb_ref, wbr_ref, wout_ref,
             dh_ref, dm_ref, dya_ref, dyb_ref, doc_ref, dhh_ref, dp_ref,
             br_ref, z_ref, acc_ref, dy_ref, dv5_ref, dvd_ref):
        isctx = _row_ids(pl.program_id(0), tm) < tc
        zero = jnp.zeros((tm, D), f32)
        wbr_v = [wbr_ref[i] for i in range(4)]
        wout_v = wout_ref[...]
        up = lambda r: r[...].astype(f32)

        def fn(h, ml, mc, ya, ga, yb, gb, oc, gc, hh, gd, mg, e0, e1, e2, e3, ey, cn, dng, dnb, lg, lb):
            return _merge_fn(h, ml, mc, isctx, ya, ga, yb, gb, oc, jnp.zeros_like(oc), gc, hh, gd, mg,
                             [e0, e1, e2, e3], ey, cn, dng, dnb, lg, lb, wbr_v, wout_v)

        _, vjp, (brs, acc) = jax.vjp(
            fn, h_ref[...], m_ref[0:1, :], m_ref[1:2, :], ya_ref[...], up(ga_ref), yb_ref[...], up(gb_ref),
            of_ref[...] + ob_ref[...], up(gc_ref), hh_ref[...], up(gd_ref), up(mg_ref), zero, zero, zero, zero, zero,
            cn_ref[...], dng_ref[...], dnb_ref[...], lg_ref[...], lb_ref[...], has_aux=True)
        (dh, dml, dmc, dya, dga, dyb, dgb, doc, dgc, dhh, dgd, dmg, z0, z1, z2, z3, dy,
         dcn, ddng, ddnb, dlg, dlb) = vjp(g_ref[...])
        dh_ref[...] = dh
        _partial_rows(dm_ref, [dml, dmc])
        dya_ref[...] = dya
        dyb_ref[...] = dyb
        doc_ref[...] = doc
        dhh_ref[...] = dhh
        dp_ref[:, 0:M_GA] = dmg.astype(bf16)
        dp_ref[:, M_GA:M_GB] = dga.astype(bf16)
        dp_ref[:, M_GB:M_GC] = dgb.astype(bf16)
        dp_ref[:, M_GC:M_GD] = dgc.astype(bf16)
        dp_ref[:, M_GD:W_M] = dgd.astype(bf16)
        for i, z in enumerate((z0, z1, z2, z3)):
            br_ref[i] = brs[i].astype(bf16)
            z_ref[i] = z.astype(bf16)
        acc_ref[...] = acc.astype(bf16)
        dy_ref[...] = dy.astype(bf16)
        _partial_rows(dv5_ref, [dcn, ddng, ddnb])
        _partial_rows(dvd_ref, [dlg, dlb])

    t = lambda w: _tok(tm, w, 0)
    part = lambda w: pl.BlockSpec((None, SUB, w), lambda i: (i, 0, 0))
    sd = jax.ShapeDtypeStruct
    return pl.pallas_call(
        body, name=name, grid=(nt,),
        in_specs=[t(D)] + _merge_specs(tm),
        out_specs=[t(D), part(3 * D)] + [t(BRW)] * 4 + [t(W_M),
                   pl.BlockSpec((4, tm, BRW), lambda i: (0, i, 0)), pl.BlockSpec((4, tm, D), lambda i: (0, i, 0)),
                   t(D), t(D), part(BRW), part(D)],
        out_shape=[sd((T, D), f32), sd((nt, SUB, 3 * D), f32)] + [sd((T, BRW), f32)] * 4 + [sd((T, W_M), bf16),
                   sd((4, T, BRW), bf16), sd((4, T, D), bf16), sd((T, D), bf16), sd((T, D), bf16),
                   sd((nt, SUB, BRW), f32), sd((nt, SUB, D), f32)],
        compiler_params=_cparams(("parallel",)),
    )(dhn, h, modv_l, ya, P, yb, P, o2[0], o2[1], P, hh, P, P, cn, dng, dnb, lg, lb, wbr, wout)


def _loss_kernel(h, tgt, tc, tm, name):
    T = h.shape[0]
    nt = T // tm
    nct = tc // tm

    def body(h_ref, t_ref, d_ref, l_ref):
        i = pl.program_id(0)
        err = h_ref[...] - t_ref[...]
        lat = (i >= nct).astype(f32)
        d_ref[...] = err * (lat / D)
        l_ref[...] = jnp.zeros((SUB, LANE), f32) + lat * 0.5 * jnp.sum(err * err) / D

    return pl.pallas_call(
        body, name=name, grid=(nt,),
        in_specs=[pl.BlockSpec((tm, D), lambda i: (i, 0)),
                  pl.BlockSpec((tm, D), lambda i: (jnp.maximum(i - nct, 0), 0))],
        out_specs=[pl.BlockSpec((tm, D), lambda i: (i, 0)), pl.BlockSpec((None, SUB, LANE), lambda i: (i, 0, 0))],
        out_shape=[jax.ShapeDtypeStruct((T, D), f32), jax.ShapeDtypeStruct((nt, SUB, LANE), f32)],
        compiler_params=_cparams(("parallel",)),
    )(h, tgt)


def _rope_tables(tc, tl):
    t = jnp.arange(tl)
    inv = ROPE_THETA ** (-jnp.arange(0, HD // 2, 2, dtype=f32) / (HD // 2))
    ang = jnp.concatenate([(t // GRID_W).astype(f32)[:, None] * inv, (t % GRID_W).astype(f32)[:, None] * inv], -1)
    cos, sin = jnp.repeat(jnp.cos(ang), 2, axis=1), jnp.repeat(jnp.sin(ang), 2, axis=1)
    even = (jnp.arange(HD) % 2 == 0)[None, :]
    cos_f = jnp.concatenate([jnp.ones((tc, HD), f32), cos], 0)
    sin_a = jnp.concatenate([jnp.zeros((tc, HD), f32), jnp.where(even, -sin, 0.0)], 0)
    sin_b = jnp.concatenate([jnp.zeros((tc, HD), f32), jnp.where(even, 0.0, sin)], 0)
    return cos_f, sin_a, sin_b


N_CHIPS = 4
SHARD = N_IN // N_CHIPS


def _group_ranges():
    return dict(M=[(S_MG, 4 * D), (S_GA, BRW), (S_GB, BRW), (S_GC, BRW), (S_GD, BRW)], A=[(S_Q, W_A)],
                C=[(S_B, 3 * BRW), (S_DA, 2 * BRW)], G=[(S_CQ, 2 * C_KW + BRW), (S_R, 2 * C_RANK)])


def _group_weights(w4):
    out = {}
    for k, ranges in _group_ranges().items():
        parts = []
        for a, n in ranges:
            n = LANE if (k, a) == ("G", S_R) else n
            while n > 0:
                s, r = divmod(a, SHARD)
                m = min(n, SHARD - r)
                parts.append(w4[s, r:r + m])
                a, n = a + m, n - m
        out[k] = jnp.concatenate(parts, 0)
    return out


def _ungroup(g):
    secs = []
    for k, ranges in _group_ranges().items():
        off = 0
        for a, n in ranges:
            secs.append((a, g[k][off:off + n]))
            off += n
    return jnp.concatenate([v for _, v in sorted(secs, key=lambda t: t[0])], 0)


PROJ_TN = dict(M=2048, A=1024, C=1280, G=1152)
DU_TK = dict(M=2048, A=1024, C=BRW, G=1152)
DWP_TN = dict(M=768, A=1024, C=BRW, G=1152)


def _gate_weights(w2_l, gb_l):
    w = jnp.zeros((LANE, 2 * C_KW), f32)
    w = w.at[0:C_RANK, 0:C_KW].set(w2_l[0]).at[C_RANK:2 * C_RANK, C_KW:2 * C_KW].set(w2_l[1])
    return w, jnp.concatenate([gb_l[0], gb_l[1]])[None, :]


def _local_step(x1, c1, ctx1, tgt1, c_ctx, b_mod, weights_of, q_norm, k_norm, b_conv, w2, gb, c_norm, d_conv_w,
                d_conv_b, d_norm_g, d_norm_b, grads_done, ln_g, ln_b, tm, token=None):
    tc, tl = ctx1.shape[0], x1.shape[0]
    T = tc + tl
    rc = min(256, tc)
    tmb = tm // 2
    tmm = 768 if T % 768 == 0 else tm
    rope = _rope_tables(tc, tl)
    cin = jnp.concatenate([c1, c_ctx[None, :], jnp.zeros((SUB - 2, D), f32)], 0)
    if token is not None:
        cin = cin + token[:, 0:1]
    row = lambda v: v[None, :]

    h = jnp.concatenate([ctx1, x1], 0)
    saved, wp, w_br, w_out, w_mod, modv = [], *([None] * DEPTH for _ in range(5))
    for l in range(DEPTH):
        wp[l], merge_weights, w_mod[l] = weights_of(l, h)
        modv[l] = _mod_fwd(cin, w_mod[l], b_mod[l], f"mod_fwd{l}")
        u = _ln_fwd(h, modv[l], tc, tm, f"ln_fwd{l}")
        P = {k: _matmul(u, wp[l][k], "nt", tmm, PROJ_TN[k], D, f"proj{l}{k}", out_dtype=bf16) for k in GROUPS}
        qn, kn, vb = _prep_fwd(P["A"], row(q_norm[l]), row(k_norm[l]), rope, tm, f"prep_fwd{l}")
        ya = _attn_fwd(qn, kn, vb, tc, tm, f"attn_fwd{l}")
        yb, hh = _conv_fwd(P["C"], b_conv[l], d_conv_w[l], row(d_conv_b[l]), tc, tl, rc, f"conv_fwd{l}")
        w2p, b2p = _gate_weights(w2[l], gb[l])
        gla = _gla_fwd(P["G"], w2p, b2p, tc, f"gla_fwd{l}")
        o2, ssave = gla[:2], gla[2:]
        w_br[l], w_out[l] = merge_weights(o2[0])
        hn = _merge_fwd(h, modv[l], ya, yb, o2, hh, P["M"], row(c_norm[l]), row(d_norm_g[l]), row(d_norm_b[l]),
                        row(ln_g[l]), row(ln_b[l]), w_br[l], w_out[l], tc, tm, f"merge_fwd{l}")
        saved.append((h, u, P, qn, kn, vb, ya, yb, hh, o2, ssave, w2p, b2p))
        h = hn

    dh, lparts = _loss_kernel(h, tgt1, tc, tm, "loss")
    loss = jnp.sum(lparts[:, 0, 0])

    g = {k: [None] * DEPTH for k in ("wp", "q_norm", "k_norm", "b_conv", "w2", "gb", "c_norm", "d_conv_w", "d_conv_b",
                                     "d_norm_g", "d_norm_b", "w_br", "w_out", "ln_g", "ln_b", "modv")}
    for l in reversed(range(DEPTH)):
        h_in, u, P, qn, kn, vb, ya, yb, hh, o2, ssave, w2p, b2p = saved[l]
        dP = {}
        (dh_res, dm_mg, dya, dyb, doc, dhh, dP["M"], br, z, acc, dy, dv5, dvd) = _merge_bwd(
            dh, h_in, modv[l], ya, yb, o2, hh, P["M"], row(c_norm[l]), row(d_norm_g[l]), row(d_norm_b[l]),
            row(ln_g[l]), row(ln_b[l]), w_br[l], w_out[l], tc, tmb, f"merge_bwd{l}")
        g["w_br"][l] = _matmul_tn_batched(br, z, N_CHIPS, f"dwbr{l}")
        g["w_out"][l] = _matmul(acc, dy, "tn", D, D, T, f"dwout{l}", out_dtype=bf16)
        tk = grads_done(l, {k: g[k][l] for k in ("w_br", "w_out")})
        qg_l = row(q_norm[l]) if tk is None else row(q_norm[l]) + tk[0:1, :]
        v5 = jnp.sum(dv5, 0)
        g["c_norm"][l], g["d_norm_g"][l], g["d_norm_b"][l] = v5[0], v5[1], v5[2]
        vd = jnp.sum(dvd, 0)
        g["ln_g"][l], g["ln_b"][l] = vd[0], vd[1]
        dqn, dkn, dv = _attn_bwd(qn, kn, vb, dya, tc, tm, f"attn_bwd{l}")
        dP["A"], dqk = _prep_bwd(P["A"], dqn, dkn, dv, qg_l, row(k_norm[l]), rope, tm, f"prep_bwd{l}")
        dqk = jnp.sum(dqk, 0)
        g["q_norm"][l], g["k_norm"][l] = dqk[0], dqk[1]
        dP["C"], dwb, dwd, dbd = _conv_bwd(P["C"], dyb, dhh, b_conv[l], d_conv_w[l], tc, tl, rc, f"conv_bwd{l}")
        g["b_conv"][l], g["d_conv_w"][l], g["d_conv_b"][l] = dwb, dwd, dbd[0]
        dpf, dpb, dw2p, db2p = _gla_bwd(P["G"], w2p, b2p, ssave, doc, tc, f"gla_bwd{l}")
        dP["G"] = _sum_dirs(dpf, dpb, tm, f"gla_sum{l}")
        db2p = db2p[0]
        g["w2"][l] = jnp.stack([dw2p[0:C_RANK, 0:C_KW], dw2p[C_RANK:2 * C_RANK, C_KW:2 * C_KW]])
        g["gb"][l] = jnp.stack([db2p[0:C_KW], db2p[C_KW:2 * C_KW]])
        g["wp"][l] = {k: _matmul(dP[k], u, "tn", DWP_TN[k], D, T, f"dwp{l}{k}", out_dtype=bf16) for k in GROUPS}
        tk = grads_done(l, {"wp": g["wp"][l]})
        du = _matmul_groups(dP, wp[l], DU_TK, tmm, f"du{l}", after=tk)
        dh, dm_ln = _ln_bwd(du, h_in, dh_res, modv[l], tc, tm, f"ln_bwd{l}", latent_only=(l == 0))
        g["modv"][l] = jnp.sum(dm_mg, 0) + jnp.sum(dm_ln, 0)

    dmodv = jnp.stack(g.pop("modv"))
    g["w_mod"], dcin = _mod_bwd(cin, w_mod, dmodv)
    g["b_mod"] = dmodv[:, 0, :] + dmodv[:, 1, :]
    g["c_ctx"] = jnp.sum(dcin, (0, 1))[1]
    return loss, dh, g


HALF_TL = 256


TILE_BYTES = 1 << 20


def _row_tile(rows, cols, itemsize=4):
    tr = min(rows, 128)
    while rows % (2 * tr) == 0 and 2 * tr * cols * itemsize <= TILE_BYTES:
        tr *= 2
    return tr


def _adamw(w, g, m, v, name, tr=None, after=None):
    L, R, C = w.shape
    tr = _row_tile(R, C) if tr is None else tr
    if R % tr == 0:
        grid, spec = (L, R // tr), pl.BlockSpec((None, tr, C), lambda l, i: (l, i, 0))
    elif R * C * 4 <= (1 << 20):
        grid, spec = (L, 1), pl.BlockSpec((None, R, C), lambda l, i: (l, 0, 0))
    else:
        grid, spec = (L, C // HALF_TL), pl.BlockSpec((None, R, HALF_TL), lambda l, i: (l, 0, i))

    def body(w_ref, g_ref, m_ref, v_ref, *rest):
        go_ref, d_ref, nm_ref, nv_ref = rest[-4:]
        gg = g_ref[...]
        go_ref[...] = gg
        nm = B1 * m_ref[...] + (1.0 - B1) * gg
        nv = B2 * v_ref[...] + (1.0 - B2) * (gg * gg)
        m_hat = nm / (1.0 - B1 ** STEP)
        v_hat = nv / (1.0 - B2 ** STEP)
        d_ref[...] = -LR * (m_hat / (jnp.sqrt(v_hat) + AEPS) + WD * w_ref[...])
        nm_ref[...] = nm
        nv_ref[...] = nv

    return pl.pallas_call(
        body, name=name, grid=grid, in_specs=[spec] * 4 + ([] if after is None else [pl.BlockSpec(memory_space=pl.ANY)]),
        out_specs=[spec] * 4, out_shape=[jax.ShapeDtypeStruct((L, R, C), f32)] * 4,
        compiler_params=_cparams(("parallel", "parallel")),
    )(w, g, m, v, *([] if after is None else [after]))


MESH = pl.DeviceIdType.MESH
ANY = pl.BlockSpec(memory_space=pl.ANY)


def _place():
    x, y, c = lax.axis_index("x"), lax.axis_index("y"), lax.axis_index("c")
    chips = [(1 - x, y), (x, 1 - y), (1 - x, 1 - y)]
    return x, y, c, chips


def _half(ref, c, axis):
    n = ref.shape[axis] // 2
    last = axis in (-1, ref.ndim - 1)
    idx = [slice(None)] * ref.ndim
    idx[axis] = pl.ds(pl.multiple_of(c * n, LANE if last else SUB), n)
    return ref.at[tuple(idx)]


def _half_shape(shape, axis):
    s = list(shape)
    s[axis] //= 2
    return tuple(s)


def _all_gather(arrs, axes, name):
    n = len(arrs)

    def body(*refs):
        ins, outs = refs[:n], refs[n:2 * n]
        send, recv = refs[2 * n:]
        x, y, c, chips = _place()
        me, sib = 2 * x + y, (x, y, 1 - c)

        def copy(a, k, chip_idx, cc, to, src=None):
            blk = _half(outs[a].at[chip_idx], cc, axes[a])
            return pltpu.make_async_remote_copy(src_ref=blk if src is None else src, dst_ref=blk,
                                                send_sem=send.at[7 * a + k], recv_sem=recv.at[7 * a + k],
                                                device_id=to, device_id_type=MESH)

        own = [pltpu.make_async_remote_copy(src_ref=ins[a], dst_ref=outs[a].at[me], send_sem=send.at[7 * a + 6],
                                            recv_sem=recv.at[7 * a + 6], device_id=sib, device_id_type=MESH)
               for a in range(n)]
        first = own + [copy(a, j, me, c, (*chip, c), src=_half(ins[a], c, axes[a]))
                       for a in range(n) for j, chip in enumerate(chips)]
        for cp in first:
            cp.start()
        passed = []
        for a in range(n):
            for j, chip in enumerate(chips):
                k = 2 * chip[0] + chip[1]
                copy(a, j, k, c, sib).wait_recv()
                fwd = copy(a, 3 + j, k, c, sib)
                fwd.start()
                passed.append(fwd)
        for a in range(n):
            own[a].wait_recv()
            for j, chip in enumerate(chips):
                copy(a, 3 + j, 2 * chip[0] + chip[1], 1 - c, sib).wait_recv()
        for cp in first + passed:
            cp.wait_send()

    return pl.pallas_call(
        body, name=name, in_specs=[ANY] * n, out_specs=[ANY] * n,
        out_shape=[jax.ShapeDtypeStruct((N_CHIPS,) + a.shape, a.dtype) for a in arrs],
        scratch_shapes=[pltpu.SemaphoreType.DMA((7 * n,)), pltpu.SemaphoreType.DMA((7 * n,))],
    )(*arrs)


def _add_half(gfull, land, cidx, axis, name, tr=None, out_dtype=bf16):
    _, hr, hc = land.shape
    if axis == 0:
        tr = min(tr, hr) if tr else _row_tile(hr, hc)
        nb, blk = hr // tr, (None, tr, hc)
        g_spec = pl.BlockSpec(blk, lambda s, i, cr: (s, cr[0] * nb + i, 0))
        l_spec = pl.BlockSpec(blk, lambda s, i, cr: (s, i, 0))
    else:
        nb, blk = hc // HALF_TL, (None, hr, HALF_TL)
        g_spec = pl.BlockSpec(blk, lambda s, i, cr: (s, 0, cr[0] * nb + i))
        l_spec = pl.BlockSpec(blk, lambda s, i, cr: (s, 0, i))

    def body(c_ref, g_ref, l_ref, o_ref):
        o_ref[...] = (g_ref[...].astype(f32) + l_ref[...].astype(f32)).astype(o_ref.dtype)

    return pl.pallas_call(
        body, name=name,
        grid_spec=pltpu.PrefetchScalarGridSpec(
            num_scalar_prefetch=1, grid=(N_CHIPS, nb), in_specs=[g_spec, l_spec], out_specs=l_spec),
        out_shape=jax.ShapeDtypeStruct((N_CHIPS, hr, hc), out_dtype),
        compiler_params=_cparams(("parallel", "parallel")),
    )(cidx, gfull, land)


def _sum_chips(land, own, place, axis, layer, into, name, tr=None):
    _, hr, hc = land.shape
    fresh = not hasattr(into, "dtype")
    shape = tuple(into) if fresh else into.shape
    if axis == 0:
        tr = min(tr, hr) if tr else _row_tile(hr, 4 * hc, 2)
        nb, blk = hr // tr, (tr, hc)
        l_map, m_map = (lambda i, p: (0, i, 0)), (lambda i, p: (p[0], i, 0))
        o_map = lambda i, p: (layer, p[1] * nb + i, 0)
    else:
        nb, blk = hc // HALF_TL, (hr, HALF_TL)
        l_map, m_map = (lambda i, p: (0, 0, i)), (lambda i, p: (p[0], 0, i))
        o_map = lambda i, p: (layer, 0, p[1] * nb + i)

    def body(p_ref, l_ref, o_ref, *rest):
        me = p_ref[0]
        mine = o_ref[...].astype(f32)
        acc = None
        for k in range(N_CHIPS):
            t = jnp.where(me == k, mine, l_ref[k].astype(f32))
            acc = t if acc is None else acc + t
        rest[-1][...] = acc

    return pl.pallas_call(
        body, name=name,
        grid_spec=pltpu.PrefetchScalarGridSpec(
            num_scalar_prefetch=1, grid=(nb,),
            in_specs=[pl.BlockSpec((N_CHIPS,) + blk, l_map), pl.BlockSpec((None,) + blk, m_map)] + ([] if fresh else [ANY]),
            out_specs=pl.BlockSpec((None,) + blk, o_map)),
        out_shape=jax.ShapeDtypeStruct(shape, f32),
        input_output_aliases={} if fresh else {3: 0},
        compiler_params=_cparams(("parallel",)),
    )(place, land, own, *([] if fresh else [into]))


def _sibling_fill(arrs, axes, name):
    n = len(arrs)

    def body(*refs):
        outs = refs[n:2 * n]
        send, recv = refs[2 * n:]
        x, y, c, _ = _place()
        cps = [pltpu.make_async_remote_copy(src_ref=_half(outs[a], c, axes[a] + 1), dst_ref=_half(outs[a], c, axes[a] + 1),
                                            send_sem=send.at[a], recv_sem=recv.at[a], device_id=(x, y, 1 - c),
                                            device_id_type=MESH) for a in range(n)]
        for cp in cps:
            cp.start()
        for a in range(n):
            blk = _half(outs[a], 1 - c, axes[a] + 1)
            pltpu.make_async_remote_copy(src_ref=blk, dst_ref=blk, send_sem=send.at[a], recv_sem=recv.at[a],
                                         device_id=(x, y, 1 - c), device_id_type=MESH).wait_recv()
        for cp in cps:
            cp.wait_send()

    return pl.pallas_call(
        body, name=name, in_specs=[ANY] * n, out_specs=[ANY] * n,
        out_shape=[jax.ShapeDtypeStruct(a.shape, a.dtype) for a in arrs],
        input_output_aliases={a: a for a in range(n)},
        scratch_shapes=[pltpu.SemaphoreType.DMA((n,)), pltpu.SemaphoreType.DMA((n,))],
    )(*arrs)


HBM = pl.BlockSpec(memory_space=pltpu.HBM)
SEM = pl.BlockSpec(memory_space=pltpu.SEMAPHORE)
EFFECT = pltpu.SideEffectType.DATAFLOW_SIDE_EFFECTING
PEERS = 4


def _split_copies(srcs, lands, send, recv, gather, axes=None):
    x, y, c, chips = _place()
    me = 2 * x + y
    if axes is not None:
        out = []
        for a in range(len(srcs)):
            sems = dict(send_sem=send.at[PEERS * a], recv_sem=recv.at[PEERS * a], device_id=(x, y, 1 - c), device_id_type=MESH)
            copy = pltpu.make_async_remote_copy(src_ref=_half(srcs[a], 1 - c, axes[a] + 1), dst_ref=lands[a], **sems)
            out.append((copy, copy))
        return out
    peers = [((*chip, c), 2 * chip[0] + chip[1]) for chip in chips] + ([((x, y, 1 - c), me)] if gather else [])
    out = []
    for a in range(len(srcs)):
        for j, (dev, k) in enumerate(peers):
            src = srcs[a] if gather else srcs[a].at[k]
            sems = dict(send_sem=send.at[PEERS * a + j], recv_sem=recv.at[PEERS * a + j], device_id=dev, device_id_type=MESH)
            out.append((pltpu.make_async_remote_copy(src_ref=src, dst_ref=lands[a].at[me], **sems),
                        pltpu.make_async_remote_copy(src_ref=src, dst_ref=lands[a].at[k], **sems)))
    return out


def _split_start(srcs, gather, after, name, axes=None):
    n = len(srcs)
    if axes is not None:
        lands = [lax.empty(_half_shape(s.shape, axes[a] + 1), s.dtype) for a, s in enumerate(srcs)]
    else:
        lands = [lax.empty(((N_CHIPS,) + s.shape) if gather else s.shape, s.dtype) for s in srcs]

    def body(*refs):
        send, recv = refs[2 * n + 1], refs[2 * n + 2]
        for start, _ in _split_copies(refs[:n], refs[n:2 * n], send, recv, gather, axes):
            start.start()
        refs[-1][...] = jnp.zeros_like(refs[-1])

    sems = pltpu.SemaphoreType.DMA((PEERS * n,))
    hbm = lambda a: pltpu.with_memory_space_constraint(a, pltpu.HBM)
    out = pl.pallas_call(
        body, name=name,
        out_shape=(sems, sems, *[pltpu.HBM(a.shape, a.dtype) for a in srcs + lands], jax.ShapeDtypeStruct((SUB, LANE), f32)),
        in_specs=[HBM] * (2 * n) + [ANY], out_specs=(SEM, SEM, *[HBM] * (2 * n), pl.BlockSpec(memory_space=pltpu.VMEM)),
        input_output_aliases={i: 2 + i for i in range(2 * n)},
        compiler_params=pltpu.CompilerParams(has_side_effects=EFFECT),
    )(*[hbm(a) for a in srcs + lands], after)
    return out[0], out[1], list(out[2:2 + n]), list(out[2 + n:2 + 2 * n]), out[-1]


def _split_wait(send, recv, srcs, lands, gather, after, name, axes=None):
    n = len(srcs)

    def body(*refs):
        for start, arrival in _split_copies(refs[:n], refs[n:2 * n], refs[2 * n], refs[2 * n + 1], gather, axes):
            start.wait_send()
            arrival.wait_recv()

    out = pl.pallas_call(
        body, name=name, out_shape=[pltpu.HBM(a.shape, a.dtype) for a in srcs + lands],
        in_specs=[HBM] * (2 * n) + [SEM, SEM, ANY], out_specs=[HBM] * (2 * n),
        input_output_aliases={i: i for i in range(2 * n)},
        compiler_params=pltpu.CompilerParams(has_side_effects=EFFECT),
    )(*srcs, *lands, send, recv, after)
    return list(out[:n]), list(out[n:])


N_DEV = 8


def _all_reduce_small(v, name, after):
    R = v.shape[0]

    def body(v_ref, after_ref, o_ref, land_ref, send, recv):
        x, y, c, _ = _place()
        me = 4 * x + 2 * y + c
        land_ref[me] = v_ref[...]
        cps = []
        for m in range(1, N_DEV):
            px, py, pc = [(1 - q) if (m >> s) & 1 else q for q, s in ((x, 2), (y, 1), (c, 0))]
            cps.append((pltpu.make_async_remote_copy(src_ref=v_ref, dst_ref=land_ref.at[me], send_sem=send.at[m - 1],
                                                     recv_sem=recv.at[m - 1], device_id=(px, py, pc), device_id_type=MESH),
                        4 * px + 2 * py + pc, m))
        for cp, *_ in cps:
            cp.start()
        for cp, peer, m in cps:
            pltpu.make_async_remote_copy(src_ref=v_ref, dst_ref=land_ref.at[peer], send_sem=send.at[m - 1],
                                         recv_sem=recv.at[m - 1], device_id=(x, y, c), device_id_type=MESH).wait_recv()
        for cp, *_ in cps:
            cp.wait_send()
        acc = land_ref[0]
        for k in range(1, N_DEV):
            acc = acc + land_ref[k]
        o_ref[...] = acc

    vm = pl.BlockSpec(memory_space=pltpu.VMEM)
    return pl.pallas_call(
        body, name=name, in_specs=[vm, ANY], out_specs=vm, out_shape=jax.ShapeDtypeStruct(v.shape, f32),
        scratch_shapes=[pltpu.VMEM((N_DEV, R, LANE), f32), pltpu.SemaphoreType.DMA((N_DEV - 1,)),
                        pltpu.SemaphoreType.DMA((N_DEV - 1,))],
        compiler_params=pltpu.CompilerParams(vmem_limit_bytes=VMEM_LIMIT),
    )(v, after)


def _pack_small(arrs, mult=2 * SUB):
    flat = jnp.concatenate([a.reshape(-1) for a in arrs])
    rows = -(-flat.shape[0] // (LANE * mult)) * mult
    return jnp.pad(flat, (0, rows * LANE - flat.shape[0])).reshape(rows, LANE)


def _unpack_small(vec, shapes):
    flat, out, o = vec.reshape(-1), [], 0
    for s in shapes:
        n = int(np.prod(s))
        out.append(flat[o:o + n].reshape(s))
        o += n
    return out


REPL_SMALL = ("c_ctx", "b_mod", "q_norm", "k_norm", "c_norm", "d_conv_b", "d_norm_g", "d_norm_b", "ln_g", "ln_b")
SHARD_SMALL = ("b_conv", "c_gate_w2", "c_gate_b", "d_conv_w")
BIG = ("w_mod", "w_in", "w_br", "w_out")
ORDER = ("c_ctx", "w_mod", "b_mod", "w_in", "q_norm", "k_norm", "b_conv", "c_gate_w2", "c_gate_b", "c_norm", "d_conv_w",
         "d_conv_b", "d_norm_g", "d_norm_b", "w_br", "w_out", "ln_g", "ln_b")


def kernel(x, c, ctx, c_ctx, w_mod, b_mod, w_in, q_norm, k_norm, b_conv, c_gate_w2, c_gate_b, c_norm, d_conv_w, d_conv_b, d_norm_g, d_norm_b, w_br, w_out, ln_g, ln_b, loss_target, m_c_ctx, m_w_mod, m_b_mod, m_w_in, m_q_norm, m_k_norm, m_b_conv, m_c_gate_w2, m_c_gate_b, m_c_norm, m_d_conv_w, m_d_conv_b, m_d_norm_g, m_d_norm_b, m_w_br, m_w_out, m_ln_g, m_ln_b, v_c_ctx, v_w_mod, v_b_mod, v_w_in, v_q_norm, v_k_norm, v_b_conv, v_c_gate_w2, v_c_gate_b, v_c_norm, v_d_conv_w, v_d_conv_b, v_d_norm_g, v_d_norm_b, v_w_br, v_w_out, v_ln_g, v_ln_b):
    W = dict(c_ctx=c_ctx, w_mod=w_mod, b_mod=b_mod, w_in=w_in, q_norm=q_norm, k_norm=k_norm, b_conv=b_conv,
             c_gate_w2=c_gate_w2, c_gate_b=c_gate_b, c_norm=c_norm, d_conv_w=d_conv_w, d_conv_b=d_conv_b,
             d_norm_g=d_norm_g, d_norm_b=d_norm_b, w_br=w_br, w_out=w_out, ln_g=ln_g, ln_b=ln_b)
    M = dict(c_ctx=m_c_ctx, w_mod=m_w_mod, b_mod=m_b_mod, w_in=m_w_in, q_norm=m_q_norm, k_norm=m_k_norm, b_conv=m_b_conv,
             c_gate_w2=m_c_gate_w2, c_gate_b=m_c_gate_b, c_norm=m_c_norm, d_conv_w=m_d_conv_w, d_conv_b=m_d_conv_b,
             d_norm_g=m_d_norm_g, d_norm_b=m_d_norm_b, w_br=m_w_br, w_out=m_w_out, ln_g=m_ln_g, ln_b=m_ln_b)
    V = dict(c_ctx=v_c_ctx, w_mod=v_w_mod, b_mod=v_b_mod, w_in=v_w_in, q_norm=v_q_norm, k_norm=v_k_norm, b_conv=v_b_conv,
             c_gate_w2=v_c_gate_w2, c_gate_b=v_c_gate_b, c_norm=v_c_norm, d_conv_w=v_d_conv_w, d_conv_b=v_d_conv_b,
             d_norm_g=v_d_norm_g, d_norm_b=v_d_norm_b, w_br=v_w_br, w_out=v_w_out, ln_g=v_ln_g, ln_b=v_ln_b)
    chip = 2 * lax.axis_index("x") + lax.axis_index("y")
    cidx = lax.axis_index("c").astype(jnp.int32).reshape(1)

    place = jnp.stack([chip, lax.axis_index("c")]).astype(jnp.int32)

    AXIS = dict(w_in=1, w_mod=0, w_br=0, w_out=0)
    ex = dict(w_in=lambda a: jnp.swapaxes(a, 1, 2), w_mod=lambda a: a.reshape(1, DEPTH * D, -1),
              w_br=lambda a: a.reshape(DEPTH, 4 * BRW, -1), w_out=lambda a: a)
    Wx, Mx, Vx = ({k: ex[k](P_[k]) for k in BIG} for P_ in (W, M, V))

    LAYER, MERGE = ("w_in", "w_br", "w_out"), ("w_br", "w_out")
    small_shard = _pack_small([W[k] for k in SHARD_SMALL])
    keys0 = ("w_in", "w_mod")
    sent = lambda k, l: (w_mod[l] if k == "w_mod" else Wx[k][l]).astype(bf16)
    got = _all_gather([sent(k, 0) for k in keys0] + [small_shard], [AXIS[k] for k in keys0] + [0], "all_gather0")
    smalls = [_unpack_small(got[-1][s], [W[k].shape for k in SHARD_SMALL]) for s in range(N_CHIPS)]
    full = {k: jnp.concatenate([smalls[s][i] for s in range(N_CHIPS)], axis=-1) for i, k in enumerate(SHARD_SMALL)}
    ag0b = _split_start([sent(k, 0) for k in MERGE], True, got[0], "all_gather0b_start")
    ag1 = _split_start([sent(k, 1) for k in keys0], True, ag0b[4], "all_gather1_start")
    ag1b = _split_start([sent(k, 1) for k in MERGE], True, ag1[4], "all_gather1b_start")

    def merge_form(w_br4, w_out4):
        return jnp.moveaxis(w_br4.reshape(N_CHIPS, 4, BRW, D // N_CHIPS), 0, 2).reshape(4, BRW, D), w_out4.reshape(D, D)

    def weights_of(l, h):
        first = got if l == 0 else _split_wait(*ag1[:4], True, h, "all_gather1_wait")[1]
        flight = (ag0b, ag1b)[l]
        return (_group_weights(first[0]),
                lambda after: merge_form(*_split_wait(*flight[:4], True, after, f"all_gather{l}b_wait")[1]), first[1])

    red = {k: Wx[k].shape for k in BIG}
    flights, held = {}, {}

    def to_sibling(tag, l, pieces):
        keys = list(pieces)
        halves = _split_start([pieces[k] for k in keys], False, jnp.zeros((SUB, LANE), f32), f"rs_sibling_halves{tag}_start",
                              axes=[AXIS[k] for k in keys])
        flights[tag] = (l, keys, halves)
        return halves[4]

    def launch(tag, after):
        l, keys, halves = flights[tag]
        axes = [AXIS[k] for k in keys]
        pieces, land_a = _split_wait(*halves[:4], False, after, f"rs_sibling_halves{tag}_wait", axes=axes)
        pair = [_add_half(p, la, cidx, ax, f"rs_pair_sum{tag}_{k}") for k, p, la, ax in zip(keys, pieces, land_a, axes)]
        flights[tag] = (l, keys, _split_start(pair, False, jnp.zeros((SUB, LANE), f32), f"rs_chip_exchange{tag}_start"))
        return flights[tag][2][4]

    def land(tag, after):
        l, keys, flight = flights.pop(tag)
        pair, land_b = _split_wait(*flight[:4], False, after, f"rs_chip_exchange{tag}_wait")
        for k, lb, pr in zip(keys, land_b, pair):
            red[k] = _sum_chips(lb, pr, place, AXIS[k], l, red[k], f"rs_chip_sum{tag}_{k}")

    def grads_done(l, gl):
        if "wp" in gl:
            pieces = dict(w_in=_ungroup(gl["wp"]).reshape(N_CHIPS, SHARD, D))
            if l == 0:
                launch("0b", gl["wp"]["A"])
                return to_sibling("0c", 0, pieces)
            return to_sibling("1", 1, {**pieces, **held.pop(1)})
        pieces = dict(w_br=gl["w_br"].reshape(N_CHIPS, 4 * BRW, D // N_CHIPS), w_out=gl["w_out"].reshape(N_CHIPS, D // N_CHIPS, D))
        if l == 0:
            launch("1", gl["w_out"])
            return to_sibling("0b", 0, pieces)
        held[1] = pieces
        return None

    loss, gx, g = _local_step(
        x[0], c, ctx[0], loss_target[0], c_ctx, b_mod, weights_of, q_norm, k_norm, full["b_conv"],
        full["c_gate_w2"], full["c_gate_b"], c_norm, full["d_conv_w"], d_conv_b, d_norm_g, d_norm_b,
        grads_done, ln_g, ln_b, tm=256, token=ag1b[4])
    g["c_gate_w2"], g["c_gate_b"] = g.pop("w2"), g.pop("gb")
    loss = lax.psum(loss, ("x", "y", "c"))

    w_mod_pieces = g["w_mod"].reshape(N_CHIPS, DEPTH * D, 3 * D // N_CHIPS)
    g = {k: (jnp.stack(v) if isinstance(v, list) else v) for k, v in g.items() if k not in ("wp", "w_br", "w_out", "w_mod")}

    small_names = REPL_SMALL + SHARD_SMALL
    launch("0c", gx)
    gs = _all_reduce_small(_pack_small([g[k] for k in small_names]), "all_reduce_small",
                           to_sibling("0d", 0, {"w_mod": w_mod_pieces}))
    gsm = dict(zip(small_names, _unpack_small(gs, [g[k].shape for k in small_names])))
    for k in SHARD_SMALL:
        wdt = W[k].shape[-1]
        gsm[k] = lax.dynamic_slice_in_dim(gsm[k], chip * wdt, wdt, axis=gsm[k].ndim - 1)

    grad, delta, new_m, new_v = {}, {}, {}, {}

    def adamw_big(keys, after):
        filled = _sibling_fill([red[k] for k in keys], [AXIS[k] for k in keys], "rs_sibling_fill_" + keys[0])
        for k, r in zip(keys, filled):
            back = (lambda a: jnp.swapaxes(a, 1, 2)) if k == "w_in" else (lambda a: a.reshape(W[k].shape))
            g_, d_, m_, v_ = _adamw(Wx[k], r, Mx[k], Vx[k], f"adamw_{k}", after=after)
            grad[k], delta[k], new_m[k], new_v[k] = back(g_), back(d_), back(m_), back(v_)
        return d_

    token = launch("0d", gs)
    land("1", gx)
    land("0b", gx)
    last = adamw_big(MERGE, token)
    shapes = [W[k].shape for k in small_names]
    _, d_, m_, v_ = _adamw(*[_pack_small([P_[k] for k in small_names])[None] for P_ in (W, gsm, M, V)], "adamw_small", after=last)
    for k, dd, mm_, vv in zip(small_names, _unpack_small(d_, shapes), _unpack_small(m_, shapes), _unpack_small(v_, shapes)):
        grad[k], delta[k], new_m[k], new_v[k] = gsm[k], dd, mm_, vv
    land("0c", d_)
    last = adamw_big(("w_in",), None)
    land("0d", last)
    adamw_big(("w_mod",), None)

    return (loss, gx[None], *[grad[k] for k in ORDER], *[delta[k] for k in ORDER], *[new_m[k] for k in ORDER],
            *[new_v[k] for k in ORDER])
```

```python
import functools

import jax
import jax.numpy as jnp
import numpy as np
from jax import lax
from jax.experimental import pallas as pl
from jax.experimental.pallas import tpu as pltpu

f32 = jnp.float32
bf16 = jnp.bfloat16

D = 1024
DEPTH = 2
GRID_W = 64
BRW = 512
HD = 128
A_HEADS = 4
C_HEADS = 4
C_KW = 256
C_RANK = 16
C_TAU = 16.0
CH = 128
KB = 3
KD = 31
ALPHA = (2 * DEPTH) ** 0.25
EPS = 1e-6
ROPE_THETA = 10000.0
N_IN = 10784
LR, B1, B2, AEPS, WD, STEP = 0.001, 0.9, 0.999, 1e-08, 0.01, 10

W_M, W_A, W_C, W_G = 4 * D + 4 * BRW, 1024, 5 * BRW, 1152
GROUPS = ("M", "A", "C", "G")
M_GA, M_GB, M_GC, M_GD = 4 * D, 4 * D + BRW, 4 * D + 2 * BRW, 4 * D + 3 * BRW
A_K, A_V = 512, 768
G_K, G_V, G_R = 256, 512, 1024
S_Q, S_GA, S_B, S_C, S_X, S_GB, S_CQ, S_CV, S_GC, S_R, S_DA, S_DG, S_GD, S_MG = (
    0, 1024, 1536, 2048, 2560, 3072, 3584, 4096, 4608, 5120, 5152, 5664, 6176, 6688)

LANE = 128
SUB = 8
VMEM_LIMIT = 56 * 1024 * 1024
CONV_PAD = 16
GLA_SUB = 16
GLA_CLAMP = 60.0


def _cparams(sem, vmem=VMEM_LIMIT):
    return pltpu.CompilerParams(dimension_semantics=sem, vmem_limit_bytes=vmem)


def _dg(a, b, ca, cb):
    return lax.dot_general(a.astype(bf16), b.astype(bf16), (((ca,), (cb,)), ((), ())),
                           preferred_element_type=f32)


@jax.custom_vjp
def mm(a, b):
    return _dg(a, b, 1, 0)


mm.defvjp(lambda a, b: (_dg(a, b, 1, 0), (a, b)),
          lambda r, ct: (_dg(ct, r[1], 1, 1).astype(r[0].dtype), _dg(r[0], ct, 0, 0).astype(r[1].dtype)))


@jax.custom_vjp
def mm_nt(a, b):
    return _dg(a, b, 1, 1)


mm_nt.defvjp(lambda a, b: (_dg(a, b, 1, 1), (a, b)),
             lambda r, ct: (_dg(ct, r[1], 1, 0).astype(r[0].dtype), _dg(ct, r[0], 0, 0).astype(r[1].dtype)))


@jax.custom_vjp
def mm_tn(a, b):
    return _dg(a, b, 0, 0)


mm_tn.defvjp(lambda a, b: (_dg(a, b, 0, 0), (a, b)),
             lambda r, ct: (_dg(r[1], ct, 1, 1).astype(r[0].dtype), _dg(r[0], ct, 1, 0).astype(r[1].dtype)))


@jax.custom_vjp
def _sigmoid(x):
    return 0.5 * jnp.tanh(0.5 * x) + 0.5


def _sigmoid_fwd(x):
    s = _sigmoid(x)
    return s, s


_sigmoid.defvjp(_sigmoid_fwd, lambda s, ct: (ct * (s - s * s),))


@jax.custom_vjp
def _silu(x):
    return x * _sigmoid(x)


def _silu_fwd(x):
    s = _sigmoid(x)
    return x * s, (x, s)


_silu.defvjp(_silu_fwd, lambda r, ct: (ct * (r[1] + r[0] * (r[1] - r[1] * r[1])),))


def _ln(x):
    mu = jnp.mean(x, -1, keepdims=True)
    xc = x - mu
    var = jnp.mean(xc * xc, -1, keepdims=True)
    return xc * lax.rsqrt(var + EPS)


def _rms(x, g):
    return x * lax.rsqrt(jnp.mean(x * x, -1, keepdims=True) + EPS) * g


@jax.custom_vjp
def _rope(x, cos_f, sin_a, sin_b):
    return x * cos_f + pltpu.roll(x, HD - 1, 1) * sin_a + pltpu.roll(x, 1, 1) * sin_b


def _rope_fwd(x, cos_f, sin_a, sin_b):
    return _rope(x, cos_f, sin_a, sin_b), (cos_f, sin_a, sin_b)


def _rope_bwd(r, ct):
    cos_f, sin_a, sin_b = r
    dx = ct * cos_f + pltpu.roll(ct * sin_a, 1, 1) + pltpu.roll(ct * sin_b, HD - 1, 1)
    return dx, jnp.zeros_like(cos_f), jnp.zeros_like(sin_a), jnp.zeros_like(sin_b)


_rope.defvjp(_rope_fwd, _rope_bwd)


def _row_ids(i, tm):
    return i * tm + lax.broadcasted_iota(jnp.int32, (tm, 1), 0)


def _partial_rows(ref, rows):
    n = len(rows)
    for k, r in enumerate(rows):
        ref[k:k + 1, :] = r
    ref[n:SUB, :] = jnp.zeros((SUB - n, ref.shape[-1]), f32)


def _matmul(a, b, mode, tm, tn, tk, name, out_dtype=f32, add=None, after=None):
    sect = a.ndim == 3
    a2 = (a.shape[1], a.shape[0] * a.shape[2]) if sect else a.shape
    if mode == "nn":
        (M, K), N = a2, b.shape[1]
        a_spec = pl.BlockSpec((None, tm, tk), lambda j, i, k: (k, i, 0)) if sect else pl.BlockSpec((tm, tk), lambda j, i, k: (i, k))
        b_spec = pl.BlockSpec((tk, tn), lambda j, i, k: (k, j))
        ca, cb = 1, 0
        assert not sect or tk == a.shape[2]
    elif mode == "nt":
        (M, K), N = a2, b.shape[0]
        assert not sect
        a_spec = pl.BlockSpec((tm, tk), lambda j, i, k: (i, k))
        b_spec = pl.BlockSpec((tn, tk), lambda j, i, k: (j, k))
        ca, cb = 1, 1
    else:
        (K, M), N = a2, b.shape[1]
        a_spec = pl.BlockSpec((None, tk, tm), lambda j, i, k: (i, k, 0)) if sect else pl.BlockSpec((tk, tm), lambda j, i, k: (k, i))
        b_spec = pl.BlockSpec((tk, tn), lambda j, i, k: (k, j))
        ca, cb = 0, 0
        assert not sect or tm == a.shape[2]
    assert M % tm == 0 and N % tn == 0 and K % tk == 0, (name, M, N, K, tm, tn, tk)
    nk = K // tk

    o_spec = pl.BlockSpec((tm, tn), lambda j, i, k: (i, j))

    def body(a_ref, b_ref, *rest):
        add_ref = rest[0] if add is not None else None
        o_ref, acc_ref = rest[-2:]
        k = pl.program_id(2)
        part = _dg(a_ref[...], b_ref[...], ca, cb)

        @pl.when(k == 0)
        def _():
            acc_ref[...] = part if add_ref is None else part + add_ref[...]

        @pl.when(k > 0)
        def _():
            acc_ref[...] += part

        @pl.when(k == nk - 1)
        def _():
            o_ref[...] = acc_ref[...].astype(o_ref.dtype)

    extra = ([] if add is None else [(o_spec, add)]) + ([] if after is None else [(pl.BlockSpec(memory_space=pl.ANY), after)])
    return pl.pallas_call(
        body, name=name, grid=(N // tn, M // tm, nk),
        in_specs=[a_spec, b_spec] + [s_ for s_, _ in extra], out_specs=o_spec,
        out_shape=jax.ShapeDtypeStruct((M, N), out_dtype),
        scratch_shapes=[pltpu.VMEM((tm, tn), f32)],
        compiler_params=_cparams(("parallel", "parallel", "arbitrary")),
    )(a, b, *[v_ for _, v_ in extra])


def _matmul_groups(a, b, tks, tm, name, after=None):
    keys = list(a)
    M = a[keys[0]].shape[-2]
    N = b[keys[0]].shape[1]
    count = {g: b[g].shape[0] // tks[g] for g in keys}
    first, total = {}, 0
    for g in keys:
        first[g], total = total, total + count[g]

    def k_of(g):
        return lambda s: jnp.clip(s - first[g], 0, count[g] - 1)

    a_specs = [pl.BlockSpec((None, tm, tks[g]), functools.partial(lambda i, s, kk: (kk(s), i, 0), kk=k_of(g)))
               if a[g].ndim == 3 else pl.BlockSpec((tm, tks[g]), functools.partial(lambda i, s, kk: (i, kk(s)), kk=k_of(g)))
               for g in keys]
    b_specs = [pl.BlockSpec((tks[g], N), functools.partial(lambda i, s, kk: (kk(s), 0), kk=k_of(g))) for g in keys]
    n = len(keys)

    def body(*refs):
        o_ref, acc_ref = refs[-2:]
        s = pl.program_id(1)

        @pl.when(s == 0)
        def _():
            acc_ref[...] = jnp.zeros_like(acc_ref)

        for j, g in enumerate(keys):
            @pl.when((s >= first[g]) & (s < first[g] + count[g]))
            def _(j=j):
                acc_ref[...] += _dg(refs[j][...], refs[n + j][...], 1, 0)

        @pl.when(s == total - 1)
        def _():
            o_ref[...] = acc_ref[...]

    extra = [] if after is None else [after]
    return pl.pallas_call(
        body, name=name, grid=(M // tm, total),
        in_specs=a_specs + b_specs + [pl.BlockSpec(memory_space=pl.ANY)] * len(extra),
        out_specs=pl.BlockSpec((tm, N), lambda i, s: (i, 0)),
        out_shape=jax.ShapeDtypeStruct((M, N), f32),
        scratch_shapes=[pltpu.VMEM((tm, N), f32)],
        compiler_params=_cparams(("parallel", "arbitrary")),
    )(*[a[g] for g in keys], *[b[g] for g in keys], *extra)


def _matmul_tn_batched(a, b, ns, name):
    B, K, M = a.shape
    N = b.shape[2] // ns

    def body(a_ref, b_ref, o_ref):
        o_ref[...] = _dg(a_ref[...], b_ref[...], 0, 0).astype(bf16)

    return pl.pallas_call(
        body, name=name, grid=(B, ns),
        in_specs=[pl.BlockSpec((None, K, M), lambda i, s: (i, 0, 0)), pl.BlockSpec((None, K, N), lambda i, s: (i, 0, s))],
        out_specs=pl.BlockSpec((None, None, M, N), lambda i, s: (s, i, 0, 0)),
        out_shape=jax.ShapeDtypeStruct((ns, B, M, N), bf16),
        compiler_params=_cparams(("parallel", "parallel")),
    )(a, b)


MOD_TN = 768


def _mod_fwd(cin, w_mod_l, b_mod_l, name):
    def body(c_ref, w_ref, b_ref, o_ref):
        o_ref[...] = mm(_silu(c_ref[...]), w_ref[...]) + b_ref[...]

    return pl.pallas_call(
        body, name=name, grid=(3 * D // MOD_TN,),
        in_specs=[pl.BlockSpec((SUB, D), lambda j: (0, 0)), pl.BlockSpec((None, D, MOD_TN), lambda j: (j, 0, 0)),
                  pl.BlockSpec((1, MOD_TN), lambda j: (0, j))],
        out_specs=pl.BlockSpec((SUB, MOD_TN), lambda j: (0, j)),
        out_shape=jax.ShapeDtypeStruct((SUB, 3 * D), f32),
        compiler_params=_cparams(("parallel",)),
    )(cin, w_mod_l, b_mod_l[None, :])


def _mod_bwd(cin, w_mods, dmodv):
    nj = 3 * D // MOD_TN

    def body(c_ref, *refs):
        g_ref, dw_ref, dc_ref = refs[DEPTH:]
        w = refs[0][...]
        for l in range(1, DEPTH):
            w = jnp.where(pl.program_id(0) == l, refs[l][...], w)
        _, vjp = jax.vjp(lambda c, w: mm(_silu(c), w), c_ref[...], w.astype(f32))
        dc, dw = vjp(g_ref[...])
        dw_ref[...] = dw.astype(bf16)
        dc_ref[...] = dc

    return pl.pallas_call(
        body, name="mod_bwd", grid=(DEPTH, nj),
        in_specs=[pl.BlockSpec((SUB, D), lambda l, j: (0, 0))]
        + [pl.BlockSpec((None, D, MOD_TN), lambda l, j: (j, 0, 0))] * DEPTH
        + [pl.BlockSpec((None, SUB, MOD_TN), lambda l, j: (l, 0, j))],
        out_specs=[pl.BlockSpec((None, None, D, MOD_TN), lambda l, j: (j, l, 0, 0)),
                   pl.BlockSpec((None, None, SUB, D), lambda l, j: (l, j, 0, 0))],
        out_shape=[jax.ShapeDtypeStruct((nj, DEPTH, D, MOD_TN), bf16),
                   jax.ShapeDtypeStruct((DEPTH, nj, SUB, D), f32)],
        compiler_params=_cparams(("parallel", "parallel")),
    )(cin, *w_mods, dmodv)


def _u_fn(h, m_l, m_c, isctx):
    n = _ln(h)
    shift = jnp.where(isctx, m_c[:, 0:D], m_l[:, 0:D])
    scale = jnp.where(isctx, m_c[:, D:2 * D], m_l[:, D:2 * D])
    return n * (1.0 + scale) + shift


def _ln_fwd(h, modv_l, tc, tm, name):
    T = h.shape[0]

    def body(h_ref, m_ref, u_ref):
        isctx = _row_ids(pl.program_id(0), tm) < tc
        u_ref[...] = _u_fn(h_ref[...], m_ref[0:1, :], m_ref[1:2, :], isctx).astype(bf16)

    return pl.pallas_call(
        body, name=name, grid=(T // tm,),
        in_specs=[pl.BlockSpec((tm, D), lambda i: (i, 0)), pl.BlockSpec((SUB, 3 * D), lambda i: (0, 0))],
        out_specs=pl.BlockSpec((tm, D), lambda i: (i, 0)),
        out_shape=jax.ShapeDtypeStruct((T, D), bf16),
        compiler_params=_cparams(("parallel",)),
    )(h, modv_l)


def _ln_bwd(du, h, dh_res, modv_l, tc, tm, name, latent_only=False):
    T = h.shape[0]
    nt, nct = T // tm, tc // tm

    def body(du_ref, h_ref, r_ref, m_ref, dh_ref, dm_ref):
        isctx = _row_ids(pl.program_id(0), tm) < tc
        _, vjp = jax.vjp(lambda h, ml, mc: _u_fn(h, ml, mc, isctx), h_ref[...], m_ref[0:1, :], m_ref[1:2, :])
        dh, dml, dmc = vjp(du_ref[...])
        dh_ref[...] = dh + r_ref[...]
        _partial_rows(dm_ref, [dml, dmc])

    dh_map = (lambda i: (jnp.maximum(i - nct, 0), 0)) if latent_only else (lambda i: (i, 0))
    return pl.pallas_call(
        body, name=name, grid=(nt,),
        in_specs=[pl.BlockSpec((tm, D), lambda i: (i, 0)), pl.BlockSpec((tm, D), lambda i: (i, 0)),
                  pl.BlockSpec((tm, D), lambda i: (i, 0)), pl.BlockSpec((SUB, 3 * D), lambda i: (0, 0))],
        out_specs=[pl.BlockSpec((tm, D), dh_map), pl.BlockSpec((None, SUB, 3 * D), lambda i: (i, 0, 0))],
        out_shape=[jax.ShapeDtypeStruct((T - tc if latent_only else T, D), f32), jax.ShapeDtypeStruct((nt, SUB, 3 * D), f32)],
        compiler_params=_cparams(("arbitrary",)),
    )(du, h, dh_res, modv_l)


def _prep_fn(q, k, qg, kg, cos_f, sin_a, sin_b):
    qs = [_rope(_rms(q[:, HD * i:HD * (i + 1)], qg), cos_f, sin_a, sin_b) * (HD ** -0.5) for i in range(A_HEADS)]
    ks = [_rope(_rms(k[:, HD * i:HD * (i + 1)], kg), cos_f, sin_a, sin_b) for i in range(A_HEADS // 2)]
    return jnp.concatenate(qs, 1), jnp.concatenate(ks, 1)


def _tok(tm, w, off):
    return pl.BlockSpec((tm, w), lambda i: (i, off // w))


def _vec(w):
    return pl.BlockSpec((1, w), lambda i: (0, 0))


def _prep_fwd(P, qg, kg, rope, tm, name):
    T = P.shape[0]

    def body(q_ref, k_ref, v_ref, qg_ref, kg_ref, c_ref, sa_ref, sb_ref, qn_ref, kn_ref, vb_ref):
        qn, kn = _prep_fn(q_ref[...].astype(f32), k_ref[...].astype(f32), qg_ref[...], kg_ref[...], c_ref[...], sa_ref[...],
                          sb_ref[...])
        qn_ref[...] = qn.astype(bf16)
        kn_ref[...] = kn.astype(bf16)
        vb_ref[...] = v_ref[...].astype(bf16)

    return pl.pallas_call(
        body, name=name, grid=(T // tm,),
        in_specs=[_tok(tm, 512, 0), _tok(tm, 256, A_K), _tok(tm, 256, A_V), _vec(HD), _vec(HD),
                  _tok(tm, HD, 0), _tok(tm, HD, 0), _tok(tm, HD, 0)],
        out_specs=[_tok(tm, 512, 0), _tok(tm, 256, 0), _tok(tm, 256, 0)],
        out_shape=[jax.ShapeDtypeStruct((T, 512), bf16), jax.ShapeDtypeStruct((T, 256), bf16),
                   jax.ShapeDtypeStruct((T, 256), bf16)],
        compiler_params=_cparams(("parallel",)),
    )(P, P, P, qg, kg, *rope)


def _prep_bwd(P, dqn, dkn, dv, qg, kg, rope, tm, name):
    T = P.shape[0]
    nt = T // tm

    def body(q_ref, k_ref, dq_ref, dk_ref, dv_ref, qg_ref, kg_ref, c_ref, sa_ref, sb_ref, o_ref, og_ref):
        tabs = (c_ref[...], sa_ref[...], sb_ref[...])
        _, vjp = jax.vjp(lambda q, k, a, b: _prep_fn(q, k, a, b, *tabs), q_ref[...].astype(f32), k_ref[...].astype(f32),
                         qg_ref[...], kg_ref[...])
        dq, dk, dqg, dkg = vjp((dq_ref[...], dk_ref[...]))
        o_ref[:, 0:A_K] = dq.astype(bf16)
        o_ref[:, A_K:A_V] = dk.astype(bf16)
        o_ref[:, A_V:W_A] = dv_ref[...].astype(bf16)
        _partial_rows(og_ref, [dqg, dkg])

    return pl.pallas_call(
        body, name=name, grid=(nt,),
        in_specs=[_tok(tm, 512, 0), _tok(tm, 256, A_K), _tok(tm, 512, 0), _tok(tm, 256, 0), _tok(tm, 256, 0),
                  _vec(HD), _vec(HD), _tok(tm, HD, 0), _tok(tm, HD, 0), _tok(tm, HD, 0)],
        out_specs=[_tok(tm, W_A, 0), pl.BlockSpec((None, SUB, HD), lambda i: (i, 0, 0))],
        out_shape=[jax.ShapeDtypeStruct((T, W_A), bf16), jax.ShapeDtypeStruct((nt, SUB, HD), f32)],
        compiler_params=_cparams(("parallel",)),
    )(P, P, dqn, dkn, dv, qg, kg, *rope)


def _attn_fn(q, k, v, lim):
    col = lax.broadcasted_iota(jnp.int32, (1, k.shape[0]), 1)
    s = mm_nt(q, k) + jnp.where(col < lim, 0.0, -1e30)
    m = lax.stop_gradient(jnp.max(s, -1, keepdims=True))
    e = jnp.exp(s - m)
    p = e * (1.0 / jnp.sum(e, -1, keepdims=True))
    return mm(p, v)


def _attn_fwd(qn, kn, vb, tc, tq, name):
    T = qn.shape[0]

    def body(q_ref, k_ref, v_ref, o_ref):
        lim = jnp.where(pl.program_id(1) * tq < tc, tc, T)
        o_ref[...] = _attn_fn(q_ref[...], k_ref[...], v_ref[...], lim)

    return pl.pallas_call(
        body, name=name, grid=(A_HEADS, T // tq),
        in_specs=[pl.BlockSpec((tq, HD), lambda h, i: (i, h)), pl.BlockSpec((T, HD), lambda h, i: (0, h // 2)),
                  pl.BlockSpec((T, HD), lambda h, i: (0, h // 2))],
        out_specs=pl.BlockSpec((tq, HD), lambda h, i: (i, h)),
        out_shape=jax.ShapeDtypeStruct((T, 512), f32),
        compiler_params=_cparams(("parallel", "parallel")),
    )(qn, kn, vb)


def _attn_bwd(qn, kn, vb, dya, tc, tq, name):
    T = qn.shape[0]

    def body(q_ref, k_ref, v_ref, g_ref, dq_ref, dk_ref, dv_ref):
        first = (pl.program_id(1) == 0) & (pl.program_id(2) == 0)
        lim = jnp.where(pl.program_id(2) * tq < tc, tc, T)
        _, vjp = jax.vjp(lambda q, k, v: _attn_fn(q, k, v, lim), q_ref[...].astype(f32), k_ref[...].astype(f32),
                         v_ref[...].astype(f32))
        dq, dk, dv = vjp(g_ref[...])
        dq_ref[...] = dq

        @pl.when(first)
        def _():
            dk_ref[...] = dk
            dv_ref[...] = dv

        @pl.when(jnp.logical_not(first))
        def _():
            dk_ref[...] += dk
            dv_ref[...] += dv

    qspec = pl.BlockSpec((tq, HD), lambda kv, g, i: (i, 2 * kv + g))
    kspec = pl.BlockSpec((T, HD), lambda kv, g, i: (0, kv))
    return pl.pallas_call(
        body, name=name, grid=(A_HEADS // 2, 2, T // tq),
        in_specs=[qspec, kspec, kspec, qspec], out_specs=[qspec, kspec, kspec],
        out_shape=[jax.ShapeDtypeStruct((T, 512), f32), jax.ShapeDtypeStruct((T, 256), f32),
                   jax.ShapeDtypeStruct((T, 256), f32)],
        compiler_params=_cparams(("parallel", "arbitrary", "arbitrary")),
    )(qn, kn, vb, dya)


def _conv_rows(tc, tl):
    return CONV_PAD + tc + CONV_PAD + tl + CONV_PAD


def _fill_pad(pad_ref, val, tc, tl):
    z = jnp.zeros((CONV_PAD, LANE), f32)
    pad_ref[0:CONV_PAD, :] = z
    pad_ref[CONV_PAD:CONV_PAD + tc, :] = val[0:tc]
    pad_ref[CONV_PAD + tc:2 * CONV_PAD + tc, :] = z
    pad_ref[2 * CONV_PAD + tc:2 * CONV_PAD + tc + tl, :] = val[tc:tc + tl]
    pad_ref[2 * CONV_PAD + tc + tl:3 * CONV_PAD + tc + tl, :] = z


def _conv_apply(pad_ref, w_ref, K, tc, tl, rc, emit, flip=False):
    half = K // 2
    for seg0, off, n in ((0, CONV_PAD, tc), (tc, 2 * CONV_PAD + tc, tl)):
        for r0 in range(0, n, rc):
            acc = None
            for k in range(K):
                sh = (half - k) if flip else (k - half)
                term = pad_ref[pl.ds(off + r0 + sh, rc), :] * w_ref[k:k + 1, :]
                acc = term if acc is None else acc + term
            emit(seg0 + r0, acc)


def _conv_wgrad(pad_ref, dy_ref, K, tc, tl, rc, dw_ref):
    half = K // 2
    for k in range(K):
        acc = jnp.zeros((1, LANE), f32)
        for seg0, off, n in ((0, CONV_PAD, tc), (tc, 2 * CONV_PAD + tc, tl)):
            for r0 in range(0, n, rc):
                acc = acc + jnp.sum(pad_ref[pl.ds(off + r0 + k - half, rc), :] * dy_ref[pl.ds(seg0 + r0, rc), :],
                                    axis=0, keepdims=True)
        dw_ref[k:k + 1, :] = acc


def _col(T, off):
    return pl.BlockSpec((T, LANE), lambda j: (0, off // LANE + j))


C_B, C_C, C_X, C_A, C_G = range(5)
N_SEC = 5


class _Sections:
    def __init__(self, refs):
        self.refs = refs

    def __getitem__(self, idx):
        rows, sec = idx
        return self.refs[sec][rows, :].astype(f32)

    def __setitem__(self, idx, val):
        rows, sec = idx
        self.refs[sec, rows, :] = val


def _sec_specs(T):
    return [pl.BlockSpec((T, LANE), functools.partial(lambda j, s: (0, s * (BRW // LANE) + j), s=s)) for s in range(N_SEC)]


def _conv_fwd(P, wb, wd, bd, tc, tl, rc, name):
    T = tc + tl

    def body(*refs):
        p_ref = _Sections(refs[:N_SEC])
        wb_ref, wd_ref, bd_ref, yb_ref, hh_ref, pad_ref = refs[N_SEC:]
        _fill_pad(pad_ref, p_ref[:, C_C] * p_ref[:, C_X], tc, tl)

        def emit_b(r0, y):
            yb_ref[pl.ds(r0, rc), :] = y * p_ref[pl.ds(r0, rc), C_B]

        _conv_apply(pad_ref, wb_ref, KB, tc, tl, rc, emit_b)
        _fill_pad(pad_ref, p_ref[:, C_A] * _sigmoid(p_ref[:, C_G]), tc, tl)

        def emit_d(r0, y):
            hh_ref[pl.ds(r0, rc), :] = y + bd_ref[...]

        _conv_apply(pad_ref, wd_ref, KD, tc, tl, rc, emit_d)

    return pl.pallas_call(
        body, name=name, grid=(BRW // LANE,),
        in_specs=_sec_specs(T) + [pl.BlockSpec((KB, LANE), lambda j: (0, j)), pl.BlockSpec((KD, LANE), lambda j: (0, j)),
                                  pl.BlockSpec((1, LANE), lambda j: (0, j))],
        out_specs=[_col(T, 0), _col(T, 0)],
        out_shape=[jax.ShapeDtypeStruct((T, BRW), f32), jax.ShapeDtypeStruct((T, BRW), f32)],
        scratch_shapes=[pltpu.VMEM((_conv_rows(tc, tl), LANE), f32)],
        compiler_params=_cparams(("parallel",)),
    )(*[P] * N_SEC, wb, wd, bd)


def _conv_bwd(P, dyb, dhh, wb, wd, tc, tl, rc, name):
    T = tc + tl

    def body(*refs):
        p_ref = _Sections(refs[:N_SEC])
        dyb_ref, dhh_ref, wb_ref, wd_ref, dp3_ref, dwb_ref, dwd_ref, dbd_ref, pad_ref, pad2_ref, tmp_ref = refs[N_SEC:]
        dp_ref = _Sections(dp3_ref)
        _fill_pad(pad_ref, p_ref[:, C_C] * p_ref[:, C_X], tc, tl)

        def emit_cv(r0, y):
            dp_ref[pl.ds(r0, rc), C_B] = (y * dyb_ref[pl.ds(r0, rc), :]).astype(bf16)

        _conv_apply(pad_ref, wb_ref, KB, tc, tl, rc, emit_cv)
        tmp_ref[...] = dyb_ref[...] * p_ref[:, C_B]
        _conv_wgrad(pad_ref, tmp_ref, KB, tc, tl, rc, dwb_ref)
        _fill_pad(pad2_ref, tmp_ref[...], tc, tl)

        def emit_ds(r0, y):
            dp_ref[pl.ds(r0, rc), C_C] = (y * p_ref[pl.ds(r0, rc), C_X]).astype(bf16)
            dp_ref[pl.ds(r0, rc), C_X] = (y * p_ref[pl.ds(r0, rc), C_C]).astype(bf16)

        _conv_apply(pad2_ref, wb_ref, KB, tc, tl, rc, emit_ds, flip=True)
        _fill_pad(pad_ref, p_ref[:, C_A] * _sigmoid(p_ref[:, C_G]), tc, tl)
        _conv_wgrad(pad_ref, dhh_ref, KD, tc, tl, rc, dwd_ref)
        dbd_ref[...] = jnp.sum(dhh_ref[...], axis=0, keepdims=True)
        _fill_pad(pad2_ref, dhh_ref[...], tc, tl)

        def emit_d2(r0, y):
            sg = _sigmoid(p_ref[pl.ds(r0, rc), C_G])
            a = p_ref[pl.ds(r0, rc), C_A]
            dp_ref[pl.ds(r0, rc), C_A] = (y * sg).astype(bf16)
            dp_ref[pl.ds(r0, rc), C_G] = (y * a * sg * (1.0 - sg)).astype(bf16)

        _conv_apply(pad2_ref, wd_ref, KD, tc, tl, rc, emit_d2, flip=True)

    return pl.pallas_call(
        body, name=name, grid=(BRW // LANE,),
        in_specs=_sec_specs(T) + [_col(T, 0), _col(T, 0),
                                  pl.BlockSpec((KB, LANE), lambda j: (0, j)), pl.BlockSpec((KD, LANE), lambda j: (0, j))],
        out_specs=[pl.BlockSpec((N_SEC, T, LANE), lambda j: (0, 0, j)), pl.BlockSpec((KB, LANE), lambda j: (0, j)),
                   pl.BlockSpec((KD, LANE), lambda j: (0, j)), pl.BlockSpec((1, LANE), lambda j: (0, j))],
        out_shape=[jax.ShapeDtypeStruct((N_SEC, T, BRW), bf16), jax.ShapeDtypeStruct((KB, BRW), f32),
                   jax.ShapeDtypeStruct((KD, BRW), f32), jax.ShapeDtypeStruct((1, BRW), f32)],
        scratch_shapes=[pltpu.VMEM((_conv_rows(tc, tl), LANE), f32), pltpu.VMEM((_conv_rows(tc, tl), LANE), f32),
                        pltpu.VMEM((T, LANE), f32)],
        compiler_params=_cparams(("parallel",)),
    )(*[P] * N_SEC, dyb, dhh, wb, wd)


def _gla_chunk(q, k, v, r, w2, b2, st, isfwd):
    z = mm(r, w2) + b2
    g = jax.nn.log_sigmoid(z[:, 0:C_KW] if isfwd else z[:, C_KW:2 * C_KW]) / C_TAU
    ri = lax.broadcasted_iota(jnp.int32, (CH, CH), 0)
    ci = lax.broadcasted_iota(jnp.int32, (CH, CH), 1)
    tri = ((ci <= ri) if isfwd else (ci >= ri)).astype(f32)
    cum = jnp.dot(tri, g, preferred_element_type=f32, precision=lax.Precision.HIGHEST)
    last = jnp.sum(g, axis=0, keepdims=True)
    q = q * (C_KW // C_HEADS) ** -0.5
    hv = lax.broadcasted_iota(jnp.int32, (BRW, C_KW), 0) // (BRW // C_HEADS)
    hk = lax.broadcasted_iota(jnp.int32, (BRW, C_KW), 1) // (C_KW // C_HEADS)
    st_new = st * jnp.exp(last) + jnp.where(hv == hk, mm_tn(v, k * jnp.exp(last - cum)), 0.0)
    o = mm_nt(q * jnp.exp(cum), st)
    rowi = lax.broadcasted_iota(jnp.int32, (CH, C_KW), 0)
    srow = lax.broadcasted_iota(jnp.int32, (C_HEADS * CH, C_KW), 0)
    slane = lax.broadcasted_iota(jnp.int32, (C_HEADS * CH, C_KW), 1)
    own_lanes = srow // CH == slane // (C_KW // C_HEADS)
    pos = lax.broadcasted_iota(jnp.int32, (C_HEADS * CH, CH), 0) % CH
    key = lax.broadcasted_iota(jnp.int32, (C_HEADS * CH, CH), 1)
    scores = jnp.zeros((C_HEADS * CH, CH), f32)
    for a in range(CH // GLA_SUB):
        idx = GLA_SUB * a - 1 if isfwd else GLA_SUB * (a + 1)
        ref = jnp.sum(jnp.where(rowi == idx, cum, 0.0), axis=0, keepdims=True)
        qa = q * jnp.exp(jnp.minimum(cum - ref, 0.0))
        ka = k * jnp.exp(jnp.minimum(ref - cum, GLA_CLAMP))
        s = mm_nt(jnp.where(own_lanes, jnp.concatenate([qa] * C_HEADS, axis=0), 0.0), ka)
        scores = scores + jnp.where(pos // GLA_SUB == a, s, 0.0)
    scores = jnp.where((key <= pos) if isfwd else (key >= pos), scores, 0.0)
    vw = BRW // C_HEADS
    o = o + jnp.concatenate([mm(scores[CH * hd:CH * (hd + 1)], v[:, vw * hd:vw * (hd + 1)]) for hd in range(C_HEADS)],
                            axis=1)
    return o, st_new


def _gla_chunk_of(d, n, nc, nch):
    back = jnp.where(n < nc, nc - 1 - n, nch - 1 - (n - nc))
    return jnp.where(d == 0, n, back)


def _gla_fwd(P, w2, b2, tc, name):
    T = P.shape[0]
    nch, nc = T // CH, tc // CH

    back = lambda n: _gla_chunk_of(1, n, nc, nch)

    def body(pf_ref, pb_ref, w_ref, b_ref, of_ref, ob_ref, ssf_ref, ssb_ref, stf_ref, stb_ref):
        @pl.when(pl.program_id(0) == 0)
        def _():
            stf_ref[...] = jnp.zeros_like(stf_ref)
            stb_ref[...] = jnp.zeros_like(stb_ref)

        for p_ref, o_ref, ss_ref, st_ref, isfwd in ((pf_ref, of_ref, ssf_ref, stf_ref, True),
                                                    (pb_ref, ob_ref, ssb_ref, stb_ref, False)):
            st = st_ref[...]
            ss_ref[...] = st
            p = p_ref[...].astype(f32)
            o, st_new = _gla_chunk(p[:, 0:G_K], p[:, G_K:G_V], p[:, G_V:G_R], p[:, G_R:W_G], w_ref[...], b_ref[...], st, isfwd)
            o_ref[...] = o
            st_ref[...] = st_new

    sd = jax.ShapeDtypeStruct
    return pl.pallas_call(
        body, name=name, grid=(nch,),
        in_specs=[pl.BlockSpec((CH, W_G), lambda n: (n, 0)), pl.BlockSpec((CH, W_G), lambda n: (back(n), 0)),
                  pl.BlockSpec((LANE, 512), lambda n: (0, 0)), pl.BlockSpec((1, 512), lambda n: (0, 0))],
        out_specs=[pl.BlockSpec((CH, BRW), lambda n: (n, 0)), pl.BlockSpec((CH, BRW), lambda n: (back(n), 0)),
                   pl.BlockSpec((None, BRW, C_KW), lambda n: (n, 0, 0)), pl.BlockSpec((None, BRW, C_KW), lambda n: (n, 0, 0))],
        out_shape=[sd((T, BRW), f32), sd((T, BRW), f32), sd((nch, BRW, C_KW), f32), sd((nch, BRW, C_KW), f32)],
        scratch_shapes=[pltpu.VMEM((BRW, C_KW), f32), pltpu.VMEM((BRW, C_KW), f32)],
        compiler_params=_cparams(("arbitrary",)),
    )(P, P, w2, b2)


def _gla_bwd(P, w2, b2, ssave, doc, tc, name):
    T = P.shape[0]
    nch, nc = T // CH, tc // CH

    fwd_chunk = lambda m: nch - 1 - m
    back_chunk = lambda m: _gla_chunk_of(1, nch - 1 - m, nc, nch)

    def body(pf_ref, pb_ref, w_ref, b_ref, ssf_ref, ssb_ref, gf_ref, gb_ref, dpf_ref, dpb_ref, dw_ref, db_ref,
             dstf_ref, dstb_ref):
        m = pl.program_id(0)

        @pl.when(m == 0)
        def _():
            dstf_ref[...] = jnp.zeros_like(dstf_ref)
            dstb_ref[...] = jnp.zeros_like(dstb_ref)

        dw_sum, db_sum = None, None
        for p_ref, ss_ref, g_ref, dp_ref, dst_ref, isfwd in ((pf_ref, ssf_ref, gf_ref, dpf_ref, dstf_ref, True),
                                                             (pb_ref, ssb_ref, gb_ref, dpb_ref, dstb_ref, False)):
            p = p_ref[...].astype(f32)
            _, vjp = jax.vjp(lambda q, k, v, r, w, b, st: _gla_chunk(q, k, v, r, w, b, st, isfwd),
                             p[:, 0:G_K], p[:, G_K:G_V], p[:, G_V:G_R], p[:, G_R:W_G], w_ref[...], b_ref[...], ss_ref[...])
            dq, dk, dv, dr, dw, db, dst = vjp((g_ref[...], dst_ref[...]))
            dp_ref[:, 0:G_K] = dq
            dp_ref[:, G_K:G_V] = dk
            dp_ref[:, G_V:G_R] = dv
            dp_ref[:, G_R:W_G] = dr
            dst_ref[...] = dst
            dw_sum = dw if dw_sum is None else dw_sum + dw
            db_sum = db if db_sum is None else db_sum + db

        @pl.when(m == 0)
        def _():
            dw_ref[...] = dw_sum
            _partial_rows(db_ref, [db_sum])

        @pl.when(m > 0)
        def _():
            dw_ref[...] += dw_sum
            db_ref[0:1, :] += db_sum

    ssf, ssb = ssave
    chunk_f = lambda w: pl.BlockSpec((CH, w), lambda m: (fwd_chunk(m), 0))
    chunk_b = lambda w: pl.BlockSpec((CH, w), lambda m: (back_chunk(m), 0))
    state = pl.BlockSpec((None, BRW, C_KW), lambda m: (nch - 1 - m, 0, 0))
    sd = jax.ShapeDtypeStruct
    return pl.pallas_call(
        body, name=name, grid=(nch,),
        in_specs=[chunk_f(W_G), chunk_b(W_G), pl.BlockSpec((LANE, 512), lambda m: (0, 0)), pl.BlockSpec((1, 512), lambda m: (0, 0)),
                  state, state, chunk_f(BRW), chunk_b(BRW)],
        out_specs=[chunk_f(W_G), chunk_b(W_G), pl.BlockSpec((LANE, 512), lambda m: (0, 0)), pl.BlockSpec((SUB, 512), lambda m: (0, 0))],
        out_shape=[sd((T, W_G), f32), sd((T, W_G), f32), sd((LANE, 512), f32), sd((SUB, 512), f32)],
        scratch_shapes=[pltpu.VMEM((BRW, C_KW), f32), pltpu.VMEM((BRW, C_KW), f32)],
        compiler_params=_cparams(("arbitrary",)),
    )(P, P, w2, b2, ssf, ssb, doc, doc)


def _sum_dirs(a, b, tm, name):
    T, W = a.shape

    def body(a_ref, b_ref, o_ref):
        o_ref[...] = (a_ref[...] + b_ref[...]).astype(bf16)

    spec = pl.BlockSpec((tm, W), lambda i: (i, 0))
    return pl.pallas_call(
        body, name=name, grid=(T // tm,), in_specs=[spec, spec], out_specs=spec,
        out_shape=jax.ShapeDtypeStruct((T, W), bf16),
        compiler_params=_cparams(("parallel",)),
    )(a, b)


def _merge_fn(h, m_l, m_c, isctx, ya, ga, yb, gb, of, ob, gc, hh, gd, mg, es, ey, cn, dng, dnb, lg, lb, wbr, wout):
    oc = of + ob
    yc = jnp.concatenate([_rms(oc[:, HD * i:HD * (i + 1)], cn[:, HD * i:HD * (i + 1)]) for i in range(C_HEADS)], 1)
    brs = [ya * _silu(ga), yb * _silu(gb), yc * _silu(gc), _silu(_ln(hh) * dng + dnb) * _silu(gd)]
    acc = None
    for i in range(4):
        t = _sigmoid(mg[:, D * i:D * (i + 1)]) * (mm(brs[i], wbr[i]) + es[i])
        acc = t if acc is None else acc + t
    y = mm(acc, wout) + ey
    gate = jnp.where(isctx, m_c[:, 2 * D:3 * D], m_l[:, 2 * D:3 * D])
    hn = _ln(ALPHA * h + gate * y) * lg + lb
    return hn, (brs, acc)


def _merge_specs(tm):
    t = lambda w, off=0: _tok(tm, w, off)
    return [t(D), pl.BlockSpec((SUB, 3 * D), lambda i: (0, 0)),
            t(BRW), t(BRW, M_GA), t(BRW), t(BRW, M_GB),
            t(BRW), t(BRW),
            t(BRW, M_GC), t(BRW), t(BRW, M_GD), t(4 * D, 0),
            _vec(BRW), _vec(BRW), _vec(BRW), _vec(D), _vec(D),
            pl.BlockSpec((4, BRW, D), lambda i: (0, 0, 0)), pl.BlockSpec((D, D), lambda i: (0, 0))]


def _merge_fwd(h, modv_l, ya, yb, o2, hh, P, cn, dng, dnb, lg, lb, wbr, wout, tc, tm, name):
    T = h.shape[0]

    def body(h_ref, m_ref, ya_ref, ga_ref, yb_ref, gb_ref, of_ref, ob_ref, gc_ref, hh_ref, gd_ref, mg_ref,
             cn_ref, dng_ref, dnb_ref, lg_ref, lb_ref, wbr_ref, wout_ref, o_ref):
        isctx = _row_ids(pl.program_id(0), tm) < tc
        zero = jnp.zeros((tm, D), f32)
        up = lambda r: r[...].astype(f32)
        hn, _ = _merge_fn(h_ref[...], m_ref[0:1, :], m_ref[1:2, :], isctx, ya_ref[...], up(ga_ref), yb_ref[...],
                          up(gb_ref), of_ref[...], ob_ref[...], up(gc_ref), hh_ref[...], up(gd_ref), up(mg_ref),
                          [zero] * 4, zero, cn_ref[...], dng_ref[...], dnb_ref[...], lg_ref[...], lb_ref[...],
                          [wbr_ref[i] for i in range(4)], wout_ref[...])
        o_ref[...] = hn

    return pl.pallas_call(
        body, name=name, grid=(T // tm,),
        in_specs=_merge_specs(tm), out_specs=_tok(tm, D, 0),
        out_shape=jax.ShapeDtypeStruct((T, D), f32),
        compiler_params=_cparams(("parallel",)),
    )(h, modv_l, ya, P, yb, P, o2[0], o2[1], P, hh, P, P, cn, dng, dnb, lg, lb, wbr, wout)


def _merge_bwd(dhn, h, modv_l, ya, yb, o2, hh, P, cn, dng, dnb, lg, lb, wbr, wout, tc, tm, name):
    T = h.shape[0]
    nt = T // tm

    def body(g_ref, h_ref, m_ref, ya_ref, ga_ref, yb_ref, gb_ref, of_ref, ob_ref, gc_ref, hh_ref, gd_ref, mg_ref,
             cn_ref, dng_ref, dnb_ref, lg_ref, lb_ref, wbr_ref, wout_ref,
             dh_ref, dm_ref, dya_ref, dyb_ref, doc_ref, dhh_ref, dp_ref,
             br_ref, z_ref, acc_ref, dy_ref, dv5_ref, dvd_ref):
        isctx = _row_ids(pl.program_id(0), tm) < tc
        zero = jnp.zeros((tm, D), f32)
        wbr_v = [wbr_ref[i] for i in range(4)]
        wout_v = wout_ref[...]
        up = lambda r: r[...].astype(f32)

        def fn(h, ml, mc, ya, ga, yb, gb, oc, gc, hh, gd, mg, e0, e1, e2, e3, ey, cn, dng, dnb, lg, lb):
            return _merge_fn(h, ml, mc, isctx, ya, ga, yb, gb, oc, jnp.zeros_like(oc), gc, hh, gd, mg,
                             [e0, e1, e2, e3], ey, cn, dng, dnb, lg, lb, wbr_v, wout_v)

        _, vjp, (brs, acc) = jax.vjp(
            fn, h_ref[...], m_ref[0:1, :], m_ref[1:2, :], ya_ref[...], up(ga_ref), yb_ref[...], up(gb_ref),
            of_ref[...] + ob_ref[...], up(gc_ref), hh_ref[...], up(gd_ref), up(mg_ref), zero, zero, zero, zero, zero,
            cn_ref[...], dng_ref[...], dnb_ref[...], lg_ref[...], lb_ref[...], has_aux=True)
        (dh, dml, dmc, dya, dga, dyb, dgb, doc, dgc, dhh, dgd, dmg, z0, z1, z2, z3, dy,
         dcn, ddng, ddnb, dlg, dlb) = vjp(g_ref[...])
        dh_ref[...] = dh
        _partial_rows(dm_ref, [dml, dmc])
        dya_ref[...] = dya
        dyb_ref[...] = dyb
        doc_ref[...] = doc
        dhh_ref[...] = dhh
        dp_ref[:, 0:M_GA] = dmg.astype(bf16)
        dp_ref[:, M_GA:M_GB] = dga.astype(bf16)
        dp_ref[:, M_GB:M_GC] = dgb.astype(bf16)
        dp_ref[:, M_GC:M_GD] = dgc.astype(bf16)
        dp_ref[:, M_GD:W_M] = dgd.astype(bf16)
        for i, z in enumerate((z0, z1, z2, z3)):
            br_ref[i] = brs[i].astype(bf16)
            z_ref[i] = z.astype(bf16)
        acc_ref[...] = acc.astype(bf16)
        dy_ref[...] = dy.astype(bf16)
        _partial_rows(dv5_ref, [dcn, ddng, ddnb])
        _partial_rows(dvd_ref, [dlg, dlb])

    t = lambda w: _tok(tm, w, 0)
    part = lambda w: pl.BlockSpec((None, SUB, w), lambda i: (i, 0, 0))
    sd = jax.ShapeDtypeStruct
    return pl.pallas_call(
        body, name=name, grid=(nt,),
        in_specs=[t(D)] + _merge_specs(tm),
        out_specs=[t(D), part(3 * D)] + [t(BRW)] * 4 + [t(W_M),
                   pl.BlockSpec((4, tm, BRW), lambda i: (0, i, 0)), pl.BlockSpec((4, tm, D), lambda i: (0, i, 0)),
                   t(D), t(D), part(BRW), part(D)],
        out_shape=[sd((T, D), f32), sd((nt, SUB, 3 * D), f32)] + [sd((T, BRW), f32)] * 4 + [sd((T, W_M), bf16),
                   sd((4, T, BRW), bf16), sd((4, T, D), bf16), sd((T, D), bf16), sd((T, D), bf16),
                   sd((nt, SUB, BRW), f32), sd((nt, SUB, D), f32)],
        compiler_params=_cparams(("parallel",)),
    )(dhn, h, modv_l, ya, P, yb, P, o2[0], o2[1], P, hh, P, P, cn, dng, dnb, lg, lb, wbr, wout)


def _loss_kernel(h, tgt, tc, tm, name):
    T = h.shape[0]
    nt = T // tm
    nct = tc // tm

    def body(h_ref, t_ref, d_ref, l_ref):
        i = pl.program_id(0)
        err = h_ref[...] - t_ref[...]
        lat = (i >= nct).astype(f32)
        d_ref[...] = err * (lat / D)
        l_ref[...] = jnp.zeros((SUB, LANE), f32) + lat * 0.5 * jnp.sum(err * err) / D

    return pl.pallas_call(
        body, name=name, grid=(nt,),
        in_specs=[pl.BlockSpec((tm, D), lambda i: (i, 0)),
                  pl.BlockSpec((tm, D), lambda i: (jnp.maximum(i - nct, 0), 0))],
        out_specs=[pl.BlockSpec((tm, D), lambda i: (i, 0)), pl.BlockSpec((None, SUB, LANE), lambda i: (i, 0, 0))],
        out_shape=[jax.ShapeDtypeStruct((T, D), f32), jax.ShapeDtypeStruct((nt, SUB, LANE), f32)],
        compiler_params=_cparams(("parallel",)),
    )(h, tgt)


def _rope_tables(tc, tl):
    t = jnp.arange(tl)
    inv = ROPE_THETA ** (-jnp.arange(0, HD // 2, 2, dtype=f32) / (HD // 2))
    ang = jnp.concatenate([(t // GRID_W).astype(f32)[:, None] * inv, (t % GRID_W).astype(f32)[:, None] * inv], -1)
    cos, sin = jnp.repeat(jnp.cos(ang), 2, axis=1), jnp.repeat(jnp.sin(ang), 2, axis=1)
    even = (jnp.arange(HD) % 2 == 0)[None, :]
    cos_f = jnp.concatenate([jnp.ones((tc, HD), f32), cos], 0)
    sin_a = jnp.concatenate([jnp.zeros((tc, HD), f32), jnp.where(even, -sin, 0.0)], 0)
    sin_b = jnp.concatenate([jnp.zeros((tc, HD), f32), jnp.where(even, 0.0, sin)], 0)
    return cos_f, sin_a, sin_b


N_CHIPS = 4
SHARD = N_IN // N_CHIPS


def _group_ranges():
    return dict(M=[(S_MG, 4 * D), (S_GA, BRW), (S_GB, BRW), (S_GC, BRW), (S_GD, BRW)], A=[(S_Q, W_A)],
                C=[(S_B, 3 * BRW), (S_DA, 2 * BRW)], G=[(S_CQ, 2 * C_KW + BRW), (S_R, 2 * C_RANK)])


def _group_weights(w4):
    out = {}
    for k, ranges in _group_ranges().items():
        parts = []
        for a, n in ranges:
            n = LANE if (k, a) == ("G", S_R) else n
            while n > 0:
                s, r = divmod(a, SHARD)
                m = min(n, SHARD - r)
                parts.append(w4[s, r:r + m])
                a, n = a + m, n - m
        out[k] = jnp.concatenate(parts, 0)
    return out


def _ungroup(g):
    secs = []
    for k, ranges in _group_ranges().items():
        off = 0
        for a, n in ranges:
            secs.append((a, g[k][off:off + n]))
            off += n
    return jnp.concatenate([v for _, v in sorted(secs, key=lambda t: t[0])], 0)


PROJ_TN = dict(M=2048, A=1024, C=1280, G=1152)
DU_TK = dict(M=2048, A=1024, C=BRW, G=1152)
DWP_TN = dict(M=768, A=1024, C=BRW, G=1152)


def _gate_weights(w2_l, gb_l):
    w = jnp.zeros((LANE, 2 * C_KW), f32)
    w = w.at[0:C_RANK, 0:C_KW].set(w2_l[0]).at[C_RANK:2 * C_RANK, C_KW:2 * C_KW].set(w2_l[1])
    return w, jnp.concatenate([gb_l[0], gb_l[1]])[None, :]


def _local_step(x1, c1, ctx1, tgt1, c_ctx, b_mod, weights_of, q_norm, k_norm, b_conv, w2, gb, c_norm, d_conv_w,
                d_conv_b, d_norm_g, d_norm_b, grads_done, ln_g, ln_b, tm, token=None):
    tc, tl = ctx1.shape[0], x1.shape[0]
    T = tc + tl
    rc = min(256, tc)
    tmb = tm // 2
    tmm = 768 if T % 768 == 0 else tm
    rope = _rope_tables(tc, tl)
    cin = jnp.concatenate([c1, c_ctx[None, :], jnp.zeros((SUB - 2, D), f32)], 0)
    if token is not None:
        cin = cin + token[:, 0:1]
    row = lambda v: v[None, :]

    h = jnp.concatenate([ctx1, x1], 0)
    saved, wp, w_br, w_out, w_mod, modv = [], *([None] * DEPTH for _ in range(5))
    for l in range(DEPTH):
        wp[l], merge_weights, w_mod[l] = weights_of(l, h)
        modv[l] = _mod_fwd(cin, w_mod[l], b_mod[l], f"mod_fwd{l}")
        u = _ln_fwd(h, modv[l], tc, tm, f"ln_fwd{l}")
        P = {k: _matmul(u, wp[l][k], "nt", tmm, PROJ_TN[k], D, f"proj{l}{k}", out_dtype=bf16) for k in GROUPS}
        qn, kn, vb = _prep_fwd(P["A"], row(q_norm[l]), row(k_norm[l]), rope, tm, f"prep_fwd{l}")
        ya = _attn_fwd(qn, kn, vb, tc, tm, f"attn_fwd{l}")
        yb, hh = _conv_fwd(P["C"], b_conv[l], d_conv_w[l], row(d_conv_b[l]), tc, tl, rc, f"conv_fwd{l}")
        w2p, b2p = _gate_weights(w2[l], gb[l])
        gla = _gla_fwd(P["G"], w2p, b2p, tc, f"gla_fwd{l}")
        o2, ssave = gla[:2], gla[2:]
        w_br[l], w_out[l] = merge_weights(o2[0])
        hn = _merge_fwd(h, modv[l], ya, yb, o2, hh, P["M"], row(c_norm[l]), row(d_norm_g[l]), row(d_norm_b[l]),
                        row(ln_g[l]), row(ln_b[l]), w_br[l], w_out[l], tc, tm, f"merge_fwd{l}")
        saved.append((h, u, P, qn, kn, vb, ya, yb, hh, o2, ssave, w2p, b2p))
        h = hn

    dh, lparts = _loss_kernel(h, tgt1, tc, tm, "loss")
    loss = jnp.sum(lparts[:, 0, 0])

    g = {k: [None] * DEPTH for k in ("wp", "q_norm", "k_norm", "b_conv", "w2", "gb", "c_norm", "d_conv_w", "d_conv_b",
                                     "d_norm_g", "d_norm_b", "w_br", "w_out", "ln_g", "ln_b", "modv")}
    for l in reversed(range(DEPTH)):
        h_in, u, P, qn, kn, vb, ya, yb, hh, o2, ssave, w2p, b2p = saved[l]
        dP = {}
        (dh_res, dm_mg, dya, dyb, doc, dhh, dP["M"], br, z, acc, dy, dv5, dvd) = _merge_bwd(
            dh, h_in, modv[l], ya, yb, o2, hh, P["M"], row(c_norm[l]), row(d_norm_g[l]), row(d_norm_b[l]),
            row(ln_g[l]), row(ln_b[l]), w_br[l], w_out[l], tc, tmb, f"merge_bwd{l}")
        g["w_br"][l] = _matmul_tn_batched(br, z, N_CHIPS, f"dwbr{l}")
        g["w_out"][l] = _matmul(acc, dy, "tn", D, D, T, f"dwout{l}", out_dtype=bf16)
        tk = grads_done(l, {k: g[k][l] for k in ("w_br", "w_out")})
        qg_l = row(q_norm[l]) if tk is None else row(q_norm[l]) + tk[0:1, :]
        v5 = jnp.sum(dv5, 0)
        g["c_norm"][l], g["d_norm_g"][l], g["d_norm_b"][l] = v5[0], v5[1], v5[2]
        vd = jnp.sum(dvd, 0)
        g["ln_g"][l], g["ln_b"][l] = vd[0], vd[1]
        dqn, dkn, dv = _attn_bwd(qn, kn, vb, dya, tc, tm, f"attn_bwd{l}")
        dP["A"], dqk = _prep_bwd(P["A"], dqn, dkn, dv, qg_l, row(k_norm[l]), rope, tm, f"prep_bwd{l}")
        dqk = jnp.sum(dqk, 0)
        g["q_norm"][l], g["k_norm"][l] = dqk[0], dqk[1]
        dP["C"], dwb, dwd, dbd = _conv_bwd(P["C"], dyb, dhh, b_conv[l], d_conv_w[l], tc, tl, rc, f"conv_bwd{l}")
        g["b_conv"][l], g["d_conv_w"][l], g["d_conv_b"][l] = dwb, dwd, dbd[0]
        dpf, dpb, dw2p, db2p = _gla_bwd(P["G"], w2p, b2p, ssave, doc, tc, f"gla_bwd{l}")
        dP["G"] = _sum_dirs(dpf, dpb, tm, f"gla_sum{l}")
        db2p = db2p[0]
        g["w2"][l] = jnp.stack([dw2p[0:C_RANK, 0:C_KW], dw2p[C_RANK:2 * C_RANK, C_KW:2 * C_KW]])
        g["gb"][l] = jnp.stack([db2p[0:C_KW], db2p[C_KW:2 * C_KW]])
        g["wp"][l] = {k: _matmul(dP[k], u, "tn", DWP_TN[k], D, T, f"dwp{l}{k}", out_dtype=bf16) for k in GROUPS}
        tk = grads_done(l, {"wp": g["wp"][l]})
        du = _matmul_groups(dP, wp[l], DU_TK, tmm, f"du{l}", after=tk)
        dh, dm_ln = _ln_bwd(du, h_in, dh_res, modv[l], tc, tm, f"ln_bwd{l}", latent_only=(l == 0))
        g["modv"][l] = jnp.sum(dm_mg, 0) + jnp.sum(dm_ln, 0)

    dmodv = jnp.stack(g.pop("modv"))
    g["w_mod"], dcin = _mod_bwd(cin, w_mod, dmodv)
    g["b_mod"] = dmodv[:, 0, :] + dmodv[:, 1, :]
    g["c_ctx"] = jnp.sum(dcin, (0, 1))[1]
    return loss, dh, g


HALF_TL = 256


TILE_BYTES = 1 << 20


def _row_tile(rows, cols, itemsize=4):
    tr = min(rows, 128)
    while rows % (2 * tr) == 0 and 2 * tr * cols * itemsize <= TILE_BYTES:
        tr *= 2
    return tr


def _adamw(w, g, m, v, name, tr=None, after=None):
    L, R, C = w.shape
    tr = _row_tile(R, C) if tr is None else tr
    if R % tr == 0:
        grid, spec = (L, R // tr), pl.BlockSpec((None, tr, C), lambda l, i: (l, i, 0))
    elif R * C * 4 <= (1 << 20):
        grid, spec = (L, 1), pl.BlockSpec((None, R, C), lambda l, i: (l, 0, 0))
    else:
        grid, spec = (L, C // HALF_TL), pl.BlockSpec((None, R, HALF_TL), lambda l, i: (l, 0, i))

    def body(w_ref, g_ref, m_ref, v_ref, *rest):
        go_ref, d_ref, nm_ref, nv_ref = rest[-4:]
        gg = g_ref[...]
        go_ref[...] = gg
        nm = B1 * m_ref[...] + (1.0 - B1) * gg
        nv = B2 * v_ref[...] + (1.0 - B2) * (gg * gg)
        m_hat = nm / (1.0 - B1 ** STEP)
        v_hat = nv / (1.0 - B2 ** STEP)
        d_ref[...] = -LR * (m_hat / (jnp.sqrt(v_hat) + AEPS) + WD * w_ref[...])
        nm_ref[...] = nm
        nv_ref[...] = nv

    return pl.pallas_call(
        body, name=name, grid=grid, in_specs=[spec] * 4 + ([] if after is None else [pl.BlockSpec(memory_space=pl.ANY)]),
        out_specs=[spec] * 4, out_shape=[jax.ShapeDtypeStruct((L, R, C), f32)] * 4,
        compiler_params=_cparams(("parallel", "parallel")),
    )(w, g, m, v, *([] if after is None else [after]))


MESH = pl.DeviceIdType.MESH
ANY = pl.BlockSpec(memory_space=pl.ANY)


def _place():
    x, y, c = lax.axis_index("x"), lax.axis_index("y"), lax.axis_index("c")
    chips = [(1 - x, y), (x, 1 - y), (1 - x, 1 - y)]
    return x, y, c, chips


def _half(ref, c, axis):
    n = ref.shape[axis] // 2
    last = axis in (-1, ref.ndim - 1)
    idx = [slice(None)] * ref.ndim
    idx[axis] = pl.ds(pl.multiple_of(c * n, LANE if last else SUB), n)
    return ref.at[tuple(idx)]


def _half_shape(shape, axis):
    s = list(shape)
    s[axis] //= 2
    return tuple(s)


def _all_gather(arrs, axes, name):
    n = len(arrs)

    def body(*refs):
        ins, outs = refs[:n], refs[n:2 * n]
        send, recv = refs[2 * n:]
        x, y, c, chips = _place()
        me, sib = 2 * x + y, (x, y, 1 - c)

        def copy(a, k, chip_idx, cc, to, src=None):
            blk = _half(outs[a].at[chip_idx], cc, axes[a])
            return pltpu.make_async_remote_copy(src_ref=blk if src is None else src, dst_ref=blk,
                                                send_sem=send.at[7 * a + k], recv_sem=recv.at[7 * a + k],
                                                device_id=to, device_id_type=MESH)

        own = [pltpu.make_async_remote_copy(src_ref=ins[a], dst_ref=outs[a].at[me], send_sem=send.at[7 * a + 6],
                                            recv_sem=recv.at[7 * a + 6], device_id=sib, device_id_type=MESH)
               for a in range(n)]
        first = own + [copy(a, j, me, c, (*chip, c), src=_half(ins[a], c, axes[a]))
                       for a in range(n) for j, chip in enumerate(chips)]
        for cp in first:
            cp.start()
        passed = []
        for a in range(n):
            for j, chip in enumerate(chips):
                k = 2 * chip[0] + chip[1]
                copy(a, j, k, c, sib).wait_recv()
                fwd = copy(a, 3 + j, k, c, sib)
                fwd.start()
                passed.append(fwd)
        for a in range(n):
            own[a].wait_recv()
            for j, chip in enumerate(chips):
                copy(a, 3 + j, 2 * chip[0] + chip[1], 1 - c, sib).wait_recv()
        for cp in first + passed:
            cp.wait_send()

    return pl.pallas_call(
        body, name=name, in_specs=[ANY] * n, out_specs=[ANY] * n,
        out_shape=[jax.ShapeDtypeStruct((N_CHIPS,) + a.shape, a.dtype) for a in arrs],
        scratch_shapes=[pltpu.SemaphoreType.DMA((7 * n,)), pltpu.SemaphoreType.DMA((7 * n,))],
    )(*arrs)


def _add_half(gfull, land, cidx, axis, name, tr=None, out_dtype=bf16):
    _, hr, hc = land.shape
    if axis == 0:
        tr = min(tr, hr) if tr else _row_tile(hr, hc)
        nb, blk = hr // tr, (None, tr, hc)
        g_spec = pl.BlockSpec(blk, lambda s, i, cr: (s, cr[0] * nb + i, 0))
        l_spec = pl.BlockSpec(blk, lambda s, i, cr: (s, i, 0))
    else:
        nb, blk = hc // HALF_TL, (None, hr, HALF_TL)
        g_spec = pl.BlockSpec(blk, lambda s, i, cr: (s, 0, cr[0] * nb + i))
        l_spec = pl.BlockSpec(blk, lambda s, i, cr: (s, 0, i))

    def body(c_ref, g_ref, l_ref, o_ref):
        o_ref[...] = (g_ref[...].astype(f32) + l_ref[...].astype(f32)).astype(o_ref.dtype)

    return pl.pallas_call(
        body, name=name,
        grid_spec=pltpu.PrefetchScalarGridSpec(
            num_scalar_prefetch=1, grid=(N_CHIPS, nb), in_specs=[g_spec, l_spec], out_specs=l_spec),
        out_shape=jax.ShapeDtypeStruct((N_CHIPS, hr, hc), out_dtype),
        compiler_params=_cparams(("parallel", "parallel")),
    )(cidx, gfull, land)


def _sum_chips(land, own, place, axis, layer, into, name, tr=None):
    _, hr, hc = land.shape
    fresh = not hasattr(into, "dtype")
    shape = tuple(into) if fresh else into.shape
    if axis == 0:
        tr = min(tr, hr) if tr else _row_tile(hr, 4 * hc, 2)
        nb, blk = hr // tr, (tr, hc)
        l_map, m_map = (lambda i, p: (0, i, 0)), (lambda i, p: (p[0], i, 0))
        o_map = lambda i, p: (layer, p[1] * nb + i, 0)
    else:
        nb, blk = hc // HALF_TL, (hr, HALF_TL)
        l_map, m_map = (lambda i, p: (0, 0, i)), (lambda i, p: (p[0], 0, i))
        o_map = lambda i, p: (layer, 0, p[1] * nb + i)

    def body(p_ref, l_ref, o_ref, *rest):
        me = p_ref[0]
        mine = o_ref[...].astype(f32)
        acc = None
        for k in range(N_CHIPS):
            t = jnp.where(me == k, mine, l_ref[k].astype(f32))
            acc = t if acc is None else acc + t
        rest[-1][...] = acc

    return pl.pallas_call(
        body, name=name,
        grid_spec=pltpu.PrefetchScalarGridSpec(
            num_scalar_prefetch=1, grid=(nb,),
            in_specs=[pl.BlockSpec((N_CHIPS,) + blk, l_map), pl.BlockSpec((None,) + blk, m_map)] + ([] if fresh else [ANY]),
            out_specs=pl.BlockSpec((None,) + blk, o_map)),
        out_shape=jax.ShapeDtypeStruct(shape, f32),
        input_output_aliases={} if fresh else {3: 0},
        compiler_params=_cparams(("parallel",)),
    )(place, land, own, *([] if fresh else [into]))


def _sibling_fill(arrs, axes, name):
    n = len(arrs)

    def body(*refs):
        outs = refs[n:2 * n]
        send, recv = refs[2 * n:]
        x, y, c, _ = _place()
        cps = [pltpu.make_async_remote_copy(src_ref=_half(outs[a], c, axes[a] + 1), dst_ref=_half(outs[a], c, axes[a] + 1),
                                            send_sem=send.at[a], recv_sem=recv.at[a], device_id=(x, y, 1 - c),
                                            device_id_type=MESH) for a in range(n)]
        for cp in cps:
            cp.start()
        for a in range(n):
            blk = _half(outs[a], 1 - c, axes[a] + 1)
            pltpu.make_async_remote_copy(src_ref=blk, dst_ref=blk, send_sem=send.at[a], recv_sem=recv.at[a],
                                         device_id=(x, y, 1 - c), device_id_type=MESH).wait_recv()
        for cp in cps:
            cp.wait_send()

    return pl.pallas_call(
        body, name=name, in_specs=[ANY] * n, out_specs=[ANY] * n,
        out_shape=[jax.ShapeDtypeStruct(a.shape, a.dtype) for a in arrs],
        input_output_aliases={a: a for a in range(n)},
        scratch_shapes=[pltpu.SemaphoreType.DMA((n,)), pltpu.SemaphoreType.DMA((n,))],
    )(*arrs)


HBM = pl.BlockSpec(memory_space=pltpu.HBM)
SEM = pl.BlockSpec(memory_space=pltpu.SEMAPHORE)
EFFECT = pltpu.SideEffectType.DATAFLOW_SIDE_EFFECTING
PEERS = 4


def _split_copies(srcs, lands, send, recv, gather, axes=None):
    x, y, c, chips = _place()
    me = 2 * x + y
    if axes is not None:
        out = []
        for a in range(len(srcs)):
            sems = dict(send_sem=send.at[PEERS * a], recv_sem=recv.at[PEERS * a], device_id=(x, y, 1 - c), device_id_type=MESH)
            copy = pltpu.make_async_remote_copy(src_ref=_half(srcs[a], 1 - c, axes[a] + 1), dst_ref=lands[a], **sems)
            out.append((copy, copy))
        return out
    peers = [((*chip, c), 2 * chip[0] + chip[1]) for chip in chips] + ([((x, y, 1 - c), me)] if gather else [])
    out = []
    for a in range(len(srcs)):
        for j, (dev, k) in enumerate(peers):
            src = srcs[a] if gather else srcs[a].at[k]
            sems = dict(send_sem=send.at[PEERS * a + j], recv_sem=recv.at[PEERS * a + j], device_id=dev, device_id_type=MESH)
            out.append((pltpu.make_async_remote_copy(src_ref=src, dst_ref=lands[a].at[me], **sems),
                        pltpu.make_async_remote_copy(src_ref=src, dst_ref=lands[a].at[k], **sems)))
    return out


def _split_start(srcs, gather, after, name, axes=None):
    n = len(srcs)
    if axes is not None:
        lands = [lax.empty(_half_shape(s.shape, axes[a] + 1), s.dtype) for a, s in enumerate(srcs)]
    else:
        lands = [lax.empty(((N_CHIPS,) + s.shape) if gather else s.shape, s.dtype) for s in srcs]

    def body(*refs):
        send, recv = refs[2 * n + 1], refs[2 * n + 2]
        for start, _ in _split_copies(refs[:n], refs[n:2 * n], send, recv, gather, axes):
            start.start()
        refs[-1][...] = jnp.zeros_like(refs[-1])

    sems = pltpu.SemaphoreType.DMA((PEERS * n,))
    hbm = lambda a: pltpu.with_memory_space_constraint(a, pltpu.HBM)
    out = pl.pallas_call(
        body, name=name,
        out_shape=(sems, sems, *[pltpu.HBM(a.shape, a.dtype) for a in srcs + lands], jax.ShapeDtypeStruct((SUB, LANE), f32)),
        in_specs=[HBM] * (2 * n) + [ANY], out_specs=(SEM, SEM, *[HBM] * (2 * n), pl.BlockSpec(memory_space=pltpu.VMEM)),
        input_output_aliases={i: 2 + i for i in range(2 * n)},
        compiler_params=pltpu.CompilerParams(has_side_effects=EFFECT),
    )(*[hbm(a) for a in srcs + lands], after)
    return out[0], out[1], list(out[2:2 + n]), list(out[2 + n:2 + 2 * n]), out[-1]


def _split_wait(send, recv, srcs, lands, gather, after, name, axes=None):
    n = len(srcs)

    def body(*refs):
        for start, arrival in _split_copies(refs[:n], refs[n:2 * n], refs[2 * n], refs[2 * n + 1], gather, axes):
            start.wait_send()
            arrival.wait_recv()

    out = pl.pallas_call(
        body, name=name, out_shape=[pltpu.HBM(a.shape, a.dtype) for a in srcs + lands],
        in_specs=[HBM] * (2 * n) + [SEM, SEM, ANY], out_specs=[HBM] * (2 * n),
        input_output_aliases={i: i for i in range(2 * n)},
        compiler_params=pltpu.CompilerParams(has_side_effects=EFFECT),
    )(*srcs, *lands, send, recv, after)
    return list(out[:n]), list(out[n:])


N_DEV = 8


def _all_reduce_small(v, name, after):
    R = v.shape[0]

    def body(v_ref, after_ref, o_ref, land_ref, send, recv):
        x, y, c, _ = _place()
        me = 4 * x + 2 * y + c
        land_ref[me] = v_ref[...]
        cps = []
        for m in range(1, N_DEV):
            px, py, pc = [(1 - q) if (m >> s) & 1 else q for q, s in ((x, 2), (y, 1), (c, 0))]
            cps.append((pltpu.make_async_remote_copy(src_ref=v_ref, dst_ref=land_ref.at[me], send_sem=send.at[m - 1],
                                                     recv_sem=recv.at[m - 1], device_id=(px, py, pc), device_id_type=MESH),
                        4 * px + 2 * py + pc, m))
        for cp, *_ in cps:
            cp.start()
        for cp, peer, m in cps:
            pltpu.make_async_remote_copy(src_ref=v_ref, dst_ref=land_ref.at[peer], send_sem=send.at[m - 1],
                                         recv_sem=recv.at[m - 1], device_id=(x, y, c), device_id_type=MESH).wait_recv()
        for cp, *_ in cps:
            cp.wait_send()
        acc = land_ref[0]
        for k in range(1, N_DEV):
            acc = acc + land_ref[k]
        o_ref[...] = acc

    vm = pl.BlockSpec(memory_space=pltpu.VMEM)
    return pl.pallas_call(
        body, name=name, in_specs=[vm, ANY], out_specs=vm, out_shape=jax.ShapeDtypeStruct(v.shape, f32),
        scratch_shapes=[pltpu.VMEM((N_DEV, R, LANE), f32), pltpu.SemaphoreType.DMA((N_DEV - 1,)),
                        pltpu.SemaphoreType.DMA((N_DEV - 1,))],
        compiler_params=pltpu.CompilerParams(vmem_limit_bytes=VMEM_LIMIT),
    )(v, after)


def _pack_small(arrs, mult=2 * SUB):
    flat = jnp.concatenate([a.reshape(-1) for a in arrs])
    rows = -(-flat.shape[0] // (LANE * mult)) * mult
    return jnp.pad(flat, (0, rows * LANE - flat.shape[0])).reshape(rows, LANE)


def _unpack_small(vec, shapes):
    flat, out, o = vec.reshape(-1), [], 0
    for s in shapes:
        n = int(np.prod(s))
        out.append(flat[o:o + n].reshape(s))
        o += n
    return out


REPL_SMALL = ("c_ctx", "b_mod", "q_norm", "k_norm", "c_norm", "d_conv_b", "d_norm_g", "d_norm_b", "ln_g", "ln_b")
SHARD_SMALL = ("b_conv", "c_gate_w2", "c_gate_b", "d_conv_w")
BIG = ("w_mod", "w_in", "w_br", "w_out")
ORDER = ("c_ctx", "w_mod", "b_mod", "w_in", "q_norm", "k_norm", "b_conv", "c_gate_w2", "c_gate_b", "c_norm", "d_conv_w",
         "d_conv_b", "d_norm_g", "d_norm_b", "w_br", "w_out", "ln_g", "ln_b")


def kernel(x, c, ctx, c_ctx, w_mod, b_mod, w_in, q_norm, k_norm, b_conv, c_gate_w2, c_gate_b, c_norm, d_conv_w, d_conv_b, d_norm_g, d_norm_b, w_br, w_out, ln_g, ln_b, loss_target, m_c_ctx, m_w_mod, m_b_mod, m_w_in, m_q_norm, m_k_norm, m_b_conv, m_c_gate_w2, m_c_gate_b, m_c_norm, m_d_conv_w, m_d_conv_b, m_d_norm_g, m_d_norm_b, m_w_br, m_w_out, m_ln_g, m_ln_b, v_c_ctx, v_w_mod, v_b_mod, v_w_in, v_q_norm, v_k_norm, v_b_conv, v_c_gate_w2, v_c_gate_b, v_c_norm, v_d_conv_w, v_d_conv_b, v_d_norm_g, v_d_norm_b, v_w_br, v_w_out, v_ln_g, v_ln_b):
    W = dict(c_ctx=c_ctx, w_mod=w_mod, b_mod=b_mod, w_in=w_in, q_norm=q_norm, k_norm=k_norm, b_conv=b_conv,
             c_gate_w2=c_gate_w2, c_gate_b=c_gate_b, c_norm=c_norm, d_conv_w=d_conv_w, d_conv_b=d_conv_b,
             d_norm_g=d_norm_g, d_norm_b=d_norm_b, w_br=w_br, w_out=w_out, ln_g=ln_g, ln_b=ln_b)
    M = dict(c_ctx=m_c_ctx, w_mod=m_w_mod, b_mod=m_b_mod, w_in=m_w_in, q_norm=m_q_norm, k_norm=m_k_norm, b_conv=m_b_conv,
             c_gate_w2=m_c_gate_w2, c_gate_b=m_c_gate_b, c_norm=m_c_norm, d_conv_w=m_d_conv_w, d_conv_b=m_d_conv_b,
             d_norm_g=m_d_norm_g, d_norm_b=m_d_norm_b, w_br=m_w_br, w_out=m_w_out, ln_g=m_ln_g, ln_b=m_ln_b)
    V = dict(c_ctx=v_c_ctx, w_mod=v_w_mod, b_mod=v_b_mod, w_in=v_w_in, q_norm=v_q_norm, k_norm=v_k_norm, b_conv=v_b_conv,
             c_gate_w2=v_c_gate_w2, c_gate_b=v_c_gate_b, c_norm=v_c_norm, d_conv_w=v_d_conv_w, d_conv_b=v_d_conv_b,
             d_norm_g=v_d_norm_g, d_norm_b=v_d_norm_b, w_br=v_w_br, w_out=v_w_out, ln_g=v_ln_g, ln_b=v_ln_b)
    chip = 2 * lax.axis_index("x") + lax.axis_index("y")
    cidx = lax.axis_index("c").astype(jnp.int32).reshape(1)

    place = jnp.stack([chip, lax.axis_index("c")]).astype(jnp.int32)

    AXIS = dict(w_in=1, w_mod=0, w_br=0, w_out=0)
    ex = dict(w_in=lambda a: jnp.swapaxes(a, 1, 2), w_mod=lambda a: a.reshape(1, DEPTH * D, -1),
              w_br=lambda a: a.reshape(DEPTH, 4 * BRW, -1), w_out=lambda a: a)
    Wx, Mx, Vx = ({k: ex[k](P_[k]) for k in BIG} for P_ in (W, M, V))

    LAYER, MERGE = ("w_in", "w_br", "w_out"), ("w_br", "w_out")
    small_shard = _pack_small([W[k] for k in SHARD_SMALL])
    keys0 = ("w_in", "w_mod")
    sent = lambda k, l: (w_mod[l] if k == "w_mod" else Wx[k][l]).astype(bf16)
    got = _all_gather([sent(k, 0) for k in keys0] + [small_shard], [AXIS[k] for k in keys0] + [0], "all_gather0")
    smalls = [_unpack_small(got[-1][s], [W[k].shape for k in SHARD_SMALL]) for s in range(N_CHIPS)]
    full = {k: jnp.concatenate([smalls[s][i] for s in range(N_CHIPS)], axis=-1) for i, k in enumerate(SHARD_SMALL)}
    ag0b = _split_start([sent(k, 0) for k in MERGE], True, got[0], "all_gather0b_start")
    ag1 = _split_start([sent(k, 1) for k in keys0], True, ag0b[4], "all_gather1_start")
    ag1b = _split_start([sent(k, 1) for k in MERGE], True, ag1[4], "all_gather1b_start")

    def merge_form(w_br4, w_out4):
        return jnp.moveaxis(w_br4.reshape(N_CHIPS, 4, BRW, D // N_CHIPS), 0, 2).reshape(4, BRW, D), w_out4.reshape(D, D)

    def weights_of(l, h):
        first = got if l == 0 else _split_wait(*ag1[:4], True, h, "all_gather1_wait")[1]
        flight = (ag0b, ag1b)[l]
        return (_group_weights(first[0]),
                lambda after: merge_form(*_split_wait(*flight[:4], True, after, f"all_gather{l}b_wait")[1]), first[1])

    red = {k: Wx[k].shape for k in BIG}
    flights, held = {}, {}

    def to_sibling(tag, l, pieces):
        keys = list(pieces)
        halves = _split_start([pieces[k] for k in keys], False, jnp.zeros((SUB, LANE), f32), f"rs_sibling_halves{tag}_start",
                              axes=[AXIS[k] for k in keys])
        flights[tag] = (l, keys, halves)
        return halves[4]

    def launch(tag, after):
        l, keys, halves = flights[tag]
        axes = [AXIS[k] for k in keys]
        pieces, land_a = _split_wait(*halves[:4], False, after, f"rs_sibling_halves{tag}_wait", axes=axes)
        pair = [_add_half(p, la, cidx, ax, f"rs_pair_sum{tag}_{k}") for k, p, la, ax in zip(keys, pieces, land_a, axes)]
        flights[tag] = (l, keys, _split_start(pair, False, jnp.zeros((SUB, LANE), f32), f"rs_chip_exchange{tag}_start"))
        return flights[tag][2][4]

    def land(tag, after):
        l, keys, flight = flights.pop(tag)
        pair, land_b = _split_wait(*flight[:4], False, after, f"rs_chip_exchange{tag}_wait")
        for k, lb, pr in zip(keys, land_b, pair):
            red[k] = _sum_chips(lb, pr, place, AXIS[k], l, red[k], f"rs_chip_sum{tag}_{k}")

    def grads_done(l, gl):
        if "wp" in gl:
            pieces = dict(w_in=_ungroup(gl["wp"]).reshape(N_CHIPS, SHARD, D))
            if l == 0:
                return launch("0b", gl["wp"]["A"]) + to_sibling("0c", 0, pieces)
            return to_sibling("1", 1, {**pieces, **held.pop(1)})
        pieces = dict(w_br=gl["w_br"].reshape(N_CHIPS, 4 * BRW, D // N_CHIPS), w_out=gl["w_out"].reshape(N_CHIPS, D // N_CHIPS, D))
        if l == 0:
            return launch("1", gl["w_out"]) + to_sibling("0b", 0, pieces)
        held[1] = pieces
        return None

    loss, gx, g = _local_step(
        x[0], c, ctx[0], loss_target[0], c_ctx, b_mod, weights_of, q_norm, k_norm, full["b_conv"],
        full["c_gate_w2"], full["c_gate_b"], c_norm, full["d_conv_w"], d_conv_b, d_norm_g, d_norm_b,
        grads_done, ln_g, ln_b, tm=256, token=ag1b[4])
    g["c_gate_w2"], g["c_gate_b"] = g.pop("w2"), g.pop("gb")
    loss = lax.psum(loss, ("x", "y", "c"))

    w_mod_pieces = g["w_mod"].reshape(N_CHIPS, DEPTH * D, 3 * D // N_CHIPS)
    g = {k: (jnp.stack(v) if isinstance(v, list) else v) for k, v in g.items() if k not in ("wp", "w_br", "w_out", "w_mod")}

    small_names = REPL_SMALL + SHARD_SMALL
    gs = _all_reduce_small(_pack_small([g[k] for k in small_names]), "all_reduce_small",
                           launch("0c", gx) + to_sibling("0d", 0, {"w_mod": w_mod_pieces}))
    gsm = dict(zip(small_names, _unpack_small(gs, [g[k].shape for k in small_names])))
    for k in SHARD_SMALL:
        wdt = W[k].shape[-1]
        gsm[k] = lax.dynamic_slice_in_dim(gsm[k], chip * wdt, wdt, axis=gsm[k].ndim - 1)

    grad, delta, new_m, new_v = {}, {}, {}, {}

    def adamw_big(keys, after):
        filled = _sibling_fill([red[k] for k in keys], [AXIS[k] for k in keys], "rs_sibling_fill_" + keys[0])
        for k, r in zip(keys, filled):
            back = (lambda a: jnp.swapaxes(a, 1, 2)) if k == "w_in" else (lambda a: a.reshape(W[k].shape))
            g_, d_, m_, v_ = _adamw(Wx[k], r, Mx[k], Vx[k], f"adamw_{k}", after=after)
            grad[k], delta[k], new_m[k], new_v[k] = back(g_), back(d_), back(m_), back(v_)
        return d_

    token = launch("0d", gs)
    land("1", gx)
    land("0b", gx)
    last = adamw_big(MERGE, token)
    shapes = [W[k].shape for k in small_names]
    _, d_, m_, v_ = _adamw(*[_pack_small([P_[k] for k in small_names])[None] for P_ in (W, gsm, M, V)], "adamw_small", after=last)
    for k, dd, mm_, vv in zip(small_names, _unpack_small(d_, shapes), _unpack_small(m_, shapes), _unpack_small(v_, shapes)):
        grad[k], delta[k], new_m[k], new_v[k] = gsm[k], dd, mm_, vv
    land("0c", d_)
    last = adamw_big(("w_in",), None)
    land("0d", last)
    adamw_big(("w_mod",), None)

    return (loss, gx[None], *[grad[k] for k in ORDER], *[delta[k] for k in ORDER], *[new_m[k] for k in ORDER],
            *[new_v[k] for k in ORDER])
```

```python
import functools

import jax
import jax.numpy as jnp
import numpy as np
from jax import lax
from jax.experimental import pallas as pl
from jax.experimental.pallas import tpu as pltpu

f32 = jnp.float32
bf16 = jnp.bfloat16

D = 1024
DEPTH = 2
GRID_W = 64
BRW = 512
HD = 128
A_HEADS = 4
C_HEADS = 4
C_KW = 256
C_RANK = 16
C_TAU = 16.0
CH = 128
KB = 3
KD = 31
ALPHA = (2 * DEPTH) ** 0.25
EPS = 1e-6
ROPE_THETA = 10000.0
N_IN = 10784
LR, B1, B2, AEPS, WD, STEP = 0.001, 0.9, 0.999, 1e-08, 0.01, 10

W_M, W_A, W_C, W_G = 4 * D + 4 * BRW, 1024, 5 * BRW, 1152
GROUPS = ("M", "A", "C", "G")
M_GA, M_GB, M_GC, M_GD = 4 * D, 4 * D + BRW, 4 * D + 2 * BRW, 4 * D + 3 * BRW
A_K, A_V = 512, 768
G_K, G_V, G_R = 256, 512, 1024
S_Q, S_GA, S_B, S_C, S_X, S_GB, S_CQ, S_CV, S_GC, S_R, S_DA, S_DG, S_GD, S_MG = (
    0, 1024, 1536, 2048, 2560, 3072, 3584, 4096, 4608, 5120, 5152, 5664, 6176, 6688)

LANE = 128
SUB = 8
VMEM_LIMIT = 56 * 1024 * 1024
CONV_PAD = 16
GLA_SUB = 16
GLA_CLAMP = 60.0


def _cparams(sem, vmem=VMEM_LIMIT):
    return pltpu.CompilerParams(dimension_semantics=sem, vmem_limit_bytes=vmem)


def _dg(a, b, ca, cb):
    return lax.dot_general(a.astype(bf16), b.astype(bf16), (((ca,), (cb,)), ((), ())),
                           preferred_element_type=f32)


@jax.custom_vjp
def mm(a, b):
    return _dg(a, b, 1, 0)


mm.defvjp(lambda a, b: (_dg(a, b, 1, 0), (a, b)),
          lambda r, ct: (_dg(ct, r[1], 1, 1).astype(r[0].dtype), _dg(r[0], ct, 0, 0).astype(r[1].dtype)))


@jax.custom_vjp
def mm_nt(a, b):
    return _dg(a, b, 1, 1)


mm_nt.defvjp(lambda a, b: (_dg(a, b, 1, 1), (a, b)),
             lambda r, ct: (_dg(ct, r[1], 1, 0).astype(r[0].dtype), _dg(ct, r[0], 0, 0).astype(r[1].dtype)))


@jax.custom_vjp
def mm_tn(a, b):
    return _dg(a, b, 0, 0)


mm_tn.defvjp(lambda a, b: (_dg(a, b, 0, 0), (a, b)),
             lambda r, ct: (_dg(r[1], ct, 1, 1).astype(r[0].dtype), _dg(r[0], ct, 1, 0).astype(r[1].dtype)))


@jax.custom_vjp
def _sigmoid(x):
    return 0.5 * jnp.tanh(0.5 * x) + 0.5


def _sigmoid_fwd(x):
    s = _sigmoid(x)
    return s, s


_sigmoid.defvjp(_sigmoid_fwd, lambda s, ct: (ct * (s - s * s),))


@jax.custom_vjp
def _silu(x):
    return x * _sigmoid(x)


def _silu_fwd(x):
    s = _sigmoid(x)
    return x * s, (x, s)


_silu.defvjp(_silu_fwd, lambda r, ct: (ct * (r[1] + r[0] * (r[1] - r[1] * r[1])),))


def _ln(x):
    mu = jnp.mean(x, -1, keepdims=True)
    xc = x - mu
    var = jnp.mean(xc * xc, -1, keepdims=True)
    return xc * lax.rsqrt(var + EPS)


def _rms(x, g):
    return x * lax.rsqrt(jnp.mean(x * x, -1, keepdims=True) + EPS) * g


@jax.custom_vjp
def _rope(x, cos_f, sin_a, sin_b):
    return x * cos_f + pltpu.roll(x, HD - 1, 1) * sin_a + pltpu.roll(x, 1, 1) * sin_b


def _rope_fwd(x, cos_f, sin_a, sin_b):
    return _rope(x, cos_f, sin_a, sin_b), (cos_f, sin_a, sin_b)


def _rope_bwd(r, ct):
    cos_f, sin_a, sin_b = r
    dx = ct * cos_f + pltpu.roll(ct * sin_a, 1, 1) + pltpu.roll(ct * sin_b, HD - 1, 1)
    return dx, jnp.zeros_like(cos_f), jnp.zeros_like(sin_a), jnp.zeros_like(sin_b)


_rope.defvjp(_rope_fwd, _rope_bwd)


def _row_ids(i, tm):
    return i * tm + lax.broadcasted_iota(jnp.int32, (tm, 1), 0)


def _partial_rows(ref, rows):
    n = len(rows)
    for k, r in enumerate(rows):
        ref[k:k + 1, :] = r
    ref[n:SUB, :] = jnp.zeros((SUB - n, ref.shape[-1]), f32)


def _matmul(a, b, mode, tm, tn, tk, name, out_dtype=f32, add=None, after=None):
    sect = a.ndim == 3
    a2 = (a.shape[1], a.shape[0] * a.shape[2]) if sect else a.shape
    if mode == "nn":
        (M, K), N = a2, b.shape[1]
        a_spec = pl.BlockSpec((None, tm, tk), lambda j, i, k: (k, i, 0)) if sect else pl.BlockSpec((tm, tk), lambda j, i, k: (i, k))
        b_spec = pl.BlockSpec((tk, tn), lambda j, i, k: (k, j))
        ca, cb = 1, 0
        assert not sect or tk == a.shape[2]
    elif mode == "nt":
        (M, K), N = a2, b.shape[0]
        assert not sect
        a_spec = pl.BlockSpec((tm, tk), lambda j, i, k: (i, k))
        b_spec = pl.BlockSpec((tn, tk), lambda j, i, k: (j, k))
        ca, cb = 1, 1
    else:
        (K, M), N = a2, b.shape[1]
        a_spec = pl.BlockSpec((None, tk, tm), lambda j, i, k: (i, k, 0)) if sect else pl.BlockSpec((tk, tm), lambda j, i, k: (k, i))
        b_spec = pl.BlockSpec((tk, tn), lambda j, i, k: (k, j))
        ca, cb = 0, 0
        assert not sect or tm == a.shape[2]
    assert M % tm == 0 and N % tn == 0 and K % tk == 0, (name, M, N, K, tm, tn, tk)
    nk = K // tk

    o_spec = pl.BlockSpec((tm, tn), lambda j, i, k: (i, j))

    def body(a_ref, b_ref, *rest):
        add_ref = rest[0] if add is not None else None
        o_ref, acc_ref = rest[-2:]
        k = pl.program_id(2)
        part = _dg(a_ref[...], b_ref[...], ca, cb)

        @pl.when(k == 0)
        def _():
            acc_ref[...] = part if add_ref is None else part + add_ref[...]

        @pl.when(k > 0)
        def _():
            acc_ref[...] += part

        @pl.when(k == nk - 1)
        def _():
            o_ref[...] = acc_ref[...].astype(o_ref.dtype)

    extra = ([] if add is None else [(o_spec, add)]) + ([] if after is None else [(pl.BlockSpec(memory_space=pl.ANY), after)])
    return pl.pallas_call(
        body, name=name, grid=(N // tn, M // tm, nk),
        in_specs=[a_spec, b_spec] + [s_ for s_, _ in extra], out_specs=o_spec,
        out_shape=jax.ShapeDtypeStruct((M, N), out_dtype),
        scratch_shapes=[pltpu.VMEM((tm, tn), f32)],
        compiler_params=_cparams(("parallel", "parallel", "arbitrary")),
    )(a, b, *[v_ for _, v_ in extra])


def _matmul_groups(a, b, tks, tm, name, after=None):
    keys = list(a)
    M = a[keys[0]].shape[-2]
    N = b[keys[0]].shape[1]
    count = {g: b[g].shape[0] // tks[g] for g in keys}
    first, total = {}, 0
    for g in keys:
        first[g], total = total, total + count[g]

    def k_of(g):
        return lambda s: jnp.clip(s - first[g], 0, count[g] - 1)

    a_specs = [pl.BlockSpec((None, tm, tks[g]), functools.partial(lambda i, s, kk: (kk(s), i, 0), kk=k_of(g)))
               if a[g].ndim == 3 else pl.BlockSpec((tm, tks[g]), functools.partial(lambda i, s, kk: (i, kk(s)), kk=k_of(g)))
               for g in keys]
    b_specs = [pl.BlockSpec((tks[g], N), functools.partial(lambda i, s, kk: (kk(s), 0), kk=k_of(g))) for g in keys]
    n = len(keys)

    def body(*refs):
        o_ref, acc_ref = refs[-2:]
        s = pl.program_id(1)

        @pl.when(s == 0)
        def _():
            acc_ref[...] = jnp.zeros_like(acc_ref)

        for j, g in enumerate(keys):
            @pl.when((s >= first[g]) & (s < first[g] + count[g]))
            def _(j=j):
                acc_ref[...] += _dg(refs[j][...], refs[n + j][...], 1, 0)

        @pl.when(s == total - 1)
        def _():
            o_ref[...] = acc_ref[...]

    extra = [] if after is None else [after]
    return pl.pallas_call(
        body, name=name, grid=(M // tm, total),
        in_specs=a_specs + b_specs + [pl.BlockSpec(memory_space=pl.ANY)] * len(extra),
        out_specs=pl.BlockSpec((tm, N), lambda i, s: (i, 0)),
        out_shape=jax.ShapeDtypeStruct((M, N), f32),
        scratch_shapes=[pltpu.VMEM((tm, N), f32)],
        compiler_params=_cparams(("parallel", "arbitrary")),
    )(*[a[g] for g in keys], *[b[g] for g in keys], *extra)


def _matmul_tn_batched(a, b, ns, name):
    B, K, M = a.shape
    N = b.shape[2] // ns

    def body(a_ref, b_ref, o_ref):
        o_ref[...] = _dg(a_ref[...], b_ref[...], 0, 0).astype(bf16)

    return pl.pallas_call(
        body, name=name, grid=(B, ns),
        in_specs=[pl.BlockSpec((None, K, M), lambda i, s: (i, 0, 0)), pl.BlockSpec((None, K, N), lambda i, s: (i, 0, s))],
        out_specs=pl.BlockSpec((None, None, M, N), lambda i, s: (s, i, 0, 0)),
        out_shape=jax.ShapeDtypeStruct((ns, B, M, N), bf16),
        compiler_params=_cparams(("parallel", "parallel")),
    )(a, b)


MOD_TN = 768


def _mod_fwd(cin, w_mod_l, b_mod_l, name):
    def body(c_ref, w_ref, b_ref, o_ref):
        o_ref[...] = mm(_silu(c_ref[...]), w_ref[...]) + b_ref[...]

    return pl.pallas_call(
        body, name=name, grid=(3 * D // MOD_TN,),
        in_specs=[pl.BlockSpec((SUB, D), lambda j: (0, 0)), pl.BlockSpec((None, D, MOD_TN), lambda j: (j, 0, 0)),
                  pl.BlockSpec((1, MOD_TN), lambda j: (0, j))],
        out_specs=pl.BlockSpec((SUB, MOD_TN), lambda j: (0, j)),
        out_shape=jax.ShapeDtypeStruct((SUB, 3 * D), f32),
        compiler_params=_cparams(("parallel",)),
    )(cin, w_mod_l, b_mod_l[None, :])


def _mod_bwd(cin, w_mods, dmodv):
    nj = 3 * D // MOD_TN

    def body(c_ref, *refs):
        g_ref, dw_ref, dc_ref = refs[DEPTH:]
        w = refs[0][...]
        for l in range(1, DEPTH):
            w = jnp.where(pl.program_id(0) == l, refs[l][...], w)
        _, vjp = jax.vjp(lambda c, w: mm(_silu(c), w), c_ref[...], w.astype(f32))
        dc, dw = vjp(g_ref[...])
        dw_ref[...] = dw.astype(bf16)
        dc_ref[...] = dc

    return pl.pallas_call(
        body, name="mod_bwd", grid=(DEPTH, nj),
        in_specs=[pl.BlockSpec((SUB, D), lambda l, j: (0, 0))]
        + [pl.BlockSpec((None, D, MOD_TN), lambda l, j: (j, 0, 0))] * DEPTH
        + [pl.BlockSpec((None, SUB, MOD_TN), lambda l, j: (l, 0, j))],
        out_specs=[pl.BlockSpec((None, None, D, MOD_TN), lambda l, j: (j, l, 0, 0)),
                   pl.BlockSpec((None, None, SUB, D), lambda l, j: (l, j, 0, 0))],
        out_shape=[jax.ShapeDtypeStruct((nj, DEPTH, D, MOD_TN), bf16),
                   jax.ShapeDtypeStruct((DEPTH, nj, SUB, D), f32)],
        compiler_params=_cparams(("parallel", "parallel")),
    )(cin, *w_mods, dmodv)


def _u_fn(h, m_l, m_c, isctx):
    n = _ln(h)
    shift = jnp.where(isctx, m_c[:, 0:D], m_l[:, 0:D])
    scale = jnp.where(isctx, m_c[:, D:2 * D], m_l[:, D:2 * D])
    return n * (1.0 + scale) + shift


def _ln_fwd(h, modv_l, tc, tm, name):
    T = h.shape[0]

    def body(h_ref, m_ref, u_ref):
        isctx = _row_ids(pl.program_id(0), tm) < tc
        u_ref[...] = _u_fn(h_ref[...], m_ref[0:1, :], m_ref[1:2, :], isctx).astype(bf16)

    return pl.pallas_call(
        body, name=name, grid=(T // tm,),
        in_specs=[pl.BlockSpec((tm, D), lambda i: (i, 0)), pl.BlockSpec((SUB, 3 * D), lambda i: (0, 0))],
        out_specs=pl.BlockSpec((tm, D), lambda i: (i, 0)),
        out_shape=jax.ShapeDtypeStruct((T, D), bf16),
        compiler_params=_cparams(("parallel",)),
    )(h, modv_l)


def _ln_bwd(du, h, dh_res, modv_l, tc, tm, name, latent_only=False):
    T = h.shape[0]
    nt, nct = T // tm, tc // tm

    def body(du_ref, h_ref, r_ref, m_ref, dh_ref, dm_ref):
        isctx = _row_ids(pl.program_id(0), tm) < tc
        _, vjp = jax.vjp(lambda h, ml, mc: _u_fn(h, ml, mc, isctx), h_ref[...], m_ref[0:1, :], m_ref[1:2, :])
        dh, dml, dmc = vjp(du_ref[...])
        dh_ref[...] = dh + r_ref[...]
        _partial_rows(dm_ref, [dml, dmc])

    dh_map = (lambda i: (jnp.maximum(i - nct, 0), 0)) if latent_only else (lambda i: (i, 0))
    return pl.pallas_call(
        body, name=name, grid=(nt,),
        in_specs=[pl.BlockSpec((tm, D), lambda i: (i, 0)), pl.BlockSpec((tm, D), lambda i: (i, 0)),
                  pl.BlockSpec((tm, D), lambda i: (i, 0)), pl.BlockSpec((SUB, 3 * D), lambda i: (0, 0))],
        out_specs=[pl.BlockSpec((tm, D), dh_map), pl.BlockSpec((None, SUB, 3 * D), lambda i: (i, 0, 0))],
        out_shape=[jax.ShapeDtypeStruct((T - tc if latent_only else T, D), f32), jax.ShapeDtypeStruct((nt, SUB, 3 * D), f32)],
        compiler_params=_cparams(("arbitrary",)),
    )(du, h, dh_res, modv_l)


def _prep_fn(q, k, qg, kg, cos_f, sin_a, sin_b):
    qs = [_rope(_rms(q[:, HD * i:HD * (i + 1)], qg), cos_f, sin_a, sin_b) * (HD ** -0.5) for i in range(A_HEADS)]
    ks = [_rope(_rms(k[:, HD * i:HD * (i + 1)], kg), cos_f, sin_a, sin_b) for i in range(A_HEADS // 2)]
    return jnp.concatenate(qs, 1), jnp.concatenate(ks, 1)


def _tok(tm, w, off):
    return pl.BlockSpec((tm, w), lambda i: (i, off // w))


def _vec(w):
    return pl.BlockSpec((1, w), lambda i: (0, 0))


def _prep_fwd(P, qg, kg, rope, tm, name):
    T = P.shape[0]

    def body(q_ref, k_ref, v_ref, qg_ref, kg_ref, c_ref, sa_ref, sb_ref, qn_ref, kn_ref, vb_ref):
        qn, kn = _prep_fn(q_ref[...].astype(f32), k_ref[...].astype(f32), qg_ref[...], kg_ref[...], c_ref[...], sa_ref[...],
                          sb_ref[...])
        qn_ref[...] = qn.astype(bf16)
        kn_ref[...] = kn.astype(bf16)
        vb_ref[...] = v_ref[...].astype(bf16)

    return pl.pallas_call(
        body, name=name, grid=(T // tm,),
        in_specs=[_tok(tm, 512, 0), _tok(tm, 256, A_K), _tok(tm, 256, A_V), _vec(HD), _vec(HD),
                  _tok(tm, HD, 0), _tok(tm, HD, 0), _tok(tm, HD, 0)],
        out_specs=[_tok(tm, 512, 0), _tok(tm, 256, 0), _tok(tm, 256, 0)],
        out_shape=[jax.ShapeDtypeStruct((T, 512), bf16), jax.ShapeDtypeStruct((T, 256), bf16),
                   jax.ShapeDtypeStruct((T, 256), bf16)],
        compiler_params=_cparams(("parallel",)),
    )(P, P, P, qg, kg, *rope)


def _prep_bwd(P, dqn, dkn, dv, qg, kg, rope, tm, name):
    T = P.shape[0]
    nt = T // tm

    def body(q_ref, k_ref, dq_ref, dk_ref, dv_ref, qg_ref, kg_ref, c_ref, sa_ref, sb_ref, o_ref, og_ref):
        tabs = (c_ref[...], sa_ref[...], sb_ref[...])
        _, vjp = jax.vjp(lambda q, k, a, b: _prep_fn(q, k, a, b, *tabs), q_ref[...].astype(f32), k_ref[...].astype(f32),
                         qg_ref[...], kg_ref[...])
        dq, dk, dqg, dkg = vjp((dq_ref[...], dk_ref[...]))
        o_ref[:, 0:A_K] = dq.astype(bf16)
        o_ref[:, A_K:A_V] = dk.astype(bf16)
        o_ref[:, A_V:W_A] = dv_ref[...].astype(bf16)
        _partial_rows(og_ref, [dqg, dkg])

    return pl.pallas_call(
        body, name=name, grid=(nt,),
        in_specs=[_tok(tm, 512, 0), _tok(tm, 256, A_K), _tok(tm, 512, 0), _tok(tm, 256, 0), _tok(tm, 256, 0),
                  _vec(HD), _vec(HD), _tok(tm, HD, 0), _tok(tm, HD, 0), _tok(tm, HD, 0)],
        out_specs=[_tok(tm, W_A, 0), pl.BlockSpec((None, SUB, HD), lambda i: (i, 0, 0))],
        out_shape=[jax.ShapeDtypeStruct((T, W_A), bf16), jax.ShapeDtypeStruct((nt, SUB, HD), f32)],
        compiler_params=_cparams(("parallel",)),
    )(P, P, dqn, dkn, dv, qg, kg, *rope)


def _attn_fn(q, k, v, lim):
    col = lax.broadcasted_iota(jnp.int32, (1, k.shape[0]), 1)
    s = mm_nt(q, k) + jnp.where(col < lim, 0.0, -1e30)
    m = lax.stop_gradient(jnp.max(s, -1, keepdims=True))
    e = jnp.exp(s - m)
    p = e * (1.0 / jnp.sum(e, -1, keepdims=True))
    return mm(p, v)


def _attn_fwd(qn, kn, vb, tc, tq, name):
    T = qn.shape[0]

    def body(q_ref, k_ref, v_ref, o_ref):
        lim = jnp.where(pl.program_id(1) * tq < tc, tc, T)
        o_ref[...] = _attn_fn(q_ref[...], k_ref[...], v_ref[...], lim)

    return pl.pallas_call(
        body, name=name, grid=(A_HEADS, T // tq),
        in_specs=[pl.BlockSpec((tq, HD), lambda h, i: (i, h)), pl.BlockSpec((T, HD), lambda h, i: (0, h // 2)),
                  pl.BlockSpec((T, HD), lambda h, i: (0, h // 2))],
        out_specs=pl.BlockSpec((tq, HD), lambda h, i: (i, h)),
        out_shape=jax.ShapeDtypeStruct((T, 512), f32),
        compiler_params=_cparams(("parallel", "parallel")),
    )(qn, kn, vb)


def _attn_bwd(qn, kn, vb, dya, tc, tq, name):
    T = qn.shape[0]

    def body(q_ref, k_ref, v_ref, g_ref, dq_ref, dk_ref, dv_ref):
        first = (pl.program_id(1) == 0) & (pl.program_id(2) == 0)
        lim = jnp.where(pl.program_id(2) * tq < tc, tc, T)
        _, vjp = jax.vjp(lambda q, k, v: _attn_fn(q, k, v, lim), q_ref[...].astype(f32), k_ref[...].astype(f32),
                         v_ref[...].astype(f32))
        dq, dk, dv = vjp(g_ref[...])
        dq_ref[...] = dq

        @pl.when(first)
        def _():
            dk_ref[...] = dk
            dv_ref[...] = dv

        @pl.when(jnp.logical_not(first))
        def _():
            dk_ref[...] += dk
            dv_ref[...] += dv

    qspec = pl.BlockSpec((tq, HD), lambda kv, g, i: (i, 2 * kv + g))
    kspec = pl.BlockSpec((T, HD), lambda kv, g, i: (0, kv))
    return pl.pallas_call(
        body, name=name, grid=(A_HEADS // 2, 2, T // tq),
        in_specs=[qspec, kspec, kspec, qspec], out_specs=[qspec, kspec, kspec],
        out_shape=[jax.ShapeDtypeStruct((T, 512), f32), jax.ShapeDtypeStruct((T, 256), f32),
                   jax.ShapeDtypeStruct((T, 256), f32)],
        compiler_params=_cparams(("parallel", "arbitrary", "arbitrary")),
    )(qn, kn, vb, dya)


def _conv_rows(tc, tl):
    return CONV_PAD + tc + CONV_PAD + tl + CONV_PAD


def _fill_pad(pad_ref, val, tc, tl):
    z = jnp.zeros((CONV_PAD, LANE), f32)
    pad_ref[0:CONV_PAD, :] = z
    pad_ref[CONV_PAD:CONV_PAD + tc, :] = val[0:tc]
    pad_ref[CONV_PAD + tc:2 * CONV_PAD + tc, :] = z
    pad_ref[2 * CONV_PAD + tc:2 * CONV_PAD + tc + tl, :] = val[tc:tc + tl]
    pad_ref[2 * CONV_PAD + tc + tl:3 * CONV_PAD + tc + tl, :] = z


def _conv_apply(pad_ref, w_ref, K, tc, tl, rc, emit, flip=False):
    half = K // 2
    for seg0, off, n in ((0, CONV_PAD, tc), (tc, 2 * CONV_PAD + tc, tl)):
        for r0 in range(0, n, rc):
            acc = None
            for k in range(K):
                sh = (half - k) if flip else (k - half)
                term = pad_ref[pl.ds(off + r0 + sh, rc), :] * w_ref[k:k + 1, :]
                acc = term if acc is None else acc + term
            emit(seg0 + r0, acc)


def _conv_wgrad(pad_ref, dy_ref, K, tc, tl, rc, dw_ref):
    half = K // 2
    for k in range(K):
        acc = jnp.zeros((1, LANE), f32)
        for seg0, off, n in ((0, CONV_PAD, tc), (tc, 2 * CONV_PAD + tc, tl)):
            for r0 in range(0, n, rc):
                acc = acc + jnp.sum(pad_ref[pl.ds(off + r0 + k - half, rc), :] * dy_ref[pl.ds(seg0 + r0, rc), :],
                                    axis=0, keepdims=True)
        dw_ref[k:k + 1, :] = acc


def _col(T, off):
    return pl.BlockSpec((T, LANE), lambda j: (0, off // LANE + j))


C_B, C_C, C_X, C_A, C_G = range(5)
N_SEC = 5


class _Sections:
    def __init__(self, refs):
        self.refs = refs

    def __getitem__(self, idx):
        rows, sec = idx
        return self.refs[sec][rows, :].astype(f32)

    def __setitem__(self, idx, val):
        rows, sec = idx
        self.refs[sec, rows, :] = val


def _sec_specs(T):
    return [pl.BlockSpec((T, LANE), functools.partial(lambda j, s: (0, s * (BRW // LANE) + j), s=s)) for s in range(N_SEC)]


def _conv_fwd(P, wb, wd, bd, tc, tl, rc, name):
    T = tc + tl

    def body(*refs):
        p_ref = _Sections(refs[:N_SEC])
        wb_ref, wd_ref, bd_ref, yb_ref, hh_ref, pad_ref = refs[N_SEC:]
        _fill_pad(pad_ref, p_ref[:, C_C] * p_ref[:, C_X], tc, tl)

        def emit_b(r0, y):
            yb_ref[pl.ds(r0, rc), :] = y * p_ref[pl.ds(r0, rc), C_B]

        _conv_apply(pad_ref, wb_ref, KB, tc, tl, rc, emit_b)
        _fill_pad(pad_ref, p_ref[:, C_A] * _sigmoid(p_ref[:, C_G]), tc, tl)

        def emit_d(r0, y):
            hh_ref[pl.ds(r0, rc), :] = y + bd_ref[...]

        _conv_apply(pad_ref, wd_ref, KD, tc, tl, rc, emit_d)

    return pl.pallas_call(
        body, name=name, grid=(BRW // LANE,),
        in_specs=_sec_specs(T) + [pl.BlockSpec((KB, LANE), lambda j: (0, j)), pl.BlockSpec((KD, LANE), lambda j: (0, j)),
                                  pl.BlockSpec((1, LANE), lambda j: (0, j))],
        out_specs=[_col(T, 0), _col(T, 0)],
        out_shape=[jax.ShapeDtypeStruct((T, BRW), f32), jax.ShapeDtypeStruct((T, BRW), f32)],
        scratch_shapes=[pltpu.VMEM((_conv_rows(tc, tl), LANE), f32)],
        compiler_params=_cparams(("parallel",)),
    )(*[P] * N_SEC, wb, wd, bd)


def _conv_bwd(P, dyb, dhh, wb, wd, tc, tl, rc, name):
    T = tc + tl

    def body(*refs):
        p_ref = _Sections(refs[:N_SEC])
        dyb_ref, dhh_ref, wb_ref, wd_ref, dp3_ref, dwb_ref, dwd_ref, dbd_ref, pad_ref, pad2_ref, tmp_ref = refs[N_SEC:]
        dp_ref = _Sections(dp3_ref)
        _fill_pad(pad_ref, p_ref[:, C_C] * p_ref[:, C_X], tc, tl)

        def emit_cv(r0, y):
            dp_ref[pl.ds(r0, rc), C_B] = (y * dyb_ref[pl.ds(r0, rc), :]).astype(bf16)

        _conv_apply(pad_ref, wb_ref, KB, tc, tl, rc, emit_cv)
        tmp_ref[...] = dyb_ref[...] * p_ref[:, C_B]
        _conv_wgrad(pad_ref, tmp_ref, KB, tc, tl, rc, dwb_ref)
        _fill_pad(pad2_ref, tmp_ref[...], tc, tl)

        def emit_ds(r0, y):
            dp_ref[pl.ds(r0, rc), C_C] = (y * p_ref[pl.ds(r0, rc), C_X]).astype(bf16)
            dp_ref[pl.ds(r0, rc), C_X] = (y * p_ref[pl.ds(r0, rc), C_C]).astype(bf16)

        _conv_apply(pad2_ref, wb_ref, KB, tc, tl, rc, emit_ds, flip=True)
        _fill_pad(pad_ref, p_ref[:, C_A] * _sigmoid(p_ref[:, C_G]), tc, tl)
        _conv_wgrad(pad_ref, dhh_ref, KD, tc, tl, rc, dwd_ref)
        dbd_ref[...] = jnp.sum(dhh_ref[...], axis=0, keepdims=True)
        _fill_pad(pad2_ref, dhh_ref[...], tc, tl)

        def emit_d2(r0, y):
            sg = _sigmoid(p_ref[pl.ds(r0, rc), C_G])
            a = p_ref[pl.ds(r0, rc), C_A]
            dp_ref[pl.ds(r0, rc), C_A] = (y * sg).astype(bf16)
            dp_ref[pl.ds(r0, rc), C_G] = (y * a * sg * (1.0 - sg)).astype(bf16)

        _conv_apply(pad2_ref, wd_ref, KD, tc, tl, rc, emit_d2, flip=True)

    return pl.pallas_call(
        body, name=name, grid=(BRW // LANE,),
        in_specs=_sec_specs(T) + [_col(T, 0), _col(T, 0),
                                  pl.BlockSpec((KB, LANE), lambda j: (0, j)), pl.BlockSpec((KD, LANE), lambda j: (0, j))],
        out_specs=[pl.BlockSpec((N_SEC, T, LANE), lambda j: (0, 0, j)), pl.BlockSpec((KB, LANE), lambda j: (0, j)),
                   pl.BlockSpec((KD, LANE), lambda j: (0, j)), pl.BlockSpec((1, LANE), lambda j: (0, j))],
        out_shape=[jax.ShapeDtypeStruct((N_SEC, T, BRW), bf16), jax.ShapeDtypeStruct((KB, BRW), f32),
                   jax.ShapeDtypeStruct((KD, BRW), f32), jax.ShapeDtypeStruct((1, BRW), f32)],
        scratch_shapes=[pltpu.VMEM((_conv_rows(tc, tl), LANE), f32), pltpu.VMEM((_conv_rows(tc, tl), LANE), f32),
                        pltpu.VMEM((T, LANE), f32)],
        compiler_params=_cparams(("parallel",)),
    )(*[P] * N_SEC, dyb, dhh, wb, wd)


def _gla_chunk(q, k, v, r, w2, b2, st, isfwd):
    z = mm(r, w2) + b2
    g = jax.nn.log_sigmoid(z[:, 0:C_KW] if isfwd else z[:, C_KW:2 * C_KW]) / C_TAU
    ri = lax.broadcasted_iota(jnp.int32, (CH, CH), 0)
    ci = lax.broadcasted_iota(jnp.int32, (CH, CH), 1)
    tri = ((ci <= ri) if isfwd else (ci >= ri)).astype(f32)
    cum = jnp.dot(tri, g, preferred_element_type=f32, precision=lax.Precision.HIGHEST)
    last = jnp.sum(g, axis=0, keepdims=True)
    q = q * (C_KW // C_HEADS) ** -0.5
    hv = lax.broadcasted_iota(jnp.int32, (BRW, C_KW), 0) // (BRW // C_HEADS)
    hk = lax.broadcasted_iota(jnp.int32, (BRW, C_KW), 1) // (C_KW // C_HEADS)
    st_new = st * jnp.exp(last) + jnp.where(hv == hk, mm_tn(v, k * jnp.exp(last - cum)), 0.0)
    o = mm_nt(q * jnp.exp(cum), st)
    rowi = lax.broadcasted_iota(jnp.int32, (CH, C_KW), 0)
    srow = lax.broadcasted_iota(jnp.int32, (C_HEADS * CH, C_KW), 0)
    slane = lax.broadcasted_iota(jnp.int32, (C_HEADS * CH, C_KW), 1)
    own_lanes = srow // CH == slane // (C_KW // C_HEADS)
    pos = lax.broadcasted_iota(jnp.int32, (C_HEADS * CH, CH), 0) % CH
    key = lax.broadcasted_iota(jnp.int32, (C_HEADS * CH, CH), 1)
    scores = jnp.zeros((C_HEADS * CH, CH), f32)
    for a in range(CH // GLA_SUB):
        idx = GLA_SUB * a - 1 if isfwd else GLA_SUB * (a + 1)
        ref = jnp.sum(jnp.where(rowi == idx, cum, 0.0), axis=0, keepdims=True)
        qa = q * jnp.exp(jnp.minimum(cum - ref, 0.0))
        ka = k * jnp.exp(jnp.minimum(ref - cum, GLA_CLAMP))
        s = mm_nt(jnp.where(own_lanes, jnp.concatenate([qa] * C_HEADS, axis=0), 0.0), ka)
        scores = scores + jnp.where(pos // GLA_SUB == a, s, 0.0)
    scores = jnp.where((key <= pos) if isfwd else (key >= pos), scores, 0.0)
    vw = BRW // C_HEADS
    o = o + jnp.concatenate([mm(scores[CH * hd:CH * (hd + 1)], v[:, vw * hd:vw * (hd + 1)]) for hd in range(C_HEADS)],
                            axis=1)
    return o, st_new


def _gla_chunk_of(d, n, nc, nch):
    back = jnp.where(n < nc, nc - 1 - n, nch - 1 - (n - nc))
    return jnp.where(d == 0, n, back)


def _gla_fwd(P, w2, b2, tc, name):
    T = P.shape[0]
    nch, nc = T // CH, tc // CH

    back = lambda n: _gla_chunk_of(1, n, nc, nch)

    def body(pf_ref, pb_ref, w_ref, b_ref, of_ref, ob_ref, ssf_ref, ssb_ref, stf_ref, stb_ref):
        @pl.when(pl.program_id(0) == 0)
        def _():
            stf_ref[...] = jnp.zeros_like(stf_ref)
            stb_ref[...] = jnp.zeros_like(stb_ref)

        for p_ref, o_ref, ss_ref, st_ref, isfwd in ((pf_ref, of_ref, ssf_ref, stf_ref, True),
                                                    (pb_ref, ob_ref, ssb_ref, stb_ref, False)):
            st = st_ref[...]
            ss_ref[...] = st
            p = p_ref[...].astype(f32)
            o, st_new = _gla_chunk(p[:, 0:G_K], p[:, G_K:G_V], p[:, G_V:G_R], p[:, G_R:W_G], w_ref[...], b_ref[...], st, isfwd)
            o_ref[...] = o
            st_ref[...] = st_new

    sd = jax.ShapeDtypeStruct
    return pl.pallas_call(
        body, name=name, grid=(nch,),
        in_specs=[pl.BlockSpec((CH, W_G), lambda n: (n, 0)), pl.BlockSpec((CH, W_G), lambda n: (back(n), 0)),
                  pl.BlockSpec((LANE, 512), lambda n: (0, 0)), pl.BlockSpec((1, 512), lambda n: (0, 0))],
        out_specs=[pl.BlockSpec((CH, BRW), lambda n: (n, 0)), pl.BlockSpec((CH, BRW), lambda n: (back(n), 0)),
                   pl.BlockSpec((None, BRW, C_KW), lambda n: (n, 0, 0)), pl.BlockSpec((None, BRW, C_KW), lambda n: (n, 0, 0))],
        out_shape=[sd((T, BRW), f32), sd((T, BRW), f32), sd((nch, BRW, C_KW), f32), sd((nch, BRW, C_KW), f32)],
        scratch_shapes=[pltpu.VMEM((BRW, C_KW), f32), pltpu.VMEM((BRW, C_KW), f32)],
        compiler_params=_cparams(("arbitrary",)),
    )(P, P, w2, b2)


def _gla_bwd(P, w2, b2, ssave, doc, tc, name):
    T = P.shape[0]
    nch, nc = T // CH, tc // CH

    fwd_chunk = lambda m: nch - 1 - m
    back_chunk = lambda m: _gla_chunk_of(1, nch - 1 - m, nc, nch)

    def body(pf_ref, pb_ref, w_ref, b_ref, ssf_ref, ssb_ref, gf_ref, gb_ref, dpf_ref, dpb_ref, dw_ref, db_ref,
             dstf_ref, dstb_ref):
        m = pl.program_id(0)

        @pl.when(m == 0)
        def _():
            dstf_ref[...] = jnp.zeros_like(dstf_ref)
            dstb_ref[...] = jnp.zeros_like(dstb_ref)

        dw_sum, db_sum = None, None
        for p_ref, ss_ref, g_ref, dp_ref, dst_ref, isfwd in ((pf_ref, ssf_ref, gf_ref, dpf_ref, dstf_ref, True),
                                                             (pb_ref, ssb_ref, gb_ref, dpb_ref, dstb_ref, False)):
            p = p_ref[...].astype(f32)
            _, vjp = jax.vjp(lambda q, k, v, r, w, b, st: _gla_chunk(q, k, v, r, w, b, st, isfwd),
                             p[:, 0:G_K], p[:, G_K:G_V], p[:, G_V:G_R], p[:, G_R:W_G], w_ref[...], b_ref[...], ss_ref[...])
            dq, dk, dv, dr, dw, db, dst = vjp((g_ref[...], dst_ref[...]))
            dp_ref[:, 0:G_K] = dq
            dp_ref[:, G_K:G_V] = dk
            dp_ref[:, G_V:G_R] = dv
            dp_ref[:, G_R:W_G] = dr
            dst_ref[...] = dst
            dw_sum = dw if dw_sum is None else dw_sum + dw
            db_sum = db if db_sum is None else db_sum + db

        @pl.when(m == 0)
        def _():
            dw_ref[...] = dw_sum
            _partial_rows(db_ref, [db_sum])

        @pl.when(m > 0)
        def _():
            dw_ref[...] += dw_sum
            db_ref[0:1, :] += db_sum

    ssf, ssb = ssave
    chunk_f = lambda w: pl.BlockSpec((CH, w), lambda m: (fwd_chunk(m), 0))
    chunk_b = lambda w: pl.BlockSpec((CH, w), lambda m: (back_chunk(m), 0))
    state = pl.BlockSpec((None, BRW, C_KW), lambda m: (nch - 1 - m, 0, 0))
    sd = jax.ShapeDtypeStruct
    return pl.pallas_call(
        body, name=name, grid=(nch,),
        in_specs=[chunk_f(W_G), chunk_b(W_G), pl.BlockSpec((LANE, 512), lambda m: (0, 0)), pl.BlockSpec((1, 512), lambda m: (0, 0)),
                  state, state, chunk_f(BRW), chunk_b(BRW)],
        out_specs=[chunk_f(W_G), chunk_b(W_G), pl.BlockSpec((LANE, 512), lambda m: (0, 0)), pl.BlockSpec((SUB, 512), lambda m: (0, 0))],
        out_shape=[sd((T, W_G), f32), sd((T, W_G), f32), sd((LANE, 512), f32), sd((SUB, 512), f32)],
        scratch_shapes=[pltpu.VMEM((BRW, C_KW), f32), pltpu.VMEM((BRW, C_KW), f32)],
        compiler_params=_cparams(("arbitrary",)),
    )(P, P, w2, b2, ssf, ssb, doc, doc)


def _sum_dirs(a, b, tm, name):
    T, W = a.shape

    def body(a_ref, b_ref, o_ref):
        o_ref[...] = (a_ref[...] + b_ref[...]).astype(bf16)

    spec = pl.BlockSpec((tm, W), lambda i: (i, 0))
    return pl.pallas_call(
        body, name=name, grid=(T // tm,), in_specs=[spec, spec], out_specs=spec,
        out_shape=jax.ShapeDtypeStruct((T, W), bf16),
        compiler_params=_cparams(("parallel",)),
    )(a, b)


def _merge_fn(h, m_l, m_c, isctx, ya, ga, yb, gb, of, ob, gc, hh, gd, mg, es, ey, cn, dng, dnb, lg, lb, wbr, wout):
    oc = of + ob
    yc = jnp.concatenate([_rms(oc[:, HD * i:HD * (i + 1)], cn[:, HD * i:HD * (i + 1)]) for i in range(C_HEADS)], 1)
    brs = [ya * _silu(ga), yb * _silu(gb), yc * _silu(gc), _silu(_ln(hh) * dng + dnb) * _silu(gd)]
    acc = None
    for i in range(4):
        t = _sigmoid(mg[:, D * i:D * (i + 1)]) * (mm(brs[i], wbr[i]) + es[i])
        acc = t if acc is None else acc + t
    y = mm(acc, wout) + ey
    gate = jnp.where(isctx, m_c[:, 2 * D:3 * D], m_l[:, 2 * D:3 * D])
    hn = _ln(ALPHA * h + gate * y) * lg + lb
    return hn, (brs, acc)


def _merge_specs(tm):
    t = lambda w, off=0: _tok(tm, w, off)
    return [t(D), pl.BlockSpec((SUB, 3 * D), lambda i: (0, 0)),
            t(BRW), t(BRW, M_GA), t(BRW), t(BRW, M_GB),
            t(BRW), t(BRW),
            t(BRW, M_GC), t(BRW), t(BRW, M_GD), t(4 * D, 0),
            _vec(BRW), _vec(BRW), _vec(BRW), _vec(D), _vec(D),
            pl.BlockSpec((4, BRW, D), lambda i: (0, 0, 0)), pl.BlockSpec((D, D), lambda i: (0, 0))]


def _merge_fwd(h, modv_l, ya, yb, o2, hh, P, cn, dng, dnb, lg, lb, wbr, wout, tc, tm, name):
    T = h.shape[0]

    def body(h_ref, m_ref, ya_ref, ga_ref, yb_ref, gb_ref, of_ref, ob_ref, gc_ref, hh_ref, gd_ref, mg_ref,
             cn_ref, dng_ref, dnb_ref, lg_ref, lb_ref, wbr_ref, wout_ref, o_ref):
        isctx = _row_ids(pl.program_id(0), tm) < tc
        zero = jnp.zeros((tm, D), f32)
        up = lambda r: r[...].astype(f32)
        hn, _ = _merge_fn(h_ref[...], m_ref[0:1, :], m_ref[1:2, :], isctx, ya_ref[...], up(ga_ref), yb_ref[...],
                          up(gb_ref), of_ref[...], ob_ref[...], up(gc_ref), hh_ref[...], up(gd_ref), up(mg_ref),
                          [zero] * 4, zero, cn_ref[...], dng_ref[...], dnb_ref[...], lg_ref[...], lb_ref[...],
                          [wbr_ref[i] for i in range(4)], wout_ref[...])
        o_ref[...] = hn

    return pl.pallas_call(
        body, name=name, grid=(T // tm,),
        in_specs=_merge_specs(tm), out_specs=_tok(tm, D, 0),
        out_shape=jax.ShapeDtypeStruct((T, D), f32),
        compiler_params=_cparams(("parallel",)),
    )(h, modv_l, ya, P, yb, P, o2[0], o2[1], P, hh, P, P, cn, dng, dnb, lg, lb, wbr, wout)


def _merge_bwd(dhn, h, modv_l, ya, yb, o2, hh, P, cn, dng, dnb, lg, lb, wbr, wout, tc, tm, name):
    T = h.shape[0]
    nt = T // tm

    def body(g_ref, h_ref, m_ref, ya_ref, ga_ref, yb_ref, gb_ref, of_ref, ob_ref, gc_ref, hh_ref, gd_ref, mg_ref,
             cn_ref, dng_ref, dnb_ref, lg_ref, lb_ref, wbr_ref, wout_ref,
             dh_ref, dm_ref, dya_ref, dyb_ref, doc_ref, dhh_ref, dp_ref,
             br_ref, z_ref, acc_ref, dy_ref, dv5_ref, dvd_ref):
        isctx = _row_ids(pl.program_id(0), tm) < tc
        zero = jnp.zeros((tm, D), f32)
        wbr_v = [wbr_ref[i] for i in range(4)]
        wout_v = wout_ref[...]
        up = lambda r: r[...].astype(f32)

        def fn(h, ml, mc, ya, ga, yb, gb, oc, gc, hh, gd, mg, e0, e1, e2, e3, ey, cn, dng, dnb, lg, lb):
            return _merge_fn(h, ml, mc, isctx, ya, ga, yb, gb, oc, jnp.zeros_like(oc), gc, hh, gd, mg,
                             [e0, e1, e2, e3], ey, cn, dng, dnb, lg, lb, wbr_v, wout_v)

        _, vjp, (brs, acc) = jax.vjp(
            fn, h_ref[...], m_ref[0:1, :], m_ref[1:2, :], ya_ref[...], up(ga_ref), yb_ref[...], up(gb_ref),
            of_ref[...] + ob_ref[...], up(gc_ref), hh_ref[...], up(gd_ref), up(mg_ref), zero, zero, zero, zero, zero,
            cn_ref[...], dng_ref[...], dnb_ref[...], lg_ref[...], lb_ref[...], has_aux=True)
        (dh, dml, dmc, dya, dga, dyb, dgb, doc, dgc, dhh, dgd, dmg, z0, z1, z2, z3, dy,
         dcn, ddng, ddnb, dlg, dlb) = vjp(g_ref[...])
        dh_ref[...] = dh
        _partial_rows(dm_ref, [dml, dmc])
        dya_ref[...] = dya
        dyb_ref[...] = dyb
        doc_ref[...] = doc
        dhh_ref[...] = dhh
        dp_ref[:, 0:M_GA] = dmg.astype(bf16)
        dp_ref[:, M_GA:M_GB] = dga.astype(bf16)
        dp_ref[:, M_GB:M_GC] = dgb.astype(bf16)
        dp_ref[:, M_GC:M_GD] = dgc.astype(bf16)
        dp_ref[:, M_GD:W_M] = dgd.astype(bf16)
        for i, z in enumerate((z0, z1, z2, z3)):
            br_ref[i] = brs[i].astype(bf16)
            z_ref[i] = z.astype(bf16)
        acc_ref[...] = acc.astype(bf16)
        dy_ref[...] = dy.astype(bf16)
        _partial_rows(dv5_ref, [dcn, ddng, ddnb])
        _partial_rows(dvd_ref, [dlg, dlb])

    t = lambda w: _tok(tm, w, 0)
    part = lambda w: pl.BlockSpec((None, SUB, w), lambda i: (i, 0, 0))
    sd = jax.ShapeDtypeStruct
    return pl.pallas_call(
        body, name=name, grid=(nt,),
        in_specs=[t(D)] + _merge_specs(tm),
        out_specs=[t(D), part(3 * D)] + [t(BRW)] * 4 + [t(W_M),
                   pl.BlockSpec((4, tm, BRW), lambda i: (0, i, 0)), pl.BlockSpec((4, tm, D), lambda i: (0, i, 0)),
                   t(D), t(D), part(BRW), part(D)],
        out_shape=[sd((T, D), f32), sd((nt, SUB, 3 * D), f32)] + [sd((T, BRW), f32)] * 4 + [sd((T, W_M), bf16),
                   sd((4, T, BRW), bf16), sd((4, T, D), bf16), sd((T, D), bf16), sd((T, D), bf16),
                   sd((nt, SUB, BRW), f32), sd((nt, SUB, D), f32)],
        compiler_params=_cparams(("parallel",)),
    )(dhn, h, modv_l, ya, P, yb, P, o2[0], o2[1], P, hh, P, P, cn, dng, dnb, lg, lb, wbr, wout)


def _loss_kernel(h, tgt, tc, tm, name):
    T = h.shape[0]
    nt = T // tm
    nct = tc // tm

    def body(h_ref, t_ref, d_ref, l_ref):
        i = pl.program_id(0)
        err = h_ref[...] - t_ref[...]
        lat = (i >= nct).astype(f32)
        d_ref[...] = err * (lat / D)
        l_ref[...] = jnp.zeros((SUB, LANE), f32) + lat * 0.5 * jnp.sum(err * err) / D

    return pl.pallas_call(
        body, name=name, grid=(nt,),
        in_specs=[pl.BlockSpec((tm, D), lambda i: (i, 0)),
                  pl.BlockSpec((tm, D), lambda i: (jnp.maximum(i - nct, 0), 0))],
        out_specs=[pl.BlockSpec((tm, D), lambda i: (i, 0)), pl.BlockSpec((None, SUB, LANE), lambda i: (i, 0, 0))],
        out_shape=[jax.ShapeDtypeStruct((T, D), f32), jax.ShapeDtypeStruct((nt, SUB, LANE), f32)],
        compiler_params=_cparams(("parallel",)),
    )(h, tgt)


def _rope_tables(tc, tl):
    t = jnp.arange(tl)
    inv = ROPE_THETA ** (-jnp.arange(0, HD // 2, 2, dtype=f32) / (HD // 2))
    ang = jnp.concatenate([(t // GRID_W).astype(f32)[:, None] * inv, (t % GRID_W).astype(f32)[:, None] * inv], -1)
    cos, sin = jnp.repeat(jnp.cos(ang), 2, axis=1), jnp.repeat(jnp.sin(ang), 2, axis=1)
    even = (jnp.arange(HD) % 2 == 0)[None, :]
    cos_f = jnp.concatenate([jnp.ones((tc, HD), f32), cos], 0)
    sin_a = jnp.concatenate([jnp.zeros((tc, HD), f32), jnp.where(even, -sin, 0.0)], 0)
    sin_b = jnp.concatenate([jnp.zeros((tc, HD), f32), jnp.where(even, 0.0, sin)], 0)
    return cos_f, sin_a, sin_b


N_CHIPS = 4
SHARD = N_IN // N_CHIPS


def _group_ranges():
    return dict(M=[(S_MG, 4 * D), (S_GA, BRW), (S_GB, BRW), (S_GC, BRW), (S_GD, BRW)], A=[(S_Q, W_A)],
                C=[(S_B, 3 * BRW), (S_DA, 2 * BRW)], G=[(S_CQ, 2 * C_KW + BRW), (S_R, 2 * C_RANK)])


def _group_weights(w4):
    out = {}
    for k, ranges in _group_ranges().items():
        parts = []
        for a, n in ranges:
            n = LANE if (k, a) == ("G", S_R) else n
            while n > 0:
                s, r = divmod(a, SHARD)
                m = min(n, SHARD - r)
                parts.append(w4[s, r:r + m])
                a, n = a + m, n - m
        out[k] = jnp.concatenate(parts, 0)
    return out


def _ungroup(g):
    secs = []
    for k, ranges in _group_ranges().items():
        off = 0
        for a, n in ranges:
            secs.append((a, g[k][off:off + n]))
            off += n
    return jnp.concatenate([v for _, v in sorted(secs, key=lambda t: t[0])], 0)


PROJ_TN = dict(M=2048, A=1024, C=1280, G=1152)
DU_TK = dict(M=2048, A=1024, C=BRW, G=1152)
DWP_TN = dict(M=768, A=1024, C=BRW, G=1152)


def _gate_weights(w2_l, gb_l):
    w = jnp.zeros((LANE, 2 * C_KW), f32)
    w = w.at[0:C_RANK, 0:C_KW].set(w2_l[0]).at[C_RANK:2 * C_RANK, C_KW:2 * C_KW].set(w2_l[1])
    return w, jnp.concatenate([gb_l[0], gb_l[1]])[None, :]


def _local_step(x1, c1, ctx1, tgt1, c_ctx, b_mod, weights_of, q_norm, k_norm, b_conv, w2, gb, c_norm, d_conv_w,
                d_conv_b, d_norm_g, d_norm_b, grads_done, ln_g, ln_b, tm, token=None):
    tc, tl = ctx1.shape[0], x1.shape[0]
    T = tc + tl
    rc = min(256, tc)
    tmb = tm // 2
    tmm = 768 if T % 768 == 0 else tm
    rope = _rope_tables(tc, tl)
    cin = jnp.concatenate([c1, c_ctx[None, :], jnp.zeros((SUB - 2, D), f32)], 0)
    if token is not None:
        cin = cin + token[:, 0:1]
    row = lambda v: v[None, :]

    h = jnp.concatenate([ctx1, x1], 0)
    saved, wp, w_br, w_out, w_mod, modv = [], *([None] * DEPTH for _ in range(5))
    for l in range(DEPTH):
        wp[l], merge_weights, w_mod[l] = weights_of(l, h)
        modv[l] = _mod_fwd(cin, w_mod[l], b_mod[l], f"mod_fwd{l}")
        u = _ln_fwd(h, modv[l], tc, tm, f"ln_fwd{l}")
        P = {k: _matmul(u, wp[l][k], "nt", tmm, PROJ_TN[k], D, f"proj{l}{k}", out_dtype=bf16) for k in GROUPS}
        qn, kn, vb = _prep_fwd(P["A"], row(q_norm[l]), row(k_norm[l]), rope, tm, f"prep_fwd{l}")
        ya = _attn_fwd(qn, kn, vb, tc, tm, f"attn_fwd{l}")
        yb, hh = _conv_fwd(P["C"], b_conv[l], d_conv_w[l], row(d_conv_b[l]), tc, tl, rc, f"conv_fwd{l}")
        w2p, b2p = _gate_weights(w2[l], gb[l])
        gla = _gla_fwd(P["G"], w2p, b2p, tc, f"gla_fwd{l}")
        o2, ssave = gla[:2], gla[2:]
        w_br[l], w_out[l] = merge_weights(o2[0])
        hn = _merge_fwd(h, modv[l], ya, yb, o2, hh, P["M"], row(c_norm[l]), row(d_norm_g[l]), row(d_norm_b[l]),
                        row(ln_g[l]), row(ln_b[l]), w_br[l], w_out[l], tc, tm, f"merge_fwd{l}")
        saved.append((h, u, P, qn, kn, vb, ya, yb, hh, o2, ssave, w2p, b2p))
        h = hn

    dh, lparts = _loss_kernel(h, tgt1, tc, tm, "loss")
    loss = jnp.sum(lparts[:, 0, 0])

    g = {k: [None] * DEPTH for k in ("wp", "q_norm", "k_norm", "b_conv", "w2", "gb", "c_norm", "d_conv_w", "d_conv_b",
                                     "d_norm_g", "d_norm_b", "w_br", "w_out", "ln_g", "ln_b", "modv")}
    for l in reversed(range(DEPTH)):
        h_in, u, P, qn, kn, vb, ya, yb, hh, o2, ssave, w2p, b2p = saved[l]
        dP = {}
        (dh_res, dm_mg, dya, dyb, doc, dhh, dP["M"], br, z, acc, dy, dv5, dvd) = _merge_bwd(
            dh, h_in, modv[l], ya, yb, o2, hh, P["M"], row(c_norm[l]), row(d_norm_g[l]), row(d_norm_b[l]),
            row(ln_g[l]), row(ln_b[l]), w_br[l], w_out[l], tc, tmb, f"merge_bwd{l}")
        g["w_br"][l] = _matmul_tn_batched(br, z, N_CHIPS, f"dwbr{l}")
        g["w_out"][l] = _matmul(acc, dy, "tn", D, D, T, f"dwout{l}", out_dtype=bf16)
        tk = grads_done(l, {k: g[k][l] for k in ("w_br", "w_out")})
        qg_l = row(q_norm[l]) if tk is None else row(q_norm[l]) + tk[0:1, :]
        v5 = jnp.sum(dv5, 0)
        g["c_norm"][l], g["d_norm_g"][l], g["d_norm_b"][l] = v5[0], v5[1], v5[2]
        vd = jnp.sum(dvd, 0)
        g["ln_g"][l], g["ln_b"][l] = vd[0], vd[1]
        dqn, dkn, dv = _attn_bwd(qn, kn, vb, dya, tc, tm, f"attn_bwd{l}")
        dP["A"], dqk = _prep_bwd(P["A"], dqn, dkn, dv, qg_l, row(k_norm[l]), rope, tm, f"prep_bwd{l}")
        dqk = jnp.sum(dqk, 0)
        g["q_norm"][l], g["k_norm"][l] = dqk[0], dqk[1]
        dP["C"], dwb, dwd, dbd = _conv_bwd(P["C"], dyb, dhh, b_conv[l], d_conv_w[l], tc, tl, rc, f"conv_bwd{l}")
        g["b_conv"][l], g["d_conv_w"][l], g["d_conv_b"][l] = dwb, dwd, dbd[0]
        dpf, dpb, dw2p, db2p = _gla_bwd(P["G"], w2p, b2p, ssave, doc, tc, f"gla_bwd{l}")
        dP["G"] = _sum_dirs(dpf, dpb, tm, f"gla_sum{l}")
        db2p = db2p[0]
        g["w2"][l] = jnp.stack([dw2p[0:C_RANK, 0:C_KW], dw2p[C_RANK:2 * C_RANK, C_KW:2 * C_KW]])
        g["gb"][l] = jnp.stack([db2p[0:C_KW], db2p[C_KW:2 * C_KW]])
        g["wp"][l] = {k: _matmul(dP[k], u, "tn", DWP_TN[k], D, T, f"dwp{l}{k}", out_dtype=bf16) for k in GROUPS}
        tk = grads_done(l, {"wp": g["wp"][l]})
        du = _matmul_groups(dP, wp[l], DU_TK, tmm, f"du{l}", after=tk)
        dh, dm_ln = _ln_bwd(du, h_in, dh_res, modv[l], tc, tm, f"ln_bwd{l}", latent_only=(l == 0))
        g["modv"][l] = jnp.sum(dm_mg, 0) + jnp.sum(dm_ln, 0)

    dmodv = jnp.stack(g.pop("modv"))
    g["w_mod"], dcin = _mod_bwd(cin, w_mod, dmodv)
    g["b_mod"] = dmodv[:, 0, :] + dmodv[:, 1, :]
    g["c_ctx"] = jnp.sum(dcin, (0, 1))[1]
    return loss, dh, g


HALF_TL = 256


TILE_BYTES = 1 << 20


def _row_tile(rows, cols, itemsize=4):
    tr = min(rows, 128)
    while rows % (2 * tr) == 0 and 2 * tr * cols * itemsize <= TILE_BYTES:
        tr *= 2
    return tr


def _adamw(w, g, m, v, name, tr=None, after=None):
    L, R, C = w.shape
    tr = _row_tile(R, C) if tr is None else tr
    if R % tr == 0:
        grid, spec = (L, R // tr), pl.BlockSpec((None, tr, C), lambda l, i: (l, i, 0))
    elif R * C * 4 <= (1 << 20):
        grid, spec = (L, 1), pl.BlockSpec((None, R, C), lambda l, i: (l, 0, 0))
    else:
        grid, spec = (L, C // HALF_TL), pl.BlockSpec((None, R, HALF_TL), lambda l, i: (l, 0, i))

    def body(w_ref, g_ref, m_ref, v_ref, *rest):
        go_ref, d_ref, nm_ref, nv_ref = rest[-4:]
        gg = g_ref[...]
        go_ref[...] = gg
        nm = B1 * m_ref[...] + (1.0 - B1) * gg
        nv = B2 * v_ref[...] + (1.0 - B2) * (gg * gg)
        m_hat = nm / (1.0 - B1 ** STEP)
        v_hat = nv / (1.0 - B2 ** STEP)
        d_ref[...] = -LR * (m_hat / (jnp.sqrt(v_hat) + AEPS) + WD * w_ref[...])
        nm_ref[...] = nm
        nv_ref[...] = nv

    return pl.pallas_call(
        body, name=name, grid=grid, in_specs=[spec] * 4 + ([] if after is None else [pl.BlockSpec(memory_space=pl.ANY)]),
        out_specs=[spec] * 4, out_shape=[jax.ShapeDtypeStruct((L, R, C), f32)] * 4,
        compiler_params=_cparams(("parallel", "parallel")),
    )(w, g, m, v, *([] if after is None else [after]))


MESH = pl.DeviceIdType.MESH
ANY = pl.BlockSpec(memory_space=pl.ANY)


def _place():
    x, y, c = lax.axis_index("x"), lax.axis_index("y"), lax.axis_index("c")
    chips = [(1 - x, y), (x, 1 - y), (1 - x, 1 - y)]
    return x, y, c, chips


def _half(ref, c, axis):
    n = ref.shape[axis] // 2
    last = axis in (-1, ref.ndim - 1)
    idx = [slice(None)] * ref.ndim
    idx[axis] = pl.ds(pl.multiple_of(c * n, LANE if last else SUB), n)
    return ref.at[tuple(idx)]


def _half_shape(shape, axis):
    s = list(shape)
    s[axis] //= 2
    return tuple(s)


def _all_gather(arrs, axes, name):
    n = len(arrs)

    def body(*refs):
        ins, outs = refs[:n], refs[n:2 * n]
        send, recv = refs[2 * n:]
        x, y, c, chips = _place()
        me, sib = 2 * x + y, (x, y, 1 - c)

        def copy(a, k, chip_idx, cc, to, src=None):
            blk = _half(outs[a].at[chip_idx], cc, axes[a])
            return pltpu.make_async_remote_copy(src_ref=blk if src is None else src, dst_ref=blk,
                                                send_sem=send.at[7 * a + k], recv_sem=recv.at[7 * a + k],
                                                device_id=to, device_id_type=MESH)

        own = [pltpu.make_async_remote_copy(src_ref=ins[a], dst_ref=outs[a].at[me], send_sem=send.at[7 * a + 6],
                                            recv_sem=recv.at[7 * a + 6], device_id=sib, device_id_type=MESH)
               for a in range(n)]
        first = own + [copy(a, j, me, c, (*chip, c), src=_half(ins[a], c, axes[a]))
                       for a in range(n) for j, chip in enumerate(chips)]
        for cp in first:
            cp.start()
        passed = []
        for a in range(n):
            for j, chip in enumerate(chips):
                k = 2 * chip[0] + chip[1]
                copy(a, j, k, c, sib).wait_recv()
                fwd = copy(a, 3 + j, k, c, sib)
                fwd.start()
                passed.append(fwd)
        for a in range(n):
            own[a].wait_recv()
            for j, chip in enumerate(chips):
                copy(a, 3 + j, 2 * chip[0] + chip[1], 1 - c, sib).wait_recv()
        for cp in first + passed:
            cp.wait_send()

    return pl.pallas_call(
        body, name=name, in_specs=[ANY] * n, out_specs=[ANY] * n,
        out_shape=[jax.ShapeDtypeStruct((N_CHIPS,) + a.shape, a.dtype) for a in arrs],
        scratch_shapes=[pltpu.SemaphoreType.DMA((7 * n,)), pltpu.SemaphoreType.DMA((7 * n,))],
    )(*arrs)


def _add_half(gfull, land, cidx, axis, name, tr=None, out_dtype=bf16):
    _, hr, hc = land.shape
    if axis == 0:
        tr = min(tr, hr) if tr else _row_tile(hr, hc)
        nb, blk = hr // tr, (None, tr, hc)
        g_spec = pl.BlockSpec(blk, lambda s, i, cr: (s, cr[0] * nb + i, 0))
        l_spec = pl.BlockSpec(blk, lambda s, i, cr: (s, i, 0))
    else:
        nb, blk = hc // HALF_TL, (None, hr, HALF_TL)
        g_spec = pl.BlockSpec(blk, lambda s, i, cr: (s, 0, cr[0] * nb + i))
        l_spec = pl.BlockSpec(blk, lambda s, i, cr: (s, 0, i))

    def body(c_ref, g_ref, l_ref, o_ref):
        o_ref[...] = (g_ref[...].astype(f32) + l_ref[...].astype(f32)).astype(o_ref.dtype)

    return pl.pallas_call(
        body, name=name,
        grid_spec=pltpu.PrefetchScalarGridSpec(
            num_scalar_prefetch=1, grid=(N_CHIPS, nb), in_specs=[g_spec, l_spec], out_specs=l_spec),
        out_shape=jax.ShapeDtypeStruct((N_CHIPS, hr, hc), out_dtype),
        compiler_params=_cparams(("parallel", "parallel")),
    )(cidx, gfull, land)


def _sum_chips(land, own, place, axis, layer, into, name, tr=None):
    _, hr, hc = land.shape
    fresh = not hasattr(into, "dtype")
    shape = tuple(into) if fresh else into.shape
    if axis == 0:
        tr = min(tr, hr) if tr else _row_tile(hr, 4 * hc, 2)
        nb, blk = hr // tr, (tr, hc)
        l_map, m_map = (lambda i, p: (0, i, 0)), (lambda i, p: (p[0], i, 0))
        o_map = lambda i, p: (layer, p[1] * nb + i, 0)
    else:
        nb, blk = hc // HALF_TL, (hr, HALF_TL)
        l_map, m_map = (lambda i, p: (0, 0, i)), (lambda i, p: (p[0], 0, i))
        o_map = lambda i, p: (layer, 0, p[1] * nb + i)

    def body(p_ref, l_ref, o_ref, *rest):
        me = p_ref[0]
        mine = o_ref[...].astype(f32)
        acc = None
        for k in range(N_CHIPS):
            t = jnp.where(me == k, mine, l_ref[k].astype(f32))
            acc = t if acc is None else acc + t
        rest[-1][...] = acc

    return pl.pallas_call(
        body, name=name,
        grid_spec=pltpu.PrefetchScalarGridSpec(
            num_scalar_prefetch=1, grid=(nb,),
            in_specs=[pl.BlockSpec((N_CHIPS,) + blk, l_map), pl.BlockSpec((None,) + blk, m_map)] + ([] if fresh else [ANY]),
            out_specs=pl.BlockSpec((None,) + blk, o_map)),
        out_shape=jax.ShapeDtypeStruct(shape, f32),
        input_output_aliases={} if fresh else {3: 0},
        compiler_params=_cparams(("parallel",)),
    )(place, land, own, *([] if fresh else [into]))


def _sibling_fill(arrs, axes, name):
    n = len(arrs)

    def body(*refs):
        outs = refs[n:2 * n]
        send, recv = refs[2 * n:]
        x, y, c, _ = _place()
        cps = [pltpu.make_async_remote_copy(src_ref=_half(outs[a], c, axes[a] + 1), dst_ref=_half(outs[a], c, axes[a] + 1),
                                            send_sem=send.at[a], recv_sem=recv.at[a], device_id=(x, y, 1 - c),
                                            device_id_type=MESH) for a in range(n)]
        for cp in cps:
            cp.start()
        for a in range(n):
            blk = _half(outs[a], 1 - c, axes[a] + 1)
            pltpu.make_async_remote_copy(src_ref=blk, dst_ref=blk, send_sem=send.at[a], recv_sem=recv.at[a],
                                         device_id=(x, y, 1 - c), device_id_type=MESH).wait_recv()
        for cp in cps:
            cp.wait_send()

    return pl.pallas_call(
        body, name=name, in_specs=[ANY] * n, out_specs=[ANY] * n,
        out_shape=[jax.ShapeDtypeStruct(a.shape, a.dtype) for a in arrs],
        input_output_aliases={a: a for a in range(n)},
        scratch_shapes=[pltpu.SemaphoreType.DMA((n,)), pltpu.SemaphoreType.DMA((n,))],
    )(*arrs)


HBM = pl.BlockSpec(memory_space=pltpu.HBM)
SEM = pl.BlockSpec(memory_space=pltpu.SEMAPHORE)
EFFECT = pltpu.SideEffectType.DATAFLOW_SIDE_EFFECTING
PEERS = 4


def _split_copies(srcs, lands, send, recv, gather, axes=None):
    x, y, c, chips = _place()
    me = 2 * x + y
    if axes is not None:
        out = []
        for a in range(len(srcs)):
            sems = dict(send_sem=send.at[PEERS * a], recv_sem=recv.at[PEERS * a], device_id=(x, y, 1 - c), device_id_type=MESH)
            copy = pltpu.make_async_remote_copy(src_ref=_half(srcs[a], 1 - c, axes[a] + 1), dst_ref=lands[a], **sems)
            out.append((copy, copy))
        return out
    peers = [((*chip, c), 2 * chip[0] + chip[1]) for chip in chips] + ([((x, y, 1 - c), me)] if gather else [])
    out = []
    for a in range(len(srcs)):
        for j, (dev, k) in enumerate(peers):
            src = srcs[a] if gather else srcs[a].at[k]
            sems = dict(send_sem=send.at[PEERS * a + j], recv_sem=recv.at[PEERS * a + j], device_id=dev, device_id_type=MESH)
            out.append((pltpu.make_async_remote_copy(src_ref=src, dst_ref=lands[a].at[me], **sems),
                        pltpu.make_async_remote_copy(src_ref=src, dst_ref=lands[a].at[k], **sems)))
    return out


def _split_start(srcs, gather, after, name, axes=None):
    n = len(srcs)
    if axes is not None:
        lands = [lax.empty(_half_shape(s.shape, axes[a] + 1), s.dtype) for a, s in enumerate(srcs)]
    else:
        lands = [lax.empty(((N_CHIPS,) + s.shape) if gather else s.shape, s.dtype) for s in srcs]

    def body(*refs):
        send, recv = refs[2 * n + 1], refs[2 * n + 2]
        for start, _ in _split_copies(refs[:n], refs[n:2 * n], send, recv, gather, axes):
            start.start()
        refs[-1][...] = jnp.zeros_like(refs[-1])

    sems = pltpu.SemaphoreType.DMA((PEERS * n,))
    hbm = lambda a: pltpu.with_memory_space_constraint(a, pltpu.HBM)
    out = pl.pallas_call(
        body, name=name,
        out_shape=(sems, sems, *[pltpu.HBM(a.shape, a.dtype) for a in srcs + lands], jax.ShapeDtypeStruct((SUB, LANE), f32)),
        in_specs=[HBM] * (2 * n) + [ANY], out_specs=(SEM, SEM, *[HBM] * (2 * n), pl.BlockSpec(memory_space=pltpu.VMEM)),
        input_output_aliases={i: 2 + i for i in range(2 * n)},
        compiler_params=pltpu.CompilerParams(has_side_effects=EFFECT),
    )(*[hbm(a) for a in srcs + lands], after)
    return out[0], out[1], list(out[2:2 + n]), list(out[2 + n:2 + 2 * n]), out[-1]


def _split_wait(send, recv, srcs, lands, gather, after, name, axes=None):
    n = len(srcs)

    def body(*refs):
        for start, arrival in _split_copies(refs[:n], refs[n:2 * n], refs[2 * n], refs[2 * n + 1], gather, axes):
            start.wait_send()
            arrival.wait_recv()

    out = pl.pallas_call(
        body, name=name, out_shape=[pltpu.HBM(a.shape, a.dtype) for a in srcs + lands],
        in_specs=[HBM] * (2 * n) + [SEM, SEM, ANY], out_specs=[HBM] * (2 * n),
        input_output_aliases={i: i for i in range(2 * n)},
        compiler_params=pltpu.CompilerParams(has_side_effects=EFFECT),
    )(*srcs, *lands, send, recv, after)
    return list(out[:n]), list(out[n:])


N_DEV = 8


def _all_reduce_small(v, name, after):
    R = v.shape[0]

    def body(v_ref, after_ref, o_ref, land_ref, send, recv):
        x, y, c, _ = _place()
        me = 4 * x + 2 * y + c
        land_ref[me] = v_ref[...]
        cps = []
        for m in range(1, N_DEV):
            px, py, pc = [(1 - q) if (m >> s) & 1 else q for q, s in ((x, 2), (y, 1), (c, 0))]
            cps.append((pltpu.make_async_remote_copy(src_ref=v_ref, dst_ref=land_ref.at[me], send_sem=send.at[m - 1],
                                                     recv_sem=recv.at[m - 1], device_id=(px, py, pc), device_id_type=MESH),
                        4 * px + 2 * py + pc, m))
        for cp, *_ in cps:
            cp.start()
        for cp, peer, m in cps:
            pltpu.make_async_remote_copy(src_ref=v_ref, dst_ref=land_ref.at[peer], send_sem=send.at[m - 1],
                                         recv_sem=recv.at[m - 1], device_id=(x, y, c), device_id_type=MESH).wait_recv()
        for cp, *_ in cps:
            cp.wait_send()
        acc = land_ref[0]
        for k in range(1, N_DEV):
            acc = acc + land_ref[k]
        o_ref[...] = acc

    vm = pl.BlockSpec(memory_space=pltpu.VMEM)
    return pl.pallas_call(
        body, name=name, in_specs=[vm, ANY], out_specs=vm, out_shape=jax.ShapeDtypeStruct(v.shape, f32),
        scratch_shapes=[pltpu.VMEM((N_DEV, R, LANE), f32), pltpu.SemaphoreType.DMA((N_DEV - 1,)),
                        pltpu.SemaphoreType.DMA((N_DEV - 1,))],
        compiler_params=pltpu.CompilerParams(vmem_limit_bytes=VMEM_LIMIT),
    )(v, after)


def _pack_small(arrs, mult=2 * SUB):
    flat = jnp.concatenate([a.reshape(-1) for a in arrs])
    rows = -(-flat.shape[0] // (LANE * mult)) * mult
    return jnp.pad(flat, (0, rows * LANE - flat.shape[0])).reshape(rows, LANE)


def _unpack_small(vec, shapes):
    flat, out, o = vec.reshape(-1), [], 0
    for s in shapes:
        n = int(np.prod(s))
        out.append(flat[o:o + n].reshape(s))
        o += n
    return out


REPL_SMALL = ("c_ctx", "b_mod", "q_norm", "k_norm", "c_norm", "d_conv_b", "d_norm_g", "d_norm_b", "ln_g", "ln_b")
SHARD_SMALL = ("b_conv", "c_gate_w2", "c_gate_b", "d_conv_w")
BIG = ("w_mod", "w_in", "w_br", "w_out")
ORDER = ("c_ctx", "w_mod", "b_mod", "w_in", "q_norm", "k_norm", "b_conv", "c_gate_w2", "c_gate_b", "c_norm", "d_conv_w",
         "d_conv_b", "d_norm_g", "d_norm_b", "w_br", "w_out", "ln_g", "ln_b")


def kernel(x, c, ctx, c_ctx, w_mod, b_mod, w_in, q_norm, k_norm, b_conv, c_gate_w2, c_gate_b, c_norm, d_conv_w, d_conv_b, d_norm_g, d_norm_b, w_br, w_out, ln_g, ln_b, loss_target, m_c_ctx, m_w_mod, m_b_mod, m_w_in, m_q_norm, m_k_norm, m_b_conv, m_c_gate_w2, m_c_gate_b, m_c_norm, m_d_conv_w, m_d_conv_b, m_d_norm_g, m_d_norm_b, m_w_br, m_w_out, m_ln_g, m_ln_b, v_c_ctx, v_w_mod, v_b_mod, v_w_in, v_q_norm, v_k_norm, v_b_conv, v_c_gate_w2, v_c_gate_b, v_c_norm, v_d_conv_w, v_d_conv_b, v_d_norm_g, v_d_norm_b, v_w_br, v_w_out, v_ln_g, v_ln_b):
    W = dict(c_ctx=c_ctx, w_mod=w_mod, b_mod=b_mod, w_in=w_in, q_norm=q_norm, k_norm=k_norm, b_conv=b_conv,
             c_gate_w2=c_gate_w2, c_gate_b=c_gate_b, c_norm=c_norm, d_conv_w=d_conv_w, d_conv_b=d_conv_b,
             d_norm_g=d_norm_g, d_norm_b=d_norm_b, w_br=w_br, w_out=w_out, ln_g=ln_g, ln_b=ln_b)
    M = dict(c_ctx=m_c_ctx, w_mod=m_w_mod, b_mod=m_b_mod, w_in=m_w_in, q_norm=m_q_norm, k_norm=m_k_norm, b_conv=m_b_conv,
             c_gate_w2=m_c_gate_w2, c_gate_b=m_c_gate_b, c_norm=m_c_norm, d_conv_w=m_d_conv_w, d_conv_b=m_d_conv_b,
             d_norm_g=m_d_norm_g, d_norm_b=m_d_norm_b, w_br=m_w_br, w_out=m_w_out, ln_g=m_ln_g, ln_b=m_ln_b)
    V = dict(c_ctx=v_c_ctx, w_mod=v_w_mod, b_mod=v_b_mod, w_in=v_w_in, q_norm=v_q_norm, k_norm=v_k_norm, b_conv=v_b_conv,
             c_gate_w2=v_c_gate_w2, c_gate_b=v_c_gate_b, c_norm=v_c_norm, d_conv_w=v_d_conv_w, d_conv_b=v_d_conv_b,
             d_norm_g=v_d_norm_g, d_norm_b=v_d_norm_b, w_br=v_w_br, w_out=v_w_out, ln_g=v_ln_g, ln_b=v_ln_b)
    chip = 2 * lax.axis_index("x") + lax.axis_index("y")
    cidx = lax.axis_index("c").astype(jnp.int32).reshape(1)

    place = jnp.stack([chip, lax.axis_index("c")]).astype(jnp.int32)

    AXIS = dict(w_in=1, w_mod=0, w_br=0, w_out=0)
    ex = dict(w_in=lambda a: jnp.swapaxes(a, 1, 2), w_mod=lambda a: a.reshape(1, DEPTH * D, -1),
              w_br=lambda a: a.reshape(DEPTH, 4 * BRW, -1), w_out=lambda a: a)
    Wx, Mx, Vx = ({k: ex[k](P_[k]) for k in BIG} for P_ in (W, M, V))

    LAYER, MERGE = ("w_in", "w_br", "w_out"), ("w_br", "w_out")
    small_shard = _pack_small([W[k] for k in SHARD_SMALL])
    keys0 = ("w_in", "w_mod")
    sent = lambda k, l: (w_mod[l] if k == "w_mod" else Wx[k][l]).astype(bf16)
    got = _all_gather([sent(k, 0) for k in keys0] + [small_shard], [AXIS[k] for k in keys0] + [0], "all_gather0")
    smalls = [_unpack_small(got[-1][s], [W[k].shape for k in SHARD_SMALL]) for s in range(N_CHIPS)]
    full = {k: jnp.concatenate([smalls[s][i] for s in range(N_CHIPS)], axis=-1) for i, k in enumerate(SHARD_SMALL)}
    ag0b = _split_start([sent(k, 0) for k in MERGE], True, got[0], "all_gather0b_start")
    ag1 = _split_start([sent(k, 1) for k in keys0], True, ag0b[4], "all_gather1_start")
    ag1b = _split_start([sent(k, 1) for k in MERGE], True, ag1[4], "all_gather1b_start")

    def merge_form(w_br4, w_out4):
        return jnp.moveaxis(w_br4.reshape(N_CHIPS, 4, BRW, D // N_CHIPS), 0, 2).reshape(4, BRW, D), w_out4.reshape(D, D)

    def weights_of(l, h):
        first = got if l == 0 else _split_wait(*ag1[:4], True, h, "all_gather1_wait")[1]
        flight = (ag0b, ag1b)[l]
        return (_group_weights(first[0]),
                lambda after: merge_form(*_split_wait(*flight[:4], True, after, f"all_gather{l}b_wait")[1]), first[1])

    red = {k: Wx[k].shape for k in BIG}
    flights, held = {}, {}

    def to_sibling(tag, l, pieces):
        keys = list(pieces)
        halves = _split_start([pieces[k] for k in keys], False, jnp.zeros((SUB, LANE), f32), f"rs_sibling_halves{tag}_start",
                              axes=[AXIS[k] for k in keys])
        flights[tag] = (l, keys, halves)
        return halves[4]

    def launch(tag, after):
        l, keys, halves = flights[tag]
        axes = [AXIS[k] for k in keys]
        pieces, land_a = _split_wait(*halves[:4], False, after, f"rs_sibling_halves{tag}_wait", axes=axes)
        pair = [_add_half(p, la, cidx, ax, f"rs_pair_sum{tag}_{k}") for k, p, la, ax in zip(keys, pieces, land_a, axes)]
        flights[tag] = (l, keys, _split_start(pair, False, jnp.zeros((SUB, LANE), f32), f"rs_chip_exchange{tag}_start"))
        return flights[tag][2][4]

    def land(tag, after):
        l, keys, flight = flights.pop(tag)
        pair, land_b = _split_wait(*flight[:4], False, after, f"rs_chip_exchange{tag}_wait")
        for k, lb, pr in zip(keys, land_b, pair):
            red[k] = _sum_chips(lb, pr, place, AXIS[k], l, red[k], f"rs_chip_sum{tag}_{k}")

    def grads_done(l, gl):
        if "wp" in gl:
            pieces = dict(w_in=_ungroup(gl["wp"]).reshape(N_CHIPS, SHARD, D))
            if l == 0:
                return launch("0b", gl["wp"]["A"]) + launch("0c", to_sibling("0c", 0, pieces))
            return to_sibling("1", 1, {**pieces, **held.pop(1)})
        pieces = dict(w_br=gl["w_br"].reshape(N_CHIPS, 4 * BRW, D // N_CHIPS), w_out=gl["w_out"].reshape(N_CHIPS, D // N_CHIPS, D))
        if l == 0:
            return launch("1", gl["w_out"]) + to_sibling("0b", 0, pieces)
        held[1] = pieces
        return None

    loss, gx, g = _local_step(
        x[0], c, ctx[0], loss_target[0], c_ctx, b_mod, weights_of, q_norm, k_norm, full["b_conv"],
        full["c_gate_w2"], full["c_gate_b"], c_norm, full["d_conv_w"], d_conv_b, d_norm_g, d_norm_b,
        grads_done, ln_g, ln_b, tm=256, token=ag1b[4])
    g["c_gate_w2"], g["c_gate_b"] = g.pop("w2"), g.pop("gb")
    loss = lax.psum(loss, ("x", "y", "c"))

    w_mod_pieces = g["w_mod"].reshape(N_CHIPS, DEPTH * D, 3 * D // N_CHIPS)
    g = {k: (jnp.stack(v) if isinstance(v, list) else v) for k, v in g.items() if k not in ("wp", "w_br", "w_out", "w_mod")}

    small_names = REPL_SMALL + SHARD_SMALL
    gs = _all_reduce_small(_pack_small([g[k] for k in small_names]), "all_reduce_small",
                           to_sibling("0d", 0, {"w_mod": w_mod_pieces}))
    gsm = dict(zip(small_names, _unpack_small(gs, [g[k].shape for k in small_names])))
    for k in SHARD_SMALL:
        wdt = W[k].shape[-1]
        gsm[k] = lax.dynamic_slice_in_dim(gsm[k], chip * wdt, wdt, axis=gsm[k].ndim - 1)

    grad, delta, new_m, new_v = {}, {}, {}, {}

    def adamw_big(keys, after):
        filled = _sibling_fill([red[k] for k in keys], [AXIS[k] for k in keys], "rs_sibling_fill_" + keys[0])
        for k, r in zip(keys, filled):
            back = (lambda a: jnp.swapaxes(a, 1, 2)) if k == "w_in" else (lambda a: a.reshape(W[k].shape))
            g_, d_, m_, v_ = _adamw(Wx[k], r, Mx[k], Vx[k], f"adamw_{k}", after=after)
            grad[k], delta[k], new_m[k], new_v[k] = back(g_), back(d_), back(m_), back(v_)
        return d_

    token = launch("0d", gs)
    land("1", gx)
    land("0b", gx)
    last = adamw_big(MERGE, token)
    shapes = [W[k].shape for k in small_names]
    _, d_, m_, v_ = _adamw(*[_pack_small([P_[k] for k in small_names])[None] for P_ in (W, gsm, M, V)], "adamw_small", after=last)
    for k, dd, mm_, vv in zip(small_names, _unpack_small(d_, shapes), _unpack_small(m_, shapes), _unpack_small(v_, shapes)):
        grad[k], delta[k], new_m[k], new_v[k] = gsm[k], dd, mm_, vv
    land("0c", d_)
    last = adamw_big(("w_in",), None)
    land("0d", last)
    adamw_big(("w_mod",), None)

    return (loss, gx[None], *[grad[k] for k in ORDER], *[delta[k] for k in ORDER], *[new_m[k] for k in ORDER],
            *[new_v[k] for k in ORDER])
```

```python
import functools

import jax
import jax.numpy as jnp
import numpy as np
from jax import lax
from jax.experimental import pallas as pl
from jax.experimental.pallas import tpu as pltpu

f32 = jnp.float32
bf16 = jnp.bfloat16

D = 1024
DEPTH = 2
GRID_W = 64
BRW = 512
HD = 128
A_HEADS = 4
C_HEADS = 4
C_KW = 256
C_RANK = 16
C_TAU = 16.0
CH = 128
KB = 3
KD = 31
ALPHA = (2 * DEPTH) ** 0.25
EPS = 1e-6
ROPE_THETA = 10000.0
N_IN = 10784
LR, B1, B2, AEPS, WD, STEP = 0.001, 0.9, 0.999, 1e-08, 0.01, 10

W_M, W_A, W_C, W_G = 4 * D + 4 * BRW, 1024, 5 * BRW, 1152
GROUPS = ("M", "A", "C", "G")
M_GA, M_GB, M_GC, M_GD = 4 * D, 4 * D + BRW, 4 * D + 2 * BRW, 4 * D + 3 * BRW
A_K, A_V = 512, 768
G_K, G_V, G_R = 256, 512, 1024
S_Q, S_GA, S_B, S_C, S_X, S_GB, S_CQ, S_CV, S_GC, S_R, S_DA, S_DG, S_GD, S_MG = (
    0, 1024, 1536, 2048, 2560, 3072, 3584, 4096, 4608, 5120, 5152, 5664, 6176, 6688)

LANE = 128
SUB = 8
VMEM_LIMIT = 56 * 1024 * 1024
CONV_PAD = 16
GLA_SUB = 16
GLA_CLAMP = 60.0


def _cparams(sem, vmem=VMEM_LIMIT):
    return pltpu.CompilerParams(dimension_semantics=sem, vmem_limit_bytes=vmem)


def _dg(a, b, ca, cb):
    return lax.dot_general(a.astype(bf16), b.astype(bf16), (((ca,), (cb,)), ((), ())),
                           preferred_element_type=f32)


@jax.custom_vjp
def mm(a, b):
    return _dg(a, b, 1, 0)


mm.defvjp(lambda a, b: (_dg(a, b, 1, 0), (a, b)),
          lambda r, ct: (_dg(ct, r[1], 1, 1).astype(r[0].dtype), _dg(r[0], ct, 0, 0).astype(r[1].dtype)))


@jax.custom_vjp
def mm_nt(a, b):
    return _dg(a, b, 1, 1)


mm_nt.defvjp(lambda a, b: (_dg(a, b, 1, 1), (a, b)),
             lambda r, ct: (_dg(ct, r[1], 1, 0).astype(r[0].dtype), _dg(ct, r[0], 0, 0).astype(r[1].dtype)))


@jax.custom_vjp
def mm_tn(a, b):
    return _dg(a, b, 0, 0)


mm_tn.defvjp(lambda a, b: (_dg(a, b, 0, 0), (a, b)),
             lambda r, ct: (_dg(r[1], ct, 1, 1).astype(r[0].dtype), _dg(r[0], ct, 1, 0).astype(r[1].dtype)))


@jax.custom_vjp
def _sigmoid(x):
    return 0.5 * jnp.tanh(0.5 * x) + 0.5


def _sigmoid_fwd(x):
    s = _sigmoid(x)
    return s, s


_sigmoid.defvjp(_sigmoid_fwd, lambda s, ct: (ct * (s - s * s),))


@jax.custom_vjp
def _silu(x):
    return x * _sigmoid(x)


def _silu_fwd(x):
    s = _sigmoid(x)
    return x * s, (x, s)


_silu.defvjp(_silu_fwd, lambda r, ct: (ct * (r[1] + r[0] * (r[1] - r[1] * r[1])),))


def _ln(x):
    mu = jnp.mean(x, -1, keepdims=True)
    xc = x - mu
    var = jnp.mean(xc * xc, -1, keepdims=True)
    return xc * lax.rsqrt(var + EPS)


def _rms(x, g):
    return x * lax.rsqrt(jnp.mean(x * x, -1, keepdims=True) + EPS) * g


@jax.custom_vjp
def _rope(x, cos_f, sin_a, sin_b):
    return x * cos_f + pltpu.roll(x, HD - 1, 1) * sin_a + pltpu.roll(x, 1, 1) * sin_b


def _rope_fwd(x, cos_f, sin_a, sin_b):
    return _rope(x, cos_f, sin_a, sin_b), (cos_f, sin_a, sin_b)


def _rope_bwd(r, ct):
    cos_f, sin_a, sin_b = r
    dx = ct * cos_f + pltpu.roll(ct * sin_a, 1, 1) + pltpu.roll(ct * sin_b, HD - 1, 1)
    return dx, jnp.zeros_like(cos_f), jnp.zeros_like(sin_a), jnp.zeros_like(sin_b)


_rope.defvjp(_rope_fwd, _rope_bwd)


def _row_ids(i, tm):
    return i * tm + lax.broadcasted_iota(jnp.int32, (tm, 1), 0)


def _partial_rows(ref, rows):
    n = len(rows)
    for k, r in enumerate(rows):
        ref[k:k + 1, :] = r
    ref[n:SUB, :] = jnp.zeros((SUB - n, ref.shape[-1]), f32)


def _matmul(a, b, mode, tm, tn, tk, name, out_dtype=f32, add=None, after=None):
    sect = a.ndim == 3
    a2 = (a.shape[1], a.shape[0] * a.shape[2]) if sect else a.shape
    if mode == "nn":
        (M, K), N = a2, b.shape[1]
        a_spec = pl.BlockSpec((None, tm, tk), lambda j, i, k: (k, i, 0)) if sect else pl.BlockSpec((tm, tk), lambda j, i, k: (i, k))
        b_spec = pl.BlockSpec((tk, tn), lambda j, i, k: (k, j))
        ca, cb = 1, 0
        assert not sect or tk == a.shape[2]
    elif mode == "nt":
        (M, K), N = a2, b.shape[0]
        assert not sect
        a_spec = pl.BlockSpec((tm, tk), lambda j, i, k: (i, k))
        b_spec = pl.BlockSpec((tn, tk), lambda j, i, k: (j, k))
        ca, cb = 1, 1
    else:
        (K, M), N = a2, b.shape[1]
        a_spec = pl.BlockSpec((None, tk, tm), lambda j, i, k: (i, k, 0)) if sect else pl.BlockSpec((tk, tm), lambda j, i, k: (k, i))
        b_spec = pl.BlockSpec((tk, tn), lambda j, i, k: (k, j))
        ca, cb = 0, 0
        assert not sect or tm == a.shape[2]
    assert M % tm == 0 and N % tn == 0 and K % tk == 0, (name, M, N, K, tm, tn, tk)
    nk = K // tk

    o_spec = pl.BlockSpec((tm, tn), lambda j, i, k: (i, j))

    def body(a_ref, b_ref, *rest):
        add_ref = rest[0] if add is not None else None
        o_ref, acc_ref = rest[-2:]
        k = pl.program_id(2)
        part = _dg(a_ref[...], b_ref[...], ca, cb)

        @pl.when(k == 0)
        def _():
            acc_ref[...] = part if add_ref is None else part + add_ref[...]

        @pl.when(k > 0)
        def _():
            acc_ref[...] += part

        @pl.when(k == nk - 1)
        def _():
            o_ref[...] = acc_ref[...].astype(o_ref.dtype)

    extra = ([] if add is None else [(o_spec, add)]) + ([] if after is None else [(pl.BlockSpec(memory_space=pl.ANY), after)])
    return pl.pallas_call(
        body, name=name, grid=(N // tn, M // tm, nk),
        in_specs=[a_spec, b_spec] + [s_ for s_, _ in extra], out_specs=o_spec,
        out_shape=jax.ShapeDtypeStruct((M, N), out_dtype),
        scratch_shapes=[pltpu.VMEM((tm, tn), f32)],
        compiler_params=_cparams(("parallel", "parallel", "arbitrary")),
    )(a, b, *[v_ for _, v_ in extra])


def _matmul_groups(a, b, tks, tm, name, after=None):
    keys = list(a)
    M = a[keys[0]].shape[-2]
    N = b[keys[0]].shape[1]
    count = {g: b[g].shape[0] // tks[g] for g in keys}
    first, total = {}, 0
    for g in keys:
        first[g], total = total, total + count[g]

    def k_of(g):
        return lambda s: jnp.clip(s - first[g], 0, count[g] - 1)

    a_specs = [pl.BlockSpec((None, tm, tks[g]), functools.partial(lambda i, s, kk: (kk(s), i, 0), kk=k_of(g)))
               if a[g].ndim == 3 else pl.BlockSpec((tm, tks[g]), functools.partial(lambda i, s, kk: (i, kk(s)), kk=k_of(g)))
               for g in keys]
    b_specs = [pl.BlockSpec((tks[g], N), functools.partial(lambda i, s, kk: (kk(s), 0), kk=k_of(g))) for g in keys]
    n = len(keys)

    def body(*refs):
        o_ref, acc_ref = refs[-2:]
        s = pl.program_id(1)

        @pl.when(s == 0)
        def _():
            acc_ref[...] = jnp.zeros_like(acc_ref)

        for j, g in enumerate(keys):
            @pl.when((s >= first[g]) & (s < first[g] + count[g]))
            def _(j=j):
                acc_ref[...] += _dg(refs[j][...], refs[n + j][...], 1, 0)

        @pl.when(s == total - 1)
        def _():
            o_ref[...] = acc_ref[...]

    extra = [] if after is None else [after]
    return pl.pallas_call(
        body, name=name, grid=(M // tm, total),
        in_specs=a_specs + b_specs + [pl.BlockSpec(memory_space=pl.ANY)] * len(extra),
        out_specs=pl.BlockSpec((tm, N), lambda i, s: (i, 0)),
        out_shape=jax.ShapeDtypeStruct((M, N), f32),
        scratch_shapes=[pltpu.VMEM((tm, N), f32)],
        compiler_params=_cparams(("parallel", "arbitrary")),
    )(*[a[g] for g in keys], *[b[g] for g in keys], *extra)


def _matmul_tn_batched(a, b, ns, name):
    B, K, M = a.shape
    N = b.shape[2] // ns

    def body(a_ref, b_ref, o_ref):
        o_ref[...] = _dg(a_ref[...], b_ref[...], 0, 0).astype(bf16)

    return pl.pallas_call(
        body, name=name, grid=(B, ns),
        in_specs=[pl.BlockSpec((None, K, M), lambda i, s: (i, 0, 0)), pl.BlockSpec((None, K, N), lambda i, s: (i, 0, s))],
        out_specs=pl.BlockSpec((None, None, M, N), lambda i, s: (s, i, 0, 0)),
        out_shape=jax.ShapeDtypeStruct((ns, B, M, N), bf16),
        compiler_params=_cparams(("parallel", "parallel")),
    )(a, b)


MOD_TN = 768


def _mod_fwd(cin, w_mod_l, b_mod_l, name):
    def body(c_ref, w_ref, b_ref, o_ref):
        o_ref[...] = mm(_silu(c_ref[...]), w_ref[...]) + b_ref[...]

    return pl.pallas_call(
        body, name=name, grid=(3 * D // MOD_TN,),
        in_specs=[pl.BlockSpec((SUB, D), lambda j: (0, 0)), pl.BlockSpec((None, D, MOD_TN), lambda j: (j, 0, 0)),
                  pl.BlockSpec((1, MOD_TN), lambda j: (0, j))],
        out_specs=pl.BlockSpec((SUB, MOD_TN), lambda j: (0, j)),
        out_shape=jax.ShapeDtypeStruct((SUB, 3 * D), f32),
        compiler_params=_cparams(("parallel",)),
    )(cin, w_mod_l, b_mod_l[None, :])


def _mod_bwd(cin, w_mods, dmodv):
    nj = 3 * D // MOD_TN

    def body(c_ref, *refs):
        g_ref, dw_ref, dc_ref = refs[DEPTH:]
        w = refs[0][...]
        for l in range(1, DEPTH):
            w = jnp.where(pl.program_id(0) == l, refs[l][...], w)
        _, vjp = jax.vjp(lambda c, w: mm(_silu(c), w), c_ref[...], w.astype(f32))
        dc, dw = vjp(g_ref[...])
        dw_ref[...] = dw.astype(bf16)
        dc_ref[...] = dc

    return pl.pallas_call(
        body, name="mod_bwd", grid=(DEPTH, nj),
        in_specs=[pl.BlockSpec((SUB, D), lambda l, j: (0, 0))]
        + [pl.BlockSpec((None, D, MOD_TN), lambda l, j: (j, 0, 0))] * DEPTH
        + [pl.BlockSpec((None, SUB, MOD_TN), lambda l, j: (l, 0, j))],
        out_specs=[pl.BlockSpec((None, None, D, MOD_TN), lambda l, j: (j, l, 0, 0)),
                   pl.BlockSpec((None, None, SUB, D), lambda l, j: (l, j, 0, 0))],
        out_shape=[jax.ShapeDtypeStruct((nj, DEPTH, D, MOD_TN), bf16),
                   jax.ShapeDtypeStruct((DEPTH, nj, SUB, D), f32)],
        compiler_params=_cparams(("parallel", "parallel")),
    )(cin, *w_mods, dmodv)


def _u_fn(h, m_l, m_c, isctx):
    n = _ln(h)
    shift = jnp.where(isctx, m_c[:, 0:D], m_l[:, 0:D])
    scale = jnp.where(isctx, m_c[:, D:2 * D], m_l[:, D:2 * D])
    return n * (1.0 + scale) + shift


def _ln_fwd(h, modv_l, tc, tm, name):
    T = h.shape[0]

    def body(h_ref, m_ref, u_ref):
        isctx = _row_ids(pl.program_id(0), tm) < tc
        u_ref[...] = _u_fn(h_ref[...], m_ref[0:1, :], m_ref[1:2, :], isctx).astype(bf16)

    return pl.pallas_call(
        body, name=name, grid=(T // tm,),
        in_specs=[pl.BlockSpec((tm, D), lambda i: (i, 0)), pl.BlockSpec((SUB, 3 * D), lambda i: (0, 0))],
        out_specs=pl.BlockSpec((tm, D), lambda i: (i, 0)),
        out_shape=jax.ShapeDtypeStruct((T, D), bf16),
        compiler_params=_cparams(("parallel",)),
    )(h, modv_l)


def _ln_bwd(du, h, dh_res, modv_l, tc, tm, name, latent_only=False):
    T = h.shape[0]
    nt, nct = T // tm, tc // tm

    def body(du_ref, h_ref, r_ref, m_ref, dh_ref, dm_ref):
        isctx = _row_ids(pl.program_id(0), tm) < tc
        _, vjp = jax.vjp(lambda h, ml, mc: _u_fn(h, ml, mc, isctx), h_ref[...], m_ref[0:1, :], m_ref[1:2, :])
        dh, dml, dmc = vjp(du_ref[...])
        dh_ref[...] = dh + r_ref[...]
        _partial_rows(dm_ref, [dml, dmc])

    dh_map = (lambda i: (jnp.maximum(i - nct, 0), 0)) if latent_only else (lambda i: (i, 0))
    return pl.pallas_call(
        body, name=name, grid=(nt,),
        in_specs=[pl.BlockSpec((tm, D), lambda i: (i, 0)), pl.BlockSpec((tm, D), lambda i: (i, 0)),
                  pl.BlockSpec((tm, D), lambda i: (i, 0)), pl.BlockSpec((SUB, 3 * D), lambda i: (0, 0))],
        out_specs=[pl.BlockSpec((tm, D), dh_map), pl.BlockSpec((None, SUB, 3 * D), lambda i: (i, 0, 0))],
        out_shape=[jax.ShapeDtypeStruct((T - tc if latent_only else T, D), f32), jax.ShapeDtypeStruct((nt, SUB, 3 * D), f32)],
        compiler_params=_cparams(("arbitrary",)),
    )(du, h, dh_res, modv_l)


def _prep_fn(q, k, qg, kg, cos_f, sin_a, sin_b):
    qs = [_rope(_rms(q[:, HD * i:HD * (i + 1)], qg), cos_f, sin_a, sin_b) * (HD ** -0.5) for i in range(A_HEADS)]
    ks = [_rope(_rms(k[:, HD * i:HD * (i + 1)], kg), cos_f, sin_a, sin_b) for i in range(A_HEADS // 2)]
    return jnp.concatenate(qs, 1), jnp.concatenate(ks, 1)


def _tok(tm, w, off):
    return pl.BlockSpec((tm, w), lambda i: (i, off // w))


def _vec(w):
    return pl.BlockSpec((1, w), lambda i: (0, 0))


def _prep_fwd(P, qg, kg, rope, tm, name):
    T = P.shape[0]

    def body(q_ref, k_ref, v_ref, qg_ref, kg_ref, c_ref, sa_ref, sb_ref, qn_ref, kn_ref, vb_ref):
        qn, kn = _prep_fn(q_ref[...].astype(f32), k_ref[...].astype(f32), qg_ref[...], kg_ref[...], c_ref[...], sa_ref[...],
                          sb_ref[...])
        qn_ref[...] = qn.astype(bf16)
        kn_ref[...] = kn.astype(bf16)
        vb_ref[...] = v_ref[...].astype(bf16)

    return pl.pallas_call(
        body, name=name, grid=(T // tm,),
        in_specs=[_tok(tm, 512, 0), _tok(tm, 256, A_K), _tok(tm, 256, A_V), _vec(HD), _vec(HD),
                  _tok(tm, HD, 0), _tok(tm, HD, 0), _tok(tm, HD, 0)],
        out_specs=[_tok(tm, 512, 0), _tok(tm, 256, 0), _tok(tm, 256, 0)],
        out_shape=[jax.ShapeDtypeStruct((T, 512), bf16), jax.ShapeDtypeStruct((T, 256), bf16),
                   jax.ShapeDtypeStruct((T, 256), bf16)],
        compiler_params=_cparams(("parallel",)),
    )(P, P, P, qg, kg, *rope)


def _prep_bwd(P, dqn, dkn, dv, qg, kg, rope, tm, name):
    T = P.shape[0]
    nt = T // tm

    def body(q_ref, k_ref, dq_ref, dk_ref, dv_ref, qg_ref, kg_ref, c_ref, sa_ref, sb_ref, o_ref, og_ref):
        tabs = (c_ref[...], sa_ref[...], sb_ref[...])
        _, vjp = jax.vjp(lambda q, k, a, b: _prep_fn(q, k, a, b, *tabs), q_ref[...].astype(f32), k_ref[...].astype(f32),
                         qg_ref[...], kg_ref[...])
        dq, dk, dqg, dkg = vjp((dq_ref[...], dk_ref[...]))
        o_ref[:, 0:A_K] = dq.astype(bf16)
        o_ref[:, A_K:A_V] = dk.astype(bf16)
        o_ref[:, A_V:W_A] = dv_ref[...].astype(bf16)
        _partial_rows(og_ref, [dqg, dkg])

    return pl.pallas_call(
        body, name=name, grid=(nt,),
        in_specs=[_tok(tm, 512, 0), _tok(tm, 256, A_K), _tok(tm, 512, 0), _tok(tm, 256, 0), _tok(tm, 256, 0),
                  _vec(HD), _vec(HD), _tok(tm, HD, 0), _tok(tm, HD, 0), _tok(tm, HD, 0)],
        out_specs=[_tok(tm, W_A, 0), pl.BlockSpec((None, SUB, HD), lambda i: (i, 0, 0))],
        out_shape=[jax.ShapeDtypeStruct((T, W_A), bf16), jax.ShapeDtypeStruct((nt, SUB, HD), f32)],
        compiler_params=_cparams(("parallel",)),
    )(P, P, dqn, dkn, dv, qg, kg, *rope)


def _attn_fn(q, k, v, lim):
    col = lax.broadcasted_iota(jnp.int32, (1, k.shape[0]), 1)
    s = mm_nt(q, k) + jnp.where(col < lim, 0.0, -1e30)
    m = lax.stop_gradient(jnp.max(s, -1, keepdims=True))
    e = jnp.exp(s - m)
    p = e * (1.0 / jnp.sum(e, -1, keepdims=True))
    return mm(p, v)


def _attn_fwd(qn, kn, vb, tc, tq, name):
    T = qn.shape[0]

    def body(q_ref, k_ref, v_ref, o_ref):
        lim = jnp.where(pl.program_id(1) * tq < tc, tc, T)
        o_ref[...] = _attn_fn(q_ref[...], k_ref[...], v_ref[...], lim)

    return pl.pallas_call(
        body, name=name, grid=(A_HEADS, T // tq),
        in_specs=[pl.BlockSpec((tq, HD), lambda h, i: (i, h)), pl.BlockSpec((T, HD), lambda h, i: (0, h // 2)),
                  pl.BlockSpec((T, HD), lambda h, i: (0, h // 2))],
        out_specs=pl.BlockSpec((tq, HD), lambda h, i: (i, h)),
        out_shape=jax.ShapeDtypeStruct((T, 512), f32),
        compiler_params=_cparams(("parallel", "parallel")),
    )(qn, kn, vb)


def _attn_bwd(qn, kn, vb, dya, tc, tq, name):
    T = qn.shape[0]

    def body(q_ref, k_ref, v_ref, g_ref, dq_ref, dk_ref, dv_ref):
        first = (pl.program_id(1) == 0) & (pl.program_id(2) == 0)
        lim = jnp.where(pl.program_id(2) * tq < tc, tc, T)
        _, vjp = jax.vjp(lambda q, k, v: _attn_fn(q, k, v, lim), q_ref[...].astype(f32), k_ref[...].astype(f32),
                         v_ref[...].astype(f32))
        dq, dk, dv = vjp(g_ref[...])
        dq_ref[...] = dq

        @pl.when(first)
        def _():
            dk_ref[...] = dk
            dv_ref[...] = dv

        @pl.when(jnp.logical_not(first))
        def _():
            dk_ref[...] += dk
            dv_ref[...] += dv

    qspec = pl.BlockSpec((tq, HD), lambda kv, g, i: (i, 2 * kv + g))
    kspec = pl.BlockSpec((T, HD), lambda kv, g, i: (0, kv))
    return pl.pallas_call(
        body, name=name, grid=(A_HEADS // 2, 2, T // tq),
        in_specs=[qspec, kspec, kspec, qspec], out_specs=[qspec, kspec, kspec],
        out_shape=[jax.ShapeDtypeStruct((T, 512), f32), jax.ShapeDtypeStruct((T, 256), f32),
                   jax.ShapeDtypeStruct((T, 256), f32)],
        compiler_params=_cparams(("parallel", "arbitrary", "arbitrary")),
    )(qn, kn, vb, dya)


def _conv_rows(tc, tl):
    return CONV_PAD + tc + CONV_PAD + tl + CONV_PAD


def _fill_pad(pad_ref, val, tc, tl):
    z = jnp.zeros((CONV_PAD, LANE), f32)
    pad_ref[0:CONV_PAD, :] = z
    pad_ref[CONV_PAD:CONV_PAD + tc, :] = val[0:tc]
    pad_ref[CONV_PAD + tc:2 * CONV_PAD + tc, :] = z
    pad_ref[2 * CONV_PAD + tc:2 * CONV_PAD + tc + tl, :] = val[tc:tc + tl]
    pad_ref[2 * CONV_PAD + tc + tl:3 * CONV_PAD + tc + tl, :] = z


def _conv_apply(pad_ref, w_ref, K, tc, tl, rc, emit, flip=False):
    half = K // 2
    for seg0, off, n in ((0, CONV_PAD, tc), (tc, 2 * CONV_PAD + tc, tl)):
        for r0 in range(0, n, rc):
            acc = None
            for k in range(K):
                sh = (half - k) if flip else (k - half)
                term = pad_ref[pl.ds(off + r0 + sh, rc), :] * w_ref[k:k + 1, :]
                acc = term if acc is None else acc + term
            emit(seg0 + r0, acc)


def _conv_wgrad(pad_ref, dy_ref, K, tc, tl, rc, dw_ref):
    half = K // 2
    for k in range(K):
        acc = jnp.zeros((1, LANE), f32)
        for seg0, off, n in ((0, CONV_PAD, tc), (tc, 2 * CONV_PAD + tc, tl)):
            for r0 in range(0, n, rc):
                acc = acc + jnp.sum(pad_ref[pl.ds(off + r0 + k - half, rc), :] * dy_ref[pl.ds(seg0 + r0, rc), :],
                                    axis=0, keepdims=True)
        dw_ref[k:k + 1, :] = acc


def _col(T, off):
    return pl.BlockSpec((T, LANE), lambda j: (0, off // LANE + j))


C_B, C_C, C_X, C_A, C_G = range(5)
N_SEC = 5


class _Sections:
    def __init__(self, refs):
        self.refs = refs

    def __getitem__(self, idx):
        rows, sec = idx
        return self.refs[sec][rows, :].astype(f32)

    def __setitem__(self, idx, val):
        rows, sec = idx
        self.refs[sec, rows, :] = val


def _sec_specs(T):
    return [pl.BlockSpec((T, LANE), functools.partial(lambda j, s: (0, s * (BRW // LANE) + j), s=s)) for s in range(N_SEC)]


def _conv_fwd(P, wb, wd, bd, tc, tl, rc, name):
    T = tc + tl

    def body(*refs):
        p_ref = _Sections(refs[:N_SEC])
        wb_ref, wd_ref, bd_ref, yb_ref, hh_ref, pad_ref = refs[N_SEC:]
        _fill_pad(pad_ref, p_ref[:, C_C] * p_ref[:, C_X], tc, tl)

        def emit_b(r0, y):
            yb_ref[pl.ds(r0, rc), :] = y * p_ref[pl.ds(r0, rc), C_B]

        _conv_apply(pad_ref, wb_ref, KB, tc, tl, rc, emit_b)
        _fill_pad(pad_ref, p_ref[:, C_A] * _sigmoid(p_ref[:, C_G]), tc, tl)

        def emit_d(r0, y):
            hh_ref[pl.ds(r0, rc), :] = y + bd_ref[...]

        _conv_apply(pad_ref, wd_ref, KD, tc, tl, rc, emit_d)

    return pl.pallas_call(
        body, name=name, grid=(BRW // LANE,),
        in_specs=_sec_specs(T) + [pl.BlockSpec((KB, LANE), lambda j: (0, j)), pl.BlockSpec((KD, LANE), lambda j: (0, j)),
                                  pl.BlockSpec((1, LANE), lambda j: (0, j))],
        out_specs=[_col(T, 0), _col(T, 0)],
        out_shape=[jax.ShapeDtypeStruct((T, BRW), f32), jax.ShapeDtypeStruct((T, BRW), f32)],
        scratch_shapes=[pltpu.VMEM((_conv_rows(tc, tl), LANE), f32)],
        compiler_params=_cparams(("parallel",)),
    )(*[P] * N_SEC, wb, wd, bd)


def _conv_bwd(P, dyb, dhh, wb, wd, tc, tl, rc, name):
    T = tc + tl

    def body(*refs):
        p_ref = _Sections(refs[:N_SEC])
        dyb_ref, dhh_ref, wb_ref, wd_ref, dp3_ref, dwb_ref, dwd_ref, dbd_ref, pad_ref, pad2_ref, tmp_ref = refs[N_SEC:]
        dp_ref = _Sections(dp3_ref)
        _fill_pad(pad_ref, p_ref[:, C_C] * p_ref[:, C_X], tc, tl)

        def emit_cv(r0, y):
            dp_ref[pl.ds(r0, rc), C_B] = (y * dyb_ref[pl.ds(r0, rc), :]).astype(bf16)

        _conv_apply(pad_ref, wb_ref, KB, tc, tl, rc, emit_cv)
        tmp_ref[...] = dyb_ref[...] * p_ref[:, C_B]
        _conv_wgrad(pad_ref, tmp_ref, KB, tc, tl, rc, dwb_ref)
        _fill_pad(pad2_ref, tmp_ref[...], tc, tl)

        def emit_ds(r0, y):
            dp_ref[pl.ds(r0, rc), C_C] = (y * p_ref[pl.ds(r0, rc), C_X]).astype(bf16)
            dp_ref[pl.ds(r0, rc), C_X] = (y * p_ref[pl.ds(r0, rc), C_C]).astype(bf16)

        _conv_apply(pad2_ref, wb_ref, KB, tc, tl, rc, emit_ds, flip=True)
        _fill_pad(pad_ref, p_ref[:, C_A] * _sigmoid(p_ref[:, C_G]), tc, tl)
        _conv_wgrad(pad_ref, dhh_ref, KD, tc, tl, rc, dwd_ref)
        dbd_ref[...] = jnp.sum(dhh_ref[...], axis=0, keepdims=True)
        _fill_pad(pad2_ref, dhh_ref[...], tc, tl)

        def emit_d2(r0, y):
            sg = _sigmoid(p_ref[pl.ds(r0, rc), C_G])
            a = p_ref[pl.ds(r0, rc), C_A]
            dp_ref[pl.ds(r0, rc), C_A] = (y * sg).astype(bf16)
            dp_ref[pl.ds(r0, rc), C_G] = (y * a * sg * (1.0 - sg)).astype(bf16)

        _conv_apply(pad2_ref, wd_ref, KD, tc, tl, rc, emit_d2, flip=True)

    return pl.pallas_call(
        body, name=name, grid=(BRW // LANE,),
        in_specs=_sec_specs(T) + [_col(T, 0), _col(T, 0),
                                  pl.BlockSpec((KB, LANE), lambda j: (0, j)), pl.BlockSpec((KD, LANE), lambda j: (0, j))],
        out_specs=[pl.BlockSpec((N_SEC, T, LANE), lambda j: (0, 0, j)), pl.BlockSpec((KB, LANE), lambda j: (0, j)),
                   pl.BlockSpec((KD, LANE), lambda j: (0, j)), pl.BlockSpec((1, LANE), lambda j: (0, j))],
        out_shape=[jax.ShapeDtypeStruct((N_SEC, T, BRW), bf16), jax.ShapeDtypeStruct((KB, BRW), f32),
                   jax.ShapeDtypeStruct((KD, BRW), f32), jax.ShapeDtypeStruct((1, BRW), f32)],
        scratch_shapes=[pltpu.VMEM((_conv_rows(tc, tl), LANE), f32), pltpu.VMEM((_conv_rows(tc, tl), LANE), f32),
                        pltpu.VMEM((T, LANE), f32)],
        compiler_params=_cparams(("parallel",)),
    )(*[P] * N_SEC, dyb, dhh, wb, wd)


def _gla_chunk(q, k, v, r, w2, b2, st, isfwd):
    z = mm(r, w2) + b2
    g = jax.nn.log_sigmoid(z[:, 0:C_KW] if isfwd else z[:, C_KW:2 * C_KW]) / C_TAU
    ri = lax.broadcasted_iota(jnp.int32, (CH, CH), 0)
    ci = lax.broadcasted_iota(jnp.int32, (CH, CH), 1)
    tri = ((ci <= ri) if isfwd else (ci >= ri)).astype(f32)
    cum = jnp.dot(tri, g, preferred_element_type=f32, precision=lax.Precision.HIGHEST)
    last = jnp.sum(g, axis=0, keepdims=True)
    q = q * (C_KW // C_HEADS) ** -0.5
    hv = lax.broadcasted_iota(jnp.int32, (BRW, C_KW), 0) // (BRW // C_HEADS)
    hk = lax.broadcasted_iota(jnp.int32, (BRW, C_KW), 1) // (C_KW // C_HEADS)
    st_new = st * jnp.exp(last) + jnp.where(hv == hk, mm_tn(v, k * jnp.exp(last - cum)), 0.0)
    o = mm_nt(q * jnp.exp(cum), st)
    rowi = lax.broadcasted_iota(jnp.int32, (CH, C_KW), 0)
    srow = lax.broadcasted_iota(jnp.int32, (C_HEADS * CH, C_KW), 0)
    slane = lax.broadcasted_iota(jnp.int32, (C_HEADS * CH, C_KW), 1)
    own_lanes = srow // CH == slane // (C_KW // C_HEADS)
    pos = lax.broadcasted_iota(jnp.int32, (C_HEADS * CH, CH), 0) % CH
    key = lax.broadcasted_iota(jnp.int32, (C_HEADS * CH, CH), 1)
    scores = jnp.zeros((C_HEADS * CH, CH), f32)
    for a in range(CH // GLA_SUB):
        idx = GLA_SUB * a - 1 if isfwd else GLA_SUB * (a + 1)
        ref = jnp.sum(jnp.where(rowi == idx, cum, 0.0), axis=0, keepdims=True)
        qa = q * jnp.exp(jnp.minimum(cum - ref, 0.0))
        ka = k * jnp.exp(jnp.minimum(ref - cum, GLA_CLAMP))
        s = mm_nt(jnp.where(own_lanes, jnp.concatenate([qa] * C_HEADS, axis=0), 0.0), ka)
        scores = scores + jnp.where(pos // GLA_SUB == a, s, 0.0)
    scores = jnp.where((key <= pos) if isfwd else (key >= pos), scores, 0.0)
    vw = BRW // C_HEADS
    o = o + jnp.concatenate([mm(scores[CH * hd:CH * (hd + 1)], v[:, vw * hd:vw * (hd + 1)]) for hd in range(C_HEADS)],
                            axis=1)
    return o, st_new


def _gla_chunk_of(d, n, nc, nch):
    back = jnp.where(n < nc, nc - 1 - n, nch - 1 - (n - nc))
    return jnp.where(d == 0, n, back)


def _gla_fwd(P, w2, b2, tc, name):
    T = P.shape[0]
    nch, nc = T // CH, tc // CH

    back = lambda n: _gla_chunk_of(1, n, nc, nch)

    def body(pf_ref, pb_ref, w_ref, b_ref, of_ref, ob_ref, ssf_ref, ssb_ref, stf_ref, stb_ref):
        @pl.when(pl.program_id(0) == 0)
        def _():
            stf_ref[...] = jnp.zeros_like(stf_ref)
            stb_ref[...] = jnp.zeros_like(stb_ref)

        for p_ref, o_ref, ss_ref, st_ref, isfwd in ((pf_ref, of_ref, ssf_ref, stf_ref, True),
                                                    (pb_ref, ob_ref, ssb_ref, stb_ref, False)):
            st = st_ref[...]
            ss_ref[...] = st
            p = p_ref[...].astype(f32)
            o, st_new = _gla_chunk(p[:, 0:G_K], p[:, G_K:G_V], p[:, G_V:G_R], p[:, G_R:W_G], w_ref[...], b_ref[...], st, isfwd)
            o_ref[...] = o
            st_ref[...] = st_new

    sd = jax.ShapeDtypeStruct
    return pl.pallas_call(
        body, name=name, grid=(nch,),
        in_specs=[pl.BlockSpec((CH, W_G), lambda n: (n, 0)), pl.BlockSpec((CH, W_G), lambda n: (back(n), 0)),
                  pl.BlockSpec((LANE, 512), lambda n: (0, 0)), pl.BlockSpec((1, 512), lambda n: (0, 0))],
        out_specs=[pl.BlockSpec((CH, BRW), lambda n: (n, 0)), pl.BlockSpec((CH, BRW), lambda n: (back(n), 0)),
                   pl.BlockSpec((None, BRW, C_KW), lambda n: (n, 0, 0)), pl.BlockSpec((None, BRW, C_KW), lambda n: (n, 0, 0))],
        out_shape=[sd((T, BRW), f32), sd((T, BRW), f32), sd((nch, BRW, C_KW), f32), sd((nch, BRW, C_KW), f32)],
        scratch_shapes=[pltpu.VMEM((BRW, C_KW), f32), pltpu.VMEM((BRW, C_KW), f32)],
        compiler_params=_cparams(("arbitrary",)),
    )(P, P, w2, b2)


def _gla_bwd(P, w2, b2, ssave, doc, tc, name):
    T = P.shape[0]
    nch, nc = T // CH, tc // CH

    fwd_chunk = lambda m: nch - 1 - m
    back_chunk = lambda m: _gla_chunk_of(1, nch - 1 - m, nc, nch)

    def body(pf_ref, pb_ref, w_ref, b_ref, ssf_ref, ssb_ref, gf_ref, gb_ref, dpf_ref, dpb_ref, dw_ref, db_ref,
             dstf_ref, dstb_ref):
        m = pl.program_id(0)

        @pl.when(m == 0)
        def _():
            dstf_ref[...] = jnp.zeros_like(dstf_ref)
            dstb_ref[...] = jnp.zeros_like(dstb_ref)

        dw_sum, db_sum = None, None
        for p_ref, ss_ref, g_ref, dp_ref, dst_ref, isfwd in ((pf_ref, ssf_ref, gf_ref, dpf_ref, dstf_ref, True),
                                                             (pb_ref, ssb_ref, gb_ref, dpb_ref, dstb_ref, False)):
            p = p_ref[...].astype(f32)
            _, vjp = jax.vjp(lambda q, k, v, r, w, b, st: _gla_chunk(q, k, v, r, w, b, st, isfwd),
                             p[:, 0:G_K], p[:, G_K:G_V], p[:, G_V:G_R], p[:, G_R:W_G], w_ref[...], b_ref[...], ss_ref[...])
            dq, dk, dv, dr, dw, db, dst = vjp((g_ref[...], dst_ref[...]))
            dp_ref[:, 0:G_K] = dq
            dp_ref[:, G_K:G_V] = dk
            dp_ref[:, G_V:G_R] = dv
            dp_ref[:, G_R:W_G] = dr
            dst_ref[...] = dst
            dw_sum = dw if dw_sum is None else dw_sum + dw
            db_sum = db if db_sum is None else db_sum + db

        @pl.when(m == 0)
        def _():
            dw_ref[...] = dw_sum
            _partial_rows(db_ref, [db_sum])

        @pl.when(m > 0)
        def _():
            dw_ref[...] += dw_sum
            db_ref[0:1, :] += db_sum

    ssf, ssb = ssave
    chunk_f = lambda w: pl.BlockSpec((CH, w), lambda m: (fwd_chunk(m), 0))
    chunk_b = lambda w: pl.BlockSpec((CH, w), lambda m: (back_chunk(m), 0))
    state = pl.BlockSpec((None, BRW, C_KW), lambda m: (nch - 1 - m, 0, 0))
    sd = jax.ShapeDtypeStruct
    return pl.pallas_call(
        body, name=name, grid=(nch,),
        in_specs=[chunk_f(W_G), chunk_b(W_G), pl.BlockSpec((LANE, 512), lambda m: (0, 0)), pl.BlockSpec((1, 512), lambda m: (0, 0)),
                  state, state, chunk_f(BRW), chunk_b(BRW)],
        out_specs=[chunk_f(W_G), chunk_b(W_G), pl.BlockSpec((LANE, 512), lambda m: (0, 0)), pl.BlockSpec((SUB, 512), lambda m: (0, 0))],
        out_shape=[sd((T, W_G), f32), sd((T, W_G), f32), sd((LANE, 512), f32), sd((SUB, 512), f32)],
        scratch_shapes=[pltpu.VMEM((BRW, C_KW), f32), pltpu.VMEM((BRW, C_KW), f32)],
        compiler_params=_cparams(("arbitrary",)),
    )(P, P, w2, b2, ssf, ssb, doc, doc)


def _sum_dirs(a, b, tm, name):
    T, W = a.shape

    def body(a_ref, b_ref, o_ref):
        o_ref[...] = (a_ref[...] + b_ref[...]).astype(bf16)

    spec = pl.BlockSpec((tm, W), lambda i: (i, 0))
    return pl.pallas_call(
        body, name=name, grid=(T // tm,), in_specs=[spec, spec], out_specs=spec,
        out_shape=jax.ShapeDtypeStruct((T, W), bf16),
        compiler_params=_cparams(("parallel",)),
    )(a, b)


def _merge_fn(h, m_l, m_c, isctx, ya, ga, yb, gb, of, ob, gc, hh, gd, mg, es, ey, cn, dng, dnb, lg, lb, wbr, wout):
    oc = of + ob
    yc = jnp.concatenate([_rms(oc[:, HD * i:HD * (i + 1)], cn[:, HD * i:HD * (i + 1)]) for i in range(C_HEADS)], 1)
    brs = [ya * _silu(ga), yb * _silu(gb), yc * _silu(gc), _silu(_ln(hh) * dng + dnb) * _silu(gd)]
    acc = None
    for i in range(4):
        t = _sigmoid(mg[:, D * i:D * (i + 1)]) * (mm(brs[i], wbr[i]) + es[i])
        acc = t if acc is None else acc + t
    y = mm(acc, wout) + ey
    gate = jnp.where(isctx, m_c[:, 2 * D:3 * D], m_l[:, 2 * D:3 * D])
    hn = _ln(ALPHA * h + gate * y) * lg + lb
    return hn, (brs, acc)


def _merge_specs(tm):
    t = lambda w, off=0: _tok(tm, w, off)
    return [t(D), pl.BlockSpec((SUB, 3 * D), lambda i: (0, 0)),
            t(BRW), t(BRW, M_GA), t(BRW), t(BRW, M_GB),
            t(BRW), t(BRW),
            t(BRW, M_GC), t(BRW), t(BRW, M_GD), t(4 * D, 0),
            _vec(BRW), _vec(BRW), _vec(BRW), _vec(D), _vec(D),
            pl.BlockSpec((4, BRW, D), lambda i: (0, 0, 0)), pl.BlockSpec((D, D), lambda i: (0, 0))]


def _merge_fwd(h, modv_l, ya, yb, o2, hh, P, cn, dng, dnb, lg, lb, wbr, wout, tc, tm, name):
    T = h.shape[0]

    def body(h_ref, m_ref, ya_ref, ga_ref, yb_ref, gb_ref, of_ref, ob_ref, gc_ref, hh_ref, gd_ref, mg_ref,
             cn_ref, dng_ref, dnb_ref, lg_ref, lb_ref, wbr_ref, wout_ref, o_ref):
        isctx = _row_ids(pl.program_id(0), tm) < tc
        zero = jnp.zeros((tm, D), f32)
        up = lambda r: r[...].astype(f32)
        hn, _ = _merge_fn(h_ref[...], m_ref[0:1, :], m_ref[1:2, :], isctx, ya_ref[...], up(ga_ref), yb_ref[...],
                          up(gb_ref), of_ref[...], ob_ref[...], up(gc_ref), hh_ref[...], up(gd_ref), up(mg_ref),
                          [zero] * 4, zero, cn_ref[...], dng_ref[...], dnb_ref[...], lg_ref[...], lb_ref[...],
                          [wbr_ref[i] for i in range(4)], wout_ref[...])
        o_ref[...] = hn

    return pl.pallas_call(
        body, name=name, grid=(T // tm,),
        in_specs=_merge_specs(tm), out_specs=_tok(tm, D, 0),
        out_shape=jax.ShapeDtypeStruct((T, D), f32),
        compiler_params=_cparams(("parallel",)),
    )(h, modv_l, ya, P, yb, P, o2[0], o2[1], P, hh, P, P, cn, dng, dnb, lg, lb, wbr, wout)


def _merge_bwd(dhn, h, modv_l, ya, yb, o2, hh, P, cn, dng, dnb, lg, lb, wbr, wout, tc, tm, name):
    T = h.shape[0]
    nt = T // tm

    def body(g_ref, h_ref, m_ref, ya_ref, ga_ref, yb_ref, gb_ref, of_ref, ob_ref, gc_ref, hh_ref, gd_ref, mg_ref,
             cn_ref, dng_ref, dnb_ref, lg_ref, lb_ref, wbr_ref, wout_ref,
             dh_ref, dm_ref, dya_ref, dyb_ref, doc_ref, dhh_ref, dp_ref,
             br_ref, z_ref, acc_ref, dy_ref, dv5_ref, dvd_ref):
        isctx = _row_ids(pl.program_id(0), tm) < tc
        zero = jnp.zeros((tm, D), f32)
        wbr_v = [wbr_ref[i] for i in range(4)]
        wout_v = wout_ref[...]
        up = lambda r: r[...].astype(f32)

        def fn(h, ml, mc, ya, ga, yb, gb, oc, gc, hh, gd, mg, e0, e1, e2, e3, ey, cn, dng, dnb, lg, lb):
            return _merge_fn(h, ml, mc, isctx, ya, ga, yb, gb, oc, jnp.zeros_like(oc), gc, hh, gd, mg,
                             [e0, e1, e2, e3], ey, cn, dng, dnb, lg, lb, wbr_v, wout_v)

        _, vjp, (brs, acc) = jax.vjp(
            fn, h_ref[...], m_ref[0:1, :], m_ref[1:2, :], ya_ref[...], up(ga_ref), yb_ref[...], up(gb_ref),
            of_ref[...] + ob_ref[...], up(gc_ref), hh_ref[...], up(gd_ref), up(mg_ref), zero, zero, zero, zero, zero,
            cn_ref[...], dng_ref[...], dnb_ref[...], lg_ref[...], lb_ref[...], has_aux=True)
        (dh, dml, dmc, dya, dga, dyb, dgb, doc, dgc, dhh, dgd, dmg, z0, z1, z2, z3, dy,
         dcn, ddng, ddnb, dlg, dlb) = vjp(g_ref[...])
        dh_ref[...] = dh
        _partial_rows(dm_ref, [dml, dmc])
        dya_ref[...] = dya
        dyb_ref[...] = dyb
        doc_ref[...] = doc
        dhh_ref[...] = dhh
        dp_ref[:, 0:M_GA] = dmg.astype(bf16)
        dp_ref[:, M_GA:M_GB] = dga.astype(bf16)
        dp_ref[:, M_GB:M_GC] = dgb.astype(bf16)
        dp_ref[:, M_GC:M_GD] = dgc.astype(bf16)
        dp_ref[:, M_GD:W_M] = dgd.astype(bf16)
        for i, z in enumerate((z0, z1, z2, z3)):
            br_ref[i] = brs[i].astype(bf16)
            z_ref[i] = z.astype(bf16)
        acc_ref[...] = acc.astype(bf16)
        dy_ref[...] = dy.astype(bf16)
        _partial_rows(dv5_ref, [dcn, ddng, ddnb])
        _partial_rows(dvd_ref, [dlg, dlb])

    t = lambda w: _tok(tm, w, 0)
    part = lambda w: pl.BlockSpec((None, SUB, w), lambda i: (i, 0, 0))
    sd = jax.ShapeDtypeStruct
    return pl.pallas_call(
        body, name=name, grid=(nt,),
        in_specs=[t(D)] + _merge_specs(tm),
        out_specs=[t(D), part(3 * D)] + [t(BRW)] * 4 + [t(W_M),
                   pl.BlockSpec((4, tm, BRW), lambda i: (0, i, 0)), pl.BlockSpec((4, tm, D), lambda i: (0, i, 0)),
                   t(D), t(D), part(BRW), part(D)],
        out_shape=[sd((T, D), f32), sd((nt, SUB, 3 * D), f32)] + [sd((T, BRW), f32)] * 4 + [sd((T, W_M), bf16),
                   sd((4, T, BRW), bf16), sd((4, T, D), bf16), sd((T, D), bf16), sd((T, D), bf16),
                   sd((nt, SUB, BRW), f32), sd((nt, SUB, D), f32)],
        compiler_params=_cparams(("parallel",)),
    )(dhn, h, modv_l, ya, P, yb, P, o2[0], o2[1], P, hh, P, P, cn, dng, dnb, lg, lb, wbr, wout)


def _loss_kernel(h, tgt, tc, tm, name):
    T = h.shape[0]
    nt = T // tm
    nct = tc // tm

    def body(h_ref, t_ref, d_ref, l_ref):
        i = pl.program_id(0)
        err = h_ref[...] - t_ref[...]
        lat = (i >= nct).astype(f32)
        d_ref[...] = err * (lat / D)
        l_ref[...] = jnp.zeros((SUB, LANE), f32) + lat * 0.5 * jnp.sum(err * err) / D

    return pl.pallas_call(
        body, name=name, grid=(nt,),
        in_specs=[pl.BlockSpec((tm, D), lambda i: (i, 0)),
                  pl.BlockSpec((tm, D), lambda i: (jnp.maximum(i - nct, 0), 0))],
        out_specs=[pl.BlockSpec((tm, D), lambda i: (i, 0)), pl.BlockSpec((None, SUB, LANE), lambda i: (i, 0, 0))],
        out_shape=[jax.ShapeDtypeStruct((T, D), f32), jax.ShapeDtypeStruct((nt, SUB, LANE), f32)],
        compiler_params=_cparams(("parallel",)),
    )(h, tgt)


def _rope_tables(tc, tl):
    t = jnp.arange(tl)
    inv = ROPE_THETA ** (-jnp.arange(0, HD // 2, 2, dtype=f32) / (HD // 2))
    ang = jnp.concatenate([(t // GRID_W).astype(f32)[:, None] * inv, (t % GRID_W).astype(f32)[:, None] * inv], -1)
    cos, sin = jnp.repeat(jnp.cos(ang), 2, axis=1), jnp.repeat(jnp.sin(ang), 2, axis=1)
    even = (jnp.arange(HD) % 2 == 0)[None, :]
    cos_f = jnp.concatenate([jnp.ones((tc, HD), f32), cos], 0)
    sin_a = jnp.concatenate([jnp.zeros((tc, HD), f32), jnp.where(even, -sin, 0.0)], 0)
    sin_b = jnp.concatenate([jnp.zeros((tc, HD), f32), jnp.where(even, 0.0, sin)], 0)
    return cos_f, sin_a, sin_b


N_CHIPS = 4
SHARD = N_IN // N_CHIPS


def _group_ranges():
    return dict(M=[(S_MG, 4 * D), (S_GA, BRW), (S_GB, BRW), (S_GC, BRW), (S_GD, BRW)], A=[(S_Q, W_A)],
                C=[(S_B, 3 * BRW), (S_DA, 2 * BRW)], G=[(S_CQ, 2 * C_KW + BRW), (S_R, 2 * C_RANK)])


def _group_weights(w4):
    out = {}
    for k, ranges in _group_ranges().items():
        parts = []
        for a, n in ranges:
            n = LANE if (k, a) == ("G", S_R) else n
            while n > 0:
                s, r = divmod(a, SHARD)
                m = min(n, SHARD - r)
                parts.append(w4[s, r:r + m])
                a, n = a + m, n - m
        out[k] = jnp.concatenate(parts, 0)
    return out


def _ungroup(g):
    secs = []
    for k, ranges in _group_ranges().items():
        off = 0
        for a, n in ranges:
            secs.append((a, g[k][off:off + n]))
            off += n
    return jnp.concatenate([v for _, v in sorted(secs, key=lambda t: t[0])], 0)


PROJ_TN = dict(M=2048, A=1024, C=1280, G=1152)
DU_TK = dict(M=2048, A=1024, C=BRW, G=1152)
DWP_TN = dict(M=768, A=1024, C=BRW, G=1152)


def _gate_weights(w2_l, gb_l):
    w = jnp.zeros((LANE, 2 * C_KW), f32)
    w = w.at[0:C_RANK, 0:C_KW].set(w2_l[0]).at[C_RANK:2 * C_RANK, C_KW:2 * C_KW].set(w2_l[1])
    return w, jnp.concatenate([gb_l[0], gb_l[1]])[None, :]


def _local_step(x1, c1, ctx1, tgt1, c_ctx, b_mod, weights_of, q_norm, k_norm, b_conv, w2, gb, c_norm, d_conv_w,
                d_conv_b, d_norm_g, d_norm_b, grads_done, ln_g, ln_b, tm, token=None):
    tc, tl = ctx1.shape[0], x1.shape[0]
    T = tc + tl
    rc = min(256, tc)
    tmb = tm // 2
    tmm = 768 if T % 768 == 0 else tm
    rope = _rope_tables(tc, tl)
    cin = jnp.concatenate([c1, c_ctx[None, :], jnp.zeros((SUB - 2, D), f32)], 0)
    if token is not None:
        cin = cin + token[:, 0:1]
    row = lambda v: v[None, :]

    h = jnp.concatenate([ctx1, x1], 0)
    saved, wp, w_br, w_out, w_mod, modv = [], *([None] * DEPTH for _ in range(5))
    for l in range(DEPTH):
        wp[l], merge_weights, w_mod[l] = weights_of(l, h)
        modv[l] = _mod_fwd(cin, w_mod[l], b_mod[l], f"mod_fwd{l}")
        u = _ln_fwd(h, modv[l], tc, tm, f"ln_fwd{l}")
        P = {k: _matmul(u, wp[l][k], "nt", tmm, PROJ_TN[k], D, f"proj{l}{k}", out_dtype=bf16) for k in GROUPS}
        qn, kn, vb = _prep_fwd(P["A"], row(q_norm[l]), row(k_norm[l]), rope, tm, f"prep_fwd{l}")
        ya = _attn_fwd(qn, kn, vb, tc, tm, f"attn_fwd{l}")
        yb, hh = _conv_fwd(P["C"], b_conv[l], d_conv_w[l], row(d_conv_b[l]), tc, tl, rc, f"conv_fwd{l}")
        w2p, b2p = _gate_weights(w2[l], gb[l])
        gla = _gla_fwd(P["G"], w2p, b2p, tc, f"gla_fwd{l}")
        o2, ssave = gla[:2], gla[2:]
        w_br[l], w_out[l] = merge_weights(o2[0])
        hn = _merge_fwd(h, modv[l], ya, yb, o2, hh, P["M"], row(c_norm[l]), row(d_norm_g[l]), row(d_norm_b[l]),
                        row(ln_g[l]), row(ln_b[l]), w_br[l], w_out[l], tc, tm, f"merge_fwd{l}")
        saved.append((h, u, P, qn, kn, vb, ya, yb, hh, o2, ssave, w2p, b2p))
        h = hn

    dh, lparts = _loss_kernel(h, tgt1, tc, tm, "loss")
    loss = jnp.sum(lparts[:, 0, 0])

    g = {k: [None] * DEPTH for k in ("wp", "q_norm", "k_norm", "b_conv", "w2", "gb", "c_norm", "d_conv_w", "d_conv_b",
                                     "d_norm_g", "d_norm_b", "w_br", "w_out", "ln_g", "ln_b", "modv")}
    for l in reversed(range(DEPTH)):
        h_in, u, P, qn, kn, vb, ya, yb, hh, o2, ssave, w2p, b2p = saved[l]
        dP = {}
        (dh_res, dm_mg, dya, dyb, doc, dhh, dP["M"], br, z, acc, dy, dv5, dvd) = _merge_bwd(
            dh, h_in, modv[l], ya, yb, o2, hh, P["M"], row(c_norm[l]), row(d_norm_g[l]), row(d_norm_b[l]),
            row(ln_g[l]), row(ln_b[l]), w_br[l], w_out[l], tc, tmb, f"merge_bwd{l}")
        g["w_br"][l] = _matmul_tn_batched(br, z, N_CHIPS, f"dwbr{l}")
        g["w_out"][l] = _matmul(acc, dy, "tn", D, D, T, f"dwout{l}", out_dtype=bf16)
        tk = grads_done(l, {k: g[k][l] for k in ("w_br", "w_out")})
        qg_l = row(q_norm[l]) if tk is None else row(q_norm[l]) + tk[0:1, :]
        v5 = jnp.sum(dv5, 0)
        g["c_norm"][l], g["d_norm_g"][l], g["d_norm_b"][l] = v5[0], v5[1], v5[2]
        vd = jnp.sum(dvd, 0)
        g["ln_g"][l], g["ln_b"][l] = vd[0], vd[1]
        dqn, dkn, dv = _attn_bwd(qn, kn, vb, dya, tc, tm, f"attn_bwd{l}")
        dP["A"], dqk = _prep_bwd(P["A"], dqn, dkn, dv, qg_l, row(k_norm[l]), rope, tm, f"prep_bwd{l}")
        dqk = jnp.sum(dqk, 0)
        g["q_norm"][l], g["k_norm"][l] = dqk[0], dqk[1]
        dP["C"], dwb, dwd, dbd = _conv_bwd(P["C"], dyb, dhh, b_conv[l], d_conv_w[l], tc, tl, rc, f"conv_bwd{l}")
        g["b_conv"][l], g["d_conv_w"][l], g["d_conv_b"][l] = dwb, dwd, dbd[0]
        dpf, dpb, dw2p, db2p = _gla_bwd(P["G"], w2p, b2p, ssave, doc, tc, f"gla_bwd{l}")
        dP["G"] = _sum_dirs(dpf, dpb, tm, f"gla_sum{l}")
        db2p = db2p[0]
        g["w2"][l] = jnp.stack([dw2p[0:C_RANK, 0:C_KW], dw2p[C_RANK:2 * C_RANK, C_KW:2 * C_KW]])
        g["gb"][l] = jnp.stack([db2p[0:C_KW], db2p[C_KW:2 * C_KW]])
        g["wp"][l] = {k: _matmul(dP[k], u, "tn", DWP_TN[k], D, T, f"dwp{l}{k}", out_dtype=bf16) for k in GROUPS}
        tk = grads_done(l, {"wp": g["wp"][l]})
        du = _matmul_groups(dP, wp[l], DU_TK, tmm, f"du{l}", after=tk)
        dh, dm_ln = _ln_bwd(du, h_in, dh_res, modv[l], tc, tm, f"ln_bwd{l}", latent_only=(l == 0))
        g["modv"][l] = jnp.sum(dm_mg, 0) + jnp.sum(dm_ln, 0)

    dmodv = jnp.stack(g.pop("modv"))
    g["w_mod"], dcin = _mod_bwd(cin, w_mod, dmodv)
    g["b_mod"] = dmodv[:, 0, :] + dmodv[:, 1, :]
    g["c_ctx"] = jnp.sum(dcin, (0, 1))[1]
    return loss, dh, g


HALF_TL = 256


TILE_BYTES = 1 << 20


def _row_tile(rows, cols, itemsize=4):
    tr = min(rows, 128)
    while rows % (2 * tr) == 0 and 2 * tr * cols * itemsize <= TILE_BYTES:
        tr *= 2
    return tr


def _adamw(w, g, m, v, name, tr=None, after=None):
    L, R, C = w.shape
    tr = _row_tile(R, C) if tr is None else tr
    if R % tr == 0:
        grid, spec = (L, R // tr), pl.BlockSpec((None, tr, C), lambda l, i: (l, i, 0))
    elif R * C * 4 <= (1 << 20):
        grid, spec = (L, 1), pl.BlockSpec((None, R, C), lambda l, i: (l, 0, 0))
    else:
        grid, spec = (L, C // HALF_TL), pl.BlockSpec((None, R, HALF_TL), lambda l, i: (l, 0, i))

    def body(w_ref, g_ref, m_ref, v_ref, *rest):
        go_ref, d_ref, nm_ref, nv_ref = rest[-4:]
        gg = g_ref[...]
        go_ref[...] = gg
        nm = B1 * m_ref[...] + (1.0 - B1) * gg
        nv = B2 * v_ref[...] + (1.0 - B2) * (gg * gg)
        m_hat = nm / (1.0 - B1 ** STEP)
        v_hat = nv / (1.0 - B2 ** STEP)
        d_ref[...] = -LR * (m_hat / (jnp.sqrt(v_hat) + AEPS) + WD * w_ref[...])
        nm_ref[...] = nm
        nv_ref[...] = nv

    return pl.pallas_call(
        body, name=name, grid=grid, in_specs=[spec] * 4 + ([] if after is None else [pl.BlockSpec(memory_space=pl.ANY)]),
        out_specs=[spec] * 4, out_shape=[jax.ShapeDtypeStruct((L, R, C), f32)] * 4,
        compiler_params=_cparams(("parallel", "parallel")),
    )(w, g, m, v, *([] if after is None else [after]))


MESH = pl.DeviceIdType.MESH
ANY = pl.BlockSpec(memory_space=pl.ANY)


def _place():
    x, y, c = lax.axis_index("x"), lax.axis_index("y"), lax.axis_index("c")
    chips = [(1 - x, y), (x, 1 - y), (1 - x, 1 - y)]
    return x, y, c, chips


def _half(ref, c, axis):
    n = ref.shape[axis] // 2
    last = axis in (-1, ref.ndim - 1)
    idx = [slice(None)] * ref.ndim
    idx[axis] = pl.ds(pl.multiple_of(c * n, LANE if last else SUB), n)
    return ref.at[tuple(idx)]


def _half_shape(shape, axis):
    s = list(shape)
    s[axis] //= 2
    return tuple(s)


def _all_gather(arrs, axes, name):
    n = len(arrs)

    def body(*refs):
        ins, outs = refs[:n], refs[n:2 * n]
        send, recv = refs[2 * n:]
        x, y, c, chips = _place()
        me, sib = 2 * x + y, (x, y, 1 - c)

        def copy(a, k, chip_idx, cc, to, src=None):
            blk = _half(outs[a].at[chip_idx], cc, axes[a])
            return pltpu.make_async_remote_copy(src_ref=blk if src is None else src, dst_ref=blk,
                                                send_sem=send.at[7 * a + k], recv_sem=recv.at[7 * a + k],
                                                device_id=to, device_id_type=MESH)

        own = [pltpu.make_async_remote_copy(src_ref=ins[a], dst_ref=outs[a].at[me], send_sem=send.at[7 * a + 6],
                                            recv_sem=recv.at[7 * a + 6], device_id=sib, device_id_type=MESH)
               for a in range(n)]
        first = own + [copy(a, j, me, c, (*chip, c), src=_half(ins[a], c, axes[a]))
                       for a in range(n) for j, chip in enumerate(chips)]
        for cp in first:
            cp.start()
        passed = []
        for a in range(n):
            for j, chip in enumerate(chips):
                k = 2 * chip[0] + chip[1]
                copy(a, j, k, c, sib).wait_recv()
                fwd = copy(a, 3 + j, k, c, sib)
                fwd.start()
                passed.append(fwd)
        for a in range(n):
            own[a].wait_recv()
            for j, chip in enumerate(chips):
                copy(a, 3 + j, 2 * chip[0] + chip[1], 1 - c, sib).wait_recv()
        for cp in first + passed:
            cp.wait_send()

    return pl.pallas_call(
        body, name=name, in_specs=[ANY] * n, out_specs=[ANY] * n,
        out_shape=[jax.ShapeDtypeStruct((N_CHIPS,) + a.shape, a.dtype) for a in arrs],
        scratch_shapes=[pltpu.SemaphoreType.DMA((7 * n,)), pltpu.SemaphoreType.DMA((7 * n,))],
    )(*arrs)


def _add_half(gfull, land, cidx, axis, name, tr=None, out_dtype=bf16):
    _, hr, hc = land.shape
    if axis == 0:
        tr = min(tr, hr) if tr else _row_tile(hr, hc)
        nb, blk = hr // tr, (None, tr, hc)
        g_spec = pl.BlockSpec(blk, lambda s, i, cr: (s, cr[0] * nb + i, 0))
        l_spec = pl.BlockSpec(blk, lambda s, i, cr: (s, i, 0))
    else:
        nb, blk = hc // HALF_TL, (None, hr, HALF_TL)
        g_spec = pl.BlockSpec(blk, lambda s, i, cr: (s, 0, cr[0] * nb + i))
        l_spec = pl.BlockSpec(blk, lambda s, i, cr: (s, 0, i))

    def body(c_ref, g_ref, l_ref, o_ref):
        o_ref[...] = (g_ref[...].astype(f32) + l_ref[...].astype(f32)).astype(o_ref.dtype)

    return pl.pallas_call(
        body, name=name,
        grid_spec=pltpu.PrefetchScalarGridSpec(
            num_scalar_prefetch=1, grid=(N_CHIPS, nb), in_specs=[g_spec, l_spec], out_specs=l_spec),
        out_shape=jax.ShapeDtypeStruct((N_CHIPS, hr, hc), out_dtype),
        compiler_params=_cparams(("parallel", "parallel")),
    )(cidx, gfull, land)


def _sum_chips(land, own, place, axis, layer, into, name, tr=None):
    _, hr, hc = land.shape
    fresh = not hasattr(into, "dtype")
    shape = tuple(into) if fresh else into.shape
    if axis == 0:
        tr = min(tr, hr) if tr else _row_tile(hr, 4 * hc, 2)
        nb, blk = hr // tr, (tr, hc)
        l_map, m_map = (lambda i, p: (0, i, 0)), (lambda i, p: (p[0], i, 0))
        o_map = lambda i, p: (layer, p[1] * nb + i, 0)
    else:
        nb, blk = hc // HALF_TL, (hr, HALF_TL)
        l_map, m_map = (lambda i, p: (0, 0, i)), (lambda i, p: (p[0], 0, i))
        o_map = lambda i, p: (layer, 0, p[1] * nb + i)

    def body(p_ref, l_ref, o_ref, *rest):
        me = p_ref[0]
        mine = o_ref[...].astype(f32)
        acc = None
        for k in range(N_CHIPS):
            t = jnp.where(me == k, mine, l_ref[k].astype(f32))
            acc = t if acc is None else acc + t
        rest[-1][...] = acc

    return pl.pallas_call(
        body, name=name,
        grid_spec=pltpu.PrefetchScalarGridSpec(
            num_scalar_prefetch=1, grid=(nb,),
            in_specs=[pl.BlockSpec((N_CHIPS,) + blk, l_map), pl.BlockSpec((None,) + blk, m_map)] + ([] if fresh else [ANY]),
            out_specs=pl.BlockSpec((None,) + blk, o_map)),
        out_shape=jax.ShapeDtypeStruct(shape, f32),
        input_output_aliases={} if fresh else {3: 0},
        compiler_params=_cparams(("parallel",)),
    )(place, land, own, *([] if fresh else [into]))


def _sibling_fill(arrs, axes, name):
    n = len(arrs)

    def body(*refs):
        outs = refs[n:2 * n]
        send, recv = refs[2 * n:]
        x, y, c, _ = _place()
        cps = [pltpu.make_async_remote_copy(src_ref=_half(outs[a], c, axes[a] + 1), dst_ref=_half(outs[a], c, axes[a] + 1),
                                            send_sem=send.at[a], recv_sem=recv.at[a], device_id=(x, y, 1 - c),
                                            device_id_type=MESH) for a in range(n)]
        for cp in cps:
            cp.start()
        for a in range(n):
            blk = _half(outs[a], 1 - c, axes[a] + 1)
            pltpu.make_async_remote_copy(src_ref=blk, dst_ref=blk, send_sem=send.at[a], recv_sem=recv.at[a],
                                         device_id=(x, y, 1 - c), device_id_type=MESH).wait_recv()
        for cp in cps:
            cp.wait_send()

    return pl.pallas_call(
        body, name=name, in_specs=[ANY] * n, out_specs=[ANY] * n,
        out_shape=[jax.ShapeDtypeStruct(a.shape, a.dtype) for a in arrs],
        input_output_aliases={a: a for a in range(n)},
        scratch_shapes=[pltpu.SemaphoreType.DMA((n,)), pltpu.SemaphoreType.DMA((n,))],
    )(*arrs)


HBM = pl.BlockSpec(memory_space=pltpu.HBM)
SEM = pl.BlockSpec(memory_space=pltpu.SEMAPHORE)
EFFECT = pltpu.SideEffectType.DATAFLOW_SIDE_EFFECTING
PEERS = 7


def _split_copies(srcs, lands, send, recv, gather, axes=None):
    x, y, c, chips = _place()
    me = 2 * x + y
    if gather == "all":
        out = []
        for a in range(len(srcs)):
            for m in range(1, N_DEV):
                px, py, pc = [(1 - q) if (m >> s) & 1 else q for q, s in ((x, 2), (y, 1), (c, 0))]
                sems = dict(send_sem=send.at[PEERS * a + m - 1], recv_sem=recv.at[PEERS * a + m - 1], device_id=(px, py, pc),
                            device_id_type=MESH)
                out.append((pltpu.make_async_remote_copy(src_ref=srcs[a], dst_ref=lands[a].at[4 * x + 2 * y + c], **sems),
                            pltpu.make_async_remote_copy(src_ref=srcs[a], dst_ref=lands[a].at[4 * px + 2 * py + pc], **sems)))
        return out
    if axes is not None:
        out = []
        for a in range(len(srcs)):
            sems = dict(send_sem=send.at[PEERS * a], recv_sem=recv.at[PEERS * a], device_id=(x, y, 1 - c), device_id_type=MESH)
            copy = pltpu.make_async_remote_copy(src_ref=_half(srcs[a], 1 - c, axes[a] + 1), dst_ref=lands[a], **sems)
            out.append((copy, copy))
        return out
    peers = [((*chip, c), 2 * chip[0] + chip[1]) for chip in chips] + ([((x, y, 1 - c), me)] if gather else [])
    out = []
    for a in range(len(srcs)):
        for j, (dev, k) in enumerate(peers):
            src = srcs[a] if gather else srcs[a].at[k]
            sems = dict(send_sem=send.at[PEERS * a + j], recv_sem=recv.at[PEERS * a + j], device_id=dev, device_id_type=MESH)
            out.append((pltpu.make_async_remote_copy(src_ref=src, dst_ref=lands[a].at[me], **sems),
                        pltpu.make_async_remote_copy(src_ref=src, dst_ref=lands[a].at[k], **sems)))
    return out


def _split_start(srcs, gather, after, name, axes=None):
    n = len(srcs)
    if axes is not None:
        lands = [lax.empty(_half_shape(s.shape, axes[a] + 1), s.dtype) for a, s in enumerate(srcs)]
    else:
        lead = (N_DEV,) if gather == "all" else (N_CHIPS,) if gather else ()
        lands = [lax.empty(lead + s.shape, s.dtype) for s in srcs]

    def body(*refs):
        send, recv = refs[2 * n + 1], refs[2 * n + 2]
        for start, _ in _split_copies(refs[:n], refs[n:2 * n], send, recv, gather, axes):
            start.start()
        refs[-1][...] = jnp.zeros_like(refs[-1])

    sems = pltpu.SemaphoreType.DMA((PEERS * n,))
    hbm = lambda a: pltpu.with_memory_space_constraint(a, pltpu.HBM)
    out = pl.pallas_call(
        body, name=name,
        out_shape=(sems, sems, *[pltpu.HBM(a.shape, a.dtype) for a in srcs + lands], jax.ShapeDtypeStruct((SUB, LANE), f32)),
        in_specs=[HBM] * (2 * n) + [ANY], out_specs=(SEM, SEM, *[HBM] * (2 * n), pl.BlockSpec(memory_space=pltpu.VMEM)),
        input_output_aliases={i: 2 + i for i in range(2 * n)},
        compiler_params=pltpu.CompilerParams(has_side_effects=EFFECT),
    )(*[hbm(a) for a in srcs + lands], after)
    return out[0], out[1], list(out[2:2 + n]), list(out[2 + n:2 + 2 * n]), out[-1]


def _split_wait(send, recv, srcs, lands, gather, after, name, axes=None):
    n = len(srcs)

    def body(*refs):
        for start, arrival in _split_copies(refs[:n], refs[n:2 * n], refs[2 * n], refs[2 * n + 1], gather, axes):
            start.wait_send()
            arrival.wait_recv()

    out = pl.pallas_call(
        body, name=name, out_shape=[pltpu.HBM(a.shape, a.dtype) for a in srcs + lands],
        in_specs=[HBM] * (2 * n) + [SEM, SEM, ANY], out_specs=[HBM] * (2 * n),
        input_output_aliases={i: i for i in range(2 * n)},
        compiler_params=pltpu.CompilerParams(has_side_effects=EFFECT),
    )(*srcs, *lands, send, recv, after)
    return list(out[:n]), list(out[n:])


N_DEV = 8


def _sum_devices(v, land, me8, name):
    def body(p_ref, v_ref, l_ref, o_ref):
        acc = None
        for k in range(N_DEV):
            t = jnp.where(p_ref[0] == k, v_ref[...], l_ref[k])
            acc = t if acc is None else acc + t
        o_ref[...] = acc

    return pl.pallas_call(
        body, name=name,
        grid_spec=pltpu.PrefetchScalarGridSpec(
            num_scalar_prefetch=1, grid=(1,),
            in_specs=[pl.BlockSpec(v.shape, lambda i, p: (0, 0)), pl.BlockSpec(land.shape, lambda i, p: (0, 0, 0))],
            out_specs=pl.BlockSpec(v.shape, lambda i, p: (0, 0))),
        out_shape=jax.ShapeDtypeStruct(v.shape, f32),
        compiler_params=_cparams(("arbitrary",)),
    )(me8, v, land)


def _pack_small(arrs, mult=2 * SUB):
    flat = jnp.concatenate([a.reshape(-1) for a in arrs])
    rows = -(-flat.shape[0] // (LANE * mult)) * mult
    return jnp.pad(flat, (0, rows * LANE - flat.shape[0])).reshape(rows, LANE)


def _unpack_small(vec, shapes):
    flat, out, o = vec.reshape(-1), [], 0
    for s in shapes:
        n = int(np.prod(s))
        out.append(flat[o:o + n].reshape(s))
        o += n
    return out


REPL_SMALL = ("c_ctx", "b_mod", "q_norm", "k_norm", "c_norm", "d_conv_b", "d_norm_g", "d_norm_b", "ln_g", "ln_b")
SHARD_SMALL = ("b_conv", "c_gate_w2", "c_gate_b", "d_conv_w")
BIG = ("w_mod", "w_in", "w_br", "w_out")
ORDER = ("c_ctx", "w_mod", "b_mod", "w_in", "q_norm", "k_norm", "b_conv", "c_gate_w2", "c_gate_b", "c_norm", "d_conv_w",
         "d_conv_b", "d_norm_g", "d_norm_b", "w_br", "w_out", "ln_g", "ln_b")


def kernel(x, c, ctx, c_ctx, w_mod, b_mod, w_in, q_norm, k_norm, b_conv, c_gate_w2, c_gate_b, c_norm, d_conv_w, d_conv_b, d_norm_g, d_norm_b, w_br, w_out, ln_g, ln_b, loss_target, m_c_ctx, m_w_mod, m_b_mod, m_w_in, m_q_norm, m_k_norm, m_b_conv, m_c_gate_w2, m_c_gate_b, m_c_norm, m_d_conv_w, m_d_conv_b, m_d_norm_g, m_d_norm_b, m_w_br, m_w_out, m_ln_g, m_ln_b, v_c_ctx, v_w_mod, v_b_mod, v_w_in, v_q_norm, v_k_norm, v_b_conv, v_c_gate_w2, v_c_gate_b, v_c_norm, v_d_conv_w, v_d_conv_b, v_d_norm_g, v_d_norm_b, v_w_br, v_w_out, v_ln_g, v_ln_b):
    W = dict(c_ctx=c_ctx, w_mod=w_mod, b_mod=b_mod, w_in=w_in, q_norm=q_norm, k_norm=k_norm, b_conv=b_conv,
             c_gate_w2=c_gate_w2, c_gate_b=c_gate_b, c_norm=c_norm, d_conv_w=d_conv_w, d_conv_b=d_conv_b,
             d_norm_g=d_norm_g, d_norm_b=d_norm_b, w_br=w_br, w_out=w_out, ln_g=ln_g, ln_b=ln_b)
    M = dict(c_ctx=m_c_ctx, w_mod=m_w_mod, b_mod=m_b_mod, w_in=m_w_in, q_norm=m_q_norm, k_norm=m_k_norm, b_conv=m_b_conv,
             c_gate_w2=m_c_gate_w2, c_gate_b=m_c_gate_b, c_norm=m_c_norm, d_conv_w=m_d_conv_w, d_conv_b=m_d_conv_b,
             d_norm_g=m_d_norm_g, d_norm_b=m_d_norm_b, w_br=m_w_br, w_out=m_w_out, ln_g=m_ln_g, ln_b=m_ln_b)
    V = dict(c_ctx=v_c_ctx, w_mod=v_w_mod, b_mod=v_b_mod, w_in=v_w_in, q_norm=v_q_norm, k_norm=v_k_norm, b_conv=v_b_conv,
             c_gate_w2=v_c_gate_w2, c_gate_b=v_c_gate_b, c_norm=v_c_norm, d_conv_w=v_d_conv_w, d_conv_b=v_d_conv_b,
             d_norm_g=v_d_norm_g, d_norm_b=v_d_norm_b, w_br=v_w_br, w_out=v_w_out, ln_g=v_ln_g, ln_b=v_ln_b)
    chip = 2 * lax.axis_index("x") + lax.axis_index("y")
    cidx = lax.axis_index("c").astype(jnp.int32).reshape(1)

    place = jnp.stack([chip, lax.axis_index("c")]).astype(jnp.int32)

    AXIS = dict(w_in=1, w_mod=0, w_br=0, w_out=0)
    ex = dict(w_in=lambda a: jnp.swapaxes(a, 1, 2), w_mod=lambda a: a.reshape(1, DEPTH * D, -1),
              w_br=lambda a: a.reshape(DEPTH, 4 * BRW, -1), w_out=lambda a: a)
    Wx, Mx, Vx = ({k: ex[k](P_[k]) for k in BIG} for P_ in (W, M, V))

    LAYER, MERGE = ("w_in", "w_br", "w_out"), ("w_br", "w_out")
    small_shard = _pack_small([W[k] for k in SHARD_SMALL])
    keys0 = ("w_in", "w_mod")
    sent = lambda k, l: (w_mod[l] if k == "w_mod" else Wx[k][l]).astype(bf16)
    got = _all_gather([sent(k, 0) for k in keys0] + [small_shard], [AXIS[k] for k in keys0] + [0], "all_gather0")
    smalls = [_unpack_small(got[-1][s], [W[k].shape for k in SHARD_SMALL]) for s in range(N_CHIPS)]
    full = {k: jnp.concatenate([smalls[s][i] for s in range(N_CHIPS)], axis=-1) for i, k in enumerate(SHARD_SMALL)}
    ag0b = _split_start([sent(k, 0) for k in MERGE], True, got[0], "all_gather0b_start")
    ag1 = _split_start([sent(k, 1) for k in keys0], True, ag0b[4], "all_gather1_start")
    ag1b = _split_start([sent(k, 1) for k in MERGE], True, ag1[4], "all_gather1b_start")

    def merge_form(w_br4, w_out4):
        return jnp.moveaxis(w_br4.reshape(N_CHIPS, 4, BRW, D // N_CHIPS), 0, 2).reshape(4, BRW, D), w_out4.reshape(D, D)

    def weights_of(l, h):
        first = got if l == 0 else _split_wait(*ag1[:4], True, h, "all_gather1_wait")[1]
        flight = (ag0b, ag1b)[l]
        return (_group_weights(first[0]),
                lambda after: merge_form(*_split_wait(*flight[:4], True, after, f"all_gather{l}b_wait")[1]), first[1])

    red = {k: Wx[k].shape for k in BIG}
    flights, held = {}, {}

    def to_sibling(tag, l, pieces):
        keys = list(pieces)
        halves = _split_start([pieces[k] for k in keys], False, jnp.zeros((SUB, LANE), f32), f"rs_sibling_halves{tag}_start",
                              axes=[AXIS[k] for k in keys])
        flights[tag] = (l, keys, halves)
        return halves[4]

    def launch(tag, after):
        l, keys, halves = flights[tag]
        axes = [AXIS[k] for k in keys]
        pieces, land_a = _split_wait(*halves[:4], False, after, f"rs_sibling_halves{tag}_wait", axes=axes)
        pair = [_add_half(p, la, cidx, ax, f"rs_pair_sum{tag}_{k}") for k, p, la, ax in zip(keys, pieces, land_a, axes)]
        flights[tag] = (l, keys, _split_start(pair, False, jnp.zeros((SUB, LANE), f32), f"rs_chip_exchange{tag}_start"))
        return flights[tag][2][4]

    def land(tag, after):
        l, keys, flight = flights.pop(tag)
        pair, land_b = _split_wait(*flight[:4], False, after, f"rs_chip_exchange{tag}_wait")
        for k, lb, pr in zip(keys, land_b, pair):
            red[k] = _sum_chips(lb, pr, place, AXIS[k], l, red[k], f"rs_chip_sum{tag}_{k}")

    def grads_done(l, gl):
        if "wp" in gl:
            pieces = dict(w_in=_ungroup(gl["wp"]).reshape(N_CHIPS, SHARD, D))
            if l == 0:
                return launch("0b", gl["wp"]["A"]) + launch("0c", to_sibling("0c", 0, pieces))
            return to_sibling("1", 1, {**pieces, **held.pop(1)})
        pieces = dict(w_br=gl["w_br"].reshape(N_CHIPS, 4 * BRW, D // N_CHIPS), w_out=gl["w_out"].reshape(N_CHIPS, D // N_CHIPS, D))
        if l == 0:
            return launch("1", gl["w_out"]) + to_sibling("0b", 0, pieces)
        held[1] = pieces
        return None

    loss, gx, g = _local_step(
        x[0], c, ctx[0], loss_target[0], c_ctx, b_mod, weights_of, q_norm, k_norm, full["b_conv"],
        full["c_gate_w2"], full["c_gate_b"], c_norm, full["d_conv_w"], d_conv_b, d_norm_g, d_norm_b,
        grads_done, ln_g, ln_b, tm=256, token=ag1b[4])
    g["c_gate_w2"], g["c_gate_b"] = g.pop("w2"), g.pop("gb")
    loss = lax.psum(loss, ("x", "y", "c"))

    w_mod_pieces = g["w_mod"].reshape(N_CHIPS, DEPTH * D, 3 * D // N_CHIPS)
    g = {k: (jnp.stack(v) if isinstance(v, list) else v) for k, v in g.items() if k not in ("wp", "w_br", "w_out", "w_mod")}

    small_names = REPL_SMALL + SHARD_SMALL
    small = _split_start([_pack_small([g[k] for k in small_names])], "all", to_sibling("0d", 0, {"w_mod": w_mod_pieces}),
                         "all_reduce_small_start")

    grad, delta, new_m, new_v = {}, {}, {}, {}

    def adamw_big(keys, after):
        filled = _sibling_fill([red[k] for k in keys], [AXIS[k] for k in keys], "rs_sibling_fill_" + keys[0])
        for k, r in zip(keys, filled):
            back = (lambda a: jnp.swapaxes(a, 1, 2)) if k == "w_in" else (lambda a: a.reshape(W[k].shape))
            g_, d_, m_, v_ = _adamw(Wx[k], r, Mx[k], Vx[k], f"adamw_{k}", after=after)
            grad[k], delta[k], new_m[k], new_v[k] = back(g_), back(d_), back(m_), back(v_)
        return d_

    token = launch("0d", small[4])
    land("1", gx)
    land("0b", gx)
    last = adamw_big(MERGE, token)
    mine, landed = _split_wait(*small[:4], "all", last, "all_reduce_small_wait")
    me8 = (2 * chip + lax.axis_index("c")).astype(jnp.int32).reshape(1)
    gs = _sum_devices(mine[0], landed[0], me8, "all_reduce_small_sum")
    gsm = dict(zip(small_names, _unpack_small(gs, [g[k].shape for k in small_names])))
    for k in SHARD_SMALL:
        wdt = W[k].shape[-1]
        gsm[k] = lax.dynamic_slice_in_dim(gsm[k], chip * wdt, wdt, axis=gsm[k].ndim - 1)
    shapes = [W[k].shape for k in small_names]
    _, d_, m_, v_ = _adamw(*[_pack_small([P_[k] for k in small_names])[None] for P_ in (W, gsm, M, V)], "adamw_small", after=last)
    for k, dd, mm_, vv in zip(small_names, _unpack_small(d_, shapes), _unpack_small(m_, shapes), _unpack_small(v_, shapes)):
        grad[k], delta[k], new_m[k], new_v[k] = gsm[k], dd, mm_, vv
    land("0c", d_)
    last = adamw_big(("w_in",), None)
    land("0d", last)
    adamw_big(("w_mod",), None)

    return (loss, gx[None], *[grad[k] for k in ORDER], *[delta[k] for k in ORDER], *[new_m[k] for k in ORDER],
            *[new_v[k] for k in ORDER])
```

```python
import functools

import jax
import jax.numpy as jnp
import numpy as np
from jax import lax
from jax.experimental import pallas as pl
from jax.experimental.pallas import tpu as pltpu

f32 = jnp.float32
bf16 = jnp.bfloat16

D = 1024
DEPTH = 2
GRID_W = 64
BRW = 512
HD = 128
A_HEADS = 4
C_HEADS = 4
C_KW = 256
C_RANK = 16
C_TAU = 16.0
CH = 128
KB = 3
KD = 31
ALPHA = (2 * DEPTH) ** 0.25
EPS = 1e-6
ROPE_THETA = 10000.0
N_IN = 10784
LR, B1, B2, AEPS, WD, STEP = 0.001, 0.9, 0.999, 1e-08, 0.01, 10

W_M, W_A, W_C, W_G = 4 * D + 4 * BRW, 1024, 5 * BRW, 1152
GROUPS = ("M", "A", "C", "G")
M_GA, M_GB, M_GC, M_GD = 4 * D, 4 * D + BRW, 4 * D + 2 * BRW, 4 * D + 3 * BRW
A_K, A_V = 512, 768
G_K, G_V, G_R = 256, 512, 1024
S_Q, S_GA, S_B, S_C, S_X, S_GB, S_CQ, S_CV, S_GC, S_R, S_DA, S_DG, S_GD, S_MG = (
    0, 1024, 1536, 2048, 2560, 3072, 3584, 4096, 4608, 5120, 5152, 5664, 6176, 6688)

LANE = 128
SUB = 8
VMEM_LIMIT = 56 * 1024 * 1024
CONV_PAD = 16
GLA_SUB = 16
GLA_CLAMP = 60.0


def _cparams(sem, vmem=VMEM_LIMIT):
    return pltpu.CompilerParams(dimension_semantics=sem, vmem_limit_bytes=vmem)


def _dg(a, b, ca, cb):
    return lax.dot_general(a.astype(bf16), b.astype(bf16), (((ca,), (cb,)), ((), ())),
                           preferred_element_type=f32)


@jax.custom_vjp
def mm(a, b):
    return _dg(a, b, 1, 0)


mm.defvjp(lambda a, b: (_dg(a, b, 1, 0), (a, b)),
          lambda r, ct: (_dg(ct, r[1], 1, 1).astype(r[0].dtype), _dg(r[0], ct, 0, 0).astype(r[1].dtype)))


@jax.custom_vjp
def mm_nt(a, b):
    return _dg(a, b, 1, 1)


mm_nt.defvjp(lambda a, b: (_dg(a, b, 1, 1), (a, b)),
             lambda r, ct: (_dg(ct, r[1], 1, 0).astype(r[0].dtype), _dg(ct, r[0], 0, 0).astype(r[1].dtype)))


@jax.custom_vjp
def mm_tn(a, b):
    return _dg(a, b, 0, 0)


mm_tn.defvjp(lambda a, b: (_dg(a, b, 0, 0), (a, b)),
             lambda r, ct: (_dg(r[1], ct, 1, 1).astype(r[0].dtype), _dg(r[0], ct, 1, 0).astype(r[1].dtype)))


@jax.custom_vjp
def _sigmoid(x):
    return 0.5 * jnp.tanh(0.5 * x) + 0.5


def _sigmoid_fwd(x):
    s = _sigmoid(x)
    return s, s


_sigmoid.defvjp(_sigmoid_fwd, lambda s, ct: (ct * (s - s * s),))


@jax.custom_vjp
def _silu(x):
    return x * _sigmoid(x)


def _silu_fwd(x):
    s = _sigmoid(x)
    return x * s, (x, s)


_silu.defvjp(_silu_fwd, lambda r, ct: (ct * (r[1] + r[0] * (r[1] - r[1] * r[1])),))


def _ln(x):
    mu = jnp.mean(x, -1, keepdims=True)
    xc = x - mu
    var = jnp.mean(xc * xc, -1, keepdims=True)
    return xc * lax.rsqrt(var + EPS)


def _rms(x, g):
    return x * lax.rsqrt(jnp.mean(x * x, -1, keepdims=True) + EPS) * g


@jax.custom_vjp
def _rope(x, cos_f, sin_a, sin_b):
    return x * cos_f + pltpu.roll(x, HD - 1, 1) * sin_a + pltpu.roll(x, 1, 1) * sin_b


def _rope_fwd(x, cos_f, sin_a, sin_b):
    return _rope(x, cos_f, sin_a, sin_b), (cos_f, sin_a, sin_b)


def _rope_bwd(r, ct):
    cos_f, sin_a, sin_b = r
    dx = ct * cos_f + pltpu.roll(ct * sin_a, 1, 1) + pltpu.roll(ct * sin_b, HD - 1, 1)
    return dx, jnp.zeros_like(cos_f), jnp.zeros_like(sin_a), jnp.zeros_like(sin_b)


_rope.defvjp(_rope_fwd, _rope_bwd)


def _row_ids(i, tm):
    return i * tm + lax.broadcasted_iota(jnp.int32, (tm, 1), 0)


def _partial_rows(ref, rows):
    n = len(rows)
    for k, r in enumerate(rows):
        ref[k:k + 1, :] = r
    ref[n:SUB, :] = jnp.zeros((SUB - n, ref.shape[-1]), f32)


def _matmul(a, b, mode, tm, tn, tk, name, out_dtype=f32, add=None, after=None):
    sect = a.ndim == 3
    a2 = (a.shape[1], a.shape[0] * a.shape[2]) if sect else a.shape
    if mode == "nn":
        (M, K), N = a2, b.shape[1]
        a_spec = pl.BlockSpec((None, tm, tk), lambda j, i, k: (k, i, 0)) if sect else pl.BlockSpec((tm, tk), lambda j, i, k: (i, k))
        b_spec = pl.BlockSpec((tk, tn), lambda j, i, k: (k, j))
        ca, cb = 1, 0
        assert not sect or tk == a.shape[2]
    elif mode == "nt":
        (M, K), N = a2, b.shape[0]
        assert not sect
        a_spec = pl.BlockSpec((tm, tk), lambda j, i, k: (i, k))
        b_spec = pl.BlockSpec((tn, tk), lambda j, i, k: (j, k))
        ca, cb = 1, 1
    else:
        (K, M), N = a2, b.shape[1]
        a_spec = pl.BlockSpec((None, tk, tm), lambda j, i, k: (i, k, 0)) if sect else pl.BlockSpec((tk, tm), lambda j, i, k: (k, i))
        b_spec = pl.BlockSpec((tk, tn), lambda j, i, k: (k, j))
        ca, cb = 0, 0
        assert not sect or tm == a.shape[2]
    assert M % tm == 0 and N % tn == 0 and K % tk == 0, (name, M, N, K, tm, tn, tk)
    nk = K // tk

    o_spec = pl.BlockSpec((tm, tn), lambda j, i, k: (i, j))

    def body(a_ref, b_ref, *rest):
        add_ref = rest[0] if add is not None else None
        o_ref, acc_ref = rest[-2:]
        k = pl.program_id(2)
        part = _dg(a_ref[...], b_ref[...], ca, cb)

        @pl.when(k == 0)
        def _():
            acc_ref[...] = part if add_ref is None else part + add_ref[...]

        @pl.when(k > 0)
        def _():
            acc_ref[...] += part

        @pl.when(k == nk - 1)
        def _():
            o_ref[...] = acc_ref[...].astype(o_ref.dtype)

    extra = ([] if add is None else [(o_spec, add)]) + ([] if after is None else [(pl.BlockSpec(memory_space=pl.ANY), after)])
    return pl.pallas_call(
        body, name=name, grid=(N // tn, M // tm, nk),
        in_specs=[a_spec, b_spec] + [s_ for s_, _ in extra], out_specs=o_spec,
        out_shape=jax.ShapeDtypeStruct((M, N), out_dtype),
        scratch_shapes=[pltpu.VMEM((tm, tn), f32)],
        compiler_params=_cparams(("parallel", "parallel", "arbitrary")),
    )(a, b, *[v_ for _, v_ in extra])


def _matmul_groups(a, b, tks, tm, name, after=None):
    keys = list(a)
    M = a[keys[0]].shape[-2]
    N = b[keys[0]].shape[1]
    count = {g: b[g].shape[0] // tks[g] for g in keys}
    first, total = {}, 0
    for g in keys:
        first[g], total = total, total + count[g]

    def k_of(g):
        return lambda s: jnp.clip(s - first[g], 0, count[g] - 1)

    a_specs = [pl.BlockSpec((None, tm, tks[g]), functools.partial(lambda i, s, kk: (kk(s), i, 0), kk=k_of(g)))
               if a[g].ndim == 3 else pl.BlockSpec((tm, tks[g]), functools.partial(lambda i, s, kk: (i, kk(s)), kk=k_of(g)))
               for g in keys]
    b_specs = [pl.BlockSpec((tks[g], N), functools.partial(lambda i, s, kk: (kk(s), 0), kk=k_of(g))) for g in keys]
    n = len(keys)

    def body(*refs):
        o_ref, acc_ref = refs[-2:]
        s = pl.program_id(1)

        @pl.when(s == 0)
        def _():
            acc_ref[...] = jnp.zeros_like(acc_ref)

        for j, g in enumerate(keys):
            @pl.when((s >= first[g]) & (s < first[g] + count[g]))
            def _(j=j):
                acc_ref[...] += _dg(refs[j][...], refs[n + j][...], 1, 0)

        @pl.when(s == total - 1)
        def _():
            o_ref[...] = acc_ref[...]

    extra = [] if after is None else [after]
    return pl.pallas_call(
        body, name=name, grid=(M // tm, total),
        in_specs=a_specs + b_specs + [pl.BlockSpec(memory_space=pl.ANY)] * len(extra),
        out_specs=pl.BlockSpec((tm, N), lambda i, s: (i, 0)),
        out_shape=jax.ShapeDtypeStruct((M, N), f32),
        scratch_shapes=[pltpu.VMEM((tm, N), f32)],
        compiler_params=_cparams(("parallel", "arbitrary")),
    )(*[a[g] for g in keys], *[b[g] for g in keys], *extra)


def _matmul_tn_batched(a, b, ns, name):
    B, K, M = a.shape
    N = b.shape[2] // ns

    def body(a_ref, b_ref, o_ref):
        o_ref[...] = _dg(a_ref[...], b_ref[...], 0, 0).astype(bf16)

    return pl.pallas_call(
        body, name=name, grid=(B, ns),
        in_specs=[pl.BlockSpec((None, K, M), lambda i, s: (i, 0, 0)), pl.BlockSpec((None, K, N), lambda i, s: (i, 0, s))],
        out_specs=pl.BlockSpec((None, None, M, N), lambda i, s: (s, i, 0, 0)),
        out_shape=jax.ShapeDtypeStruct((ns, B, M, N), bf16),
        compiler_params=_cparams(("parallel", "parallel")),
    )(a, b)


MOD_TN = 768


def _mod_fwd(cin, w_mod_l, b_mod_l, name):
    def body(c_ref, w_ref, b_ref, o_ref):
        o_ref[...] = mm(_silu(c_ref[...]), w_ref[...]) + b_ref[...]

    return pl.pallas_call(
        body, name=name, grid=(3 * D // MOD_TN,),
        in_specs=[pl.BlockSpec((SUB, D), lambda j: (0, 0)), pl.BlockSpec((None, D, MOD_TN), lambda j: (j, 0, 0)),
                  pl.BlockSpec((1, MOD_TN), lambda j: (0, j))],
        out_specs=pl.BlockSpec((SUB, MOD_TN), lambda j: (0, j)),
        out_shape=jax.ShapeDtypeStruct((SUB, 3 * D), f32),
        compiler_params=_cparams(("parallel",)),
    )(cin, w_mod_l, b_mod_l[None, :])


def _mod_bwd(cin, w_mods, dmodv):
    nj = 3 * D // MOD_TN

    def body(c_ref, *refs):
        g_ref, dw_ref, dc_ref = refs[DEPTH:]
        w = refs[0][...]
        for l in range(1, DEPTH):
            w = jnp.where(pl.program_id(0) == l, refs[l][...], w)
        _, vjp = jax.vjp(lambda c, w: mm(_silu(c), w), c_ref[...], w.astype(f32))
        dc, dw = vjp(g_ref[...])
        dw_ref[...] = dw.astype(bf16)
        dc_ref[...] = dc

    return pl.pallas_call(
        body, name="mod_bwd", grid=(DEPTH, nj),
        in_specs=[pl.BlockSpec((SUB, D), lambda l, j: (0, 0))]
        + [pl.BlockSpec((None, D, MOD_TN), lambda l, j: (j, 0, 0))] * DEPTH
        + [pl.BlockSpec((None, SUB, MOD_TN), lambda l, j: (l, 0, j))],
        out_specs=[pl.BlockSpec((None, None, D, MOD_TN), lambda l, j: (j, l, 0, 0)),
                   pl.BlockSpec((None, None, SUB, D), lambda l, j: (l, j, 0, 0))],
        out_shape=[jax.ShapeDtypeStruct((nj, DEPTH, D, MOD_TN), bf16),
                   jax.ShapeDtypeStruct((DEPTH, nj, SUB, D), f32)],
        compiler_params=_cparams(("parallel", "parallel")),
    )(cin, *w_mods, dmodv)


def _u_fn(h, m_l, m_c, isctx):
    n = _ln(h)
    shift = jnp.where(isctx, m_c[:, 0:D], m_l[:, 0:D])
    scale = jnp.where(isctx, m_c[:, D:2 * D], m_l[:, D:2 * D])
    return n * (1.0 + scale) + shift


def _ln_fwd(h, modv_l, tc, tm, name):
    T = h.shape[0]

    def body(h_ref, m_ref, u_ref):
        isctx = _row_ids(pl.program_id(0), tm) < tc
        u_ref[...] = _u_fn(h_ref[...], m_ref[0:1, :], m_ref[1:2, :], isctx).astype(bf16)

    return pl.pallas_call(
        body, name=name, grid=(T // tm,),
        in_specs=[pl.BlockSpec((tm, D), lambda i: (i, 0)), pl.BlockSpec((SUB, 3 * D), lambda i: (0, 0))],
        out_specs=pl.BlockSpec((tm, D), lambda i: (i, 0)),
        out_shape=jax.ShapeDtypeStruct((T, D), bf16),
        compiler_params=_cparams(("parallel",)),
    )(h, modv_l)


def _ln_bwd(du, h, dh_res, modv_l, tc, tm, name, latent_only=False):
    T = h.shape[0]
    nt, nct = T // tm, tc // tm

    def body(du_ref, h_ref, r_ref, m_ref, dh_ref, dm_ref):
        isctx = _row_ids(pl.program_id(0), tm) < tc
        _, vjp = jax.vjp(lambda h, ml, mc: _u_fn(h, ml, mc, isctx), h_ref[...], m_ref[0:1, :], m_ref[1:2, :])
        dh, dml, dmc = vjp(du_ref[...])
        dh_ref[...] = dh + r_ref[...]
        _partial_rows(dm_ref, [dml, dmc])

    dh_map = (lambda i: (jnp.maximum(i - nct, 0), 0)) if latent_only else (lambda i: (i, 0))
    return pl.pallas_call(
        body, name=name, grid=(nt,),
        in_specs=[pl.BlockSpec((tm, D), lambda i: (i, 0)), pl.BlockSpec((tm, D), lambda i: (i, 0)),
                  pl.BlockSpec((tm, D), lambda i: (i, 0)), pl.BlockSpec((SUB, 3 * D), lambda i: (0, 0))],
        out_specs=[pl.BlockSpec((tm, D), dh_map), pl.BlockSpec((None, SUB, 3 * D), lambda i: (i, 0, 0))],
        out_shape=[jax.ShapeDtypeStruct((T - tc if latent_only else T, D), f32), jax.ShapeDtypeStruct((nt, SUB, 3 * D), f32)],
        compiler_params=_cparams(("arbitrary",)),
    )(du, h, dh_res, modv_l)


def _prep_fn(q, k, qg, kg, cos_f, sin_a, sin_b):
    qs = [_rope(_rms(q[:, HD * i:HD * (i + 1)], qg), cos_f, sin_a, sin_b) * (HD ** -0.5) for i in range(A_HEADS)]
    ks = [_rope(_rms(k[:, HD * i:HD * (i + 1)], kg), cos_f, sin_a, sin_b) for i in range(A_HEADS // 2)]
    return jnp.concatenate(qs, 1), jnp.concatenate(ks, 1)


def _tok(tm, w, off):
    return pl.BlockSpec((tm, w), lambda i: (i, off // w))


def _vec(w):
    return pl.BlockSpec((1, w), lambda i: (0, 0))


def _prep_fwd(P, qg, kg, rope, tm, name):
    T = P.shape[0]

    def body(q_ref, k_ref, v_ref, qg_ref, kg_ref, c_ref, sa_ref, sb_ref, qn_ref, kn_ref, vb_ref):
        qn, kn = _prep_fn(q_ref[...].astype(f32), k_ref[...].astype(f32), qg_ref[...], kg_ref[...], c_ref[...], sa_ref[...],
                          sb_ref[...])
        qn_ref[...] = qn.astype(bf16)
        kn_ref[...] = kn.astype(bf16)
        vb_ref[...] = v_ref[...].astype(bf16)

    return pl.pallas_call(
        body, name=name, grid=(T // tm,),
        in_specs=[_tok(tm, 512, 0), _tok(tm, 256, A_K), _tok(tm, 256, A_V), _vec(HD), _vec(HD),
                  _tok(tm, HD, 0), _tok(tm, HD, 0), _tok(tm, HD, 0)],
        out_specs=[_tok(tm, 512, 0), _tok(tm, 256, 0), _tok(tm, 256, 0)],
        out_shape=[jax.ShapeDtypeStruct((T, 512), bf16), jax.ShapeDtypeStruct((T, 256), bf16),
                   jax.ShapeDtypeStruct((T, 256), bf16)],
        compiler_params=_cparams(("parallel",)),
    )(P, P, P, qg, kg, *rope)


def _prep_bwd(P, dqn, dkn, dv, qg, kg, rope, tm, name):
    T = P.shape[0]
    nt = T // tm

    def body(q_ref, k_ref, dq_ref, dk_ref, dv_ref, qg_ref, kg_ref, c_ref, sa_ref, sb_ref, o_ref, og_ref):
        tabs = (c_ref[...], sa_ref[...], sb_ref[...])
        _, vjp = jax.vjp(lambda q, k, a, b: _prep_fn(q, k, a, b, *tabs), q_ref[...].astype(f32), k_ref[...].astype(f32),
                         qg_ref[...], kg_ref[...])
        dq, dk, dqg, dkg = vjp((dq_ref[...], dk_ref[...]))
        o_ref[:, 0:A_K] = dq.astype(bf16)
        o_ref[:, A_K:A_V] = dk.astype(bf16)
        o_ref[:, A_V:W_A] = dv_ref[...].astype(bf16)
        _partial_rows(og_ref, [dqg, dkg])

    return pl.pallas_call(
        body, name=name, grid=(nt,),
        in_specs=[_tok(tm, 512, 0), _tok(tm, 256, A_K), _tok(tm, 512, 0), _tok(tm, 256, 0), _tok(tm, 256, 0),
                  _vec(HD), _vec(HD), _tok(tm, HD, 0), _tok(tm, HD, 0), _tok(tm, HD, 0)],
        out_specs=[_tok(tm, W_A, 0), pl.BlockSpec((None, SUB, HD), lambda i: (i, 0, 0))],
        out_shape=[jax.ShapeDtypeStruct((T, W_A), bf16), jax.ShapeDtypeStruct((nt, SUB, HD), f32)],
        compiler_params=_cparams(("parallel",)),
    )(P, P, dqn, dkn, dv, qg, kg, *rope)


def _attn_fn(q, k, v, lim):
    col = lax.broadcasted_iota(jnp.int32, (1, k.shape[0]), 1)
    s = mm_nt(q, k) + jnp.where(col < lim, 0.0, -1e30)
    m = lax.stop_gradient(jnp.max(s, -1, keepdims=True))
    e = jnp.exp(s - m)
    p = e * (1.0 / jnp.sum(e, -1, keepdims=True))
    return mm(p, v)


def _attn_fwd(qn, kn, vb, tc, tq, name):
    T = qn.shape[0]

    def body(q_ref, k_ref, v_ref, o_ref):
        lim = jnp.where(pl.program_id(1) * tq < tc, tc, T)
        o_ref[...] = _attn_fn(q_ref[...], k_ref[...], v_ref[...], lim)

    return pl.pallas_call(
        body, name=name, grid=(A_HEADS, T // tq),
        in_specs=[pl.BlockSpec((tq, HD), lambda h, i: (i, h)), pl.BlockSpec((T, HD), lambda h, i: (0, h // 2)),
                  pl.BlockSpec((T, HD), lambda h, i: (0, h // 2))],
        out_specs=pl.BlockSpec((tq, HD), lambda h, i: (i, h)),
        out_shape=jax.ShapeDtypeStruct((T, 512), f32),
        compiler_params=_cparams(("parallel", "parallel")),
    )(qn, kn, vb)


def _attn_bwd(qn, kn, vb, dya, tc, tq, name):
    T = qn.shape[0]

    def body(q_ref, k_ref, v_ref, g_ref, dq_ref, dk_ref, dv_ref):
        first = (pl.program_id(1) == 0) & (pl.program_id(2) == 0)
        lim = jnp.where(pl.program_id(2) * tq < tc, tc, T)
        _, vjp = jax.vjp(lambda q, k, v: _attn_fn(q, k, v, lim), q_ref[...].astype(f32), k_ref[...].astype(f32),
                         v_ref[...].astype(f32))
        dq, dk, dv = vjp(g_ref[...])
        dq_ref[...] = dq

        @pl.when(first)
        def _():
            dk_ref[...] = dk
            dv_ref[...] = dv

        @pl.when(jnp.logical_not(first))
        def _():
            dk_ref[...] += dk
            dv_ref[...] += dv

    qspec = pl.BlockSpec((tq, HD), lambda kv, g, i: (i, 2 * kv + g))
    kspec = pl.BlockSpec((T, HD), lambda kv, g, i: (0, kv))
    return pl.pallas_call(
        body, name=name, grid=(A_HEADS // 2, 2, T // tq),
        in_specs=[qspec, kspec, kspec, qspec], out_specs=[qspec, kspec, kspec],
        out_shape=[jax.ShapeDtypeStruct((T, 512), f32), jax.ShapeDtypeStruct((T, 256), f32),
                   jax.ShapeDtypeStruct((T, 256), f32)],
        compiler_params=_cparams(("parallel", "arbitrary", "arbitrary")),
    )(qn, kn, vb, dya)


def _conv_rows(tc, tl):
    return CONV_PAD + tc + CONV_PAD + tl + CONV_PAD


def _fill_pad(pad_ref, val, tc, tl):
    z = jnp.zeros((CONV_PAD, LANE), f32)
    pad_ref[0:CONV_PAD, :] = z
    pad_ref[CONV_PAD:CONV_PAD + tc, :] = val[0:tc]
    pad_ref[CONV_PAD + tc:2 * CONV_PAD + tc, :] = z
    pad_ref[2 * CONV_PAD + tc:2 * CONV_PAD + tc + tl, :] = val[tc:tc + tl]
    pad_ref[2 * CONV_PAD + tc + tl:3 * CONV_PAD + tc + tl, :] = z


def _conv_apply(pad_ref, w_ref, K, tc, tl, rc, emit, flip=False):
    half = K // 2
    for seg0, off, n in ((0, CONV_PAD, tc), (tc, 2 * CONV_PAD + tc, tl)):
        for r0 in range(0, n, rc):
            acc = None
            for k in range(K):
                sh = (half - k) if flip else (k - half)
                term = pad_ref[pl.ds(off + r0 + sh, rc), :] * w_ref[k:k + 1, :]
                acc = term if acc is None else acc + term
            emit(seg0 + r0, acc)


def _conv_wgrad(pad_ref, dy_ref, K, tc, tl, rc, dw_ref):
    half = K // 2
    for k in range(K):
        acc = jnp.zeros((1, LANE), f32)
        for seg0, off, n in ((0, CONV_PAD, tc), (tc, 2 * CONV_PAD + tc, tl)):
            for r0 in range(0, n, rc):
                acc = acc + jnp.sum(pad_ref[pl.ds(off + r0 + k - half, rc), :] * dy_ref[pl.ds(seg0 + r0, rc), :],
                                    axis=0, keepdims=True)
        dw_ref[k:k + 1, :] = acc


def _col(T, off):
    return pl.BlockSpec((T, LANE), lambda j: (0, off // LANE + j))


C_B, C_C, C_X, C_A, C_G = range(5)
N_SEC = 5


class _Sections:
    def __init__(self, refs):
        self.refs = refs

    def __getitem__(self, idx):
        rows, sec = idx
        return self.refs[sec][rows, :].astype(f32)

    def __setitem__(self, idx, val):
        rows, sec = idx
        self.refs[sec, rows, :] = val


def _sec_specs(T):
    return [pl.BlockSpec((T, LANE), functools.partial(lambda j, s: (0, s * (BRW // LANE) + j), s=s)) for s in range(N_SEC)]


def _conv_fwd(P, wb, wd, bd, tc, tl, rc, name):
    T = tc + tl

    def body(*refs):
        p_ref = _Sections(refs[:N_SEC])
        wb_ref, wd_ref, bd_ref, yb_ref, hh_ref, pad_ref = refs[N_SEC:]
        _fill_pad(pad_ref, p_ref[:, C_C] * p_ref[:, C_X], tc, tl)

        def emit_b(r0, y):
            yb_ref[pl.ds(r0, rc), :] = y * p_ref[pl.ds(r0, rc), C_B]

        _conv_apply(pad_ref, wb_ref, KB, tc, tl, rc, emit_b)
        _fill_pad(pad_ref, p_ref[:, C_A] * _sigmoid(p_ref[:, C_G]), tc, tl)

        def emit_d(r0, y):
            hh_ref[pl.ds(r0, rc), :] = y + bd_ref[...]

        _conv_apply(pad_ref, wd_ref, KD, tc, tl, rc, emit_d)

    return pl.pallas_call(
        body, name=name, grid=(BRW // LANE,),
        in_specs=_sec_specs(T) + [pl.BlockSpec((KB, LANE), lambda j: (0, j)), pl.BlockSpec((KD, LANE), lambda j: (0, j)),
                                  pl.BlockSpec((1, LANE), lambda j: (0, j))],
        out_specs=[_col(T, 0), _col(T, 0)],
        out_shape=[jax.ShapeDtypeStruct((T, BRW), f32), jax.ShapeDtypeStruct((T, BRW), f32)],
        scratch_shapes=[pltpu.VMEM((_conv_rows(tc, tl), LANE), f32)],
        compiler_params=_cparams(("parallel",)),
    )(*[P] * N_SEC, wb, wd, bd)


def _conv_bwd(P, dyb, dhh, wb, wd, tc, tl, rc, name):
    T = tc + tl

    def body(*refs):
        p_ref = _Sections(refs[:N_SEC])
        dyb_ref, dhh_ref, wb_ref, wd_ref, dp3_ref, dwb_ref, dwd_ref, dbd_ref, pad_ref, pad2_ref, tmp_ref = refs[N_SEC:]
        dp_ref = _Sections(dp3_ref)
        _fill_pad(pad_ref, p_ref[:, C_C] * p_ref[:, C_X], tc, tl)

        def emit_cv(r0, y):
            dp_ref[pl.ds(r0, rc), C_B] = (y * dyb_ref[pl.ds(r0, rc), :]).astype(bf16)

        _conv_apply(pad_ref, wb_ref, KB, tc, tl, rc, emit_cv)
        tmp_ref[...] = dyb_ref[...] * p_ref[:, C_B]
        _conv_wgrad(pad_ref, tmp_ref, KB, tc, tl, rc, dwb_ref)
        _fill_pad(pad2_ref, tmp_ref[...], tc, tl)

        def emit_ds(r0, y):
            dp_ref[pl.ds(r0, rc), C_C] = (y * p_ref[pl.ds(r0, rc), C_X]).astype(bf16)
            dp_ref[pl.ds(r0, rc), C_X] = (y * p_ref[pl.ds(r0, rc), C_C]).astype(bf16)

        _conv_apply(pad2_ref, wb_ref, KB, tc, tl, rc, emit_ds, flip=True)
        _fill_pad(pad_ref, p_ref[:, C_A] * _sigmoid(p_ref[:, C_G]), tc, tl)
        _conv_wgrad(pad_ref, dhh_ref, KD, tc, tl, rc, dwd_ref)
        dbd_ref[...] = jnp.sum(dhh_ref[...], axis=0, keepdims=True)
        _fill_pad(pad2_ref, dhh_ref[...], tc, tl)

        def emit_d2(r0, y):
            sg = _sigmoid(p_ref[pl.ds(r0, rc), C_G])
            a = p_ref[pl.ds(r0, rc), C_A]
            dp_ref[pl.ds(r0, rc), C_A] = (y * sg).astype(bf16)
            dp_ref[pl.ds(r0, rc), C_G] = (y * a * sg * (1.0 - sg)).astype(bf16)

        _conv_apply(pad2_ref, wd_ref, KD, tc, tl, rc, emit_d2, flip=True)

    return pl.pallas_call(
        body, name=name, grid=(BRW // LANE,),
        in_specs=_sec_specs(T) + [_col(T, 0), _col(T, 0),
                                  pl.BlockSpec((KB, LANE), lambda j: (0, j)), pl.BlockSpec((KD, LANE), lambda j: (0, j))],
        out_specs=[pl.BlockSpec((N_SEC, T, LANE), lambda j: (0, 0, j)), pl.BlockSpec((KB, LANE), lambda j: (0, j)),
                   pl.BlockSpec((KD, LANE), lambda j: (0, j)), pl.BlockSpec((1, LANE), lambda j: (0, j))],
        out_shape=[jax.ShapeDtypeStruct((N_SEC, T, BRW), bf16), jax.ShapeDtypeStruct((KB, BRW), f32),
                   jax.ShapeDtypeStruct((KD, BRW), f32), jax.ShapeDtypeStruct((1, BRW), f32)],
        scratch_shapes=[pltpu.VMEM((_conv_rows(tc, tl), LANE), f32), pltpu.VMEM((_conv_rows(tc, tl), LANE), f32),
                        pltpu.VMEM((T, LANE), f32)],
        compiler_params=_cparams(("parallel",)),
    )(*[P] * N_SEC, dyb, dhh, wb, wd)


def _gla_chunk(q, k, v, r, w2, b2, st, isfwd):
    z = mm(r, w2) + b2
    g = jax.nn.log_sigmoid(z[:, 0:C_KW] if isfwd else z[:, C_KW:2 * C_KW]) / C_TAU
    ri = lax.broadcasted_iota(jnp.int32, (CH, CH), 0)
    ci = lax.broadcasted_iota(jnp.int32, (CH, CH), 1)
    tri = ((ci <= ri) if isfwd else (ci >= ri)).astype(f32)
    cum = jnp.dot(tri, g, preferred_element_type=f32, precision=lax.Precision.HIGHEST)
    last = jnp.sum(g, axis=0, keepdims=True)
    q = q * (C_KW // C_HEADS) ** -0.5
    hv = lax.broadcasted_iota(jnp.int32, (BRW, C_KW), 0) // (BRW // C_HEADS)
    hk = lax.broadcasted_iota(jnp.int32, (BRW, C_KW), 1) // (C_KW // C_HEADS)
    st_new = st * jnp.exp(last) + jnp.where(hv == hk, mm_tn(v, k * jnp.exp(last - cum)), 0.0)
    o = mm_nt(q * jnp.exp(cum), st)
    rowi = lax.broadcasted_iota(jnp.int32, (CH, C_KW), 0)
    srow = lax.broadcasted_iota(jnp.int32, (C_HEADS * CH, C_KW), 0)
    slane = lax.broadcasted_iota(jnp.int32, (C_HEADS * CH, C_KW), 1)
    own_lanes = srow // CH == slane // (C_KW // C_HEADS)
    pos = lax.broadcasted_iota(jnp.int32, (C_HEADS * CH, CH), 0) % CH
    key = lax.broadcasted_iota(jnp.int32, (C_HEADS * CH, CH), 1)
    scores = jnp.zeros((C_HEADS * CH, CH), f32)
    for a in range(CH // GLA_SUB):
        idx = GLA_SUB * a - 1 if isfwd else GLA_SUB * (a + 1)
        ref = jnp.sum(jnp.where(rowi == idx, cum, 0.0), axis=0, keepdims=True)
        qa = q * jnp.exp(jnp.minimum(cum - ref, 0.0))
        ka = k * jnp.exp(jnp.minimum(ref - cum, GLA_CLAMP))
        s = mm_nt(jnp.where(own_lanes, jnp.concatenate([qa] * C_HEADS, axis=0), 0.0), ka)
        scores = scores + jnp.where(pos // GLA_SUB == a, s, 0.0)
    scores = jnp.where((key <= pos) if isfwd else (key >= pos), scores, 0.0)
    vw = BRW // C_HEADS
    o = o + jnp.concatenate([mm(scores[CH * hd:CH * (hd + 1)], v[:, vw * hd:vw * (hd + 1)]) for hd in range(C_HEADS)],
                            axis=1)
    return o, st_new


def _gla_chunk_of(d, n, nc, nch):
    back = jnp.where(n < nc, nc - 1 - n, nch - 1 - (n - nc))
    return jnp.where(d == 0, n, back)


def _gla_fwd(P, w2, b2, tc, name):
    T = P.shape[0]
    nch, nc = T // CH, tc // CH

    back = lambda n: _gla_chunk_of(1, n, nc, nch)

    def body(pf_ref, pb_ref, w_ref, b_ref, of_ref, ob_ref, ssf_ref, ssb_ref, stf_ref, stb_ref):
        @pl.when(pl.program_id(0) == 0)
        def _():
            stf_ref[...] = jnp.zeros_like(stf_ref)
            stb_ref[...] = jnp.zeros_like(stb_ref)

        for p_ref, o_ref, ss_ref, st_ref, isfwd in ((pf_ref, of_ref, ssf_ref, stf_ref, True),
                                                    (pb_ref, ob_ref, ssb_ref, stb_ref, False)):
            st = st_ref[...]
            ss_ref[...] = st
            p = p_ref[...].astype(f32)
            o, st_new = _gla_chunk(p[:, 0:G_K], p[:, G_K:G_V], p[:, G_V:G_R], p[:, G_R:W_G], w_ref[...], b_ref[...], st, isfwd)
            o_ref[...] = o
            st_ref[...] = st_new

    sd = jax.ShapeDtypeStruct
    return pl.pallas_call(
        body, name=name, grid=(nch,),
        in_specs=[pl.BlockSpec((CH, W_G), lambda n: (n, 0)), pl.BlockSpec((CH, W_G), lambda n: (back(n), 0)),
                  pl.BlockSpec((LANE, 512), lambda n: (0, 0)), pl.BlockSpec((1, 512), lambda n: (0, 0))],
        out_specs=[pl.BlockSpec((CH, BRW), lambda n: (n, 0)), pl.BlockSpec((CH, BRW), lambda n: (back(n), 0)),
                   pl.BlockSpec((None, BRW, C_KW), lambda n: (n, 0, 0)), pl.BlockSpec((None, BRW, C_KW), lambda n: (n, 0, 0))],
        out_shape=[sd((T, BRW), f32), sd((T, BRW), f32), sd((nch, BRW, C_KW), f32), sd((nch, BRW, C_KW), f32)],
        scratch_shapes=[pltpu.VMEM((BRW, C_KW), f32), pltpu.VMEM((BRW, C_KW), f32)],
        compiler_params=_cparams(("arbitrary",)),
    )(P, P, w2, b2)


def _gla_bwd(P, w2, b2, ssave, doc, tc, name):
    T = P.shape[0]
    nch, nc = T // CH, tc // CH

    fwd_chunk = lambda m: nch - 1 - m
    back_chunk = lambda m: _gla_chunk_of(1, nch - 1 - m, nc, nch)

    def body(pf_ref, pb_ref, w_ref, b_ref, ssf_ref, ssb_ref, gf_ref, gb_ref, dpf_ref, dpb_ref, dw_ref, db_ref,
             dstf_ref, dstb_ref):
        m = pl.program_id(0)

        @pl.when(m == 0)
        def _():
            dstf_ref[...] = jnp.zeros_like(dstf_ref)
            dstb_ref[...] = jnp.zeros_like(dstb_ref)

        dw_sum, db_sum = None, None
        for p_ref, ss_ref, g_ref, dp_ref, dst_ref, isfwd in ((pf_ref, ssf_ref, gf_ref, dpf_ref, dstf_ref, True),
                                                             (pb_ref, ssb_ref, gb_ref, dpb_ref, dstb_ref, False)):
            p = p_ref[...].astype(f32)
            _, vjp = jax.vjp(lambda q, k, v, r, w, b, st: _gla_chunk(q, k, v, r, w, b, st, isfwd),
                             p[:, 0:G_K], p[:, G_K:G_V], p[:, G_V:G_R], p[:, G_R:W_G], w_ref[...], b_ref[...], ss_ref[...])
            dq, dk, dv, dr, dw, db, dst = vjp((g_ref[...], dst_ref[...]))
            dp_ref[:, 0:G_K] = dq
            dp_ref[:, G_K:G_V] = dk
            dp_ref[:, G_V:G_R] = dv
            dp_ref[:, G_R:W_G] = dr
            dst_ref[...] = dst
            dw_sum = dw if dw_sum is None else dw_sum + dw
            db_sum = db if db_sum is None else db_sum + db

        @pl.when(m == 0)
        def _():
            dw_ref[...] = dw_sum
            _partial_rows(db_ref, [db_sum])

        @pl.when(m > 0)
        def _():
            dw_ref[...] += dw_sum
            db_ref[0:1, :] += db_sum

    ssf, ssb = ssave
    chunk_f = lambda w: pl.BlockSpec((CH, w), lambda m: (fwd_chunk(m), 0))
    chunk_b = lambda w: pl.BlockSpec((CH, w), lambda m: (back_chunk(m), 0))
    state = pl.BlockSpec((None, BRW, C_KW), lambda m: (nch - 1 - m, 0, 0))
    sd = jax.ShapeDtypeStruct
    return pl.pallas_call(
        body, name=name, grid=(nch,),
        in_specs=[chunk_f(W_G), chunk_b(W_G), pl.BlockSpec((LANE, 512), lambda m: (0, 0)), pl.BlockSpec((1, 512), lambda m: (0, 0)),
                  state, state, chunk_f(BRW), chunk_b(BRW)],
        out_specs=[chunk_f(W_G), chunk_b(W_G), pl.BlockSpec((LANE, 512), lambda m: (0, 0)), pl.BlockSpec((SUB, 512), lambda m: (0, 0))],
        out_shape=[sd((T, W_G), f32), sd((T, W_G), f32), sd((LANE, 512), f32), sd((SUB, 512), f32)],
        scratch_shapes=[pltpu.VMEM((BRW, C_KW), f32), pltpu.VMEM((BRW, C_KW), f32)],
        compiler_params=_cparams(("arbitrary",)),
    )(P, P, w2, b2, ssf, ssb, doc, doc)


def _sum_dirs(a, b, tm, name):
    T, W = a.shape

    def body(a_ref, b_ref, o_ref):
        o_ref[...] = (a_ref[...] + b_ref[...]).astype(bf16)

    spec = pl.BlockSpec((tm, W), lambda i: (i, 0))
    return pl.pallas_call(
        body, name=name, grid=(T // tm,), in_specs=[spec, spec], out_specs=spec,
        out_shape=jax.ShapeDtypeStruct((T, W), bf16),
        compiler_params=_cparams(("parallel",)),
    )(a, b)


def _merge_fn(h, m_l, m_c, isctx, ya, ga, yb, gb, of, ob, gc, hh, gd, mg, es, ey, cn, dng, dnb, lg, lb, wbr, wout):
    oc = of + ob
    yc = jnp.concatenate([_rms(oc[:, HD * i:HD * (i + 1)], cn[:, HD * i:HD * (i + 1)]) for i in range(C_HEADS)], 1)
    brs = [ya * _silu(ga), yb * _silu(gb), yc * _silu(gc), _silu(_ln(hh) * dng + dnb) * _silu(gd)]
    acc = None
    for i in range(4):
        t = _sigmoid(mg[:, D * i:D * (i + 1)]) * (mm(brs[i], wbr[i]) + es[i])
        acc = t if acc is None else acc + t
    y = mm(acc, wout) + ey
    gate = jnp.where(isctx, m_c[:, 2 * D:3 * D], m_l[:, 2 * D:3 * D])
    hn = _ln(ALPHA * h + gate * y) * lg + lb
    return hn, (brs, acc)


def _merge_specs(tm):
    t = lambda w, off=0: _tok(tm, w, off)
    return [t(D), pl.BlockSpec((SUB, 3 * D), lambda i: (0, 0)),
            t(BRW), t(BRW, M_GA), t(BRW), t(BRW, M_GB),
            t(BRW), t(BRW),
            t(BRW, M_GC), t(BRW), t(BRW, M_GD), t(4 * D, 0),
            _vec(BRW), _vec(BRW), _vec(BRW), _vec(D), _vec(D),
            pl.BlockSpec((4, BRW, D), lambda i: (0, 0, 0)), pl.BlockSpec((D, D), lambda i: (0, 0))]


def _merge_fwd(h, modv_l, ya, yb, o2, hh, P, cn, dng, dnb, lg, lb, wbr, wout, tc, tm, name):
    T = h.shape[0]

    def body(h_ref, m_ref, ya_ref, ga_ref, yb_ref, gb_ref, of_ref, ob_ref, gc_ref, hh_ref, gd_ref, mg_ref,
             cn_ref, dng_ref, dnb_ref, lg_ref, lb_ref, wbr_ref, wout_ref, o_ref):
        isctx = _row_ids(pl.program_id(0), tm) < tc
        zero = jnp.zeros((tm, D), f32)
        up = lambda r: r[...].astype(f32)
        hn, _ = _merge_fn(h_ref[...], m_ref[0:1, :], m_ref[1:2, :], isctx, ya_ref[...], up(ga_ref), yb_ref[...],
                          up(gb_ref), of_ref[...], ob_ref[...], up(gc_ref), hh_ref[...], up(gd_ref), up(mg_ref),
                          [zero] * 4, zero, cn_ref[...], dng_ref[...], dnb_ref[...], lg_ref[...], lb_ref[...],
                          [wbr_ref[i] for i in range(4)], wout_ref[...])
        o_ref[...] = hn

    return pl.pallas_call(
        body, name=name, grid=(T // tm,),
        in_specs=_merge_specs(tm), out_specs=_tok(tm, D, 0),
        out_shape=jax.ShapeDtypeStruct((T, D), f32),
        compiler_params=_cparams(("parallel",)),
    )(h, modv_l, ya, P, yb, P, o2[0], o2[1], P, hh, P, P, cn, dng, dnb, lg, lb, wbr, wout)


def _merge_bwd(dhn, h, modv_l, ya, yb, o2, hh, P, cn, dng, dnb, lg, lb, wbr, wout, tc, tm, name):
    T = h.shape[0]
    nt = T // tm

    def body(g_ref, h_ref, m_ref, ya_ref, ga_ref, yb_ref, gb_ref, of_ref, ob_ref, gc_ref, hh_ref, gd_ref, mg_ref,
             cn_ref, dng_ref, dnb_ref, lg_ref, lb_ref, wbr_ref, wout_ref,
             dh_ref, dm_ref, dya_ref, dyb_ref, doc_ref, dhh_ref, dp_ref,
             br_ref, z_ref, acc_ref, dy_ref, dv5_ref, dvd_ref):
        isctx = _row_ids(pl.program_id(0), tm) < tc
        zero = jnp.zeros((tm, D), f32)
        wbr_v = [wbr_ref[i] for i in range(4)]
        wout_v = wout_ref[...]
        up = lambda r: r[...].astype(f32)

        def fn(h, ml, mc, ya, ga, yb, gb, oc, gc, hh, gd, mg, e0, e1, e2, e3, ey, cn, dng, dnb, lg, lb):
            return _merge_fn(h, ml, mc, isctx, ya, ga, yb, gb, oc, jnp.zeros_like(oc), gc, hh, gd, mg,
                             [e0, e1, e2, e3], ey, cn, dng, dnb, lg, lb, wbr_v, wout_v)

        _, vjp, (brs, acc) = jax.vjp(
            fn, h_ref[...], m_ref[0:1, :], m_ref[1:2, :], ya_ref[...], up(ga_ref), yb_ref[...], up(gb_ref),
            of_ref[...] + ob_ref[...], up(gc_ref), hh_ref[...], up(gd_ref), up(mg_ref), zero, zero, zero, zero, zero,
            cn_ref[...], dng_ref[...], dnb_ref[...], lg_ref[...], lb_ref[...], has_aux=True)
        (dh, dml, dmc, dya, dga, dyb, dgb, doc, dgc, dhh, dgd, dmg, z0, z1, z2, z3, dy,
         dcn, ddng, ddnb, dlg, dlb) = vjp(g_ref[...])
        dh_ref[...] = dh
        _partial_rows(dm_ref, [dml, dmc])
        dya_ref[...] = dya
        dyb_ref[...] = dyb
        doc_ref[...] = doc
        dhh_ref[...] = dhh
        dp_ref[:, 0:M_GA] = dmg.astype(bf16)
        dp_ref[:, M_GA:M_GB] = dga.astype(bf16)
        dp_ref[:, M_GB:M_GC] = dgb.astype(bf16)
        dp_ref[:, M_GC:M_GD] = dgc.astype(bf16)
        dp_ref[:, M_GD:W_M] = dgd.astype(bf16)
        for i, z in enumerate((z0, z1, z2, z3)):
            br_ref[i] = brs[i].astype(bf16)
            z_ref[i] = z.astype(bf16)
        acc_ref[...] = acc.astype(bf16)
        dy_ref[...] = dy.astype(bf16)
        _partial_rows(dv5_ref, [dcn, ddng, ddnb])
        _partial_rows(dvd_ref, [dlg, dlb])

    t = lambda w: _tok(tm, w, 0)
    part = lambda w: pl.BlockSpec((None, SUB, w), lambda i: (i, 0, 0))
    sd = jax.ShapeDtypeStruct
    return pl.pallas_call(
        body, name=name, grid=(nt,),
        in_specs=[t(D)] + _merge_specs(tm),
        out_specs=[t(D), part(3 * D)] + [t(BRW)] * 4 + [t(W_M),
                   pl.BlockSpec((4, tm, BRW), lambda i: (0, i, 0)), pl.BlockSpec((4, tm, D), lambda i: (0, i, 0)),
                   t(D), t(D), part(BRW), part(D)],
        out_shape=[sd((T, D), f32), sd((nt, SUB, 3 * D), f32)] + [sd((T, BRW), f32)] * 4 + [sd((T, W_M), bf16),
                   sd((4, T, BRW), bf16), sd((4, T, D), bf16), sd((T, D), bf16), sd((T, D), bf16),
                   sd((nt, SUB, BRW), f32), sd((nt, SUB, D), f32)],
        compiler_params=_cparams(("parallel",)),
    )(dhn, h, modv_l, ya, P, yb, P, o2[0], o2[1], P, hh, P, P, cn, dng, dnb, lg, lb, wbr, wout)


def _loss_kernel(h, tgt, tc, tm, name):
    T = h.shape[0]
    nt = T // tm
    nct = tc // tm

    def body(h_ref, t_ref, d_ref, l_ref):
        i = pl.program_id(0)
        err = h_ref[...] - t_ref[...]
        lat = (i >= nct).astype(f32)
        d_ref[...] = err * (lat / D)
        l_ref[...] = jnp.zeros((SUB, LANE), f32) + lat * 0.5 * jnp.sum(err * err) / D

    return pl.pallas_call(
        body, name=name, grid=(nt,),
        in_specs=[pl.BlockSpec((tm, D), lambda i: (i, 0)),
                  pl.BlockSpec((tm, D), lambda i: (jnp.maximum(i - nct, 0), 0))],
        out_specs=[pl.BlockSpec((tm, D), lambda i: (i, 0)), pl.BlockSpec((None, SUB, LANE), lambda i: (i, 0, 0))],
        out_shape=[jax.ShapeDtypeStruct((T, D), f32), jax.ShapeDtypeStruct((nt, SUB, LANE), f32)],
        compiler_params=_cparams(("parallel",)),
    )(h, tgt)


def _rope_tables(tc, tl):
    t = jnp.arange(tl)
    inv = ROPE_THETA ** (-jnp.arange(0, HD // 2, 2, dtype=f32) / (HD // 2))
    ang = jnp.concatenate([(t // GRID_W).astype(f32)[:, None] * inv, (t % GRID_W).astype(f32)[:, None] * inv], -1)
    cos, sin = jnp.repeat(jnp.cos(ang), 2, axis=1), jnp.repeat(jnp.sin(ang), 2, axis=1)
    even = (jnp.arange(HD) % 2 == 0)[None, :]
    cos_f = jnp.concatenate([jnp.ones((tc, HD), f32), cos], 0)
    sin_a = jnp.concatenate([jnp.zeros((tc, HD), f32), jnp.where(even, -sin, 0.0)], 0)
    sin_b = jnp.concatenate([jnp.zeros((tc, HD), f32), jnp.where(even, 0.0, sin)], 0)
    return cos_f, sin_a, sin_b


N_CHIPS = 4
SHARD = N_IN // N_CHIPS


def _group_ranges():
    return dict(M=[(S_MG, 4 * D), (S_GA, BRW), (S_GB, BRW), (S_GC, BRW), (S_GD, BRW)], A=[(S_Q, W_A)],
                C=[(S_B, 3 * BRW), (S_DA, 2 * BRW)], G=[(S_CQ, 2 * C_KW + BRW), (S_R, 2 * C_RANK)])


def _group_weights(w4):
    out = {}
    for k, ranges in _group_ranges().items():
        parts = []
        for a, n in ranges:
            n = LANE if (k, a) == ("G", S_R) else n
            while n > 0:
                s, r = divmod(a, SHARD)
                m = min(n, SHARD - r)
                parts.append(w4[s, r:r + m])
                a, n = a + m, n - m
        out[k] = jnp.concatenate(parts, 0)
    return out


def _ungroup(g):
    secs = []
    for k, ranges in _group_ranges().items():
        off = 0
        for a, n in ranges:
            secs.append((a, g[k][off:off + n]))
            off += n
    return jnp.concatenate([v for _, v in sorted(secs, key=lambda t: t[0])], 0)


PROJ_TN = dict(M=2048, A=1024, C=1280, G=1152)
DU_TK = dict(M=2048, A=1024, C=BRW, G=1152)
DWP_TN = dict(M=768, A=1024, C=BRW, G=1152)


def _gate_weights(w2_l, gb_l):
    w = jnp.zeros((LANE, 2 * C_KW), f32)
    w = w.at[0:C_RANK, 0:C_KW].set(w2_l[0]).at[C_RANK:2 * C_RANK, C_KW:2 * C_KW].set(w2_l[1])
    return w, jnp.concatenate([gb_l[0], gb_l[1]])[None, :]


def _local_step(x1, c1, ctx1, tgt1, c_ctx, b_mod, weights_of, q_norm, k_norm, b_conv, w2, gb, c_norm, d_conv_w,
                d_conv_b, d_norm_g, d_norm_b, grads_done, ln_g, ln_b, tm, token=None):
    tc, tl = ctx1.shape[0], x1.shape[0]
    T = tc + tl
    rc = min(256, tc)
    tmb = tm // 2
    tmm = 768 if T % 768 == 0 else tm
    rope = _rope_tables(tc, tl)
    cin = jnp.concatenate([c1, c_ctx[None, :], jnp.zeros((SUB - 2, D), f32)], 0)
    if token is not None:
        cin = cin + token[:, 0:1]
    row = lambda v: v[None, :]

    h = jnp.concatenate([ctx1, x1], 0)
    saved, wp, w_br, w_out, w_mod, modv = [], *([None] * DEPTH for _ in range(5))
    for l in range(DEPTH):
        wp[l], merge_weights, w_mod[l] = weights_of(l, h)
        modv[l] = _mod_fwd(cin, w_mod[l], b_mod[l], f"mod_fwd{l}")
        u = _ln_fwd(h, modv[l], tc, tm, f"ln_fwd{l}")
        P = {k: _matmul(u, wp[l][k], "nt", tmm, PROJ_TN[k], D, f"proj{l}{k}", out_dtype=bf16) for k in GROUPS}
        qn, kn, vb = _prep_fwd(P["A"], row(q_norm[l]), row(k_norm[l]), rope, tm, f"prep_fwd{l}")
        ya = _attn_fwd(qn, kn, vb, tc, tm, f"attn_fwd{l}")
        yb, hh = _conv_fwd(P["C"], b_conv[l], d_conv_w[l], row(d_conv_b[l]), tc, tl, rc, f"conv_fwd{l}")
        w2p, b2p = _gate_weights(w2[l], gb[l])
        gla = _gla_fwd(P["G"], w2p, b2p, tc, f"gla_fwd{l}")
        o2, ssave = gla[:2], gla[2:]
        w_br[l], w_out[l] = merge_weights(o2[0])
        hn = _merge_fwd(h, modv[l], ya, yb, o2, hh, P["M"], row(c_norm[l]), row(d_norm_g[l]), row(d_norm_b[l]),
                        row(ln_g[l]), row(ln_b[l]), w_br[l], w_out[l], tc, tm, f"merge_fwd{l}")
        saved.append((h, u, P, qn, kn, vb, ya, yb, hh, o2, ssave, w2p, b2p))
        h = hn

    dh, lparts = _loss_kernel(h, tgt1, tc, tm, "loss")
    loss = jnp.sum(lparts[:, 0, 0])

    g = {k: [None] * DEPTH for k in ("wp", "q_norm", "k_norm", "b_conv", "w2", "gb", "c_norm", "d_conv_w", "d_conv_b",
                                     "d_norm_g", "d_norm_b", "w_br", "w_out", "ln_g", "ln_b", "modv")}
    for l in reversed(range(DEPTH)):
        h_in, u, P, qn, kn, vb, ya, yb, hh, o2, ssave, w2p, b2p = saved[l]
        dP = {}
        (dh_res, dm_mg, dya, dyb, doc, dhh, dP["M"], br, z, acc, dy, dv5, dvd) = _merge_bwd(
            dh, h_in, modv[l], ya, yb, o2, hh, P["M"], row(c_norm[l]), row(d_norm_g[l]), row(d_norm_b[l]),
            row(ln_g[l]), row(ln_b[l]), w_br[l], w_out[l], tc, tmb, f"merge_bwd{l}")
        g["w_br"][l] = _matmul_tn_batched(br, z, N_CHIPS, f"dwbr{l}")
        g["w_out"][l] = _matmul(acc, dy, "tn", D, D, T, f"dwout{l}", out_dtype=bf16)
        tk = grads_done(l, {k: g[k][l] for k in ("w_br", "w_out")})
        qg_l = row(q_norm[l]) if tk is None else row(q_norm[l]) + tk[0:1, :]
        v5 = jnp.sum(dv5, 0)
        g["c_norm"][l], g["d_norm_g"][l], g["d_norm_b"][l] = v5[0], v5[1], v5[2]
        vd = jnp.sum(dvd, 0)
        g["ln_g"][l], g["ln_b"][l] = vd[0], vd[1]
        dqn, dkn, dv = _attn_bwd(qn, kn, vb, dya, tc, tm, f"attn_bwd{l}")
        dP["A"], dqk = _prep_bwd(P["A"], dqn, dkn, dv, qg_l, row(k_norm[l]), rope, tm, f"prep_bwd{l}")
        dqk = jnp.sum(dqk, 0)
        g["q_norm"][l], g["k_norm"][l] = dqk[0], dqk[1]
        dP["C"], dwb, dwd, dbd = _conv_bwd(P["C"], dyb, dhh, b_conv[l], d_conv_w[l], tc, tl, rc, f"conv_bwd{l}")
        g["b_conv"][l], g["d_conv_w"][l], g["d_conv_b"][l] = dwb, dwd, dbd[0]
        dpf, dpb, dw2p, db2p = _gla_bwd(P["G"], w2p, b2p, ssave, doc, tc, f"gla_bwd{l}")
        dP["G"] = _sum_dirs(dpf, dpb, tm, f"gla_sum{l}")
        db2p = db2p[0]
        g["w2"][l] = jnp.stack([dw2p[0:C_RANK, 0:C_KW], dw2p[C_RANK:2 * C_RANK, C_KW:2 * C_KW]])
        g["gb"][l] = jnp.stack([db2p[0:C_KW], db2p[C_KW:2 * C_KW]])
        g["wp"][l] = {k: _matmul(dP[k], u, "tn", DWP_TN[k], D, T, f"dwp{l}{k}", out_dtype=bf16) for k in GROUPS}
        tk = grads_done(l, {"wp": g["wp"][l]})
        du = _matmul_groups(dP, wp[l], DU_TK, tmm, f"du{l}", after=tk)
        dh, dm_ln = _ln_bwd(du, h_in, dh_res, modv[l], tc, tm, f"ln_bwd{l}", latent_only=(l == 0))
        g["modv"][l] = jnp.sum(dm_mg, 0) + jnp.sum(dm_ln, 0)

    dmodv = jnp.stack(g.pop("modv"))
    g["w_mod"], dcin = _mod_bwd(cin, w_mod, dmodv)
    g["b_mod"] = dmodv[:, 0, :] + dmodv[:, 1, :]
    g["c_ctx"] = jnp.sum(dcin, (0, 1))[1]
    return loss, dh, g


HALF_TL = 256


TILE_BYTES = 1 << 20


def _row_tile(rows, cols, itemsize=4):
    tr = min(rows, 128)
    while rows % (2 * tr) == 0 and 2 * tr * cols * itemsize <= TILE_BYTES:
        tr *= 2
    return tr


def _adamw(w, g, m, v, name, tr=None, after=None):
    L, R, C = w.shape
    tr = _row_tile(R, C) if tr is None else tr
    if R % tr == 0:
        grid, spec = (L, R // tr), pl.BlockSpec((None, tr, C), lambda l, i: (l, i, 0))
    elif R * C * 4 <= (1 << 20):
        grid, spec = (L, 1), pl.BlockSpec((None, R, C), lambda l, i: (l, 0, 0))
    else:
        grid, spec = (L, C // HALF_TL), pl.BlockSpec((None, R, HALF_TL), lambda l, i: (l, 0, i))

    def body(w_ref, g_ref, m_ref, v_ref, *rest):
        go_ref, d_ref, nm_ref, nv_ref = rest[-4:]
        gg = g_ref[...]
        go_ref[...] = gg
        nm = B1 * m_ref[...] + (1.0 - B1) * gg
        nv = B2 * v_ref[...] + (1.0 - B2) * (gg * gg)
        m_hat = nm / (1.0 - B1 ** STEP)
        v_hat = nv / (1.0 - B2 ** STEP)
        d_ref[...] = -LR * (m_hat / (jnp.sqrt(v_hat) + AEPS) + WD * w_ref[...])
        nm_ref[...] = nm
        nv_ref[...] = nv

    return pl.pallas_call(
        body, name=name, grid=grid, in_specs=[spec] * 4 + ([] if after is None else [pl.BlockSpec(memory_space=pl.ANY)]),
        out_specs=[spec] * 4, out_shape=[jax.ShapeDtypeStruct((L, R, C), f32)] * 4,
        compiler_params=_cparams(("parallel", "parallel")),
    )(w, g, m, v, *([] if after is None else [after]))


MESH = pl.DeviceIdType.MESH
ANY = pl.BlockSpec(memory_space=pl.ANY)


def _place():
    x, y, c = lax.axis_index("x"), lax.axis_index("y"), lax.axis_index("c")
    chips = [(1 - x, y), (x, 1 - y), (1 - x, 1 - y)]
    return x, y, c, chips


def _half(ref, c, axis):
    n = ref.shape[axis] // 2
    last = axis in (-1, ref.ndim - 1)
    idx = [slice(None)] * ref.ndim
    idx[axis] = pl.ds(pl.multiple_of(c * n, LANE if last else SUB), n)
    return ref.at[tuple(idx)]


def _half_shape(shape, axis):
    s = list(shape)
    s[axis] //= 2
    return tuple(s)


def _all_gather(arrs, axes, name):
    n = len(arrs)
    tile = lambda a: LANE if axes[a] == 1 else 2 * SUB * (4 // arrs[a].dtype.itemsize)
    ring = [arrs[a].shape[axes[a]] % (4 * tile(a)) == 0 for a in range(n)]
    NS = 8

    def body(*refs):
        ins, outs = refs[:n], refs[n:2 * n]
        send, recv = refs[2 * n:]
        x, y, c, chips = _place()
        me, sib = 2 * x + y, (x, y, 1 - c)
        kx, ky, kd = (2 * chip[0] + chip[1] for chip in chips)
        nx, ny, nd = ((*chip, c) for chip in chips)

        def copy(a, k, blk, to, src=None):
            return pltpu.make_async_remote_copy(src_ref=blk if src is None else src, dst_ref=blk, send_sem=send.at[NS * a + k],
                                                recv_sem=recv.at[NS * a + k], device_id=to, device_id_type=MESH)

        half = lambda a, chip_idx, cc: _half(outs[a].at[chip_idx], cc, axes[a])
        quarter = lambda a, chip_idx, q: _half(half(a, chip_idx, c), q, axes[a])
        sent = []
        for a in range(n):
            mine = _half(ins[a], c, axes[a])
            sent += [copy(a, 6, outs[a].at[me], sib, src=ins[a]), copy(a, 0, half(a, me, c), nx, src=mine),
                     copy(a, 1, half(a, me, c), ny, src=mine)]
            if not ring[a]:
                sent.append(copy(a, 2, half(a, me, c), nd, src=mine))
        for cp in sent:
            cp.start()
        for a in range(n):
            copy(a, 0, half(a, kx, c), sib).wait_recv()
            late = [copy(a, 4, half(a, kx, c), sib)] + ([copy(a, 2, quarter(a, kx, 0), ny)] if ring[a] else [])
            for cp in late:
                cp.start()
            sent += late
            copy(a, 1, half(a, ky, c), sib).wait_recv()
            late = [copy(a, 5, half(a, ky, c), sib)] + ([copy(a, 3, quarter(a, ky, 1), nx)] if ring[a] else [])
            for cp in late:
                cp.start()
            sent += late
        for a in range(n):
            if ring[a]:
                copy(a, 2, quarter(a, kd, 0), sib).wait_recv()
                copy(a, 3, quarter(a, kd, 1), sib).wait_recv()
            else:
                copy(a, 2, half(a, kd, c), sib).wait_recv()
            fwd = copy(a, 7, half(a, kd, c), sib)
            fwd.start()
            sent.append(fwd)
        for a in range(n):
            copy(a, 6, outs[a].at[me], sib).wait_recv()
            for k, chip_idx in ((4, kx), (5, ky), (7, kd)):
                copy(a, k, half(a, chip_idx, 1 - c), sib).wait_recv()
        for cp in sent:
            cp.wait_send()

    return pl.pallas_call(
        body, name=name, in_specs=[ANY] * n, out_specs=[ANY] * n,
        out_shape=[jax.ShapeDtypeStruct((N_CHIPS,) + a.shape, a.dtype) for a in arrs],
        scratch_shapes=[pltpu.SemaphoreType.DMA((NS * n,)), pltpu.SemaphoreType.DMA((NS * n,))],
    )(*arrs)


def _add_half(gfull, land, cidx, axis, name, tr=None, out_dtype=bf16):
    _, hr, hc = land.shape
    if axis == 0:
        tr = min(tr, hr) if tr else _row_tile(hr, hc)
        nb, blk = hr // tr, (None, tr, hc)
        g_spec = pl.BlockSpec(blk, lambda s, i, cr: (s, cr[0] * nb + i, 0))
        l_spec = pl.BlockSpec(blk, lambda s, i, cr: (s, i, 0))
    else:
        nb, blk = hc // HALF_TL, (None, hr, HALF_TL)
        g_spec = pl.BlockSpec(blk, lambda s, i, cr: (s, 0, cr[0] * nb + i))
        l_spec = pl.BlockSpec(blk, lambda s, i, cr: (s, 0, i))

    def body(c_ref, g_ref, l_ref, o_ref):
        o_ref[...] = (g_ref[...].astype(f32) + l_ref[...].astype(f32)).astype(o_ref.dtype)

    return pl.pallas_call(
        body, name=name,
        grid_spec=pltpu.PrefetchScalarGridSpec(
            num_scalar_prefetch=1, grid=(N_CHIPS, nb), in_specs=[g_spec, l_spec], out_specs=l_spec),
        out_shape=jax.ShapeDtypeStruct((N_CHIPS, hr, hc), out_dtype),
        compiler_params=_cparams(("parallel", "parallel")),
    )(cidx, gfull, land)


def _sum_chips(land, own, place, axis, layer, into, name, tr=None):
    _, hr, hc = land.shape
    fresh = not hasattr(into, "dtype")
    shape = tuple(into) if fresh else into.shape
    if axis == 0:
        tr = min(tr, hr) if tr else _row_tile(hr, 4 * hc, 2)
        nb, blk = hr // tr, (tr, hc)
        l_map, m_map = (lambda i, p: (0, i, 0)), (lambda i, p: (p[0], i, 0))
        o_map = lambda i, p: (layer, p[1] * nb + i, 0)
    else:
        nb, blk = hc // HALF_TL, (hr, HALF_TL)
        l_map, m_map = (lambda i, p: (0, 0, i)), (lambda i, p: (p[0], 0, i))
        o_map = lambda i, p: (layer, 0, p[1] * nb + i)

    def body(p_ref, l_ref, o_ref, *rest):
        me = p_ref[0]
        mine = o_ref[...].astype(f32)
        acc = None
        for k in range(N_CHIPS):
            t = jnp.where(me == k, mine, l_ref[k].astype(f32))
            acc = t if acc is None else acc + t
        rest[-1][...] = acc

    return pl.pallas_call(
        body, name=name,
        grid_spec=pltpu.PrefetchScalarGridSpec(
            num_scalar_prefetch=1, grid=(nb,),
            in_specs=[pl.BlockSpec((N_CHIPS,) + blk, l_map), pl.BlockSpec((None,) + blk, m_map)] + ([] if fresh else [ANY]),
            out_specs=pl.BlockSpec((None,) + blk, o_map)),
        out_shape=jax.ShapeDtypeStruct(shape, f32),
        input_output_aliases={} if fresh else {3: 0},
        compiler_params=_cparams(("parallel",)),
    )(place, land, own, *([] if fresh else [into]))


def _sibling_fill(arrs, axes, name):
    n = len(arrs)

    def body(*refs):
        outs = refs[n:2 * n]
        send, recv = refs[2 * n:]
        x, y, c, _ = _place()
        cps = [pltpu.make_async_remote_copy(src_ref=_half(outs[a], c, axes[a] + 1), dst_ref=_half(outs[a], c, axes[a] + 1),
                                            send_sem=send.at[a], recv_sem=recv.at[a], device_id=(x, y, 1 - c),
                                            device_id_type=MESH) for a in range(n)]
        for cp in cps:
            cp.start()
        for a in range(n):
            blk = _half(outs[a], 1 - c, axes[a] + 1)
            pltpu.make_async_remote_copy(src_ref=blk, dst_ref=blk, send_sem=send.at[a], recv_sem=recv.at[a],
                                         device_id=(x, y, 1 - c), device_id_type=MESH).wait_recv()
        for cp in cps:
            cp.wait_send()

    return pl.pallas_call(
        body, name=name, in_specs=[ANY] * n, out_specs=[ANY] * n,
        out_shape=[jax.ShapeDtypeStruct(a.shape, a.dtype) for a in arrs],
        input_output_aliases={a: a for a in range(n)},
        scratch_shapes=[pltpu.SemaphoreType.DMA((n,)), pltpu.SemaphoreType.DMA((n,))],
    )(*arrs)


HBM = pl.BlockSpec(memory_space=pltpu.HBM)
SEM = pl.BlockSpec(memory_space=pltpu.SEMAPHORE)
EFFECT = pltpu.SideEffectType.DATAFLOW_SIDE_EFFECTING
PEERS = 7


def _split_copies(srcs, lands, send, recv, gather, axes=None):
    x, y, c, chips = _place()
    me = 2 * x + y
    if gather == "all":
        out = []
        for a in range(len(srcs)):
            for m in range(1, N_DEV):
                px, py, pc = [(1 - q) if (m >> s) & 1 else q for q, s in ((x, 2), (y, 1), (c, 0))]
                sems = dict(send_sem=send.at[PEERS * a + m - 1], recv_sem=recv.at[PEERS * a + m - 1], device_id=(px, py, pc),
                            device_id_type=MESH)
                out.append((pltpu.make_async_remote_copy(src_ref=srcs[a], dst_ref=lands[a].at[4 * x + 2 * y + c], **sems),
                            pltpu.make_async_remote_copy(src_ref=srcs[a], dst_ref=lands[a].at[4 * px + 2 * py + pc], **sems)))
        return out
    if axes is not None:
        out = []
        for a in range(len(srcs)):
            sems = dict(send_sem=send.at[PEERS * a], recv_sem=recv.at[PEERS * a], device_id=(x, y, 1 - c), device_id_type=MESH)
            copy = pltpu.make_async_remote_copy(src_ref=_half(srcs[a], 1 - c, axes[a] + 1), dst_ref=lands[a], **sems)
            out.append((copy, copy))
        return out
    peers = [((*chip, c), 2 * chip[0] + chip[1]) for chip in chips] + ([((x, y, 1 - c), me)] if gather else [])
    out = []
    for a in range(len(srcs)):
        for j, (dev, k) in enumerate(peers):
            src = srcs[a] if gather else srcs[a].at[k]
            sems = dict(send_sem=send.at[PEERS * a + j], recv_sem=recv.at[PEERS * a + j], device_id=dev, device_id_type=MESH)
            out.append((pltpu.make_async_remote_copy(src_ref=src, dst_ref=lands[a].at[me], **sems),
                        pltpu.make_async_remote_copy(src_ref=src, dst_ref=lands[a].at[k], **sems)))
    return out


def _split_start(srcs, gather, after, name, axes=None):
    n = len(srcs)
    if axes is not None:
        lands = [lax.empty(_half_shape(s.shape, axes[a] + 1), s.dtype) for a, s in enumerate(srcs)]
    else:
        lead = (N_DEV,) if gather == "all" else (N_CHIPS,) if gather else ()
        lands = [lax.empty(lead + s.shape, s.dtype) for s in srcs]

    def body(*refs):
        send, recv = refs[2 * n + 1], refs[2 * n + 2]
        for start, _ in _split_copies(refs[:n], refs[n:2 * n], send, recv, gather, axes):
            start.start()
        refs[-1][...] = jnp.zeros_like(refs[-1])

    sems = pltpu.SemaphoreType.DMA((PEERS * n,))
    hbm = lambda a: pltpu.with_memory_space_constraint(a, pltpu.HBM)
    out = pl.pallas_call(
        body, name=name,
        out_shape=(sems, sems, *[pltpu.HBM(a.shape, a.dtype) for a in srcs + lands], jax.ShapeDtypeStruct((SUB, LANE), f32)),
        in_specs=[HBM] * (2 * n) + [ANY], out_specs=(SEM, SEM, *[HBM] * (2 * n), pl.BlockSpec(memory_space=pltpu.VMEM)),
        input_output_aliases={i: 2 + i for i in range(2 * n)},
        compiler_params=pltpu.CompilerParams(has_side_effects=EFFECT),
    )(*[hbm(a) for a in srcs + lands], after)
    return out[0], out[1], list(out[2:2 + n]), list(out[2 + n:2 + 2 * n]), out[-1]


def _split_wait(send, recv, srcs, lands, gather, after, name, axes=None):
    n = len(srcs)

    def body(*refs):
        for start, arrival in _split_copies(refs[:n], refs[n:2 * n], refs[2 * n], refs[2 * n + 1], gather, axes):
            start.wait_send()
            arrival.wait_recv()

    out = pl.pallas_call(
        body, name=name, out_shape=[pltpu.HBM(a.shape, a.dtype) for a in srcs + lands],
        in_specs=[HBM] * (2 * n) + [SEM, SEM, ANY], out_specs=[HBM] * (2 * n),
        input_output_aliases={i: i for i in range(2 * n)},
        compiler_params=pltpu.CompilerParams(has_side_effects=EFFECT),
    )(*srcs, *lands, send, recv, after)
    return list(out[:n]), list(out[n:])


N_DEV = 8


def _sum_devices(v, land, me8, name):
    def body(p_ref, v_ref, l_ref, o_ref):
        acc = None
        for k in range(N_DEV):
            t = jnp.where(p_ref[0] == k, v_ref[...], l_ref[k])
            acc = t if acc is None else acc + t
        o_ref[...] = acc

    return pl.pallas_call(
        body, name=name,
        grid_spec=pltpu.PrefetchScalarGridSpec(
            num_scalar_prefetch=1, grid=(1,),
            in_specs=[pl.BlockSpec(v.shape, lambda i, p: (0, 0)), pl.BlockSpec(land.shape, lambda i, p: (0, 0, 0))],
            out_specs=pl.BlockSpec(v.shape, lambda i, p: (0, 0))),
        out_shape=jax.ShapeDtypeStruct(v.shape, f32),
        compiler_params=_cparams(("arbitrary",)),
    )(me8, v, land)


def _pack_small(arrs, mult=2 * SUB):
    flat = jnp.concatenate([a.reshape(-1) for a in arrs])
    rows = -(-flat.shape[0] // (LANE * mult)) * mult
    return jnp.pad(flat, (0, rows * LANE - flat.shape[0])).reshape(rows, LANE)


def _unpack_small(vec, shapes):
    flat, out, o = vec.reshape(-1), [], 0
    for s in shapes:
        n = int(np.prod(s))
        out.append(flat[o:o + n].reshape(s))
        o += n
    return out


REPL_SMALL = ("c_ctx", "b_mod", "q_norm", "k_norm", "c_norm", "d_conv_b", "d_norm_g", "d_norm_b", "ln_g", "ln_b")
SHARD_SMALL = ("b_conv", "c_gate_w2", "c_gate_b", "d_conv_w")
BIG = ("w_mod", "w_in", "w_br", "w_out")
ORDER = ("c_ctx", "w_mod", "b_mod", "w_in", "q_norm", "k_norm", "b_conv", "c_gate_w2", "c_gate_b", "c_norm", "d_conv_w",
         "d_conv_b", "d_norm_g", "d_norm_b", "w_br", "w_out", "ln_g", "ln_b")


def kernel(x, c, ctx, c_ctx, w_mod, b_mod, w_in, q_norm, k_norm, b_conv, c_gate_w2, c_gate_b, c_norm, d_conv_w, d_conv_b, d_norm_g, d_norm_b, w_br, w_out, ln_g, ln_b, loss_target, m_c_ctx, m_w_mod, m_b_mod, m_w_in, m_q_norm, m_k_norm, m_b_conv, m_c_gate_w2, m_c_gate_b, m_c_norm, m_d_conv_w, m_d_conv_b, m_d_norm_g, m_d_norm_b, m_w_br, m_w_out, m_ln_g, m_ln_b, v_c_ctx, v_w_mod, v_b_mod, v_w_in, v_q_norm, v_k_norm, v_b_conv, v_c_gate_w2, v_c_gate_b, v_c_norm, v_d_conv_w, v_d_conv_b, v_d_norm_g, v_d_norm_b, v_w_br, v_w_out, v_ln_g, v_ln_b):
    W = dict(c_ctx=c_ctx, w_mod=w_mod, b_mod=b_mod, w_in=w_in, q_norm=q_norm, k_norm=k_norm, b_conv=b_conv,
             c_gate_w2=c_gate_w2, c_gate_b=c_gate_b, c_norm=c_norm, d_conv_w=d_conv_w, d_conv_b=d_conv_b,
             d_norm_g=d_norm_g, d_norm_b=d_norm_b, w_br=w_br, w_out=w_out, ln_g=ln_g, ln_b=ln_b)
    M = dict(c_ctx=m_c_ctx, w_mod=m_w_mod, b_mod=m_b_mod, w_in=m_w_in, q_norm=m_q_norm, k_norm=m_k_norm, b_conv=m_b_conv,
             c_gate_w2=m_c_gate_w2, c_gate_b=m_c_gate_b, c_norm=m_c_norm, d_conv_w=m_d_conv_w, d_conv_b=m_d_conv_b,
             d_norm_g=m_d_norm_g, d_norm_b=m_d_norm_b, w_br=m_w_br, w_out=m_w_out, ln_g=m_ln_g, ln_b=m_ln_b)
    V = dict(c_ctx=v_c_ctx, w_mod=v_w_mod, b_mod=v_b_mod, w_in=v_w_in, q_norm=v_q_norm, k_norm=v_k_norm, b_conv=v_b_conv,
             c_gate_w2=v_c_gate_w2, c_gate_b=v_c_gate_b, c_norm=v_c_norm, d_conv_w=v_d_conv_w, d_conv_b=v_d_conv_b,
             d_norm_g=v_d_norm_g, d_norm_b=v_d_norm_b, w_br=v_w_br, w_out=v_w_out, ln_g=v_ln_g, ln_b=v_ln_b)
    chip = 2 * lax.axis_index("x") + lax.axis_index("y")
    cidx = lax.axis_index("c").astype(jnp.int32).reshape(1)

    place = jnp.stack([chip, lax.axis_index("c")]).astype(jnp.int32)

    AXIS = dict(w_in=1, w_mod=0, w_br=0, w_out=0)
    ex = dict(w_in=lambda a: jnp.swapaxes(a, 1, 2), w_mod=lambda a: a.reshape(1, DEPTH * D, -1),
              w_br=lambda a: a.reshape(DEPTH, 4 * BRW, -1), w_out=lambda a: a)
    Wx, Mx, Vx = ({k: ex[k](P_[k]) for k in BIG} for P_ in (W, M, V))

    LAYER, MERGE = ("w_in", "w_br", "w_out"), ("w_br", "w_out")
    small_shard = _pack_small([W[k] for k in SHARD_SMALL])
    keys0 = ("w_in", "w_mod")
    sent = lambda k, l: (w_mod[l] if k == "w_mod" else Wx[k][l]).astype(bf16)
    got = _all_gather([sent(k, 0) for k in keys0] + [small_shard], [AXIS[k] for k in keys0] + [0], "all_gather0")
    smalls = [_unpack_small(got[-1][s], [W[k].shape for k in SHARD_SMALL]) for s in range(N_CHIPS)]
    full = {k: jnp.concatenate([smalls[s][i] for s in range(N_CHIPS)], axis=-1) for i, k in enumerate(SHARD_SMALL)}
    ag0b = _split_start([sent(k, 0) for k in MERGE], True, got[0], "all_gather0b_start")
    ag1 = _split_start([sent(k, 1) for k in keys0], True, ag0b[4], "all_gather1_start")
    ag1b = _split_start([sent(k, 1) for k in MERGE], True, ag1[4], "all_gather1b_start")

    def merge_form(w_br4, w_out4):
        return jnp.moveaxis(w_br4.reshape(N_CHIPS, 4, BRW, D // N_CHIPS), 0, 2).reshape(4, BRW, D), w_out4.reshape(D, D)

    def weights_of(l, h):
        first = got if l == 0 else _split_wait(*ag1[:4], True, h, "all_gather1_wait")[1]
        flight = (ag0b, ag1b)[l]
        return (_group_weights(first[0]),
                lambda after: merge_form(*_split_wait(*flight[:4], True, after, f"all_gather{l}b_wait")[1]), first[1])

    red = {k: Wx[k].shape for k in BIG}
    flights, held = {}, {}

    def to_sibling(tag, l, pieces):
        keys = list(pieces)
        halves = _split_start([pieces[k] for k in keys], False, jnp.zeros((SUB, LANE), f32), f"rs_sibling_halves{tag}_start",
                              axes=[AXIS[k] for k in keys])
        flights[tag] = (l, keys, halves)
        return halves[4]

    def launch(tag, after):
        l, keys, halves = flights[tag]
        axes = [AXIS[k] for k in keys]
        pieces, land_a = _split_wait(*halves[:4], False, after, f"rs_sibling_halves{tag}_wait", axes=axes)
        pair = [_add_half(p, la, cidx, ax, f"rs_pair_sum{tag}_{k}") for k, p, la, ax in zip(keys, pieces, land_a, axes)]
        flights[tag] = (l, keys, _split_start(pair, False, jnp.zeros((SUB, LANE), f32), f"rs_chip_exchange{tag}_start"))
        return flights[tag][2][4]

    def land(tag, after):
        l, keys, flight = flights.pop(tag)
        pair, land_b = _split_wait(*flight[:4], False, after, f"rs_chip_exchange{tag}_wait")
        for k, lb, pr in zip(keys, land_b, pair):
            red[k] = _sum_chips(lb, pr, place, AXIS[k], l, red[k], f"rs_chip_sum{tag}_{k}")

    def grads_done(l, gl):
        if "wp" in gl:
            pieces = dict(w_in=_ungroup(gl["wp"]).reshape(N_CHIPS, SHARD, D))
            if l == 0:
                return launch("0b", gl["wp"]["A"]) + launch("0c", to_sibling("0c", 0, pieces))
            return to_sibling("1", 1, {**pieces, **held.pop(1)})
        pieces = dict(w_br=gl["w_br"].reshape(N_CHIPS, 4 * BRW, D // N_CHIPS), w_out=gl["w_out"].reshape(N_CHIPS, D // N_CHIPS, D))
        if l == 0:
            return launch("1", gl["w_out"]) + to_sibling("0b", 0, pieces)
        held[1] = pieces
        return None

    loss, gx, g = _local_step(
        x[0], c, ctx[0], loss_target[0], c_ctx, b_mod, weights_of, q_norm, k_norm, full["b_conv"],
        full["c_gate_w2"], full["c_gate_b"], c_norm, full["d_conv_w"], d_conv_b, d_norm_g, d_norm_b,
        grads_done, ln_g, ln_b, tm=256, token=ag1b[4])
    g["c_gate_w2"], g["c_gate_b"] = g.pop("w2"), g.pop("gb")
    loss = lax.psum(loss, ("x", "y", "c"))

    w_mod_pieces = g["w_mod"].reshape(N_CHIPS, DEPTH * D, 3 * D // N_CHIPS)
    g = {k: (jnp.stack(v) if isinstance(v, list) else v) for k, v in g.items() if k not in ("wp", "w_br", "w_out", "w_mod")}

    small_names = REPL_SMALL + SHARD_SMALL
    small = _split_start([_pack_small([g[k] for k in small_names])], "all", to_sibling("0d", 0, {"w_mod": w_mod_pieces}),
                         "all_reduce_small_start")

    grad, delta, new_m, new_v = {}, {}, {}, {}

    def adamw_big(keys, after):
        filled = _sibling_fill([red[k] for k in keys], [AXIS[k] for k in keys], "rs_sibling_fill_" + keys[0])
        for k, r in zip(keys, filled):
            back = (lambda a: jnp.swapaxes(a, 1, 2)) if k == "w_in" else (lambda a: a.reshape(W[k].shape))
            g_, d_, m_, v_ = _adamw(Wx[k], r, Mx[k], Vx[k], f"adamw_{k}", after=after)
            grad[k], delta[k], new_m[k], new_v[k] = back(g_), back(d_), back(m_), back(v_)
        return d_

    token = launch("0d", small[4])
    land("1", gx)
    land("0b", gx)
    last = adamw_big(MERGE, token)
    mine, landed = _split_wait(*small[:4], "all", last, "all_reduce_small_wait")
    me8 = (2 * chip + lax.axis_index("c")).astype(jnp.int32).reshape(1)
    gs = _sum_devices(mine[0], landed[0], me8, "all_reduce_small_sum")
    gsm = dict(zip(small_names, _unpack_small(gs, [g[k].shape for k in small_names])))
    for k in SHARD_SMALL:
        wdt = W[k].shape[-1]
        gsm[k] = lax.dynamic_slice_in_dim(gsm[k], chip * wdt, wdt, axis=gsm[k].ndim - 1)
    shapes = [W[k].shape for k in small_names]
    _, d_, m_, v_ = _adamw(*[_pack_small([P_[k] for k in small_names])[None] for P_ in (W, gsm, M, V)], "adamw_small", after=last)
    for k, dd, mm_, vv in zip(small_names, _unpack_small(d_, shapes), _unpack_small(m_, shapes), _unpack_small(v_, shapes)):
        grad[k], delta[k], new_m[k], new_v[k] = gsm[k], dd, mm_, vv
    land("0c", d_)
    last = adamw_big(("w_in",), None)
    land("0d", last)
    adamw_big(("w_mod",), None)

    return (loss, gx[None], *[grad[k] for k in ORDER], *[delta[k] for k in ORDER], *[new_m[k] for k in ORDER],
            *[new_v[k] for k in ORDER])
```

```python
import functools

import jax
import jax.numpy as jnp
import numpy as np
from jax import lax
from jax.experimental import pallas as pl
from jax.experimental.pallas import tpu as pltpu

f32 = jnp.float32
bf16 = jnp.bfloat16

D = 1024
DEPTH = 2
GRID_W = 64
BRW = 512
HD = 128
A_HEADS = 4
C_HEADS = 4
C_KW = 256
C_RANK = 16
C_TAU = 16.0
CH = 128
KB = 3
KD = 31
ALPHA = (2 * DEPTH) ** 0.25
EPS = 1e-6
ROPE_THETA = 10000.0
N_IN = 10784
LR, B1, B2, AEPS, WD, STEP = 0.001, 0.9, 0.999, 1e-08, 0.01, 10

W_M, W_A, W_C, W_G = 4 * D + 4 * BRW, 1024, 5 * BRW, 1152
GROUPS = ("M", "A", "C", "G")
M_GA, M_GB, M_GC, M_GD = 4 * D, 4 * D + BRW, 4 * D + 2 * BRW, 4 * D + 3 * BRW
A_K, A_V = 512, 768
G_K, G_V, G_R = 256, 512, 1024
S_Q, S_GA, S_B, S_C, S_X, S_GB, S_CQ, S_CV, S_GC, S_R, S_DA, S_DG, S_GD, S_MG = (
    0, 1024, 1536, 2048, 2560, 3072, 3584, 4096, 4608, 5120, 5152, 5664, 6176, 6688)

LANE = 128
SUB = 8
VMEM_LIMIT = 56 * 1024 * 1024
CONV_PAD = 16
GLA_SUB = 16
GLA_CLAMP = 60.0


def _cparams(sem, vmem=VMEM_LIMIT):
    return pltpu.CompilerParams(dimension_semantics=sem, vmem_limit_bytes=vmem)


def _dg(a, b, ca, cb):
    return lax.dot_general(a.astype(bf16), b.astype(bf16), (((ca,), (cb,)), ((), ())),
                           preferred_element_type=f32)


@jax.custom_vjp
def mm(a, b):
    return _dg(a, b, 1, 0)


mm.defvjp(lambda a, b: (_dg(a, b, 1, 0), (a, b)),
          lambda r, ct: (_dg(ct, r[1], 1, 1).astype(r[0].dtype), _dg(r[0], ct, 0, 0).astype(r[1].dtype)))


@jax.custom_vjp
def mm_nt(a, b):
    return _dg(a, b, 1, 1)


mm_nt.defvjp(lambda a, b: (_dg(a, b, 1, 1), (a, b)),
             lambda r, ct: (_dg(ct, r[1], 1, 0).astype(r[0].dtype), _dg(ct, r[0], 0, 0).astype(r[1].dtype)))


@jax.custom_vjp
def mm_tn(a, b):
    return _dg(a, b, 0, 0)


mm_tn.defvjp(lambda a, b: (_dg(a, b, 0, 0), (a, b)),
             lambda r, ct: (_dg(r[1], ct, 1, 1).astype(r[0].dtype), _dg(r[0], ct, 1, 0).astype(r[1].dtype)))


@jax.custom_vjp
def _sigmoid(x):
    return 0.5 * jnp.tanh(0.5 * x) + 0.5


def _sigmoid_fwd(x):
    s = _sigmoid(x)
    return s, s


_sigmoid.defvjp(_sigmoid_fwd, lambda s, ct: (ct * (s - s * s),))


@jax.custom_vjp
def _silu(x):
    return x * _sigmoid(x)


def _silu_fwd(x):
    s = _sigmoid(x)
    return x * s, (x, s)


_silu.defvjp(_silu_fwd, lambda r, ct: (ct * (r[1] + r[0] * (r[1] - r[1] * r[1])),))


def _ln(x):
    mu = jnp.mean(x, -1, keepdims=True)
    xc = x - mu
    var = jnp.mean(xc * xc, -1, keepdims=True)
    return xc * lax.rsqrt(var + EPS)


def _rms(x, g):
    return x * lax.rsqrt(jnp.mean(x * x, -1, keepdims=True) + EPS) * g


@jax.custom_vjp
def _rope(x, cos_f, sin_a, sin_b):
    return x * cos_f + pltpu.roll(x, HD - 1, 1) * sin_a + pltpu.roll(x, 1, 1) * sin_b


def _rope_fwd(x, cos_f, sin_a, sin_b):
    return _rope(x, cos_f, sin_a, sin_b), (cos_f, sin_a, sin_b)


def _rope_bwd(r, ct):
    cos_f, sin_a, sin_b = r
    dx = ct * cos_f + pltpu.roll(ct * sin_a, 1, 1) + pltpu.roll(ct * sin_b, HD - 1, 1)
    return dx, jnp.zeros_like(cos_f), jnp.zeros_like(sin_a), jnp.zeros_like(sin_b)


_rope.defvjp(_rope_fwd, _rope_bwd)


def _row_ids(i, tm):
    return i * tm + lax.broadcasted_iota(jnp.int32, (tm, 1), 0)


def _partial_rows(ref, rows):
    n = len(rows)
    for k, r in enumerate(rows):
        ref[k:k + 1, :] = r
    ref[n:SUB, :] = jnp.zeros((SUB - n, ref.shape[-1]), f32)


def _matmul(a, b, mode, tm, tn, tk, name, out_dtype=f32, add=None, after=None):
    sect = a.ndim == 3
    a2 = (a.shape[1], a.shape[0] * a.shape[2]) if sect else a.shape
    if mode == "nn":
        (M, K), N = a2, b.shape[1]
        a_spec = pl.BlockSpec((None, tm, tk), lambda j, i, k: (k, i, 0)) if sect else pl.BlockSpec((tm, tk), lambda j, i, k: (i, k))
        b_spec = pl.BlockSpec((tk, tn), lambda j, i, k: (k, j))
        ca, cb = 1, 0
        assert not sect or tk == a.shape[2]
    elif mode == "nt":
        (M, K), N = a2, b.shape[0]
        assert not sect
        a_spec = pl.BlockSpec((tm, tk), lambda j, i, k: (i, k))
        b_spec = pl.BlockSpec((tn, tk), lambda j, i, k: (j, k))
        ca, cb = 1, 1
    else:
        (K, M), N = a2, b.shape[1]
        a_spec = pl.BlockSpec((None, tk, tm), lambda j, i, k: (i, k, 0)) if sect else pl.BlockSpec((tk, tm), lambda j, i, k: (k, i))
        b_spec = pl.BlockSpec((tk, tn), lambda j, i, k: (k, j))
        ca, cb = 0, 0
        assert not sect or tm == a.shape[2]
    assert M % tm == 0 and N % tn == 0 and K % tk == 0, (name, M, N, K, tm, tn, tk)
    nk = K // tk

    o_spec = pl.BlockSpec((tm, tn), lambda j, i, k: (i, j))

    def body(a_ref, b_ref, *rest):
        add_ref = rest[0] if add is not None else None
        o_ref, acc_ref = rest[-2:]
        k = pl.program_id(2)
        part = _dg(a_ref[...], b_ref[...], ca, cb)

        @pl.when(k == 0)
        def _():
            acc_ref[...] = part if add_ref is None else part + add_ref[...]

        @pl.when(k > 0)
        def _():
            acc_ref[...] += part

        @pl.when(k == nk - 1)
        def _():
            o_ref[...] = acc_ref[...].astype(o_ref.dtype)

    extra = ([] if add is None else [(o_spec, add)]) + ([] if after is None else [(pl.BlockSpec(memory_space=pl.ANY), after)])
    return pl.pallas_call(
        body, name=name, grid=(N // tn, M // tm, nk),
        in_specs=[a_spec, b_spec] + [s_ for s_, _ in extra], out_specs=o_spec,
        out_shape=jax.ShapeDtypeStruct((M, N), out_dtype),
        scratch_shapes=[pltpu.VMEM((tm, tn), f32)],
        compiler_params=_cparams(("parallel", "parallel", "arbitrary")),
    )(a, b, *[v_ for _, v_ in extra])


def _matmul_groups(a, b, tks, tm, name, after=None):
    keys = list(a)
    M = a[keys[0]].shape[-2]
    N = b[keys[0]].shape[1]
    count = {g: b[g].shape[0] // tks[g] for g in keys}
    first, total = {}, 0
    for g in keys:
        first[g], total = total, total + count[g]

    def k_of(g):
        return lambda s: jnp.clip(s - first[g], 0, count[g] - 1)

    a_specs = [pl.BlockSpec((None, tm, tks[g]), functools.partial(lambda i, s, kk: (kk(s), i, 0), kk=k_of(g)))
               if a[g].ndim == 3 else pl.BlockSpec((tm, tks[g]), functools.partial(lambda i, s, kk: (i, kk(s)), kk=k_of(g)))
               for g in keys]
    b_specs = [pl.BlockSpec((tks[g], N), functools.partial(lambda i, s, kk: (kk(s), 0), kk=k_of(g))) for g in keys]
    n = len(keys)

    def body(*refs):
        o_ref, acc_ref = refs[-2:]
        s = pl.program_id(1)

        @pl.when(s == 0)
        def _():
            acc_ref[...] = jnp.zeros_like(acc_ref)

        for j, g in enumerate(keys):
            @pl.when((s >= first[g]) & (s < first[g] + count[g]))
            def _(j=j):
                acc_ref[...] += _dg(refs[j][...], refs[n + j][...], 1, 0)

        @pl.when(s == total - 1)
        def _():
            o_ref[...] = acc_ref[...]

    extra = [] if after is None else [after]
    return pl.pallas_call(
        body, name=name, grid=(M // tm, total),
        in_specs=a_specs + b_specs + [pl.BlockSpec(memory_space=pl.ANY)] * len(extra),
        out_specs=pl.BlockSpec((tm, N), lambda i, s: (i, 0)),
        out_shape=jax.ShapeDtypeStruct((M, N), f32),
        scratch_shapes=[pltpu.VMEM((tm, N), f32)],
        compiler_params=_cparams(("parallel", "arbitrary")),
    )(*[a[g] for g in keys], *[b[g] for g in keys], *extra)


def _matmul_tn_batched(a, b, ns, name):
    B, K, M = a.shape
    N = b.shape[2] // ns

    def body(a_ref, b_ref, o_ref):
        o_ref[...] = _dg(a_ref[...], b_ref[...], 0, 0).astype(bf16)

    return pl.pallas_call(
        body, name=name, grid=(B, ns),
        in_specs=[pl.BlockSpec((None, K, M), lambda i, s: (i, 0, 0)), pl.BlockSpec((None, K, N), lambda i, s: (i, 0, s))],
        out_specs=pl.BlockSpec((None, None, M, N), lambda i, s: (s, i, 0, 0)),
        out_shape=jax.ShapeDtypeStruct((ns, B, M, N), bf16),
        compiler_params=_cparams(("parallel", "parallel")),
    )(a, b)


MOD_TN = 768


def _mod_fwd(cin, w_mod_l, b_mod_l, name):
    def body(c_ref, w_ref, b_ref, o_ref):
        o_ref[...] = mm(_silu(c_ref[...]), w_ref[...]) + b_ref[...]

    return pl.pallas_call(
        body, name=name, grid=(3 * D // MOD_TN,),
        in_specs=[pl.BlockSpec((SUB, D), lambda j: (0, 0)), pl.BlockSpec((None, D, MOD_TN), lambda j: (j, 0, 0)),
                  pl.BlockSpec((1, MOD_TN), lambda j: (0, j))],
        out_specs=pl.BlockSpec((SUB, MOD_TN), lambda j: (0, j)),
        out_shape=jax.ShapeDtypeStruct((SUB, 3 * D), f32),
        compiler_params=_cparams(("parallel",)),
    )(cin, w_mod_l, b_mod_l[None, :])


def _mod_bwd(cin, w_mods, dmodv):
    nj = 3 * D // MOD_TN

    def body(c_ref, *refs):
        g_ref, dw_ref, dc_ref = refs[DEPTH:]
        w = refs[0][...]
        for l in range(1, DEPTH):
            w = jnp.where(pl.program_id(0) == l, refs[l][...], w)
        _, vjp = jax.vjp(lambda c, w: mm(_silu(c), w), c_ref[...], w.astype(f32))
        dc, dw = vjp(g_ref[...])
        dw_ref[...] = dw.astype(bf16)
        dc_ref[...] = dc

    return pl.pallas_call(
        body, name="mod_bwd", grid=(DEPTH, nj),
        in_specs=[pl.BlockSpec((SUB, D), lambda l, j: (0, 0))]
        + [pl.BlockSpec((None, D, MOD_TN), lambda l, j: (j, 0, 0))] * DEPTH
        + [pl.BlockSpec((None, SUB, MOD_TN), lambda l, j: (l, 0, j))],
        out_specs=[pl.BlockSpec((None, None, D, MOD_TN), lambda l, j: (j, l, 0, 0)),
                   pl.BlockSpec((None, None, SUB, D), lambda l, j: (l, j, 0, 0))],
        out_shape=[jax.ShapeDtypeStruct((nj, DEPTH, D, MOD_TN), bf16),
                   jax.ShapeDtypeStruct((DEPTH, nj, SUB, D), f32)],
        compiler_params=_cparams(("parallel", "parallel")),
    )(cin, *w_mods, dmodv)


def _u_fn(h, m_l, m_c, isctx):
    n = _ln(h)
    shift = jnp.where(isctx, m_c[:, 0:D], m_l[:, 0:D])
    scale = jnp.where(isctx, m_c[:, D:2 * D], m_l[:, D:2 * D])
    return n * (1.0 + scale) + shift


def _ln_fwd(h, modv_l, tc, tm, name):
    T = h.shape[0]

    def body(h_ref, m_ref, u_ref):
        isctx = _row_ids(pl.program_id(0), tm) < tc
        u_ref[...] = _u_fn(h_ref[...], m_ref[0:1, :], m_ref[1:2, :], isctx).astype(bf16)

    return pl.pallas_call(
        body, name=name, grid=(T // tm,),
        in_specs=[pl.BlockSpec((tm, D), lambda i: (i, 0)), pl.BlockSpec((SUB, 3 * D), lambda i: (0, 0))],
        out_specs=pl.BlockSpec((tm, D), lambda i: (i, 0)),
        out_shape=jax.ShapeDtypeStruct((T, D), bf16),
        compiler_params=_cparams(("parallel",)),
    )(h, modv_l)


def _ln_bwd(du, h, dh_res, modv_l, tc, tm, name, latent_only=False):
    T = h.shape[0]
    nt, nct = T // tm, tc // tm

    def body(du_ref, h_ref, r_ref, m_ref, dh_ref, dm_ref):
        isctx = _row_ids(pl.program_id(0), tm) < tc
        _, vjp = jax.vjp(lambda h, ml, mc: _u_fn(h, ml, mc, isctx), h_ref[...], m_ref[0:1, :], m_ref[1:2, :])
        dh, dml, dmc = vjp(du_ref[...])
        dh_ref[...] = dh + r_ref[...]
        _partial_rows(dm_ref, [dml, dmc])

    dh_map = (lambda i: (jnp.maximum(i - nct, 0), 0)) if latent_only else (lambda i: (i, 0))
    return pl.pallas_call(
        body, name=name, grid=(nt,),
        in_specs=[pl.BlockSpec((tm, D), lambda i: (i, 0)), pl.BlockSpec((tm, D), lambda i: (i, 0)),
                  pl.BlockSpec((tm, D), lambda i: (i, 0)), pl.BlockSpec((SUB, 3 * D), lambda i: (0, 0))],
        out_specs=[pl.BlockSpec((tm, D), dh_map), pl.BlockSpec((None, SUB, 3 * D), lambda i: (i, 0, 0))],
        out_shape=[jax.ShapeDtypeStruct((T - tc if latent_only else T, D), f32), jax.ShapeDtypeStruct((nt, SUB, 3 * D), f32)],
        compiler_params=_cparams(("arbitrary",)),
    )(du, h, dh_res, modv_l)


def _prep_fn(q, k, qg, kg, cos_f, sin_a, sin_b):
    qs = [_rope(_rms(q[:, HD * i:HD * (i + 1)], qg), cos_f, sin_a, sin_b) * (HD ** -0.5) for i in range(A_HEADS)]
    ks = [_rope(_rms(k[:, HD * i:HD * (i + 1)], kg), cos_f, sin_a, sin_b) for i in range(A_HEADS // 2)]
    return jnp.concatenate(qs, 1), jnp.concatenate(ks, 1)


def _tok(tm, w, off):
    return pl.BlockSpec((tm, w), lambda i: (i, off // w))


def _vec(w):
    return pl.BlockSpec((1, w), lambda i: (0, 0))


def _prep_fwd(P, qg, kg, rope, tm, name):
    T = P.shape[0]

    def body(q_ref, k_ref, v_ref, qg_ref, kg_ref, c_ref, sa_ref, sb_ref, qn_ref, kn_ref, vb_ref):
        qn, kn = _prep_fn(q_ref[...].astype(f32), k_ref[...].astype(f32), qg_ref[...], kg_ref[...], c_ref[...], sa_ref[...],
                          sb_ref[...])
        qn_ref[...] = qn.astype(bf16)
        kn_ref[...] = kn.astype(bf16)
        vb_ref[...] = v_ref[...].astype(bf16)

    return pl.pallas_call(
        body, name=name, grid=(T // tm,),
        in_specs=[_tok(tm, 512, 0), _tok(tm, 256, A_K), _tok(tm, 256, A_V), _vec(HD), _vec(HD),
                  _tok(tm, HD, 0), _tok(tm, HD, 0), _tok(tm, HD, 0)],
        out_specs=[_tok(tm, 512, 0), _tok(tm, 256, 0), _tok(tm, 256, 0)],
        out_shape=[jax.ShapeDtypeStruct((T, 512), bf16), jax.ShapeDtypeStruct((T, 256), bf16),
                   jax.ShapeDtypeStruct((T, 256), bf16)],
        compiler_params=_cparams(("parallel",)),
    )(P, P, P, qg, kg, *rope)


def _prep_bwd(P, dqn, dkn, dv, qg, kg, rope, tm, name):
    T = P.shape[0]
    nt = T // tm

    def body(q_ref, k_ref, dq_ref, dk_ref, dv_ref, qg_ref, kg_ref, c_ref, sa_ref, sb_ref, o_ref, og_ref):
        tabs = (c_ref[...], sa_ref[...], sb_ref[...])
        _, vjp = jax.vjp(lambda q, k, a, b: _prep_fn(q, k, a, b, *tabs), q_ref[...].astype(f32), k_ref[...].astype(f32),
                         qg_ref[...], kg_ref[...])
        dq, dk, dqg, dkg = vjp((dq_ref[...], dk_ref[...]))
        o_ref[:, 0:A_K] = dq.astype(bf16)
        o_ref[:, A_K:A_V] = dk.astype(bf16)
        o_ref[:, A_V:W_A] = dv_ref[...].astype(bf16)
        _partial_rows(og_ref, [dqg, dkg])

    return pl.pallas_call(
        body, name=name, grid=(nt,),
        in_specs=[_tok(tm, 512, 0), _tok(tm, 256, A_K), _tok(tm, 512, 0), _tok(tm, 256, 0), _tok(tm, 256, 0),
                  _vec(HD), _vec(HD), _tok(tm, HD, 0), _tok(tm, HD, 0), _tok(tm, HD, 0)],
        out_specs=[_tok(tm, W_A, 0), pl.BlockSpec((None, SUB, HD), lambda i: (i, 0, 0))],
        out_shape=[jax.ShapeDtypeStruct((T, W_A), bf16), jax.ShapeDtypeStruct((nt, SUB, HD), f32)],
        compiler_params=_cparams(("parallel",)),
    )(P, P, dqn, dkn, dv, qg, kg, *rope)


def _attn_fn(q, k, v, lim):
    col = lax.broadcasted_iota(jnp.int32, (1, k.shape[0]), 1)
    s = mm_nt(q, k) + jnp.where(col < lim, 0.0, -1e30)
    m = lax.stop_gradient(jnp.max(s, -1, keepdims=True))
    e = jnp.exp(s - m)
    p = e * (1.0 / jnp.sum(e, -1, keepdims=True))
    return mm(p, v)


def _attn_fwd(qn, kn, vb, tc, tq, name):
    T = qn.shape[0]

    def body(q_ref, k_ref, v_ref, o_ref):
        lim = jnp.where(pl.program_id(1) * tq < tc, tc, T)
        o_ref[...] = _attn_fn(q_ref[...], k_ref[...], v_ref[...], lim)

    return pl.pallas_call(
        body, name=name, grid=(A_HEADS, T // tq),
        in_specs=[pl.BlockSpec((tq, HD), lambda h, i: (i, h)), pl.BlockSpec((T, HD), lambda h, i: (0, h // 2)),
                  pl.BlockSpec((T, HD), lambda h, i: (0, h // 2))],
        out_specs=pl.BlockSpec((tq, HD), lambda h, i: (i, h)),
        out_shape=jax.ShapeDtypeStruct((T, 512), f32),
        compiler_params=_cparams(("parallel", "parallel")),
    )(qn, kn, vb)


def _attn_bwd(qn, kn, vb, dya, tc, tq, name):
    T = qn.shape[0]

    def body(q_ref, k_ref, v_ref, g_ref, dq_ref, dk_ref, dv_ref):
        first = (pl.program_id(1) == 0) & (pl.program_id(2) == 0)
        lim = jnp.where(pl.program_id(2) * tq < tc, tc, T)
        _, vjp = jax.vjp(lambda q, k, v: _attn_fn(q, k, v, lim), q_ref[...].astype(f32), k_ref[...].astype(f32),
                         v_ref[...].astype(f32))
        dq, dk, dv = vjp(g_ref[...])
        dq_ref[...] = dq

        @pl.when(first)
        def _():
            dk_ref[...] = dk
            dv_ref[...] = dv

        @pl.when(jnp.logical_not(first))
        def _():
            dk_ref[...] += dk
            dv_ref[...] += dv

    qspec = pl.BlockSpec((tq, HD), lambda kv, g, i: (i, 2 * kv + g))
    kspec = pl.BlockSpec((T, HD), lambda kv, g, i: (0, kv))
    return pl.pallas_call(
        body, name=name, grid=(A_HEADS // 2, 2, T // tq),
        in_specs=[qspec, kspec, kspec, qspec], out_specs=[qspec, kspec, kspec],
        out_shape=[jax.ShapeDtypeStruct((T, 512), f32), jax.ShapeDtypeStruct((T, 256), f32),
                   jax.ShapeDtypeStruct((T, 256), f32)],
        compiler_params=_cparams(("parallel", "arbitrary", "arbitrary")),
    )(qn, kn, vb, dya)


def _conv_rows(tc, tl):
    return CONV_PAD + tc + CONV_PAD + tl + CONV_PAD


def _fill_pad(pad_ref, val, tc, tl):
    z = jnp.zeros((CONV_PAD, LANE), f32)
    pad_ref[0:CONV_PAD, :] = z
    pad_ref[CONV_PAD:CONV_PAD + tc, :] = val[0:tc]
    pad_ref[CONV_PAD + tc:2 * CONV_PAD + tc, :] = z
    pad_ref[2 * CONV_PAD + tc:2 * CONV_PAD + tc + tl, :] = val[tc:tc + tl]
    pad_ref[2 * CONV_PAD + tc + tl:3 * CONV_PAD + tc + tl, :] = z


def _conv_apply(pad_ref, w_ref, K, tc, tl, rc, emit, flip=False):
    half = K // 2
    for seg0, off, n in ((0, CONV_PAD, tc), (tc, 2 * CONV_PAD + tc, tl)):
        for r0 in range(0, n, rc):
            acc = None
            for k in range(K):
                sh = (half - k) if flip else (k - half)
                term = pad_ref[pl.ds(off + r0 + sh, rc), :] * w_ref[k:k + 1, :]
                acc = term if acc is None else acc + term
            emit(seg0 + r0, acc)


def _conv_wgrad(pad_ref, dy_ref, K, tc, tl, rc, dw_ref):
    half = K // 2
    for k in range(K):
        acc = jnp.zeros((1, LANE), f32)
        for seg0, off, n in ((0, CONV_PAD, tc), (tc, 2 * CONV_PAD + tc, tl)):
            for r0 in range(0, n, rc):
                acc = acc + jnp.sum(pad_ref[pl.ds(off + r0 + k - half, rc), :] * dy_ref[pl.ds(seg0 + r0, rc), :],
                                    axis=0, keepdims=True)
        dw_ref[k:k + 1, :] = acc


def _col(T, off):
    return pl.BlockSpec((T, LANE), lambda j: (0, off // LANE + j))


C_B, C_C, C_X, C_A, C_G = range(5)
N_SEC = 5


class _Sections:
    def __init__(self, refs):
        self.refs = refs

    def __getitem__(self, idx):
        rows, sec = idx
        return self.refs[sec][rows, :].astype(f32)

    def __setitem__(self, idx, val):
        rows, sec = idx
        self.refs[sec, rows, :] = val


def _sec_specs(T):
    return [pl.BlockSpec((T, LANE), functools.partial(lambda j, s: (0, s * (BRW // LANE) + j), s=s)) for s in range(N_SEC)]


def _conv_fwd(P, wb, wd, bd, tc, tl, rc, name):
    T = tc + tl

    def body(*refs):
        p_ref = _Sections(refs[:N_SEC])
        wb_ref, wd_ref, bd_ref, yb_ref, hh_ref, pad_ref = refs[N_SEC:]
        _fill_pad(pad_ref, p_ref[:, C_C] * p_ref[:, C_X], tc, tl)

        def emit_b(r0, y):
            yb_ref[pl.ds(r0, rc), :] = y * p_ref[pl.ds(r0, rc), C_B]

        _conv_apply(pad_ref, wb_ref, KB, tc, tl, rc, emit_b)
        _fill_pad(pad_ref, p_ref[:, C_A] * _sigmoid(p_ref[:, C_G]), tc, tl)

        def emit_d(r0, y):
            hh_ref[pl.ds(r0, rc), :] = y + bd_ref[...]

        _conv_apply(pad_ref, wd_ref, KD, tc, tl, rc, emit_d)

    return pl.pallas_call(
        body, name=name, grid=(BRW // LANE,),
        in_specs=_sec_specs(T) + [pl.BlockSpec((KB, LANE), lambda j: (0, j)), pl.BlockSpec((KD, LANE), lambda j: (0, j)),
                                  pl.BlockSpec((1, LANE), lambda j: (0, j))],
        out_specs=[_col(T, 0), _col(T, 0)],
        out_shape=[jax.ShapeDtypeStruct((T, BRW), f32), jax.ShapeDtypeStruct((T, BRW), f32)],
        scratch_shapes=[pltpu.VMEM((_conv_rows(tc, tl), LANE), f32)],
        compiler_params=_cparams(("parallel",)),
    )(*[P] * N_SEC, wb, wd, bd)


def _conv_bwd(P, dyb, dhh, wb, wd, tc, tl, rc, name):
    T = tc + tl

    def body(*refs):
        p_ref = _Sections(refs[:N_SEC])
        dyb_ref, dhh_ref, wb_ref, wd_ref, dp3_ref, dwb_ref, dwd_ref, dbd_ref, pad_ref, pad2_ref, tmp_ref = refs[N_SEC:]
        dp_ref = _Sections(dp3_ref)
        _fill_pad(pad_ref, p_ref[:, C_C] * p_ref[:, C_X], tc, tl)

        def emit_cv(r0, y):
            dp_ref[pl.ds(r0, rc), C_B] = (y * dyb_ref[pl.ds(r0, rc), :]).astype(bf16)

        _conv_apply(pad_ref, wb_ref, KB, tc, tl, rc, emit_cv)
        tmp_ref[...] = dyb_ref[...] * p_ref[:, C_B]
        _conv_wgrad(pad_ref, tmp_ref, KB, tc, tl, rc, dwb_ref)
        _fill_pad(pad2_ref, tmp_ref[...], tc, tl)

        def emit_ds(r0, y):
            dp_ref[pl.ds(r0, rc), C_C] = (y * p_ref[pl.ds(r0, rc), C_X]).astype(bf16)
            dp_ref[pl.ds(r0, rc), C_X] = (y * p_ref[pl.ds(r0, rc), C_C]).astype(bf16)

        _conv_apply(pad2_ref, wb_ref, KB, tc, tl, rc, emit_ds, flip=True)
        _fill_pad(pad_ref, p_ref[:, C_A] * _sigmoid(p_ref[:, C_G]), tc, tl)
        _conv_wgrad(pad_ref, dhh_ref, KD, tc, tl, rc, dwd_ref)
        dbd_ref[...] = jnp.sum(dhh_ref[...], axis=0, keepdims=True)
        _fill_pad(pad2_ref, dhh_ref[...], tc, tl)

        def emit_d2(r0, y):
            sg = _sigmoid(p_ref[pl.ds(r0, rc), C_G])
            a = p_ref[pl.ds(r0, rc), C_A]
            dp_ref[pl.ds(r0, rc), C_A] = (y * sg).astype(bf16)
            dp_ref[pl.ds(r0, rc), C_G] = (y * a * sg * (1.0 - sg)).astype(bf16)

        _conv_apply(pad2_ref, wd_ref, KD, tc, tl, rc, emit_d2, flip=True)

    return pl.pallas_call(
        body, name=name, grid=(BRW // LANE,),
        in_specs=_sec_specs(T) + [_col(T, 0), _col(T, 0),
                                  pl.BlockSpec((KB, LANE), lambda j: (0, j)), pl.BlockSpec((KD, LANE), lambda j: (0, j))],
        out_specs=[pl.BlockSpec((N_SEC, T, LANE), lambda j: (0, 0, j)), pl.BlockSpec((KB, LANE), lambda j: (0, j)),
                   pl.BlockSpec((KD, LANE), lambda j: (0, j)), pl.BlockSpec((1, LANE), lambda j: (0, j))],
        out_shape=[jax.ShapeDtypeStruct((N_SEC, T, BRW), bf16), jax.ShapeDtypeStruct((KB, BRW), f32),
                   jax.ShapeDtypeStruct((KD, BRW), f32), jax.ShapeDtypeStruct((1, BRW), f32)],
        scratch_shapes=[pltpu.VMEM((_conv_rows(tc, tl), LANE), f32), pltpu.VMEM((_conv_rows(tc, tl), LANE), f32),
                        pltpu.VMEM((T, LANE), f32)],
        compiler_params=_cparams(("parallel",)),
    )(*[P] * N_SEC, dyb, dhh, wb, wd)


def _gla_chunk(q, k, v, r, w2, b2, st, isfwd):
    z = mm(r, w2) + b2
    g = jax.nn.log_sigmoid(z[:, 0:C_KW] if isfwd else z[:, C_KW:2 * C_KW]) / C_TAU
    ri = lax.broadcasted_iota(jnp.int32, (CH, CH), 0)
    ci = lax.broadcasted_iota(jnp.int32, (CH, CH), 1)
    tri = ((ci <= ri) if isfwd else (ci >= ri)).astype(f32)
    cum = jnp.dot(tri, g, preferred_element_type=f32, precision=lax.Precision.HIGHEST)
    last = jnp.sum(g, axis=0, keepdims=True)
    q = q * (C_KW // C_HEADS) ** -0.5
    hv = lax.broadcasted_iota(jnp.int32, (BRW, C_KW), 0) // (BRW // C_HEADS)
    hk = lax.broadcasted_iota(jnp.int32, (BRW, C_KW), 1) // (C_KW // C_HEADS)
    st_new = st * jnp.exp(last) + jnp.where(hv == hk, mm_tn(v, k * jnp.exp(last - cum)), 0.0)
    o = mm_nt(q * jnp.exp(cum), st)
    rowi = lax.broadcasted_iota(jnp.int32, (CH, C_KW), 0)
    srow = lax.broadcasted_iota(jnp.int32, (C_HEADS * CH, C_KW), 0)
    slane = lax.broadcasted_iota(jnp.int32, (C_HEADS * CH, C_KW), 1)
    own_lanes = srow // CH == slane // (C_KW // C_HEADS)
    pos = lax.broadcasted_iota(jnp.int32, (C_HEADS * CH, CH), 0) % CH
    key = lax.broadcasted_iota(jnp.int32, (C_HEADS * CH, CH), 1)
    scores = jnp.zeros((C_HEADS * CH, CH), f32)
    for a in range(CH // GLA_SUB):
        idx = GLA_SUB * a - 1 if isfwd else GLA_SUB * (a + 1)
        ref = jnp.sum(jnp.where(rowi == idx, cum, 0.0), axis=0, keepdims=True)
        qa = q * jnp.exp(jnp.minimum(cum - ref, 0.0))
        ka = k * jnp.exp(jnp.minimum(ref - cum, GLA_CLAMP))
        s = mm_nt(jnp.where(own_lanes, jnp.concatenate([qa] * C_HEADS, axis=0), 0.0), ka)
        scores = scores + jnp.where(pos // GLA_SUB == a, s, 0.0)
    scores = jnp.where((key <= pos) if isfwd else (key >= pos), scores, 0.0)
    vw = BRW // C_HEADS
    o = o + jnp.concatenate([mm(scores[CH * hd:CH * (hd + 1)], v[:, vw * hd:vw * (hd + 1)]) for hd in range(C_HEADS)],
                            axis=1)
    return o, st_new


def _gla_chunk_of(d, n, nc, nch):
    back = jnp.where(n < nc, nc - 1 - n, nch - 1 - (n - nc))
    return jnp.where(d == 0, n, back)


def _gla_fwd(P, w2, b2, tc, name):
    T = P.shape[0]
    nch, nc = T // CH, tc // CH

    back = lambda n: _gla_chunk_of(1, n, nc, nch)

    def body(pf_ref, pb_ref, w_ref, b_ref, of_ref, ob_ref, ssf_ref, ssb_ref, stf_ref, stb_ref):
        @pl.when(pl.program_id(0) == 0)
        def _():
            stf_ref[...] = jnp.zeros_like(stf_ref)
            stb_ref[...] = jnp.zeros_like(stb_ref)

        for p_ref, o_ref, ss_ref, st_ref, isfwd in ((pf_ref, of_ref, ssf_ref, stf_ref, True),
                                                    (pb_ref, ob_ref, ssb_ref, stb_ref, False)):
            st = st_ref[...]
            ss_ref[...] = st
            p = p_ref[...].astype(f32)
            o, st_new = _gla_chunk(p[:, 0:G_K], p[:, G_K:G_V], p[:, G_V:G_R], p[:, G_R:W_G], w_ref[...], b_ref[...], st, isfwd)
            o_ref[...] = o
            st_ref[...] = st_new

    sd = jax.ShapeDtypeStruct
    return pl.pallas_call(
        body, name=name, grid=(nch,),
        in_specs=[pl.BlockSpec((CH, W_G), lambda n: (n, 0)), pl.BlockSpec((CH, W_G), lambda n: (back(n), 0)),
                  pl.BlockSpec((LANE, 512), lambda n: (0, 0)), pl.BlockSpec((1, 512), lambda n: (0, 0))],
        out_specs=[pl.BlockSpec((CH, BRW), lambda n: (n, 0)), pl.BlockSpec((CH, BRW), lambda n: (back(n), 0)),
                   pl.BlockSpec((None, BRW, C_KW), lambda n: (n, 0, 0)), pl.BlockSpec((None, BRW, C_KW), lambda n: (n, 0, 0))],
        out_shape=[sd((T, BRW), f32), sd((T, BRW), f32), sd((nch, BRW, C_KW), f32), sd((nch, BRW, C_KW), f32)],
        scratch_shapes=[pltpu.VMEM((BRW, C_KW), f32), pltpu.VMEM((BRW, C_KW), f32)],
        compiler_params=_cparams(("arbitrary",)),
    )(P, P, w2, b2)


def _gla_bwd(P, w2, b2, ssave, doc, tc, name):
    T = P.shape[0]
    nch, nc = T // CH, tc // CH

    fwd_chunk = lambda m: nch - 1 - m
    back_chunk = lambda m: _gla_chunk_of(1, nch - 1 - m, nc, nch)

    def body(pf_ref, pb_ref, w_ref, b_ref, ssf_ref, ssb_ref, gf_ref, gb_ref, dpf_ref, dpb_ref, dw_ref, db_ref,
             dstf_ref, dstb_ref):
        m = pl.program_id(0)

        @pl.when(m == 0)
        def _():
            dstf_ref[...] = jnp.zeros_like(dstf_ref)
            dstb_ref[...] = jnp.zeros_like(dstb_ref)

        dw_sum, db_sum = None, None
        for p_ref, ss_ref, g_ref, dp_ref, dst_ref, isfwd in ((pf_ref, ssf_ref, gf_ref, dpf_ref, dstf_ref, True),
                                                             (pb_ref, ssb_ref, gb_ref, dpb_ref, dstb_ref, False)):
            p = p_ref[...].astype(f32)
            _, vjp = jax.vjp(lambda q, k, v, r, w, b, st: _gla_chunk(q, k, v, r, w, b, st, isfwd),
                             p[:, 0:G_K], p[:, G_K:G_V], p[:, G_V:G_R], p[:, G_R:W_G], w_ref[...], b_ref[...], ss_ref[...])
            dq, dk, dv, dr, dw, db, dst = vjp((g_ref[...], dst_ref[...]))
            dp_ref[:, 0:G_K] = dq
            dp_ref[:, G_K:G_V] = dk
            dp_ref[:, G_V:G_R] = dv
            dp_ref[:, G_R:W_G] = dr
            dst_ref[...] = dst
            dw_sum = dw if dw_sum is None else dw_sum + dw
            db_sum = db if db_sum is None else db_sum + db

        @pl.when(m == 0)
        def _():
            dw_ref[...] = dw_sum
            _partial_rows(db_ref, [db_sum])

        @pl.when(m > 0)
        def _():
            dw_ref[...] += dw_sum
            db_ref[0:1, :] += db_sum

    ssf, ssb = ssave
    chunk_f = lambda w: pl.BlockSpec((CH, w), lambda m: (fwd_chunk(m), 0))
    chunk_b = lambda w: pl.BlockSpec((CH, w), lambda m: (back_chunk(m), 0))
    state = pl.BlockSpec((None, BRW, C_KW), lambda m: (nch - 1 - m, 0, 0))
    sd = jax.ShapeDtypeStruct
    return pl.pallas_call(
        body, name=name, grid=(nch,),
        in_specs=[chunk_f(W_G), chunk_b(W_G), pl.BlockSpec((LANE, 512), lambda m: (0, 0)), pl.BlockSpec((1, 512), lambda m: (0, 0)),
                  state, state, chunk_f(BRW), chunk_b(BRW)],
        out_specs=[chunk_f(W_G), chunk_b(W_G), pl.BlockSpec((LANE, 512), lambda m: (0, 0)), pl.BlockSpec((SUB, 512), lambda m: (0, 0))],
        out_shape=[sd((T, W_G), f32), sd((T, W_G), f32), sd((LANE, 512), f32), sd((SUB, 512), f32)],
        scratch_shapes=[pltpu.VMEM((BRW, C_KW), f32), pltpu.VMEM((BRW, C_KW), f32)],
        compiler_params=_cparams(("arbitrary",)),
    )(P, P, w2, b2, ssf, ssb, doc, doc)


def _sum_dirs(a, b, tm, name):
    T, W = a.shape

    def body(a_ref, b_ref, o_ref):
        o_ref[...] = (a_ref[...] + b_ref[...]).astype(bf16)

    spec = pl.BlockSpec((tm, W), lambda i: (i, 0))
    return pl.pallas_call(
        body, name=name, grid=(T // tm,), in_specs=[spec, spec], out_specs=spec,
        out_shape=jax.ShapeDtypeStruct((T, W), bf16),
        compiler_params=_cparams(("parallel",)),
    )(a, b)


def _merge_fn(h, m_l, m_c, isctx, ya, ga, yb, gb, of, ob, gc, hh, gd, mg, es, ey, cn, dng, dnb, lg, lb, wbr, wout):
    oc = of + ob
    yc = jnp.concatenate([_rms(oc[:, HD * i:HD * (i + 1)], cn[:, HD * i:HD * (i + 1)]) for i in range(C_HEADS)], 1)
    brs = [ya * _silu(ga), yb * _silu(gb), yc * _silu(gc), _silu(_ln(hh) * dng + dnb) * _silu(gd)]
    acc = None
    for i in range(4):
        t = _sigmoid(mg[:, D * i:D * (i + 1)]) * (mm(brs[i], wbr[i]) + es[i])
        acc = t if acc is None else acc + t
    y = mm(acc, wout) + ey
    gate = jnp.where(isctx, m_c[:, 2 * D:3 * D], m_l[:, 2 * D:3 * D])
    hn = _ln(ALPHA * h + gate * y) * lg + lb
    return hn, (brs, acc)


def _merge_specs(tm):
    t = lambda w, off=0: _tok(tm, w, off)
    return [t(D), pl.BlockSpec((SUB, 3 * D), lambda i: (0, 0)),
            t(BRW), t(BRW, M_GA), t(BRW), t(BRW, M_GB),
            t(BRW), t(BRW),
            t(BRW, M_GC), t(BRW), t(BRW, M_GD), t(4 * D, 0),
            _vec(BRW), _vec(BRW), _vec(BRW), _vec(D), _vec(D),
            pl.BlockSpec((4, BRW, D), lambda i: (0, 0, 0)), pl.BlockSpec((D, D), lambda i: (0, 0))]


def _merge_fwd(h, modv_l, ya, yb, o2, hh, P, cn, dng, dnb, lg, lb, wbr, wout, tc, tm, name):
    T = h.shape[0]

    def body(h_ref, m_ref, ya_ref, ga_ref, yb_ref, gb_ref, of_ref, ob_ref, gc_ref, hh_ref, gd_ref, mg_ref,
             cn_ref, dng_ref, dnb_ref, lg_ref, lb_ref, wbr_ref, wout_ref, o_ref):
        isctx = _row_ids(pl.program_id(0), tm) < tc
        zero = jnp.zeros((tm, D), f32)
        up = lambda r: r[...].astype(f32)
        hn, _ = _merge_fn(h_ref[...], m_ref[0:1, :], m_ref[1:2, :], isctx, ya_ref[...], up(ga_ref), yb_ref[...],
                          up(gb_ref), of_ref[...], ob_ref[...], up(gc_ref), hh_ref[...], up(gd_ref), up(mg_ref),
                          [zero] * 4, zero, cn_ref[...], dng_ref[...], dnb_ref[...], lg_ref[...], lb_ref[...],
                          [wbr_ref[i] for i in range(4)], wout_ref[...])
        o_ref[...] = hn

    return pl.pallas_call(
        body, name=name, grid=(T // tm,),
        in_specs=_merge_specs(tm), out_specs=_tok(tm, D, 0),
        out_shape=jax.ShapeDtypeStruct((T, D), f32),
        compiler_params=_cparams(("parallel",)),
    )(h, modv_l, ya, P, yb, P, o2[0], o2[1], P, hh, P, P, cn, dng, dnb, lg, lb, wbr, wout)


def _merge_bwd(dhn, h, modv_l, ya, yb, o2, hh, P, cn, dng, dnb, lg, lb, wbr, wout, tc, tm, name):
    T = h.shape[0]
    nt = T // tm

    def body(g_ref, h_ref, m_ref, ya_ref, ga_ref, yb_ref, gb_ref, of_ref, ob_ref, gc_ref, hh_ref, gd_ref, mg_ref,
             cn_ref, dng_ref, dnb_ref, lg_ref, lb_ref, wbr_ref, wout_ref,
             dh_ref, dm_ref, dya_ref, dyb_ref, doc_ref, dhh_ref, dp_ref,
             br_ref, z_ref, acc_ref, dy_ref, dv5_ref, dvd_ref):
        isctx = _row_ids(pl.program_id(0), tm) < tc
        zero = jnp.zeros((tm, D), f32)
        wbr_v = [wbr_ref[i] for i in range(4)]
        wout_v = wout_ref[...]
        up = lambda r: r[...].astype(f32)

        def fn(h, ml, mc, ya, ga, yb, gb, oc, gc, hh, gd, mg, e0, e1, e2, e3, ey, cn, dng, dnb, lg, lb):
            return _merge_fn(h, ml, mc, isctx, ya, ga, yb, gb, oc, jnp.zeros_like(oc), gc, hh, gd, mg,
                             [e0, e1, e2, e3], ey, cn, dng, dnb, lg, lb, wbr_v, wout_v)

        _, vjp, (brs, acc) = jax.vjp(
            fn, h_ref[...], m_ref[0:1, :], m_ref[1:2, :], ya_ref[...], up(ga_ref), yb_ref[...], up(gb_ref),
            of_ref[...] + ob_ref[...], up(gc_ref), hh_ref[...], up(gd_ref), up(mg_ref), zero, zero, zero, zero, zero,
            cn_ref[...], dng_ref[...], dnb_ref[...], lg_ref[...], lb_ref[...], has_aux=True)
        (dh, dml, dmc, dya, dga, dyb, dgb, doc, dgc, dhh, dgd, dmg, z0, z1, z2, z3, dy,
         dcn, ddng, ddnb, dlg, dlb) = vjp(g_ref[...])
        dh_ref[...] = dh
        _partial_rows(dm_ref, [dml, dmc])
        dya_ref[...] = dya
        dyb_ref[...] = dyb
        doc_ref[...] = doc
        dhh_ref[...] = dhh
        dp_ref[:, 0:M_GA] = dmg.astype(bf16)
        dp_ref[:, M_GA:M_GB] = dga.astype(bf16)
        dp_ref[:, M_GB:M_GC] = dgb.astype(bf16)
        dp_ref[:, M_GC:M_GD] = dgc.astype(bf16)
        dp_ref[:, M_GD:W_M] = dgd.astype(bf16)
        for i, z in enumerate((z0, z1, z2, z3)):
            br_ref[i] = brs[i].astype(bf16)
            z_ref[i] = z.astype(bf16)
        acc_ref[...] = acc.astype(bf16)
        dy_ref[...] = dy.astype(bf16)
        _partial_rows(dv5_ref, [dcn, ddng, ddnb])
        _partial_rows(dvd_ref, [dlg, dlb])

    t = lambda w: _tok(tm, w, 0)
    part = lambda w: pl.BlockSpec((None, SUB, w), lambda i: (i, 0, 0))
    sd = jax.ShapeDtypeStruct
    return pl.pallas_call(
        body, name=name, grid=(nt,),
        in_specs=[t(D)] + _merge_specs(tm),
        out_specs=[t(D), part(3 * D)] + [t(BRW)] * 4 + [t(W_M),
                   pl.BlockSpec((4, tm, BRW), lambda i: (0, i, 0)), pl.BlockSpec((4, tm, D), lambda i: (0, i, 0)),
                   t(D), t(D), part(BRW), part(D)],
        out_shape=[sd((T, D), f32), sd((nt, SUB, 3 * D), f32)] + [sd((T, BRW), f32)] * 4 + [sd((T, W_M), bf16),
                   sd((4, T, BRW), bf16), sd((4, T, D), bf16), sd((T, D), bf16), sd((T, D), bf16),
                   sd((nt, SUB, BRW), f32), sd((nt, SUB, D), f32)],
        compiler_params=_cparams(("parallel",)),
    )(dhn, h, modv_l, ya, P, yb, P, o2[0], o2[1], P, hh, P, P, cn, dng, dnb, lg, lb, wbr, wout)


def _loss_kernel(h, tgt, tc, tm, name):
    T = h.shape[0]
    nt = T // tm
    nct = tc // tm

    def body(h_ref, t_ref, d_ref, l_ref):
        i = pl.program_id(0)
        err = h_ref[...] - t_ref[...]
        lat = (i >= nct).astype(f32)
        d_ref[...] = err * (lat / D)
        l_ref[...] = jnp.zeros((SUB, LANE), f32) + lat * 0.5 * jnp.sum(err * err) / D

    return pl.pallas_call(
        body, name=name, grid=(nt,),
        in_specs=[pl.BlockSpec((tm, D), lambda i: (i, 0)),
                  pl.BlockSpec((tm, D), lambda i: (jnp.maximum(i - nct, 0), 0))],
        out_specs=[pl.BlockSpec((tm, D), lambda i: (i, 0)), pl.BlockSpec((None, SUB, LANE), lambda i: (i, 0, 0))],
        out_shape=[jax.ShapeDtypeStruct((T, D), f32), jax.ShapeDtypeStruct((nt, SUB, LANE), f32)],
        compiler_params=_cparams(("parallel",)),
    )(h, tgt)


def _rope_tables(tc, tl):
    t = jnp.arange(tl)
    inv = ROPE_THETA ** (-jnp.arange(0, HD // 2, 2, dtype=f32) / (HD // 2))
    ang = jnp.concatenate([(t // GRID_W).astype(f32)[:, None] * inv, (t % GRID_W).astype(f32)[:, None] * inv], -1)
    cos, sin = jnp.repeat(jnp.cos(ang), 2, axis=1), jnp.repeat(jnp.sin(ang), 2, axis=1)
    even = (jnp.arange(HD) % 2 == 0)[None, :]
    cos_f = jnp.concatenate([jnp.ones((tc, HD), f32), cos], 0)
    sin_a = jnp.concatenate([jnp.zeros((tc, HD), f32), jnp.where(even, -sin, 0.0)], 0)
    sin_b = jnp.concatenate([jnp.zeros((tc, HD), f32), jnp.where(even, 0.0, sin)], 0)
    return cos_f, sin_a, sin_b


N_CHIPS = 4
SHARD = N_IN // N_CHIPS


def _group_ranges():
    return dict(M=[(S_MG, 4 * D), (S_GA, BRW), (S_GB, BRW), (S_GC, BRW), (S_GD, BRW)], A=[(S_Q, W_A)],
                C=[(S_B, 3 * BRW), (S_DA, 2 * BRW)], G=[(S_CQ, 2 * C_KW + BRW), (S_R, 2 * C_RANK)])


def _group_weights(w4):
    out = {}
    for k, ranges in _group_ranges().items():
        parts = []
        for a, n in ranges:
            n = LANE if (k, a) == ("G", S_R) else n
            while n > 0:
                s, r = divmod(a, SHARD)
                m = min(n, SHARD - r)
                parts.append(w4[s, r:r + m])
                a, n = a + m, n - m
        out[k] = jnp.concatenate(parts, 0)
    return out


def _ungroup(g):
    secs = []
    for k, ranges in _group_ranges().items():
        off = 0
        for a, n in ranges:
            secs.append((a, g[k][off:off + n]))
            off += n
    return jnp.concatenate([v for _, v in sorted(secs, key=lambda t: t[0])], 0)


PROJ_TN = dict(M=2048, A=1024, C=1280, G=1152)
DU_TK = dict(M=2048, A=1024, C=BRW, G=1152)
DWP_TN = dict(M=768, A=1024, C=BRW, G=1152)


def _gate_weights(w2_l, gb_l):
    w = jnp.zeros((LANE, 2 * C_KW), f32)
    w = w.at[0:C_RANK, 0:C_KW].set(w2_l[0]).at[C_RANK:2 * C_RANK, C_KW:2 * C_KW].set(w2_l[1])
    return w, jnp.concatenate([gb_l[0], gb_l[1]])[None, :]


def _local_step(x1, c1, ctx1, tgt1, c_ctx, b_mod, weights_of, q_norm, k_norm, b_conv, w2, gb, c_norm, d_conv_w,
                d_conv_b, d_norm_g, d_norm_b, grads_done, ln_g, ln_b, tm, token=None):
    tc, tl = ctx1.shape[0], x1.shape[0]
    T = tc + tl
    rc = min(256, tc)
    tmb = tm // 2
    tmm = 768 if T % 768 == 0 else tm
    rope = _rope_tables(tc, tl)
    cin = jnp.concatenate([c1, c_ctx[None, :], jnp.zeros((SUB - 2, D), f32)], 0)
    if token is not None:
        cin = cin + token[:, 0:1]
    row = lambda v: v[None, :]

    h = jnp.concatenate([ctx1, x1], 0)
    saved, wp, w_br, w_out, w_mod, modv = [], *([None] * DEPTH for _ in range(5))
    for l in range(DEPTH):
        wp[l], merge_weights, w_mod[l] = weights_of(l, h)
        modv[l] = _mod_fwd(cin, w_mod[l], b_mod[l], f"mod_fwd{l}")
        u = _ln_fwd(h, modv[l], tc, tm, f"ln_fwd{l}")
        P = {k: _matmul(u, wp[l][k], "nt", tmm, PROJ_TN[k], D, f"proj{l}{k}", out_dtype=bf16) for k in GROUPS}
        qn, kn, vb = _prep_fwd(P["A"], row(q_norm[l]), row(k_norm[l]), rope, tm, f"prep_fwd{l}")
        ya = _attn_fwd(qn, kn, vb, tc, tm, f"attn_fwd{l}")
        yb, hh = _conv_fwd(P["C"], b_conv[l], d_conv_w[l], row(d_conv_b[l]), tc, tl, rc, f"conv_fwd{l}")
        w2p, b2p = _gate_weights(w2[l], gb[l])
        gla = _gla_fwd(P["G"], w2p, b2p, tc, f"gla_fwd{l}")
        o2, ssave = gla[:2], gla[2:]
        w_br[l], w_out[l] = merge_weights(o2[0])
        hn = _merge_fwd(h, modv[l], ya, yb, o2, hh, P["M"], row(c_norm[l]), row(d_norm_g[l]), row(d_norm_b[l]),
                        row(ln_g[l]), row(ln_b[l]), w_br[l], w_out[l], tc, tm, f"merge_fwd{l}")
        saved.append((h, u, P, qn, kn, vb, ya, yb, hh, o2, ssave, w2p, b2p))
        h = hn

    dh, lparts = _loss_kernel(h, tgt1, tc, tm, "loss")
    loss = jnp.sum(lparts[:, 0, 0])

    g = {k: [None] * DEPTH for k in ("wp", "q_norm", "k_norm", "b_conv", "w2", "gb", "c_norm", "d_conv_w", "d_conv_b",
                                     "d_norm_g", "d_norm_b", "w_br", "w_out", "ln_g", "ln_b", "modv")}
    for l in reversed(range(DEPTH)):
        h_in, u, P, qn, kn, vb, ya, yb, hh, o2, ssave, w2p, b2p = saved[l]
        dP = {}
        (dh_res, dm_mg, dya, dyb, doc, dhh, dP["M"], br, z, acc, dy, dv5, dvd) = _merge_bwd(
            dh, h_in, modv[l], ya, yb, o2, hh, P["M"], row(c_norm[l]), row(d_norm_g[l]), row(d_norm_b[l]),
            row(ln_g[l]), row(ln_b[l]), w_br[l], w_out[l], tc, tmb, f"merge_bwd{l}")
        g["w_br"][l] = _matmul_tn_batched(br, z, N_CHIPS, f"dwbr{l}")
        g["w_out"][l] = _matmul(acc, dy, "tn", D, D, T, f"dwout{l}", out_dtype=bf16)
        tk = grads_done(l, {k: g[k][l] for k in ("w_br", "w_out")})
        qg_l = row(q_norm[l]) if tk is None else row(q_norm[l]) + tk[0:1, :]
        v5 = jnp.sum(dv5, 0)
        g["c_norm"][l], g["d_norm_g"][l], g["d_norm_b"][l] = v5[0], v5[1], v5[2]
        vd = jnp.sum(dvd, 0)
        g["ln_g"][l], g["ln_b"][l] = vd[0], vd[1]
        dqn, dkn, dv = _attn_bwd(qn, kn, vb, dya, tc, tm, f"attn_bwd{l}")
        dP["A"], dqk = _prep_bwd(P["A"], dqn, dkn, dv, qg_l, row(k_norm[l]), rope, tm, f"prep_bwd{l}")
        dqk = jnp.sum(dqk, 0)
        g["q_norm"][l], g["k_norm"][l] = dqk[0], dqk[1]
        dP["C"], dwb, dwd, dbd = _conv_bwd(P["C"], dyb, dhh, b_conv[l], d_conv_w[l], tc, tl, rc, f"conv_bwd{l}")
        g["b_conv"][l], g["d_conv_w"][l], g["d_conv_b"][l] = dwb, dwd, dbd[0]
        dpf, dpb, dw2p, db2p = _gla_bwd(P["G"], w2p, b2p, ssave, doc, tc, f"gla_bwd{l}")
        dP["G"] = _sum_dirs(dpf, dpb, tm, f"gla_sum{l}")
        db2p = db2p[0]
        g["w2"][l] = jnp.stack([dw2p[0:C_RANK, 0:C_KW], dw2p[C_RANK:2 * C_RANK, C_KW:2 * C_KW]])
        g["gb"][l] = jnp.stack([db2p[0:C_KW], db2p[C_KW:2 * C_KW]])
        g["wp"][l] = {k: _matmul(dP[k], u, "tn", DWP_TN[k], D, T, f"dwp{l}{k}", out_dtype=bf16) for k in GROUPS}
        tk = grads_done(l, {"wp": g["wp"][l]})
        du = _matmul_groups(dP, wp[l], DU_TK, tmm, f"du{l}", after=tk)
        dh, dm_ln = _ln_bwd(du, h_in, dh_res, modv[l], tc, tm, f"ln_bwd{l}", latent_only=(l == 0))
        g["modv"][l] = jnp.sum(dm_mg, 0) + jnp.sum(dm_ln, 0)

    dmodv = jnp.stack(g.pop("modv"))
    g["w_mod"], dcin = _mod_bwd(cin, w_mod, dmodv)
    g["b_mod"] = dmodv[:, 0, :] + dmodv[:, 1, :]
    g["c_ctx"] = jnp.sum(dcin, (0, 1))[1]
    return loss, dh, g


HALF_TL = 256


TILE_BYTES = 1 << 20


def _row_tile(rows, cols, itemsize=4):
    tr = min(rows, 128)
    while rows % (2 * tr) == 0 and 2 * tr * cols * itemsize <= TILE_BYTES:
        tr *= 2
    return tr


def _adamw(w, g, m, v, name, tr=None, after=None):
    L, R, C = w.shape
    tr = _row_tile(R, C) if tr is None else tr
    if R % tr == 0:
        grid, spec = (L, R // tr), pl.BlockSpec((None, tr, C), lambda l, i: (l, i, 0))
    elif R * C * 4 <= (1 << 20):
        grid, spec = (L, 1), pl.BlockSpec((None, R, C), lambda l, i: (l, 0, 0))
    else:
        grid, spec = (L, C // HALF_TL), pl.BlockSpec((None, R, HALF_TL), lambda l, i: (l, 0, i))

    def body(w_ref, g_ref, m_ref, v_ref, *rest):
        go_ref, d_ref, nm_ref, nv_ref = rest[-4:]
        gg = g_ref[...]
        go_ref[...] = gg
        nm = B1 * m_ref[...] + (1.0 - B1) * gg
        nv = B2 * v_ref[...] + (1.0 - B2) * (gg * gg)
        m_hat = nm / (1.0 - B1 ** STEP)
        v_hat = nv / (1.0 - B2 ** STEP)
        d_ref[...] = -LR * (m_hat / (jnp.sqrt(v_hat) + AEPS) + WD * w_ref[...])
        nm_ref[...] = nm
        nv_ref[...] = nv

    return pl.pallas_call(
        body, name=name, grid=grid, in_specs=[spec] * 4 + ([] if after is None else [pl.BlockSpec(memory_space=pl.ANY)]),
        out_specs=[spec] * 4, out_shape=[jax.ShapeDtypeStruct((L, R, C), f32)] * 4,
        compiler_params=_cparams(("parallel", "parallel")),
    )(w, g, m, v, *([] if after is None else [after]))


MESH = pl.DeviceIdType.MESH
ANY = pl.BlockSpec(memory_space=pl.ANY)


def _place():
    x, y, c = lax.axis_index("x"), lax.axis_index("y"), lax.axis_index("c")
    chips = [(1 - x, y), (x, 1 - y), (1 - x, 1 - y)]
    return x, y, c, chips


def _half(ref, c, axis):
    n = ref.shape[axis] // 2
    last = axis in (-1, ref.ndim - 1)
    idx = [slice(None)] * ref.ndim
    idx[axis] = pl.ds(pl.multiple_of(c * n, LANE if last else SUB), n)
    return ref.at[tuple(idx)]


def _half_shape(shape, axis):
    s = list(shape)
    s[axis] //= 2
    return tuple(s)


def _all_gather(arrs, axes, name):
    n = len(arrs)
    tile = lambda a: LANE if axes[a] == 1 else 2 * SUB * (4 // arrs[a].dtype.itemsize)
    ring = [arrs[a].shape[axes[a]] % (4 * tile(a)) == 0 for a in range(n)]
    NS = 8

    def body(*refs):
        ins, outs = refs[:n], refs[n:2 * n]
        send, recv = refs[2 * n:]
        x, y, c, chips = _place()
        me, sib = 2 * x + y, (x, y, 1 - c)
        kx, ky, kd = (2 * chip[0] + chip[1] for chip in chips)
        nx, ny, nd = ((*chip, c) for chip in chips)

        def copy(a, k, blk, to, src=None):
            return pltpu.make_async_remote_copy(src_ref=blk if src is None else src, dst_ref=blk, send_sem=send.at[NS * a + k],
                                                recv_sem=recv.at[NS * a + k], device_id=to, device_id_type=MESH)

        half = lambda a, chip_idx, cc: _half(outs[a].at[chip_idx], cc, axes[a])
        quarter = lambda a, chip_idx, q: _half(half(a, chip_idx, c), q, axes[a])
        sent = []
        for a in range(n):
            mine = _half(ins[a], c, axes[a])
            sent += [copy(a, 6, outs[a].at[me], sib, src=ins[a]), copy(a, 0, half(a, me, c), nx, src=mine),
                     copy(a, 1, half(a, me, c), ny, src=mine)]
            if not ring[a]:
                sent.append(copy(a, 2, half(a, me, c), nd, src=mine))
        for cp in sent:
            cp.start()
        for a in range(n):
            copy(a, 0, half(a, kx, c), sib).wait_recv()
            late = [copy(a, 4, half(a, kx, c), sib)] + ([copy(a, 2, quarter(a, kx, 0), ny)] if ring[a] else [])
            for cp in late:
                cp.start()
            sent += late
            copy(a, 1, half(a, ky, c), sib).wait_recv()
            late = [copy(a, 5, half(a, ky, c), sib)] + ([copy(a, 3, quarter(a, ky, 1), nx)] if ring[a] else [])
            for cp in late:
                cp.start()
            sent += late
        for a in range(n):
            if ring[a]:
                copy(a, 2, quarter(a, kd, 0), sib).wait_recv()
                copy(a, 3, quarter(a, kd, 1), sib).wait_recv()
            else:
                copy(a, 2, half(a, kd, c), sib).wait_recv()
            fwd = copy(a, 7, half(a, kd, c), sib)
            fwd.start()
            sent.append(fwd)
        for a in range(n):
            copy(a, 6, outs[a].at[me], sib).wait_recv()
            for k, chip_idx in ((4, kx), (5, ky), (7, kd)):
                copy(a, k, half(a, chip_idx, 1 - c), sib).wait_recv()
        for cp in sent:
            cp.wait_send()

    return pl.pallas_call(
        body, name=name, in_specs=[ANY] * n, out_specs=[ANY] * n,
        out_shape=[jax.ShapeDtypeStruct((N_CHIPS,) + a.shape, a.dtype) for a in arrs],
        scratch_shapes=[pltpu.SemaphoreType.DMA((NS * n,)), pltpu.SemaphoreType.DMA((NS * n,))],
    )(*arrs)


def _add_half(gfull, land, cidx, axis, name, tr=None, out_dtype=bf16):
    _, hr, hc = land.shape
    if axis == 0:
        tr = min(tr, hr) if tr else _row_tile(hr, hc)
        nb, blk = hr // tr, (None, tr, hc)
        g_spec = pl.BlockSpec(blk, lambda s, i, cr: (s, cr[0] * nb + i, 0))
        l_spec = pl.BlockSpec(blk, lambda s, i, cr: (s, i, 0))
    else:
        nb, blk = hc // HALF_TL, (None, hr, HALF_TL)
        g_spec = pl.BlockSpec(blk, lambda s, i, cr: (s, 0, cr[0] * nb + i))
        l_spec = pl.BlockSpec(blk, lambda s, i, cr: (s, 0, i))

    def body(c_ref, g_ref, l_ref, o_ref):
        o_ref[...] = (g_ref[...].astype(f32) + l_ref[...].astype(f32)).astype(o_ref.dtype)

    return pl.pallas_call(
        body, name=name,
        grid_spec=pltpu.PrefetchScalarGridSpec(
            num_scalar_prefetch=1, grid=(N_CHIPS, nb), in_specs=[g_spec, l_spec], out_specs=l_spec),
        out_shape=jax.ShapeDtypeStruct((N_CHIPS, hr, hc), out_dtype),
        compiler_params=_cparams(("parallel", "parallel")),
    )(cidx, gfull, land)


def _sum_chips(land, own, place, axis, layer, into, name, tr=None):
    _, hr, hc = land.shape
    fresh = not hasattr(into, "dtype")
    shape = tuple(into) if fresh else into.shape
    if axis == 0:
        tr = min(tr, hr) if tr else _row_tile(hr, 4 * hc, 2)
        nb, blk = hr // tr, (tr, hc)
        l_map, m_map = (lambda i, p: (0, i, 0)), (lambda i, p: (p[0], i, 0))
        o_map = lambda i, p: (layer, p[1] * nb + i, 0)
    else:
        nb, blk = hc // HALF_TL, (hr, HALF_TL)
        l_map, m_map = (lambda i, p: (0, 0, i)), (lambda i, p: (p[0], 0, i))
        o_map = lambda i, p: (layer, 0, p[1] * nb + i)

    def body(p_ref, l_ref, o_ref, *rest):
        me = p_ref[0]
        mine = o_ref[...].astype(f32)
        acc = None
        for k in range(N_CHIPS):
            t = jnp.where(me == k, mine, l_ref[k].astype(f32))
            acc = t if acc is None else acc + t
        rest[-1][...] = acc

    return pl.pallas_call(
        body, name=name,
        grid_spec=pltpu.PrefetchScalarGridSpec(
            num_scalar_prefetch=1, grid=(nb,),
            in_specs=[pl.BlockSpec((N_CHIPS,) + blk, l_map), pl.BlockSpec((None,) + blk, m_map)] + ([] if fresh else [ANY]),
            out_specs=pl.BlockSpec((None,) + blk, o_map)),
        out_shape=jax.ShapeDtypeStruct(shape, f32),
        input_output_aliases={} if fresh else {3: 0},
        compiler_params=_cparams(("parallel",)),
    )(place, land, own, *([] if fresh else [into]))


def _sibling_fill(arrs, axes, name):
    n = len(arrs)

    def body(*refs):
        outs = refs[n:2 * n]
        send, recv = refs[2 * n:]
        x, y, c, _ = _place()
        cps = [pltpu.make_async_remote_copy(src_ref=_half(outs[a], c, axes[a] + 1), dst_ref=_half(outs[a], c, axes[a] + 1),
                                            send_sem=send.at[a], recv_sem=recv.at[a], device_id=(x, y, 1 - c),
                                            device_id_type=MESH) for a in range(n)]
        for cp in cps:
            cp.start()
        for a in range(n):
            blk = _half(outs[a], 1 - c, axes[a] + 1)
            pltpu.make_async_remote_copy(src_ref=blk, dst_ref=blk, send_sem=send.at[a], recv_sem=recv.at[a],
                                         device_id=(x, y, 1 - c), device_id_type=MESH).wait_recv()
        for cp in cps:
            cp.wait_send()

    return pl.pallas_call(
        body, name=name, in_specs=[ANY] * n, out_specs=[ANY] * n,
        out_shape=[jax.ShapeDtypeStruct(a.shape, a.dtype) for a in arrs],
        input_output_aliases={a: a for a in range(n)},
        scratch_shapes=[pltpu.SemaphoreType.DMA((n,)), pltpu.SemaphoreType.DMA((n,))],
    )(*arrs)


HBM = pl.BlockSpec(memory_space=pltpu.HBM)
SEM = pl.BlockSpec(memory_space=pltpu.SEMAPHORE)
EFFECT = pltpu.SideEffectType.DATAFLOW_SIDE_EFFECTING
PEERS = 7


def _split_copies(srcs, lands, send, recv, gather, axes=None):
    x, y, c, chips = _place()
    me = 2 * x + y
    if gather == "all":
        out = []
        for a in range(len(srcs)):
            for m in range(1, N_DEV):
                px, py, pc = [(1 - q) if (m >> s) & 1 else q for q, s in ((x, 2), (y, 1), (c, 0))]
                sems = dict(send_sem=send.at[PEERS * a + m - 1], recv_sem=recv.at[PEERS * a + m - 1], device_id=(px, py, pc),
                            device_id_type=MESH)
                out.append((pltpu.make_async_remote_copy(src_ref=srcs[a], dst_ref=lands[a].at[4 * x + 2 * y + c], **sems),
                            pltpu.make_async_remote_copy(src_ref=srcs[a], dst_ref=lands[a].at[4 * px + 2 * py + pc], **sems)))
        return out
    if axes is not None:
        out = []
        for a in range(len(srcs)):
            sems = dict(send_sem=send.at[PEERS * a], recv_sem=recv.at[PEERS * a], device_id=(x, y, 1 - c), device_id_type=MESH)
            copy = pltpu.make_async_remote_copy(src_ref=_half(srcs[a], 1 - c, axes[a] + 1), dst_ref=lands[a], **sems)
            out.append((copy, copy))
        return out
    peers = [((*chip, c), 2 * chip[0] + chip[1]) for chip in chips] + ([((x, y, 1 - c), me)] if gather else [])
    out = []
    for a in range(len(srcs)):
        for j, (dev, k) in enumerate(peers):
            src = srcs[a] if gather else srcs[a].at[k]
            sems = dict(send_sem=send.at[PEERS * a + j], recv_sem=recv.at[PEERS * a + j], device_id=dev, device_id_type=MESH)
            out.append((pltpu.make_async_remote_copy(src_ref=src, dst_ref=lands[a].at[me], **sems),
                        pltpu.make_async_remote_copy(src_ref=src, dst_ref=lands[a].at[k], **sems)))
    return out


def _split_start(srcs, gather, after, name, axes=None):
    n = len(srcs)
    if axes is not None:
        lands = [lax.empty(_half_shape(s.shape, axes[a] + 1), s.dtype) for a, s in enumerate(srcs)]
    else:
        lead = (N_DEV,) if gather == "all" else (N_CHIPS,) if gather else ()
        lands = [lax.empty(lead + s.shape, s.dtype) for s in srcs]

    def body(*refs):
        send, recv = refs[2 * n + 1], refs[2 * n + 2]
        for start, _ in _split_copies(refs[:n], refs[n:2 * n], send, recv, gather, axes):
            start.start()
        refs[-1][...] = jnp.zeros_like(refs[-1])

    sems = pltpu.SemaphoreType.DMA((PEERS * n,))
    hbm = lambda a: pltpu.with_memory_space_constraint(a, pltpu.HBM)
    out = pl.pallas_call(
        body, name=name,
        out_shape=(sems, sems, *[pltpu.HBM(a.shape, a.dtype) for a in srcs + lands], jax.ShapeDtypeStruct((SUB, LANE), f32)),
        in_specs=[HBM] * (2 * n) + [ANY], out_specs=(SEM, SEM, *[HBM] * (2 * n), pl.BlockSpec(memory_space=pltpu.VMEM)),
        input_output_aliases={i: 2 + i for i in range(2 * n)},
        compiler_params=pltpu.CompilerParams(has_side_effects=EFFECT),
    )(*[hbm(a) for a in srcs + lands], after)
    return out[0], out[1], list(out[2:2 + n]), list(out[2 + n:2 + 2 * n]), out[-1]


def _split_wait(send, recv, srcs, lands, gather, after, name, axes=None):
    n = len(srcs)

    def body(*refs):
        for start, arrival in _split_copies(refs[:n], refs[n:2 * n], refs[2 * n], refs[2 * n + 1], gather, axes):
            start.wait_send()
            arrival.wait_recv()

    out = pl.pallas_call(
        body, name=name, out_shape=[pltpu.HBM(a.shape, a.dtype) for a in srcs + lands],
        in_specs=[HBM] * (2 * n) + [SEM, SEM, ANY], out_specs=[HBM] * (2 * n),
        input_output_aliases={i: i for i in range(2 * n)},
        compiler_params=pltpu.CompilerParams(has_side_effects=EFFECT),
    )(*srcs, *lands, send, recv, after)
    return list(out[:n]), list(out[n:])


N_DEV = 8


def _sum_devices(v, land, me8, name):
    def body(p_ref, v_ref, l_ref, o_ref):
        acc = None
        for k in range(N_DEV):
            t = jnp.where(p_ref[0] == k, v_ref[...], l_ref[k])
            acc = t if acc is None else acc + t
        o_ref[...] = acc

    return pl.pallas_call(
        body, name=name,
        grid_spec=pltpu.PrefetchScalarGridSpec(
            num_scalar_prefetch=1, grid=(1,),
            in_specs=[pl.BlockSpec(v.shape, lambda i, p: (0, 0)), pl.BlockSpec(land.shape, lambda i, p: (0, 0, 0))],
            out_specs=pl.BlockSpec(v.shape, lambda i, p: (0, 0))),
        out_shape=jax.ShapeDtypeStruct(v.shape, f32),
        compiler_params=_cparams(("arbitrary",)),
    )(me8, v, land)


def _pack_small(arrs, mult=2 * SUB):
    flat = jnp.concatenate([a.reshape(-1) for a in arrs])
    rows = -(-flat.shape[0] // (LANE * mult)) * mult
    return jnp.pad(flat, (0, rows * LANE - flat.shape[0])).reshape(rows, LANE)


def _unpack_small(vec, shapes):
    flat, out, o = vec.reshape(-1), [], 0
    for s in shapes:
        n = int(np.prod(s))
        out.append(flat[o:o + n].reshape(s))
        o += n
    return out


REPL_SMALL = ("c_ctx", "b_mod", "q_norm", "k_norm", "c_norm", "d_conv_b", "d_norm_g", "d_norm_b", "ln_g", "ln_b")
SHARD_SMALL = ("b_conv", "c_gate_w2", "c_gate_b", "d_conv_w")
BIG = ("w_mod", "w_in", "w_br", "w_out")
ORDER = ("c_ctx", "w_mod", "b_mod", "w_in", "q_norm", "k_norm", "b_conv", "c_gate_w2", "c_gate_b", "c_norm", "d_conv_w",
         "d_conv_b", "d_norm_g", "d_norm_b", "w_br", "w_out", "ln_g", "ln_b")


def kernel(x, c, ctx, c_ctx, w_mod, b_mod, w_in, q_norm, k_norm, b_conv, c_gate_w2, c_gate_b, c_norm, d_conv_w, d_conv_b, d_norm_g, d_norm_b, w_br, w_out, ln_g, ln_b, loss_target, m_c_ctx, m_w_mod, m_b_mod, m_w_in, m_q_norm, m_k_norm, m_b_conv, m_c_gate_w2, m_c_gate_b, m_c_norm, m_d_conv_w, m_d_conv_b, m_d_norm_g, m_d_norm_b, m_w_br, m_w_out, m_ln_g, m_ln_b, v_c_ctx, v_w_mod, v_b_mod, v_w_in, v_q_norm, v_k_norm, v_b_conv, v_c_gate_w2, v_c_gate_b, v_c_norm, v_d_conv_w, v_d_conv_b, v_d_norm_g, v_d_norm_b, v_w_br, v_w_out, v_ln_g, v_ln_b):
    W = dict(c_ctx=c_ctx, w_mod=w_mod, b_mod=b_mod, w_in=w_in, q_norm=q_norm, k_norm=k_norm, b_conv=b_conv,
             c_gate_w2=c_gate_w2, c_gate_b=c_gate_b, c_norm=c_norm, d_conv_w=d_conv_w, d_conv_b=d_conv_b,
             d_norm_g=d_norm_g, d_norm_b=d_norm_b, w_br=w_br, w_out=w_out, ln_g=ln_g, ln_b=ln_b)
    M = dict(c_ctx=m_c_ctx, w_mod=m_w_mod, b_mod=m_b_mod, w_in=m_w_in, q_norm=m_q_norm, k_norm=m_k_norm, b_conv=m_b_conv,
             c_gate_w2=m_c_gate_w2, c_gate_b=m_c_gate_b, c_norm=m_c_norm, d_conv_w=m_d_conv_w, d_conv_b=m_d_conv_b,
             d_norm_g=m_d_norm_g, d_norm_b=m_d_norm_b, w_br=m_w_br, w_out=m_w_out, ln_g=m_ln_g, ln_b=m_ln_b)
    V = dict(c_ctx=v_c_ctx, w_mod=v_w_mod, b_mod=v_b_mod, w_in=v_w_in, q_norm=v_q_norm, k_norm=v_k_norm, b_conv=v_b_conv,
             c_gate_w2=v_c_gate_w2, c_gate_b=v_c_gate_b, c_norm=v_c_norm, d_conv_w=v_d_conv_w, d_conv_b=v_d_conv_b,
             d_norm_g=v_d_norm_g, d_norm_b=v_d_norm_b, w_br=v_w_br, w_out=v_w_out, ln_g=v_ln_g, ln_b=v_ln_b)
    chip = 2 * lax.axis_index("x") + lax.axis_index("y")
    cidx = lax.axis_index("c").astype(jnp.int32).reshape(1)

    place = jnp.stack([chip, lax.axis_index("c")]).astype(jnp.int32)

    AXIS = dict(w_in=1, w_mod=0, w_br=0, w_out=0)
    ex = dict(w_in=lambda a: jnp.swapaxes(a, 1, 2), w_mod=lambda a: a.reshape(1, DEPTH * D, -1),
              w_br=lambda a: a.reshape(DEPTH, 4 * BRW, -1), w_out=lambda a: a)
    Wx, Mx, Vx = ({k: ex[k](P_[k]) for k in BIG} for P_ in (W, M, V))

    LAYER, MERGE = ("w_in", "w_br", "w_out"), ("w_br", "w_out")
    small_shard = _pack_small([W[k] for k in SHARD_SMALL])
    keys0 = ("w_in", "w_mod")
    sent = lambda k, l: (w_mod[l] if k == "w_mod" else Wx[k][l]).astype(bf16)
    got = _all_gather([sent(k, 0) for k in keys0] + [small_shard], [AXIS[k] for k in keys0] + [0], "all_gather0")
    smalls = [_unpack_small(got[-1][s], [W[k].shape for k in SHARD_SMALL]) for s in range(N_CHIPS)]
    full = {k: jnp.concatenate([smalls[s][i] for s in range(N_CHIPS)], axis=-1) for i, k in enumerate(SHARD_SMALL)}
    ag0b = _split_start([sent(k, 0) for k in MERGE], True, got[0], "all_gather0b_start")
    ag1 = _split_start([sent(k, 1) for k in keys0], True, ag0b[4], "all_gather1_start")
    ag1b = _split_start([sent(k, 1) for k in MERGE], True, ag1[4], "all_gather1b_start")

    def merge_form(w_br4, w_out4):
        return jnp.moveaxis(w_br4.reshape(N_CHIPS, 4, BRW, D // N_CHIPS), 0, 2).reshape(4, BRW, D), w_out4.reshape(D, D)

    def weights_of(l, h):
        first = got if l == 0 else _split_wait(*ag1[:4], True, h, "all_gather1_wait")[1]
        flight = (ag0b, ag1b)[l]
        return (_group_weights(first[0]),
                lambda after: merge_form(*_split_wait(*flight[:4], True, after, f"all_gather{l}b_wait")[1]), first[1])

    red = {k: Wx[k].shape for k in BIG}
    flights, held = {}, {}

    def to_sibling(tag, l, pieces):
        keys = list(pieces)
        halves = _split_start([pieces[k] for k in keys], False, jnp.zeros((SUB, LANE), f32), f"rs_sibling_halves{tag}_start",
                              axes=[AXIS[k] for k in keys])
        flights[tag] = (l, keys, halves)
        return halves[4]

    def launch(tag, after):
        l, keys, halves = flights[tag]
        axes = [AXIS[k] for k in keys]
        pieces, land_a = _split_wait(*halves[:4], False, after, f"rs_sibling_halves{tag}_wait", axes=axes)
        pair = [_add_half(p, la, cidx, ax, f"rs_pair_sum{tag}_{k}") for k, p, la, ax in zip(keys, pieces, land_a, axes)]
        flights[tag] = (l, keys, _split_start(pair, False, jnp.zeros((SUB, LANE), f32), f"rs_chip_exchange{tag}_start"))
        return flights[tag][2][4]

    def land(tag, after):
        l, keys, flight = flights.pop(tag)
        pair, land_b = _split_wait(*flight[:4], False, after, f"rs_chip_exchange{tag}_wait")
        for k, lb, pr in zip(keys, land_b, pair):
            red[k] = _sum_chips(lb, pr, place, AXIS[k], l, red[k], f"rs_chip_sum{tag}_{k}")

    def grads_done(l, gl):
        if "wp" in gl:
            pieces = dict(w_in=_ungroup(gl["wp"]).reshape(N_CHIPS, SHARD, D))
            if l == 0:
                started = launch("0b", to_sibling("0c", 0, pieces))
                return started + launch("0c", started)
            return to_sibling("1", 1, {**pieces, **held.pop(1)})
        pieces = dict(w_br=gl["w_br"].reshape(N_CHIPS, 4 * BRW, D // N_CHIPS), w_out=gl["w_out"].reshape(N_CHIPS, D // N_CHIPS, D))
        if l == 0:
            return launch("1", gl["w_out"]) + to_sibling("0b", 0, pieces)
        held[1] = pieces
        return None

    loss, gx, g = _local_step(
        x[0], c, ctx[0], loss_target[0], c_ctx, b_mod, weights_of, q_norm, k_norm, full["b_conv"],
        full["c_gate_w2"], full["c_gate_b"], c_norm, full["d_conv_w"], d_conv_b, d_norm_g, d_norm_b,
        grads_done, ln_g, ln_b, tm=256, token=ag1b[4])
    g["c_gate_w2"], g["c_gate_b"] = g.pop("w2"), g.pop("gb")
    loss = lax.psum(loss, ("x", "y", "c"))

    w_mod_pieces = g["w_mod"].reshape(N_CHIPS, DEPTH * D, 3 * D // N_CHIPS)
    g = {k: (jnp.stack(v) if isinstance(v, list) else v) for k, v in g.items() if k not in ("wp", "w_br", "w_out", "w_mod")}

    small_names = REPL_SMALL + SHARD_SMALL
    small = _split_start([_pack_small([g[k] for k in small_names])], "all", to_sibling("0d", 0, {"w_mod": w_mod_pieces}),
                         "all_reduce_small_start")

    grad, delta, new_m, new_v = {}, {}, {}, {}

    def adamw_big(keys, after):
        filled = _sibling_fill([red[k] for k in keys], [AXIS[k] for k in keys], "rs_sibling_fill_" + keys[0])
        for k, r in zip(keys, filled):
            back = (lambda a: jnp.swapaxes(a, 1, 2)) if k == "w_in" else (lambda a: a.reshape(W[k].shape))
            g_, d_, m_, v_ = _adamw(Wx[k], r, Mx[k], Vx[k], f"adamw_{k}", after=after)
            grad[k], delta[k], new_m[k], new_v[k] = back(g_), back(d_), back(m_), back(v_)
        return d_

    token = launch("0d", small[4])
    land("1", gx)
    land("0b", gx)
    last = adamw_big(MERGE, token)
    mine, landed = _split_wait(*small[:4], "all", last, "all_reduce_small_wait")
    me8 = (2 * chip + lax.axis_index("c")).astype(jnp.int32).reshape(1)
    gs = _sum_devices(mine[0], landed[0], me8, "all_reduce_small_sum")
    gsm = dict(zip(small_names, _unpack_small(gs, [g[k].shape for k in small_names])))
    for k in SHARD_SMALL:
        wdt = W[k].shape[-1]
        gsm[k] = lax.dynamic_slice_in_dim(gsm[k], chip * wdt, wdt, axis=gsm[k].ndim - 1)
    shapes = [W[k].shape for k in small_names]
    _, d_, m_, v_ = _adamw(*[_pack_small([P_[k] for k in small_names])[None] for P_ in (W, gsm, M, V)], "adamw_small", after=last)
    for k, dd, mm_, vv in zip(small_names, _unpack_small(d_, shapes), _unpack_small(m_, shapes), _unpack_small(v_, shapes)):
        grad[k], delta[k], new_m[k], new_v[k] = gsm[k], dd, mm_, vv
    land("0c", d_)
    last = adamw_big(("w_in",), None)
    land("0d", last)
    adamw_big(("w_mod",), None)

    return (loss, gx[None], *[grad[k] for k in ORDER], *[delta[k] for k in ORDER], *[new_m[k] for k in ORDER],
            *[new_v[k] for k in ORDER])
```
